```python
import math
import jax, jax.numpy as jnp
from jax import lax
import numpy as np

D_MODEL = 1024
BATCH = 8
SEQ = 2048
DEPTH = 2

N_A_LAYERS = DEPTH // 2
N_B_LAYERS = DEPTH - N_A_LAYERS
EPS = 1e-5

SSM_EXPAND = 2
SSM_INNER = SSM_EXPAND * D_MODEL
SSM_HEAD_DIM = 64
SSM_HEADS = SSM_INNER // SSM_HEAD_DIM
SSM_GROUPS = 8
SSM_STATE = 128
SSM_CONV = 4
SSM_CHUNK = 256
SSM_CONV_DIM = SSM_INNER + 2 * SSM_GROUPS * SSM_STATE
SSM_PROJ = 2 * SSM_INNER + 2 * SSM_GROUPS * SSM_STATE + SSM_HEADS

ATT_HEAD_DIM = 64
ATT_Q_HEADS = D_MODEL // ATT_HEAD_DIM
ATT_KV_HEADS = 4
ATT_GROUP = ATT_Q_HEADS // ATT_KV_HEADS
WINDOW = 128
ROPE_THETA = 10000.0

FFN_DIM = 2816
FFN_CONV = 3

kernel_name = "yoco_mamba2_swa_sink_convffn"


def rms_norm(x, g):
    xf = x.astype(jnp.float32)
    xf = xf * lax.rsqrt(jnp.mean(xf * xf, axis=-1, keepdims=True) + EPS)
    return xf.astype(x.dtype) * g


def group_rms_norm(y, g, groups):
    b, s, c = y.shape
    yf = y.astype(jnp.float32).reshape(b, s, groups, c // groups)
    yf = yf * lax.rsqrt(jnp.mean(yf * yf, axis=-1, keepdims=True) + EPS)
    return yf.reshape(b, s, c).astype(y.dtype) * g


def causal_dwconv(x, w, bias):
    width, ch = w.shape
    out = lax.conv_general_dilated(
        x, w[:, None, :].astype(x.dtype), window_strides=(1,),
        padding=[(width - 1, 0)], dimension_numbers=("NWC", "WIO", "NWC"),
        feature_group_count=ch)
    return out + bias


def rotary(x, positions):
    half = x.shape[-1] // 2
    inv_freq = ROPE_THETA ** (-jnp.arange(half, dtype=jnp.float32) / half)
    ang = positions.astype(jnp.float32)[..., None] * inv_freq
    cos = jnp.cos(ang)[:, :, None, :]
    sin = jnp.sin(ang)[:, :, None, :]
    xf = x.astype(jnp.float32)
    x1, x2 = xf[..., :half], xf[..., half:]
    return jnp.concatenate([x1 * cos - x2 * sin, x2 * cos + x1 * sin], axis=-1).astype(x.dtype)


def ssd_chunked(x, dt, A, Bm, Cm):
    b, s, h, p = x.shape
    g, n = Bm.shape[2], Bm.shape[3]
    e = h // g
    pad = (-s) % SSM_CHUNK
    x = jnp.pad(x, ((0, 0), (0, pad), (0, 0), (0, 0)))
    dt = jnp.pad(dt, ((0, 0), (0, pad), (0, 0)))
    Bm = jnp.pad(Bm, ((0, 0), (0, pad), (0, 0), (0, 0)))
    Cm = jnp.pad(Cm, ((0, 0), (0, pad), (0, 0), (0, 0)))
    L = SSM_CHUNK
    c = (s + pad) // L
    xd = (x * dt[..., None]).reshape(b, c, L, g, e, p)
    a = (dt * A).reshape(b, c, L, g, e)
    a_cum = jnp.cumsum(a, axis=2)
    Bc = Bm.reshape(b, c, L, g, n)
    Cc = Cm.reshape(b, c, L, g, n)
    seg = a_cum[:, :, :, None] - a_cum[:, :, None, :]
    causal = jnp.tril(jnp.ones((L, L), dtype=bool))[None, None, :, :, None, None]
    decay = jnp.exp(jnp.where(causal, seg, -jnp.inf))
    cb = jnp.einsum("bclgn,bcsgn->bclsg", Cc, Bc)
    w = cb[..., None] * decay
    y_diag = jnp.einsum("bclsge,bcsgep->bclgep", w, xd)
    decay_to_end = jnp.exp(a_cum[:, :, -1:] - a_cum)
    states = jnp.einsum("bclgn,bclge,bclgep->bcgepn", Bc, decay_to_end, xd)
    chunk_decay = jnp.exp(a_cum[:, :, -1])

    def step(state, inp):
        dec, new = inp
        return state * dec[..., None, None] + new, state

    init = jnp.zeros((b, g, e, p, n), jnp.float32)
    _, prev = lax.scan(step, init, (jnp.moveaxis(chunk_decay, 1, 0), jnp.moveaxis(states, 1, 0)))
    prev = jnp.moveaxis(prev, 0, 1)
    y_off = jnp.einsum("bclgn,bcgepn->bclgep", Cc, prev) * jnp.exp(a_cum)[..., None]
    y = (y_diag + y_off).reshape(b, c * L, h, p)
    return y[:, :s]


def mamba2_mixer(h, in_proj, conv_w, conv_b, dt_bias, A_log, D, gnorm, out_proj):
    b, s, _ = h.shape
    zxbcdt = h @ in_proj
    z, xBC, dt = jnp.split(zxbcdt, [SSM_INNER, SSM_INNER + SSM_CONV_DIM], axis=-1)
    xBC = jax.nn.silu(causal_dwconv(xBC, conv_w, conv_b))
    xs, Bm, Cm = jnp.split(xBC, [SSM_INNER, SSM_INNER + SSM_GROUPS * SSM_STATE], axis=-1)
    xs = xs.reshape(b, s, SSM_HEADS, SSM_HEAD_DIM).astype(jnp.float32)
    Bm = Bm.reshape(b, s, SSM_GROUPS, SSM_STATE).astype(jnp.float32)
    Cm = Cm.reshape(b, s, SSM_GROUPS, SSM_STATE).astype(jnp.float32)
    dt = jax.nn.softplus(dt.astype(jnp.float32) + dt_bias.astype(jnp.float32))
    A = -jnp.exp(A_log.astype(jnp.float32))
    y = ssd_chunked(xs, dt, A, Bm, Cm) + xs * D.astype(jnp.float32)[:, None]
    y = y.reshape(b, s, SSM_INNER).astype(h.dtype)
    y = group_rms_norm(y * jax.nn.silu(z), gnorm, SSM_GROUPS)
    return y @ out_proj


def sliding_window_sink_attention(q, k, v, sinks):
    b, s, _, d = q.shape
    nb = s // WINDOW
    qb = q.reshape(b, nb, WINDOW, ATT_KV_HEADS, ATT_GROUP, d)
    kb = k.reshape(b, nb, WINDOW, ATT_KV_HEADS, d)
    vb = v.reshape(b, nb, WINDOW, ATT_KV_HEADS, d)
    blk_pad = ((0, 0), (1, 0), (0, 0), (0, 0), (0, 0))
    k_band = jnp.concatenate([jnp.pad(kb, blk_pad)[:, :-1], kb], axis=2)
    v_band = jnp.concatenate([jnp.pad(vb, blk_pad)[:, :-1], vb], axis=2)
    scores = jnp.einsum("bnqhgd,bnkhd->bnhgqk", qb, k_band).astype(jnp.float32) * (d ** -0.5)
    qi = jnp.arange(WINDOW)[:, None]
    ki = jnp.arange(2 * WINDOW)[None, :]
    rel = qi + WINDOW - ki
    kpos = jnp.arange(nb)[:, None, None] * WINDOW + ki[None] - WINDOW
    mask = (rel >= 0)[None] & (rel < WINDOW)[None] & (kpos >= 0)
    scores = jnp.where(mask[None, :, None, None], scores, -jnp.inf)
    sink = jnp.broadcast_to(
        sinks.astype(jnp.float32).reshape(ATT_KV_HEADS, ATT_GROUP)[None, None, :, :, None, None],
        scores.shape[:-1] + (1,))
    probs = jax.nn.softmax(jnp.concatenate([scores, sink], axis=-1), axis=-1)[..., :-1]
    out = jnp.einsum("bnhgqk,bnkhd->bnqhgd", probs.astype(v.dtype), v_band)
    return out.reshape(b, s, ATT_Q_HEADS * d)


def conv_ffn(x, norm_g, w_in, conv_w, conv_b, w_down):
    h = rms_norm(x, norm_g)
    gate, val = jnp.split(h @ w_in, [FFN_DIM], axis=-1)
    gate = causal_dwconv(gate, conv_w, conv_b)
    return (jax.nn.silu(gate) * val) @ w_down


def _dense(key, shape, fan_in):
    return jax.random.normal(key, shape, jnp.float32) * (fan_in ** -0.5)


def _gain(key, shape):
    return 1.0 + 0.02 * jax.random.normal(key, shape, jnp.float32)


def _small(key, shape):
    return 0.02 * jax.random.normal(key, shape, jnp.float32)


def _fwd_setup_inputs(seed: int = 0) -> dict:
    key = jax.random.key(seed)
    ks = jax.random.split(key, 32)
    qkv_dim = ATT_KV_HEADS * ATT_HEAD_DIM
    q_dim = ATT_Q_HEADS * ATT_HEAD_DIM
    x = jax.random.normal(ks[0], (BATCH, SEQ, D_MODEL), jnp.float32)
    positions = (jnp.arange(SEQ, dtype=jnp.int32)[None, :]
                 + jax.random.randint(ks[1], (BATCH, 1), 0, 4096, dtype=jnp.int32))
    dt0 = jnp.exp(jax.random.uniform(ks[6], (N_A_LAYERS, SSM_HEADS), jnp.float32,
                                     minval=math.log(1e-3), maxval=math.log(1e-1)))
    return {
        "x": x,
        "positions": positions,
        "a_norm": _gain(ks[2], (N_A_LAYERS, D_MODEL)),
        "a_in_proj": _dense(ks[3], (N_A_LAYERS, D_MODEL, SSM_PROJ), D_MODEL),
        "a_conv_w": jax.random.normal(ks[4], (N_A_LAYERS, SSM_CONV, SSM_CONV_DIM), jnp.float32) * (SSM_CONV ** -0.5),
        "a_conv_b": _small(ks[5], (N_A_LAYERS, SSM_CONV_DIM)),
        "a_dt_bias": dt0 + jnp.log(-jnp.expm1(-dt0)),
        "a_A_log": jnp.log(jax.random.uniform(ks[7], (N_A_LAYERS, SSM_HEADS), jnp.float32, minval=1.0, maxval=16.0)),
        "a_D": 1.0 + 0.1 * jax.random.normal(ks[8], (N_A_LAYERS, SSM_HEADS), jnp.float32),
        "a_gnorm": _gain(ks[9], (N_A_LAYERS, SSM_INNER)),
        "a_out_proj": _dense(ks[10], (N_A_LAYERS, SSM_INNER, D_MODEL), SSM_INNER),
        "kv_norm": _gain(ks[11], (D_MODEL,)),
        "w_kv": _dense(ks[12], (D_MODEL, 2 * qkv_dim), D_MODEL),
        "b_kv": _small(ks[13], (2 * qkv_dim,)),
        "k_norm": _gain(ks[14], (ATT_HEAD_DIM,)),
        "b_norm": _gain(ks[15], (N_B_LAYERS, D_MODEL)),
        "w_q": _dense(ks[16], (N_B_LAYERS, D_MODEL, q_dim), D_MODEL),
        "b_q": _small(ks[17], (N_B_LAYERS, q_dim)),
        "q_norm": _gain(ks[18], (N_B_LAYERS, ATT_HEAD_DIM)),
        "sinks": jax.random.normal(ks[19], (N_B_LAYERS, ATT_Q_HEADS), jnp.float32),
        "w_o": _dense(ks[20], (N_B_LAYERS, q_dim, D_MODEL), q_dim),
        "b_o": _small(ks[21], (N_B_LAYERS, D_MODEL)),
        "f_norm": _gain(ks[22], (DEPTH, D_MODEL)),
        "f_w_in": _dense(ks[23], (DEPTH, D_MODEL, 2 * FFN_DIM), D_MODEL),
        "f_conv_w": jax.random.normal(ks[24], (DEPTH, FFN_CONV, FFN_DIM), jnp.float32) * (FFN_CONV ** -0.5),
        "f_conv_b": _small(ks[25], (DEPTH, FFN_DIM)),
        "f_w_down": _dense(ks[26], (DEPTH, FFN_DIM, D_MODEL), FFN_DIM),
    }


def _fwd_reference(x, positions, a_norm, a_in_proj, a_conv_w, a_conv_b, a_dt_bias, a_A_log, a_D,
              a_gnorm, a_out_proj, kv_norm, w_kv, b_kv, k_norm, b_norm, w_q, b_q, q_norm,
              sinks, w_o, b_o, f_norm, f_w_in, f_conv_w, f_conv_b, f_w_down):
    b, s, _ = x.shape
    k_shared = None
    v_shared = None
    for layer in range(DEPTH):
        if layer < N_A_LAYERS:
            i = layer
            h = rms_norm(x, a_norm[i])
            x = x + mamba2_mixer(h, a_in_proj[i], a_conv_w[i], a_conv_b[i], a_dt_bias[i],
                                 a_A_log[i], a_D[i], a_gnorm[i], a_out_proj[i])
        else:
            i = layer - N_A_LAYERS
            if i == 0:
                kv = rms_norm(x, kv_norm) @ w_kv + b_kv
                k_shared, v_shared = jnp.split(kv, 2, axis=-1)
                k_shared = k_shared.reshape(b, s, ATT_KV_HEADS, ATT_HEAD_DIM)
                v_shared = v_shared.reshape(b, s, ATT_KV_HEADS, ATT_HEAD_DIM)
                k_shared = rotary(rms_norm(k_shared, k_norm), positions)
            h = rms_norm(x, b_norm[i])
            q = (h @ w_q[i] + b_q[i]).reshape(b, s, ATT_Q_HEADS, ATT_HEAD_DIM)
            q = rotary(rms_norm(q, q_norm[i]), positions)
            att = sliding_window_sink_attention(q, k_shared, v_shared, sinks[i])
            x = x + att @ w_o[i] + b_o[i]
        x = x + conv_ffn(x, f_norm[layer], f_w_in[layer], f_conv_w[layer], f_conv_b[layer], f_w_down[layer])
    return x


import jax as _jax
import jax.numpy as _jnp

TWIN_FORMAT = 'train_step'
FWD_PARAMS = ['x', 'positions', 'a_norm', 'a_in_proj', 'a_conv_w', 'a_conv_b', 'a_dt_bias', 'a_A_log', 'a_D', 'a_gnorm', 'a_out_proj', 'kv_norm', 'w_kv', 'b_kv', 'k_norm', 'b_norm', 'w_q', 'b_q', 'q_norm', 'sinks', 'w_o', 'b_o', 'f_norm', 'f_w_in', 'f_conv_w', 'f_conv_b', 'f_w_down']
TWIN_WEIGHTS = ['a_norm', 'a_in_proj', 'a_conv_w', 'a_conv_b', 'a_dt_bias', 'a_A_log', 'a_D', 'a_gnorm', 'a_out_proj', 'kv_norm', 'w_kv', 'b_kv', 'k_norm', 'b_norm', 'w_q', 'b_q', 'q_norm', 'sinks', 'w_o', 'b_o', 'f_norm', 'f_w_in', 'f_conv_w', 'f_conv_b', 'f_w_down']
TWIN_DIFF_INPUT = 'x'
TWIN_INPUTS = ['x', 'positions', 'a_norm', 'a_in_proj', 'a_conv_w', 'a_conv_b', 'a_dt_bias', 'a_A_log', 'a_D', 'a_gnorm', 'a_out_proj', 'kv_norm', 'w_kv', 'b_kv', 'k_norm', 'b_norm', 'w_q', 'b_q', 'q_norm', 'sinks', 'w_o', 'b_o', 'f_norm', 'f_w_in', 'f_conv_w', 'f_conv_b', 'f_w_down', 'loss_target', 'm_a_norm', 'm_a_in_proj', 'm_a_conv_w', 'm_a_conv_b', 'm_a_dt_bias', 'm_a_A_log', 'm_a_D', 'm_a_gnorm', 'm_a_out_proj', 'm_kv_norm', 'm_w_kv', 'm_b_kv', 'm_k_norm', 'm_b_norm', 'm_w_q', 'm_b_q', 'm_q_norm', 'm_sinks', 'm_w_o', 'm_b_o', 'm_f_norm', 'm_f_w_in', 'm_f_conv_w', 'm_f_conv_b', 'm_f_w_down', 'v_a_norm', 'v_a_in_proj', 'v_a_conv_w', 'v_a_conv_b', 'v_a_dt_bias', 'v_a_A_log', 'v_a_D', 'v_a_gnorm', 'v_a_out_proj', 'v_kv_norm', 'v_w_kv', 'v_b_kv', 'v_k_norm', 'v_b_norm', 'v_w_q', 'v_b_q', 'v_q_norm', 'v_sinks', 'v_w_o', 'v_b_o', 'v_f_norm', 'v_f_w_in', 'v_f_conv_w', 'v_f_conv_b', 'v_f_w_down']
TWIN_OUTPUTS = ['loss', 'grad_x', 'grad_a_norm', 'grad_a_in_proj', 'grad_a_conv_w', 'grad_a_conv_b', 'grad_a_dt_bias', 'grad_a_A_log', 'grad_a_D', 'grad_a_gnorm', 'grad_a_out_proj', 'grad_kv_norm', 'grad_w_kv', 'grad_b_kv', 'grad_k_norm', 'grad_b_norm', 'grad_w_q', 'grad_b_q', 'grad_q_norm', 'grad_sinks', 'grad_w_o', 'grad_b_o', 'grad_f_norm', 'grad_f_w_in', 'grad_f_conv_w', 'grad_f_conv_b', 'grad_f_w_down', 'delta_a_norm', 'delta_a_in_proj', 'delta_a_conv_w', 'delta_a_conv_b', 'delta_a_dt_bias', 'delta_a_A_log', 'delta_a_D', 'delta_a_gnorm', 'delta_a_out_proj', 'delta_kv_norm', 'delta_w_kv', 'delta_b_kv', 'delta_k_norm', 'delta_b_norm', 'delta_w_q', 'delta_b_q', 'delta_q_norm', 'delta_sinks', 'delta_w_o', 'delta_b_o', 'delta_f_norm', 'delta_f_w_in', 'delta_f_conv_w', 'delta_f_conv_b', 'delta_f_w_down', 'new_m_a_norm', 'new_m_a_in_proj', 'new_m_a_conv_w', 'new_m_a_conv_b', 'new_m_a_dt_bias', 'new_m_a_A_log', 'new_m_a_D', 'new_m_a_gnorm', 'new_m_a_out_proj', 'new_m_kv_norm', 'new_m_w_kv', 'new_m_b_kv', 'new_m_k_norm', 'new_m_b_norm', 'new_m_w_q', 'new_m_b_q', 'new_m_q_norm', 'new_m_sinks', 'new_m_w_o', 'new_m_b_o', 'new_m_f_norm', 'new_m_f_w_in', 'new_m_f_conv_w', 'new_m_f_conv_b', 'new_m_f_w_down', 'new_v_a_norm', 'new_v_a_in_proj', 'new_v_a_conv_w', 'new_v_a_conv_b', 'new_v_a_dt_bias', 'new_v_a_A_log', 'new_v_a_D', 'new_v_a_gnorm', 'new_v_a_out_proj', 'new_v_kv_norm', 'new_v_w_kv', 'new_v_b_kv', 'new_v_k_norm', 'new_v_b_norm', 'new_v_w_q', 'new_v_b_q', 'new_v_q_norm', 'new_v_sinks', 'new_v_w_o', 'new_v_b_o', 'new_v_f_norm', 'new_v_f_w_in', 'new_v_f_conv_w', 'new_v_f_conv_b', 'new_v_f_w_down']
TWIN_LEAF_KINDS = {'loss': 'loss', 'grad_x': 'grad_x', 'grad_a_norm': 'grad_w', 'grad_a_in_proj': 'grad_w', 'grad_a_conv_w': 'grad_w', 'grad_a_conv_b': 'grad_w', 'grad_a_dt_bias': 'grad_w', 'grad_a_A_log': 'grad_w', 'grad_a_D': 'grad_w', 'grad_a_gnorm': 'grad_w', 'grad_a_out_proj': 'grad_w', 'grad_kv_norm': 'grad_w', 'grad_w_kv': 'grad_w', 'grad_b_kv': 'grad_w', 'grad_k_norm': 'grad_w', 'grad_b_norm': 'grad_w', 'grad_w_q': 'grad_w', 'grad_b_q': 'grad_w', 'grad_q_norm': 'grad_w', 'grad_sinks': 'grad_w', 'grad_w_o': 'grad_w', 'grad_b_o': 'grad_w', 'grad_f_norm': 'grad_w', 'grad_f_w_in': 'grad_w', 'grad_f_conv_w': 'grad_w', 'grad_f_conv_b': 'grad_w', 'grad_f_w_down': 'grad_w', 'delta_a_norm': 'delta_w', 'delta_a_in_proj': 'delta_w', 'delta_a_conv_w': 'delta_w', 'delta_a_conv_b': 'delta_w', 'delta_a_dt_bias': 'delta_w', 'delta_a_A_log': 'delta_w', 'delta_a_D': 'delta_w', 'delta_a_gnorm': 'delta_w', 'delta_a_out_proj': 'delta_w', 'delta_kv_norm': 'delta_w', 'delta_w_kv': 'delta_w', 'delta_b_kv': 'delta_w', 'delta_k_norm': 'delta_w', 'delta_b_norm': 'delta_w', 'delta_w_q': 'delta_w', 'delta_b_q': 'delta_w', 'delta_q_norm': 'delta_w', 'delta_sinks': 'delta_w', 'delta_w_o': 'delta_w', 'delta_b_o': 'delta_w', 'delta_f_norm': 'delta_w', 'delta_f_w_in': 'delta_w', 'delta_f_conv_w': 'delta_w', 'delta_f_conv_b': 'delta_w', 'delta_f_w_down': 'delta_w', 'new_m_a_norm': 'new_m', 'new_m_a_in_proj': 'new_m', 'new_m_a_conv_w': 'new_m', 'new_m_a_conv_b': 'new_m', 'new_m_a_dt_bias': 'new_m', 'new_m_a_A_log': 'new_m', 'new_m_a_D': 'new_m', 'new_m_a_gnorm': 'new_m', 'new_m_a_out_proj': 'new_m', 'new_m_kv_norm': 'new_m', 'new_m_w_kv': 'new_m', 'new_m_b_kv': 'new_m', 'new_m_k_norm': 'new_m', 'new_m_b_norm': 'new_m', 'new_m_w_q': 'new_m', 'new_m_b_q': 'new_m', 'new_m_q_norm': 'new_m', 'new_m_sinks': 'new_m', 'new_m_w_o': 'new_m', 'new_m_b_o': 'new_m', 'new_m_f_norm': 'new_m', 'new_m_f_w_in': 'new_m', 'new_m_f_conv_w': 'new_m', 'new_m_f_conv_b': 'new_m', 'new_m_f_w_down': 'new_m', 'new_v_a_norm': 'new_v', 'new_v_a_in_proj': 'new_v', 'new_v_a_conv_w': 'new_v', 'new_v_a_conv_b': 'new_v', 'new_v_a_dt_bias': 'new_v', 'new_v_a_A_log': 'new_v', 'new_v_a_D': 'new_v', 'new_v_a_gnorm': 'new_v', 'new_v_a_out_proj': 'new_v', 'new_v_kv_norm': 'new_v', 'new_v_w_kv': 'new_v', 'new_v_b_kv': 'new_v', 'new_v_k_norm': 'new_v', 'new_v_b_norm': 'new_v', 'new_v_w_q': 'new_v', 'new_v_b_q': 'new_v', 'new_v_q_norm': 'new_v', 'new_v_sinks': 'new_v', 'new_v_w_o': 'new_v', 'new_v_b_o': 'new_v', 'new_v_f_norm': 'new_v', 'new_v_f_w_in': 'new_v', 'new_v_f_conv_w': 'new_v', 'new_v_f_conv_b': 'new_v', 'new_v_f_w_down': 'new_v'}


def _forward(args):
    return _fwd_reference(*[args[k] for k in FWD_PARAMS])


def _output_shape():
    out = _jax.eval_shape(lambda: _forward(_fwd_setup_inputs(0)))
    return out.shape, out.dtype

N_MICROBATCH = 1
ADAM_LR = 0.001
ADAM_B1 = 0.9
ADAM_B2 = 0.999
ADAM_EPS = 1e-08
ADAM_WD = 0.01
ADAM_STEP = 10
PER_EXAMPLE_BATCH_AXIS = {'x': 0, 'positions': 0, 'loss_target': 0}
SHARED_INPUTS = []
_WEIGHT_DTYPES = {'a_norm': _jnp.float32, 'a_in_proj': _jnp.float32, 'a_conv_w': _jnp.float32, 'a_conv_b': _jnp.float32, 'a_dt_bias': _jnp.float32, 'a_A_log': _jnp.float32, 'a_D': _jnp.float32, 'a_gnorm': _jnp.float32, 'a_out_proj': _jnp.float32, 'kv_norm': _jnp.float32, 'w_kv': _jnp.float32, 'b_kv': _jnp.float32, 'k_norm': _jnp.float32, 'b_norm': _jnp.float32, 'w_q': _jnp.float32, 'b_q': _jnp.float32, 'q_norm': _jnp.float32, 'sinks': _jnp.float32, 'w_o': _jnp.float32, 'b_o': _jnp.float32, 'f_norm': _jnp.float32, 'f_w_in': _jnp.float32, 'f_conv_w': _jnp.float32, 'f_conv_b': _jnp.float32, 'f_w_down': _jnp.float32}
MOMENT_SCALE = {'a_norm': 6.236453e-01, 'a_in_proj': 2.431538e-01, 'a_conv_w': 4.533994e-01, 'a_conv_b': 1.544207e+00, 'a_dt_bias': 1.585285e+00, 'a_A_log': 2.597188e+00, 'a_D': 2.980187e+00, 'a_gnorm': 1.136474e+01, 'a_out_proj': 1.407523e+00, 'kv_norm': 8.993942e-01, 'w_kv': 9.880826e-01, 'b_kv': 7.933231e+00, 'k_norm': 3.498457e+00, 'b_norm': 5.394586e-02, 'w_q': 5.605092e-02, 'b_q': 9.291215e-02, 'q_norm': 3.483896e+00, 'sinks': 6.446992e-01, 'w_o': 5.805461e-01, 'b_o': 4.586302e+00, 'f_norm': 1.303949e+01, 'f_w_in': 2.776719e-01, 'f_conv_w': 1.513549e+00, 'f_conv_b': 1.830343e+00, 'f_w_down': 3.004515e-01}


def _to_microbatches(a, axis):
    t = _jnp.moveaxis(a, axis, 0)
    t = t.reshape((N_MICROBATCH, t.shape[0] // N_MICROBATCH) + t.shape[1:])
    return _jnp.moveaxis(t, 1, axis + 1)


def setup_inputs(seed: int = 0) -> dict:
    inp = _fwd_setup_inputs(seed)
    key = _jax.random.fold_in(_jax.random.key(seed), 7919)
    shape, _ = _output_shape()
    out = dict(inp)
    out["loss_target"] = _jax.random.normal(_jax.random.fold_in(key, 0), shape, _jnp.float32)
    for i, name in enumerate(TWIN_WEIGHTS):
        w = inp[name].astype(_jnp.float32)
        if MOMENT_SCALE is None:
            s = _jnp.sqrt(_jnp.mean(_jnp.square(w)) + 1e-30)
        else:
            s = MOMENT_SCALE[name]
        km, kv = _jax.random.split(_jax.random.fold_in(key, i + 1))
        out[name] = w
        out["m_" + name] = s * _jax.random.normal(km, w.shape, _jnp.float32)
        out["v_" + name] = (s * s) * _jax.random.uniform(kv, w.shape, _jnp.float32, 0.5, 1.5)
    if N_MICROBATCH > 1:
        for name, axis in PER_EXAMPLE_BATCH_AXIS.items():
            out[name] = _to_microbatches(out[name], axis)
    return {'x': out['x'], 'positions': out['positions'], 'a_norm': out['a_norm'], 'a_in_proj': out['a_in_proj'], 'a_conv_w': out['a_conv_w'], 'a_conv_b': out['a_conv_b'], 'a_dt_bias': out['a_dt_bias'], 'a_A_log': out['a_A_log'], 'a_D': out['a_D'], 'a_gnorm': out['a_gnorm'], 'a_out_proj': out['a_out_proj'], 'kv_norm': out['kv_norm'], 'w_kv': out['w_kv'], 'b_kv': out['b_kv'], 'k_norm': out['k_norm'], 'b_norm': out['b_norm'], 'w_q': out['w_q'], 'b_q': out['b_q'], 'q_norm': out['q_norm'], 'sinks': out['sinks'], 'w_o': out['w_o'], 'b_o': out['b_o'], 'f_norm': out['f_norm'], 'f_w_in': out['f_w_in'], 'f_conv_w': out['f_conv_w'], 'f_conv_b': out['f_conv_b'], 'f_w_down': out['f_w_down'], 'loss_target': out['loss_target'], 'm_a_norm': out['m_a_norm'], 'm_a_in_proj': out['m_a_in_proj'], 'm_a_conv_w': out['m_a_conv_w'], 'm_a_conv_b': out['m_a_conv_b'], 'm_a_dt_bias': out['m_a_dt_bias'], 'm_a_A_log': out['m_a_A_log'], 'm_a_D': out['m_a_D'], 'm_a_gnorm': out['m_a_gnorm'], 'm_a_out_proj': out['m_a_out_proj'], 'm_kv_norm': out['m_kv_norm'], 'm_w_kv': out['m_w_kv'], 'm_b_kv': out['m_b_kv'], 'm_k_norm': out['m_k_norm'], 'm_b_norm': out['m_b_norm'], 'm_w_q': out['m_w_q'], 'm_b_q': out['m_b_q'], 'm_q_norm': out['m_q_norm'], 'm_sinks': out['m_sinks'], 'm_w_o': out['m_w_o'], 'm_b_o': out['m_b_o'], 'm_f_norm': out['m_f_norm'], 'm_f_w_in': out['m_f_w_in'], 'm_f_conv_w': out['m_f_conv_w'], 'm_f_conv_b': out['m_f_conv_b'], 'm_f_w_down': out['m_f_w_down'], 'v_a_norm': out['v_a_norm'], 'v_a_in_proj': out['v_a_in_proj'], 'v_a_conv_w': out['v_a_conv_w'], 'v_a_conv_b': out['v_a_conv_b'], 'v_a_dt_bias': out['v_a_dt_bias'], 'v_a_A_log': out['v_a_A_log'], 'v_a_D': out['v_a_D'], 'v_a_gnorm': out['v_a_gnorm'], 'v_a_out_proj': out['v_a_out_proj'], 'v_kv_norm': out['v_kv_norm'], 'v_w_kv': out['v_w_kv'], 'v_b_kv': out['v_b_kv'], 'v_k_norm': out['v_k_norm'], 'v_b_norm': out['v_b_norm'], 'v_w_q': out['v_w_q'], 'v_b_q': out['v_b_q'], 'v_q_norm': out['v_q_norm'], 'v_sinks': out['v_sinks'], 'v_w_o': out['v_w_o'], 'v_b_o': out['v_b_o'], 'v_f_norm': out['v_f_norm'], 'v_f_w_in': out['v_f_w_in'], 'v_f_conv_w': out['v_f_conv_w'], 'v_f_conv_b': out['v_f_conv_b'], 'v_f_w_down': out['v_f_w_down']}


def _loss(weights, diff, rest, loss_target):
    with _jax.named_scope("forward"):
        args = {**rest, TWIN_DIFF_INPUT: diff, **{k: w.astype(_WEIGHT_DTYPES[k]) for k, w in weights.items()}}
        y = _forward(args)
    with _jax.named_scope("loss_head"):
        err = _jnp.square(y.astype(_jnp.float32) - loss_target)
        return 0.5 * _jnp.sum(_jnp.mean(err, axis=-1)) if err.ndim else 0.5 * err


def _adamw(w, g, m, v):
    m = ADAM_B1 * m + (1.0 - ADAM_B1) * g
    v = ADAM_B2 * v + (1.0 - ADAM_B2) * _jnp.square(g)
    m_hat = m / (1.0 - ADAM_B1 ** ADAM_STEP)
    v_hat = v / (1.0 - ADAM_B2 ** ADAM_STEP)
    delta = -ADAM_LR * (m_hat / (_jnp.sqrt(v_hat) + ADAM_EPS) + ADAM_WD * w)
    return delta, m, v


def reference(x, positions, a_norm, a_in_proj, a_conv_w, a_conv_b, a_dt_bias, a_A_log, a_D, a_gnorm, a_out_proj, kv_norm, w_kv, b_kv, k_norm, b_norm, w_q, b_q, q_norm, sinks, w_o, b_o, f_norm, f_w_in, f_conv_w, f_conv_b, f_w_down, loss_target, m_a_norm, m_a_in_proj, m_a_conv_w, m_a_conv_b, m_a_dt_bias, m_a_A_log, m_a_D, m_a_gnorm, m_a_out_proj, m_kv_norm, m_w_kv, m_b_kv, m_k_norm, m_b_norm, m_w_q, m_b_q, m_q_norm, m_sinks, m_w_o, m_b_o, m_f_norm, m_f_w_in, m_f_conv_w, m_f_conv_b, m_f_w_down, v_a_norm, v_a_in_proj, v_a_conv_w, v_a_conv_b, v_a_dt_bias, v_a_A_log, v_a_D, v_a_gnorm, v_a_out_proj, v_kv_norm, v_w_kv, v_b_kv, v_k_norm, v_b_norm, v_w_q, v_b_q, v_q_norm, v_sinks, v_w_o, v_b_o, v_f_norm, v_f_w_in, v_f_conv_w, v_f_conv_b, v_f_w_down):
    given = dict(x=x, positions=positions, a_norm=a_norm, a_in_proj=a_in_proj, a_conv_w=a_conv_w, a_conv_b=a_conv_b, a_dt_bias=a_dt_bias, a_A_log=a_A_log, a_D=a_D, a_gnorm=a_gnorm, a_out_proj=a_out_proj, kv_norm=kv_norm, w_kv=w_kv, b_kv=b_kv, k_norm=k_norm, b_norm=b_norm, w_q=w_q, b_q=b_q, q_norm=q_norm, sinks=sinks, w_o=w_o, b_o=b_o, f_norm=f_norm, f_w_in=f_w_in, f_conv_w=f_conv_w, f_conv_b=f_conv_b, f_w_down=f_w_down, loss_target=loss_target, m_a_norm=m_a_norm, m_a_in_proj=m_a_in_proj, m_a_conv_w=m_a_conv_w, m_a_conv_b=m_a_conv_b, m_a_dt_bias=m_a_dt_bias, m_a_A_log=m_a_A_log, m_a_D=m_a_D, m_a_gnorm=m_a_gnorm, m_a_out_proj=m_a_out_proj, m_kv_norm=m_kv_norm, m_w_kv=m_w_kv, m_b_kv=m_b_kv, m_k_norm=m_k_norm, m_b_norm=m_b_norm, m_w_q=m_w_q, m_b_q=m_b_q, m_q_norm=m_q_norm, m_sinks=m_sinks, m_w_o=m_w_o, m_b_o=m_b_o, m_f_norm=m_f_norm, m_f_w_in=m_f_w_in, m_f_conv_w=m_f_conv_w, m_f_conv_b=m_f_conv_b, m_f_w_down=m_f_w_down, v_a_norm=v_a_norm, v_a_in_proj=v_a_in_proj, v_a_conv_w=v_a_conv_w, v_a_conv_b=v_a_conv_b, v_a_dt_bias=v_a_dt_bias, v_a_A_log=v_a_A_log, v_a_D=v_a_D, v_a_gnorm=v_a_gnorm, v_a_out_proj=v_a_out_proj, v_kv_norm=v_kv_norm, v_w_kv=v_w_kv, v_b_kv=v_b_kv, v_k_norm=v_k_norm, v_b_norm=v_b_norm, v_w_q=v_w_q, v_b_q=v_b_q, v_q_norm=v_q_norm, v_sinks=v_sinks, v_w_o=v_w_o, v_b_o=v_b_o, v_f_norm=v_f_norm, v_f_w_in=v_f_w_in, v_f_conv_w=v_f_conv_w, v_f_conv_b=v_f_conv_b, v_f_w_down=v_f_w_down)
    weights = {n: given[n] for n in TWIN_WEIGHTS}
    shared = {n: given[n] for n in SHARED_INPUTS}
    per_example = {n: given[n] for n in ['x', 'positions']}
    grad_fn = _jax.value_and_grad(_loss, argnums=(0, 1))

    def one_microbatch(ex, loss_target):
        ex = dict(ex)
        diff = ex.pop(TWIN_DIFF_INPUT)
        return grad_fn(weights, diff, {**shared, **ex}, loss_target)

    if N_MICROBATCH == 1:
        loss, (grad_w, grad_x) = one_microbatch(per_example, given["loss_target"])
    else:
        def body(carry, xs):
            loss_sum, grad_sum = carry
            l_k, (gw_k, gx_k) = one_microbatch(xs[0], xs[1])
            with _jax.named_scope("update"):
                return (loss_sum + l_k, _jax.tree.map(_jnp.add, grad_sum, gw_k)), gx_k

        init = (_jnp.zeros((), _jnp.float32), _jax.tree.map(_jnp.zeros_like, weights))
        (loss, grad_w), grad_x = _jax.lax.scan(body, init, (per_example, given["loss_target"]))
    with _jax.named_scope("update"):
        delta_w, new_m, new_v = {}, {}, {}
        for n in TWIN_WEIGHTS:
            delta_w[n], new_m[n], new_v[n] = _adamw(weights[n], grad_w[n], given["m_" + n], given["v_" + n])
    return (loss, grad_x, *[grad_w[n] for n in TWIN_WEIGHTS], *[delta_w[n] for n in TWIN_WEIGHTS],
            *[new_m[n] for n in TWIN_WEIGHTS], *[new_v[n] for n in TWIN_WEIGHTS])
```

```python
import functools
import math

import jax
import jax.numpy as jnp
from jax import lax
from jax.experimental import pallas as pl
from jax.experimental.pallas import tpu as pltpu

f32 = jnp.float32
bf16 = jnp.bfloat16
MXU = bf16

N_DEV = 8
S = 2048
D = 1024
EPS = 1e-5
INNER = 2048
HEADS = 32
HP = 64
GROUPS = 8
HPG = HEADS // GROUPS
NSTATE = 128
CONV_A = 4
CHUNK = 256
NCHUNK = S // CHUNK
AH = 64
QH = 16
KVH = 4
QPK = QH // KVH
WIN = 128
NBLK = S // WIN
ROPE_THETA = 10000.0
FFN = 2816
CONV_F = 3
LANES = 128
PACK_W = 1024
BIG_TILE = 496
VMEM_LIMIT = 56 * 1024 * 1024

ADAM_LR, ADAM_B1, ADAM_B2, ADAM_EPS, ADAM_WD, ADAM_STEP = 0.001, 0.9, 0.999, 1e-08, 0.01, 10

MESH = pl.DeviceIdType.MESH
AXES = ("x", "y", "c")


def _cparams(sem=None):
    return pltpu.CompilerParams(dimension_semantics=sem, vmem_limit_bytes=VMEM_LIMIT)


def _pick(n, cands):
    for c in cands:
        if n % c == 0:
            return c
    return n


def matmul(name, a, b, mode, out_dtype=f32, bias=None, residual=None):
    if mode == "nn":
        (M, K), (K2, N) = a.shape, b.shape
    elif mode == "nt":
        (M, K), (N, K2) = a.shape, b.shape
    else:
        (K, M), (K2, N) = a.shape, b.shape
    assert K == K2, (name, a.shape, b.shape)
    tm = _pick(M, (512, 256, 128))
    tn = _pick(N, (512, 256, 128))
    tk = K if K <= 2048 else _pick(K, (1408, 1024, 512))
    nk = K // tk
    if mode == "nn":
        a_spec = pl.BlockSpec((tm, tk), lambda i, j, k: (i, k))
        b_spec = pl.BlockSpec((tk, tn), lambda i, j, k: (k, j))
        dims = (((1,), (0,)), ((), ()))
    elif mode == "nt":
        a_spec = pl.BlockSpec((tm, tk), lambda i, j, k: (i, k))
        b_spec = pl.BlockSpec((tn, tk), lambda i, j, k: (j, k))
        dims = (((1,), (1,)), ((), ()))
    else:
        a_spec = pl.BlockSpec((tk, tm), lambda i, j, k: (k, i))
        b_spec = pl.BlockSpec((tk, tn), lambda i, j, k: (k, j))
        dims = (((0,), (0,)), ((), ()))
    ins, in_specs = [a, b], [a_spec, b_spec]
    if bias is not None:
        ins.append(bias)
        in_specs.append(pl.BlockSpec((1, tn), lambda i, j, k: (0, j)))
    if residual is not None:
        ins.append(residual)
        in_specs.append(pl.BlockSpec((tm, tn), lambda i, j, k: (i, j)))
    has_bias, has_res = bias is not None, residual is not None

    def body(a_ref, b_ref, *rest):
        rest = list(rest)
        bias_ref = rest.pop(0) if has_bias else None
        res_ref = rest.pop(0) if has_res else None
        o_ref, acc = rest
        k = pl.program_id(2)

        @pl.when(k == 0)
        def _():
            acc[...] = jnp.zeros_like(acc)

        acc[...] += lax.dot_general(a_ref[...].astype(MXU), b_ref[...].astype(MXU), dims, preferred_element_type=f32)

        @pl.when(k == nk - 1)
        def _():
            r = acc[...]
            if has_bias:
                r = r + bias_ref[...]
            if has_res:
                r = r + res_ref[...]
            o_ref[...] = r.astype(out_dtype)

    return pl.pallas_call(
        body, name=name, grid=(M // tm, N // tn, nk), in_specs=in_specs,
        out_specs=pl.BlockSpec((tm, tn), lambda i, j, k: (i, j)),
        out_shape=jax.ShapeDtypeStruct((M, N), out_dtype),
        scratch_shapes=[pltpu.VMEM((tm, tn), f32)],
        compiler_params=_cparams(("parallel", "parallel", "arbitrary")),
    )(*ins)


def rowwise(name, fn, rows, pars, outs, accs=(), tile=256):
    n_in, n_out = len(rows) + len(pars), len(outs)
    in_specs = [pl.BlockSpec((tile, r.shape[1]), lambda i: (i, 0)) for r in rows]
    in_specs += [pl.BlockSpec(p.shape, lambda i: (0, 0)) for p in pars]
    out_specs = [pl.BlockSpec((tile, c), lambda i: (i, 0)) for c, _ in outs]
    out_specs += [pl.BlockSpec(shp, lambda i: (0, 0)) for shp in accs]
    out_shape = [jax.ShapeDtypeStruct((S, c), dt) for c, dt in outs]
    out_shape += [jax.ShapeDtypeStruct(shp, f32) for shp in accs]

    def body(*refs):
        res = fn(*[r[...] for r in refs[:n_in]])
        o_refs = refs[n_in:n_in + n_out]
        a_refs = refs[n_in + n_out:]
        for ref, val in zip(o_refs, res[:n_out]):
            ref[...] = val.astype(ref.dtype)
        if a_refs:
            @pl.when(pl.program_id(0) == 0)
            def _():
                for ref in a_refs:
                    ref[...] = jnp.zeros_like(ref)
            for ref, val in zip(a_refs, res[n_out:]):
                ref[...] += val

    res = pl.pallas_call(
        body, name=name, grid=(S // tile,), in_specs=in_specs, out_specs=out_specs, out_shape=out_shape,
        compiler_params=_cparams(("arbitrary",) if accs else ("parallel",)),
    )(*rows, *pars)
    return res


def colwise(name, fn, cols, pars, outs, pouts=(), ct=128):
    C = cols[0].shape[1]
    n_in, n_out = len(cols) + len(pars), len(outs)
    in_specs = [pl.BlockSpec((S, ct), lambda j: (0, j)) for _ in cols]
    in_specs += [pl.BlockSpec((p.shape[0], ct), lambda j: (0, j)) for p in pars]
    out_specs = [pl.BlockSpec((S, ct), lambda j: (0, j)) for _ in outs]
    out_specs += [pl.BlockSpec((r, ct), lambda j: (0, j)) for r in pouts]
    out_shape = [jax.ShapeDtypeStruct((S, C), dt) for dt in outs]
    out_shape += [jax.ShapeDtypeStruct((r, C), f32) for r in pouts]

    def body(*refs):
        res = fn(*[r[...] for r in refs[:n_in]])
        for ref, val in zip(refs[n_in:], res):
            ref[...] = val.astype(ref.dtype)

    return pl.pallas_call(
        body, name=name, grid=(C // ct,), in_specs=in_specs, out_specs=out_specs, out_shape=out_shape,
        compiler_params=_cparams(("parallel",)),
    )(*cols, *pars)


def _sigmoid(x):
    return 1.0 / (1.0 + jnp.exp(-x))


def _silu(x):
    return x * _sigmoid(x)


def _dsilu(x):
    sg = _sigmoid(x)
    return sg * (1.0 + x * (1.0 - sg))


def _softplus(x):
    return jnp.maximum(x, 0.0) + jnp.log(1.0 + jnp.exp(-jnp.abs(x)))


def _rms_fwd(x, g):
    r = lax.rsqrt(jnp.mean(x * x, axis=-1, keepdims=True) + EPS)
    return x * r * g


def _rms_bwd(x, g, dh):
    r = lax.rsqrt(jnp.mean(x * x, axis=-1, keepdims=True) + EPS)
    xh = x * r
    dxh = dh * g
    dx = r * (dxh - xh * jnp.mean(dxh * xh, axis=-1, keepdims=True))
    return dx, jnp.sum(dh * xh, axis=0, keepdims=True)


def _shift_down(x, s):
    if s == 0:
        return x
    row = lax.broadcasted_iota(jnp.int32, x.shape, 0)
    return jnp.where(row >= s, pltpu.roll(x, s, 0), 0.0)


def _shift_up(x, s):
    if s == 0:
        return x
    n = x.shape[0]
    row = lax.broadcasted_iota(jnp.int32, x.shape, 0)
    return jnp.where(row < n - s, pltpu.roll(x, n - s, 0), 0.0)


def _conv(x, w, b):
    width = w.shape[0]
    out = b + w[width - 1:width, :] * x
    for k in range(width - 1):
        out = out + w[k:k + 1, :] * _shift_down(x, width - 1 - k)
    return out


def _conv_bwd(x, w, dc):
    width = w.shape[0]
    dx = w[width - 1:width, :] * dc
    dws = []
    for k in range(width - 1):
        s = width - 1 - k
        dx = dx + w[k:k + 1, :] * _shift_up(dc, s)
        dws.append(jnp.sum(dc * _shift_down(x, s), axis=0, keepdims=True))
    dws.append(jnp.sum(dc * x, axis=0, keepdims=True))
    return dx, jnp.concatenate(dws, axis=0), jnp.sum(dc, axis=0, keepdims=True)


def _rope_tables(pos, inv_freq):
    ang = pos * inv_freq
    return jnp.cos(ang), jnp.sin(ang)


def _headnorm_rope_fwd(x, g, cos, sin, heads):
    half = AH // 2
    outs = []
    for h in range(heads):
        seg = x[:, AH * h:AH * (h + 1)]
        n = _rms_fwd(seg, g)
        n1, n2 = n[:, :half], n[:, half:]
        outs += [n1 * cos - n2 * sin, n2 * cos + n1 * sin]
    return jnp.concatenate(outs, axis=1)


def _headnorm_rope_bwd(x, g, cos, sin, dout, heads):
    half = AH // 2
    dxs, dg = [], jnp.zeros((1, AH), f32)
    for h in range(heads):
        seg = x[:, AH * h:AH * (h + 1)]
        d = dout[:, AH * h:AH * (h + 1)]
        d1, d2 = d[:, :half], d[:, half:]
        dn = jnp.concatenate([d1 * cos + d2 * sin, d2 * cos - d1 * sin], axis=1)
        dx, dgh = _rms_bwd(seg, g, dn)
        dxs.append(dx)
        dg = dg + dgh
    return jnp.concatenate(dxs, axis=1), dg


def _ssd_prep(dt_pre, dt_bias, a_log, dt_s, acum_s, acumT_s):
    dt = _softplus(dt_pre + dt_bias)
    a = dt * (-jnp.exp(a_log))
    row = lax.broadcasted_iota(jnp.int32, (CHUNK, CHUNK), 0)
    col = lax.broadcasted_iota(jnp.int32, (CHUNK, CHUNK), 1)
    dt_s[...] = dt
    acum_s[...] = jnp.dot((col <= row).astype(f32), a, precision=lax.Precision.HIGHEST, preferred_element_type=f32)
    acumT_s[...] = lax.dot_general(a, (row <= col).astype(f32), (((0,), (0,)), ((), ())),
                                   precision=lax.Precision.HIGHEST, preferred_element_type=f32)


def _head_cols(h, dt_s, acum_s, acumT_s):
    lane = lax.broadcasted_iota(jnp.int32, (1, LANES), 1)
    oh_l = (lane == h).astype(f32)
    sub = lax.broadcasted_iota(jnp.int32, (LANES, 1), 0)
    oh_s = (sub == h).astype(f32)
    dt_h = jnp.sum(dt_s[...] * oh_l, axis=1, keepdims=True)
    ac_h = jnp.sum(acum_s[...] * oh_l, axis=1, keepdims=True)
    acr_h = jnp.sum(acumT_s[...] * oh_s, axis=0, keepdims=True)
    return oh_l, dt_h, ac_h, acr_h


def ssd_fwd(xs, Bm, Cm, dt_pre, dt_bias, a_log, d_skip):
    def body(xs_ref, b_ref, c_ref, dtp_ref, bias_ref, alog_ref, d_ref, y_ref, st_ref, state, dt_s, acum_s, acumT_s):
        c, g = pl.program_id(0), pl.program_id(1)

        @pl.when(g == 0)
        def _():
            _ssd_prep(dtp_ref[...], bias_ref[...], alog_ref[...], dt_s, acum_s, acumT_s)

        row = lax.broadcasted_iota(jnp.int32, (CHUNK, CHUNK), 0)
        col = lax.broadcasted_iota(jnp.int32, (CHUNK, CHUNK), 1)
        causal = col <= row
        Bb, Cb = b_ref[...], c_ref[...]
        cb = lax.dot_general(Cb.astype(MXU), Bb.astype(MXU), (((1,), (1,)), ((), ())), preferred_element_type=f32)
        xs_blk = xs_ref[...]
        ys = []
        for j in range(HPG):
            h = g * HPG + j
            oh_l, dt_h, ac_h, acr_h = _head_cols(h, dt_s, acum_s, acumT_s)

            @pl.when(c == 0)
            def _():
                state[h] = jnp.zeros((NSTATE, HP), f32)

            prev = state[h]
            decay = jnp.exp(jnp.where(causal, ac_h - acr_h, -1e30))
            w = (cb * decay).astype(MXU)
            xs_h = xs_blk[:, HP * j:HP * (j + 1)]
            xd = (xs_h * dt_h).astype(MXU)
            y_diag = jnp.dot(w, xd, preferred_element_type=f32)
            y_off = jnp.dot(Cb.astype(MXU), prev.astype(MXU), preferred_element_type=f32) * jnp.exp(ac_h)
            d_h = jnp.sum(d_ref[...] * oh_l, axis=1, keepdims=True)
            ys.append(y_diag + y_off + xs_h * d_h)
            a_end = ac_h[CHUNK - 1:CHUNK, :]
            dte = jnp.exp(a_end - ac_h)
            s_c = lax.dot_general((Bb * dte).astype(MXU), xd, (((0,), (0,)), ((), ())), preferred_element_type=f32)
            st_ref[0, j] = prev
            state[h] = prev * jnp.exp(a_end) + s_c
        y_ref[...] = jnp.concatenate(ys, axis=1)

    par = pl.BlockSpec((1, LANES), lambda c, g: (0, 0))
    return pl.pallas_call(
        body, name="ssd_fwd", grid=(NCHUNK, GROUPS),
        in_specs=[pl.BlockSpec((CHUNK, HPG * HP), lambda c, g: (c, g)),
                  pl.BlockSpec((CHUNK, NSTATE), lambda c, g: (c, g)),
                  pl.BlockSpec((CHUNK, NSTATE), lambda c, g: (c, g)),
                  pl.BlockSpec((CHUNK, LANES), lambda c, g: (c, 0)), par, par, par],
        out_specs=[pl.BlockSpec((CHUNK, HPG * HP), lambda c, g: (c, g)),
                   pl.BlockSpec((1, HPG, NSTATE, HP), lambda c, g: (c, g, 0, 0))],
        out_shape=[jax.ShapeDtypeStruct((S, INNER), f32), jax.ShapeDtypeStruct((NCHUNK, HEADS, NSTATE, HP), f32)],
        scratch_shapes=[pltpu.VMEM((HEADS, NSTATE, HP), f32), pltpu.VMEM((CHUNK, LANES), f32),
                        pltpu.VMEM((CHUNK, LANES), f32), pltpu.VMEM((LANES, CHUNK), f32)],
        compiler_params=_cparams(("arbitrary", "arbitrary")),
    )(xs, Bm, Cm, dt_pre, dt_bias, a_log, d_skip)


def ssd_bwd(xs, Bm, Cm, dt_pre, dt_bias, a_log, d_skip, states, dy):
    rev = lambda c: NCHUNK - 1 - c

    def body(xs_ref, b_ref, c_ref, dtp_ref, bias_ref, alog_ref, d_ref, st_ref, dy_ref,
             dxs_ref, db_ref, dc_ref, ddt_ref, dbias_ref, dalog_ref, dd_ref,
             dstate, dt_s, acum_s, acumT_s, dacum_s, ddt_s, da_s):
        c, g = pl.program_id(0), pl.program_id(1)

        @pl.when(g == 0)
        def _():
            _ssd_prep(dtp_ref[...], bias_ref[...], alog_ref[...], dt_s, acum_s, acumT_s)
            dacum_s[...] = jnp.zeros_like(dacum_s)
            ddt_s[...] = jnp.zeros_like(ddt_s)

        @pl.when((c == 0) & (g == 0))
        def _():
            da_s[...] = jnp.zeros_like(da_s)
            dd_ref[...] = jnp.zeros_like(dd_ref)
            dbias_ref[...] = jnp.zeros_like(dbias_ref)
            dalog_ref[...] = jnp.zeros_like(dalog_ref)

        row = lax.broadcasted_iota(jnp.int32, (CHUNK, CHUNK), 0)
        col = lax.broadcasted_iota(jnp.int32, (CHUNK, CHUNK), 1)
        sub_l = lax.broadcasted_iota(jnp.int32, (CHUNK, 1), 0)
        last = (sub_l == CHUNK - 1).astype(f32)
        nt = (((1,), (1,)), ((), ()))
        tn = (((0,), (0,)), ((), ()))
        Bb, Cb = b_ref[...], c_ref[...]
        Bm_, Cm_ = Bb.astype(MXU), Cb.astype(MXU)
        cb = lax.dot_general(Cm_, Bm_, nt, preferred_element_type=f32)
        bc = lax.dot_general(Bm_, Cm_, nt, preferred_element_type=f32)
        xs_blk, dy_blk = xs_ref[...], dy_ref[...]
        dxs, dB, dC = [], jnp.zeros((CHUNK, NSTATE), f32), jnp.zeros((CHUNK, NSTATE), f32)
        for j in range(HPG):
            h = g * HPG + j
            oh_l, dt_h, ac_h, acr_h = _head_cols(h, dt_s, acum_s, acumT_s)

            @pl.when(c == 0)
            def _():
                dstate[h] = jnp.zeros((NSTATE, HP), f32)

            dnext = dstate[h]
            prev = st_ref[0, j]
            lm = jnp.exp(jnp.where(col <= row, ac_h - acr_h, -1e30))
            lmT = jnp.exp(jnp.where(row <= col, acr_h - ac_h, -1e30))
            xs_h = xs_blk[:, HP * j:HP * (j + 1)]
            dy_h = dy_blk[:, HP * j:HP * (j + 1)]
            xd = xs_h * dt_h
            xdm, dym = xd.astype(MXU), dy_h.astype(MXU)
            ea = jnp.exp(ac_h)
            a_end = ac_h[CHUNK - 1:CHUNK, :]
            e_end = jnp.exp(a_end)
            dte = jnp.exp(a_end - ac_h)
            dnm, pvm = dnext.astype(MXU), prev.astype(MXU)
            bd = jnp.dot(Bm_, dnm, preferred_element_type=f32)
            dxd = jnp.dot((bc * lmT).astype(MXU), dym, preferred_element_type=f32) + dte * bd
            dw = lax.dot_general(dym, xdm, nt, preferred_element_type=f32)
            dwT = lax.dot_general(xdm, dym, nt, preferred_element_type=f32)
            dcb = dw * lm
            dbc = dwT * lmT
            eady = (ea * dy_h).astype(MXU)
            dC = dC + jnp.dot(dcb.astype(MXU), Bm_, preferred_element_type=f32) \
                + lax.dot_general(eady, pvm, nt, preferred_element_type=f32)
            dB = dB + jnp.dot(dbc.astype(MXU), Cm_, preferred_element_type=f32) \
                + dte * lax.dot_general(xdm, dnm, nt, preferred_element_type=f32)
            dstate[h] = lax.dot_general(Cm_, eady, tn, preferred_element_type=f32) + e_end * dnext
            r1 = jnp.sum(dcb * cb, axis=1, keepdims=True)
            r2 = jnp.sum(dbc * bc, axis=1, keepdims=True)
            y_off = jnp.dot(Cm_, pvm, preferred_element_type=f32) * ea
            t3 = jnp.sum(dy_h * y_off, axis=1, keepdims=True)
            t4 = jnp.sum(bd * xd, axis=1, keepdims=True) * dte
            end_extra = jnp.sum(t4, axis=0, keepdims=True) + e_end * jnp.sum(jnp.sum(prev * dnext, axis=1, keepdims=True), axis=0, keepdims=True)
            dacum_h = r1 - r2 + t3 - t4 + last * end_extra
            dacum_s[...] += dacum_h * oh_l
            ddt_s[...] += jnp.sum(dxd * xs_h, axis=1, keepdims=True) * oh_l
            d_h = jnp.sum(d_ref[...] * oh_l, axis=1, keepdims=True)
            dxs.append(dxd * dt_h + dy_h * d_h)
            dd_ref[...] += oh_l * jnp.sum(jnp.sum(dy_h * xs_h, axis=1, keepdims=True), axis=0, keepdims=True)
        dxs_ref[...] = jnp.concatenate(dxs, axis=1)
        db_ref[...] = dB
        dc_ref[...] = dC

        @pl.when(g == GROUPS - 1)
        def _():
            a_row = -jnp.exp(alog_ref[...])
            da = jnp.dot((row <= col).astype(f32), dacum_s[...], precision=lax.Precision.HIGHEST, preferred_element_type=f32)
            da_s[...] += jnp.sum(da * dt_s[...], axis=0, keepdims=True)
            z = dtp_ref[...] + bias_ref[...]
            ddt_pre = (ddt_s[...] + da * a_row) * _sigmoid(z)
            ddt_ref[...] = ddt_pre
            dbias_ref[...] += jnp.sum(ddt_pre, axis=0, keepdims=True)

            @pl.when(c == NCHUNK - 1)
            def _():
                dalog_ref[...] = da_s[...] * a_row

    par = pl.BlockSpec((1, LANES), lambda c, g: (0, 0))
    return pl.pallas_call(
        body, name="ssd_bwd", grid=(NCHUNK, GROUPS),
        in_specs=[pl.BlockSpec((CHUNK, HPG * HP), lambda c, g: (rev(c), g)),
                  pl.BlockSpec((CHUNK, NSTATE), lambda c, g: (rev(c), g)),
                  pl.BlockSpec((CHUNK, NSTATE), lambda c, g: (rev(c), g)),
                  pl.BlockSpec((CHUNK, LANES), lambda c, g: (rev(c), 0)), par, par, par,
                  pl.BlockSpec((1, HPG, NSTATE, HP), lambda c, g: (rev(c), g, 0, 0)),
                  pl.BlockSpec((CHUNK, HPG * HP), lambda c, g: (rev(c), g))],
        out_specs=[pl.BlockSpec((CHUNK, HPG * HP), lambda c, g: (rev(c), g)),
                   pl.BlockSpec((CHUNK, NSTATE), lambda c, g: (rev(c), g)),
                   pl.BlockSpec((CHUNK, NSTATE), lambda c, g: (rev(c), g)),
                   pl.BlockSpec((CHUNK, LANES), lambda c, g: (rev(c), 0)), par, par, par],
        out_shape=[jax.ShapeDtypeStruct((S, INNER), f32), jax.ShapeDtypeStruct((S, GROUPS * NSTATE), f32),
                   jax.ShapeDtypeStruct((S, GROUPS * NSTATE), f32), jax.ShapeDtypeStruct((S, LANES), f32),
                   jax.ShapeDtypeStruct((1, LANES), f32), jax.ShapeDtypeStruct((1, LANES), f32),
                   jax.ShapeDtypeStruct((1, LANES), f32)],
        scratch_shapes=[pltpu.VMEM((HEADS, NSTATE, HP), f32), pltpu.VMEM((CHUNK, LANES), f32),
                        pltpu.VMEM((CHUNK, LANES), f32), pltpu.VMEM((LANES, CHUNK), f32),
                        pltpu.VMEM((CHUNK, LANES), f32), pltpu.VMEM((CHUNK, LANES), f32), pltpu.VMEM((1, LANES), f32)],
        compiler_params=_cparams(("arbitrary", "arbitrary")),
    )(xs, Bm, Cm, dt_pre, dt_bias, a_log, d_skip, states, dy)


def _attn_block(n, q, kp, kc, vp, vc, sinks, hk):
    rows = QPK * WIN
    qi = lax.broadcasted_iota(jnp.int32, (rows, 2 * WIN), 0) & (WIN - 1)
    ki = lax.broadcasted_iota(jnp.int32, (rows, 2 * WIN), 1)
    rel = qi + WIN - ki
    mask = (rel >= 0) & (rel < WIN) & ((ki >= WIN) | (n > 0))
    sl = slice(AH * hk, AH * (hk + 1))
    kb = jnp.concatenate([kp[:, sl], kc[:, sl]], axis=0).astype(MXU)
    vb = jnp.concatenate([vp[:, sl], vc[:, sl]], axis=0).astype(MXU)
    qg = jnp.concatenate([q[:, AH * (QPK * hk + g):AH * (QPK * hk + g + 1)] for g in range(QPK)], axis=0).astype(MXU)
    s = lax.dot_general(qg, kb, (((1,), (1,)), ((), ())), preferred_element_type=f32) * (AH ** -0.5)
    s = jnp.where(mask, s, -1e30)
    sink = jnp.concatenate([jnp.broadcast_to(sinks[:, QPK * hk + g:QPK * hk + g + 1], (WIN, 1)) for g in range(QPK)], axis=0)
    m = jnp.maximum(jnp.max(s, axis=1, keepdims=True), sink)
    p = jnp.exp(s - m)
    ps = jnp.exp(sink - m)
    den = jnp.sum(p, axis=1, keepdims=True) + ps
    return qg, kb, vb, p / den, ps / den


def _kv_specs():
    prev = lambda n: (jnp.maximum(n - 1, 0), 0)
    cur = lambda n: (n, 0)
    w = KVH * AH
    return [pl.BlockSpec((WIN, w), prev), pl.BlockSpec((WIN, w), cur), pl.BlockSpec((WIN, w), prev), pl.BlockSpec((WIN, w), cur)]


def attn_fwd(q, k, v, sinks):
    def body(q_ref, kp_ref, kc_ref, vp_ref, vc_ref, s_ref, o_ref):
        n = pl.program_id(0)
        q_, kp, kc, vp, vc, sk = q_ref[...], kp_ref[...], kc_ref[...], vp_ref[...], vc_ref[...], s_ref[...]
        outs = []
        for hk in range(KVH):
            _, _, vb, pr, _ = _attn_block(n, q_, kp, kc, vp, vc, sk, hk)
            o = jnp.dot(pr.astype(MXU), vb, preferred_element_type=f32)
            outs += [o[WIN * g:WIN * (g + 1)] for g in range(QPK)]
        o_ref[...] = jnp.concatenate(outs, axis=1)

    return pl.pallas_call(
        body, name="attn_fwd", grid=(NBLK,),
        in_specs=[pl.BlockSpec((WIN, D), lambda n: (n, 0))] + _kv_specs() + [pl.BlockSpec((1, QH), lambda n: (0, 0))],
        out_specs=pl.BlockSpec((WIN, D), lambda n: (n, 0)),
        out_shape=jax.ShapeDtypeStruct((S, D), f32),
        compiler_params=_cparams(("parallel",)),
    )(q, k, k, v, v, sinks)


def attn_bwd(q, k, v, sinks, dout):
    def body(q_ref, kp_ref, kc_ref, vp_ref, vc_ref, s_ref, do_ref, dq_ref, dkp_ref, dkc_ref, dvp_ref, dvc_ref, ds_ref):
        n = pl.program_id(0)

        @pl.when(n == 0)
        def _():
            ds_ref[...] = jnp.zeros_like(ds_ref)

        q_, kp, kc, vp, vc, sk, do = q_ref[...], kp_ref[...], kc_ref[...], vp_ref[...], vc_ref[...], s_ref[...], do_ref[...]
        lane = lax.broadcasted_iota(jnp.int32, (1, QH), 1)
        nt = (((1,), (1,)), ((), ()))
        tn = (((0,), (0,)), ((), ()))
        dqs, dkps, dkcs, dvps, dvcs = [], [], [], [], []
        dsink = jnp.zeros((1, QH), f32)
        for hk in range(KVH):
            qg, kb, vb, pr, prs = _attn_block(n, q_, kp, kc, vp, vc, sk, hk)
            dog = jnp.concatenate([do[:, AH * (QPK * hk + g):AH * (QPK * hk + g + 1)] for g in range(QPK)], axis=0).astype(MXU)
            dp = lax.dot_general(dog, vb, nt, preferred_element_type=f32)
            dvb = lax.dot_general(pr.astype(MXU), dog, tn, preferred_element_type=f32)
            delta = jnp.sum(pr * dp, axis=1, keepdims=True)
            ds = (pr * (dp - delta)).astype(MXU)
            dsk = -prs * delta
            for g in range(QPK):
                dsink = dsink + jnp.sum(dsk[WIN * g:WIN * (g + 1)], axis=0, keepdims=True) * (lane == QPK * hk + g).astype(f32)
            dqg = jnp.dot(ds, kb, preferred_element_type=f32) * (AH ** -0.5)
            dkb = lax.dot_general(ds, qg, tn, preferred_element_type=f32) * (AH ** -0.5)
            dqs += [dqg[WIN * g:WIN * (g + 1)] for g in range(QPK)]
            dkps.append(dkb[:WIN])
            dkcs.append(dkb[WIN:])
            dvps.append(dvb[:WIN])
            dvcs.append(dvb[WIN:])
        dq_ref[...] = jnp.concatenate(dqs, axis=1)
        dkp_ref[...] = jnp.concatenate(dkps, axis=1)
        dkc_ref[...] = jnp.concatenate(dkcs, axis=1)
        dvp_ref[...] = jnp.concatenate(dvps, axis=1)
        dvc_ref[...] = jnp.concatenate(dvcs, axis=1)
        ds_ref[...] += dsink

    w = KVH * AH
    blk = lambda width: pl.BlockSpec((WIN, width), lambda n: (n, 0))
    return pl.pallas_call(
        body, name="attn_bwd", grid=(NBLK,),
        in_specs=[blk(D)] + _kv_specs() + [pl.BlockSpec((1, QH), lambda n: (0, 0)), blk(D)],
        out_specs=[blk(D), blk(w), blk(w), blk(w), blk(w), pl.BlockSpec((1, QH), lambda n: (0, 0))],
        out_shape=[jax.ShapeDtypeStruct((S, D), f32)] + [jax.ShapeDtypeStruct((S, w), f32)] * 4 + [jax.ShapeDtypeStruct((1, QH), f32)],
        compiler_params=_cparams(("arbitrary",)),
    )(q, k, k, v, v, sinks, dout)


def kv_bwd(kv, pos, inv_freq, k_norm, dkp, dkc, dvp, dvc):
    w = KVH * AH

    def body(kv_ref, pos_ref, if_ref, g_ref, dkp_ref, dkc_ref, dvp_ref, dvc_ref, o_ref, dg_ref, db_ref):
        n = pl.program_id(0)

        @pl.when(n == 0)
        def _():
            dg_ref[...] = jnp.zeros_like(dg_ref)
            db_ref[...] = jnp.zeros_like(db_ref)

        inside = (n < NBLK - 1).astype(f32)
        dk = dkc_ref[...] + inside * dkp_ref[...]
        dv = dvc_ref[...] + inside * dvp_ref[...]
        cos, sin = _rope_tables(pos_ref[...], if_ref[...])
        dkpre, dg = _headnorm_rope_bwd(kv_ref[...], g_ref[...], cos, sin, dk, KVH)
        dkv = jnp.concatenate([dkpre, dv], axis=1)
        o_ref[...] = dkv
        dg_ref[...] += dg
        db_ref[...] += jnp.sum(dkv, axis=0, keepdims=True)

    nxt = lambda n: (jnp.minimum(n + 1, NBLK - 1), 0)
    cur = lambda n: (n, 0)
    const = lambda n: (0, 0)
    return pl.pallas_call(
        body, name="kv_bwd", grid=(NBLK,),
        in_specs=[pl.BlockSpec((WIN, w), cur), pl.BlockSpec((WIN, 1), cur), pl.BlockSpec((1, AH // 2), const),
                  pl.BlockSpec((1, AH), const), pl.BlockSpec((WIN, w), nxt), pl.BlockSpec((WIN, w), cur),
                  pl.BlockSpec((WIN, w), nxt), pl.BlockSpec((WIN, w), cur)],
        out_specs=[pl.BlockSpec((WIN, 2 * w), cur), pl.BlockSpec((1, AH), const), pl.BlockSpec((1, 2 * w), const)],
        out_shape=[jax.ShapeDtypeStruct((S, 2 * w), f32), jax.ShapeDtypeStruct((1, AH), f32), jax.ShapeDtypeStruct((1, 2 * w), f32)],
        compiler_params=_cparams(("arbitrary",)),
    )(kv, pos, inv_freq, k_norm, dkp, dkc, dvp, dvc)


def adamw(name, w, g, m, v):
    R, C = w.shape
    tr = _pick(R, (256, 128, 64, 32, 16, 8))

    def body(w_ref, g_ref, m_ref, v_ref, d_ref, nm_ref, nv_ref):
        g_ = g_ref[...]
        m_ = ADAM_B1 * m_ref[...] + (1.0 - ADAM_B1) * g_
        v_ = ADAM_B2 * v_ref[...] + (1.0 - ADAM_B2) * (g_ * g_)
        m_hat = m_ / (1.0 - ADAM_B1 ** ADAM_STEP)
        v_hat = v_ / (1.0 - ADAM_B2 ** ADAM_STEP)
        d_ref[...] = -ADAM_LR * (m_hat / (jnp.sqrt(v_hat) + ADAM_EPS) + ADAM_WD * w_ref[...])
        nm_ref[...] = m_
        nv_ref[...] = v_

    spec = pl.BlockSpec((tr, C), lambda i: (i, 0))
    return pl.pallas_call(
        body, name=name, grid=(R // tr,), in_specs=[spec] * 4, out_specs=[spec] * 3,
        out_shape=[jax.ShapeDtypeStruct((R, C), f32)] * 3, compiler_params=_cparams(("parallel",)),
    )(w, g, m, v)


def _me():
    return lax.axis_index("x"), lax.axis_index("y"), lax.axis_index("c")


def all_gather(name, xs):
    n = len(xs)

    def body(*refs):
        x_refs, o_refs = refs[:n], refs[n:2 * n]
        send_sems, recv_sems, local_sems = refs[2 * n:]
        x, y, c = _me()
        me, sibling = (x, y, c), (x, y, 1 - c)
        chips = [(1 - x, y), (x, 1 - y), (1 - x, 1 - y)]

        def copy(a, k, block, to, src=None):
            dst = o_refs[a].at[4 * block[0] + 2 * block[1] + block[2]]
            return pltpu.make_async_remote_copy(
                src_ref=dst if src is None else src, dst_ref=dst,
                send_sem=send_sems.at[7 * a + k], recv_sem=recv_sems.at[7 * a + k], device_id=to, device_id_type=MESH)

        mine = [pltpu.make_async_copy(x_refs[a], o_refs[a].at[4 * x + 2 * y + c], local_sems.at[a]) for a in range(n)]
        for cp in mine:
            cp.start()
        first = []
        for a in range(n):
            first.append(copy(a, 0, me, sibling, src=x_refs[a]))
            first += [copy(a, 1 + j, me, (*chip, c), src=x_refs[a]) for j, chip in enumerate(chips)]
        for cp in first:
            cp.start()
        passed = []
        for j, chip in enumerate(chips):
            for a in range(n):
                copy(a, 1 + j, (*chip, c), me).wait_recv()
                cp = copy(a, 4 + j, (*chip, c), sibling)
                cp.start()
                passed.append(cp)
        for a in range(n):
            copy(a, 0, sibling, me).wait_recv()
            for j, chip in enumerate(chips):
                copy(a, 4 + j, (*chip, 1 - c), me).wait_recv()
        for cp in first + passed:
            cp.wait_send()
        for cp in mine:
            cp.wait()

    any_spec = pl.BlockSpec(memory_space=pl.ANY)
    return pl.pallas_call(
        body, name=name, in_specs=[any_spec] * n, out_specs=[any_spec] * n,
        out_shape=[jax.ShapeDtypeStruct((N_DEV,) + a.shape, a.dtype) for a in xs],
        scratch_shapes=[pltpu.SemaphoreType.DMA((7 * n,)), pltpu.SemaphoreType.DMA((7 * n,)), pltpu.SemaphoreType.DMA((n,))],
    )(*xs)


def sibling_exchange(gs):
    n = len(gs)

    def body(*refs):
        g_refs, o_refs, send_sems, recv_sems = refs[:n], refs[n:2 * n], refs[2 * n], refs[2 * n + 1]
        x, y, c = _me()
        cps = [pltpu.make_async_remote_copy(
            src_ref=g_refs[a].at[:, 1 - c], dst_ref=o_refs[a], send_sem=send_sems.at[a], recv_sem=recv_sems.at[a],
            device_id=(x, y, 1 - c), device_id_type=MESH) for a in range(n)]
        for cp in cps:
            cp.start()
        for cp in cps:
            cp.wait()

    any_spec = pl.BlockSpec(memory_space=pl.ANY)
    return pl.pallas_call(
        body, name="rs_sibling_exchange", in_specs=[any_spec] * n, out_specs=[any_spec] * n,
        out_shape=[jax.ShapeDtypeStruct((4,) + g.shape[2:], g.dtype) for g in gs],
        scratch_shapes=[pltpu.SemaphoreType.DMA((n,)), pltpu.SemaphoreType.DMA((n,))],
    )(*gs)


def chip_exchange(ts):
    n = len(ts)

    def body(*refs):
        t_refs, o_refs, send_sems, recv_sems = refs[:n], refs[n:2 * n], refs[2 * n], refs[2 * n + 1]
        x, y, c = _me()
        chips = [(1 - x, y), (x, 1 - y), (1 - x, 1 - y)]
        cps = [pltpu.make_async_remote_copy(
            src_ref=t_refs[a].at[2 * px + py], dst_ref=o_refs[a].at[j],
            send_sem=send_sems.at[3 * a + j], recv_sem=recv_sems.at[3 * a + j],
            device_id=(px, py, c), device_id_type=MESH) for j, (px, py) in enumerate(chips) for a in range(n)]
        for cp in cps:
            cp.start()
        for cp in cps:
            cp.wait()

    any_spec = pl.BlockSpec(memory_space=pl.ANY)
    return pl.pallas_call(
        body, name="rs_chip_exchange", in_specs=[any_spec] * n, out_specs=[any_spec] * n,
        out_shape=[jax.ShapeDtypeStruct((3,) + t.shape[1:], t.dtype) for t in ts],
        scratch_shapes=[pltpu.SemaphoreType.DMA((3 * n,)), pltpu.SemaphoreType.DMA((3 * n,))],
    )(*ts)


def _row_tile(rows):
    return _pick(rows, (512, 304, 256, 128))


def pair_add(name, g, r):
    _, _, R, C = g.shape
    tr = _row_tile(R)

    def body(c_ref, g_ref, r_ref, o_ref):
        o_ref[0] = (g_ref[0, 0].astype(f32) + r_ref[0].astype(f32)).astype(o_ref.dtype)

    return pl.pallas_call(
        body, name=name,
        grid_spec=pltpu.PrefetchScalarGridSpec(
            num_scalar_prefetch=1, grid=(4, R // tr),
            in_specs=[pl.BlockSpec((1, 1, tr, C), lambda p, i, c: (p, c[0], i, 0)),
                      pl.BlockSpec((1, tr, C), lambda p, i, c: (p, i, 0))],
            out_specs=pl.BlockSpec((1, tr, C), lambda p, i, c: (p, i, 0))),
        out_shape=jax.ShapeDtypeStruct((4, R, C), g.dtype),
        compiler_params=_cparams(("parallel", "parallel")),
    )(lax.axis_index("c").reshape(1).astype(jnp.int32), g, r)


def final_add(name, t, r):
    _, R, C = t.shape
    tr = _row_tile(R)

    def body(p_ref, t_ref, r_ref, o_ref):
        o_ref[...] = ((t_ref[0].astype(f32) + r_ref[0].astype(f32)) + r_ref[1].astype(f32)) + r_ref[2].astype(f32)

    chip = 2 * lax.axis_index("x") + lax.axis_index("y")
    return pl.pallas_call(
        body, name=name,
        grid_spec=pltpu.PrefetchScalarGridSpec(
            num_scalar_prefetch=1, grid=(R // tr,),
            in_specs=[pl.BlockSpec((1, tr, C), lambda i, p: (p[0], i, 0)),
                      pl.BlockSpec((3, tr, C), lambda i, p: (0, i, 0))],
            out_specs=pl.BlockSpec((tr, C), lambda i, p: (i, 0))),
        out_shape=jax.ShapeDtypeStruct((R, C), f32),
        compiler_params=_cparams(("parallel",)),
    )(chip.reshape(1).astype(jnp.int32), t, r)


def reduce_scatter(gs):
    gs = [g.reshape((4, 2) + g.shape[1:]) for g in gs]
    pairs = [pair_add(f"rs_pair_add_{i}", g, r) for i, (g, r) in enumerate(zip(gs, sibling_exchange(gs)))]
    return [final_add(f"rs_final_add_{i}", t, r) for i, (t, r) in enumerate(zip(pairs, chip_exchange(pairs)))]


IN_SPLIT = (("z", INNER), ("xs", INNER), ("B", GROUPS * NSTATE), ("C", GROUPS * NSTATE), ("dt", LANES))
IN_COLS = 2 * INNER + 2 * GROUPS * NSTATE + HEADS


def assemble_in_proj(g):
    tr = 256

    def body(g_ref, *o_refs):
        whole = jnp.concatenate([g_ref[j, 0] for j in range(N_DEV)], axis=1)
        o = 0
        for ref, (_, w) in zip(o_refs[:-1], IN_SPLIT[:-1]):
            ref[...] = whole[:, o:o + w]
            o += w
        o_refs[-1][...] = jnp.concatenate([whole[:, o:], jnp.zeros((tr, LANES - HEADS), whole.dtype)], axis=1)

    return pl.pallas_call(
        body, name="assemble_in_proj", grid=(D // tr,),
        in_specs=[pl.BlockSpec((N_DEV, 1, tr, IN_COLS // N_DEV), lambda i: (0, 0, i, 0))],
        out_specs=[pl.BlockSpec((tr, w), lambda i: (i, 0)) for _, w in IN_SPLIT],
        out_shape=[jax.ShapeDtypeStruct((D, w), g.dtype) for _, w in IN_SPLIT],
        compiler_params=_cparams(("parallel",)),
    )(g)


def shards_in_proj(parts):
    tr = 256
    ws = IN_COLS // N_DEV

    def body(*refs):
        vals = [r[...] for r in refs[:-1]]
        whole = jnp.concatenate(vals[:-1] + [vals[-1][:, :HEADS]], axis=1)
        for j in range(N_DEV):
            refs[-1][j] = whole[:, ws * j:ws * (j + 1)].astype(MXU)

    return pl.pallas_call(
        body, name="shards_in_proj", grid=(D // tr,),
        in_specs=[pl.BlockSpec((tr, w), lambda i: (i, 0)) for _, w in IN_SPLIT],
        out_specs=pl.BlockSpec((N_DEV, tr, ws), lambda i: (0, i, 0)),
        out_shape=jax.ShapeDtypeStruct((N_DEV, D, ws), MXU),
        compiler_params=_cparams(("parallel",)),
    )(*parts)


def assemble_ffn_in(g):
    tr = 256
    half = N_DEV // 2

    def body(g_ref, gate_ref, val_ref):
        gate_ref[0] = jnp.concatenate([g_ref[j, 0] for j in range(half)], axis=1)
        val_ref[0] = jnp.concatenate([g_ref[j, 0] for j in range(half, N_DEV)], axis=1)

    return pl.pallas_call(
        body, name="assemble_ffn_in", grid=(2, D // tr),
        in_specs=[pl.BlockSpec((N_DEV, 1, tr, 2 * FFN // N_DEV), lambda l, i: (0, l, i, 0))],
        out_specs=[pl.BlockSpec((1, tr, FFN), lambda l, i: (l, i, 0))] * 2,
        out_shape=[jax.ShapeDtypeStruct((2, D, FFN), g.dtype)] * 2,
        compiler_params=_cparams(("parallel", "parallel")),
    )(g)


def shards_ffn_in(gate0, val0, gate1, val1):
    tr = 256
    half = N_DEV // 2
    ws = 2 * FFN // N_DEV

    def body(g0, v0, g1, v1, o_ref):
        for l, (g_ref, v_ref) in enumerate(((g0, v0), (g1, v1))):
            gate, val = g_ref[...], v_ref[...]
            for j in range(half):
                o_ref[j, l] = gate[:, ws * j:ws * (j + 1)].astype(MXU)
                o_ref[half + j, l] = val[:, ws * j:ws * (j + 1)].astype(MXU)

    return pl.pallas_call(
        body, name="shards_ffn_in", grid=(D // tr,),
        in_specs=[pl.BlockSpec((tr, FFN), lambda i: (i, 0))] * 4,
        out_specs=pl.BlockSpec((N_DEV, 2, tr, ws), lambda i: (0, 0, i, 0)),
        out_shape=jax.ShapeDtypeStruct((N_DEV, 2, D, ws), MXU),
        compiler_params=_cparams(("parallel",)),
    )(gate0, val0, gate1, val1)


def sum_devices(g):
    def body(g_ref, o_ref):
        acc = g_ref[0]
        for i in range(1, N_DEV):
            acc = acc + g_ref[i]
        o_ref[...] = acc

    return pl.pallas_call(body, name="sum_devices", out_shape=jax.ShapeDtypeStruct(g.shape[1:], f32),
                          compiler_params=_cparams())(g)


def _pack(parts, unit, dtype, lead=()):
    flat = jnp.concatenate([p.reshape(lead + (-1,)).astype(dtype) for p in parts], axis=-1)
    n = flat.shape[-1]
    rows = -(-n // (unit * PACK_W)) * unit
    flat = jnp.pad(flat, [(0, 0)] * len(lead) + [(0, rows * PACK_W - n)])
    return flat.reshape(lead + (rows, PACK_W))


def _unpack(buf, shapes, lead=()):
    flat = buf.reshape(lead + (-1,))
    out, off = [], 0
    for shp in shapes:
        n = math.prod(shp)
        out.append(flat[..., off:off + n].reshape(lead + tuple(shp)))
        off += n
    return out


def _pad_lanes(a):
    return jnp.pad(a, [(0, 0)] * (a.ndim - 1) + [(0, LANES - a.shape[-1])])


def _ffn_fwd(tag, x, norm_g, w_gate, w_val, conv_w, conv_b, w_down):
    (h,) = rowwise(f"{tag}_norm", lambda x_, g_: (_rms_fwd(x_, g_),), [x], [norm_g], [(D, MXU)])
    gate_pre = matmul(f"{tag}_gate", h, w_gate, "nn")
    val = matmul(f"{tag}_val", h, w_val, "nn")
    (act,) = colwise(f"{tag}_mid", lambda gp, v_, w_, b_: (_silu(_conv(gp, w_, b_)) * v_,), [gate_pre, val], [conv_w, conv_b], [MXU])
    out = matmul(f"{tag}_down", act, w_down, "nn", residual=x)
    return out, (x, h, gate_pre, val, act)


def _ffn_bwd(tag, saved, norm_g, w_gate, w_val, conv_w, conv_b, w_down, dout):
    x, h, gate_pre, val, act = saved
    dact = matmul(f"{tag}_ddown", dout, w_down, "nt")
    g_down = matmul(f"{tag}_wdown", act, dout, "tn")

    def mid_bwd(gp, v_, da, w_, b_):
        gate = _conv(gp, w_, b_)
        dgate = da * v_ * _dsilu(gate)
        dgp, dw, db = _conv_bwd(gp, w_, dgate)
        return dgp, da * _silu(gate), dw, db

    dgp, dval, g_cw, g_cb = colwise(f"{tag}_dmid", mid_bwd, [gate_pre, val, dact], [conv_w, conv_b], [f32, f32], [CONV_F, 1])
    dh = matmul(f"{tag}_dgate", dgp, w_gate, "nt")
    dh = matmul(f"{tag}_dval", dval, w_val, "nt", residual=dh)
    g_gate = matmul(f"{tag}_wgate", h, dgp, "tn")
    g_val = matmul(f"{tag}_wval", h, dval, "tn")

    def norm_bwd(x_, dh_, do_, g_):
        dx, dg = _rms_bwd(x_, g_, dh_)
        return do_ + dx, dg

    dx, g_norm = rowwise(f"{tag}_dnorm", norm_bwd, [x, dh, dout], [norm_g], [(D, f32)], [(1, D)])
    return dx, dict(norm=g_norm, gate=g_gate, val=g_val, conv_w=g_cw, conv_b=g_cb, down=g_down)


def _local_step(x, pos, tgt, W):
    G = {}
    inv_freq = (ROPE_THETA ** (-jnp.arange(AH // 2, dtype=f32) / (AH // 2))).reshape(1, AH // 2)

    (h0,) = rowwise("a_norm", lambda x_, g_: (_rms_fwd(x_, g_),), [x], [W["a_norm"]], [(D, MXU)])
    z = matmul("a_in_z", h0, W["in_z"], "nn")
    pre = {k: matmul(f"a_in_{k}", h0, W[f"in_{k}"], "nn") for k in ("xs", "B", "C")}
    dt_pre = matmul("a_in_dt", h0, W["in_dt"], "nn")
    conv = {}
    for k in ("xs", "B", "C"):
        (conv[k],) = colwise(f"a_conv_{k}", lambda p_, w_, b_: (_silu(_conv(p_, w_, b_)),), [pre[k]], [W[f"cw_{k}"], W[f"cb_{k}"]], [f32])
    y, states = ssd_fwd(conv["xs"], conv["B"], conv["C"], dt_pre, W["dt_bias"], W["A_log"], W["D"])

    def gate_norm(y_, z_, g_):
        yg = y_ * _silu(z_)
        w = INNER // GROUPS
        return (jnp.concatenate([_rms_fwd(yg[:, w * i:w * (i + 1)], g_[:, w * i:w * (i + 1)]) for i in range(GROUPS)], axis=1),)

    (gn,) = rowwise("a_gnorm", gate_norm, [y, z], [W["a_gnorm"]], [(INNER, MXU)], tile=128)
    x1 = matmul("a_out", gn, W["a_out"], "nn", residual=x)

    x2, ffn0 = _ffn_fwd("f0", x1, W["f_norm0"], W["f_gate0"], W["f_val0"], W["f_cw0"], W["f_cb0"], W["f_down0"])

    (kvn,) = rowwise("kv_norm", lambda x_, g_: (_rms_fwd(x_, g_),), [x2], [W["kv_norm"]], [(D, MXU)])
    kv = matmul("kv_proj", kvn, W["w_kv"], "nn", bias=W["b_kv"])
    kw = KVH * AH

    def k_fwd(kv_, pos_, if_, g_):
        cos, sin = _rope_tables(pos_, if_)
        return _headnorm_rope_fwd(kv_[:, :kw], g_, cos, sin, KVH), kv_[:, kw:]

    k_rot, v_val = rowwise("k_rope", k_fwd, [kv, pos], [inv_freq, W["k_norm"]], [(kw, f32), (kw, f32)])
    (h2,) = rowwise("b_norm", lambda x_, g_: (_rms_fwd(x_, g_),), [x2], [W["b_norm"]], [(D, MXU)])
    q_pre = matmul("q_proj", h2, W["w_q"], "nn", bias=W["b_q"])

    def q_fwd(q_, pos_, if_, g_):
        cos, sin = _rope_tables(pos_, if_)
        return (_headnorm_rope_fwd(q_, g_, cos, sin, QH),)

    (q,) = rowwise("q_rope", q_fwd, [q_pre, pos], [inv_freq, W["q_norm"]], [(D, f32)])
    att = attn_fwd(q, k_rot, v_val, W["sinks"])
    x3 = matmul("o_proj", att, W["w_o"], "nn", bias=W["b_o"], residual=x2)

    x4, ffn1 = _ffn_fwd("f1", x3, W["f_norm1"], W["f_gate1"], W["f_val1"], W["f_cw1"], W["f_cb1"], W["f_down1"])

    def loss_fn(y_, t_):
        diff = y_ - t_
        rows = jnp.sum(diff * diff, axis=1, keepdims=True) * (0.5 / D)
        return diff * (1.0 / D), jnp.sum(rows, axis=0, keepdims=True)

    dx4, loss = rowwise("loss", loss_fn, [x4, tgt], [], [(D, f32)], [(1, 1)])

    dx3, g = _ffn_bwd("f1", ffn1, W["f_norm1"], W["f_gate1"], W["f_val1"], W["f_cw1"], W["f_cb1"], W["f_down1"], dx4)
    G.update({f"f_{k}1": val for k, val in g.items()})

    datt = matmul("o_dproj", dx3, W["w_o"], "nt")
    G["w_o"] = matmul("o_wproj", att, dx3, "tn")
    dq, dkp, dkc, dvp, dvc, G["sinks"] = attn_bwd(q, k_rot, v_val, W["sinks"], datt)

    def q_bwd(q_, pos_, dq_, dx_, if_, g_):
        cos, sin = _rope_tables(pos_, if_)
        dqp, dg = _headnorm_rope_bwd(q_, g_, cos, sin, dq_, QH)
        return dqp, dg, jnp.sum(dqp, axis=0, keepdims=True), jnp.sum(dx_, axis=0, keepdims=True)

    dq_pre, G["q_norm"], G["b_q"], G["b_o"] = rowwise("q_drope", q_bwd, [q_pre, pos, dq, dx3], [inv_freq, W["q_norm"]],
                                                      [(D, f32)], [(1, AH), (1, D), (1, D)])
    dh2 = matmul("q_dproj", dq_pre, W["w_q"], "nt")
    G["w_q"] = matmul("q_wproj", h2, dq_pre, "tn")
    dkv, G["k_norm"], G["b_kv"] = kv_bwd(kv, pos, inv_freq, W["k_norm"], dkp, dkc, dvp, dvc)
    dkvn = matmul("kv_dproj", dkv, W["w_kv"], "nt")
    G["w_kv"] = matmul("kv_wproj", kvn, dkv, "tn")

    def x2_bwd(x_, dh2_, dkvn_, dx_, gb_, gk_):
        d1, dgb = _rms_bwd(x_, gb_, dh2_)
        d2, dgk = _rms_bwd(x_, gk_, dkvn_)
        return dx_ + d1 + d2, dgb, dgk

    dx2, G["b_norm"], G["kv_norm"] = rowwise("x2_dnorm", x2_bwd, [x2, dh2, dkvn, dx3], [W["b_norm"], W["kv_norm"]],
                                             [(D, f32)], [(1, D), (1, D)])

    dx1, g = _ffn_bwd("f0", ffn0, W["f_norm0"], W["f_gate0"], W["f_val0"], W["f_cw0"], W["f_cb0"], W["f_down0"], dx2)
    G.update({f"f_{k}0": val for k, val in g.items()})

    dgn = matmul("a_dout", dx1, W["a_out"], "nt")
    G["a_out"] = matmul("a_wout", gn, dx1, "tn")

    def gate_norm_bwd(y_, z_, dgn_, g_):
        w = INNER // GROUPS
        sz = _silu(z_)
        yg = y_ * sz
        parts, dgs = [], []
        for i in range(GROUPS):
            dseg, dg = _rms_bwd(yg[:, w * i:w * (i + 1)], g_[:, w * i:w * (i + 1)], dgn_[:, w * i:w * (i + 1)])
            parts.append(dseg)
            dgs.append(dg)
        dyg = jnp.concatenate(parts, axis=1)
        return dyg * sz, dyg * y_ * _dsilu(z_), jnp.concatenate(dgs, axis=1)

    dy, dz, G["a_gnorm"] = rowwise("a_dgnorm", gate_norm_bwd, [y, z, dgn], [W["a_gnorm"]], [(INNER, f32), (INNER, f32)], [(1, INNER)], tile=128)
    dconv = {}
    dconv["xs"], dconv["B"], dconv["C"], ddt_pre, G["dt_bias"], G["A_log"], G["D"] = ssd_bwd(
        conv["xs"], conv["B"], conv["C"], dt_pre, W["dt_bias"], W["A_log"], W["D"], states, dy)

    def conv_bwd(p_, do_, w_, b_):
        c = _conv(p_, w_, b_)
        return _conv_bwd(p_, w_, do_ * _dsilu(c))

    dh0 = matmul("a_din_z", dz, W["in_z"], "nt")
    G["in_z"] = matmul("a_win_z", h0, dz, "tn")
    for k in ("xs", "B", "C"):
        dpre, G[f"cw_{k}"], G[f"cb_{k}"] = colwise(f"a_dconv_{k}", conv_bwd, [pre[k], dconv[k]], [W[f"cw_{k}"], W[f"cb_{k}"]], [f32], [CONV_A, 1])
        dh0 = matmul(f"a_din_{k}", dpre, W[f"in_{k}"], "nt", residual=dh0)
        G[f"in_{k}"] = matmul(f"a_win_{k}", h0, dpre, "tn")
    dh0 = matmul("a_din_dt", ddt_pre, W["in_dt"], "nt", residual=dh0)
    G["in_dt"] = matmul("a_win_dt", h0, ddt_pre, "tn")

    def x0_bwd(x_, dh_, do_, g_):
        dx, dg = _rms_bwd(x_, g_, dh_)
        return do_ + dx, dg

    dx, G["a_norm"] = rowwise("a_dnorm", x0_bwd, [x, dh0, dx1], [W["a_norm"]], [(D, f32)], [(1, D)])
    return loss, dx, G


BIG = ("a_in_proj", "a_out_proj", "w_kv", "w_q", "w_o", "f_w_in", "f_w_down")
SMALL_SHARDED = ("a_norm", "a_conv_w", "a_conv_b", "a_gnorm", "f_conv_w")
REPLICATED = ("a_dt_bias", "a_A_log", "a_D", "kv_norm", "b_kv", "k_norm", "b_norm", "b_q", "q_norm", "sinks", "b_o",
              "f_norm", "f_conv_b")
ORDER = ("a_norm", "a_in_proj", "a_conv_w", "a_conv_b", "a_dt_bias", "a_A_log", "a_D", "a_gnorm", "a_out_proj", "kv_norm",
         "w_kv", "b_kv", "k_norm", "b_norm", "w_q", "b_q", "q_norm", "sinks", "w_o", "b_o", "f_norm", "f_w_in",
         "f_conv_w", "f_conv_b", "f_w_down")


def _gathered_to_whole(name, g):
    if name == "a_conv_w":
        return jnp.moveaxis(g[:, 0], 0, 1).reshape(g.shape[2], -1)
    if name in ("a_norm", "a_conv_b", "a_gnorm"):
        return g[:, 0].reshape(1, -1)
    if name == "f_conv_w":
        return jnp.moveaxis(g, 0, 2).reshape(g.shape[1], g.shape[2], -1)
    raise ValueError(name)


def _whole_to_shards(name, w):
    if name == "a_conv_w":
        return jnp.moveaxis(w.reshape(w.shape[0], N_DEV, -1), 1, 0)[:, None]
    if name in ("a_norm", "a_conv_b", "a_gnorm"):
        return w.reshape(N_DEV, 1, -1)
    if name == "f_conv_w":
        return jnp.moveaxis(w.reshape(w.shape[0], w.shape[1], N_DEV, -1), 2, 0)
    raise ValueError(name)


def _small_weights(whole):
    W = {}
    cw, cb = whole["a_conv_w"], whole["a_conv_b"]
    o = 0
    for k, n in (("xs", INNER), ("B", GROUPS * NSTATE), ("C", GROUPS * NSTATE)):
        W[f"cw_{k}"], W[f"cb_{k}"] = cw[:, o:o + n], cb[:, o:o + n]
        o += n
    W["a_norm"], W["a_gnorm"] = whole["a_norm"], whole["a_gnorm"]
    W["dt_bias"], W["A_log"], W["D"] = (_pad_lanes(whole[k]) for k in ("a_dt_bias", "a_A_log", "a_D"))
    W["kv_norm"], W["b_kv"], W["k_norm"] = whole["kv_norm"].reshape(1, -1), whole["b_kv"].reshape(1, -1), whole["k_norm"].reshape(1, -1)
    for k in ("b_norm", "b_q", "q_norm", "sinks", "b_o"):
        W[k] = whole[k]
    for i in range(2):
        W[f"f_norm{i}"] = whole["f_norm"][i:i + 1]
        W[f"f_cw{i}"], W[f"f_cb{i}"] = whole["f_conv_w"][i], whole["f_conv_b"][i:i + 1]
    return W


def _small_grads(G, shapes):
    nh = HEADS
    out = {
        "a_conv_w": jnp.concatenate([G["cw_xs"], G["cw_B"], G["cw_C"]], axis=1),
        "a_conv_b": jnp.concatenate([G["cb_xs"], G["cb_B"], G["cb_C"]], axis=1),
        "a_norm": G["a_norm"], "a_gnorm": G["a_gnorm"],
        "a_dt_bias": G["dt_bias"][:, :nh], "a_A_log": G["A_log"][:, :nh], "a_D": G["D"][:, :nh],
        "kv_norm": G["kv_norm"], "b_kv": G["b_kv"], "k_norm": G["k_norm"], "b_norm": G["b_norm"],
        "b_q": G["b_q"], "q_norm": G["q_norm"], "sinks": G["sinks"], "b_o": G["b_o"],
        "f_norm": jnp.concatenate([G["f_norm0"], G["f_norm1"]], axis=0),
        "f_conv_w": jnp.stack([G["f_conv_w0"], G["f_conv_w1"]]),
        "f_conv_b": jnp.concatenate([G["f_conv_b0"], G["f_conv_b1"]], axis=0),
    }
    return {k: val.reshape(shapes[k]) if k in shapes else val for k, val in out.items()}


def kernel(x, positions, a_norm, a_in_proj, a_conv_w, a_conv_b, a_dt_bias, a_A_log, a_D, a_gnorm, a_out_proj, kv_norm, w_kv, b_kv, k_norm, b_norm, w_q, b_q, q_norm, sinks, w_o, b_o, f_norm, f_w_in, f_conv_w, f_conv_b, f_w_down, loss_target, m_a_norm, m_a_in_proj, m_a_conv_w, m_a_conv_b, m_a_dt_bias, m_a_A_log, m_a_D, m_a_gnorm, m_a_out_proj, m_kv_norm, m_w_kv, m_b_kv, m_k_norm, m_b_norm, m_w_q, m_b_q, m_q_norm, m_sinks, m_w_o, m_b_o, m_f_norm, m_f_w_in, m_f_conv_w, m_f_conv_b, m_f_w_down, v_a_norm, v_a_in_proj, v_a_conv_w, v_a_conv_b, v_a_dt_bias, v_a_A_log, v_a_D, v_a_gnorm, v_a_out_proj, v_kv_norm, v_w_kv, v_b_kv, v_k_norm, v_b_norm, v_w_q, v_b_q, v_q_norm, v_sinks, v_w_o, v_b_o, v_f_norm, v_f_w_in, v_f_conv_w, v_f_conv_b, v_f_w_down):
    given = dict(locals())
    w_in = {n: given[n] for n in ORDER}
    m_in = {n: given["m_" + n] for n in ORDER}
    v_in = {n: given["v_" + n] for n in ORDER}
    dev = 4 * lax.axis_index("x") + 2 * lax.axis_index("y") + lax.axis_index("c")

    small_pack = _pack([w_in[n] for n in SMALL_SHARDED], 8, f32)
    gathered = all_gather("ag_weights", [w_in[n].astype(MXU) for n in BIG] + [small_pack])
    big = dict(zip(BIG, gathered))
    whole = {n: w_in[n] for n in REPLICATED}
    for n, g in zip(SMALL_SHARDED, _unpack(gathered[-1], [w_in[n].shape for n in SMALL_SHARDED], lead=(N_DEV,))):
        whole[n] = _gathered_to_whole(n, g)
    W = _small_weights(whole)
    for (k, _), a in zip(IN_SPLIT, assemble_in_proj(big["a_in_proj"])):
        W[f"in_{k}"] = a
    gate, val = assemble_ffn_in(big["f_w_in"])
    down = jnp.moveaxis(big["f_w_down"], 0, 1).reshape(2, FFN, D)
    for i in range(2):
        W[f"f_gate{i}"], W[f"f_val{i}"], W[f"f_down{i}"] = gate[i], val[i], down[i]
    W["a_out"] = big["a_out_proj"].reshape(INNER, D)
    W["w_q"], W["w_o"], W["w_kv"] = big["w_q"].reshape(D, D), big["w_o"].reshape(D, D), big["w_kv"].reshape(D, 2 * KVH * AH)

    loss, dx, G = _local_step(x[0], positions.reshape(S, 1).astype(f32), loss_target[0], W)
    grads = _small_grads(G, {n: whole[n].shape for n in REPLICATED})

    rows_1024 = [G["a_out"], G["w_q"], G["w_o"], G["f_down0"], G["f_down1"]]
    send = [shards_in_proj([G[f"in_{k}"] for k, _ in IN_SPLIT]),
            shards_ffn_in(G["f_gate0"], G["f_val0"], G["f_gate1"], G["f_val1"]).reshape(N_DEV, 2 * D, 2 * FFN // N_DEV),
            jnp.concatenate([a.reshape(N_DEV, -1, D) for a in rows_1024], axis=1).astype(MXU),
            G["w_kv"].reshape(N_DEV, -1, 2 * KVH * AH).astype(MXU)]
    s_in, s_ffn, s_rows, s_kv = reduce_scatter(send)
    g_out = {"a_in_proj": s_in.reshape(w_in["a_in_proj"].shape), "f_w_in": s_ffn.reshape(w_in["f_w_in"].shape), "w_kv": s_kv}
    o = 0
    for n in ("a_out_proj", "w_q", "w_o", "f_w_down"):
        r = math.prod(w_in[n].shape[:-1])
        g_out[n] = s_rows[o:o + r].reshape(w_in[n].shape)
        o += r

    small_names = SMALL_SHARDED + REPLICATED
    small_part = _pack([grads[n] for n in small_names], 8, f32)
    (small_parts,) = all_gather("ag_small_grads", [small_part])
    small_sum = sum_devices(small_parts)
    for n, g in zip(small_names, _unpack(small_sum, [grads[n].shape for n in small_names])):
        if n in SMALL_SHARDED:
            g_out[n] = lax.dynamic_index_in_dim(_whole_to_shards(n, g), dev, axis=0, keepdims=False)
        else:
            g_out[n] = g.reshape(w_in[n].shape)

    delta, new_m, new_v = {}, {}, {}
    for n in BIG:
        shp = w_in[n].shape
        as2d = lambda a: a.reshape(-1, shp[-1])
        d_, m_, v_ = adamw(f"adamw_{n}", as2d(w_in[n]), as2d(g_out[n]), as2d(m_in[n]), as2d(v_in[n]))
        delta[n], new_m[n], new_v[n] = d_.reshape(shp), m_.reshape(shp), v_.reshape(shp)
    packs = [_pack([src[n] for n in small_names], 8, f32) for src in (w_in, g_out, m_in, v_in)]
    outs = adamw("adamw_small", *packs)
    for dst, buf in zip((delta, new_m, new_v), outs):
        for n, a in zip(small_names, _unpack(buf, [w_in[n].shape for n in small_names])):
            dst[n] = a

    loss_all = lax.psum(loss[0, 0], AXES)
    return (loss_all, dx[None], *[g_out[n] for n in ORDER], *[delta[n] for n in ORDER],
            *[new_m[n] for n in ORDER], *[new_v[n] for n in ORDER])
```

```python
import functools
import math

import jax
import jax.numpy as jnp
from jax import lax
from jax.experimental import pallas as pl
from jax.experimental.pallas import tpu as pltpu

f32 = jnp.float32
bf16 = jnp.bfloat16
MXU = bf16

N_DEV = 8
S = 2048
D = 1024
EPS = 1e-5
INNER = 2048
HEADS = 32
HP = 64
GROUPS = 8
HPG = HEADS // GROUPS
NSTATE = 128
CONV_A = 4
CHUNK = 256
NCHUNK = S // CHUNK
AH = 64
QH = 16
KVH = 4
QPK = QH // KVH
WIN = 128
NBLK = S // WIN
ROPE_THETA = 10000.0
FFN = 2816
CONV_F = 3
LANES = 128
PACK_W = 1024
BIG_TILE = 496
VMEM_LIMIT = 56 * 1024 * 1024

ADAM_LR, ADAM_B1, ADAM_B2, ADAM_EPS, ADAM_WD, ADAM_STEP = 0.001, 0.9, 0.999, 1e-08, 0.01, 10

MESH = pl.DeviceIdType.MESH
AXES = ("x", "y", "c")


def _cparams(sem=None):
    return pltpu.CompilerParams(dimension_semantics=sem, vmem_limit_bytes=VMEM_LIMIT)


def _pick(n, cands):
    for c in cands:
        if n % c == 0:
            return c
    return n


def matmul(name, a, b, mode, out_dtype=f32, bias=None, residual=None):
    if mode == "nn":
        (M, K), (K2, N) = a.shape, b.shape
    elif mode == "nt":
        (M, K), (N, K2) = a.shape, b.shape
    else:
        (K, M), (K2, N) = a.shape, b.shape
    assert K == K2, (name, a.shape, b.shape)
    if mode == "tn":
        tm, tn = M, _pick(N, (512, 256, 128) if M <= 1024 else (256, 128))
        a_spec = pl.BlockSpec((K, M), lambda j: (0, 0))
        b_spec = pl.BlockSpec((K, tn), lambda j: (0, j))
        dims = (((0,), (0,)), ((), ()))
        grid, o_map, row_map = (N // tn,), (lambda j: (0, j)), (lambda j: (0, j))
    else:
        tm, tn = (256 if N >= 2048 else 512), N
        a_spec = pl.BlockSpec((tm, K), lambda i: (i, 0))
        b_spec = pl.BlockSpec(b.shape, lambda i: (0, 0))
        dims = (((1,), (0,)), ((), ())) if mode == "nn" else (((1,), (1,)), ((), ()))
        grid, o_map, row_map = (M // tm,), (lambda i: (i, 0)), (lambda i: (0, 0))
    ins, in_specs = [a, b], [a_spec, b_spec]
    if bias is not None:
        ins.append(bias)
        in_specs.append(pl.BlockSpec((1, tn), row_map))
    if residual is not None:
        ins.append(residual)
        in_specs.append(pl.BlockSpec((tm, tn), o_map))
    has_bias, has_res = bias is not None, residual is not None

    def body(a_ref, b_ref, *rest):
        rest = list(rest)
        bias_ref = rest.pop(0) if has_bias else None
        res_ref = rest.pop(0) if has_res else None
        (o_ref,) = rest
        r = lax.dot_general(a_ref[...].astype(MXU), b_ref[...].astype(MXU), dims, preferred_element_type=f32)
        if has_bias:
            r = r + bias_ref[...]
        if has_res:
            r = r + res_ref[...]
        o_ref[...] = r.astype(out_dtype)

    return pl.pallas_call(
        body, name=name, grid=grid, in_specs=in_specs,
        out_specs=pl.BlockSpec((tm, tn), o_map),
        out_shape=jax.ShapeDtypeStruct((M, N), out_dtype),
        compiler_params=_cparams(("parallel",)),
    )(*ins)


def rowwise(name, fn, rows, pars, outs, accs=(), tile=256):
    n_in, n_out = len(rows) + len(pars), len(outs)
    in_specs = [pl.BlockSpec((tile, r.shape[1]), lambda i: (i, 0)) for r in rows]
    in_specs += [pl.BlockSpec(p.shape, lambda i: (0, 0)) for p in pars]
    out_specs = [pl.BlockSpec((tile, c), lambda i: (i, 0)) for c, _ in outs]
    out_specs += [pl.BlockSpec(shp, lambda i: (0, 0)) for shp in accs]
    out_shape = [jax.ShapeDtypeStruct((S, c), dt) for c, dt in outs]
    out_shape += [jax.ShapeDtypeStruct(shp, f32) for shp in accs]

    def body(*refs):
        res = fn(*[r[...] for r in refs[:n_in]])
        o_refs = refs[n_in:n_in + n_out]
        a_refs = refs[n_in + n_out:]
        for ref, val in zip(o_refs, res[:n_out]):
            ref[...] = val.astype(ref.dtype)
        if a_refs:
            @pl.when(pl.program_id(0) == 0)
            def _():
                for ref in a_refs:
                    ref[...] = jnp.zeros_like(ref)
            for ref, val in zip(a_refs, res[n_out:]):
                ref[...] += val

    res = pl.pallas_call(
        body, name=name, grid=(S // tile,), in_specs=in_specs, out_specs=out_specs, out_shape=out_shape,
        compiler_params=_cparams(("arbitrary",) if accs else ("parallel",)),
    )(*rows, *pars)
    return res


def colwise(name, fn, cols, pars, outs, pouts=(), ct=128):
    C = cols[0].shape[1]
    n_in, n_out = len(cols) + len(pars), len(outs)
    in_specs = [pl.BlockSpec((S, ct), lambda j: (0, j)) for _ in cols]
    in_specs += [pl.BlockSpec((p.shape[0], ct), lambda j: (0, j)) for p in pars]
    out_specs = [pl.BlockSpec((S, ct), lambda j: (0, j)) for _ in outs]
    out_specs += [pl.BlockSpec((r, ct), lambda j: (0, j)) for r in pouts]
    out_shape = [jax.ShapeDtypeStruct((S, C), dt) for dt in outs]
    out_shape += [jax.ShapeDtypeStruct((r, C), f32) for r in pouts]

    def body(*refs):
        res = fn(*[r[...] for r in refs[:n_in]])
        for ref, val in zip(refs[n_in:], res):
            ref[...] = val.astype(ref.dtype)

    return pl.pallas_call(
        body, name=name, grid=(C // ct,), in_specs=in_specs, out_specs=out_specs, out_shape=out_shape,
        compiler_params=_cparams(("parallel",)),
    )(*cols, *pars)


def _sigmoid(x):
    return 1.0 / (1.0 + jnp.exp(-x))


def _silu(x):
    return x * _sigmoid(x)


def _dsilu(x):
    sg = _sigmoid(x)
    return sg * (1.0 + x * (1.0 - sg))


def _softplus(x):
    return jnp.maximum(x, 0.0) + jnp.log(1.0 + jnp.exp(-jnp.abs(x)))


def _rms_fwd(x, g):
    r = lax.rsqrt(jnp.mean(x * x, axis=-1, keepdims=True) + EPS)
    return x * r * g


def _rms_bwd(x, g, dh):
    r = lax.rsqrt(jnp.mean(x * x, axis=-1, keepdims=True) + EPS)
    xh = x * r
    dxh = dh * g
    dx = r * (dxh - xh * jnp.mean(dxh * xh, axis=-1, keepdims=True))
    return dx, jnp.sum(dh * xh, axis=0, keepdims=True)


def _shift_down(x, s):
    if s == 0:
        return x
    row = lax.broadcasted_iota(jnp.int32, x.shape, 0)
    return jnp.where(row >= s, pltpu.roll(x, s, 0), 0.0)


def _shift_up(x, s):
    if s == 0:
        return x
    n = x.shape[0]
    row = lax.broadcasted_iota(jnp.int32, x.shape, 0)
    return jnp.where(row < n - s, pltpu.roll(x, n - s, 0), 0.0)


def _conv(x, w, b):
    width = w.shape[0]
    out = b + w[width - 1:width, :] * x
    for k in range(width - 1):
        out = out + w[k:k + 1, :] * _shift_down(x, width - 1 - k)
    return out


def _conv_bwd(x, w, dc):
    width = w.shape[0]
    dx = w[width - 1:width, :] * dc
    dws = []
    for k in range(width - 1):
        s = width - 1 - k
        dx = dx + w[k:k + 1, :] * _shift_up(dc, s)
        dws.append(jnp.sum(dc * _shift_down(x, s), axis=0, keepdims=True))
    dws.append(jnp.sum(dc * x, axis=0, keepdims=True))
    return dx, jnp.concatenate(dws, axis=0), jnp.sum(dc, axis=0, keepdims=True)


def _rope_tables(pos, inv_freq):
    ang = pos * inv_freq
    return jnp.cos(ang), jnp.sin(ang)


def _headnorm_rope_fwd(x, g, cos, sin, heads):
    half = AH // 2
    outs = []
    for h in range(heads):
        seg = x[:, AH * h:AH * (h + 1)]
        n = _rms_fwd(seg, g)
        n1, n2 = n[:, :half], n[:, half:]
        outs += [n1 * cos - n2 * sin, n2 * cos + n1 * sin]
    return jnp.concatenate(outs, axis=1)


def _headnorm_rope_bwd(x, g, cos, sin, dout, heads):
    half = AH // 2
    dxs, dg = [], jnp.zeros((1, AH), f32)
    for h in range(heads):
        seg = x[:, AH * h:AH * (h + 1)]
        d = dout[:, AH * h:AH * (h + 1)]
        d1, d2 = d[:, :half], d[:, half:]
        dn = jnp.concatenate([d1 * cos + d2 * sin, d2 * cos - d1 * sin], axis=1)
        dx, dgh = _rms_bwd(seg, g, dn)
        dxs.append(dx)
        dg = dg + dgh
    return jnp.concatenate(dxs, axis=1), dg


def _ssd_prep(dt_pre, dt_bias, a_log, dt_s, acum_s, acumT_s):
    dt = _softplus(dt_pre + dt_bias)
    a = dt * (-jnp.exp(a_log))
    row = lax.broadcasted_iota(jnp.int32, (CHUNK, CHUNK), 0)
    col = lax.broadcasted_iota(jnp.int32, (CHUNK, CHUNK), 1)
    dt_s[...] = dt
    acum_s[...] = jnp.dot((col <= row).astype(f32), a, precision=lax.Precision.HIGHEST, preferred_element_type=f32)
    acumT_s[...] = lax.dot_general(a, (row <= col).astype(f32), (((0,), (0,)), ((), ())),
                                   precision=lax.Precision.HIGHEST, preferred_element_type=f32)


def _head_cols(h, dt_s, acum_s, acumT_s):
    lane = lax.broadcasted_iota(jnp.int32, (1, LANES), 1)
    oh_l = (lane == h).astype(f32)
    sub = lax.broadcasted_iota(jnp.int32, (LANES, 1), 0)
    oh_s = (sub == h).astype(f32)
    dt_h = jnp.sum(dt_s[...] * oh_l, axis=1, keepdims=True)
    ac_h = jnp.sum(acum_s[...] * oh_l, axis=1, keepdims=True)
    acr_h = jnp.sum(acumT_s[...] * oh_s, axis=0, keepdims=True)
    return oh_l, dt_h, ac_h, acr_h


def ssd_fwd(xs, Bm, Cm, dt_pre, dt_bias, a_log, d_skip):
    def body(xs_ref, b_ref, c_ref, dtp_ref, bias_ref, alog_ref, d_ref, y_ref, st_ref, state, dt_s, acum_s, acumT_s):
        c, g = pl.program_id(0), pl.program_id(1)

        @pl.when(g == 0)
        def _():
            _ssd_prep(dtp_ref[...], bias_ref[...], alog_ref[...], dt_s, acum_s, acumT_s)

        row = lax.broadcasted_iota(jnp.int32, (CHUNK, CHUNK), 0)
        col = lax.broadcasted_iota(jnp.int32, (CHUNK, CHUNK), 1)
        causal = col <= row
        Bb, Cb = b_ref[...], c_ref[...]
        cb = lax.dot_general(Cb.astype(MXU), Bb.astype(MXU), (((1,), (1,)), ((), ())), preferred_element_type=f32)
        xs_blk = xs_ref[...]
        ys = []
        for j in range(HPG):
            h = g * HPG + j
            oh_l, dt_h, ac_h, acr_h = _head_cols(h, dt_s, acum_s, acumT_s)

            @pl.when(c == 0)
            def _():
                state[h] = jnp.zeros((NSTATE, HP), f32)

            prev = state[h]
            decay = jnp.exp(jnp.where(causal, ac_h - acr_h, -1e30))
            w = (cb * decay).astype(MXU)
            xs_h = xs_blk[:, HP * j:HP * (j + 1)]
            xd = (xs_h * dt_h).astype(MXU)
            y_diag = jnp.dot(w, xd, preferred_element_type=f32)
            y_off = jnp.dot(Cb.astype(MXU), prev.astype(MXU), preferred_element_type=f32) * jnp.exp(ac_h)
            d_h = jnp.sum(d_ref[...] * oh_l, axis=1, keepdims=True)
            ys.append(y_diag + y_off + xs_h * d_h)
            a_end = ac_h[CHUNK - 1:CHUNK, :]
            dte = jnp.exp(a_end - ac_h)
            s_c = lax.dot_general((Bb * dte).astype(MXU), xd, (((0,), (0,)), ((), ())), preferred_element_type=f32)
            st_ref[0, j] = prev
            state[h] = prev * jnp.exp(a_end) + s_c
        y_ref[...] = jnp.concatenate(ys, axis=1)

    par = pl.BlockSpec((1, LANES), lambda c, g: (0, 0))
    return pl.pallas_call(
        body, name="ssd_fwd", grid=(NCHUNK, GROUPS),
        in_specs=[pl.BlockSpec((CHUNK, HPG * HP), lambda c, g: (c, g)),
                  pl.BlockSpec((CHUNK, NSTATE), lambda c, g: (c, g)),
                  pl.BlockSpec((CHUNK, NSTATE), lambda c, g: (c, g)),
                  pl.BlockSpec((CHUNK, LANES), lambda c, g: (c, 0)), par, par, par],
        out_specs=[pl.BlockSpec((CHUNK, HPG * HP), lambda c, g: (c, g)),
                   pl.BlockSpec((1, HPG, NSTATE, HP), lambda c, g: (c, g, 0, 0))],
        out_shape=[jax.ShapeDtypeStruct((S, INNER), f32), jax.ShapeDtypeStruct((NCHUNK, HEADS, NSTATE, HP), f32)],
        scratch_shapes=[pltpu.VMEM((HEADS, NSTATE, HP), f32), pltpu.VMEM((CHUNK, LANES), f32),
                        pltpu.VMEM((CHUNK, LANES), f32), pltpu.VMEM((LANES, CHUNK), f32)],
        compiler_params=_cparams(("arbitrary", "arbitrary")),
    )(xs, Bm, Cm, dt_pre, dt_bias, a_log, d_skip)


def ssd_bwd(xs, Bm, Cm, dt_pre, dt_bias, a_log, d_skip, states, dy):
    rev = lambda c: NCHUNK - 1 - c

    def body(xs_ref, b_ref, c_ref, dtp_ref, bias_ref, alog_ref, d_ref, st_ref, dy_ref,
             dxs_ref, db_ref, dc_ref, ddt_ref, dbias_ref, dalog_ref, dd_ref,
             dstate, dt_s, acum_s, acumT_s, dacum_s, ddt_s, da_s):
        c, g = pl.program_id(0), pl.program_id(1)

        @pl.when(g == 0)
        def _():
            _ssd_prep(dtp_ref[...], bias_ref[...], alog_ref[...], dt_s, acum_s, acumT_s)
            dacum_s[...] = jnp.zeros_like(dacum_s)
            ddt_s[...] = jnp.zeros_like(ddt_s)

        @pl.when((c == 0) & (g == 0))
        def _():
            da_s[...] = jnp.zeros_like(da_s)
            dd_ref[...] = jnp.zeros_like(dd_ref)
            dbias_ref[...] = jnp.zeros_like(dbias_ref)
            dalog_ref[...] = jnp.zeros_like(dalog_ref)

        row = lax.broadcasted_iota(jnp.int32, (CHUNK, CHUNK), 0)
        col = lax.broadcasted_iota(jnp.int32, (CHUNK, CHUNK), 1)
        sub_l = lax.broadcasted_iota(jnp.int32, (CHUNK, 1), 0)
        last = (sub_l == CHUNK - 1).astype(f32)
        nt = (((1,), (1,)), ((), ()))
        tn = (((0,), (0,)), ((), ()))
        Bb, Cb = b_ref[...], c_ref[...]
        Bm_, Cm_ = Bb.astype(MXU), Cb.astype(MXU)
        cb = lax.dot_general(Cm_, Bm_, nt, preferred_element_type=f32)
        bc = lax.dot_general(Bm_, Cm_, nt, preferred_element_type=f32)
        xs_blk, dy_blk = xs_ref[...], dy_ref[...]
        dxs, dB, dC = [], jnp.zeros((CHUNK, NSTATE), f32), jnp.zeros((CHUNK, NSTATE), f32)
        for j in range(HPG):
            h = g * HPG + j
            oh_l, dt_h, ac_h, acr_h = _head_cols(h, dt_s, acum_s, acumT_s)

            @pl.when(c == 0)
            def _():
                dstate[h] = jnp.zeros((NSTATE, HP), f32)

            dnext = dstate[h]
            prev = st_ref[0, j]
            lm = jnp.exp(jnp.where(col <= row, ac_h - acr_h, -1e30))
            lmT = jnp.exp(jnp.where(row <= col, acr_h - ac_h, -1e30))
            xs_h = xs_blk[:, HP * j:HP * (j + 1)]
            dy_h = dy_blk[:, HP * j:HP * (j + 1)]
            xd = xs_h * dt_h
            xdm, dym = xd.astype(MXU), dy_h.astype(MXU)
            ea = jnp.exp(ac_h)
            a_end = ac_h[CHUNK - 1:CHUNK, :]
            e_end = jnp.exp(a_end)
            dte = jnp.exp(a_end - ac_h)
            dnm, pvm = dnext.astype(MXU), prev.astype(MXU)
            bd = jnp.dot(Bm_, dnm, preferred_element_type=f32)
            dxd = jnp.dot((bc * lmT).astype(MXU), dym, preferred_element_type=f32) + dte * bd
            dw = lax.dot_general(dym, xdm, nt, preferred_element_type=f32)
            dwT = lax.dot_general(xdm, dym, nt, preferred_element_type=f32)
            dcb = dw * lm
            dbc = dwT * lmT
            eady = (ea * dy_h).astype(MXU)
            dC = dC + jnp.dot(dcb.astype(MXU), Bm_, preferred_element_type=f32) \
                + lax.dot_general(eady, pvm, nt, preferred_element_type=f32)
            dB = dB + jnp.dot(dbc.astype(MXU), Cm_, preferred_element_type=f32) \
                + dte * lax.dot_general(xdm, dnm, nt, preferred_element_type=f32)
            dstate[h] = lax.dot_general(Cm_, eady, tn, preferred_element_type=f32) + e_end * dnext
            r1 = jnp.sum(dcb * cb, axis=1, keepdims=True)
            r2 = jnp.sum(dbc * bc, axis=1, keepdims=True)
            y_off = jnp.dot(Cm_, pvm, preferred_element_type=f32) * ea
            t3 = jnp.sum(dy_h * y_off, axis=1, keepdims=True)
            t4 = jnp.sum(bd * xd, axis=1, keepdims=True) * dte
            end_extra = jnp.sum(t4, axis=0, keepdims=True) + e_end * jnp.sum(jnp.sum(prev * dnext, axis=1, keepdims=True), axis=0, keepdims=True)
            dacum_h = r1 - r2 + t3 - t4 + last * end_extra
            dacum_s[...] += dacum_h * oh_l
            ddt_s[...] += jnp.sum(dxd * xs_h, axis=1, keepdims=True) * oh_l
            d_h = jnp.sum(d_ref[...] * oh_l, axis=1, keepdims=True)
            dxs.append(dxd * dt_h + dy_h * d_h)
            dd_ref[...] += oh_l * jnp.sum(jnp.sum(dy_h * xs_h, axis=1, keepdims=True), axis=0, keepdims=True)
        dxs_ref[...] = jnp.concatenate(dxs, axis=1)
        db_ref[...] = dB
        dc_ref[...] = dC

        @pl.when(g == GROUPS - 1)
        def _():
            a_row = -jnp.exp(alog_ref[...])
            da = jnp.dot((row <= col).astype(f32), dacum_s[...], precision=lax.Precision.HIGHEST, preferred_element_type=f32)
            da_s[...] += jnp.sum(da * dt_s[...], axis=0, keepdims=True)
            z = dtp_ref[...] + bias_ref[...]
            ddt_pre = (ddt_s[...] + da * a_row) * _sigmoid(z)
            ddt_ref[...] = ddt_pre.astype(ddt_ref.dtype)
            dbias_ref[...] += jnp.sum(ddt_pre, axis=0, keepdims=True)

            @pl.when(c == NCHUNK - 1)
            def _():
                dalog_ref[...] = da_s[...] * a_row

    par = pl.BlockSpec((1, LANES), lambda c, g: (0, 0))
    return pl.pallas_call(
        body, name="ssd_bwd", grid=(NCHUNK, GROUPS),
        in_specs=[pl.BlockSpec((CHUNK, HPG * HP), lambda c, g: (rev(c), g)),
                  pl.BlockSpec((CHUNK, NSTATE), lambda c, g: (rev(c), g)),
                  pl.BlockSpec((CHUNK, NSTATE), lambda c, g: (rev(c), g)),
                  pl.BlockSpec((CHUNK, LANES), lambda c, g: (rev(c), 0)), par, par, par,
                  pl.BlockSpec((1, HPG, NSTATE, HP), lambda c, g: (rev(c), g, 0, 0)),
                  pl.BlockSpec((CHUNK, HPG * HP), lambda c, g: (rev(c), g))],
        out_specs=[pl.BlockSpec((CHUNK, HPG * HP), lambda c, g: (rev(c), g)),
                   pl.BlockSpec((CHUNK, NSTATE), lambda c, g: (rev(c), g)),
                   pl.BlockSpec((CHUNK, NSTATE), lambda c, g: (rev(c), g)),
                   pl.BlockSpec((CHUNK, LANES), lambda c, g: (rev(c), 0)), par, par, par],
        out_shape=[jax.ShapeDtypeStruct((S, INNER), f32), jax.ShapeDtypeStruct((S, GROUPS * NSTATE), f32),
                   jax.ShapeDtypeStruct((S, GROUPS * NSTATE), f32), jax.ShapeDtypeStruct((S, LANES), MXU),
                   jax.ShapeDtypeStruct((1, LANES), f32), jax.ShapeDtypeStruct((1, LANES), f32),
                   jax.ShapeDtypeStruct((1, LANES), f32)],
        scratch_shapes=[pltpu.VMEM((HEADS, NSTATE, HP), f32), pltpu.VMEM((CHUNK, LANES), f32),
                        pltpu.VMEM((CHUNK, LANES), f32), pltpu.VMEM((LANES, CHUNK), f32),
                        pltpu.VMEM((CHUNK, LANES), f32), pltpu.VMEM((CHUNK, LANES), f32), pltpu.VMEM((1, LANES), f32)],
        compiler_params=_cparams(("arbitrary", "arbitrary")),
    )(xs, Bm, Cm, dt_pre, dt_bias, a_log, d_skip, states, dy)


def _attn_block(n, q, kp, kc, vp, vc, sinks, hk):
    rows = QPK * WIN
    qi = lax.broadcasted_iota(jnp.int32, (rows, 2 * WIN), 0) & (WIN - 1)
    ki = lax.broadcasted_iota(jnp.int32, (rows, 2 * WIN), 1)
    rel = qi + WIN - ki
    mask = (rel >= 0) & (rel < WIN) & ((ki >= WIN) | (n > 0))
    sl = slice(AH * hk, AH * (hk + 1))
    kb = jnp.concatenate([kp[:, sl], kc[:, sl]], axis=0).astype(MXU)
    vb = jnp.concatenate([vp[:, sl], vc[:, sl]], axis=0).astype(MXU)
    qg = jnp.concatenate([q[:, AH * (QPK * hk + g):AH * (QPK * hk + g + 1)] for g in range(QPK)], axis=0).astype(MXU)
    s = lax.dot_general(qg, kb, (((1,), (1,)), ((), ())), preferred_element_type=f32) * (AH ** -0.5)
    s = jnp.where(mask, s, -1e30)
    sink = jnp.concatenate([jnp.broadcast_to(sinks[:, QPK * hk + g:QPK * hk + g + 1], (WIN, 1)) for g in range(QPK)], axis=0)
    m = jnp.maximum(jnp.max(s, axis=1, keepdims=True), sink)
    p = jnp.exp(s - m)
    ps = jnp.exp(sink - m)
    den = jnp.sum(p, axis=1, keepdims=True) + ps
    return qg, kb, vb, p / den, ps / den


def _kv_specs():
    prev = lambda n: (jnp.maximum(n - 1, 0), 0)
    cur = lambda n: (n, 0)
    w = KVH * AH
    return [pl.BlockSpec((WIN, w), prev), pl.BlockSpec((WIN, w), cur), pl.BlockSpec((WIN, w), prev), pl.BlockSpec((WIN, w), cur)]


def attn_fwd(q, k, v, sinks):
    def body(q_ref, kp_ref, kc_ref, vp_ref, vc_ref, s_ref, o_ref):
        n = pl.program_id(0)
        q_, kp, kc, vp, vc, sk = q_ref[...], kp_ref[...], kc_ref[...], vp_ref[...], vc_ref[...], s_ref[...]
        outs = []
        for hk in range(KVH):
            _, _, vb, pr, _ = _attn_block(n, q_, kp, kc, vp, vc, sk, hk)
            o = jnp.dot(pr.astype(MXU), vb, preferred_element_type=f32)
            outs += [o[WIN * g:WIN * (g + 1)] for g in range(QPK)]
        o_ref[...] = jnp.concatenate(outs, axis=1)

    return pl.pallas_call(
        body, name="attn_fwd", grid=(NBLK,),
        in_specs=[pl.BlockSpec((WIN, D), lambda n: (n, 0))] + _kv_specs() + [pl.BlockSpec((1, QH), lambda n: (0, 0))],
        out_specs=pl.BlockSpec((WIN, D), lambda n: (n, 0)),
        out_shape=jax.ShapeDtypeStruct((S, D), f32),
        compiler_params=_cparams(("parallel",)),
    )(q, k, k, v, v, sinks)


def attn_bwd(q, k, v, sinks, dout):
    def body(q_ref, kp_ref, kc_ref, vp_ref, vc_ref, s_ref, do_ref, dq_ref, dkp_ref, dkc_ref, dvp_ref, dvc_ref, ds_ref):
        n = pl.program_id(0)

        @pl.when(n == 0)
        def _():
            ds_ref[...] = jnp.zeros_like(ds_ref)

        q_, kp, kc, vp, vc, sk, do = q_ref[...], kp_ref[...], kc_ref[...], vp_ref[...], vc_ref[...], s_ref[...], do_ref[...]
        lane = lax.broadcasted_iota(jnp.int32, (1, QH), 1)
        nt = (((1,), (1,)), ((), ()))
        tn = (((0,), (0,)), ((), ()))
        dqs, dkps, dkcs, dvps, dvcs = [], [], [], [], []
        dsink = jnp.zeros((1, QH), f32)
        for hk in range(KVH):
            qg, kb, vb, pr, prs = _attn_block(n, q_, kp, kc, vp, vc, sk, hk)
            dog = jnp.concatenate([do[:, AH * (QPK * hk + g):AH * (QPK * hk + g + 1)] for g in range(QPK)], axis=0).astype(MXU)
            dp = lax.dot_general(dog, vb, nt, preferred_element_type=f32)
            dvb = lax.dot_general(pr.astype(MXU), dog, tn, preferred_element_type=f32)
            delta = jnp.sum(pr * dp, axis=1, keepdims=True)
            ds = (pr * (dp - delta)).astype(MXU)
            dsk = -prs * delta
            for g in range(QPK):
                dsink = dsink + jnp.sum(dsk[WIN * g:WIN * (g + 1)], axis=0, keepdims=True) * (lane == QPK * hk + g).astype(f32)
            dqg = jnp.dot(ds, kb, preferred_element_type=f32) * (AH ** -0.5)
            dkb = lax.dot_general(ds, qg, tn, preferred_element_type=f32) * (AH ** -0.5)
            dqs += [dqg[WIN * g:WIN * (g + 1)] for g in range(QPK)]
            dkps.append(dkb[:WIN])
            dkcs.append(dkb[WIN:])
            dvps.append(dvb[:WIN])
            dvcs.append(dvb[WIN:])
        dq_ref[...] = jnp.concatenate(dqs, axis=1)
        dkp_ref[...] = jnp.concatenate(dkps, axis=1)
        dkc_ref[...] = jnp.concatenate(dkcs, axis=1)
        dvp_ref[...] = jnp.concatenate(dvps, axis=1)
        dvc_ref[...] = jnp.concatenate(dvcs, axis=1)
        ds_ref[...] += dsink

    w = KVH * AH
    blk = lambda width: pl.BlockSpec((WIN, width), lambda n: (n, 0))
    return pl.pallas_call(
        body, name="attn_bwd", grid=(NBLK,),
        in_specs=[blk(D)] + _kv_specs() + [pl.BlockSpec((1, QH), lambda n: (0, 0)), blk(D)],
        out_specs=[blk(D), blk(w), blk(w), blk(w), blk(w), pl.BlockSpec((1, QH), lambda n: (0, 0))],
        out_shape=[jax.ShapeDtypeStruct((S, D), f32)] + [jax.ShapeDtypeStruct((S, w), f32)] * 4 + [jax.ShapeDtypeStruct((1, QH), f32)],
        compiler_params=_cparams(("arbitrary",)),
    )(q, k, k, v, v, sinks, dout)


def kv_bwd(kv, pos, inv_freq, k_norm, dkp, dkc, dvp, dvc):
    w = KVH * AH

    def body(kv_ref, pos_ref, if_ref, g_ref, dkp_ref, dkc_ref, dvp_ref, dvc_ref, o_ref, dg_ref, db_ref):
        n = pl.program_id(0)

        @pl.when(n == 0)
        def _():
            dg_ref[...] = jnp.zeros_like(dg_ref)
            db_ref[...] = jnp.zeros_like(db_ref)

        inside = (n < NBLK - 1).astype(f32)
        dk = dkc_ref[...] + inside * dkp_ref[...]
        dv = dvc_ref[...] + inside * dvp_ref[...]
        cos, sin = _rope_tables(pos_ref[...], if_ref[...])
        dkpre, dg = _headnorm_rope_bwd(kv_ref[...], g_ref[...], cos, sin, dk, KVH)
        dkv = jnp.concatenate([dkpre, dv], axis=1)
        o_ref[...] = dkv.astype(o_ref.dtype)
        dg_ref[...] += dg
        db_ref[...] += jnp.sum(dkv, axis=0, keepdims=True)

    nxt = lambda n: (jnp.minimum(n + 1, NBLK - 1), 0)
    cur = lambda n: (n, 0)
    const = lambda n: (0, 0)
    return pl.pallas_call(
        body, name="kv_bwd", grid=(NBLK,),
        in_specs=[pl.BlockSpec((WIN, w), cur), pl.BlockSpec((WIN, 1), cur), pl.BlockSpec((1, AH // 2), const),
                  pl.BlockSpec((1, AH), const), pl.BlockSpec((WIN, w), nxt), pl.BlockSpec((WIN, w), cur),
                  pl.BlockSpec((WIN, w), nxt), pl.BlockSpec((WIN, w), cur)],
        out_specs=[pl.BlockSpec((WIN, 2 * w), cur), pl.BlockSpec((1, AH), const), pl.BlockSpec((1, 2 * w), const)],
        out_shape=[jax.ShapeDtypeStruct((S, 2 * w), MXU), jax.ShapeDtypeStruct((1, AH), f32), jax.ShapeDtypeStruct((1, 2 * w), f32)],
        compiler_params=_cparams(("arbitrary",)),
    )(kv, pos, inv_freq, k_norm, dkp, dkc, dvp, dvc)


def adamw(name, w, g, m, v):
    R, C = w.shape
    tr = _pick(R, (256, 128, 64, 32, 16, 8))

    def body(w_ref, g_ref, m_ref, v_ref, d_ref, nm_ref, nv_ref):
        g_ = g_ref[...]
        m_ = ADAM_B1 * m_ref[...] + (1.0 - ADAM_B1) * g_
        v_ = ADAM_B2 * v_ref[...] + (1.0 - ADAM_B2) * (g_ * g_)
        m_hat = m_ / (1.0 - ADAM_B1 ** ADAM_STEP)
        v_hat = v_ / (1.0 - ADAM_B2 ** ADAM_STEP)
        d_ref[...] = -ADAM_LR * (m_hat / (jnp.sqrt(v_hat) + ADAM_EPS) + ADAM_WD * w_ref[...])
        nm_ref[...] = m_
        nv_ref[...] = v_

    spec = pl.BlockSpec((tr, C), lambda i: (i, 0))
    return pl.pallas_call(
        body, name=name, grid=(R // tr,), in_specs=[spec] * 4, out_specs=[spec] * 3,
        out_shape=[jax.ShapeDtypeStruct((R, C), f32)] * 3, compiler_params=_cparams(("parallel",)),
    )(w, g, m, v)


def _me():
    return lax.axis_index("x"), lax.axis_index("y"), lax.axis_index("c")


def all_gather(name, xs):
    n = len(xs)

    def body(*refs):
        x_refs, o_refs = refs[:n], refs[n:2 * n]
        send_sems, recv_sems, local_sems = refs[2 * n:]
        x, y, c = _me()
        me, sibling = (x, y, c), (x, y, 1 - c)
        chips = [(1 - x, y), (x, 1 - y), (1 - x, 1 - y)]

        def copy(a, k, block, to, src=None):
            dst = o_refs[a].at[4 * block[0] + 2 * block[1] + block[2]]
            return pltpu.make_async_remote_copy(
                src_ref=dst if src is None else src, dst_ref=dst,
                send_sem=send_sems.at[7 * a + k], recv_sem=recv_sems.at[7 * a + k], device_id=to, device_id_type=MESH)

        mine = [pltpu.make_async_copy(x_refs[a], o_refs[a].at[4 * x + 2 * y + c], local_sems.at[a]) for a in range(n)]
        for cp in mine:
            cp.start()
        first = []
        for a in range(n):
            first.append(copy(a, 0, me, sibling, src=x_refs[a]))
            first += [copy(a, 1 + j, me, (*chip, c), src=x_refs[a]) for j, chip in enumerate(chips)]
        for cp in first:
            cp.start()
        passed = []
        for j, chip in enumerate(chips):
            for a in range(n):
                copy(a, 1 + j, (*chip, c), me).wait_recv()
                cp = copy(a, 4 + j, (*chip, c), sibling)
                cp.start()
                passed.append(cp)
        for a in range(n):
            copy(a, 0, sibling, me).wait_recv()
            for j, chip in enumerate(chips):
                copy(a, 4 + j, (*chip, 1 - c), me).wait_recv()
        for cp in first + passed:
            cp.wait_send()
        for cp in mine:
            cp.wait()

    any_spec = pl.BlockSpec(memory_space=pl.ANY)
    return pl.pallas_call(
        body, name=name, in_specs=[any_spec] * n, out_specs=[any_spec] * n,
        out_shape=[jax.ShapeDtypeStruct((N_DEV,) + a.shape, a.dtype) for a in xs],
        scratch_shapes=[pltpu.SemaphoreType.DMA((7 * n,)), pltpu.SemaphoreType.DMA((7 * n,)), pltpu.SemaphoreType.DMA((n,))],
    )(*xs)


def sibling_exchange(gs):
    n = len(gs)

    def body(*refs):
        g_refs, o_refs, send_sems, recv_sems = refs[:n], refs[n:2 * n], refs[2 * n], refs[2 * n + 1]
        x, y, c = _me()
        cps = [pltpu.make_async_remote_copy(
            src_ref=g_refs[a].at[:, 1 - c], dst_ref=o_refs[a], send_sem=send_sems.at[a], recv_sem=recv_sems.at[a],
            device_id=(x, y, 1 - c), device_id_type=MESH) for a in range(n)]
        for cp in cps:
            cp.start()
        for cp in cps:
            cp.wait()

    any_spec = pl.BlockSpec(memory_space=pl.ANY)
    return pl.pallas_call(
        body, name="rs_sibling_exchange", in_specs=[any_spec] * n, out_specs=[any_spec] * n,
        out_shape=[jax.ShapeDtypeStruct((4,) + g.shape[2:], g.dtype) for g in gs],
        scratch_shapes=[pltpu.SemaphoreType.DMA((n,)), pltpu.SemaphoreType.DMA((n,))],
    )(*gs)


def chip_exchange(ts):
    n = len(ts)

    def body(*refs):
        t_refs, o_refs, send_sems, recv_sems = refs[:n], refs[n:2 * n], refs[2 * n], refs[2 * n + 1]
        x, y, c = _me()
        chips = [(1 - x, y), (x, 1 - y), (1 - x, 1 - y)]
        cps = [pltpu.make_async_remote_copy(
            src_ref=t_refs[a].at[2 * px + py], dst_ref=o_refs[a].at[j],
            send_sem=send_sems.at[3 * a + j], recv_sem=recv_sems.at[3 * a + j],
            device_id=(px, py, c), device_id_type=MESH) for j, (px, py) in enumerate(chips) for a in range(n)]
        for cp in cps:
            cp.start()
        for cp in cps:
            cp.wait()

    any_spec = pl.BlockSpec(memory_space=pl.ANY)
    return pl.pallas_call(
        body, name="rs_chip_exchange", in_specs=[any_spec] * n, out_specs=[any_spec] * n,
        out_shape=[jax.ShapeDtypeStruct((3,) + t.shape[1:], t.dtype) for t in ts],
        scratch_shapes=[pltpu.SemaphoreType.DMA((3 * n,)), pltpu.SemaphoreType.DMA((3 * n,))],
    )(*ts)


def _row_tile(rows):
    return _pick(rows, (512, 304, 256, 128))


def pair_add(name, g, r):
    _, _, R, C = g.shape
    tr = _row_tile(R)

    def body(c_ref, g_ref, r_ref, o_ref):
        o_ref[0] = (g_ref[0, 0].astype(f32) + r_ref[0].astype(f32)).astype(o_ref.dtype)

    return pl.pallas_call(
        body, name=name,
        grid_spec=pltpu.PrefetchScalarGridSpec(
            num_scalar_prefetch=1, grid=(4, R // tr),
            in_specs=[pl.BlockSpec((1, 1, tr, C), lambda p, i, c: (p, c[0], i, 0)),
                      pl.BlockSpec((1, tr, C), lambda p, i, c: (p, i, 0))],
            out_specs=pl.BlockSpec((1, tr, C), lambda p, i, c: (p, i, 0))),
        out_shape=jax.ShapeDtypeStruct((4, R, C), g.dtype),
        compiler_params=_cparams(("parallel", "parallel")),
    )(lax.axis_index("c").reshape(1).astype(jnp.int32), g, r)


def final_add(name, t, r):
    _, R, C = t.shape
    tr = _row_tile(R)

    def body(p_ref, t_ref, r_ref, o_ref):
        o_ref[...] = ((t_ref[0].astype(f32) + r_ref[0].astype(f32)) + r_ref[1].astype(f32)) + r_ref[2].astype(f32)

    chip = 2 * lax.axis_index("x") + lax.axis_index("y")
    return pl.pallas_call(
        body, name=name,
        grid_spec=pltpu.PrefetchScalarGridSpec(
            num_scalar_prefetch=1, grid=(R // tr,),
            in_specs=[pl.BlockSpec((1, tr, C), lambda i, p: (p[0], i, 0)),
                      pl.BlockSpec((3, tr, C), lambda i, p: (0, i, 0))],
            out_specs=pl.BlockSpec((tr, C), lambda i, p: (i, 0))),
        out_shape=jax.ShapeDtypeStruct((R, C), f32),
        compiler_params=_cparams(("parallel",)),
    )(chip.reshape(1).astype(jnp.int32), t, r)


def reduce_scatter(gs):
    gs = [g.reshape((4, 2) + g.shape[1:]) for g in gs]
    pairs = [pair_add(f"rs_pair_add_{i}", g, r) for i, (g, r) in enumerate(zip(gs, sibling_exchange(gs)))]
    return [final_add(f"rs_final_add_{i}", t, r) for i, (t, r) in enumerate(zip(pairs, chip_exchange(pairs)))]


IN_SPLIT = (("z", INNER), ("xs", INNER), ("B", GROUPS * NSTATE), ("C", GROUPS * NSTATE), ("dt", LANES))
IN_COLS = 2 * INNER + 2 * GROUPS * NSTATE + HEADS


def assemble_in_proj(g):
    tr = 256

    def body(g_ref, *o_refs):
        whole = jnp.concatenate([g_ref[j, 0] for j in range(N_DEV)], axis=1)
        o = 0
        for ref, (_, w) in zip(o_refs[:-1], IN_SPLIT[:-1]):
            ref[...] = whole[:, o:o + w]
            o += w
        o_refs[-1][...] = jnp.concatenate([whole[:, o:], jnp.zeros((tr, LANES - HEADS), whole.dtype)], axis=1)

    return pl.pallas_call(
        body, name="assemble_in_proj", grid=(D // tr,),
        in_specs=[pl.BlockSpec((N_DEV, 1, tr, IN_COLS // N_DEV), lambda i: (0, 0, i, 0))],
        out_specs=[pl.BlockSpec((tr, w), lambda i: (i, 0)) for _, w in IN_SPLIT],
        out_shape=[jax.ShapeDtypeStruct((D, w), g.dtype) for _, w in IN_SPLIT],
        compiler_params=_cparams(("parallel",)),
    )(g)


def shards_in_proj(parts):
    tr = 256
    ws = IN_COLS // N_DEV

    def body(*refs):
        vals = [r[...] for r in refs[:-1]]
        whole = jnp.concatenate(vals[:-1] + [vals[-1][:, :HEADS]], axis=1)
        for j in range(N_DEV):
            refs[-1][j] = whole[:, ws * j:ws * (j + 1)].astype(MXU)

    return pl.pallas_call(
        body, name="shards_in_proj", grid=(D // tr,),
        in_specs=[pl.BlockSpec((tr, w), lambda i: (i, 0)) for _, w in IN_SPLIT],
        out_specs=pl.BlockSpec((N_DEV, tr, ws), lambda i: (0, i, 0)),
        out_shape=jax.ShapeDtypeStruct((N_DEV, D, ws), MXU),
        compiler_params=_cparams(("parallel",)),
    )(*parts)


def assemble_ffn_in(g):
    tr = 256
    half = N_DEV // 2

    def body(g_ref, gate_ref, val_ref):
        gate_ref[0] = jnp.concatenate([g_ref[j, 0] for j in range(half)], axis=1)
        val_ref[0] = jnp.concatenate([g_ref[j, 0] for j in range(half, N_DEV)], axis=1)

    return pl.pallas_call(
        body, name="assemble_ffn_in", grid=(2, D // tr),
        in_specs=[pl.BlockSpec((N_DEV, 1, tr, 2 * FFN // N_DEV), lambda l, i: (0, l, i, 0))],
        out_specs=[pl.BlockSpec((1, tr, FFN), lambda l, i: (l, i, 0))] * 2,
        out_shape=[jax.ShapeDtypeStruct((2, D, FFN), g.dtype)] * 2,
        compiler_params=_cparams(("parallel", "parallel")),
    )(g)


def shards_ffn_in(gate0, val0, gate1, val1):
    tr = 256
    half = N_DEV // 2
    ws = 2 * FFN // N_DEV

    def body(g0, v0, g1, v1, o_ref):
        for l, (g_ref, v_ref) in enumerate(((g0, v0), (g1, v1))):
            gate, val = g_ref[...], v_ref[...]
            for j in range(half):
                o_ref[j, l] = gate[:, ws * j:ws * (j + 1)].astype(MXU)
                o_ref[half + j, l] = val[:, ws * j:ws * (j + 1)].astype(MXU)

    return pl.pallas_call(
        body, name="shards_ffn_in", grid=(D // tr,),
        in_specs=[pl.BlockSpec((tr, FFN), lambda i: (i, 0))] * 4,
        out_specs=pl.BlockSpec((N_DEV, 2, tr, ws), lambda i: (0, 0, i, 0)),
        out_shape=jax.ShapeDtypeStruct((N_DEV, 2, D, ws), MXU),
        compiler_params=_cparams(("parallel",)),
    )(gate0, val0, gate1, val1)


def sum_devices(g):
    def body(g_ref, o_ref):
        acc = g_ref[0]
        for i in range(1, N_DEV):
            acc = acc + g_ref[i]
        o_ref[...] = acc

    return pl.pallas_call(body, name="sum_devices", out_shape=jax.ShapeDtypeStruct(g.shape[1:], f32),
                          compiler_params=_cparams())(g)


def _pack(parts, unit, dtype, lead=()):
    flat = jnp.concatenate([p.reshape(lead + (-1,)).astype(dtype) for p in parts], axis=-1)
    n = flat.shape[-1]
    rows = -(-n // (unit * PACK_W)) * unit
    flat = jnp.pad(flat, [(0, 0)] * len(lead) + [(0, rows * PACK_W - n)])
    return flat.reshape(lead + (rows, PACK_W))


def _unpack(buf, shapes, lead=()):
    flat = buf.reshape(lead + (-1,))
    out, off = [], 0
    for shp in shapes:
        n = math.prod(shp)
        out.append(flat[..., off:off + n].reshape(lead + tuple(shp)))
        off += n
    return out


def _pad_lanes(a):
    return jnp.pad(a, [(0, 0)] * (a.ndim - 1) + [(0, LANES - a.shape[-1])])


def _ffn_fwd(tag, x, norm_g, w_gate, w_val, conv_w, conv_b, w_down):
    (h,) = rowwise(f"{tag}_norm", lambda x_, g_: (_rms_fwd(x_, g_),), [x], [norm_g], [(D, MXU)])
    gate_pre = matmul(f"{tag}_gate", h, w_gate, "nn")
    val = matmul(f"{tag}_val", h, w_val, "nn")
    (act,) = colwise(f"{tag}_mid", lambda gp, v_, w_, b_: (_silu(_conv(gp, w_, b_)) * v_,), [gate_pre, val], [conv_w, conv_b], [MXU])
    out = matmul(f"{tag}_down", act, w_down, "nn", residual=x)
    return out, (x, h, gate_pre, val, act)


def _ffn_bwd(tag, saved, norm_g, w_gate, w_val, conv_w, conv_b, w_down, dout):
    x, h, gate_pre, val, act = saved
    dact = matmul(f"{tag}_ddown", dout, w_down, "nt")
    g_down = matmul(f"{tag}_wdown", act, dout, "tn")

    def mid_bwd(gp, v_, da, w_, b_):
        gate = _conv(gp, w_, b_)
        dgate = da * v_ * _dsilu(gate)
        dgp, dw, db = _conv_bwd(gp, w_, dgate)
        return dgp, da * _silu(gate), dw, db

    dgp, dval, g_cw, g_cb = colwise(f"{tag}_dmid", mid_bwd, [gate_pre, val, dact], [conv_w, conv_b], [MXU, MXU], [CONV_F, 1])
    dh = matmul(f"{tag}_dgate", dgp, w_gate, "nt")
    dh = matmul(f"{tag}_dval", dval, w_val, "nt", residual=dh)
    g_gate = matmul(f"{tag}_wgate", h, dgp, "tn")
    g_val = matmul(f"{tag}_wval", h, dval, "tn")

    def norm_bwd(x_, dh_, do_, g_):
        dx, dg = _rms_bwd(x_, g_, dh_)
        return do_ + dx, dg

    dx, g_norm = rowwise(f"{tag}_dnorm", norm_bwd, [x, dh, dout], [norm_g], [(D, f32)], [(1, D)])
    return dx, dict(norm=g_norm, gate=g_gate, val=g_val, conv_w=g_cw, conv_b=g_cb, down=g_down)


def _local_step(x, pos, tgt, W):
    G = {}
    inv_freq = (ROPE_THETA ** (-jnp.arange(AH // 2, dtype=f32) / (AH // 2))).reshape(1, AH // 2)

    (h0,) = rowwise("a_norm", lambda x_, g_: (_rms_fwd(x_, g_),), [x], [W["a_norm"]], [(D, MXU)])
    z = matmul("a_in_z", h0, W["in_z"], "nn")
    pre = {k: matmul(f"a_in_{k}", h0, W[f"in_{k}"], "nn") for k in ("xs", "B", "C")}
    dt_pre = matmul("a_in_dt", h0, W["in_dt"], "nn")
    conv = {}
    for k in ("xs", "B", "C"):
        (conv[k],) = colwise(f"a_conv_{k}", lambda p_, w_, b_: (_silu(_conv(p_, w_, b_)),), [pre[k]], [W[f"cw_{k}"], W[f"cb_{k}"]], [f32])
    y, states = ssd_fwd(conv["xs"], conv["B"], conv["C"], dt_pre, W["dt_bias"], W["A_log"], W["D"])

    def gate_norm(y_, z_, g_):
        yg = y_ * _silu(z_)
        w = INNER // GROUPS
        return (jnp.concatenate([_rms_fwd(yg[:, w * i:w * (i + 1)], g_[:, w * i:w * (i + 1)]) for i in range(GROUPS)], axis=1),)

    (gn,) = rowwise("a_gnorm", gate_norm, [y, z], [W["a_gnorm"]], [(INNER, MXU)], tile=128)
    x1 = matmul("a_out", gn, W["a_out"], "nn", residual=x)

    x2, ffn0 = _ffn_fwd("f0", x1, W["f_norm0"], W["f_gate0"], W["f_val0"], W["f_cw0"], W["f_cb0"], W["f_down0"])

    (kvn,) = rowwise("kv_norm", lambda x_, g_: (_rms_fwd(x_, g_),), [x2], [W["kv_norm"]], [(D, MXU)])
    kv = matmul("kv_proj", kvn, W["w_kv"], "nn", bias=W["b_kv"])
    kw = KVH * AH

    def k_fwd(kv_, pos_, if_, g_):
        cos, sin = _rope_tables(pos_, if_)
        return _headnorm_rope_fwd(kv_[:, :kw], g_, cos, sin, KVH), kv_[:, kw:]

    k_rot, v_val = rowwise("k_rope", k_fwd, [kv, pos], [inv_freq, W["k_norm"]], [(kw, f32), (kw, f32)])
    (h2,) = rowwise("b_norm", lambda x_, g_: (_rms_fwd(x_, g_),), [x2], [W["b_norm"]], [(D, MXU)])
    q_pre = matmul("q_proj", h2, W["w_q"], "nn", bias=W["b_q"])

    def q_fwd(q_, pos_, if_, g_):
        cos, sin = _rope_tables(pos_, if_)
        return (_headnorm_rope_fwd(q_, g_, cos, sin, QH),)

    (q,) = rowwise("q_rope", q_fwd, [q_pre, pos], [inv_freq, W["q_norm"]], [(D, f32)])
    att = attn_fwd(q, k_rot, v_val, W["sinks"])
    x3 = matmul("o_proj", att, W["w_o"], "nn", bias=W["b_o"], residual=x2)

    x4, ffn1 = _ffn_fwd("f1", x3, W["f_norm1"], W["f_gate1"], W["f_val1"], W["f_cw1"], W["f_cb1"], W["f_down1"])

    def loss_fn(y_, t_):
        diff = y_ - t_
        rows = jnp.sum(diff * diff, axis=1, keepdims=True) * (0.5 / D)
        return diff * (1.0 / D), jnp.sum(rows, axis=0, keepdims=True)

    dx4, loss = rowwise("loss", loss_fn, [x4, tgt], [], [(D, f32)], [(1, 1)])

    dx3, g = _ffn_bwd("f1", ffn1, W["f_norm1"], W["f_gate1"], W["f_val1"], W["f_cw1"], W["f_cb1"], W["f_down1"], dx4)
    G.update({f"f_{k}1": val for k, val in g.items()})

    datt = matmul("o_dproj", dx3, W["w_o"], "nt")
    G["w_o"] = matmul("o_wproj", att, dx3, "tn")
    dq, dkp, dkc, dvp, dvc, G["sinks"] = attn_bwd(q, k_rot, v_val, W["sinks"], datt)

    def q_bwd(q_, pos_, dq_, dx_, if_, g_):
        cos, sin = _rope_tables(pos_, if_)
        dqp, dg = _headnorm_rope_bwd(q_, g_, cos, sin, dq_, QH)
        return dqp, dg, jnp.sum(dqp, axis=0, keepdims=True), jnp.sum(dx_, axis=0, keepdims=True)

    dq_pre, G["q_norm"], G["b_q"], G["b_o"] = rowwise("q_drope", q_bwd, [q_pre, pos, dq, dx3], [inv_freq, W["q_norm"]],
                                                      [(D, MXU)], [(1, AH), (1, D), (1, D)])
    dh2 = matmul("q_dproj", dq_pre, W["w_q"], "nt")
    G["w_q"] = matmul("q_wproj", h2, dq_pre, "tn")
    dkv, G["k_norm"], G["b_kv"] = kv_bwd(kv, pos, inv_freq, W["k_norm"], dkp, dkc, dvp, dvc)
    dkvn = matmul("kv_dproj", dkv, W["w_kv"], "nt")
    G["w_kv"] = matmul("kv_wproj", kvn, dkv, "tn")

    def x2_bwd(x_, dh2_, dkvn_, dx_, gb_, gk_):
        d1, dgb = _rms_bwd(x_, gb_, dh2_)
        d2, dgk = _rms_bwd(x_, gk_, dkvn_)
        return dx_ + d1 + d2, dgb, dgk

    dx2, G["b_norm"], G["kv_norm"] = rowwise("x2_dnorm", x2_bwd, [x2, dh2, dkvn, dx3], [W["b_norm"], W["kv_norm"]],
                                             [(D, f32)], [(1, D), (1, D)])

    dx1, g = _ffn_bwd("f0", ffn0, W["f_norm0"], W["f_gate0"], W["f_val0"], W["f_cw0"], W["f_cb0"], W["f_down0"], dx2)
    G.update({f"f_{k}0": val for k, val in g.items()})

    dgn = matmul("a_dout", dx1, W["a_out"], "nt")
    G["a_out"] = matmul("a_wout", gn, dx1, "tn")

    def gate_norm_bwd(y_, z_, dgn_, g_):
        w = INNER // GROUPS
        sz = _silu(z_)
        yg = y_ * sz
        parts, dgs = [], []
        for i in range(GROUPS):
            dseg, dg = _rms_bwd(yg[:, w * i:w * (i + 1)], g_[:, w * i:w * (i + 1)], dgn_[:, w * i:w * (i + 1)])
            parts.append(dseg)
            dgs.append(dg)
        dyg = jnp.concatenate(parts, axis=1)
        return dyg * sz, dyg * y_ * _dsilu(z_), jnp.concatenate(dgs, axis=1)

    dy, dz, G["a_gnorm"] = rowwise("a_dgnorm", gate_norm_bwd, [y, z, dgn], [W["a_gnorm"]], [(INNER, f32), (INNER, MXU)], [(1, INNER)], tile=128)
    dconv = {}
    dconv["xs"], dconv["B"], dconv["C"], ddt_pre, G["dt_bias"], G["A_log"], G["D"] = ssd_bwd(
        conv["xs"], conv["B"], conv["C"], dt_pre, W["dt_bias"], W["A_log"], W["D"], states, dy)

    def conv_bwd(p_, do_, w_, b_):
        c = _conv(p_, w_, b_)
        return _conv_bwd(p_, w_, do_ * _dsilu(c))

    dh0 = matmul("a_din_z", dz, W["in_z"], "nt")
    G["in_z"] = matmul("a_win_z", h0, dz, "tn")
    for k in ("xs", "B", "C"):
        dpre, G[f"cw_{k}"], G[f"cb_{k}"] = colwise(f"a_dconv_{k}", conv_bwd, [pre[k], dconv[k]], [W[f"cw_{k}"], W[f"cb_{k}"]], [MXU], [CONV_A, 1])
        dh0 = matmul(f"a_din_{k}", dpre, W[f"in_{k}"], "nt", residual=dh0)
        G[f"in_{k}"] = matmul(f"a_win_{k}", h0, dpre, "tn")
    dh0 = matmul("a_din_dt", ddt_pre, W["in_dt"], "nt", residual=dh0)
    G["in_dt"] = matmul("a_win_dt", h0, ddt_pre, "tn")

    def x0_bwd(x_, dh_, do_, g_):
        dx, dg = _rms_bwd(x_, g_, dh_)
        return do_ + dx, dg

    dx, G["a_norm"] = rowwise("a_dnorm", x0_bwd, [x, dh0, dx1], [W["a_norm"]], [(D, f32)], [(1, D)])
    return loss, dx, G


BIG = ("a_in_proj", "a_out_proj", "w_kv", "w_q", "w_o", "f_w_in", "f_w_down")
SMALL_SHARDED = ("a_norm", "a_conv_w", "a_conv_b", "a_gnorm", "f_conv_w")
REPLICATED = ("a_dt_bias", "a_A_log", "a_D", "kv_norm", "b_kv", "k_norm", "b_norm", "b_q", "q_norm", "sinks", "b_o",
              "f_norm", "f_conv_b")
ORDER = ("a_norm", "a_in_proj", "a_conv_w", "a_conv_b", "a_dt_bias", "a_A_log", "a_D", "a_gnorm", "a_out_proj", "kv_norm",
         "w_kv", "b_kv", "k_norm", "b_norm", "w_q", "b_q", "q_norm", "sinks", "w_o", "b_o", "f_norm", "f_w_in",
         "f_conv_w", "f_conv_b", "f_w_down")


def _gathered_to_whole(name, g):
    if name == "a_conv_w":
        return jnp.moveaxis(g[:, 0], 0, 1).reshape(g.shape[2], -1)
    if name in ("a_norm", "a_conv_b", "a_gnorm"):
        return g[:, 0].reshape(1, -1)
    if name == "f_conv_w":
        return jnp.moveaxis(g, 0, 2).reshape(g.shape[1], g.shape[2], -1)
    raise ValueError(name)


def _whole_to_shards(name, w):
    if name == "a_conv_w":
        return jnp.moveaxis(w.reshape(w.shape[0], N_DEV, -1), 1, 0)[:, None]
    if name in ("a_norm", "a_conv_b", "a_gnorm"):
        return w.reshape(N_DEV, 1, -1)
    if name == "f_conv_w":
        return jnp.moveaxis(w.reshape(w.shape[0], w.shape[1], N_DEV, -1), 2, 0)
    raise ValueError(name)


def _small_weights(whole):
    W = {}
    cw, cb = whole["a_conv_w"], whole["a_conv_b"]
    o = 0
    for k, n in (("xs", INNER), ("B", GROUPS * NSTATE), ("C", GROUPS * NSTATE)):
        W[f"cw_{k}"], W[f"cb_{k}"] = cw[:, o:o + n], cb[:, o:o + n]
        o += n
    W["a_norm"], W["a_gnorm"] = whole["a_norm"], whole["a_gnorm"]
    W["dt_bias"], W["A_log"], W["D"] = (_pad_lanes(whole[k]) for k in ("a_dt_bias", "a_A_log", "a_D"))
    W["kv_norm"], W["b_kv"], W["k_norm"] = whole["kv_norm"].reshape(1, -1), whole["b_kv"].reshape(1, -1), whole["k_norm"].reshape(1, -1)
    for k in ("b_norm", "b_q", "q_norm", "sinks", "b_o"):
        W[k] = whole[k]
    for i in range(2):
        W[f"f_norm{i}"] = whole["f_norm"][i:i + 1]
        W[f"f_cw{i}"], W[f"f_cb{i}"] = whole["f_conv_w"][i], whole["f_conv_b"][i:i + 1]
    return W


def _small_grads(G, shapes):
    nh = HEADS
    out = {
        "a_conv_w": jnp.concatenate([G["cw_xs"], G["cw_B"], G["cw_C"]], axis=1),
        "a_conv_b": jnp.concatenate([G["cb_xs"], G["cb_B"], G["cb_C"]], axis=1),
        "a_norm": G["a_norm"], "a_gnorm": G["a_gnorm"],
        "a_dt_bias": G["dt_bias"][:, :nh], "a_A_log": G["A_log"][:, :nh], "a_D": G["D"][:, :nh],
        "kv_norm": G["kv_norm"], "b_kv": G["b_kv"], "k_norm": G["k_norm"], "b_norm": G["b_norm"],
        "b_q": G["b_q"], "q_norm": G["q_norm"], "sinks": G["sinks"], "b_o": G["b_o"],
        "f_norm": jnp.concatenate([G["f_norm0"], G["f_norm1"]], axis=0),
        "f_conv_w": jnp.stack([G["f_conv_w0"], G["f_conv_w1"]]),
        "f_conv_b": jnp.concatenate([G["f_conv_b0"], G["f_conv_b1"]], axis=0),
    }
    return {k: val.reshape(shapes[k]) if k in shapes else val for k, val in out.items()}


def kernel(x, positions, a_norm, a_in_proj, a_conv_w, a_conv_b, a_dt_bias, a_A_log, a_D, a_gnorm, a_out_proj, kv_norm, w_kv, b_kv, k_norm, b_norm, w_q, b_q, q_norm, sinks, w_o, b_o, f_norm, f_w_in, f_conv_w, f_conv_b, f_w_down, loss_target, m_a_norm, m_a_in_proj, m_a_conv_w, m_a_conv_b, m_a_dt_bias, m_a_A_log, m_a_D, m_a_gnorm, m_a_out_proj, m_kv_norm, m_w_kv, m_b_kv, m_k_norm, m_b_norm, m_w_q, m_b_q, m_q_norm, m_sinks, m_w_o, m_b_o, m_f_norm, m_f_w_in, m_f_conv_w, m_f_conv_b, m_f_w_down, v_a_norm, v_a_in_proj, v_a_conv_w, v_a_conv_b, v_a_dt_bias, v_a_A_log, v_a_D, v_a_gnorm, v_a_out_proj, v_kv_norm, v_w_kv, v_b_kv, v_k_norm, v_b_norm, v_w_q, v_b_q, v_q_norm, v_sinks, v_w_o, v_b_o, v_f_norm, v_f_w_in, v_f_conv_w, v_f_conv_b, v_f_w_down):
    given = dict(locals())
    w_in = {n: given[n] for n in ORDER}
    m_in = {n: given["m_" + n] for n in ORDER}
    v_in = {n: given["v_" + n] for n in ORDER}
    dev = 4 * lax.axis_index("x") + 2 * lax.axis_index("y") + lax.axis_index("c")

    small_pack = _pack([w_in[n] for n in SMALL_SHARDED], 8, f32)
    gathered = all_gather("ag_weights", [w_in[n].astype(MXU) for n in BIG] + [small_pack])
    big = dict(zip(BIG, gathered))
    whole = {n: w_in[n] for n in REPLICATED}
    for n, g in zip(SMALL_SHARDED, _unpack(gathered[-1], [w_in[n].shape for n in SMALL_SHARDED], lead=(N_DEV,))):
        whole[n] = _gathered_to_whole(n, g)
    W = _small_weights(whole)
    for (k, _), a in zip(IN_SPLIT, assemble_in_proj(big["a_in_proj"])):
        W[f"in_{k}"] = a
    gate, val = assemble_ffn_in(big["f_w_in"])
    down = jnp.moveaxis(big["f_w_down"], 0, 1).reshape(2, FFN, D)
    for i in range(2):
        W[f"f_gate{i}"], W[f"f_val{i}"], W[f"f_down{i}"] = gate[i], val[i], down[i]
    W["a_out"] = big["a_out_proj"].reshape(INNER, D)
    W["w_q"], W["w_o"], W["w_kv"] = big["w_q"].reshape(D, D), big["w_o"].reshape(D, D), big["w_kv"].reshape(D, 2 * KVH * AH)

    loss, dx, G = _local_step(x[0], positions.reshape(S, 1).astype(f32), loss_target[0], W)
    grads = _small_grads(G, {n: whole[n].shape for n in REPLICATED})

    rows_1024 = [G["a_out"], G["w_q"], G["w_o"], G["f_down0"], G["f_down1"]]
    send = [shards_in_proj([G[f"in_{k}"] for k, _ in IN_SPLIT]),
            shards_ffn_in(G["f_gate0"], G["f_val0"], G["f_gate1"], G["f_val1"]).reshape(N_DEV, 2 * D, 2 * FFN // N_DEV),
            jnp.concatenate([a.reshape(N_DEV, -1, D) for a in rows_1024], axis=1).astype(MXU),
            G["w_kv"].reshape(N_DEV, -1, 2 * KVH * AH).astype(MXU)]
    s_in, s_ffn, s_rows, s_kv = reduce_scatter(send)
    g_out = {"a_in_proj": s_in.reshape(w_in["a_in_proj"].shape), "f_w_in": s_ffn.reshape(w_in["f_w_in"].shape), "w_kv": s_kv}
    o = 0
    for n in ("a_out_proj", "w_q", "w_o", "f_w_down"):
        r = math.prod(w_in[n].shape[:-1])
        g_out[n] = s_rows[o:o + r].reshape(w_in[n].shape)
        o += r

    small_names = SMALL_SHARDED + REPLICATED
    small_part = _pack([grads[n] for n in small_names], 8, f32)
    (small_parts,) = all_gather("ag_small_grads", [small_part])
    small_sum = sum_devices(small_parts)
    for n, g in zip(small_names, _unpack(small_sum, [grads[n].shape for n in small_names])):
        if n in SMALL_SHARDED:
            g_out[n] = lax.dynamic_index_in_dim(_whole_to_shards(n, g), dev, axis=0, keepdims=False)
        else:
            g_out[n] = g.reshape(w_in[n].shape)

    delta, new_m, new_v = {}, {}, {}
    for n in BIG:
        shp = w_in[n].shape
        as2d = lambda a: a.reshape(-1, shp[-1])
        d_, m_, v_ = adamw(f"adamw_{n}", as2d(w_in[n]), as2d(g_out[n]), as2d(m_in[n]), as2d(v_in[n]))
        delta[n], new_m[n], new_v[n] = d_.reshape(shp), m_.reshape(shp), v_.reshape(shp)
    packs = [_pack([src[n] for n in small_names], 8, f32) for src in (w_in, g_out, m_in, v_in)]
    outs = adamw("adamw_small", *packs)
    for dst, buf in zip((delta, new_m, new_v), outs):
        for n, a in zip(small_names, _unpack(buf, [w_in[n].shape for n in small_names])):
            dst[n] = a

    loss_all = lax.psum(loss[0, 0], AXES)
    return (loss_all, dx[None], *[g_out[n] for n in ORDER], *[delta[n] for n in ORDER],
            *[new_m[n] for n in ORDER], *[new_v[n] for n in ORDER])
```

```python
import functools
import math

import jax
import jax.numpy as jnp
from jax import lax
from jax.experimental import pallas as pl
from jax.experimental.pallas import tpu as pltpu

f32 = jnp.float32
bf16 = jnp.bfloat16
MXU = bf16

N_DEV = 8
S = 2048
D = 1024
EPS = 1e-5
INNER = 2048
HEADS = 32
HP = 64
GROUPS = 8
HPG = HEADS // GROUPS
NSTATE = 128
CONV_A = 4
CHUNK = 256
NCHUNK = S // CHUNK
AH = 64
QH = 16
KVH = 4
QPK = QH // KVH
WIN = 128
NBLK = S // WIN
ROPE_THETA = 10000.0
FFN = 2816
CONV_F = 3
LANES = 128
PACK_W = 1024
BIG_TILE = 496
VMEM_LIMIT = 56 * 1024 * 1024

ADAM_LR, ADAM_B1, ADAM_B2, ADAM_EPS, ADAM_WD, ADAM_STEP = 0.001, 0.9, 0.999, 1e-08, 0.01, 10

MESH = pl.DeviceIdType.MESH
AXES = ("x", "y", "c")


def _cparams(sem=None):
    return pltpu.CompilerParams(dimension_semantics=sem, vmem_limit_bytes=VMEM_LIMIT)


def _pick(n, cands):
    for c in cands:
        if n % c == 0:
            return c
    return n


def matmul(name, a, b, mode, out_dtype=f32, bias=None, residual=None, b_rows=None, a_lead=None):
    b_idx, b_shape = (0, b.shape) if b_rows is None else (b_rows[0], (b_rows[1], b.shape[1]))
    a_shape = a.shape if a_lead is None else a.shape[1:]
    if mode == "nn":
        (M, K), (K2, N) = a_shape, b_shape
    elif mode == "nt":
        (M, K), (N, K2) = a_shape, b_shape
    else:
        (K, M), (K2, N) = a_shape, b_shape
    assert K == K2, (name, a_shape, b_shape)
    if mode == "tn":
        tm, tn = M, _pick(N, (512, 256, 128) if M <= 1024 else (256, 128))
        a_spec = pl.BlockSpec((K, M), lambda j: (0, 0))
        b_spec = pl.BlockSpec((K, tn), lambda j: (0, j))
        dims = (((0,), (0,)), ((), ()))
        grid, o_map, row_map = (N // tn,), (lambda j: (0, j)), (lambda j: (0, j))
    else:
        tm, tn = (256 if N >= 2048 else 512), N
        a_spec = pl.BlockSpec((tm, K), lambda i: (i, 0)) if a_lead is None else pl.BlockSpec((None, tm, K), lambda i: (a_lead, i, 0))
        b_spec = pl.BlockSpec(b_shape, lambda i: (b_idx, 0))
        dims = (((1,), (0,)), ((), ())) if mode == "nn" else (((1,), (1,)), ((), ()))
        grid, o_map, row_map = (M // tm,), (lambda i: (i, 0)), (lambda i: (0, 0))
    ins, in_specs = [a, b], [a_spec, b_spec]
    if bias is not None:
        ins.append(bias)
        in_specs.append(pl.BlockSpec((1, tn), row_map))
    if residual is not None:
        ins.append(residual)
        in_specs.append(pl.BlockSpec((tm, tn), o_map))
    has_bias, has_res = bias is not None, residual is not None

    def body(a_ref, b_ref, *rest):
        rest = list(rest)
        bias_ref = rest.pop(0) if has_bias else None
        res_ref = rest.pop(0) if has_res else None
        (o_ref,) = rest
        r = lax.dot_general(a_ref[...].astype(MXU), b_ref[...].astype(MXU), dims, preferred_element_type=f32)
        if has_bias:
            r = r + bias_ref[...]
        if has_res:
            r = r + res_ref[...]
        o_ref[...] = r.astype(out_dtype)

    return pl.pallas_call(
        body, name=name, grid=grid, in_specs=in_specs,
        out_specs=pl.BlockSpec((tm, tn), o_map),
        out_shape=jax.ShapeDtypeStruct((M, N), out_dtype),
        compiler_params=_cparams(("parallel",)),
    )(*ins)


def matmul_tn_stacked(name, a, b, out_dtype):
    R, K, M = a.shape
    N = b.shape[1]
    tn = _pick(N, (256, 128))

    def body(a_ref, b_ref, o_ref):
        o_ref[0] = lax.dot_general(a_ref[0].astype(MXU), b_ref[...].astype(MXU), (((0,), (0,)), ((), ())),
                                   preferred_element_type=f32).astype(out_dtype)

    out = pl.pallas_call(
        body, name=name, grid=(R, N // tn),
        in_specs=[pl.BlockSpec((1, K, M), lambda r, j: (r, 0, 0)), pl.BlockSpec((K, tn), lambda r, j: (0, j))],
        out_specs=pl.BlockSpec((1, M, tn), lambda r, j: (r, 0, j)),
        out_shape=jax.ShapeDtypeStruct((R, M, N), out_dtype),
        compiler_params=_cparams(("parallel", "parallel")),
    )(a, b)
    return out.reshape(R * M, N)


def rowwise(name, fn, rows, pars, outs, accs=(), tile=256):
    n_in, n_out = len(rows) + len(pars), len(outs)
    in_specs = [pl.BlockSpec((tile, r.shape[1]), lambda i: (i, 0)) for r in rows]
    in_specs += [pl.BlockSpec(p.shape, lambda i: (0, 0)) for p in pars]
    out_specs = [pl.BlockSpec((tile, c), lambda i: (i, 0)) for c, _ in outs]
    out_specs += [pl.BlockSpec(shp, lambda i: (0, 0)) for shp in accs]
    out_shape = [jax.ShapeDtypeStruct((S, c), dt) for c, dt in outs]
    out_shape += [jax.ShapeDtypeStruct(shp, f32) for shp in accs]

    def body(*refs):
        res = fn(*[r[...] for r in refs[:n_in]])
        o_refs = refs[n_in:n_in + n_out]
        a_refs = refs[n_in + n_out:]
        for ref, val in zip(o_refs, res[:n_out]):
            ref[...] = val.astype(ref.dtype)
        if a_refs:
            @pl.when(pl.program_id(0) == 0)
            def _():
                for ref in a_refs:
                    ref[...] = jnp.zeros_like(ref)
            for ref, val in zip(a_refs, res[n_out:]):
                ref[...] += val

    res = pl.pallas_call(
        body, name=name, grid=(S // tile,), in_specs=in_specs, out_specs=out_specs, out_shape=out_shape,
        compiler_params=_cparams(("arbitrary",) if accs else ("parallel",)),
    )(*rows, *pars)
    return res


def colwise(name, fn, cols, pars, outs, pouts=(), ct=128):
    C = cols[0].shape[1]
    n_in, n_out = len(cols) + len(pars), len(outs)
    in_specs = [pl.BlockSpec((S, ct), lambda j: (0, j)) for _ in cols]
    in_specs += [pl.BlockSpec((p.shape[0], ct), lambda j: (0, j)) for p in pars]
    out_specs = [pl.BlockSpec((S, ct), lambda j: (0, j)) for _ in outs]
    out_specs += [pl.BlockSpec((r, ct), lambda j: (0, j)) for r in pouts]
    out_shape = [jax.ShapeDtypeStruct((S, C), dt) for dt in outs]
    out_shape += [jax.ShapeDtypeStruct((r, C), f32) for r in pouts]

    def body(*refs):
        res = fn(*[r[...] for r in refs[:n_in]])
        for ref, val in zip(refs[n_in:], res):
            ref[...] = val.astype(ref.dtype)

    return pl.pallas_call(
        body, name=name, grid=(C // ct,), in_specs=in_specs, out_specs=out_specs, out_shape=out_shape,
        compiler_params=_cparams(("parallel",)),
    )(*cols, *pars)


def _sigmoid(x):
    return 1.0 / (1.0 + jnp.exp(-x))


def _silu(x):
    return x * _sigmoid(x)


def _dsilu(x):
    sg = _sigmoid(x)
    return sg * (1.0 + x * (1.0 - sg))


def _softplus(x):
    return jnp.maximum(x, 0.0) + jnp.log(1.0 + jnp.exp(-jnp.abs(x)))


def _rms_fwd(x, g):
    r = lax.rsqrt(jnp.mean(x * x, axis=-1, keepdims=True) + EPS)
    return x * r * g


def _rms_bwd(x, g, dh):
    r = lax.rsqrt(jnp.mean(x * x, axis=-1, keepdims=True) + EPS)
    xh = x * r
    dxh = dh * g
    dx = r * (dxh - xh * jnp.mean(dxh * xh, axis=-1, keepdims=True))
    return dx, jnp.sum(dh * xh, axis=0, keepdims=True)


def _shift_down(x, s):
    if s == 0:
        return x
    row = lax.broadcasted_iota(jnp.int32, x.shape, 0)
    return jnp.where(row >= s, pltpu.roll(x, s, 0), 0.0)


def _shift_up(x, s):
    if s == 0:
        return x
    n = x.shape[0]
    row = lax.broadcasted_iota(jnp.int32, x.shape, 0)
    return jnp.where(row < n - s, pltpu.roll(x, n - s, 0), 0.0)


def _conv(x, w, b):
    width = w.shape[0]
    out = b + w[width - 1:width, :] * x
    for k in range(width - 1):
        out = out + w[k:k + 1, :] * _shift_down(x, width - 1 - k)
    return out


def _conv_bwd(x, w, dc):
    width = w.shape[0]
    dx = w[width - 1:width, :] * dc
    dws = []
    for k in range(width - 1):
        s = width - 1 - k
        dx = dx + w[k:k + 1, :] * _shift_up(dc, s)
        dws.append(jnp.sum(dc * _shift_down(x, s), axis=0, keepdims=True))
    dws.append(jnp.sum(dc * x, axis=0, keepdims=True))
    return dx, jnp.concatenate(dws, axis=0), jnp.sum(dc, axis=0, keepdims=True)


def _rope_tables(pos, inv_freq):
    ang = pos * inv_freq
    return jnp.cos(ang), jnp.sin(ang)


def _headnorm_rope_fwd(x, g, cos, sin, heads):
    half = AH // 2
    outs = []
    for h in range(heads):
        seg = x[:, AH * h:AH * (h + 1)]
        n = _rms_fwd(seg, g)
        n1, n2 = n[:, :half], n[:, half:]
        outs += [n1 * cos - n2 * sin, n2 * cos + n1 * sin]
    return jnp.concatenate(outs, axis=1)


def _headnorm_rope_bwd(x, g, cos, sin, dout, heads):
    half = AH // 2
    dxs, dg = [], jnp.zeros((1, AH), f32)
    for h in range(heads):
        seg = x[:, AH * h:AH * (h + 1)]
        d = dout[:, AH * h:AH * (h + 1)]
        d1, d2 = d[:, :half], d[:, half:]
        dn = jnp.concatenate([d1 * cos + d2 * sin, d2 * cos - d1 * sin], axis=1)
        dx, dgh = _rms_bwd(seg, g, dn)
        dxs.append(dx)
        dg = dg + dgh
    return jnp.concatenate(dxs, axis=1), dg


def _ssd_prep(dt_pre, dt_bias, a_log, dt_s, acum_s, acumT_s):
    dt = _softplus(dt_pre + dt_bias)
    a = dt * (-jnp.exp(a_log))
    row = lax.broadcasted_iota(jnp.int32, (CHUNK, CHUNK), 0)
    col = lax.broadcasted_iota(jnp.int32, (CHUNK, CHUNK), 1)
    dt_s[...] = dt
    acum_s[...] = jnp.dot((col <= row).astype(f32), a, precision=lax.Precision.HIGHEST, preferred_element_type=f32)
    acumT_s[...] = lax.dot_general(a, (row <= col).astype(f32), (((0,), (0,)), ((), ())),
                                   precision=lax.Precision.HIGHEST, preferred_element_type=f32)


def _head_cols(h, dt_s, acum_s, acumT_s):
    lane = lax.broadcasted_iota(jnp.int32, (1, LANES), 1)
    oh_l = (lane == h).astype(f32)
    sub = lax.broadcasted_iota(jnp.int32, (LANES, 1), 0)
    oh_s = (sub == h).astype(f32)
    dt_h = jnp.sum(dt_s[...] * oh_l, axis=1, keepdims=True)
    ac_h = jnp.sum(acum_s[...] * oh_l, axis=1, keepdims=True)
    acr_h = jnp.sum(acumT_s[...] * oh_s, axis=0, keepdims=True)
    return oh_l, dt_h, ac_h, acr_h


def ssd_fwd(xs, Bm, Cm, dt_pre, dt_bias, a_log, d_skip):
    def body(xs_ref, b_ref, c_ref, dtp_ref, bias_ref, alog_ref, d_ref, y_ref, st_ref, state, dt_s, acum_s, acumT_s):
        c, g = pl.program_id(0), pl.program_id(1)

        @pl.when(g == 0)
        def _():
            _ssd_prep(dtp_ref[...], bias_ref[...], alog_ref[...], dt_s, acum_s, acumT_s)

        row = lax.broadcasted_iota(jnp.int32, (CHUNK, CHUNK), 0)
        col = lax.broadcasted_iota(jnp.int32, (CHUNK, CHUNK), 1)
        causal = col <= row
        Bb, Cb = b_ref[...], c_ref[...]
        cb = lax.dot_general(Cb.astype(MXU), Bb.astype(MXU), (((1,), (1,)), ((), ())), preferred_element_type=f32)
        xs_blk = xs_ref[...]
        ys = []
        for j in range(HPG):
            h = g * HPG + j
            oh_l, dt_h, ac_h, acr_h = _head_cols(h, dt_s, acum_s, acumT_s)

            @pl.when(c == 0)
            def _():
                state[h] = jnp.zeros((NSTATE, HP), f32)

            prev = state[h]
            decay = jnp.exp(jnp.where(causal, ac_h - acr_h, -1e30))
            w = (cb * decay).astype(MXU)
            xs_h = xs_blk[:, HP * j:HP * (j + 1)]
            xd = (xs_h * dt_h).astype(MXU)
            y_diag = jnp.dot(w, xd, preferred_element_type=f32)
            y_off = jnp.dot(Cb.astype(MXU), prev.astype(MXU), preferred_element_type=f32) * jnp.exp(ac_h)
            d_h = jnp.sum(d_ref[...] * oh_l, axis=1, keepdims=True)
            ys.append(y_diag + y_off + xs_h * d_h)
            a_end = ac_h[CHUNK - 1:CHUNK, :]
            dte = jnp.exp(a_end - ac_h)
            s_c = lax.dot_general((Bb * dte).astype(MXU), xd, (((0,), (0,)), ((), ())), preferred_element_type=f32)
            st_ref[0, j] = prev
            state[h] = prev * jnp.exp(a_end) + s_c
        y_ref[...] = jnp.concatenate(ys, axis=1)

    par = pl.BlockSpec((1, LANES), lambda c, g: (0, 0))
    return pl.pallas_call(
        body, name="ssd_fwd", grid=(NCHUNK, GROUPS),
        in_specs=[pl.BlockSpec((CHUNK, HPG * HP), lambda c, g: (c, g)),
                  pl.BlockSpec((CHUNK, NSTATE), lambda c, g: (c, g)),
                  pl.BlockSpec((CHUNK, NSTATE), lambda c, g: (c, g)),
                  pl.BlockSpec((CHUNK, LANES), lambda c, g: (c, 0)), par, par, par],
        out_specs=[pl.BlockSpec((CHUNK, HPG * HP), lambda c, g: (c, g)),
                   pl.BlockSpec((1, HPG, NSTATE, HP), lambda c, g: (c, g, 0, 0))],
        out_shape=[jax.ShapeDtypeStruct((S, INNER), f32), jax.ShapeDtypeStruct((NCHUNK, HEADS, NSTATE, HP), f32)],
        scratch_shapes=[pltpu.VMEM((HEADS, NSTATE, HP), f32), pltpu.VMEM((CHUNK, LANES), f32),
                        pltpu.VMEM((CHUNK, LANES), f32), pltpu.VMEM((LANES, CHUNK), f32)],
        compiler_params=_cparams(("arbitrary", "arbitrary")),
    )(xs, Bm, Cm, dt_pre, dt_bias, a_log, d_skip)


def ssd_bwd(xs, Bm, Cm, dt_pre, dt_bias, a_log, d_skip, states, dy):
    rev = lambda c: NCHUNK - 1 - c

    def body(xs_ref, b_ref, c_ref, dtp_ref, bias_ref, alog_ref, d_ref, st_ref, dy_ref,
             dxs_ref, db_ref, dc_ref, ddt_ref, dbias_ref, dalog_ref, dd_ref,
             dstate, dt_s, acum_s, acumT_s, dacum_s, ddt_s, da_s):
        c, g = pl.program_id(0), pl.program_id(1)

        @pl.when(g == 0)
        def _():
            _ssd_prep(dtp_ref[...], bias_ref[...], alog_ref[...], dt_s, acum_s, acumT_s)
            dacum_s[...] = jnp.zeros_like(dacum_s)
            ddt_s[...] = jnp.zeros_like(ddt_s)

        @pl.when((c == 0) & (g == 0))
        def _():
            da_s[...] = jnp.zeros_like(da_s)
            dd_ref[...] = jnp.zeros_like(dd_ref)
            dbias_ref[...] = jnp.zeros_like(dbias_ref)
            dalog_ref[...] = jnp.zeros_like(dalog_ref)

        row = lax.broadcasted_iota(jnp.int32, (CHUNK, CHUNK), 0)
        col = lax.broadcasted_iota(jnp.int32, (CHUNK, CHUNK), 1)
        sub_l = lax.broadcasted_iota(jnp.int32, (CHUNK, 1), 0)
        last = (sub_l == CHUNK - 1).astype(f32)
        nt = (((1,), (1,)), ((), ()))
        tn = (((0,), (0,)), ((), ()))
        Bb, Cb = b_ref[...], c_ref[...]
        Bm_, Cm_ = Bb.astype(MXU), Cb.astype(MXU)
        cb = lax.dot_general(Cm_, Bm_, nt, preferred_element_type=f32)
        bc = lax.dot_general(Bm_, Cm_, nt, preferred_element_type=f32)
        xs_blk, dy_blk = xs_ref[...], dy_ref[...]
        dxs, dB, dC = [], jnp.zeros((CHUNK, NSTATE), f32), jnp.zeros((CHUNK, NSTATE), f32)
        for j in range(HPG):
            h = g * HPG + j
            oh_l, dt_h, ac_h, acr_h = _head_cols(h, dt_s, acum_s, acumT_s)

            @pl.when(c == 0)
            def _():
                dstate[h] = jnp.zeros((NSTATE, HP), f32)

            dnext = dstate[h]
            prev = st_ref[0, j]
            lm = jnp.exp(jnp.where(col <= row, ac_h - acr_h, -1e30))
            lmT = jnp.exp(jnp.where(row <= col, acr_h - ac_h, -1e30))
            xs_h = xs_blk[:, HP * j:HP * (j + 1)]
            dy_h = dy_blk[:, HP * j:HP * (j + 1)]
            xd = xs_h * dt_h
            xdm, dym = xd.astype(MXU), dy_h.astype(MXU)
            ea = jnp.exp(ac_h)
            a_end = ac_h[CHUNK - 1:CHUNK, :]
            e_end = jnp.exp(a_end)
            dte = jnp.exp(a_end - ac_h)
            dnm, pvm = dnext.astype(MXU), prev.astype(MXU)
            bd = jnp.dot(Bm_, dnm, preferred_element_type=f32)
            dxd = jnp.dot((bc * lmT).astype(MXU), dym, preferred_element_type=f32) + dte * bd
            dw = lax.dot_general(dym, xdm, nt, preferred_element_type=f32)
            dwT = lax.dot_general(xdm, dym, nt, preferred_element_type=f32)
            dcb = dw * lm
            dbc = dwT * lmT
            eady = (ea * dy_h).astype(MXU)
            dC = dC + jnp.dot(dcb.astype(MXU), Bm_, preferred_element_type=f32) \
                + lax.dot_general(eady, pvm, nt, preferred_element_type=f32)
            dB = dB + jnp.dot(dbc.astype(MXU), Cm_, preferred_element_type=f32) \
                + dte * lax.dot_general(xdm, dnm, nt, preferred_element_type=f32)
            dstate[h] = lax.dot_general(Cm_, eady, tn, preferred_element_type=f32) + e_end * dnext
            r1 = jnp.sum(dcb * cb, axis=1, keepdims=True)
            r2 = jnp.sum(dbc * bc, axis=1, keepdims=True)
            y_off = jnp.dot(Cm_, pvm, preferred_element_type=f32) * ea
            t3 = jnp.sum(dy_h * y_off, axis=1, keepdims=True)
            t4 = jnp.sum(bd * xd, axis=1, keepdims=True) * dte
            end_extra = jnp.sum(t4, axis=0, keepdims=True) + e_end * jnp.sum(jnp.sum(prev * dnext, axis=1, keepdims=True), axis=0, keepdims=True)
            dacum_h = r1 - r2 + t3 - t4 + last * end_extra
            dacum_s[...] += dacum_h * oh_l
            ddt_s[...] += jnp.sum(dxd * xs_h, axis=1, keepdims=True) * oh_l
            d_h = jnp.sum(d_ref[...] * oh_l, axis=1, keepdims=True)
            dxs.append(dxd * dt_h + dy_h * d_h)
            dd_ref[...] += oh_l * jnp.sum(jnp.sum(dy_h * xs_h, axis=1, keepdims=True), axis=0, keepdims=True)
        dxs_ref[...] = jnp.concatenate(dxs, axis=1)
        db_ref[...] = dB
        dc_ref[...] = dC

        @pl.when(g == GROUPS - 1)
        def _():
            a_row = -jnp.exp(alog_ref[...])
            da = jnp.dot((row <= col).astype(f32), dacum_s[...], precision=lax.Precision.HIGHEST, preferred_element_type=f32)
            da_s[...] += jnp.sum(da * dt_s[...], axis=0, keepdims=True)
            z = dtp_ref[...] + bias_ref[...]
            ddt_pre = (ddt_s[...] + da * a_row) * _sigmoid(z)
            ddt_ref[...] = ddt_pre.astype(ddt_ref.dtype)
            dbias_ref[...] += jnp.sum(ddt_pre, axis=0, keepdims=True)

            @pl.when(c == NCHUNK - 1)
            def _():
                dalog_ref[...] = da_s[...] * a_row

    par = pl.BlockSpec((1, LANES), lambda c, g: (0, 0))
    return pl.pallas_call(
        body, name="ssd_bwd", grid=(NCHUNK, GROUPS),
        in_specs=[pl.BlockSpec((CHUNK, HPG * HP), lambda c, g: (rev(c), g)),
                  pl.BlockSpec((CHUNK, NSTATE), lambda c, g: (rev(c), g)),
                  pl.BlockSpec((CHUNK, NSTATE), lambda c, g: (rev(c), g)),
                  pl.BlockSpec((CHUNK, LANES), lambda c, g: (rev(c), 0)), par, par, par,
                  pl.BlockSpec((1, HPG, NSTATE, HP), lambda c, g: (rev(c), g, 0, 0)),
                  pl.BlockSpec((CHUNK, HPG * HP), lambda c, g: (rev(c), g))],
        out_specs=[pl.BlockSpec((CHUNK, HPG * HP), lambda c, g: (rev(c), g)),
                   pl.BlockSpec((CHUNK, NSTATE), lambda c, g: (rev(c), g)),
                   pl.BlockSpec((CHUNK, NSTATE), lambda c, g: (rev(c), g)),
                   pl.BlockSpec((CHUNK, LANES), lambda c, g: (rev(c), 0)), par, par, par],
        out_shape=[jax.ShapeDtypeStruct((S, INNER), f32), jax.ShapeDtypeStruct((S, GROUPS * NSTATE), f32),
                   jax.ShapeDtypeStruct((S, GROUPS * NSTATE), f32), jax.ShapeDtypeStruct((S, LANES), MXU),
                   jax.ShapeDtypeStruct((1, LANES), f32), jax.ShapeDtypeStruct((1, LANES), f32),
                   jax.ShapeDtypeStruct((1, LANES), f32)],
        scratch_shapes=[pltpu.VMEM((HEADS, NSTATE, HP), f32), pltpu.VMEM((CHUNK, LANES), f32),
                        pltpu.VMEM((CHUNK, LANES), f32), pltpu.VMEM((LANES, CHUNK), f32),
                        pltpu.VMEM((CHUNK, LANES), f32), pltpu.VMEM((CHUNK, LANES), f32), pltpu.VMEM((1, LANES), f32)],
        compiler_params=_cparams(("arbitrary", "arbitrary")),
    )(xs, Bm, Cm, dt_pre, dt_bias, a_log, d_skip, states, dy)


def _attn_block(n, q, kp, kc, vp, vc, sinks, hk):
    rows = QPK * WIN
    qi = lax.broadcasted_iota(jnp.int32, (rows, 2 * WIN), 0) & (WIN - 1)
    ki = lax.broadcasted_iota(jnp.int32, (rows, 2 * WIN), 1)
    rel = qi + WIN - ki
    mask = (rel >= 0) & (rel < WIN) & ((ki >= WIN) | (n > 0))
    sl = slice(AH * hk, AH * (hk + 1))
    kb = jnp.concatenate([kp[:, sl], kc[:, sl]], axis=0).astype(MXU)
    vb = jnp.concatenate([vp[:, sl], vc[:, sl]], axis=0).astype(MXU)
    qg = jnp.concatenate([q[:, AH * (QPK * hk + g):AH * (QPK * hk + g + 1)] for g in range(QPK)], axis=0).astype(MXU)
    s = lax.dot_general(qg, kb, (((1,), (1,)), ((), ())), preferred_element_type=f32) * (AH ** -0.5)
    s = jnp.where(mask, s, -1e30)
    sink = jnp.concatenate([jnp.broadcast_to(sinks[:, QPK * hk + g:QPK * hk + g + 1], (WIN, 1)) for g in range(QPK)], axis=0)
    m = jnp.maximum(jnp.max(s, axis=1, keepdims=True), sink)
    p = jnp.exp(s - m)
    ps = jnp.exp(sink - m)
    den = jnp.sum(p, axis=1, keepdims=True) + ps
    return qg, kb, vb, p / den, ps / den


def _kv_specs():
    prev = lambda n: (jnp.maximum(n - 1, 0), 0)
    cur = lambda n: (n, 0)
    w = KVH * AH
    return [pl.BlockSpec((WIN, w), prev), pl.BlockSpec((WIN, w), cur), pl.BlockSpec((WIN, w), prev), pl.BlockSpec((WIN, w), cur)]


def attn_fwd(q, k, v, sinks):
    def body(q_ref, kp_ref, kc_ref, vp_ref, vc_ref, s_ref, o_ref):
        n = pl.program_id(0)
        q_, kp, kc, vp, vc, sk = q_ref[...], kp_ref[...], kc_ref[...], vp_ref[...], vc_ref[...], s_ref[...]
        outs = []
        for hk in range(KVH):
            _, _, vb, pr, _ = _attn_block(n, q_, kp, kc, vp, vc, sk, hk)
            o = jnp.dot(pr.astype(MXU), vb, preferred_element_type=f32)
            outs += [o[WIN * g:WIN * (g + 1)] for g in range(QPK)]
        o_ref[...] = jnp.concatenate(outs, axis=1)

    return pl.pallas_call(
        body, name="attn_fwd", grid=(NBLK,),
        in_specs=[pl.BlockSpec((WIN, D), lambda n: (n, 0))] + _kv_specs() + [pl.BlockSpec((1, QH), lambda n: (0, 0))],
        out_specs=pl.BlockSpec((WIN, D), lambda n: (n, 0)),
        out_shape=jax.ShapeDtypeStruct((S, D), f32),
        compiler_params=_cparams(("parallel",)),
    )(q, k, k, v, v, sinks)


def attn_bwd(q, k, v, sinks, dout):
    def body(q_ref, kp_ref, kc_ref, vp_ref, vc_ref, s_ref, do_ref, dq_ref, dkp_ref, dkc_ref, dvp_ref, dvc_ref, ds_ref):
        n = pl.program_id(0)

        @pl.when(n == 0)
        def _():
            ds_ref[...] = jnp.zeros_like(ds_ref)

        q_, kp, kc, vp, vc, sk, do = q_ref[...], kp_ref[...], kc_ref[...], vp_ref[...], vc_ref[...], s_ref[...], do_ref[...]
        lane = lax.broadcasted_iota(jnp.int32, (1, QH), 1)
        nt = (((1,), (1,)), ((), ()))
        tn = (((0,), (0,)), ((), ()))
        dqs, dkps, dkcs, dvps, dvcs = [], [], [], [], []
        dsink = jnp.zeros((1, QH), f32)
        for hk in range(KVH):
            qg, kb, vb, pr, prs = _attn_block(n, q_, kp, kc, vp, vc, sk, hk)
            dog = jnp.concatenate([do[:, AH * (QPK * hk + g):AH * (QPK * hk + g + 1)] for g in range(QPK)], axis=0).astype(MXU)
            dp = lax.dot_general(dog, vb, nt, preferred_element_type=f32)
            dvb = lax.dot_general(pr.astype(MXU), dog, tn, preferred_element_type=f32)
            delta = jnp.sum(pr * dp, axis=1, keepdims=True)
            ds = (pr * (dp - delta)).astype(MXU)
            dsk = -prs * delta
            for g in range(QPK):
                dsink = dsink + jnp.sum(dsk[WIN * g:WIN * (g + 1)], axis=0, keepdims=True) * (lane == QPK * hk + g).astype(f32)
            dqg = jnp.dot(ds, kb, preferred_element_type=f32) * (AH ** -0.5)
            dkb = lax.dot_general(ds, qg, tn, preferred_element_type=f32) * (AH ** -0.5)
            dqs += [dqg[WIN * g:WIN * (g + 1)] for g in range(QPK)]
            dkps.append(dkb[:WIN])
            dkcs.append(dkb[WIN:])
            dvps.append(dvb[:WIN])
            dvcs.append(dvb[WIN:])
        dq_ref[...] = jnp.concatenate(dqs, axis=1)
        dkp_ref[...] = jnp.concatenate(dkps, axis=1)
        dkc_ref[...] = jnp.concatenate(dkcs, axis=1)
        dvp_ref[...] = jnp.concatenate(dvps, axis=1)
        dvc_ref[...] = jnp.concatenate(dvcs, axis=1)
        ds_ref[...] += dsink

    w = KVH * AH
    blk = lambda width: pl.BlockSpec((WIN, width), lambda n: (n, 0))
    return pl.pallas_call(
        body, name="attn_bwd", grid=(NBLK,),
        in_specs=[blk(D)] + _kv_specs() + [pl.BlockSpec((1, QH), lambda n: (0, 0)), blk(D)],
        out_specs=[blk(D), blk(w), blk(w), blk(w), blk(w), pl.BlockSpec((1, QH), lambda n: (0, 0))],
        out_shape=[jax.ShapeDtypeStruct((S, D), f32)] + [jax.ShapeDtypeStruct((S, w), f32)] * 4 + [jax.ShapeDtypeStruct((1, QH), f32)],
        compiler_params=_cparams(("arbitrary",)),
    )(q, k, k, v, v, sinks, dout)


def kv_bwd(kv, pos, inv_freq, k_norm, dkp, dkc, dvp, dvc):
    w = KVH * AH

    def body(kv_ref, pos_ref, if_ref, g_ref, dkp_ref, dkc_ref, dvp_ref, dvc_ref, o_ref, dg_ref, db_ref):
        n = pl.program_id(0)

        @pl.when(n == 0)
        def _():
            dg_ref[...] = jnp.zeros_like(dg_ref)
            db_ref[...] = jnp.zeros_like(db_ref)

        inside = (n < NBLK - 1).astype(f32)
        dk = dkc_ref[...] + inside * dkp_ref[...]
        dv = dvc_ref[...] + inside * dvp_ref[...]
        cos, sin = _rope_tables(pos_ref[...], if_ref[...])
        dkpre, dg = _headnorm_rope_bwd(kv_ref[...], g_ref[...], cos, sin, dk, KVH)
        dkv = jnp.concatenate([dkpre, dv], axis=1)
        o_ref[...] = dkv.astype(o_ref.dtype)
        dg_ref[...] += dg
        db_ref[...] += jnp.sum(dkv, axis=0, keepdims=True)

    nxt = lambda n: (jnp.minimum(n + 1, NBLK - 1), 0)
    cur = lambda n: (n, 0)
    const = lambda n: (0, 0)
    return pl.pallas_call(
        body, name="kv_bwd", grid=(NBLK,),
        in_specs=[pl.BlockSpec((WIN, w), cur), pl.BlockSpec((WIN, 1), cur), pl.BlockSpec((1, AH // 2), const),
                  pl.BlockSpec((1, AH), const), pl.BlockSpec((WIN, w), nxt), pl.BlockSpec((WIN, w), cur),
                  pl.BlockSpec((WIN, w), nxt), pl.BlockSpec((WIN, w), cur)],
        out_specs=[pl.BlockSpec((WIN, 2 * w), cur), pl.BlockSpec((1, AH), const), pl.BlockSpec((1, 2 * w), const)],
        out_shape=[jax.ShapeDtypeStruct((S, 2 * w), MXU), jax.ShapeDtypeStruct((1, AH), f32), jax.ShapeDtypeStruct((1, 2 * w), f32)],
        compiler_params=_cparams(("arbitrary",)),
    )(kv, pos, inv_freq, k_norm, dkp, dkc, dvp, dvc)


def adamw(name, w, g, m, v):
    R, C = w.shape
    tr = _pick(R, (256, 128, 64, 32, 16, 8))
    tc = C if tr < R or C % 256 else 256

    def body(w_ref, g_ref, m_ref, v_ref, d_ref, nm_ref, nv_ref):
        g_ = g_ref[...]
        m_ = ADAM_B1 * m_ref[...] + (1.0 - ADAM_B1) * g_
        v_ = ADAM_B2 * v_ref[...] + (1.0 - ADAM_B2) * (g_ * g_)
        m_hat = m_ / (1.0 - ADAM_B1 ** ADAM_STEP)
        v_hat = v_ / (1.0 - ADAM_B2 ** ADAM_STEP)
        d_ref[...] = -ADAM_LR * (m_hat / (jnp.sqrt(v_hat) + ADAM_EPS) + ADAM_WD * w_ref[...])
        nm_ref[...] = m_
        nv_ref[...] = v_

    spec = pl.BlockSpec((tr, tc), lambda i, j: (i, j))
    return pl.pallas_call(
        body, name=name, grid=(R // tr, C // tc), in_specs=[spec] * 4, out_specs=[spec] * 3,
        out_shape=[jax.ShapeDtypeStruct((R, C), f32)] * 3, compiler_params=_cparams(("parallel", "parallel")),
    )(w, g, m, v)


def _me():
    return lax.axis_index("x"), lax.axis_index("y"), lax.axis_index("c")


def all_gather(name, xs):
    n = len(xs)

    def body(*refs):
        x_refs, o_refs = refs[:n], refs[n:2 * n]
        send_sems, recv_sems, local_sems = refs[2 * n:]
        x, y, c = _me()
        me, sibling = (x, y, c), (x, y, 1 - c)
        chips = [(1 - x, y), (x, 1 - y), (1 - x, 1 - y)]

        def copy(a, k, block, to, src=None):
            dst = o_refs[a].at[4 * block[0] + 2 * block[1] + block[2]]
            return pltpu.make_async_remote_copy(
                src_ref=dst if src is None else src, dst_ref=dst,
                send_sem=send_sems.at[7 * a + k], recv_sem=recv_sems.at[7 * a + k], device_id=to, device_id_type=MESH)

        mine = [pltpu.make_async_copy(x_refs[a], o_refs[a].at[4 * x + 2 * y + c], local_sems.at[a]) for a in range(n)]
        for cp in mine:
            cp.start()
        first = []
        for a in range(n):
            first.append(copy(a, 0, me, sibling, src=x_refs[a]))
            first += [copy(a, 1 + j, me, (*chip, c), src=x_refs[a]) for j, chip in enumerate(chips)]
        for cp in first:
            cp.start()
        passed = []
        for j, chip in enumerate(chips):
            for a in range(n):
                copy(a, 1 + j, (*chip, c), me).wait_recv()
                cp = copy(a, 4 + j, (*chip, c), sibling)
                cp.start()
                passed.append(cp)
        for a in range(n):
            copy(a, 0, sibling, me).wait_recv()
            for j, chip in enumerate(chips):
                copy(a, 4 + j, (*chip, 1 - c), me).wait_recv()
        for cp in first + passed:
            cp.wait_send()
        for cp in mine:
            cp.wait()

    any_spec = pl.BlockSpec(memory_space=pl.ANY)
    return pl.pallas_call(
        body, name=name, in_specs=[any_spec] * n, out_specs=[any_spec] * n,
        out_shape=[jax.ShapeDtypeStruct((N_DEV,) + a.shape, a.dtype) for a in xs],
        scratch_shapes=[pltpu.SemaphoreType.DMA((7 * n,)), pltpu.SemaphoreType.DMA((7 * n,)), pltpu.SemaphoreType.DMA((n,))],
    )(*xs)


def sibling_exchange(gs):
    n = len(gs)

    def body(*refs):
        g_refs, o_refs, send_sems, recv_sems = refs[:n], refs[n:2 * n], refs[2 * n], refs[2 * n + 1]
        x, y, c = _me()
        cps = [pltpu.make_async_remote_copy(
            src_ref=g_refs[a].at[:, 1 - c], dst_ref=o_refs[a], send_sem=send_sems.at[a], recv_sem=recv_sems.at[a],
            device_id=(x, y, 1 - c), device_id_type=MESH) for a in range(n)]
        for cp in cps:
            cp.start()
        for cp in cps:
            cp.wait()

    any_spec = pl.BlockSpec(memory_space=pl.ANY)
    return pl.pallas_call(
        body, name="rs_sibling_exchange", in_specs=[any_spec] * n, out_specs=[any_spec] * n,
        out_shape=[jax.ShapeDtypeStruct((4,) + g.shape[2:], g.dtype) for g in gs],
        scratch_shapes=[pltpu.SemaphoreType.DMA((n,)), pltpu.SemaphoreType.DMA((n,))],
    )(*gs)


def chip_exchange(ts):
    n = len(ts)

    def body(*refs):
        t_refs, o_refs, send_sems, recv_sems = refs[:n], refs[n:2 * n], refs[2 * n], refs[2 * n + 1]
        x, y, c = _me()
        chips = [(1 - x, y), (x, 1 - y), (1 - x, 1 - y)]
        cps = [pltpu.make_async_remote_copy(
            src_ref=t_refs[a].at[2 * px + py], dst_ref=o_refs[a].at[j],
            send_sem=send_sems.at[3 * a + j], recv_sem=recv_sems.at[3 * a + j],
            device_id=(px, py, c), device_id_type=MESH) for j, (px, py) in enumerate(chips) for a in range(n)]
        for cp in cps:
            cp.start()
        for cp in cps:
            cp.wait()

    any_spec = pl.BlockSpec(memory_space=pl.ANY)
    return pl.pallas_call(
        body, name="rs_chip_exchange", in_specs=[any_spec] * n, out_specs=[any_spec] * n,
        out_shape=[jax.ShapeDtypeStruct((3,) + t.shape[1:], t.dtype) for t in ts],
        scratch_shapes=[pltpu.SemaphoreType.DMA((3 * n,)), pltpu.SemaphoreType.DMA((3 * n,))],
    )(*ts)


def _row_tile(rows):
    return _pick(rows, (512, 304, 256, 128))


def pair_add(name, g, r):
    _, _, R, C = g.shape
    tr = _row_tile(R)

    def body(c_ref, g_ref, r_ref, o_ref):
        o_ref[0] = (g_ref[0, 0].astype(f32) + r_ref[0].astype(f32)).astype(o_ref.dtype)

    return pl.pallas_call(
        body, name=name,
        grid_spec=pltpu.PrefetchScalarGridSpec(
            num_scalar_prefetch=1, grid=(4, R // tr),
            in_specs=[pl.BlockSpec((1, 1, tr, C), lambda p, i, c: (p, c[0], i, 0)),
                      pl.BlockSpec((1, tr, C), lambda p, i, c: (p, i, 0))],
            out_specs=pl.BlockSpec((1, tr, C), lambda p, i, c: (p, i, 0))),
        out_shape=jax.ShapeDtypeStruct((4, R, C), g.dtype),
        compiler_params=_cparams(("parallel", "parallel")),
    )(lax.axis_index("c").reshape(1).astype(jnp.int32), g, r)


def final_add(name, t, r):
    _, R, C = t.shape
    tr = _row_tile(R)

    def body(p_ref, t_ref, r_ref, o_ref):
        o_ref[...] = ((t_ref[0].astype(f32) + r_ref[0].astype(f32)) + r_ref[1].astype(f32)) + r_ref[2].astype(f32)

    chip = 2 * lax.axis_index("x") + lax.axis_index("y")
    return pl.pallas_call(
        body, name=name,
        grid_spec=pltpu.PrefetchScalarGridSpec(
            num_scalar_prefetch=1, grid=(R // tr,),
            in_specs=[pl.BlockSpec((1, tr, C), lambda i, p: (p[0], i, 0)),
                      pl.BlockSpec((3, tr, C), lambda i, p: (0, i, 0))],
            out_specs=pl.BlockSpec((tr, C), lambda i, p: (i, 0))),
        out_shape=jax.ShapeDtypeStruct((R, C), f32),
        compiler_params=_cparams(("parallel",)),
    )(chip.reshape(1).astype(jnp.int32), t, r)


def reduce_scatter(gs):
    gs = [g.reshape((4, 2) + g.shape[1:]) for g in gs]
    pairs = [pair_add(f"rs_pair_add_{i}", g, r) for i, (g, r) in enumerate(zip(gs, sibling_exchange(gs)))]
    return [final_add(f"rs_final_add_{i}", t, r) for i, (t, r) in enumerate(zip(pairs, chip_exchange(pairs)))]


IN_ROWS = {"z": (0, INNER), "xs": (1, INNER), "B": (4, GROUPS * NSTATE), "C": (5, GROUPS * NSTATE)}
IN_COLS = 2 * INNER + 2 * GROUPS * NSTATE + HEADS


def sum_devices(g):
    def body(g_ref, o_ref):
        acc = g_ref[0]
        for i in range(1, N_DEV):
            acc = acc + g_ref[i]
        o_ref[...] = acc

    return pl.pallas_call(body, name="sum_devices", out_shape=jax.ShapeDtypeStruct(g.shape[1:], f32),
                          compiler_params=_cparams())(g)


def _pack(parts, unit, dtype, lead=()):
    flat = jnp.concatenate([p.reshape(lead + (-1,)).astype(dtype) for p in parts], axis=-1)
    n = flat.shape[-1]
    rows = -(-n // (unit * PACK_W)) * unit
    flat = jnp.pad(flat, [(0, 0)] * len(lead) + [(0, rows * PACK_W - n)])
    return flat.reshape(lead + (rows, PACK_W))


def _unpack(buf, shapes, lead=()):
    flat = buf.reshape(lead + (-1,))
    out, off = [], 0
    for shp in shapes:
        n = math.prod(shp)
        out.append(flat[..., off:off + n].reshape(lead + tuple(shp)))
        off += n
    return out


def _pad_lanes(a):
    return jnp.pad(a, [(0, 0)] * (a.ndim - 1) + [(0, LANES - a.shape[-1])])


def _ffn_fwd(tag, x, norm_g, w_inT, conv_w, conv_b, w_down):
    (h,) = rowwise(f"{tag}_norm", lambda x_, g_: (_rms_fwd(x_, g_),), [x], [norm_g], [(D, MXU)])
    gate_pre = matmul(f"{tag}_gate", h, w_inT, "nt", b_rows=(0, FFN))
    val = matmul(f"{tag}_val", h, w_inT, "nt", b_rows=(1, FFN))
    (act,) = colwise(f"{tag}_mid", lambda gp, v_, w_, b_: (_silu(_conv(gp, w_, b_)) * v_,), [gate_pre, val], [conv_w, conv_b], [MXU])
    out = matmul(f"{tag}_down", act, w_down, "nn", residual=x)
    return out, (x, h, gate_pre, val, act)


def _ffn_mid_bwd(name, gate_pre, val, dact, conv_w, conv_b):
    ct = 128

    def body(gp_ref, v_ref, da_ref, w_ref, b_ref, dgv_ref, dw_ref, db_ref):
        gp, v_, da, w_ = gp_ref[...], v_ref[...], da_ref[...], w_ref[...]
        gate = _conv(gp, w_, b_ref[...])
        dgp, dw, db = _conv_bwd(gp, w_, da * v_ * _dsilu(gate))
        dgv_ref[0] = dgp.astype(dgv_ref.dtype)
        dgv_ref[1] = (da * _silu(gate)).astype(dgv_ref.dtype)
        dw_ref[...] = dw
        db_ref[...] = db

    col = pl.BlockSpec((S, ct), lambda j: (0, j))
    return pl.pallas_call(
        body, name=name, grid=(FFN // ct,),
        in_specs=[col, col, col, pl.BlockSpec((CONV_F, ct), lambda j: (0, j)), pl.BlockSpec((1, ct), lambda j: (0, j))],
        out_specs=[pl.BlockSpec((2, S, ct), lambda j: (0, 0, j)), pl.BlockSpec((CONV_F, ct), lambda j: (0, j)),
                   pl.BlockSpec((1, ct), lambda j: (0, j))],
        out_shape=[jax.ShapeDtypeStruct((2, S, FFN), MXU), jax.ShapeDtypeStruct((CONV_F, FFN), f32), jax.ShapeDtypeStruct((1, FFN), f32)],
        compiler_params=_cparams(("parallel",)),
    )(gate_pre, val, dact, conv_w, conv_b)


def _ffn_bwd(tag, saved, norm_g, w_inT, conv_w, conv_b, w_down, dout):
    x, h, gate_pre, val, act = saved
    dact = matmul(f"{tag}_ddown", dout, w_down, "nt")
    g_down = matmul(f"{tag}_wdown", act, dout, "tn", out_dtype=MXU)
    dgv, g_cw, g_cb = _ffn_mid_bwd(f"{tag}_dmid", gate_pre, val, dact, conv_w, conv_b)
    dh = matmul(f"{tag}_dgate", dgv, w_inT, "nn", b_rows=(0, FFN), a_lead=0)
    dh = matmul(f"{tag}_dval", dgv, w_inT, "nn", b_rows=(1, FFN), a_lead=1, residual=dh)
    g_inT = matmul_tn_stacked(f"{tag}_win", dgv, h, MXU)

    def norm_bwd(x_, dh_, do_, g_):
        dx, dg = _rms_bwd(x_, g_, dh_)
        return do_ + dx, dg

    dx, g_norm = rowwise(f"{tag}_dnorm", norm_bwd, [x, dh, dout], [norm_g], [(D, f32)], [(1, D)])
    return dx, dict(norm=g_norm, inT=g_inT, conv_w=g_cw, conv_b=g_cb, down=g_down)


def _local_step(x, pos, tgt, W):
    G = {}
    inv_freq = (ROPE_THETA ** (-jnp.arange(AH // 2, dtype=f32) / (AH // 2))).reshape(1, AH // 2)

    (h0,) = rowwise("a_norm", lambda x_, g_: (_rms_fwd(x_, g_),), [x], [W["a_norm"]], [(D, MXU)])
    z = matmul("a_in_z", h0, W["inT"], "nt", b_rows=IN_ROWS["z"])
    pre = {k: matmul(f"a_in_{k}", h0, W["inT"], "nt", b_rows=IN_ROWS[k]) for k in ("xs", "B", "C")}
    dt_pre = matmul("a_in_dt", h0, W["in_dtT"], "nt")
    conv = {}
    for k in ("xs", "B", "C"):
        (conv[k],) = colwise(f"a_conv_{k}", lambda p_, w_, b_: (_silu(_conv(p_, w_, b_)),), [pre[k]], [W[f"cw_{k}"], W[f"cb_{k}"]], [f32])
    y, states = ssd_fwd(conv["xs"], conv["B"], conv["C"], dt_pre, W["dt_bias"], W["A_log"], W["D"])

    def gate_norm(y_, z_, g_):
        yg = y_ * _silu(z_)
        w = INNER // GROUPS
        return (jnp.concatenate([_rms_fwd(yg[:, w * i:w * (i + 1)], g_[:, w * i:w * (i + 1)]) for i in range(GROUPS)], axis=1),)

    (gn,) = rowwise("a_gnorm", gate_norm, [y, z], [W["a_gnorm"]], [(INNER, MXU)], tile=128)
    x1 = matmul("a_out", gn, W["a_out"], "nn", residual=x)

    x2, ffn0 = _ffn_fwd("f0", x1, W["f_norm0"], W["f_inT0"], W["f_cw0"], W["f_cb0"], W["f_down0"])

    (kvn,) = rowwise("kv_norm", lambda x_, g_: (_rms_fwd(x_, g_),), [x2], [W["kv_norm"]], [(D, MXU)])
    kv = matmul("kv_proj", kvn, W["w_kv"], "nn", bias=W["b_kv"])
    kw = KVH * AH

    def k_fwd(kv_, pos_, if_, g_):
        cos, sin = _rope_tables(pos_, if_)
        return _headnorm_rope_fwd(kv_[:, :kw], g_, cos, sin, KVH), kv_[:, kw:]

    k_rot, v_val = rowwise("k_rope", k_fwd, [kv, pos], [inv_freq, W["k_norm"]], [(kw, f32), (kw, f32)])
    (h2,) = rowwise("b_norm", lambda x_, g_: (_rms_fwd(x_, g_),), [x2], [W["b_norm"]], [(D, MXU)])
    q_pre = matmul("q_proj", h2, W["w_q"], "nn", bias=W["b_q"])

    def q_fwd(q_, pos_, if_, g_):
        cos, sin = _rope_tables(pos_, if_)
        return (_headnorm_rope_fwd(q_, g_, cos, sin, QH),)

    (q,) = rowwise("q_rope", q_fwd, [q_pre, pos], [inv_freq, W["q_norm"]], [(D, f32)])
    att = attn_fwd(q, k_rot, v_val, W["sinks"])
    x3 = matmul("o_proj", att, W["w_o"], "nn", bias=W["b_o"], residual=x2)

    x4, ffn1 = _ffn_fwd("f1", x3, W["f_norm1"], W["f_inT1"], W["f_cw1"], W["f_cb1"], W["f_down1"])

    def loss_fn(y_, t_):
        diff = y_ - t_
        rows = jnp.sum(diff * diff, axis=1, keepdims=True) * (0.5 / D)
        return diff * (1.0 / D), jnp.sum(rows, axis=0, keepdims=True)

    dx4, loss = rowwise("loss", loss_fn, [x4, tgt], [], [(D, f32)], [(1, 1)])

    dx3, g = _ffn_bwd("f1", ffn1, W["f_norm1"], W["f_inT1"], W["f_cw1"], W["f_cb1"], W["f_down1"], dx4)
    G.update({f"f_{k}1": val for k, val in g.items()})

    datt = matmul("o_dproj", dx3, W["w_o"], "nt")
    G["w_o"] = matmul("o_wproj", att, dx3, "tn", out_dtype=MXU)
    dq, dkp, dkc, dvp, dvc, G["sinks"] = attn_bwd(q, k_rot, v_val, W["sinks"], datt)

    def q_bwd(q_, pos_, dq_, dx_, if_, g_):
        cos, sin = _rope_tables(pos_, if_)
        dqp, dg = _headnorm_rope_bwd(q_, g_, cos, sin, dq_, QH)
        return dqp, dg, jnp.sum(dqp, axis=0, keepdims=True), jnp.sum(dx_, axis=0, keepdims=True)

    dq_pre, G["q_norm"], G["b_q"], G["b_o"] = rowwise("q_drope", q_bwd, [q_pre, pos, dq, dx3], [inv_freq, W["q_norm"]],
                                                      [(D, MXU)], [(1, AH), (1, D), (1, D)])
    dh2 = matmul("q_dproj", dq_pre, W["w_q"], "nt")
    G["w_q"] = matmul("q_wproj", h2, dq_pre, "tn", out_dtype=MXU)
    dkv, G["k_norm"], G["b_kv"] = kv_bwd(kv, pos, inv_freq, W["k_norm"], dkp, dkc, dvp, dvc)
    dkvn = matmul("kv_dproj", dkv, W["w_kv"], "nt")
    G["w_kv"] = matmul("kv_wproj", kvn, dkv, "tn", out_dtype=MXU)

    def x2_bwd(x_, dh2_, dkvn_, dx_, gb_, gk_):
        d1, dgb = _rms_bwd(x_, gb_, dh2_)
        d2, dgk = _rms_bwd(x_, gk_, dkvn_)
        return dx_ + d1 + d2, dgb, dgk

    dx2, G["b_norm"], G["kv_norm"] = rowwise("x2_dnorm", x2_bwd, [x2, dh2, dkvn, dx3], [W["b_norm"], W["kv_norm"]],
                                             [(D, f32)], [(1, D), (1, D)])

    dx1, g = _ffn_bwd("f0", ffn0, W["f_norm0"], W["f_inT0"], W["f_cw0"], W["f_cb0"], W["f_down0"], dx2)
    G.update({f"f_{k}0": val for k, val in g.items()})

    dgn = matmul("a_dout", dx1, W["a_out"], "nt")
    G["a_out"] = matmul("a_wout", gn, dx1, "tn", out_dtype=MXU)

    def gate_norm_bwd(y_, z_, dgn_, g_):
        w = INNER // GROUPS
        sz = _silu(z_)
        yg = y_ * sz
        parts, dgs = [], []
        for i in range(GROUPS):
            dseg, dg = _rms_bwd(yg[:, w * i:w * (i + 1)], g_[:, w * i:w * (i + 1)], dgn_[:, w * i:w * (i + 1)])
            parts.append(dseg)
            dgs.append(dg)
        dyg = jnp.concatenate(parts, axis=1)
        return dyg * sz, dyg * y_ * _dsilu(z_), jnp.concatenate(dgs, axis=1)

    dy, dz, G["a_gnorm"] = rowwise("a_dgnorm", gate_norm_bwd, [y, z, dgn], [W["a_gnorm"]], [(INNER, f32), (INNER, MXU)], [(1, INNER)], tile=128)
    dconv = {}
    dconv["xs"], dconv["B"], dconv["C"], ddt_pre, G["dt_bias"], G["A_log"], G["D"] = ssd_bwd(
        conv["xs"], conv["B"], conv["C"], dt_pre, W["dt_bias"], W["A_log"], W["D"], states, dy)

    def conv_bwd(p_, do_, w_, b_):
        c = _conv(p_, w_, b_)
        return _conv_bwd(p_, w_, do_ * _dsilu(c))

    dh0 = matmul("a_din_z", dz, W["inT"], "nn", b_rows=IN_ROWS["z"])
    G["inT_z"] = matmul("a_win_z", dz, h0, "tn", out_dtype=MXU)
    for k in ("xs", "B", "C"):
        dpre, G[f"cw_{k}"], G[f"cb_{k}"] = colwise(f"a_dconv_{k}", conv_bwd, [pre[k], dconv[k]], [W[f"cw_{k}"], W[f"cb_{k}"]], [MXU], [CONV_A, 1])
        dh0 = matmul(f"a_din_{k}", dpre, W["inT"], "nn", b_rows=IN_ROWS[k], residual=dh0)
        G[f"inT_{k}"] = matmul(f"a_win_{k}", dpre, h0, "tn", out_dtype=MXU)
    dh0 = matmul("a_din_dt", ddt_pre, W["in_dtT"], "nn", residual=dh0)
    G["inT_dt"] = matmul("a_win_dt", ddt_pre, h0, "tn", out_dtype=MXU)

    def x0_bwd(x_, dh_, do_, g_):
        dx, dg = _rms_bwd(x_, g_, dh_)
        return do_ + dx, dg

    dx, G["a_norm"] = rowwise("a_dnorm", x0_bwd, [x, dh0, dx1], [W["a_norm"]], [(D, f32)], [(1, D)])
    return loss, dx, G


ROW_KEYS = ("inT", "a_out", "f_inT0", "f_down0", "w_kv", "w_q", "w_o", "f_inT1", "f_down1")


def _row_blocks(src):
    return {"inT": src["a_in_proj"][0].T, "a_out": src["a_out_proj"][0], "w_kv": src["w_kv"], "w_q": src["w_q"][0],
            "w_o": src["w_o"][0], "f_inT0": src["f_w_in"][0].T, "f_inT1": src["f_w_in"][1].T,
            "f_down0": src["f_w_down"][0], "f_down1": src["f_w_down"][1]}


def _from_row_blocks(rb):
    return {"a_in_proj": rb["inT"].T[None], "a_out_proj": rb["a_out"][None], "w_kv": rb["w_kv"], "w_q": rb["w_q"][None],
            "w_o": rb["w_o"][None], "f_w_in": jnp.stack([rb["f_inT0"].T, rb["f_inT1"].T]),
            "f_w_down": jnp.stack([rb["f_down0"], rb["f_down1"]])}


SMALL_SHARDED = ("a_norm", "a_conv_w", "a_conv_b", "a_gnorm", "f_conv_w")
REPLICATED = ("a_dt_bias", "a_A_log", "a_D", "kv_norm", "b_kv", "k_norm", "b_norm", "b_q", "q_norm", "sinks", "b_o",
              "f_norm", "f_conv_b")
ORDER = ("a_norm", "a_in_proj", "a_conv_w", "a_conv_b", "a_dt_bias", "a_A_log", "a_D", "a_gnorm", "a_out_proj", "kv_norm",
         "w_kv", "b_kv", "k_norm", "b_norm", "w_q", "b_q", "q_norm", "sinks", "w_o", "b_o", "f_norm", "f_w_in",
         "f_conv_w", "f_conv_b", "f_w_down")


def _gathered_to_whole(name, g):
    if name == "a_conv_w":
        return jnp.moveaxis(g[:, 0], 0, 1).reshape(g.shape[2], -1)
    if name in ("a_norm", "a_conv_b", "a_gnorm"):
        return g[:, 0].reshape(1, -1)
    if name == "f_conv_w":
        return jnp.moveaxis(g, 0, 2).reshape(g.shape[1], g.shape[2], -1)
    raise ValueError(name)


def _whole_to_shards(name, w):
    if name == "a_conv_w":
        return jnp.moveaxis(w.reshape(w.shape[0], N_DEV, -1), 1, 0)[:, None]
    if name in ("a_norm", "a_conv_b", "a_gnorm"):
        return w.reshape(N_DEV, 1, -1)
    if name == "f_conv_w":
        return jnp.moveaxis(w.reshape(w.shape[0], w.shape[1], N_DEV, -1), 2, 0)
    raise ValueError(name)


def _small_weights(whole):
    W = {}
    cw, cb = whole["a_conv_w"], whole["a_conv_b"]
    o = 0
    for k, n in (("xs", INNER), ("B", GROUPS * NSTATE), ("C", GROUPS * NSTATE)):
        W[f"cw_{k}"], W[f"cb_{k}"] = cw[:, o:o + n], cb[:, o:o + n]
        o += n
    W["a_norm"], W["a_gnorm"] = whole["a_norm"], whole["a_gnorm"]
    W["dt_bias"], W["A_log"], W["D"] = (_pad_lanes(whole[k]) for k in ("a_dt_bias", "a_A_log", "a_D"))
    W["kv_norm"], W["b_kv"], W["k_norm"] = whole["kv_norm"].reshape(1, -1), whole["b_kv"].reshape(1, -1), whole["k_norm"].reshape(1, -1)
    for k in ("b_norm", "b_q", "q_norm", "sinks", "b_o"):
        W[k] = whole[k]
    for i in range(2):
        W[f"f_norm{i}"] = whole["f_norm"][i:i + 1]
        W[f"f_cw{i}"], W[f"f_cb{i}"] = whole["f_conv_w"][i], whole["f_conv_b"][i:i + 1]
    return W


def _small_grads(G, shapes):
    nh = HEADS
    out = {
        "a_conv_w": jnp.concatenate([G["cw_xs"], G["cw_B"], G["cw_C"]], axis=1),
        "a_conv_b": jnp.concatenate([G["cb_xs"], G["cb_B"], G["cb_C"]], axis=1),
        "a_norm": G["a_norm"], "a_gnorm": G["a_gnorm"],
        "a_dt_bias": G["dt_bias"][:, :nh], "a_A_log": G["A_log"][:, :nh], "a_D": G["D"][:, :nh],
        "kv_norm": G["kv_norm"], "b_kv": G["b_kv"], "k_norm": G["k_norm"], "b_norm": G["b_norm"],
        "b_q": G["b_q"], "q_norm": G["q_norm"], "sinks": G["sinks"], "b_o": G["b_o"],
        "f_norm": jnp.concatenate([G["f_norm0"], G["f_norm1"]], axis=0),
        "f_conv_w": jnp.stack([G["f_conv_w0"], G["f_conv_w1"]]),
        "f_conv_b": jnp.concatenate([G["f_conv_b0"], G["f_conv_b1"]], axis=0),
    }
    return {k: val.reshape(shapes[k]) if k in shapes else val for k, val in out.items()}


def kernel(x, positions, a_norm, a_in_proj, a_conv_w, a_conv_b, a_dt_bias, a_A_log, a_D, a_gnorm, a_out_proj, kv_norm, w_kv, b_kv, k_norm, b_norm, w_q, b_q, q_norm, sinks, w_o, b_o, f_norm, f_w_in, f_conv_w, f_conv_b, f_w_down, loss_target, m_a_norm, m_a_in_proj, m_a_conv_w, m_a_conv_b, m_a_dt_bias, m_a_A_log, m_a_D, m_a_gnorm, m_a_out_proj, m_kv_norm, m_w_kv, m_b_kv, m_k_norm, m_b_norm, m_w_q, m_b_q, m_q_norm, m_sinks, m_w_o, m_b_o, m_f_norm, m_f_w_in, m_f_conv_w, m_f_conv_b, m_f_w_down, v_a_norm, v_a_in_proj, v_a_conv_w, v_a_conv_b, v_a_dt_bias, v_a_A_log, v_a_D, v_a_gnorm, v_a_out_proj, v_kv_norm, v_w_kv, v_b_kv, v_k_norm, v_b_norm, v_w_q, v_b_q, v_q_norm, v_sinks, v_w_o, v_b_o, v_f_norm, v_f_w_in, v_f_conv_w, v_f_conv_b, v_f_w_down):
    given = dict(locals())
    w_in = {n: given[n] for n in ORDER}
    m_in = {n: given["m_" + n] for n in ORDER}
    v_in = {n: given["v_" + n] for n in ORDER}
    dev = 4 * lax.axis_index("x") + 2 * lax.axis_index("y") + lax.axis_index("c")

    w2, m2, v2 = _row_blocks(w_in), _row_blocks(m_in), _row_blocks(v_in)
    small_pack = _pack([w_in[n] for n in SMALL_SHARDED], 8, f32)
    gathered = all_gather("ag_weights", [w2[k].astype(MXU) for k in ROW_KEYS] + [small_pack])
    whole = {n: w_in[n] for n in REPLICATED}
    for n, g in zip(SMALL_SHARDED, _unpack(gathered[-1], [w_in[n].shape for n in SMALL_SHARDED], lead=(N_DEV,))):
        whole[n] = _gathered_to_whole(n, g)
    W = _small_weights(whole)
    for k, g in zip(ROW_KEYS, gathered):
        W[k] = g.reshape(-1, g.shape[2])
    W["in_dtT"] = jnp.pad(W["inT"][IN_COLS - HEADS:], ((0, LANES - HEADS), (0, 0)))

    loss, dx, G = _local_step(x[0], positions.reshape(S, 1).astype(f32), loss_target[0], W)
    grads = _small_grads(G, {n: whole[n].shape for n in REPLICATED})

    G["inT"] = jnp.concatenate([G["inT_z"], G["inT_xs"], G["inT_B"], G["inT_C"], G["inT_dt"][:HEADS]], axis=0)
    sums = dict(zip(ROW_KEYS, reduce_scatter([G[k].reshape((N_DEV,) + w2[k].shape) for k in ROW_KEYS])))
    g_out = _from_row_blocks(sums)

    small_names = SMALL_SHARDED + REPLICATED
    small_part = _pack([grads[n] for n in small_names], 8, f32)
    (small_parts,) = all_gather("ag_small_grads", [small_part])
    small_sum = sum_devices(small_parts)
    for n, g in zip(small_names, _unpack(small_sum, [grads[n].shape for n in small_names])):
        if n in SMALL_SHARDED:
            g_out[n] = lax.dynamic_index_in_dim(_whole_to_shards(n, g), dev, axis=0, keepdims=False)
        else:
            g_out[n] = g.reshape(w_in[n].shape)

    stepped = {k: adamw(f"adamw_{k}", w2[k], sums[k], m2[k], v2[k]) for k in ROW_KEYS}
    delta, new_m, new_v = (_from_row_blocks({k: stepped[k][i] for k in ROW_KEYS}) for i in range(3))
    packs = [_pack([src[n] for n in small_names], 8, f32) for src in (w_in, g_out, m_in, v_in)]
    outs = adamw("adamw_small", *packs)
    for dst, buf in zip((delta, new_m, new_v), outs):
        for n, a in zip(small_names, _unpack(buf, [w_in[n].shape for n in small_names])):
            dst[n] = a

    loss_all = lax.psum(loss[0, 0], AXES)
    return (loss_all, dx[None], *[g_out[n] for n in ORDER], *[delta[n] for n in ORDER],
            *[new_m[n] for n in ORDER], *[new_v[n] for n in ORDER])
```

```python
import functools
import math

import jax
import jax.numpy as jnp
from jax import lax
from jax.experimental import pallas as pl
from jax.experimental.pallas import tpu as pltpu

f32 = jnp.float32
bf16 = jnp.bfloat16
MXU = bf16

N_DEV = 8
S = 2048
D = 1024
EPS = 1e-5
INNER = 2048
HEADS = 32
HP = 64
GROUPS = 8
HPG = HEADS // GROUPS
NSTATE = 128
CONV_A = 4
CHUNK = 256
NCHUNK = S // CHUNK
AH = 64
QH = 16
KVH = 4
QPK = QH // KVH
WIN = 128
NBLK = S // WIN
ROPE_THETA = 10000.0
FFN = 2816
CONV_F = 3
LANES = 128
PACK_W = 1024
BIG_TILE = 496
VMEM_LIMIT = 56 * 1024 * 1024

ADAM_LR, ADAM_B1, ADAM_B2, ADAM_EPS, ADAM_WD, ADAM_STEP = 0.001, 0.9, 0.999, 1e-08, 0.01, 10

MESH = pl.DeviceIdType.MESH
AXES = ("x", "y", "c")


def _cparams(sem=None):
    return pltpu.CompilerParams(dimension_semantics=sem, vmem_limit_bytes=VMEM_LIMIT)


def _pick(n, cands):
    for c in cands:
        if n % c == 0:
            return c
    return n


class Comm:
    def __init__(self, ins, out_shapes, sems, start, finish):
        self.ins, self.out_shapes, self.sems, self.start, self.finish = list(ins), list(out_shapes), list(sems), start, finish
        self.results, self.children = None, ()

    def set_results(self, res):
        self.results, o = list(res), 0
        for ch in self.children:
            ch.set_results(res[o:o + len(ch.out_shapes)])
            o += len(ch.out_shapes)


def merge_comms(comms):
    def each(fn_name, ins, outs, sems):
        i = o = s = 0
        for c in comms:
            getattr(c, fn_name)(ins[i:i + len(c.ins)], outs[o:o + len(c.out_shapes)], sems[s:s + len(c.sems)])
            i, o, s = i + len(c.ins), o + len(c.out_shapes), s + len(c.sems)

    merged = Comm([a for c in comms for a in c.ins], [a for c in comms for a in c.out_shapes], [a for c in comms for a in c.sems],
                  functools.partial(each, "start"), functools.partial(each, "finish"))
    merged.children = tuple(comms)
    return merged


def _call(body, args, *, name, grid, in_specs, out_specs, out_shape, scratch=(), sem=None, comm=None):
    if comm is None:
        return pl.pallas_call(body, name=name, grid=grid, in_specs=list(in_specs), out_specs=list(out_specs),
                              out_shape=list(out_shape), scratch_shapes=list(scratch), compiler_params=_cparams(sem))(*args)
    n_in, n_out, n_scr, c_in, c_out = len(in_specs), len(out_shape), len(scratch), len(comm.ins), len(comm.out_shapes)
    any_spec = pl.BlockSpec(memory_space=pl.ANY)

    def outer(*refs):
        ins, c_ins = refs[:n_in], refs[n_in:n_in + c_in]
        o = n_in + c_in
        outs, c_outs = refs[o:o + n_out], refs[o + n_out:o + n_out + c_out]
        o += n_out + c_out
        scr, c_sems = refs[o:o + n_scr], refs[o + n_scr:]
        ids = [pl.program_id(i) for i in range(len(grid))]
        first = functools.reduce(jnp.logical_and, [i == 0 for i in ids])
        last = functools.reduce(jnp.logical_and, [i == g - 1 for i, g in zip(ids, grid)])

        @pl.when(first)
        def _():
            comm.start(c_ins, c_outs, c_sems)

        body(*ins, *outs, *scr)

        @pl.when(last)
        def _():
            comm.finish(c_ins, c_outs, c_sems)

    res = pl.pallas_call(
        outer, name=name, grid=grid, in_specs=list(in_specs) + [any_spec] * c_in,
        out_specs=list(out_specs) + [any_spec] * c_out, out_shape=list(out_shape) + comm.out_shapes,
        scratch_shapes=list(scratch) + comm.sems, compiler_params=_cparams(("arbitrary",) * len(grid)),
    )(*args, *comm.ins)
    comm.set_results(res[n_out:])
    return res[:n_out]


def matmul(name, a, b, mode, out_dtype=f32, bias=None, residual=None, b_rows=None, a_lead=None):
    b_idx, b_shape = (0, b.shape) if b_rows is None else (b_rows[0], (b_rows[1], b.shape[1]))
    a_shape = a.shape if a_lead is None else a.shape[1:]
    if mode == "nn":
        (M, K), (K2, N) = a_shape, b_shape
    elif mode == "nt":
        (M, K), (N, K2) = a_shape, b_shape
    else:
        (K, M), (K2, N) = a_shape, b_shape
    assert K == K2, (name, a_shape, b_shape)
    if mode == "tn":
        tm, tn = M, _pick(N, (512, 256, 128) if M <= 1024 else (256, 128))
        a_spec = pl.BlockSpec((K, M), lambda j: (0, 0))
        b_spec = pl.BlockSpec((K, tn), lambda j: (0, j))
        dims = (((0,), (0,)), ((), ()))
        grid, o_map, row_map = (N // tn,), (lambda j: (0, j)), (lambda j: (0, j))
    else:
        tm, tn = (256 if N >= 2048 else 512), N
        a_spec = pl.BlockSpec((tm, K), lambda i: (i, 0)) if a_lead is None else pl.BlockSpec((None, tm, K), lambda i: (a_lead, i, 0))
        b_spec = pl.BlockSpec(b_shape, lambda i: (b_idx, 0))
        dims = (((1,), (0,)), ((), ())) if mode == "nn" else (((1,), (1,)), ((), ()))
        grid, o_map, row_map = (M // tm,), (lambda i: (i, 0)), (lambda i: (0, 0))
    ins, in_specs = [a, b], [a_spec, b_spec]
    if bias is not None:
        ins.append(bias)
        in_specs.append(pl.BlockSpec((1, tn), row_map))
    if residual is not None:
        ins.append(residual)
        in_specs.append(pl.BlockSpec((tm, tn), o_map))
    has_bias, has_res = bias is not None, residual is not None

    def body(a_ref, b_ref, *rest):
        rest = list(rest)
        bias_ref = rest.pop(0) if has_bias else None
        res_ref = rest.pop(0) if has_res else None
        (o_ref,) = rest
        r = lax.dot_general(a_ref[...].astype(MXU), b_ref[...].astype(MXU), dims, preferred_element_type=f32)
        if has_bias:
            r = r + bias_ref[...]
        if has_res:
            r = r + res_ref[...]
        o_ref[...] = r.astype(out_dtype)

    return pl.pallas_call(
        body, name=name, grid=grid, in_specs=in_specs,
        out_specs=pl.BlockSpec((tm, tn), o_map),
        out_shape=jax.ShapeDtypeStruct((M, N), out_dtype),
        compiler_params=_cparams(("parallel",)),
    )(*ins)


def matmul_tn_stacked(name, a, b, out_dtype):
    R, K, M = a.shape
    N = b.shape[1]
    tn = _pick(N, (256, 128))

    def body(a_ref, b_ref, o_ref):
        o_ref[0] = lax.dot_general(a_ref[0].astype(MXU), b_ref[...].astype(MXU), (((0,), (0,)), ((), ())),
                                   preferred_element_type=f32).astype(out_dtype)

    out = pl.pallas_call(
        body, name=name, grid=(R, N // tn),
        in_specs=[pl.BlockSpec((1, K, M), lambda r, j: (r, 0, 0)), pl.BlockSpec((K, tn), lambda r, j: (0, j))],
        out_specs=pl.BlockSpec((1, M, tn), lambda r, j: (r, 0, j)),
        out_shape=jax.ShapeDtypeStruct((R, M, N), out_dtype),
        compiler_params=_cparams(("parallel", "parallel")),
    )(a, b)
    return out.reshape(R * M, N)


def rowwise(name, fn, rows, pars, outs, accs=(), tile=256, comm=None):
    n_in, n_out = len(rows) + len(pars), len(outs)
    in_specs = [pl.BlockSpec((tile, r.shape[1]), lambda i: (i, 0)) for r in rows]
    in_specs += [pl.BlockSpec(p.shape, lambda i: (0, 0)) for p in pars]
    out_specs = [pl.BlockSpec((tile, c), lambda i: (i, 0)) for c, _ in outs]
    out_specs += [pl.BlockSpec(shp, lambda i: (0, 0)) for shp in accs]
    out_shape = [jax.ShapeDtypeStruct((S, c), dt) for c, dt in outs]
    out_shape += [jax.ShapeDtypeStruct(shp, f32) for shp in accs]

    def body(*refs):
        res = fn(*[r[...] for r in refs[:n_in]])
        o_refs = refs[n_in:n_in + n_out]
        a_refs = refs[n_in + n_out:]
        for ref, val in zip(o_refs, res[:n_out]):
            ref[...] = val.astype(ref.dtype)
        if a_refs:
            @pl.when(pl.program_id(0) == 0)
            def _():
                for ref in a_refs:
                    ref[...] = jnp.zeros_like(ref)
            for ref, val in zip(a_refs, res[n_out:]):
                ref[...] += val

    return _call(body, [*rows, *pars], name=name, grid=(S // tile,), in_specs=in_specs, out_specs=out_specs,
                 out_shape=out_shape, sem=("arbitrary",) if accs else ("parallel",), comm=comm)


def colwise(name, fn, cols, pars, outs, pouts=(), ct=128, comm=None):
    C = cols[0].shape[1]
    n_in, n_out = len(cols) + len(pars), len(outs)
    in_specs = [pl.BlockSpec((S, ct), lambda j: (0, j)) for _ in cols]
    in_specs += [pl.BlockSpec((p.shape[0], ct), lambda j: (0, j)) for p in pars]
    out_specs = [pl.BlockSpec((S, ct), lambda j: (0, j)) for _ in outs]
    out_specs += [pl.BlockSpec((r, ct), lambda j: (0, j)) for r in pouts]
    out_shape = [jax.ShapeDtypeStruct((S, C), dt) for dt in outs]
    out_shape += [jax.ShapeDtypeStruct((r, C), f32) for r in pouts]

    def body(*refs):
        res = fn(*[r[...] for r in refs[:n_in]])
        for ref, val in zip(refs[n_in:], res):
            ref[...] = val.astype(ref.dtype)

    return _call(body, [*cols, *pars], name=name, grid=(C // ct,), in_specs=in_specs, out_specs=out_specs,
                 out_shape=out_shape, sem=("parallel",), comm=comm)


def _sigmoid(x):
    return 1.0 / (1.0 + jnp.exp(-x))


def _silu(x):
    return x * _sigmoid(x)


def _dsilu(x):
    sg = _sigmoid(x)
    return sg * (1.0 + x * (1.0 - sg))


def _softplus(x):
    return jnp.maximum(x, 0.0) + jnp.log(1.0 + jnp.exp(-jnp.abs(x)))


def _rms_fwd(x, g):
    r = lax.rsqrt(jnp.mean(x * x, axis=-1, keepdims=True) + EPS)
    return x * r * g


def _rms_bwd(x, g, dh):
    r = lax.rsqrt(jnp.mean(x * x, axis=-1, keepdims=True) + EPS)
    xh = x * r
    dxh = dh * g
    dx = r * (dxh - xh * jnp.mean(dxh * xh, axis=-1, keepdims=True))
    return dx, jnp.sum(dh * xh, axis=0, keepdims=True)


def _shift_down(x, s):
    if s == 0:
        return x
    row = lax.broadcasted_iota(jnp.int32, x.shape, 0)
    return jnp.where(row >= s, pltpu.roll(x, s, 0), 0.0)


def _shift_up(x, s):
    if s == 0:
        return x
    n = x.shape[0]
    row = lax.broadcasted_iota(jnp.int32, x.shape, 0)
    return jnp.where(row < n - s, pltpu.roll(x, n - s, 0), 0.0)


def _conv(x, w, b):
    width = w.shape[0]
    out = b + w[width - 1:width, :] * x
    for k in range(width - 1):
        out = out + w[k:k + 1, :] * _shift_down(x, width - 1 - k)
    return out


def _conv_bwd(x, w, dc):
    width = w.shape[0]
    dx = w[width - 1:width, :] * dc
    dws = []
    for k in range(width - 1):
        s = width - 1 - k
        dx = dx + w[k:k + 1, :] * _shift_up(dc, s)
        dws.append(jnp.sum(dc * _shift_down(x, s), axis=0, keepdims=True))
    dws.append(jnp.sum(dc * x, axis=0, keepdims=True))
    return dx, jnp.concatenate(dws, axis=0), jnp.sum(dc, axis=0, keepdims=True)


def _rope_tables(pos, inv_freq):
    ang = pos * inv_freq
    return jnp.cos(ang), jnp.sin(ang)


def _headnorm_rope_fwd(x, g, cos, sin, heads):
    half = AH // 2
    outs = []
    for h in range(heads):
        seg = x[:, AH * h:AH * (h + 1)]
        n = _rms_fwd(seg, g)
        n1, n2 = n[:, :half], n[:, half:]
        outs += [n1 * cos - n2 * sin, n2 * cos + n1 * sin]
    return jnp.concatenate(outs, axis=1)


def _headnorm_rope_bwd(x, g, cos, sin, dout, heads):
    half = AH // 2
    dxs, dg = [], jnp.zeros((1, AH), f32)
    for h in range(heads):
        seg = x[:, AH * h:AH * (h + 1)]
        d = dout[:, AH * h:AH * (h + 1)]
        d1, d2 = d[:, :half], d[:, half:]
        dn = jnp.concatenate([d1 * cos + d2 * sin, d2 * cos - d1 * sin], axis=1)
        dx, dgh = _rms_bwd(seg, g, dn)
        dxs.append(dx)
        dg = dg + dgh
    return jnp.concatenate(dxs, axis=1), dg


def _ssd_prep(dt_pre, dt_bias, a_log, dt_s, acum_s, acumT_s):
    dt = _softplus(dt_pre + dt_bias)
    a = dt * (-jnp.exp(a_log))
    row = lax.broadcasted_iota(jnp.int32, (CHUNK, CHUNK), 0)
    col = lax.broadcasted_iota(jnp.int32, (CHUNK, CHUNK), 1)
    dt_s[...] = dt
    acum_s[...] = jnp.dot((col <= row).astype(f32), a, precision=lax.Precision.HIGHEST, preferred_element_type=f32)
    acumT_s[...] = lax.dot_general(a, (row <= col).astype(f32), (((0,), (0,)), ((), ())),
                                   precision=lax.Precision.HIGHEST, preferred_element_type=f32)


def _head_cols(h, dt_s, acum_s, acumT_s):
    lane = lax.broadcasted_iota(jnp.int32, (1, LANES), 1)
    oh_l = (lane == h).astype(f32)
    sub = lax.broadcasted_iota(jnp.int32, (LANES, 1), 0)
    oh_s = (sub == h).astype(f32)
    dt_h = jnp.sum(dt_s[...] * oh_l, axis=1, keepdims=True)
    ac_h = jnp.sum(acum_s[...] * oh_l, axis=1, keepdims=True)
    acr_h = jnp.sum(acumT_s[...] * oh_s, axis=0, keepdims=True)
    return oh_l, dt_h, ac_h, acr_h


def ssd_fwd(xs, Bm, Cm, dt_pre, dt_bias, a_log, d_skip, comm=None):
    def body(xs_ref, b_ref, c_ref, dtp_ref, bias_ref, alog_ref, d_ref, y_ref, st_ref, state, dt_s, acum_s, acumT_s):
        c, g = pl.program_id(0), pl.program_id(1)

        @pl.when(g == 0)
        def _():
            _ssd_prep(dtp_ref[...], bias_ref[...], alog_ref[...], dt_s, acum_s, acumT_s)

        row = lax.broadcasted_iota(jnp.int32, (CHUNK, CHUNK), 0)
        col = lax.broadcasted_iota(jnp.int32, (CHUNK, CHUNK), 1)
        causal = col <= row
        Bb, Cb = b_ref[...], c_ref[...]
        cb = lax.dot_general(Cb.astype(MXU), Bb.astype(MXU), (((1,), (1,)), ((), ())), preferred_element_type=f32)
        xs_blk = xs_ref[...]
        ys = []
        for j in range(HPG):
            h = g * HPG + j
            oh_l, dt_h, ac_h, acr_h = _head_cols(h, dt_s, acum_s, acumT_s)

            @pl.when(c == 0)
            def _():
                state[h] = jnp.zeros((NSTATE, HP), f32)

            prev = state[h]
            decay = jnp.exp(jnp.where(causal, ac_h - acr_h, -1e30))
            w = (cb * decay).astype(MXU)
            xs_h = xs_blk[:, HP * j:HP * (j + 1)]
            xd = (xs_h * dt_h).astype(MXU)
            y_diag = jnp.dot(w, xd, preferred_element_type=f32)
            y_off = jnp.dot(Cb.astype(MXU), prev.astype(MXU), preferred_element_type=f32) * jnp.exp(ac_h)
            d_h = jnp.sum(d_ref[...] * oh_l, axis=1, keepdims=True)
            ys.append(y_diag + y_off + xs_h * d_h)
            a_end = ac_h[CHUNK - 1:CHUNK, :]
            dte = jnp.exp(a_end - ac_h)
            s_c = lax.dot_general((Bb * dte).astype(MXU), xd, (((0,), (0,)), ((), ())), preferred_element_type=f32)
            st_ref[0, j] = prev
            state[h] = prev * jnp.exp(a_end) + s_c
        y_ref[...] = jnp.concatenate(ys, axis=1)

    par = pl.BlockSpec((1, LANES), lambda c, g: (0, 0))
    return _call(
        body, [xs, Bm, Cm, dt_pre, dt_bias, a_log, d_skip], comm=comm, name="ssd_fwd", grid=(NCHUNK, GROUPS),
        in_specs=[pl.BlockSpec((CHUNK, HPG * HP), lambda c, g: (c, g)),
                  pl.BlockSpec((CHUNK, NSTATE), lambda c, g: (c, g)),
                  pl.BlockSpec((CHUNK, NSTATE), lambda c, g: (c, g)),
                  pl.BlockSpec((CHUNK, LANES), lambda c, g: (c, 0)), par, par, par],
        out_specs=[pl.BlockSpec((CHUNK, HPG * HP), lambda c, g: (c, g)),
                   pl.BlockSpec((1, HPG, NSTATE, HP), lambda c, g: (c, g, 0, 0))],
        out_shape=[jax.ShapeDtypeStruct((S, INNER), f32), jax.ShapeDtypeStruct((NCHUNK, HEADS, NSTATE, HP), f32)],
        scratch=[pltpu.VMEM((HEADS, NSTATE, HP), f32), pltpu.VMEM((CHUNK, LANES), f32),
                 pltpu.VMEM((CHUNK, LANES), f32), pltpu.VMEM((LANES, CHUNK), f32)],
        sem=("arbitrary", "arbitrary"))


def ssd_bwd(xs, Bm, Cm, dt_pre, dt_bias, a_log, d_skip, states, dy, comm=None):
    rev = lambda c: NCHUNK - 1 - c

    def body(xs_ref, b_ref, c_ref, dtp_ref, bias_ref, alog_ref, d_ref, st_ref, dy_ref,
             dxs_ref, db_ref, dc_ref, ddt_ref, dbias_ref, dalog_ref, dd_ref,
             dstate, dt_s, acum_s, acumT_s, dacum_s, ddt_s, da_s):
        c, g = pl.program_id(0), pl.program_id(1)

        @pl.when(g == 0)
        def _():
            _ssd_prep(dtp_ref[...], bias_ref[...], alog_ref[...], dt_s, acum_s, acumT_s)
            dacum_s[...] = jnp.zeros_like(dacum_s)
            ddt_s[...] = jnp.zeros_like(ddt_s)

        @pl.when((c == 0) & (g == 0))
        def _():
            da_s[...] = jnp.zeros_like(da_s)
            dd_ref[...] = jnp.zeros_like(dd_ref)
            dbias_ref[...] = jnp.zeros_like(dbias_ref)
            dalog_ref[...] = jnp.zeros_like(dalog_ref)

        row = lax.broadcasted_iota(jnp.int32, (CHUNK, CHUNK), 0)
        col = lax.broadcasted_iota(jnp.int32, (CHUNK, CHUNK), 1)
        sub_l = lax.broadcasted_iota(jnp.int32, (CHUNK, 1), 0)
        last = (sub_l == CHUNK - 1).astype(f32)
        nt = (((1,), (1,)), ((), ()))
        tn = (((0,), (0,)), ((), ()))
        Bb, Cb = b_ref[...], c_ref[...]
        Bm_, Cm_ = Bb.astype(MXU), Cb.astype(MXU)
        cb = lax.dot_general(Cm_, Bm_, nt, preferred_element_type=f32)
        bc = lax.dot_general(Bm_, Cm_, nt, preferred_element_type=f32)
        xs_blk, dy_blk = xs_ref[...], dy_ref[...]
        dxs, dB, dC = [], jnp.zeros((CHUNK, NSTATE), f32), jnp.zeros((CHUNK, NSTATE), f32)
        for j in range(HPG):
            h = g * HPG + j
            oh_l, dt_h, ac_h, acr_h = _head_cols(h, dt_s, acum_s, acumT_s)

            @pl.when(c == 0)
            def _():
                dstate[h] = jnp.zeros((NSTATE, HP), f32)

            dnext = dstate[h]
            prev = st_ref[0, j]
            lm = jnp.exp(jnp.where(col <= row, ac_h - acr_h, -1e30))
            lmT = jnp.exp(jnp.where(row <= col, acr_h - ac_h, -1e30))
            xs_h = xs_blk[:, HP * j:HP * (j + 1)]
            dy_h = dy_blk[:, HP * j:HP * (j + 1)]
            xd = xs_h * dt_h
            xdm, dym = xd.astype(MXU), dy_h.astype(MXU)
            ea = jnp.exp(ac_h)
            a_end = ac_h[CHUNK - 1:CHUNK, :]
            e_end = jnp.exp(a_end)
            dte = jnp.exp(a_end - ac_h)
            dnm, pvm = dnext.astype(MXU), prev.astype(MXU)
            bd = jnp.dot(Bm_, dnm, preferred_element_type=f32)
            dxd = jnp.dot((bc * lmT).astype(MXU), dym, preferred_element_type=f32) + dte * bd
            dw = lax.dot_general(dym, xdm, nt, preferred_element_type=f32)
            dwT = lax.dot_general(xdm, dym, nt, preferred_element_type=f32)
            dcb = dw * lm
            dbc = dwT * lmT
            eady = (ea * dy_h).astype(MXU)
            dC = dC + jnp.dot(dcb.astype(MXU), Bm_, preferred_element_type=f32) \
                + lax.dot_general(eady, pvm, nt, preferred_element_type=f32)
            dB = dB + jnp.dot(dbc.astype(MXU), Cm_, preferred_element_type=f32) \
                + dte * lax.dot_general(xdm, dnm, nt, preferred_element_type=f32)
            dstate[h] = lax.dot_general(Cm_, eady, tn, preferred_element_type=f32) + e_end * dnext
            r1 = jnp.sum(dcb * cb, axis=1, keepdims=True)
            r2 = jnp.sum(dbc * bc, axis=1, keepdims=True)
            y_off = jnp.dot(Cm_, pvm, preferred_element_type=f32) * ea
            t3 = jnp.sum(dy_h * y_off, axis=1, keepdims=True)
            t4 = jnp.sum(bd * xd, axis=1, keepdims=True) * dte
            end_extra = jnp.sum(t4, axis=0, keepdims=True) + e_end * jnp.sum(jnp.sum(prev * dnext, axis=1, keepdims=True), axis=0, keepdims=True)
            dacum_h = r1 - r2 + t3 - t4 + last * end_extra
            dacum_s[...] += dacum_h * oh_l
            ddt_s[...] += jnp.sum(dxd * xs_h, axis=1, keepdims=True) * oh_l
            d_h = jnp.sum(d_ref[...] * oh_l, axis=1, keepdims=True)
            dxs.append(dxd * dt_h + dy_h * d_h)
            dd_ref[...] += oh_l * jnp.sum(jnp.sum(dy_h * xs_h, axis=1, keepdims=True), axis=0, keepdims=True)
        dxs_ref[...] = jnp.concatenate(dxs, axis=1)
        db_ref[...] = dB
        dc_ref[...] = dC

        @pl.when(g == GROUPS - 1)
        def _():
            a_row = -jnp.exp(alog_ref[...])
            da = jnp.dot((row <= col).astype(f32), dacum_s[...], precision=lax.Precision.HIGHEST, preferred_element_type=f32)
            da_s[...] += jnp.sum(da * dt_s[...], axis=0, keepdims=True)
            z = dtp_ref[...] + bias_ref[...]
            ddt_pre = (ddt_s[...] + da * a_row) * _sigmoid(z)
            ddt_ref[...] = ddt_pre.astype(ddt_ref.dtype)
            dbias_ref[...] += jnp.sum(ddt_pre, axis=0, keepdims=True)

            @pl.when(c == NCHUNK - 1)
            def _():
                dalog_ref[...] = da_s[...] * a_row

    par = pl.BlockSpec((1, LANES), lambda c, g: (0, 0))
    return _call(
        body, [xs, Bm, Cm, dt_pre, dt_bias, a_log, d_skip, states, dy], comm=comm, name="ssd_bwd", grid=(NCHUNK, GROUPS),
        in_specs=[pl.BlockSpec((CHUNK, HPG * HP), lambda c, g: (rev(c), g)),
                  pl.BlockSpec((CHUNK, NSTATE), lambda c, g: (rev(c), g)),
                  pl.BlockSpec((CHUNK, NSTATE), lambda c, g: (rev(c), g)),
                  pl.BlockSpec((CHUNK, LANES), lambda c, g: (rev(c), 0)), par, par, par,
                  pl.BlockSpec((1, HPG, NSTATE, HP), lambda c, g: (rev(c), g, 0, 0)),
                  pl.BlockSpec((CHUNK, HPG * HP), lambda c, g: (rev(c), g))],
        out_specs=[pl.BlockSpec((CHUNK, HPG * HP), lambda c, g: (rev(c), g)),
                   pl.BlockSpec((CHUNK, NSTATE), lambda c, g: (rev(c), g)),
                   pl.BlockSpec((CHUNK, NSTATE), lambda c, g: (rev(c), g)),
                   pl.BlockSpec((CHUNK, LANES), lambda c, g: (rev(c), 0)), par, par, par],
        out_shape=[jax.ShapeDtypeStruct((S, INNER), f32), jax.ShapeDtypeStruct((S, GROUPS * NSTATE), f32),
                   jax.ShapeDtypeStruct((S, GROUPS * NSTATE), f32), jax.ShapeDtypeStruct((S, LANES), MXU),
                   jax.ShapeDtypeStruct((1, LANES), f32), jax.ShapeDtypeStruct((1, LANES), f32),
                   jax.ShapeDtypeStruct((1, LANES), f32)],
        scratch=[pltpu.VMEM((HEADS, NSTATE, HP), f32), pltpu.VMEM((CHUNK, LANES), f32),
                 pltpu.VMEM((CHUNK, LANES), f32), pltpu.VMEM((LANES, CHUNK), f32),
                 pltpu.VMEM((CHUNK, LANES), f32), pltpu.VMEM((CHUNK, LANES), f32), pltpu.VMEM((1, LANES), f32)],
        sem=("arbitrary", "arbitrary"))


def _attn_block(n, q, kp, kc, vp, vc, sinks, hk):
    rows = QPK * WIN
    qi = lax.broadcasted_iota(jnp.int32, (rows, 2 * WIN), 0) & (WIN - 1)
    ki = lax.broadcasted_iota(jnp.int32, (rows, 2 * WIN), 1)
    rel = qi + WIN - ki
    mask = (rel >= 0) & (rel < WIN) & ((ki >= WIN) | (n > 0))
    sl = slice(AH * hk, AH * (hk + 1))
    kb = jnp.concatenate([kp[:, sl], kc[:, sl]], axis=0).astype(MXU)
    vb = jnp.concatenate([vp[:, sl], vc[:, sl]], axis=0).astype(MXU)
    qg = jnp.concatenate([q[:, AH * (QPK * hk + g):AH * (QPK * hk + g + 1)] for g in range(QPK)], axis=0).astype(MXU)
    s = lax.dot_general(qg, kb, (((1,), (1,)), ((), ())), preferred_element_type=f32) * (AH ** -0.5)
    s = jnp.where(mask, s, -1e30)
    sink = jnp.concatenate([jnp.broadcast_to(sinks[:, QPK * hk + g:QPK * hk + g + 1], (WIN, 1)) for g in range(QPK)], axis=0)
    m = jnp.maximum(jnp.max(s, axis=1, keepdims=True), sink)
    p = jnp.exp(s - m)
    ps = jnp.exp(sink - m)
    den = jnp.sum(p, axis=1, keepdims=True) + ps
    return qg, kb, vb, p / den, ps / den


def _kv_specs():
    prev = lambda n: (jnp.maximum(n - 1, 0), 0)
    cur = lambda n: (n, 0)
    w = KVH * AH
    return [pl.BlockSpec((WIN, w), prev), pl.BlockSpec((WIN, w), cur), pl.BlockSpec((WIN, w), prev), pl.BlockSpec((WIN, w), cur)]


def attn_fwd(q, k, v, sinks, comm=None):
    def body(q_ref, kp_ref, kc_ref, vp_ref, vc_ref, s_ref, o_ref):
        n = pl.program_id(0)
        q_, kp, kc, vp, vc, sk = q_ref[...], kp_ref[...], kc_ref[...], vp_ref[...], vc_ref[...], s_ref[...]
        outs = []
        for hk in range(KVH):
            _, _, vb, pr, _ = _attn_block(n, q_, kp, kc, vp, vc, sk, hk)
            o = jnp.dot(pr.astype(MXU), vb, preferred_element_type=f32)
            outs += [o[WIN * g:WIN * (g + 1)] for g in range(QPK)]
        o_ref[...] = jnp.concatenate(outs, axis=1)

    return _call(
        body, [q, k, k, v, v, sinks], comm=comm, name="attn_fwd", grid=(NBLK,),
        in_specs=[pl.BlockSpec((WIN, D), lambda n: (n, 0))] + _kv_specs() + [pl.BlockSpec((1, QH), lambda n: (0, 0))],
        out_specs=[pl.BlockSpec((WIN, D), lambda n: (n, 0))],
        out_shape=[jax.ShapeDtypeStruct((S, D), f32)], sem=("parallel",))[0]


def attn_bwd(q, k, v, sinks, dout, comm=None):
    def body(q_ref, kp_ref, kc_ref, vp_ref, vc_ref, s_ref, do_ref, dq_ref, dkp_ref, dkc_ref, dvp_ref, dvc_ref, ds_ref):
        n = pl.program_id(0)

        @pl.when(n == 0)
        def _():
            ds_ref[...] = jnp.zeros_like(ds_ref)

        q_, kp, kc, vp, vc, sk, do = q_ref[...], kp_ref[...], kc_ref[...], vp_ref[...], vc_ref[...], s_ref[...], do_ref[...]
        lane = lax.broadcasted_iota(jnp.int32, (1, QH), 1)
        nt = (((1,), (1,)), ((), ()))
        tn = (((0,), (0,)), ((), ()))
        dqs, dkps, dkcs, dvps, dvcs = [], [], [], [], []
        dsink = jnp.zeros((1, QH), f32)
        for hk in range(KVH):
            qg, kb, vb, pr, prs = _attn_block(n, q_, kp, kc, vp, vc, sk, hk)
            dog = jnp.concatenate([do[:, AH * (QPK * hk + g):AH * (QPK * hk + g + 1)] for g in range(QPK)], axis=0).astype(MXU)
            dp = lax.dot_general(dog, vb, nt, preferred_element_type=f32)
            dvb = lax.dot_general(pr.astype(MXU), dog, tn, preferred_element_type=f32)
            delta = jnp.sum(pr * dp, axis=1, keepdims=True)
            ds = (pr * (dp - delta)).astype(MXU)
            dsk = -prs * delta
            for g in range(QPK):
                dsink = dsink + jnp.sum(dsk[WIN * g:WIN * (g + 1)], axis=0, keepdims=True) * (lane == QPK * hk + g).astype(f32)
            dqg = jnp.dot(ds, kb, preferred_element_type=f32) * (AH ** -0.5)
            dkb = lax.dot_general(ds, qg, tn, preferred_element_type=f32) * (AH ** -0.5)
            dqs += [dqg[WIN * g:WIN * (g + 1)] for g in range(QPK)]
            dkps.append(dkb[:WIN])
            dkcs.append(dkb[WIN:])
            dvps.append(dvb[:WIN])
            dvcs.append(dvb[WIN:])
        dq_ref[...] = jnp.concatenate(dqs, axis=1)
        dkp_ref[...] = jnp.concatenate(dkps, axis=1)
        dkc_ref[...] = jnp.concatenate(dkcs, axis=1)
        dvp_ref[...] = jnp.concatenate(dvps, axis=1)
        dvc_ref[...] = jnp.concatenate(dvcs, axis=1)
        ds_ref[...] += dsink

    w = KVH * AH
    blk = lambda width: pl.BlockSpec((WIN, width), lambda n: (n, 0))
    return _call(
        body, [q, k, k, v, v, sinks, dout], comm=comm, name="attn_bwd", grid=(NBLK,),
        in_specs=[blk(D)] + _kv_specs() + [pl.BlockSpec((1, QH), lambda n: (0, 0)), blk(D)],
        out_specs=[blk(D), blk(w), blk(w), blk(w), blk(w), pl.BlockSpec((1, QH), lambda n: (0, 0))],
        out_shape=[jax.ShapeDtypeStruct((S, D), f32)] + [jax.ShapeDtypeStruct((S, w), f32)] * 4 + [jax.ShapeDtypeStruct((1, QH), f32)],
        sem=("arbitrary",))


def kv_bwd(kv, pos, inv_freq, k_norm, dkp, dkc, dvp, dvc):
    w = KVH * AH

    def body(kv_ref, pos_ref, if_ref, g_ref, dkp_ref, dkc_ref, dvp_ref, dvc_ref, o_ref, dg_ref, db_ref):
        n = pl.program_id(0)

        @pl.when(n == 0)
        def _():
            dg_ref[...] = jnp.zeros_like(dg_ref)
            db_ref[...] = jnp.zeros_like(db_ref)

        inside = (n < NBLK - 1).astype(f32)
        dk = dkc_ref[...] + inside * dkp_ref[...]
        dv = dvc_ref[...] + inside * dvp_ref[...]
        cos, sin = _rope_tables(pos_ref[...], if_ref[...])
        dkpre, dg = _headnorm_rope_bwd(kv_ref[...], g_ref[...], cos, sin, dk, KVH)
        dkv = jnp.concatenate([dkpre, dv], axis=1)
        o_ref[...] = dkv.astype(o_ref.dtype)
        dg_ref[...] += dg
        db_ref[...] += jnp.sum(dkv, axis=0, keepdims=True)

    nxt = lambda n: (jnp.minimum(n + 1, NBLK - 1), 0)
    cur = lambda n: (n, 0)
    const = lambda n: (0, 0)
    return pl.pallas_call(
        body, name="kv_bwd", grid=(NBLK,),
        in_specs=[pl.BlockSpec((WIN, w), cur), pl.BlockSpec((WIN, 1), cur), pl.BlockSpec((1, AH // 2), const),
                  pl.BlockSpec((1, AH), const), pl.BlockSpec((WIN, w), nxt), pl.BlockSpec((WIN, w), cur),
                  pl.BlockSpec((WIN, w), nxt), pl.BlockSpec((WIN, w), cur)],
        out_specs=[pl.BlockSpec((WIN, 2 * w), cur), pl.BlockSpec((1, AH), const), pl.BlockSpec((1, 2 * w), const)],
        out_shape=[jax.ShapeDtypeStruct((S, 2 * w), MXU), jax.ShapeDtypeStruct((1, AH), f32), jax.ShapeDtypeStruct((1, 2 * w), f32)],
        compiler_params=_cparams(("arbitrary",)),
    )(kv, pos, inv_freq, k_norm, dkp, dkc, dvp, dvc)


def adamw(name, w, g, m, v):
    R, C = w.shape
    tr = _pick(R, (256, 128, 64, 32, 16, 8))
    tc = C if tr < R or C % 256 else 256

    def body(w_ref, g_ref, m_ref, v_ref, d_ref, nm_ref, nv_ref):
        g_ = g_ref[...]
        m_ = ADAM_B1 * m_ref[...] + (1.0 - ADAM_B1) * g_
        v_ = ADAM_B2 * v_ref[...] + (1.0 - ADAM_B2) * (g_ * g_)
        m_hat = m_ / (1.0 - ADAM_B1 ** ADAM_STEP)
        v_hat = v_ / (1.0 - ADAM_B2 ** ADAM_STEP)
        d_ref[...] = -ADAM_LR * (m_hat / (jnp.sqrt(v_hat) + ADAM_EPS) + ADAM_WD * w_ref[...])
        nm_ref[...] = m_
        nv_ref[...] = v_

    spec = pl.BlockSpec((tr, tc), lambda i, j: (i, j))
    return pl.pallas_call(
        body, name=name, grid=(R // tr, C // tc), in_specs=[spec] * 4, out_specs=[spec] * 3,
        out_shape=[jax.ShapeDtypeStruct((R, C), f32)] * 3, compiler_params=_cparams(("parallel", "parallel")),
    )(w, g, m, v)


def _me():
    return lax.axis_index("x"), lax.axis_index("y"), lax.axis_index("c")


def gather_comm(xs):
    n = len(xs)

    def parts(x_refs, o_refs, sems):
        send_sems, recv_sems, local_sems = sems
        x, y, c = _me()
        me, sibling = (x, y, c), (x, y, 1 - c)
        chips = [(1 - x, y), (x, 1 - y), (1 - x, 1 - y)]

        def copy(a, k, block, to, src=None):
            dst = o_refs[a].at[4 * block[0] + 2 * block[1] + block[2]]
            return pltpu.make_async_remote_copy(
                src_ref=dst if src is None else src, dst_ref=dst,
                send_sem=send_sems.at[7 * a + k], recv_sem=recv_sems.at[7 * a + k], device_id=to, device_id_type=MESH)

        mine = [pltpu.make_async_copy(x_refs[a], o_refs[a].at[4 * x + 2 * y + c], local_sems.at[a]) for a in range(n)]
        first = []
        for a in range(n):
            first.append(copy(a, 0, me, sibling, src=x_refs[a]))
            first += [copy(a, 1 + j, me, (*chip, c), src=x_refs[a]) for j, chip in enumerate(chips)]
        return copy, mine, first, me, sibling, chips, c

    def start(x_refs, o_refs, sems):
        _, mine, first, *_ = parts(x_refs, o_refs, sems)
        for cp in mine + first:
            cp.start()

    def finish(x_refs, o_refs, sems):
        copy, mine, first, me, sibling, chips, c = parts(x_refs, o_refs, sems)
        passed = []
        for j, chip in enumerate(chips):
            for a in range(n):
                copy(a, 1 + j, (*chip, c), me).wait_recv()
                cp = copy(a, 4 + j, (*chip, c), sibling)
                cp.start()
                passed.append(cp)
        for a in range(n):
            copy(a, 0, sibling, me).wait_recv()
            for j, chip in enumerate(chips):
                copy(a, 4 + j, (*chip, 1 - c), me).wait_recv()
        for cp in first + passed:
            cp.wait_send()
        for cp in mine:
            cp.wait()

    return Comm(xs, [jax.ShapeDtypeStruct((N_DEV,) + a.shape, a.dtype) for a in xs],
                [pltpu.SemaphoreType.DMA((7 * n,)), pltpu.SemaphoreType.DMA((7 * n,)), pltpu.SemaphoreType.DMA((n,))], start, finish)


def run_comm(name, comm):
    _call(lambda: None, [], name=name, grid=(1,), in_specs=[], out_specs=[], out_shape=[], comm=comm)
    return comm.results


def sibling_comm(gs):
    n = len(gs)

    def copies(g_refs, o_refs, sems):
        x, y, c = _me()
        return [pltpu.make_async_remote_copy(
            src_ref=g_refs[a].at[:, 1 - c], dst_ref=o_refs[a], send_sem=sems[0].at[a], recv_sem=sems[1].at[a],
            device_id=(x, y, 1 - c), device_id_type=MESH) for a in range(n)]

    def start(g_refs, o_refs, sems):
        for cp in copies(g_refs, o_refs, sems):
            cp.start()

    def finish(g_refs, o_refs, sems):
        for cp in copies(g_refs, o_refs, sems):
            cp.wait()

    return Comm(gs, [jax.ShapeDtypeStruct((4,) + g.shape[2:], g.dtype) for g in gs],
                [pltpu.SemaphoreType.DMA((n,)), pltpu.SemaphoreType.DMA((n,))], start, finish)


def chip_comm(ts):
    n = len(ts)

    def copies(t_refs, o_refs, sems):
        x, y, c = _me()
        chips = [(1 - x, y), (x, 1 - y), (1 - x, 1 - y)]
        return [pltpu.make_async_remote_copy(
            src_ref=t_refs[a].at[2 * px + py], dst_ref=o_refs[a].at[j],
            send_sem=sems[0].at[3 * a + j], recv_sem=sems[1].at[3 * a + j],
            device_id=(px, py, c), device_id_type=MESH) for j, (px, py) in enumerate(chips) for a in range(n)]

    def start(t_refs, o_refs, sems):
        for cp in copies(t_refs, o_refs, sems):
            cp.start()

    def finish(t_refs, o_refs, sems):
        for cp in copies(t_refs, o_refs, sems):
            cp.wait()

    return Comm(ts, [jax.ShapeDtypeStruct((3,) + t.shape[1:], t.dtype) for t in ts],
                [pltpu.SemaphoreType.DMA((3 * n,)), pltpu.SemaphoreType.DMA((3 * n,))], start, finish)


def _row_tile(rows):
    return _pick(rows, (512, 304, 256, 128))


def pair_add(name, g, r):
    _, _, R, C = g.shape
    tr = _row_tile(R)

    def body(c_ref, g_ref, r_ref, o_ref):
        o_ref[0] = (g_ref[0, 0].astype(f32) + r_ref[0].astype(f32)).astype(o_ref.dtype)

    return pl.pallas_call(
        body, name=name,
        grid_spec=pltpu.PrefetchScalarGridSpec(
            num_scalar_prefetch=1, grid=(4, R // tr),
            in_specs=[pl.BlockSpec((1, 1, tr, C), lambda p, i, c: (p, c[0], i, 0)),
                      pl.BlockSpec((1, tr, C), lambda p, i, c: (p, i, 0))],
            out_specs=pl.BlockSpec((1, tr, C), lambda p, i, c: (p, i, 0))),
        out_shape=jax.ShapeDtypeStruct((4, R, C), g.dtype),
        compiler_params=_cparams(("parallel", "parallel")),
    )(lax.axis_index("c").reshape(1).astype(jnp.int32), g, r)


def final_add(name, t, r):
    _, R, C = t.shape
    tr = _row_tile(R)

    def body(p_ref, t_ref, r_ref, o_ref):
        o_ref[...] = ((t_ref[0].astype(f32) + r_ref[0].astype(f32)) + r_ref[1].astype(f32)) + r_ref[2].astype(f32)

    chip = 2 * lax.axis_index("x") + lax.axis_index("y")
    return pl.pallas_call(
        body, name=name,
        grid_spec=pltpu.PrefetchScalarGridSpec(
            num_scalar_prefetch=1, grid=(R // tr,),
            in_specs=[pl.BlockSpec((1, tr, C), lambda i, p: (p[0], i, 0)),
                      pl.BlockSpec((3, tr, C), lambda i, p: (0, i, 0))],
            out_specs=pl.BlockSpec((tr, C), lambda i, p: (i, 0))),
        out_shape=jax.ShapeDtypeStruct((R, C), f32),
        compiler_params=_cparams(("parallel",)),
    )(chip.reshape(1).astype(jnp.int32), t, r)


class ReduceScatter:
    def __init__(self, tag, keys, grads):
        self.tag, self.keys, self.grads = tag, keys, grads
        self.send = [g.reshape((4, 2, g.shape[0] // N_DEV) + g.shape[1:]) for g in grads]

    def sibling(self):
        self.c1 = sibling_comm(self.send)
        return self.c1

    def chips(self):
        self.pairs = [pair_add(f"rs_pair_add_{self.tag}{i}", g, r) for i, (g, r) in enumerate(zip(self.send, self.c1.results))]
        self.c2 = chip_comm(self.pairs)
        return self.c2

    def sums(self):
        return {k: final_add(f"rs_final_add_{k}", t, r) for k, t, r in zip(self.keys, self.pairs, self.c2.results)}


IN_ROWS = {"z": (0, INNER), "xs": (1, INNER), "B": (4, GROUPS * NSTATE), "C": (5, GROUPS * NSTATE)}
IN_COLS = 2 * INNER + 2 * GROUPS * NSTATE + HEADS


def sum_devices(g):
    def body(g_ref, o_ref):
        acc = g_ref[0]
        for i in range(1, N_DEV):
            acc = acc + g_ref[i]
        o_ref[...] = acc

    return pl.pallas_call(body, name="sum_devices", out_shape=jax.ShapeDtypeStruct(g.shape[1:], f32),
                          compiler_params=_cparams())(g)


def _pack(parts, unit, dtype, lead=()):
    flat = jnp.concatenate([p.reshape(lead + (-1,)).astype(dtype) for p in parts], axis=-1)
    n = flat.shape[-1]
    rows = -(-n // (unit * PACK_W)) * unit
    flat = jnp.pad(flat, [(0, 0)] * len(lead) + [(0, rows * PACK_W - n)])
    return flat.reshape(lead + (rows, PACK_W))


def _unpack(buf, shapes, lead=()):
    flat = buf.reshape(lead + (-1,))
    out, off = [], 0
    for shp in shapes:
        n = math.prod(shp)
        out.append(flat[..., off:off + n].reshape(lead + tuple(shp)))
        off += n
    return out


def _pad_lanes(a):
    return jnp.pad(a, [(0, 0)] * (a.ndim - 1) + [(0, LANES - a.shape[-1])])


def _ffn_fwd(tag, x, norm_g, w_inT, conv_w, conv_b, w_down, mid_comm=None):
    (h,) = rowwise(f"{tag}_norm", lambda x_, g_: (_rms_fwd(x_, g_),), [x], [norm_g], [(D, MXU)])
    gate_pre = matmul(f"{tag}_gate", h, w_inT, "nt", b_rows=(0, FFN))
    val = matmul(f"{tag}_val", h, w_inT, "nt", b_rows=(1, FFN))
    (act,) = colwise(f"{tag}_mid", lambda gp, v_, w_, b_: (_silu(_conv(gp, w_, b_)) * v_,), [gate_pre, val], [conv_w, conv_b], [MXU],
                     comm=mid_comm)
    out = matmul(f"{tag}_down", act, w_down, "nn", residual=x)
    return out, (x, h, gate_pre, val, act)


def _ffn_mid_bwd(name, gate_pre, val, dact, conv_w, conv_b, comm=None):
    ct = 128

    def body(gp_ref, v_ref, da_ref, w_ref, b_ref, dgv_ref, dw_ref, db_ref):
        gp, v_, da, w_ = gp_ref[...], v_ref[...], da_ref[...], w_ref[...]
        gate = _conv(gp, w_, b_ref[...])
        dgp, dw, db = _conv_bwd(gp, w_, da * v_ * _dsilu(gate))
        dgv_ref[0] = dgp.astype(dgv_ref.dtype)
        dgv_ref[1] = (da * _silu(gate)).astype(dgv_ref.dtype)
        dw_ref[...] = dw
        db_ref[...] = db

    col = pl.BlockSpec((S, ct), lambda j: (0, j))
    return _call(
        body, [gate_pre, val, dact, conv_w, conv_b], comm=comm, name=name, grid=(FFN // ct,),
        in_specs=[col, col, col, pl.BlockSpec((CONV_F, ct), lambda j: (0, j)), pl.BlockSpec((1, ct), lambda j: (0, j))],
        out_specs=[pl.BlockSpec((2, S, ct), lambda j: (0, 0, j)), pl.BlockSpec((CONV_F, ct), lambda j: (0, j)),
                   pl.BlockSpec((1, ct), lambda j: (0, j))],
        out_shape=[jax.ShapeDtypeStruct((2, S, FFN), MXU), jax.ShapeDtypeStruct((CONV_F, FFN), f32), jax.ShapeDtypeStruct((1, FFN), f32)],
        sem=("parallel",))


def _ffn_bwd(tag, layer, saved, norm_g, w_inT, conv_w, conv_b, w_down, dout, mid_comm=None):
    x, h, gate_pre, val, act = saved
    dact = matmul(f"{tag}_ddown", dout, w_down, "nt")
    g_down = matmul(f"{tag}_wdown", act, dout, "tn", out_dtype=MXU)
    dgv, g_cw, g_cb = _ffn_mid_bwd(f"{tag}_dmid", gate_pre, val, dact, conv_w, conv_b, comm=mid_comm)
    dh = matmul(f"{tag}_dgate", dgv, w_inT, "nn", b_rows=(0, FFN), a_lead=0)
    dh = matmul(f"{tag}_dval", dgv, w_inT, "nn", b_rows=(1, FFN), a_lead=1, residual=dh)
    g_inT = matmul_tn_stacked(f"{tag}_win", dgv, h, MXU)

    def norm_bwd(x_, dh_, do_, g_):
        dx, dg = _rms_bwd(x_, g_, dh_)
        return do_ + dx, dg

    rs = ReduceScatter(tag, (f"f_inT{layer}", f"f_down{layer}"), [g_inT, g_down])
    dx, g_norm = rowwise(f"{tag}_dnorm", norm_bwd, [x, dh, dout], [norm_g], [(D, f32)], [(1, D)], comm=rs.sibling())
    return dx, {f"f_norm{layer}": g_norm, f"f_conv_w{layer}": g_cw, f"f_conv_b{layer}": g_cb}, rs


def _land(W, keys, comm):
    for k, g in zip(keys, comm.results):
        W[k] = g.reshape(-1, g.shape[2])


def _local_step(x, pos, tgt, W, shards):
    G = {}
    gather = lambda *keys: gather_comm([shards[k] for k in keys])
    inv_freq = (ROPE_THETA ** (-jnp.arange(AH // 2, dtype=f32) / (AH // 2))).reshape(1, AH // 2)

    (h0,) = rowwise("a_norm", lambda x_, g_: (_rms_fwd(x_, g_),), [x], [W["a_norm"]], [(D, MXU)])
    z = matmul("a_in_z", h0, W["inT"], "nt", b_rows=IN_ROWS["z"])
    pre = {k: matmul(f"a_in_{k}", h0, W["inT"], "nt", b_rows=IN_ROWS[k]) for k in ("xs", "B", "C")}
    dt_pre = matmul("a_in_dt", h0, W["in_dtT"], "nt")
    conv = {}
    for k in ("xs", "B", "C"):
        (conv[k],) = colwise(f"a_conv_{k}", lambda p_, w_, b_: (_silu(_conv(p_, w_, b_)),), [pre[k]], [W[f"cw_{k}"], W[f"cb_{k}"]], [f32])
    c = gather("a_out", "f_inT0", "f_down0")
    y, states = ssd_fwd(conv["xs"], conv["B"], conv["C"], dt_pre, W["dt_bias"], W["A_log"], W["D"], comm=c)
    _land(W, ("a_out", "f_inT0", "f_down0"), c)

    def gate_norm(y_, z_, g_):
        yg = y_ * _silu(z_)
        w = INNER // GROUPS
        return (jnp.concatenate([_rms_fwd(yg[:, w * i:w * (i + 1)], g_[:, w * i:w * (i + 1)]) for i in range(GROUPS)], axis=1),)

    (gn,) = rowwise("a_gnorm", gate_norm, [y, z], [W["a_gnorm"]], [(INNER, MXU)], tile=128)
    x1 = matmul("a_out", gn, W["a_out"], "nn", residual=x)

    c = gather("w_kv", "w_q", "w_o")
    x2, ffn0 = _ffn_fwd("f0", x1, W["f_norm0"], W["f_inT0"], W["f_cw0"], W["f_cb0"], W["f_down0"], mid_comm=c)
    _land(W, ("w_kv", "w_q", "w_o"), c)

    (kvn,) = rowwise("kv_norm", lambda x_, g_: (_rms_fwd(x_, g_),), [x2], [W["kv_norm"]], [(D, MXU)])
    kv = matmul("kv_proj", kvn, W["w_kv"], "nn", bias=W["b_kv"])
    kw = KVH * AH

    def k_fwd(kv_, pos_, if_, g_):
        cos, sin = _rope_tables(pos_, if_)
        return _headnorm_rope_fwd(kv_[:, :kw], g_, cos, sin, KVH), kv_[:, kw:]

    k_rot, v_val = rowwise("k_rope", k_fwd, [kv, pos], [inv_freq, W["k_norm"]], [(kw, f32), (kw, f32)])
    (h2,) = rowwise("b_norm", lambda x_, g_: (_rms_fwd(x_, g_),), [x2], [W["b_norm"]], [(D, MXU)])
    q_pre = matmul("q_proj", h2, W["w_q"], "nn", bias=W["b_q"])

    def q_fwd(q_, pos_, if_, g_):
        cos, sin = _rope_tables(pos_, if_)
        return (_headnorm_rope_fwd(q_, g_, cos, sin, QH),)

    c = gather("f_down1")
    (q,) = rowwise("q_rope", q_fwd, [q_pre, pos], [inv_freq, W["q_norm"]], [(D, f32)], comm=c)
    _land(W, ("f_down1",), c)
    c = gather("f_inT1")
    att = attn_fwd(q, k_rot, v_val, W["sinks"], comm=c)
    _land(W, ("f_inT1",), c)
    x3 = matmul("o_proj", att, W["w_o"], "nn", bias=W["b_o"], residual=x2)

    x4, ffn1 = _ffn_fwd("f1", x3, W["f_norm1"], W["f_inT1"], W["f_cw1"], W["f_cb1"], W["f_down1"])

    def loss_fn(y_, t_):
        diff = y_ - t_
        rows = jnp.sum(diff * diff, axis=1, keepdims=True) * (0.5 / D)
        return diff * (1.0 / D), jnp.sum(rows, axis=0, keepdims=True)

    dx4, loss = rowwise("loss", loss_fn, [x4, tgt], [], [(D, f32)], [(1, 1)])

    dx3, g, rs_f1 = _ffn_bwd("f1", 1, ffn1, W["f_norm1"], W["f_inT1"], W["f_cw1"], W["f_cb1"], W["f_down1"], dx4)
    G.update(g)

    datt = matmul("o_dproj", dx3, W["w_o"], "nt")
    g_wo = matmul("o_wproj", att, dx3, "tn", out_dtype=MXU)
    dq, dkp, dkc, dvp, dvc, G["sinks"] = attn_bwd(q, k_rot, v_val, W["sinks"], datt, comm=rs_f1.chips())

    def q_bwd(q_, pos_, dq_, dx_, if_, g_):
        cos, sin = _rope_tables(pos_, if_)
        dqp, dg = _headnorm_rope_bwd(q_, g_, cos, sin, dq_, QH)
        return dqp, dg, jnp.sum(dqp, axis=0, keepdims=True), jnp.sum(dx_, axis=0, keepdims=True)

    dq_pre, G["q_norm"], G["b_q"], G["b_o"] = rowwise("q_drope", q_bwd, [q_pre, pos, dq, dx3], [inv_freq, W["q_norm"]],
                                                      [(D, MXU)], [(1, AH), (1, D), (1, D)])
    dh2 = matmul("q_dproj", dq_pre, W["w_q"], "nt")
    g_wq = matmul("q_wproj", h2, dq_pre, "tn", out_dtype=MXU)
    dkv, G["k_norm"], G["b_kv"] = kv_bwd(kv, pos, inv_freq, W["k_norm"], dkp, dkc, dvp, dvc)
    dkvn = matmul("kv_dproj", dkv, W["w_kv"], "nt")
    g_wkv = matmul("kv_wproj", kvn, dkv, "tn", out_dtype=MXU)
    rs_att = ReduceScatter("att", ("w_kv", "w_q", "w_o"), [g_wkv, g_wq, g_wo])

    def x2_bwd(x_, dh2_, dkvn_, dx_, gb_, gk_):
        d1, dgb = _rms_bwd(x_, gb_, dh2_)
        d2, dgk = _rms_bwd(x_, gk_, dkvn_)
        return dx_ + d1 + d2, dgb, dgk

    dx2, G["b_norm"], G["kv_norm"] = rowwise("x2_dnorm", x2_bwd, [x2, dh2, dkvn, dx3], [W["b_norm"], W["kv_norm"]],
                                             [(D, f32)], [(1, D), (1, D)], comm=rs_att.sibling())

    dx1, g, rs_f0 = _ffn_bwd("f0", 0, ffn0, W["f_norm0"], W["f_inT0"], W["f_cw0"], W["f_cb0"], W["f_down0"], dx2,
                             mid_comm=rs_att.chips())
    G.update(g)

    dgn = matmul("a_dout", dx1, W["a_out"], "nt")
    rs_out = ReduceScatter("a_out", ("a_out",), [matmul("a_wout", gn, dx1, "tn", out_dtype=MXU)])

    def gate_norm_bwd(y_, z_, dgn_, g_):
        w = INNER // GROUPS
        sz = _silu(z_)
        yg = y_ * sz
        parts, dgs = [], []
        for i in range(GROUPS):
            dseg, dg = _rms_bwd(yg[:, w * i:w * (i + 1)], g_[:, w * i:w * (i + 1)], dgn_[:, w * i:w * (i + 1)])
            parts.append(dseg)
            dgs.append(dg)
        dyg = jnp.concatenate(parts, axis=1)
        return dyg * sz, dyg * y_ * _dsilu(z_), jnp.concatenate(dgs, axis=1)

    dy, dz, G["a_gnorm"] = rowwise("a_dgnorm", gate_norm_bwd, [y, z, dgn], [W["a_gnorm"]], [(INNER, f32), (INNER, MXU)], [(1, INNER)],
                                   tile=128, comm=rs_out.sibling())
    dconv = {}
    dconv["xs"], dconv["B"], dconv["C"], ddt_pre, G["dt_bias"], G["A_log"], G["D"] = ssd_bwd(
        conv["xs"], conv["B"], conv["C"], dt_pre, W["dt_bias"], W["A_log"], W["D"], states, dy,
        comm=merge_comms([rs_f0.chips(), rs_out.chips()]))

    def conv_bwd(p_, do_, w_, b_):
        c = _conv(p_, w_, b_)
        return _conv_bwd(p_, w_, do_ * _dsilu(c))

    dh0 = matmul("a_din_z", dz, W["inT"], "nn", b_rows=IN_ROWS["z"])
    g_in = [matmul("a_win_z", dz, h0, "tn", out_dtype=MXU)]
    for k in ("xs", "B", "C"):
        dpre, G[f"cw_{k}"], G[f"cb_{k}"] = colwise(f"a_dconv_{k}", conv_bwd, [pre[k], dconv[k]], [W[f"cw_{k}"], W[f"cb_{k}"]], [MXU], [CONV_A, 1])
        dh0 = matmul(f"a_din_{k}", dpre, W["inT"], "nn", b_rows=IN_ROWS[k], residual=dh0)
        g_in.append(matmul(f"a_win_{k}", dpre, h0, "tn", out_dtype=MXU))
    dh0 = matmul("a_din_dt", ddt_pre, W["in_dtT"], "nn", residual=dh0)
    g_in.append(matmul("a_win_dt", ddt_pre, h0, "tn", out_dtype=MXU)[:HEADS])
    rs_in = ReduceScatter("a_in", ("inT",), [jnp.concatenate(g_in, axis=0)])

    def x0_bwd(x_, dh_, do_, g_):
        dx, dg = _rms_bwd(x_, g_, dh_)
        return do_ + dx, dg

    dx, G["a_norm"] = rowwise("a_dnorm", x0_bwd, [x, dh0, dx1], [W["a_norm"]], [(D, f32)], [(1, D)], comm=rs_in.sibling())
    return loss, dx, G, [rs_f1, rs_att, rs_f0, rs_out, rs_in]


ROW_KEYS = ("inT", "a_out", "f_inT0", "f_down0", "w_kv", "w_q", "w_o", "f_inT1", "f_down1")


def _row_blocks(src):
    return {"inT": src["a_in_proj"][0].T, "a_out": src["a_out_proj"][0], "w_kv": src["w_kv"], "w_q": src["w_q"][0],
            "w_o": src["w_o"][0], "f_inT0": src["f_w_in"][0].T, "f_inT1": src["f_w_in"][1].T,
            "f_down0": src["f_w_down"][0], "f_down1": src["f_w_down"][1]}


def _from_row_blocks(rb):
    return {"a_in_proj": rb["inT"].T[None], "a_out_proj": rb["a_out"][None], "w_kv": rb["w_kv"], "w_q": rb["w_q"][None],
            "w_o": rb["w_o"][None], "f_w_in": jnp.stack([rb["f_inT0"].T, rb["f_inT1"].T]),
            "f_w_down": jnp.stack([rb["f_down0"], rb["f_down1"]])}


SMALL_SHARDED = ("a_norm", "a_conv_w", "a_conv_b", "a_gnorm", "f_conv_w")
REPLICATED = ("a_dt_bias", "a_A_log", "a_D", "kv_norm", "b_kv", "k_norm", "b_norm", "b_q", "q_norm", "sinks", "b_o",
              "f_norm", "f_conv_b")
ORDER = ("a_norm", "a_in_proj", "a_conv_w", "a_conv_b", "a_dt_bias", "a_A_log", "a_D", "a_gnorm", "a_out_proj", "kv_norm",
         "w_kv", "b_kv", "k_norm", "b_norm", "w_q", "b_q", "q_norm", "sinks", "w_o", "b_o", "f_norm", "f_w_in",
         "f_conv_w", "f_conv_b", "f_w_down")


def _gathered_to_whole(name, g):
    if name == "a_conv_w":
        return jnp.moveaxis(g[:, 0], 0, 1).reshape(g.shape[2], -1)
    if name in ("a_norm", "a_conv_b", "a_gnorm"):
        return g[:, 0].reshape(1, -1)
    if name == "f_conv_w":
        return jnp.moveaxis(g, 0, 2).reshape(g.shape[1], g.shape[2], -1)
    raise ValueError(name)


def _whole_to_shards(name, w):
    if name == "a_conv_w":
        return jnp.moveaxis(w.reshape(w.shape[0], N_DEV, -1), 1, 0)[:, None]
    if name in ("a_norm", "a_conv_b", "a_gnorm"):
        return w.reshape(N_DEV, 1, -1)
    if name == "f_conv_w":
        return jnp.moveaxis(w.reshape(w.shape[0], w.shape[1], N_DEV, -1), 2, 0)
    raise ValueError(name)


def _small_weights(whole):
    W = {}
    cw, cb = whole["a_conv_w"], whole["a_conv_b"]
    o = 0
    for k, n in (("xs", INNER), ("B", GROUPS * NSTATE), ("C", GROUPS * NSTATE)):
        W[f"cw_{k}"], W[f"cb_{k}"] = cw[:, o:o + n], cb[:, o:o + n]
        o += n
    W["a_norm"], W["a_gnorm"] = whole["a_norm"], whole["a_gnorm"]
    W["dt_bias"], W["A_log"], W["D"] = (_pad_lanes(whole[k]) for k in ("a_dt_bias", "a_A_log", "a_D"))
    W["kv_norm"], W["b_kv"], W["k_norm"] = whole["kv_norm"].reshape(1, -1), whole["b_kv"].reshape(1, -1), whole["k_norm"].reshape(1, -1)
    for k in ("b_norm", "b_q", "q_norm", "sinks", "b_o"):
        W[k] = whole[k]
    for i in range(2):
        W[f"f_norm{i}"] = whole["f_norm"][i:i + 1]
        W[f"f_cw{i}"], W[f"f_cb{i}"] = whole["f_conv_w"][i], whole["f_conv_b"][i:i + 1]
    return W


def _small_grads(G, shapes):
    nh = HEADS
    out = {
        "a_conv_w": jnp.concatenate([G["cw_xs"], G["cw_B"], G["cw_C"]], axis=1),
        "a_conv_b": jnp.concatenate([G["cb_xs"], G["cb_B"], G["cb_C"]], axis=1),
        "a_norm": G["a_norm"], "a_gnorm": G["a_gnorm"],
        "a_dt_bias": G["dt_bias"][:, :nh], "a_A_log": G["A_log"][:, :nh], "a_D": G["D"][:, :nh],
        "kv_norm": G["kv_norm"], "b_kv": G["b_kv"], "k_norm": G["k_norm"], "b_norm": G["b_norm"],
        "b_q": G["b_q"], "q_norm": G["q_norm"], "sinks": G["sinks"], "b_o": G["b_o"],
        "f_norm": jnp.concatenate([G["f_norm0"], G["f_norm1"]], axis=0),
        "f_conv_w": jnp.stack([G["f_conv_w0"], G["f_conv_w1"]]),
        "f_conv_b": jnp.concatenate([G["f_conv_b0"], G["f_conv_b1"]], axis=0),
    }
    return {k: val.reshape(shapes[k]) if k in shapes else val for k, val in out.items()}


def kernel(x, positions, a_norm, a_in_proj, a_conv_w, a_conv_b, a_dt_bias, a_A_log, a_D, a_gnorm, a_out_proj, kv_norm, w_kv, b_kv, k_norm, b_norm, w_q, b_q, q_norm, sinks, w_o, b_o, f_norm, f_w_in, f_conv_w, f_conv_b, f_w_down, loss_target, m_a_norm, m_a_in_proj, m_a_conv_w, m_a_conv_b, m_a_dt_bias, m_a_A_log, m_a_D, m_a_gnorm, m_a_out_proj, m_kv_norm, m_w_kv, m_b_kv, m_k_norm, m_b_norm, m_w_q, m_b_q, m_q_norm, m_sinks, m_w_o, m_b_o, m_f_norm, m_f_w_in, m_f_conv_w, m_f_conv_b, m_f_w_down, v_a_norm, v_a_in_proj, v_a_conv_w, v_a_conv_b, v_a_dt_bias, v_a_A_log, v_a_D, v_a_gnorm, v_a_out_proj, v_kv_norm, v_w_kv, v_b_kv, v_k_norm, v_b_norm, v_w_q, v_b_q, v_q_norm, v_sinks, v_w_o, v_b_o, v_f_norm, v_f_w_in, v_f_conv_w, v_f_conv_b, v_f_w_down):
    given = dict(locals())
    w_in = {n: given[n] for n in ORDER}
    m_in = {n: given["m_" + n] for n in ORDER}
    v_in = {n: given["v_" + n] for n in ORDER}
    dev = 4 * lax.axis_index("x") + 2 * lax.axis_index("y") + lax.axis_index("c")

    w2, m2, v2 = _row_blocks(w_in), _row_blocks(m_in), _row_blocks(v_in)
    small_pack = _pack([w_in[n] for n in SMALL_SHARDED], 8, f32)
    shards = {k: w2[k].astype(MXU) for k in ROW_KEYS}
    in_all, small_all = run_comm("ag_head", gather_comm([shards["inT"], small_pack]))
    whole = {n: w_in[n] for n in REPLICATED}
    for n, g in zip(SMALL_SHARDED, _unpack(small_all, [w_in[n].shape for n in SMALL_SHARDED], lead=(N_DEV,))):
        whole[n] = _gathered_to_whole(n, g)
    W = _small_weights(whole)
    W["inT"] = in_all.reshape(-1, D)
    W["in_dtT"] = jnp.pad(W["inT"][IN_COLS - HEADS:], ((0, LANES - HEADS), (0, 0)))

    loss, dx, G, scatters = _local_step(x[0], positions.reshape(S, 1).astype(f32), loss_target[0], W, shards)
    grads = _small_grads(G, {n: whole[n].shape for n in REPLICATED})

    small_names = SMALL_SHARDED + REPLICATED
    small_part = _pack([grads[n] for n in small_names], 8, f32)
    small_gather = gather_comm([small_part])
    run_comm("rs_tail", merge_comms([scatters[-1].chips(), small_gather]))
    sums = {}
    for rs in scatters:
        sums.update(rs.sums())
    g_out = _from_row_blocks(sums)
    small_sum = sum_devices(small_gather.results[0])
    for n, g in zip(small_names, _unpack(small_sum, [grads[n].shape for n in small_names])):
        if n in SMALL_SHARDED:
            g_out[n] = lax.dynamic_index_in_dim(_whole_to_shards(n, g), dev, axis=0, keepdims=False)
        else:
            g_out[n] = g.reshape(w_in[n].shape)

    stepped = {k: adamw(f"adamw_{k}", w2[k], sums[k], m2[k], v2[k]) for k in ROW_KEYS}
    delta, new_m, new_v = (_from_row_blocks({k: stepped[k][i] for k in ROW_KEYS}) for i in range(3))
    packs = [_pack([src[n] for n in small_names], 8, f32) for src in (w_in, g_out, m_in, v_in)]
    outs = adamw("adamw_small", *packs)
    for dst, buf in zip((delta, new_m, new_v), outs):
        for n, a in zip(small_names, _unpack(buf, [w_in[n].shape for n in small_names])):
            dst[n] = a

    loss_all = lax.psum(loss[0, 0], AXES)
    return (loss_all, dx[None], *[g_out[n] for n in ORDER], *[delta[n] for n in ORDER],
            *[new_m[n] for n in ORDER], *[new_v[n] for n in ORDER])
```

```python
import functools
import math

import jax
import jax.numpy as jnp
from jax import lax
from jax.experimental import pallas as pl
from jax.experimental.pallas import tpu as pltpu

f32 = jnp.float32
bf16 = jnp.bfloat16
MXU = bf16

N_DEV = 8
S = 2048
D = 1024
EPS = 1e-5
INNER = 2048
HEADS = 32
HP = 64
GROUPS = 8
HPG = HEADS // GROUPS
NSTATE = 128
CONV_A = 4
CHUNK = 256
NCHUNK = S // CHUNK
AH = 64
QH = 16
KVH = 4
QPK = QH // KVH
WIN = 128
NBLK = S // WIN
ROPE_THETA = 10000.0
FFN = 2816
CONV_F = 3
LANES = 128
PACK_W = 1024
BIG_TILE = 496
VMEM_LIMIT = 56 * 1024 * 1024

ADAM_LR, ADAM_B1, ADAM_B2, ADAM_EPS, ADAM_WD, ADAM_STEP = 0.001, 0.9, 0.999, 1e-08, 0.01, 10

MESH = pl.DeviceIdType.MESH
AXES = ("x", "y", "c")


def _cparams(sem=None):
    return pltpu.CompilerParams(dimension_semantics=sem, vmem_limit_bytes=VMEM_LIMIT)


def _pick(n, cands):
    for c in cands:
        if n % c == 0:
            return c
    return n


class Comm:
    def __init__(self, ins, out_shapes, sems, start, finish):
        self.ins, self.out_shapes, self.sems, self.start, self.finish = list(ins), list(out_shapes), list(sems), start, finish
        self.results, self.children = None, ()

    def set_results(self, res):
        self.results, o = list(res), 0
        for ch in self.children:
            ch.set_results(res[o:o + len(ch.out_shapes)])
            o += len(ch.out_shapes)


def merge_comms(comms):
    def each(fn_name, ins, outs, sems):
        i = o = s = 0
        for c in comms:
            getattr(c, fn_name)(ins[i:i + len(c.ins)], outs[o:o + len(c.out_shapes)], sems[s:s + len(c.sems)])
            i, o, s = i + len(c.ins), o + len(c.out_shapes), s + len(c.sems)

    merged = Comm([a for c in comms for a in c.ins], [a for c in comms for a in c.out_shapes], [a for c in comms for a in c.sems],
                  functools.partial(each, "start"), functools.partial(each, "finish"))
    merged.children = tuple(comms)
    return merged


def _call(body, args, *, name, grid, in_specs, out_specs, out_shape, scratch=(), sem=None, comm=None):
    if comm is None:
        return pl.pallas_call(body, name=name, grid=grid, in_specs=list(in_specs), out_specs=list(out_specs),
                              out_shape=list(out_shape), scratch_shapes=list(scratch), compiler_params=_cparams(sem))(*args)
    n_in, n_out, n_scr, c_in, c_out = len(in_specs), len(out_shape), len(scratch), len(comm.ins), len(comm.out_shapes)
    any_spec = pl.BlockSpec(memory_space=pl.ANY)

    def outer(*refs):
        ins, c_ins = refs[:n_in], refs[n_in:n_in + c_in]
        o = n_in + c_in
        outs, c_outs = refs[o:o + n_out], refs[o + n_out:o + n_out + c_out]
        o += n_out + c_out
        scr, c_sems = refs[o:o + n_scr], refs[o + n_scr:]
        ids = [pl.program_id(i) for i in range(len(grid))]
        first = functools.reduce(jnp.logical_and, [i == 0 for i in ids])
        last = functools.reduce(jnp.logical_and, [i == g - 1 for i, g in zip(ids, grid)])

        @pl.when(first)
        def _():
            comm.start(c_ins, c_outs, c_sems)

        body(*ins, *outs, *scr)

        @pl.when(last)
        def _():
            comm.finish(c_ins, c_outs, c_sems)

    res = pl.pallas_call(
        outer, name=name, grid=grid, in_specs=list(in_specs) + [any_spec] * c_in,
        out_specs=list(out_specs) + [any_spec] * c_out, out_shape=list(out_shape) + comm.out_shapes,
        scratch_shapes=list(scratch) + comm.sems, compiler_params=_cparams(("arbitrary",) * len(grid)),
    )(*args, *comm.ins)
    comm.set_results(res[n_out:])
    return res[:n_out]


def matmul(name, a, b, mode, out_dtype=f32, bias=None, residual=None, b_rows=None, a_lead=None):
    b_idx, b_shape = (0, b.shape) if b_rows is None else (b_rows[0], (b_rows[1], b.shape[1]))
    a_shape = a.shape if a_lead is None else a.shape[1:]
    if mode == "nn":
        (M, K), (K2, N) = a_shape, b_shape
    elif mode == "nt":
        (M, K), (N, K2) = a_shape, b_shape
    else:
        (K, M), (K2, N) = a_shape, b_shape
    assert K == K2, (name, a_shape, b_shape)
    if mode == "tn":
        tm, tn = M, _pick(N, (512, 256, 128) if M <= 1024 else (256, 128))
        a_spec = pl.BlockSpec((K, M), lambda j: (0, 0))
        b_spec = pl.BlockSpec((K, tn), lambda j: (0, j))
        dims = (((0,), (0,)), ((), ()))
        grid, o_map, row_map = (N // tn,), (lambda j: (0, j)), (lambda j: (0, j))
    else:
        tm, tn = (256 if N >= 2048 else 512), N
        a_spec = pl.BlockSpec((tm, K), lambda i: (i, 0)) if a_lead is None else pl.BlockSpec((None, tm, K), lambda i: (a_lead, i, 0))
        b_spec = pl.BlockSpec(b_shape, lambda i: (b_idx, 0))
        dims = (((1,), (0,)), ((), ())) if mode == "nn" else (((1,), (1,)), ((), ()))
        grid, o_map, row_map = (M // tm,), (lambda i: (i, 0)), (lambda i: (0, 0))
    ins, in_specs = [a, b], [a_spec, b_spec]
    if bias is not None:
        ins.append(bias)
        in_specs.append(pl.BlockSpec((1, tn), row_map))
    if residual is not None:
        ins.append(residual)
        in_specs.append(pl.BlockSpec((tm, tn), o_map))
    has_bias, has_res = bias is not None, residual is not None

    def body(a_ref, b_ref, *rest):
        rest = list(rest)
        bias_ref = rest.pop(0) if has_bias else None
        res_ref = rest.pop(0) if has_res else None
        (o_ref,) = rest
        r = lax.dot_general(a_ref[...].astype(MXU), b_ref[...].astype(MXU), dims, preferred_element_type=f32)
        if has_bias:
            r = r + bias_ref[...]
        if has_res:
            r = r + res_ref[...]
        o_ref[...] = r.astype(out_dtype)

    return pl.pallas_call(
        body, name=name, grid=grid, in_specs=in_specs,
        out_specs=pl.BlockSpec((tm, tn), o_map),
        out_shape=jax.ShapeDtypeStruct((M, N), out_dtype),
        compiler_params=_cparams(("parallel",)),
    )(*ins)


def matmul_tn_stacked(name, a, b, out_dtype):
    R, K, M = a.shape
    N = b.shape[1]
    tn = _pick(N, (256, 128))

    def body(a_ref, b_ref, o_ref):
        o_ref[0] = lax.dot_general(a_ref[0].astype(MXU), b_ref[...].astype(MXU), (((0,), (0,)), ((), ())),
                                   preferred_element_type=f32).astype(out_dtype)

    out = pl.pallas_call(
        body, name=name, grid=(R, N // tn),
        in_specs=[pl.BlockSpec((1, K, M), lambda r, j: (r, 0, 0)), pl.BlockSpec((K, tn), lambda r, j: (0, j))],
        out_specs=pl.BlockSpec((1, M, tn), lambda r, j: (r, 0, j)),
        out_shape=jax.ShapeDtypeStruct((R, M, N), out_dtype),
        compiler_params=_cparams(("parallel", "parallel")),
    )(a, b)
    return out.reshape(R * M, N)


def rowwise(name, fn, rows, pars, outs, accs=(), tile=256, comm=None):
    n_in, n_out = len(rows) + len(pars), len(outs)
    in_specs = [pl.BlockSpec((None, tile, r[0].shape[2]), functools.partial(lambda i, lead: (lead, i, 0), lead=r[1]))
                if isinstance(r, tuple) else pl.BlockSpec((tile, r.shape[1]), lambda i: (i, 0)) for r in rows]
    rows = [r[0] if isinstance(r, tuple) else r for r in rows]
    in_specs += [pl.BlockSpec(p.shape, lambda i: (0, 0)) for p in pars]
    out_specs = [pl.BlockSpec((tile, c), lambda i: (i, 0)) for c, _ in outs]
    out_specs += [pl.BlockSpec(shp, lambda i: (0, 0)) for shp in accs]
    out_shape = [jax.ShapeDtypeStruct((S, c), dt) for c, dt in outs]
    out_shape += [jax.ShapeDtypeStruct(shp, f32) for shp in accs]

    def body(*refs):
        res = fn(*[r[...] for r in refs[:n_in]])
        o_refs = refs[n_in:n_in + n_out]
        a_refs = refs[n_in + n_out:]
        for ref, val in zip(o_refs, res[:n_out]):
            ref[...] = val.astype(ref.dtype)
        if a_refs:
            @pl.when(pl.program_id(0) == 0)
            def _():
                for ref in a_refs:
                    ref[...] = jnp.zeros_like(ref)
            for ref, val in zip(a_refs, res[n_out:]):
                ref[...] += val

    return _call(body, [*rows, *pars], name=name, grid=(S // tile,), in_specs=in_specs, out_specs=out_specs,
                 out_shape=out_shape, sem=("arbitrary",) if accs else ("parallel",), comm=comm)


def colwise(name, fn, cols, pars, outs, pouts=(), ct=128, comm=None):
    C = cols[0].shape[1]
    n_in, n_out = len(cols) + len(pars), len(outs)
    in_specs = [pl.BlockSpec((S, ct), lambda j: (0, j)) for _ in cols]
    in_specs += [pl.BlockSpec((p.shape[0], ct), lambda j: (0, j)) for p in pars]
    out_specs = [pl.BlockSpec((S, ct), lambda j: (0, j)) for _ in outs]
    out_specs += [pl.BlockSpec((r, ct), lambda j: (0, j)) for r in pouts]
    out_shape = [jax.ShapeDtypeStruct((S, C), dt) for dt in outs]
    out_shape += [jax.ShapeDtypeStruct((r, C), f32) for r in pouts]

    def body(*refs):
        res = fn(*[r[...] for r in refs[:n_in]])
        for ref, val in zip(refs[n_in:], res):
            ref[...] = val.astype(ref.dtype)

    return _call(body, [*cols, *pars], name=name, grid=(C // ct,), in_specs=in_specs, out_specs=out_specs,
                 out_shape=out_shape, sem=("parallel",), comm=comm)


def _sigmoid(x):
    return 0.5 * jnp.tanh(0.5 * x) + 0.5


def _silu(x):
    return x * _sigmoid(x)


def _dsilu(x):
    sg = _sigmoid(x)
    return sg * (1.0 + x * (1.0 - sg))


def _softplus(x):
    return jnp.maximum(x, 0.0) + jnp.log(1.0 + jnp.exp(-jnp.abs(x)))


def _rms_fwd(x, g):
    r = lax.rsqrt(jnp.mean(x * x, axis=-1, keepdims=True) + EPS)
    return x * r * g


def _rms_bwd(x, g, dh):
    r = lax.rsqrt(jnp.mean(x * x, axis=-1, keepdims=True) + EPS)
    xh = x * r
    dxh = dh * g
    dx = r * (dxh - xh * jnp.mean(dxh * xh, axis=-1, keepdims=True))
    return dx, jnp.sum(dh * xh, axis=0, keepdims=True)


def _taps(x, width):
    row = lax.broadcasted_iota(jnp.int32, x.shape, 0)
    return [jnp.where(row >= s, pltpu.roll(x, s, 0), 0.0) for s in range(width - 1, 0, -1)] + [x]


def _conv(x, w, b, taps=None):
    width = w.shape[0]
    taps = _taps(x, width) if taps is None else taps
    out = b + w[0:1, :] * taps[0]
    for k in range(1, width):
        out = out + w[k:k + 1, :] * taps[k]
    return out


def _conv_bwd(x, w, dc, taps=None):
    width, n = w.shape[0], x.shape[0]
    taps = _taps(x, width) if taps is None else taps
    row = lax.broadcasted_iota(jnp.int32, x.shape, 0)
    dx = w[width - 1:width, :] * dc
    for k in range(width - 1):
        s = width - 1 - k
        dx = dx + w[k:k + 1, :] * jnp.where(row < n - s, pltpu.roll(dc, n - s, 0), 0.0)
    dw = jnp.concatenate([jnp.sum(dc * t, axis=0, keepdims=True) for t in taps], axis=0)
    return dx, dw, jnp.sum(dc, axis=0, keepdims=True)


def _rope_tables(pos, inv_freq):
    ang = pos * inv_freq
    return jnp.cos(ang), jnp.sin(ang)


def _headnorm_rope_fwd(x, g, cos, sin, heads):
    half = AH // 2
    outs = []
    for h in range(heads):
        seg = x[:, AH * h:AH * (h + 1)]
        n = _rms_fwd(seg, g)
        n1, n2 = n[:, :half], n[:, half:]
        outs += [n1 * cos - n2 * sin, n2 * cos + n1 * sin]
    return jnp.concatenate(outs, axis=1)


def _headnorm_rope_bwd(x, g, cos, sin, dout, heads):
    half = AH // 2
    dxs, dg = [], jnp.zeros((1, AH), f32)
    for h in range(heads):
        seg = x[:, AH * h:AH * (h + 1)]
        d = dout[:, AH * h:AH * (h + 1)]
        d1, d2 = d[:, :half], d[:, half:]
        dn = jnp.concatenate([d1 * cos + d2 * sin, d2 * cos - d1 * sin], axis=1)
        dx, dgh = _rms_bwd(seg, g, dn)
        dxs.append(dx)
        dg = dg + dgh
    return jnp.concatenate(dxs, axis=1), dg


def _ssd_prep(dt_pre, dt_bias, a_log, dt_s, acum_s, acumT_s):
    dt = _softplus(dt_pre + dt_bias)
    a = dt * (-jnp.exp(a_log))
    row = lax.broadcasted_iota(jnp.int32, (CHUNK, CHUNK), 0)
    col = lax.broadcasted_iota(jnp.int32, (CHUNK, CHUNK), 1)
    dt_s[...] = dt
    acum_s[...] = jnp.dot((col <= row).astype(f32), a, precision=lax.Precision.HIGHEST, preferred_element_type=f32)
    acumT_s[...] = lax.dot_general(a, (row <= col).astype(f32), (((0,), (0,)), ((), ())),
                                   precision=lax.Precision.HIGHEST, preferred_element_type=f32)


def _head_cols(h, dt_s, acum_s, acumT_s):
    lane = lax.broadcasted_iota(jnp.int32, (1, LANES), 1)
    oh_l = (lane == h).astype(f32)
    sub = lax.broadcasted_iota(jnp.int32, (LANES, 1), 0)
    oh_s = (sub == h).astype(f32)
    dt_h = jnp.sum(dt_s[...] * oh_l, axis=1, keepdims=True)
    ac_h = jnp.sum(acum_s[...] * oh_l, axis=1, keepdims=True)
    acr_h = jnp.sum(acumT_s[...] * oh_s, axis=0, keepdims=True)
    return oh_l, dt_h, ac_h, acr_h


def ssd_fwd(xs, Bm, Cm, dt_pre, dt_bias, a_log, d_skip, comm=None):
    def body(xs_ref, b_ref, c_ref, dtp_ref, bias_ref, alog_ref, d_ref, y_ref, st_ref, state, dt_s, acum_s, acumT_s):
        c, g = pl.program_id(0), pl.program_id(1)

        @pl.when(g == 0)
        def _():
            _ssd_prep(dtp_ref[...], bias_ref[...], alog_ref[...], dt_s, acum_s, acumT_s)

        row = lax.broadcasted_iota(jnp.int32, (CHUNK, CHUNK), 0)
        col = lax.broadcasted_iota(jnp.int32, (CHUNK, CHUNK), 1)
        causal = col <= row
        Bb, Cb = b_ref[...], c_ref[...]
        cb = lax.dot_general(Cb.astype(MXU), Bb.astype(MXU), (((1,), (1,)), ((), ())), preferred_element_type=f32)
        xs_blk = xs_ref[...]

        @pl.when(c == 0)
        def _():
            for j in range(HPG):
                state[g * HPG + j] = jnp.zeros((NSTATE, HP), f32)

        prevs = [state[g * HPG + j] for j in range(HPG)]
        y_off_all = jnp.dot(Cb.astype(MXU), jnp.concatenate(prevs, axis=1).astype(MXU), preferred_element_type=f32)
        ys, xds, e_ends = [], [], []
        for j in range(HPG):
            oh_l, dt_h, ac_h, acr_h = _head_cols(g * HPG + j, dt_s, acum_s, acumT_s)
            decay = jnp.exp(jnp.where(causal, ac_h - acr_h, -1e30))
            w = (cb * decay).astype(MXU)
            xs_h = xs_blk[:, HP * j:HP * (j + 1)]
            xd = xs_h * dt_h
            y_diag = jnp.dot(w, xd.astype(MXU), preferred_element_type=f32)
            y_off = y_off_all[:, HP * j:HP * (j + 1)] * jnp.exp(ac_h)
            d_h = jnp.sum(d_ref[...] * oh_l, axis=1, keepdims=True)
            ys.append(y_diag + y_off + xs_h * d_h)
            a_end = ac_h[CHUNK - 1:CHUNK, :]
            xds.append(xd * jnp.exp(a_end - ac_h))
            e_ends.append(jnp.exp(a_end))
        s_c = lax.dot_general(Bb.astype(MXU), jnp.concatenate(xds, axis=1).astype(MXU), (((0,), (0,)), ((), ())),
                              preferred_element_type=f32)
        for j in range(HPG):
            st_ref[0, j] = prevs[j]
            state[g * HPG + j] = prevs[j] * e_ends[j] + s_c[:, HP * j:HP * (j + 1)]
        y_ref[...] = jnp.concatenate(ys, axis=1)

    par = pl.BlockSpec((1, LANES), lambda c, g: (0, 0))
    return _call(
        body, [xs, Bm, Cm, dt_pre, dt_bias, a_log, d_skip], comm=comm, name="ssd_fwd", grid=(NCHUNK, GROUPS),
        in_specs=[pl.BlockSpec((CHUNK, HPG * HP), lambda c, g: (c, g)),
                  pl.BlockSpec((CHUNK, NSTATE), lambda c, g: (c, g)),
                  pl.BlockSpec((CHUNK, NSTATE), lambda c, g: (c, g)),
                  pl.BlockSpec((CHUNK, LANES), lambda c, g: (c, 0)), par, par, par],
        out_specs=[pl.BlockSpec((CHUNK, HPG * HP), lambda c, g: (c, g)),
                   pl.BlockSpec((1, HPG, NSTATE, HP), lambda c, g: (c, g, 0, 0))],
        out_shape=[jax.ShapeDtypeStruct((S, INNER), f32), jax.ShapeDtypeStruct((NCHUNK, HEADS, NSTATE, HP), f32)],
        scratch=[pltpu.VMEM((HEADS, NSTATE, HP), f32), pltpu.VMEM((CHUNK, LANES), f32),
                 pltpu.VMEM((CHUNK, LANES), f32), pltpu.VMEM((LANES, CHUNK), f32)],
        sem=("arbitrary", "arbitrary"))


def ssd_bwd(xs, Bm, Cm, dt_pre, dt_bias, a_log, d_skip, states, dy, comm=None):
    rev = lambda c: NCHUNK - 1 - c

    def body(xs_ref, b_ref, c_ref, dtp_ref, bias_ref, alog_ref, d_ref, st_ref, dy_ref,
             dxs_ref, db_ref, dc_ref, ddt_ref, dbias_ref, dalog_ref, dd_ref,
             dstate, dt_s, acum_s, acumT_s, dacum_s, ddt_s, da_s):
        c, g = pl.program_id(0), pl.program_id(1)

        @pl.when(g == 0)
        def _():
            _ssd_prep(dtp_ref[...], bias_ref[...], alog_ref[...], dt_s, acum_s, acumT_s)
            dacum_s[...] = jnp.zeros_like(dacum_s)
            ddt_s[...] = jnp.zeros_like(ddt_s)

        @pl.when((c == 0) & (g == 0))
        def _():
            da_s[...] = jnp.zeros_like(da_s)
            dd_ref[...] = jnp.zeros_like(dd_ref)
            dbias_ref[...] = jnp.zeros_like(dbias_ref)
            dalog_ref[...] = jnp.zeros_like(dalog_ref)

        row = lax.broadcasted_iota(jnp.int32, (CHUNK, CHUNK), 0)
        col = lax.broadcasted_iota(jnp.int32, (CHUNK, CHUNK), 1)
        sub_l = lax.broadcasted_iota(jnp.int32, (CHUNK, 1), 0)
        last = (sub_l == CHUNK - 1).astype(f32)
        nt = (((1,), (1,)), ((), ()))
        tn = (((0,), (0,)), ((), ()))
        Bb, Cb = b_ref[...], c_ref[...]
        Bm_, Cm_ = Bb.astype(MXU), Cb.astype(MXU)
        cb = lax.dot_general(Cm_, Bm_, nt, preferred_element_type=f32)
        bc = lax.dot_general(Bm_, Cm_, nt, preferred_element_type=f32)
        xs_blk, dy_blk = xs_ref[...], dy_ref[...]
        @pl.when(c == 0)
        def _():
            for j in range(HPG):
                dstate[g * HPG + j] = jnp.zeros((NSTATE, HP), f32)

        dnexts =[dstate[g * HPG + j] for j in range(HPG)]
        prevs = [st_ref[0, j] for j in range(HPG)]
        dn_all = jnp.concatenate(dnexts, axis=1).astype(MXU)
        pv_all = jnp.concatenate(prevs, axis=1).astype(MXU)
        bd_all = jnp.dot(Bm_, dn_all, preferred_element_type=f32)
        yo_all = jnp.dot(Cm_, pv_all, preferred_element_type=f32)
        dxs, eadys, xdts, e_ends = [], [], [], []
        dcb_sum, dbc_sum = jnp.zeros((CHUNK, CHUNK), f32), jnp.zeros((CHUNK, CHUNK), f32)
        for j in range(HPG):
            hs = slice(HP * j, HP * (j + 1))
            oh_l, dt_h, ac_h, acr_h = _head_cols(g * HPG + j, dt_s, acum_s, acumT_s)
            lm = jnp.exp(jnp.where(col <= row, ac_h - acr_h, -1e30))
            lmT = jnp.exp(jnp.where(row <= col, acr_h - ac_h, -1e30))
            xs_h, dy_h = xs_blk[:, hs], dy_blk[:, hs]
            xd = xs_h * dt_h
            xdm, dym = xd.astype(MXU), dy_h.astype(MXU)
            ea = jnp.exp(ac_h)
            a_end = ac_h[CHUNK - 1:CHUNK, :]
            e_end = jnp.exp(a_end)
            dte = jnp.exp(a_end - ac_h)
            bd = bd_all[:, hs]
            dxd = jnp.dot((bc * lmT).astype(MXU), dym, preferred_element_type=f32) + dte * bd
            dw = lax.dot_general(dym, xdm, nt, preferred_element_type=f32)
            dwT = lax.dot_general(xdm, dym, nt, preferred_element_type=f32)
            dcb = dw * lm
            dbc = dwT * lmT
            dcb_sum, dbc_sum = dcb_sum + dcb, dbc_sum + dbc
            eady = ea * dy_h
            eadys.append(eady)
            xdts.append(xd * dte)
            e_ends.append(e_end)
            r1 = jnp.sum(dcb * cb, axis=1, keepdims=True)
            r2 = jnp.sum(dbc * bc, axis=1, keepdims=True)
            t3 = jnp.sum(eady * yo_all[:, hs], axis=1, keepdims=True)
            t4 = jnp.sum(bd * xd, axis=1, keepdims=True) * dte
            end_extra = jnp.sum(t4, axis=0, keepdims=True) + e_end * jnp.sum(jnp.sum(prevs[j] * dnexts[j], axis=1, keepdims=True), axis=0, keepdims=True)
            dacum_h = r1 - r2 + t3 - t4 + last * end_extra
            dacum_s[...] += dacum_h * oh_l
            ddt_s[...] += jnp.sum(dxd * xs_h, axis=1, keepdims=True) * oh_l
            d_h = jnp.sum(d_ref[...] * oh_l, axis=1, keepdims=True)
            dxs.append(dxd * dt_h + dy_h * d_h)
            dd_ref[...] += oh_l * jnp.sum(jnp.sum(dy_h * xs_h, axis=1, keepdims=True), axis=0, keepdims=True)
        ea_all = jnp.concatenate(eadys, axis=1).astype(MXU)
        xt_all = jnp.concatenate(xdts, axis=1).astype(MXU)
        dxs_ref[...] = jnp.concatenate(dxs, axis=1)
        dc_ref[...] = jnp.dot(dcb_sum.astype(MXU), Bm_, preferred_element_type=f32) \
            + lax.dot_general(ea_all, pv_all, nt, preferred_element_type=f32)
        db_ref[...] = jnp.dot(dbc_sum.astype(MXU), Cm_, preferred_element_type=f32) \
            + lax.dot_general(xt_all, dn_all, nt, preferred_element_type=f32)
        ds_all = lax.dot_general(Cm_, ea_all, tn, preferred_element_type=f32)
        for j in range(HPG):
            dstate[g * HPG + j] = ds_all[:, HP * j:HP * (j + 1)] + e_ends[j] * dnexts[j]

        @pl.when(g == GROUPS - 1)
        def _():
            a_row = -jnp.exp(alog_ref[...])
            da = jnp.dot((row <= col).astype(f32), dacum_s[...], precision=lax.Precision.HIGHEST, preferred_element_type=f32)
            da_s[...] += jnp.sum(da * dt_s[...], axis=0, keepdims=True)
            z = dtp_ref[...] + bias_ref[...]
            ddt_pre = (ddt_s[...] + da * a_row) * _sigmoid(z)
            ddt_ref[...] = ddt_pre.astype(ddt_ref.dtype)
            dbias_ref[...] += jnp.sum(ddt_pre, axis=0, keepdims=True)

            @pl.when(c == NCHUNK - 1)
            def _():
                dalog_ref[...] = da_s[...] * a_row

    par = pl.BlockSpec((1, LANES), lambda c, g: (0, 0))
    return _call(
        body, [xs, Bm, Cm, dt_pre, dt_bias, a_log, d_skip, states, dy], comm=comm, name="ssd_bwd", grid=(NCHUNK, GROUPS),
        in_specs=[pl.BlockSpec((CHUNK, HPG * HP), lambda c, g: (rev(c), g)),
                  pl.BlockSpec((CHUNK, NSTATE), lambda c, g: (rev(c), g)),
                  pl.BlockSpec((CHUNK, NSTATE), lambda c, g: (rev(c), g)),
                  pl.BlockSpec((CHUNK, LANES), lambda c, g: (rev(c), 0)), par, par, par,
                  pl.BlockSpec((1, HPG, NSTATE, HP), lambda c, g: (rev(c), g, 0, 0)),
                  pl.BlockSpec((CHUNK, HPG * HP), lambda c, g: (rev(c), g))],
        out_specs=[pl.BlockSpec((CHUNK, HPG * HP), lambda c, g: (rev(c), g)),
                   pl.BlockSpec((CHUNK, NSTATE), lambda c, g: (rev(c), g)),
                   pl.BlockSpec((CHUNK, NSTATE), lambda c, g: (rev(c), g)),
                   pl.BlockSpec((CHUNK, LANES), lambda c, g: (rev(c), 0)), par, par, par],
        out_shape=[jax.ShapeDtypeStruct((S, INNER), f32), jax.ShapeDtypeStruct((S, GROUPS * NSTATE), f32),
                   jax.ShapeDtypeStruct((S, GROUPS * NSTATE), f32), jax.ShapeDtypeStruct((S, LANES), MXU),
                   jax.ShapeDtypeStruct((1, LANES), f32), jax.ShapeDtypeStruct((1, LANES), f32),
                   jax.ShapeDtypeStruct((1, LANES), f32)],
        scratch=[pltpu.VMEM((HEADS, NSTATE, HP), f32), pltpu.VMEM((CHUNK, LANES), f32),
                 pltpu.VMEM((CHUNK, LANES), f32), pltpu.VMEM((LANES, CHUNK), f32),
                 pltpu.VMEM((CHUNK, LANES), f32), pltpu.VMEM((CHUNK, LANES), f32), pltpu.VMEM((1, LANES), f32)],
        sem=("arbitrary", "arbitrary"))


def _attn_block(n, q, kp, kc, vp, vc, sinks, hk):
    rows = QPK * WIN
    qi = lax.broadcasted_iota(jnp.int32, (rows, 2 * WIN), 0) & (WIN - 1)
    ki = lax.broadcasted_iota(jnp.int32, (rows, 2 * WIN), 1)
    rel = qi + WIN - ki
    mask = (rel >= 0) & (rel < WIN) & ((ki >= WIN) | (n > 0))
    sl = slice(AH * hk, AH * (hk + 1))
    kb = jnp.concatenate([kp[:, sl], kc[:, sl]], axis=0).astype(MXU)
    vb = jnp.concatenate([vp[:, sl], vc[:, sl]], axis=0).astype(MXU)
    qg = jnp.concatenate([q[:, AH * (QPK * hk + g):AH * (QPK * hk + g + 1)] for g in range(QPK)], axis=0).astype(MXU)
    s = lax.dot_general(qg, kb, (((1,), (1,)), ((), ())), preferred_element_type=f32) * (AH ** -0.5)
    s = jnp.where(mask, s, -1e30)
    sink = jnp.concatenate([jnp.broadcast_to(sinks[:, QPK * hk + g:QPK * hk + g + 1], (WIN, 1)) for g in range(QPK)], axis=0)
    m = jnp.maximum(jnp.max(s, axis=1, keepdims=True), sink)
    p = jnp.exp(s - m)
    ps = jnp.exp(sink - m)
    den = jnp.sum(p, axis=1, keepdims=True) + ps
    return qg, kb, vb, p / den, ps / den


def _kv_specs():
    prev = lambda n: (jnp.maximum(n - 1, 0), 0)
    cur = lambda n: (n, 0)
    w = KVH * AH
    return [pl.BlockSpec((WIN, w), prev), pl.BlockSpec((WIN, w), cur), pl.BlockSpec((WIN, w), prev), pl.BlockSpec((WIN, w), cur)]


def attn_fwd(q, k, v, sinks, comm=None):
    def body(q_ref, kp_ref, kc_ref, vp_ref, vc_ref, s_ref, o_ref):
        n = pl.program_id(0)
        q_, kp, kc, vp, vc, sk = q_ref[...], kp_ref[...], kc_ref[...], vp_ref[...], vc_ref[...], s_ref[...]
        outs = []
        for hk in range(KVH):
            _, _, vb, pr, _ = _attn_block(n, q_, kp, kc, vp, vc, sk, hk)
            o = jnp.dot(pr.astype(MXU), vb, preferred_element_type=f32)
            outs += [o[WIN * g:WIN * (g + 1)] for g in range(QPK)]
        o_ref[...] = jnp.concatenate(outs, axis=1)

    return _call(
        body, [q, k, k, v, v, sinks], comm=comm, name="attn_fwd", grid=(NBLK,),
        in_specs=[pl.BlockSpec((WIN, D), lambda n: (n, 0))] + _kv_specs() + [pl.BlockSpec((1, QH), lambda n: (0, 0))],
        out_specs=[pl.BlockSpec((WIN, D), lambda n: (n, 0))],
        out_shape=[jax.ShapeDtypeStruct((S, D), f32)], sem=("parallel",))[0]


def attn_bwd(q, k, v, sinks, dout, comm=None):
    def body(q_ref, kp_ref, kc_ref, vp_ref, vc_ref, s_ref, do_ref, dq_ref, dkp_ref, dkc_ref, dvp_ref, dvc_ref, ds_ref):
        n = pl.program_id(0)

        @pl.when(n == 0)
        def _():
            ds_ref[...] = jnp.zeros_like(ds_ref)

        q_, kp, kc, vp, vc, sk, do = q_ref[...], kp_ref[...], kc_ref[...], vp_ref[...], vc_ref[...], s_ref[...], do_ref[...]
        lane = lax.broadcasted_iota(jnp.int32, (1, QH), 1)
        nt = (((1,), (1,)), ((), ()))
        tn = (((0,), (0,)), ((), ()))
        dqs, dkps, dkcs, dvps, dvcs = [], [], [], [], []
        dsink = jnp.zeros((1, QH), f32)
        for hk in range(KVH):
            qg, kb, vb, pr, prs = _attn_block(n, q_, kp, kc, vp, vc, sk, hk)
            dog = jnp.concatenate([do[:, AH * (QPK * hk + g):AH * (QPK * hk + g + 1)] for g in range(QPK)], axis=0).astype(MXU)
            dp = lax.dot_general(dog, vb, nt, preferred_element_type=f32)
            dvb = lax.dot_general(pr.astype(MXU), dog, tn, preferred_element_type=f32)
            delta = jnp.sum(pr * dp, axis=1, keepdims=True)
            ds = (pr * (dp - delta)).astype(MXU)
            dsk = -prs * delta
            for g in range(QPK):
                dsink = dsink + jnp.sum(dsk[WIN * g:WIN * (g + 1)], axis=0, keepdims=True) * (lane == QPK * hk + g).astype(f32)
            dqg = jnp.dot(ds, kb, preferred_element_type=f32) * (AH ** -0.5)
            dkb = lax.dot_general(ds, qg, tn, preferred_element_type=f32) * (AH ** -0.5)
            dqs += [dqg[WIN * g:WIN * (g + 1)] for g in range(QPK)]
            dkps.append(dkb[:WIN])
            dkcs.append(dkb[WIN:])
            dvps.append(dvb[:WIN])
            dvcs.append(dvb[WIN:])
        dq_ref[...] = jnp.concatenate(dqs, axis=1)
        dkp_ref[...] = jnp.concatenate(dkps, axis=1)
        dkc_ref[...] = jnp.concatenate(dkcs, axis=1)
        dvp_ref[...] = jnp.concatenate(dvps, axis=1)
        dvc_ref[...] = jnp.concatenate(dvcs, axis=1)
        ds_ref[...] += dsink

    w = KVH * AH
    blk = lambda width: pl.BlockSpec((WIN, width), lambda n: (n, 0))
    return _call(
        body, [q, k, k, v, v, sinks, dout], comm=comm, name="attn_bwd", grid=(NBLK,),
        in_specs=[blk(D)] + _kv_specs() + [pl.BlockSpec((1, QH), lambda n: (0, 0)), blk(D)],
        out_specs=[blk(D), blk(w), blk(w), blk(w), blk(w), pl.BlockSpec((1, QH), lambda n: (0, 0))],
        out_shape=[jax.ShapeDtypeStruct((S, D), f32)] + [jax.ShapeDtypeStruct((S, w), f32)] * 4 + [jax.ShapeDtypeStruct((1, QH), f32)],
        sem=("arbitrary",))


def kv_bwd(kv, pos, inv_freq, k_norm, dkp, dkc, dvp, dvc):
    w = KVH * AH

    def body(kv_ref, pos_ref, if_ref, g_ref, dkp_ref, dkc_ref, dvp_ref, dvc_ref, o_ref, dg_ref, db_ref):
        n = pl.program_id(0)

        @pl.when(n == 0)
        def _():
            dg_ref[...] = jnp.zeros_like(dg_ref)
            db_ref[...] = jnp.zeros_like(db_ref)

        inside = (n < NBLK - 1).astype(f32)
        dk = dkc_ref[...] + inside * dkp_ref[...]
        dv = dvc_ref[...] + inside * dvp_ref[...]
        cos, sin = _rope_tables(pos_ref[...], if_ref[...])
        dkpre, dg = _headnorm_rope_bwd(kv_ref[...], g_ref[...], cos, sin, dk, KVH)
        dkv = jnp.concatenate([dkpre, dv], axis=1)
        o_ref[...] = dkv.astype(o_ref.dtype)
        dg_ref[...] += dg
        db_ref[...] += jnp.sum(dkv, axis=0, keepdims=True)

    nxt = lambda n: (jnp.minimum(n + 1, NBLK - 1), 0)
    cur = lambda n: (n, 0)
    const = lambda n: (0, 0)
    return pl.pallas_call(
        body, name="kv_bwd", grid=(NBLK,),
        in_specs=[pl.BlockSpec((WIN, w), cur), pl.BlockSpec((WIN, 1), cur), pl.BlockSpec((1, AH // 2), const),
                  pl.BlockSpec((1, AH), const), pl.BlockSpec((WIN, w), nxt), pl.BlockSpec((WIN, w), cur),
                  pl.BlockSpec((WIN, w), nxt), pl.BlockSpec((WIN, w), cur)],
        out_specs=[pl.BlockSpec((WIN, 2 * w), cur), pl.BlockSpec((1, AH), const), pl.BlockSpec((1, 2 * w), const)],
        out_shape=[jax.ShapeDtypeStruct((S, 2 * w), MXU), jax.ShapeDtypeStruct((1, AH), f32), jax.ShapeDtypeStruct((1, 2 * w), f32)],
        compiler_params=_cparams(("arbitrary",)),
    )(kv, pos, inv_freq, k_norm, dkp, dkc, dvp, dvc)


def adamw(name, w, g, m, v):
    R, C = w.shape
    tr = _pick(R, (256, 128, 64, 32, 16, 8))
    tc = C if tr < R or C % 256 else 256

    def body(w_ref, g_ref, m_ref, v_ref, d_ref, nm_ref, nv_ref):
        g_ = g_ref[...]
        m_ = ADAM_B1 * m_ref[...] + (1.0 - ADAM_B1) * g_
        v_ = ADAM_B2 * v_ref[...] + (1.0 - ADAM_B2) * (g_ * g_)
        m_hat = m_ / (1.0 - ADAM_B1 ** ADAM_STEP)
        v_hat = v_ / (1.0 - ADAM_B2 ** ADAM_STEP)
        d_ref[...] = -ADAM_LR * (m_hat / (jnp.sqrt(v_hat) + ADAM_EPS) + ADAM_WD * w_ref[...])
        nm_ref[...] = m_
        nv_ref[...] = v_

    spec = pl.BlockSpec((tr, tc), lambda i, j: (i, j))
    return pl.pallas_call(
        body, name=name, grid=(R // tr, C // tc), in_specs=[spec] * 4, out_specs=[spec] * 3,
        out_shape=[jax.ShapeDtypeStruct((R, C), f32)] * 3, compiler_params=_cparams(("parallel", "parallel")),
    )(w, g, m, v)


def _me():
    return lax.axis_index("x"), lax.axis_index("y"), lax.axis_index("c")


def gather_comm(xs):
    n = len(xs)

    def parts(x_refs, o_refs, sems):
        send_sems, recv_sems, local_sems = sems
        x, y, c = _me()
        me, sibling = (x, y, c), (x, y, 1 - c)
        chips = [(1 - x, y), (x, 1 - y), (1 - x, 1 - y)]

        def copy(a, k, block, to, src=None):
            dst = o_refs[a].at[4 * block[0] + 2 * block[1] + block[2]]
            return pltpu.make_async_remote_copy(
                src_ref=dst if src is None else src, dst_ref=dst,
                send_sem=send_sems.at[7 * a + k], recv_sem=recv_sems.at[7 * a + k], device_id=to, device_id_type=MESH)

        mine = [pltpu.make_async_copy(x_refs[a], o_refs[a].at[4 * x + 2 * y + c], local_sems.at[a]) for a in range(n)]
        first = []
        for a in range(n):
            first.append(copy(a, 0, me, sibling, src=x_refs[a]))
            first += [copy(a, 1 + j, me, (*chip, c), src=x_refs[a]) for j, chip in enumerate(chips)]
        return copy, mine, first, me, sibling, chips, c

    def start(x_refs, o_refs, sems):
        _, mine, first, *_ = parts(x_refs, o_refs, sems)
        for cp in mine + first:
            cp.start()

    def finish(x_refs, o_refs, sems):
        copy, mine, first, me, sibling, chips, c = parts(x_refs, o_refs, sems)
        passed = []
        for j, chip in enumerate(chips):
            for a in range(n):
                copy(a, 1 + j, (*chip, c), me).wait_recv()
                cp = copy(a, 4 + j, (*chip, c), sibling)
                cp.start()
                passed.append(cp)
        for a in range(n):
            copy(a, 0, sibling, me).wait_recv()
            for j, chip in enumerate(chips):
                copy(a, 4 + j, (*chip, 1 - c), me).wait_recv()
        for cp in first + passed:
            cp.wait_send()
        for cp in mine:
            cp.wait()

    return Comm(xs, [jax.ShapeDtypeStruct((N_DEV,) + a.shape, a.dtype) for a in xs],
                [pltpu.SemaphoreType.DMA((7 * n,)), pltpu.SemaphoreType.DMA((7 * n,)), pltpu.SemaphoreType.DMA((n,))], start, finish)


def run_comm(name, comm):
    _call(lambda: None, [], name=name, grid=(1,), in_specs=[], out_specs=[], out_shape=[], comm=comm)
    return comm.results


def sibling_comm(gs):
    n = len(gs)

    def copies(g_refs, o_refs, sems):
        x, y, c = _me()
        return [pltpu.make_async_remote_copy(
            src_ref=g_refs[a].at[:, 1 - c], dst_ref=o_refs[a], send_sem=sems[0].at[a], recv_sem=sems[1].at[a],
            device_id=(x, y, 1 - c), device_id_type=MESH) for a in range(n)]

    def start(g_refs, o_refs, sems):
        for cp in copies(g_refs, o_refs, sems):
            cp.start()

    def finish(g_refs, o_refs, sems):
        for cp in copies(g_refs, o_refs, sems):
            cp.wait()

    return Comm(gs, [jax.ShapeDtypeStruct((4,) + g.shape[2:], g.dtype) for g in gs],
                [pltpu.SemaphoreType.DMA((n,)), pltpu.SemaphoreType.DMA((n,))], start, finish)


def chip_comm(ts):
    n = len(ts)

    def copies(t_refs, o_refs, sems):
        x, y, c = _me()
        chips = [(1 - x, y), (x, 1 - y), (1 - x, 1 - y)]
        return [pltpu.make_async_remote_copy(
            src_ref=t_refs[a].at[2 * px + py], dst_ref=o_refs[a].at[j],
            send_sem=sems[0].at[3 * a + j], recv_sem=sems[1].at[3 * a + j],
            device_id=(px, py, c), device_id_type=MESH) for j, (px, py) in enumerate(chips) for a in range(n)]

    def start(t_refs, o_refs, sems):
        for cp in copies(t_refs, o_refs, sems):
            cp.start()

    def finish(t_refs, o_refs, sems):
        for cp in copies(t_refs, o_refs, sems):
            cp.wait()

    return Comm(ts, [jax.ShapeDtypeStruct((3,) + t.shape[1:], t.dtype) for t in ts],
                [pltpu.SemaphoreType.DMA((3 * n,)), pltpu.SemaphoreType.DMA((3 * n,))], start, finish)


def _row_tile(rows):
    return _pick(rows, (512, 304, 256, 128))


def pair_add(name, g, r):
    _, _, R, C = g.shape
    tr = _row_tile(R)

    def body(c_ref, g_ref, r_ref, o_ref):
        o_ref[0] = (g_ref[0, 0].astype(f32) + r_ref[0].astype(f32)).astype(o_ref.dtype)

    return pl.pallas_call(
        body, name=name,
        grid_spec=pltpu.PrefetchScalarGridSpec(
            num_scalar_prefetch=1, grid=(4, R // tr),
            in_specs=[pl.BlockSpec((1, 1, tr, C), lambda p, i, c: (p, c[0], i, 0)),
                      pl.BlockSpec((1, tr, C), lambda p, i, c: (p, i, 0))],
            out_specs=pl.BlockSpec((1, tr, C), lambda p, i, c: (p, i, 0))),
        out_shape=jax.ShapeDtypeStruct((4, R, C), g.dtype),
        compiler_params=_cparams(("parallel", "parallel")),
    )(lax.axis_index("c").reshape(1).astype(jnp.int32), g, r)


def final_add(name, t, r):
    _, R, C = t.shape
    tr = _row_tile(R)

    def body(p_ref, t_ref, r_ref, o_ref):
        o_ref[...] = ((t_ref[0].astype(f32) + r_ref[0].astype(f32)) + r_ref[1].astype(f32)) + r_ref[2].astype(f32)

    chip = 2 * lax.axis_index("x") + lax.axis_index("y")
    return pl.pallas_call(
        body, name=name,
        grid_spec=pltpu.PrefetchScalarGridSpec(
            num_scalar_prefetch=1, grid=(R // tr,),
            in_specs=[pl.BlockSpec((1, tr, C), lambda i, p: (p[0], i, 0)),
                      pl.BlockSpec((3, tr, C), lambda i, p: (0, i, 0))],
            out_specs=pl.BlockSpec((tr, C), lambda i, p: (i, 0))),
        out_shape=jax.ShapeDtypeStruct((R, C), f32),
        compiler_params=_cparams(("parallel",)),
    )(chip.reshape(1).astype(jnp.int32), t, r)


class ReduceScatter:
    def __init__(self, tag, keys, grads):
        self.tag, self.keys, self.grads = tag, keys, grads
        self.send = [g.reshape((4, 2, g.shape[0] // N_DEV) + g.shape[1:]) for g in grads]

    def sibling(self):
        self.c1 = sibling_comm(self.send)
        return self.c1

    def chips(self):
        self.pairs = [pair_add(f"rs_pair_add_{self.tag}{i}", g, r) for i, (g, r) in enumerate(zip(self.send, self.c1.results))]
        self.c2 = chip_comm(self.pairs)
        return self.c2

    def sums(self):
        return {k: final_add(f"rs_final_add_{k}", t, r) for k, t, r in zip(self.keys, self.pairs, self.c2.results)}


IN_ROWS = {"z": (0, 2048), "xs": (2048, 4096), "B": (4096, 5120), "C": (5120, 6144)}
IN_COLS = 2 * INNER + 2 * GROUPS * NSTATE + HEADS


def sum_devices(g):
    def body(g_ref, o_ref):
        acc = g_ref[0]
        for i in range(1, N_DEV):
            acc = acc + g_ref[i]
        o_ref[...] = acc

    return pl.pallas_call(body, name="sum_devices", out_shape=jax.ShapeDtypeStruct(g.shape[1:], f32),
                          compiler_params=_cparams())(g)


def _pack(parts, unit, dtype, lead=()):
    flat = jnp.concatenate([p.reshape(lead + (-1,)).astype(dtype) for p in parts], axis=-1)
    n = flat.shape[-1]
    rows = -(-n // (unit * PACK_W)) * unit
    flat = jnp.pad(flat, [(0, 0)] * len(lead) + [(0, rows * PACK_W - n)])
    return flat.reshape(lead + (rows, PACK_W))


def _unpack(buf, shapes, lead=()):
    flat = buf.reshape(lead + (-1,))
    out, off = [], 0
    for shp in shapes:
        n = math.prod(shp)
        out.append(flat[..., off:off + n].reshape(lead + tuple(shp)))
        off += n
    return out


def _pad_lanes(a):
    return jnp.pad(a, [(0, 0)] * (a.ndim - 1) + [(0, LANES - a.shape[-1])])


_NN = (((1,), (0,)), ((), ()))
_NT = (((1,), (1,)), ((), ()))


def _mm(a, b, dims):
    return lax.dot_general(a.astype(MXU), b.astype(MXU), dims, preferred_element_type=f32)


def _ffn_fwd(tag, x, norm_g, w_inT, conv_w, conv_b, mid_comm=None):
    def in_fn(x_, g_, wT):
        h_ = _rms_fwd(x_, g_).astype(MXU)
        return h_, _mm(h_, wT[:FFN], _NT), _mm(h_, wT[FFN:], _NT)

    h, gate_pre, val = rowwise(f"{tag}_in", in_fn, [x], [norm_g, w_inT], [(D, MXU), (FFN, f32), (FFN, f32)], tile=128)
    (act,) = colwise(f"{tag}_mid", lambda gp, v_, w_, b_: (_silu(_conv(gp, w_, b_)) * v_,), [gate_pre, val], [conv_w, conv_b], [MXU],
                     comm=mid_comm)
    return act, (x, h, gate_pre, val, act)


def _ffn_mid_bwd(name, gate_pre, val, dact, conv_w, conv_b, comm=None):
    ct = 128

    def body(gp_ref, v_ref, da_ref, w_ref, b_ref, dgv_ref, dw_ref, db_ref):
        gp, v_, da, w_ = gp_ref[...], v_ref[...], da_ref[...], w_ref[...]
        taps = _taps(gp, CONV_F)
        gate = _conv(gp, w_, b_ref[...], taps)
        sg = _sigmoid(gate)
        dgp, dw, db = _conv_bwd(gp, w_, da * v_ * (sg * (1.0 + gate * (1.0 - sg))), taps)
        dgv_ref[0] = dgp.astype(dgv_ref.dtype)
        dgv_ref[1] = (da * (gate * sg)).astype(dgv_ref.dtype)
        dw_ref[...] = dw
        db_ref[...] = db

    col = pl.BlockSpec((S, ct), lambda j: (0, j))
    return _call(
        body, [gate_pre, val, dact, conv_w, conv_b], comm=comm, name=name, grid=(FFN // ct,),
        in_specs=[col, col, col, pl.BlockSpec((CONV_F, ct), lambda j: (0, j)), pl.BlockSpec((1, ct), lambda j: (0, j))],
        out_specs=[pl.BlockSpec((2, S, ct), lambda j: (0, 0, j)), pl.BlockSpec((CONV_F, ct), lambda j: (0, j)),
                   pl.BlockSpec((1, ct), lambda j: (0, j))],
        out_shape=[jax.ShapeDtypeStruct((2, S, FFN), MXU), jax.ShapeDtypeStruct((CONV_F, FFN), f32), jax.ShapeDtypeStruct((1, FFN), f32)],
        sem=("parallel",))


def _ffn_bwd(tag, layer, saved, norm_g, w_inT, conv_w, conv_b, w_down, dout, mid_comm=None):
    x, h, gate_pre, val, act = saved
    dact = matmul(f"{tag}_ddown", dout, w_down, "nt")
    g_down = matmul(f"{tag}_wdown", act, dout, "tn", out_dtype=MXU)
    dgv, g_cw, g_cb = _ffn_mid_bwd(f"{tag}_dmid", gate_pre, val, dact, conv_w, conv_b, comm=mid_comm)
    g_inT = matmul_tn_stacked(f"{tag}_win", dgv, h, MXU)

    def din_fn(dg_, dv_, x_, do_, g_, wT):
        dx, dg = _rms_bwd(x_, g_, _mm(dg_, wT[:FFN], _NN) + _mm(dv_, wT[FFN:], _NN))
        return do_ + dx, dg

    rs = ReduceScatter(tag, (f"f_inT{layer}", f"f_down{layer}"), [g_inT, g_down])
    dx, g_norm = rowwise(f"{tag}_din", din_fn, [(dgv, 0), (dgv, 1), x, dout], [norm_g, w_inT], [(D, f32)], [(1, D)],
                         tile=128, comm=rs.sibling())
    return dx, {f"f_norm{layer}": g_norm, f"f_conv_w{layer}": g_cw, f"f_conv_b{layer}": g_cb}, rs


def _land(W, keys, comm):
    for k, g in zip(keys, comm.results):
        W[k] = g.reshape(-1, g.shape[2])


def _local_step(x, pos, tgt, W, shards):
    G = {}
    gather = lambda *keys: gather_comm([shards[k] for k in keys])
    inv_freq = (ROPE_THETA ** (-jnp.arange(AH // 2, dtype=f32) / (AH // 2))).reshape(1, AH // 2)

    def in_fn(x_, g_, wT, wdtT):
        h_ = _rms_fwd(x_, g_).astype(MXU)
        return (h_,) + tuple(_mm(h_, wT[a:b], _NT) for a, b in IN_ROWS.values()) + (_mm(h_, wdtT, _NT),)

    pre = {}
    h0, z, pre["xs"], pre["B"], pre["C"], dt_pre = rowwise(
        "a_in", in_fn, [x], [W["a_norm"], W["inT"], W["in_dtT"]],
        [(D, MXU)] + [(b - a, f32) for a, b in IN_ROWS.values()] + [(LANES, f32)], tile=128)
    conv = {}
    for k in ("xs", "B", "C"):
        (conv[k],) = colwise(f"a_conv_{k}", lambda p_, w_, b_: (_silu(_conv(p_, w_, b_)),), [pre[k]], [W[f"cw_{k}"], W[f"cb_{k}"]], [f32])
    c = gather("a_out", "f_inT0", "f_down0")
    y, states = ssd_fwd(conv["xs"], conv["B"], conv["C"], dt_pre, W["dt_bias"], W["A_log"], W["D"], comm=c)
    _land(W, ("a_out", "f_inT0", "f_down0"), c)

    def gate_norm(y_, z_, g_):
        yg = y_ * _silu(z_)
        w = INNER // GROUPS
        return (jnp.concatenate([_rms_fwd(yg[:, w * i:w * (i + 1)], g_[:, w * i:w * (i + 1)]) for i in range(GROUPS)], axis=1),)

    def out_fn(y_, z_, x_, g_, w_):
        (gn_,) = gate_norm(y_, z_, g_)
        gn_ = gn_.astype(MXU)
        return gn_, x_ + _mm(gn_, w_, _NN)

    gn, x1 = rowwise("a_out", out_fn, [y, z, x], [W["a_gnorm"], W["a_out"]], [(INNER, MXU), (D, f32)], tile=128)

    c = gather("w_kv", "w_q", "w_o")
    act0, ffn0 = _ffn_fwd("f0", x1, W["f_norm0"], W["f_inT0"], W["f_cw0"], W["f_cb0"], mid_comm=c)
    _land(W, ("w_kv", "w_q", "w_o"), c)
    x2 = matmul("f0_down", act0, W["f_down0"], "nn", residual=x1)

    def qkv_fn(x_, gk, gb, wkv, bkv, wq, bq):
        kvn_, h2_ = _rms_fwd(x_, gk).astype(MXU), _rms_fwd(x_, gb).astype(MXU)
        return kvn_, h2_, _mm(kvn_, wkv, _NN) + bkv, _mm(h2_, wq, _NN) + bq

    kw = KVH * AH
    kvn, h2, kv, q_pre = rowwise("qkv_proj", qkv_fn, [x2], [W["kv_norm"], W["b_norm"], W["w_kv"], W["b_kv"], W["w_q"], W["b_q"]],
                                 [(D, MXU), (D, MXU), (2 * kw, f32), (D, f32)])

    def k_fwd(kv_, pos_, if_, g_):
        cos, sin = _rope_tables(pos_, if_)
        return _headnorm_rope_fwd(kv_[:, :kw], g_, cos, sin, KVH), kv_[:, kw:]

    k_rot, v_val = rowwise("k_rope", k_fwd, [kv, pos], [inv_freq, W["k_norm"]], [(kw, f32), (kw, f32)])

    def q_fwd(q_, pos_, if_, g_):
        cos, sin = _rope_tables(pos_, if_)
        return (_headnorm_rope_fwd(q_, g_, cos, sin, QH),)

    c = gather("f_down1")
    (q,) = rowwise("q_rope", q_fwd, [q_pre, pos], [inv_freq, W["q_norm"]], [(D, f32)], comm=c)
    _land(W, ("f_down1",), c)
    c = gather("f_inT1")
    att = attn_fwd(q, k_rot, v_val, W["sinks"], comm=c)
    _land(W, ("f_inT1",), c)
    x3 = matmul("o_proj", att, W["w_o"], "nn", bias=W["b_o"], residual=x2)

    act1, ffn1 = _ffn_fwd("f1", x3, W["f_norm1"], W["f_inT1"], W["f_cw1"], W["f_cb1"])

    def loss_fn(a_, x_, t_, w_):
        diff = x_ + _mm(a_, w_, _NN) - t_
        rows = jnp.sum(diff * diff, axis=1, keepdims=True) * (0.5 / D)
        return diff * (1.0 / D), jnp.sum(rows, axis=0, keepdims=True)

    dx4, loss = rowwise("f1_down_loss", loss_fn, [act1, x3, tgt], [W["f_down1"]], [(D, f32)], [(1, 1)])

    dx3, g, rs_f1 = _ffn_bwd("f1", 1, ffn1, W["f_norm1"], W["f_inT1"], W["f_cw1"], W["f_cb1"], W["f_down1"], dx4)
    G.update(g)

    datt = matmul("o_dproj", dx3, W["w_o"], "nt")
    g_wo = matmul("o_wproj", att, dx3, "tn", out_dtype=MXU)
    dq, dkp, dkc, dvp, dvc, G["sinks"] = attn_bwd(q, k_rot, v_val, W["sinks"], datt, comm=rs_f1.chips())

    def q_bwd(q_, pos_, dq_, dx_, if_, g_):
        cos, sin = _rope_tables(pos_, if_)
        dqp, dg = _headnorm_rope_bwd(q_, g_, cos, sin, dq_, QH)
        return dqp, dg, jnp.sum(dqp, axis=0, keepdims=True), jnp.sum(dx_, axis=0, keepdims=True)

    dq_pre, G["q_norm"], G["b_q"], G["b_o"] = rowwise("q_drope", q_bwd, [q_pre, pos, dq, dx3], [inv_freq, W["q_norm"]],
                                                      [(D, MXU)], [(1, AH), (1, D), (1, D)])
    g_wq = matmul("q_wproj", h2, dq_pre, "tn", out_dtype=MXU)
    dkv, G["k_norm"], G["b_kv"] = kv_bwd(kv, pos, inv_freq, W["k_norm"], dkp, dkc, dvp, dvc)
    g_wkv = matmul("kv_wproj", kvn, dkv, "tn", out_dtype=MXU)
    rs_att = ReduceScatter("att", ("w_kv", "w_q", "w_o"), [g_wkv, g_wq, g_wo])

    def x2_bwd(x_, dq_, dkv_, dx_, gb_, gk_, wq, wkv):
        d1, dgb = _rms_bwd(x_, gb_, _mm(dq_, wq, _NT))
        d2, dgk = _rms_bwd(x_, gk_, _mm(dkv_, wkv, _NT))
        return dx_ + d1 + d2, dgb, dgk

    dx2, G["b_norm"], G["kv_norm"] = rowwise("qkv_dproj", x2_bwd, [x2, dq_pre, dkv, dx3],
                                             [W["b_norm"], W["kv_norm"], W["w_q"], W["w_kv"]],
                                             [(D, f32)], [(1, D), (1, D)], comm=rs_att.sibling())

    dx1, g, rs_f0 = _ffn_bwd("f0", 0, ffn0, W["f_norm0"], W["f_inT0"], W["f_cw0"], W["f_cb0"], W["f_down0"], dx2,
                             mid_comm=rs_att.chips())
    G.update(g)

    dgn = matmul("a_dout", dx1, W["a_out"], "nt")
    rs_out = ReduceScatter("a_out", ("a_out",), [matmul("a_wout", gn, dx1, "tn", out_dtype=MXU)])

    def gate_norm_bwd(y_, z_, dgn_, g_):
        w = INNER // GROUPS
        sg = _sigmoid(z_)
        sz = z_ * sg
        yg = y_ * sz
        parts, dgs = [], []
        for i in range(GROUPS):
            dseg, dg = _rms_bwd(yg[:, w * i:w * (i + 1)], g_[:, w * i:w * (i + 1)], dgn_[:, w * i:w * (i + 1)])
            parts.append(dseg)
            dgs.append(dg)
        dyg = jnp.concatenate(parts, axis=1)
        return dyg * sz, dyg * y_ * (sg * (1.0 + z_ * (1.0 - sg))), jnp.concatenate(dgs, axis=1)

    dy, dz, G["a_gnorm"] = rowwise("a_dgnorm", gate_norm_bwd, [y, z, dgn], [W["a_gnorm"]], [(INNER, f32), (INNER, MXU)], [(1, INNER)],
                                   tile=128, comm=rs_out.sibling())
    dconv = {}
    dconv["xs"], dconv["B"], dconv["C"], ddt_pre, G["dt_bias"], G["A_log"], G["D"] = ssd_bwd(
        conv["xs"], conv["B"], conv["C"], dt_pre, W["dt_bias"], W["A_log"], W["D"], states, dy,
        comm=merge_comms([rs_f0.chips(), rs_out.chips()]))

    def conv_bwd(p_, do_, w_, b_):
        taps = _taps(p_, CONV_A)
        return _conv_bwd(p_, w_, do_ * _dsilu(_conv(p_, w_, b_, taps)), taps)

    g_in, dpre = [matmul("a_win_z", dz, h0, "tn", out_dtype=MXU)], {}
    for k in ("xs", "B", "C"):
        dpre[k], G[f"cw_{k}"], G[f"cb_{k}"] = colwise(f"a_dconv_{k}", conv_bwd, [pre[k], dconv[k]], [W[f"cw_{k}"], W[f"cb_{k}"]], [MXU], [CONV_A, 1])
        g_in.append(matmul(f"a_win_{k}", dpre[k], h0, "tn", out_dtype=MXU))
    g_in.append(matmul("a_win_dt", ddt_pre, h0, "tn", out_dtype=MXU)[:HEADS])
    rs_in = ReduceScatter("a_in", ("inT",), [jnp.concatenate(g_in, axis=0)])

    def x0_bwd(dz_, dxs_, db_, dc_, ddt_, x_, do_, g_, wT, wdtT):
        parts = zip((dz_, dxs_, db_, dc_), IN_ROWS.values())
        dh = sum(_mm(d_, wT[a:b], _NN) for d_, (a, b) in parts) + _mm(ddt_, wdtT, _NN)
        dx, dg = _rms_bwd(x_, g_, dh)
        return do_ + dx, dg

    dx, G["a_norm"] = rowwise("a_din", x0_bwd, [dz, dpre["xs"], dpre["B"], dpre["C"], ddt_pre, x, dx1],
                              [W["a_norm"], W["inT"], W["in_dtT"]], [(D, f32)], [(1, D)], tile=128, comm=rs_in.sibling())
    return loss, dx, G, [rs_f1, rs_att, rs_f0, rs_out, rs_in]


ROW_KEYS = ("inT", "a_out", "f_inT0", "f_down0", "w_kv", "w_q", "w_o", "f_inT1", "f_down1")


def _row_blocks(src):
    return {"inT": src["a_in_proj"][0].T, "a_out": src["a_out_proj"][0], "w_kv": src["w_kv"], "w_q": src["w_q"][0],
            "w_o": src["w_o"][0], "f_inT0": src["f_w_in"][0].T, "f_inT1": src["f_w_in"][1].T,
            "f_down0": src["f_w_down"][0], "f_down1": src["f_w_down"][1]}


def _from_row_blocks(rb):
    return {"a_in_proj": rb["inT"].T[None], "a_out_proj": rb["a_out"][None], "w_kv": rb["w_kv"], "w_q": rb["w_q"][None],
            "w_o": rb["w_o"][None], "f_w_in": jnp.stack([rb["f_inT0"].T, rb["f_inT1"].T]),
            "f_w_down": jnp.stack([rb["f_down0"], rb["f_down1"]])}


SMALL_SHARDED = ("a_norm", "a_conv_w", "a_conv_b", "a_gnorm", "f_conv_w")
REPLICATED = ("a_dt_bias", "a_A_log", "a_D", "kv_norm", "b_kv", "k_norm", "b_norm", "b_q", "q_norm", "sinks", "b_o",
              "f_norm", "f_conv_b")
ORDER = ("a_norm", "a_in_proj", "a_conv_w", "a_conv_b", "a_dt_bias", "a_A_log", "a_D", "a_gnorm", "a_out_proj", "kv_norm",
         "w_kv", "b_kv", "k_norm", "b_norm", "w_q", "b_q", "q_norm", "sinks", "w_o", "b_o", "f_norm", "f_w_in",
         "f_conv_w", "f_conv_b", "f_w_down")


def _gathered_to_whole(name, g):
    if name == "a_conv_w":
        return jnp.moveaxis(g[:, 0], 0, 1).reshape(g.shape[2], -1)
    if name in ("a_norm", "a_conv_b", "a_gnorm"):
        return g[:, 0].reshape(1, -1)
    if name == "f_conv_w":
        return jnp.moveaxis(g, 0, 2).reshape(g.shape[1], g.shape[2], -1)
    raise ValueError(name)


def _whole_to_shards(name, w):
    if name == "a_conv_w":
        return jnp.moveaxis(w.reshape(w.shape[0], N_DEV, -1), 1, 0)[:, None]
    if name in ("a_norm", "a_conv_b", "a_gnorm"):
        return w.reshape(N_DEV, 1, -1)
    if name == "f_conv_w":
        return jnp.moveaxis(w.reshape(w.shape[0], w.shape[1], N_DEV, -1), 2, 0)
    raise ValueError(name)


def _small_weights(whole):
    W = {}
    cw, cb = whole["a_conv_w"], whole["a_conv_b"]
    o = 0
    for k, n in (("xs", INNER), ("B", GROUPS * NSTATE), ("C", GROUPS * NSTATE)):
        W[f"cw_{k}"], W[f"cb_{k}"] = cw[:, o:o + n], cb[:, o:o + n]
        o += n
    W["a_norm"], W["a_gnorm"] = whole["a_norm"], whole["a_gnorm"]
    W["dt_bias"], W["A_log"], W["D"] = (_pad_lanes(whole[k]) for k in ("a_dt_bias", "a_A_log", "a_D"))
    W["kv_norm"], W["b_kv"], W["k_norm"] = whole["kv_norm"].reshape(1, -1), whole["b_kv"].reshape(1, -1), whole["k_norm"].reshape(1, -1)
    for k in ("b_norm", "b_q", "q_norm", "sinks", "b_o"):
        W[k] = whole[k]
    for i in range(2):
        W[f"f_norm{i}"] = whole["f_norm"][i:i + 1]
        W[f"f_cw{i}"], W[f"f_cb{i}"] = whole["f_conv_w"][i], whole["f_conv_b"][i:i + 1]
    return W


def _small_grads(G, shapes):
    nh = HEADS
    out = {
        "a_conv_w": jnp.concatenate([G["cw_xs"], G["cw_B"], G["cw_C"]], axis=1),
        "a_conv_b": jnp.concatenate([G["cb_xs"], G["cb_B"], G["cb_C"]], axis=1),
        "a_norm": G["a_norm"], "a_gnorm": G["a_gnorm"],
        "a_dt_bias": G["dt_bias"][:, :nh], "a_A_log": G["A_log"][:, :nh], "a_D": G["D"][:, :nh],
        "kv_norm": G["kv_norm"], "b_kv": G["b_kv"], "k_norm": G["k_norm"], "b_norm": G["b_norm"],
        "b_q": G["b_q"], "q_norm": G["q_norm"], "sinks": G["sinks"], "b_o": G["b_o"],
        "f_norm": jnp.concatenate([G["f_norm0"], G["f_norm1"]], axis=0),
        "f_conv_w": jnp.stack([G["f_conv_w0"], G["f_conv_w1"]]),
        "f_conv_b": jnp.concatenate([G["f_conv_b0"], G["f_conv_b1"]], axis=0),
    }
    return {k: val.reshape(shapes[k]) if k in shapes else val for k, val in out.items()}


def kernel(x, positions, a_norm, a_in_proj, a_conv_w, a_conv_b, a_dt_bias, a_A_log, a_D, a_gnorm, a_out_proj, kv_norm, w_kv, b_kv, k_norm, b_norm, w_q, b_q, q_norm, sinks, w_o, b_o, f_norm, f_w_in, f_conv_w, f_conv_b, f_w_down, loss_target, m_a_norm, m_a_in_proj, m_a_conv_w, m_a_conv_b, m_a_dt_bias, m_a_A_log, m_a_D, m_a_gnorm, m_a_out_proj, m_kv_norm, m_w_kv, m_b_kv, m_k_norm, m_b_norm, m_w_q, m_b_q, m_q_norm, m_sinks, m_w_o, m_b_o, m_f_norm, m_f_w_in, m_f_conv_w, m_f_conv_b, m_f_w_down, v_a_norm, v_a_in_proj, v_a_conv_w, v_a_conv_b, v_a_dt_bias, v_a_A_log, v_a_D, v_a_gnorm, v_a_out_proj, v_kv_norm, v_w_kv, v_b_kv, v_k_norm, v_b_norm, v_w_q, v_b_q, v_q_norm, v_sinks, v_w_o, v_b_o, v_f_norm, v_f_w_in, v_f_conv_w, v_f_conv_b, v_f_w_down):
    given = dict(locals())
    w_in = {n: given[n] for n in ORDER}
    m_in = {n: given["m_" + n] for n in ORDER}
    v_in = {n: given["v_" + n] for n in ORDER}
    dev = 4 * lax.axis_index("x") + 2 * lax.axis_index("y") + lax.axis_index("c")

    w2, m2, v2 = _row_blocks(w_in), _row_blocks(m_in), _row_blocks(v_in)
    small_pack = _pack([w_in[n] for n in SMALL_SHARDED], 8, f32)
    shards = {k: w2[k].astype(MXU) for k in ROW_KEYS}
    in_all, small_all = run_comm("ag_head", gather_comm([shards["inT"], small_pack]))
    whole = {n: w_in[n] for n in REPLICATED}
    for n, g in zip(SMALL_SHARDED, _unpack(small_all, [w_in[n].shape for n in SMALL_SHARDED], lead=(N_DEV,))):
        whole[n] = _gathered_to_whole(n, g)
    W = _small_weights(whole)
    W["inT"] = in_all.reshape(-1, D)
    W["in_dtT"] = jnp.pad(W["inT"][IN_COLS - HEADS:], ((0, LANES - HEADS), (0, 0)))

    loss, dx, G, scatters = _local_step(x[0], positions.reshape(S, 1).astype(f32), loss_target[0], W, shards)
    grads = _small_grads(G, {n: whole[n].shape for n in REPLICATED})

    small_names = SMALL_SHARDED + REPLICATED
    small_part = _pack([grads[n] for n in small_names], 8, f32)
    small_gather = gather_comm([small_part])
    run_comm("rs_tail", merge_comms([scatters[-1].chips(), small_gather]))
    sums = {}
    for rs in scatters:
        sums.update(rs.sums())
    g_out = _from_row_blocks(sums)
    small_sum = sum_devices(small_gather.results[0])
    for n, g in zip(small_names, _unpack(small_sum, [grads[n].shape for n in small_names])):
        if n in SMALL_SHARDED:
            g_out[n] = lax.dynamic_index_in_dim(_whole_to_shards(n, g), dev, axis=0, keepdims=False)
        else:
            g_out[n] = g.reshape(w_in[n].shape)

    stepped = {k: adamw(f"adamw_{k}", w2[k], sums[k], m2[k], v2[k]) for k in ROW_KEYS}
    delta, new_m, new_v = (_from_row_blocks({k: stepped[k][i] for k in ROW_KEYS}) for i in range(3))
    packs = [_pack([src[n] for n in small_names], 8, f32) for src in (w_in, g_out, m_in, v_in)]
    outs = adamw("adamw_small", *packs)
    for dst, buf in zip((delta, new_m, new_v), outs):
        for n, a in zip(small_names, _unpack(buf, [w_in[n].shape for n in small_names])):
            dst[n] = a

    loss_all = lax.psum(loss[0, 0], AXES)
    return (loss_all, dx[None], *[g_out[n] for n in ORDER], *[delta[n] for n in ORDER],
            *[new_m[n] for n in ORDER], *[new_v[n] for n in ORDER])
```

```python
import functools
import math

import jax
import jax.numpy as jnp
from jax import lax
from jax.experimental import pallas as pl
from jax.experimental.pallas import tpu as pltpu

f32 = jnp.float32
bf16 = jnp.bfloat16
MXU = bf16

N_DEV = 8
S = 2048
D = 1024
EPS = 1e-5
INNER = 2048
HEADS = 32
HP = 64
GROUPS = 8
HPG = HEADS // GROUPS
NSTATE = 128
CONV_A = 4
CHUNK = 256
NCHUNK = S // CHUNK
AH = 64
QH = 16
KVH = 4
QPK = QH // KVH
WIN = 128
NBLK = S // WIN
ROPE_THETA = 10000.0
FFN = 2816
CONV_F = 3
LANES = 128
PACK_W = 1024
BIG_TILE = 496
VMEM_LIMIT = 56 * 1024 * 1024

ADAM_LR, ADAM_B1, ADAM_B2, ADAM_EPS, ADAM_WD, ADAM_STEP = 0.001, 0.9, 0.999, 1e-08, 0.01, 10

MESH = pl.DeviceIdType.MESH
AXES = ("x", "y", "c")


def _cparams(sem=None):
    return pltpu.CompilerParams(dimension_semantics=sem, vmem_limit_bytes=VMEM_LIMIT)


def _pick(n, cands):
    for c in cands:
        if n % c == 0:
            return c
    return n


class Comm:
    def __init__(self, ins, out_shapes, sems, start, finish):
        self.ins, self.out_shapes, self.sems, self.start, self.finish = list(ins), list(out_shapes), list(sems), start, finish
        self.results, self.children = None, ()

    def set_results(self, res):
        self.results, o = list(res), 0
        for ch in self.children:
            ch.set_results(res[o:o + len(ch.out_shapes)])
            o += len(ch.out_shapes)


def merge_comms(comms):
    def each(fn_name, ins, outs, sems):
        i = o = s = 0
        for c in comms:
            getattr(c, fn_name)(ins[i:i + len(c.ins)], outs[o:o + len(c.out_shapes)], sems[s:s + len(c.sems)])
            i, o, s = i + len(c.ins), o + len(c.out_shapes), s + len(c.sems)

    merged = Comm([a for c in comms for a in c.ins], [a for c in comms for a in c.out_shapes], [a for c in comms for a in c.sems],
                  functools.partial(each, "start"), functools.partial(each, "finish"))
    merged.children = tuple(comms)
    return merged


def _call(body, args, *, name, grid, in_specs, out_specs, out_shape, scratch=(), sem=None, comm=None):
    if comm is None:
        return pl.pallas_call(body, name=name, grid=grid, in_specs=list(in_specs), out_specs=list(out_specs),
                              out_shape=list(out_shape), scratch_shapes=list(scratch), compiler_params=_cparams(sem))(*args)
    n_in, n_out, n_scr, c_in, c_out = len(in_specs), len(out_shape), len(scratch), len(comm.ins), len(comm.out_shapes)
    any_spec = pl.BlockSpec(memory_space=pl.ANY)

    def outer(*refs):
        ins, c_ins = refs[:n_in], refs[n_in:n_in + c_in]
        o = n_in + c_in
        outs, c_outs = refs[o:o + n_out], refs[o + n_out:o + n_out + c_out]
        o += n_out + c_out
        scr, c_sems = refs[o:o + n_scr], refs[o + n_scr:]
        ids = [pl.program_id(i) for i in range(len(grid))]
        first = functools.reduce(jnp.logical_and, [i == 0 for i in ids])
        last = functools.reduce(jnp.logical_and, [i == g - 1 for i, g in zip(ids, grid)])

        @pl.when(first)
        def _():
            comm.start(c_ins, c_outs, c_sems)

        body(*ins, *outs, *scr)

        @pl.when(last)
        def _():
            comm.finish(c_ins, c_outs, c_sems)

    res = pl.pallas_call(
        outer, name=name, grid=grid, in_specs=list(in_specs) + [any_spec] * c_in,
        out_specs=list(out_specs) + [any_spec] * c_out, out_shape=list(out_shape) + comm.out_shapes,
        scratch_shapes=list(scratch) + comm.sems, compiler_params=_cparams(("arbitrary",) * len(grid)),
    )(*args, *comm.ins)
    comm.set_results(res[n_out:])
    return res[:n_out]


def matmul(name, a, b, mode, out_dtype=f32, bias=None, residual=None, b_rows=None, a_lead=None):
    b_idx, b_shape = (0, b.shape) if b_rows is None else (b_rows[0], (b_rows[1], b.shape[1]))
    a_shape = a.shape if a_lead is None else a.shape[1:]
    if mode == "nn":
        (M, K), (K2, N) = a_shape, b_shape
    elif mode == "nt":
        (M, K), (N, K2) = a_shape, b_shape
    else:
        (K, M), (K2, N) = a_shape, b_shape
    assert K == K2, (name, a_shape, b_shape)
    if mode == "tn":
        tm, tn = M, _pick(N, (512, 256, 128) if M <= 1024 else (256, 128))
        a_spec = pl.BlockSpec((K, M), lambda j: (0, 0))
        b_spec = pl.BlockSpec((K, tn), lambda j: (0, j))
        dims = (((0,), (0,)), ((), ()))
        grid, o_map, row_map = (N // tn,), (lambda j: (0, j)), (lambda j: (0, j))
    else:
        tm, tn = (256 if N >= 2048 else 512), N
        a_spec = pl.BlockSpec((tm, K), lambda i: (i, 0)) if a_lead is None else pl.BlockSpec((None, tm, K), lambda i: (a_lead, i, 0))
        b_spec = pl.BlockSpec(b_shape, lambda i: (b_idx, 0))
        dims = (((1,), (0,)), ((), ())) if mode == "nn" else (((1,), (1,)), ((), ()))
        grid, o_map, row_map = (M // tm,), (lambda i: (i, 0)), (lambda i: (0, 0))
    ins, in_specs = [a, b], [a_spec, b_spec]
    if bias is not None:
        ins.append(bias)
        in_specs.append(pl.BlockSpec((1, tn), row_map))
    if residual is not None:
        ins.append(residual)
        in_specs.append(pl.BlockSpec((tm, tn), o_map))
    has_bias, has_res = bias is not None, residual is not None

    def body(a_ref, b_ref, *rest):
        rest = list(rest)
        bias_ref = rest.pop(0) if has_bias else None
        res_ref = rest.pop(0) if has_res else None
        (o_ref,) = rest
        r = lax.dot_general(a_ref[...].astype(MXU), b_ref[...].astype(MXU), dims, preferred_element_type=f32)
        if has_bias:
            r = r + bias_ref[...]
        if has_res:
            r = r + res_ref[...]
        o_ref[...] = r.astype(out_dtype)

    return pl.pallas_call(
        body, name=name, grid=grid, in_specs=in_specs,
        out_specs=pl.BlockSpec((tm, tn), o_map),
        out_shape=jax.ShapeDtypeStruct((M, N), out_dtype),
        compiler_params=_cparams(("parallel",)),
    )(*ins)


def matmul_tn_stacked(name, a, b, out_dtype):
    R, K, M = a.shape
    N = b.shape[1]
    tn = _pick(N, (256, 128))

    def body(a_ref, b_ref, o_ref):
        o_ref[0] = lax.dot_general(a_ref[0].astype(MXU), b_ref[...].astype(MXU), (((0,), (0,)), ((), ())),
                                   preferred_element_type=f32).astype(out_dtype)

    out = pl.pallas_call(
        body, name=name, grid=(R, N // tn),
        in_specs=[pl.BlockSpec((1, K, M), lambda r, j: (r, 0, 0)), pl.BlockSpec((K, tn), lambda r, j: (0, j))],
        out_specs=pl.BlockSpec((1, M, tn), lambda r, j: (r, 0, j)),
        out_shape=jax.ShapeDtypeStruct((R, M, N), out_dtype),
        compiler_params=_cparams(("parallel", "parallel")),
    )(a, b)
    return out.reshape(R * M, N)


def rowwise(name, fn, rows, pars, outs, accs=(), tile=256, comm=None):
    n_in, n_out = len(rows) + len(pars), len(outs)
    in_specs = [pl.BlockSpec((None, tile, r[0].shape[2]), functools.partial(lambda i, lead: (lead, i, 0), lead=r[1]))
                if isinstance(r, tuple) else pl.BlockSpec((tile, r.shape[1]), lambda i: (i, 0)) for r in rows]
    rows = [r[0] if isinstance(r, tuple) else r for r in rows]
    in_specs += [pl.BlockSpec(p.shape, lambda i: (0, 0)) for p in pars]
    out_specs = [pl.BlockSpec((tile, c), lambda i: (i, 0)) for c, _ in outs]
    out_specs += [pl.BlockSpec(shp, lambda i: (0, 0)) for shp in accs]
    out_shape = [jax.ShapeDtypeStruct((S, c), dt) for c, dt in outs]
    out_shape += [jax.ShapeDtypeStruct(shp, f32) for shp in accs]

    def body(*refs):
        res = fn(*[r[...] for r in refs[:n_in]])
        o_refs = refs[n_in:n_in + n_out]
        a_refs = refs[n_in + n_out:]
        for ref, val in zip(o_refs, res[:n_out]):
            ref[...] = val.astype(ref.dtype)
        if a_refs:
            @pl.when(pl.program_id(0) == 0)
            def _():
                for ref in a_refs:
                    ref[...] = jnp.zeros_like(ref)
            for ref, val in zip(a_refs, res[n_out:]):
                ref[...] += val

    return _call(body, [*rows, *pars], name=name, grid=(S // tile,), in_specs=in_specs, out_specs=out_specs,
                 out_shape=out_shape, sem=("arbitrary",) if accs else ("parallel",), comm=comm)


def colwise(name, fn, cols, pars, outs, pouts=(), ct=128, comm=None):
    C = cols[0].shape[1]
    n_in, n_out = len(cols) + len(pars), len(outs)
    in_specs = [pl.BlockSpec((S, ct), lambda j: (0, j)) for _ in cols]
    in_specs += [pl.BlockSpec((p.shape[0], ct), lambda j: (0, j)) for p in pars]
    out_specs = [pl.BlockSpec((S, ct), lambda j: (0, j)) for _ in outs]
    out_specs += [pl.BlockSpec((r, ct), lambda j: (0, j)) for r in pouts]
    out_shape = [jax.ShapeDtypeStruct((S, C), dt) for dt in outs]
    out_shape += [jax.ShapeDtypeStruct((r, C), f32) for r in pouts]

    def body(*refs):
        res = fn(*[r[...] for r in refs[:n_in]])
        for ref, val in zip(refs[n_in:], res):
            ref[...] = val.astype(ref.dtype)

    return _call(body, [*cols, *pars], name=name, grid=(C // ct,), in_specs=in_specs, out_specs=out_specs,
                 out_shape=out_shape, sem=("parallel",), comm=comm)


def _sigmoid(x):
    return 0.5 * jnp.tanh(0.5 * x) + 0.5


def _silu(x):
    return x * _sigmoid(x)


def _dsilu(x):
    sg = _sigmoid(x)
    return sg * (1.0 + x * (1.0 - sg))


def _softplus(x):
    return jnp.maximum(x, 0.0) + jnp.log(1.0 + jnp.exp(-jnp.abs(x)))


def _rms_fwd(x, g):
    r = lax.rsqrt(jnp.mean(x * x, axis=-1, keepdims=True) + EPS)
    return x * r * g


def _rms_bwd(x, g, dh):
    r = lax.rsqrt(jnp.mean(x * x, axis=-1, keepdims=True) + EPS)
    xh = x * r
    dxh = dh * g
    dx = r * (dxh - xh * jnp.mean(dxh * xh, axis=-1, keepdims=True))
    return dx, jnp.sum(dh * xh, axis=0, keepdims=True)


def _taps(x, width):
    row = lax.broadcasted_iota(jnp.int32, x.shape, 0)
    return [jnp.where(row >= s, pltpu.roll(x, s, 0), 0.0) for s in range(width - 1, 0, -1)] + [x]


def _conv(x, w, b, taps=None):
    width = w.shape[0]
    taps = _taps(x, width) if taps is None else taps
    out = b + w[0:1, :] * taps[0]
    for k in range(1, width):
        out = out + w[k:k + 1, :] * taps[k]
    return out


def _conv_bwd(x, w, dc, taps=None):
    width, n = w.shape[0], x.shape[0]
    taps = _taps(x, width) if taps is None else taps
    row = lax.broadcasted_iota(jnp.int32, x.shape, 0)
    dx = w[width - 1:width, :] * dc
    for k in range(width - 1):
        s = width - 1 - k
        dx = dx + w[k:k + 1, :] * jnp.where(row < n - s, pltpu.roll(dc, n - s, 0), 0.0)
    dw = jnp.concatenate([jnp.sum(dc * t, axis=0, keepdims=True) for t in taps], axis=0)
    return dx, dw, jnp.sum(dc, axis=0, keepdims=True)


def _rope_tables(pos, inv_freq):
    ang = pos * inv_freq
    return jnp.cos(ang), jnp.sin(ang)


def _headnorm_rope_fwd(x, g, cos, sin, heads):
    half = AH // 2
    outs = []
    for h in range(heads):
        seg = x[:, AH * h:AH * (h + 1)]
        n = _rms_fwd(seg, g)
        n1, n2 = n[:, :half], n[:, half:]
        outs += [n1 * cos - n2 * sin, n2 * cos + n1 * sin]
    return jnp.concatenate(outs, axis=1)


def _headnorm_rope_bwd(x, g, cos, sin, dout, heads):
    half = AH // 2
    dxs, dg = [], jnp.zeros((1, AH), f32)
    for h in range(heads):
        seg = x[:, AH * h:AH * (h + 1)]
        d = dout[:, AH * h:AH * (h + 1)]
        d1, d2 = d[:, :half], d[:, half:]
        dn = jnp.concatenate([d1 * cos + d2 * sin, d2 * cos - d1 * sin], axis=1)
        dx, dgh = _rms_bwd(seg, g, dn)
        dxs.append(dx)
        dg = dg + dgh
    return jnp.concatenate(dxs, axis=1), dg


def _ssd_prep(dt_pre, dt_bias, a_log, dt_s, acum_s, acumT_s):
    dt = _softplus(dt_pre + dt_bias)
    a = dt * (-jnp.exp(a_log))
    row = lax.broadcasted_iota(jnp.int32, (CHUNK, CHUNK), 0)
    col = lax.broadcasted_iota(jnp.int32, (CHUNK, CHUNK), 1)
    dt_s[...] = dt
    acum_s[...] = jnp.dot((col <= row).astype(f32), a, precision=lax.Precision.HIGHEST, preferred_element_type=f32)
    acumT_s[...] = lax.dot_general(a, (row <= col).astype(f32), (((0,), (0,)), ((), ())),
                                   precision=lax.Precision.HIGHEST, preferred_element_type=f32)


def _head_cols(h, dt_s, acum_s, acumT_s):
    lane = lax.broadcasted_iota(jnp.int32, (1, LANES), 1)
    oh_l = (lane == h).astype(f32)
    sub = lax.broadcasted_iota(jnp.int32, (LANES, 1), 0)
    oh_s = (sub == h).astype(f32)
    dt_h = jnp.sum(dt_s[...] * oh_l, axis=1, keepdims=True)
    ac_h = jnp.sum(acum_s[...] * oh_l, axis=1, keepdims=True)
    acr_h = jnp.sum(acumT_s[...] * oh_s, axis=0, keepdims=True)
    return oh_l, dt_h, ac_h, acr_h


def ssd_fwd(xs, Bm, Cm, dt_pre, dt_bias, a_log, d_skip, comm=None):
    def body(xs_ref, b_ref, c_ref, dtp_ref, bias_ref, alog_ref, d_ref, y_ref, st_ref, state, dt_s, acum_s, acumT_s):
        c, g = pl.program_id(0), pl.program_id(1)

        @pl.when(g == 0)
        def _():
            _ssd_prep(dtp_ref[...], bias_ref[...], alog_ref[...], dt_s, acum_s, acumT_s)

        row = lax.broadcasted_iota(jnp.int32, (CHUNK, CHUNK), 0)
        col = lax.broadcasted_iota(jnp.int32, (CHUNK, CHUNK), 1)
        causal = col <= row
        Bb, Cb = b_ref[...], c_ref[...]
        cb = lax.dot_general(Cb.astype(MXU), Bb.astype(MXU), (((1,), (1,)), ((), ())), preferred_element_type=f32)
        xs_blk = xs_ref[...]

        @pl.when(c == 0)
        def _():
            for j in range(HPG):
                state[g * HPG + j] = jnp.zeros((NSTATE, HP), f32)

        prevs = [state[g * HPG + j] for j in range(HPG)]
        y_off_all = jnp.dot(Cb.astype(MXU), jnp.concatenate(prevs, axis=1).astype(MXU), preferred_element_type=f32)
        ys, xds, e_ends = [], [], []
        for j in range(HPG):
            oh_l, dt_h, ac_h, acr_h = _head_cols(g * HPG + j, dt_s, acum_s, acumT_s)
            decay = jnp.exp(jnp.where(causal, ac_h - acr_h, -1e30))
            w = (cb * decay).astype(MXU)
            xs_h = xs_blk[:, HP * j:HP * (j + 1)]
            xd = xs_h * dt_h
            y_diag = jnp.dot(w, xd.astype(MXU), preferred_element_type=f32)
            y_off = y_off_all[:, HP * j:HP * (j + 1)] * jnp.exp(ac_h)
            d_h = jnp.sum(d_ref[...] * oh_l, axis=1, keepdims=True)
            ys.append(y_diag + y_off + xs_h * d_h)
            a_end = ac_h[CHUNK - 1:CHUNK, :]
            xds.append(xd * jnp.exp(a_end - ac_h))
            e_ends.append(jnp.exp(a_end))
        s_c = lax.dot_general(Bb.astype(MXU), jnp.concatenate(xds, axis=1).astype(MXU), (((0,), (0,)), ((), ())),
                              preferred_element_type=f32)
        for j in range(HPG):
            st_ref[0, j] = prevs[j]
            state[g * HPG + j] = prevs[j] * e_ends[j] + s_c[:, HP * j:HP * (j + 1)]
        y_ref[...] = jnp.concatenate(ys, axis=1)

    par = pl.BlockSpec((1, LANES), lambda c, g: (0, 0))
    return _call(
        body, [xs, Bm, Cm, dt_pre, dt_bias, a_log, d_skip], comm=comm, name="ssd_fwd", grid=(NCHUNK, GROUPS),
        in_specs=[pl.BlockSpec((CHUNK, HPG * HP), lambda c, g: (c, g)),
                  pl.BlockSpec((CHUNK, NSTATE), lambda c, g: (c, g)),
                  pl.BlockSpec((CHUNK, NSTATE), lambda c, g: (c, g)),
                  pl.BlockSpec((CHUNK, LANES), lambda c, g: (c, 0)), par, par, par],
        out_specs=[pl.BlockSpec((CHUNK, HPG * HP), lambda c, g: (c, g)),
                   pl.BlockSpec((1, HPG, NSTATE, HP), lambda c, g: (c, g, 0, 0))],
        out_shape=[jax.ShapeDtypeStruct((S, INNER), f32), jax.ShapeDtypeStruct((NCHUNK, HEADS, NSTATE, HP), f32)],
        scratch=[pltpu.VMEM((HEADS, NSTATE, HP), f32), pltpu.VMEM((CHUNK, LANES), f32),
                 pltpu.VMEM((CHUNK, LANES), f32), pltpu.VMEM((LANES, CHUNK), f32)],
        sem=("arbitrary", "arbitrary"))


def ssd_bwd(xs, Bm, Cm, dt_pre, dt_bias, a_log, d_skip, states, dy, comm=None):
    rev = lambda c: NCHUNK - 1 - c

    def body(xs_ref, b_ref, c_ref, dtp_ref, bias_ref, alog_ref, d_ref, st_ref, dy_ref,
             dxs_ref, db_ref, dc_ref, ddt_ref, dbias_ref, dalog_ref, dd_ref,
             dstate, dt_s, acum_s, acumT_s, dacum_s, ddt_s, da_s):
        c, g = pl.program_id(0), pl.program_id(1)

        @pl.when(g == 0)
        def _():
            _ssd_prep(dtp_ref[...], bias_ref[...], alog_ref[...], dt_s, acum_s, acumT_s)
            dacum_s[...] = jnp.zeros_like(dacum_s)
            ddt_s[...] = jnp.zeros_like(ddt_s)

        @pl.when((c == 0) & (g == 0))
        def _():
            da_s[...] = jnp.zeros_like(da_s)
            dd_ref[...] = jnp.zeros_like(dd_ref)
            dbias_ref[...] = jnp.zeros_like(dbias_ref)
            dalog_ref[...] = jnp.zeros_like(dalog_ref)

        row = lax.broadcasted_iota(jnp.int32, (CHUNK, CHUNK), 0)
        col = lax.broadcasted_iota(jnp.int32, (CHUNK, CHUNK), 1)
        sub_l = lax.broadcasted_iota(jnp.int32, (CHUNK, 1), 0)
        last = (sub_l == CHUNK - 1).astype(f32)
        nt = (((1,), (1,)), ((), ()))
        tn = (((0,), (0,)), ((), ()))
        Bb, Cb = b_ref[...], c_ref[...]
        Bm_, Cm_ = Bb.astype(MXU), Cb.astype(MXU)
        cb = lax.dot_general(Cm_, Bm_, nt, preferred_element_type=f32)
        bc = lax.dot_general(Bm_, Cm_, nt, preferred_element_type=f32)
        xs_blk, dy_blk = xs_ref[...], dy_ref[...]
        dxs, dB, dC = [], jnp.zeros((CHUNK, NSTATE), f32), jnp.zeros((CHUNK, NSTATE), f32)
        for j in range(HPG):
            h = g * HPG + j
            oh_l, dt_h, ac_h, acr_h = _head_cols(h, dt_s, acum_s, acumT_s)

            @pl.when(c == 0)
            def _():
                dstate[h] = jnp.zeros((NSTATE, HP), f32)

            dnext = dstate[h]
            prev = st_ref[0, j]
            lm = jnp.exp(jnp.where(col <= row, ac_h - acr_h, -1e30))
            lmT = jnp.exp(jnp.where(row <= col, acr_h - ac_h, -1e30))
            xs_h = xs_blk[:, HP * j:HP * (j + 1)]
            dy_h = dy_blk[:, HP * j:HP * (j + 1)]
            xd = xs_h * dt_h
            xdm, dym = xd.astype(MXU), dy_h.astype(MXU)
            ea = jnp.exp(ac_h)
            a_end = ac_h[CHUNK - 1:CHUNK, :]
            e_end = jnp.exp(a_end)
            dte = jnp.exp(a_end - ac_h)
            dnm, pvm = dnext.astype(MXU), prev.astype(MXU)
            bd = jnp.dot(Bm_, dnm, preferred_element_type=f32)
            dxd = jnp.dot((bc * lmT).astype(MXU), dym, preferred_element_type=f32) + dte * bd
            dw = lax.dot_general(dym, xdm, nt, preferred_element_type=f32)
            dwT = lax.dot_general(xdm, dym, nt, preferred_element_type=f32)
            dcb = dw * lm
            dbc = dwT * lmT
            eady = (ea * dy_h).astype(MXU)
            dC = dC + jnp.dot(dcb.astype(MXU), Bm_, preferred_element_type=f32) \
                + lax.dot_general(eady, pvm, nt, preferred_element_type=f32)
            dB = dB + jnp.dot(dbc.astype(MXU), Cm_, preferred_element_type=f32) \
                + dte * lax.dot_general(xdm, dnm, nt, preferred_element_type=f32)
            dstate[h] = lax.dot_general(Cm_, eady, tn, preferred_element_type=f32) + e_end * dnext
            r1 = jnp.sum(dcb * cb, axis=1, keepdims=True)
            r2 = jnp.sum(dbc * bc, axis=1, keepdims=True)
            y_off = jnp.dot(Cm_, pvm, preferred_element_type=f32) * ea
            t3 = jnp.sum(dy_h * y_off, axis=1, keepdims=True)
            t4 = jnp.sum(bd * xd, axis=1, keepdims=True) * dte
            end_extra = jnp.sum(t4, axis=0, keepdims=True) + e_end * jnp.sum(jnp.sum(prev * dnext, axis=1, keepdims=True), axis=0, keepdims=True)
            dacum_h = r1 - r2 + t3 - t4 + last * end_extra
            dacum_s[...] += dacum_h * oh_l
            ddt_s[...] += jnp.sum(dxd * xs_h, axis=1, keepdims=True) * oh_l
            d_h = jnp.sum(d_ref[...] * oh_l, axis=1, keepdims=True)
            dxs.append(dxd * dt_h + dy_h * d_h)
            dd_ref[...] += oh_l * jnp.sum(jnp.sum(dy_h * xs_h, axis=1, keepdims=True), axis=0, keepdims=True)
        dxs_ref[...] = jnp.concatenate(dxs, axis=1)
        db_ref[...] = dB
        dc_ref[...] = dC

        @pl.when(g == GROUPS - 1)
        def _():
            a_row = -jnp.exp(alog_ref[...])
            da = jnp.dot((row <= col).astype(f32), dacum_s[...], precision=lax.Precision.HIGHEST, preferred_element_type=f32)
            da_s[...] += jnp.sum(da * dt_s[...], axis=0, keepdims=True)
            z = dtp_ref[...] + bias_ref[...]
            ddt_pre = (ddt_s[...] + da * a_row) * _sigmoid(z)
            ddt_ref[...] = ddt_pre.astype(ddt_ref.dtype)
            dbias_ref[...] += jnp.sum(ddt_pre, axis=0, keepdims=True)

            @pl.when(c == NCHUNK - 1)
            def _():
                dalog_ref[...] = da_s[...] * a_row

    par = pl.BlockSpec((1, LANES), lambda c, g: (0, 0))
    return _call(
        body, [xs, Bm, Cm, dt_pre, dt_bias, a_log, d_skip, states, dy], comm=comm, name="ssd_bwd", grid=(NCHUNK, GROUPS),
        in_specs=[pl.BlockSpec((CHUNK, HPG * HP), lambda c, g: (rev(c), g)),
                  pl.BlockSpec((CHUNK, NSTATE), lambda c, g: (rev(c), g)),
                  pl.BlockSpec((CHUNK, NSTATE), lambda c, g: (rev(c), g)),
                  pl.BlockSpec((CHUNK, LANES), lambda c, g: (rev(c), 0)), par, par, par,
                  pl.BlockSpec((1, HPG, NSTATE, HP), lambda c, g: (rev(c), g, 0, 0)),
                  pl.BlockSpec((CHUNK, HPG * HP), lambda c, g: (rev(c), g))],
        out_specs=[pl.BlockSpec((CHUNK, HPG * HP), lambda c, g: (rev(c), g)),
                   pl.BlockSpec((CHUNK, NSTATE), lambda c, g: (rev(c), g)),
                   pl.BlockSpec((CHUNK, NSTATE), lambda c, g: (rev(c), g)),
                   pl.BlockSpec((CHUNK, LANES), lambda c, g: (rev(c), 0)), par, par, par],
        out_shape=[jax.ShapeDtypeStruct((S, INNER), f32), jax.ShapeDtypeStruct((S, GROUPS * NSTATE), f32),
                   jax.ShapeDtypeStruct((S, GROUPS * NSTATE), f32), jax.ShapeDtypeStruct((S, LANES), MXU),
                   jax.ShapeDtypeStruct((1, LANES), f32), jax.ShapeDtypeStruct((1, LANES), f32),
                   jax.ShapeDtypeStruct((1, LANES), f32)],
        scratch=[pltpu.VMEM((HEADS, NSTATE, HP), f32), pltpu.VMEM((CHUNK, LANES), f32),
                 pltpu.VMEM((CHUNK, LANES), f32), pltpu.VMEM((LANES, CHUNK), f32),
                 pltpu.VMEM((CHUNK, LANES), f32), pltpu.VMEM((CHUNK, LANES), f32), pltpu.VMEM((1, LANES), f32)],
        sem=("arbitrary", "arbitrary"))


def _attn_block(n, q, kp, kc, vp, vc, sinks, hk):
    rows = QPK * WIN
    qi = lax.broadcasted_iota(jnp.int32, (rows, 2 * WIN), 0) & (WIN - 1)
    ki = lax.broadcasted_iota(jnp.int32, (rows, 2 * WIN), 1)
    rel = qi + WIN - ki
    mask = (rel >= 0) & (rel < WIN) & ((ki >= WIN) | (n > 0))
    sl = slice(AH * hk, AH * (hk + 1))
    kb = jnp.concatenate([kp[:, sl], kc[:, sl]], axis=0).astype(MXU)
    vb = jnp.concatenate([vp[:, sl], vc[:, sl]], axis=0).astype(MXU)
    qg = jnp.concatenate([q[:, AH * (QPK * hk + g):AH * (QPK * hk + g + 1)] for g in range(QPK)], axis=0).astype(MXU)
    s = lax.dot_general(qg, kb, (((1,), (1,)), ((), ())), preferred_element_type=f32) * (AH ** -0.5)
    s = jnp.where(mask, s, -1e30)
    sink = jnp.concatenate([jnp.broadcast_to(sinks[:, QPK * hk + g:QPK * hk + g + 1], (WIN, 1)) for g in range(QPK)], axis=0)
    m = jnp.maximum(jnp.max(s, axis=1, keepdims=True), sink)
    p = jnp.exp(s - m)
    ps = jnp.exp(sink - m)
    den = jnp.sum(p, axis=1, keepdims=True) + ps
    return qg, kb, vb, p / den, ps / den


def _kv_specs():
    prev = lambda n: (jnp.maximum(n - 1, 0), 0)
    cur = lambda n: (n, 0)
    w = KVH * AH
    return [pl.BlockSpec((WIN, w), prev), pl.BlockSpec((WIN, w), cur), pl.BlockSpec((WIN, w), prev), pl.BlockSpec((WIN, w), cur)]


def attn_fwd(q, k, v, sinks, comm=None):
    def body(q_ref, kp_ref, kc_ref, vp_ref, vc_ref, s_ref, o_ref):
        n = pl.program_id(0)
        q_, kp, kc, vp, vc, sk = q_ref[...], kp_ref[...], kc_ref[...], vp_ref[...], vc_ref[...], s_ref[...]
        outs = []
        for hk in range(KVH):
            _, _, vb, pr, _ = _attn_block(n, q_, kp, kc, vp, vc, sk, hk)
            o = jnp.dot(pr.astype(MXU), vb, preferred_element_type=f32)
            outs += [o[WIN * g:WIN * (g + 1)] for g in range(QPK)]
        o_ref[...] = jnp.concatenate(outs, axis=1)

    return _call(
        body, [q, k, k, v, v, sinks], comm=comm, name="attn_fwd", grid=(NBLK,),
        in_specs=[pl.BlockSpec((WIN, D), lambda n: (n, 0))] + _kv_specs() + [pl.BlockSpec((1, QH), lambda n: (0, 0))],
        out_specs=[pl.BlockSpec((WIN, D), lambda n: (n, 0))],
        out_shape=[jax.ShapeDtypeStruct((S, D), f32)], sem=("parallel",))[0]


def attn_bwd(q, k, v, sinks, dout, comm=None):
    def body(q_ref, kp_ref, kc_ref, vp_ref, vc_ref, s_ref, do_ref, dq_ref, dkp_ref, dkc_ref, dvp_ref, dvc_ref, ds_ref):
        n = pl.program_id(0)

        @pl.when(n == 0)
        def _():
            ds_ref[...] = jnp.zeros_like(ds_ref)

        q_, kp, kc, vp, vc, sk, do = q_ref[...], kp_ref[...], kc_ref[...], vp_ref[...], vc_ref[...], s_ref[...], do_ref[...]
        lane = lax.broadcasted_iota(jnp.int32, (1, QH), 1)
        nt = (((1,), (1,)), ((), ()))
        tn = (((0,), (0,)), ((), ()))
        dqs, dkps, dkcs, dvps, dvcs = [], [], [], [], []
        dsink = jnp.zeros((1, QH), f32)
        for hk in range(KVH):
            qg, kb, vb, pr, prs = _attn_block(n, q_, kp, kc, vp, vc, sk, hk)
            dog = jnp.concatenate([do[:, AH * (QPK * hk + g):AH * (QPK * hk + g + 1)] for g in range(QPK)], axis=0).astype(MXU)
            dp = lax.dot_general(dog, vb, nt, preferred_element_type=f32)
            dvb = lax.dot_general(pr.astype(MXU), dog, tn, preferred_element_type=f32)
            delta = jnp.sum(pr * dp, axis=1, keepdims=True)
            ds = (pr * (dp - delta)).astype(MXU)
            dsk = -prs * delta
            for g in range(QPK):
                dsink = dsink + jnp.sum(dsk[WIN * g:WIN * (g + 1)], axis=0, keepdims=True) * (lane == QPK * hk + g).astype(f32)
            dqg = jnp.dot(ds, kb, preferred_element_type=f32) * (AH ** -0.5)
            dkb = lax.dot_general(ds, qg, tn, preferred_element_type=f32) * (AH ** -0.5)
            dqs += [dqg[WIN * g:WIN * (g + 1)] for g in range(QPK)]
            dkps.append(dkb[:WIN])
            dkcs.append(dkb[WIN:])
            dvps.append(dvb[:WIN])
            dvcs.append(dvb[WIN:])
        dq_ref[...] = jnp.concatenate(dqs, axis=1)
        dkp_ref[...] = jnp.concatenate(dkps, axis=1)
        dkc_ref[...] = jnp.concatenate(dkcs, axis=1)
        dvp_ref[...] = jnp.concatenate(dvps, axis=1)
        dvc_ref[...] = jnp.concatenate(dvcs, axis=1)
        ds_ref[...] += dsink

    w = KVH * AH
    blk = lambda width: pl.BlockSpec((WIN, width), lambda n: (n, 0))
    return _call(
        body, [q, k, k, v, v, sinks, dout], comm=comm, name="attn_bwd", grid=(NBLK,),
        in_specs=[blk(D)] + _kv_specs() + [pl.BlockSpec((1, QH), lambda n: (0, 0)), blk(D)],
        out_specs=[blk(D), blk(w), blk(w), blk(w), blk(w), pl.BlockSpec((1, QH), lambda n: (0, 0))],
        out_shape=[jax.ShapeDtypeStruct((S, D), f32)] + [jax.ShapeDtypeStruct((S, w), f32)] * 4 + [jax.ShapeDtypeStruct((1, QH), f32)],
        sem=("arbitrary",))


def kv_bwd(kv, pos, inv_freq, k_norm, dkp, dkc, dvp, dvc):
    w = KVH * AH

    def body(kv_ref, pos_ref, if_ref, g_ref, dkp_ref, dkc_ref, dvp_ref, dvc_ref, o_ref, dg_ref, db_ref):
        n = pl.program_id(0)

        @pl.when(n == 0)
        def _():
            dg_ref[...] = jnp.zeros_like(dg_ref)
            db_ref[...] = jnp.zeros_like(db_ref)

        inside = (n < NBLK - 1).astype(f32)
        dk = dkc_ref[...] + inside * dkp_ref[...]
        dv = dvc_ref[...] + inside * dvp_ref[...]
        cos, sin = _rope_tables(pos_ref[...], if_ref[...])
        dkpre, dg = _headnorm_rope_bwd(kv_ref[...], g_ref[...], cos, sin, dk, KVH)
        dkv = jnp.concatenate([dkpre, dv], axis=1)
        o_ref[...] = dkv.astype(o_ref.dtype)
        dg_ref[...] += dg
        db_ref[...] += jnp.sum(dkv, axis=0, keepdims=True)

    nxt = lambda n: (jnp.minimum(n + 1, NBLK - 1), 0)
    cur = lambda n: (n, 0)
    const = lambda n: (0, 0)
    return pl.pallas_call(
        body, name="kv_bwd", grid=(NBLK,),
        in_specs=[pl.BlockSpec((WIN, w), cur), pl.BlockSpec((WIN, 1), cur), pl.BlockSpec((1, AH // 2), const),
                  pl.BlockSpec((1, AH), const), pl.BlockSpec((WIN, w), nxt), pl.BlockSpec((WIN, w), cur),
                  pl.BlockSpec((WIN, w), nxt), pl.BlockSpec((WIN, w), cur)],
        out_specs=[pl.BlockSpec((WIN, 2 * w), cur), pl.BlockSpec((1, AH), const), pl.BlockSpec((1, 2 * w), const)],
        out_shape=[jax.ShapeDtypeStruct((S, 2 * w), MXU), jax.ShapeDtypeStruct((1, AH), f32), jax.ShapeDtypeStruct((1, 2 * w), f32)],
        compiler_params=_cparams(("arbitrary",)),
    )(kv, pos, inv_freq, k_norm, dkp, dkc, dvp, dvc)


def adamw(name, w, g, m, v):
    R, C = w.shape
    tr = _pick(R, (256, 128, 64, 32, 16, 8))
    tc = C if tr < R or C % 256 else 256

    def body(w_ref, g_ref, m_ref, v_ref, d_ref, nm_ref, nv_ref):
        g_ = g_ref[...]
        m_ = ADAM_B1 * m_ref[...] + (1.0 - ADAM_B1) * g_
        v_ = ADAM_B2 * v_ref[...] + (1.0 - ADAM_B2) * (g_ * g_)
        m_hat = m_ / (1.0 - ADAM_B1 ** ADAM_STEP)
        v_hat = v_ / (1.0 - ADAM_B2 ** ADAM_STEP)
        d_ref[...] = -ADAM_LR * (m_hat / (jnp.sqrt(v_hat) + ADAM_EPS) + ADAM_WD * w_ref[...])
        nm_ref[...] = m_
        nv_ref[...] = v_

    spec = pl.BlockSpec((tr, tc), lambda i, j: (i, j))
    return pl.pallas_call(
        body, name=name, grid=(R // tr, C // tc), in_specs=[spec] * 4, out_specs=[spec] * 3,
        out_shape=[jax.ShapeDtypeStruct((R, C), f32)] * 3, compiler_params=_cparams(("parallel", "parallel")),
    )(w, g, m, v)


def _me():
    return lax.axis_index("x"), lax.axis_index("y"), lax.axis_index("c")


def gather_comm(xs):
    n = len(xs)

    def parts(x_refs, o_refs, sems):
        send_sems, recv_sems, local_sems = sems
        x, y, c = _me()
        me, sibling = (x, y, c), (x, y, 1 - c)
        chips = [(1 - x, y), (x, 1 - y), (1 - x, 1 - y)]

        def copy(a, k, block, to, src=None):
            dst = o_refs[a].at[4 * block[0] + 2 * block[1] + block[2]]
            return pltpu.make_async_remote_copy(
                src_ref=dst if src is None else src, dst_ref=dst,
                send_sem=send_sems.at[7 * a + k], recv_sem=recv_sems.at[7 * a + k], device_id=to, device_id_type=MESH)

        mine = [pltpu.make_async_copy(x_refs[a], o_refs[a].at[4 * x + 2 * y + c], local_sems.at[a]) for a in range(n)]
        first = []
        for a in range(n):
            first.append(copy(a, 0, me, sibling, src=x_refs[a]))
            first += [copy(a, 1 + j, me, (*chip, c), src=x_refs[a]) for j, chip in enumerate(chips)]
        return copy, mine, first, me, sibling, chips, c

    def start(x_refs, o_refs, sems):
        _, mine, first, *_ = parts(x_refs, o_refs, sems)
        for cp in mine + first:
            cp.start()

    def finish(x_refs, o_refs, sems):
        copy, mine, first, me, sibling, chips, c = parts(x_refs, o_refs, sems)
        passed = []
        for j, chip in enumerate(chips):
            for a in range(n):
                copy(a, 1 + j, (*chip, c), me).wait_recv()
                cp = copy(a, 4 + j, (*chip, c), sibling)
                cp.start()
                passed.append(cp)
        for a in range(n):
            copy(a, 0, sibling, me).wait_recv()
            for j, chip in enumerate(chips):
                copy(a, 4 + j, (*chip, 1 - c), me).wait_recv()
        for cp in first + passed:
            cp.wait_send()
        for cp in mine:
            cp.wait()

    return Comm(xs, [jax.ShapeDtypeStruct((N_DEV,) + a.shape, a.dtype) for a in xs],
                [pltpu.SemaphoreType.DMA((7 * n,)), pltpu.SemaphoreType.DMA((7 * n,)), pltpu.SemaphoreType.DMA((n,))], start, finish)


def run_comm(name, comm):
    _call(lambda: None, [], name=name, grid=(1,), in_specs=[], out_specs=[], out_shape=[], comm=comm)
    return comm.results


def sibling_comm(gs):
    n = len(gs)

    def copies(g_refs, o_refs, sems):
        x, y, c = _me()
        return [pltpu.make_async_remote_copy(
            src_ref=g_refs[a].at[:, 1 - c], dst_ref=o_refs[a], send_sem=sems[0].at[a], recv_sem=sems[1].at[a],
            device_id=(x, y, 1 - c), device_id_type=MESH) for a in range(n)]

    def start(g_refs, o_refs, sems):
        for cp in copies(g_refs, o_refs, sems):
            cp.start()

    def finish(g_refs, o_refs, sems):
        for cp in copies(g_refs, o_refs, sems):
            cp.wait()

    return Comm(gs, [jax.ShapeDtypeStruct((4,) + g.shape[2:], g.dtype) for g in gs],
                [pltpu.SemaphoreType.DMA((n,)), pltpu.SemaphoreType.DMA((n,))], start, finish)


def chip_comm(ts):
    n = len(ts)

    def copies(t_refs, o_refs, sems):
        x, y, c = _me()
        chips = [(1 - x, y), (x, 1 - y), (1 - x, 1 - y)]
        return [pltpu.make_async_remote_copy(
            src_ref=t_refs[a].at[2 * px + py], dst_ref=o_refs[a].at[j],
            send_sem=sems[0].at[3 * a + j], recv_sem=sems[1].at[3 * a + j],
            device_id=(px, py, c), device_id_type=MESH) for j, (px, py) in enumerate(chips) for a in range(n)]

    def start(t_refs, o_refs, sems):
        for cp in copies(t_refs, o_refs, sems):
            cp.start()

    def finish(t_refs, o_refs, sems):
        for cp in copies(t_refs, o_refs, sems):
            cp.wait()

    return Comm(ts, [jax.ShapeDtypeStruct((3,) + t.shape[1:], t.dtype) for t in ts],
                [pltpu.SemaphoreType.DMA((3 * n,)), pltpu.SemaphoreType.DMA((3 * n,))], start, finish)


def _row_tile(rows):
    return _pick(rows, (512, 304, 256, 128))


def pair_add(name, g, r):
    _, _, R, C = g.shape
    tr = _row_tile(R)

    def body(c_ref, g_ref, r_ref, o_ref):
        o_ref[0] = (g_ref[0, 0].astype(f32) + r_ref[0].astype(f32)).astype(o_ref.dtype)

    return pl.pallas_call(
        body, name=name,
        grid_spec=pltpu.PrefetchScalarGridSpec(
            num_scalar_prefetch=1, grid=(4, R // tr),
            in_specs=[pl.BlockSpec((1, 1, tr, C), lambda p, i, c: (p, c[0], i, 0)),
                      pl.BlockSpec((1, tr, C), lambda p, i, c: (p, i, 0))],
            out_specs=pl.BlockSpec((1, tr, C), lambda p, i, c: (p, i, 0))),
        out_shape=jax.ShapeDtypeStruct((4, R, C), g.dtype),
        compiler_params=_cparams(("parallel", "parallel")),
    )(lax.axis_index("c").reshape(1).astype(jnp.int32), g, r)


def final_add(name, t, r):
    _, R, C = t.shape
    tr = _row_tile(R)

    def body(p_ref, t_ref, r_ref, o_ref):
        o_ref[...] = ((t_ref[0].astype(f32) + r_ref[0].astype(f32)) + r_ref[1].astype(f32)) + r_ref[2].astype(f32)

    chip = 2 * lax.axis_index("x") + lax.axis_index("y")
    return pl.pallas_call(
        body, name=name,
        grid_spec=pltpu.PrefetchScalarGridSpec(
            num_scalar_prefetch=1, grid=(R // tr,),
            in_specs=[pl.BlockSpec((1, tr, C), lambda i, p: (p[0], i, 0)),
                      pl.BlockSpec((3, tr, C), lambda i, p: (0, i, 0))],
            out_specs=pl.BlockSpec((tr, C), lambda i, p: (i, 0))),
        out_shape=jax.ShapeDtypeStruct((R, C), f32),
        compiler_params=_cparams(("parallel",)),
    )(chip.reshape(1).astype(jnp.int32), t, r)


class ReduceScatter:
    def __init__(self, tag, keys, grads):
        self.tag, self.keys, self.grads = tag, keys, grads
        self.send = [g.reshape((4, 2, g.shape[0] // N_DEV) + g.shape[1:]) for g in grads]

    def sibling(self):
        self.c1 = sibling_comm(self.send)
        return self.c1

    def chips(self):
        self.pairs = [pair_add(f"rs_pair_add_{self.tag}{i}", g, r) for i, (g, r) in enumerate(zip(self.send, self.c1.results))]
        self.c2 = chip_comm(self.pairs)
        return self.c2

    def sums(self):
        return {k: final_add(f"rs_final_add_{k}", t, r) for k, t, r in zip(self.keys, self.pairs, self.c2.results)}


IN_ROWS = {"z": (0, 2048), "xs": (2048, 4096), "B": (4096, 5120), "C": (5120, 6144)}
IN_COLS = 2 * INNER + 2 * GROUPS * NSTATE + HEADS


def sum_devices(g):
    def body(g_ref, o_ref):
        acc = g_ref[0]
        for i in range(1, N_DEV):
            acc = acc + g_ref[i]
        o_ref[...] = acc

    return pl.pallas_call(body, name="sum_devices", out_shape=jax.ShapeDtypeStruct(g.shape[1:], f32),
                          compiler_params=_cparams())(g)


def _pack(parts, unit, dtype, lead=()):
    flat = jnp.concatenate([p.reshape(lead + (-1,)).astype(dtype) for p in parts], axis=-1)
    n = flat.shape[-1]
    rows = -(-n // (unit * PACK_W)) * unit
    flat = jnp.pad(flat, [(0, 0)] * len(lead) + [(0, rows * PACK_W - n)])
    return flat.reshape(lead + (rows, PACK_W))


def _unpack(buf, shapes, lead=()):
    flat = buf.reshape(lead + (-1,))
    out, off = [], 0
    for shp in shapes:
        n = math.prod(shp)
        out.append(flat[..., off:off + n].reshape(lead + tuple(shp)))
        off += n
    return out


def _pad_lanes(a):
    return jnp.pad(a, [(0, 0)] * (a.ndim - 1) + [(0, LANES - a.shape[-1])])


_NN = (((1,), (0,)), ((), ()))
_NT = (((1,), (1,)), ((), ()))


def _mm(a, b, dims):
    return lax.dot_general(a.astype(MXU), b.astype(MXU), dims, preferred_element_type=f32)


def _ffn_fwd(tag, x, norm_g, w_inT, conv_w, conv_b, mid_comm=None):
    def in_fn(x_, g_, wT):
        h_ = _rms_fwd(x_, g_).astype(MXU)
        return h_, _mm(h_, wT[:FFN], _NT), _mm(h_, wT[FFN:], _NT)

    h, gate_pre, val = rowwise(f"{tag}_in", in_fn, [x], [norm_g, w_inT], [(D, MXU), (FFN, f32), (FFN, f32)])
    (act,) = colwise(f"{tag}_mid", lambda gp, v_, w_, b_: (_silu(_conv(gp, w_, b_)) * v_,), [gate_pre, val], [conv_w, conv_b], [MXU],
                     comm=mid_comm)
    return act, (x, h, gate_pre, val, act)


def _ffn_mid_bwd(name, gate_pre, val, dact, conv_w, conv_b, comm=None):
    ct = 128

    def body(gp_ref, v_ref, da_ref, w_ref, b_ref, dgv_ref, dw_ref, db_ref):
        gp, v_, da, w_ = gp_ref[...], v_ref[...], da_ref[...], w_ref[...]
        taps = _taps(gp, CONV_F)
        gate = _conv(gp, w_, b_ref[...], taps)
        sg = _sigmoid(gate)
        dgp, dw, db = _conv_bwd(gp, w_, da * v_ * (sg * (1.0 + gate * (1.0 - sg))), taps)
        dgv_ref[0] = dgp.astype(dgv_ref.dtype)
        dgv_ref[1] = (da * (gate * sg)).astype(dgv_ref.dtype)
        dw_ref[...] = dw
        db_ref[...] = db

    col = pl.BlockSpec((S, ct), lambda j: (0, j))
    return _call(
        body, [gate_pre, val, dact, conv_w, conv_b], comm=comm, name=name, grid=(FFN // ct,),
        in_specs=[col, col, col, pl.BlockSpec((CONV_F, ct), lambda j: (0, j)), pl.BlockSpec((1, ct), lambda j: (0, j))],
        out_specs=[pl.BlockSpec((2, S, ct), lambda j: (0, 0, j)), pl.BlockSpec((CONV_F, ct), lambda j: (0, j)),
                   pl.BlockSpec((1, ct), lambda j: (0, j))],
        out_shape=[jax.ShapeDtypeStruct((2, S, FFN), MXU), jax.ShapeDtypeStruct((CONV_F, FFN), f32), jax.ShapeDtypeStruct((1, FFN), f32)],
        sem=("parallel",))


def _ffn_bwd(tag, layer, saved, norm_g, w_inT, conv_w, conv_b, w_down, dout, mid_comm=None):
    x, h, gate_pre, val, act = saved
    dact = matmul(f"{tag}_ddown", dout, w_down, "nt")
    g_down = matmul(f"{tag}_wdown", act, dout, "tn", out_dtype=MXU)
    dgv, g_cw, g_cb = _ffn_mid_bwd(f"{tag}_dmid", gate_pre, val, dact, conv_w, conv_b, comm=mid_comm)
    g_inT = matmul_tn_stacked(f"{tag}_win", dgv, h, MXU)

    def din_fn(dg_, dv_, x_, do_, g_, wT):
        dx, dg = _rms_bwd(x_, g_, _mm(dg_, wT[:FFN], _NN) + _mm(dv_, wT[FFN:], _NN))
        return do_ + dx, dg

    rs = ReduceScatter(tag, (f"f_inT{layer}", f"f_down{layer}"), [g_inT, g_down])
    dx, g_norm = rowwise(f"{tag}_din", din_fn, [(dgv, 0), (dgv, 1), x, dout], [norm_g, w_inT], [(D, f32)], [(1, D)],
                         comm=rs.sibling())
    return dx, {f"f_norm{layer}": g_norm, f"f_conv_w{layer}": g_cw, f"f_conv_b{layer}": g_cb}, rs


def _land(W, keys, comm):
    for k, g in zip(keys, comm.results):
        W[k] = g.reshape(-1, g.shape[2])


def _local_step(x, pos, tgt, W, shards):
    G = {}
    gather = lambda *keys: gather_comm([shards[k] for k in keys])
    inv_freq = (ROPE_THETA ** (-jnp.arange(AH // 2, dtype=f32) / (AH // 2))).reshape(1, AH // 2)

    def in_fn(x_, g_, wT, wdtT):
        h_ = _rms_fwd(x_, g_).astype(MXU)
        return (h_,) + tuple(_mm(h_, wT[a:b], _NT) for a, b in IN_ROWS.values()) + (_mm(h_, wdtT, _NT),)

    pre = {}
    h0, z, pre["xs"], pre["B"], pre["C"], dt_pre = rowwise(
        "a_in", in_fn, [x], [W["a_norm"], W["inT"], W["in_dtT"]],
        [(D, MXU)] + [(b - a, f32) for a, b in IN_ROWS.values()] + [(LANES, f32)])
    conv = {}
    for k in ("xs", "B", "C"):
        (conv[k],) = colwise(f"a_conv_{k}", lambda p_, w_, b_: (_silu(_conv(p_, w_, b_)),), [pre[k]], [W[f"cw_{k}"], W[f"cb_{k}"]], [f32])
    c = gather("a_out", "f_inT0", "f_down0")
    y, states = ssd_fwd(conv["xs"], conv["B"], conv["C"], dt_pre, W["dt_bias"], W["A_log"], W["D"], comm=c)
    _land(W, ("a_out", "f_inT0", "f_down0"), c)

    def gate_norm(y_, z_, g_):
        yg = y_ * _silu(z_)
        w = INNER // GROUPS
        return (jnp.concatenate([_rms_fwd(yg[:, w * i:w * (i + 1)], g_[:, w * i:w * (i + 1)]) for i in range(GROUPS)], axis=1),)

    def out_fn(y_, z_, x_, g_, w_):
        (gn_,) = gate_norm(y_, z_, g_)
        gn_ = gn_.astype(MXU)
        return gn_, x_ + _mm(gn_, w_, _NN)

    gn, x1 = rowwise("a_out", out_fn, [y, z, x], [W["a_gnorm"], W["a_out"]], [(INNER, MXU), (D, f32)])

    c = gather("w_kv", "w_q", "w_o")
    act0, ffn0 = _ffn_fwd("f0", x1, W["f_norm0"], W["f_inT0"], W["f_cw0"], W["f_cb0"], mid_comm=c)
    _land(W, ("w_kv", "w_q", "w_o"), c)
    x2 = matmul("f0_down", act0, W["f_down0"], "nn", residual=x1)

    def qkv_fn(x_, gk, gb, wkv, bkv, wq, bq):
        kvn_, h2_ = _rms_fwd(x_, gk).astype(MXU), _rms_fwd(x_, gb).astype(MXU)
        return kvn_, h2_, _mm(kvn_, wkv, _NN) + bkv, _mm(h2_, wq, _NN) + bq

    kw = KVH * AH
    kvn, h2, kv, q_pre = rowwise("qkv_proj", qkv_fn, [x2], [W["kv_norm"], W["b_norm"], W["w_kv"], W["b_kv"], W["w_q"], W["b_q"]],
                                 [(D, MXU), (D, MXU), (2 * kw, f32), (D, f32)])

    def k_fwd(kv_, pos_, if_, g_):
        cos, sin = _rope_tables(pos_, if_)
        return _headnorm_rope_fwd(kv_[:, :kw], g_, cos, sin, KVH), kv_[:, kw:]

    k_rot, v_val = rowwise("k_rope", k_fwd, [kv, pos], [inv_freq, W["k_norm"]], [(kw, f32), (kw, f32)])

    def q_fwd(q_, pos_, if_, g_):
        cos, sin = _rope_tables(pos_, if_)
        return (_headnorm_rope_fwd(q_, g_, cos, sin, QH),)

    c = gather("f_down1")
    (q,) = rowwise("q_rope", q_fwd, [q_pre, pos], [inv_freq, W["q_norm"]], [(D, f32)], comm=c)
    _land(W, ("f_down1",), c)
    c = gather("f_inT1")
    att = attn_fwd(q, k_rot, v_val, W["sinks"], comm=c)
    _land(W, ("f_inT1",), c)
    x3 = matmul("o_proj", att, W["w_o"], "nn", bias=W["b_o"], residual=x2)

    act1, ffn1 = _ffn_fwd("f1", x3, W["f_norm1"], W["f_inT1"], W["f_cw1"], W["f_cb1"])

    def loss_fn(a_, x_, t_, w_):
        diff = x_ + _mm(a_, w_, _NN) - t_
        rows = jnp.sum(diff * diff, axis=1, keepdims=True) * (0.5 / D)
        return diff * (1.0 / D), jnp.sum(rows, axis=0, keepdims=True)

    dx4, loss = rowwise("f1_down_loss", loss_fn, [act1, x3, tgt], [W["f_down1"]], [(D, f32)], [(1, 1)])

    dx3, g, rs_f1 = _ffn_bwd("f1", 1, ffn1, W["f_norm1"], W["f_inT1"], W["f_cw1"], W["f_cb1"], W["f_down1"], dx4)
    G.update(g)

    datt = matmul("o_dproj", dx3, W["w_o"], "nt")
    g_wo = matmul("o_wproj", att, dx3, "tn", out_dtype=MXU)
    dq, dkp, dkc, dvp, dvc, G["sinks"] = attn_bwd(q, k_rot, v_val, W["sinks"], datt, comm=rs_f1.chips())

    def q_bwd(q_, pos_, dq_, dx_, if_, g_):
        cos, sin = _rope_tables(pos_, if_)
        dqp, dg = _headnorm_rope_bwd(q_, g_, cos, sin, dq_, QH)
        return dqp, dg, jnp.sum(dqp, axis=0, keepdims=True), jnp.sum(dx_, axis=0, keepdims=True)

    dq_pre, G["q_norm"], G["b_q"], G["b_o"] = rowwise("q_drope", q_bwd, [q_pre, pos, dq, dx3], [inv_freq, W["q_norm"]],
                                                      [(D, MXU)], [(1, AH), (1, D), (1, D)])
    g_wq = matmul("q_wproj", h2, dq_pre, "tn", out_dtype=MXU)
    dkv, G["k_norm"], G["b_kv"] = kv_bwd(kv, pos, inv_freq, W["k_norm"], dkp, dkc, dvp, dvc)
    g_wkv = matmul("kv_wproj", kvn, dkv, "tn", out_dtype=MXU)
    rs_att = ReduceScatter("att", ("w_kv", "w_q", "w_o"), [g_wkv, g_wq, g_wo])

    def x2_bwd(x_, dq_, dkv_, dx_, gb_, gk_, wq, wkv):
        d1, dgb = _rms_bwd(x_, gb_, _mm(dq_, wq, _NT))
        d2, dgk = _rms_bwd(x_, gk_, _mm(dkv_, wkv, _NT))
        return dx_ + d1 + d2, dgb, dgk

    dx2, G["b_norm"], G["kv_norm"] = rowwise("qkv_dproj", x2_bwd, [x2, dq_pre, dkv, dx3],
                                             [W["b_norm"], W["kv_norm"], W["w_q"], W["w_kv"]],
                                             [(D, f32)], [(1, D), (1, D)], comm=rs_att.sibling())

    dx1, g, rs_f0 = _ffn_bwd("f0", 0, ffn0, W["f_norm0"], W["f_inT0"], W["f_cw0"], W["f_cb0"], W["f_down0"], dx2,
                             mid_comm=rs_att.chips())
    G.update(g)

    dgn = matmul("a_dout", dx1, W["a_out"], "nt")
    rs_out = ReduceScatter("a_out", ("a_out",), [matmul("a_wout", gn, dx1, "tn", out_dtype=MXU)])

    def gate_norm_bwd(y_, z_, dgn_, g_):
        w = INNER // GROUPS
        sg = _sigmoid(z_)
        sz = z_ * sg
        yg = y_ * sz
        parts, dgs = [], []
        for i in range(GROUPS):
            dseg, dg = _rms_bwd(yg[:, w * i:w * (i + 1)], g_[:, w * i:w * (i + 1)], dgn_[:, w * i:w * (i + 1)])
            parts.append(dseg)
            dgs.append(dg)
        dyg = jnp.concatenate(parts, axis=1)
        return dyg * sz, dyg * y_ * (sg * (1.0 + z_ * (1.0 - sg))), jnp.concatenate(dgs, axis=1)

    dy, dz, G["a_gnorm"] = rowwise("a_dgnorm", gate_norm_bwd, [y, z, dgn], [W["a_gnorm"]], [(INNER, f32), (INNER, MXU)], [(1, INNER)],
                                   tile=128, comm=rs_out.sibling())
    dconv = {}
    dconv["xs"], dconv["B"], dconv["C"], ddt_pre, G["dt_bias"], G["A_log"], G["D"] = ssd_bwd(
        conv["xs"], conv["B"], conv["C"], dt_pre, W["dt_bias"], W["A_log"], W["D"], states, dy,
        comm=merge_comms([rs_f0.chips(), rs_out.chips()]))

    def conv_bwd(p_, do_, w_, b_):
        taps = _taps(p_, CONV_A)
        return _conv_bwd(p_, w_, do_ * _dsilu(_conv(p_, w_, b_, taps)), taps)

    g_in, dpre = [matmul("a_win_z", dz, h0, "tn", out_dtype=MXU)], {}
    for k in ("xs", "B", "C"):
        dpre[k], G[f"cw_{k}"], G[f"cb_{k}"] = colwise(f"a_dconv_{k}", conv_bwd, [pre[k], dconv[k]], [W[f"cw_{k}"], W[f"cb_{k}"]], [MXU], [CONV_A, 1])
        g_in.append(matmul(f"a_win_{k}", dpre[k], h0, "tn", out_dtype=MXU))
    g_in.append(matmul("a_win_dt", ddt_pre, h0, "tn", out_dtype=MXU)[:HEADS])
    rs_in = ReduceScatter("a_in", ("inT",), [jnp.concatenate(g_in, axis=0)])

    def x0_bwd(dz_, dxs_, db_, dc_, ddt_, x_, do_, g_, wT, wdtT):
        parts = zip((dz_, dxs_, db_, dc_), IN_ROWS.values())
        dh = sum(_mm(d_, wT[a:b], _NN) for d_, (a, b) in parts) + _mm(ddt_, wdtT, _NN)
        dx, dg = _rms_bwd(x_, g_, dh)
        return do_ + dx, dg

    dx, G["a_norm"] = rowwise("a_din", x0_bwd, [dz, dpre["xs"], dpre["B"], dpre["C"], ddt_pre, x, dx1],
                              [W["a_norm"], W["inT"], W["in_dtT"]], [(D, f32)], [(1, D)], comm=rs_in.sibling())
    return loss, dx, G, [rs_f1, rs_att, rs_f0, rs_out, rs_in]


ROW_KEYS = ("inT", "a_out", "f_inT0", "f_down0", "w_kv", "w_q", "w_o", "f_inT1", "f_down1")


def _row_blocks(src):
    return {"inT": src["a_in_proj"][0].T, "a_out": src["a_out_proj"][0], "w_kv": src["w_kv"], "w_q": src["w_q"][0],
            "w_o": src["w_o"][0], "f_inT0": src["f_w_in"][0].T, "f_inT1": src["f_w_in"][1].T,
            "f_down0": src["f_w_down"][0], "f_down1": src["f_w_down"][1]}


def _from_row_blocks(rb):
    return {"a_in_proj": rb["inT"].T[None], "a_out_proj": rb["a_out"][None], "w_kv": rb["w_kv"], "w_q": rb["w_q"][None],
            "w_o": rb["w_o"][None], "f_w_in": jnp.stack([rb["f_inT0"].T, rb["f_inT1"].T]),
            "f_w_down": jnp.stack([rb["f_down0"], rb["f_down1"]])}


SMALL_SHARDED = ("a_norm", "a_conv_w", "a_conv_b", "a_gnorm", "f_conv_w")
REPLICATED = ("a_dt_bias", "a_A_log", "a_D", "kv_norm", "b_kv", "k_norm", "b_norm", "b_q", "q_norm", "sinks", "b_o",
              "f_norm", "f_conv_b")
ORDER = ("a_norm", "a_in_proj", "a_conv_w", "a_conv_b", "a_dt_bias", "a_A_log", "a_D", "a_gnorm", "a_out_proj", "kv_norm",
         "w_kv", "b_kv", "k_norm", "b_norm", "w_q", "b_q", "q_norm", "sinks", "w_o", "b_o", "f_norm", "f_w_in",
         "f_conv_w", "f_conv_b", "f_w_down")


def _gathered_to_whole(name, g):
    if name == "a_conv_w":
        return jnp.moveaxis(g[:, 0], 0, 1).reshape(g.shape[2], -1)
    if name in ("a_norm", "a_conv_b", "a_gnorm"):
        return g[:, 0].reshape(1, -1)
    if name == "f_conv_w":
        return jnp.moveaxis(g, 0, 2).reshape(g.shape[1], g.shape[2], -1)
    raise ValueError(name)


def _whole_to_shards(name, w):
    if name == "a_conv_w":
        return jnp.moveaxis(w.reshape(w.shape[0], N_DEV, -1), 1, 0)[:, None]
    if name in ("a_norm", "a_conv_b", "a_gnorm"):
        return w.reshape(N_DEV, 1, -1)
    if name == "f_conv_w":
        return jnp.moveaxis(w.reshape(w.shape[0], w.shape[1], N_DEV, -1), 2, 0)
    raise ValueError(name)


def _small_weights(whole):
    W = {}
    cw, cb = whole["a_conv_w"], whole["a_conv_b"]
    o = 0
    for k, n in (("xs", INNER), ("B", GROUPS * NSTATE), ("C", GROUPS * NSTATE)):
        W[f"cw_{k}"], W[f"cb_{k}"] = cw[:, o:o + n], cb[:, o:o + n]
        o += n
    W["a_norm"], W["a_gnorm"] = whole["a_norm"], whole["a_gnorm"]
    W["dt_bias"], W["A_log"], W["D"] = (_pad_lanes(whole[k]) for k in ("a_dt_bias", "a_A_log", "a_D"))
    W["kv_norm"], W["b_kv"], W["k_norm"] = whole["kv_norm"].reshape(1, -1), whole["b_kv"].reshape(1, -1), whole["k_norm"].reshape(1, -1)
    for k in ("b_norm", "b_q", "q_norm", "sinks", "b_o"):
        W[k] = whole[k]
    for i in range(2):
        W[f"f_norm{i}"] = whole["f_norm"][i:i + 1]
        W[f"f_cw{i}"], W[f"f_cb{i}"] = whole["f_conv_w"][i], whole["f_conv_b"][i:i + 1]
    return W


def _small_grads(G, shapes):
    nh = HEADS
    out = {
        "a_conv_w": jnp.concatenate([G["cw_xs"], G["cw_B"], G["cw_C"]], axis=1),
        "a_conv_b": jnp.concatenate([G["cb_xs"], G["cb_B"], G["cb_C"]], axis=1),
        "a_norm": G["a_norm"], "a_gnorm": G["a_gnorm"],
        "a_dt_bias": G["dt_bias"][:, :nh], "a_A_log": G["A_log"][:, :nh], "a_D": G["D"][:, :nh],
        "kv_norm": G["kv_norm"], "b_kv": G["b_kv"], "k_norm": G["k_norm"], "b_norm": G["b_norm"],
        "b_q": G["b_q"], "q_norm": G["q_norm"], "sinks": G["sinks"], "b_o": G["b_o"],
        "f_norm": jnp.concatenate([G["f_norm0"], G["f_norm1"]], axis=0),
        "f_conv_w": jnp.stack([G["f_conv_w0"], G["f_conv_w1"]]),
        "f_conv_b": jnp.concatenate([G["f_conv_b0"], G["f_conv_b1"]], axis=0),
    }
    return {k: val.reshape(shapes[k]) if k in shapes else val for k, val in out.items()}


def kernel(x, positions, a_norm, a_in_proj, a_conv_w, a_conv_b, a_dt_bias, a_A_log, a_D, a_gnorm, a_out_proj, kv_norm, w_kv, b_kv, k_norm, b_norm, w_q, b_q, q_norm, sinks, w_o, b_o, f_norm, f_w_in, f_conv_w, f_conv_b, f_w_down, loss_target, m_a_norm, m_a_in_proj, m_a_conv_w, m_a_conv_b, m_a_dt_bias, m_a_A_log, m_a_D, m_a_gnorm, m_a_out_proj, m_kv_norm, m_w_kv, m_b_kv, m_k_norm, m_b_norm, m_w_q, m_b_q, m_q_norm, m_sinks, m_w_o, m_b_o, m_f_norm, m_f_w_in, m_f_conv_w, m_f_conv_b, m_f_w_down, v_a_norm, v_a_in_proj, v_a_conv_w, v_a_conv_b, v_a_dt_bias, v_a_A_log, v_a_D, v_a_gnorm, v_a_out_proj, v_kv_norm, v_w_kv, v_b_kv, v_k_norm, v_b_norm, v_w_q, v_b_q, v_q_norm, v_sinks, v_w_o, v_b_o, v_f_norm, v_f_w_in, v_f_conv_w, v_f_conv_b, v_f_w_down):
    given = dict(locals())
    w_in = {n: given[n] for n in ORDER}
    m_in = {n: given["m_" + n] for n in ORDER}
    v_in = {n: given["v_" + n] for n in ORDER}
    dev = 4 * lax.axis_index("x") + 2 * lax.axis_index("y") + lax.axis_index("c")

    w2, m2, v2 = _row_blocks(w_in), _row_blocks(m_in), _row_blocks(v_in)
    small_pack = _pack([w_in[n] for n in SMALL_SHARDED], 8, f32)
    shards = {k: w2[k].astype(MXU) for k in ROW_KEYS}
    in_all, small_all = run_comm("ag_head", gather_comm([shards["inT"], small_pack]))
    whole = {n: w_in[n] for n in REPLICATED}
    for n, g in zip(SMALL_SHARDED, _unpack(small_all, [w_in[n].shape for n in SMALL_SHARDED], lead=(N_DEV,))):
        whole[n] = _gathered_to_whole(n, g)
    W = _small_weights(whole)
    W["inT"] = in_all.reshape(-1, D)
    W["in_dtT"] = jnp.pad(W["inT"][IN_COLS - HEADS:], ((0, LANES - HEADS), (0, 0)))

    loss, dx, G, scatters = _local_step(x[0], positions.reshape(S, 1).astype(f32), loss_target[0], W, shards)
    grads = _small_grads(G, {n: whole[n].shape for n in REPLICATED})

    small_names = SMALL_SHARDED + REPLICATED
    small_part = _pack([grads[n] for n in small_names], 8, f32)
    small_gather = gather_comm([small_part])
    run_comm("rs_tail", merge_comms([scatters[-1].chips(), small_gather]))
    sums = {}
    for rs in scatters:
        sums.update(rs.sums())
    g_out = _from_row_blocks(sums)
    small_sum = sum_devices(small_gather.results[0])
    for n, g in zip(small_names, _unpack(small_sum, [grads[n].shape for n in small_names])):
        if n in SMALL_SHARDED:
            g_out[n] = lax.dynamic_index_in_dim(_whole_to_shards(n, g), dev, axis=0, keepdims=False)
        else:
            g_out[n] = g.reshape(w_in[n].shape)

    stepped = {k: adamw(f"adamw_{k}", w2[k], sums[k], m2[k], v2[k]) for k in ROW_KEYS}
    delta, new_m, new_v = (_from_row_blocks({k: stepped[k][i] for k in ROW_KEYS}) for i in range(3))
    packs = [_pack([src[n] for n in small_names], 8, f32) for src in (w_in, g_out, m_in, v_in)]
    outs = adamw("adamw_small", *packs)
    for dst, buf in zip((delta, new_m, new_v), outs):
        for n, a in zip(small_names, _unpack(buf, [w_in[n].shape for n in small_names])):
            dst[n] = a

    loss_all = lax.psum(loss[0, 0], AXES)
    return (loss_all, dx[None], *[g_out[n] for n in ORDER], *[delta[n] for n in ORDER],
            *[new_m[n] for n in ORDER], *[new_v[n] for n in ORDER])
```

```python
import functools
import math

import jax
import jax.numpy as jnp
from jax import lax
from jax.experimental import pallas as pl
from jax.experimental.pallas import tpu as pltpu

f32 = jnp.float32
bf16 = jnp.bfloat16
MXU = bf16

N_DEV = 8
S = 2048
D = 1024
EPS = 1e-5
INNER = 2048
HEADS = 32
HP = 64
GROUPS = 8
HPG = HEADS // GROUPS
NSTATE = 128
CONV_A = 4
CHUNK = 256
NCHUNK = S // CHUNK
AH = 64
QH = 16
KVH = 4
QPK = QH // KVH
WIN = 128
NBLK = S // WIN
ROPE_THETA = 10000.0
FFN = 2816
CONV_F = 3
LANES = 128
PACK_W = 1024
BIG_TILE = 496
VMEM_LIMIT = 56 * 1024 * 1024

ADAM_LR, ADAM_B1, ADAM_B2, ADAM_EPS, ADAM_WD, ADAM_STEP = 0.001, 0.9, 0.999, 1e-08, 0.01, 10

MESH = pl.DeviceIdType.MESH
AXES = ("x", "y", "c")


def _cparams(sem=None):
    return pltpu.CompilerParams(dimension_semantics=sem, vmem_limit_bytes=VMEM_LIMIT)


def _pick(n, cands):
    for c in cands:
        if n % c == 0:
            return c
    return n


class Comm:
    def __init__(self, ins, out_shapes, sems, start, finish):
        self.ins, self.out_shapes, self.sems, self.start, self.finish = list(ins), list(out_shapes), list(sems), start, finish
        self.results, self.children = None, ()

    def set_results(self, res):
        self.results, o = list(res), 0
        for ch in self.children:
            ch.set_results(res[o:o + len(ch.out_shapes)])
            o += len(ch.out_shapes)


def merge_comms(comms):
    def each(fn_name, ins, outs, sems):
        i = o = s = 0
        for c in comms:
            getattr(c, fn_name)(ins[i:i + len(c.ins)], outs[o:o + len(c.out_shapes)], sems[s:s + len(c.sems)])
            i, o, s = i + len(c.ins), o + len(c.out_shapes), s + len(c.sems)

    merged = Comm([a for c in comms for a in c.ins], [a for c in comms for a in c.out_shapes], [a for c in comms for a in c.sems],
                  functools.partial(each, "start"), functools.partial(each, "finish"))
    merged.children = tuple(comms)
    return merged


def _call(body, args, *, name, grid, in_specs, out_specs, out_shape, scratch=(), sem=None, comm=None):
    if comm is None:
        return pl.pallas_call(body, name=name, grid=grid, in_specs=list(in_specs), out_specs=list(out_specs),
                              out_shape=list(out_shape), scratch_shapes=list(scratch), compiler_params=_cparams(sem))(*args)
    n_in, n_out, n_scr, c_in, c_out = len(in_specs), len(out_shape), len(scratch), len(comm.ins), len(comm.out_shapes)
    any_spec = pl.BlockSpec(memory_space=pl.ANY)

    def outer(*refs):
        ins, c_ins = refs[:n_in], refs[n_in:n_in + c_in]
        o = n_in + c_in
        outs, c_outs = refs[o:o + n_out], refs[o + n_out:o + n_out + c_out]
        o += n_out + c_out
        scr, c_sems = refs[o:o + n_scr], refs[o + n_scr:]
        ids = [pl.program_id(i) for i in range(len(grid))]
        first = functools.reduce(jnp.logical_and, [i == 0 for i in ids])
        last = functools.reduce(jnp.logical_and, [i == g - 1 for i, g in zip(ids, grid)])

        @pl.when(first)
        def _():
            comm.start(c_ins, c_outs, c_sems)

        body(*ins, *outs, *scr)

        @pl.when(last)
        def _():
            comm.finish(c_ins, c_outs, c_sems)

    res = pl.pallas_call(
        outer, name=name, grid=grid, in_specs=list(in_specs) + [any_spec] * c_in,
        out_specs=list(out_specs) + [any_spec] * c_out, out_shape=list(out_shape) + comm.out_shapes,
        scratch_shapes=list(scratch) + comm.sems, compiler_params=_cparams(("arbitrary",) * len(grid)),
    )(*args, *comm.ins)
    comm.set_results(res[n_out:])
    return res[:n_out]


def matmul(name, a, b, mode, out_dtype=f32, bias=None, residual=None, b_rows=None, a_lead=None):
    b_idx, b_shape = (0, b.shape) if b_rows is None else (b_rows[0], (b_rows[1], b.shape[1]))
    a_shape = a.shape if a_lead is None else a.shape[1:]
    if mode == "nn":
        (M, K), (K2, N) = a_shape, b_shape
    elif mode == "nt":
        (M, K), (N, K2) = a_shape, b_shape
    else:
        (K, M), (K2, N) = a_shape, b_shape
    assert K == K2, (name, a_shape, b_shape)
    if mode == "tn":
        tm, tn = M, _pick(N, (512, 256, 128) if M <= 1024 else (256, 128))
        a_spec = pl.BlockSpec((K, M), lambda j: (0, 0))
        b_spec = pl.BlockSpec((K, tn), lambda j: (0, j))
        dims = (((0,), (0,)), ((), ()))
        grid, o_map, row_map = (N // tn,), (lambda j: (0, j)), (lambda j: (0, j))
    else:
        tm, tn = (256 if N >= 2048 else 512), N
        a_spec = pl.BlockSpec((tm, K), lambda i: (i, 0)) if a_lead is None else pl.BlockSpec((None, tm, K), lambda i: (a_lead, i, 0))
        b_spec = pl.BlockSpec(b_shape, lambda i: (b_idx, 0))
        dims = (((1,), (0,)), ((), ())) if mode == "nn" else (((1,), (1,)), ((), ()))
        grid, o_map, row_map = (M // tm,), (lambda i: (i, 0)), (lambda i: (0, 0))
    ins, in_specs = [a, b], [a_spec, b_spec]
    if bias is not None:
        ins.append(bias)
        in_specs.append(pl.BlockSpec((1, tn), row_map))
    if residual is not None:
        ins.append(residual)
        in_specs.append(pl.BlockSpec((tm, tn), o_map))
    has_bias, has_res = bias is not None, residual is not None

    def body(a_ref, b_ref, *rest):
        rest = list(rest)
        bias_ref = rest.pop(0) if has_bias else None
        res_ref = rest.pop(0) if has_res else None
        (o_ref,) = rest
        r = lax.dot_general(a_ref[...].astype(MXU), b_ref[...].astype(MXU), dims, preferred_element_type=f32)
        if has_bias:
            r = r + bias_ref[...]
        if has_res:
            r = r + res_ref[...]
        o_ref[...] = r.astype(out_dtype)

    return pl.pallas_call(
        body, name=name, grid=grid, in_specs=in_specs,
        out_specs=pl.BlockSpec((tm, tn), o_map),
        out_shape=jax.ShapeDtypeStruct((M, N), out_dtype),
        compiler_params=_cparams(("parallel",)),
    )(*ins)


def matmul_tn_stacked(name, a, b, out_dtype):
    R, K, M = a.shape
    N = b.shape[1]
    tn = _pick(N, (256, 128))

    def body(a_ref, b_ref, o_ref):
        o_ref[0] = lax.dot_general(a_ref[0].astype(MXU), b_ref[...].astype(MXU), (((0,), (0,)), ((), ())),
                                   preferred_element_type=f32).astype(out_dtype)

    out = pl.pallas_call(
        body, name=name, grid=(R, N // tn),
        in_specs=[pl.BlockSpec((1, K, M), lambda r, j: (r, 0, 0)), pl.BlockSpec((K, tn), lambda r, j: (0, j))],
        out_specs=pl.BlockSpec((1, M, tn), lambda r, j: (r, 0, j)),
        out_shape=jax.ShapeDtypeStruct((R, M, N), out_dtype),
        compiler_params=_cparams(("parallel", "parallel")),
    )(a, b)
    return out.reshape(R * M, N)


def rowwise(name, fn, rows, pars, outs, accs=(), tile=256, comm=None):
    n_in, n_out = len(rows) + len(pars), len(outs)
    in_specs = [pl.BlockSpec((None, tile, r[0].shape[2]), functools.partial(lambda i, lead: (lead, i, 0), lead=r[1]))
                if isinstance(r, tuple) else pl.BlockSpec((tile, r.shape[1]), lambda i: (i, 0)) for r in rows]
    rows = [r[0] if isinstance(r, tuple) else r for r in rows]
    in_specs += [pl.BlockSpec(p.shape, lambda i: (0, 0)) for p in pars]
    out_specs = [pl.BlockSpec((tile, c), lambda i: (i, 0)) for c, _ in outs]
    out_specs += [pl.BlockSpec(shp, lambda i: (0, 0)) for shp in accs]
    out_shape = [jax.ShapeDtypeStruct((S, c), dt) for c, dt in outs]
    out_shape += [jax.ShapeDtypeStruct(shp, f32) for shp in accs]

    def body(*refs):
        res = fn(*[r[...] for r in refs[:n_in]])
        o_refs = refs[n_in:n_in + n_out]
        a_refs = refs[n_in + n_out:]
        for ref, val in zip(o_refs, res[:n_out]):
            ref[...] = val.astype(ref.dtype)
        if a_refs:
            @pl.when(pl.program_id(0) == 0)
            def _():
                for ref in a_refs:
                    ref[...] = jnp.zeros_like(ref)
            for ref, val in zip(a_refs, res[n_out:]):
                ref[...] += val

    return _call(body, [*rows, *pars], name=name, grid=(S // tile,), in_specs=in_specs, out_specs=out_specs,
                 out_shape=out_shape, sem=("arbitrary",) if accs else ("parallel",), comm=comm)


def colwise(name, fn, cols, pars, outs, pouts=(), ct=128, comm=None):
    C = cols[0].shape[1]
    n_in, n_out = len(cols) + len(pars), len(outs)
    in_specs = [pl.BlockSpec((S, ct), lambda j: (0, j)) for _ in cols]
    in_specs += [pl.BlockSpec((p.shape[0], ct), lambda j: (0, j)) for p in pars]
    out_specs = [pl.BlockSpec((S, ct), lambda j: (0, j)) for _ in outs]
    out_specs += [pl.BlockSpec((r, ct), lambda j: (0, j)) for r in pouts]
    out_shape = [jax.ShapeDtypeStruct((S, C), dt) for dt in outs]
    out_shape += [jax.ShapeDtypeStruct((r, C), f32) for r in pouts]

    def body(*refs):
        res = fn(*[r[...] for r in refs[:n_in]])
        for ref, val in zip(refs[n_in:], res):
            ref[...] = val.astype(ref.dtype)

    return _call(body, [*cols, *pars], name=name, grid=(C // ct,), in_specs=in_specs, out_specs=out_specs,
                 out_shape=out_shape, sem=("parallel",), comm=comm)


def _sigmoid(x):
    return 0.5 * jnp.tanh(0.5 * x) + 0.5


def _silu(x):
    return x * _sigmoid(x)


def _dsilu(x):
    sg = _sigmoid(x)
    return sg * (1.0 + x * (1.0 - sg))


def _softplus(x):
    return jnp.maximum(x, 0.0) + jnp.log(1.0 + jnp.exp(-jnp.abs(x)))


def _rms_fwd(x, g):
    r = lax.rsqrt(jnp.mean(x * x, axis=-1, keepdims=True) + EPS)
    return x * r * g


def _rms_bwd(x, g, dh):
    r = lax.rsqrt(jnp.mean(x * x, axis=-1, keepdims=True) + EPS)
    xh = x * r
    dxh = dh * g
    dx = r * (dxh - xh * jnp.mean(dxh * xh, axis=-1, keepdims=True))
    return dx, jnp.sum(dh * xh, axis=0, keepdims=True)


def _taps(x, width):
    row = lax.broadcasted_iota(jnp.int32, x.shape, 0)
    return [jnp.where(row >= s, pltpu.roll(x, s, 0), 0.0) for s in range(width - 1, 0, -1)] + [x]


def _conv(x, w, b, taps=None):
    width = w.shape[0]
    taps = _taps(x, width) if taps is None else taps
    out = b + w[0:1, :] * taps[0]
    for k in range(1, width):
        out = out + w[k:k + 1, :] * taps[k]
    return out


def _conv_bwd(x, w, dc, taps=None):
    width, n = w.shape[0], x.shape[0]
    taps = _taps(x, width) if taps is None else taps
    row = lax.broadcasted_iota(jnp.int32, x.shape, 0)
    dx = w[width - 1:width, :] * dc
    for k in range(width - 1):
        s = width - 1 - k
        dx = dx + w[k:k + 1, :] * jnp.where(row < n - s, pltpu.roll(dc, n - s, 0), 0.0)
    dw = jnp.concatenate([jnp.sum(dc * t, axis=0, keepdims=True) for t in taps], axis=0)
    return dx, dw, jnp.sum(dc, axis=0, keepdims=True)


def _rope_tables(pos, inv_freq):
    ang = pos * inv_freq
    return jnp.cos(ang), jnp.sin(ang)


def _headnorm_rope_fwd(x, g, cos, sin, heads):
    half = AH // 2
    outs = []
    for h in range(heads):
        seg = x[:, AH * h:AH * (h + 1)]
        n = _rms_fwd(seg, g)
        n1, n2 = n[:, :half], n[:, half:]
        outs += [n1 * cos - n2 * sin, n2 * cos + n1 * sin]
    return jnp.concatenate(outs, axis=1)


def _headnorm_rope_bwd(x, g, cos, sin, dout, heads):
    half = AH // 2
    dxs, dg = [], jnp.zeros((1, AH), f32)
    for h in range(heads):
        seg = x[:, AH * h:AH * (h + 1)]
        d = dout[:, AH * h:AH * (h + 1)]
        d1, d2 = d[:, :half], d[:, half:]
        dn = jnp.concatenate([d1 * cos + d2 * sin, d2 * cos - d1 * sin], axis=1)
        dx, dgh = _rms_bwd(seg, g, dn)
        dxs.append(dx)
        dg = dg + dgh
    return jnp.concatenate(dxs, axis=1), dg


def _ssd_prep(dt_pre, dt_bias, a_log, dt_s, acum_s, acumT_s):
    dt = _softplus(dt_pre + dt_bias)
    a = dt * (-jnp.exp(a_log))
    row = lax.broadcasted_iota(jnp.int32, (CHUNK, CHUNK), 0)
    col = lax.broadcasted_iota(jnp.int32, (CHUNK, CHUNK), 1)
    dt_s[...] = dt
    acum_s[...] = jnp.dot((col <= row).astype(f32), a, precision=lax.Precision.HIGHEST, preferred_element_type=f32)
    acumT_s[...] = lax.dot_general(a, (row <= col).astype(f32), (((0,), (0,)), ((), ())),
                                   precision=lax.Precision.HIGHEST, preferred_element_type=f32)


def _head_cols(h, dt_s, acum_s, acumT_s):
    lane = lax.broadcasted_iota(jnp.int32, (1, LANES), 1)
    oh_l = (lane == h).astype(f32)
    sub = lax.broadcasted_iota(jnp.int32, (LANES, 1), 0)
    oh_s = (sub == h).astype(f32)
    dt_h = jnp.sum(dt_s[...] * oh_l, axis=1, keepdims=True)
    ac_h = jnp.sum(acum_s[...] * oh_l, axis=1, keepdims=True)
    acr_h = jnp.sum(acumT_s[...] * oh_s, axis=0, keepdims=True)
    return oh_l, dt_h, ac_h, acr_h


def ssd_fwd(xs, Bm, Cm, dt_pre, dt_bias, a_log, d_skip, comm=None):
    def body(xs_ref, b_ref, c_ref, dtp_ref, bias_ref, alog_ref, d_ref, y_ref, st_ref, state, dt_s, acum_s, acumT_s):
        c, g = pl.program_id(0), pl.program_id(1)

        @pl.when(g == 0)
        def _():
            _ssd_prep(dtp_ref[...], bias_ref[...], alog_ref[...], dt_s, acum_s, acumT_s)

        row = lax.broadcasted_iota(jnp.int32, (CHUNK, CHUNK), 0)
        col = lax.broadcasted_iota(jnp.int32, (CHUNK, CHUNK), 1)
        causal = col <= row
        Bb, Cb = b_ref[...], c_ref[...]
        cb = lax.dot_general(Cb.astype(MXU), Bb.astype(MXU), (((1,), (1,)), ((), ())), preferred_element_type=f32)
        xs_blk = xs_ref[...]

        @pl.when(c == 0)
        def _():
            for j in range(HPG):
                state[g * HPG + j] = jnp.zeros((NSTATE, HP), f32)

        prevs = [state[g * HPG + j] for j in range(HPG)]
        y_off_all = jnp.dot(Cb.astype(MXU), jnp.concatenate(prevs, axis=1).astype(MXU), preferred_element_type=f32)
        ys, xds, e_ends = [], [], []
        for j in range(HPG):
            oh_l, dt_h, ac_h, acr_h = _head_cols(g * HPG + j, dt_s, acum_s, acumT_s)
            decay = jnp.exp(jnp.where(causal, ac_h - acr_h, -1e30))
            w = (cb * decay).astype(MXU)
            xs_h = xs_blk[:, HP * j:HP * (j + 1)]
            xd = xs_h * dt_h
            y_diag = jnp.dot(w, xd.astype(MXU), preferred_element_type=f32)
            y_off = y_off_all[:, HP * j:HP * (j + 1)] * jnp.exp(ac_h)
            d_h = jnp.sum(d_ref[...] * oh_l, axis=1, keepdims=True)
            ys.append(y_diag + y_off + xs_h * d_h)
            a_end = ac_h[CHUNK - 1:CHUNK, :]
            xds.append(xd * jnp.exp(a_end - ac_h))
            e_ends.append(jnp.exp(a_end))
        s_c = lax.dot_general(Bb.astype(MXU), jnp.concatenate(xds, axis=1).astype(MXU), (((0,), (0,)), ((), ())),
                              preferred_element_type=f32)
        for j in range(HPG):
            st_ref[0, j] = prevs[j]
            state[g * HPG + j] = prevs[j] * e_ends[j] + s_c[:, HP * j:HP * (j + 1)]
        y_ref[...] = jnp.concatenate(ys, axis=1)

    par = pl.BlockSpec((1, LANES), lambda c, g: (0, 0))
    return _call(
        body, [xs, Bm, Cm, dt_pre, dt_bias, a_log, d_skip], comm=comm, name="ssd_fwd", grid=(NCHUNK, GROUPS),
        in_specs=[pl.BlockSpec((CHUNK, HPG * HP), lambda c, g: (c, g)),
                  pl.BlockSpec((CHUNK, NSTATE), lambda c, g: (c, g)),
                  pl.BlockSpec((CHUNK, NSTATE), lambda c, g: (c, g)),
                  pl.BlockSpec((CHUNK, LANES), lambda c, g: (c, 0)), par, par, par],
        out_specs=[pl.BlockSpec((CHUNK, HPG * HP), lambda c, g: (c, g)),
                   pl.BlockSpec((1, HPG, NSTATE, HP), lambda c, g: (c, g, 0, 0))],
        out_shape=[jax.ShapeDtypeStruct((S, INNER), f32), jax.ShapeDtypeStruct((NCHUNK, HEADS, NSTATE, HP), f32)],
        scratch=[pltpu.VMEM((HEADS, NSTATE, HP), f32), pltpu.VMEM((CHUNK, LANES), f32),
                 pltpu.VMEM((CHUNK, LANES), f32), pltpu.VMEM((LANES, CHUNK), f32)],
        sem=("arbitrary", "arbitrary"))


def ssd_bwd(xs, Bm, Cm, dt_pre, dt_bias, a_log, d_skip, states, dy, comm=None):
    rev = lambda c: NCHUNK - 1 - c

    def body(xs_ref, b_ref, c_ref, dtp_ref, bias_ref, alog_ref, d_ref, st_ref, dy_ref,
             dxs_ref, db_ref, dc_ref, ddt_ref, dbias_ref, dalog_ref, dd_ref,
             dstate, dt_s, acum_s, acumT_s, dacum_s, ddt_s, da_s):
        c, g = pl.program_id(0), pl.program_id(1)

        @pl.when(g == 0)
        def _():
            _ssd_prep(dtp_ref[...], bias_ref[...], alog_ref[...], dt_s, acum_s, acumT_s)
            dacum_s[...] = jnp.zeros_like(dacum_s)
            ddt_s[...] = jnp.zeros_like(ddt_s)

        @pl.when((c == 0) & (g == 0))
        def _():
            da_s[...] = jnp.zeros_like(da_s)
            dd_ref[...] = jnp.zeros_like(dd_ref)
            dbias_ref[...] = jnp.zeros_like(dbias_ref)
            dalog_ref[...] = jnp.zeros_like(dalog_ref)

        row = lax.broadcasted_iota(jnp.int32, (CHUNK, CHUNK), 0)
        col = lax.broadcasted_iota(jnp.int32, (CHUNK, CHUNK), 1)
        sub_l = lax.broadcasted_iota(jnp.int32, (CHUNK, 1), 0)
        last = (sub_l == CHUNK - 1).astype(f32)
        nt = (((1,), (1,)), ((), ()))
        tn = (((0,), (0,)), ((), ()))
        Bb, Cb = b_ref[...], c_ref[...]
        Bm_, Cm_ = Bb.astype(MXU), Cb.astype(MXU)
        cb = lax.dot_general(Cm_, Bm_, nt, preferred_element_type=f32)
        bc = lax.dot_general(Bm_, Cm_, nt, preferred_element_type=f32)
        xs_blk, dy_blk = xs_ref[...], dy_ref[...]
        dxs, dB, dC = [], jnp.zeros((CHUNK, NSTATE), f32), jnp.zeros((CHUNK, NSTATE), f32)
        for j in range(HPG):
            h = g * HPG + j
            oh_l, dt_h, ac_h, acr_h = _head_cols(h, dt_s, acum_s, acumT_s)

            @pl.when(c == 0)
            def _():
                dstate[h] = jnp.zeros((NSTATE, HP), f32)

            dnext = dstate[h]
            prev = st_ref[0, j]
            lm = jnp.exp(jnp.where(col <= row, ac_h - acr_h, -1e30))
            lmT = jnp.exp(jnp.where(row <= col, acr_h - ac_h, -1e30))
            xs_h = xs_blk[:, HP * j:HP * (j + 1)]
            dy_h = dy_blk[:, HP * j:HP * (j + 1)]
            xd = xs_h * dt_h
            xdm, dym = xd.astype(MXU), dy_h.astype(MXU)
            ea = jnp.exp(ac_h)
            a_end = ac_h[CHUNK - 1:CHUNK, :]
            e_end = jnp.exp(a_end)
            dte = jnp.exp(a_end - ac_h)
            dnm, pvm = dnext.astype(MXU), prev.astype(MXU)
            bd = jnp.dot(Bm_, dnm, preferred_element_type=f32)
            dxd = jnp.dot((bc * lmT).astype(MXU), dym, preferred_element_type=f32) + dte * bd
            dw = lax.dot_general(dym, xdm, nt, preferred_element_type=f32)
            dwT = lax.dot_general(xdm, dym, nt, preferred_element_type=f32)
            dcb = dw * lm
            dbc = dwT * lmT
            eady = (ea * dy_h).astype(MXU)
            dC = dC + jnp.dot(dcb.astype(MXU), Bm_, preferred_element_type=f32) \
                + lax.dot_general(eady, pvm, nt, preferred_element_type=f32)
            dB = dB + jnp.dot(dbc.astype(MXU), Cm_, preferred_element_type=f32) \
                + dte * lax.dot_general(xdm, dnm, nt, preferred_element_type=f32)
            dstate[h] = lax.dot_general(Cm_, eady, tn, preferred_element_type=f32) + e_end * dnext
            r1 = jnp.sum(dcb * cb, axis=1, keepdims=True)
            r2 = jnp.sum(dbc * bc, axis=1, keepdims=True)
            y_off = jnp.dot(Cm_, pvm, preferred_element_type=f32) * ea
            t3 = jnp.sum(dy_h * y_off, axis=1, keepdims=True)
            t4 = jnp.sum(bd * xd, axis=1, keepdims=True) * dte
            end_extra = jnp.sum(t4, axis=0, keepdims=True) + e_end * jnp.sum(jnp.sum(prev * dnext, axis=1, keepdims=True), axis=0, keepdims=True)
            dacum_h = r1 - r2 + t3 - t4 + last * end_extra
            dacum_s[...] += dacum_h * oh_l
            ddt_s[...] += jnp.sum(dxd * xs_h, axis=1, keepdims=True) * oh_l
            d_h = jnp.sum(d_ref[...] * oh_l, axis=1, keepdims=True)
            dxs.append(dxd * dt_h + dy_h * d_h)
            dd_ref[...] += oh_l * jnp.sum(jnp.sum(dy_h * xs_h, axis=1, keepdims=True), axis=0, keepdims=True)
        dxs_ref[...] = jnp.concatenate(dxs, axis=1)
        db_ref[...] = dB
        dc_ref[...] = dC

        @pl.when(g == GROUPS - 1)
        def _():
            a_row = -jnp.exp(alog_ref[...])
            da = jnp.dot((row <= col).astype(f32), dacum_s[...], precision=lax.Precision.HIGHEST, preferred_element_type=f32)
            da_s[...] += jnp.sum(da * dt_s[...], axis=0, keepdims=True)
            z = dtp_ref[...] + bias_ref[...]
            ddt_pre = (ddt_s[...] + da * a_row) * _sigmoid(z)
            ddt_ref[...] = ddt_pre.astype(ddt_ref.dtype)
            dbias_ref[...] += jnp.sum(ddt_pre, axis=0, keepdims=True)

            @pl.when(c == NCHUNK - 1)
            def _():
                dalog_ref[...] = da_s[...] * a_row

    par = pl.BlockSpec((1, LANES), lambda c, g: (0, 0))
    return _call(
        body, [xs, Bm, Cm, dt_pre, dt_bias, a_log, d_skip, states, dy], comm=comm, name="ssd_bwd", grid=(NCHUNK, GROUPS),
        in_specs=[pl.BlockSpec((CHUNK, HPG * HP), lambda c, g: (rev(c), g)),
                  pl.BlockSpec((CHUNK, NSTATE), lambda c, g: (rev(c), g)),
                  pl.BlockSpec((CHUNK, NSTATE), lambda c, g: (rev(c), g)),
                  pl.BlockSpec((CHUNK, LANES), lambda c, g: (rev(c), 0)), par, par, par,
                  pl.BlockSpec((1, HPG, NSTATE, HP), lambda c, g: (rev(c), g, 0, 0)),
                  pl.BlockSpec((CHUNK, HPG * HP), lambda c, g: (rev(c), g))],
        out_specs=[pl.BlockSpec((CHUNK, HPG * HP), lambda c, g: (rev(c), g)),
                   pl.BlockSpec((CHUNK, NSTATE), lambda c, g: (rev(c), g)),
                   pl.BlockSpec((CHUNK, NSTATE), lambda c, g: (rev(c), g)),
                   pl.BlockSpec((CHUNK, LANES), lambda c, g: (rev(c), 0)), par, par, par],
        out_shape=[jax.ShapeDtypeStruct((S, INNER), f32), jax.ShapeDtypeStruct((S, GROUPS * NSTATE), f32),
                   jax.ShapeDtypeStruct((S, GROUPS * NSTATE), f32), jax.ShapeDtypeStruct((S, LANES), MXU),
                   jax.ShapeDtypeStruct((1, LANES), f32), jax.ShapeDtypeStruct((1, LANES), f32),
                   jax.ShapeDtypeStruct((1, LANES), f32)],
        scratch=[pltpu.VMEM((HEADS, NSTATE, HP), f32), pltpu.VMEM((CHUNK, LANES), f32),
                 pltpu.VMEM((CHUNK, LANES), f32), pltpu.VMEM((LANES, CHUNK), f32),
                 pltpu.VMEM((CHUNK, LANES), f32), pltpu.VMEM((CHUNK, LANES), f32), pltpu.VMEM((1, LANES), f32)],
        sem=("arbitrary", "arbitrary"))


ATT_STACK_FWD, ATT_STACK_BWD = 4, 2


def _attn_kv(kp, kc, vp, vc, hk):
    sl = slice(AH * hk, AH * (hk + 1))
    return (jnp.concatenate([kp[:, sl], kc[:, sl]], axis=0).astype(MXU),
            jnp.concatenate([vp[:, sl], vc[:, sl]], axis=0).astype(MXU))


def _stack_heads(x, heads):
    return jnp.concatenate([x[:, AH * h:AH * (h + 1)] for h in heads], axis=0)


def _attn_block(n, q, kb, sinks, heads):
    rows = len(heads) * WIN
    qi = lax.broadcasted_iota(jnp.int32, (rows, 2 * WIN), 0) & (WIN - 1)
    ki = lax.broadcasted_iota(jnp.int32, (rows, 2 * WIN), 1)
    rel = qi + WIN - ki
    mask = (rel >= 0) & (rel < WIN) & ((ki >= WIN) | (n > 0))
    qg = _stack_heads(q, heads).astype(MXU)
    s = lax.dot_general(qg, kb, (((1,), (1,)), ((), ())), preferred_element_type=f32) * (AH ** -0.5)
    s = jnp.where(mask, s, -1e30)
    sink = jnp.concatenate([jnp.broadcast_to(sinks[:, h:h + 1], (WIN, 1)) for h in heads], axis=0)
    m = jnp.maximum(jnp.max(s, axis=1, keepdims=True), sink)
    p = jnp.exp(s - m)
    ps = jnp.exp(sink - m)
    inv = 1.0 / (jnp.sum(p, axis=1, keepdims=True) + ps)
    return qg, p * inv, ps * inv


def _head_blocks(hk, stack):
    return [list(range(QPK * hk + i, QPK * hk + i + stack)) for i in range(0, QPK, stack)]


def _kv_specs():
    prev = lambda n: (jnp.maximum(n - 1, 0), 0)
    cur = lambda n: (n, 0)
    w = KVH * AH
    return [pl.BlockSpec((WIN, w), prev), pl.BlockSpec((WIN, w), cur), pl.BlockSpec((WIN, w), prev), pl.BlockSpec((WIN, w), cur)]


def attn_fwd(q, k, v, sinks, comm=None):
    def body(q_ref, kp_ref, kc_ref, vp_ref, vc_ref, s_ref, o_ref):
        n = pl.program_id(0)
        q_, kp, kc, vp, vc, sk = q_ref[...], kp_ref[...], kc_ref[...], vp_ref[...], vc_ref[...], s_ref[...]
        outs = []
        for hk in range(KVH):
            kb, vb = _attn_kv(kp, kc, vp, vc, hk)
            for heads in _head_blocks(hk, ATT_STACK_FWD):
                _, pr, _ = _attn_block(n, q_, kb, sk, heads)
                o = jnp.dot(pr.astype(MXU), vb, preferred_element_type=f32)
                outs += [o[WIN * i:WIN * (i + 1)] for i in range(len(heads))]
        o_ref[...] = jnp.concatenate(outs, axis=1)

    return _call(
        body, [q, k, k, v, v, sinks], comm=comm, name="attn_fwd", grid=(NBLK,),
        in_specs=[pl.BlockSpec((WIN, D), lambda n: (n, 0))] + _kv_specs() + [pl.BlockSpec((1, QH), lambda n: (0, 0))],
        out_specs=[pl.BlockSpec((WIN, D), lambda n: (n, 0))],
        out_shape=[jax.ShapeDtypeStruct((S, D), f32)], sem=("parallel",))[0]


def attn_bwd(q, k, v, sinks, dout, comm=None):
    def body(q_ref, kp_ref, kc_ref, vp_ref, vc_ref, s_ref, do_ref, dq_ref, dkp_ref, dkc_ref, dvp_ref, dvc_ref, ds_ref):
        n = pl.program_id(0)

        @pl.when(n == 0)
        def _():
            ds_ref[...] = jnp.zeros_like(ds_ref)

        q_, kp, kc, vp, vc, sk, do = q_ref[...], kp_ref[...], kc_ref[...], vp_ref[...], vc_ref[...], s_ref[...], do_ref[...]
        lane = lax.broadcasted_iota(jnp.int32, (1, QH), 1)
        nt = (((1,), (1,)), ((), ()))
        tn = (((0,), (0,)), ((), ()))
        dqs, dkps, dkcs, dvps, dvcs = [], [], [], [], []
        dsink = jnp.zeros((1, QH), f32)
        for hk in range(KVH):
            kb, vb = _attn_kv(kp, kc, vp, vc, hk)
            dkb, dvb = jnp.zeros((2 * WIN, AH), f32), jnp.zeros((2 * WIN, AH), f32)
            for heads in _head_blocks(hk, ATT_STACK_BWD):
                qg, pr, prs = _attn_block(n, q_, kb, sk, heads)
                dog = _stack_heads(do, heads).astype(MXU)
                dp = lax.dot_general(dog, vb, nt, preferred_element_type=f32)
                dvb = dvb + lax.dot_general(pr.astype(MXU), dog, tn, preferred_element_type=f32)
                delta = jnp.sum(pr * dp, axis=1, keepdims=True)
                ds = (pr * (dp - delta)).astype(MXU)
                dsk = -prs * delta
                for i, h in enumerate(heads):
                    dsink = dsink + jnp.sum(dsk[WIN * i:WIN * (i + 1)], axis=0, keepdims=True) * (lane == h).astype(f32)
                dqg = jnp.dot(ds, kb, preferred_element_type=f32) * (AH ** -0.5)
                dkb = dkb + lax.dot_general(ds, qg, tn, preferred_element_type=f32) * (AH ** -0.5)
                dqs += [dqg[WIN * i:WIN * (i + 1)] for i in range(len(heads))]
            dkps.append(dkb[:WIN])
            dkcs.append(dkb[WIN:])
            dvps.append(dvb[:WIN])
            dvcs.append(dvb[WIN:])
        dq_ref[...] = jnp.concatenate(dqs, axis=1)
        dkp_ref[...] = jnp.concatenate(dkps, axis=1)
        dkc_ref[...] = jnp.concatenate(dkcs, axis=1)
        dvp_ref[...] = jnp.concatenate(dvps, axis=1)
        dvc_ref[...] = jnp.concatenate(dvcs, axis=1)
        ds_ref[...] += dsink

    w = KVH * AH
    blk = lambda width: pl.BlockSpec((WIN, width), lambda n: (n, 0))
    return _call(
        body, [q, k, k, v, v, sinks, dout], comm=comm, name="attn_bwd", grid=(NBLK,),
        in_specs=[blk(D)] + _kv_specs() + [pl.BlockSpec((1, QH), lambda n: (0, 0)), blk(D)],
        out_specs=[blk(D), blk(w), blk(w), blk(w), blk(w), pl.BlockSpec((1, QH), lambda n: (0, 0))],
        out_shape=[jax.ShapeDtypeStruct((S, D), f32)] + [jax.ShapeDtypeStruct((S, w), f32)] * 4 + [jax.ShapeDtypeStruct((1, QH), f32)],
        sem=("arbitrary",))


def kv_bwd(kv, pos, inv_freq, k_norm, dkp, dkc, dvp, dvc):
    w = KVH * AH

    def body(kv_ref, pos_ref, if_ref, g_ref, dkp_ref, dkc_ref, dvp_ref, dvc_ref, o_ref, dg_ref, db_ref):
        n = pl.program_id(0)

        @pl.when(n == 0)
        def _():
            dg_ref[...] = jnp.zeros_like(dg_ref)
            db_ref[...] = jnp.zeros_like(db_ref)

        inside = (n < NBLK - 1).astype(f32)
        dk = dkc_ref[...] + inside * dkp_ref[...]
        dv = dvc_ref[...] + inside * dvp_ref[...]
        cos, sin = _rope_tables(pos_ref[...], if_ref[...])
        dkpre, dg = _headnorm_rope_bwd(kv_ref[...], g_ref[...], cos, sin, dk, KVH)
        dkv = jnp.concatenate([dkpre, dv], axis=1)
        o_ref[...] = dkv.astype(o_ref.dtype)
        dg_ref[...] += dg
        db_ref[...] += jnp.sum(dkv, axis=0, keepdims=True)

    nxt = lambda n: (jnp.minimum(n + 1, NBLK - 1), 0)
    cur = lambda n: (n, 0)
    const = lambda n: (0, 0)
    return pl.pallas_call(
        body, name="kv_bwd", grid=(NBLK,),
        in_specs=[pl.BlockSpec((WIN, w), cur), pl.BlockSpec((WIN, 1), cur), pl.BlockSpec((1, AH // 2), const),
                  pl.BlockSpec((1, AH), const), pl.BlockSpec((WIN, w), nxt), pl.BlockSpec((WIN, w), cur),
                  pl.BlockSpec((WIN, w), nxt), pl.BlockSpec((WIN, w), cur)],
        out_specs=[pl.BlockSpec((WIN, 2 * w), cur), pl.BlockSpec((1, AH), const), pl.BlockSpec((1, 2 * w), const)],
        out_shape=[jax.ShapeDtypeStruct((S, 2 * w), MXU), jax.ShapeDtypeStruct((1, AH), f32), jax.ShapeDtypeStruct((1, 2 * w), f32)],
        compiler_params=_cparams(("arbitrary",)),
    )(kv, pos, inv_freq, k_norm, dkp, dkc, dvp, dvc)


def adamw(name, w, g, m, v):
    R, C = w.shape
    tr = _pick(R, (256, 128, 64, 32, 16, 8))
    tc = C if tr < R or C % 256 else 256

    def body(w_ref, g_ref, m_ref, v_ref, d_ref, nm_ref, nv_ref):
        g_ = g_ref[...]
        m_ = ADAM_B1 * m_ref[...] + (1.0 - ADAM_B1) * g_
        v_ = ADAM_B2 * v_ref[...] + (1.0 - ADAM_B2) * (g_ * g_)
        m_hat = m_ / (1.0 - ADAM_B1 ** ADAM_STEP)
        v_hat = v_ / (1.0 - ADAM_B2 ** ADAM_STEP)
        d_ref[...] = -ADAM_LR * (m_hat / (jnp.sqrt(v_hat) + ADAM_EPS) + ADAM_WD * w_ref[...])
        nm_ref[...] = m_
        nv_ref[...] = v_

    spec = pl.BlockSpec((tr, tc), lambda i, j: (i, j))
    return pl.pallas_call(
        body, name=name, grid=(R // tr, C // tc), in_specs=[spec] * 4, out_specs=[spec] * 3,
        out_shape=[jax.ShapeDtypeStruct((R, C), f32)] * 3, compiler_params=_cparams(("parallel", "parallel")),
    )(w, g, m, v)


def _me():
    return lax.axis_index("x"), lax.axis_index("y"), lax.axis_index("c")


def gather_comm(xs):
    n = len(xs)

    def parts(x_refs, o_refs, sems):
        send_sems, recv_sems, local_sems = sems
        x, y, c = _me()
        me, sibling = (x, y, c), (x, y, 1 - c)
        chips = [(1 - x, y), (x, 1 - y), (1 - x, 1 - y)]

        def copy(a, k, block, to, src=None):
            dst = o_refs[a].at[4 * block[0] + 2 * block[1] + block[2]]
            return pltpu.make_async_remote_copy(
                src_ref=dst if src is None else src, dst_ref=dst,
                send_sem=send_sems.at[7 * a + k], recv_sem=recv_sems.at[7 * a + k], device_id=to, device_id_type=MESH)

        mine = [pltpu.make_async_copy(x_refs[a], o_refs[a].at[4 * x + 2 * y + c], local_sems.at[a]) for a in range(n)]
        first = []
        for a in range(n):
            first.append(copy(a, 0, me, sibling, src=x_refs[a]))
            first += [copy(a, 1 + j, me, (*chip, c), src=x_refs[a]) for j, chip in enumerate(chips)]
        return copy, mine, first, me, sibling, chips, c

    def start(x_refs, o_refs, sems):
        _, mine, first, *_ = parts(x_refs, o_refs, sems)
        for cp in mine + first:
            cp.start()

    def finish(x_refs, o_refs, sems):
        copy, mine, first, me, sibling, chips, c = parts(x_refs, o_refs, sems)
        passed = []
        for j, chip in enumerate(chips):
            for a in range(n):
                copy(a, 1 + j, (*chip, c), me).wait_recv()
                cp = copy(a, 4 + j, (*chip, c), sibling)
                cp.start()
                passed.append(cp)
        for a in range(n):
            copy(a, 0, sibling, me).wait_recv()
            for j, chip in enumerate(chips):
                copy(a, 4 + j, (*chip, 1 - c), me).wait_recv()
        for cp in first + passed:
            cp.wait_send()
        for cp in mine:
            cp.wait()

    return Comm(xs, [jax.ShapeDtypeStruct((N_DEV,) + a.shape, a.dtype) for a in xs],
                [pltpu.SemaphoreType.DMA((7 * n,)), pltpu.SemaphoreType.DMA((7 * n,)), pltpu.SemaphoreType.DMA((n,))], start, finish)


def run_comm(name, comm):
    _call(lambda: None, [], name=name, grid=(1,), in_specs=[], out_specs=[], out_shape=[], comm=comm)
    return comm.results


def sibling_comm(gs):
    n = len(gs)

    def copies(g_refs, o_refs, sems):
        x, y, c = _me()
        return [pltpu.make_async_remote_copy(
            src_ref=g_refs[a].at[:, 1 - c], dst_ref=o_refs[a], send_sem=sems[0].at[a], recv_sem=sems[1].at[a],
            device_id=(x, y, 1 - c), device_id_type=MESH) for a in range(n)]

    def start(g_refs, o_refs, sems):
        for cp in copies(g_refs, o_refs, sems):
            cp.start()

    def finish(g_refs, o_refs, sems):
        for cp in copies(g_refs, o_refs, sems):
            cp.wait()

    return Comm(gs, [jax.ShapeDtypeStruct((4,) + g.shape[2:], g.dtype) for g in gs],
                [pltpu.SemaphoreType.DMA((n,)), pltpu.SemaphoreType.DMA((n,))], start, finish)


def chip_comm(ts):
    n = len(ts)

    def copies(t_refs, o_refs, sems):
        x, y, c = _me()
        chips = [(1 - x, y), (x, 1 - y), (1 - x, 1 - y)]
        return [pltpu.make_async_remote_copy(
            src_ref=t_refs[a].at[2 * px + py], dst_ref=o_refs[a].at[j],
            send_sem=sems[0].at[3 * a + j], recv_sem=sems[1].at[3 * a + j],
            device_id=(px, py, c), device_id_type=MESH) for j, (px, py) in enumerate(chips) for a in range(n)]

    def start(t_refs, o_refs, sems):
        for cp in copies(t_refs, o_refs, sems):
            cp.start()

    def finish(t_refs, o_refs, sems):
        for cp in copies(t_refs, o_refs, sems):
            cp.wait()

    return Comm(ts, [jax.ShapeDtypeStruct((3,) + t.shape[1:], t.dtype) for t in ts],
                [pltpu.SemaphoreType.DMA((3 * n,)), pltpu.SemaphoreType.DMA((3 * n,))], start, finish)


def _row_tile(rows):
    return _pick(rows, (512, 304, 256, 128))


def pair_add(name, g, r):
    _, _, R, C = g.shape
    tr = _row_tile(R)

    def body(c_ref, g_ref, r_ref, o_ref):
        o_ref[0] = (g_ref[0, 0].astype(f32) + r_ref[0].astype(f32)).astype(o_ref.dtype)

    return pl.pallas_call(
        body, name=name,
        grid_spec=pltpu.PrefetchScalarGridSpec(
            num_scalar_prefetch=1, grid=(4, R // tr),
            in_specs=[pl.BlockSpec((1, 1, tr, C), lambda p, i, c: (p, c[0], i, 0)),
                      pl.BlockSpec((1, tr, C), lambda p, i, c: (p, i, 0))],
            out_specs=pl.BlockSpec((1, tr, C), lambda p, i, c: (p, i, 0))),
        out_shape=jax.ShapeDtypeStruct((4, R, C), g.dtype),
        compiler_params=_cparams(("parallel", "parallel")),
    )(lax.axis_index("c").reshape(1).astype(jnp.int32), g, r)


def final_add(name, t, r):
    _, R, C = t.shape
    tr = _row_tile(R)

    def body(p_ref, t_ref, r_ref, o_ref):
        o_ref[...] = ((t_ref[0].astype(f32) + r_ref[0].astype(f32)) + r_ref[1].astype(f32)) + r_ref[2].astype(f32)

    chip = 2 * lax.axis_index("x") + lax.axis_index("y")
    return pl.pallas_call(
        body, name=name,
        grid_spec=pltpu.PrefetchScalarGridSpec(
            num_scalar_prefetch=1, grid=(R // tr,),
            in_specs=[pl.BlockSpec((1, tr, C), lambda i, p: (p[0], i, 0)),
                      pl.BlockSpec((3, tr, C), lambda i, p: (0, i, 0))],
            out_specs=pl.BlockSpec((tr, C), lambda i, p: (i, 0))),
        out_shape=jax.ShapeDtypeStruct((R, C), f32),
        compiler_params=_cparams(("parallel",)),
    )(chip.reshape(1).astype(jnp.int32), t, r)


class ReduceScatter:
    def __init__(self, tag, keys, grads):
        self.tag, self.keys, self.grads = tag, keys, grads
        self.send = [g.reshape((4, 2, g.shape[0] // N_DEV) + g.shape[1:]) for g in grads]

    def sibling(self):
        self.c1 = sibling_comm(self.send)
        return self.c1

    def chips(self):
        self.pairs = [pair_add(f"rs_pair_add_{self.tag}{i}", g, r) for i, (g, r) in enumerate(zip(self.send, self.c1.results))]
        self.c2 = chip_comm(self.pairs)
        return self.c2

    def sums(self):
        return {k: final_add(f"rs_final_add_{k}", t, r) for k, t, r in zip(self.keys, self.pairs, self.c2.results)}


IN_ROWS = {"z": (0, 2048), "xs": (2048, 4096), "B": (4096, 5120), "C": (5120, 6144)}
IN_COLS = 2 * INNER + 2 * GROUPS * NSTATE + HEADS


def sum_devices(g):
    def body(g_ref, o_ref):
        acc = g_ref[0]
        for i in range(1, N_DEV):
            acc = acc + g_ref[i]
        o_ref[...] = acc

    return pl.pallas_call(body, name="sum_devices", out_shape=jax.ShapeDtypeStruct(g.shape[1:], f32),
                          compiler_params=_cparams())(g)


def _pack(parts, unit, dtype, lead=()):
    flat = jnp.concatenate([p.reshape(lead + (-1,)).astype(dtype) for p in parts], axis=-1)
    n = flat.shape[-1]
    rows = -(-n // (unit * PACK_W)) * unit
    flat = jnp.pad(flat, [(0, 0)] * len(lead) + [(0, rows * PACK_W - n)])
    return flat.reshape(lead + (rows, PACK_W))


def _unpack(buf, shapes, lead=()):
    flat = buf.reshape(lead + (-1,))
    out, off = [], 0
    for shp in shapes:
        n = math.prod(shp)
        out.append(flat[..., off:off + n].reshape(lead + tuple(shp)))
        off += n
    return out


def _pad_lanes(a):
    return jnp.pad(a, [(0, 0)] * (a.ndim - 1) + [(0, LANES - a.shape[-1])])


_NN = (((1,), (0,)), ((), ()))
_NT = (((1,), (1,)), ((), ()))


def _mm(a, b, dims):
    return lax.dot_general(a.astype(MXU), b.astype(MXU), dims, preferred_element_type=f32)


def _ffn_fwd(tag, x, norm_g, w_inT, conv_w, conv_b, mid_comm=None):
    (h,) = rowwise(f"{tag}_norm", lambda x_, g_: (_rms_fwd(x_, g_),), [x], [norm_g], [(D, MXU)])
    ct, nblk = FFN_CT, FFN // FFN_CT

    def body(h_ref, wg_ref, wv_ref, cw_ref, cb_ref, gp_ref, v_ref, a_ref):
        h_ = h_ref[...]
        gp, v_ = _mm(h_, wg_ref[...], _NT), _mm(h_, wv_ref[...], _NT)
        gp_ref[...] = gp
        v_ref[...] = v_
        a_ref[...] = (_silu(_conv(gp, cw_ref[...], cb_ref[...])) * v_).astype(a_ref.dtype)

    col = pl.BlockSpec((S, ct), lambda j: (0, j))
    gate_pre, val, act = _call(
        body, [h, w_inT, w_inT, conv_w, conv_b], comm=mid_comm, name=f"{tag}_in", grid=(nblk,),
        in_specs=[pl.BlockSpec((S, D), lambda j: (0, 0)), pl.BlockSpec((ct, D), lambda j: (j, 0)),
                  pl.BlockSpec((ct, D), lambda j: (nblk + j, 0)), pl.BlockSpec((CONV_F, ct), lambda j: (0, j)),
                  pl.BlockSpec((1, ct), lambda j: (0, j))],
        out_specs=[col, col, col],
        out_shape=[jax.ShapeDtypeStruct((S, FFN), f32), jax.ShapeDtypeStruct((S, FFN), f32), jax.ShapeDtypeStruct((S, FFN), MXU)],
        sem=("parallel",))
    return act, (x, h, gate_pre, val, act)


FFN_CT = 256


def _ffn_mid_bwd(name, dout, w_down, gate_pre, val, conv_w, conv_b, comm=None):
    ct = FFN_CT

    def body(do_ref, wd_ref, gp_ref, v_ref, w_ref, b_ref, dgv_ref, dw_ref, db_ref, dob_s):
        @pl.when(pl.program_id(0) == 0)
        def _():
            dob_s[...] = do_ref[...].astype(MXU)

        da = _mm(dob_s[...], wd_ref[...], _NT)
        gp, v_, w_ = gp_ref[...], v_ref[...], w_ref[...]
        taps = _taps(gp, CONV_F)
        gate = _conv(gp, w_, b_ref[...], taps)
        sg = _sigmoid(gate)
        dgp, dw, db = _conv_bwd(gp, w_, da * v_ * (sg * (1.0 + gate * (1.0 - sg))), taps)
        dgv_ref[0] = dgp.astype(dgv_ref.dtype)
        dgv_ref[1] = (da * (gate * sg)).astype(dgv_ref.dtype)
        dw_ref[...] = dw
        db_ref[...] = db

    col = pl.BlockSpec((S, ct), lambda j: (0, j))
    return _call(
        body, [dout, w_down, gate_pre, val, conv_w, conv_b], comm=comm, name=name, grid=(FFN // ct,),
        in_specs=[pl.BlockSpec((S, D), lambda j: (0, 0)), pl.BlockSpec((ct, D), lambda j: (j, 0)), col, col,
                  pl.BlockSpec((CONV_F, ct), lambda j: (0, j)), pl.BlockSpec((1, ct), lambda j: (0, j))],
        out_specs=[pl.BlockSpec((2, S, ct), lambda j: (0, 0, j)), pl.BlockSpec((CONV_F, ct), lambda j: (0, j)),
                   pl.BlockSpec((1, ct), lambda j: (0, j))],
        out_shape=[jax.ShapeDtypeStruct((2, S, FFN), MXU), jax.ShapeDtypeStruct((CONV_F, FFN), f32), jax.ShapeDtypeStruct((1, FFN), f32)],
        scratch=[pltpu.VMEM((S, D), MXU)], sem=("arbitrary",))


def _ffn_bwd(tag, layer, saved, norm_g, w_inT, conv_w, conv_b, w_down, dout, mid_comm=None):
    x, h, gate_pre, val, act = saved
    g_down = matmul(f"{tag}_wdown", act, dout, "tn", out_dtype=MXU)
    dgv, g_cw, g_cb = _ffn_mid_bwd(f"{tag}_dmid", dout, w_down, gate_pre, val, conv_w, conv_b, comm=mid_comm)
    g_inT = matmul_tn_stacked(f"{tag}_win", dgv, h, MXU)

    def din_fn(dg_, dv_, x_, do_, g_, wT):
        dx, dg = _rms_bwd(x_, g_, _mm(dg_, wT[:FFN], _NN) + _mm(dv_, wT[FFN:], _NN))
        return do_ + dx, dg

    rs = ReduceScatter(tag, (f"f_inT{layer}", f"f_down{layer}"), [g_inT, g_down])
    dx, g_norm = rowwise(f"{tag}_din", din_fn, [(dgv, 0), (dgv, 1), x, dout], [norm_g, w_inT], [(D, f32)], [(1, D)],
                         comm=rs.sibling())
    return dx, {f"f_norm{layer}": g_norm, f"f_conv_w{layer}": g_cw, f"f_conv_b{layer}": g_cb}, rs


def _land(W, keys, comm):
    for k, g in zip(keys, comm.results):
        W[k] = g.reshape(-1, g.shape[2])


def _local_step(x, pos, tgt, W, shards):
    G = {}
    gather = lambda *keys: gather_comm([shards[k] for k in keys])
    inv_freq = (ROPE_THETA ** (-jnp.arange(AH // 2, dtype=f32) / (AH // 2))).reshape(1, AH // 2)

    def in_fn(x_, g_, wT, wdtT):
        h_ = _rms_fwd(x_, g_).astype(MXU)
        return (h_,) + tuple(_mm(h_, wT[a:b], _NT) for a, b in IN_ROWS.values()) + (_mm(h_, wdtT, _NT),)

    pre = {}
    h0, z, pre["xs"], pre["B"], pre["C"], dt_pre = rowwise(
        "a_in", in_fn, [x], [W["a_norm"], W["inT"], W["in_dtT"]],
        [(D, MXU)] + [(b - a, f32) for a, b in IN_ROWS.values()] + [(LANES, f32)])
    conv = {}
    for k in ("xs", "B", "C"):
        (conv[k],) = colwise(f"a_conv_{k}", lambda p_, w_, b_: (_silu(_conv(p_, w_, b_)),), [pre[k]], [W[f"cw_{k}"], W[f"cb_{k}"]], [f32])
    c = gather("a_out", "f_inT0", "f_down0")
    y, states = ssd_fwd(conv["xs"], conv["B"], conv["C"], dt_pre, W["dt_bias"], W["A_log"], W["D"], comm=c)
    _land(W, ("a_out", "f_inT0", "f_down0"), c)

    def gate_norm(y_, z_, g_):
        yg = y_ * _silu(z_)
        w = INNER // GROUPS
        return (jnp.concatenate([_rms_fwd(yg[:, w * i:w * (i + 1)], g_[:, w * i:w * (i + 1)]) for i in range(GROUPS)], axis=1),)

    def out_fn(y_, z_, x_, g_, w_):
        (gn_,) = gate_norm(y_, z_, g_)
        gn_ = gn_.astype(MXU)
        return gn_, x_ + _mm(gn_, w_, _NN)

    gn, x1 = rowwise("a_out", out_fn, [y, z, x], [W["a_gnorm"], W["a_out"]], [(INNER, MXU), (D, f32)])

    c = gather("w_kv", "w_q", "w_o")
    act0, ffn0 = _ffn_fwd("f0", x1, W["f_norm0"], W["f_inT0"], W["f_cw0"], W["f_cb0"], mid_comm=c)
    _land(W, ("w_kv", "w_q", "w_o"), c)
    x2 = matmul("f0_down", act0, W["f_down0"], "nn", residual=x1)

    def qkv_fn(x_, gk, gb, wkv, bkv, wq, bq):
        kvn_, h2_ = _rms_fwd(x_, gk).astype(MXU), _rms_fwd(x_, gb).astype(MXU)
        return kvn_, h2_, _mm(kvn_, wkv, _NN) + bkv, _mm(h2_, wq, _NN) + bq

    kw = KVH * AH
    kvn, h2, kv, q_pre = rowwise("qkv_proj", qkv_fn, [x2], [W["kv_norm"], W["b_norm"], W["w_kv"], W["b_kv"], W["w_q"], W["b_q"]],
                                 [(D, MXU), (D, MXU), (2 * kw, f32), (D, f32)])

    def k_fwd(kv_, pos_, if_, g_):
        cos, sin = _rope_tables(pos_, if_)
        return _headnorm_rope_fwd(kv_[:, :kw], g_, cos, sin, KVH), kv_[:, kw:]

    k_rot, v_val = rowwise("k_rope", k_fwd, [kv, pos], [inv_freq, W["k_norm"]], [(kw, f32), (kw, f32)])

    def q_fwd(q_, pos_, if_, g_):
        cos, sin = _rope_tables(pos_, if_)
        return (_headnorm_rope_fwd(q_, g_, cos, sin, QH),)

    c = gather("f_down1")
    (q,) = rowwise("q_rope", q_fwd, [q_pre, pos], [inv_freq, W["q_norm"]], [(D, f32)], comm=c)
    _land(W, ("f_down1",), c)
    c = gather("f_inT1")
    att = attn_fwd(q, k_rot, v_val, W["sinks"], comm=c)
    _land(W, ("f_inT1",), c)
    x3 = matmul("o_proj", att, W["w_o"], "nn", bias=W["b_o"], residual=x2)

    act1, ffn1 = _ffn_fwd("f1", x3, W["f_norm1"], W["f_inT1"], W["f_cw1"], W["f_cb1"])

    def loss_fn(a_, x_, t_, w_):
        diff = x_ + _mm(a_, w_, _NN) - t_
        rows = jnp.sum(diff * diff, axis=1, keepdims=True) * (0.5 / D)
        return diff * (1.0 / D), jnp.sum(rows, axis=0, keepdims=True)

    dx4, loss = rowwise("f1_down_loss", loss_fn, [act1, x3, tgt], [W["f_down1"]], [(D, f32)], [(1, 1)])

    dx3, g, rs_f1 = _ffn_bwd("f1", 1, ffn1, W["f_norm1"], W["f_inT1"], W["f_cw1"], W["f_cb1"], W["f_down1"], dx4)
    G.update(g)

    datt = matmul("o_dproj", dx3, W["w_o"], "nt")
    g_wo = matmul("o_wproj", att, dx3, "tn", out_dtype=MXU)
    dq, dkp, dkc, dvp, dvc, G["sinks"] = attn_bwd(q, k_rot, v_val, W["sinks"], datt, comm=rs_f1.chips())

    def q_bwd(q_, pos_, dq_, dx_, if_, g_):
        cos, sin = _rope_tables(pos_, if_)
        dqp, dg = _headnorm_rope_bwd(q_, g_, cos, sin, dq_, QH)
        return dqp, dg, jnp.sum(dqp, axis=0, keepdims=True), jnp.sum(dx_, axis=0, keepdims=True)

    dq_pre, G["q_norm"], G["b_q"], G["b_o"] = rowwise("q_drope", q_bwd, [q_pre, pos, dq, dx3], [inv_freq, W["q_norm"]],
                                                      [(D, MXU)], [(1, AH), (1, D), (1, D)])
    g_wq = matmul("q_wproj", h2, dq_pre, "tn", out_dtype=MXU)
    dkv, G["k_norm"], G["b_kv"] = kv_bwd(kv, pos, inv_freq, W["k_norm"], dkp, dkc, dvp, dvc)
    g_wkv = matmul("kv_wproj", kvn, dkv, "tn", out_dtype=MXU)
    rs_att = ReduceScatter("att", ("w_kv", "w_q", "w_o"), [g_wkv, g_wq, g_wo])

    def x2_bwd(x_, dq_, dkv_, dx_, gb_, gk_, wq, wkv):
        d1, dgb = _rms_bwd(x_, gb_, _mm(dq_, wq, _NT))
        d2, dgk = _rms_bwd(x_, gk_, _mm(dkv_, wkv, _NT))
        return dx_ + d1 + d2, dgb, dgk

    dx2, G["b_norm"], G["kv_norm"] = rowwise("qkv_dproj", x2_bwd, [x2, dq_pre, dkv, dx3],
                                             [W["b_norm"], W["kv_norm"], W["w_q"], W["w_kv"]],
                                             [(D, f32)], [(1, D), (1, D)], comm=rs_att.sibling())

    dx1, g, rs_f0 = _ffn_bwd("f0", 0, ffn0, W["f_norm0"], W["f_inT0"], W["f_cw0"], W["f_cb0"], W["f_down0"], dx2,
                             mid_comm=rs_att.chips())
    G.update(g)

    dgn = matmul("a_dout", dx1, W["a_out"], "nt")
    rs_out = ReduceScatter("a_out", ("a_out",), [matmul("a_wout", gn, dx1, "tn", out_dtype=MXU)])

    def gate_norm_bwd(y_, z_, dgn_, g_):
        w = INNER // GROUPS
        sg = _sigmoid(z_)
        sz = z_ * sg
        yg = y_ * sz
        parts, dgs = [], []
        for i in range(GROUPS):
            dseg, dg = _rms_bwd(yg[:, w * i:w * (i + 1)], g_[:, w * i:w * (i + 1)], dgn_[:, w * i:w * (i + 1)])
            parts.append(dseg)
            dgs.append(dg)
        dyg = jnp.concatenate(parts, axis=1)
        return dyg * sz, dyg * y_ * (sg * (1.0 + z_ * (1.0 - sg))), jnp.concatenate(dgs, axis=1)

    dy, dz, G["a_gnorm"] = rowwise("a_dgnorm", gate_norm_bwd, [y, z, dgn], [W["a_gnorm"]], [(INNER, f32), (INNER, MXU)], [(1, INNER)],
                                   tile=128, comm=rs_out.sibling())
    dconv = {}
    dconv["xs"], dconv["B"], dconv["C"], ddt_pre, G["dt_bias"], G["A_log"], G["D"] = ssd_bwd(
        conv["xs"], conv["B"], conv["C"], dt_pre, W["dt_bias"], W["A_log"], W["D"], states, dy,
        comm=merge_comms([rs_f0.chips(), rs_out.chips()]))

    def conv_bwd(p_, do_, w_, b_):
        taps = _taps(p_, CONV_A)
        return _conv_bwd(p_, w_, do_ * _dsilu(_conv(p_, w_, b_, taps)), taps)

    g_in, dpre = [matmul("a_win_z", dz, h0, "tn", out_dtype=MXU)], {}
    for k in ("xs", "B", "C"):
        dpre[k], G[f"cw_{k}"], G[f"cb_{k}"] = colwise(f"a_dconv_{k}", conv_bwd, [pre[k], dconv[k]], [W[f"cw_{k}"], W[f"cb_{k}"]], [MXU], [CONV_A, 1])
        g_in.append(matmul(f"a_win_{k}", dpre[k], h0, "tn", out_dtype=MXU))
    g_in.append(matmul("a_win_dt", ddt_pre, h0, "tn", out_dtype=MXU)[:HEADS])
    rs_in = ReduceScatter("a_in", ("inT",), [jnp.concatenate(g_in, axis=0)])

    def x0_bwd(dz_, dxs_, db_, dc_, ddt_, x_, do_, g_, wT, wdtT):
        parts = zip((dz_, dxs_, db_, dc_), IN_ROWS.values())
        dh = sum(_mm(d_, wT[a:b], _NN) for d_, (a, b) in parts) + _mm(ddt_, wdtT, _NN)
        dx, dg = _rms_bwd(x_, g_, dh)
        return do_ + dx, dg

    dx, G["a_norm"] = rowwise("a_din", x0_bwd, [dz, dpre["xs"], dpre["B"], dpre["C"], ddt_pre, x, dx1],
                              [W["a_norm"], W["inT"], W["in_dtT"]], [(D, f32)], [(1, D)], comm=rs_in.sibling())
    return loss, dx, G, [rs_f1, rs_att, rs_f0, rs_out, rs_in]


ROW_KEYS = ("inT", "a_out", "f_inT0", "f_down0", "w_kv", "w_q", "w_o", "f_inT1", "f_down1")


def _row_blocks(src):
    return {"inT": src["a_in_proj"][0].T, "a_out": src["a_out_proj"][0], "w_kv": src["w_kv"], "w_q": src["w_q"][0],
            "w_o": src["w_o"][0], "f_inT0": src["f_w_in"][0].T, "f_inT1": src["f_w_in"][1].T,
            "f_down0": src["f_w_down"][0], "f_down1": src["f_w_down"][1]}


def _from_row_blocks(rb):
    return {"a_in_proj": rb["inT"].T[None], "a_out_proj": rb["a_out"][None], "w_kv": rb["w_kv"], "w_q": rb["w_q"][None],
            "w_o": rb["w_o"][None], "f_w_in": jnp.stack([rb["f_inT0"].T, rb["f_inT1"].T]),
            "f_w_down": jnp.stack([rb["f_down0"], rb["f_down1"]])}


SMALL_SHARDED = ("a_norm", "a_conv_w", "a_conv_b", "a_gnorm", "f_conv_w")
REPLICATED = ("a_dt_bias", "a_A_log", "a_D", "kv_norm", "b_kv", "k_norm", "b_norm", "b_q", "q_norm", "sinks", "b_o",
              "f_norm", "f_conv_b")
ORDER = ("a_norm", "a_in_proj", "a_conv_w", "a_conv_b", "a_dt_bias", "a_A_log", "a_D", "a_gnorm", "a_out_proj", "kv_norm",
         "w_kv", "b_kv", "k_norm", "b_norm", "w_q", "b_q", "q_norm", "sinks", "w_o", "b_o", "f_norm", "f_w_in",
         "f_conv_w", "f_conv_b", "f_w_down")


def _gathered_to_whole(name, g):
    if name == "a_conv_w":
        return jnp.moveaxis(g[:, 0], 0, 1).reshape(g.shape[2], -1)
    if name in ("a_norm", "a_conv_b", "a_gnorm"):
        return g[:, 0].reshape(1, -1)
    if name == "f_conv_w":
        return jnp.moveaxis(g, 0, 2).reshape(g.shape[1], g.shape[2], -1)
    raise ValueError(name)


def _whole_to_shards(name, w):
    if name == "a_conv_w":
        return jnp.moveaxis(w.reshape(w.shape[0], N_DEV, -1), 1, 0)[:, None]
    if name in ("a_norm", "a_conv_b", "a_gnorm"):
        return w.reshape(N_DEV, 1, -1)
    if name == "f_conv_w":
        return jnp.moveaxis(w.reshape(w.shape[0], w.shape[1], N_DEV, -1), 2, 0)
    raise ValueError(name)


def _small_weights(whole):
    W = {}
    cw, cb = whole["a_conv_w"], whole["a_conv_b"]
    o = 0
    for k, n in (("xs", INNER), ("B", GROUPS * NSTATE), ("C", GROUPS * NSTATE)):
        W[f"cw_{k}"], W[f"cb_{k}"] = cw[:, o:o + n], cb[:, o:o + n]
        o += n
    W["a_norm"], W["a_gnorm"] = whole["a_norm"], whole["a_gnorm"]
    W["dt_bias"], W["A_log"], W["D"] = (_pad_lanes(whole[k]) for k in ("a_dt_bias", "a_A_log", "a_D"))
    W["kv_norm"], W["b_kv"], W["k_norm"] = whole["kv_norm"].reshape(1, -1), whole["b_kv"].reshape(1, -1), whole["k_norm"].reshape(1, -1)
    for k in ("b_norm", "b_q", "q_norm", "sinks", "b_o"):
        W[k] = whole[k]
    for i in range(2):
        W[f"f_norm{i}"] = whole["f_norm"][i:i + 1]
        W[f"f_cw{i}"], W[f"f_cb{i}"] = whole["f_conv_w"][i], whole["f_conv_b"][i:i + 1]
    return W


def _small_grads(G, shapes):
    nh = HEADS
    out = {
        "a_conv_w": jnp.concatenate([G["cw_xs"], G["cw_B"], G["cw_C"]], axis=1),
        "a_conv_b": jnp.concatenate([G["cb_xs"], G["cb_B"], G["cb_C"]], axis=1),
        "a_norm": G["a_norm"], "a_gnorm": G["a_gnorm"],
        "a_dt_bias": G["dt_bias"][:, :nh], "a_A_log": G["A_log"][:, :nh], "a_D": G["D"][:, :nh],
        "kv_norm": G["kv_norm"], "b_kv": G["b_kv"], "k_norm": G["k_norm"], "b_norm": G["b_norm"],
        "b_q": G["b_q"], "q_norm": G["q_norm"], "sinks": G["sinks"], "b_o": G["b_o"],
        "f_norm": jnp.concatenate([G["f_norm0"], G["f_norm1"]], axis=0),
        "f_conv_w": jnp.stack([G["f_conv_w0"], G["f_conv_w1"]]),
        "f_conv_b": jnp.concatenate([G["f_conv_b0"], G["f_conv_b1"]], axis=0),
    }
    return {k: val.reshape(shapes[k]) if k in shapes else val for k, val in out.items()}


def kernel(x, positions, a_norm, a_in_proj, a_conv_w, a_conv_b, a_dt_bias, a_A_log, a_D, a_gnorm, a_out_proj, kv_norm, w_kv, b_kv, k_norm, b_norm, w_q, b_q, q_norm, sinks, w_o, b_o, f_norm, f_w_in, f_conv_w, f_conv_b, f_w_down, loss_target, m_a_norm, m_a_in_proj, m_a_conv_w, m_a_conv_b, m_a_dt_bias, m_a_A_log, m_a_D, m_a_gnorm, m_a_out_proj, m_kv_norm, m_w_kv, m_b_kv, m_k_norm, m_b_norm, m_w_q, m_b_q, m_q_norm, m_sinks, m_w_o, m_b_o, m_f_norm, m_f_w_in, m_f_conv_w, m_f_conv_b, m_f_w_down, v_a_norm, v_a_in_proj, v_a_conv_w, v_a_conv_b, v_a_dt_bias, v_a_A_log, v_a_D, v_a_gnorm, v_a_out_proj, v_kv_norm, v_w_kv, v_b_kv, v_k_norm, v_b_norm, v_w_q, v_b_q, v_q_norm, v_sinks, v_w_o, v_b_o, v_f_norm, v_f_w_in, v_f_conv_w, v_f_conv_b, v_f_w_down):
    given = dict(locals())
    w_in = {n: given[n] for n in ORDER}
    m_in = {n: given["m_" + n] for n in ORDER}
    v_in = {n: given["v_" + n] for n in ORDER}
    dev = 4 * lax.axis_index("x") + 2 * lax.axis_index("y") + lax.axis_index("c")

    w2, m2, v2 = _row_blocks(w_in), _row_blocks(m_in), _row_blocks(v_in)
    small_pack = _pack([w_in[n] for n in SMALL_SHARDED], 8, f32)
    shards = {k: w2[k].astype(MXU) for k in ROW_KEYS}
    in_all, small_all = run_comm("ag_head", gather_comm([shards["inT"], small_pack]))
    whole = {n: w_in[n] for n in REPLICATED}
    for n, g in zip(SMALL_SHARDED, _unpack(small_all, [w_in[n].shape for n in SMALL_SHARDED], lead=(N_DEV,))):
        whole[n] = _gathered_to_whole(n, g)
    W = _small_weights(whole)
    W["inT"] = in_all.reshape(-1, D)
    W["in_dtT"] = jnp.pad(W["inT"][IN_COLS - HEADS:], ((0, LANES - HEADS), (0, 0)))

    loss, dx, G, scatters = _local_step(x[0], positions.reshape(S, 1).astype(f32), loss_target[0], W, shards)
    grads = _small_grads(G, {n: whole[n].shape for n in REPLICATED})

    small_names = SMALL_SHARDED + REPLICATED
    small_part = _pack([grads[n] for n in small_names], 8, f32)
    small_gather = gather_comm([small_part])
    run_comm("rs_tail", merge_comms([scatters[-1].chips(), small_gather]))
    sums = {}
    for rs in scatters:
        sums.update(rs.sums())
    g_out = _from_row_blocks(sums)
    small_sum = sum_devices(small_gather.results[0])
    for n, g in zip(small_names, _unpack(small_sum, [grads[n].shape for n in small_names])):
        if n in SMALL_SHARDED:
            g_out[n] = lax.dynamic_index_in_dim(_whole_to_shards(n, g), dev, axis=0, keepdims=False)
        else:
            g_out[n] = g.reshape(w_in[n].shape)

    stepped = {k: adamw(f"adamw_{k}", w2[k], sums[k], m2[k], v2[k]) for k in ROW_KEYS}
    delta, new_m, new_v = (_from_row_blocks({k: stepped[k][i] for k in ROW_KEYS}) for i in range(3))
    packs = [_pack([src[n] for n in small_names], 8, f32) for src in (w_in, g_out, m_in, v_in)]
    outs = adamw("adamw_small", *packs)
    for dst, buf in zip((delta, new_m, new_v), outs):
        for n, a in zip(small_names, _unpack(buf, [w_in[n].shape for n in small_names])):
            dst[n] = a

    loss_all = lax.psum(loss[0, 0], AXES)
    return (loss_all, dx[None], *[g_out[n] for n in ORDER], *[delta[n] for n in ORDER],
            *[new_m[n] for n in ORDER], *[new_v[n] for n in ORDER])
```

```python
import functools
import math

import jax
import jax.numpy as jnp
from jax import lax
from jax.experimental import pallas as pl
from jax.experimental.pallas import tpu as pltpu

f32 = jnp.float32
bf16 = jnp.bfloat16
MXU = bf16

N_DEV = 8
S = 2048
D = 1024
EPS = 1e-5
INNER = 2048
HEADS = 32
HP = 64
GROUPS = 8
HPG = HEADS // GROUPS
NSTATE = 128
CONV_A = 4
CHUNK = 256
NCHUNK = S // CHUNK
AH = 64
QH = 16
KVH = 4
QPK = QH // KVH
WIN = 128
NBLK = S // WIN
ROPE_THETA = 10000.0
FFN = 2816
CONV_F = 3
LANES = 128
PACK_W = 1024
BIG_TILE = 496
VMEM_LIMIT = 56 * 1024 * 1024

ADAM_LR, ADAM_B1, ADAM_B2, ADAM_EPS, ADAM_WD, ADAM_STEP = 0.001, 0.9, 0.999, 1e-08, 0.01, 10

MESH = pl.DeviceIdType.MESH
AXES = ("x", "y", "c")


def _cparams(sem=None):
    return pltpu.CompilerParams(dimension_semantics=sem, vmem_limit_bytes=VMEM_LIMIT)


def _pick(n, cands):
    for c in cands:
        if n % c == 0:
            return c
    return n


class Comm:
    def __init__(self, ins, out_shapes, sems, start, finish):
        self.ins, self.out_shapes, self.sems, self.start, self.finish = list(ins), list(out_shapes), list(sems), start, finish
        self.results, self.children = None, ()

    def set_results(self, res):
        self.results, o = list(res), 0
        for ch in self.children:
            ch.set_results(res[o:o + len(ch.out_shapes)])
            o += len(ch.out_shapes)


def merge_comms(comms):
    def each(fn_name, ins, outs, sems):
        i = o = s = 0
        for c in comms:
            getattr(c, fn_name)(ins[i:i + len(c.ins)], outs[o:o + len(c.out_shapes)], sems[s:s + len(c.sems)])
            i, o, s = i + len(c.ins), o + len(c.out_shapes), s + len(c.sems)

    merged = Comm([a for c in comms for a in c.ins], [a for c in comms for a in c.out_shapes], [a for c in comms for a in c.sems],
                  functools.partial(each, "start"), functools.partial(each, "finish"))
    merged.children = tuple(comms)
    return merged


def _call(body, args, *, name, grid, in_specs, out_specs, out_shape, scratch=(), sem=None, comm=None):
    if comm is None:
        return pl.pallas_call(body, name=name, grid=grid, in_specs=list(in_specs), out_specs=list(out_specs),
                              out_shape=list(out_shape), scratch_shapes=list(scratch), compiler_params=_cparams(sem))(*args)
    n_in, n_out, n_scr, c_in, c_out = len(in_specs), len(out_shape), len(scratch), len(comm.ins), len(comm.out_shapes)
    any_spec = pl.BlockSpec(memory_space=pl.ANY)

    def outer(*refs):
        ins, c_ins = refs[:n_in], refs[n_in:n_in + c_in]
        o = n_in + c_in
        outs, c_outs = refs[o:o + n_out], refs[o + n_out:o + n_out + c_out]
        o += n_out + c_out
        scr, c_sems = refs[o:o + n_scr], refs[o + n_scr:]
        ids = [pl.program_id(i) for i in range(len(grid))]
        first = functools.reduce(jnp.logical_and, [i == 0 for i in ids])
        last = functools.reduce(jnp.logical_and, [i == g - 1 for i, g in zip(ids, grid)])

        @pl.when(first)
        def _():
            comm.start(c_ins, c_outs, c_sems)

        body(*ins, *outs, *scr)

        @pl.when(last)
        def _():
            comm.finish(c_ins, c_outs, c_sems)

    res = pl.pallas_call(
        outer, name=name, grid=grid, in_specs=list(in_specs) + [any_spec] * c_in,
        out_specs=list(out_specs) + [any_spec] * c_out, out_shape=list(out_shape) + comm.out_shapes,
        scratch_shapes=list(scratch) + comm.sems, compiler_params=_cparams(("arbitrary",) * len(grid)),
    )(*args, *comm.ins)
    comm.set_results(res[n_out:])
    return res[:n_out]


def matmul(name, a, b, mode, out_dtype=f32, bias=None, residual=None, b_rows=None, a_lead=None):
    b_idx, b_shape = (0, b.shape) if b_rows is None else (b_rows[0], (b_rows[1], b.shape[1]))
    a_shape = a.shape if a_lead is None else a.shape[1:]
    if mode == "nn":
        (M, K), (K2, N) = a_shape, b_shape
    elif mode == "nt":
        (M, K), (N, K2) = a_shape, b_shape
    else:
        (K, M), (K2, N) = a_shape, b_shape
    assert K == K2, (name, a_shape, b_shape)
    if mode == "tn":
        tm, tn = M, _pick(N, (512, 256, 128) if M <= 1024 else (256, 128))
        a_spec = pl.BlockSpec((K, M), lambda j: (0, 0))
        b_spec = pl.BlockSpec((K, tn), lambda j: (0, j))
        dims = (((0,), (0,)), ((), ()))
        grid, o_map, row_map = (N // tn,), (lambda j: (0, j)), (lambda j: (0, j))
    else:
        tm, tn = (256 if N >= 2048 else 512), N
        a_spec = pl.BlockSpec((tm, K), lambda i: (i, 0)) if a_lead is None else pl.BlockSpec((None, tm, K), lambda i: (a_lead, i, 0))
        b_spec = pl.BlockSpec(b_shape, lambda i: (b_idx, 0))
        dims = (((1,), (0,)), ((), ())) if mode == "nn" else (((1,), (1,)), ((), ()))
        grid, o_map, row_map = (M // tm,), (lambda i: (i, 0)), (lambda i: (0, 0))
    ins, in_specs = [a, b], [a_spec, b_spec]
    if bias is not None:
        ins.append(bias)
        in_specs.append(pl.BlockSpec((1, tn), row_map))
    if residual is not None:
        ins.append(residual)
        in_specs.append(pl.BlockSpec((tm, tn), o_map))
    has_bias, has_res = bias is not None, residual is not None

    def body(a_ref, b_ref, *rest):
        rest = list(rest)
        bias_ref = rest.pop(0) if has_bias else None
        res_ref = rest.pop(0) if has_res else None
        (o_ref,) = rest
        r = lax.dot_general(a_ref[...].astype(MXU), b_ref[...].astype(MXU), dims, preferred_element_type=f32)
        if has_bias:
            r = r + bias_ref[...]
        if has_res:
            r = r + res_ref[...]
        o_ref[...] = r.astype(out_dtype)

    return pl.pallas_call(
        body, name=name, grid=grid, in_specs=in_specs,
        out_specs=pl.BlockSpec((tm, tn), o_map),
        out_shape=jax.ShapeDtypeStruct((M, N), out_dtype),
        compiler_params=_cparams(("parallel",)),
    )(*ins)


def matmul_tn_stacked(name, a, b, out_dtype):
    R, K, M = a.shape
    N = b.shape[1]
    tn = _pick(N, (256, 128))

    def body(a_ref, b_ref, o_ref):
        o_ref[0] = lax.dot_general(a_ref[0].astype(MXU), b_ref[...].astype(MXU), (((0,), (0,)), ((), ())),
                                   preferred_element_type=f32).astype(out_dtype)

    out = pl.pallas_call(
        body, name=name, grid=(R, N // tn),
        in_specs=[pl.BlockSpec((1, K, M), lambda r, j: (r, 0, 0)), pl.BlockSpec((K, tn), lambda r, j: (0, j))],
        out_specs=pl.BlockSpec((1, M, tn), lambda r, j: (r, 0, j)),
        out_shape=jax.ShapeDtypeStruct((R, M, N), out_dtype),
        compiler_params=_cparams(("parallel", "parallel")),
    )(a, b)
    return out.reshape(R * M, N)


def rowwise(name, fn, rows, pars, outs, accs=(), tile=256, comm=None):
    n_in, n_out = len(rows) + len(pars), len(outs)
    in_specs = [pl.BlockSpec((None, tile, r[0].shape[2]), functools.partial(lambda i, lead: (lead, i, 0), lead=r[1]))
                if isinstance(r, tuple) else pl.BlockSpec((tile, r.shape[1]), lambda i: (i, 0)) for r in rows]
    rows = [r[0] if isinstance(r, tuple) else r for r in rows]
    in_specs += [pl.BlockSpec(p.shape, lambda i: (0, 0)) for p in pars]
    out_specs = [pl.BlockSpec((tile, c), lambda i: (i, 0)) for c, _ in outs]
    out_specs += [pl.BlockSpec(shp, lambda i: (0, 0)) for shp in accs]
    out_shape = [jax.ShapeDtypeStruct((S, c), dt) for c, dt in outs]
    out_shape += [jax.ShapeDtypeStruct(shp, f32) for shp in accs]

    def body(*refs):
        res = fn(*[r[...] for r in refs[:n_in]])
        o_refs = refs[n_in:n_in + n_out]
        a_refs = refs[n_in + n_out:]
        for ref, val in zip(o_refs, res[:n_out]):
            ref[...] = val.astype(ref.dtype)
        if a_refs:
            @pl.when(pl.program_id(0) == 0)
            def _():
                for ref in a_refs:
                    ref[...] = jnp.zeros_like(ref)
            for ref, val in zip(a_refs, res[n_out:]):
                ref[...] += val

    return _call(body, [*rows, *pars], name=name, grid=(S // tile,), in_specs=in_specs, out_specs=out_specs,
                 out_shape=out_shape, sem=("arbitrary",) if accs else ("parallel",), comm=comm)


def colwise(name, fn, cols, pars, outs, pouts=(), ct=128, comm=None):
    C = cols[0].shape[1]
    n_in, n_out = len(cols) + len(pars), len(outs)
    in_specs = [pl.BlockSpec((S, ct), lambda j: (0, j)) for _ in cols]
    in_specs += [pl.BlockSpec((p.shape[0], ct), lambda j: (0, j)) for p in pars]
    out_specs = [pl.BlockSpec((S, ct), lambda j: (0, j)) for _ in outs]
    out_specs += [pl.BlockSpec((r, ct), lambda j: (0, j)) for r in pouts]
    out_shape = [jax.ShapeDtypeStruct((S, C), dt) for dt in outs]
    out_shape += [jax.ShapeDtypeStruct((r, C), f32) for r in pouts]

    def body(*refs):
        res = fn(*[r[...] for r in refs[:n_in]])
        for ref, val in zip(refs[n_in:], res):
            ref[...] = val.astype(ref.dtype)

    return _call(body, [*cols, *pars], name=name, grid=(C // ct,), in_specs=in_specs, out_specs=out_specs,
                 out_shape=out_shape, sem=("parallel",), comm=comm)


def _sigmoid(x):
    return 0.5 * jnp.tanh(0.5 * x) + 0.5


def _silu(x):
    return x * _sigmoid(x)


def _dsilu(x):
    sg = _sigmoid(x)
    return sg * (1.0 + x * (1.0 - sg))


def _softplus(x):
    return jnp.maximum(x, 0.0) + jnp.log(1.0 + jnp.exp(-jnp.abs(x)))


def _rms_fwd(x, g):
    r = lax.rsqrt(jnp.mean(x * x, axis=-1, keepdims=True) + EPS)
    return x * r * g


def _rms_bwd(x, g, dh):
    r = lax.rsqrt(jnp.mean(x * x, axis=-1, keepdims=True) + EPS)
    xh = x * r
    dxh = dh * g
    dx = r * (dxh - xh * jnp.mean(dxh * xh, axis=-1, keepdims=True))
    return dx, jnp.sum(dh * xh, axis=0, keepdims=True)


def _taps(x, width):
    row = lax.broadcasted_iota(jnp.int32, x.shape, 0)
    return [jnp.where(row >= s, pltpu.roll(x, s, 0), 0.0) for s in range(width - 1, 0, -1)] + [x]


def _conv(x, w, b, taps=None):
    width = w.shape[0]
    taps = _taps(x, width) if taps is None else taps
    out = b + w[0:1, :] * taps[0]
    for k in range(1, width):
        out = out + w[k:k + 1, :] * taps[k]
    return out


def _conv_bwd(x, w, dc, taps=None):
    width, n = w.shape[0], x.shape[0]
    taps = _taps(x, width) if taps is None else taps
    row = lax.broadcasted_iota(jnp.int32, x.shape, 0)
    dx = w[width - 1:width, :] * dc
    for k in range(width - 1):
        s = width - 1 - k
        dx = dx + w[k:k + 1, :] * jnp.where(row < n - s, pltpu.roll(dc, n - s, 0), 0.0)
    dw = jnp.concatenate([jnp.sum(dc * t, axis=0, keepdims=True) for t in taps], axis=0)
    return dx, dw, jnp.sum(dc, axis=0, keepdims=True)


def _rope_tables(pos, inv_freq):
    ang = pos * inv_freq
    return jnp.cos(ang), jnp.sin(ang)


def _headnorm_rope_fwd(x, g, cos, sin, heads):
    half = AH // 2
    outs = []
    for h in range(heads):
        seg = x[:, AH * h:AH * (h + 1)]
        n = _rms_fwd(seg, g)
        n1, n2 = n[:, :half], n[:, half:]
        outs += [n1 * cos - n2 * sin, n2 * cos + n1 * sin]
    return jnp.concatenate(outs, axis=1)


def _headnorm_rope_bwd(x, g, cos, sin, dout, heads):
    half = AH // 2
    dxs, dg = [], jnp.zeros((1, AH), f32)
    for h in range(heads):
        seg = x[:, AH * h:AH * (h + 1)]
        d = dout[:, AH * h:AH * (h + 1)]
        d1, d2 = d[:, :half], d[:, half:]
        dn = jnp.concatenate([d1 * cos + d2 * sin, d2 * cos - d1 * sin], axis=1)
        dx, dgh = _rms_bwd(seg, g, dn)
        dxs.append(dx)
        dg = dg + dgh
    return jnp.concatenate(dxs, axis=1), dg


def _ssd_prep(dt_pre, dt_bias, a_log, dt_s, acum_s, acumT_s):
    dt = _softplus(dt_pre + dt_bias)
    a = dt * (-jnp.exp(a_log))
    row = lax.broadcasted_iota(jnp.int32, (CHUNK, CHUNK), 0)
    col = lax.broadcasted_iota(jnp.int32, (CHUNK, CHUNK), 1)
    dt_s[...] = dt
    acum_s[...] = jnp.dot((col <= row).astype(f32), a, precision=lax.Precision.HIGHEST, preferred_element_type=f32)
    acumT_s[...] = lax.dot_general(a, (row <= col).astype(f32), (((0,), (0,)), ((), ())),
                                   precision=lax.Precision.HIGHEST, preferred_element_type=f32)


def _head_cols(h, dt_s, acum_s, acumT_s):
    lane = lax.broadcasted_iota(jnp.int32, (1, LANES), 1)
    oh_l = (lane == h).astype(f32)
    sub = lax.broadcasted_iota(jnp.int32, (LANES, 1), 0)
    oh_s = (sub == h).astype(f32)
    dt_h = jnp.sum(dt_s[...] * oh_l, axis=1, keepdims=True)
    ac_h = jnp.sum(acum_s[...] * oh_l, axis=1, keepdims=True)
    acr_h = jnp.sum(acumT_s[...] * oh_s, axis=0, keepdims=True)
    return oh_l, dt_h, ac_h, acr_h


def ssd_fwd(xs, Bm, Cm, dt_pre, dt_bias, a_log, d_skip, comm=None):
    def body(xs_ref, b_ref, c_ref, dtp_ref, bias_ref, alog_ref, d_ref, y_ref, st_ref, state, dt_s, acum_s, acumT_s):
        c, g = pl.program_id(0), pl.program_id(1)

        @pl.when(g == 0)
        def _():
            _ssd_prep(dtp_ref[...], bias_ref[...], alog_ref[...], dt_s, acum_s, acumT_s)

        row = lax.broadcasted_iota(jnp.int32, (CHUNK, CHUNK), 0)
        col = lax.broadcasted_iota(jnp.int32, (CHUNK, CHUNK), 1)
        causal = col <= row
        Bb, Cb = b_ref[...], c_ref[...]
        cb = lax.dot_general(Cb.astype(MXU), Bb.astype(MXU), (((1,), (1,)), ((), ())), preferred_element_type=f32)
        xs_blk = xs_ref[...]

        @pl.when(c == 0)
        def _():
            for j in range(HPG):
                state[g * HPG + j] = jnp.zeros((NSTATE, HP), f32)

        prevs = [state[g * HPG + j] for j in range(HPG)]
        y_off_all = jnp.dot(Cb.astype(MXU), jnp.concatenate(prevs, axis=1).astype(MXU), preferred_element_type=f32)
        ys, xds, e_ends = [], [], []
        for j in range(HPG):
            oh_l, dt_h, ac_h, acr_h = _head_cols(g * HPG + j, dt_s, acum_s, acumT_s)
            decay = jnp.exp(jnp.where(causal, ac_h - acr_h, -1e30))
            w = (cb * decay).astype(MXU)
            xs_h = xs_blk[:, HP * j:HP * (j + 1)]
            xd = xs_h * dt_h
            y_diag = jnp.dot(w, xd.astype(MXU), preferred_element_type=f32)
            y_off = y_off_all[:, HP * j:HP * (j + 1)] * jnp.exp(ac_h)
            d_h = jnp.sum(d_ref[...] * oh_l, axis=1, keepdims=True)
            ys.append(y_diag + y_off + xs_h * d_h)
            a_end = ac_h[CHUNK - 1:CHUNK, :]
            xds.append(xd * jnp.exp(a_end - ac_h))
            e_ends.append(jnp.exp(a_end))
        s_c = lax.dot_general(Bb.astype(MXU), jnp.concatenate(xds, axis=1).astype(MXU), (((0,), (0,)), ((), ())),
                              preferred_element_type=f32)
        for j in range(HPG):
            st_ref[0, j] = prevs[j]
            state[g * HPG + j] = prevs[j] * e_ends[j] + s_c[:, HP * j:HP * (j + 1)]
        y_ref[...] = jnp.concatenate(ys, axis=1)

    par = pl.BlockSpec((1, LANES), lambda c, g: (0, 0))
    return _call(
        body, [xs, Bm, Cm, dt_pre, dt_bias, a_log, d_skip], comm=comm, name="ssd_fwd", grid=(NCHUNK, GROUPS),
        in_specs=[pl.BlockSpec((CHUNK, HPG * HP), lambda c, g: (c, g)),
                  pl.BlockSpec((CHUNK, NSTATE), lambda c, g: (c, g)),
                  pl.BlockSpec((CHUNK, NSTATE), lambda c, g: (c, g)),
                  pl.BlockSpec((CHUNK, LANES), lambda c, g: (c, 0)), par, par, par],
        out_specs=[pl.BlockSpec((CHUNK, HPG * HP), lambda c, g: (c, g)),
                   pl.BlockSpec((1, HPG, NSTATE, HP), lambda c, g: (c, g, 0, 0))],
        out_shape=[jax.ShapeDtypeStruct((S, INNER), f32), jax.ShapeDtypeStruct((NCHUNK, HEADS, NSTATE, HP), f32)],
        scratch=[pltpu.VMEM((HEADS, NSTATE, HP), f32), pltpu.VMEM((CHUNK, LANES), f32),
                 pltpu.VMEM((CHUNK, LANES), f32), pltpu.VMEM((LANES, CHUNK), f32)],
        sem=("arbitrary", "arbitrary"))


def ssd_bwd(xs, Bm, Cm, dt_pre, dt_bias, a_log, d_skip, states, dy, comm=None):
    rev = lambda c: NCHUNK - 1 - c

    def body(xs_ref, b_ref, c_ref, dtp_ref, bias_ref, alog_ref, d_ref, st_ref, dy_ref,
             dxs_ref, db_ref, dc_ref, ddt_ref, dbias_ref, dalog_ref, dd_ref,
             dstate, dt_s, acum_s, acumT_s, dacum_s, ddt_s, da_s):
        c, g = pl.program_id(0), pl.program_id(1)

        @pl.when(g == 0)
        def _():
            _ssd_prep(dtp_ref[...], bias_ref[...], alog_ref[...], dt_s, acum_s, acumT_s)
            dacum_s[...] = jnp.zeros_like(dacum_s)
            ddt_s[...] = jnp.zeros_like(ddt_s)

        @pl.when((c == 0) & (g == 0))
        def _():
            da_s[...] = jnp.zeros_like(da_s)
            dd_ref[...] = jnp.zeros_like(dd_ref)
            dbias_ref[...] = jnp.zeros_like(dbias_ref)
            dalog_ref[...] = jnp.zeros_like(dalog_ref)

        row = lax.broadcasted_iota(jnp.int32, (CHUNK, CHUNK), 0)
        col = lax.broadcasted_iota(jnp.int32, (CHUNK, CHUNK), 1)
        sub_l = lax.broadcasted_iota(jnp.int32, (CHUNK, 1), 0)
        last = (sub_l == CHUNK - 1).astype(f32)
        nt = (((1,), (1,)), ((), ()))
        tn = (((0,), (0,)), ((), ()))
        Bb, Cb = b_ref[...], c_ref[...]
        Bm_, Cm_ = Bb.astype(MXU), Cb.astype(MXU)
        cb = lax.dot_general(Cm_, Bm_, nt, preferred_element_type=f32)
        bc = lax.dot_general(Bm_, Cm_, nt, preferred_element_type=f32)
        xs_blk, dy_blk = xs_ref[...], dy_ref[...]
        dxs, dB, dC = [], jnp.zeros((CHUNK, NSTATE), f32), jnp.zeros((CHUNK, NSTATE), f32)
        for j in range(HPG):
            h = g * HPG + j
            oh_l, dt_h, ac_h, acr_h = _head_cols(h, dt_s, acum_s, acumT_s)

            @pl.when(c == 0)
            def _():
                dstate[h] = jnp.zeros((NSTATE, HP), f32)

            dnext = dstate[h]
            prev = st_ref[0, j]
            lm = jnp.exp(jnp.where(col <= row, ac_h - acr_h, -1e30))
            lmT = jnp.exp(jnp.where(row <= col, acr_h - ac_h, -1e30))
            xs_h = xs_blk[:, HP * j:HP * (j + 1)]
            dy_h = dy_blk[:, HP * j:HP * (j + 1)]
            xd = xs_h * dt_h
            xdm, dym = xd.astype(MXU), dy_h.astype(MXU)
            ea = jnp.exp(ac_h)
            a_end = ac_h[CHUNK - 1:CHUNK, :]
            e_end = jnp.exp(a_end)
            dte = jnp.exp(a_end - ac_h)
            dnm, pvm = dnext.astype(MXU), prev.astype(MXU)
            bd = jnp.dot(Bm_, dnm, preferred_element_type=f32)
            dxd = jnp.dot((bc * lmT).astype(MXU), dym, preferred_element_type=f32) + dte * bd
            dw = lax.dot_general(dym, xdm, nt, preferred_element_type=f32)
            dwT = lax.dot_general(xdm, dym, nt, preferred_element_type=f32)
            dcb = dw * lm
            dbc = dwT * lmT
            eady = (ea * dy_h).astype(MXU)
            dC = dC + jnp.dot(dcb.astype(MXU), Bm_, preferred_element_type=f32) \
                + lax.dot_general(eady, pvm, nt, preferred_element_type=f32)
            dB = dB + jnp.dot(dbc.astype(MXU), Cm_, preferred_element_type=f32) \
                + dte * lax.dot_general(xdm, dnm, nt, preferred_element_type=f32)
            dstate[h] = lax.dot_general(Cm_, eady, tn, preferred_element_type=f32) + e_end * dnext
            r1 = jnp.sum(dcb * cb, axis=1, keepdims=True)
            r2 = jnp.sum(dbc * bc, axis=1, keepdims=True)
            y_off = jnp.dot(Cm_, pvm, preferred_element_type=f32) * ea
            t3 = jnp.sum(dy_h * y_off, axis=1, keepdims=True)
            t4 = jnp.sum(bd * xd, axis=1, keepdims=True) * dte
            end_extra = jnp.sum(t4, axis=0, keepdims=True) + e_end * jnp.sum(jnp.sum(prev * dnext, axis=1, keepdims=True), axis=0, keepdims=True)
            dacum_h = r1 - r2 + t3 - t4 + last * end_extra
            dacum_s[...] += dacum_h * oh_l
            ddt_s[...] += jnp.sum(dxd * xs_h, axis=1, keepdims=True) * oh_l
            d_h = jnp.sum(d_ref[...] * oh_l, axis=1, keepdims=True)
            dxs.append(dxd * dt_h + dy_h * d_h)
            dd_ref[...] += oh_l * jnp.sum(jnp.sum(dy_h * xs_h, axis=1, keepdims=True), axis=0, keepdims=True)
        dxs_ref[...] = jnp.concatenate(dxs, axis=1)
        db_ref[...] = dB
        dc_ref[...] = dC

        @pl.when(g == GROUPS - 1)
        def _():
            a_row = -jnp.exp(alog_ref[...])
            da = jnp.dot((row <= col).astype(f32), dacum_s[...], precision=lax.Precision.HIGHEST, preferred_element_type=f32)
            da_s[...] += jnp.sum(da * dt_s[...], axis=0, keepdims=True)
            z = dtp_ref[...] + bias_ref[...]
            ddt_pre = (ddt_s[...] + da * a_row) * _sigmoid(z)
            ddt_ref[...] = ddt_pre.astype(ddt_ref.dtype)
            dbias_ref[...] += jnp.sum(ddt_pre, axis=0, keepdims=True)

            @pl.when(c == NCHUNK - 1)
            def _():
                dalog_ref[...] = da_s[...] * a_row

    par = pl.BlockSpec((1, LANES), lambda c, g: (0, 0))
    return _call(
        body, [xs, Bm, Cm, dt_pre, dt_bias, a_log, d_skip, states, dy], comm=comm, name="ssd_bwd", grid=(NCHUNK, GROUPS),
        in_specs=[pl.BlockSpec((CHUNK, HPG * HP), lambda c, g: (rev(c), g)),
                  pl.BlockSpec((CHUNK, NSTATE), lambda c, g: (rev(c), g)),
                  pl.BlockSpec((CHUNK, NSTATE), lambda c, g: (rev(c), g)),
                  pl.BlockSpec((CHUNK, LANES), lambda c, g: (rev(c), 0)), par, par, par,
                  pl.BlockSpec((1, HPG, NSTATE, HP), lambda c, g: (rev(c), g, 0, 0)),
                  pl.BlockSpec((CHUNK, HPG * HP), lambda c, g: (rev(c), g))],
        out_specs=[pl.BlockSpec((CHUNK, HPG * HP), lambda c, g: (rev(c), g)),
                   pl.BlockSpec((CHUNK, NSTATE), lambda c, g: (rev(c), g)),
                   pl.BlockSpec((CHUNK, NSTATE), lambda c, g: (rev(c), g)),
                   pl.BlockSpec((CHUNK, LANES), lambda c, g: (rev(c), 0)), par, par, par],
        out_shape=[jax.ShapeDtypeStruct((S, INNER), f32), jax.ShapeDtypeStruct((S, GROUPS * NSTATE), f32),
                   jax.ShapeDtypeStruct((S, GROUPS * NSTATE), f32), jax.ShapeDtypeStruct((S, LANES), MXU),
                   jax.ShapeDtypeStruct((1, LANES), f32), jax.ShapeDtypeStruct((1, LANES), f32),
                   jax.ShapeDtypeStruct((1, LANES), f32)],
        scratch=[pltpu.VMEM((HEADS, NSTATE, HP), f32), pltpu.VMEM((CHUNK, LANES), f32),
                 pltpu.VMEM((CHUNK, LANES), f32), pltpu.VMEM((LANES, CHUNK), f32),
                 pltpu.VMEM((CHUNK, LANES), f32), pltpu.VMEM((CHUNK, LANES), f32), pltpu.VMEM((1, LANES), f32)],
        sem=("arbitrary", "arbitrary"))


ATT_STACK_FWD, ATT_STACK_BWD = 4, 2


def _attn_kv(kp, kc, vp, vc, hk):
    sl = slice(AH * hk, AH * (hk + 1))
    return (jnp.concatenate([kp[:, sl], kc[:, sl]], axis=0).astype(MXU),
            jnp.concatenate([vp[:, sl], vc[:, sl]], axis=0).astype(MXU))


def _stack_heads(x, heads):
    return jnp.concatenate([x[:, AH * h:AH * (h + 1)] for h in heads], axis=0)


def _attn_block(n, q, kb, sinks, heads):
    rows = len(heads) * WIN
    qi = lax.broadcasted_iota(jnp.int32, (rows, 2 * WIN), 0) & (WIN - 1)
    ki = lax.broadcasted_iota(jnp.int32, (rows, 2 * WIN), 1)
    rel = qi + WIN - ki
    mask = (rel >= 0) & (rel < WIN) & ((ki >= WIN) | (n > 0))
    qg = _stack_heads(q, heads).astype(MXU)
    s = lax.dot_general(qg, kb, (((1,), (1,)), ((), ())), preferred_element_type=f32) * (AH ** -0.5)
    s = jnp.where(mask, s, -1e30)
    sink = jnp.concatenate([jnp.broadcast_to(sinks[:, h:h + 1], (WIN, 1)) for h in heads], axis=0)
    m = jnp.maximum(jnp.max(s, axis=1, keepdims=True), sink)
    p = jnp.exp(s - m)
    ps = jnp.exp(sink - m)
    inv = 1.0 / (jnp.sum(p, axis=1, keepdims=True) + ps)
    return qg, p * inv, ps * inv


def _head_blocks(hk, stack):
    return [list(range(QPK * hk + i, QPK * hk + i + stack)) for i in range(0, QPK, stack)]


def _kv_specs():
    prev = lambda n: (jnp.maximum(n - 1, 0), 0)
    cur = lambda n: (n, 0)
    w = KVH * AH
    return [pl.BlockSpec((WIN, w), prev), pl.BlockSpec((WIN, w), cur), pl.BlockSpec((WIN, w), prev), pl.BlockSpec((WIN, w), cur)]


def attn_fwd(q, k, v, sinks, comm=None):
    def body(q_ref, kp_ref, kc_ref, vp_ref, vc_ref, s_ref, o_ref):
        n = pl.program_id(0)
        q_, kp, kc, vp, vc, sk = q_ref[...], kp_ref[...], kc_ref[...], vp_ref[...], vc_ref[...], s_ref[...]
        outs = []
        for hk in range(KVH):
            kb, vb = _attn_kv(kp, kc, vp, vc, hk)
            for heads in _head_blocks(hk, ATT_STACK_FWD):
                _, pr, _ = _attn_block(n, q_, kb, sk, heads)
                o = jnp.dot(pr.astype(MXU), vb, preferred_element_type=f32)
                outs += [o[WIN * i:WIN * (i + 1)] for i in range(len(heads))]
        o_ref[...] = jnp.concatenate(outs, axis=1)

    return _call(
        body, [q, k, k, v, v, sinks], comm=comm, name="attn_fwd", grid=(NBLK,),
        in_specs=[pl.BlockSpec((WIN, D), lambda n: (n, 0))] + _kv_specs() + [pl.BlockSpec((1, QH), lambda n: (0, 0))],
        out_specs=[pl.BlockSpec((WIN, D), lambda n: (n, 0))],
        out_shape=[jax.ShapeDtypeStruct((S, D), f32)], sem=("parallel",))[0]


def attn_bwd(q, k, v, sinks, dout, comm=None):
    def body(q_ref, kp_ref, kc_ref, vp_ref, vc_ref, s_ref, do_ref, dq_ref, dkp_ref, dkc_ref, dvp_ref, dvc_ref, ds_ref):
        n = pl.program_id(0)

        @pl.when(n == 0)
        def _():
            ds_ref[...] = jnp.zeros_like(ds_ref)

        q_, kp, kc, vp, vc, sk, do = q_ref[...], kp_ref[...], kc_ref[...], vp_ref[...], vc_ref[...], s_ref[...], do_ref[...]
        lane = lax.broadcasted_iota(jnp.int32, (1, QH), 1)
        nt = (((1,), (1,)), ((), ()))
        tn = (((0,), (0,)), ((), ()))
        dqs, dkps, dkcs, dvps, dvcs = [], [], [], [], []
        dsink = jnp.zeros((1, QH), f32)
        for hk in range(KVH):
            kb, vb = _attn_kv(kp, kc, vp, vc, hk)
            dkb, dvb = jnp.zeros((2 * WIN, AH), f32), jnp.zeros((2 * WIN, AH), f32)
            for heads in _head_blocks(hk, ATT_STACK_BWD):
                qg, pr, prs = _attn_block(n, q_, kb, sk, heads)
                dog = _stack_heads(do, heads).astype(MXU)
                dp = lax.dot_general(dog, vb, nt, preferred_element_type=f32)
                dvb = dvb + lax.dot_general(pr.astype(MXU), dog, tn, preferred_element_type=f32)
                delta = jnp.sum(pr * dp, axis=1, keepdims=True)
                ds = (pr * (dp - delta)).astype(MXU)
                dsk = -prs * delta
                for i, h in enumerate(heads):
                    dsink = dsink + jnp.sum(dsk[WIN * i:WIN * (i + 1)], axis=0, keepdims=True) * (lane == h).astype(f32)
                dqg = jnp.dot(ds, kb, preferred_element_type=f32) * (AH ** -0.5)
                dkb = dkb + lax.dot_general(ds, qg, tn, preferred_element_type=f32) * (AH ** -0.5)
                dqs += [dqg[WIN * i:WIN * (i + 1)] for i in range(len(heads))]
            dkps.append(dkb[:WIN])
            dkcs.append(dkb[WIN:])
            dvps.append(dvb[:WIN])
            dvcs.append(dvb[WIN:])
        dq_ref[...] = jnp.concatenate(dqs, axis=1)
        dkp_ref[...] = jnp.concatenate(dkps, axis=1)
        dkc_ref[...] = jnp.concatenate(dkcs, axis=1)
        dvp_ref[...] = jnp.concatenate(dvps, axis=1)
        dvc_ref[...] = jnp.concatenate(dvcs, axis=1)
        ds_ref[...] += dsink

    w = KVH * AH
    blk = lambda width: pl.BlockSpec((WIN, width), lambda n: (n, 0))
    return _call(
        body, [q, k, k, v, v, sinks, dout], comm=comm, name="attn_bwd", grid=(NBLK,),
        in_specs=[blk(D)] + _kv_specs() + [pl.BlockSpec((1, QH), lambda n: (0, 0)), blk(D)],
        out_specs=[blk(D), blk(w), blk(w), blk(w), blk(w), pl.BlockSpec((1, QH), lambda n: (0, 0))],
        out_shape=[jax.ShapeDtypeStruct((S, D), f32)] + [jax.ShapeDtypeStruct((S, w), f32)] * 4 + [jax.ShapeDtypeStruct((1, QH), f32)],
        sem=("arbitrary",))


def kv_bwd(kv, pos, inv_freq, k_norm, dkp, dkc, dvp, dvc):
    w = KVH * AH

    def body(kv_ref, pos_ref, if_ref, g_ref, dkp_ref, dkc_ref, dvp_ref, dvc_ref, o_ref, dg_ref, db_ref):
        n = pl.program_id(0)

        @pl.when(n == 0)
        def _():
            dg_ref[...] = jnp.zeros_like(dg_ref)
            db_ref[...] = jnp.zeros_like(db_ref)

        inside = (n < NBLK - 1).astype(f32)
        dk = dkc_ref[...] + inside * dkp_ref[...]
        dv = dvc_ref[...] + inside * dvp_ref[...]
        cos, sin = _rope_tables(pos_ref[...], if_ref[...])
        dkpre, dg = _headnorm_rope_bwd(kv_ref[...], g_ref[...], cos, sin, dk, KVH)
        dkv = jnp.concatenate([dkpre, dv], axis=1)
        o_ref[...] = dkv.astype(o_ref.dtype)
        dg_ref[...] += dg
        db_ref[...] += jnp.sum(dkv, axis=0, keepdims=True)

    nxt = lambda n: (jnp.minimum(n + 1, NBLK - 1), 0)
    cur = lambda n: (n, 0)
    const = lambda n: (0, 0)
    return pl.pallas_call(
        body, name="kv_bwd", grid=(NBLK,),
        in_specs=[pl.BlockSpec((WIN, w), cur), pl.BlockSpec((WIN, 1), cur), pl.BlockSpec((1, AH // 2), const),
                  pl.BlockSpec((1, AH), const), pl.BlockSpec((WIN, w), nxt), pl.BlockSpec((WIN, w), cur),
                  pl.BlockSpec((WIN, w), nxt), pl.BlockSpec((WIN, w), cur)],
        out_specs=[pl.BlockSpec((WIN, 2 * w), cur), pl.BlockSpec((1, AH), const), pl.BlockSpec((1, 2 * w), const)],
        out_shape=[jax.ShapeDtypeStruct((S, 2 * w), MXU), jax.ShapeDtypeStruct((1, AH), f32), jax.ShapeDtypeStruct((1, 2 * w), f32)],
        compiler_params=_cparams(("arbitrary",)),
    )(kv, pos, inv_freq, k_norm, dkp, dkc, dvp, dvc)


def adamw(name, w, g, m, v):
    R, C = w.shape
    tr = _pick(R, (256, 128, 64, 32, 16, 8))
    tc = C if tr < R or C % 256 else 256

    def body(w_ref, g_ref, m_ref, v_ref, d_ref, nm_ref, nv_ref):
        g_ = g_ref[...]
        m_ = ADAM_B1 * m_ref[...] + (1.0 - ADAM_B1) * g_
        v_ = ADAM_B2 * v_ref[...] + (1.0 - ADAM_B2) * (g_ * g_)
        m_hat = m_ / (1.0 - ADAM_B1 ** ADAM_STEP)
        v_hat = v_ / (1.0 - ADAM_B2 ** ADAM_STEP)
        d_ref[...] = -ADAM_LR * (m_hat / (jnp.sqrt(v_hat) + ADAM_EPS) + ADAM_WD * w_ref[...])
        nm_ref[...] = m_
        nv_ref[...] = v_

    spec = pl.BlockSpec((tr, tc), lambda i, j: (i, j))
    return pl.pallas_call(
        body, name=name, grid=(R // tr, C // tc), in_specs=[spec] * 4, out_specs=[spec] * 3,
        out_shape=[jax.ShapeDtypeStruct((R, C), f32)] * 3, compiler_params=_cparams(("parallel", "parallel")),
    )(w, g, m, v)


def _me():
    return lax.axis_index("x"), lax.axis_index("y"), lax.axis_index("c")


def gather_comm(xs):
    n = len(xs)

    def parts(x_refs, o_refs, sems):
        send_sems, recv_sems, local_sems = sems
        x, y, c = _me()
        me, sibling = (x, y, c), (x, y, 1 - c)
        chips = [(1 - x, y), (x, 1 - y), (1 - x, 1 - y)]

        def copy(a, k, block, to, src=None):
            dst = o_refs[a].at[4 * block[0] + 2 * block[1] + block[2]]
            return pltpu.make_async_remote_copy(
                src_ref=dst if src is None else src, dst_ref=dst,
                send_sem=send_sems.at[7 * a + k], recv_sem=recv_sems.at[7 * a + k], device_id=to, device_id_type=MESH)

        mine = [pltpu.make_async_copy(x_refs[a], o_refs[a].at[4 * x + 2 * y + c], local_sems.at[a]) for a in range(n)]
        first = []
        for a in range(n):
            first.append(copy(a, 0, me, sibling, src=x_refs[a]))
            first += [copy(a, 1 + j, me, (*chip, c), src=x_refs[a]) for j, chip in enumerate(chips)]
        return copy, mine, first, me, sibling, chips, c

    def start(x_refs, o_refs, sems):
        _, mine, first, *_ = parts(x_refs, o_refs, sems)
        for cp in mine + first:
            cp.start()

    def finish(x_refs, o_refs, sems):
        copy, mine, first, me, sibling, chips, c = parts(x_refs, o_refs, sems)
        passed = []
        for j, chip in enumerate(chips):
            for a in range(n):
                copy(a, 1 + j, (*chip, c), me).wait_recv()
                cp = copy(a, 4 + j, (*chip, c), sibling)
                cp.start()
                passed.append(cp)
        for a in range(n):
            copy(a, 0, sibling, me).wait_recv()
            for j, chip in enumerate(chips):
                copy(a, 4 + j, (*chip, 1 - c), me).wait_recv()
        for cp in first + passed:
            cp.wait_send()
        for cp in mine:
            cp.wait()

    return Comm(xs, [jax.ShapeDtypeStruct((N_DEV,) + a.shape, a.dtype) for a in xs],
                [pltpu.SemaphoreType.DMA((7 * n,)), pltpu.SemaphoreType.DMA((7 * n,)), pltpu.SemaphoreType.DMA((n,))], start, finish)


def run_comm(name, comm):
    _call(lambda: None, [], name=name, grid=(1,), in_specs=[], out_specs=[], out_shape=[], comm=comm)
    return comm.results


def sibling_comm(gs):
    n = len(gs)

    def copies(g_refs, o_refs, sems):
        x, y, c = _me()
        return [pltpu.make_async_remote_copy(
            src_ref=g_refs[a].at[:, 1 - c], dst_ref=o_refs[a], send_sem=sems[0].at[a], recv_sem=sems[1].at[a],
            device_id=(x, y, 1 - c), device_id_type=MESH) for a in range(n)]

    def start(g_refs, o_refs, sems):
        for cp in copies(g_refs, o_refs, sems):
            cp.start()

    def finish(g_refs, o_refs, sems):
        for cp in copies(g_refs, o_refs, sems):
            cp.wait()

    return Comm(gs, [jax.ShapeDtypeStruct((4,) + g.shape[2:], g.dtype) for g in gs],
                [pltpu.SemaphoreType.DMA((n,)), pltpu.SemaphoreType.DMA((n,))], start, finish)


def chip_comm(ts):
    n = len(ts)

    def copies(t_refs, o_refs, sems):
        x, y, c = _me()
        chips = [(1 - x, y), (x, 1 - y), (1 - x, 1 - y)]
        return [pltpu.make_async_remote_copy(
            src_ref=t_refs[a].at[2 * px + py], dst_ref=o_refs[a].at[j],
            send_sem=sems[0].at[3 * a + j], recv_sem=sems[1].at[3 * a + j],
            device_id=(px, py, c), device_id_type=MESH) for j, (px, py) in enumerate(chips) for a in range(n)]

    def start(t_refs, o_refs, sems):
        for cp in copies(t_refs, o_refs, sems):
            cp.start()

    def finish(t_refs, o_refs, sems):
        for cp in copies(t_refs, o_refs, sems):
            cp.wait()

    return Comm(ts, [jax.ShapeDtypeStruct((3,) + t.shape[1:], t.dtype) for t in ts],
                [pltpu.SemaphoreType.DMA((3 * n,)), pltpu.SemaphoreType.DMA((3 * n,))], start, finish)


def _row_tile(rows):
    return _pick(rows, (512, 304, 256, 128))


def pair_add(name, g, r):
    _, _, R, C = g.shape
    tr = _row_tile(R)

    def body(c_ref, g_ref, r_ref, o_ref):
        o_ref[0] = (g_ref[0, 0].astype(f32) + r_ref[0].astype(f32)).astype(o_ref.dtype)

    return pl.pallas_call(
        body, name=name,
        grid_spec=pltpu.PrefetchScalarGridSpec(
            num_scalar_prefetch=1, grid=(4, R // tr),
            in_specs=[pl.BlockSpec((1, 1, tr, C), lambda p, i, c: (p, c[0], i, 0)),
                      pl.BlockSpec((1, tr, C), lambda p, i, c: (p, i, 0))],
            out_specs=pl.BlockSpec((1, tr, C), lambda p, i, c: (p, i, 0))),
        out_shape=jax.ShapeDtypeStruct((4, R, C), g.dtype),
        compiler_params=_cparams(("parallel", "parallel")),
    )(lax.axis_index("c").reshape(1).astype(jnp.int32), g, r)


def final_add(name, t, r):
    _, R, C = t.shape
    tr = _row_tile(R)

    def body(p_ref, t_ref, r_ref, o_ref):
        o_ref[...] = ((t_ref[0].astype(f32) + r_ref[0].astype(f32)) + r_ref[1].astype(f32)) + r_ref[2].astype(f32)

    chip = 2 * lax.axis_index("x") + lax.axis_index("y")
    return pl.pallas_call(
        body, name=name,
        grid_spec=pltpu.PrefetchScalarGridSpec(
            num_scalar_prefetch=1, grid=(R // tr,),
            in_specs=[pl.BlockSpec((1, tr, C), lambda i, p: (p[0], i, 0)),
                      pl.BlockSpec((3, tr, C), lambda i, p: (0, i, 0))],
            out_specs=pl.BlockSpec((tr, C), lambda i, p: (i, 0))),
        out_shape=jax.ShapeDtypeStruct((R, C), f32),
        compiler_params=_cparams(("parallel",)),
    )(chip.reshape(1).astype(jnp.int32), t, r)


class ReduceScatter:
    def __init__(self, tag, keys, grads):
        self.tag, self.keys, self.grads = tag, keys, grads
        self.send = [g.reshape((4, 2, g.shape[0] // N_DEV) + g.shape[1:]) for g in grads]

    def sibling(self):
        self.c1 = sibling_comm(self.send)
        return self.c1

    def chips(self):
        self.pairs = [pair_add(f"rs_pair_add_{self.tag}{i}", g, r) for i, (g, r) in enumerate(zip(self.send, self.c1.results))]
        self.c2 = chip_comm(self.pairs)
        return self.c2

    def sums(self):
        return {k: final_add(f"rs_final_add_{k}", t, r) for k, t, r in zip(self.keys, self.pairs, self.c2.results)}


IN_ROWS = {"z": (0, 2048), "xs": (2048, 4096), "B": (4096, 5120), "C": (5120, 6144)}
IN_COLS = 2 * INNER + 2 * GROUPS * NSTATE + HEADS


def sum_devices(g):
    def body(g_ref, o_ref):
        acc = g_ref[0]
        for i in range(1, N_DEV):
            acc = acc + g_ref[i]
        o_ref[...] = acc

    return pl.pallas_call(body, name="sum_devices", out_shape=jax.ShapeDtypeStruct(g.shape[1:], f32),
                          compiler_params=_cparams())(g)


def _pack(parts, unit, dtype, lead=()):
    flat = jnp.concatenate([p.reshape(lead + (-1,)).astype(dtype) for p in parts], axis=-1)
    n = flat.shape[-1]
    rows = -(-n // (unit * PACK_W)) * unit
    flat = jnp.pad(flat, [(0, 0)] * len(lead) + [(0, rows * PACK_W - n)])
    return flat.reshape(lead + (rows, PACK_W))


def _unpack(buf, shapes, lead=()):
    flat = buf.reshape(lead + (-1,))
    out, off = [], 0
    for shp in shapes:
        n = math.prod(shp)
        out.append(flat[..., off:off + n].reshape(lead + tuple(shp)))
        off += n
    return out


def _pad_lanes(a):
    return jnp.pad(a, [(0, 0)] * (a.ndim - 1) + [(0, LANES - a.shape[-1])])


_NN = (((1,), (0,)), ((), ()))
_NT = (((1,), (1,)), ((), ()))


def _mm(a, b, dims):
    return lax.dot_general(a.astype(MXU), b.astype(MXU), dims, preferred_element_type=f32)


def _ffn_fwd(tag, x, norm_g, w_inT, conv_w, conv_b, mid_comm=None):
    (h,) = rowwise(f"{tag}_norm", lambda x_, g_: (_rms_fwd(x_, g_),), [x], [norm_g], [(D, MXU)])
    ct, nblk = FFN_CT, FFN // FFN_CT

    def body(h_ref, wg_ref, wv_ref, cw_ref, cb_ref, gp_ref, v_ref, a_ref):
        h_ = h_ref[...]
        gp, v_ = _mm(h_, wg_ref[...], _NT), _mm(h_, wv_ref[...], _NT)
        gp_ref[...] = gp
        v_ref[...] = v_
        a_ref[...] = (_silu(_conv(gp, cw_ref[...], cb_ref[...])) * v_).astype(a_ref.dtype)

    col = pl.BlockSpec((S, ct), lambda j: (0, j))
    gate_pre, val, act = _call(
        body, [h, w_inT, w_inT, conv_w, conv_b], comm=mid_comm, name=f"{tag}_in", grid=(nblk,),
        in_specs=[pl.BlockSpec((S, D), lambda j: (0, 0)), pl.BlockSpec((ct, D), lambda j: (j, 0)),
                  pl.BlockSpec((ct, D), lambda j: (nblk + j, 0)), pl.BlockSpec((CONV_F, ct), lambda j: (0, j)),
                  pl.BlockSpec((1, ct), lambda j: (0, j))],
        out_specs=[col, col, col],
        out_shape=[jax.ShapeDtypeStruct((S, FFN), f32), jax.ShapeDtypeStruct((S, FFN), f32), jax.ShapeDtypeStruct((S, FFN), MXU)],
        sem=("parallel",))
    return act, (x, h, gate_pre, val, act)


FFN_CT = 256
CONV_CT = 256


def _proj_conv(name, h, wT, row0, cw, cb):
    C, ct = cw.shape[1], CONV_CT

    def body(h_ref, w_ref, cw_ref, cb_ref, p_ref, c_ref):
        p = _mm(h_ref[...], w_ref[...], _NT)
        p_ref[...] = p
        c_ref[...] = _silu(_conv(p, cw_ref[...], cb_ref[...]))

    col = pl.BlockSpec((S, ct), lambda j: (0, j))
    return _call(
        body, [h, wT, cw, cb], name=name, grid=(C // ct,),
        in_specs=[pl.BlockSpec((S, D), lambda j: (0, 0)), pl.BlockSpec((ct, D), lambda j: (row0 // ct + j, 0)),
                  pl.BlockSpec((CONV_A, ct), lambda j: (0, j)), pl.BlockSpec((1, ct), lambda j: (0, j))],
        out_specs=[col, col], out_shape=[jax.ShapeDtypeStruct((S, C), f32)] * 2, sem=("parallel",))


def _dconv_wgrad(name, pre, dconv, cw, cb, h):
    C, ct = cw.shape[1], CONV_CT

    def body(p_ref, do_ref, cw_ref, cb_ref, h_ref, dp_ref, g_ref, dw_ref, db_ref):
        p_, w_ = p_ref[...], cw_ref[...]
        taps = _taps(p_, CONV_A)
        dx, dw, db = _conv_bwd(p_, w_, do_ref[...] * _dsilu(_conv(p_, w_, cb_ref[...], taps)), taps)
        dpm = dx.astype(MXU)
        dp_ref[...] = dpm
        g_ref[...] = lax.dot_general(dpm, h_ref[...].astype(MXU), (((0,), (0,)), ((), ())),
                                     preferred_element_type=f32).astype(g_ref.dtype)
        dw_ref[...] = dw
        db_ref[...] = db

    col = pl.BlockSpec((S, ct), lambda j: (0, j))
    return _call(
        body, [pre, dconv, cw, cb, h], name=name, grid=(C // ct,),
        in_specs=[col, col, pl.BlockSpec((CONV_A, ct), lambda j: (0, j)), pl.BlockSpec((1, ct), lambda j: (0, j)),
                  pl.BlockSpec((S, D), lambda j: (0, 0))],
        out_specs=[col, pl.BlockSpec((ct, D), lambda j: (j, 0)), pl.BlockSpec((CONV_A, ct), lambda j: (0, j)),
                   pl.BlockSpec((1, ct), lambda j: (0, j))],
        out_shape=[jax.ShapeDtypeStruct((S, C), MXU), jax.ShapeDtypeStruct((C, D), MXU),
                   jax.ShapeDtypeStruct((CONV_A, C), f32), jax.ShapeDtypeStruct((1, C), f32)],
        sem=("parallel",))


def _ffn_mid_bwd(name, dout, w_down, gate_pre, val, conv_w, conv_b, comm=None):
    ct = FFN_CT

    def body(do_ref, wd_ref, gp_ref, v_ref, w_ref, b_ref, dgv_ref, dw_ref, db_ref, dob_s):
        @pl.when(pl.program_id(0) == 0)
        def _():
            dob_s[...] = do_ref[...].astype(MXU)

        da = _mm(dob_s[...], wd_ref[...], _NT)
        gp, v_, w_ = gp_ref[...], v_ref[...], w_ref[...]
        taps = _taps(gp, CONV_F)
        gate = _conv(gp, w_, b_ref[...], taps)
        sg = _sigmoid(gate)
        dgp, dw, db = _conv_bwd(gp, w_, da * v_ * (sg * (1.0 + gate * (1.0 - sg))), taps)
        dgv_ref[0] = dgp.astype(dgv_ref.dtype)
        dgv_ref[1] = (da * (gate * sg)).astype(dgv_ref.dtype)
        dw_ref[...] = dw
        db_ref[...] = db

    col = pl.BlockSpec((S, ct), lambda j: (0, j))
    return _call(
        body, [dout, w_down, gate_pre, val, conv_w, conv_b], comm=comm, name=name, grid=(FFN // ct,),
        in_specs=[pl.BlockSpec((S, D), lambda j: (0, 0)), pl.BlockSpec((ct, D), lambda j: (j, 0)), col, col,
                  pl.BlockSpec((CONV_F, ct), lambda j: (0, j)), pl.BlockSpec((1, ct), lambda j: (0, j))],
        out_specs=[pl.BlockSpec((2, S, ct), lambda j: (0, 0, j)), pl.BlockSpec((CONV_F, ct), lambda j: (0, j)),
                   pl.BlockSpec((1, ct), lambda j: (0, j))],
        out_shape=[jax.ShapeDtypeStruct((2, S, FFN), MXU), jax.ShapeDtypeStruct((CONV_F, FFN), f32), jax.ShapeDtypeStruct((1, FFN), f32)],
        scratch=[pltpu.VMEM((S, D), MXU)], sem=("arbitrary",))


def _ffn_bwd(tag, layer, saved, norm_g, w_inT, conv_w, conv_b, w_down, dout, mid_comm=None):
    x, h, gate_pre, val, act = saved
    g_down = matmul(f"{tag}_wdown", act, dout, "tn", out_dtype=MXU)
    dgv, g_cw, g_cb = _ffn_mid_bwd(f"{tag}_dmid", dout, w_down, gate_pre, val, conv_w, conv_b, comm=mid_comm)
    g_inT = matmul_tn_stacked(f"{tag}_win", dgv, h, MXU)

    def din_fn(dg_, dv_, x_, do_, g_, wT):
        dx, dg = _rms_bwd(x_, g_, _mm(dg_, wT[:FFN], _NN) + _mm(dv_, wT[FFN:], _NN))
        return do_ + dx, dg

    rs = ReduceScatter(tag, (f"f_inT{layer}", f"f_down{layer}"), [g_inT, g_down])
    dx, g_norm = rowwise(f"{tag}_din", din_fn, [(dgv, 0), (dgv, 1), x, dout], [norm_g, w_inT], [(D, f32)], [(1, D)],
                         comm=rs.sibling())
    return dx, {f"f_norm{layer}": g_norm, f"f_conv_w{layer}": g_cw, f"f_conv_b{layer}": g_cb}, rs


def _land(W, keys, comm):
    for k, g in zip(keys, comm.results):
        W[k] = g.reshape(-1, g.shape[2])


def _local_step(x, pos, tgt, W, shards):
    G = {}
    gather = lambda *keys: gather_comm([shards[k] for k in keys])
    inv_freq = (ROPE_THETA ** (-jnp.arange(AH // 2, dtype=f32) / (AH // 2))).reshape(1, AH // 2)

    def in_fn(x_, g_, wT, wdtT):
        h_ = _rms_fwd(x_, g_).astype(MXU)
        return h_, _mm(h_, wT[slice(*IN_ROWS["z"])], _NT), _mm(h_, wdtT, _NT)

    h0, z, dt_pre = rowwise("a_in", in_fn, [x], [W["a_norm"], W["inT"], W["in_dtT"]], [(D, MXU), (INNER, f32), (LANES, f32)])
    pre, conv = {}, {}
    for k in ("xs", "B", "C"):
        pre[k], conv[k] = _proj_conv(f"a_in_{k}", h0, W["inT"], IN_ROWS[k][0], W[f"cw_{k}"], W[f"cb_{k}"])
    c = gather("a_out", "f_inT0", "f_down0")
    y, states = ssd_fwd(conv["xs"], conv["B"], conv["C"], dt_pre, W["dt_bias"], W["A_log"], W["D"], comm=c)
    _land(W, ("a_out", "f_inT0", "f_down0"), c)

    def gate_norm(y_, z_, g_):
        yg = y_ * _silu(z_)
        w = INNER // GROUPS
        return (jnp.concatenate([_rms_fwd(yg[:, w * i:w * (i + 1)], g_[:, w * i:w * (i + 1)]) for i in range(GROUPS)], axis=1),)

    def out_fn(y_, z_, x_, g_, w_):
        (gn_,) = gate_norm(y_, z_, g_)
        gn_ = gn_.astype(MXU)
        return gn_, x_ + _mm(gn_, w_, _NN)

    gn, x1 = rowwise("a_out", out_fn, [y, z, x], [W["a_gnorm"], W["a_out"]], [(INNER, MXU), (D, f32)])

    c = gather("w_kv", "w_q", "w_o")
    act0, ffn0 = _ffn_fwd("f0", x1, W["f_norm0"], W["f_inT0"], W["f_cw0"], W["f_cb0"], mid_comm=c)
    _land(W, ("w_kv", "w_q", "w_o"), c)
    x2 = matmul("f0_down", act0, W["f_down0"], "nn", residual=x1)

    def qkv_fn(x_, gk, gb, wkv, bkv, wq, bq):
        kvn_, h2_ = _rms_fwd(x_, gk).astype(MXU), _rms_fwd(x_, gb).astype(MXU)
        return kvn_, h2_, _mm(kvn_, wkv, _NN) + bkv, _mm(h2_, wq, _NN) + bq

    kw = KVH * AH
    kvn, h2, kv, q_pre = rowwise("qkv_proj", qkv_fn, [x2], [W["kv_norm"], W["b_norm"], W["w_kv"], W["b_kv"], W["w_q"], W["b_q"]],
                                 [(D, MXU), (D, MXU), (2 * kw, f32), (D, f32)])

    def k_fwd(kv_, pos_, if_, g_):
        cos, sin = _rope_tables(pos_, if_)
        return _headnorm_rope_fwd(kv_[:, :kw], g_, cos, sin, KVH), kv_[:, kw:]

    k_rot, v_val = rowwise("k_rope", k_fwd, [kv, pos], [inv_freq, W["k_norm"]], [(kw, f32), (kw, f32)])

    def q_fwd(q_, pos_, if_, g_):
        cos, sin = _rope_tables(pos_, if_)
        return (_headnorm_rope_fwd(q_, g_, cos, sin, QH),)

    c = gather("f_down1")
    (q,) = rowwise("q_rope", q_fwd, [q_pre, pos], [inv_freq, W["q_norm"]], [(D, f32)], comm=c)
    _land(W, ("f_down1",), c)
    c = gather("f_inT1")
    att = attn_fwd(q, k_rot, v_val, W["sinks"], comm=c)
    _land(W, ("f_inT1",), c)
    x3 = matmul("o_proj", att, W["w_o"], "nn", bias=W["b_o"], residual=x2)

    act1, ffn1 = _ffn_fwd("f1", x3, W["f_norm1"], W["f_inT1"], W["f_cw1"], W["f_cb1"])

    def loss_fn(a_, x_, t_, w_):
        diff = x_ + _mm(a_, w_, _NN) - t_
        rows = jnp.sum(diff * diff, axis=1, keepdims=True) * (0.5 / D)
        return diff * (1.0 / D), jnp.sum(rows, axis=0, keepdims=True)

    dx4, loss = rowwise("f1_down_loss", loss_fn, [act1, x3, tgt], [W["f_down1"]], [(D, f32)], [(1, 1)])

    dx3, g, rs_f1 = _ffn_bwd("f1", 1, ffn1, W["f_norm1"], W["f_inT1"], W["f_cw1"], W["f_cb1"], W["f_down1"], dx4)
    G.update(g)

    datt = matmul("o_dproj", dx3, W["w_o"], "nt")
    g_wo = matmul("o_wproj", att, dx3, "tn", out_dtype=MXU)
    dq, dkp, dkc, dvp, dvc, G["sinks"] = attn_bwd(q, k_rot, v_val, W["sinks"], datt, comm=rs_f1.chips())

    def q_bwd(q_, pos_, dq_, dx_, if_, g_):
        cos, sin = _rope_tables(pos_, if_)
        dqp, dg = _headnorm_rope_bwd(q_, g_, cos, sin, dq_, QH)
        return dqp, dg, jnp.sum(dqp, axis=0, keepdims=True), jnp.sum(dx_, axis=0, keepdims=True)

    dq_pre, G["q_norm"], G["b_q"], G["b_o"] = rowwise("q_drope", q_bwd, [q_pre, pos, dq, dx3], [inv_freq, W["q_norm"]],
                                                      [(D, MXU)], [(1, AH), (1, D), (1, D)])
    g_wq = matmul("q_wproj", h2, dq_pre, "tn", out_dtype=MXU)
    dkv, G["k_norm"], G["b_kv"] = kv_bwd(kv, pos, inv_freq, W["k_norm"], dkp, dkc, dvp, dvc)
    g_wkv = matmul("kv_wproj", kvn, dkv, "tn", out_dtype=MXU)
    rs_att = ReduceScatter("att", ("w_kv", "w_q", "w_o"), [g_wkv, g_wq, g_wo])

    def x2_bwd(x_, dq_, dkv_, dx_, gb_, gk_, wq, wkv):
        d1, dgb = _rms_bwd(x_, gb_, _mm(dq_, wq, _NT))
        d2, dgk = _rms_bwd(x_, gk_, _mm(dkv_, wkv, _NT))
        return dx_ + d1 + d2, dgb, dgk

    dx2, G["b_norm"], G["kv_norm"] = rowwise("qkv_dproj", x2_bwd, [x2, dq_pre, dkv, dx3],
                                             [W["b_norm"], W["kv_norm"], W["w_q"], W["w_kv"]],
                                             [(D, f32)], [(1, D), (1, D)], comm=rs_att.sibling())

    dx1, g, rs_f0 = _ffn_bwd("f0", 0, ffn0, W["f_norm0"], W["f_inT0"], W["f_cw0"], W["f_cb0"], W["f_down0"], dx2,
                             mid_comm=rs_att.chips())
    G.update(g)

    rs_out = ReduceScatter("a_out", ("a_out",), [matmul("a_wout", gn, dx1, "tn", out_dtype=MXU)])

    def gate_norm_bwd(y_, z_, dx_, g_, w_out):
        dgn_ = _mm(dx_, w_out, _NT)
        w = INNER // GROUPS
        sg = _sigmoid(z_)
        sz = z_ * sg
        yg = y_ * sz
        parts, dgs = [], []
        for i in range(GROUPS):
            dseg, dg = _rms_bwd(yg[:, w * i:w * (i + 1)], g_[:, w * i:w * (i + 1)], dgn_[:, w * i:w * (i + 1)])
            parts.append(dseg)
            dgs.append(dg)
        dyg = jnp.concatenate(parts, axis=1)
        return dyg * sz, dyg * y_ * (sg * (1.0 + z_ * (1.0 - sg))), jnp.concatenate(dgs, axis=1)

    dy, dz, G["a_gnorm"] = rowwise("a_dout", gate_norm_bwd, [y, z, dx1], [W["a_gnorm"], W["a_out"]],
                                   [(INNER, f32), (INNER, MXU)], [(1, INNER)], comm=rs_out.sibling())
    dconv = {}
    dconv["xs"], dconv["B"], dconv["C"], ddt_pre, G["dt_bias"], G["A_log"], G["D"] = ssd_bwd(
        conv["xs"], conv["B"], conv["C"], dt_pre, W["dt_bias"], W["A_log"], W["D"], states, dy,
        comm=merge_comms([rs_f0.chips(), rs_out.chips()]))

    g_in, dpre = [matmul("a_win_z", dz, h0, "tn", out_dtype=MXU)], {}
    for k in ("xs", "B", "C"):
        dpre[k], g_k, G[f"cw_{k}"], G[f"cb_{k}"] = _dconv_wgrad(f"a_dconv_{k}", pre[k], dconv[k], W[f"cw_{k}"], W[f"cb_{k}"], h0)
        g_in.append(g_k)
    g_in.append(matmul("a_win_dt", ddt_pre, h0, "tn", out_dtype=MXU)[:HEADS])
    rs_in = ReduceScatter("a_in", ("inT",), [jnp.concatenate(g_in, axis=0)])
    run_comm("rs_in_sibling", rs_in.sibling())

    def x0_bwd(dz_, dxs_, db_, dc_, ddt_, x_, do_, g_, wT, wdtT):
        parts = zip((dz_, dxs_, db_, dc_), IN_ROWS.values())
        dh = sum(_mm(d_, wT[a:b], _NN) for d_, (a, b) in parts) + _mm(ddt_, wdtT, _NN)
        dx, dg = _rms_bwd(x_, g_, dh)
        return do_ + dx, dg

    dx, G["a_norm"] = rowwise("a_din", x0_bwd, [dz, dpre["xs"], dpre["B"], dpre["C"], ddt_pre, x, dx1],
                              [W["a_norm"], W["inT"], W["in_dtT"]], [(D, f32)], [(1, D)], comm=rs_in.chips())
    return loss, dx, G, [rs_f1, rs_att, rs_f0, rs_out, rs_in]


ROW_KEYS = ("inT", "a_out", "f_inT0", "f_down0", "w_kv", "w_q", "w_o", "f_inT1", "f_down1")


def _row_blocks(src):
    return {"inT": src["a_in_proj"][0].T, "a_out": src["a_out_proj"][0], "w_kv": src["w_kv"], "w_q": src["w_q"][0],
            "w_o": src["w_o"][0], "f_inT0": src["f_w_in"][0].T, "f_inT1": src["f_w_in"][1].T,
            "f_down0": src["f_w_down"][0], "f_down1": src["f_w_down"][1]}


def _from_row_blocks(rb):
    return {"a_in_proj": rb["inT"].T[None], "a_out_proj": rb["a_out"][None], "w_kv": rb["w_kv"], "w_q": rb["w_q"][None],
            "w_o": rb["w_o"][None], "f_w_in": jnp.stack([rb["f_inT0"].T, rb["f_inT1"].T]),
            "f_w_down": jnp.stack([rb["f_down0"], rb["f_down1"]])}


SMALL_SHARDED = ("a_norm", "a_conv_w", "a_conv_b", "a_gnorm", "f_conv_w")
REPLICATED = ("a_dt_bias", "a_A_log", "a_D", "kv_norm", "b_kv", "k_norm", "b_norm", "b_q", "q_norm", "sinks", "b_o",
              "f_norm", "f_conv_b")
ORDER = ("a_norm", "a_in_proj", "a_conv_w", "a_conv_b", "a_dt_bias", "a_A_log", "a_D", "a_gnorm", "a_out_proj", "kv_norm",
         "w_kv", "b_kv", "k_norm", "b_norm", "w_q", "b_q", "q_norm", "sinks", "w_o", "b_o", "f_norm", "f_w_in",
         "f_conv_w", "f_conv_b", "f_w_down")


def _gathered_to_whole(name, g):
    if name == "a_conv_w":
        return jnp.moveaxis(g[:, 0], 0, 1).reshape(g.shape[2], -1)
    if name in ("a_norm", "a_conv_b", "a_gnorm"):
        return g[:, 0].reshape(1, -1)
    if name == "f_conv_w":
        return jnp.moveaxis(g, 0, 2).reshape(g.shape[1], g.shape[2], -1)
    raise ValueError(name)


def _whole_to_shards(name, w):
    if name == "a_conv_w":
        return jnp.moveaxis(w.reshape(w.shape[0], N_DEV, -1), 1, 0)[:, None]
    if name in ("a_norm", "a_conv_b", "a_gnorm"):
        return w.reshape(N_DEV, 1, -1)
    if name == "f_conv_w":
        return jnp.moveaxis(w.reshape(w.shape[0], w.shape[1], N_DEV, -1), 2, 0)
    raise ValueError(name)


def _small_weights(whole):
    W = {}
    cw, cb = whole["a_conv_w"], whole["a_conv_b"]
    o = 0
    for k, n in (("xs", INNER), ("B", GROUPS * NSTATE), ("C", GROUPS * NSTATE)):
        W[f"cw_{k}"], W[f"cb_{k}"] = cw[:, o:o + n], cb[:, o:o + n]
        o += n
    W["a_norm"], W["a_gnorm"] = whole["a_norm"], whole["a_gnorm"]
    W["dt_bias"], W["A_log"], W["D"] = (_pad_lanes(whole[k]) for k in ("a_dt_bias", "a_A_log", "a_D"))
    W["kv_norm"], W["b_kv"], W["k_norm"] = whole["kv_norm"].reshape(1, -1), whole["b_kv"].reshape(1, -1), whole["k_norm"].reshape(1, -1)
    for k in ("b_norm", "b_q", "q_norm", "sinks", "b_o"):
        W[k] = whole[k]
    for i in range(2):
        W[f"f_norm{i}"] = whole["f_norm"][i:i + 1]
        W[f"f_cw{i}"], W[f"f_cb{i}"] = whole["f_conv_w"][i], whole["f_conv_b"][i:i + 1]
    return W


def _small_grads(G, shapes):
    nh = HEADS
    out = {
        "a_conv_w": jnp.concatenate([G["cw_xs"], G["cw_B"], G["cw_C"]], axis=1),
        "a_conv_b": jnp.concatenate([G["cb_xs"], G["cb_B"], G["cb_C"]], axis=1),
        "a_norm": G["a_norm"], "a_gnorm": G["a_gnorm"],
        "a_dt_bias": G["dt_bias"][:, :nh], "a_A_log": G["A_log"][:, :nh], "a_D": G["D"][:, :nh],
        "kv_norm": G["kv_norm"], "b_kv": G["b_kv"], "k_norm": G["k_norm"], "b_norm": G["b_norm"],
        "b_q": G["b_q"], "q_norm": G["q_norm"], "sinks": G["sinks"], "b_o": G["b_o"],
        "f_norm": jnp.concatenate([G["f_norm0"], G["f_norm1"]], axis=0),
        "f_conv_w": jnp.stack([G["f_conv_w0"], G["f_conv_w1"]]),
        "f_conv_b": jnp.concatenate([G["f_conv_b0"], G["f_conv_b1"]], axis=0),
    }
    return {k: val.reshape(shapes[k]) if k in shapes else val for k, val in out.items()}


def kernel(x, positions, a_norm, a_in_proj, a_conv_w, a_conv_b, a_dt_bias, a_A_log, a_D, a_gnorm, a_out_proj, kv_norm, w_kv, b_kv, k_norm, b_norm, w_q, b_q, q_norm, sinks, w_o, b_o, f_norm, f_w_in, f_conv_w, f_conv_b, f_w_down, loss_target, m_a_norm, m_a_in_proj, m_a_conv_w, m_a_conv_b, m_a_dt_bias, m_a_A_log, m_a_D, m_a_gnorm, m_a_out_proj, m_kv_norm, m_w_kv, m_b_kv, m_k_norm, m_b_norm, m_w_q, m_b_q, m_q_norm, m_sinks, m_w_o, m_b_o, m_f_norm, m_f_w_in, m_f_conv_w, m_f_conv_b, m_f_w_down, v_a_norm, v_a_in_proj, v_a_conv_w, v_a_conv_b, v_a_dt_bias, v_a_A_log, v_a_D, v_a_gnorm, v_a_out_proj, v_kv_norm, v_w_kv, v_b_kv, v_k_norm, v_b_norm, v_w_q, v_b_q, v_q_norm, v_sinks, v_w_o, v_b_o, v_f_norm, v_f_w_in, v_f_conv_w, v_f_conv_b, v_f_w_down):
    given = dict(locals())
    w_in = {n: given[n] for n in ORDER}
    m_in = {n: given["m_" + n] for n in ORDER}
    v_in = {n: given["v_" + n] for n in ORDER}
    dev = 4 * lax.axis_index("x") + 2 * lax.axis_index("y") + lax.axis_index("c")

    w2, m2, v2 = _row_blocks(w_in), _row_blocks(m_in), _row_blocks(v_in)
    small_pack = _pack([w_in[n] for n in SMALL_SHARDED], 8, f32)
    shards = {k: w2[k].astype(MXU) for k in ROW_KEYS}
    in_all, small_all = run_comm("ag_head", gather_comm([shards["inT"], small_pack]))
    whole = {n: w_in[n] for n in REPLICATED}
    for n, g in zip(SMALL_SHARDED, _unpack(small_all, [w_in[n].shape for n in SMALL_SHARDED], lead=(N_DEV,))):
        whole[n] = _gathered_to_whole(n, g)
    W = _small_weights(whole)
    W["inT"] = in_all.reshape(-1, D)
    W["in_dtT"] = jnp.pad(W["inT"][IN_COLS - HEADS:], ((0, LANES - HEADS), (0, 0)))

    loss, dx, G, scatters = _local_step(x[0], positions.reshape(S, 1).astype(f32), loss_target[0], W, shards)
    grads = _small_grads(G, {n: whole[n].shape for n in REPLICATED})

    small_names = SMALL_SHARDED + REPLICATED
    small_part = _pack([grads[n] for n in small_names], 8, f32)
    small_gather = gather_comm([small_part])
    run_comm("ag_small_grads", small_gather)
    sums = {}
    for rs in scatters:
        sums.update(rs.sums())
    g_out = _from_row_blocks(sums)
    small_sum = sum_devices(small_gather.results[0])
    for n, g in zip(small_names, _unpack(small_sum, [grads[n].shape for n in small_names])):
        if n in SMALL_SHARDED:
            g_out[n] = lax.dynamic_index_in_dim(_whole_to_shards(n, g), dev, axis=0, keepdims=False)
        else:
            g_out[n] = g.reshape(w_in[n].shape)

    stepped = {k: adamw(f"adamw_{k}", w2[k], sums[k], m2[k], v2[k]) for k in ROW_KEYS}
    delta, new_m, new_v = (_from_row_blocks({k: stepped[k][i] for k in ROW_KEYS}) for i in range(3))
    packs = [_pack([src[n] for n in small_names], 8, f32) for src in (w_in, g_out, m_in, v_in)]
    outs = adamw("adamw_small", *packs)
    for dst, buf in zip((delta, new_m, new_v), outs):
        for n, a in zip(small_names, _unpack(buf, [w_in[n].shape for n in small_names])):
            dst[n] = a

    loss_all = lax.psum(loss[0, 0], AXES)
    return (loss_all, dx[None], *[g_out[n] for n in ORDER], *[delta[n] for n in ORDER],
            *[new_m[n] for n in ORDER], *[new_v[n] for n in ORDER])
```

```python
import functools
import math

import jax
import jax.numpy as jnp
from jax import lax
from jax.experimental import pallas as pl
from jax.experimental.pallas import tpu as pltpu

f32 = jnp.float32
bf16 = jnp.bfloat16
MXU = bf16

N_DEV = 8
S = 2048
D = 1024
EPS = 1e-5
INNER = 2048
HEADS = 32
HP = 64
GROUPS = 8
HPG = HEADS // GROUPS
NSTATE = 128
CONV_A = 4
CHUNK = 256
NCHUNK = S // CHUNK
AH = 64
QH = 16
KVH = 4
QPK = QH // KVH
WIN = 128
NBLK = S // WIN
ROPE_THETA = 10000.0
FFN = 2816
CONV_F = 3
LANES = 128
PACK_W = 1024
BIG_TILE = 496
VMEM_LIMIT = 56 * 1024 * 1024

ADAM_LR, ADAM_B1, ADAM_B2, ADAM_EPS, ADAM_WD, ADAM_STEP = 0.001, 0.9, 0.999, 1e-08, 0.01, 10

MESH = pl.DeviceIdType.MESH
AXES = ("x", "y", "c")


def _cparams(sem=None):
    return pltpu.CompilerParams(dimension_semantics=sem, vmem_limit_bytes=VMEM_LIMIT)


def _pick(n, cands):
    for c in cands:
        if n % c == 0:
            return c
    return n


class Comm:
    def __init__(self, ins, out_shapes, sems, start, finish):
        self.ins, self.out_shapes, self.sems, self.start, self.finish = list(ins), list(out_shapes), list(sems), start, finish
        self.results, self.children = None, ()

    def set_results(self, res):
        self.results, o = list(res), 0
        for ch in self.children:
            ch.set_results(res[o:o + len(ch.out_shapes)])
            o += len(ch.out_shapes)


def merge_comms(comms):
    def each(fn_name, ins, outs, sems):
        i = o = s = 0
        for c in comms:
            getattr(c, fn_name)(ins[i:i + len(c.ins)], outs[o:o + len(c.out_shapes)], sems[s:s + len(c.sems)])
            i, o, s = i + len(c.ins), o + len(c.out_shapes), s + len(c.sems)

    merged = Comm([a for c in comms for a in c.ins], [a for c in comms for a in c.out_shapes], [a for c in comms for a in c.sems],
                  functools.partial(each, "start"), functools.partial(each, "finish"))
    merged.children = tuple(comms)
    return merged


def _call(body, args, *, name, grid, in_specs, out_specs, out_shape, scratch=(), sem=None, comm=None):
    if comm is None:
        return pl.pallas_call(body, name=name, grid=grid, in_specs=list(in_specs), out_specs=list(out_specs),
                              out_shape=list(out_shape), scratch_shapes=list(scratch), compiler_params=_cparams(sem))(*args)
    n_in, n_out, n_scr, c_in, c_out = len(in_specs), len(out_shape), len(scratch), len(comm.ins), len(comm.out_shapes)
    any_spec = pl.BlockSpec(memory_space=pl.ANY)

    def outer(*refs):
        ins, c_ins = refs[:n_in], refs[n_in:n_in + c_in]
        o = n_in + c_in
        outs, c_outs = refs[o:o + n_out], refs[o + n_out:o + n_out + c_out]
        o += n_out + c_out
        scr, c_sems = refs[o:o + n_scr], refs[o + n_scr:]
        ids = [pl.program_id(i) for i in range(len(grid))]
        first = functools.reduce(jnp.logical_and, [i == 0 for i in ids])
        last = functools.reduce(jnp.logical_and, [i == g - 1 for i, g in zip(ids, grid)])

        @pl.when(first)
        def _():
            comm.start(c_ins, c_outs, c_sems)

        body(*ins, *outs, *scr)

        @pl.when(last)
        def _():
            comm.finish(c_ins, c_outs, c_sems)

    res = pl.pallas_call(
        outer, name=name, grid=grid, in_specs=list(in_specs) + [any_spec] * c_in,
        out_specs=list(out_specs) + [any_spec] * c_out, out_shape=list(out_shape) + comm.out_shapes,
        scratch_shapes=list(scratch) + comm.sems, compiler_params=_cparams(("arbitrary",) * len(grid)),
    )(*args, *comm.ins)
    comm.set_results(res[n_out:])
    return res[:n_out]


def matmul(name, a, b, mode, out_dtype=f32, bias=None, residual=None, b_rows=None, a_lead=None):
    b_idx, b_shape = (0, b.shape) if b_rows is None else (b_rows[0], (b_rows[1], b.shape[1]))
    a_shape = a.shape if a_lead is None else a.shape[1:]
    if mode == "nn":
        (M, K), (K2, N) = a_shape, b_shape
    elif mode == "nt":
        (M, K), (N, K2) = a_shape, b_shape
    else:
        (K, M), (K2, N) = a_shape, b_shape
    assert K == K2, (name, a_shape, b_shape)
    if mode == "tn":
        tm, tn = M, _pick(N, (512, 256, 128) if M <= 1024 else (256, 128))
        a_spec = pl.BlockSpec((K, M), lambda j: (0, 0))
        b_spec = pl.BlockSpec((K, tn), lambda j: (0, j))
        dims = (((0,), (0,)), ((), ()))
        grid, o_map, row_map = (N // tn,), (lambda j: (0, j)), (lambda j: (0, j))
    else:
        tm, tn = (256 if N >= 2048 else 512), N
        a_spec = pl.BlockSpec((tm, K), lambda i: (i, 0)) if a_lead is None else pl.BlockSpec((None, tm, K), lambda i: (a_lead, i, 0))
        b_spec = pl.BlockSpec(b_shape, lambda i: (b_idx, 0))
        dims = (((1,), (0,)), ((), ())) if mode == "nn" else (((1,), (1,)), ((), ()))
        grid, o_map, row_map = (M // tm,), (lambda i: (i, 0)), (lambda i: (0, 0))
    ins, in_specs = [a, b], [a_spec, b_spec]
    if bias is not None:
        ins.append(bias)
        in_specs.append(pl.BlockSpec((1, tn), row_map))
    if residual is not None:
        ins.append(residual)
        in_specs.append(pl.BlockSpec((tm, tn), o_map))
    has_bias, has_res = bias is not None, residual is not None

    def body(a_ref, b_ref, *rest):
        rest = list(rest)
        bias_ref = rest.pop(0) if has_bias else None
        res_ref = rest.pop(0) if has_res else None
        (o_ref,) = rest
        r = lax.dot_general(a_ref[...].astype(MXU), b_ref[...].astype(MXU), dims, preferred_element_type=f32)
        if has_bias:
            r = r + bias_ref[...]
        if has_res:
            r = r + res_ref[...]
        o_ref[...] = r.astype(out_dtype)

    return pl.pallas_call(
        body, name=name, grid=grid, in_specs=in_specs,
        out_specs=pl.BlockSpec((tm, tn), o_map),
        out_shape=jax.ShapeDtypeStruct((M, N), out_dtype),
        compiler_params=_cparams(("parallel",)),
    )(*ins)


def matmul_tn_stacked(name, a, b, out_dtype):
    R, K, M = a.shape
    N = b.shape[1]
    tn = _pick(N, (256, 128))

    def body(a_ref, b_ref, o_ref):
        o_ref[0] = lax.dot_general(a_ref[0].astype(MXU), b_ref[...].astype(MXU), (((0,), (0,)), ((), ())),
                                   preferred_element_type=f32).astype(out_dtype)

    out = pl.pallas_call(
        body, name=name, grid=(R, N // tn),
        in_specs=[pl.BlockSpec((1, K, M), lambda r, j: (r, 0, 0)), pl.BlockSpec((K, tn), lambda r, j: (0, j))],
        out_specs=pl.BlockSpec((1, M, tn), lambda r, j: (r, 0, j)),
        out_shape=jax.ShapeDtypeStruct((R, M, N), out_dtype),
        compiler_params=_cparams(("parallel", "parallel")),
    )(a, b)
    return out.reshape(R * M, N)


def rowwise(name, fn, rows, pars, outs, accs=(), tile=256, comm=None):
    n_in, n_out = len(rows) + len(pars), len(outs)
    in_specs = [pl.BlockSpec((None, tile, r[0].shape[2]), functools.partial(lambda i, lead: (lead, i, 0), lead=r[1]))
                if isinstance(r, tuple) else pl.BlockSpec((tile, r.shape[1]), lambda i: (i, 0)) for r in rows]
    rows = [r[0] if isinstance(r, tuple) else r for r in rows]
    in_specs += [pl.BlockSpec(p.shape, lambda i: (0, 0)) for p in pars]
    out_specs = [pl.BlockSpec((tile, c), lambda i: (i, 0)) for c, _ in outs]
    out_specs += [pl.BlockSpec(shp, lambda i: (0, 0)) for shp in accs]
    out_shape = [jax.ShapeDtypeStruct((S, c), dt) for c, dt in outs]
    out_shape += [jax.ShapeDtypeStruct(shp, f32) for shp in accs]

    def body(*refs):
        res = fn(*[r[...] for r in refs[:n_in]])
        o_refs = refs[n_in:n_in + n_out]
        a_refs = refs[n_in + n_out:]
        for ref, val in zip(o_refs, res[:n_out]):
            ref[...] = val.astype(ref.dtype)
        if a_refs:
            @pl.when(pl.program_id(0) == 0)
            def _():
                for ref in a_refs:
                    ref[...] = jnp.zeros_like(ref)
            for ref, val in zip(a_refs, res[n_out:]):
                ref[...] += val

    return _call(body, [*rows, *pars], name=name, grid=(S // tile,), in_specs=in_specs, out_specs=out_specs,
                 out_shape=out_shape, sem=("arbitrary",) if accs else ("parallel",), comm=comm)


def colwise(name, fn, cols, pars, outs, pouts=(), ct=128, comm=None):
    C = cols[0].shape[1]
    n_in, n_out = len(cols) + len(pars), len(outs)
    in_specs = [pl.BlockSpec((S, ct), lambda j: (0, j)) for _ in cols]
    in_specs += [pl.BlockSpec((p.shape[0], ct), lambda j: (0, j)) for p in pars]
    out_specs = [pl.BlockSpec((S, ct), lambda j: (0, j)) for _ in outs]
    out_specs += [pl.BlockSpec((r, ct), lambda j: (0, j)) for r in pouts]
    out_shape = [jax.ShapeDtypeStruct((S, C), dt) for dt in outs]
    out_shape += [jax.ShapeDtypeStruct((r, C), f32) for r in pouts]

    def body(*refs):
        res = fn(*[r[...] for r in refs[:n_in]])
        for ref, val in zip(refs[n_in:], res):
            ref[...] = val.astype(ref.dtype)

    return _call(body, [*cols, *pars], name=name, grid=(C // ct,), in_specs=in_specs, out_specs=out_specs,
                 out_shape=out_shape, sem=("parallel",), comm=comm)


def _sigmoid(x):
    return 0.5 * jnp.tanh(0.5 * x) + 0.5


def _silu(x):
    return x * _sigmoid(x)


def _dsilu(x):
    sg = _sigmoid(x)
    return sg * (1.0 + x * (1.0 - sg))


def _softplus(x):
    return jnp.maximum(x, 0.0) + jnp.log(1.0 + jnp.exp(-jnp.abs(x)))


def _rms_fwd(x, g):
    r = lax.rsqrt(jnp.mean(x * x, axis=-1, keepdims=True) + EPS)
    return x * r * g


def _rms_bwd(x, g, dh):
    r = lax.rsqrt(jnp.mean(x * x, axis=-1, keepdims=True) + EPS)
    xh = x * r
    dxh = dh * g
    dx = r * (dxh - xh * jnp.mean(dxh * xh, axis=-1, keepdims=True))
    return dx, jnp.sum(dh * xh, axis=0, keepdims=True)


def _taps(x, width):
    row = lax.broadcasted_iota(jnp.int32, (8, x.shape[1]), 0)

    def shifted(s):
        r = pltpu.roll(x, s, 0)
        return jnp.concatenate([jnp.where(row >= s, r[:8], 0.0), r[8:]], axis=0)

    return [shifted(s) for s in range(width - 1, 0, -1)] + [x]


def _conv(x, w, b, taps=None):
    width = w.shape[0]
    taps = _taps(x, width) if taps is None else taps
    out = b + w[0:1, :] * taps[0]
    for k in range(1, width):
        out = out + w[k:k + 1, :] * taps[k]
    return out


def _conv_bwd(x, w, dc, taps=None):
    width, n = w.shape[0], x.shape[0]
    taps = _taps(x, width) if taps is None else taps
    row = lax.broadcasted_iota(jnp.int32, (8, x.shape[1]), 0)
    dx = w[width - 1:width, :] * dc
    for k in range(width - 1):
        s = width - 1 - k
        r = pltpu.roll(dc, n - s, 0)
        dx = dx + w[k:k + 1, :] * jnp.concatenate([r[:n - 8], jnp.where(row < 8 - s, r[n - 8:], 0.0)], axis=0)
    dw = jnp.concatenate([jnp.sum(dc * t, axis=0, keepdims=True) for t in taps], axis=0)
    return dx, dw, jnp.sum(dc, axis=0, keepdims=True)


def _rope_tables(pos, inv_freq):
    ang = pos * inv_freq
    return jnp.cos(ang), jnp.sin(ang)


def _headnorm_rope_fwd(x, g, cos, sin, heads):
    half = AH // 2
    outs = []
    for h in range(heads):
        seg = x[:, AH * h:AH * (h + 1)]
        n = _rms_fwd(seg, g)
        n1, n2 = n[:, :half], n[:, half:]
        outs += [n1 * cos - n2 * sin, n2 * cos + n1 * sin]
    return jnp.concatenate(outs, axis=1)


def _headnorm_rope_bwd(x, g, cos, sin, dout, heads):
    half = AH // 2
    dxs, dg = [], jnp.zeros((1, AH), f32)
    for h in range(heads):
        seg = x[:, AH * h:AH * (h + 1)]
        d = dout[:, AH * h:AH * (h + 1)]
        d1, d2 = d[:, :half], d[:, half:]
        dn = jnp.concatenate([d1 * cos + d2 * sin, d2 * cos - d1 * sin], axis=1)
        dx, dgh = _rms_bwd(seg, g, dn)
        dxs.append(dx)
        dg = dg + dgh
    return jnp.concatenate(dxs, axis=1), dg


def _ssd_prep(dt_pre, dt_bias, a_log, dt_s, acum_s, acumT_s):
    dt = _softplus(dt_pre + dt_bias)
    a = dt * (-jnp.exp(a_log))
    row = lax.broadcasted_iota(jnp.int32, (CHUNK, CHUNK), 0)
    col = lax.broadcasted_iota(jnp.int32, (CHUNK, CHUNK), 1)
    dt_s[...] = dt
    acum_s[...] = jnp.dot((col <= row).astype(f32), a, precision=lax.Precision.HIGHEST, preferred_element_type=f32)
    acumT_s[...] = lax.dot_general(a, (row <= col).astype(f32), (((0,), (0,)), ((), ())),
                                   precision=lax.Precision.HIGHEST, preferred_element_type=f32)


def _head_cols(h, dt_s, acum_s, acumT_s):
    lane = lax.broadcasted_iota(jnp.int32, (1, LANES), 1)
    oh_l = (lane == h).astype(f32)
    sub = lax.broadcasted_iota(jnp.int32, (LANES, 1), 0)
    oh_s = (sub == h).astype(f32)
    dt_h = jnp.sum(dt_s[...] * oh_l, axis=1, keepdims=True)
    ac_h = jnp.sum(acum_s[...] * oh_l, axis=1, keepdims=True)
    acr_h = jnp.sum(acumT_s[...] * oh_s, axis=0, keepdims=True)
    return oh_l, dt_h, ac_h, acr_h


def ssd_fwd(xs, Bm, Cm, dt_pre, dt_bias, a_log, d_skip, comm=None):
    def body(xs_ref, b_ref, c_ref, dtp_ref, bias_ref, alog_ref, d_ref, y_ref, st_ref, state, dt_s, acum_s, acumT_s):
        c, g = pl.program_id(0), pl.program_id(1)

        @pl.when(g == 0)
        def _():
            _ssd_prep(dtp_ref[...], bias_ref[...], alog_ref[...], dt_s, acum_s, acumT_s)

        row = lax.broadcasted_iota(jnp.int32, (CHUNK, CHUNK), 0)
        col = lax.broadcasted_iota(jnp.int32, (CHUNK, CHUNK), 1)
        causal = col <= row
        Bb, Cb = b_ref[...], c_ref[...]
        cb = lax.dot_general(Cb.astype(MXU), Bb.astype(MXU), (((1,), (1,)), ((), ())), preferred_element_type=f32)
        xs_blk = xs_ref[...]

        @pl.when(c == 0)
        def _():
            for j in range(HPG):
                state[g * HPG + j] = jnp.zeros((NSTATE, HP), f32)

        prevs = [state[g * HPG + j] for j in range(HPG)]
        y_off_all = jnp.dot(Cb.astype(MXU), jnp.concatenate(prevs, axis=1).astype(MXU), preferred_element_type=f32)
        ys, xds, e_ends = [], [], []
        for j in range(HPG):
            oh_l, dt_h, ac_h, acr_h = _head_cols(g * HPG + j, dt_s, acum_s, acumT_s)
            decay = jnp.exp(jnp.where(causal, ac_h - acr_h, -1e30))
            w = (cb * decay).astype(MXU)
            xs_h = xs_blk[:, HP * j:HP * (j + 1)]
            xd = xs_h * dt_h
            y_diag = jnp.dot(w, xd.astype(MXU), preferred_element_type=f32)
            y_off = y_off_all[:, HP * j:HP * (j + 1)] * jnp.exp(ac_h)
            d_h = jnp.sum(d_ref[...] * oh_l, axis=1, keepdims=True)
            ys.append(y_diag + y_off + xs_h * d_h)
            a_end = ac_h[CHUNK - 1:CHUNK, :]
            xds.append(xd * jnp.exp(a_end - ac_h))
            e_ends.append(jnp.exp(a_end))
        s_c = lax.dot_general(Bb.astype(MXU), jnp.concatenate(xds, axis=1).astype(MXU), (((0,), (0,)), ((), ())),
                              preferred_element_type=f32)
        for j in range(HPG):
            st_ref[0, j] = prevs[j]
            state[g * HPG + j] = prevs[j] * e_ends[j] + s_c[:, HP * j:HP * (j + 1)]
        y_ref[...] = jnp.concatenate(ys, axis=1)

    par = pl.BlockSpec((1, LANES), lambda c, g: (0, 0))
    return _call(
        body, [xs, Bm, Cm, dt_pre, dt_bias, a_log, d_skip], comm=comm, name="ssd_fwd", grid=(NCHUNK, GROUPS),
        in_specs=[pl.BlockSpec((CHUNK, HPG * HP), lambda c, g: (c, g)),
                  pl.BlockSpec((CHUNK, NSTATE), lambda c, g: (c, g)),
                  pl.BlockSpec((CHUNK, NSTATE), lambda c, g: (c, g)),
                  pl.BlockSpec((CHUNK, LANES), lambda c, g: (c, 0)), par, par, par],
        out_specs=[pl.BlockSpec((CHUNK, HPG * HP), lambda c, g: (c, g)),
                   pl.BlockSpec((1, HPG, NSTATE, HP), lambda c, g: (c, g, 0, 0))],
        out_shape=[jax.ShapeDtypeStruct((S, INNER), f32), jax.ShapeDtypeStruct((NCHUNK, HEADS, NSTATE, HP), f32)],
        scratch=[pltpu.VMEM((HEADS, NSTATE, HP), f32), pltpu.VMEM((CHUNK, LANES), f32),
                 pltpu.VMEM((CHUNK, LANES), f32), pltpu.VMEM((LANES, CHUNK), f32)],
        sem=("arbitrary", "arbitrary"))


def ssd_bwd(xs, Bm, Cm, dt_pre, dt_bias, a_log, d_skip, states, dy, comm=None):
    rev = lambda c: NCHUNK - 1 - c

    def body(xs_ref, b_ref, c_ref, dtp_ref, bias_ref, alog_ref, d_ref, st_ref, dy_ref,
             dxs_ref, db_ref, dc_ref, ddt_ref, dbias_ref, dalog_ref, dd_ref,
             dstate, dt_s, acum_s, acumT_s, dacum_s, ddt_s, da_s):
        c, g = pl.program_id(0), pl.program_id(1)

        @pl.when(g == 0)
        def _():
            _ssd_prep(dtp_ref[...], bias_ref[...], alog_ref[...], dt_s, acum_s, acumT_s)
            dacum_s[...] = jnp.zeros_like(dacum_s)
            ddt_s[...] = jnp.zeros_like(ddt_s)

        @pl.when((c == 0) & (g == 0))
        def _():
            da_s[...] = jnp.zeros_like(da_s)
            dd_ref[...] = jnp.zeros_like(dd_ref)
            dbias_ref[...] = jnp.zeros_like(dbias_ref)
            dalog_ref[...] = jnp.zeros_like(dalog_ref)

        row = lax.broadcasted_iota(jnp.int32, (CHUNK, CHUNK), 0)
        col = lax.broadcasted_iota(jnp.int32, (CHUNK, CHUNK), 1)
        sub_l = lax.broadcasted_iota(jnp.int32, (CHUNK, 1), 0)
        last = (sub_l == CHUNK - 1).astype(f32)
        nt = (((1,), (1,)), ((), ()))
        tn = (((0,), (0,)), ((), ()))
        Bb, Cb = b_ref[...], c_ref[...]
        Bm_, Cm_ = Bb.astype(MXU), Cb.astype(MXU)
        cb = lax.dot_general(Cm_, Bm_, nt, preferred_element_type=f32)
        bc = lax.dot_general(Bm_, Cm_, nt, preferred_element_type=f32)
        xs_blk, dy_blk = xs_ref[...], dy_ref[...]
        dxs, dB, dC = [], jnp.zeros((CHUNK, NSTATE), f32), jnp.zeros((CHUNK, NSTATE), f32)
        for j in range(HPG):
            h = g * HPG + j
            oh_l, dt_h, ac_h, acr_h = _head_cols(h, dt_s, acum_s, acumT_s)

            @pl.when(c == 0)
            def _():
                dstate[h] = jnp.zeros((NSTATE, HP), f32)

            dnext = dstate[h]
            prev = st_ref[0, j]
            lm = jnp.exp(jnp.where(col <= row, ac_h - acr_h, -1e30))
            lmT = jnp.exp(jnp.where(row <= col, acr_h - ac_h, -1e30))
            xs_h = xs_blk[:, HP * j:HP * (j + 1)]
            dy_h = dy_blk[:, HP * j:HP * (j + 1)]
            xd = xs_h * dt_h
            xdm, dym = xd.astype(MXU), dy_h.astype(MXU)
            ea = jnp.exp(ac_h)
            a_end = ac_h[CHUNK - 1:CHUNK, :]
            e_end = jnp.exp(a_end)
            dte = jnp.exp(a_end - ac_h)
            dnm, pvm = dnext.astype(MXU), prev.astype(MXU)
            bd = jnp.dot(Bm_, dnm, preferred_element_type=f32)
            dxd = jnp.dot((bc * lmT).astype(MXU), dym, preferred_element_type=f32) + dte * bd
            dw = lax.dot_general(dym, xdm, nt, preferred_element_type=f32)
            dwT = lax.dot_general(xdm, dym, nt, preferred_element_type=f32)
            dcb = dw * lm
            dbc = dwT * lmT
            eady = (ea * dy_h).astype(MXU)
            dC = dC + jnp.dot(dcb.astype(MXU), Bm_, preferred_element_type=f32) \
                + lax.dot_general(eady, pvm, nt, preferred_element_type=f32)
            dB = dB + jnp.dot(dbc.astype(MXU), Cm_, preferred_element_type=f32) \
                + dte * lax.dot_general(xdm, dnm, nt, preferred_element_type=f32)
            dstate[h] = lax.dot_general(Cm_, eady, tn, preferred_element_type=f32) + e_end * dnext
            r1 = jnp.sum(dcb * cb, axis=1, keepdims=True)
            r2 = jnp.sum(dbc * bc, axis=1, keepdims=True)
            y_off = jnp.dot(Cm_, pvm, preferred_element_type=f32) * ea
            t3 = jnp.sum(dy_h * y_off, axis=1, keepdims=True)
            t4 = jnp.sum(bd * xd, axis=1, keepdims=True) * dte
            end_extra = jnp.sum(t4, axis=0, keepdims=True) + e_end * jnp.sum(jnp.sum(prev * dnext, axis=1, keepdims=True), axis=0, keepdims=True)
            dacum_h = r1 - r2 + t3 - t4 + last * end_extra
            dacum_s[...] += dacum_h * oh_l
            ddt_s[...] += jnp.sum(dxd * xs_h, axis=1, keepdims=True) * oh_l
            d_h = jnp.sum(d_ref[...] * oh_l, axis=1, keepdims=True)
            dxs.append(dxd * dt_h + dy_h * d_h)
            dd_ref[...] += oh_l * jnp.sum(jnp.sum(dy_h * xs_h, axis=1, keepdims=True), axis=0, keepdims=True)
        dxs_ref[...] = jnp.concatenate(dxs, axis=1)
        db_ref[...] = dB
        dc_ref[...] = dC

        @pl.when(g == GROUPS - 1)
        def _():
            a_row = -jnp.exp(alog_ref[...])
            da = jnp.dot((row <= col).astype(f32), dacum_s[...], precision=lax.Precision.HIGHEST, preferred_element_type=f32)
            da_s[...] += jnp.sum(da * dt_s[...], axis=0, keepdims=True)
            z = dtp_ref[...] + bias_ref[...]
            ddt_pre = (ddt_s[...] + da * a_row) * _sigmoid(z)
            ddt_ref[...] = ddt_pre.astype(ddt_ref.dtype)
            dbias_ref[...] += jnp.sum(ddt_pre, axis=0, keepdims=True)

            @pl.when(c == NCHUNK - 1)
            def _():
                dalog_ref[...] = da_s[...] * a_row

    par = pl.BlockSpec((1, LANES), lambda c, g: (0, 0))
    return _call(
        body, [xs, Bm, Cm, dt_pre, dt_bias, a_log, d_skip, states, dy], comm=comm, name="ssd_bwd", grid=(NCHUNK, GROUPS),
        in_specs=[pl.BlockSpec((CHUNK, HPG * HP), lambda c, g: (rev(c), g)),
                  pl.BlockSpec((CHUNK, NSTATE), lambda c, g: (rev(c), g)),
                  pl.BlockSpec((CHUNK, NSTATE), lambda c, g: (rev(c), g)),
                  pl.BlockSpec((CHUNK, LANES), lambda c, g: (rev(c), 0)), par, par, par,
                  pl.BlockSpec((1, HPG, NSTATE, HP), lambda c, g: (rev(c), g, 0, 0)),
                  pl.BlockSpec((CHUNK, HPG * HP), lambda c, g: (rev(c), g))],
        out_specs=[pl.BlockSpec((CHUNK, HPG * HP), lambda c, g: (rev(c), g)),
                   pl.BlockSpec((CHUNK, NSTATE), lambda c, g: (rev(c), g)),
                   pl.BlockSpec((CHUNK, NSTATE), lambda c, g: (rev(c), g)),
                   pl.BlockSpec((CHUNK, LANES), lambda c, g: (rev(c), 0)), par, par, par],
        out_shape=[jax.ShapeDtypeStruct((S, INNER), f32), jax.ShapeDtypeStruct((S, GROUPS * NSTATE), f32),
                   jax.ShapeDtypeStruct((S, GROUPS * NSTATE), f32), jax.ShapeDtypeStruct((S, LANES), MXU),
                   jax.ShapeDtypeStruct((1, LANES), f32), jax.ShapeDtypeStruct((1, LANES), f32),
                   jax.ShapeDtypeStruct((1, LANES), f32)],
        scratch=[pltpu.VMEM((HEADS, NSTATE, HP), f32), pltpu.VMEM((CHUNK, LANES), f32),
                 pltpu.VMEM((CHUNK, LANES), f32), pltpu.VMEM((LANES, CHUNK), f32),
                 pltpu.VMEM((CHUNK, LANES), f32), pltpu.VMEM((CHUNK, LANES), f32), pltpu.VMEM((1, LANES), f32)],
        sem=("arbitrary", "arbitrary"))


ATT_STACK_FWD, ATT_STACK_BWD = 4, 2


def _attn_kv(kp, kc, vp, vc, hk):
    sl = slice(AH * hk, AH * (hk + 1))
    return (jnp.concatenate([kp[:, sl], kc[:, sl]], axis=0).astype(MXU),
            jnp.concatenate([vp[:, sl], vc[:, sl]], axis=0).astype(MXU))


def _stack_heads(x, heads):
    return jnp.concatenate([x[:, AH * h:AH * (h + 1)] for h in heads], axis=0)


def _attn_block(n, q, kb, sinks, heads):
    rows = len(heads) * WIN
    qi = lax.broadcasted_iota(jnp.int32, (rows, 2 * WIN), 0) & (WIN - 1)
    ki = lax.broadcasted_iota(jnp.int32, (rows, 2 * WIN), 1)
    rel = qi + WIN - ki
    mask = (rel >= 0) & (rel < WIN) & ((ki >= WIN) | (n > 0))
    qg = _stack_heads(q, heads).astype(MXU)
    s = lax.dot_general(qg, kb, (((1,), (1,)), ((), ())), preferred_element_type=f32) * (AH ** -0.5)
    s = jnp.where(mask, s, -1e30)
    sink = jnp.concatenate([jnp.broadcast_to(sinks[:, h:h + 1], (WIN, 1)) for h in heads], axis=0)
    m = jnp.maximum(jnp.max(s, axis=1, keepdims=True), sink)
    p = jnp.exp(s - m)
    ps = jnp.exp(sink - m)
    inv = 1.0 / (jnp.sum(p, axis=1, keepdims=True) + ps)
    return qg, p * inv, ps * inv


def _head_blocks(hk, stack):
    return [list(range(QPK * hk + i, QPK * hk + i + stack)) for i in range(0, QPK, stack)]


def _kv_specs():
    prev = lambda n: (jnp.maximum(n - 1, 0), 0)
    cur = lambda n: (n, 0)
    w = KVH * AH
    return [pl.BlockSpec((WIN, w), prev), pl.BlockSpec((WIN, w), cur), pl.BlockSpec((WIN, w), prev), pl.BlockSpec((WIN, w), cur)]


def attn_fwd(q, k, v, sinks, comm=None):
    def body(q_ref, kp_ref, kc_ref, vp_ref, vc_ref, s_ref, o_ref):
        n = pl.program_id(0)
        q_, kp, kc, vp, vc, sk = q_ref[...], kp_ref[...], kc_ref[...], vp_ref[...], vc_ref[...], s_ref[...]
        outs = []
        for hk in range(KVH):
            kb, vb = _attn_kv(kp, kc, vp, vc, hk)
            for heads in _head_blocks(hk, ATT_STACK_FWD):
                _, pr, _ = _attn_block(n, q_, kb, sk, heads)
                o = jnp.dot(pr.astype(MXU), vb, preferred_element_type=f32)
                outs += [o[WIN * i:WIN * (i + 1)] for i in range(len(heads))]
        o_ref[...] = jnp.concatenate(outs, axis=1)

    return _call(
        body, [q, k, k, v, v, sinks], comm=comm, name="attn_fwd", grid=(NBLK,),
        in_specs=[pl.BlockSpec((WIN, D), lambda n: (n, 0))] + _kv_specs() + [pl.BlockSpec((1, QH), lambda n: (0, 0))],
        out_specs=[pl.BlockSpec((WIN, D), lambda n: (n, 0))],
        out_shape=[jax.ShapeDtypeStruct((S, D), f32)], sem=("parallel",))[0]


def attn_bwd(q, k, v, sinks, dout, comm=None):
    def body(q_ref, kp_ref, kc_ref, vp_ref, vc_ref, s_ref, do_ref, dq_ref, dkp_ref, dkc_ref, dvp_ref, dvc_ref, ds_ref):
        n = pl.program_id(0)

        @pl.when(n == 0)
        def _():
            ds_ref[...] = jnp.zeros_like(ds_ref)

        q_, kp, kc, vp, vc, sk, do = q_ref[...], kp_ref[...], kc_ref[...], vp_ref[...], vc_ref[...], s_ref[...], do_ref[...]
        lane = lax.broadcasted_iota(jnp.int32, (1, QH), 1)
        nt = (((1,), (1,)), ((), ()))
        tn = (((0,), (0,)), ((), ()))
        dqs, dkps, dkcs, dvps, dvcs = [], [], [], [], []
        dsink = jnp.zeros((1, QH), f32)
        for hk in range(KVH):
            kb, vb = _attn_kv(kp, kc, vp, vc, hk)
            dkb, dvb = jnp.zeros((2 * WIN, AH), f32), jnp.zeros((2 * WIN, AH), f32)
            for heads in _head_blocks(hk, ATT_STACK_BWD):
                qg, pr, prs = _attn_block(n, q_, kb, sk, heads)
                dog = _stack_heads(do, heads).astype(MXU)
                dp = lax.dot_general(dog, vb, nt, preferred_element_type=f32)
                dvb = dvb + lax.dot_general(pr.astype(MXU), dog, tn, preferred_element_type=f32)
                delta = jnp.sum(pr * dp, axis=1, keepdims=True)
                ds = (pr * (dp - delta)).astype(MXU)
                dsk = -prs * delta
                for i, h in enumerate(heads):
                    dsink = dsink + jnp.sum(dsk[WIN * i:WIN * (i + 1)], axis=0, keepdims=True) * (lane == h).astype(f32)
                dqg = jnp.dot(ds, kb, preferred_element_type=f32) * (AH ** -0.5)
                dkb = dkb + lax.dot_general(ds, qg, tn, preferred_element_type=f32) * (AH ** -0.5)
                dqs += [dqg[WIN * i:WIN * (i + 1)] for i in range(len(heads))]
            dkps.append(dkb[:WIN])
            dkcs.append(dkb[WIN:])
            dvps.append(dvb[:WIN])
            dvcs.append(dvb[WIN:])
        dq_ref[...] = jnp.concatenate(dqs, axis=1)
        dkp_ref[...] = jnp.concatenate(dkps, axis=1)
        dkc_ref[...] = jnp.concatenate(dkcs, axis=1)
        dvp_ref[...] = jnp.concatenate(dvps, axis=1)
        dvc_ref[...] = jnp.concatenate(dvcs, axis=1)
        ds_ref[...] += dsink

    w = KVH * AH
    blk = lambda width: pl.BlockSpec((WIN, width), lambda n: (n, 0))
    return _call(
        body, [q, k, k, v, v, sinks, dout], comm=comm, name="attn_bwd", grid=(NBLK,),
        in_specs=[blk(D)] + _kv_specs() + [pl.BlockSpec((1, QH), lambda n: (0, 0)), blk(D)],
        out_specs=[blk(D), blk(w), blk(w), blk(w), blk(w), pl.BlockSpec((1, QH), lambda n: (0, 0))],
        out_shape=[jax.ShapeDtypeStruct((S, D), f32)] + [jax.ShapeDtypeStruct((S, w), f32)] * 4 + [jax.ShapeDtypeStruct((1, QH), f32)],
        sem=("arbitrary",))


def kv_bwd(kv, pos, inv_freq, k_norm, dkp, dkc, dvp, dvc):
    w = KVH * AH

    def body(kv_ref, pos_ref, if_ref, g_ref, dkp_ref, dkc_ref, dvp_ref, dvc_ref, o_ref, dg_ref, db_ref):
        n = pl.program_id(0)

        @pl.when(n == 0)
        def _():
            dg_ref[...] = jnp.zeros_like(dg_ref)
            db_ref[...] = jnp.zeros_like(db_ref)

        inside = (n < NBLK - 1).astype(f32)
        dk = dkc_ref[...] + inside * dkp_ref[...]
        dv = dvc_ref[...] + inside * dvp_ref[...]
        cos, sin = _rope_tables(pos_ref[...], if_ref[...])
        dkpre, dg = _headnorm_rope_bwd(kv_ref[...], g_ref[...], cos, sin, dk, KVH)
        dkv = jnp.concatenate([dkpre, dv], axis=1)
        o_ref[...] = dkv.astype(o_ref.dtype)
        dg_ref[...] += dg
        db_ref[...] += jnp.sum(dkv, axis=0, keepdims=True)

    nxt = lambda n: (jnp.minimum(n + 1, NBLK - 1), 0)
    cur = lambda n: (n, 0)
    const = lambda n: (0, 0)
    return pl.pallas_call(
        body, name="kv_bwd", grid=(NBLK,),
        in_specs=[pl.BlockSpec((WIN, w), cur), pl.BlockSpec((WIN, 1), cur), pl.BlockSpec((1, AH // 2), const),
                  pl.BlockSpec((1, AH), const), pl.BlockSpec((WIN, w), nxt), pl.BlockSpec((WIN, w), cur),
                  pl.BlockSpec((WIN, w), nxt), pl.BlockSpec((WIN, w), cur)],
        out_specs=[pl.BlockSpec((WIN, 2 * w), cur), pl.BlockSpec((1, AH), const), pl.BlockSpec((1, 2 * w), const)],
        out_shape=[jax.ShapeDtypeStruct((S, 2 * w), MXU), jax.ShapeDtypeStruct((1, AH), f32), jax.ShapeDtypeStruct((1, 2 * w), f32)],
        compiler_params=_cparams(("arbitrary",)),
    )(kv, pos, inv_freq, k_norm, dkp, dkc, dvp, dvc)


def _adam_math(w, g, m, v):
    m = ADAM_B1 * m + (1.0 - ADAM_B1) * g
    v = ADAM_B2 * v + (1.0 - ADAM_B2) * (g * g)
    m_hat = m / (1.0 - ADAM_B1 ** ADAM_STEP)
    v_hat = v / (1.0 - ADAM_B2 ** ADAM_STEP)
    return -ADAM_LR * (m_hat / (jnp.sqrt(v_hat) + ADAM_EPS) + ADAM_WD * w), m, v


def adamw_layers(name, w, g0, g1, m, v):
    _, R, C = w.shape
    tr = _pick(R, (256, 128, 64, 32, 16, 8))

    def body(w_ref, g0_ref, g1_ref, m_ref, v_ref, g_ref, d_ref, nm_ref, nv_ref):
        g_ = jnp.where(pl.program_id(0) == 0, g0_ref[...], g1_ref[...])
        g_ref[0] = g_
        d_ref[0], nm_ref[0], nv_ref[0] = _adam_math(w_ref[0], g_, m_ref[0], v_ref[0])

    st = pl.BlockSpec((1, tr, C), lambda l, i: (l, i, 0))
    fl = pl.BlockSpec((tr, C), lambda l, i: (i, 0))
    return pl.pallas_call(
        body, name=name, grid=(2, R // tr), in_specs=[st, fl, fl, st, st], out_specs=[st] * 4,
        out_shape=[jax.ShapeDtypeStruct((2, R, C), f32)] * 4, compiler_params=_cparams(("parallel", "parallel")),
    )(w, g0, g1, m, v)


def adamw(name, w, g, m, v):
    R, C = w.shape
    tr = _pick(R, (256, 128, 64, 32, 16, 8))
    tc = C if tr < R or C % 256 else 256

    def body(w_ref, g_ref, m_ref, v_ref, d_ref, nm_ref, nv_ref):
        d_ref[...], nm_ref[...], nv_ref[...] = _adam_math(w_ref[...], g_ref[...], m_ref[...], v_ref[...])

    spec = pl.BlockSpec((tr, tc), lambda i, j: (i, j))
    return pl.pallas_call(
        body, name=name, grid=(R // tr, C // tc), in_specs=[spec] * 4, out_specs=[spec] * 3,
        out_shape=[jax.ShapeDtypeStruct((R, C), f32)] * 3, compiler_params=_cparams(("parallel", "parallel")),
    )(w, g, m, v)


def _me():
    return lax.axis_index("x"), lax.axis_index("y"), lax.axis_index("c")


def gather_comm(xs):
    n = len(xs)

    def parts(x_refs, o_refs, sems):
        send_sems, recv_sems, local_sems = sems
        x, y, c = _me()
        me, sibling = (x, y, c), (x, y, 1 - c)
        chips = [(1 - x, y), (x, 1 - y), (1 - x, 1 - y)]

        def copy(a, k, block, to, src=None):
            dst = o_refs[a].at[4 * block[0] + 2 * block[1] + block[2]]
            return pltpu.make_async_remote_copy(
                src_ref=dst if src is None else src, dst_ref=dst,
                send_sem=send_sems.at[7 * a + k], recv_sem=recv_sems.at[7 * a + k], device_id=to, device_id_type=MESH)

        mine = [pltpu.make_async_copy(x_refs[a], o_refs[a].at[4 * x + 2 * y + c], local_sems.at[a]) for a in range(n)]
        first = []
        for a in range(n):
            first.append(copy(a, 0, me, sibling, src=x_refs[a]))
            first += [copy(a, 1 + j, me, (*chip, c), src=x_refs[a]) for j, chip in enumerate(chips)]
        return copy, mine, first, me, sibling, chips, c

    def start(x_refs, o_refs, sems):
        _, mine, first, *_ = parts(x_refs, o_refs, sems)
        for cp in mine + first:
            cp.start()

    def finish(x_refs, o_refs, sems):
        copy, mine, first, me, sibling, chips, c = parts(x_refs, o_refs, sems)
        passed = []
        for j, chip in enumerate(chips):
            for a in range(n):
                copy(a, 1 + j, (*chip, c), me).wait_recv()
                cp = copy(a, 4 + j, (*chip, c), sibling)
                cp.start()
                passed.append(cp)
        for a in range(n):
            copy(a, 0, sibling, me).wait_recv()
            for j, chip in enumerate(chips):
                copy(a, 4 + j, (*chip, 1 - c), me).wait_recv()
        for cp in first + passed:
            cp.wait_send()
        for cp in mine:
            cp.wait()

    return Comm(xs, [jax.ShapeDtypeStruct((N_DEV,) + a.shape, a.dtype) for a in xs],
                [pltpu.SemaphoreType.DMA((7 * n,)), pltpu.SemaphoreType.DMA((7 * n,)), pltpu.SemaphoreType.DMA((n,))], start, finish)


def run_comm(name, comm):
    _call(lambda: None, [], name=name, grid=(1,), in_specs=[], out_specs=[], out_shape=[], comm=comm)
    return comm.results


def sibling_comm(gs):
    n = len(gs)

    def copies(g_refs, o_refs, sems):
        x, y, c = _me()
        return [pltpu.make_async_remote_copy(
            src_ref=g_refs[a].at[:, 1 - c], dst_ref=o_refs[a], send_sem=sems[0].at[a], recv_sem=sems[1].at[a],
            device_id=(x, y, 1 - c), device_id_type=MESH) for a in range(n)]

    def start(g_refs, o_refs, sems):
        for cp in copies(g_refs, o_refs, sems):
            cp.start()

    def finish(g_refs, o_refs, sems):
        for cp in copies(g_refs, o_refs, sems):
            cp.wait()

    return Comm(gs, [jax.ShapeDtypeStruct((4,) + g.shape[2:], g.dtype) for g in gs],
                [pltpu.SemaphoreType.DMA((n,)), pltpu.SemaphoreType.DMA((n,))], start, finish)


def chip_comm(ts):
    n = len(ts)

    def copies(t_refs, o_refs, sems):
        x, y, c = _me()
        chips = [(1 - x, y), (x, 1 - y), (1 - x, 1 - y)]
        return [pltpu.make_async_remote_copy(
            src_ref=t_refs[a].at[2 * px + py], dst_ref=o_refs[a].at[j],
            send_sem=sems[0].at[3 * a + j], recv_sem=sems[1].at[3 * a + j],
            device_id=(px, py, c), device_id_type=MESH) for j, (px, py) in enumerate(chips) for a in range(n)]

    def start(t_refs, o_refs, sems):
        for cp in copies(t_refs, o_refs, sems):
            cp.start()

    def finish(t_refs, o_refs, sems):
        for cp in copies(t_refs, o_refs, sems):
            cp.wait()

    return Comm(ts, [jax.ShapeDtypeStruct((3,) + t.shape[1:], t.dtype) for t in ts],
                [pltpu.SemaphoreType.DMA((3 * n,)), pltpu.SemaphoreType.DMA((3 * n,))], start, finish)


def _row_tile(rows):
    return _pick(rows, (512, 304, 256, 128))


def pair_add(name, g, r):
    _, _, R, C = g.shape
    tr = _row_tile(R)

    def body(c_ref, g_ref, r_ref, o_ref):
        o_ref[0] = (g_ref[0, 0].astype(f32) + r_ref[0].astype(f32)).astype(o_ref.dtype)

    return pl.pallas_call(
        body, name=name,
        grid_spec=pltpu.PrefetchScalarGridSpec(
            num_scalar_prefetch=1, grid=(4, R // tr),
            in_specs=[pl.BlockSpec((1, 1, tr, C), lambda p, i, c: (p, c[0], i, 0)),
                      pl.BlockSpec((1, tr, C), lambda p, i, c: (p, i, 0))],
            out_specs=pl.BlockSpec((1, tr, C), lambda p, i, c: (p, i, 0))),
        out_shape=jax.ShapeDtypeStruct((4, R, C), g.dtype),
        compiler_params=_cparams(("parallel", "parallel")),
    )(lax.axis_index("c").reshape(1).astype(jnp.int32), g, r)


def final_add(name, t, r):
    _, R, C = t.shape
    tr = _row_tile(R)

    def body(p_ref, t_ref, r_ref, o_ref):
        o_ref[...] = ((t_ref[0].astype(f32) + r_ref[0].astype(f32)) + r_ref[1].astype(f32)) + r_ref[2].astype(f32)

    chip = 2 * lax.axis_index("x") + lax.axis_index("y")
    return pl.pallas_call(
        body, name=name,
        grid_spec=pltpu.PrefetchScalarGridSpec(
            num_scalar_prefetch=1, grid=(R // tr,),
            in_specs=[pl.BlockSpec((1, tr, C), lambda i, p: (p[0], i, 0)),
                      pl.BlockSpec((3, tr, C), lambda i, p: (0, i, 0))],
            out_specs=pl.BlockSpec((tr, C), lambda i, p: (i, 0))),
        out_shape=jax.ShapeDtypeStruct((R, C), f32),
        compiler_params=_cparams(("parallel",)),
    )(chip.reshape(1).astype(jnp.int32), t, r)


class ReduceScatter:
    def __init__(self, tag, keys, grads):
        self.tag, self.keys, self.grads = tag, keys, grads
        self.send = [g.reshape((4, 2, g.shape[0] // N_DEV) + g.shape[1:]) for g in grads]

    def sibling(self):
        self.c1 = sibling_comm(self.send)
        return self.c1

    def chips(self):
        self.pairs = [pair_add(f"rs_pair_add_{self.tag}{i}", g, r) for i, (g, r) in enumerate(zip(self.send, self.c1.results))]
        self.c2 = chip_comm(self.pairs)
        return self.c2

    def sums(self):
        return {k: final_add(f"rs_final_add_{k}", t, r) for k, t, r in zip(self.keys, self.pairs, self.c2.results)}


IN_ROWS = {"z": (0, 2048), "xs": (2048, 4096), "B": (4096, 5120), "C": (5120, 6144)}
IN_COLS = 2 * INNER + 2 * GROUPS * NSTATE + HEADS


def sum_devices(g):
    def body(g_ref, o_ref):
        acc = g_ref[0]
        for i in range(1, N_DEV):
            acc = acc + g_ref[i]
        o_ref[...] = acc

    return pl.pallas_call(body, name="sum_devices", out_shape=jax.ShapeDtypeStruct(g.shape[1:], f32),
                          compiler_params=_cparams())(g)


def _pack(parts, unit, dtype, lead=()):
    flat = jnp.concatenate([p.reshape(lead + (-1,)).astype(dtype) for p in parts], axis=-1)
    n = flat.shape[-1]
    rows = -(-n // (unit * PACK_W)) * unit
    flat = jnp.pad(flat, [(0, 0)] * len(lead) + [(0, rows * PACK_W - n)])
    return flat.reshape(lead + (rows, PACK_W))


def _unpack(buf, shapes, lead=()):
    flat = buf.reshape(lead + (-1,))
    out, off = [], 0
    for shp in shapes:
        n = math.prod(shp)
        out.append(flat[..., off:off + n].reshape(lead + tuple(shp)))
        off += n
    return out


def _pad_lanes(a):
    return jnp.pad(a, [(0, 0)] * (a.ndim - 1) + [(0, LANES - a.shape[-1])])


_NN = (((1,), (0,)), ((), ()))
_NT = (((1,), (1,)), ((), ()))


def _mm(a, b, dims):
    return lax.dot_general(a.astype(MXU), b.astype(MXU), dims, preferred_element_type=f32)


def _ffn_fwd(tag, x, norm_g, w_inT, conv_w, conv_b, mid_comm=None):
    (h,) = rowwise(f"{tag}_norm", lambda x_, g_: (_rms_fwd(x_, g_),), [x], [norm_g], [(D, MXU)])
    ct, nblk = FFN_CT, FFN // FFN_CT

    def body(h_ref, wg_ref, wv_ref, cw_ref, cb_ref, gp_ref, v_ref, a_ref):
        h_ = h_ref[...]
        gp, v_ = _mm(h_, wg_ref[...], _NT), _mm(h_, wv_ref[...], _NT)
        gp_ref[...] = gp
        v_ref[...] = v_
        a_ref[...] = (_silu(_conv(gp, cw_ref[...], cb_ref[...])) * v_).astype(a_ref.dtype)

    col = pl.BlockSpec((S, ct), lambda j: (0, j))
    gate_pre, val, act = _call(
        body, [h, w_inT, w_inT, conv_w, conv_b], comm=mid_comm, name=f"{tag}_in", grid=(nblk,),
        in_specs=[pl.BlockSpec((S, D), lambda j: (0, 0)), pl.BlockSpec((ct, D), lambda j: (j, 0)),
                  pl.BlockSpec((ct, D), lambda j: (nblk + j, 0)), pl.BlockSpec((CONV_F, ct), lambda j: (0, j)),
                  pl.BlockSpec((1, ct), lambda j: (0, j))],
        out_specs=[col, col, col],
        out_shape=[jax.ShapeDtypeStruct((S, FFN), f32), jax.ShapeDtypeStruct((S, FFN), f32), jax.ShapeDtypeStruct((S, FFN), MXU)],
        sem=("parallel",))
    return act, (x, h, gate_pre, val, act)


FFN_CT = 256
CONV_CT = 256


def _proj_conv(name, h, wT, row0, cw, cb):
    C, ct = cw.shape[1], CONV_CT

    def body(h_ref, w_ref, cw_ref, cb_ref, p_ref, c_ref):
        p = _mm(h_ref[...], w_ref[...], _NT)
        p_ref[...] = p
        c_ref[...] = _silu(_conv(p, cw_ref[...], cb_ref[...]))

    col = pl.BlockSpec((S, ct), lambda j: (0, j))
    return _call(
        body, [h, wT, cw, cb], name=name, grid=(C // ct,),
        in_specs=[pl.BlockSpec((S, D), lambda j: (0, 0)), pl.BlockSpec((ct, D), lambda j: (row0 // ct + j, 0)),
                  pl.BlockSpec((CONV_A, ct), lambda j: (0, j)), pl.BlockSpec((1, ct), lambda j: (0, j))],
        out_specs=[col, col], out_shape=[jax.ShapeDtypeStruct((S, C), f32)] * 2, sem=("parallel",))


def _dconv_wgrad(name, pre, dconv, cw, cb, h):
    C, ct = cw.shape[1], CONV_CT

    def body(p_ref, do_ref, cw_ref, cb_ref, h_ref, dp_ref, g_ref, dw_ref, db_ref):
        p_, w_ = p_ref[...], cw_ref[...]
        taps = _taps(p_, CONV_A)
        dx, dw, db = _conv_bwd(p_, w_, do_ref[...] * _dsilu(_conv(p_, w_, cb_ref[...], taps)), taps)
        dpm = dx.astype(MXU)
        dp_ref[...] = dpm
        g_ref[...] = lax.dot_general(dpm, h_ref[...].astype(MXU), (((0,), (0,)), ((), ())),
                                     preferred_element_type=f32).astype(g_ref.dtype)
        dw_ref[...] = dw
        db_ref[...] = db

    col = pl.BlockSpec((S, ct), lambda j: (0, j))
    return _call(
        body, [pre, dconv, cw, cb, h], name=name, grid=(C // ct,),
        in_specs=[col, col, pl.BlockSpec((CONV_A, ct), lambda j: (0, j)), pl.BlockSpec((1, ct), lambda j: (0, j)),
                  pl.BlockSpec((S, D), lambda j: (0, 0))],
        out_specs=[col, pl.BlockSpec((ct, D), lambda j: (j, 0)), pl.BlockSpec((CONV_A, ct), lambda j: (0, j)),
                   pl.BlockSpec((1, ct), lambda j: (0, j))],
        out_shape=[jax.ShapeDtypeStruct((S, C), MXU), jax.ShapeDtypeStruct((C, D), MXU),
                   jax.ShapeDtypeStruct((CONV_A, C), f32), jax.ShapeDtypeStruct((1, C), f32)],
        sem=("parallel",))


def _ffn_mid_bwd(name, dout, w_down, gate_pre, val, conv_w, conv_b, comm=None):
    ct = FFN_CT

    def body(do_ref, wd_ref, gp_ref, v_ref, w_ref, b_ref, dgv_ref, dw_ref, db_ref, dob_s):
        @pl.when(pl.program_id(0) == 0)
        def _():
            dob_s[...] = do_ref[...].astype(MXU)

        da = _mm(dob_s[...], wd_ref[...], _NT)
        gp, v_, w_ = gp_ref[...], v_ref[...], w_ref[...]
        taps = _taps(gp, CONV_F)
        gate = _conv(gp, w_, b_ref[...], taps)
        sg = _sigmoid(gate)
        dgp, dw, db = _conv_bwd(gp, w_, da * v_ * (sg * (1.0 + gate * (1.0 - sg))), taps)
        dgv_ref[0] = dgp.astype(dgv_ref.dtype)
        dgv_ref[1] = (da * (gate * sg)).astype(dgv_ref.dtype)
        dw_ref[...] = dw
        db_ref[...] = db

    col = pl.BlockSpec((S, ct), lambda j: (0, j))
    return _call(
        body, [dout, w_down, gate_pre, val, conv_w, conv_b], comm=comm, name=name, grid=(FFN // ct,),
        in_specs=[pl.BlockSpec((S, D), lambda j: (0, 0)), pl.BlockSpec((ct, D), lambda j: (j, 0)), col, col,
                  pl.BlockSpec((CONV_F, ct), lambda j: (0, j)), pl.BlockSpec((1, ct), lambda j: (0, j))],
        out_specs=[pl.BlockSpec((2, S, ct), lambda j: (0, 0, j)), pl.BlockSpec((CONV_F, ct), lambda j: (0, j)),
                   pl.BlockSpec((1, ct), lambda j: (0, j))],
        out_shape=[jax.ShapeDtypeStruct((2, S, FFN), MXU), jax.ShapeDtypeStruct((CONV_F, FFN), f32), jax.ShapeDtypeStruct((1, FFN), f32)],
        scratch=[pltpu.VMEM((S, D), MXU)], sem=("arbitrary",))


def _ffn_bwd(tag, layer, saved, norm_g, w_inT, conv_w, conv_b, w_down, dout, mid_comm=None):
    x, h, gate_pre, val, act = saved
    g_down = matmul(f"{tag}_wdown", act, dout, "tn", out_dtype=MXU)
    dgv, g_cw, g_cb = _ffn_mid_bwd(f"{tag}_dmid", dout, w_down, gate_pre, val, conv_w, conv_b, comm=mid_comm)
    g_inT = matmul_tn_stacked(f"{tag}_win", dgv, h, MXU)

    def din_fn(dg_, dv_, x_, do_, g_, wT):
        dx, dg = _rms_bwd(x_, g_, _mm(dg_, wT[:FFN], _NN) + _mm(dv_, wT[FFN:], _NN))
        return do_ + dx, dg

    rs = ReduceScatter(tag, (f"f_inT{layer}", f"f_down{layer}"), [g_inT, g_down])
    dx, g_norm = rowwise(f"{tag}_din", din_fn, [(dgv, 0), (dgv, 1), x, dout], [norm_g, w_inT], [(D, f32)], [(1, D)],
                         comm=rs.sibling())
    return dx, {f"f_norm{layer}": g_norm, f"f_conv_w{layer}": g_cw, f"f_conv_b{layer}": g_cb}, rs


def _land(W, keys, comm):
    for k, g in zip(keys, comm.results):
        W[k] = g.reshape(-1, g.shape[2])


def _local_step(x, pos, tgt, W, shards):
    G = {}
    gather = lambda *keys: gather_comm([shards[k] for k in keys])
    inv_freq = (ROPE_THETA ** (-jnp.arange(AH // 2, dtype=f32) / (AH // 2))).reshape(1, AH // 2)

    def in_fn(x_, g_, wT, wdtT):
        h_ = _rms_fwd(x_, g_).astype(MXU)
        return h_, _mm(h_, wT[slice(*IN_ROWS["z"])], _NT), _mm(h_, wdtT, _NT)

    h0, z, dt_pre = rowwise("a_in", in_fn, [x], [W["a_norm"], W["inT"], W["in_dtT"]], [(D, MXU), (INNER, f32), (LANES, f32)])
    pre, conv = {}, {}
    for k in ("xs", "B", "C"):
        pre[k], conv[k] = _proj_conv(f"a_in_{k}", h0, W["inT"], IN_ROWS[k][0], W[f"cw_{k}"], W[f"cb_{k}"])
    c = gather("a_out", "f_inT0", "f_down0")
    y, states = ssd_fwd(conv["xs"], conv["B"], conv["C"], dt_pre, W["dt_bias"], W["A_log"], W["D"], comm=c)
    _land(W, ("a_out", "f_inT0", "f_down0"), c)

    def gate_norm(y_, z_, g_):
        yg = y_ * _silu(z_)
        w = INNER // GROUPS
        return (jnp.concatenate([_rms_fwd(yg[:, w * i:w * (i + 1)], g_[:, w * i:w * (i + 1)]) for i in range(GROUPS)], axis=1),)

    def out_fn(y_, z_, x_, g_, w_):
        (gn_,) = gate_norm(y_, z_, g_)
        gn_ = gn_.astype(MXU)
        return gn_, x_ + _mm(gn_, w_, _NN)

    gn, x1 = rowwise("a_out", out_fn, [y, z, x], [W["a_gnorm"], W["a_out"]], [(INNER, MXU), (D, f32)])

    c = gather("w_kv", "w_q", "w_o")
    act0, ffn0 = _ffn_fwd("f0", x1, W["f_norm0"], W["f_inT0"], W["f_cw0"], W["f_cb0"], mid_comm=c)
    _land(W, ("w_kv", "w_q", "w_o"), c)
    x2 = matmul("f0_down", act0, W["f_down0"], "nn", residual=x1)

    def qkv_fn(x_, gk, gb, wkv, bkv, wq, bq):
        kvn_, h2_ = _rms_fwd(x_, gk).astype(MXU), _rms_fwd(x_, gb).astype(MXU)
        return kvn_, h2_, _mm(kvn_, wkv, _NN) + bkv, _mm(h2_, wq, _NN) + bq

    kw = KVH * AH
    kvn, h2, kv, q_pre = rowwise("qkv_proj", qkv_fn, [x2], [W["kv_norm"], W["b_norm"], W["w_kv"], W["b_kv"], W["w_q"], W["b_q"]],
                                 [(D, MXU), (D, MXU), (2 * kw, f32), (D, f32)])

    def k_fwd(kv_, pos_, if_, g_):
        cos, sin = _rope_tables(pos_, if_)
        return _headnorm_rope_fwd(kv_[:, :kw], g_, cos, sin, KVH), kv_[:, kw:]

    k_rot, v_val = rowwise("k_rope", k_fwd, [kv, pos], [inv_freq, W["k_norm"]], [(kw, f32), (kw, f32)])

    def q_fwd(q_, pos_, if_, g_):
        cos, sin = _rope_tables(pos_, if_)
        return (_headnorm_rope_fwd(q_, g_, cos, sin, QH),)

    c = gather("f_down1")
    (q,) = rowwise("q_rope", q_fwd, [q_pre, pos], [inv_freq, W["q_norm"]], [(D, f32)], comm=c)
    _land(W, ("f_down1",), c)
    c = gather("f_inT1")
    att = attn_fwd(q, k_rot, v_val, W["sinks"], comm=c)
    _land(W, ("f_inT1",), c)
    x3 = matmul("o_proj", att, W["w_o"], "nn", bias=W["b_o"], residual=x2)

    act1, ffn1 = _ffn_fwd("f1", x3, W["f_norm1"], W["f_inT1"], W["f_cw1"], W["f_cb1"])

    def loss_fn(a_, x_, t_, w_):
        diff = x_ + _mm(a_, w_, _NN) - t_
        rows = jnp.sum(diff * diff, axis=1, keepdims=True) * (0.5 / D)
        return diff * (1.0 / D), jnp.sum(rows, axis=0, keepdims=True)

    dx4, loss = rowwise("f1_down_loss", loss_fn, [act1, x3, tgt], [W["f_down1"]], [(D, f32)], [(1, 1)])

    dx3, g, rs_f1 = _ffn_bwd("f1", 1, ffn1, W["f_norm1"], W["f_inT1"], W["f_cw1"], W["f_cb1"], W["f_down1"], dx4)
    G.update(g)

    datt = matmul("o_dproj", dx3, W["w_o"], "nt")
    g_wo = matmul("o_wproj", att, dx3, "tn", out_dtype=MXU)
    dq, dkp, dkc, dvp, dvc, G["sinks"] = attn_bwd(q, k_rot, v_val, W["sinks"], datt, comm=rs_f1.chips())

    def q_bwd(q_, pos_, dq_, dx_, if_, g_):
        cos, sin = _rope_tables(pos_, if_)
        dqp, dg = _headnorm_rope_bwd(q_, g_, cos, sin, dq_, QH)
        return dqp, dg, jnp.sum(dqp, axis=0, keepdims=True), jnp.sum(dx_, axis=0, keepdims=True)

    dq_pre, G["q_norm"], G["b_q"], G["b_o"] = rowwise("q_drope", q_bwd, [q_pre, pos, dq, dx3], [inv_freq, W["q_norm"]],
                                                      [(D, MXU)], [(1, AH), (1, D), (1, D)])
    g_wq = matmul("q_wproj", h2, dq_pre, "tn", out_dtype=MXU)
    dkv, G["k_norm"], G["b_kv"] = kv_bwd(kv, pos, inv_freq, W["k_norm"], dkp, dkc, dvp, dvc)
    g_wkv = matmul("kv_wproj", kvn, dkv, "tn", out_dtype=MXU)
    rs_att = ReduceScatter("att", ("w_kv", "w_q", "w_o"), [g_wkv, g_wq, g_wo])

    def x2_bwd(x_, dq_, dkv_, dx_, gb_, gk_, wq, wkv):
        d1, dgb = _rms_bwd(x_, gb_, _mm(dq_, wq, _NT))
        d2, dgk = _rms_bwd(x_, gk_, _mm(dkv_, wkv, _NT))
        return dx_ + d1 + d2, dgb, dgk

    dx2, G["b_norm"], G["kv_norm"] = rowwise("qkv_dproj", x2_bwd, [x2, dq_pre, dkv, dx3],
                                             [W["b_norm"], W["kv_norm"], W["w_q"], W["w_kv"]],
                                             [(D, f32)], [(1, D), (1, D)], comm=rs_att.sibling())

    dx1, g, rs_f0 = _ffn_bwd("f0", 0, ffn0, W["f_norm0"], W["f_inT0"], W["f_cw0"], W["f_cb0"], W["f_down0"], dx2,
                             mid_comm=rs_att.chips())
    G.update(g)

    rs_out = ReduceScatter("a_out", ("a_out",), [matmul("a_wout", gn, dx1, "tn", out_dtype=MXU)])

    def gate_norm_bwd(y_, z_, dx_, g_, w_out):
        dgn_ = _mm(dx_, w_out, _NT)
        w = INNER // GROUPS
        sg = _sigmoid(z_)
        sz = z_ * sg
        yg = y_ * sz
        parts, dgs = [], []
        for i in range(GROUPS):
            dseg, dg = _rms_bwd(yg[:, w * i:w * (i + 1)], g_[:, w * i:w * (i + 1)], dgn_[:, w * i:w * (i + 1)])
            parts.append(dseg)
            dgs.append(dg)
        dyg = jnp.concatenate(parts, axis=1)
        return dyg * sz, dyg * y_ * (sg * (1.0 + z_ * (1.0 - sg))), jnp.concatenate(dgs, axis=1)

    dy, dz, G["a_gnorm"] = rowwise("a_dout", gate_norm_bwd, [y, z, dx1], [W["a_gnorm"], W["a_out"]],
                                   [(INNER, f32), (INNER, MXU)], [(1, INNER)], comm=rs_out.sibling())
    dconv = {}
    dconv["xs"], dconv["B"], dconv["C"], ddt_pre, G["dt_bias"], G["A_log"], G["D"] = ssd_bwd(
        conv["xs"], conv["B"], conv["C"], dt_pre, W["dt_bias"], W["A_log"], W["D"], states, dy,
        comm=merge_comms([rs_f0.chips(), rs_out.chips()]))

    g_in, dpre = [matmul("a_win_z", dz, h0, "tn", out_dtype=MXU)], {}
    for k in ("xs", "B", "C"):
        dpre[k], g_k, G[f"cw_{k}"], G[f"cb_{k}"] = _dconv_wgrad(f"a_dconv_{k}", pre[k], dconv[k], W[f"cw_{k}"], W[f"cb_{k}"], h0)
        g_in.append(g_k)
    g_in.append(matmul("a_win_dt", ddt_pre, h0, "tn", out_dtype=MXU)[:HEADS])
    rs_in = ReduceScatter("a_in", ("inT",), [jnp.concatenate(g_in, axis=0)])
    run_comm("rs_in_sibling", rs_in.sibling())

    def x0_bwd(dz_, dxs_, db_, dc_, ddt_, x_, do_, g_, wT, wdtT):
        parts = zip((dz_, dxs_, db_, dc_), IN_ROWS.values())
        dh = sum(_mm(d_, wT[a:b], _NN) for d_, (a, b) in parts) + _mm(ddt_, wdtT, _NN)
        dx, dg = _rms_bwd(x_, g_, dh)
        return do_ + dx, dg

    dx, G["a_norm"] = rowwise("a_din", x0_bwd, [dz, dpre["xs"], dpre["B"], dpre["C"], ddt_pre, x, dx1],
                              [W["a_norm"], W["inT"], W["in_dtT"]], [(D, f32)], [(1, D)], comm=rs_in.chips())
    return loss, dx, G, [rs_f1, rs_att, rs_f0, rs_out, rs_in]


ROW_KEYS = ("inT", "a_out", "f_inT0", "f_down0", "w_kv", "w_q", "w_o", "f_inT1", "f_down1")


def _row_blocks(src):
    return {"inT": src["a_in_proj"][0].T, "a_out": src["a_out_proj"][0], "w_kv": src["w_kv"], "w_q": src["w_q"][0],
            "w_o": src["w_o"][0], "f_inT0": src["f_w_in"][0].T, "f_inT1": src["f_w_in"][1].T,
            "f_down0": src["f_w_down"][0], "f_down1": src["f_w_down"][1]}


def _from_row_blocks(rb):
    out = {"a_in_proj": rb["inT"].T[None], "a_out_proj": rb["a_out"][None], "w_kv": rb["w_kv"], "w_q": rb["w_q"][None],
           "w_o": rb["w_o"][None]}
    if "f_inT0" in rb:
        out["f_w_in"] = jnp.stack([rb["f_inT0"].T, rb["f_inT1"].T])
        out["f_w_down"] = jnp.stack([rb["f_down0"], rb["f_down1"]])
    return out


SMALL_SHARDED = ("a_norm", "a_conv_w", "a_conv_b", "a_gnorm", "f_conv_w")
REPLICATED = ("a_dt_bias", "a_A_log", "a_D", "kv_norm", "b_kv", "k_norm", "b_norm", "b_q", "q_norm", "sinks", "b_o",
              "f_norm", "f_conv_b")
ORDER = ("a_norm", "a_in_proj", "a_conv_w", "a_conv_b", "a_dt_bias", "a_A_log", "a_D", "a_gnorm", "a_out_proj", "kv_norm",
         "w_kv", "b_kv", "k_norm", "b_norm", "w_q", "b_q", "q_norm", "sinks", "w_o", "b_o", "f_norm", "f_w_in",
         "f_conv_w", "f_conv_b", "f_w_down")


def _gathered_to_whole(name, g):
    if name == "a_conv_w":
        return jnp.moveaxis(g[:, 0], 0, 1).reshape(g.shape[2], -1)
    if name in ("a_norm", "a_conv_b", "a_gnorm"):
        return g[:, 0].reshape(1, -1)
    if name == "f_conv_w":
        return jnp.moveaxis(g, 0, 2).reshape(g.shape[1], g.shape[2], -1)
    raise ValueError(name)


def _whole_to_shards(name, w):
    if name == "a_conv_w":
        return jnp.moveaxis(w.reshape(w.shape[0], N_DEV, -1), 1, 0)[:, None]
    if name in ("a_norm", "a_conv_b", "a_gnorm"):
        return w.reshape(N_DEV, 1, -1)
    if name == "f_conv_w":
        return jnp.moveaxis(w.reshape(w.shape[0], w.shape[1], N_DEV, -1), 2, 0)
    raise ValueError(name)


def _small_weights(whole):
    W = {}
    cw, cb = whole["a_conv_w"], whole["a_conv_b"]
    o = 0
    for k, n in (("xs", INNER), ("B", GROUPS * NSTATE), ("C", GROUPS * NSTATE)):
        W[f"cw_{k}"], W[f"cb_{k}"] = cw[:, o:o + n], cb[:, o:o + n]
        o += n
    W["a_norm"], W["a_gnorm"] = whole["a_norm"], whole["a_gnorm"]
    W["dt_bias"], W["A_log"], W["D"] = (_pad_lanes(whole[k]) for k in ("a_dt_bias", "a_A_log", "a_D"))
    W["kv_norm"], W["b_kv"], W["k_norm"] = whole["kv_norm"].reshape(1, -1), whole["b_kv"].reshape(1, -1), whole["k_norm"].reshape(1, -1)
    for k in ("b_norm", "b_q", "q_norm", "sinks", "b_o"):
        W[k] = whole[k]
    for i in range(2):
        W[f"f_norm{i}"] = whole["f_norm"][i:i + 1]
        W[f"f_cw{i}"], W[f"f_cb{i}"] = whole["f_conv_w"][i], whole["f_conv_b"][i:i + 1]
    return W


def _small_grads(G, shapes):
    nh = HEADS
    out = {
        "a_conv_w": jnp.concatenate([G["cw_xs"], G["cw_B"], G["cw_C"]], axis=1),
        "a_conv_b": jnp.concatenate([G["cb_xs"], G["cb_B"], G["cb_C"]], axis=1),
        "a_norm": G["a_norm"], "a_gnorm": G["a_gnorm"],
        "a_dt_bias": G["dt_bias"][:, :nh], "a_A_log": G["A_log"][:, :nh], "a_D": G["D"][:, :nh],
        "kv_norm": G["kv_norm"], "b_kv": G["b_kv"], "k_norm": G["k_norm"], "b_norm": G["b_norm"],
        "b_q": G["b_q"], "q_norm": G["q_norm"], "sinks": G["sinks"], "b_o": G["b_o"],
        "f_norm": jnp.concatenate([G["f_norm0"], G["f_norm1"]], axis=0),
        "f_conv_w": jnp.stack([G["f_conv_w0"], G["f_conv_w1"]]),
        "f_conv_b": jnp.concatenate([G["f_conv_b0"], G["f_conv_b1"]], axis=0),
    }
    return {k: val.reshape(shapes[k]) if k in shapes else val for k, val in out.items()}


def kernel(x, positions, a_norm, a_in_proj, a_conv_w, a_conv_b, a_dt_bias, a_A_log, a_D, a_gnorm, a_out_proj, kv_norm, w_kv, b_kv, k_norm, b_norm, w_q, b_q, q_norm, sinks, w_o, b_o, f_norm, f_w_in, f_conv_w, f_conv_b, f_w_down, loss_target, m_a_norm, m_a_in_proj, m_a_conv_w, m_a_conv_b, m_a_dt_bias, m_a_A_log, m_a_D, m_a_gnorm, m_a_out_proj, m_kv_norm, m_w_kv, m_b_kv, m_k_norm, m_b_norm, m_w_q, m_b_q, m_q_norm, m_sinks, m_w_o, m_b_o, m_f_norm, m_f_w_in, m_f_conv_w, m_f_conv_b, m_f_w_down, v_a_norm, v_a_in_proj, v_a_conv_w, v_a_conv_b, v_a_dt_bias, v_a_A_log, v_a_D, v_a_gnorm, v_a_out_proj, v_kv_norm, v_w_kv, v_b_kv, v_k_norm, v_b_norm, v_w_q, v_b_q, v_q_norm, v_sinks, v_w_o, v_b_o, v_f_norm, v_f_w_in, v_f_conv_w, v_f_conv_b, v_f_w_down):
    given = dict(locals())
    w_in = {n: given[n] for n in ORDER}
    m_in = {n: given["m_" + n] for n in ORDER}
    v_in = {n: given["v_" + n] for n in ORDER}
    dev = 4 * lax.axis_index("x") + 2 * lax.axis_index("y") + lax.axis_index("c")

    w2, m2, v2 = _row_blocks(w_in), _row_blocks(m_in), _row_blocks(v_in)
    small_pack = _pack([w_in[n] for n in SMALL_SHARDED], 8, f32)
    shards = {k: w2[k].astype(MXU) for k in ROW_KEYS}
    in_all, small_all = run_comm("ag_head", gather_comm([shards["inT"], small_pack]))
    whole = {n: w_in[n] for n in REPLICATED}
    for n, g in zip(SMALL_SHARDED, _unpack(small_all, [w_in[n].shape for n in SMALL_SHARDED], lead=(N_DEV,))):
        whole[n] = _gathered_to_whole(n, g)
    W = _small_weights(whole)
    W["inT"] = in_all.reshape(-1, D)
    W["in_dtT"] = jnp.pad(W["inT"][IN_COLS - HEADS:], ((0, LANES - HEADS), (0, 0)))

    loss, dx, G, scatters = _local_step(x[0], positions.reshape(S, 1).astype(f32), loss_target[0], W, shards)
    grads = _small_grads(G, {n: whole[n].shape for n in REPLICATED})

    small_names = SMALL_SHARDED + REPLICATED
    small_part = _pack([grads[n] for n in small_names], 8, f32)
    small_gather = gather_comm([small_part])
    run_comm("ag_small_grads", small_gather)
    sums = {}
    for rs in scatters:
        sums.update(rs.sums())
    single = tuple(k for k in ROW_KEYS if not k.startswith("f_"))
    g_out = _from_row_blocks({k: sums[k] for k in single})
    small_sum = sum_devices(small_gather.results[0])
    for n, g in zip(small_names, _unpack(small_sum, [grads[n].shape for n in small_names])):
        if n in SMALL_SHARDED:
            g_out[n] = lax.dynamic_index_in_dim(_whole_to_shards(n, g), dev, axis=0, keepdims=False)
        else:
            g_out[n] = g.reshape(w_in[n].shape)

    stepped = {k: adamw(f"adamw_{k}", w2[k], sums[k], m2[k], v2[k]) for k in single}
    delta, new_m, new_v = (_from_row_blocks({k: stepped[k][i] for k in single}) for i in range(3))
    for n, key, lay in (("f_w_in", "f_inT", lambda a: jnp.swapaxes(a, 1, 2)), ("f_w_down", "f_down", lambda a: a)):
        res = adamw_layers(f"adamw_{n}", lay(w_in[n]), sums[key + "0"], sums[key + "1"], lay(m_in[n]), lay(v_in[n]))
        g_out[n], delta[n], new_m[n], new_v[n] = (lay(a) for a in res)
    packs = [_pack([src[n] for n in small_names], 8, f32) for src in (w_in, g_out, m_in, v_in)]
    outs = adamw("adamw_small", *packs)
    for dst, buf in zip((delta, new_m, new_v), outs):
        for n, a in zip(small_names, _unpack(buf, [w_in[n].shape for n in small_names])):
            dst[n] = a

    loss_all = lax.psum(loss[0, 0], AXES)
    return (loss_all, dx[None], *[g_out[n] for n in ORDER], *[delta[n] for n in ORDER],
            *[new_m[n] for n in ORDER], *[new_v[n] for n in ORDER])
```

```python
import functools
import math

import jax
import jax.numpy as jnp
from jax import lax
from jax.experimental import pallas as pl
from jax.experimental.pallas import tpu as pltpu

f32 = jnp.float32
bf16 = jnp.bfloat16
MXU = bf16

N_DEV = 8
S = 2048
D = 1024
EPS = 1e-5
INNER = 2048
HEADS = 32
HP = 64
GROUPS = 8
HPG = HEADS // GROUPS
NSTATE = 128
CONV_A = 4
CHUNK = 256
NCHUNK = S // CHUNK
AH = 64
QH = 16
KVH = 4
QPK = QH // KVH
WIN = 128
NBLK = S // WIN
ROPE_THETA = 10000.0
FFN = 2816
CONV_F = 3
LANES = 128
PACK_W = 1024
VMEM_LIMIT = 56 * 1024 * 1024

ADAM_LR, ADAM_B1, ADAM_B2, ADAM_EPS, ADAM_WD, ADAM_STEP = 0.001, 0.9, 0.999, 1e-08, 0.01, 10

MESH = pl.DeviceIdType.MESH
AXES = ("x", "y", "c")


def _cparams(sem=None):
    return pltpu.CompilerParams(dimension_semantics=sem, vmem_limit_bytes=VMEM_LIMIT)


def _pick(n, cands):
    for c in cands:
        if n % c == 0:
            return c
    return n


class Comm:
    def __init__(self, ins, out_shapes, sems, start, finish):
        self.ins, self.out_shapes, self.sems, self.start, self.finish = list(ins), list(out_shapes), list(sems), start, finish
        self.results, self.children = None, ()

    def set_results(self, res):
        self.results, o = list(res), 0
        for ch in self.children:
            ch.set_results(res[o:o + len(ch.out_shapes)])
            o += len(ch.out_shapes)


def merge_comms(comms):
    def each(fn_name, ins, outs, sems):
        i = o = s = 0
        for c in comms:
            getattr(c, fn_name)(ins[i:i + len(c.ins)], outs[o:o + len(c.out_shapes)], sems[s:s + len(c.sems)])
            i, o, s = i + len(c.ins), o + len(c.out_shapes), s + len(c.sems)

    merged = Comm([a for c in comms for a in c.ins], [a for c in comms for a in c.out_shapes], [a for c in comms for a in c.sems],
                  functools.partial(each, "start"), functools.partial(each, "finish"))
    merged.children = tuple(comms)
    return merged


def _call(body, args, *, name, grid, in_specs, out_specs, out_shape, scratch=(), sem=None, comm=None):
    if comm is None:
        return pl.pallas_call(body, name=name, grid=grid, in_specs=list(in_specs), out_specs=list(out_specs),
                              out_shape=list(out_shape), scratch_shapes=list(scratch), compiler_params=_cparams(sem))(*args)
    n_in, n_out, n_scr, c_in, c_out = len(in_specs), len(out_shape), len(scratch), len(comm.ins), len(comm.out_shapes)
    any_spec = pl.BlockSpec(memory_space=pl.ANY)

    def outer(*refs):
        ins, c_ins = refs[:n_in], refs[n_in:n_in + c_in]
        o = n_in + c_in
        outs, c_outs = refs[o:o + n_out], refs[o + n_out:o + n_out + c_out]
        o += n_out + c_out
        scr, c_sems = refs[o:o + n_scr], refs[o + n_scr:]
        ids = [pl.program_id(i) for i in range(len(grid))]
        first = functools.reduce(jnp.logical_and, [i == 0 for i in ids])
        last = functools.reduce(jnp.logical_and, [i == g - 1 for i, g in zip(ids, grid)])

        @pl.when(first)
        def _():
            comm.start(c_ins, c_outs, c_sems)

        body(*ins, *outs, *scr)

        @pl.when(last)
        def _():
            comm.finish(c_ins, c_outs, c_sems)

    res = pl.pallas_call(
        outer, name=name, grid=grid, in_specs=list(in_specs) + [any_spec] * c_in,
        out_specs=list(out_specs) + [any_spec] * c_out, out_shape=list(out_shape) + comm.out_shapes,
        scratch_shapes=list(scratch) + comm.sems, compiler_params=_cparams(("arbitrary",) * len(grid)),
    )(*args, *comm.ins)
    comm.set_results(res[n_out:])
    return res[:n_out]


def matmul(name, a, b, mode, out_dtype=f32, bias=None, residual=None):
    if mode == "nn":
        (M, K), (K2, N) = a.shape, b.shape
    elif mode == "nt":
        (M, K), (N, K2) = a.shape, b.shape
    else:
        (K, M), (K2, N) = a.shape, b.shape
    assert K == K2, (name, a.shape, b.shape)
    if mode == "tn":
        tm, tn = M, _pick(N, (512, 256, 128) if M <= 1024 else (256, 128))
        a_spec = pl.BlockSpec((K, M), lambda j: (0, 0))
        b_spec = pl.BlockSpec((K, tn), lambda j: (0, j))
        dims = (((0,), (0,)), ((), ()))
        grid, o_map, row_map = (N // tn,), (lambda j: (0, j)), (lambda j: (0, j))
    else:
        tm, tn = (256 if N >= 2048 else 512), N
        a_spec = pl.BlockSpec((tm, K), lambda i: (i, 0))
        b_spec = pl.BlockSpec(b.shape, lambda i: (0, 0))
        dims = (((1,), (0,)), ((), ())) if mode == "nn" else (((1,), (1,)), ((), ()))
        grid, o_map, row_map = (M // tm,), (lambda i: (i, 0)), (lambda i: (0, 0))
    ins, in_specs = [a, b], [a_spec, b_spec]
    if bias is not None:
        ins.append(bias)
        in_specs.append(pl.BlockSpec((1, tn), row_map))
    if residual is not None:
        ins.append(residual)
        in_specs.append(pl.BlockSpec((tm, tn), o_map))
    has_bias, has_res = bias is not None, residual is not None

    def body(a_ref, b_ref, *rest):
        rest = list(rest)
        bias_ref = rest.pop(0) if has_bias else None
        res_ref = rest.pop(0) if has_res else None
        (o_ref,) = rest
        r = lax.dot_general(a_ref[...].astype(MXU), b_ref[...].astype(MXU), dims, preferred_element_type=f32)
        if has_bias:
            r = r + bias_ref[...]
        if has_res:
            r = r + res_ref[...]
        o_ref[...] = r.astype(out_dtype)

    return pl.pallas_call(
        body, name=name, grid=grid, in_specs=in_specs,
        out_specs=pl.BlockSpec((tm, tn), o_map),
        out_shape=jax.ShapeDtypeStruct((M, N), out_dtype),
        compiler_params=_cparams(("parallel",)),
    )(*ins)


def matmul_tn_stacked(name, a, b, out_dtype):
    R, K, M = a.shape
    N = b.shape[1]
    tn = _pick(N, (256, 128))

    def body(a_ref, b_ref, o_ref):
        o_ref[0] = lax.dot_general(a_ref[0].astype(MXU), b_ref[...].astype(MXU), (((0,), (0,)), ((), ())),
                                   preferred_element_type=f32).astype(out_dtype)

    out = pl.pallas_call(
        body, name=name, grid=(R, N // tn),
        in_specs=[pl.BlockSpec((1, K, M), lambda r, j: (r, 0, 0)), pl.BlockSpec((K, tn), lambda r, j: (0, j))],
        out_specs=pl.BlockSpec((1, M, tn), lambda r, j: (r, 0, j)),
        out_shape=jax.ShapeDtypeStruct((R, M, N), out_dtype),
        compiler_params=_cparams(("parallel", "parallel")),
    )(a, b)
    return out.reshape(R * M, N)


def rowwise(name, fn, rows, pars, outs, accs=(), tile=256, comm=None):
    n_in, n_out = len(rows) + len(pars), len(outs)
    in_specs = [pl.BlockSpec((None, tile, r[0].shape[2]), functools.partial(lambda i, lead: (lead, i, 0), lead=r[1]))
                if isinstance(r, tuple) else pl.BlockSpec((tile, r.shape[1]), lambda i: (i, 0)) for r in rows]
    rows = [r[0] if isinstance(r, tuple) else r for r in rows]
    in_specs += [pl.BlockSpec(p.shape, lambda i: (0, 0)) for p in pars]
    out_specs = [pl.BlockSpec((tile, c), lambda i: (i, 0)) for c, _ in outs]
    out_specs += [pl.BlockSpec(shp, lambda i: (0, 0)) for shp in accs]
    out_shape = [jax.ShapeDtypeStruct((S, c), dt) for c, dt in outs]
    out_shape += [jax.ShapeDtypeStruct(shp, f32) for shp in accs]

    def body(*refs):
        res = fn(*[r[...] for r in refs[:n_in]])
        o_refs = refs[n_in:n_in + n_out]
        a_refs = refs[n_in + n_out:]
        for ref, val in zip(o_refs, res[:n_out]):
            ref[...] = val.astype(ref.dtype)
        if a_refs:
            @pl.when(pl.program_id(0) == 0)
            def _():
                for ref in a_refs:
                    ref[...] = jnp.zeros_like(ref)
            for ref, val in zip(a_refs, res[n_out:]):
                ref[...] += val

    return _call(body, [*rows, *pars], name=name, grid=(S // tile,), in_specs=in_specs, out_specs=out_specs,
                 out_shape=out_shape, sem=("arbitrary",) if accs else ("parallel",), comm=comm)


def _sigmoid(x):
    return 0.5 * jnp.tanh(0.5 * x) + 0.5


def _silu(x):
    return x * _sigmoid(x)


def _dsilu(x):
    sg = _sigmoid(x)
    return sg * (1.0 + x * (1.0 - sg))


def _softplus(x):
    return jnp.maximum(x, 0.0) + jnp.log(1.0 + jnp.exp(-jnp.abs(x)))


def _rms_fwd(x, g):
    r = lax.rsqrt(jnp.mean(x * x, axis=-1, keepdims=True) + EPS)
    return x * r * g


def _rms_bwd(x, g, dh):
    r = lax.rsqrt(jnp.mean(x * x, axis=-1, keepdims=True) + EPS)
    xh = x * r
    dxh = dh * g
    dx = r * (dxh - xh * jnp.mean(dxh * xh, axis=-1, keepdims=True))
    return dx, jnp.sum(dh * xh, axis=0, keepdims=True)


def _taps(x, width):
    row = lax.broadcasted_iota(jnp.int32, (8, x.shape[1]), 0)

    def shifted(s):
        r = pltpu.roll(x, s, 0)
        return jnp.concatenate([jnp.where(row >= s, r[:8], 0.0), r[8:]], axis=0)

    return [shifted(s) for s in range(width - 1, 0, -1)] + [x]


def _conv(x, w, b, taps=None):
    width = w.shape[0]
    taps = _taps(x, width) if taps is None else taps
    out = b + w[0:1, :] * taps[0]
    for k in range(1, width):
        out = out + w[k:k + 1, :] * taps[k]
    return out


def _conv_bwd(x, w, dc, taps=None):
    width, n = w.shape[0], x.shape[0]
    taps = _taps(x, width) if taps is None else taps
    row = lax.broadcasted_iota(jnp.int32, (8, x.shape[1]), 0)
    dx = w[width - 1:width, :] * dc
    for k in range(width - 1):
        s = width - 1 - k
        r = pltpu.roll(dc, n - s, 0)
        dx = dx + w[k:k + 1, :] * jnp.concatenate([r[:n - 8], jnp.where(row < 8 - s, r[n - 8:], 0.0)], axis=0)
    dw = jnp.concatenate([jnp.sum(dc * t, axis=0, keepdims=True) for t in taps], axis=0)
    return dx, dw, jnp.sum(dc, axis=0, keepdims=True)


def _rope_tables(pos, inv_freq):
    ang = pos * inv_freq
    return jnp.cos(ang), jnp.sin(ang)


def _split2(v):
    hi = v.astype(bf16)
    return hi, (v - hi.astype(f32)).astype(bf16)


def _head_maps(width):
    shift = AH.bit_length() - 1
    to_head = (lax.broadcasted_iota(jnp.int32, (width, LANES), 0) >> shift) == lax.broadcasted_iota(jnp.int32, (width, LANES), 1)
    from_head = lax.broadcasted_iota(jnp.int32, (LANES, width), 0) == (lax.broadcasted_iota(jnp.int32, (LANES, width), 1) >> shift)
    return to_head.astype(bf16), from_head.astype(bf16)


def _head_sums(v, to_head):
    hi, lo = _split2(v)
    return jnp.dot(hi, to_head, preferred_element_type=f32) + jnp.dot(lo, to_head, preferred_element_type=f32)


def _head_spread(s, from_head):
    hi, lo = _split2(s)
    return jnp.dot(hi, from_head, preferred_element_type=f32) + jnp.dot(lo, from_head, preferred_element_type=f32)


def _rope_full(cos, sin, width):
    half = AH // 2
    pad = jnp.zeros((cos.shape[0], LANES - half), f32)
    r = lax.broadcasted_iota(jnp.int32, (LANES, width), 0)
    lane = lax.broadcasted_iota(jnp.int32, (LANES, width), 1)
    spread = ((lane & (half - 1)) == r).astype(bf16)
    full = lambda t: _head_spread(jnp.concatenate([t, pad], axis=1), spread)
    first = (lax.broadcasted_iota(jnp.int32, (1, width), 1) & (AH - 1)) < half
    sin_f = full(sin)
    return full(cos), jnp.where(first, -sin_f, sin_f), first


def _swap_halves(v, first):
    half, width = AH // 2, v.shape[1]
    return jnp.where(first, pltpu.roll(v, width - half, 1), pltpu.roll(v, half, 1))


def _headnorm_rope_fwd(x, g, cos, sin, heads):
    to_head, from_head = _head_maps(heads * AH)
    cos_f, sin_s, first = _rope_full(cos, sin, heads * AH)
    r = _head_spread(lax.rsqrt(_head_sums(x * x, to_head) * (1.0 / AH) + EPS), from_head)
    n = x * r * jnp.tile(g, (1, heads))
    return n * cos_f + _swap_halves(n, first) * sin_s


def _headnorm_rope_bwd(x, g, cos, sin, dout, heads):
    width = heads * AH
    to_head, from_head = _head_maps(width)
    cos_f, sin_s, first = _rope_full(cos, sin, width)
    r = _head_spread(lax.rsqrt(_head_sums(x * x, to_head) * (1.0 / AH) + EPS), from_head)
    xh = x * r
    dn = dout * cos_f - _swap_halves(dout, first) * sin_s
    dxh = dn * jnp.tile(g, (1, heads))
    m = _head_spread(_head_sums(dxh * xh, to_head) * (1.0 / AH), from_head)
    dx = r * (dxh - xh * m)
    dg_lanes = jnp.sum(dn * xh, axis=0, keepdims=True)
    fold = ((lax.broadcasted_iota(jnp.int32, (width, LANES), 0) & (AH - 1))
            == lax.broadcasted_iota(jnp.int32, (width, LANES), 1)).astype(f32)
    dg = jnp.dot(jnp.broadcast_to(dg_lanes, (8, width)), fold, precision=lax.Precision.HIGHEST, preferred_element_type=f32)
    return dx, dg[0:1, :AH]


def _ssd_prep(dt_pre, dt_bias, a_log, dt_s, acum_s, acumT_s):
    dt = _softplus(dt_pre + dt_bias)
    a = dt * (-jnp.exp(a_log))
    row = lax.broadcasted_iota(jnp.int32, (CHUNK, CHUNK), 0)
    col = lax.broadcasted_iota(jnp.int32, (CHUNK, CHUNK), 1)
    dt_s[...] = dt
    acum_s[...] = jnp.dot((col <= row).astype(f32), a, precision=lax.Precision.HIGHEST, preferred_element_type=f32)
    acumT_s[...] = lax.dot_general(a, (row <= col).astype(f32), (((0,), (0,)), ((), ())),
                                   precision=lax.Precision.HIGHEST, preferred_element_type=f32)


def _head_cols(h, dt_s, acum_s, acumT_s):
    lane = lax.broadcasted_iota(jnp.int32, (1, LANES), 1)
    oh_l = (lane == h).astype(f32)
    sub = lax.broadcasted_iota(jnp.int32, (LANES, 1), 0)
    oh_s = (sub == h).astype(f32)
    dt_h = jnp.sum(dt_s[...] * oh_l, axis=1, keepdims=True)
    ac_h = jnp.sum(acum_s[...] * oh_l, axis=1, keepdims=True)
    acr_h = jnp.sum(acumT_s[...] * oh_s, axis=0, keepdims=True)
    return oh_l, dt_h, ac_h, acr_h


def ssd_fwd(xs, Bm, Cm, dt_pre, dt_bias, a_log, d_skip, comm=None):
    def body(xs_ref, b_ref, c_ref, dtp_ref, bias_ref, alog_ref, d_ref, y_ref, st_ref, state, dt_s, acum_s, acumT_s):
        c, g = pl.program_id(0), pl.program_id(1)

        @pl.when(g == 0)
        def _():
            _ssd_prep(dtp_ref[...], bias_ref[...], alog_ref[...], dt_s, acum_s, acumT_s)

        row = lax.broadcasted_iota(jnp.int32, (CHUNK, CHUNK), 0)
        col = lax.broadcasted_iota(jnp.int32, (CHUNK, CHUNK), 1)
        causal = col <= row
        Bb, Cb = b_ref[...], c_ref[...]
        cb = lax.dot_general(Cb.astype(MXU), Bb.astype(MXU), (((1,), (1,)), ((), ())), preferred_element_type=f32)
        xs_blk = xs_ref[...]

        @pl.when(c == 0)
        def _():
            for j in range(HPG):
                state[g * HPG + j] = jnp.zeros((NSTATE, HP), f32)

        prevs = [state[g * HPG + j] for j in range(HPG)]
        y_off_all = jnp.dot(Cb.astype(MXU), jnp.concatenate(prevs, axis=1).astype(MXU), preferred_element_type=f32)
        ys, xds, e_ends = [], [], []
        for j in range(HPG):
            oh_l, dt_h, ac_h, acr_h = _head_cols(g * HPG + j, dt_s, acum_s, acumT_s)
            decay = jnp.exp(jnp.where(causal, ac_h - acr_h, -1e30))
            w = (cb * decay).astype(MXU)
            xs_h = xs_blk[:, HP * j:HP * (j + 1)]
            xd = xs_h * dt_h
            y_diag = jnp.dot(w, xd.astype(MXU), preferred_element_type=f32)
            y_off = y_off_all[:, HP * j:HP * (j + 1)] * jnp.exp(ac_h)
            d_h = jnp.sum(d_ref[...] * oh_l, axis=1, keepdims=True)
            ys.append(y_diag + y_off + xs_h * d_h)
            a_end = ac_h[CHUNK - 1:CHUNK, :]
            xds.append(xd * jnp.exp(a_end - ac_h))
            e_ends.append(jnp.exp(a_end))
        s_c = lax.dot_general(Bb.astype(MXU), jnp.concatenate(xds, axis=1).astype(MXU), (((0,), (0,)), ((), ())),
                              preferred_element_type=f32)
        for j in range(HPG):
            st_ref[0, j] = prevs[j]
            state[g * HPG + j] = prevs[j] * e_ends[j] + s_c[:, HP * j:HP * (j + 1)]
        y_ref[...] = jnp.concatenate(ys, axis=1)

    par = pl.BlockSpec((1, LANES), lambda c, g: (0, 0))
    return _call(
        body, [xs, Bm, Cm, dt_pre, dt_bias, a_log, d_skip], comm=comm, name="ssd_fwd", grid=(NCHUNK, GROUPS),
        in_specs=[pl.BlockSpec((CHUNK, HPG * HP), lambda c, g: (c, g)),
                  pl.BlockSpec((CHUNK, NSTATE), lambda c, g: (c, g)),
                  pl.BlockSpec((CHUNK, NSTATE), lambda c, g: (c, g)),
                  pl.BlockSpec((CHUNK, LANES), lambda c, g: (c, 0)), par, par, par],
        out_specs=[pl.BlockSpec((CHUNK, HPG * HP), lambda c, g: (c, g)),
                   pl.BlockSpec((1, HPG, NSTATE, HP), lambda c, g: (c, g, 0, 0))],
        out_shape=[jax.ShapeDtypeStruct((S, INNER), f32), jax.ShapeDtypeStruct((NCHUNK, HEADS, NSTATE, HP), f32)],
        scratch=[pltpu.VMEM((HEADS, NSTATE, HP), f32), pltpu.VMEM((CHUNK, LANES), f32),
                 pltpu.VMEM((CHUNK, LANES), f32), pltpu.VMEM((LANES, CHUNK), f32)],
        sem=("arbitrary", "arbitrary"))


def ssd_bwd(xs, Bm, Cm, dt_pre, dt_bias, a_log, d_skip, states, dy, comm=None):
    rev = lambda c: NCHUNK - 1 - c

    def body(xs_ref, b_ref, c_ref, dtp_ref, bias_ref, alog_ref, d_ref, st_ref, dy_ref,
             dxs_ref, db_ref, dc_ref, ddt_ref, dbias_ref, dalog_ref, dd_ref,
             dstate, dt_s, acum_s, acumT_s, dacum_s, ddt_s, da_s):
        c, g = pl.program_id(0), pl.program_id(1)

        @pl.when(g == 0)
        def _():
            _ssd_prep(dtp_ref[...], bias_ref[...], alog_ref[...], dt_s, acum_s, acumT_s)
            dacum_s[...] = jnp.zeros_like(dacum_s)
            ddt_s[...] = jnp.zeros_like(ddt_s)

        @pl.when((c == 0) & (g == 0))
        def _():
            da_s[...] = jnp.zeros_like(da_s)
            dd_ref[...] = jnp.zeros_like(dd_ref)
            dbias_ref[...] = jnp.zeros_like(dbias_ref)
            dalog_ref[...] = jnp.zeros_like(dalog_ref)

        row = lax.broadcasted_iota(jnp.int32, (CHUNK, CHUNK), 0)
        col = lax.broadcasted_iota(jnp.int32, (CHUNK, CHUNK), 1)
        sub_l = lax.broadcasted_iota(jnp.int32, (CHUNK, 1), 0)
        last = (sub_l == CHUNK - 1).astype(f32)
        nt = (((1,), (1,)), ((), ()))
        tn = (((0,), (0,)), ((), ()))
        Bb, Cb = b_ref[...], c_ref[...]
        Bm_, Cm_ = Bb.astype(MXU), Cb.astype(MXU)
        cb = lax.dot_general(Cm_, Bm_, nt, preferred_element_type=f32)
        bc = lax.dot_general(Bm_, Cm_, nt, preferred_element_type=f32)
        xs_blk, dy_blk = xs_ref[...], dy_ref[...]
        dxs, dB, dC = [], jnp.zeros((CHUNK, NSTATE), f32), jnp.zeros((CHUNK, NSTATE), f32)
        for j in range(HPG):
            h = g * HPG + j
            oh_l, dt_h, ac_h, acr_h = _head_cols(h, dt_s, acum_s, acumT_s)

            @pl.when(c == 0)
            def _():
                dstate[h] = jnp.zeros((NSTATE, HP), f32)

            dnext = dstate[h]
            prev = st_ref[0, j]
            lm = jnp.exp(jnp.where(col <= row, ac_h - acr_h, -1e30))
            lmT = jnp.exp(jnp.where(row <= col, acr_h - ac_h, -1e30))
            xs_h = xs_blk[:, HP * j:HP * (j + 1)]
            dy_h = dy_blk[:, HP * j:HP * (j + 1)]
            xd = xs_h * dt_h
            xdm, dym = xd.astype(MXU), dy_h.astype(MXU)
            ea = jnp.exp(ac_h)
            a_end = ac_h[CHUNK - 1:CHUNK, :]
            e_end = jnp.exp(a_end)
            dte = jnp.exp(a_end - ac_h)
            dnm, pvm = dnext.astype(MXU), prev.astype(MXU)
            bd = jnp.dot(Bm_, dnm, preferred_element_type=f32)
            dxd = jnp.dot((bc * lmT).astype(MXU), dym, preferred_element_type=f32) + dte * bd
            dw = lax.dot_general(dym, xdm, nt, preferred_element_type=f32)
            dwT = lax.dot_general(xdm, dym, nt, preferred_element_type=f32)
            dcb = dw * lm
            dbc = dwT * lmT
            eady = (ea * dy_h).astype(MXU)
            dC = dC + jnp.dot(dcb.astype(MXU), Bm_, preferred_element_type=f32) \
                + lax.dot_general(eady, pvm, nt, preferred_element_type=f32)
            dB = dB + jnp.dot(dbc.astype(MXU), Cm_, preferred_element_type=f32) \
                + dte * lax.dot_general(xdm, dnm, nt, preferred_element_type=f32)
            dstate[h] = lax.dot_general(Cm_, eady, tn, preferred_element_type=f32) + e_end * dnext
            r1 = jnp.sum(dcb * cb, axis=1, keepdims=True)
            r2 = jnp.sum(dbc * bc, axis=1, keepdims=True)
            y_off = jnp.dot(Cm_, pvm, preferred_element_type=f32) * ea
            t3 = jnp.sum(dy_h * y_off, axis=1, keepdims=True)
            t4 = jnp.sum(bd * xd, axis=1, keepdims=True) * dte
            end_extra = jnp.sum(t4, axis=0, keepdims=True) + e_end * jnp.sum(jnp.sum(prev * dnext, axis=1, keepdims=True), axis=0, keepdims=True)
            dacum_h = r1 - r2 + t3 - t4 + last * end_extra
            dacum_s[...] += dacum_h * oh_l
            ddt_s[...] += jnp.sum(dxd * xs_h, axis=1, keepdims=True) * oh_l
            d_h = jnp.sum(d_ref[...] * oh_l, axis=1, keepdims=True)
            dxs.append(dxd * dt_h + dy_h * d_h)
            dd_ref[...] += oh_l * jnp.sum(jnp.sum(dy_h * xs_h, axis=1, keepdims=True), axis=0, keepdims=True)
        dxs_ref[...] = jnp.concatenate(dxs, axis=1)
        db_ref[...] = dB
        dc_ref[...] = dC

        @pl.when(g == GROUPS - 1)
        def _():
            a_row = -jnp.exp(alog_ref[...])
            da = jnp.dot((row <= col).astype(f32), dacum_s[...], precision=lax.Precision.HIGHEST, preferred_element_type=f32)
            da_s[...] += jnp.sum(da * dt_s[...], axis=0, keepdims=True)
            z = dtp_ref[...] + bias_ref[...]
            ddt_pre = (ddt_s[...] + da * a_row) * _sigmoid(z)
            ddt_ref[...] = ddt_pre.astype(ddt_ref.dtype)
            dbias_ref[...] += jnp.sum(ddt_pre, axis=0, keepdims=True)

            @pl.when(c == NCHUNK - 1)
            def _():
                dalog_ref[...] = da_s[...] * a_row

    par = pl.BlockSpec((1, LANES), lambda c, g: (0, 0))
    return _call(
        body, [xs, Bm, Cm, dt_pre, dt_bias, a_log, d_skip, states, dy], comm=comm, name="ssd_bwd", grid=(NCHUNK, GROUPS),
        in_specs=[pl.BlockSpec((CHUNK, HPG * HP), lambda c, g: (rev(c), g)),
                  pl.BlockSpec((CHUNK, NSTATE), lambda c, g: (rev(c), g)),
                  pl.BlockSpec((CHUNK, NSTATE), lambda c, g: (rev(c), g)),
                  pl.BlockSpec((CHUNK, LANES), lambda c, g: (rev(c), 0)), par, par, par,
                  pl.BlockSpec((1, HPG, NSTATE, HP), lambda c, g: (rev(c), g, 0, 0)),
                  pl.BlockSpec((CHUNK, HPG * HP), lambda c, g: (rev(c), g))],
        out_specs=[pl.BlockSpec((CHUNK, HPG * HP), lambda c, g: (rev(c), g)),
                   pl.BlockSpec((CHUNK, NSTATE), lambda c, g: (rev(c), g)),
                   pl.BlockSpec((CHUNK, NSTATE), lambda c, g: (rev(c), g)),
                   pl.BlockSpec((CHUNK, LANES), lambda c, g: (rev(c), 0)), par, par, par],
        out_shape=[jax.ShapeDtypeStruct((S, INNER), f32), jax.ShapeDtypeStruct((S, GROUPS * NSTATE), f32),
                   jax.ShapeDtypeStruct((S, GROUPS * NSTATE), f32), jax.ShapeDtypeStruct((S, LANES), MXU),
                   jax.ShapeDtypeStruct((1, LANES), f32), jax.ShapeDtypeStruct((1, LANES), f32),
                   jax.ShapeDtypeStruct((1, LANES), f32)],
        scratch=[pltpu.VMEM((HEADS, NSTATE, HP), f32), pltpu.VMEM((CHUNK, LANES), f32),
                 pltpu.VMEM((CHUNK, LANES), f32), pltpu.VMEM((LANES, CHUNK), f32),
                 pltpu.VMEM((CHUNK, LANES), f32), pltpu.VMEM((CHUNK, LANES), f32), pltpu.VMEM((1, LANES), f32)],
        sem=("arbitrary", "arbitrary"))


ATT_STACK_FWD, ATT_STACK_BWD = 4, 2


def _attn_kv(kp, kc, vp, vc, hk):
    sl = slice(AH * hk, AH * (hk + 1))
    return (jnp.concatenate([kp[:, sl], kc[:, sl]], axis=0).astype(MXU),
            jnp.concatenate([vp[:, sl], vc[:, sl]], axis=0).astype(MXU))


def _stack_heads(x, heads):
    return jnp.concatenate([x[:, AH * h:AH * (h + 1)] for h in heads], axis=0)


def _attn_block(n, q, kb, sinks, heads):
    rows = len(heads) * WIN
    qi = lax.broadcasted_iota(jnp.int32, (rows, 2 * WIN), 0) & (WIN - 1)
    ki = lax.broadcasted_iota(jnp.int32, (rows, 2 * WIN), 1)
    rel = qi + WIN - ki
    mask = (rel >= 0) & (rel < WIN) & ((ki >= WIN) | (n > 0))
    qg = _stack_heads(q, heads).astype(MXU)
    s = lax.dot_general(qg, kb, (((1,), (1,)), ((), ())), preferred_element_type=f32) * (AH ** -0.5)
    s = jnp.where(mask, s, -1e30)
    sink = jnp.concatenate([jnp.broadcast_to(sinks[:, h:h + 1], (WIN, 1)) for h in heads], axis=0)
    m = jnp.maximum(jnp.max(s, axis=1, keepdims=True), sink)
    p = jnp.exp(s - m)
    ps = jnp.exp(sink - m)
    inv = 1.0 / (jnp.sum(p, axis=1, keepdims=True) + ps)
    return qg, p * inv, ps * inv


def _head_blocks(hk, stack):
    return [list(range(QPK * hk + i, QPK * hk + i + stack)) for i in range(0, QPK, stack)]


def _kv_specs():
    prev = lambda n: (jnp.maximum(n - 1, 0), 0)
    cur = lambda n: (n, 0)
    w = KVH * AH
    return [pl.BlockSpec((WIN, w), prev), pl.BlockSpec((WIN, w), cur), pl.BlockSpec((WIN, w), prev), pl.BlockSpec((WIN, w), cur)]


def attn_fwd(q, k, v, sinks, comm=None):
    def body(q_ref, kp_ref, kc_ref, vp_ref, vc_ref, s_ref, o_ref):
        n = pl.program_id(0)
        q_, kp, kc, vp, vc, sk = q_ref[...], kp_ref[...], kc_ref[...], vp_ref[...], vc_ref[...], s_ref[...]
        outs = []
        for hk in range(KVH):
            kb, vb = _attn_kv(kp, kc, vp, vc, hk)
            for heads in _head_blocks(hk, ATT_STACK_FWD):
                _, pr, _ = _attn_block(n, q_, kb, sk, heads)
                o = jnp.dot(pr.astype(MXU), vb, preferred_element_type=f32)
                outs += [o[WIN * i:WIN * (i + 1)] for i in range(len(heads))]
        o_ref[...] = jnp.concatenate(outs, axis=1)

    return _call(
        body, [q, k, k, v, v, sinks], comm=comm, name="attn_fwd", grid=(NBLK,),
        in_specs=[pl.BlockSpec((WIN, D), lambda n: (n, 0))] + _kv_specs() + [pl.BlockSpec((1, QH), lambda n: (0, 0))],
        out_specs=[pl.BlockSpec((WIN, D), lambda n: (n, 0))],
        out_shape=[jax.ShapeDtypeStruct((S, D), f32)], sem=("parallel",))[0]


def attn_bwd(q, k, v, sinks, dout, comm=None):
    def body(q_ref, kp_ref, kc_ref, vp_ref, vc_ref, s_ref, do_ref, dq_ref, dkp_ref, dkc_ref, dvp_ref, dvc_ref, ds_ref):
        n = pl.program_id(0)

        @pl.when(n == 0)
        def _():
            ds_ref[...] = jnp.zeros_like(ds_ref)

        q_, kp, kc, vp, vc, sk, do = q_ref[...], kp_ref[...], kc_ref[...], vp_ref[...], vc_ref[...], s_ref[...], do_ref[...]
        lane = lax.broadcasted_iota(jnp.int32, (1, QH), 1)
        nt = (((1,), (1,)), ((), ()))
        tn = (((0,), (0,)), ((), ()))
        dqs, dkps, dkcs, dvps, dvcs = [], [], [], [], []
        dsink = jnp.zeros((1, QH), f32)
        for hk in range(KVH):
            kb, vb = _attn_kv(kp, kc, vp, vc, hk)
            dkb, dvb = jnp.zeros((2 * WIN, AH), f32), jnp.zeros((2 * WIN, AH), f32)
            for heads in _head_blocks(hk, ATT_STACK_BWD):
                qg, pr, prs = _attn_block(n, q_, kb, sk, heads)
                dog = _stack_heads(do, heads).astype(MXU)
                dp = lax.dot_general(dog, vb, nt, preferred_element_type=f32)
                dvb = dvb + lax.dot_general(pr.astype(MXU), dog, tn, preferred_element_type=f32)
                delta = jnp.sum(pr * dp, axis=1, keepdims=True)
                ds = (pr * (dp - delta)).astype(MXU)
                dsk = -prs * delta
                for i, h in enumerate(heads):
                    dsink = dsink + jnp.sum(dsk[WIN * i:WIN * (i + 1)], axis=0, keepdims=True) * (lane == h).astype(f32)
                dqg = jnp.dot(ds, kb, preferred_element_type=f32) * (AH ** -0.5)
                dkb = dkb + lax.dot_general(ds, qg, tn, preferred_element_type=f32) * (AH ** -0.5)
                dqs += [dqg[WIN * i:WIN * (i + 1)] for i in range(len(heads))]
            dkps.append(dkb[:WIN])
            dkcs.append(dkb[WIN:])
            dvps.append(dvb[:WIN])
            dvcs.append(dvb[WIN:])
        dq_ref[...] = jnp.concatenate(dqs, axis=1)
        dkp_ref[...] = jnp.concatenate(dkps, axis=1)
        dkc_ref[...] = jnp.concatenate(dkcs, axis=1)
        dvp_ref[...] = jnp.concatenate(dvps, axis=1)
        dvc_ref[...] = jnp.concatenate(dvcs, axis=1)
        ds_ref[...] += dsink

    w = KVH * AH
    blk = lambda width: pl.BlockSpec((WIN, width), lambda n: (n, 0))
    return _call(
        body, [q, k, k, v, v, sinks, dout], comm=comm, name="attn_bwd", grid=(NBLK,),
        in_specs=[blk(D)] + _kv_specs() + [pl.BlockSpec((1, QH), lambda n: (0, 0)), blk(D)],
        out_specs=[blk(D), blk(w), blk(w), blk(w), blk(w), pl.BlockSpec((1, QH), lambda n: (0, 0))],
        out_shape=[jax.ShapeDtypeStruct((S, D), f32)] + [jax.ShapeDtypeStruct((S, w), f32)] * 4 + [jax.ShapeDtypeStruct((1, QH), f32)],
        sem=("arbitrary",))


def kv_bwd(kv, pos, inv_freq, k_norm, dkp, dkc, dvp, dvc):
    w = KVH * AH

    def body(kv_ref, pos_ref, if_ref, g_ref, dkp_ref, dkc_ref, dvp_ref, dvc_ref, o_ref, dg_ref, db_ref):
        n = pl.program_id(0)

        @pl.when(n == 0)
        def _():
            dg_ref[...] = jnp.zeros_like(dg_ref)
            db_ref[...] = jnp.zeros_like(db_ref)

        inside = (n < NBLK - 1).astype(f32)
        dk = dkc_ref[...] + inside * dkp_ref[...]
        dv = dvc_ref[...] + inside * dvp_ref[...]
        cos, sin = _rope_tables(pos_ref[...], if_ref[...])
        dkpre, dg = _headnorm_rope_bwd(kv_ref[...], g_ref[...], cos, sin, dk, KVH)
        dkv = jnp.concatenate([dkpre, dv], axis=1)
        o_ref[...] = dkv.astype(o_ref.dtype)
        dg_ref[...] += dg
        db_ref[...] += jnp.sum(dkv, axis=0, keepdims=True)

    nxt = lambda n: (jnp.minimum(n + 1, NBLK - 1), 0)
    cur = lambda n: (n, 0)
    const = lambda n: (0, 0)
    return pl.pallas_call(
        body, name="kv_bwd", grid=(NBLK,),
        in_specs=[pl.BlockSpec((WIN, w), cur), pl.BlockSpec((WIN, 1), cur), pl.BlockSpec((1, AH // 2), const),
                  pl.BlockSpec((1, AH), const), pl.BlockSpec((WIN, w), nxt), pl.BlockSpec((WIN, w), cur),
                  pl.BlockSpec((WIN, w), nxt), pl.BlockSpec((WIN, w), cur)],
        out_specs=[pl.BlockSpec((WIN, 2 * w), cur), pl.BlockSpec((1, AH), const), pl.BlockSpec((1, 2 * w), const)],
        out_shape=[jax.ShapeDtypeStruct((S, 2 * w), MXU), jax.ShapeDtypeStruct((1, AH), f32), jax.ShapeDtypeStruct((1, 2 * w), f32)],
        compiler_params=_cparams(("arbitrary",)),
    )(kv, pos, inv_freq, k_norm, dkp, dkc, dvp, dvc)


def _adam_math(w, g, m, v):
    m = ADAM_B1 * m + (1.0 - ADAM_B1) * g
    v = ADAM_B2 * v + (1.0 - ADAM_B2) * (g * g)
    m_hat = m / (1.0 - ADAM_B1 ** ADAM_STEP)
    v_hat = v / (1.0 - ADAM_B2 ** ADAM_STEP)
    return -ADAM_LR * (m_hat / (jnp.sqrt(v_hat) + ADAM_EPS) + ADAM_WD * w), m, v


def adamw_layers(name, w, g0, g1, m, v):
    _, R, C = w.shape
    tr = _pick(R, (256, 128, 64, 32, 16, 8))

    def body(w_ref, g0_ref, g1_ref, m_ref, v_ref, g_ref, d_ref, nm_ref, nv_ref):
        g_ = jnp.where(pl.program_id(0) == 0, g0_ref[...], g1_ref[...])
        g_ref[0] = g_
        d_ref[0], nm_ref[0], nv_ref[0] = _adam_math(w_ref[0], g_, m_ref[0], v_ref[0])

    st = pl.BlockSpec((1, tr, C), lambda l, i: (l, i, 0))
    fl = pl.BlockSpec((tr, C), lambda l, i: (i, 0))
    return pl.pallas_call(
        body, name=name, grid=(2, R // tr), in_specs=[st, fl, fl, st, st], out_specs=[st] * 4,
        out_shape=[jax.ShapeDtypeStruct((2, R, C), f32)] * 4, compiler_params=_cparams(("parallel", "parallel")),
    )(w, g0, g1, m, v)


def adamw(name, w, g, m, v):
    R, C = w.shape
    tr = _pick(R, (256, 128, 64, 32, 16, 8))
    tc = C if tr < R or C % 256 else 256

    def body(w_ref, g_ref, m_ref, v_ref, d_ref, nm_ref, nv_ref):
        d_ref[...], nm_ref[...], nv_ref[...] = _adam_math(w_ref[...], g_ref[...], m_ref[...], v_ref[...])

    spec = pl.BlockSpec((tr, tc), lambda i, j: (i, j))
    return pl.pallas_call(
        body, name=name, grid=(R // tr, C // tc), in_specs=[spec] * 4, out_specs=[spec] * 3,
        out_shape=[jax.ShapeDtypeStruct((R, C), f32)] * 3, compiler_params=_cparams(("parallel", "parallel")),
    )(w, g, m, v)


def _me():
    return lax.axis_index("x"), lax.axis_index("y"), lax.axis_index("c")


def gather_comm(xs):
    n = len(xs)

    def parts(x_refs, o_refs, sems):
        send_sems, recv_sems, local_sems = sems
        x, y, c = _me()
        me, sibling = (x, y, c), (x, y, 1 - c)
        chips = [(1 - x, y), (x, 1 - y), (1 - x, 1 - y)]

        def copy(a, k, block, to, src=None):
            dst = o_refs[a].at[4 * block[0] + 2 * block[1] + block[2]]
            return pltpu.make_async_remote_copy(
                src_ref=dst if src is None else src, dst_ref=dst,
                send_sem=send_sems.at[7 * a + k], recv_sem=recv_sems.at[7 * a + k], device_id=to, device_id_type=MESH)

        mine = [pltpu.make_async_copy(x_refs[a], o_refs[a].at[4 * x + 2 * y + c], local_sems.at[a]) for a in range(n)]
        first = []
        for a in range(n):
            first.append(copy(a, 0, me, sibling, src=x_refs[a]))
            first += [copy(a, 1 + j, me, (*chip, c), src=x_refs[a]) for j, chip in enumerate(chips)]
        return copy, mine, first, me, sibling, chips, c

    def start(x_refs, o_refs, sems):
        _, mine, first, *_ = parts(x_refs, o_refs, sems)
        for cp in mine + first:
            cp.start()

    def finish(x_refs, o_refs, sems):
        copy, mine, first, me, sibling, chips, c = parts(x_refs, o_refs, sems)
        passed = []
        for j, chip in enumerate(chips):
            for a in range(n):
                copy(a, 1 + j, (*chip, c), me).wait_recv()
                cp = copy(a, 4 + j, (*chip, c), sibling)
                cp.start()
                passed.append(cp)
        for a in range(n):
            copy(a, 0, sibling, me).wait_recv()
            for j, chip in enumerate(chips):
                copy(a, 4 + j, (*chip, 1 - c), me).wait_recv()
        for cp in first + passed:
            cp.wait_send()
        for cp in mine:
            cp.wait()

    return Comm(xs, [jax.ShapeDtypeStruct((N_DEV,) + a.shape, a.dtype) for a in xs],
                [pltpu.SemaphoreType.DMA((7 * n,)), pltpu.SemaphoreType.DMA((7 * n,)), pltpu.SemaphoreType.DMA((n,))], start, finish)


def run_comm(name, comm):
    _call(lambda: None, [], name=name, grid=(1,), in_specs=[], out_specs=[], out_shape=[], comm=comm)
    return comm.results


def sibling_comm(gs):
    n = len(gs)

    def copies(g_refs, o_refs, sems):
        x, y, c = _me()
        return [pltpu.make_async_remote_copy(
            src_ref=g_refs[a].at[:, 1 - c], dst_ref=o_refs[a], send_sem=sems[0].at[a], recv_sem=sems[1].at[a],
            device_id=(x, y, 1 - c), device_id_type=MESH) for a in range(n)]

    def start(g_refs, o_refs, sems):
        for cp in copies(g_refs, o_refs, sems):
            cp.start()

    def finish(g_refs, o_refs, sems):
        for cp in copies(g_refs, o_refs, sems):
            cp.wait()

    return Comm(gs, [jax.ShapeDtypeStruct((4,) + g.shape[2:], g.dtype) for g in gs],
                [pltpu.SemaphoreType.DMA((n,)), pltpu.SemaphoreType.DMA((n,))], start, finish)


def chip_comm(ts):
    n = len(ts)

    def copies(t_refs, o_refs, sems):
        x, y, c = _me()
        chips = [(1 - x, y), (x, 1 - y), (1 - x, 1 - y)]
        return [pltpu.make_async_remote_copy(
            src_ref=t_refs[a].at[2 * px + py], dst_ref=o_refs[a].at[j],
            send_sem=sems[0].at[3 * a + j], recv_sem=sems[1].at[3 * a + j],
            device_id=(px, py, c), device_id_type=MESH) for j, (px, py) in enumerate(chips) for a in range(n)]

    def start(t_refs, o_refs, sems):
        for cp in copies(t_refs, o_refs, sems):
            cp.start()

    def finish(t_refs, o_refs, sems):
        for cp in copies(t_refs, o_refs, sems):
            cp.wait()

    return Comm(ts, [jax.ShapeDtypeStruct((3,) + t.shape[1:], t.dtype) for t in ts],
                [pltpu.SemaphoreType.DMA((3 * n,)), pltpu.SemaphoreType.DMA((3 * n,))], start, finish)


def _row_tile(rows):
    return _pick(rows, (512, 304, 256, 128))


def pair_add(name, g, r):
    _, _, R, C = g.shape
    tr = _row_tile(R)

    def body(c_ref, g_ref, r_ref, o_ref):
        o_ref[0] = (g_ref[0, 0].astype(f32) + r_ref[0].astype(f32)).astype(o_ref.dtype)

    return pl.pallas_call(
        body, name=name,
        grid_spec=pltpu.PrefetchScalarGridSpec(
            num_scalar_prefetch=1, grid=(4, R // tr),
            in_specs=[pl.BlockSpec((1, 1, tr, C), lambda p, i, c: (p, c[0], i, 0)),
                      pl.BlockSpec((1, tr, C), lambda p, i, c: (p, i, 0))],
            out_specs=pl.BlockSpec((1, tr, C), lambda p, i, c: (p, i, 0))),
        out_shape=jax.ShapeDtypeStruct((4, R, C), g.dtype),
        compiler_params=_cparams(("parallel", "parallel")),
    )(lax.axis_index("c").reshape(1).astype(jnp.int32), g, r)


def final_add(name, t, r):
    _, R, C = t.shape
    tr = _row_tile(R)

    def body(p_ref, t_ref, r_ref, o_ref):
        o_ref[...] = ((t_ref[0].astype(f32) + r_ref[0].astype(f32)) + r_ref[1].astype(f32)) + r_ref[2].astype(f32)

    chip = 2 * lax.axis_index("x") + lax.axis_index("y")
    return pl.pallas_call(
        body, name=name,
        grid_spec=pltpu.PrefetchScalarGridSpec(
            num_scalar_prefetch=1, grid=(R // tr,),
            in_specs=[pl.BlockSpec((1, tr, C), lambda i, p: (p[0], i, 0)),
                      pl.BlockSpec((3, tr, C), lambda i, p: (0, i, 0))],
            out_specs=pl.BlockSpec((tr, C), lambda i, p: (i, 0))),
        out_shape=jax.ShapeDtypeStruct((R, C), f32),
        compiler_params=_cparams(("parallel",)),
    )(chip.reshape(1).astype(jnp.int32), t, r)


class ReduceScatter:
    def __init__(self, tag, keys, grads):
        self.tag, self.keys, self.grads = tag, keys, grads
        self.send = [g.reshape((4, 2, g.shape[0] // N_DEV) + g.shape[1:]) for g in grads]

    def sibling(self):
        self.c1 = sibling_comm(self.send)
        return self.c1

    def chips(self):
        self.pairs = [pair_add(f"rs_pair_add_{self.tag}{i}", g, r) for i, (g, r) in enumerate(zip(self.send, self.c1.results))]
        self.c2 = chip_comm(self.pairs)
        return self.c2

    def sums(self):
        return {k: final_add(f"rs_final_add_{k}", t, r) for k, t, r in zip(self.keys, self.pairs, self.c2.results)}


IN_ROWS = {"z": (0, 2048), "xs": (2048, 4096), "B": (4096, 5120), "C": (5120, 6144)}
IN_COLS = 2 * INNER + 2 * GROUPS * NSTATE + HEADS


def sum_devices(g):
    def body(g_ref, o_ref):
        acc = g_ref[0]
        for i in range(1, N_DEV):
            acc = acc + g_ref[i]
        o_ref[...] = acc

    return pl.pallas_call(body, name="sum_devices", out_shape=jax.ShapeDtypeStruct(g.shape[1:], f32),
                          compiler_params=_cparams())(g)


def _pack(parts, unit, dtype, lead=()):
    flat = jnp.concatenate([p.reshape(lead + (-1,)).astype(dtype) for p in parts], axis=-1)
    n = flat.shape[-1]
    rows = -(-n // (unit * PACK_W)) * unit
    flat = jnp.pad(flat, [(0, 0)] * len(lead) + [(0, rows * PACK_W - n)])
    return flat.reshape(lead + (rows, PACK_W))


def _unpack(buf, shapes, lead=()):
    flat = buf.reshape(lead + (-1,))
    out, off = [], 0
    for shp in shapes:
        n = math.prod(shp)
        out.append(flat[..., off:off + n].reshape(lead + tuple(shp)))
        off += n
    return out


def _pad_lanes(a):
    return jnp.pad(a, [(0, 0)] * (a.ndim - 1) + [(0, LANES - a.shape[-1])])


_NN = (((1,), (0,)), ((), ()))
_NT = (((1,), (1,)), ((), ()))


def _mm(a, b, dims):
    return lax.dot_general(a.astype(MXU), b.astype(MXU), dims, preferred_element_type=f32)


def _ffn_fwd(tag, x, norm_g, w_inT, conv_w, conv_b, mid_comm=None):
    (h,) = rowwise(f"{tag}_norm", lambda x_, g_: (_rms_fwd(x_, g_),), [x], [norm_g], [(D, MXU)])
    ct, nblk = FFN_CT, FFN // FFN_CT

    def body(h_ref, wg_ref, wv_ref, cw_ref, cb_ref, gp_ref, v_ref, a_ref):
        h_ = h_ref[...]
        gp, v_ = _mm(h_, wg_ref[...], _NT), _mm(h_, wv_ref[...], _NT)
        gp_ref[...] = gp
        v_ref[...] = v_
        a_ref[...] = (_silu(_conv(gp, cw_ref[...], cb_ref[...])) * v_).astype(a_ref.dtype)

    col = pl.BlockSpec((S, ct), lambda j: (0, j))
    gate_pre, val, act = _call(
        body, [h, w_inT, w_inT, conv_w, conv_b], comm=mid_comm, name=f"{tag}_in", grid=(nblk,),
        in_specs=[pl.BlockSpec((S, D), lambda j: (0, 0)), pl.BlockSpec((ct, D), lambda j: (j, 0)),
                  pl.BlockSpec((ct, D), lambda j: (nblk + j, 0)), pl.BlockSpec((CONV_F, ct), lambda j: (0, j)),
                  pl.BlockSpec((1, ct), lambda j: (0, j))],
        out_specs=[col, col, col],
        out_shape=[jax.ShapeDtypeStruct((S, FFN), f32), jax.ShapeDtypeStruct((S, FFN), f32), jax.ShapeDtypeStruct((S, FFN), MXU)],
        sem=("parallel",))
    return act, (x, h, gate_pre, val, act)


FFN_CT = 256
CONV_CT = 256


def _proj_conv(name, h, wT, row0, cw, cb):
    C, ct = cw.shape[1], CONV_CT

    def body(h_ref, w_ref, cw_ref, cb_ref, p_ref, c_ref):
        p = _mm(h_ref[...], w_ref[...], _NT)
        p_ref[...] = p
        c_ref[...] = _silu(_conv(p, cw_ref[...], cb_ref[...]))

    col = pl.BlockSpec((S, ct), lambda j: (0, j))
    return _call(
        body, [h, wT, cw, cb], name=name, grid=(C // ct,),
        in_specs=[pl.BlockSpec((S, D), lambda j: (0, 0)), pl.BlockSpec((ct, D), lambda j: (row0 // ct + j, 0)),
                  pl.BlockSpec((CONV_A, ct), lambda j: (0, j)), pl.BlockSpec((1, ct), lambda j: (0, j))],
        out_specs=[col, col], out_shape=[jax.ShapeDtypeStruct((S, C), f32)] * 2, sem=("parallel",))


def _dconv_wgrad(name, pre, dconv, cw, cb, h):
    C, ct = cw.shape[1], CONV_CT

    def body(p_ref, do_ref, cw_ref, cb_ref, h_ref, dp_ref, g_ref, dw_ref, db_ref):
        p_, w_ = p_ref[...], cw_ref[...]
        taps = _taps(p_, CONV_A)
        dx, dw, db = _conv_bwd(p_, w_, do_ref[...] * _dsilu(_conv(p_, w_, cb_ref[...], taps)), taps)
        dpm = dx.astype(MXU)
        dp_ref[...] = dpm
        g_ref[...] = lax.dot_general(dpm, h_ref[...].astype(MXU), (((0,), (0,)), ((), ())),
                                     preferred_element_type=f32).astype(g_ref.dtype)
        dw_ref[...] = dw
        db_ref[...] = db

    col = pl.BlockSpec((S, ct), lambda j: (0, j))
    return _call(
        body, [pre, dconv, cw, cb, h], name=name, grid=(C // ct,),
        in_specs=[col, col, pl.BlockSpec((CONV_A, ct), lambda j: (0, j)), pl.BlockSpec((1, ct), lambda j: (0, j)),
                  pl.BlockSpec((S, D), lambda j: (0, 0))],
        out_specs=[col, pl.BlockSpec((ct, D), lambda j: (j, 0)), pl.BlockSpec((CONV_A, ct), lambda j: (0, j)),
                   pl.BlockSpec((1, ct), lambda j: (0, j))],
        out_shape=[jax.ShapeDtypeStruct((S, C), MXU), jax.ShapeDtypeStruct((C, D), MXU),
                   jax.ShapeDtypeStruct((CONV_A, C), f32), jax.ShapeDtypeStruct((1, C), f32)],
        sem=("parallel",))


def _ffn_mid_bwd(name, dout, w_down, gate_pre, val, conv_w, conv_b, comm=None):
    ct = FFN_CT

    def body(do_ref, wd_ref, gp_ref, v_ref, w_ref, b_ref, dgv_ref, dw_ref, db_ref, dob_s):
        @pl.when(pl.program_id(0) == 0)
        def _():
            dob_s[...] = do_ref[...].astype(MXU)

        da = _mm(dob_s[...], wd_ref[...], _NT)
        gp, v_, w_ = gp_ref[...], v_ref[...], w_ref[...]
        taps = _taps(gp, CONV_F)
        gate = _conv(gp, w_, b_ref[...], taps)
        sg = _sigmoid(gate)
        dgp, dw, db = _conv_bwd(gp, w_, da * v_ * (sg * (1.0 + gate * (1.0 - sg))), taps)
        dgv_ref[0] = dgp.astype(dgv_ref.dtype)
        dgv_ref[1] = (da * (gate * sg)).astype(dgv_ref.dtype)
        dw_ref[...] = dw
        db_ref[...] = db

    col = pl.BlockSpec((S, ct), lambda j: (0, j))
    return _call(
        body, [dout, w_down, gate_pre, val, conv_w, conv_b], comm=comm, name=name, grid=(FFN // ct,),
        in_specs=[pl.BlockSpec((S, D), lambda j: (0, 0)), pl.BlockSpec((ct, D), lambda j: (j, 0)), col, col,
                  pl.BlockSpec((CONV_F, ct), lambda j: (0, j)), pl.BlockSpec((1, ct), lambda j: (0, j))],
        out_specs=[pl.BlockSpec((2, S, ct), lambda j: (0, 0, j)), pl.BlockSpec((CONV_F, ct), lambda j: (0, j)),
                   pl.BlockSpec((1, ct), lambda j: (0, j))],
        out_shape=[jax.ShapeDtypeStruct((2, S, FFN), MXU), jax.ShapeDtypeStruct((CONV_F, FFN), f32), jax.ShapeDtypeStruct((1, FFN), f32)],
        scratch=[pltpu.VMEM((S, D), MXU)], sem=("arbitrary",))


def _ffn_bwd(tag, layer, saved, norm_g, w_inT, conv_w, conv_b, w_down, dout, mid_comm=None):
    x, h, gate_pre, val, act = saved
    g_down = matmul(f"{tag}_wdown", act, dout, "tn", out_dtype=MXU)
    dgv, g_cw, g_cb = _ffn_mid_bwd(f"{tag}_dmid", dout, w_down, gate_pre, val, conv_w, conv_b, comm=mid_comm)
    g_inT = matmul_tn_stacked(f"{tag}_win", dgv, h, MXU)

    def din_fn(dg_, dv_, x_, do_, g_, wT):
        dx, dg = _rms_bwd(x_, g_, _mm(dg_, wT[:FFN], _NN) + _mm(dv_, wT[FFN:], _NN))
        return do_ + dx, dg

    rs = ReduceScatter(tag, (f"f_inT{layer}", f"f_down{layer}"), [g_inT, g_down])
    dx, g_norm = rowwise(f"{tag}_din", din_fn, [(dgv, 0), (dgv, 1), x, dout], [norm_g, w_inT], [(D, f32)], [(1, D)],
                         comm=rs.sibling())
    return dx, {f"f_norm{layer}": g_norm, f"f_conv_w{layer}": g_cw, f"f_conv_b{layer}": g_cb}, rs


def _land(W, keys, comm):
    for k, g in zip(keys, comm.results):
        W[k] = g.reshape(-1, g.shape[2])


def _local_step(x, pos, tgt, W, shards):
    G = {}
    gather = lambda *keys: gather_comm([shards[k] for k in keys])
    inv_freq = (ROPE_THETA ** (-jnp.arange(AH // 2, dtype=f32) / (AH // 2))).reshape(1, AH // 2)

    def in_fn(x_, g_, wT, wdtT):
        h_ = _rms_fwd(x_, g_).astype(MXU)
        return h_, _mm(h_, wT[slice(*IN_ROWS["z"])], _NT), _mm(h_, wdtT, _NT)

    h0, z, dt_pre = rowwise("a_in", in_fn, [x], [W["a_norm"], W["inT"], W["in_dtT"]], [(D, MXU), (INNER, f32), (LANES, f32)])
    pre, conv = {}, {}
    for k in ("xs", "B", "C"):
        pre[k], conv[k] = _proj_conv(f"a_in_{k}", h0, W["inT"], IN_ROWS[k][0], W[f"cw_{k}"], W[f"cb_{k}"])
    c = gather("a_out", "f_inT0", "f_down0")
    y, states = ssd_fwd(conv["xs"], conv["B"], conv["C"], dt_pre, W["dt_bias"], W["A_log"], W["D"], comm=c)
    _land(W, ("a_out", "f_inT0", "f_down0"), c)

    def gate_norm(y_, z_, g_):
        yg = y_ * _silu(z_)
        w = INNER // GROUPS
        return (jnp.concatenate([_rms_fwd(yg[:, w * i:w * (i + 1)], g_[:, w * i:w * (i + 1)]) for i in range(GROUPS)], axis=1),)

    def out_fn(y_, z_, x_, g_, w_):
        (gn_,) = gate_norm(y_, z_, g_)
        gn_ = gn_.astype(MXU)
        return gn_, x_ + _mm(gn_, w_, _NN)

    gn, x1 = rowwise("a_out", out_fn, [y, z, x], [W["a_gnorm"], W["a_out"]], [(INNER, MXU), (D, f32)])

    c = gather("w_kv", "w_q", "w_o")
    act0, ffn0 = _ffn_fwd("f0", x1, W["f_norm0"], W["f_inT0"], W["f_cw0"], W["f_cb0"], mid_comm=c)
    _land(W, ("w_kv", "w_q", "w_o"), c)
    x2 = matmul("f0_down", act0, W["f_down0"], "nn", residual=x1)

    def qkv_fn(x_, gk, gb, wkv, bkv, wq, bq):
        kvn_, h2_ = _rms_fwd(x_, gk).astype(MXU), _rms_fwd(x_, gb).astype(MXU)
        return kvn_, h2_, _mm(kvn_, wkv, _NN) + bkv, _mm(h2_, wq, _NN) + bq

    kw = KVH * AH
    kvn, h2, kv, q_pre = rowwise("qkv_proj", qkv_fn, [x2], [W["kv_norm"], W["b_norm"], W["w_kv"], W["b_kv"], W["w_q"], W["b_q"]],
                                 [(D, MXU), (D, MXU), (2 * kw, f32), (D, f32)])

    def k_fwd(kv_, pos_, if_, g_):
        cos, sin = _rope_tables(pos_, if_)
        return _headnorm_rope_fwd(kv_[:, :kw], g_, cos, sin, KVH), kv_[:, kw:]

    k_rot, v_val = rowwise("k_rope", k_fwd, [kv, pos], [inv_freq, W["k_norm"]], [(kw, f32), (kw, f32)])

    def q_fwd(q_, pos_, if_, g_):
        cos, sin = _rope_tables(pos_, if_)
        return (_headnorm_rope_fwd(q_, g_, cos, sin, QH),)

    c = gather("f_down1")
    (q,) = rowwise("q_rope", q_fwd, [q_pre, pos], [inv_freq, W["q_norm"]], [(D, f32)], comm=c)
    _land(W, ("f_down1",), c)
    c = gather("f_inT1")
    att = attn_fwd(q, k_rot, v_val, W["sinks"], comm=c)
    _land(W, ("f_inT1",), c)
    x3 = matmul("o_proj", att, W["w_o"], "nn", bias=W["b_o"], residual=x2)

    act1, ffn1 = _ffn_fwd("f1", x3, W["f_norm1"], W["f_inT1"], W["f_cw1"], W["f_cb1"])

    def loss_fn(a_, x_, t_, w_):
        diff = x_ + _mm(a_, w_, _NN) - t_
        rows = jnp.sum(diff * diff, axis=1, keepdims=True) * (0.5 / D)
        return diff * (1.0 / D), jnp.sum(rows, axis=0, keepdims=True)

    dx4, loss = rowwise("f1_down_loss", loss_fn, [act1, x3, tgt], [W["f_down1"]], [(D, f32)], [(1, 1)])

    dx3, g, rs_f1 = _ffn_bwd("f1", 1, ffn1, W["f_norm1"], W["f_inT1"], W["f_cw1"], W["f_cb1"], W["f_down1"], dx4)
    G.update(g)

    datt = matmul("o_dproj", dx3, W["w_o"], "nt")
    g_wo = matmul("o_wproj", att, dx3, "tn", out_dtype=MXU)
    dq, dkp, dkc, dvp, dvc, G["sinks"] = attn_bwd(q, k_rot, v_val, W["sinks"], datt, comm=rs_f1.chips())

    def q_bwd(q_, pos_, dq_, dx_, if_, g_):
        cos, sin = _rope_tables(pos_, if_)
        dqp, dg = _headnorm_rope_bwd(q_, g_, cos, sin, dq_, QH)
        return dqp, dg, jnp.sum(dqp, axis=0, keepdims=True), jnp.sum(dx_, axis=0, keepdims=True)

    dq_pre, G["q_norm"], G["b_q"], G["b_o"] = rowwise("q_drope", q_bwd, [q_pre, pos, dq, dx3], [inv_freq, W["q_norm"]],
                                                      [(D, MXU)], [(1, AH), (1, D), (1, D)])
    g_wq = matmul("q_wproj", h2, dq_pre, "tn", out_dtype=MXU)
    dkv, G["k_norm"], G["b_kv"] = kv_bwd(kv, pos, inv_freq, W["k_norm"], dkp, dkc, dvp, dvc)
    g_wkv = matmul("kv_wproj", kvn, dkv, "tn", out_dtype=MXU)
    rs_att = ReduceScatter("att", ("w_kv", "w_q", "w_o"), [g_wkv, g_wq, g_wo])

    def x2_bwd(x_, dq_, dkv_, dx_, gb_, gk_, wq, wkv):
        d1, dgb = _rms_bwd(x_, gb_, _mm(dq_, wq, _NT))
        d2, dgk = _rms_bwd(x_, gk_, _mm(dkv_, wkv, _NT))
        return dx_ + d1 + d2, dgb, dgk

    dx2, G["b_norm"], G["kv_norm"] = rowwise("qkv_dproj", x2_bwd, [x2, dq_pre, dkv, dx3],
                                             [W["b_norm"], W["kv_norm"], W["w_q"], W["w_kv"]],
                                             [(D, f32)], [(1, D), (1, D)], comm=rs_att.sibling())

    dx1, g, rs_f0 = _ffn_bwd("f0", 0, ffn0, W["f_norm0"], W["f_inT0"], W["f_cw0"], W["f_cb0"], W["f_down0"], dx2,
                             mid_comm=rs_att.chips())
    G.update(g)

    rs_out = ReduceScatter("a_out", ("a_out",), [matmul("a_wout", gn, dx1, "tn", out_dtype=MXU)])

    def gate_norm_bwd(y_, z_, dx_, g_, w_out):
        dgn_ = _mm(dx_, w_out, _NT)
        w = INNER // GROUPS
        sg = _sigmoid(z_)
        sz = z_ * sg
        yg = y_ * sz
        parts, dgs = [], []
        for i in range(GROUPS):
            dseg, dg = _rms_bwd(yg[:, w * i:w * (i + 1)], g_[:, w * i:w * (i + 1)], dgn_[:, w * i:w * (i + 1)])
            parts.append(dseg)
            dgs.append(dg)
        dyg = jnp.concatenate(parts, axis=1)
        return dyg * sz, dyg * y_ * (sg * (1.0 + z_ * (1.0 - sg))), jnp.concatenate(dgs, axis=1)

    dy, dz, G["a_gnorm"] = rowwise("a_dout", gate_norm_bwd, [y, z, dx1], [W["a_gnorm"], W["a_out"]],
                                   [(INNER, f32), (INNER, MXU)], [(1, INNER)], comm=rs_out.sibling())
    dconv = {}
    dconv["xs"], dconv["B"], dconv["C"], ddt_pre, G["dt_bias"], G["A_log"], G["D"] = ssd_bwd(
        conv["xs"], conv["B"], conv["C"], dt_pre, W["dt_bias"], W["A_log"], W["D"], states, dy,
        comm=merge_comms([rs_f0.chips(), rs_out.chips()]))

    g_in, dpre = [matmul("a_win_z", dz, h0, "tn", out_dtype=MXU)], {}
    for k in ("xs", "B", "C"):
        dpre[k], g_k, G[f"cw_{k}"], G[f"cb_{k}"] = _dconv_wgrad(f"a_dconv_{k}", pre[k], dconv[k], W[f"cw_{k}"], W[f"cb_{k}"], h0)
        g_in.append(g_k)
    g_in.append(matmul("a_win_dt", ddt_pre, h0, "tn", out_dtype=MXU)[:HEADS])
    rs_in = ReduceScatter("a_in", ("inT",), [jnp.concatenate(g_in, axis=0)])
    run_comm("rs_in_sibling", rs_in.sibling())

    def x0_bwd(dz_, dxs_, db_, dc_, ddt_, x_, do_, g_, wT, wdtT):
        parts = zip((dz_, dxs_, db_, dc_), IN_ROWS.values())
        dh = sum(_mm(d_, wT[a:b], _NN) for d_, (a, b) in parts) + _mm(ddt_, wdtT, _NN)
        dx, dg = _rms_bwd(x_, g_, dh)
        return do_ + dx, dg

    dx, G["a_norm"] = rowwise("a_din", x0_bwd, [dz, dpre["xs"], dpre["B"], dpre["C"], ddt_pre, x, dx1],
                              [W["a_norm"], W["inT"], W["in_dtT"]], [(D, f32)], [(1, D)], comm=rs_in.chips())
    return loss, dx, G, [rs_f1, rs_att, rs_f0, rs_out, rs_in]


ROW_KEYS = ("inT", "a_out", "f_inT0", "f_down0", "w_kv", "w_q", "w_o", "f_inT1", "f_down1")


def _row_blocks(src):
    return {"inT": src["a_in_proj"][0].T, "a_out": src["a_out_proj"][0], "w_kv": src["w_kv"], "w_q": src["w_q"][0],
            "w_o": src["w_o"][0], "f_inT0": src["f_w_in"][0].T, "f_inT1": src["f_w_in"][1].T,
            "f_down0": src["f_w_down"][0], "f_down1": src["f_w_down"][1]}


def _from_row_blocks(rb):
    out = {"a_in_proj": rb["inT"].T[None], "a_out_proj": rb["a_out"][None], "w_kv": rb["w_kv"], "w_q": rb["w_q"][None],
           "w_o": rb["w_o"][None]}
    if "f_inT0" in rb:
        out["f_w_in"] = jnp.stack([rb["f_inT0"].T, rb["f_inT1"].T])
        out["f_w_down"] = jnp.stack([rb["f_down0"], rb["f_down1"]])
    return out


SMALL_SHARDED = ("a_norm", "a_conv_w", "a_conv_b", "a_gnorm", "f_conv_w")
REPLICATED = ("a_dt_bias", "a_A_log", "a_D", "kv_norm", "b_kv", "k_norm", "b_norm", "b_q", "q_norm", "sinks", "b_o",
              "f_norm", "f_conv_b")
ORDER = ("a_norm", "a_in_proj", "a_conv_w", "a_conv_b", "a_dt_bias", "a_A_log", "a_D", "a_gnorm", "a_out_proj", "kv_norm",
         "w_kv", "b_kv", "k_norm", "b_norm", "w_q", "b_q", "q_norm", "sinks", "w_o", "b_o", "f_norm", "f_w_in",
         "f_conv_w", "f_conv_b", "f_w_down")


def _gathered_to_whole(name, g):
    if name == "a_conv_w":
        return jnp.moveaxis(g[:, 0], 0, 1).reshape(g.shape[2], -1)
    if name in ("a_norm", "a_conv_b", "a_gnorm"):
        return g[:, 0].reshape(1, -1)
    if name == "f_conv_w":
        return jnp.moveaxis(g, 0, 2).reshape(g.shape[1], g.shape[2], -1)
    raise ValueError(name)


def _whole_to_shards(name, w):
    if name == "a_conv_w":
        return jnp.moveaxis(w.reshape(w.shape[0], N_DEV, -1), 1, 0)[:, None]
    if name in ("a_norm", "a_conv_b", "a_gnorm"):
        return w.reshape(N_DEV, 1, -1)
    if name == "f_conv_w":
        return jnp.moveaxis(w.reshape(w.shape[0], w.shape[1], N_DEV, -1), 2, 0)
    raise ValueError(name)


def _small_weights(whole):
    W = {}
    cw, cb = whole["a_conv_w"], whole["a_conv_b"]
    o = 0
    for k, n in (("xs", INNER), ("B", GROUPS * NSTATE), ("C", GROUPS * NSTATE)):
        W[f"cw_{k}"], W[f"cb_{k}"] = cw[:, o:o + n], cb[:, o:o + n]
        o += n
    W["a_norm"], W["a_gnorm"] = whole["a_norm"], whole["a_gnorm"]
    W["dt_bias"], W["A_log"], W["D"] = (_pad_lanes(whole[k]) for k in ("a_dt_bias", "a_A_log", "a_D"))
    W["kv_norm"], W["b_kv"], W["k_norm"] = whole["kv_norm"].reshape(1, -1), whole["b_kv"].reshape(1, -1), whole["k_norm"].reshape(1, -1)
    for k in ("b_norm", "b_q", "q_norm", "sinks", "b_o"):
        W[k] = whole[k]
    for i in range(2):
        W[f"f_norm{i}"] = whole["f_norm"][i:i + 1]
        W[f"f_cw{i}"], W[f"f_cb{i}"] = whole["f_conv_w"][i], whole["f_conv_b"][i:i + 1]
    return W


def _small_grads(G, shapes):
    nh = HEADS
    out = {
        "a_conv_w": jnp.concatenate([G["cw_xs"], G["cw_B"], G["cw_C"]], axis=1),
        "a_conv_b": jnp.concatenate([G["cb_xs"], G["cb_B"], G["cb_C"]], axis=1),
        "a_norm": G["a_norm"], "a_gnorm": G["a_gnorm"],
        "a_dt_bias": G["dt_bias"][:, :nh], "a_A_log": G["A_log"][:, :nh], "a_D": G["D"][:, :nh],
        "kv_norm": G["kv_norm"], "b_kv": G["b_kv"], "k_norm": G["k_norm"], "b_norm": G["b_norm"],
        "b_q": G["b_q"], "q_norm": G["q_norm"], "sinks": G["sinks"], "b_o": G["b_o"],
        "f_norm": jnp.concatenate([G["f_norm0"], G["f_norm1"]], axis=0),
        "f_conv_w": jnp.stack([G["f_conv_w0"], G["f_conv_w1"]]),
        "f_conv_b": jnp.concatenate([G["f_conv_b0"], G["f_conv_b1"]], axis=0),
    }
    return {k: val.reshape(shapes[k]) if k in shapes else val for k, val in out.items()}


def kernel(x, positions, a_norm, a_in_proj, a_conv_w, a_conv_b, a_dt_bias, a_A_log, a_D, a_gnorm, a_out_proj, kv_norm, w_kv, b_kv, k_norm, b_norm, w_q, b_q, q_norm, sinks, w_o, b_o, f_norm, f_w_in, f_conv_w, f_conv_b, f_w_down, loss_target, m_a_norm, m_a_in_proj, m_a_conv_w, m_a_conv_b, m_a_dt_bias, m_a_A_log, m_a_D, m_a_gnorm, m_a_out_proj, m_kv_norm, m_w_kv, m_b_kv, m_k_norm, m_b_norm, m_w_q, m_b_q, m_q_norm, m_sinks, m_w_o, m_b_o, m_f_norm, m_f_w_in, m_f_conv_w, m_f_conv_b, m_f_w_down, v_a_norm, v_a_in_proj, v_a_conv_w, v_a_conv_b, v_a_dt_bias, v_a_A_log, v_a_D, v_a_gnorm, v_a_out_proj, v_kv_norm, v_w_kv, v_b_kv, v_k_norm, v_b_norm, v_w_q, v_b_q, v_q_norm, v_sinks, v_w_o, v_b_o, v_f_norm, v_f_w_in, v_f_conv_w, v_f_conv_b, v_f_w_down):
    given = dict(locals())
    w_in = {n: given[n] for n in ORDER}
    m_in = {n: given["m_" + n] for n in ORDER}
    v_in = {n: given["v_" + n] for n in ORDER}
    dev = 4 * lax.axis_index("x") + 2 * lax.axis_index("y") + lax.axis_index("c")

    w2, m2, v2 = _row_blocks(w_in), _row_blocks(m_in), _row_blocks(v_in)
    small_pack = _pack([w_in[n] for n in SMALL_SHARDED], 8, f32)
    shards = {k: w2[k].astype(MXU) for k in ROW_KEYS}
    in_all, small_all = run_comm("ag_head", gather_comm([shards["inT"], small_pack]))
    whole = {n: w_in[n] for n in REPLICATED}
    for n, g in zip(SMALL_SHARDED, _unpack(small_all, [w_in[n].shape for n in SMALL_SHARDED], lead=(N_DEV,))):
        whole[n] = _gathered_to_whole(n, g)
    W = _small_weights(whole)
    W["inT"] = in_all.reshape(-1, D)
    W["in_dtT"] = jnp.pad(W["inT"][IN_COLS - HEADS:], ((0, LANES - HEADS), (0, 0)))

    loss, dx, G, scatters = _local_step(x[0], positions.reshape(S, 1).astype(f32), loss_target[0], W, shards)
    grads = _small_grads(G, {n: whole[n].shape for n in REPLICATED})

    small_names = SMALL_SHARDED + REPLICATED
    small_part = _pack([grads[n] for n in small_names], 8, f32)
    small_gather = gather_comm([small_part])
    run_comm("ag_small_grads", small_gather)
    sums = {}
    for rs in scatters:
        sums.update(rs.sums())
    single = tuple(k for k in ROW_KEYS if not k.startswith("f_"))
    g_out = _from_row_blocks({k: sums[k] for k in single})
    small_sum = sum_devices(small_gather.results[0])
    for n, g in zip(small_names, _unpack(small_sum, [grads[n].shape for n in small_names])):
        if n in SMALL_SHARDED:
            g_out[n] = lax.dynamic_index_in_dim(_whole_to_shards(n, g), dev, axis=0, keepdims=False)
        else:
            g_out[n] = g.reshape(w_in[n].shape)

    stepped = {k: adamw(f"adamw_{k}", w2[k], sums[k], m2[k], v2[k]) for k in single}
    delta, new_m, new_v = (_from_row_blocks({k: stepped[k][i] for k in single}) for i in range(3))
    for n, key, lay in (("f_w_in", "f_inT", lambda a: jnp.swapaxes(a, 1, 2)), ("f_w_down", "f_down", lambda a: a)):
        res = adamw_layers(f"adamw_{n}", lay(w_in[n]), sums[key + "0"], sums[key + "1"], lay(m_in[n]), lay(v_in[n]))
        g_out[n], delta[n], new_m[n], new_v[n] = (lay(a) for a in res)
    packs = [_pack([src[n] for n in small_names], 8, f32) for src in (w_in, g_out, m_in, v_in)]
    outs = adamw("adamw_small", *packs)
    for dst, buf in zip((delta, new_m, new_v), outs):
        for n, a in zip(small_names, _unpack(buf, [w_in[n].shape for n in small_names])):
            dst[n] = a

    loss_all = lax.psum(loss[0, 0], AXES)
    return (loss_all, dx[None], *[g_out[n] for n in ORDER], *[delta[n] for n in ORDER],
            *[new_m[n] for n in ORDER], *[new_v[n] for n in ORDER])
```

```python
import functools
import math

import jax
import jax.numpy as jnp
from jax import lax
from jax.experimental import pallas as pl
from jax.experimental.pallas import tpu as pltpu

f32 = jnp.float32
bf16 = jnp.bfloat16
MXU = bf16

N_DEV = 8
S = 2048
D = 1024
EPS = 1e-5
INNER = 2048
HEADS = 32
HP = 64
GROUPS = 8
HPG = HEADS // GROUPS
NSTATE = 128
CONV_A = 4
CHUNK = 256
NCHUNK = S // CHUNK
AH = 64
QH = 16
KVH = 4
QPK = QH // KVH
WIN = 128
NBLK = S // WIN
ROPE_THETA = 10000.0
FFN = 2816
CONV_F = 3
LANES = 128
PACK_W = 1024
VMEM_LIMIT = 56 * 1024 * 1024

ADAM_LR, ADAM_B1, ADAM_B2, ADAM_EPS, ADAM_WD, ADAM_STEP = 0.001, 0.9, 0.999, 1e-08, 0.01, 10

MESH = pl.DeviceIdType.MESH
AXES = ("x", "y", "c")


def _cparams(sem=None):
    return pltpu.CompilerParams(dimension_semantics=sem, vmem_limit_bytes=VMEM_LIMIT)


def _pick(n, cands):
    for c in cands:
        if n % c == 0:
            return c
    return n


class Comm:
    def __init__(self, ins, out_shapes, sems, start, finish):
        self.ins, self.out_shapes, self.sems, self.start, self.finish = list(ins), list(out_shapes), list(sems), start, finish
        self.results, self.children = None, ()

    def set_results(self, res):
        self.results, o = list(res), 0
        for ch in self.children:
            ch.set_results(res[o:o + len(ch.out_shapes)])
            o += len(ch.out_shapes)


def merge_comms(comms):
    def each(fn_name, ins, outs, sems):
        i = o = s = 0
        for c in comms:
            getattr(c, fn_name)(ins[i:i + len(c.ins)], outs[o:o + len(c.out_shapes)], sems[s:s + len(c.sems)])
            i, o, s = i + len(c.ins), o + len(c.out_shapes), s + len(c.sems)

    merged = Comm([a for c in comms for a in c.ins], [a for c in comms for a in c.out_shapes], [a for c in comms for a in c.sems],
                  functools.partial(each, "start"), functools.partial(each, "finish"))
    merged.children = tuple(comms)
    return merged


def _call(body, args, *, name, grid, in_specs, out_specs, out_shape, scratch=(), sem=None, comm=None):
    if comm is None:
        return pl.pallas_call(body, name=name, grid=grid, in_specs=list(in_specs), out_specs=list(out_specs),
                              out_shape=list(out_shape), scratch_shapes=list(scratch), compiler_params=_cparams(sem))(*args)
    n_in, n_out, n_scr, c_in, c_out = len(in_specs), len(out_shape), len(scratch), len(comm.ins), len(comm.out_shapes)
    any_spec = pl.BlockSpec(memory_space=pl.ANY)

    def outer(*refs):
        ins, c_ins = refs[:n_in], refs[n_in:n_in + c_in]
        o = n_in + c_in
        outs, c_outs = refs[o:o + n_out], refs[o + n_out:o + n_out + c_out]
        o += n_out + c_out
        scr, c_sems = refs[o:o + n_scr], refs[o + n_scr:]
        ids = [pl.program_id(i) for i in range(len(grid))]
        first = functools.reduce(jnp.logical_and, [i == 0 for i in ids])
        last = functools.reduce(jnp.logical_and, [i == g - 1 for i, g in zip(ids, grid)])

        @pl.when(first)
        def _():
            comm.start(c_ins, c_outs, c_sems)

        body(*ins, *outs, *scr)

        @pl.when(last)
        def _():
            comm.finish(c_ins, c_outs, c_sems)

    res = pl.pallas_call(
        outer, name=name, grid=grid, in_specs=list(in_specs) + [any_spec] * c_in,
        out_specs=list(out_specs) + [any_spec] * c_out, out_shape=list(out_shape) + comm.out_shapes,
        scratch_shapes=list(scratch) + comm.sems, compiler_params=_cparams(("arbitrary",) * len(grid)),
    )(*args, *comm.ins)
    comm.set_results(res[n_out:])
    return res[:n_out]


def matmul(name, a, b, mode, out_dtype=f32, bias=None, residual=None):
    if mode == "nn":
        (M, K), (K2, N) = a.shape, b.shape
    elif mode == "nt":
        (M, K), (N, K2) = a.shape, b.shape
    else:
        (K, M), (K2, N) = a.shape, b.shape
    assert K == K2, (name, a.shape, b.shape)
    if mode == "tn":
        tm, tn = M, _pick(N, (512, 256, 128) if M <= 1024 else (256, 128))
        a_spec = pl.BlockSpec((K, M), lambda j: (0, 0))
        b_spec = pl.BlockSpec((K, tn), lambda j: (0, j))
        dims = (((0,), (0,)), ((), ()))
        grid, o_map, row_map = (N // tn,), (lambda j: (0, j)), (lambda j: (0, j))
    else:
        tm, tn = (256 if N >= 2048 else 512), N
        a_spec = pl.BlockSpec((tm, K), lambda i: (i, 0))
        b_spec = pl.BlockSpec(b.shape, lambda i: (0, 0))
        dims = (((1,), (0,)), ((), ())) if mode == "nn" else (((1,), (1,)), ((), ()))
        grid, o_map, row_map = (M // tm,), (lambda i: (i, 0)), (lambda i: (0, 0))
    ins, in_specs = [a, b], [a_spec, b_spec]
    if bias is not None:
        ins.append(bias)
        in_specs.append(pl.BlockSpec((1, tn), row_map))
    if residual is not None:
        ins.append(residual)
        in_specs.append(pl.BlockSpec((tm, tn), o_map))
    has_bias, has_res = bias is not None, residual is not None

    def body(a_ref, b_ref, *rest):
        rest = list(rest)
        bias_ref = rest.pop(0) if has_bias else None
        res_ref = rest.pop(0) if has_res else None
        (o_ref,) = rest
        r = lax.dot_general(a_ref[...].astype(MXU), b_ref[...].astype(MXU), dims, preferred_element_type=f32)
        if has_bias:
            r = r + bias_ref[...]
        if has_res:
            r = r + res_ref[...]
        o_ref[...] = r.astype(out_dtype)

    return pl.pallas_call(
        body, name=name, grid=grid, in_specs=in_specs,
        out_specs=pl.BlockSpec((tm, tn), o_map),
        out_shape=jax.ShapeDtypeStruct((M, N), out_dtype),
        compiler_params=_cparams(("parallel",)),
    )(*ins)


def matmul_tn_stacked(name, a, b, out_dtype):
    R, K, M = a.shape
    N = b.shape[1]
    tn = _pick(N, (256, 128))

    def body(a_ref, b_ref, o_ref):
        o_ref[0] = lax.dot_general(a_ref[0].astype(MXU), b_ref[...].astype(MXU), (((0,), (0,)), ((), ())),
                                   preferred_element_type=f32).astype(out_dtype)

    out = pl.pallas_call(
        body, name=name, grid=(R, N // tn),
        in_specs=[pl.BlockSpec((1, K, M), lambda r, j: (r, 0, 0)), pl.BlockSpec((K, tn), lambda r, j: (0, j))],
        out_specs=pl.BlockSpec((1, M, tn), lambda r, j: (r, 0, j)),
        out_shape=jax.ShapeDtypeStruct((R, M, N), out_dtype),
        compiler_params=_cparams(("parallel", "parallel")),
    )(a, b)
    return out.reshape(R * M, N)


def rowwise(name, fn, rows, pars, outs, accs=(), tile=256, comm=None):
    n_in, n_out = len(rows) + len(pars), len(outs)
    in_specs = [pl.BlockSpec((None, tile, r[0].shape[2]), functools.partial(lambda i, lead: (lead, i, 0), lead=r[1]))
                if isinstance(r, tuple) else pl.BlockSpec((tile, r.shape[1]), lambda i: (i, 0)) for r in rows]
    rows = [r[0] if isinstance(r, tuple) else r for r in rows]
    in_specs += [pl.BlockSpec(p.shape, lambda i: (0, 0)) for p in pars]
    out_specs = [pl.BlockSpec((tile, c), lambda i: (i, 0)) for c, _ in outs]
    out_specs += [pl.BlockSpec(shp, lambda i: (0, 0)) for shp in accs]
    out_shape = [jax.ShapeDtypeStruct((S, c), dt) for c, dt in outs]
    out_shape += [jax.ShapeDtypeStruct(shp, f32) for shp in accs]

    def body(*refs):
        res = fn(*[r[...] for r in refs[:n_in]])
        o_refs = refs[n_in:n_in + n_out]
        a_refs = refs[n_in + n_out:]
        for ref, val in zip(o_refs, res[:n_out]):
            ref[...] = val.astype(ref.dtype)
        if a_refs:
            @pl.when(pl.program_id(0) == 0)
            def _():
                for ref in a_refs:
                    ref[...] = jnp.zeros_like(ref)
            for ref, val in zip(a_refs, res[n_out:]):
                ref[...] += val

    return _call(body, [*rows, *pars], name=name, grid=(S // tile,), in_specs=in_specs, out_specs=out_specs,
                 out_shape=out_shape, sem=("arbitrary",) if accs else ("parallel",), comm=comm)


def _sigmoid(x):
    return 0.5 * jnp.tanh(0.5 * x) + 0.5


def _silu(x):
    return x * _sigmoid(x)


def _dsilu(x):
    sg = _sigmoid(x)
    return sg * (1.0 + x * (1.0 - sg))


def _softplus(x):
    return jnp.maximum(x, 0.0) + jnp.log(1.0 + jnp.exp(-jnp.abs(x)))


def _rms_fwd(x, g):
    r = lax.rsqrt(jnp.mean(x * x, axis=-1, keepdims=True) + EPS)
    return x * r * g


def _rms_bwd(x, g, dh):
    r = lax.rsqrt(jnp.mean(x * x, axis=-1, keepdims=True) + EPS)
    xh = x * r
    dxh = dh * g
    dx = r * (dxh - xh * jnp.mean(dxh * xh, axis=-1, keepdims=True))
    return dx, jnp.sum(dh * xh, axis=0, keepdims=True)


def _taps(x, width):
    row = lax.broadcasted_iota(jnp.int32, (8, x.shape[1]), 0)

    def shifted(s):
        r = pltpu.roll(x, s, 0)
        return jnp.concatenate([jnp.where(row >= s, r[:8], 0.0), r[8:]], axis=0)

    return [shifted(s) for s in range(width - 1, 0, -1)] + [x]


def _conv(x, w, b, taps=None):
    width = w.shape[0]
    taps = _taps(x, width) if taps is None else taps
    out = b + w[0:1, :] * taps[0]
    for k in range(1, width):
        out = out + w[k:k + 1, :] * taps[k]
    return out


def _conv_bwd(x, w, dc, taps=None):
    width, n = w.shape[0], x.shape[0]
    taps = _taps(x, width) if taps is None else taps
    row = lax.broadcasted_iota(jnp.int32, (8, x.shape[1]), 0)
    dx = w[width - 1:width, :] * dc
    for k in range(width - 1):
        s = width - 1 - k
        r = pltpu.roll(dc, n - s, 0)
        dx = dx + w[k:k + 1, :] * jnp.concatenate([r[:n - 8], jnp.where(row < 8 - s, r[n - 8:], 0.0)], axis=0)
    dw = jnp.concatenate([jnp.sum(dc * t, axis=0, keepdims=True) for t in taps], axis=0)
    return dx, dw, jnp.sum(dc, axis=0, keepdims=True)


def _rope_tables(pos, inv_freq):
    ang = pos * inv_freq
    return jnp.cos(ang), jnp.sin(ang)


def _split2(v):
    hi = v.astype(bf16)
    return hi, (v - hi.astype(f32)).astype(bf16)


def _head_maps(width):
    shift = AH.bit_length() - 1
    to_head = (lax.broadcasted_iota(jnp.int32, (width, LANES), 0) >> shift) == lax.broadcasted_iota(jnp.int32, (width, LANES), 1)
    from_head = lax.broadcasted_iota(jnp.int32, (LANES, width), 0) == (lax.broadcasted_iota(jnp.int32, (LANES, width), 1) >> shift)
    return to_head.astype(bf16), from_head.astype(bf16)


def _head_sums(v, to_head):
    hi, lo = _split2(v)
    return jnp.dot(hi, to_head, preferred_element_type=f32) + jnp.dot(lo, to_head, preferred_element_type=f32)


def _head_spread(s, from_head):
    hi, lo = _split2(s)
    return jnp.dot(hi, from_head, preferred_element_type=f32) + jnp.dot(lo, from_head, preferred_element_type=f32)


def _rope_full(cos, sin, width):
    half = AH // 2
    pad = jnp.zeros((cos.shape[0], LANES - half), f32)
    r = lax.broadcasted_iota(jnp.int32, (LANES, width), 0)
    lane = lax.broadcasted_iota(jnp.int32, (LANES, width), 1)
    spread = ((lane & (half - 1)) == r).astype(bf16)
    full = lambda t: _head_spread(jnp.concatenate([t, pad], axis=1), spread)
    first = (lax.broadcasted_iota(jnp.int32, (1, width), 1) & (AH - 1)) < half
    sin_f = full(sin)
    return full(cos), jnp.where(first, -sin_f, sin_f), first


def _swap_halves(v, first):
    half, width = AH // 2, v.shape[1]
    return jnp.where(first, pltpu.roll(v, width - half, 1), pltpu.roll(v, half, 1))


def _headnorm_rope_fwd(x, g, cos, sin, heads):
    to_head, from_head = _head_maps(heads * AH)
    cos_f, sin_s, first = _rope_full(cos, sin, heads * AH)
    r = _head_spread(lax.rsqrt(_head_sums(x * x, to_head) * (1.0 / AH) + EPS), from_head)
    n = x * r * jnp.tile(g, (1, heads))
    return n * cos_f + _swap_halves(n, first) * sin_s


def _headnorm_rope_bwd(x, g, cos, sin, dout, heads):
    width = heads * AH
    to_head, from_head = _head_maps(width)
    cos_f, sin_s, first = _rope_full(cos, sin, width)
    r = _head_spread(lax.rsqrt(_head_sums(x * x, to_head) * (1.0 / AH) + EPS), from_head)
    xh = x * r
    dn = dout * cos_f - _swap_halves(dout, first) * sin_s
    dxh = dn * jnp.tile(g, (1, heads))
    m = _head_spread(_head_sums(dxh * xh, to_head) * (1.0 / AH), from_head)
    dx = r * (dxh - xh * m)
    dg_lanes = jnp.sum(dn * xh, axis=0, keepdims=True)
    fold = ((lax.broadcasted_iota(jnp.int32, (width, LANES), 0) & (AH - 1))
            == lax.broadcasted_iota(jnp.int32, (width, LANES), 1)).astype(f32)
    dg = jnp.dot(jnp.broadcast_to(dg_lanes, (8, width)), fold, precision=lax.Precision.HIGHEST, preferred_element_type=f32)
    return dx, dg[0:1, :AH]


def _ssd_prep(dt_pre, dt_bias, a_log, dt_s, acum_s, acumT_s):
    dt = _softplus(dt_pre + dt_bias)
    a = dt * (-jnp.exp(a_log))
    row = lax.broadcasted_iota(jnp.int32, (CHUNK, CHUNK), 0)
    col = lax.broadcasted_iota(jnp.int32, (CHUNK, CHUNK), 1)
    dt_s[...] = dt
    acum_s[...] = jnp.dot((col <= row).astype(f32), a, precision=lax.Precision.HIGHEST, preferred_element_type=f32)
    acumT_s[...] = lax.dot_general(a, (row <= col).astype(f32), (((0,), (0,)), ((), ())),
                                   precision=lax.Precision.HIGHEST, preferred_element_type=f32)


def _head_cols(h, dt_s, acum_s, acumT_s):
    lane = lax.broadcasted_iota(jnp.int32, (1, LANES), 1)
    oh_l = (lane == h).astype(f32)
    sub = lax.broadcasted_iota(jnp.int32, (LANES, 1), 0)
    oh_s = (sub == h).astype(f32)
    dt_h = jnp.sum(dt_s[...] * oh_l, axis=1, keepdims=True)
    ac_h = jnp.sum(acum_s[...] * oh_l, axis=1, keepdims=True)
    acr_h = jnp.sum(acumT_s[...] * oh_s, axis=0, keepdims=True)
    return oh_l, dt_h, ac_h, acr_h


def ssd_fwd(xs, Bm, Cm, dt_pre, dt_bias, a_log, d_skip, comm=None):
    def body(xs_ref, b_ref, c_ref, dtp_ref, bias_ref, alog_ref, d_ref, y_ref, st_ref, state, dt_s, acum_s, acumT_s):
        c, g = pl.program_id(0), pl.program_id(1)

        @pl.when(g == 0)
        def _():
            _ssd_prep(dtp_ref[...], bias_ref[...], alog_ref[...], dt_s, acum_s, acumT_s)

        row = lax.broadcasted_iota(jnp.int32, (CHUNK, CHUNK), 0)
        col = lax.broadcasted_iota(jnp.int32, (CHUNK, CHUNK), 1)
        causal = col <= row
        Bb, Cb = b_ref[...], c_ref[...]
        cb = lax.dot_general(Cb.astype(MXU), Bb.astype(MXU), (((1,), (1,)), ((), ())), preferred_element_type=f32)
        xs_blk = xs_ref[...]

        @pl.when(c == 0)
        def _():
            for j in range(HPG):
                state[g * HPG + j] = jnp.zeros((NSTATE, HP), f32)

        prevs = [state[g * HPG + j] for j in range(HPG)]
        y_off_all = jnp.dot(Cb.astype(MXU), jnp.concatenate(prevs, axis=1).astype(MXU), preferred_element_type=f32)
        ys, xds, e_ends = [], [], []
        for j in range(HPG):
            oh_l, dt_h, ac_h, acr_h = _head_cols(g * HPG + j, dt_s, acum_s, acumT_s)
            decay = jnp.exp(jnp.where(causal, ac_h - acr_h, -1e30))
            w = (cb * decay).astype(MXU)
            xs_h = xs_blk[:, HP * j:HP * (j + 1)]
            xd = xs_h * dt_h
            y_diag = jnp.dot(w, xd.astype(MXU), preferred_element_type=f32)
            y_off = y_off_all[:, HP * j:HP * (j + 1)] * jnp.exp(ac_h)
            d_h = jnp.sum(d_ref[...] * oh_l, axis=1, keepdims=True)
            ys.append(y_diag + y_off + xs_h * d_h)
            a_end = ac_h[CHUNK - 1:CHUNK, :]
            xds.append(xd * jnp.exp(a_end - ac_h))
            e_ends.append(jnp.exp(a_end))
        s_c = lax.dot_general(Bb.astype(MXU), jnp.concatenate(xds, axis=1).astype(MXU), (((0,), (0,)), ((), ())),
                              preferred_element_type=f32)
        for j in range(HPG):
            st_ref[0, j] = prevs[j]
            state[g * HPG + j] = prevs[j] * e_ends[j] + s_c[:, HP * j:HP * (j + 1)]
        y_ref[...] = jnp.concatenate(ys, axis=1)

    par = pl.BlockSpec((1, LANES), lambda c, g: (0, 0))
    return _call(
        body, [xs, Bm, Cm, dt_pre, dt_bias, a_log, d_skip], comm=comm, name="ssd_fwd", grid=(NCHUNK, GROUPS),
        in_specs=[pl.BlockSpec((CHUNK, HPG * HP), lambda c, g: (c, g)),
                  pl.BlockSpec((CHUNK, NSTATE), lambda c, g: (c, g)),
                  pl.BlockSpec((CHUNK, NSTATE), lambda c, g: (c, g)),
                  pl.BlockSpec((CHUNK, LANES), lambda c, g: (c, 0)), par, par, par],
        out_specs=[pl.BlockSpec((CHUNK, HPG * HP), lambda c, g: (c, g)),
                   pl.BlockSpec((1, HPG, NSTATE, HP), lambda c, g: (c, g, 0, 0))],
        out_shape=[jax.ShapeDtypeStruct((S, INNER), f32), jax.ShapeDtypeStruct((NCHUNK, HEADS, NSTATE, HP), f32)],
        scratch=[pltpu.VMEM((HEADS, NSTATE, HP), f32), pltpu.VMEM((CHUNK, LANES), f32),
                 pltpu.VMEM((CHUNK, LANES), f32), pltpu.VMEM((LANES, CHUNK), f32)],
        sem=("arbitrary", "arbitrary"))


def ssd_bwd(xs, Bm, Cm, dt_pre, dt_bias, a_log, d_skip, states, dy, comm=None):
    rev = lambda c: NCHUNK - 1 - c

    def body(xs_ref, b_ref, c_ref, dtp_ref, bias_ref, alog_ref, d_ref, st_ref, dy_ref,
             dxs_ref, db_ref, dc_ref, ddt_ref, dbias_ref, dalog_ref, dd_ref,
             dstate, dt_s, acum_s, acumT_s, dacum_s, ddt_s, da_s):
        c, g = pl.program_id(0), pl.program_id(1)

        @pl.when(g == 0)
        def _():
            _ssd_prep(dtp_ref[...], bias_ref[...], alog_ref[...], dt_s, acum_s, acumT_s)
            dacum_s[...] = jnp.zeros_like(dacum_s)
            ddt_s[...] = jnp.zeros_like(ddt_s)

        @pl.when((c == 0) & (g == 0))
        def _():
            da_s[...] = jnp.zeros_like(da_s)
            dd_ref[...] = jnp.zeros_like(dd_ref)
            dbias_ref[...] = jnp.zeros_like(dbias_ref)
            dalog_ref[...] = jnp.zeros_like(dalog_ref)

        row = lax.broadcasted_iota(jnp.int32, (CHUNK, CHUNK), 0)
        col = lax.broadcasted_iota(jnp.int32, (CHUNK, CHUNK), 1)
        sub_l = lax.broadcasted_iota(jnp.int32, (CHUNK, 1), 0)
        last = (sub_l == CHUNK - 1).astype(f32)
        nt = (((1,), (1,)), ((), ()))
        tn = (((0,), (0,)), ((), ()))
        Bb, Cb = b_ref[...], c_ref[...]
        Bm_, Cm_ = Bb.astype(MXU), Cb.astype(MXU)
        cb = lax.dot_general(Cm_, Bm_, nt, preferred_element_type=f32)
        bc = lax.dot_general(Bm_, Cm_, nt, preferred_element_type=f32)
        xs_blk, dy_blk = xs_ref[...], dy_ref[...]
        dxs, dB, dC = [], jnp.zeros((CHUNK, NSTATE), f32), jnp.zeros((CHUNK, NSTATE), f32)
        for j in range(HPG):
            h = g * HPG + j
            oh_l, dt_h, ac_h, acr_h = _head_cols(h, dt_s, acum_s, acumT_s)

            @pl.when(c == 0)
            def _():
                dstate[h] = jnp.zeros((NSTATE, HP), f32)

            dnext = dstate[h]
            prev = st_ref[0, j]
            lm = jnp.exp(jnp.where(col <= row, ac_h - acr_h, -1e30))
            lmT = jnp.exp(jnp.where(row <= col, acr_h - ac_h, -1e30))
            xs_h = xs_blk[:, HP * j:HP * (j + 1)]
            dy_h = dy_blk[:, HP * j:HP * (j + 1)]
            xd = xs_h * dt_h
            xdm, dym = xd.astype(MXU), dy_h.astype(MXU)
            ea = jnp.exp(ac_h)
            a_end = ac_h[CHUNK - 1:CHUNK, :]
            e_end = jnp.exp(a_end)
            dte = jnp.exp(a_end - ac_h)
            dnm, pvm = dnext.astype(MXU), prev.astype(MXU)
            bd = jnp.dot(Bm_, dnm, preferred_element_type=f32)
            dxd = jnp.dot((bc * lmT).astype(MXU), dym, preferred_element_type=f32) + dte * bd
            dw = lax.dot_general(dym, xdm, nt, preferred_element_type=f32)
            dwT = lax.dot_general(xdm, dym, nt, preferred_element_type=f32)
            dcb = dw * lm
            dbc = dwT * lmT
            eady = (ea * dy_h).astype(MXU)
            dC = dC + jnp.dot(dcb.astype(MXU), Bm_, preferred_element_type=f32) \
                + lax.dot_general(eady, pvm, nt, preferred_element_type=f32)
            dB = dB + jnp.dot(dbc.astype(MXU), Cm_, preferred_element_type=f32) \
                + dte * lax.dot_general(xdm, dnm, nt, preferred_element_type=f32)
            dstate[h] = lax.dot_general(Cm_, eady, tn, preferred_element_type=f32) + e_end * dnext
            r1 = jnp.sum(dcb * cb, axis=1, keepdims=True)
            r2 = jnp.sum(dbc * bc, axis=1, keepdims=True)
            y_off = jnp.dot(Cm_, pvm, preferred_element_type=f32) * ea
            t3 = jnp.sum(dy_h * y_off, axis=1, keepdims=True)
            t4 = jnp.sum(bd * xd, axis=1, keepdims=True) * dte
            end_extra = jnp.sum(t4, axis=0, keepdims=True) + e_end * jnp.sum(jnp.sum(prev * dnext, axis=1, keepdims=True), axis=0, keepdims=True)
            dacum_h = r1 - r2 + t3 - t4 + last * end_extra
            dacum_s[...] += dacum_h * oh_l
            ddt_s[...] += jnp.sum(dxd * xs_h, axis=1, keepdims=True) * oh_l
            d_h = jnp.sum(d_ref[...] * oh_l, axis=1, keepdims=True)
            dxs.append(dxd * dt_h + dy_h * d_h)
            dd_ref[...] += oh_l * jnp.sum(jnp.sum(dy_h * xs_h, axis=1, keepdims=True), axis=0, keepdims=True)
        dxs_ref[...] = jnp.concatenate(dxs, axis=1)
        db_ref[...] = dB
        dc_ref[...] = dC

        @pl.when(g == GROUPS - 1)
        def _():
            a_row = -jnp.exp(alog_ref[...])
            da = jnp.dot((row <= col).astype(f32), dacum_s[...], precision=lax.Precision.HIGHEST, preferred_element_type=f32)
            da_s[...] += jnp.sum(da * dt_s[...], axis=0, keepdims=True)
            z = dtp_ref[...] + bias_ref[...]
            ddt_pre = (ddt_s[...] + da * a_row) * _sigmoid(z)
            ddt_ref[...] = ddt_pre.astype(ddt_ref.dtype)
            dbias_ref[...] += jnp.sum(ddt_pre, axis=0, keepdims=True)

            @pl.when(c == NCHUNK - 1)
            def _():
                dalog_ref[...] = da_s[...] * a_row

    par = pl.BlockSpec((1, LANES), lambda c, g: (0, 0))
    return _call(
        body, [xs, Bm, Cm, dt_pre, dt_bias, a_log, d_skip, states, dy], comm=comm, name="ssd_bwd", grid=(NCHUNK, GROUPS),
        in_specs=[pl.BlockSpec((CHUNK, HPG * HP), lambda c, g: (rev(c), g)),
                  pl.BlockSpec((CHUNK, NSTATE), lambda c, g: (rev(c), g)),
                  pl.BlockSpec((CHUNK, NSTATE), lambda c, g: (rev(c), g)),
                  pl.BlockSpec((CHUNK, LANES), lambda c, g: (rev(c), 0)), par, par, par,
                  pl.BlockSpec((1, HPG, NSTATE, HP), lambda c, g: (rev(c), g, 0, 0)),
                  pl.BlockSpec((CHUNK, HPG * HP), lambda c, g: (rev(c), g))],
        out_specs=[pl.BlockSpec((CHUNK, HPG * HP), lambda c, g: (rev(c), g)),
                   pl.BlockSpec((CHUNK, NSTATE), lambda c, g: (rev(c), g)),
                   pl.BlockSpec((CHUNK, NSTATE), lambda c, g: (rev(c), g)),
                   pl.BlockSpec((CHUNK, LANES), lambda c, g: (rev(c), 0)), par, par, par],
        out_shape=[jax.ShapeDtypeStruct((S, INNER), f32), jax.ShapeDtypeStruct((S, GROUPS * NSTATE), f32),
                   jax.ShapeDtypeStruct((S, GROUPS * NSTATE), f32), jax.ShapeDtypeStruct((S, LANES), MXU),
                   jax.ShapeDtypeStruct((1, LANES), f32), jax.ShapeDtypeStruct((1, LANES), f32),
                   jax.ShapeDtypeStruct((1, LANES), f32)],
        scratch=[pltpu.VMEM((HEADS, NSTATE, HP), f32), pltpu.VMEM((CHUNK, LANES), f32),
                 pltpu.VMEM((CHUNK, LANES), f32), pltpu.VMEM((LANES, CHUNK), f32),
                 pltpu.VMEM((CHUNK, LANES), f32), pltpu.VMEM((CHUNK, LANES), f32), pltpu.VMEM((1, LANES), f32)],
        sem=("arbitrary", "arbitrary"))


ATT_STACK_FWD, ATT_STACK_BWD = 4, 2


def _attn_kv(kp, kc, vp, vc, hk):
    sl = slice(AH * hk, AH * (hk + 1))
    return (jnp.concatenate([kp[:, sl], kc[:, sl]], axis=0).astype(MXU),
            jnp.concatenate([vp[:, sl], vc[:, sl]], axis=0).astype(MXU))


def _stack_heads(x, heads):
    return jnp.concatenate([x[:, AH * h:AH * (h + 1)] for h in heads], axis=0)


def _attn_block(n, q, kb, sinks, heads):
    rows = len(heads) * WIN
    qi = lax.broadcasted_iota(jnp.int32, (rows, 2 * WIN), 0) & (WIN - 1)
    ki = lax.broadcasted_iota(jnp.int32, (rows, 2 * WIN), 1)
    rel = qi + WIN - ki
    mask = (rel >= 0) & (rel < WIN) & ((ki >= WIN) | (n > 0))
    qg = _stack_heads(q, heads).astype(MXU)
    s = lax.dot_general(qg, kb, (((1,), (1,)), ((), ())), preferred_element_type=f32) * (AH ** -0.5)
    s = jnp.where(mask, s, -1e30)
    sink = jnp.concatenate([jnp.broadcast_to(sinks[:, h:h + 1], (WIN, 1)) for h in heads], axis=0)
    m = jnp.maximum(jnp.max(s, axis=1, keepdims=True), sink)
    p = jnp.exp(s - m)
    ps = jnp.exp(sink - m)
    inv = 1.0 / (jnp.sum(p, axis=1, keepdims=True) + ps)
    return qg, p * inv, ps * inv


def _head_blocks(hk, stack):
    return [list(range(QPK * hk + i, QPK * hk + i + stack)) for i in range(0, QPK, stack)]


def _kv_specs():
    prev = lambda n: (jnp.maximum(n - 1, 0), 0)
    cur = lambda n: (n, 0)
    w = KVH * AH
    return [pl.BlockSpec((WIN, w), prev), pl.BlockSpec((WIN, w), cur), pl.BlockSpec((WIN, w), prev), pl.BlockSpec((WIN, w), cur)]


def attn_fwd(q, k, v, sinks, comm=None):
    def body(q_ref, kp_ref, kc_ref, vp_ref, vc_ref, s_ref, o_ref):
        n = pl.program_id(0)
        q_, kp, kc, vp, vc, sk = q_ref[...], kp_ref[...], kc_ref[...], vp_ref[...], vc_ref[...], s_ref[...]
        outs = []
        for hk in range(KVH):
            kb, vb = _attn_kv(kp, kc, vp, vc, hk)
            for heads in _head_blocks(hk, ATT_STACK_FWD):
                _, pr, _ = _attn_block(n, q_, kb, sk, heads)
                o = jnp.dot(pr.astype(MXU), vb, preferred_element_type=f32)
                outs += [o[WIN * i:WIN * (i + 1)] for i in range(len(heads))]
        o_ref[...] = jnp.concatenate(outs, axis=1)

    return _call(
        body, [q, k, k, v, v, sinks], comm=comm, name="attn_fwd", grid=(NBLK,),
        in_specs=[pl.BlockSpec((WIN, D), lambda n: (n, 0))] + _kv_specs() + [pl.BlockSpec((1, QH), lambda n: (0, 0))],
        out_specs=[pl.BlockSpec((WIN, D), lambda n: (n, 0))],
        out_shape=[jax.ShapeDtypeStruct((S, D), f32)], sem=("parallel",))[0]


def attn_bwd(q, k, v, sinks, dout, comm=None):
    def body(q_ref, kp_ref, kc_ref, vp_ref, vc_ref, s_ref, do_ref, dq_ref, dkp_ref, dkc_ref, dvp_ref, dvc_ref, ds_ref):
        n = pl.program_id(0)

        @pl.when(n == 0)
        def _():
            ds_ref[...] = jnp.zeros_like(ds_ref)

        q_, kp, kc, vp, vc, sk, do = q_ref[...], kp_ref[...], kc_ref[...], vp_ref[...], vc_ref[...], s_ref[...], do_ref[...]
        lane = lax.broadcasted_iota(jnp.int32, (1, QH), 1)
        nt = (((1,), (1,)), ((), ()))
        tn = (((0,), (0,)), ((), ()))
        dqs, dkps, dkcs, dvps, dvcs = [], [], [], [], []
        dsink = jnp.zeros((1, QH), f32)
        for hk in range(KVH):
            kb, vb = _attn_kv(kp, kc, vp, vc, hk)
            dkb, dvb = jnp.zeros((2 * WIN, AH), f32), jnp.zeros((2 * WIN, AH), f32)
            for heads in _head_blocks(hk, ATT_STACK_BWD):
                qg, pr, prs = _attn_block(n, q_, kb, sk, heads)
                dog = _stack_heads(do, heads).astype(MXU)
                dp = lax.dot_general(dog, vb, nt, preferred_element_type=f32)
                dvb = dvb + lax.dot_general(pr.astype(MXU), dog, tn, preferred_element_type=f32)
                delta = jnp.sum(pr * dp, axis=1, keepdims=True)
                ds = (pr * (dp - delta)).astype(MXU)
                dsk = -prs * delta
                for i, h in enumerate(heads):
                    dsink = dsink + jnp.sum(dsk[WIN * i:WIN * (i + 1)], axis=0, keepdims=True) * (lane == h).astype(f32)
                dqg = jnp.dot(ds, kb, preferred_element_type=f32) * (AH ** -0.5)
                dkb = dkb + lax.dot_general(ds, qg, tn, preferred_element_type=f32) * (AH ** -0.5)
                dqs += [dqg[WIN * i:WIN * (i + 1)] for i in range(len(heads))]
            dkps.append(dkb[:WIN])
            dkcs.append(dkb[WIN:])
            dvps.append(dvb[:WIN])
            dvcs.append(dvb[WIN:])
        dq_ref[...] = jnp.concatenate(dqs, axis=1)
        dkp_ref[...] = jnp.concatenate(dkps, axis=1)
        dkc_ref[...] = jnp.concatenate(dkcs, axis=1)
        dvp_ref[...] = jnp.concatenate(dvps, axis=1)
        dvc_ref[...] = jnp.concatenate(dvcs, axis=1)
        ds_ref[...] += dsink

    w = KVH * AH
    blk = lambda width: pl.BlockSpec((WIN, width), lambda n: (n, 0))
    return _call(
        body, [q, k, k, v, v, sinks, dout], comm=comm, name="attn_bwd", grid=(NBLK,),
        in_specs=[blk(D)] + _kv_specs() + [pl.BlockSpec((1, QH), lambda n: (0, 0)), blk(D)],
        out_specs=[blk(D), blk(w), blk(w), blk(w), blk(w), pl.BlockSpec((1, QH), lambda n: (0, 0))],
        out_shape=[jax.ShapeDtypeStruct((S, D), f32)] + [jax.ShapeDtypeStruct((S, w), f32)] * 4 + [jax.ShapeDtypeStruct((1, QH), f32)],
        sem=("arbitrary",))


def kv_bwd(kv, pos, inv_freq, k_norm, dkp, dkc, dvp, dvc):
    w = KVH * AH

    def body(kv_ref, pos_ref, if_ref, g_ref, dkp_ref, dkc_ref, dvp_ref, dvc_ref, o_ref, dg_ref, db_ref):
        n = pl.program_id(0)

        @pl.when(n == 0)
        def _():
            dg_ref[...] = jnp.zeros_like(dg_ref)
            db_ref[...] = jnp.zeros_like(db_ref)

        inside = (n < NBLK - 1).astype(f32)
        dk = dkc_ref[...] + inside * dkp_ref[...]
        dv = dvc_ref[...] + inside * dvp_ref[...]
        cos, sin = _rope_tables(pos_ref[...], if_ref[...])
        dkpre, dg = _headnorm_rope_bwd(kv_ref[...], g_ref[...], cos, sin, dk, KVH)
        dkv = jnp.concatenate([dkpre, dv], axis=1)
        o_ref[...] = dkv.astype(o_ref.dtype)
        dg_ref[...] += dg
        db_ref[...] += jnp.sum(dkv, axis=0, keepdims=True)

    nxt = lambda n: (jnp.minimum(n + 1, NBLK - 1), 0)
    cur = lambda n: (n, 0)
    const = lambda n: (0, 0)
    return pl.pallas_call(
        body, name="kv_bwd", grid=(NBLK,),
        in_specs=[pl.BlockSpec((WIN, w), cur), pl.BlockSpec((WIN, 1), cur), pl.BlockSpec((1, AH // 2), const),
                  pl.BlockSpec((1, AH), const), pl.BlockSpec((WIN, w), nxt), pl.BlockSpec((WIN, w), cur),
                  pl.BlockSpec((WIN, w), nxt), pl.BlockSpec((WIN, w), cur)],
        out_specs=[pl.BlockSpec((WIN, 2 * w), cur), pl.BlockSpec((1, AH), const), pl.BlockSpec((1, 2 * w), const)],
        out_shape=[jax.ShapeDtypeStruct((S, 2 * w), MXU), jax.ShapeDtypeStruct((1, AH), f32), jax.ShapeDtypeStruct((1, 2 * w), f32)],
        compiler_params=_cparams(("arbitrary",)),
    )(kv, pos, inv_freq, k_norm, dkp, dkc, dvp, dvc)


def _adam_math(w, g, m, v):
    m = ADAM_B1 * m + (1.0 - ADAM_B1) * g
    v = ADAM_B2 * v + (1.0 - ADAM_B2) * (g * g)
    m_hat = m / (1.0 - ADAM_B1 ** ADAM_STEP)
    v_hat = v / (1.0 - ADAM_B2 ** ADAM_STEP)
    return -ADAM_LR * (m_hat / (jnp.sqrt(v_hat) + ADAM_EPS) + ADAM_WD * w), m, v


def adamw_layers(name, w, g0, g1, m, v):
    _, R, C = w.shape
    tr = _pick(R, (256, 128, 64, 32, 16, 8))

    def body(w_ref, g0_ref, g1_ref, m_ref, v_ref, g_ref, d_ref, nm_ref, nv_ref):
        g_ = jnp.where(pl.program_id(0) == 0, g0_ref[...], g1_ref[...])
        g_ref[0] = g_
        d_ref[0], nm_ref[0], nv_ref[0] = _adam_math(w_ref[0], g_, m_ref[0], v_ref[0])

    st = pl.BlockSpec((1, tr, C), lambda l, i: (l, i, 0))
    fl = pl.BlockSpec((tr, C), lambda l, i: (i, 0))
    return pl.pallas_call(
        body, name=name, grid=(2, R // tr), in_specs=[st, fl, fl, st, st], out_specs=[st] * 4,
        out_shape=[jax.ShapeDtypeStruct((2, R, C), f32)] * 4, compiler_params=_cparams(("parallel", "parallel")),
    )(w, g0, g1, m, v)


def adamw(name, w, g, m, v):
    R, C = w.shape
    tr = _pick(R, (256, 128, 64, 32, 16, 8))
    tc = C if tr < R or C % 256 else 256

    def body(w_ref, g_ref, m_ref, v_ref, d_ref, nm_ref, nv_ref):
        d_ref[...], nm_ref[...], nv_ref[...] = _adam_math(w_ref[...], g_ref[...], m_ref[...], v_ref[...])

    spec = pl.BlockSpec((tr, tc), lambda i, j: (i, j))
    return pl.pallas_call(
        body, name=name, grid=(R // tr, C // tc), in_specs=[spec] * 4, out_specs=[spec] * 3,
        out_shape=[jax.ShapeDtypeStruct((R, C), f32)] * 3, compiler_params=_cparams(("parallel", "parallel")),
    )(w, g, m, v)


def _me():
    return lax.axis_index("x"), lax.axis_index("y"), lax.axis_index("c")


def gather_comm(xs):
    n = len(xs)

    def parts(x_refs, o_refs, sems):
        send_sems, recv_sems, local_sems = sems
        x, y, c = _me()
        me, sibling = (x, y, c), (x, y, 1 - c)
        chips = [(1 - x, y), (x, 1 - y), (1 - x, 1 - y)]

        def copy(a, k, block, to, src=None):
            dst = o_refs[a].at[4 * block[0] + 2 * block[1] + block[2]]
            return pltpu.make_async_remote_copy(
                src_ref=dst if src is None else src, dst_ref=dst,
                send_sem=send_sems.at[7 * a + k], recv_sem=recv_sems.at[7 * a + k], device_id=to, device_id_type=MESH)

        mine = [pltpu.make_async_copy(x_refs[a], o_refs[a].at[4 * x + 2 * y + c], local_sems.at[a]) for a in range(n)]
        first = []
        for a in range(n):
            first.append(copy(a, 0, me, sibling, src=x_refs[a]))
            first += [copy(a, 1 + j, me, (*chip, c), src=x_refs[a]) for j, chip in enumerate(chips)]
        return copy, mine, first, me, sibling, chips, c

    def start(x_refs, o_refs, sems):
        _, mine, first, *_ = parts(x_refs, o_refs, sems)
        for cp in mine + first:
            cp.start()

    def finish(x_refs, o_refs, sems):
        copy, mine, first, me, sibling, chips, c = parts(x_refs, o_refs, sems)
        passed = []
        for j, chip in enumerate(chips):
            for a in range(n):
                copy(a, 1 + j, (*chip, c), me).wait_recv()
                cp = copy(a, 4 + j, (*chip, c), sibling)
                cp.start()
                passed.append(cp)
        for a in range(n):
            copy(a, 0, sibling, me).wait_recv()
            for j, chip in enumerate(chips):
                copy(a, 4 + j, (*chip, 1 - c), me).wait_recv()
        for cp in first + passed:
            cp.wait_send()
        for cp in mine:
            cp.wait()

    return Comm(xs, [jax.ShapeDtypeStruct((N_DEV,) + a.shape, a.dtype) for a in xs],
                [pltpu.SemaphoreType.DMA((7 * n,)), pltpu.SemaphoreType.DMA((7 * n,)), pltpu.SemaphoreType.DMA((n,))], start, finish)


def run_comm(name, comm):
    _call(lambda: None, [], name=name, grid=(1,), in_specs=[], out_specs=[], out_shape=[], comm=comm)
    return comm.results


def sibling_comm(gs):
    n = len(gs)

    def copies(g_refs, o_refs, sems):
        x, y, c = _me()
        return [pltpu.make_async_remote_copy(
            src_ref=g_refs[a].at[:, 1 - c], dst_ref=o_refs[a], send_sem=sems[0].at[a], recv_sem=sems[1].at[a],
            device_id=(x, y, 1 - c), device_id_type=MESH) for a in range(n)]

    def start(g_refs, o_refs, sems):
        for cp in copies(g_refs, o_refs, sems):
            cp.start()

    def finish(g_refs, o_refs, sems):
        for cp in copies(g_refs, o_refs, sems):
            cp.wait()

    return Comm(gs, [jax.ShapeDtypeStruct((4,) + g.shape[2:], g.dtype) for g in gs],
                [pltpu.SemaphoreType.DMA((n,)), pltpu.SemaphoreType.DMA((n,))], start, finish)


def chip_comm(ts):
    n = len(ts)

    def copies(t_refs, o_refs, sems):
        x, y, c = _me()
        chips = [(1 - x, y), (x, 1 - y), (1 - x, 1 - y)]
        return [pltpu.make_async_remote_copy(
            src_ref=t_refs[a].at[2 * px + py], dst_ref=o_refs[a].at[j],
            send_sem=sems[0].at[3 * a + j], recv_sem=sems[1].at[3 * a + j],
            device_id=(px, py, c), device_id_type=MESH) for j, (px, py) in enumerate(chips) for a in range(n)]

    def start(t_refs, o_refs, sems):
        for cp in copies(t_refs, o_refs, sems):
            cp.start()

    def finish(t_refs, o_refs, sems):
        for cp in copies(t_refs, o_refs, sems):
            cp.wait()

    return Comm(ts, [jax.ShapeDtypeStruct((3,) + t.shape[1:], t.dtype) for t in ts],
                [pltpu.SemaphoreType.DMA((3 * n,)), pltpu.SemaphoreType.DMA((3 * n,))], start, finish)


def _row_tile(rows):
    return _pick(rows, (512, 304, 256, 128))


def pair_add(name, g, r):
    _, _, R, C = g.shape
    tr = _row_tile(R)

    def body(c_ref, g_ref, r_ref, o_ref):
        o_ref[0] = (g_ref[0, 0].astype(f32) + r_ref[0].astype(f32)).astype(o_ref.dtype)

    return pl.pallas_call(
        body, name=name,
        grid_spec=pltpu.PrefetchScalarGridSpec(
            num_scalar_prefetch=1, grid=(4, R // tr),
            in_specs=[pl.BlockSpec((1, 1, tr, C), lambda p, i, c: (p, c[0], i, 0)),
                      pl.BlockSpec((1, tr, C), lambda p, i, c: (p, i, 0))],
            out_specs=pl.BlockSpec((1, tr, C), lambda p, i, c: (p, i, 0))),
        out_shape=jax.ShapeDtypeStruct((4, R, C), g.dtype),
        compiler_params=_cparams(("parallel", "parallel")),
    )(lax.axis_index("c").reshape(1).astype(jnp.int32), g, r)


def final_add(name, t, r):
    _, R, C = t.shape
    tr = _row_tile(R)

    def body(p_ref, t_ref, r_ref, o_ref):
        o_ref[...] = ((t_ref[0].astype(f32) + r_ref[0].astype(f32)) + r_ref[1].astype(f32)) + r_ref[2].astype(f32)

    chip = 2 * lax.axis_index("x") + lax.axis_index("y")
    return pl.pallas_call(
        body, name=name,
        grid_spec=pltpu.PrefetchScalarGridSpec(
            num_scalar_prefetch=1, grid=(R // tr,),
            in_specs=[pl.BlockSpec((1, tr, C), lambda i, p: (p[0], i, 0)),
                      pl.BlockSpec((3, tr, C), lambda i, p: (0, i, 0))],
            out_specs=pl.BlockSpec((tr, C), lambda i, p: (i, 0))),
        out_shape=jax.ShapeDtypeStruct((R, C), f32),
        compiler_params=_cparams(("parallel",)),
    )(chip.reshape(1).astype(jnp.int32), t, r)


class ReduceScatter:
    def __init__(self, tag, keys, grads):
        self.tag, self.keys, self.grads = tag, keys, grads
        self.send = [g.reshape((4, 2, g.shape[0] // N_DEV) + g.shape[1:]) for g in grads]

    def sibling(self):
        self.c1 = sibling_comm(self.send)
        return self.c1

    def chips(self):
        self.pairs = [pair_add(f"rs_pair_add_{self.tag}{i}", g, r) for i, (g, r) in enumerate(zip(self.send, self.c1.results))]
        self.c2 = chip_comm(self.pairs)
        return self.c2

    def sums(self):
        return {k: final_add(f"rs_final_add_{k}", t, r) for k, t, r in zip(self.keys, self.pairs, self.c2.results)}


IN_ROWS = {"z": (0, 2048), "xs": (2048, 4096), "B": (4096, 5120), "C": (5120, 6144)}
IN_COLS = 2 * INNER + 2 * GROUPS * NSTATE + HEADS


def sum_devices(g):
    def body(g_ref, o_ref):
        acc = g_ref[0]
        for i in range(1, N_DEV):
            acc = acc + g_ref[i]
        o_ref[...] = acc

    return pl.pallas_call(body, name="sum_devices", out_shape=jax.ShapeDtypeStruct(g.shape[1:], f32),
                          compiler_params=_cparams())(g)


def _pack(parts, unit, dtype, lead=()):
    flat = jnp.concatenate([p.reshape(lead + (-1,)).astype(dtype) for p in parts], axis=-1)
    n = flat.shape[-1]
    rows = -(-n // (unit * PACK_W)) * unit
    flat = jnp.pad(flat, [(0, 0)] * len(lead) + [(0, rows * PACK_W - n)])
    return flat.reshape(lead + (rows, PACK_W))


def _unpack(buf, shapes, lead=()):
    flat = buf.reshape(lead + (-1,))
    out, off = [], 0
    for shp in shapes:
        n = math.prod(shp)
        out.append(flat[..., off:off + n].reshape(lead + tuple(shp)))
        off += n
    return out


def _pad_lanes(a):
    return jnp.pad(a, [(0, 0)] * (a.ndim - 1) + [(0, LANES - a.shape[-1])])


_NN = (((1,), (0,)), ((), ()))
_NT = (((1,), (1,)), ((), ()))


def _mm(a, b, dims):
    return lax.dot_general(a.astype(MXU), b.astype(MXU), dims, preferred_element_type=f32)


def _ffn_fwd(tag, x, norm_g, w_inT, conv_w, conv_b, mid_comm=None):
    (h,) = rowwise(f"{tag}_norm", lambda x_, g_: (_rms_fwd(x_, g_),), [x], [norm_g], [(D, MXU)])
    ct, nblk = FFN_CT, FFN // FFN_CT

    def body(h_ref, wg_ref, wv_ref, cw_ref, cb_ref, gp_ref, v_ref, a_ref):
        h_ = h_ref[...]
        gp, v_ = _mm(h_, wg_ref[...], _NT), _mm(h_, wv_ref[...], _NT)
        gp_ref[...] = gp
        v_ref[...] = v_
        a_ref[...] = (_silu(_conv(gp, cw_ref[...], cb_ref[...])) * v_).astype(a_ref.dtype)

    col = pl.BlockSpec((S, ct), lambda j: (0, j))
    gate_pre, val, act = _call(
        body, [h, w_inT, w_inT, conv_w, conv_b], comm=mid_comm, name=f"{tag}_in", grid=(nblk,),
        in_specs=[pl.BlockSpec((S, D), lambda j: (0, 0)), pl.BlockSpec((ct, D), lambda j: (j, 0)),
                  pl.BlockSpec((ct, D), lambda j: (nblk + j, 0)), pl.BlockSpec((CONV_F, ct), lambda j: (0, j)),
                  pl.BlockSpec((1, ct), lambda j: (0, j))],
        out_specs=[col, col, col],
        out_shape=[jax.ShapeDtypeStruct((S, FFN), f32), jax.ShapeDtypeStruct((S, FFN), f32), jax.ShapeDtypeStruct((S, FFN), MXU)],
        sem=("parallel",))
    return act, (x, h, gate_pre, val, act)


FFN_CT = 256
CONV_CT = 256


def _proj_conv(name, h, wT, row0, cw, cb, comm=None):
    C, ct = cw.shape[1], CONV_CT

    def body(h_ref, w_ref, cw_ref, cb_ref, p_ref, c_ref):
        p = _mm(h_ref[...], w_ref[...], _NT)
        p_ref[...] = p
        c_ref[...] = _silu(_conv(p, cw_ref[...], cb_ref[...]))

    col = pl.BlockSpec((S, ct), lambda j: (0, j))
    return _call(
        body, [h, wT, cw, cb], comm=comm, name=name, grid=(C // ct,),
        in_specs=[pl.BlockSpec((S, D), lambda j: (0, 0)), pl.BlockSpec((ct, D), lambda j: (row0 // ct + j, 0)),
                  pl.BlockSpec((CONV_A, ct), lambda j: (0, j)), pl.BlockSpec((1, ct), lambda j: (0, j))],
        out_specs=[col, col], out_shape=[jax.ShapeDtypeStruct((S, C), f32)] * 2, sem=("parallel",))


def _dconv_wgrad(name, pre, dconv, cw, cb, h):
    C, ct = cw.shape[1], CONV_CT

    def body(p_ref, do_ref, cw_ref, cb_ref, h_ref, dp_ref, g_ref, dw_ref, db_ref):
        p_, w_ = p_ref[...], cw_ref[...]
        taps = _taps(p_, CONV_A)
        dx, dw, db = _conv_bwd(p_, w_, do_ref[...] * _dsilu(_conv(p_, w_, cb_ref[...], taps)), taps)
        dpm = dx.astype(MXU)
        dp_ref[...] = dpm
        g_ref[...] = lax.dot_general(dpm, h_ref[...].astype(MXU), (((0,), (0,)), ((), ())),
                                     preferred_element_type=f32).astype(g_ref.dtype)
        dw_ref[...] = dw
        db_ref[...] = db

    col = pl.BlockSpec((S, ct), lambda j: (0, j))
    return _call(
        body, [pre, dconv, cw, cb, h], name=name, grid=(C // ct,),
        in_specs=[col, col, pl.BlockSpec((CONV_A, ct), lambda j: (0, j)), pl.BlockSpec((1, ct), lambda j: (0, j)),
                  pl.BlockSpec((S, D), lambda j: (0, 0))],
        out_specs=[col, pl.BlockSpec((ct, D), lambda j: (j, 0)), pl.BlockSpec((CONV_A, ct), lambda j: (0, j)),
                   pl.BlockSpec((1, ct), lambda j: (0, j))],
        out_shape=[jax.ShapeDtypeStruct((S, C), MXU), jax.ShapeDtypeStruct((C, D), MXU),
                   jax.ShapeDtypeStruct((CONV_A, C), f32), jax.ShapeDtypeStruct((1, C), f32)],
        sem=("parallel",))


def _ffn_mid_bwd(name, dout, w_down, gate_pre, val, conv_w, conv_b, comm=None):
    ct = FFN_CT

    def body(do_ref, wd_ref, gp_ref, v_ref, w_ref, b_ref, dgv_ref, dw_ref, db_ref, dob_s):
        @pl.when(pl.program_id(0) == 0)
        def _():
            dob_s[...] = do_ref[...].astype(MXU)

        da = _mm(dob_s[...], wd_ref[...], _NT)
        gp, v_, w_ = gp_ref[...], v_ref[...], w_ref[...]
        taps = _taps(gp, CONV_F)
        gate = _conv(gp, w_, b_ref[...], taps)
        sg = _sigmoid(gate)
        dgp, dw, db = _conv_bwd(gp, w_, da * v_ * (sg * (1.0 + gate * (1.0 - sg))), taps)
        dgv_ref[0] = dgp.astype(dgv_ref.dtype)
        dgv_ref[1] = (da * (gate * sg)).astype(dgv_ref.dtype)
        dw_ref[...] = dw
        db_ref[...] = db

    col = pl.BlockSpec((S, ct), lambda j: (0, j))
    return _call(
        body, [dout, w_down, gate_pre, val, conv_w, conv_b], comm=comm, name=name, grid=(FFN // ct,),
        in_specs=[pl.BlockSpec((S, D), lambda j: (0, 0)), pl.BlockSpec((ct, D), lambda j: (j, 0)), col, col,
                  pl.BlockSpec((CONV_F, ct), lambda j: (0, j)), pl.BlockSpec((1, ct), lambda j: (0, j))],
        out_specs=[pl.BlockSpec((2, S, ct), lambda j: (0, 0, j)), pl.BlockSpec((CONV_F, ct), lambda j: (0, j)),
                   pl.BlockSpec((1, ct), lambda j: (0, j))],
        out_shape=[jax.ShapeDtypeStruct((2, S, FFN), MXU), jax.ShapeDtypeStruct((CONV_F, FFN), f32), jax.ShapeDtypeStruct((1, FFN), f32)],
        scratch=[pltpu.VMEM((S, D), MXU)], sem=("arbitrary",))


def _ffn_bwd(tag, layer, saved, norm_g, w_inT, conv_w, conv_b, w_down, dout, mid_comm=None):
    x, h, gate_pre, val, act = saved
    g_down = matmul(f"{tag}_wdown", act, dout, "tn", out_dtype=MXU)
    dgv, g_cw, g_cb = _ffn_mid_bwd(f"{tag}_dmid", dout, w_down, gate_pre, val, conv_w, conv_b, comm=mid_comm)
    g_inT = matmul_tn_stacked(f"{tag}_win", dgv, h, MXU)

    def din_fn(dg_, dv_, x_, do_, g_, wT):
        dx, dg = _rms_bwd(x_, g_, _mm(dg_, wT[:FFN], _NN) + _mm(dv_, wT[FFN:], _NN))
        return do_ + dx, dg

    rs = ReduceScatter(tag, (f"f_inT{layer}", f"f_down{layer}"), [g_inT, g_down])
    dx, g_norm = rowwise(f"{tag}_din", din_fn, [(dgv, 0), (dgv, 1), x, dout], [norm_g, w_inT], [(D, f32)], [(1, D)],
                         comm=rs.sibling())
    return dx, {f"f_norm{layer}": g_norm, f"f_conv_w{layer}": g_cw, f"f_conv_b{layer}": g_cb}, rs


def _land(W, keys, comm):
    for k, g in zip(keys, comm.results):
        W[k] = g.reshape(-1, g.shape[2])


def _local_step(x, pos, tgt, W, shards):
    G = {}
    gather = lambda *keys: gather_comm([shards[k] for k in keys])
    inv_freq = (ROPE_THETA ** (-jnp.arange(AH // 2, dtype=f32) / (AH // 2))).reshape(1, AH // 2)

    def in_fn(x_, g_, wT, wdtT):
        h_ = _rms_fwd(x_, g_).astype(MXU)
        return h_, _mm(h_, wT[slice(*IN_ROWS["z"])], _NT), _mm(h_, wdtT, _NT)

    h0, z, dt_pre = rowwise("a_in", in_fn, [x], [W["a_norm"], W["inT"], W["in_dtT"]], [(D, MXU), (INNER, f32), (LANES, f32)])
    pre, conv = {}, {}
    early = {"xs": ("a_out",), "B": (), "C": ()}
    for k in ("xs", "B", "C"):
        c = gather(*early[k]) if early[k] else None
        pre[k], conv[k] = _proj_conv(f"a_in_{k}", h0, W["inT"], IN_ROWS[k][0], W[f"cw_{k}"], W[f"cb_{k}"], comm=c)
        if c is not None:
            _land(W, early[k], c)
    c = gather("f_inT0", "f_down0")
    y, states = ssd_fwd(conv["xs"], conv["B"], conv["C"], dt_pre, W["dt_bias"], W["A_log"], W["D"], comm=c)
    _land(W, ("f_inT0", "f_down0"), c)

    def gate_norm(y_, z_, g_):
        yg = y_ * _silu(z_)
        w = INNER // GROUPS
        return (jnp.concatenate([_rms_fwd(yg[:, w * i:w * (i + 1)], g_[:, w * i:w * (i + 1)]) for i in range(GROUPS)], axis=1),)

    def out_fn(y_, z_, x_, g_, w_):
        (gn_,) = gate_norm(y_, z_, g_)
        gn_ = gn_.astype(MXU)
        return gn_, x_ + _mm(gn_, w_, _NN)

    c = gather("f_down1")
    gn, x1 = rowwise("a_out", out_fn, [y, z, x], [W["a_gnorm"], W["a_out"]], [(INNER, MXU), (D, f32)], comm=c)
    _land(W, ("f_down1",), c)

    c = gather("w_kv", "w_q", "w_o")
    act0, ffn0 = _ffn_fwd("f0", x1, W["f_norm0"], W["f_inT0"], W["f_cw0"], W["f_cb0"], mid_comm=c)
    _land(W, ("w_kv", "w_q", "w_o"), c)
    x2 = matmul("f0_down", act0, W["f_down0"], "nn", residual=x1)

    def qkv_fn(x_, gk, gb, wkv, bkv, wq, bq):
        kvn_, h2_ = _rms_fwd(x_, gk).astype(MXU), _rms_fwd(x_, gb).astype(MXU)
        return kvn_, h2_, _mm(kvn_, wkv, _NN) + bkv, _mm(h2_, wq, _NN) + bq

    kw = KVH * AH
    kvn, h2, kv, q_pre = rowwise("qkv_proj", qkv_fn, [x2], [W["kv_norm"], W["b_norm"], W["w_kv"], W["b_kv"], W["w_q"], W["b_q"]],
                                 [(D, MXU), (D, MXU), (2 * kw, f32), (D, f32)])

    def k_fwd(kv_, pos_, if_, g_):
        cos, sin = _rope_tables(pos_, if_)
        return _headnorm_rope_fwd(kv_[:, :kw], g_, cos, sin, KVH), kv_[:, kw:]

    k_rot, v_val = rowwise("k_rope", k_fwd, [kv, pos], [inv_freq, W["k_norm"]], [(kw, f32), (kw, f32)])

    def q_fwd(q_, pos_, if_, g_):
        cos, sin = _rope_tables(pos_, if_)
        return (_headnorm_rope_fwd(q_, g_, cos, sin, QH),)

    (q,) = rowwise("q_rope", q_fwd, [q_pre, pos], [inv_freq, W["q_norm"]], [(D, f32)])
    c = gather("f_inT1")
    att = attn_fwd(q, k_rot, v_val, W["sinks"], comm=c)
    _land(W, ("f_inT1",), c)
    x3 = matmul("o_proj", att, W["w_o"], "nn", bias=W["b_o"], residual=x2)

    act1, ffn1 = _ffn_fwd("f1", x3, W["f_norm1"], W["f_inT1"], W["f_cw1"], W["f_cb1"])

    def loss_fn(a_, x_, t_, w_):
        diff = x_ + _mm(a_, w_, _NN) - t_
        rows = jnp.sum(diff * diff, axis=1, keepdims=True) * (0.5 / D)
        return diff * (1.0 / D), jnp.sum(rows, axis=0, keepdims=True)

    dx4, loss = rowwise("f1_down_loss", loss_fn, [act1, x3, tgt], [W["f_down1"]], [(D, f32)], [(1, 1)])

    dx3, g, rs_f1 = _ffn_bwd("f1", 1, ffn1, W["f_norm1"], W["f_inT1"], W["f_cw1"], W["f_cb1"], W["f_down1"], dx4)
    G.update(g)

    datt = matmul("o_dproj", dx3, W["w_o"], "nt")
    g_wo = matmul("o_wproj", att, dx3, "tn", out_dtype=MXU)
    dq, dkp, dkc, dvp, dvc, G["sinks"] = attn_bwd(q, k_rot, v_val, W["sinks"], datt, comm=rs_f1.chips())

    def q_bwd(q_, pos_, dq_, dx_, if_, g_):
        cos, sin = _rope_tables(pos_, if_)
        dqp, dg = _headnorm_rope_bwd(q_, g_, cos, sin, dq_, QH)
        return dqp, dg, jnp.sum(dqp, axis=0, keepdims=True), jnp.sum(dx_, axis=0, keepdims=True)

    dq_pre, G["q_norm"], G["b_q"], G["b_o"] = rowwise("q_drope", q_bwd, [q_pre, pos, dq, dx3], [inv_freq, W["q_norm"]],
                                                      [(D, MXU)], [(1, AH), (1, D), (1, D)])
    g_wq = matmul("q_wproj", h2, dq_pre, "tn", out_dtype=MXU)
    dkv, G["k_norm"], G["b_kv"] = kv_bwd(kv, pos, inv_freq, W["k_norm"], dkp, dkc, dvp, dvc)
    g_wkv = matmul("kv_wproj", kvn, dkv, "tn", out_dtype=MXU)
    rs_att = ReduceScatter("att", ("w_kv", "w_q", "w_o"), [g_wkv, g_wq, g_wo])

    def x2_bwd(x_, dq_, dkv_, dx_, gb_, gk_, wq, wkv):
        d1, dgb = _rms_bwd(x_, gb_, _mm(dq_, wq, _NT))
        d2, dgk = _rms_bwd(x_, gk_, _mm(dkv_, wkv, _NT))
        return dx_ + d1 + d2, dgb, dgk

    dx2, G["b_norm"], G["kv_norm"] = rowwise("qkv_dproj", x2_bwd, [x2, dq_pre, dkv, dx3],
                                             [W["b_norm"], W["kv_norm"], W["w_q"], W["w_kv"]],
                                             [(D, f32)], [(1, D), (1, D)], comm=rs_att.sibling())

    dx1, g, rs_f0 = _ffn_bwd("f0", 0, ffn0, W["f_norm0"], W["f_inT0"], W["f_cw0"], W["f_cb0"], W["f_down0"], dx2,
                             mid_comm=rs_att.chips())
    G.update(g)

    rs_out = ReduceScatter("a_out", ("a_out",), [matmul("a_wout", gn, dx1, "tn", out_dtype=MXU)])

    def gate_norm_bwd(y_, z_, dx_, g_, w_out):
        dgn_ = _mm(dx_, w_out, _NT)
        w = INNER // GROUPS
        sg = _sigmoid(z_)
        sz = z_ * sg
        yg = y_ * sz
        parts, dgs = [], []
        for i in range(GROUPS):
            dseg, dg = _rms_bwd(yg[:, w * i:w * (i + 1)], g_[:, w * i:w * (i + 1)], dgn_[:, w * i:w * (i + 1)])
            parts.append(dseg)
            dgs.append(dg)
        dyg = jnp.concatenate(parts, axis=1)
        return dyg * sz, dyg * y_ * (sg * (1.0 + z_ * (1.0 - sg))), jnp.concatenate(dgs, axis=1)

    dy, dz, G["a_gnorm"] = rowwise("a_dout", gate_norm_bwd, [y, z, dx1], [W["a_gnorm"], W["a_out"]],
                                   [(INNER, f32), (INNER, MXU)], [(1, INNER)], comm=rs_out.sibling())
    dconv = {}
    dconv["xs"], dconv["B"], dconv["C"], ddt_pre, G["dt_bias"], G["A_log"], G["D"] = ssd_bwd(
        conv["xs"], conv["B"], conv["C"], dt_pre, W["dt_bias"], W["A_log"], W["D"], states, dy,
        comm=merge_comms([rs_f0.chips(), rs_out.chips()]))

    g_in, dpre = [matmul("a_win_z", dz, h0, "tn", out_dtype=MXU)], {}
    for k in ("xs", "B", "C"):
        dpre[k], g_k, G[f"cw_{k}"], G[f"cb_{k}"] = _dconv_wgrad(f"a_dconv_{k}", pre[k], dconv[k], W[f"cw_{k}"], W[f"cb_{k}"], h0)
        g_in.append(g_k)
    g_in.append(matmul("a_win_dt", ddt_pre, h0, "tn", out_dtype=MXU)[:HEADS])
    rs_in = ReduceScatter("a_in", ("inT",), [jnp.concatenate(g_in, axis=0)])
    run_comm("rs_in_sibling", rs_in.sibling())

    def x0_bwd(dz_, dxs_, db_, dc_, ddt_, x_, do_, g_, wT, wdtT):
        parts = zip((dz_, dxs_, db_, dc_), IN_ROWS.values())
        dh = sum(_mm(d_, wT[a:b], _NN) for d_, (a, b) in parts) + _mm(ddt_, wdtT, _NN)
        dx, dg = _rms_bwd(x_, g_, dh)
        return do_ + dx, dg

    dx, G["a_norm"] = rowwise("a_din", x0_bwd, [dz, dpre["xs"], dpre["B"], dpre["C"], ddt_pre, x, dx1],
                              [W["a_norm"], W["inT"], W["in_dtT"]], [(D, f32)], [(1, D)], comm=rs_in.chips())
    return loss, dx, G, [rs_f1, rs_att, rs_f0, rs_out, rs_in]


ROW_KEYS = ("inT", "a_out", "f_inT0", "f_down0", "w_kv", "w_q", "w_o", "f_inT1", "f_down1")


def _row_blocks(src):
    return {"inT": src["a_in_proj"][0].T, "a_out": src["a_out_proj"][0], "w_kv": src["w_kv"], "w_q": src["w_q"][0],
            "w_o": src["w_o"][0], "f_inT0": src["f_w_in"][0].T, "f_inT1": src["f_w_in"][1].T,
            "f_down0": src["f_w_down"][0], "f_down1": src["f_w_down"][1]}


def _from_row_blocks(rb):
    out = {"a_in_proj": rb["inT"].T[None], "a_out_proj": rb["a_out"][None], "w_kv": rb["w_kv"], "w_q": rb["w_q"][None],
           "w_o": rb["w_o"][None]}
    if "f_inT0" in rb:
        out["f_w_in"] = jnp.stack([rb["f_inT0"].T, rb["f_inT1"].T])
        out["f_w_down"] = jnp.stack([rb["f_down0"], rb["f_down1"]])
    return out


SMALL_SHARDED = ("a_norm", "a_conv_w", "a_conv_b", "a_gnorm", "f_conv_w")
REPLICATED = ("a_dt_bias", "a_A_log", "a_D", "kv_norm", "b_kv", "k_norm", "b_norm", "b_q", "q_norm", "sinks", "b_o",
              "f_norm", "f_conv_b")
ORDER = ("a_norm", "a_in_proj", "a_conv_w", "a_conv_b", "a_dt_bias", "a_A_log", "a_D", "a_gnorm", "a_out_proj", "kv_norm",
         "w_kv", "b_kv", "k_norm", "b_norm", "w_q", "b_q", "q_norm", "sinks", "w_o", "b_o", "f_norm", "f_w_in",
         "f_conv_w", "f_conv_b", "f_w_down")


def _gathered_to_whole(name, g):
    if name == "a_conv_w":
        return jnp.moveaxis(g[:, 0], 0, 1).reshape(g.shape[2], -1)
    if name in ("a_norm", "a_conv_b", "a_gnorm"):
        return g[:, 0].reshape(1, -1)
    if name == "f_conv_w":
        return jnp.moveaxis(g, 0, 2).reshape(g.shape[1], g.shape[2], -1)
    raise ValueError(name)


def _whole_to_shards(name, w):
    if name == "a_conv_w":
        return jnp.moveaxis(w.reshape(w.shape[0], N_DEV, -1), 1, 0)[:, None]
    if name in ("a_norm", "a_conv_b", "a_gnorm"):
        return w.reshape(N_DEV, 1, -1)
    if name == "f_conv_w":
        return jnp.moveaxis(w.reshape(w.shape[0], w.shape[1], N_DEV, -1), 2, 0)
    raise ValueError(name)


def _small_weights(whole):
    W = {}
    cw, cb = whole["a_conv_w"], whole["a_conv_b"]
    o = 0
    for k, n in (("xs", INNER), ("B", GROUPS * NSTATE), ("C", GROUPS * NSTATE)):
        W[f"cw_{k}"], W[f"cb_{k}"] = cw[:, o:o + n], cb[:, o:o + n]
        o += n
    W["a_norm"], W["a_gnorm"] = whole["a_norm"], whole["a_gnorm"]
    W["dt_bias"], W["A_log"], W["D"] = (_pad_lanes(whole[k]) for k in ("a_dt_bias", "a_A_log", "a_D"))
    W["kv_norm"], W["b_kv"], W["k_norm"] = whole["kv_norm"].reshape(1, -1), whole["b_kv"].reshape(1, -1), whole["k_norm"].reshape(1, -1)
    for k in ("b_norm", "b_q", "q_norm", "sinks", "b_o"):
        W[k] = whole[k]
    for i in range(2):
        W[f"f_norm{i}"] = whole["f_norm"][i:i + 1]
        W[f"f_cw{i}"], W[f"f_cb{i}"] = whole["f_conv_w"][i], whole["f_conv_b"][i:i + 1]
    return W


def _small_grads(G, shapes):
    nh = HEADS
    out = {
        "a_conv_w": jnp.concatenate([G["cw_xs"], G["cw_B"], G["cw_C"]], axis=1),
        "a_conv_b": jnp.concatenate([G["cb_xs"], G["cb_B"], G["cb_C"]], axis=1),
        "a_norm": G["a_norm"], "a_gnorm": G["a_gnorm"],
        "a_dt_bias": G["dt_bias"][:, :nh], "a_A_log": G["A_log"][:, :nh], "a_D": G["D"][:, :nh],
        "kv_norm": G["kv_norm"], "b_kv": G["b_kv"], "k_norm": G["k_norm"], "b_norm": G["b_norm"],
        "b_q": G["b_q"], "q_norm": G["q_norm"], "sinks": G["sinks"], "b_o": G["b_o"],
        "f_norm": jnp.concatenate([G["f_norm0"], G["f_norm1"]], axis=0),
        "f_conv_w": jnp.stack([G["f_conv_w0"], G["f_conv_w1"]]),
        "f_conv_b": jnp.concatenate([G["f_conv_b0"], G["f_conv_b1"]], axis=0),
    }
    return {k: val.reshape(shapes[k]) if k in shapes else val for k, val in out.items()}


def kernel(x, positions, a_norm, a_in_proj, a_conv_w, a_conv_b, a_dt_bias, a_A_log, a_D, a_gnorm, a_out_proj, kv_norm, w_kv, b_kv, k_norm, b_norm, w_q, b_q, q_norm, sinks, w_o, b_o, f_norm, f_w_in, f_conv_w, f_conv_b, f_w_down, loss_target, m_a_norm, m_a_in_proj, m_a_conv_w, m_a_conv_b, m_a_dt_bias, m_a_A_log, m_a_D, m_a_gnorm, m_a_out_proj, m_kv_norm, m_w_kv, m_b_kv, m_k_norm, m_b_norm, m_w_q, m_b_q, m_q_norm, m_sinks, m_w_o, m_b_o, m_f_norm, m_f_w_in, m_f_conv_w, m_f_conv_b, m_f_w_down, v_a_norm, v_a_in_proj, v_a_conv_w, v_a_conv_b, v_a_dt_bias, v_a_A_log, v_a_D, v_a_gnorm, v_a_out_proj, v_kv_norm, v_w_kv, v_b_kv, v_k_norm, v_b_norm, v_w_q, v_b_q, v_q_norm, v_sinks, v_w_o, v_b_o, v_f_norm, v_f_w_in, v_f_conv_w, v_f_conv_b, v_f_w_down):
    given = dict(locals())
    w_in = {n: given[n] for n in ORDER}
    m_in = {n: given["m_" + n] for n in ORDER}
    v_in = {n: given["v_" + n] for n in ORDER}
    dev = 4 * lax.axis_index("x") + 2 * lax.axis_index("y") + lax.axis_index("c")

    w2, m2, v2 = _row_blocks(w_in), _row_blocks(m_in), _row_blocks(v_in)
    small_pack = _pack([w_in[n] for n in SMALL_SHARDED], 8, f32)
    shards = {k: w2[k].astype(MXU) for k in ROW_KEYS}
    in_all, small_all = run_comm("ag_head", gather_comm([shards["inT"], small_pack]))
    whole = {n: w_in[n] for n in REPLICATED}
    for n, g in zip(SMALL_SHARDED, _unpack(small_all, [w_in[n].shape for n in SMALL_SHARDED], lead=(N_DEV,))):
        whole[n] = _gathered_to_whole(n, g)
    W = _small_weights(whole)
    W["inT"] = in_all.reshape(-1, D)
    W["in_dtT"] = jnp.pad(W["inT"][IN_COLS - HEADS:], ((0, LANES - HEADS), (0, 0)))

    loss, dx, G, scatters = _local_step(x[0], positions.reshape(S, 1).astype(f32), loss_target[0], W, shards)
    grads = _small_grads(G, {n: whole[n].shape for n in REPLICATED})

    small_names = SMALL_SHARDED + REPLICATED
    small_part = _pack([grads[n] for n in small_names], 8, f32)
    small_gather = gather_comm([small_part])
    run_comm("ag_small_grads", small_gather)
    sums = {}
    for rs in scatters:
        sums.update(rs.sums())
    single = tuple(k for k in ROW_KEYS if not k.startswith("f_"))
    g_out = _from_row_blocks({k: sums[k] for k in single})
    small_sum = sum_devices(small_gather.results[0])
    for n, g in zip(small_names, _unpack(small_sum, [grads[n].shape for n in small_names])):
        if n in SMALL_SHARDED:
            g_out[n] = lax.dynamic_index_in_dim(_whole_to_shards(n, g), dev, axis=0, keepdims=False)
        else:
            g_out[n] = g.reshape(w_in[n].shape)

    stepped = {k: adamw(f"adamw_{k}", w2[k], sums[k], m2[k], v2[k]) for k in single}
    delta, new_m, new_v = (_from_row_blocks({k: stepped[k][i] for k in single}) for i in range(3))
    for n, key, lay in (("f_w_in", "f_inT", lambda a: jnp.swapaxes(a, 1, 2)), ("f_w_down", "f_down", lambda a: a)):
        res = adamw_layers(f"adamw_{n}", lay(w_in[n]), sums[key + "0"], sums[key + "1"], lay(m_in[n]), lay(v_in[n]))
        g_out[n], delta[n], new_m[n], new_v[n] = (lay(a) for a in res)
    packs = [_pack([src[n] for n in small_names], 8, f32) for src in (w_in, g_out, m_in, v_in)]
    outs = adamw("adamw_small", *packs)
    for dst, buf in zip((delta, new_m, new_v), outs):
        for n, a in zip(small_names, _unpack(buf, [w_in[n].shape for n in small_names])):
            dst[n] = a

    loss_all = lax.psum(loss[0, 0], AXES)
    return (loss_all, dx[None], *[g_out[n] for n in ORDER], *[delta[n] for n in ORDER],
            *[new_m[n] for n in ORDER], *[new_v[n] for n in ORDER])
```

```python
import functools
import math

import jax
import jax.numpy as jnp
from jax import lax
from jax.experimental import pallas as pl
from jax.experimental.pallas import tpu as pltpu

f32 = jnp.float32
bf16 = jnp.bfloat16
MXU = bf16

N_DEV = 8
S = 2048
D = 1024
EPS = 1e-5
INNER = 2048
HEADS = 32
HP = 64
GROUPS = 8
HPG = HEADS // GROUPS
NSTATE = 128
CONV_A = 4
CHUNK = 256
NCHUNK = S // CHUNK
AH = 64
QH = 16
KVH = 4
QPK = QH // KVH
WIN = 128
NBLK = S // WIN
ROPE_THETA = 10000.0
FFN = 2816
CONV_F = 3
LANES = 128
PACK_W = 1024
VMEM_LIMIT = 56 * 1024 * 1024

ADAM_LR, ADAM_B1, ADAM_B2, ADAM_EPS, ADAM_WD, ADAM_STEP = 0.001, 0.9, 0.999, 1e-08, 0.01, 10

MESH = pl.DeviceIdType.MESH
AXES = ("x", "y", "c")


def _cparams(sem=None):
    return pltpu.CompilerParams(dimension_semantics=sem, vmem_limit_bytes=VMEM_LIMIT)


def _pick(n, cands):
    for c in cands:
        if n % c == 0:
            return c
    return n


class Comm:
    def __init__(self, ins, out_shapes, sems, start, finish):
        self.ins, self.out_shapes, self.sems, self.start, self.finish = list(ins), list(out_shapes), list(sems), start, finish
        self.results, self.children = None, ()

    def set_results(self, res):
        self.results, o = list(res), 0
        for ch in self.children:
            ch.set_results(res[o:o + len(ch.out_shapes)])
            o += len(ch.out_shapes)


def merge_comms(comms):
    def each(fn_name, ins, outs, sems):
        i = o = s = 0
        for c in comms:
            getattr(c, fn_name)(ins[i:i + len(c.ins)], outs[o:o + len(c.out_shapes)], sems[s:s + len(c.sems)])
            i, o, s = i + len(c.ins), o + len(c.out_shapes), s + len(c.sems)

    merged = Comm([a for c in comms for a in c.ins], [a for c in comms for a in c.out_shapes], [a for c in comms for a in c.sems],
                  functools.partial(each, "start"), functools.partial(each, "finish"))
    merged.children = tuple(comms)
    return merged


def _call(body, args, *, name, grid, in_specs, out_specs, out_shape, scratch=(), sem=None, comm=None):
    if comm is None:
        return pl.pallas_call(body, name=name, grid=grid, in_specs=list(in_specs), out_specs=list(out_specs),
                              out_shape=list(out_shape), scratch_shapes=list(scratch), compiler_params=_cparams(sem))(*args)
    n_in, n_out, n_scr, c_in, c_out = len(in_specs), len(out_shape), len(scratch), len(comm.ins), len(comm.out_shapes)
    any_spec = pl.BlockSpec(memory_space=pl.ANY)

    def outer(*refs):
        ins, c_ins = refs[:n_in], refs[n_in:n_in + c_in]
        o = n_in + c_in
        outs, c_outs = refs[o:o + n_out], refs[o + n_out:o + n_out + c_out]
        o += n_out + c_out
        scr, c_sems = refs[o:o + n_scr], refs[o + n_scr:]
        ids = [pl.program_id(i) for i in range(len(grid))]
        first = functools.reduce(jnp.logical_and, [i == 0 for i in ids])
        last = functools.reduce(jnp.logical_and, [i == g - 1 for i, g in zip(ids, grid)])

        @pl.when(first)
        def _():
            comm.start(c_ins, c_outs, c_sems)

        body(*ins, *outs, *scr)

        @pl.when(last)
        def _():
            comm.finish(c_ins, c_outs, c_sems)

    res = pl.pallas_call(
        outer, name=name, grid=grid, in_specs=list(in_specs) + [any_spec] * c_in,
        out_specs=list(out_specs) + [any_spec] * c_out, out_shape=list(out_shape) + comm.out_shapes,
        scratch_shapes=list(scratch) + comm.sems, compiler_params=_cparams(("arbitrary",) * len(grid)),
    )(*args, *comm.ins)
    comm.set_results(res[n_out:])
    return res[:n_out]


def matmul(name, a, b, mode, out_dtype=f32, bias=None, residual=None):
    if mode == "nn":
        (M, K), (K2, N) = a.shape, b.shape
    elif mode == "nt":
        (M, K), (N, K2) = a.shape, b.shape
    else:
        (K, M), (K2, N) = a.shape, b.shape
    assert K == K2, (name, a.shape, b.shape)
    if mode == "tn":
        tm, tn = M, _pick(N, (512, 256, 128) if M <= 1024 else (256, 128))
        a_spec = pl.BlockSpec((K, M), lambda j: (0, 0))
        b_spec = pl.BlockSpec((K, tn), lambda j: (0, j))
        dims = (((0,), (0,)), ((), ()))
        grid, o_map, row_map = (N // tn,), (lambda j: (0, j)), (lambda j: (0, j))
    else:
        tm, tn = (256 if N >= 2048 else 512), N
        a_spec = pl.BlockSpec((tm, K), lambda i: (i, 0))
        b_spec = pl.BlockSpec(b.shape, lambda i: (0, 0))
        dims = (((1,), (0,)), ((), ())) if mode == "nn" else (((1,), (1,)), ((), ()))
        grid, o_map, row_map = (M // tm,), (lambda i: (i, 0)), (lambda i: (0, 0))
    ins, in_specs = [a, b], [a_spec, b_spec]
    if bias is not None:
        ins.append(bias)
        in_specs.append(pl.BlockSpec((1, tn), row_map))
    if residual is not None:
        ins.append(residual)
        in_specs.append(pl.BlockSpec((tm, tn), o_map))
    has_bias, has_res = bias is not None, residual is not None

    def body(a_ref, b_ref, *rest):
        rest = list(rest)
        bias_ref = rest.pop(0) if has_bias else None
        res_ref = rest.pop(0) if has_res else None
        (o_ref,) = rest
        r = lax.dot_general(a_ref[...].astype(MXU), b_ref[...].astype(MXU), dims, preferred_element_type=f32)
        if has_bias:
            r = r + bias_ref[...]
        if has_res:
            r = r + res_ref[...]
        o_ref[...] = r.astype(out_dtype)

    return pl.pallas_call(
        body, name=name, grid=grid, in_specs=in_specs,
        out_specs=pl.BlockSpec((tm, tn), o_map),
        out_shape=jax.ShapeDtypeStruct((M, N), out_dtype),
        compiler_params=_cparams(("parallel",)),
    )(*ins)


def matmul_tn_stacked(name, a, b, out_dtype):
    R, K, M = a.shape
    N = b.shape[1]
    tn = _pick(N, (256, 128))

    def body(a_ref, b_ref, o_ref):
        o_ref[0] = lax.dot_general(a_ref[0].astype(MXU), b_ref[...].astype(MXU), (((0,), (0,)), ((), ())),
                                   preferred_element_type=f32).astype(out_dtype)

    out = pl.pallas_call(
        body, name=name, grid=(R, N // tn),
        in_specs=[pl.BlockSpec((1, K, M), lambda r, j: (r, 0, 0)), pl.BlockSpec((K, tn), lambda r, j: (0, j))],
        out_specs=pl.BlockSpec((1, M, tn), lambda r, j: (r, 0, j)),
        out_shape=jax.ShapeDtypeStruct((R, M, N), out_dtype),
        compiler_params=_cparams(("parallel", "parallel")),
    )(a, b)
    return out.reshape(R * M, N)


def rowwise(name, fn, rows, pars, outs, accs=(), tile=256, comm=None):
    n_in, n_out = len(rows) + len(pars), len(outs)
    in_specs = [pl.BlockSpec((None, tile, r[0].shape[2]), functools.partial(lambda i, lead: (lead, i, 0), lead=r[1]))
                if isinstance(r, tuple) else pl.BlockSpec((tile, r.shape[1]), lambda i: (i, 0)) for r in rows]
    rows = [r[0] if isinstance(r, tuple) else r for r in rows]
    in_specs += [pl.BlockSpec(p.shape, lambda i: (0, 0)) for p in pars]
    out_specs = [pl.BlockSpec((tile, c), lambda i: (i, 0)) for c, _ in outs]
    out_specs += [pl.BlockSpec(shp, lambda i: (0, 0)) for shp in accs]
    out_shape = [jax.ShapeDtypeStruct((S, c), dt) for c, dt in outs]
    out_shape += [jax.ShapeDtypeStruct(shp, f32) for shp in accs]

    def body(*refs):
        res = fn(*[r[...] for r in refs[:n_in]])
        o_refs = refs[n_in:n_in + n_out]
        a_refs = refs[n_in + n_out:]
        for ref, val in zip(o_refs, res[:n_out]):
            ref[...] = val.astype(ref.dtype)
        if a_refs:
            @pl.when(pl.program_id(0) == 0)
            def _():
                for ref in a_refs:
                    ref[...] = jnp.zeros_like(ref)
            for ref, val in zip(a_refs, res[n_out:]):
                ref[...] += val

    return _call(body, [*rows, *pars], name=name, grid=(S // tile,), in_specs=in_specs, out_specs=out_specs,
                 out_shape=out_shape, sem=("arbitrary",) if accs else ("parallel",), comm=comm)


def _sigmoid(x):
    return 0.5 * jnp.tanh(0.5 * x) + 0.5


def _silu(x):
    return x * _sigmoid(x)


def _dsilu(x):
    sg = _sigmoid(x)
    return sg * (1.0 + x * (1.0 - sg))


def _softplus(x):
    return jnp.maximum(x, 0.0) + jnp.log(1.0 + jnp.exp(-jnp.abs(x)))


def _rms_fwd(x, g):
    r = lax.rsqrt(jnp.mean(x * x, axis=-1, keepdims=True) + EPS)
    return x * r * g


def _rms_bwd(x, g, dh):
    r = lax.rsqrt(jnp.mean(x * x, axis=-1, keepdims=True) + EPS)
    xh = x * r
    dxh = dh * g
    dx = r * (dxh - xh * jnp.mean(dxh * xh, axis=-1, keepdims=True))
    return dx, jnp.sum(dh * xh, axis=0, keepdims=True)


def _taps(x, width):
    row = lax.broadcasted_iota(jnp.int32, (8, x.shape[1]), 0)

    def shifted(s):
        r = pltpu.roll(x, s, 0)
        return jnp.concatenate([jnp.where(row >= s, r[:8], 0.0), r[8:]], axis=0)

    return [shifted(s) for s in range(width - 1, 0, -1)] + [x]


def _conv(x, w, b, taps=None):
    width = w.shape[0]
    taps = _taps(x, width) if taps is None else taps
    out = b + w[0:1, :] * taps[0]
    for k in range(1, width):
        out = out + w[k:k + 1, :] * taps[k]
    return out


def _conv_bwd(x, w, dc, taps=None):
    width, n = w.shape[0], x.shape[0]
    taps = _taps(x, width) if taps is None else taps
    row = lax.broadcasted_iota(jnp.int32, (8, x.shape[1]), 0)
    dx = w[width - 1:width, :] * dc
    for k in range(width - 1):
        s = width - 1 - k
        r = pltpu.roll(dc, n - s, 0)
        dx = dx + w[k:k + 1, :] * jnp.concatenate([r[:n - 8], jnp.where(row < 8 - s, r[n - 8:], 0.0)], axis=0)
    dw = jnp.concatenate([jnp.sum(dc * t, axis=0, keepdims=True) for t in taps], axis=0)
    return dx, dw, jnp.sum(dc, axis=0, keepdims=True)


def _rope_tables(pos, inv_freq):
    ang = pos * inv_freq
    return jnp.cos(ang), jnp.sin(ang)


def _split2(v):
    hi = v.astype(bf16)
    return hi, (v - hi.astype(f32)).astype(bf16)


def _head_maps(width):
    shift = AH.bit_length() - 1
    to_head = (lax.broadcasted_iota(jnp.int32, (width, LANES), 0) >> shift) == lax.broadcasted_iota(jnp.int32, (width, LANES), 1)
    from_head = lax.broadcasted_iota(jnp.int32, (LANES, width), 0) == (lax.broadcasted_iota(jnp.int32, (LANES, width), 1) >> shift)
    return to_head.astype(bf16), from_head.astype(bf16)


def _head_sums(v, to_head):
    hi, lo = _split2(v)
    return jnp.dot(hi, to_head, preferred_element_type=f32) + jnp.dot(lo, to_head, preferred_element_type=f32)


def _head_spread(s, from_head):
    hi, lo = _split2(s)
    return jnp.dot(hi, from_head, preferred_element_type=f32) + jnp.dot(lo, from_head, preferred_element_type=f32)


def _rope_full(cos, sin, width):
    half = AH // 2
    pad = jnp.zeros((cos.shape[0], LANES - half), f32)
    r = lax.broadcasted_iota(jnp.int32, (LANES, width), 0)
    lane = lax.broadcasted_iota(jnp.int32, (LANES, width), 1)
    spread = ((lane & (half - 1)) == r).astype(bf16)
    full = lambda t: _head_spread(jnp.concatenate([t, pad], axis=1), spread)
    first = (lax.broadcasted_iota(jnp.int32, (1, width), 1) & (AH - 1)) < half
    sin_f = full(sin)
    return full(cos), jnp.where(first, -sin_f, sin_f), first


def _swap_halves(v, first):
    half, width = AH // 2, v.shape[1]
    return jnp.where(first, pltpu.roll(v, width - half, 1), pltpu.roll(v, half, 1))


def _headnorm_rope_fwd(x, g, cos, sin, heads):
    to_head, from_head = _head_maps(heads * AH)
    cos_f, sin_s, first = _rope_full(cos, sin, heads * AH)
    r = _head_spread(lax.rsqrt(_head_sums(x * x, to_head) * (1.0 / AH) + EPS), from_head)
    n = x * r * jnp.tile(g, (1, heads))
    return n * cos_f + _swap_halves(n, first) * sin_s


def _headnorm_rope_bwd(x, g, cos, sin, dout, heads):
    width = heads * AH
    to_head, from_head = _head_maps(width)
    cos_f, sin_s, first = _rope_full(cos, sin, width)
    r = _head_spread(lax.rsqrt(_head_sums(x * x, to_head) * (1.0 / AH) + EPS), from_head)
    xh = x * r
    dn = dout * cos_f - _swap_halves(dout, first) * sin_s
    dxh = dn * jnp.tile(g, (1, heads))
    m = _head_spread(_head_sums(dxh * xh, to_head) * (1.0 / AH), from_head)
    dx = r * (dxh - xh * m)
    dg_lanes = jnp.sum(dn * xh, axis=0, keepdims=True)
    fold = ((lax.broadcasted_iota(jnp.int32, (width, LANES), 0) & (AH - 1))
            == lax.broadcasted_iota(jnp.int32, (width, LANES), 1)).astype(f32)
    dg = jnp.dot(jnp.broadcast_to(dg_lanes, (8, width)), fold, precision=lax.Precision.HIGHEST, preferred_element_type=f32)
    return dx, dg[0:1, :AH]


def _ssd_prep(dt_pre, dt_bias, a_log, dt_s, acum_s, acumT_s):
    dt = _softplus(dt_pre + dt_bias)
    a = dt * (-jnp.exp(a_log))
    row = lax.broadcasted_iota(jnp.int32, (CHUNK, CHUNK), 0)
    col = lax.broadcasted_iota(jnp.int32, (CHUNK, CHUNK), 1)
    dt_s[...] = dt
    acum_s[...] = jnp.dot((col <= row).astype(f32), a, precision=lax.Precision.HIGHEST, preferred_element_type=f32)
    acumT_s[...] = lax.dot_general(a, (row <= col).astype(f32), (((0,), (0,)), ((), ())),
                                   precision=lax.Precision.HIGHEST, preferred_element_type=f32)


def _head_cols(h, dt_s, acum_s, acumT_s):
    lane = lax.broadcasted_iota(jnp.int32, (1, LANES), 1)
    oh_l = (lane == h).astype(f32)
    sub = lax.broadcasted_iota(jnp.int32, (LANES, 1), 0)
    oh_s = (sub == h).astype(f32)
    dt_h = jnp.sum(dt_s[...] * oh_l, axis=1, keepdims=True)
    ac_h = jnp.sum(acum_s[...] * oh_l, axis=1, keepdims=True)
    acr_h = jnp.sum(acumT_s[...] * oh_s, axis=0, keepdims=True)
    return oh_l, dt_h, ac_h, acr_h


def ssd_fwd(xs, Bm, Cm, dt_pre, dt_bias, a_log, d_skip, comm=None):
    def body(xs_ref, b_ref, c_ref, dtp_ref, bias_ref, alog_ref, d_ref, y_ref, st_ref, state, dt_s, acum_s, acumT_s):
        c, g = pl.program_id(0), pl.program_id(1)

        @pl.when(g == 0)
        def _():
            _ssd_prep(dtp_ref[...], bias_ref[...], alog_ref[...], dt_s, acum_s, acumT_s)

        row = lax.broadcasted_iota(jnp.int32, (CHUNK, CHUNK), 0)
        col = lax.broadcasted_iota(jnp.int32, (CHUNK, CHUNK), 1)
        causal = col <= row
        Bb, Cb = b_ref[...], c_ref[...]
        cb = lax.dot_general(Cb.astype(MXU), Bb.astype(MXU), (((1,), (1,)), ((), ())), preferred_element_type=f32)
        xs_blk = xs_ref[...]

        @pl.when(c == 0)
        def _():
            for j in range(HPG):
                state[g * HPG + j] = jnp.zeros((NSTATE, HP), f32)

        prevs = [state[g * HPG + j] for j in range(HPG)]
        y_off_all = jnp.dot(Cb.astype(MXU), jnp.concatenate(prevs, axis=1).astype(MXU), preferred_element_type=f32)
        ys, xds, e_ends = [], [], []
        for j in range(HPG):
            oh_l, dt_h, ac_h, acr_h = _head_cols(g * HPG + j, dt_s, acum_s, acumT_s)
            decay = jnp.exp(jnp.where(causal, ac_h - acr_h, -1e30))
            w = (cb * decay).astype(MXU)
            xs_h = xs_blk[:, HP * j:HP * (j + 1)]
            xd = xs_h * dt_h
            y_diag = jnp.dot(w, xd.astype(MXU), preferred_element_type=f32)
            y_off = y_off_all[:, HP * j:HP * (j + 1)] * jnp.exp(ac_h)
            d_h = jnp.sum(d_ref[...] * oh_l, axis=1, keepdims=True)
            ys.append(y_diag + y_off + xs_h * d_h)
            a_end = ac_h[CHUNK - 1:CHUNK, :]
            xds.append(xd * jnp.exp(a_end - ac_h))
            e_ends.append(jnp.exp(a_end))
        s_c = lax.dot_general(Bb.astype(MXU), jnp.concatenate(xds, axis=1).astype(MXU), (((0,), (0,)), ((), ())),
                              preferred_element_type=f32)
        for j in range(HPG):
            st_ref[0, j] = prevs[j]
            state[g * HPG + j] = prevs[j] * e_ends[j] + s_c[:, HP * j:HP * (j + 1)]
        y_ref[...] = jnp.concatenate(ys, axis=1)

    par = pl.BlockSpec((1, LANES), lambda c, g: (0, 0))
    return _call(
        body, [xs, Bm, Cm, dt_pre, dt_bias, a_log, d_skip], comm=comm, name="ssd_fwd", grid=(NCHUNK, GROUPS),
        in_specs=[pl.BlockSpec((CHUNK, HPG * HP), lambda c, g: (c, g)),
                  pl.BlockSpec((CHUNK, NSTATE), lambda c, g: (c, g)),
                  pl.BlockSpec((CHUNK, NSTATE), lambda c, g: (c, g)),
                  pl.BlockSpec((CHUNK, LANES), lambda c, g: (c, 0)), par, par, par],
        out_specs=[pl.BlockSpec((CHUNK, HPG * HP), lambda c, g: (c, g)),
                   pl.BlockSpec((1, HPG, NSTATE, HP), lambda c, g: (c, g, 0, 0))],
        out_shape=[jax.ShapeDtypeStruct((S, INNER), f32), jax.ShapeDtypeStruct((NCHUNK, HEADS, NSTATE, HP), f32)],
        scratch=[pltpu.VMEM((HEADS, NSTATE, HP), f32), pltpu.VMEM((CHUNK, LANES), f32),
                 pltpu.VMEM((CHUNK, LANES), f32), pltpu.VMEM((LANES, CHUNK), f32)],
        sem=("arbitrary", "arbitrary"))


def ssd_bwd(xs, Bm, Cm, dt_pre, dt_bias, a_log, d_skip, states, dy, comm=None):
    rev = lambda c: NCHUNK - 1 - c

    def body(xs_ref, b_ref, c_ref, dtp_ref, bias_ref, alog_ref, d_ref, st_ref, dy_ref,
             dxs_ref, db_ref, dc_ref, ddt_ref, dbias_ref, dalog_ref, dd_ref,
             dstate, dt_s, acum_s, acumT_s, dacum_s, ddt_s, da_s):
        c, g = pl.program_id(0), pl.program_id(1)

        @pl.when(g == 0)
        def _():
            _ssd_prep(dtp_ref[...], bias_ref[...], alog_ref[...], dt_s, acum_s, acumT_s)
            dacum_s[...] = jnp.zeros_like(dacum_s)
            ddt_s[...] = jnp.zeros_like(ddt_s)

        @pl.when((c == 0) & (g == 0))
        def _():
            da_s[...] = jnp.zeros_like(da_s)
            dd_ref[...] = jnp.zeros_like(dd_ref)
            dbias_ref[...] = jnp.zeros_like(dbias_ref)
            dalog_ref[...] = jnp.zeros_like(dalog_ref)

        row = lax.broadcasted_iota(jnp.int32, (CHUNK, CHUNK), 0)
        col = lax.broadcasted_iota(jnp.int32, (CHUNK, CHUNK), 1)
        sub_l = lax.broadcasted_iota(jnp.int32, (CHUNK, 1), 0)
        last = (sub_l == CHUNK - 1).astype(f32)
        nt = (((1,), (1,)), ((), ()))
        tn = (((0,), (0,)), ((), ()))
        Bb, Cb = b_ref[...], c_ref[...]
        Bm_, Cm_ = Bb.astype(MXU), Cb.astype(MXU)
        cb = lax.dot_general(Cm_, Bm_, nt, preferred_element_type=f32)
        bc = lax.dot_general(Bm_, Cm_, nt, preferred_element_type=f32)
        xs_blk, dy_blk = xs_ref[...], dy_ref[...]
        dxs, dB, dC = [], jnp.zeros((CHUNK, NSTATE), f32), jnp.zeros((CHUNK, NSTATE), f32)
        for j in range(HPG):
            h = g * HPG + j
            oh_l, dt_h, ac_h, acr_h = _head_cols(h, dt_s, acum_s, acumT_s)

            @pl.when(c == 0)
            def _():
                dstate[h] = jnp.zeros((NSTATE, HP), f32)

            dnext = dstate[h]
            prev = st_ref[0, j]
            lm = jnp.exp(jnp.where(col <= row, ac_h - acr_h, -1e30))
            lmT = jnp.exp(jnp.where(row <= col, acr_h - ac_h, -1e30))
            xs_h = xs_blk[:, HP * j:HP * (j + 1)]
            dy_h = dy_blk[:, HP * j:HP * (j + 1)]
            xd = xs_h * dt_h
            xdm, dym = xd.astype(MXU), dy_h.astype(MXU)
            ea = jnp.exp(ac_h)
            a_end = ac_h[CHUNK - 1:CHUNK, :]
            e_end = jnp.exp(a_end)
            dte = jnp.exp(a_end - ac_h)
            dnm, pvm = dnext.astype(MXU), prev.astype(MXU)
            bd = jnp.dot(Bm_, dnm, preferred_element_type=f32)
            dxd = jnp.dot((bc * lmT).astype(MXU), dym, preferred_element_type=f32) + dte * bd
            dw = lax.dot_general(dym, xdm, nt, preferred_element_type=f32)
            dwT = lax.dot_general(xdm, dym, nt, preferred_element_type=f32)
            dcb = dw * lm
            dbc = dwT * lmT
            eady = (ea * dy_h).astype(MXU)
            dC = dC + jnp.dot(dcb.astype(MXU), Bm_, preferred_element_type=f32) \
                + lax.dot_general(eady, pvm, nt, preferred_element_type=f32)
            dB = dB + jnp.dot(dbc.astype(MXU), Cm_, preferred_element_type=f32) \
                + dte * lax.dot_general(xdm, dnm, nt, preferred_element_type=f32)
            dstate[h] = lax.dot_general(Cm_, eady, tn, preferred_element_type=f32) + e_end * dnext
            r1 = jnp.sum(dcb * cb, axis=1, keepdims=True)
            r2 = jnp.sum(dbc * bc, axis=1, keepdims=True)
            y_off = jnp.dot(Cm_, pvm, preferred_element_type=f32) * ea
            t3 = jnp.sum(dy_h * y_off, axis=1, keepdims=True)
            t4 = jnp.sum(bd * xd, axis=1, keepdims=True) * dte
            end_extra = jnp.sum(t4, axis=0, keepdims=True) + e_end * jnp.sum(jnp.sum(prev * dnext, axis=1, keepdims=True), axis=0, keepdims=True)
            dacum_h = r1 - r2 + t3 - t4 + last * end_extra
            dacum_s[...] += dacum_h * oh_l
            ddt_s[...] += jnp.sum(dxd * xs_h, axis=1, keepdims=True) * oh_l
            d_h = jnp.sum(d_ref[...] * oh_l, axis=1, keepdims=True)
            dxs.append(dxd * dt_h + dy_h * d_h)
            dd_ref[...] += oh_l * jnp.sum(jnp.sum(dy_h * xs_h, axis=1, keepdims=True), axis=0, keepdims=True)
        dxs_ref[...] = jnp.concatenate(dxs, axis=1)
        db_ref[...] = dB
        dc_ref[...] = dC

        @pl.when(g == GROUPS - 1)
        def _():
            a_row = -jnp.exp(alog_ref[...])
            da = jnp.dot((row <= col).astype(f32), dacum_s[...], precision=lax.Precision.HIGHEST, preferred_element_type=f32)
            da_s[...] += jnp.sum(da * dt_s[...], axis=0, keepdims=True)
            z = dtp_ref[...] + bias_ref[...]
            ddt_pre = (ddt_s[...] + da * a_row) * _sigmoid(z)
            ddt_ref[...] = ddt_pre.astype(ddt_ref.dtype)
            dbias_ref[...] += jnp.sum(ddt_pre, axis=0, keepdims=True)

            @pl.when(c == NCHUNK - 1)
            def _():
                dalog_ref[...] = da_s[...] * a_row

    par = pl.BlockSpec((1, LANES), lambda c, g: (0, 0))
    return _call(
        body, [xs, Bm, Cm, dt_pre, dt_bias, a_log, d_skip, states, dy], comm=comm, name="ssd_bwd", grid=(NCHUNK, GROUPS),
        in_specs=[pl.BlockSpec((CHUNK, HPG * HP), lambda c, g: (rev(c), g)),
                  pl.BlockSpec((CHUNK, NSTATE), lambda c, g: (rev(c), g)),
                  pl.BlockSpec((CHUNK, NSTATE), lambda c, g: (rev(c), g)),
                  pl.BlockSpec((CHUNK, LANES), lambda c, g: (rev(c), 0)), par, par, par,
                  pl.BlockSpec((1, HPG, NSTATE, HP), lambda c, g: (rev(c), g, 0, 0)),
                  pl.BlockSpec((CHUNK, HPG * HP), lambda c, g: (rev(c), g))],
        out_specs=[pl.BlockSpec((CHUNK, HPG * HP), lambda c, g: (rev(c), g)),
                   pl.BlockSpec((CHUNK, NSTATE), lambda c, g: (rev(c), g)),
                   pl.BlockSpec((CHUNK, NSTATE), lambda c, g: (rev(c), g)),
                   pl.BlockSpec((CHUNK, LANES), lambda c, g: (rev(c), 0)), par, par, par],
        out_shape=[jax.ShapeDtypeStruct((S, INNER), f32), jax.ShapeDtypeStruct((S, GROUPS * NSTATE), f32),
                   jax.ShapeDtypeStruct((S, GROUPS * NSTATE), f32), jax.ShapeDtypeStruct((S, LANES), MXU),
                   jax.ShapeDtypeStruct((1, LANES), f32), jax.ShapeDtypeStruct((1, LANES), f32),
                   jax.ShapeDtypeStruct((1, LANES), f32)],
        scratch=[pltpu.VMEM((HEADS, NSTATE, HP), f32), pltpu.VMEM((CHUNK, LANES), f32),
                 pltpu.VMEM((CHUNK, LANES), f32), pltpu.VMEM((LANES, CHUNK), f32),
                 pltpu.VMEM((CHUNK, LANES), f32), pltpu.VMEM((CHUNK, LANES), f32), pltpu.VMEM((1, LANES), f32)],
        sem=("arbitrary", "arbitrary"))


ATT_STACK_FWD, ATT_STACK_BWD = 4, 2


def _attn_kv(kp, kc, vp, vc, hk):
    sl = slice(AH * hk, AH * (hk + 1))
    return (jnp.concatenate([kp[:, sl], kc[:, sl]], axis=0).astype(MXU),
            jnp.concatenate([vp[:, sl], vc[:, sl]], axis=0).astype(MXU))


def _stack_heads(x, heads):
    return jnp.concatenate([x[:, AH * h:AH * (h + 1)] for h in heads], axis=0)


def _attn_block(n, q, kb, sinks, heads):
    rows = len(heads) * WIN
    qi = lax.broadcasted_iota(jnp.int32, (rows, 2 * WIN), 0) & (WIN - 1)
    ki = lax.broadcasted_iota(jnp.int32, (rows, 2 * WIN), 1)
    rel = qi + WIN - ki
    mask = (rel >= 0) & (rel < WIN) & ((ki >= WIN) | (n > 0))
    qg = _stack_heads(q, heads).astype(MXU)
    s = lax.dot_general(qg, kb, (((1,), (1,)), ((), ())), preferred_element_type=f32) * (AH ** -0.5)
    s = jnp.where(mask, s, -1e30)
    sink = jnp.concatenate([jnp.broadcast_to(sinks[:, h:h + 1], (WIN, 1)) for h in heads], axis=0)
    m = jnp.maximum(jnp.max(s, axis=1, keepdims=True), sink)
    p = jnp.exp(s - m)
    ps = jnp.exp(sink - m)
    inv = 1.0 / (jnp.sum(p, axis=1, keepdims=True) + ps)
    return qg, p * inv, ps * inv


def _head_blocks(hk, stack):
    return [list(range(QPK * hk + i, QPK * hk + i + stack)) for i in range(0, QPK, stack)]


def _kv_specs():
    prev = lambda n: (jnp.maximum(n - 1, 0), 0)
    cur = lambda n: (n, 0)
    w = KVH * AH
    return [pl.BlockSpec((WIN, w), prev), pl.BlockSpec((WIN, w), cur), pl.BlockSpec((WIN, w), prev), pl.BlockSpec((WIN, w), cur)]


def attn_fwd(q, k, v, sinks, comm=None):
    def body(q_ref, kp_ref, kc_ref, vp_ref, vc_ref, s_ref, o_ref):
        n = pl.program_id(0)
        q_, kp, kc, vp, vc, sk = q_ref[...], kp_ref[...], kc_ref[...], vp_ref[...], vc_ref[...], s_ref[...]
        outs = []
        for hk in range(KVH):
            kb, vb = _attn_kv(kp, kc, vp, vc, hk)
            for heads in _head_blocks(hk, ATT_STACK_FWD):
                _, pr, _ = _attn_block(n, q_, kb, sk, heads)
                o = jnp.dot(pr.astype(MXU), vb, preferred_element_type=f32)
                outs += [o[WIN * i:WIN * (i + 1)] for i in range(len(heads))]
        o_ref[...] = jnp.concatenate(outs, axis=1)

    return _call(
        body, [q, k, k, v, v, sinks], comm=comm, name="attn_fwd", grid=(NBLK,),
        in_specs=[pl.BlockSpec((WIN, D), lambda n: (n, 0))] + _kv_specs() + [pl.BlockSpec((1, QH), lambda n: (0, 0))],
        out_specs=[pl.BlockSpec((WIN, D), lambda n: (n, 0))],
        out_shape=[jax.ShapeDtypeStruct((S, D), f32)], sem=("parallel",))[0]


def attn_bwd(q, k, v, sinks, dout, comm=None):
    def body(q_ref, kp_ref, kc_ref, vp_ref, vc_ref, s_ref, do_ref, dq_ref, dkp_ref, dkc_ref, dvp_ref, dvc_ref, ds_ref):
        n = pl.program_id(0)

        @pl.when(n == 0)
        def _():
            ds_ref[...] = jnp.zeros_like(ds_ref)

        q_, kp, kc, vp, vc, sk, do = q_ref[...], kp_ref[...], kc_ref[...], vp_ref[...], vc_ref[...], s_ref[...], do_ref[...]
        lane = lax.broadcasted_iota(jnp.int32, (1, QH), 1)
        nt = (((1,), (1,)), ((), ()))
        tn = (((0,), (0,)), ((), ()))
        dqs, dkps, dkcs, dvps, dvcs = [], [], [], [], []
        dsink = jnp.zeros((1, QH), f32)
        for hk in range(KVH):
            kb, vb = _attn_kv(kp, kc, vp, vc, hk)
            dkb, dvb = jnp.zeros((2 * WIN, AH), f32), jnp.zeros((2 * WIN, AH), f32)
            for heads in _head_blocks(hk, ATT_STACK_BWD):
                qg, pr, prs = _attn_block(n, q_, kb, sk, heads)
                dog = _stack_heads(do, heads).astype(MXU)
                dp = lax.dot_general(dog, vb, nt, preferred_element_type=f32)
                dvb = dvb + lax.dot_general(pr.astype(MXU), dog, tn, preferred_element_type=f32)
                delta = jnp.sum(pr * dp, axis=1, keepdims=True)
                ds = (pr * (dp - delta)).astype(MXU)
                dsk = -prs * delta
                for i, h in enumerate(heads):
                    dsink = dsink + jnp.sum(dsk[WIN * i:WIN * (i + 1)], axis=0, keepdims=True) * (lane == h).astype(f32)
                dqg = jnp.dot(ds, kb, preferred_element_type=f32) * (AH ** -0.5)
                dkb = dkb + lax.dot_general(ds, qg, tn, preferred_element_type=f32) * (AH ** -0.5)
                dqs += [dqg[WIN * i:WIN * (i + 1)] for i in range(len(heads))]
            dkps.append(dkb[:WIN])
            dkcs.append(dkb[WIN:])
            dvps.append(dvb[:WIN])
            dvcs.append(dvb[WIN:])
        dq_ref[...] = jnp.concatenate(dqs, axis=1)
        dkp_ref[...] = jnp.concatenate(dkps, axis=1)
        dkc_ref[...] = jnp.concatenate(dkcs, axis=1)
        dvp_ref[...] = jnp.concatenate(dvps, axis=1)
        dvc_ref[...] = jnp.concatenate(dvcs, axis=1)
        ds_ref[...] += dsink

    w = KVH * AH
    blk = lambda width: pl.BlockSpec((WIN, width), lambda n: (n, 0))
    return _call(
        body, [q, k, k, v, v, sinks, dout], comm=comm, name="attn_bwd", grid=(NBLK,),
        in_specs=[blk(D)] + _kv_specs() + [pl.BlockSpec((1, QH), lambda n: (0, 0)), blk(D)],
        out_specs=[blk(D), blk(w), blk(w), blk(w), blk(w), pl.BlockSpec((1, QH), lambda n: (0, 0))],
        out_shape=[jax.ShapeDtypeStruct((S, D), f32)] + [jax.ShapeDtypeStruct((S, w), f32)] * 4 + [jax.ShapeDtypeStruct((1, QH), f32)],
        sem=("arbitrary",))


def kv_bwd(kv, pos, inv_freq, k_norm, dkp, dkc, dvp, dvc):
    w = KVH * AH

    def body(kv_ref, pos_ref, if_ref, g_ref, dkp_ref, dkc_ref, dvp_ref, dvc_ref, o_ref, dg_ref, db_ref):
        n = pl.program_id(0)

        @pl.when(n == 0)
        def _():
            dg_ref[...] = jnp.zeros_like(dg_ref)
            db_ref[...] = jnp.zeros_like(db_ref)

        inside = (n < NBLK - 1).astype(f32)
        dk = dkc_ref[...] + inside * dkp_ref[...]
        dv = dvc_ref[...] + inside * dvp_ref[...]
        cos, sin = _rope_tables(pos_ref[...], if_ref[...])
        dkpre, dg = _headnorm_rope_bwd(kv_ref[...], g_ref[...], cos, sin, dk, KVH)
        dkv = jnp.concatenate([dkpre, dv], axis=1)
        o_ref[...] = dkv.astype(o_ref.dtype)
        dg_ref[...] += dg
        db_ref[...] += jnp.sum(dkv, axis=0, keepdims=True)

    nxt = lambda n: (jnp.minimum(n + 1, NBLK - 1), 0)
    cur = lambda n: (n, 0)
    const = lambda n: (0, 0)
    return pl.pallas_call(
        body, name="kv_bwd", grid=(NBLK,),
        in_specs=[pl.BlockSpec((WIN, w), cur), pl.BlockSpec((WIN, 1), cur), pl.BlockSpec((1, AH // 2), const),
                  pl.BlockSpec((1, AH), const), pl.BlockSpec((WIN, w), nxt), pl.BlockSpec((WIN, w), cur),
                  pl.BlockSpec((WIN, w), nxt), pl.BlockSpec((WIN, w), cur)],
        out_specs=[pl.BlockSpec((WIN, 2 * w), cur), pl.BlockSpec((1, AH), const), pl.BlockSpec((1, 2 * w), const)],
        out_shape=[jax.ShapeDtypeStruct((S, 2 * w), MXU), jax.ShapeDtypeStruct((1, AH), f32), jax.ShapeDtypeStruct((1, 2 * w), f32)],
        compiler_params=_cparams(("arbitrary",)),
    )(kv, pos, inv_freq, k_norm, dkp, dkc, dvp, dvc)


def _adam_math(w, g, m, v):
    m = ADAM_B1 * m + (1.0 - ADAM_B1) * g
    v = ADAM_B2 * v + (1.0 - ADAM_B2) * (g * g)
    m_hat = m / (1.0 - ADAM_B1 ** ADAM_STEP)
    v_hat = v / (1.0 - ADAM_B2 ** ADAM_STEP)
    return -ADAM_LR * (m_hat / (jnp.sqrt(v_hat) + ADAM_EPS) + ADAM_WD * w), m, v


def adamw(name, w, g, m, v):
    R, C = w.shape
    tr = _pick(R, (256, 128, 64, 32, 16, 8))
    tc = C if tr < R or C % 256 else 256

    def body(w_ref, g_ref, m_ref, v_ref, d_ref, nm_ref, nv_ref):
        d_ref[...], nm_ref[...], nv_ref[...] = _adam_math(w_ref[...], g_ref[...], m_ref[...], v_ref[...])

    spec = pl.BlockSpec((tr, tc), lambda i, j: (i, j))
    return pl.pallas_call(
        body, name=name, grid=(R // tr, C // tc), in_specs=[spec] * 4, out_specs=[spec] * 3,
        out_shape=[jax.ShapeDtypeStruct((R, C), f32)] * 3, compiler_params=_cparams(("parallel", "parallel")),
    )(w, g, m, v)


def _me():
    return lax.axis_index("x"), lax.axis_index("y"), lax.axis_index("c")


def gather_comm(xs):
    n = len(xs)

    def parts(x_refs, o_refs, sems):
        send_sems, recv_sems, local_sems = sems
        x, y, c = _me()
        me, sibling = (x, y, c), (x, y, 1 - c)
        chips = [(1 - x, y), (x, 1 - y), (1 - x, 1 - y)]

        def copy(a, k, block, to, src=None):
            dst = o_refs[a].at[4 * block[0] + 2 * block[1] + block[2]]
            return pltpu.make_async_remote_copy(
                src_ref=dst if src is None else src, dst_ref=dst,
                send_sem=send_sems.at[7 * a + k], recv_sem=recv_sems.at[7 * a + k], device_id=to, device_id_type=MESH)

        mine = [pltpu.make_async_copy(x_refs[a], o_refs[a].at[4 * x + 2 * y + c], local_sems.at[a]) for a in range(n)]
        first = []
        for a in range(n):
            first.append(copy(a, 0, me, sibling, src=x_refs[a]))
            first += [copy(a, 1 + j, me, (*chip, c), src=x_refs[a]) for j, chip in enumerate(chips)]
        return copy, mine, first, me, sibling, chips, c

    def start(x_refs, o_refs, sems):
        _, mine, first, *_ = parts(x_refs, o_refs, sems)
        for cp in mine + first:
            cp.start()

    def finish(x_refs, o_refs, sems):
        copy, mine, first, me, sibling, chips, c = parts(x_refs, o_refs, sems)
        passed = []
        for j, chip in enumerate(chips):
            for a in range(n):
                copy(a, 1 + j, (*chip, c), me).wait_recv()
                cp = copy(a, 4 + j, (*chip, c), sibling)
                cp.start()
                passed.append(cp)
        for a in range(n):
            copy(a, 0, sibling, me).wait_recv()
            for j, chip in enumerate(chips):
                copy(a, 4 + j, (*chip, 1 - c), me).wait_recv()
        for cp in first + passed:
            cp.wait_send()
        for cp in mine:
            cp.wait()

    return Comm(xs, [jax.ShapeDtypeStruct((N_DEV,) + a.shape, a.dtype) for a in xs],
                [pltpu.SemaphoreType.DMA((7 * n,)), pltpu.SemaphoreType.DMA((7 * n,)), pltpu.SemaphoreType.DMA((n,))], start, finish)


def run_comm(name, comm):
    _call(lambda: None, [], name=name, grid=(1,), in_specs=[], out_specs=[], out_shape=[], comm=comm)
    return comm.results


def sibling_comm(gs):
    n = len(gs)

    def copies(g_refs, o_refs, sems):
        x, y, c = _me()
        return [pltpu.make_async_remote_copy(
            src_ref=g_refs[a].at[:, 1 - c], dst_ref=o_refs[a], send_sem=sems[0].at[a], recv_sem=sems[1].at[a],
            device_id=(x, y, 1 - c), device_id_type=MESH) for a in range(n)]

    def start(g_refs, o_refs, sems):
        for cp in copies(g_refs, o_refs, sems):
            cp.start()

    def finish(g_refs, o_refs, sems):
        for cp in copies(g_refs, o_refs, sems):
            cp.wait()

    return Comm(gs, [jax.ShapeDtypeStruct((4,) + g.shape[2:], g.dtype) for g in gs],
                [pltpu.SemaphoreType.DMA((n,)), pltpu.SemaphoreType.DMA((n,))], start, finish)


def chip_comm(ts):
    n = len(ts)

    def copies(t_refs, o_refs, sems):
        x, y, c = _me()
        chips = [(1 - x, y), (x, 1 - y), (1 - x, 1 - y)]
        return [pltpu.make_async_remote_copy(
            src_ref=t_refs[a].at[2 * px + py], dst_ref=o_refs[a].at[j],
            send_sem=sems[0].at[3 * a + j], recv_sem=sems[1].at[3 * a + j],
            device_id=(px, py, c), device_id_type=MESH) for j, (px, py) in enumerate(chips) for a in range(n)]

    def start(t_refs, o_refs, sems):
        for cp in copies(t_refs, o_refs, sems):
            cp.start()

    def finish(t_refs, o_refs, sems):
        for cp in copies(t_refs, o_refs, sems):
            cp.wait()

    return Comm(ts, [jax.ShapeDtypeStruct((3,) + t.shape[1:], t.dtype) for t in ts],
                [pltpu.SemaphoreType.DMA((3 * n,)), pltpu.SemaphoreType.DMA((3 * n,))], start, finish)


def _row_tile(rows):
    return _pick(rows, (512, 304, 256, 128))


def pair_add(name, g, r):
    _, _, R, C = g.shape
    tr = _row_tile(R)

    def body(c_ref, g_ref, r_ref, o_ref):
        o_ref[0] = (g_ref[0, 0].astype(f32) + r_ref[0].astype(f32)).astype(o_ref.dtype)

    return pl.pallas_call(
        body, name=name,
        grid_spec=pltpu.PrefetchScalarGridSpec(
            num_scalar_prefetch=1, grid=(4, R // tr),
            in_specs=[pl.BlockSpec((1, 1, tr, C), lambda p, i, c: (p, c[0], i, 0)),
                      pl.BlockSpec((1, tr, C), lambda p, i, c: (p, i, 0))],
            out_specs=pl.BlockSpec((1, tr, C), lambda p, i, c: (p, i, 0))),
        out_shape=jax.ShapeDtypeStruct((4, R, C), g.dtype),
        compiler_params=_cparams(("parallel", "parallel")),
    )(lax.axis_index("c").reshape(1).astype(jnp.int32), g, r)


def _sum_of_four(t_ref, r_ref):
    return ((t_ref[0].astype(f32) + r_ref[0].astype(f32)) + r_ref[1].astype(f32)) + r_ref[2].astype(f32)


def _my_chip():
    return (2 * lax.axis_index("x") + lax.axis_index("y")).reshape(1).astype(jnp.int32)


def final_adamw(name, t, r, w, m, v):
    _, R, C = t.shape
    tr = _pick(R, (256, 128, 64, 32, 16))
    tc = C if tr < R or C % 256 else 256

    def body(p_ref, t_ref, r_ref, w_ref, m_ref, v_ref, g_ref, d_ref, nm_ref, nv_ref):
        g_ = _sum_of_four(t_ref, r_ref)
        g_ref[...] = g_
        d_ref[...], nm_ref[...], nv_ref[...] = _adam_math(w_ref[...], g_, m_ref[...], v_ref[...])

    flat = pl.BlockSpec((tr, tc), lambda i, j, p: (i, j))
    return pl.pallas_call(
        body, name=name,
        grid_spec=pltpu.PrefetchScalarGridSpec(
            num_scalar_prefetch=1, grid=(R // tr, C // tc),
            in_specs=[pl.BlockSpec((1, tr, tc), lambda i, j, p: (p[0], i, j)),
                      pl.BlockSpec((3, tr, tc), lambda i, j, p: (0, i, j)), flat, flat, flat],
            out_specs=[flat] * 4),
        out_shape=[jax.ShapeDtypeStruct((R, C), f32)] * 4,
        compiler_params=_cparams(("parallel", "parallel")),
    )(_my_chip(), t, r, w, m, v)


def final_adamw_layers(name, t0, r0, t1, r1, w, m, v):
    _, R, C = t0.shape
    tr = _pick(R, (256, 128, 64, 32, 16))

    def body(p_ref, t0_ref, r0_ref, t1_ref, r1_ref, w_ref, m_ref, v_ref, g_ref, d_ref, nm_ref, nv_ref):
        g_ = jnp.where(pl.program_id(0) == 0, _sum_of_four(t0_ref, r0_ref), _sum_of_four(t1_ref, r1_ref))
        g_ref[0] = g_
        d_ref[0], nm_ref[0], nv_ref[0] = _adam_math(w_ref[0], g_, m_ref[0], v_ref[0])

    mine = pl.BlockSpec((1, tr, C), lambda l, i, p: (p[0], i, 0))
    theirs = pl.BlockSpec((3, tr, C), lambda l, i, p: (0, i, 0))
    layer = pl.BlockSpec((1, tr, C), lambda l, i, p: (l, i, 0))
    return pl.pallas_call(
        body, name=name,
        grid_spec=pltpu.PrefetchScalarGridSpec(
            num_scalar_prefetch=1, grid=(2, R // tr),
            in_specs=[mine, theirs, mine, theirs, layer, layer, layer], out_specs=[layer] * 4),
        out_shape=[jax.ShapeDtypeStruct((2, R, C), f32)] * 4,
        compiler_params=_cparams(("parallel", "parallel")),
    )(_my_chip(), t0, r0, t1, r1, w, m, v)


class ReduceScatter:
    def __init__(self, tag, keys, grads):
        self.tag, self.keys, self.grads = tag, keys, grads
        self.send = [g.reshape((4, 2, g.shape[0] // N_DEV) + g.shape[1:]) for g in grads]

    def sibling(self):
        self.c1 = sibling_comm(self.send)
        return self.c1

    def chips(self):
        self.pairs = [pair_add(f"rs_pair_add_{self.tag}{i}", g, r) for i, (g, r) in enumerate(zip(self.send, self.c1.results))]
        self.c2 = chip_comm(self.pairs)
        return self.c2

    def parts(self):
        return {k: (t, r) for k, t, r in zip(self.keys, self.pairs, self.c2.results)}


IN_ROWS = {"z": (0, 2048), "xs": (2048, 4096), "B": (4096, 5120), "C": (5120, 6144)}
IN_COLS = 2 * INNER + 2 * GROUPS * NSTATE + HEADS


def sum_devices(g):
    def body(g_ref, o_ref):
        acc = g_ref[0]
        for i in range(1, N_DEV):
            acc = acc + g_ref[i]
        o_ref[...] = acc

    return pl.pallas_call(body, name="sum_devices", out_shape=jax.ShapeDtypeStruct(g.shape[1:], f32),
                          compiler_params=_cparams())(g)


def _pack(parts, unit, dtype, lead=()):
    flat = jnp.concatenate([p.reshape(lead + (-1,)).astype(dtype) for p in parts], axis=-1)
    n = flat.shape[-1]
    rows = -(-n // (unit * PACK_W)) * unit
    flat = jnp.pad(flat, [(0, 0)] * len(lead) + [(0, rows * PACK_W - n)])
    return flat.reshape(lead + (rows, PACK_W))


def _unpack(buf, shapes, lead=()):
    flat = buf.reshape(lead + (-1,))
    out, off = [], 0
    for shp in shapes:
        n = math.prod(shp)
        out.append(flat[..., off:off + n].reshape(lead + tuple(shp)))
        off += n
    return out


def _pad_lanes(a):
    return jnp.pad(a, [(0, 0)] * (a.ndim - 1) + [(0, LANES - a.shape[-1])])


_NN = (((1,), (0,)), ((), ()))
_NT = (((1,), (1,)), ((), ()))


def _mm(a, b, dims):
    return lax.dot_general(a.astype(MXU), b.astype(MXU), dims, preferred_element_type=f32)


def _ffn_fwd(tag, x, norm_g, w_inT, conv_w, conv_b, mid_comm=None):
    (h,) = rowwise(f"{tag}_norm", lambda x_, g_: (_rms_fwd(x_, g_),), [x], [norm_g], [(D, MXU)])
    ct, nblk = FFN_CT, FFN // FFN_CT

    def body(h_ref, wg_ref, wv_ref, cw_ref, cb_ref, gp_ref, v_ref, a_ref):
        h_ = h_ref[...]
        gp, v_ = _mm(h_, wg_ref[...], _NT), _mm(h_, wv_ref[...], _NT)
        gp_ref[...] = gp
        v_ref[...] = v_
        a_ref[...] = (_silu(_conv(gp, cw_ref[...], cb_ref[...])) * v_).astype(a_ref.dtype)

    col = pl.BlockSpec((S, ct), lambda j: (0, j))
    gate_pre, val, act = _call(
        body, [h, w_inT, w_inT, conv_w, conv_b], comm=mid_comm, name=f"{tag}_in", grid=(nblk,),
        in_specs=[pl.BlockSpec((S, D), lambda j: (0, 0)), pl.BlockSpec((ct, D), lambda j: (j, 0)),
                  pl.BlockSpec((ct, D), lambda j: (nblk + j, 0)), pl.BlockSpec((CONV_F, ct), lambda j: (0, j)),
                  pl.BlockSpec((1, ct), lambda j: (0, j))],
        out_specs=[col, col, col],
        out_shape=[jax.ShapeDtypeStruct((S, FFN), f32), jax.ShapeDtypeStruct((S, FFN), f32), jax.ShapeDtypeStruct((S, FFN), MXU)],
        sem=("parallel",))
    return act, (x, h, gate_pre, val, act)


FFN_CT = 256
CONV_CT = 256


def _proj_conv(name, h, wT, row0, cw, cb, comm=None):
    C, ct = cw.shape[1], CONV_CT

    def body(h_ref, w_ref, cw_ref, cb_ref, p_ref, c_ref):
        p = _mm(h_ref[...], w_ref[...], _NT)
        p_ref[...] = p
        c_ref[...] = _silu(_conv(p, cw_ref[...], cb_ref[...]))

    col = pl.BlockSpec((S, ct), lambda j: (0, j))
    return _call(
        body, [h, wT, cw, cb], comm=comm, name=name, grid=(C // ct,),
        in_specs=[pl.BlockSpec((S, D), lambda j: (0, 0)), pl.BlockSpec((ct, D), lambda j: (row0 // ct + j, 0)),
                  pl.BlockSpec((CONV_A, ct), lambda j: (0, j)), pl.BlockSpec((1, ct), lambda j: (0, j))],
        out_specs=[col, col], out_shape=[jax.ShapeDtypeStruct((S, C), f32)] * 2, sem=("parallel",))


def _dconv_wgrad(name, pre, dconv, cw, cb, h):
    C, ct = cw.shape[1], CONV_CT

    def body(p_ref, do_ref, cw_ref, cb_ref, h_ref, dp_ref, g_ref, dw_ref, db_ref):
        p_, w_ = p_ref[...], cw_ref[...]
        taps = _taps(p_, CONV_A)
        dx, dw, db = _conv_bwd(p_, w_, do_ref[...] * _dsilu(_conv(p_, w_, cb_ref[...], taps)), taps)
        dpm = dx.astype(MXU)
        dp_ref[...] = dpm
        g_ref[...] = lax.dot_general(dpm, h_ref[...].astype(MXU), (((0,), (0,)), ((), ())),
                                     preferred_element_type=f32).astype(g_ref.dtype)
        dw_ref[...] = dw
        db_ref[...] = db

    col = pl.BlockSpec((S, ct), lambda j: (0, j))
    return _call(
        body, [pre, dconv, cw, cb, h], name=name, grid=(C // ct,),
        in_specs=[col, col, pl.BlockSpec((CONV_A, ct), lambda j: (0, j)), pl.BlockSpec((1, ct), lambda j: (0, j)),
                  pl.BlockSpec((S, D), lambda j: (0, 0))],
        out_specs=[col, pl.BlockSpec((ct, D), lambda j: (j, 0)), pl.BlockSpec((CONV_A, ct), lambda j: (0, j)),
                   pl.BlockSpec((1, ct), lambda j: (0, j))],
        out_shape=[jax.ShapeDtypeStruct((S, C), MXU), jax.ShapeDtypeStruct((C, D), MXU),
                   jax.ShapeDtypeStruct((CONV_A, C), f32), jax.ShapeDtypeStruct((1, C), f32)],
        sem=("parallel",))


def _ffn_mid_bwd(name, dout, w_down, gate_pre, val, conv_w, conv_b, comm=None):
    ct = FFN_CT

    def body(do_ref, wd_ref, gp_ref, v_ref, w_ref, b_ref, dgv_ref, dw_ref, db_ref, dob_s):
        @pl.when(pl.program_id(0) == 0)
        def _():
            dob_s[...] = do_ref[...].astype(MXU)

        da = _mm(dob_s[...], wd_ref[...], _NT)
        gp, v_, w_ = gp_ref[...], v_ref[...], w_ref[...]
        taps = _taps(gp, CONV_F)
        gate = _conv(gp, w_, b_ref[...], taps)
        sg = _sigmoid(gate)
        dgp, dw, db = _conv_bwd(gp, w_, da * v_ * (sg * (1.0 + gate * (1.0 - sg))), taps)
        dgv_ref[0] = dgp.astype(dgv_ref.dtype)
        dgv_ref[1] = (da * (gate * sg)).astype(dgv_ref.dtype)
        dw_ref[...] = dw
        db_ref[...] = db

    col = pl.BlockSpec((S, ct), lambda j: (0, j))
    return _call(
        body, [dout, w_down, gate_pre, val, conv_w, conv_b], comm=comm, name=name, grid=(FFN // ct,),
        in_specs=[pl.BlockSpec((S, D), lambda j: (0, 0)), pl.BlockSpec((ct, D), lambda j: (j, 0)), col, col,
                  pl.BlockSpec((CONV_F, ct), lambda j: (0, j)), pl.BlockSpec((1, ct), lambda j: (0, j))],
        out_specs=[pl.BlockSpec((2, S, ct), lambda j: (0, 0, j)), pl.BlockSpec((CONV_F, ct), lambda j: (0, j)),
                   pl.BlockSpec((1, ct), lambda j: (0, j))],
        out_shape=[jax.ShapeDtypeStruct((2, S, FFN), MXU), jax.ShapeDtypeStruct((CONV_F, FFN), f32), jax.ShapeDtypeStruct((1, FFN), f32)],
        scratch=[pltpu.VMEM((S, D), MXU)], sem=("arbitrary",))


def _ffn_bwd(tag, layer, saved, norm_g, w_inT, conv_w, conv_b, w_down, dout, mid_comm=None):
    x, h, gate_pre, val, act = saved
    g_down = matmul(f"{tag}_wdown", act, dout, "tn", out_dtype=MXU)
    dgv, g_cw, g_cb = _ffn_mid_bwd(f"{tag}_dmid", dout, w_down, gate_pre, val, conv_w, conv_b, comm=mid_comm)
    g_inT = matmul_tn_stacked(f"{tag}_win", dgv, h, MXU)

    def din_fn(dg_, dv_, x_, do_, g_, wT):
        dx, dg = _rms_bwd(x_, g_, _mm(dg_, wT[:FFN], _NN) + _mm(dv_, wT[FFN:], _NN))
        return do_ + dx, dg

    rs = ReduceScatter(tag, (f"f_inT{layer}", f"f_down{layer}"), [g_inT, g_down])
    dx, g_norm = rowwise(f"{tag}_din", din_fn, [(dgv, 0), (dgv, 1), x, dout], [norm_g, w_inT], [(D, f32)], [(1, D)],
                         comm=rs.sibling())
    return dx, {f"f_norm{layer}": g_norm, f"f_conv_w{layer}": g_cw, f"f_conv_b{layer}": g_cb}, rs


def _land(W, keys, comm):
    for k, g in zip(keys, comm.results):
        W[k] = g.reshape(-1, g.shape[2])


def _local_step(x, pos, tgt, W, shards):
    G = {}
    gather = lambda *keys: gather_comm([shards[k] for k in keys])
    inv_freq = (ROPE_THETA ** (-jnp.arange(AH // 2, dtype=f32) / (AH // 2))).reshape(1, AH // 2)

    def in_fn(x_, g_, wT, wdtT):
        h_ = _rms_fwd(x_, g_).astype(MXU)
        return h_, _mm(h_, wT[slice(*IN_ROWS["z"])], _NT), _mm(h_, wdtT, _NT)

    h0, z, dt_pre = rowwise("a_in", in_fn, [x], [W["a_norm"], W["inT"], W["in_dtT"]], [(D, MXU), (INNER, f32), (LANES, f32)])
    pre, conv = {}, {}
    early = {"xs": ("a_out",), "B": (), "C": ()}
    for k in ("xs", "B", "C"):
        c = gather(*early[k]) if early[k] else None
        pre[k], conv[k] = _proj_conv(f"a_in_{k}", h0, W["inT"], IN_ROWS[k][0], W[f"cw_{k}"], W[f"cb_{k}"], comm=c)
        if c is not None:
            _land(W, early[k], c)
    c = gather("f_inT0", "f_down0")
    y, states = ssd_fwd(conv["xs"], conv["B"], conv["C"], dt_pre, W["dt_bias"], W["A_log"], W["D"], comm=c)
    _land(W, ("f_inT0", "f_down0"), c)

    def gate_norm(y_, z_, g_):
        yg = y_ * _silu(z_)
        w = INNER // GROUPS
        return (jnp.concatenate([_rms_fwd(yg[:, w * i:w * (i + 1)], g_[:, w * i:w * (i + 1)]) for i in range(GROUPS)], axis=1),)

    def out_fn(y_, z_, x_, g_, w_):
        (gn_,) = gate_norm(y_, z_, g_)
        gn_ = gn_.astype(MXU)
        return gn_, x_ + _mm(gn_, w_, _NN)

    c = gather("f_down1")
    gn, x1 = rowwise("a_out", out_fn, [y, z, x], [W["a_gnorm"], W["a_out"]], [(INNER, MXU), (D, f32)], comm=c)
    _land(W, ("f_down1",), c)

    c = gather("w_kv", "w_q", "w_o")
    act0, ffn0 = _ffn_fwd("f0", x1, W["f_norm0"], W["f_inT0"], W["f_cw0"], W["f_cb0"], mid_comm=c)
    _land(W, ("w_kv", "w_q", "w_o"), c)
    x2 = matmul("f0_down", act0, W["f_down0"], "nn", residual=x1)

    def qkv_fn(x_, gk, gb, wkv, bkv, wq, bq):
        kvn_, h2_ = _rms_fwd(x_, gk).astype(MXU), _rms_fwd(x_, gb).astype(MXU)
        return kvn_, h2_, _mm(kvn_, wkv, _NN) + bkv, _mm(h2_, wq, _NN) + bq

    kw = KVH * AH
    kvn, h2, kv, q_pre = rowwise("qkv_proj", qkv_fn, [x2], [W["kv_norm"], W["b_norm"], W["w_kv"], W["b_kv"], W["w_q"], W["b_q"]],
                                 [(D, MXU), (D, MXU), (2 * kw, f32), (D, f32)])

    def k_fwd(kv_, pos_, if_, g_):
        cos, sin = _rope_tables(pos_, if_)
        return _headnorm_rope_fwd(kv_[:, :kw], g_, cos, sin, KVH), kv_[:, kw:]

    k_rot, v_val = rowwise("k_rope", k_fwd, [kv, pos], [inv_freq, W["k_norm"]], [(kw, f32), (kw, f32)])

    def q_fwd(q_, pos_, if_, g_):
        cos, sin = _rope_tables(pos_, if_)
        return (_headnorm_rope_fwd(q_, g_, cos, sin, QH),)

    (q,) = rowwise("q_rope", q_fwd, [q_pre, pos], [inv_freq, W["q_norm"]], [(D, f32)])
    c = gather("f_inT1")
    att = attn_fwd(q, k_rot, v_val, W["sinks"], comm=c)
    _land(W, ("f_inT1",), c)
    x3 = matmul("o_proj", att, W["w_o"], "nn", bias=W["b_o"], residual=x2)

    act1, ffn1 = _ffn_fwd("f1", x3, W["f_norm1"], W["f_inT1"], W["f_cw1"], W["f_cb1"])

    def loss_fn(a_, x_, t_, w_):
        diff = x_ + _mm(a_, w_, _NN) - t_
        rows = jnp.sum(diff * diff, axis=1, keepdims=True) * (0.5 / D)
        return diff * (1.0 / D), jnp.sum(rows, axis=0, keepdims=True)

    dx4, loss = rowwise("f1_down_loss", loss_fn, [act1, x3, tgt], [W["f_down1"]], [(D, f32)], [(1, 1)])

    dx3, g, rs_f1 = _ffn_bwd("f1", 1, ffn1, W["f_norm1"], W["f_inT1"], W["f_cw1"], W["f_cb1"], W["f_down1"], dx4)
    G.update(g)

    datt = matmul("o_dproj", dx3, W["w_o"], "nt")
    g_wo = matmul("o_wproj", att, dx3, "tn", out_dtype=MXU)
    dq, dkp, dkc, dvp, dvc, G["sinks"] = attn_bwd(q, k_rot, v_val, W["sinks"], datt, comm=rs_f1.chips())

    def q_bwd(q_, pos_, dq_, dx_, if_, g_):
        cos, sin = _rope_tables(pos_, if_)
        dqp, dg = _headnorm_rope_bwd(q_, g_, cos, sin, dq_, QH)
        return dqp, dg, jnp.sum(dqp, axis=0, keepdims=True), jnp.sum(dx_, axis=0, keepdims=True)

    dq_pre, G["q_norm"], G["b_q"], G["b_o"] = rowwise("q_drope", q_bwd, [q_pre, pos, dq, dx3], [inv_freq, W["q_norm"]],
                                                      [(D, MXU)], [(1, AH), (1, D), (1, D)])
    g_wq = matmul("q_wproj", h2, dq_pre, "tn", out_dtype=MXU)
    dkv, G["k_norm"], G["b_kv"] = kv_bwd(kv, pos, inv_freq, W["k_norm"], dkp, dkc, dvp, dvc)
    g_wkv = matmul("kv_wproj", kvn, dkv, "tn", out_dtype=MXU)
    rs_att = ReduceScatter("att", ("w_kv", "w_q", "w_o"), [g_wkv, g_wq, g_wo])

    def x2_bwd(x_, dq_, dkv_, dx_, gb_, gk_, wq, wkv):
        d1, dgb = _rms_bwd(x_, gb_, _mm(dq_, wq, _NT))
        d2, dgk = _rms_bwd(x_, gk_, _mm(dkv_, wkv, _NT))
        return dx_ + d1 + d2, dgb, dgk

    dx2, G["b_norm"], G["kv_norm"] = rowwise("qkv_dproj", x2_bwd, [x2, dq_pre, dkv, dx3],
                                             [W["b_norm"], W["kv_norm"], W["w_q"], W["w_kv"]],
                                             [(D, f32)], [(1, D), (1, D)], comm=rs_att.sibling())

    dx1, g, rs_f0 = _ffn_bwd("f0", 0, ffn0, W["f_norm0"], W["f_inT0"], W["f_cw0"], W["f_cb0"], W["f_down0"], dx2,
                             mid_comm=rs_att.chips())
    G.update(g)

    rs_out = ReduceScatter("a_out", ("a_out",), [matmul("a_wout", gn, dx1, "tn", out_dtype=MXU)])

    def gate_norm_bwd(y_, z_, dx_, g_, w_out):
        dgn_ = _mm(dx_, w_out, _NT)
        w = INNER // GROUPS
        sg = _sigmoid(z_)
        sz = z_ * sg
        yg = y_ * sz
        parts, dgs = [], []
        for i in range(GROUPS):
            dseg, dg = _rms_bwd(yg[:, w * i:w * (i + 1)], g_[:, w * i:w * (i + 1)], dgn_[:, w * i:w * (i + 1)])
            parts.append(dseg)
            dgs.append(dg)
        dyg = jnp.concatenate(parts, axis=1)
        return dyg * sz, dyg * y_ * (sg * (1.0 + z_ * (1.0 - sg))), jnp.concatenate(dgs, axis=1)

    dy, dz, G["a_gnorm"] = rowwise("a_dout", gate_norm_bwd, [y, z, dx1], [W["a_gnorm"], W["a_out"]],
                                   [(INNER, f32), (INNER, MXU)], [(1, INNER)], comm=rs_out.sibling())
    dconv = {}
    dconv["xs"], dconv["B"], dconv["C"], ddt_pre, G["dt_bias"], G["A_log"], G["D"] = ssd_bwd(
        conv["xs"], conv["B"], conv["C"], dt_pre, W["dt_bias"], W["A_log"], W["D"], states, dy,
        comm=merge_comms([rs_f0.chips(), rs_out.chips()]))

    g_in, dpre = [matmul("a_win_z", dz, h0, "tn", out_dtype=MXU)], {}
    for k in ("xs", "B", "C"):
        dpre[k], g_k, G[f"cw_{k}"], G[f"cb_{k}"] = _dconv_wgrad(f"a_dconv_{k}", pre[k], dconv[k], W[f"cw_{k}"], W[f"cb_{k}"], h0)
        g_in.append(g_k)
    g_in.append(matmul("a_win_dt", ddt_pre, h0, "tn", out_dtype=MXU)[:HEADS])
    rs_in = ReduceScatter("a_in", ("inT",), [jnp.concatenate(g_in, axis=0)])
    run_comm("rs_in_sibling", rs_in.sibling())

    def x0_bwd(dz_, dxs_, db_, dc_, ddt_, x_, do_, g_, wT, wdtT):
        parts = zip((dz_, dxs_, db_, dc_), IN_ROWS.values())
        dh = sum(_mm(d_, wT[a:b], _NN) for d_, (a, b) in parts) + _mm(ddt_, wdtT, _NN)
        dx, dg = _rms_bwd(x_, g_, dh)
        return do_ + dx, dg

    dx, G["a_norm"] = rowwise("a_din", x0_bwd, [dz, dpre["xs"], dpre["B"], dpre["C"], ddt_pre, x, dx1],
                              [W["a_norm"], W["inT"], W["in_dtT"]], [(D, f32)], [(1, D)], comm=rs_in.chips())
    return loss, dx, G, [rs_f1, rs_att, rs_f0, rs_out, rs_in]


ROW_KEYS = ("inT", "a_out", "f_inT0", "f_down0", "w_kv", "w_q", "w_o", "f_inT1", "f_down1")


def _row_blocks(src):
    return {"inT": src["a_in_proj"][0].T, "a_out": src["a_out_proj"][0], "w_kv": src["w_kv"], "w_q": src["w_q"][0],
            "w_o": src["w_o"][0], "f_inT0": src["f_w_in"][0].T, "f_inT1": src["f_w_in"][1].T,
            "f_down0": src["f_w_down"][0], "f_down1": src["f_w_down"][1]}


def _from_row_blocks(rb):
    out = {"a_in_proj": rb["inT"].T[None], "a_out_proj": rb["a_out"][None], "w_kv": rb["w_kv"], "w_q": rb["w_q"][None],
           "w_o": rb["w_o"][None]}
    if "f_inT0" in rb:
        out["f_w_in"] = jnp.stack([rb["f_inT0"].T, rb["f_inT1"].T])
        out["f_w_down"] = jnp.stack([rb["f_down0"], rb["f_down1"]])
    return out


SMALL_SHARDED = ("a_norm", "a_conv_w", "a_conv_b", "a_gnorm", "f_conv_w")
REPLICATED = ("a_dt_bias", "a_A_log", "a_D", "kv_norm", "b_kv", "k_norm", "b_norm", "b_q", "q_norm", "sinks", "b_o",
              "f_norm", "f_conv_b")
ORDER = ("a_norm", "a_in_proj", "a_conv_w", "a_conv_b", "a_dt_bias", "a_A_log", "a_D", "a_gnorm", "a_out_proj", "kv_norm",
         "w_kv", "b_kv", "k_norm", "b_norm", "w_q", "b_q", "q_norm", "sinks", "w_o", "b_o", "f_norm", "f_w_in",
         "f_conv_w", "f_conv_b", "f_w_down")


def _gathered_to_whole(name, g):
    if name == "a_conv_w":
        return jnp.moveaxis(g[:, 0], 0, 1).reshape(g.shape[2], -1)
    if name in ("a_norm", "a_conv_b", "a_gnorm"):
        return g[:, 0].reshape(1, -1)
    if name == "f_conv_w":
        return jnp.moveaxis(g, 0, 2).reshape(g.shape[1], g.shape[2], -1)
    raise ValueError(name)


def _whole_to_shards(name, w):
    if name == "a_conv_w":
        return jnp.moveaxis(w.reshape(w.shape[0], N_DEV, -1), 1, 0)[:, None]
    if name in ("a_norm", "a_conv_b", "a_gnorm"):
        return w.reshape(N_DEV, 1, -1)
    if name == "f_conv_w":
        return jnp.moveaxis(w.reshape(w.shape[0], w.shape[1], N_DEV, -1), 2, 0)
    raise ValueError(name)


def _small_weights(whole):
    W = {}
    cw, cb = whole["a_conv_w"], whole["a_conv_b"]
    o = 0
    for k, n in (("xs", INNER), ("B", GROUPS * NSTATE), ("C", GROUPS * NSTATE)):
        W[f"cw_{k}"], W[f"cb_{k}"] = cw[:, o:o + n], cb[:, o:o + n]
        o += n
    W["a_norm"], W["a_gnorm"] = whole["a_norm"], whole["a_gnorm"]
    W["dt_bias"], W["A_log"], W["D"] = (_pad_lanes(whole[k]) for k in ("a_dt_bias", "a_A_log", "a_D"))
    W["kv_norm"], W["b_kv"], W["k_norm"] = whole["kv_norm"].reshape(1, -1), whole["b_kv"].reshape(1, -1), whole["k_norm"].reshape(1, -1)
    for k in ("b_norm", "b_q", "q_norm", "sinks", "b_o"):
        W[k] = whole[k]
    for i in range(2):
        W[f"f_norm{i}"] = whole["f_norm"][i:i + 1]
        W[f"f_cw{i}"], W[f"f_cb{i}"] = whole["f_conv_w"][i], whole["f_conv_b"][i:i + 1]
    return W


def _small_grads(G, shapes):
    nh = HEADS
    out = {
        "a_conv_w": jnp.concatenate([G["cw_xs"], G["cw_B"], G["cw_C"]], axis=1),
        "a_conv_b": jnp.concatenate([G["cb_xs"], G["cb_B"], G["cb_C"]], axis=1),
        "a_norm": G["a_norm"], "a_gnorm": G["a_gnorm"],
        "a_dt_bias": G["dt_bias"][:, :nh], "a_A_log": G["A_log"][:, :nh], "a_D": G["D"][:, :nh],
        "kv_norm": G["kv_norm"], "b_kv": G["b_kv"], "k_norm": G["k_norm"], "b_norm": G["b_norm"],
        "b_q": G["b_q"], "q_norm": G["q_norm"], "sinks": G["sinks"], "b_o": G["b_o"],
        "f_norm": jnp.concatenate([G["f_norm0"], G["f_norm1"]], axis=0),
        "f_conv_w": jnp.stack([G["f_conv_w0"], G["f_conv_w1"]]),
        "f_conv_b": jnp.concatenate([G["f_conv_b0"], G["f_conv_b1"]], axis=0),
    }
    return {k: val.reshape(shapes[k]) if k in shapes else val for k, val in out.items()}


def kernel(x, positions, a_norm, a_in_proj, a_conv_w, a_conv_b, a_dt_bias, a_A_log, a_D, a_gnorm, a_out_proj, kv_norm, w_kv, b_kv, k_norm, b_norm, w_q, b_q, q_norm, sinks, w_o, b_o, f_norm, f_w_in, f_conv_w, f_conv_b, f_w_down, loss_target, m_a_norm, m_a_in_proj, m_a_conv_w, m_a_conv_b, m_a_dt_bias, m_a_A_log, m_a_D, m_a_gnorm, m_a_out_proj, m_kv_norm, m_w_kv, m_b_kv, m_k_norm, m_b_norm, m_w_q, m_b_q, m_q_norm, m_sinks, m_w_o, m_b_o, m_f_norm, m_f_w_in, m_f_conv_w, m_f_conv_b, m_f_w_down, v_a_norm, v_a_in_proj, v_a_conv_w, v_a_conv_b, v_a_dt_bias, v_a_A_log, v_a_D, v_a_gnorm, v_a_out_proj, v_kv_norm, v_w_kv, v_b_kv, v_k_norm, v_b_norm, v_w_q, v_b_q, v_q_norm, v_sinks, v_w_o, v_b_o, v_f_norm, v_f_w_in, v_f_conv_w, v_f_conv_b, v_f_w_down):
    given = dict(locals())
    w_in = {n: given[n] for n in ORDER}
    m_in = {n: given["m_" + n] for n in ORDER}
    v_in = {n: given["v_" + n] for n in ORDER}
    dev = 4 * lax.axis_index("x") + 2 * lax.axis_index("y") + lax.axis_index("c")

    w2, m2, v2 = _row_blocks(w_in), _row_blocks(m_in), _row_blocks(v_in)
    small_pack = _pack([w_in[n] for n in SMALL_SHARDED], 8, f32)
    shards = {k: w2[k].astype(MXU) for k in ROW_KEYS}
    in_all, small_all = run_comm("ag_head", gather_comm([shards["inT"], small_pack]))
    whole = {n: w_in[n] for n in REPLICATED}
    for n, g in zip(SMALL_SHARDED, _unpack(small_all, [w_in[n].shape for n in SMALL_SHARDED], lead=(N_DEV,))):
        whole[n] = _gathered_to_whole(n, g)
    W = _small_weights(whole)
    W["inT"] = in_all.reshape(-1, D)
    W["in_dtT"] = jnp.pad(W["inT"][IN_COLS - HEADS:], ((0, LANES - HEADS), (0, 0)))

    loss, dx, G, scatters = _local_step(x[0], positions.reshape(S, 1).astype(f32), loss_target[0], W, shards)
    grads = _small_grads(G, {n: whole[n].shape for n in REPLICATED})

    small_names = SMALL_SHARDED + REPLICATED
    small_part = _pack([grads[n] for n in small_names], 8, f32)
    small_gather = gather_comm([small_part])
    run_comm("ag_small_grads", small_gather)
    parts = {}
    for rs in scatters:
        parts.update(rs.parts())

    single = tuple(k for k in ROW_KEYS if not k.startswith("f_"))
    stepped = {k: final_adamw(f"adamw_{k}", *parts[k], w2[k], m2[k], v2[k]) for k in single}
    g_out, delta, new_m, new_v = (_from_row_blocks({k: stepped[k][i] for k in single}) for i in range(4))
    for n, key, lay in (("f_w_in", "f_inT", lambda a: jnp.swapaxes(a, 1, 2)), ("f_w_down", "f_down", lambda a: a)):
        res = final_adamw_layers(f"adamw_{n}", *parts[key + "0"], *parts[key + "1"], lay(w_in[n]), lay(m_in[n]), lay(v_in[n]))
        g_out[n], delta[n], new_m[n], new_v[n] = (lay(a) for a in res)
    small_sum = sum_devices(small_gather.results[0])
    for n, g in zip(small_names, _unpack(small_sum, [grads[n].shape for n in small_names])):
        if n in SMALL_SHARDED:
            g_out[n] = lax.dynamic_index_in_dim(_whole_to_shards(n, g), dev, axis=0, keepdims=False)
        else:
            g_out[n] = g.reshape(w_in[n].shape)

    packs = [_pack([src[n] for n in small_names], 8, f32) for src in (w_in, g_out, m_in, v_in)]
    outs = adamw("adamw_small", *packs)
    for dst, buf in zip((delta, new_m, new_v), outs):
        for n, a in zip(small_names, _unpack(buf, [w_in[n].shape for n in small_names])):
            dst[n] = a

    loss_all = lax.psum(loss[0, 0], AXES)
    return (loss_all, dx[None], *[g_out[n] for n in ORDER], *[delta[n] for n in ORDER],
            *[new_m[n] for n in ORDER], *[new_v[n] for n in ORDER])
```

```python
import functools
import math

import jax
import jax.numpy as jnp
from jax import lax
from jax.experimental import pallas as pl
from jax.experimental.pallas import tpu as pltpu

f32 = jnp.float32
bf16 = jnp.bfloat16
MXU = bf16

N_DEV = 8
S = 2048
D = 1024
EPS = 1e-5
INNER = 2048
HEADS = 32
HP = 64
GROUPS = 8
HPG = HEADS // GROUPS
NSTATE = 128
CONV_A = 4
CHUNK = 256
NCHUNK = S // CHUNK
AH = 64
QH = 16
KVH = 4
QPK = QH // KVH
WIN = 128
NBLK = S // WIN
ROPE_THETA = 10000.0
FFN = 2816
CONV_F = 3
LANES = 128
PACK_W = 1024
VMEM_LIMIT = 56 * 1024 * 1024

ADAM_LR, ADAM_B1, ADAM_B2, ADAM_EPS, ADAM_WD, ADAM_STEP = 0.001, 0.9, 0.999, 1e-08, 0.01, 10

MESH = pl.DeviceIdType.MESH
AXES = ("x", "y", "c")


def _cparams(sem=None):
    return pltpu.CompilerParams(dimension_semantics=sem, vmem_limit_bytes=VMEM_LIMIT)


def _pick(n, cands):
    for c in cands:
        if n % c == 0:
            return c
    return n


class Comm:
    def __init__(self, ins, out_shapes, sems, start, finish, relay=None):
        self.ins, self.out_shapes, self.sems, self.start, self.finish = list(ins), list(out_shapes), list(sems), start, finish
        self.relay, self.results, self.children = relay, None, ()

    def set_results(self, res):
        self.results, o = list(res), 0
        for ch in self.children:
            ch.set_results(res[o:o + len(ch.out_shapes)])
            o += len(ch.out_shapes)


def merge_comms(comms):
    def each(fn_name, ins, outs, sems):
        i = o = s = 0
        for c in comms:
            if getattr(c, fn_name) is not None:
                getattr(c, fn_name)(ins[i:i + len(c.ins)], outs[o:o + len(c.out_shapes)], sems[s:s + len(c.sems)])
            i, o, s = i + len(c.ins), o + len(c.out_shapes), s + len(c.sems)

    merged = Comm([a for c in comms for a in c.ins], [a for c in comms for a in c.out_shapes], [a for c in comms for a in c.sems],
                  functools.partial(each, "start"), functools.partial(each, "finish"), functools.partial(each, "relay"))
    merged.children = tuple(comms)
    return merged


def _call(body, args, *, name, grid, in_specs, out_specs, out_shape, scratch=(), sem=None, comm=None):
    if comm is None:
        return pl.pallas_call(body, name=name, grid=grid, in_specs=list(in_specs), out_specs=list(out_specs),
                              out_shape=list(out_shape), scratch_shapes=list(scratch), compiler_params=_cparams(sem))(*args)
    n_in, n_out, n_scr, c_in, c_out = len(in_specs), len(out_shape), len(scratch), len(comm.ins), len(comm.out_shapes)
    any_spec = pl.BlockSpec(memory_space=pl.ANY)

    def outer(*refs):
        ins, c_ins = refs[:n_in], refs[n_in:n_in + c_in]
        o = n_in + c_in
        outs, c_outs = refs[o:o + n_out], refs[o + n_out:o + n_out + c_out]
        o += n_out + c_out
        scr, c_sems = refs[o:o + n_scr], refs[o + n_scr:]
        ids = [pl.program_id(i) for i in range(len(grid))]
        first = functools.reduce(jnp.logical_and, [i == 0 for i in ids])
        last = functools.reduce(jnp.logical_and, [i == g - 1 for i, g in zip(ids, grid)])

        @pl.when(first)
        def _():
            comm.start(c_ins, c_outs, c_sems)

        body(*ins, *outs, *scr)

        if comm.relay is not None:
            step, total = 0, math.prod(grid)
            for i, g in zip(ids, grid):
                step = step * g + i

            @pl.when(step == (3 * total) // 4 - (total > 1))
            def _():
                comm.relay(c_ins, c_outs, c_sems)

        @pl.when(last)
        def _():
            comm.finish(c_ins, c_outs, c_sems)

    res = pl.pallas_call(
        outer, name=name, grid=grid, in_specs=list(in_specs) + [any_spec] * c_in,
        out_specs=list(out_specs) + [any_spec] * c_out, out_shape=list(out_shape) + comm.out_shapes,
        scratch_shapes=list(scratch) + comm.sems, compiler_params=_cparams(("arbitrary",) * len(grid)),
    )(*args, *comm.ins)
    comm.set_results(res[n_out:])
    return res[:n_out]


def matmul(name, a, b, mode, out_dtype=f32, bias=None, residual=None):
    if mode == "nn":
        (M, K), (K2, N) = a.shape, b.shape
    elif mode == "nt":
        (M, K), (N, K2) = a.shape, b.shape
    else:
        (K, M), (K2, N) = a.shape, b.shape
    assert K == K2, (name, a.shape, b.shape)
    if mode == "tn":
        tm, tn = M, _pick(N, (512, 256, 128) if M <= 1024 else (256, 128))
        a_spec = pl.BlockSpec((K, M), lambda j: (0, 0))
        b_spec = pl.BlockSpec((K, tn), lambda j: (0, j))
        dims = (((0,), (0,)), ((), ()))
        grid, o_map, row_map = (N // tn,), (lambda j: (0, j)), (lambda j: (0, j))
    else:
        tm, tn = (256 if N >= 2048 else 512), N
        a_spec = pl.BlockSpec((tm, K), lambda i: (i, 0))
        b_spec = pl.BlockSpec(b.shape, lambda i: (0, 0))
        dims = (((1,), (0,)), ((), ())) if mode == "nn" else (((1,), (1,)), ((), ()))
        grid, o_map, row_map = (M // tm,), (lambda i: (i, 0)), (lambda i: (0, 0))
    ins, in_specs = [a, b], [a_spec, b_spec]
    if bias is not None:
        ins.append(bias)
        in_specs.append(pl.BlockSpec((1, tn), row_map))
    if residual is not None:
        ins.append(residual)
        in_specs.append(pl.BlockSpec((tm, tn), o_map))
    has_bias, has_res = bias is not None, residual is not None

    def body(a_ref, b_ref, *rest):
        rest = list(rest)
        bias_ref = rest.pop(0) if has_bias else None
        res_ref = rest.pop(0) if has_res else None
        (o_ref,) = rest
        r = lax.dot_general(a_ref[...].astype(MXU), b_ref[...].astype(MXU), dims, preferred_element_type=f32)
        if has_bias:
            r = r + bias_ref[...]
        if has_res:
            r = r + res_ref[...]
        o_ref[...] = r.astype(out_dtype)

    return pl.pallas_call(
        body, name=name, grid=grid, in_specs=in_specs,
        out_specs=pl.BlockSpec((tm, tn), o_map),
        out_shape=jax.ShapeDtypeStruct((M, N), out_dtype),
        compiler_params=_cparams(("parallel",)),
    )(*ins)


def matmul_tn_stacked(name, a, b, out_dtype):
    R, K, M = a.shape
    N = b.shape[1]
    tn = _pick(N, (256, 128))

    def body(a_ref, b_ref, o_ref):
        o_ref[0] = lax.dot_general(a_ref[0].astype(MXU), b_ref[...].astype(MXU), (((0,), (0,)), ((), ())),
                                   preferred_element_type=f32).astype(out_dtype)

    out = pl.pallas_call(
        body, name=name, grid=(R, N // tn),
        in_specs=[pl.BlockSpec((1, K, M), lambda r, j: (r, 0, 0)), pl.BlockSpec((K, tn), lambda r, j: (0, j))],
        out_specs=pl.BlockSpec((1, M, tn), lambda r, j: (r, 0, j)),
        out_shape=jax.ShapeDtypeStruct((R, M, N), out_dtype),
        compiler_params=_cparams(("parallel", "parallel")),
    )(a, b)
    return out.reshape(R * M, N)


def rowwise(name, fn, rows, pars, outs, accs=(), tile=256, comm=None):
    n_in, n_out = len(rows) + len(pars), len(outs)
    in_specs = [pl.BlockSpec((None, tile, r[0].shape[2]), functools.partial(lambda i, lead: (lead, i, 0), lead=r[1]))
                if isinstance(r, tuple) else pl.BlockSpec((tile, r.shape[1]), lambda i: (i, 0)) for r in rows]
    rows = [r[0] if isinstance(r, tuple) else r for r in rows]
    in_specs += [pl.BlockSpec(p.shape, lambda i: (0, 0)) for p in pars]
    out_specs = [pl.BlockSpec((tile, c), lambda i: (i, 0)) for c, _ in outs]
    out_specs += [pl.BlockSpec(shp, lambda i: (0, 0)) for shp in accs]
    out_shape = [jax.ShapeDtypeStruct((S, c), dt) for c, dt in outs]
    out_shape += [jax.ShapeDtypeStruct(shp, f32) for shp in accs]

    def body(*refs):
        res = fn(*[r[...] for r in refs[:n_in]])
        o_refs = refs[n_in:n_in + n_out]
        a_refs = refs[n_in + n_out:]
        for ref, val in zip(o_refs, res[:n_out]):
            ref[...] = val.astype(ref.dtype)
        if a_refs:
            @pl.when(pl.program_id(0) == 0)
            def _():
                for ref in a_refs:
                    ref[...] = jnp.zeros_like(ref)
            for ref, val in zip(a_refs, res[n_out:]):
                ref[...] += val

    return _call(body, [*rows, *pars], name=name, grid=(S // tile,), in_specs=in_specs, out_specs=out_specs,
                 out_shape=out_shape, sem=("arbitrary",) if accs else ("parallel",), comm=comm)


def _sigmoid(x):
    return 0.5 * jnp.tanh(0.5 * x) + 0.5


def _silu(x):
    return x * _sigmoid(x)


def _dsilu(x):
    sg = _sigmoid(x)
    return sg * (1.0 + x * (1.0 - sg))


def _softplus(x):
    return jnp.maximum(x, 0.0) + jnp.log(1.0 + jnp.exp(-jnp.abs(x)))


def _rms_fwd(x, g):
    r = lax.rsqrt(jnp.mean(x * x, axis=-1, keepdims=True) + EPS)
    return x * r * g


def _rms_bwd(x, g, dh):
    r = lax.rsqrt(jnp.mean(x * x, axis=-1, keepdims=True) + EPS)
    xh = x * r
    dxh = dh * g
    dx = r * (dxh - xh * jnp.mean(dxh * xh, axis=-1, keepdims=True))
    return dx, jnp.sum(dh * xh, axis=0, keepdims=True)


def _taps(x, width):
    row = lax.broadcasted_iota(jnp.int32, (8, x.shape[1]), 0)

    def shifted(s):
        r = pltpu.roll(x, s, 0)
        return jnp.concatenate([jnp.where(row >= s, r[:8], 0.0), r[8:]], axis=0)

    return [shifted(s) for s in range(width - 1, 0, -1)] + [x]


def _conv(x, w, b, taps=None):
    width = w.shape[0]
    taps = _taps(x, width) if taps is None else taps
    out = b + w[0:1, :] * taps[0]
    for k in range(1, width):
        out = out + w[k:k + 1, :] * taps[k]
    return out


def _conv_bwd(x, w, dc, taps=None):
    width, n = w.shape[0], x.shape[0]
    taps = _taps(x, width) if taps is None else taps
    row = lax.broadcasted_iota(jnp.int32, (8, x.shape[1]), 0)
    dx = w[width - 1:width, :] * dc
    for k in range(width - 1):
        s = width - 1 - k
        r = pltpu.roll(dc, n - s, 0)
        dx = dx + w[k:k + 1, :] * jnp.concatenate([r[:n - 8], jnp.where(row < 8 - s, r[n - 8:], 0.0)], axis=0)
    dw = jnp.concatenate([jnp.sum(dc * t, axis=0, keepdims=True) for t in taps], axis=0)
    return dx, dw, jnp.sum(dc, axis=0, keepdims=True)


def _rope_tables(pos, inv_freq):
    ang = pos * inv_freq
    return jnp.cos(ang), jnp.sin(ang)


def _split2(v):
    hi = v.astype(bf16)
    return hi, (v - hi.astype(f32)).astype(bf16)


def _head_maps(width):
    shift = AH.bit_length() - 1
    to_head = (lax.broadcasted_iota(jnp.int32, (width, LANES), 0) >> shift) == lax.broadcasted_iota(jnp.int32, (width, LANES), 1)
    from_head = lax.broadcasted_iota(jnp.int32, (LANES, width), 0) == (lax.broadcasted_iota(jnp.int32, (LANES, width), 1) >> shift)
    return to_head.astype(bf16), from_head.astype(bf16)


def _head_sums(v, to_head):
    hi, lo = _split2(v)
    return jnp.dot(hi, to_head, preferred_element_type=f32) + jnp.dot(lo, to_head, preferred_element_type=f32)


def _head_spread(s, from_head):
    hi, lo = _split2(s)
    return jnp.dot(hi, from_head, preferred_element_type=f32) + jnp.dot(lo, from_head, preferred_element_type=f32)


def _rope_full(cos, sin, width):
    half = AH // 2
    pad = jnp.zeros((cos.shape[0], LANES - half), f32)
    r = lax.broadcasted_iota(jnp.int32, (LANES, width), 0)
    lane = lax.broadcasted_iota(jnp.int32, (LANES, width), 1)
    spread = ((lane & (half - 1)) == r).astype(bf16)
    full = lambda t: _head_spread(jnp.concatenate([t, pad], axis=1), spread)
    first = (lax.broadcasted_iota(jnp.int32, (1, width), 1) & (AH - 1)) < half
    sin_f = full(sin)
    return full(cos), jnp.where(first, -sin_f, sin_f), first


def _swap_halves(v, first):
    half, width = AH // 2, v.shape[1]
    return jnp.where(first, pltpu.roll(v, width - half, 1), pltpu.roll(v, half, 1))


def _headnorm_rope_fwd(x, g, cos, sin, heads):
    to_head, from_head = _head_maps(heads * AH)
    cos_f, sin_s, first = _rope_full(cos, sin, heads * AH)
    r = _head_spread(lax.rsqrt(_head_sums(x * x, to_head) * (1.0 / AH) + EPS), from_head)
    n = x * r * jnp.tile(g, (1, heads))
    return n * cos_f + _swap_halves(n, first) * sin_s


def _headnorm_rope_bwd(x, g, cos, sin, dout, heads):
    width = heads * AH
    to_head, from_head = _head_maps(width)
    cos_f, sin_s, first = _rope_full(cos, sin, width)
    r = _head_spread(lax.rsqrt(_head_sums(x * x, to_head) * (1.0 / AH) + EPS), from_head)
    xh = x * r
    dn = dout * cos_f - _swap_halves(dout, first) * sin_s
    dxh = dn * jnp.tile(g, (1, heads))
    m = _head_spread(_head_sums(dxh * xh, to_head) * (1.0 / AH), from_head)
    dx = r * (dxh - xh * m)
    dg_lanes = jnp.sum(dn * xh, axis=0, keepdims=True)
    fold = ((lax.broadcasted_iota(jnp.int32, (width, LANES), 0) & (AH - 1))
            == lax.broadcasted_iota(jnp.int32, (width, LANES), 1)).astype(f32)
    dg = jnp.dot(jnp.broadcast_to(dg_lanes, (8, width)), fold, precision=lax.Precision.HIGHEST, preferred_element_type=f32)
    return dx, dg[0:1, :AH]


def _ssd_prep(dt_pre, dt_bias, a_log, dt_s, acum_s, acumT_s):
    dt = _softplus(dt_pre + dt_bias)
    a = dt * (-jnp.exp(a_log))
    row = lax.broadcasted_iota(jnp.int32, (CHUNK, CHUNK), 0)
    col = lax.broadcasted_iota(jnp.int32, (CHUNK, CHUNK), 1)
    dt_s[...] = dt
    acum_s[...] = jnp.dot((col <= row).astype(f32), a, precision=lax.Precision.HIGHEST, preferred_element_type=f32)
    acumT_s[...] = lax.dot_general(a, (row <= col).astype(f32), (((0,), (0,)), ((), ())),
                                   precision=lax.Precision.HIGHEST, preferred_element_type=f32)


def _head_cols(h, dt_s, acum_s, acumT_s):
    lane = lax.broadcasted_iota(jnp.int32, (1, LANES), 1)
    oh_l = (lane == h).astype(f32)
    sub = lax.broadcasted_iota(jnp.int32, (LANES, 1), 0)
    oh_s = (sub == h).astype(f32)
    dt_h = jnp.sum(dt_s[...] * oh_l, axis=1, keepdims=True)
    ac_h = jnp.sum(acum_s[...] * oh_l, axis=1, keepdims=True)
    acr_h = jnp.sum(acumT_s[...] * oh_s, axis=0, keepdims=True)
    return oh_l, dt_h, ac_h, acr_h


def ssd_fwd(xs, Bm, Cm, dt_pre, dt_bias, a_log, d_skip, comm=None):
    def body(xs_ref, b_ref, c_ref, dtp_ref, bias_ref, alog_ref, d_ref, y_ref, st_ref, state, dt_s, acum_s, acumT_s):
        c, g = pl.program_id(0), pl.program_id(1)

        @pl.when(g == 0)
        def _():
            _ssd_prep(dtp_ref[...], bias_ref[...], alog_ref[...], dt_s, acum_s, acumT_s)

        row = lax.broadcasted_iota(jnp.int32, (CHUNK, CHUNK), 0)
        col = lax.broadcasted_iota(jnp.int32, (CHUNK, CHUNK), 1)
        causal = col <= row
        Bb, Cb = b_ref[...], c_ref[...]
        cb = lax.dot_general(Cb.astype(MXU), Bb.astype(MXU), (((1,), (1,)), ((), ())), preferred_element_type=f32)
        xs_blk = xs_ref[...]

        @pl.when(c == 0)
        def _():
            for j in range(HPG):
                state[g * HPG + j] = jnp.zeros((NSTATE, HP), f32)

        prevs = [state[g * HPG + j] for j in range(HPG)]
        y_off_all = jnp.dot(Cb.astype(MXU), jnp.concatenate(prevs, axis=1).astype(MXU), preferred_element_type=f32)
        ys, xds, e_ends = [], [], []
        for j in range(HPG):
            oh_l, dt_h, ac_h, acr_h = _head_cols(g * HPG + j, dt_s, acum_s, acumT_s)
            decay = jnp.exp(jnp.where(causal, ac_h - acr_h, -1e30))
            w = (cb * decay).astype(MXU)
            xs_h = xs_blk[:, HP * j:HP * (j + 1)]
            xd = xs_h * dt_h
            y_diag = jnp.dot(w, xd.astype(MXU), preferred_element_type=f32)
            y_off = y_off_all[:, HP * j:HP * (j + 1)] * jnp.exp(ac_h)
            d_h = jnp.sum(d_ref[...] * oh_l, axis=1, keepdims=True)
            ys.append(y_diag + y_off + xs_h * d_h)
            a_end = ac_h[CHUNK - 1:CHUNK, :]
            xds.append(xd * jnp.exp(a_end - ac_h))
            e_ends.append(jnp.exp(a_end))
        s_c = lax.dot_general(Bb.astype(MXU), jnp.concatenate(xds, axis=1).astype(MXU), (((0,), (0,)), ((), ())),
                              preferred_element_type=f32)
        for j in range(HPG):
            st_ref[0, j] = prevs[j]
            state[g * HPG + j] = prevs[j] * e_ends[j] + s_c[:, HP * j:HP * (j + 1)]
        y_ref[...] = jnp.concatenate(ys, axis=1)

    par = pl.BlockSpec((1, LANES), lambda c, g: (0, 0))
    return _call(
        body, [xs, Bm, Cm, dt_pre, dt_bias, a_log, d_skip], comm=comm, name="ssd_fwd", grid=(NCHUNK, GROUPS),
        in_specs=[pl.BlockSpec((CHUNK, HPG * HP), lambda c, g: (c, g)),
                  pl.BlockSpec((CHUNK, NSTATE), lambda c, g: (c, g)),
                  pl.BlockSpec((CHUNK, NSTATE), lambda c, g: (c, g)),
                  pl.BlockSpec((CHUNK, LANES), lambda c, g: (c, 0)), par, par, par],
        out_specs=[pl.BlockSpec((CHUNK, HPG * HP), lambda c, g: (c, g)),
                   pl.BlockSpec((1, HPG, NSTATE, HP), lambda c, g: (c, g, 0, 0))],
        out_shape=[jax.ShapeDtypeStruct((S, INNER), f32), jax.ShapeDtypeStruct((NCHUNK, HEADS, NSTATE, HP), f32)],
        scratch=[pltpu.VMEM((HEADS, NSTATE, HP), f32), pltpu.VMEM((CHUNK, LANES), f32),
                 pltpu.VMEM((CHUNK, LANES), f32), pltpu.VMEM((LANES, CHUNK), f32)],
        sem=("arbitrary", "arbitrary"))


def ssd_bwd(xs, Bm, Cm, dt_pre, dt_bias, a_log, d_skip, states, dy, comm=None):
    rev = lambda c: NCHUNK - 1 - c

    def body(xs_ref, b_ref, c_ref, dtp_ref, bias_ref, alog_ref, d_ref, st_ref, dy_ref,
             dxs_ref, db_ref, dc_ref, ddt_ref, dbias_ref, dalog_ref, dd_ref,
             dstate, dt_s, acum_s, acumT_s, dacum_s, ddt_s, da_s):
        c, g = pl.program_id(0), pl.program_id(1)

        @pl.when(g == 0)
        def _():
            _ssd_prep(dtp_ref[...], bias_ref[...], alog_ref[...], dt_s, acum_s, acumT_s)
            dacum_s[...] = jnp.zeros_like(dacum_s)
            ddt_s[...] = jnp.zeros_like(ddt_s)

        @pl.when((c == 0) & (g == 0))
        def _():
            da_s[...] = jnp.zeros_like(da_s)
            dd_ref[...] = jnp.zeros_like(dd_ref)
            dbias_ref[...] = jnp.zeros_like(dbias_ref)
            dalog_ref[...] = jnp.zeros_like(dalog_ref)

        row = lax.broadcasted_iota(jnp.int32, (CHUNK, CHUNK), 0)
        col = lax.broadcasted_iota(jnp.int32, (CHUNK, CHUNK), 1)
        sub_l = lax.broadcasted_iota(jnp.int32, (CHUNK, 1), 0)
        last = (sub_l == CHUNK - 1).astype(f32)
        nt = (((1,), (1,)), ((), ()))
        tn = (((0,), (0,)), ((), ()))
        Bb, Cb = b_ref[...], c_ref[...]
        Bm_, Cm_ = Bb.astype(MXU), Cb.astype(MXU)
        cb = lax.dot_general(Cm_, Bm_, nt, preferred_element_type=f32)
        bc = lax.dot_general(Bm_, Cm_, nt, preferred_element_type=f32)
        xs_blk, dy_blk = xs_ref[...], dy_ref[...]
        dxs, dB, dC = [], jnp.zeros((CHUNK, NSTATE), f32), jnp.zeros((CHUNK, NSTATE), f32)
        for j in range(HPG):
            h = g * HPG + j
            oh_l, dt_h, ac_h, acr_h = _head_cols(h, dt_s, acum_s, acumT_s)

            @pl.when(c == 0)
            def _():
                dstate[h] = jnp.zeros((NSTATE, HP), f32)

            dnext = dstate[h]
            prev = st_ref[0, j]
            lm = jnp.exp(jnp.where(col <= row, ac_h - acr_h, -1e30))
            lmT = jnp.exp(jnp.where(row <= col, acr_h - ac_h, -1e30))
            xs_h = xs_blk[:, HP * j:HP * (j + 1)]
            dy_h = dy_blk[:, HP * j:HP * (j + 1)]
            xd = xs_h * dt_h
            xdm, dym = xd.astype(MXU), dy_h.astype(MXU)
            ea = jnp.exp(ac_h)
            a_end = ac_h[CHUNK - 1:CHUNK, :]
            e_end = jnp.exp(a_end)
            dte = jnp.exp(a_end - ac_h)
            dnm, pvm = dnext.astype(MXU), prev.astype(MXU)
            bd = jnp.dot(Bm_, dnm, preferred_element_type=f32)
            dxd = jnp.dot((bc * lmT).astype(MXU), dym, preferred_element_type=f32) + dte * bd
            dw = lax.dot_general(dym, xdm, nt, preferred_element_type=f32)
            dwT = lax.dot_general(xdm, dym, nt, preferred_element_type=f32)
            dcb = dw * lm
            dbc = dwT * lmT
            eady = (ea * dy_h).astype(MXU)
            dC = dC + jnp.dot(dcb.astype(MXU), Bm_, preferred_element_type=f32) \
                + lax.dot_general(eady, pvm, nt, preferred_element_type=f32)
            dB = dB + jnp.dot(dbc.astype(MXU), Cm_, preferred_element_type=f32) \
                + dte * lax.dot_general(xdm, dnm, nt, preferred_element_type=f32)
            dstate[h] = lax.dot_general(Cm_, eady, tn, preferred_element_type=f32) + e_end * dnext
            r1 = jnp.sum(dcb * cb, axis=1, keepdims=True)
            r2 = jnp.sum(dbc * bc, axis=1, keepdims=True)
            y_off = jnp.dot(Cm_, pvm, preferred_element_type=f32) * ea
            t3 = jnp.sum(dy_h * y_off, axis=1, keepdims=True)
            t4 = jnp.sum(bd * xd, axis=1, keepdims=True) * dte
            end_extra = jnp.sum(t4, axis=0, keepdims=True) + e_end * jnp.sum(jnp.sum(prev * dnext, axis=1, keepdims=True), axis=0, keepdims=True)
            dacum_h = r1 - r2 + t3 - t4 + last * end_extra
            dacum_s[...] += dacum_h * oh_l
            ddt_s[...] += jnp.sum(dxd * xs_h, axis=1, keepdims=True) * oh_l
            d_h = jnp.sum(d_ref[...] * oh_l, axis=1, keepdims=True)
            dxs.append(dxd * dt_h + dy_h * d_h)
            dd_ref[...] += oh_l * jnp.sum(jnp.sum(dy_h * xs_h, axis=1, keepdims=True), axis=0, keepdims=True)
        dxs_ref[...] = jnp.concatenate(dxs, axis=1)
        db_ref[...] = dB
        dc_ref[...] = dC

        @pl.when(g == GROUPS - 1)
        def _():
            a_row = -jnp.exp(alog_ref[...])
            da = jnp.dot((row <= col).astype(f32), dacum_s[...], precision=lax.Precision.HIGHEST, preferred_element_type=f32)
            da_s[...] += jnp.sum(da * dt_s[...], axis=0, keepdims=True)
            z = dtp_ref[...] + bias_ref[...]
            ddt_pre = (ddt_s[...] + da * a_row) * _sigmoid(z)
            ddt_ref[...] = ddt_pre.astype(ddt_ref.dtype)
            dbias_ref[...] += jnp.sum(ddt_pre, axis=0, keepdims=True)

            @pl.when(c == NCHUNK - 1)
            def _():
                dalog_ref[...] = da_s[...] * a_row

    par = pl.BlockSpec((1, LANES), lambda c, g: (0, 0))
    return _call(
        body, [xs, Bm, Cm, dt_pre, dt_bias, a_log, d_skip, states, dy], comm=comm, name="ssd_bwd", grid=(NCHUNK, GROUPS),
        in_specs=[pl.BlockSpec((CHUNK, HPG * HP), lambda c, g: (rev(c), g)),
                  pl.BlockSpec((CHUNK, NSTATE), lambda c, g: (rev(c), g)),
                  pl.BlockSpec((CHUNK, NSTATE), lambda c, g: (rev(c), g)),
                  pl.BlockSpec((CHUNK, LANES), lambda c, g: (rev(c), 0)), par, par, par,
                  pl.BlockSpec((1, HPG, NSTATE, HP), lambda c, g: (rev(c), g, 0, 0)),
                  pl.BlockSpec((CHUNK, HPG * HP), lambda c, g: (rev(c), g))],
        out_specs=[pl.BlockSpec((CHUNK, HPG * HP), lambda c, g: (rev(c), g)),
                   pl.BlockSpec((CHUNK, NSTATE), lambda c, g: (rev(c), g)),
                   pl.BlockSpec((CHUNK, NSTATE), lambda c, g: (rev(c), g)),
                   pl.BlockSpec((CHUNK, LANES), lambda c, g: (rev(c), 0)), par, par, par],
        out_shape=[jax.ShapeDtypeStruct((S, INNER), f32), jax.ShapeDtypeStruct((S, GROUPS * NSTATE), f32),
                   jax.ShapeDtypeStruct((S, GROUPS * NSTATE), f32), jax.ShapeDtypeStruct((S, LANES), MXU),
                   jax.ShapeDtypeStruct((1, LANES), f32), jax.ShapeDtypeStruct((1, LANES), f32),
                   jax.ShapeDtypeStruct((1, LANES), f32)],
        scratch=[pltpu.VMEM((HEADS, NSTATE, HP), f32), pltpu.VMEM((CHUNK, LANES), f32),
                 pltpu.VMEM((CHUNK, LANES), f32), pltpu.VMEM((LANES, CHUNK), f32),
                 pltpu.VMEM((CHUNK, LANES), f32), pltpu.VMEM((CHUNK, LANES), f32), pltpu.VMEM((1, LANES), f32)],
        sem=("arbitrary", "arbitrary"))


ATT_STACK_FWD, ATT_STACK_BWD = 4, 2


def _attn_kv(kp, kc, vp, vc, hk):
    sl = slice(AH * hk, AH * (hk + 1))
    return (jnp.concatenate([kp[:, sl], kc[:, sl]], axis=0).astype(MXU),
            jnp.concatenate([vp[:, sl], vc[:, sl]], axis=0).astype(MXU))


def _stack_heads(x, heads):
    return jnp.concatenate([x[:, AH * h:AH * (h + 1)] for h in heads], axis=0)


def _attn_block(n, q, kb, sinks, heads):
    rows = len(heads) * WIN
    qi = lax.broadcasted_iota(jnp.int32, (rows, 2 * WIN), 0) & (WIN - 1)
    ki = lax.broadcasted_iota(jnp.int32, (rows, 2 * WIN), 1)
    rel = qi + WIN - ki
    mask = (rel >= 0) & (rel < WIN) & ((ki >= WIN) | (n > 0))
    qg = _stack_heads(q, heads).astype(MXU)
    s = lax.dot_general(qg, kb, (((1,), (1,)), ((), ())), preferred_element_type=f32) * (AH ** -0.5)
    s = jnp.where(mask, s, -1e30)
    sink = jnp.concatenate([jnp.broadcast_to(sinks[:, h:h + 1], (WIN, 1)) for h in heads], axis=0)
    m = jnp.maximum(jnp.max(s, axis=1, keepdims=True), sink)
    p = jnp.exp(s - m)
    ps = jnp.exp(sink - m)
    inv = 1.0 / (jnp.sum(p, axis=1, keepdims=True) + ps)
    return qg, p * inv, ps * inv


def _head_blocks(hk, stack):
    return [list(range(QPK * hk + i, QPK * hk + i + stack)) for i in range(0, QPK, stack)]


def _kv_specs():
    prev = lambda n: (jnp.maximum(n - 1, 0), 0)
    cur = lambda n: (n, 0)
    w = KVH * AH
    return [pl.BlockSpec((WIN, w), prev), pl.BlockSpec((WIN, w), cur), pl.BlockSpec((WIN, w), prev), pl.BlockSpec((WIN, w), cur)]


def attn_fwd(q, k, v, sinks, comm=None):
    def body(q_ref, kp_ref, kc_ref, vp_ref, vc_ref, s_ref, o_ref):
        n = pl.program_id(0)
        q_, kp, kc, vp, vc, sk = q_ref[...], kp_ref[...], kc_ref[...], vp_ref[...], vc_ref[...], s_ref[...]
        outs = []
        for hk in range(KVH):
            kb, vb = _attn_kv(kp, kc, vp, vc, hk)
            for heads in _head_blocks(hk, ATT_STACK_FWD):
                _, pr, _ = _attn_block(n, q_, kb, sk, heads)
                o = jnp.dot(pr.astype(MXU), vb, preferred_element_type=f32)
                outs += [o[WIN * i:WIN * (i + 1)] for i in range(len(heads))]
        o_ref[...] = jnp.concatenate(outs, axis=1)

    return _call(
        body, [q, k, k, v, v, sinks], comm=comm, name="attn_fwd", grid=(NBLK,),
        in_specs=[pl.BlockSpec((WIN, D), lambda n: (n, 0))] + _kv_specs() + [pl.BlockSpec((1, QH), lambda n: (0, 0))],
        out_specs=[pl.BlockSpec((WIN, D), lambda n: (n, 0))],
        out_shape=[jax.ShapeDtypeStruct((S, D), f32)], sem=("parallel",))[0]


def attn_bwd(q, k, v, sinks, dout, comm=None):
    def body(q_ref, kp_ref, kc_ref, vp_ref, vc_ref, s_ref, do_ref, dq_ref, dkp_ref, dkc_ref, dvp_ref, dvc_ref, ds_ref):
        n = pl.program_id(0)

        @pl.when(n == 0)
        def _():
            ds_ref[...] = jnp.zeros_like(ds_ref)

        q_, kp, kc, vp, vc, sk, do = q_ref[...], kp_ref[...], kc_ref[...], vp_ref[...], vc_ref[...], s_ref[...], do_ref[...]
        lane = lax.broadcasted_iota(jnp.int32, (1, QH), 1)
        nt = (((1,), (1,)), ((), ()))
        tn = (((0,), (0,)), ((), ()))
        dqs, dkps, dkcs, dvps, dvcs = [], [], [], [], []
        dsink = jnp.zeros((1, QH), f32)
        for hk in range(KVH):
            kb, vb = _attn_kv(kp, kc, vp, vc, hk)
            dkb, dvb = jnp.zeros((2 * WIN, AH), f32), jnp.zeros((2 * WIN, AH), f32)
            for heads in _head_blocks(hk, ATT_STACK_BWD):
                qg, pr, prs = _attn_block(n, q_, kb, sk, heads)
                dog = _stack_heads(do, heads).astype(MXU)
                dp = lax.dot_general(dog, vb, nt, preferred_element_type=f32)
                dvb = dvb + lax.dot_general(pr.astype(MXU), dog, tn, preferred_element_type=f32)
                delta = jnp.sum(pr * dp, axis=1, keepdims=True)
                ds = (pr * (dp - delta)).astype(MXU)
                dsk = -prs * delta
                for i, h in enumerate(heads):
                    dsink = dsink + jnp.sum(dsk[WIN * i:WIN * (i + 1)], axis=0, keepdims=True) * (lane == h).astype(f32)
                dqg = jnp.dot(ds, kb, preferred_element_type=f32) * (AH ** -0.5)
                dkb = dkb + lax.dot_general(ds, qg, tn, preferred_element_type=f32) * (AH ** -0.5)
                dqs += [dqg[WIN * i:WIN * (i + 1)] for i in range(len(heads))]
            dkps.append(dkb[:WIN])
            dkcs.append(dkb[WIN:])
            dvps.append(dvb[:WIN])
            dvcs.append(dvb[WIN:])
        dq_ref[...] = jnp.concatenate(dqs, axis=1)
        dkp_ref[...] = jnp.concatenate(dkps, axis=1)
        dkc_ref[...] = jnp.concatenate(dkcs, axis=1)
        dvp_ref[...] = jnp.concatenate(dvps, axis=1)
        dvc_ref[...] = jnp.concatenate(dvcs, axis=1)
        ds_ref[...] += dsink

    w = KVH * AH
    blk = lambda width: pl.BlockSpec((WIN, width), lambda n: (n, 0))
    return _call(
        body, [q, k, k, v, v, sinks, dout], comm=comm, name="attn_bwd", grid=(NBLK,),
        in_specs=[blk(D)] + _kv_specs() + [pl.BlockSpec((1, QH), lambda n: (0, 0)), blk(D)],
        out_specs=[blk(D), blk(w), blk(w), blk(w), blk(w), pl.BlockSpec((1, QH), lambda n: (0, 0))],
        out_shape=[jax.ShapeDtypeStruct((S, D), f32)] + [jax.ShapeDtypeStruct((S, w), f32)] * 4 + [jax.ShapeDtypeStruct((1, QH), f32)],
        sem=("arbitrary",))


def kv_bwd(kv, pos, inv_freq, k_norm, dkp, dkc, dvp, dvc):
    w = KVH * AH

    def body(kv_ref, pos_ref, if_ref, g_ref, dkp_ref, dkc_ref, dvp_ref, dvc_ref, o_ref, dg_ref, db_ref):
        n = pl.program_id(0)

        @pl.when(n == 0)
        def _():
            dg_ref[...] = jnp.zeros_like(dg_ref)
            db_ref[...] = jnp.zeros_like(db_ref)

        inside = (n < NBLK - 1).astype(f32)
        dk = dkc_ref[...] + inside * dkp_ref[...]
        dv = dvc_ref[...] + inside * dvp_ref[...]
        cos, sin = _rope_tables(pos_ref[...], if_ref[...])
        dkpre, dg = _headnorm_rope_bwd(kv_ref[...], g_ref[...], cos, sin, dk, KVH)
        dkv = jnp.concatenate([dkpre, dv], axis=1)
        o_ref[...] = dkv.astype(o_ref.dtype)
        dg_ref[...] += dg
        db_ref[...] += jnp.sum(dkv, axis=0, keepdims=True)

    nxt = lambda n: (jnp.minimum(n + 1, NBLK - 1), 0)
    cur = lambda n: (n, 0)
    const = lambda n: (0, 0)
    return pl.pallas_call(
        body, name="kv_bwd", grid=(NBLK,),
        in_specs=[pl.BlockSpec((WIN, w), cur), pl.BlockSpec((WIN, 1), cur), pl.BlockSpec((1, AH // 2), const),
                  pl.BlockSpec((1, AH), const), pl.BlockSpec((WIN, w), nxt), pl.BlockSpec((WIN, w), cur),
                  pl.BlockSpec((WIN, w), nxt), pl.BlockSpec((WIN, w), cur)],
        out_specs=[pl.BlockSpec((WIN, 2 * w), cur), pl.BlockSpec((1, AH), const), pl.BlockSpec((1, 2 * w), const)],
        out_shape=[jax.ShapeDtypeStruct((S, 2 * w), MXU), jax.ShapeDtypeStruct((1, AH), f32), jax.ShapeDtypeStruct((1, 2 * w), f32)],
        compiler_params=_cparams(("arbitrary",)),
    )(kv, pos, inv_freq, k_norm, dkp, dkc, dvp, dvc)


def _adam_math(w, g, m, v):
    m = ADAM_B1 * m + (1.0 - ADAM_B1) * g
    v = ADAM_B2 * v + (1.0 - ADAM_B2) * (g * g)
    m_hat = m / (1.0 - ADAM_B1 ** ADAM_STEP)
    v_hat = v / (1.0 - ADAM_B2 ** ADAM_STEP)
    return -ADAM_LR * (m_hat / (jnp.sqrt(v_hat) + ADAM_EPS) + ADAM_WD * w), m, v


def adamw(name, w, g, m, v):
    R, C = w.shape
    tr = _pick(R, (256, 128, 64, 32, 16, 8))
    tc = C if tr < R or C % 256 else 256

    def body(w_ref, g_ref, m_ref, v_ref, d_ref, nm_ref, nv_ref):
        d_ref[...], nm_ref[...], nv_ref[...] = _adam_math(w_ref[...], g_ref[...], m_ref[...], v_ref[...])

    spec = pl.BlockSpec((tr, tc), lambda i, j: (i, j))
    return pl.pallas_call(
        body, name=name, grid=(R // tr, C // tc), in_specs=[spec] * 4, out_specs=[spec] * 3,
        out_shape=[jax.ShapeDtypeStruct((R, C), f32)] * 3, compiler_params=_cparams(("parallel", "parallel")),
    )(w, g, m, v)


def _me():
    return lax.axis_index("x"), lax.axis_index("y"), lax.axis_index("c")


def gather_comm(xs):
    n = len(xs)

    def parts(x_refs, o_refs, sems):
        send_sems, recv_sems, local_sems = sems
        x, y, c = _me()
        me, sibling = (x, y, c), (x, y, 1 - c)
        chips = [(1 - x, y), (x, 1 - y), (1 - x, 1 - y)]

        def copy(a, k, block, to, src=None):
            dst = o_refs[a].at[4 * block[0] + 2 * block[1] + block[2]]
            return pltpu.make_async_remote_copy(
                src_ref=dst if src is None else src, dst_ref=dst,
                send_sem=send_sems.at[7 * a + k], recv_sem=recv_sems.at[7 * a + k], device_id=to, device_id_type=MESH)

        mine = [pltpu.make_async_copy(x_refs[a], o_refs[a].at[4 * x + 2 * y + c], local_sems.at[a]) for a in range(n)]
        first = []
        for a in range(n):
            first.append(copy(a, 0, me, sibling, src=x_refs[a]))
            first += [copy(a, 1 + j, me, (*chip, c), src=x_refs[a]) for j, chip in enumerate(chips)]
        return copy, mine, first, me, sibling, chips, c

    def start(x_refs, o_refs, sems):
        _, mine, first, *_ = parts(x_refs, o_refs, sems)
        for cp in mine + first:
            cp.start()

    def relay(x_refs, o_refs, sems):
        copy, _, _, me, sibling, chips, c = parts(x_refs, o_refs, sems)
        for j, chip in enumerate(chips):
            for a in range(n):
                copy(a, 1 + j, (*chip, c), me).wait_recv()
                copy(a, 4 + j, (*chip, c), sibling).start()

    def finish(x_refs, o_refs, sems):
        copy, mine, first, me, sibling, chips, c = parts(x_refs, o_refs, sems)
        for a in range(n):
            copy(a, 0, sibling, me).wait_recv()
            for j, chip in enumerate(chips):
                copy(a, 4 + j, (*chip, 1 - c), me).wait_recv()
        for cp in first + [copy(a, 4 + j, (*chip, c), sibling) for j, chip in enumerate(chips) for a in range(n)]:
            cp.wait_send()
        for cp in mine:
            cp.wait()

    return Comm(xs, [jax.ShapeDtypeStruct((N_DEV,) + a.shape, a.dtype) for a in xs],
                [pltpu.SemaphoreType.DMA((7 * n,)), pltpu.SemaphoreType.DMA((7 * n,)), pltpu.SemaphoreType.DMA((n,))],
                start, finish, relay)


def run_comm(name, comm):
    _call(lambda: None, [], name=name, grid=(1,), in_specs=[], out_specs=[], out_shape=[], comm=comm)
    return comm.results


def sibling_comm(gs):
    n = len(gs)

    def copies(g_refs, o_refs, sems):
        x, y, c = _me()
        return [pltpu.make_async_remote_copy(
            src_ref=g_refs[a].at[:, 1 - c], dst_ref=o_refs[a], send_sem=sems[0].at[a], recv_sem=sems[1].at[a],
            device_id=(x, y, 1 - c), device_id_type=MESH) for a in range(n)]

    def start(g_refs, o_refs, sems):
        for cp in copies(g_refs, o_refs, sems):
            cp.start()

    def finish(g_refs, o_refs, sems):
        for cp in copies(g_refs, o_refs, sems):
            cp.wait()

    return Comm(gs, [jax.ShapeDtypeStruct((4,) + g.shape[2:], g.dtype) for g in gs],
                [pltpu.SemaphoreType.DMA((n,)), pltpu.SemaphoreType.DMA((n,))], start, finish)


def chip_comm(ts):
    n = len(ts)

    def copies(t_refs, o_refs, sems):
        x, y, c = _me()
        chips = [(1 - x, y), (x, 1 - y), (1 - x, 1 - y)]
        return [pltpu.make_async_remote_copy(
            src_ref=t_refs[a].at[2 * px + py], dst_ref=o_refs[a].at[j],
            send_sem=sems[0].at[3 * a + j], recv_sem=sems[1].at[3 * a + j],
            device_id=(px, py, c), device_id_type=MESH) for j, (px, py) in enumerate(chips) for a in range(n)]

    def start(t_refs, o_refs, sems):
        for cp in copies(t_refs, o_refs, sems):
            cp.start()

    def finish(t_refs, o_refs, sems):
        for cp in copies(t_refs, o_refs, sems):
            cp.wait()

    return Comm(ts, [jax.ShapeDtypeStruct((3,) + t.shape[1:], t.dtype) for t in ts],
                [pltpu.SemaphoreType.DMA((3 * n,)), pltpu.SemaphoreType.DMA((3 * n,))], start, finish)


def _row_tile(rows):
    return _pick(rows, (512, 304, 256, 128))


def pair_add(name, g, r):
    _, _, R, C = g.shape
    tr = _row_tile(R)

    def body(c_ref, g_ref, r_ref, o_ref):
        o_ref[0] = (g_ref[0, 0].astype(f32) + r_ref[0].astype(f32)).astype(o_ref.dtype)

    return pl.pallas_call(
        body, name=name,
        grid_spec=pltpu.PrefetchScalarGridSpec(
            num_scalar_prefetch=1, grid=(4, R // tr),
            in_specs=[pl.BlockSpec((1, 1, tr, C), lambda p, i, c: (p, c[0], i, 0)),
                      pl.BlockSpec((1, tr, C), lambda p, i, c: (p, i, 0))],
            out_specs=pl.BlockSpec((1, tr, C), lambda p, i, c: (p, i, 0))),
        out_shape=jax.ShapeDtypeStruct((4, R, C), g.dtype),
        compiler_params=_cparams(("parallel", "parallel")),
    )(lax.axis_index("c").reshape(1).astype(jnp.int32), g, r)


def _sum_of_four(t_ref, r_ref):
    return ((t_ref[0].astype(f32) + r_ref[0].astype(f32)) + r_ref[1].astype(f32)) + r_ref[2].astype(f32)


def _my_chip():
    return (2 * lax.axis_index("x") + lax.axis_index("y")).reshape(1).astype(jnp.int32)


def final_adamw(name, t, r, w, m, v):
    _, R, C = t.shape
    tr = _pick(R, (256, 128, 64, 32, 16))
    tc = C if tr < R or C % 256 else 256

    def body(p_ref, t_ref, r_ref, w_ref, m_ref, v_ref, g_ref, d_ref, nm_ref, nv_ref):
        g_ = _sum_of_four(t_ref, r_ref)
        g_ref[...] = g_
        d_ref[...], nm_ref[...], nv_ref[...] = _adam_math(w_ref[...], g_, m_ref[...], v_ref[...])

    flat = pl.BlockSpec((tr, tc), lambda i, j, p: (i, j))
    return pl.pallas_call(
        body, name=name,
        grid_spec=pltpu.PrefetchScalarGridSpec(
            num_scalar_prefetch=1, grid=(R // tr, C // tc),
            in_specs=[pl.BlockSpec((1, tr, tc), lambda i, j, p: (p[0], i, j)),
                      pl.BlockSpec((3, tr, tc), lambda i, j, p: (0, i, j)), flat, flat, flat],
            out_specs=[flat] * 4),
        out_shape=[jax.ShapeDtypeStruct((R, C), f32)] * 4,
        compiler_params=_cparams(("parallel", "parallel")),
    )(_my_chip(), t, r, w, m, v)


def final_adamw_layers(name, t0, r0, t1, r1, w, m, v):
    _, R, C = t0.shape
    tr = _pick(R, (256, 128, 64, 32, 16))

    def body(p_ref, t0_ref, r0_ref, t1_ref, r1_ref, w_ref, m_ref, v_ref, g_ref, d_ref, nm_ref, nv_ref):
        g_ = jnp.where(pl.program_id(0) == 0, _sum_of_four(t0_ref, r0_ref), _sum_of_four(t1_ref, r1_ref))
        g_ref[0] = g_
        d_ref[0], nm_ref[0], nv_ref[0] = _adam_math(w_ref[0], g_, m_ref[0], v_ref[0])

    mine = pl.BlockSpec((1, tr, C), lambda l, i, p: (p[0], i, 0))
    theirs = pl.BlockSpec((3, tr, C), lambda l, i, p: (0, i, 0))
    layer = pl.BlockSpec((1, tr, C), lambda l, i, p: (l, i, 0))
    return pl.pallas_call(
        body, name=name,
        grid_spec=pltpu.PrefetchScalarGridSpec(
            num_scalar_prefetch=1, grid=(2, R // tr),
            in_specs=[mine, theirs, mine, theirs, layer, layer, layer], out_specs=[layer] * 4),
        out_shape=[jax.ShapeDtypeStruct((2, R, C), f32)] * 4,
        compiler_params=_cparams(("parallel", "parallel")),
    )(_my_chip(), t0, r0, t1, r1, w, m, v)


class ReduceScatter:
    def __init__(self, tag, keys, grads):
        self.tag, self.keys, self.grads = tag, keys, grads
        self.send = [g.reshape((4, 2, g.shape[0] // N_DEV) + g.shape[1:]) for g in grads]

    def sibling(self):
        self.c1 = sibling_comm(self.send)
        return self.c1

    def chips(self):
        self.pairs = [pair_add(f"rs_pair_add_{self.tag}{i}", g, r) for i, (g, r) in enumerate(zip(self.send, self.c1.results))]
        self.c2 = chip_comm(self.pairs)
        return self.c2

    def parts(self):
        return {k: (t, r) for k, t, r in zip(self.keys, self.pairs, self.c2.results)}


IN_ROWS = {"z": (0, 2048), "xs": (2048, 4096), "B": (4096, 5120), "C": (5120, 6144)}
IN_COLS = 2 * INNER + 2 * GROUPS * NSTATE + HEADS


def sum_devices(g):
    def body(g_ref, o_ref):
        acc = g_ref[0]
        for i in range(1, N_DEV):
            acc = acc + g_ref[i]
        o_ref[...] = acc

    return pl.pallas_call(body, name="sum_devices", out_shape=jax.ShapeDtypeStruct(g.shape[1:], f32),
                          compiler_params=_cparams())(g)


def _pack(parts, unit, dtype, lead=()):
    flat = jnp.concatenate([p.reshape(lead + (-1,)).astype(dtype) for p in parts], axis=-1)
    n = flat.shape[-1]
    rows = -(-n // (unit * PACK_W)) * unit
    flat = jnp.pad(flat, [(0, 0)] * len(lead) + [(0, rows * PACK_W - n)])
    return flat.reshape(lead + (rows, PACK_W))


def _unpack(buf, shapes, lead=()):
    flat = buf.reshape(lead + (-1,))
    out, off = [], 0
    for shp in shapes:
        n = math.prod(shp)
        out.append(flat[..., off:off + n].reshape(lead + tuple(shp)))
        off += n
    return out


def _pad_lanes(a):
    return jnp.pad(a, [(0, 0)] * (a.ndim - 1) + [(0, LANES - a.shape[-1])])


_NN = (((1,), (0,)), ((), ()))
_NT = (((1,), (1,)), ((), ()))


def _mm(a, b, dims):
    return lax.dot_general(a.astype(MXU), b.astype(MXU), dims, preferred_element_type=f32)


def _ffn_fwd(tag, x, norm_g, w_inT, conv_w, conv_b, mid_comm=None):
    (h,) = rowwise(f"{tag}_norm", lambda x_, g_: (_rms_fwd(x_, g_),), [x], [norm_g], [(D, MXU)])
    ct, nblk = FFN_CT, FFN // FFN_CT

    def body(h_ref, wg_ref, wv_ref, cw_ref, cb_ref, gp_ref, v_ref, a_ref):
        h_ = h_ref[...]
        gp, v_ = _mm(h_, wg_ref[...], _NT), _mm(h_, wv_ref[...], _NT)
        gp_ref[...] = gp
        v_ref[...] = v_
        a_ref[...] = (_silu(_conv(gp, cw_ref[...], cb_ref[...])) * v_).astype(a_ref.dtype)

    col = pl.BlockSpec((S, ct), lambda j: (0, j))
    gate_pre, val, act = _call(
        body, [h, w_inT, w_inT, conv_w, conv_b], comm=mid_comm, name=f"{tag}_in", grid=(nblk,),
        in_specs=[pl.BlockSpec((S, D), lambda j: (0, 0)), pl.BlockSpec((ct, D), lambda j: (j, 0)),
                  pl.BlockSpec((ct, D), lambda j: (nblk + j, 0)), pl.BlockSpec((CONV_F, ct), lambda j: (0, j)),
                  pl.BlockSpec((1, ct), lambda j: (0, j))],
        out_specs=[col, col, col],
        out_shape=[jax.ShapeDtypeStruct((S, FFN), f32), jax.ShapeDtypeStruct((S, FFN), f32), jax.ShapeDtypeStruct((S, FFN), MXU)],
        sem=("parallel",))
    return act, (x, h, gate_pre, val, act)


FFN_CT = 256
CONV_CT = 256


def _proj_conv(name, h, wT, row0, cw, cb, comm=None):
    C, ct = cw.shape[1], CONV_CT

    def body(h_ref, w_ref, cw_ref, cb_ref, p_ref, c_ref):
        p = _mm(h_ref[...], w_ref[...], _NT)
        p_ref[...] = p
        c_ref[...] = _silu(_conv(p, cw_ref[...], cb_ref[...]))

    col = pl.BlockSpec((S, ct), lambda j: (0, j))
    return _call(
        body, [h, wT, cw, cb], comm=comm, name=name, grid=(C // ct,),
        in_specs=[pl.BlockSpec((S, D), lambda j: (0, 0)), pl.BlockSpec((ct, D), lambda j: (row0 // ct + j, 0)),
                  pl.BlockSpec((CONV_A, ct), lambda j: (0, j)), pl.BlockSpec((1, ct), lambda j: (0, j))],
        out_specs=[col, col], out_shape=[jax.ShapeDtypeStruct((S, C), f32)] * 2, sem=("parallel",))


def _dconv_wgrad(name, pre, dconv, cw, cb, h):
    C, ct = cw.shape[1], CONV_CT

    def body(p_ref, do_ref, cw_ref, cb_ref, h_ref, dp_ref, g_ref, dw_ref, db_ref):
        p_, w_ = p_ref[...], cw_ref[...]
        taps = _taps(p_, CONV_A)
        dx, dw, db = _conv_bwd(p_, w_, do_ref[...] * _dsilu(_conv(p_, w_, cb_ref[...], taps)), taps)
        dpm = dx.astype(MXU)
        dp_ref[...] = dpm
        g_ref[...] = lax.dot_general(dpm, h_ref[...].astype(MXU), (((0,), (0,)), ((), ())),
                                     preferred_element_type=f32).astype(g_ref.dtype)
        dw_ref[...] = dw
        db_ref[...] = db

    col = pl.BlockSpec((S, ct), lambda j: (0, j))
    return _call(
        body, [pre, dconv, cw, cb, h], name=name, grid=(C // ct,),
        in_specs=[col, col, pl.BlockSpec((CONV_A, ct), lambda j: (0, j)), pl.BlockSpec((1, ct), lambda j: (0, j)),
                  pl.BlockSpec((S, D), lambda j: (0, 0))],
        out_specs=[col, pl.BlockSpec((ct, D), lambda j: (j, 0)), pl.BlockSpec((CONV_A, ct), lambda j: (0, j)),
                   pl.BlockSpec((1, ct), lambda j: (0, j))],
        out_shape=[jax.ShapeDtypeStruct((S, C), MXU), jax.ShapeDtypeStruct((C, D), MXU),
                   jax.ShapeDtypeStruct((CONV_A, C), f32), jax.ShapeDtypeStruct((1, C), f32)],
        sem=("parallel",))


def _ffn_mid_bwd(name, dout, w_down, gate_pre, val, conv_w, conv_b, comm=None):
    ct = FFN_CT

    def body(do_ref, wd_ref, gp_ref, v_ref, w_ref, b_ref, dgv_ref, dw_ref, db_ref, dob_s):
        @pl.when(pl.program_id(0) == 0)
        def _():
            dob_s[...] = do_ref[...].astype(MXU)

        da = _mm(dob_s[...], wd_ref[...], _NT)
        gp, v_, w_ = gp_ref[...], v_ref[...], w_ref[...]
        taps = _taps(gp, CONV_F)
        gate = _conv(gp, w_, b_ref[...], taps)
        sg = _sigmoid(gate)
        dgp, dw, db = _conv_bwd(gp, w_, da * v_ * (sg * (1.0 + gate * (1.0 - sg))), taps)
        dgv_ref[0] = dgp.astype(dgv_ref.dtype)
        dgv_ref[1] = (da * (gate * sg)).astype(dgv_ref.dtype)
        dw_ref[...] = dw
        db_ref[...] = db

    col = pl.BlockSpec((S, ct), lambda j: (0, j))
    return _call(
        body, [dout, w_down, gate_pre, val, conv_w, conv_b], comm=comm, name=name, grid=(FFN // ct,),
        in_specs=[pl.BlockSpec((S, D), lambda j: (0, 0)), pl.BlockSpec((ct, D), lambda j: (j, 0)), col, col,
                  pl.BlockSpec((CONV_F, ct), lambda j: (0, j)), pl.BlockSpec((1, ct), lambda j: (0, j))],
        out_specs=[pl.BlockSpec((2, S, ct), lambda j: (0, 0, j)), pl.BlockSpec((CONV_F, ct), lambda j: (0, j)),
                   pl.BlockSpec((1, ct), lambda j: (0, j))],
        out_shape=[jax.ShapeDtypeStruct((2, S, FFN), MXU), jax.ShapeDtypeStruct((CONV_F, FFN), f32), jax.ShapeDtypeStruct((1, FFN), f32)],
        scratch=[pltpu.VMEM((S, D), MXU)], sem=("arbitrary",))


def _ffn_bwd(tag, layer, saved, norm_g, w_inT, conv_w, conv_b, w_down, dout, mid_comm=None):
    x, h, gate_pre, val, act = saved
    g_down = matmul(f"{tag}_wdown", act, dout, "tn", out_dtype=MXU)
    dgv, g_cw, g_cb = _ffn_mid_bwd(f"{tag}_dmid", dout, w_down, gate_pre, val, conv_w, conv_b, comm=mid_comm)
    g_inT = matmul_tn_stacked(f"{tag}_win", dgv, h, MXU)

    def din_fn(dg_, dv_, x_, do_, g_, wT):
        dx, dg = _rms_bwd(x_, g_, _mm(dg_, wT[:FFN], _NN) + _mm(dv_, wT[FFN:], _NN))
        return do_ + dx, dg

    rs = ReduceScatter(tag, (f"f_inT{layer}", f"f_down{layer}"), [g_inT, g_down])
    dx, g_norm = rowwise(f"{tag}_din", din_fn, [(dgv, 0), (dgv, 1), x, dout], [norm_g, w_inT], [(D, f32)], [(1, D)],
                         comm=rs.sibling())
    return dx, {f"f_norm{layer}": g_norm, f"f_conv_w{layer}": g_cw, f"f_conv_b{layer}": g_cb}, rs


def _land(W, keys, comm):
    for k, g in zip(keys, comm.results):
        W[k] = g.reshape(-1, g.shape[2])


def _local_step(x, pos, tgt, W, shards):
    G = {}
    gather = lambda *keys: gather_comm([shards[k] for k in keys])
    inv_freq = (ROPE_THETA ** (-jnp.arange(AH // 2, dtype=f32) / (AH // 2))).reshape(1, AH // 2)

    def in_fn(x_, g_, wT, wdtT):
        h_ = _rms_fwd(x_, g_).astype(MXU)
        return h_, _mm(h_, wT[slice(*IN_ROWS["z"])], _NT), _mm(h_, wdtT, _NT)

    h0, z, dt_pre = rowwise("a_in", in_fn, [x], [W["a_norm"], W["inT"], W["in_dtT"]], [(D, MXU), (INNER, f32), (LANES, f32)])
    pre, conv = {}, {}
    early = {"xs": ("a_out",), "B": (), "C": ()}
    for k in ("xs", "B", "C"):
        c = gather(*early[k]) if early[k] else None
        pre[k], conv[k] = _proj_conv(f"a_in_{k}", h0, W["inT"], IN_ROWS[k][0], W[f"cw_{k}"], W[f"cb_{k}"], comm=c)
        if c is not None:
            _land(W, early[k], c)
    c = gather("f_inT0", "f_down0")
    y, states = ssd_fwd(conv["xs"], conv["B"], conv["C"], dt_pre, W["dt_bias"], W["A_log"], W["D"], comm=c)
    _land(W, ("f_inT0", "f_down0"), c)

    def gate_norm(y_, z_, g_):
        yg = y_ * _silu(z_)
        w = INNER // GROUPS
        return (jnp.concatenate([_rms_fwd(yg[:, w * i:w * (i + 1)], g_[:, w * i:w * (i + 1)]) for i in range(GROUPS)], axis=1),)

    def out_fn(y_, z_, x_, g_, w_):
        (gn_,) = gate_norm(y_, z_, g_)
        gn_ = gn_.astype(MXU)
        return gn_, x_ + _mm(gn_, w_, _NN)

    c = gather("f_down1")
    gn, x1 = rowwise("a_out", out_fn, [y, z, x], [W["a_gnorm"], W["a_out"]], [(INNER, MXU), (D, f32)], comm=c)
    _land(W, ("f_down1",), c)

    c = gather("w_kv", "w_q", "w_o")
    act0, ffn0 = _ffn_fwd("f0", x1, W["f_norm0"], W["f_inT0"], W["f_cw0"], W["f_cb0"], mid_comm=c)
    _land(W, ("w_kv", "w_q", "w_o"), c)
    x2 = matmul("f0_down", act0, W["f_down0"], "nn", residual=x1)

    def qkv_fn(x_, gk, gb, wkv, bkv, wq, bq):
        kvn_, h2_ = _rms_fwd(x_, gk).astype(MXU), _rms_fwd(x_, gb).astype(MXU)
        return kvn_, h2_, _mm(kvn_, wkv, _NN) + bkv, _mm(h2_, wq, _NN) + bq

    kw = KVH * AH
    kvn, h2, kv, q_pre = rowwise("qkv_proj", qkv_fn, [x2], [W["kv_norm"], W["b_norm"], W["w_kv"], W["b_kv"], W["w_q"], W["b_q"]],
                                 [(D, MXU), (D, MXU), (2 * kw, f32), (D, f32)])

    def k_fwd(kv_, pos_, if_, g_):
        cos, sin = _rope_tables(pos_, if_)
        return _headnorm_rope_fwd(kv_[:, :kw], g_, cos, sin, KVH), kv_[:, kw:]

    k_rot, v_val = rowwise("k_rope", k_fwd, [kv, pos], [inv_freq, W["k_norm"]], [(kw, f32), (kw, f32)])

    def q_fwd(q_, pos_, if_, g_):
        cos, sin = _rope_tables(pos_, if_)
        return (_headnorm_rope_fwd(q_, g_, cos, sin, QH),)

    (q,) = rowwise("q_rope", q_fwd, [q_pre, pos], [inv_freq, W["q_norm"]], [(D, f32)])
    c = gather("f_inT1")
    att = attn_fwd(q, k_rot, v_val, W["sinks"], comm=c)
    _land(W, ("f_inT1",), c)
    x3 = matmul("o_proj", att, W["w_o"], "nn", bias=W["b_o"], residual=x2)

    act1, ffn1 = _ffn_fwd("f1", x3, W["f_norm1"], W["f_inT1"], W["f_cw1"], W["f_cb1"])

    def loss_fn(a_, x_, t_, w_):
        diff = x_ + _mm(a_, w_, _NN) - t_
        rows = jnp.sum(diff * diff, axis=1, keepdims=True) * (0.5 / D)
        return diff * (1.0 / D), jnp.sum(rows, axis=0, keepdims=True)

    dx4, loss = rowwise("f1_down_loss", loss_fn, [act1, x3, tgt], [W["f_down1"]], [(D, f32)], [(1, 1)])

    dx3, g, rs_f1 = _ffn_bwd("f1", 1, ffn1, W["f_norm1"], W["f_inT1"], W["f_cw1"], W["f_cb1"], W["f_down1"], dx4)
    G.update(g)

    datt = matmul("o_dproj", dx3, W["w_o"], "nt")
    g_wo = matmul("o_wproj", att, dx3, "tn", out_dtype=MXU)
    dq, dkp, dkc, dvp, dvc, G["sinks"] = attn_bwd(q, k_rot, v_val, W["sinks"], datt, comm=rs_f1.chips())

    def q_bwd(q_, pos_, dq_, dx_, if_, g_):
        cos, sin = _rope_tables(pos_, if_)
        dqp, dg = _headnorm_rope_bwd(q_, g_, cos, sin, dq_, QH)
        return dqp, dg, jnp.sum(dqp, axis=0, keepdims=True), jnp.sum(dx_, axis=0, keepdims=True)

    dq_pre, G["q_norm"], G["b_q"], G["b_o"] = rowwise("q_drope", q_bwd, [q_pre, pos, dq, dx3], [inv_freq, W["q_norm"]],
                                                      [(D, MXU)], [(1, AH), (1, D), (1, D)])
    g_wq = matmul("q_wproj", h2, dq_pre, "tn", out_dtype=MXU)
    dkv, G["k_norm"], G["b_kv"] = kv_bwd(kv, pos, inv_freq, W["k_norm"], dkp, dkc, dvp, dvc)
    g_wkv = matmul("kv_wproj", kvn, dkv, "tn", out_dtype=MXU)
    rs_att = ReduceScatter("att", ("w_kv", "w_q", "w_o"), [g_wkv, g_wq, g_wo])

    def x2_bwd(x_, dq_, dkv_, dx_, gb_, gk_, wq, wkv):
        d1, dgb = _rms_bwd(x_, gb_, _mm(dq_, wq, _NT))
        d2, dgk = _rms_bwd(x_, gk_, _mm(dkv_, wkv, _NT))
        return dx_ + d1 + d2, dgb, dgk

    dx2, G["b_norm"], G["kv_norm"] = rowwise("qkv_dproj", x2_bwd, [x2, dq_pre, dkv, dx3],
                                             [W["b_norm"], W["kv_norm"], W["w_q"], W["w_kv"]],
                                             [(D, f32)], [(1, D), (1, D)], comm=rs_att.sibling())

    dx1, g, rs_f0 = _ffn_bwd("f0", 0, ffn0, W["f_norm0"], W["f_inT0"], W["f_cw0"], W["f_cb0"], W["f_down0"], dx2,
                             mid_comm=rs_att.chips())
    G.update(g)

    rs_out = ReduceScatter("a_out", ("a_out",), [matmul("a_wout", gn, dx1, "tn", out_dtype=MXU)])

    def gate_norm_bwd(y_, z_, dx_, g_, w_out):
        dgn_ = _mm(dx_, w_out, _NT)
        w = INNER // GROUPS
        sg = _sigmoid(z_)
        sz = z_ * sg
        yg = y_ * sz
        parts, dgs = [], []
        for i in range(GROUPS):
            dseg, dg = _rms_bwd(yg[:, w * i:w * (i + 1)], g_[:, w * i:w * (i + 1)], dgn_[:, w * i:w * (i + 1)])
            parts.append(dseg)
            dgs.append(dg)
        dyg = jnp.concatenate(parts, axis=1)
        return dyg * sz, dyg * y_ * (sg * (1.0 + z_ * (1.0 - sg))), jnp.concatenate(dgs, axis=1)

    dy, dz, G["a_gnorm"] = rowwise("a_dout", gate_norm_bwd, [y, z, dx1], [W["a_gnorm"], W["a_out"]],
                                   [(INNER, f32), (INNER, MXU)], [(1, INNER)], comm=rs_out.sibling())
    dconv = {}
    dconv["xs"], dconv["B"], dconv["C"], ddt_pre, G["dt_bias"], G["A_log"], G["D"] = ssd_bwd(
        conv["xs"], conv["B"], conv["C"], dt_pre, W["dt_bias"], W["A_log"], W["D"], states, dy,
        comm=merge_comms([rs_f0.chips(), rs_out.chips()]))

    g_in, dpre = [matmul("a_win_z", dz, h0, "tn", out_dtype=MXU)], {}
    for k in ("xs", "B", "C"):
        dpre[k], g_k, G[f"cw_{k}"], G[f"cb_{k}"] = _dconv_wgrad(f"a_dconv_{k}", pre[k], dconv[k], W[f"cw_{k}"], W[f"cb_{k}"], h0)
        g_in.append(g_k)
    g_in.append(matmul("a_win_dt", ddt_pre, h0, "tn", out_dtype=MXU)[:HEADS])
    rs_in = ReduceScatter("a_in", ("inT",), [jnp.concatenate(g_in, axis=0)])
    run_comm("rs_in_sibling", rs_in.sibling())

    def x0_bwd(dz_, dxs_, db_, dc_, ddt_, x_, do_, g_, wT, wdtT):
        parts = zip((dz_, dxs_, db_, dc_), IN_ROWS.values())
        dh = sum(_mm(d_, wT[a:b], _NN) for d_, (a, b) in parts) + _mm(ddt_, wdtT, _NN)
        dx, dg = _rms_bwd(x_, g_, dh)
        return do_ + dx, dg

    dx, G["a_norm"] = rowwise("a_din", x0_bwd, [dz, dpre["xs"], dpre["B"], dpre["C"], ddt_pre, x, dx1],
                              [W["a_norm"], W["inT"], W["in_dtT"]], [(D, f32)], [(1, D)], comm=rs_in.chips())
    return loss, dx, G, [rs_f1, rs_att, rs_f0, rs_out, rs_in]


ROW_KEYS = ("inT", "a_out", "f_inT0", "f_down0", "w_kv", "w_q", "w_o", "f_inT1", "f_down1")


def _row_blocks(src):
    return {"inT": src["a_in_proj"][0].T, "a_out": src["a_out_proj"][0], "w_kv": src["w_kv"], "w_q": src["w_q"][0],
            "w_o": src["w_o"][0], "f_inT0": src["f_w_in"][0].T, "f_inT1": src["f_w_in"][1].T,
            "f_down0": src["f_w_down"][0], "f_down1": src["f_w_down"][1]}


def _from_row_blocks(rb):
    out = {"a_in_proj": rb["inT"].T[None], "a_out_proj": rb["a_out"][None], "w_kv": rb["w_kv"], "w_q": rb["w_q"][None],
           "w_o": rb["w_o"][None]}
    if "f_inT0" in rb:
        out["f_w_in"] = jnp.stack([rb["f_inT0"].T, rb["f_inT1"].T])
        out["f_w_down"] = jnp.stack([rb["f_down0"], rb["f_down1"]])
    return out


SMALL_SHARDED = ("a_norm", "a_conv_w", "a_conv_b", "a_gnorm", "f_conv_w")
REPLICATED = ("a_dt_bias", "a_A_log", "a_D", "kv_norm", "b_kv", "k_norm", "b_norm", "b_q", "q_norm", "sinks", "b_o",
              "f_norm", "f_conv_b")
ORDER = ("a_norm", "a_in_proj", "a_conv_w", "a_conv_b", "a_dt_bias", "a_A_log", "a_D", "a_gnorm", "a_out_proj", "kv_norm",
         "w_kv", "b_kv", "k_norm", "b_norm", "w_q", "b_q", "q_norm", "sinks", "w_o", "b_o", "f_norm", "f_w_in",
         "f_conv_w", "f_conv_b", "f_w_down")


def _gathered_to_whole(name, g):
    if name == "a_conv_w":
        return jnp.moveaxis(g[:, 0], 0, 1).reshape(g.shape[2], -1)
    if name in ("a_norm", "a_conv_b", "a_gnorm"):
        return g[:, 0].reshape(1, -1)
    if name == "f_conv_w":
        return jnp.moveaxis(g, 0, 2).reshape(g.shape[1], g.shape[2], -1)
    raise ValueError(name)


def _whole_to_shards(name, w):
    if name == "a_conv_w":
        return jnp.moveaxis(w.reshape(w.shape[0], N_DEV, -1), 1, 0)[:, None]
    if name in ("a_norm", "a_conv_b", "a_gnorm"):
        return w.reshape(N_DEV, 1, -1)
    if name == "f_conv_w":
        return jnp.moveaxis(w.reshape(w.shape[0], w.shape[1], N_DEV, -1), 2, 0)
    raise ValueError(name)


def _small_weights(whole):
    W = {}
    cw, cb = whole["a_conv_w"], whole["a_conv_b"]
    o = 0
    for k, n in (("xs", INNER), ("B", GROUPS * NSTATE), ("C", GROUPS * NSTATE)):
        W[f"cw_{k}"], W[f"cb_{k}"] = cw[:, o:o + n], cb[:, o:o + n]
        o += n
    W["a_norm"], W["a_gnorm"] = whole["a_norm"], whole["a_gnorm"]
    W["dt_bias"], W["A_log"], W["D"] = (_pad_lanes(whole[k]) for k in ("a_dt_bias", "a_A_log", "a_D"))
    W["kv_norm"], W["b_kv"], W["k_norm"] = whole["kv_norm"].reshape(1, -1), whole["b_kv"].reshape(1, -1), whole["k_norm"].reshape(1, -1)
    for k in ("b_norm", "b_q", "q_norm", "sinks", "b_o"):
        W[k] = whole[k]
    for i in range(2):
        W[f"f_norm{i}"] = whole["f_norm"][i:i + 1]
        W[f"f_cw{i}"], W[f"f_cb{i}"] = whole["f_conv_w"][i], whole["f_conv_b"][i:i + 1]
    return W


def _small_grads(G, shapes):
    nh = HEADS
    out = {
        "a_conv_w": jnp.concatenate([G["cw_xs"], G["cw_B"], G["cw_C"]], axis=1),
        "a_conv_b": jnp.concatenate([G["cb_xs"], G["cb_B"], G["cb_C"]], axis=1),
        "a_norm": G["a_norm"], "a_gnorm": G["a_gnorm"],
        "a_dt_bias": G["dt_bias"][:, :nh], "a_A_log": G["A_log"][:, :nh], "a_D": G["D"][:, :nh],
        "kv_norm": G["kv_norm"], "b_kv": G["b_kv"], "k_norm": G["k_norm"], "b_norm": G["b_norm"],
        "b_q": G["b_q"], "q_norm": G["q_norm"], "sinks": G["sinks"], "b_o": G["b_o"],
        "f_norm": jnp.concatenate([G["f_norm0"], G["f_norm1"]], axis=0),
        "f_conv_w": jnp.stack([G["f_conv_w0"], G["f_conv_w1"]]),
        "f_conv_b": jnp.concatenate([G["f_conv_b0"], G["f_conv_b1"]], axis=0),
    }
    return {k: val.reshape(shapes[k]) if k in shapes else val for k, val in out.items()}


def kernel(x, positions, a_norm, a_in_proj, a_conv_w, a_conv_b, a_dt_bias, a_A_log, a_D, a_gnorm, a_out_proj, kv_norm, w_kv, b_kv, k_norm, b_norm, w_q, b_q, q_norm, sinks, w_o, b_o, f_norm, f_w_in, f_conv_w, f_conv_b, f_w_down, loss_target, m_a_norm, m_a_in_proj, m_a_conv_w, m_a_conv_b, m_a_dt_bias, m_a_A_log, m_a_D, m_a_gnorm, m_a_out_proj, m_kv_norm, m_w_kv, m_b_kv, m_k_norm, m_b_norm, m_w_q, m_b_q, m_q_norm, m_sinks, m_w_o, m_b_o, m_f_norm, m_f_w_in, m_f_conv_w, m_f_conv_b, m_f_w_down, v_a_norm, v_a_in_proj, v_a_conv_w, v_a_conv_b, v_a_dt_bias, v_a_A_log, v_a_D, v_a_gnorm, v_a_out_proj, v_kv_norm, v_w_kv, v_b_kv, v_k_norm, v_b_norm, v_w_q, v_b_q, v_q_norm, v_sinks, v_w_o, v_b_o, v_f_norm, v_f_w_in, v_f_conv_w, v_f_conv_b, v_f_w_down):
    given = dict(locals())
    w_in = {n: given[n] for n in ORDER}
    m_in = {n: given["m_" + n] for n in ORDER}
    v_in = {n: given["v_" + n] for n in ORDER}
    dev = 4 * lax.axis_index("x") + 2 * lax.axis_index("y") + lax.axis_index("c")

    w2, m2, v2 = _row_blocks(w_in), _row_blocks(m_in), _row_blocks(v_in)
    small_pack = _pack([w_in[n] for n in SMALL_SHARDED], 8, f32)
    shards = {k: w2[k].astype(MXU) for k in ROW_KEYS}
    in_all, small_all = run_comm("ag_head", gather_comm([shards["inT"], small_pack]))
    whole = {n: w_in[n] for n in REPLICATED}
    for n, g in zip(SMALL_SHARDED, _unpack(small_all, [w_in[n].shape for n in SMALL_SHARDED], lead=(N_DEV,))):
        whole[n] = _gathered_to_whole(n, g)
    W = _small_weights(whole)
    W["inT"] = in_all.reshape(-1, D)
    W["in_dtT"] = jnp.pad(W["inT"][IN_COLS - HEADS:], ((0, LANES - HEADS), (0, 0)))

    loss, dx, G, scatters = _local_step(x[0], positions.reshape(S, 1).astype(f32), loss_target[0], W, shards)
    grads = _small_grads(G, {n: whole[n].shape for n in REPLICATED})

    small_names = SMALL_SHARDED + REPLICATED
    small_part = _pack([grads[n] for n in small_names], 8, f32)
    small_gather = gather_comm([small_part])
    run_comm("ag_small_grads", small_gather)
    parts = {}
    for rs in scatters:
        parts.update(rs.parts())

    single = tuple(k for k in ROW_KEYS if not k.startswith("f_"))
    stepped = {k: final_adamw(f"adamw_{k}", *parts[k], w2[k], m2[k], v2[k]) for k in single}
    g_out, delta, new_m, new_v = (_from_row_blocks({k: stepped[k][i] for k in single}) for i in range(4))
    for n, key, lay in (("f_w_in", "f_inT", lambda a: jnp.swapaxes(a, 1, 2)), ("f_w_down", "f_down", lambda a: a)):
        res = final_adamw_layers(f"adamw_{n}", *parts[key + "0"], *parts[key + "1"], lay(w_in[n]), lay(m_in[n]), lay(v_in[n]))
        g_out[n], delta[n], new_m[n], new_v[n] = (lay(a) for a in res)
    small_sum = sum_devices(small_gather.results[0])
    for n, g in zip(small_names, _unpack(small_sum, [grads[n].shape for n in small_names])):
        if n in SMALL_SHARDED:
            g_out[n] = lax.dynamic_index_in_dim(_whole_to_shards(n, g), dev, axis=0, keepdims=False)
        else:
            g_out[n] = g.reshape(w_in[n].shape)

    packs = [_pack([src[n] for n in small_names], 8, f32) for src in (w_in, g_out, m_in, v_in)]
    outs = adamw("adamw_small", *packs)
    for dst, buf in zip((delta, new_m, new_v), outs):
        for n, a in zip(small_names, _unpack(buf, [w_in[n].shape for n in small_names])):
            dst[n] = a

    loss_all = lax.psum(loss[0, 0], AXES)
    return (loss_all, dx[None], *[g_out[n] for n in ORDER], *[delta[n] for n in ORDER],
            *[new_m[n] for n in ORDER], *[new_v[n] for n in ORDER])
```

```python
import functools
import math

import jax
import jax.numpy as jnp
from jax import lax
from jax.experimental import pallas as pl
from jax.experimental.pallas import tpu as pltpu

f32 = jnp.float32
bf16 = jnp.bfloat16
MXU = bf16

N_DEV = 8
S = 2048
D = 1024
EPS = 1e-5
INNER = 2048
HEADS = 32
HP = 64
GROUPS = 8
HPG = HEADS // GROUPS
NSTATE = 128
CONV_A = 4
CHUNK = 256
NCHUNK = S // CHUNK
AH = 64
QH = 16
KVH = 4
QPK = QH // KVH
WIN = 128
NBLK = S // WIN
ROPE_THETA = 10000.0
FFN = 2816
CONV_F = 3
LANES = 128
PACK_W = 1024
VMEM_LIMIT = 56 * 1024 * 1024

ADAM_LR, ADAM_B1, ADAM_B2, ADAM_EPS, ADAM_WD, ADAM_STEP = 0.001, 0.9, 0.999, 1e-08, 0.01, 10

MESH = pl.DeviceIdType.MESH


def _cparams(sem=None):
    return pltpu.CompilerParams(dimension_semantics=sem, vmem_limit_bytes=VMEM_LIMIT)


def _pick(n, cands):
    for c in cands:
        if n % c == 0:
            return c
    return n


class Comm:
    def __init__(self, ins, out_shapes, sems, start, finish):
        self.ins, self.out_shapes, self.sems, self.start, self.finish = list(ins), list(out_shapes), list(sems), start, finish
        self.results, self.children = None, ()

    def set_results(self, res):
        self.results, o = list(res), 0
        for ch in self.children:
            ch.set_results(res[o:o + len(ch.out_shapes)])
            o += len(ch.out_shapes)


def merge_comms(comms):
    def each(fn_name, ins, outs, sems):
        i = o = s = 0
        for c in comms:
            getattr(c, fn_name)(ins[i:i + len(c.ins)], outs[o:o + len(c.out_shapes)], sems[s:s + len(c.sems)])
            i, o, s = i + len(c.ins), o + len(c.out_shapes), s + len(c.sems)

    merged = Comm([a for c in comms for a in c.ins], [a for c in comms for a in c.out_shapes], [a for c in comms for a in c.sems],
                  functools.partial(each, "start"), functools.partial(each, "finish"))
    merged.children = tuple(comms)
    return merged


def _call(body, args, *, name, grid, in_specs, out_specs, out_shape, scratch=(), sem=None, comm=None):
    if comm is None:
        return pl.pallas_call(body, name=name, grid=grid, in_specs=list(in_specs), out_specs=list(out_specs),
                              out_shape=list(out_shape), scratch_shapes=list(scratch), compiler_params=_cparams(sem))(*args)
    n_in, n_out, n_scr, c_in, c_out = len(in_specs), len(out_shape), len(scratch), len(comm.ins), len(comm.out_shapes)
    any_spec = pl.BlockSpec(memory_space=pl.ANY)

    def outer(*refs):
        ins, c_ins = refs[:n_in], refs[n_in:n_in + c_in]
        o = n_in + c_in
        outs, c_outs = refs[o:o + n_out], refs[o + n_out:o + n_out + c_out]
        o += n_out + c_out
        scr, c_sems = refs[o:o + n_scr], refs[o + n_scr:]
        ids = [pl.program_id(i) for i in range(len(grid))]
        first = functools.reduce(jnp.logical_and, [i == 0 for i in ids])
        last = functools.reduce(jnp.logical_and, [i == g - 1 for i, g in zip(ids, grid)])

        @pl.when(first)
        def _():
            comm.start(c_ins, c_outs, c_sems)

        body(*ins, *outs, *scr)

        @pl.when(last)
        def _():
            comm.finish(c_ins, c_outs, c_sems)

    res = pl.pallas_call(
        outer, name=name, grid=grid, in_specs=list(in_specs) + [any_spec] * c_in,
        out_specs=list(out_specs) + [any_spec] * c_out, out_shape=list(out_shape) + comm.out_shapes,
        scratch_shapes=list(scratch) + comm.sems, compiler_params=_cparams(("arbitrary",) * len(grid)),
    )(*args, *comm.ins)
    comm.set_results(res[n_out:])
    return res[:n_out]


def matmul(name, a, b, mode, out_dtype=f32, bias=None, residual=None):
    if mode == "nn":
        (M, K), (K2, N) = a.shape, b.shape
    elif mode == "nt":
        (M, K), (N, K2) = a.shape, b.shape
    else:
        (K, M), (K2, N) = a.shape, b.shape
    assert K == K2, (name, a.shape, b.shape)
    if mode == "tn":
        tm, tn = M, _pick(N, (512, 256, 128) if M <= 1024 else (256, 128))
        a_spec = pl.BlockSpec((K, M), lambda j: (0, 0))
        b_spec = pl.BlockSpec((K, tn), lambda j: (0, j))
        dims = (((0,), (0,)), ((), ()))
        grid, o_map, row_map = (N // tn,), (lambda j: (0, j)), (lambda j: (0, j))
    else:
        tm, tn = (256 if N >= 2048 else 512), N
        a_spec = pl.BlockSpec((tm, K), lambda i: (i, 0))
        b_spec = pl.BlockSpec(b.shape, lambda i: (0, 0))
        dims = (((1,), (0,)), ((), ())) if mode == "nn" else (((1,), (1,)), ((), ()))
        grid, o_map, row_map = (M // tm,), (lambda i: (i, 0)), (lambda i: (0, 0))
    ins, in_specs = [a, b], [a_spec, b_spec]
    if bias is not None:
        ins.append(bias)
        in_specs.append(pl.BlockSpec((1, tn), row_map))
    if residual is not None:
        ins.append(residual)
        in_specs.append(pl.BlockSpec((tm, tn), o_map))
    has_bias, has_res = bias is not None, residual is not None

    def body(a_ref, b_ref, *rest):
        rest = list(rest)
        bias_ref = rest.pop(0) if has_bias else None
        res_ref = rest.pop(0) if has_res else None
        (o_ref,) = rest
        r = lax.dot_general(a_ref[...].astype(MXU), b_ref[...].astype(MXU), dims, preferred_element_type=f32)
        if has_bias:
            r = r + bias_ref[...]
        if has_res:
            r = r + res_ref[...]
        o_ref[...] = r.astype(out_dtype)

    return pl.pallas_call(
        body, name=name, grid=grid, in_specs=in_specs,
        out_specs=pl.BlockSpec((tm, tn), o_map),
        out_shape=jax.ShapeDtypeStruct((M, N), out_dtype),
        compiler_params=_cparams(("parallel",)),
    )(*ins)


def matmul_tn_stacked(name, a, b, out_dtype):
    R, K, M = a.shape
    N = b.shape[1]
    tn = _pick(N, (256, 128))

    def body(a_ref, b_ref, o_ref):
        o_ref[0] = lax.dot_general(a_ref[0].astype(MXU), b_ref[...].astype(MXU), (((0,), (0,)), ((), ())),
                                   preferred_element_type=f32).astype(out_dtype)

    out = pl.pallas_call(
        body, name=name, grid=(R, N // tn),
        in_specs=[pl.BlockSpec((1, K, M), lambda r, j: (r, 0, 0)), pl.BlockSpec((K, tn), lambda r, j: (0, j))],
        out_specs=pl.BlockSpec((1, M, tn), lambda r, j: (r, 0, j)),
        out_shape=jax.ShapeDtypeStruct((R, M, N), out_dtype),
        compiler_params=_cparams(("parallel", "parallel")),
    )(a, b)
    return out.reshape(R * M, N)


def rowwise(name, fn, rows, pars, outs, accs=(), tile=256, comm=None):
    n_in, n_out = len(rows) + len(pars), len(outs)
    in_specs = [pl.BlockSpec((None, tile, r[0].shape[2]), functools.partial(lambda i, lead: (lead, i, 0), lead=r[1]))
                if isinstance(r, tuple) else pl.BlockSpec((tile, r.shape[1]), lambda i: (i, 0)) for r in rows]
    rows = [r[0] if isinstance(r, tuple) else r for r in rows]
    in_specs += [pl.BlockSpec(p.shape, lambda i: (0, 0)) for p in pars]
    out_specs = [pl.BlockSpec((tile, c), lambda i: (i, 0)) for c, _ in outs]
    out_specs += [pl.BlockSpec(shp, lambda i: (0, 0)) for shp in accs]
    out_shape = [jax.ShapeDtypeStruct((S, c), dt) for c, dt in outs]
    out_shape += [jax.ShapeDtypeStruct(shp, f32) for shp in accs]

    def body(*refs):
        res = fn(*[r[...] for r in refs[:n_in]])
        o_refs = refs[n_in:n_in + n_out]
        a_refs = refs[n_in + n_out:]
        for ref, val in zip(o_refs, res[:n_out]):
            ref[...] = val.astype(ref.dtype)
        if a_refs:
            @pl.when(pl.program_id(0) == 0)
            def _():
                for ref in a_refs:
                    ref[...] = jnp.zeros_like(ref)
            for ref, val in zip(a_refs, res[n_out:]):
                ref[...] += val

    return _call(body, [*rows, *pars], name=name, grid=(S // tile,), in_specs=in_specs, out_specs=out_specs,
                 out_shape=out_shape, sem=("arbitrary",) if accs else ("parallel",), comm=comm)


def _sigmoid(x):
    return 0.5 * jnp.tanh(0.5 * x) + 0.5


def _silu(x):
    return x * _sigmoid(x)


def _dsilu(x):
    sg = _sigmoid(x)
    return sg * (1.0 + x * (1.0 - sg))


def _softplus(x):
    return jnp.maximum(x, 0.0) + jnp.log(1.0 + jnp.exp(-jnp.abs(x)))


def _rms_fwd(x, g):
    r = lax.rsqrt(jnp.mean(x * x, axis=-1, keepdims=True) + EPS)
    return x * r * g


def _rms_bwd(x, g, dh):
    r = lax.rsqrt(jnp.mean(x * x, axis=-1, keepdims=True) + EPS)
    xh = x * r
    dxh = dh * g
    dx = r * (dxh - xh * jnp.mean(dxh * xh, axis=-1, keepdims=True))
    return dx, jnp.sum(dh * xh, axis=0, keepdims=True)


def _taps(x, width):
    row = lax.broadcasted_iota(jnp.int32, (8, x.shape[1]), 0)

    def shifted(s):
        r = pltpu.roll(x, s, 0)
        return jnp.concatenate([jnp.where(row >= s, r[:8], 0.0), r[8:]], axis=0)

    return [shifted(s) for s in range(width - 1, 0, -1)] + [x]


def _conv(x, w, b, taps=None):
    width = w.shape[0]
    taps = _taps(x, width) if taps is None else taps
    out = b + w[0:1, :] * taps[0]
    for k in range(1, width):
        out = out + w[k:k + 1, :] * taps[k]
    return out


def _conv_bwd(x, w, dc, taps=None):
    width, n = w.shape[0], x.shape[0]
    taps = _taps(x, width) if taps is None else taps
    row = lax.broadcasted_iota(jnp.int32, (8, x.shape[1]), 0)
    dx = w[width - 1:width, :] * dc
    for k in range(width - 1):
        s = width - 1 - k
        r = pltpu.roll(dc, n - s, 0)
        dx = dx + w[k:k + 1, :] * jnp.concatenate([r[:n - 8], jnp.where(row < 8 - s, r[n - 8:], 0.0)], axis=0)
    dw = jnp.concatenate([jnp.sum(dc * t, axis=0, keepdims=True) for t in taps], axis=0)
    return dx, dw, jnp.sum(dc, axis=0, keepdims=True)


def _rope_tables(pos, inv_freq):
    ang = pos * inv_freq
    return jnp.cos(ang), jnp.sin(ang)


def _split2(v):
    hi = v.astype(bf16)
    return hi, (v - hi.astype(f32)).astype(bf16)


def _head_maps(width):
    shift = AH.bit_length() - 1
    to_head = (lax.broadcasted_iota(jnp.int32, (width, LANES), 0) >> shift) == lax.broadcasted_iota(jnp.int32, (width, LANES), 1)
    from_head = lax.broadcasted_iota(jnp.int32, (LANES, width), 0) == (lax.broadcasted_iota(jnp.int32, (LANES, width), 1) >> shift)
    return to_head.astype(bf16), from_head.astype(bf16)


def _head_sums(v, to_head):
    hi, lo = _split2(v)
    return jnp.dot(hi, to_head, preferred_element_type=f32) + jnp.dot(lo, to_head, preferred_element_type=f32)


def _head_spread(s, from_head):
    hi, lo = _split2(s)
    return jnp.dot(hi, from_head, preferred_element_type=f32) + jnp.dot(lo, from_head, preferred_element_type=f32)


def _rope_full(cos, sin, width):
    half = AH // 2
    pad = jnp.zeros((cos.shape[0], LANES - half), f32)
    r = lax.broadcasted_iota(jnp.int32, (LANES, width), 0)
    lane = lax.broadcasted_iota(jnp.int32, (LANES, width), 1)
    spread = ((lane & (half - 1)) == r).astype(bf16)
    full = lambda t: _head_spread(jnp.concatenate([t, pad], axis=1), spread)
    first = (lax.broadcasted_iota(jnp.int32, (1, width), 1) & (AH - 1)) < half
    sin_f = full(sin)
    return full(cos), jnp.where(first, -sin_f, sin_f), first


def _swap_halves(v, first):
    half, width = AH // 2, v.shape[1]
    return jnp.where(first, pltpu.roll(v, width - half, 1), pltpu.roll(v, half, 1))


def _headnorm_rope_fwd(x, g, cos, sin, heads):
    to_head, from_head = _head_maps(heads * AH)
    cos_f, sin_s, first = _rope_full(cos, sin, heads * AH)
    r = _head_spread(lax.rsqrt(_head_sums(x * x, to_head) * (1.0 / AH) + EPS), from_head)
    n = x * r * jnp.tile(g, (1, heads))
    return n * cos_f + _swap_halves(n, first) * sin_s


def _headnorm_rope_bwd(x, g, cos, sin, dout, heads):
    width = heads * AH
    to_head, from_head = _head_maps(width)
    cos_f, sin_s, first = _rope_full(cos, sin, width)
    r = _head_spread(lax.rsqrt(_head_sums(x * x, to_head) * (1.0 / AH) + EPS), from_head)
    xh = x * r
    dn = dout * cos_f - _swap_halves(dout, first) * sin_s
    dxh = dn * jnp.tile(g, (1, heads))
    m = _head_spread(_head_sums(dxh * xh, to_head) * (1.0 / AH), from_head)
    dx = r * (dxh - xh * m)
    dg_lanes = jnp.sum(dn * xh, axis=0, keepdims=True)
    fold = ((lax.broadcasted_iota(jnp.int32, (width, LANES), 0) & (AH - 1))
            == lax.broadcasted_iota(jnp.int32, (width, LANES), 1)).astype(f32)
    dg = jnp.dot(jnp.broadcast_to(dg_lanes, (8, width)), fold, precision=lax.Precision.HIGHEST, preferred_element_type=f32)
    return dx, dg[0:1, :AH]


def _ssd_prep(dt_pre, dt_bias, a_log, dt_s, acum_s, acumT_s):
    dt = _softplus(dt_pre + dt_bias)
    a = dt * (-jnp.exp(a_log))
    row = lax.broadcasted_iota(jnp.int32, (CHUNK, CHUNK), 0)
    col = lax.broadcasted_iota(jnp.int32, (CHUNK, CHUNK), 1)
    dt_s[...] = dt
    acum_s[...] = jnp.dot((col <= row).astype(f32), a, precision=lax.Precision.HIGHEST, preferred_element_type=f32)
    acumT_s[...] = lax.dot_general(a, (row <= col).astype(f32), (((0,), (0,)), ((), ())),
                                   precision=lax.Precision.HIGHEST, preferred_element_type=f32)


def _head_cols(h, dt_s, acum_s, acumT_s):
    lane = lax.broadcasted_iota(jnp.int32, (1, LANES), 1)
    oh_l = (lane == h).astype(f32)
    sub = lax.broadcasted_iota(jnp.int32, (LANES, 1), 0)
    oh_s = (sub == h).astype(f32)
    dt_h = jnp.sum(dt_s[...] * oh_l, axis=1, keepdims=True)
    ac_h = jnp.sum(acum_s[...] * oh_l, axis=1, keepdims=True)
    acr_h = jnp.sum(acumT_s[...] * oh_s, axis=0, keepdims=True)
    return oh_l, dt_h, ac_h, acr_h


def ssd_fwd(xs, Bm, Cm, dt_pre, dt_bias, a_log, d_skip, comm=None):
    def body(xs_ref, b_ref, c_ref, dtp_ref, bias_ref, alog_ref, d_ref, y_ref, st_ref, state, dt_s, acum_s, acumT_s):
        c, g = pl.program_id(0), pl.program_id(1)

        @pl.when(g == 0)
        def _():
            _ssd_prep(dtp_ref[...], bias_ref[...], alog_ref[...], dt_s, acum_s, acumT_s)

        row = lax.broadcasted_iota(jnp.int32, (CHUNK, CHUNK), 0)
        col = lax.broadcasted_iota(jnp.int32, (CHUNK, CHUNK), 1)
        causal = col <= row
        Bb, Cb = b_ref[...], c_ref[...]
        cb = lax.dot_general(Cb.astype(MXU), Bb.astype(MXU), (((1,), (1,)), ((), ())), preferred_element_type=f32)
        xs_blk = xs_ref[...]

        @pl.when(c == 0)
        def _():
            for j in range(HPG):
                state[g * HPG + j] = jnp.zeros((NSTATE, HP), f32)

        prevs = [state[g * HPG + j] for j in range(HPG)]
        y_off_all = jnp.dot(Cb.astype(MXU), jnp.concatenate(prevs, axis=1).astype(MXU), preferred_element_type=f32)
        ys, xds, e_ends = [], [], []
        for j in range(HPG):
            oh_l, dt_h, ac_h, acr_h = _head_cols(g * HPG + j, dt_s, acum_s, acumT_s)
            decay = jnp.exp(jnp.where(causal, ac_h - acr_h, -1e30))
            w = (cb * decay).astype(MXU)
            xs_h = xs_blk[:, HP * j:HP * (j + 1)]
            xd = xs_h * dt_h
            y_diag = jnp.dot(w, xd.astype(MXU), preferred_element_type=f32)
            y_off = y_off_all[:, HP * j:HP * (j + 1)] * jnp.exp(ac_h)
            d_h = jnp.sum(d_ref[...] * oh_l, axis=1, keepdims=True)
            ys.append(y_diag + y_off + xs_h * d_h)
            a_end = ac_h[CHUNK - 1:CHUNK, :]
            xds.append(xd * jnp.exp(a_end - ac_h))
            e_ends.append(jnp.exp(a_end))
        s_c = lax.dot_general(Bb.astype(MXU), jnp.concatenate(xds, axis=1).astype(MXU), (((0,), (0,)), ((), ())),
                              preferred_element_type=f32)
        for j in range(HPG):
            st_ref[0, j] = prevs[j]
            state[g * HPG + j] = prevs[j] * e_ends[j] + s_c[:, HP * j:HP * (j + 1)]
        y_ref[...] = jnp.concatenate(ys, axis=1)

    par = pl.BlockSpec((1, LANES), lambda c, g: (0, 0))
    return _call(
        body, [xs, Bm, Cm, dt_pre, dt_bias, a_log, d_skip], comm=comm, name="ssd_fwd", grid=(NCHUNK, GROUPS),
        in_specs=[pl.BlockSpec((CHUNK, HPG * HP), lambda c, g: (c, g)),
                  pl.BlockSpec((CHUNK, NSTATE), lambda c, g: (c, g)),
                  pl.BlockSpec((CHUNK, NSTATE), lambda c, g: (c, g)),
                  pl.BlockSpec((CHUNK, LANES), lambda c, g: (c, 0)), par, par, par],
        out_specs=[pl.BlockSpec((CHUNK, HPG * HP), lambda c, g: (c, g)),
                   pl.BlockSpec((1, HPG, NSTATE, HP), lambda c, g: (c, g, 0, 0))],
        out_shape=[jax.ShapeDtypeStruct((S, INNER), f32), jax.ShapeDtypeStruct((NCHUNK, HEADS, NSTATE, HP), f32)],
        scratch=[pltpu.VMEM((HEADS, NSTATE, HP), f32), pltpu.VMEM((CHUNK, LANES), f32),
                 pltpu.VMEM((CHUNK, LANES), f32), pltpu.VMEM((LANES, CHUNK), f32)],
        sem=("arbitrary", "arbitrary"))


def ssd_bwd(xs, Bm, Cm, dt_pre, dt_bias, a_log, d_skip, states, dy, comm=None):
    rev = lambda c: NCHUNK - 1 - c

    def body(xs_ref, b_ref, c_ref, dtp_ref, bias_ref, alog_ref, d_ref, st_ref, dy_ref,
             dxs_ref, db_ref, dc_ref, ddt_ref, dbias_ref, dalog_ref, dd_ref,
             dstate, dt_s, acum_s, acumT_s, dacum_s, ddt_s, da_s):
        c, g = pl.program_id(0), pl.program_id(1)

        @pl.when(g == 0)
        def _():
            _ssd_prep(dtp_ref[...], bias_ref[...], alog_ref[...], dt_s, acum_s, acumT_s)
            dacum_s[...] = jnp.zeros_like(dacum_s)
            ddt_s[...] = jnp.zeros_like(ddt_s)

        @pl.when((c == 0) & (g == 0))
        def _():
            da_s[...] = jnp.zeros_like(da_s)
            dd_ref[...] = jnp.zeros_like(dd_ref)
            dbias_ref[...] = jnp.zeros_like(dbias_ref)
            dalog_ref[...] = jnp.zeros_like(dalog_ref)

        row = lax.broadcasted_iota(jnp.int32, (CHUNK, CHUNK), 0)
        col = lax.broadcasted_iota(jnp.int32, (CHUNK, CHUNK), 1)
        sub_l = lax.broadcasted_iota(jnp.int32, (CHUNK, 1), 0)
        last = (sub_l == CHUNK - 1).astype(f32)
        nt = (((1,), (1,)), ((), ()))
        tn = (((0,), (0,)), ((), ()))
        Bb, Cb = b_ref[...], c_ref[...]
        Bm_, Cm_ = Bb.astype(MXU), Cb.astype(MXU)
        cb = lax.dot_general(Cm_, Bm_, nt, preferred_element_type=f32)
        bc = lax.dot_general(Bm_, Cm_, nt, preferred_element_type=f32)
        xs_blk, dy_blk = xs_ref[...], dy_ref[...]
        dxs, dB, dC = [], jnp.zeros((CHUNK, NSTATE), f32), jnp.zeros((CHUNK, NSTATE), f32)
        for j in range(HPG):
            h = g * HPG + j
            oh_l, dt_h, ac_h, acr_h = _head_cols(h, dt_s, acum_s, acumT_s)

            @pl.when(c == 0)
            def _():
                dstate[h] = jnp.zeros((NSTATE, HP), f32)

            dnext = dstate[h]
            prev = st_ref[0, j]
            lm = jnp.exp(jnp.where(col <= row, ac_h - acr_h, -1e30))
            lmT = jnp.exp(jnp.where(row <= col, acr_h - ac_h, -1e30))
            xs_h = xs_blk[:, HP * j:HP * (j + 1)]
            dy_h = dy_blk[:, HP * j:HP * (j + 1)]
            xd = xs_h * dt_h
            xdm, dym = xd.astype(MXU), dy_h.astype(MXU)
            ea = jnp.exp(ac_h)
            a_end = ac_h[CHUNK - 1:CHUNK, :]
            e_end = jnp.exp(a_end)
            dte = jnp.exp(a_end - ac_h)
            dnm, pvm = dnext.astype(MXU), prev.astype(MXU)
            bd = jnp.dot(Bm_, dnm, preferred_element_type=f32)
            dxd = jnp.dot((bc * lmT).astype(MXU), dym, preferred_element_type=f32) + dte * bd
            dw = lax.dot_general(dym, xdm, nt, preferred_element_type=f32)
            dwT = lax.dot_general(xdm, dym, nt, preferred_element_type=f32)
            dcb = dw * lm
            dbc = dwT * lmT
            eady = (ea * dy_h).astype(MXU)
            dC = dC + jnp.dot(dcb.astype(MXU), Bm_, preferred_element_type=f32) \
                + lax.dot_general(eady, pvm, nt, preferred_element_type=f32)
            dB = dB + jnp.dot(dbc.astype(MXU), Cm_, preferred_element_type=f32) \
                + dte * lax.dot_general(xdm, dnm, nt, preferred_element_type=f32)
            dstate[h] = lax.dot_general(Cm_, eady, tn, preferred_element_type=f32) + e_end * dnext
            r1 = jnp.sum(dcb * cb, axis=1, keepdims=True)
            r2 = jnp.sum(dbc * bc, axis=1, keepdims=True)
            y_off = jnp.dot(Cm_, pvm, preferred_element_type=f32) * ea
            t3 = jnp.sum(dy_h * y_off, axis=1, keepdims=True)
            t4 = jnp.sum(bd * xd, axis=1, keepdims=True) * dte
            end_extra = jnp.sum(t4, axis=0, keepdims=True) + e_end * jnp.sum(jnp.sum(prev * dnext, axis=1, keepdims=True), axis=0, keepdims=True)
            dacum_h = r1 - r2 + t3 - t4 + last * end_extra
            dacum_s[...] += dacum_h * oh_l
            ddt_s[...] += jnp.sum(dxd * xs_h, axis=1, keepdims=True) * oh_l
            d_h = jnp.sum(d_ref[...] * oh_l, axis=1, keepdims=True)
            dxs.append(dxd * dt_h + dy_h * d_h)
            dd_ref[...] += oh_l * jnp.sum(jnp.sum(dy_h * xs_h, axis=1, keepdims=True), axis=0, keepdims=True)
        dxs_ref[...] = jnp.concatenate(dxs, axis=1)
        db_ref[...] = dB
        dc_ref[...] = dC

        @pl.when(g == GROUPS - 1)
        def _():
            a_row = -jnp.exp(alog_ref[...])
            da = jnp.dot((row <= col).astype(f32), dacum_s[...], precision=lax.Precision.HIGHEST, preferred_element_type=f32)
            da_s[...] += jnp.sum(da * dt_s[...], axis=0, keepdims=True)
            z = dtp_ref[...] + bias_ref[...]
            ddt_pre = (ddt_s[...] + da * a_row) * _sigmoid(z)
            ddt_ref[...] = ddt_pre.astype(ddt_ref.dtype)
            dbias_ref[...] += jnp.sum(ddt_pre, axis=0, keepdims=True)

            @pl.when(c == NCHUNK - 1)
            def _():
                dalog_ref[...] = da_s[...] * a_row

    par = pl.BlockSpec((1, LANES), lambda c, g: (0, 0))
    return _call(
        body, [xs, Bm, Cm, dt_pre, dt_bias, a_log, d_skip, states, dy], comm=comm, name="ssd_bwd", grid=(NCHUNK, GROUPS),
        in_specs=[pl.BlockSpec((CHUNK, HPG * HP), lambda c, g: (rev(c), g)),
                  pl.BlockSpec((CHUNK, NSTATE), lambda c, g: (rev(c), g)),
                  pl.BlockSpec((CHUNK, NSTATE), lambda c, g: (rev(c), g)),
                  pl.BlockSpec((CHUNK, LANES), lambda c, g: (rev(c), 0)), par, par, par,
                  pl.BlockSpec((1, HPG, NSTATE, HP), lambda c, g: (rev(c), g, 0, 0)),
                  pl.BlockSpec((CHUNK, HPG * HP), lambda c, g: (rev(c), g))],
        out_specs=[pl.BlockSpec((CHUNK, HPG * HP), lambda c, g: (rev(c), g)),
                   pl.BlockSpec((CHUNK, NSTATE), lambda c, g: (rev(c), g)),
                   pl.BlockSpec((CHUNK, NSTATE), lambda c, g: (rev(c), g)),
                   pl.BlockSpec((CHUNK, LANES), lambda c, g: (rev(c), 0)), par, par, par],
        out_shape=[jax.ShapeDtypeStruct((S, INNER), f32), jax.ShapeDtypeStruct((S, GROUPS * NSTATE), f32),
                   jax.ShapeDtypeStruct((S, GROUPS * NSTATE), f32), jax.ShapeDtypeStruct((S, LANES), MXU),
                   jax.ShapeDtypeStruct((1, LANES), f32), jax.ShapeDtypeStruct((1, LANES), f32),
                   jax.ShapeDtypeStruct((1, LANES), f32)],
        scratch=[pltpu.VMEM((HEADS, NSTATE, HP), f32), pltpu.VMEM((CHUNK, LANES), f32),
                 pltpu.VMEM((CHUNK, LANES), f32), pltpu.VMEM((LANES, CHUNK), f32),
                 pltpu.VMEM((CHUNK, LANES), f32), pltpu.VMEM((CHUNK, LANES), f32), pltpu.VMEM((1, LANES), f32)],
        sem=("arbitrary", "arbitrary"))


ATT_STACK_FWD, ATT_STACK_BWD = 4, 2


def _attn_kv(kp, kc, vp, vc, hk):
    sl = slice(AH * hk, AH * (hk + 1))
    return (jnp.concatenate([kp[:, sl], kc[:, sl]], axis=0).astype(MXU),
            jnp.concatenate([vp[:, sl], vc[:, sl]], axis=0).astype(MXU))


def _stack_heads(x, heads):
    return jnp.concatenate([x[:, AH * h:AH * (h + 1)] for h in heads], axis=0)


def _attn_block(n, q, kb, sinks, heads):
    rows = len(heads) * WIN
    qi = lax.broadcasted_iota(jnp.int32, (rows, 2 * WIN), 0) & (WIN - 1)
    ki = lax.broadcasted_iota(jnp.int32, (rows, 2 * WIN), 1)
    rel = qi + WIN - ki
    mask = (rel >= 0) & (rel < WIN) & ((ki >= WIN) | (n > 0))
    qg = _stack_heads(q, heads).astype(MXU)
    s = lax.dot_general(qg, kb, (((1,), (1,)), ((), ())), preferred_element_type=f32) * (AH ** -0.5)
    s = jnp.where(mask, s, -1e30)
    sink = jnp.concatenate([jnp.broadcast_to(sinks[:, h:h + 1], (WIN, 1)) for h in heads], axis=0)
    m = jnp.maximum(jnp.max(s, axis=1, keepdims=True), sink)
    p = jnp.exp(s - m)
    ps = jnp.exp(sink - m)
    inv = 1.0 / (jnp.sum(p, axis=1, keepdims=True) + ps)
    return qg, p * inv, ps * inv


def _head_blocks(hk, stack):
    return [list(range(QPK * hk + i, QPK * hk + i + stack)) for i in range(0, QPK, stack)]


def _kv_specs():
    prev = lambda n: (jnp.maximum(n - 1, 0), 0)
    cur = lambda n: (n, 0)
    w = KVH * AH
    return [pl.BlockSpec((WIN, w), prev), pl.BlockSpec((WIN, w), cur), pl.BlockSpec((WIN, w), prev), pl.BlockSpec((WIN, w), cur)]


def attn_fwd(q, k, v, sinks, comm=None):
    def body(q_ref, kp_ref, kc_ref, vp_ref, vc_ref, s_ref, o_ref):
        n = pl.program_id(0)
        q_, kp, kc, vp, vc, sk = q_ref[...], kp_ref[...], kc_ref[...], vp_ref[...], vc_ref[...], s_ref[...]
        outs = []
        for hk in range(KVH):
            kb, vb = _attn_kv(kp, kc, vp, vc, hk)
            for heads in _head_blocks(hk, ATT_STACK_FWD):
                _, pr, _ = _attn_block(n, q_, kb, sk, heads)
                o = jnp.dot(pr.astype(MXU), vb, preferred_element_type=f32)
                outs += [o[WIN * i:WIN * (i + 1)] for i in range(len(heads))]
        o_ref[...] = jnp.concatenate(outs, axis=1)

    return _call(
        body, [q, k, k, v, v, sinks], comm=comm, name="attn_fwd", grid=(NBLK,),
        in_specs=[pl.BlockSpec((WIN, D), lambda n: (n, 0))] + _kv_specs() + [pl.BlockSpec((1, QH), lambda n: (0, 0))],
        out_specs=[pl.BlockSpec((WIN, D), lambda n: (n, 0))],
        out_shape=[jax.ShapeDtypeStruct((S, D), f32)], sem=("parallel",))[0]


def attn_bwd(q, k, v, sinks, dout, comm=None):
    def body(q_ref, kp_ref, kc_ref, vp_ref, vc_ref, s_ref, do_ref, dq_ref, dkp_ref, dkc_ref, dvp_ref, dvc_ref, ds_ref):
        n = pl.program_id(0)

        @pl.when(n == 0)
        def _():
            ds_ref[...] = jnp.zeros_like(ds_ref)

        q_, kp, kc, vp, vc, sk, do = q_ref[...], kp_ref[...], kc_ref[...], vp_ref[...], vc_ref[...], s_ref[...], do_ref[...]
        lane = lax.broadcasted_iota(jnp.int32, (1, QH), 1)
        nt = (((1,), (1,)), ((), ()))
        tn = (((0,), (0,)), ((), ()))
        dqs, dkps, dkcs, dvps, dvcs = [], [], [], [], []
        dsink = jnp.zeros((1, QH), f32)
        for hk in range(KVH):
            kb, vb = _attn_kv(kp, kc, vp, vc, hk)
            dkb, dvb = jnp.zeros((2 * WIN, AH), f32), jnp.zeros((2 * WIN, AH), f32)
            for heads in _head_blocks(hk, ATT_STACK_BWD):
                qg, pr, prs = _attn_block(n, q_, kb, sk, heads)
                dog = _stack_heads(do, heads).astype(MXU)
                dp = lax.dot_general(dog, vb, nt, preferred_element_type=f32)
                dvb = dvb + lax.dot_general(pr.astype(MXU), dog, tn, preferred_element_type=f32)
                delta = jnp.sum(pr * dp, axis=1, keepdims=True)
                ds = (pr * (dp - delta)).astype(MXU)
                dsk = -prs * delta
                for i, h in enumerate(heads):
                    dsink = dsink + jnp.sum(dsk[WIN * i:WIN * (i + 1)], axis=0, keepdims=True) * (lane == h).astype(f32)
                dqg = jnp.dot(ds, kb, preferred_element_type=f32) * (AH ** -0.5)
                dkb = dkb + lax.dot_general(ds, qg, tn, preferred_element_type=f32) * (AH ** -0.5)
                dqs += [dqg[WIN * i:WIN * (i + 1)] for i in range(len(heads))]
            dkps.append(dkb[:WIN])
            dkcs.append(dkb[WIN:])
            dvps.append(dvb[:WIN])
            dvcs.append(dvb[WIN:])
        dq_ref[...] = jnp.concatenate(dqs, axis=1)
        dkp_ref[...] = jnp.concatenate(dkps, axis=1)
        dkc_ref[...] = jnp.concatenate(dkcs, axis=1)
        dvp_ref[...] = jnp.concatenate(dvps, axis=1)
        dvc_ref[...] = jnp.concatenate(dvcs, axis=1)
        ds_ref[...] += dsink

    w = KVH * AH
    blk = lambda width: pl.BlockSpec((WIN, width), lambda n: (n, 0))
    return _call(
        body, [q, k, k, v, v, sinks, dout], comm=comm, name="attn_bwd", grid=(NBLK,),
        in_specs=[blk(D)] + _kv_specs() + [pl.BlockSpec((1, QH), lambda n: (0, 0)), blk(D)],
        out_specs=[blk(D), blk(w), blk(w), blk(w), blk(w), pl.BlockSpec((1, QH), lambda n: (0, 0))],
        out_shape=[jax.ShapeDtypeStruct((S, D), f32)] + [jax.ShapeDtypeStruct((S, w), f32)] * 4 + [jax.ShapeDtypeStruct((1, QH), f32)],
        sem=("arbitrary",))


def kv_bwd(kv, pos, inv_freq, k_norm, dkp, dkc, dvp, dvc):
    w = KVH * AH

    def body(kv_ref, pos_ref, if_ref, g_ref, dkp_ref, dkc_ref, dvp_ref, dvc_ref, o_ref, dg_ref, db_ref):
        n = pl.program_id(0)

        @pl.when(n == 0)
        def _():
            dg_ref[...] = jnp.zeros_like(dg_ref)
            db_ref[...] = jnp.zeros_like(db_ref)

        inside = (n < NBLK - 1).astype(f32)
        dk = dkc_ref[...] + inside * dkp_ref[...]
        dv = dvc_ref[...] + inside * dvp_ref[...]
        cos, sin = _rope_tables(pos_ref[...], if_ref[...])
        dkpre, dg = _headnorm_rope_bwd(kv_ref[...], g_ref[...], cos, sin, dk, KVH)
        dkv = jnp.concatenate([dkpre, dv], axis=1)
        o_ref[...] = dkv.astype(o_ref.dtype)
        dg_ref[...] += dg
        db_ref[...] += jnp.sum(dkv, axis=0, keepdims=True)

    nxt = lambda n: (jnp.minimum(n + 1, NBLK - 1), 0)
    cur = lambda n: (n, 0)
    const = lambda n: (0, 0)
    return pl.pallas_call(
        body, name="kv_bwd", grid=(NBLK,),
        in_specs=[pl.BlockSpec((WIN, w), cur), pl.BlockSpec((WIN, 1), cur), pl.BlockSpec((1, AH // 2), const),
                  pl.BlockSpec((1, AH), const), pl.BlockSpec((WIN, w), nxt), pl.BlockSpec((WIN, w), cur),
                  pl.BlockSpec((WIN, w), nxt), pl.BlockSpec((WIN, w), cur)],
        out_specs=[pl.BlockSpec((WIN, 2 * w), cur), pl.BlockSpec((1, AH), const), pl.BlockSpec((1, 2 * w), const)],
        out_shape=[jax.ShapeDtypeStruct((S, 2 * w), MXU), jax.ShapeDtypeStruct((1, AH), f32), jax.ShapeDtypeStruct((1, 2 * w), f32)],
        compiler_params=_cparams(("arbitrary",)),
    )(kv, pos, inv_freq, k_norm, dkp, dkc, dvp, dvc)


def _adam_math(w, g, m, v):
    m = ADAM_B1 * m + (1.0 - ADAM_B1) * g
    v = ADAM_B2 * v + (1.0 - ADAM_B2) * (g * g)
    m_hat = m / (1.0 - ADAM_B1 ** ADAM_STEP)
    v_hat = v / (1.0 - ADAM_B2 ** ADAM_STEP)
    return -ADAM_LR * (m_hat / (jnp.sqrt(v_hat) + ADAM_EPS) + ADAM_WD * w), m, v


def adamw(name, w, g, m, v):
    R, C = w.shape
    tr = _pick(R, (256, 128, 64, 32, 16, 8))
    tc = C if tr < R or C % 256 else 256

    def body(w_ref, g_ref, m_ref, v_ref, d_ref, nm_ref, nv_ref):
        d_ref[...], nm_ref[...], nv_ref[...] = _adam_math(w_ref[...], g_ref[...], m_ref[...], v_ref[...])

    spec = pl.BlockSpec((tr, tc), lambda i, j: (i, j))
    return pl.pallas_call(
        body, name=name, grid=(R // tr, C // tc), in_specs=[spec] * 4, out_specs=[spec] * 3,
        out_shape=[jax.ShapeDtypeStruct((R, C), f32)] * 3, compiler_params=_cparams(("parallel", "parallel")),
    )(w, g, m, v)


def _me():
    return lax.axis_index("x"), lax.axis_index("y"), lax.axis_index("c")


def gather_comm(xs):
    n = len(xs)

    def parts(x_refs, o_refs, sems):
        send_sems, recv_sems, local_sems = sems
        x, y, c = _me()
        me, sibling = (x, y, c), (x, y, 1 - c)
        chips = [(1 - x, y), (x, 1 - y), (1 - x, 1 - y)]

        def copy(a, k, block, to, src=None):
            dst = o_refs[a].at[4 * block[0] + 2 * block[1] + block[2]]
            return pltpu.make_async_remote_copy(
                src_ref=dst if src is None else src, dst_ref=dst,
                send_sem=send_sems.at[7 * a + k], recv_sem=recv_sems.at[7 * a + k], device_id=to, device_id_type=MESH)

        mine = [pltpu.make_async_copy(x_refs[a], o_refs[a].at[4 * x + 2 * y + c], local_sems.at[a]) for a in range(n)]
        first = []
        for a in range(n):
            first.append(copy(a, 0, me, sibling, src=x_refs[a]))
            first += [copy(a, 1 + j, me, (*chip, c), src=x_refs[a]) for j, chip in enumerate(chips)]
        return copy, mine, first, me, sibling, chips, c

    def start(x_refs, o_refs, sems):
        _, mine, first, *_ = parts(x_refs, o_refs, sems)
        for cp in mine + first:
            cp.start()

    def finish(x_refs, o_refs, sems):
        copy, mine, first, me, sibling, chips, c = parts(x_refs, o_refs, sems)
        passed = []
        for j, chip in enumerate(chips):
            for a in range(n):
                copy(a, 1 + j, (*chip, c), me).wait_recv()
                cp = copy(a, 4 + j, (*chip, c), sibling)
                cp.start()
                passed.append(cp)
        for a in range(n):
            copy(a, 0, sibling, me).wait_recv()
            for j, chip in enumerate(chips):
                copy(a, 4 + j, (*chip, 1 - c), me).wait_recv()
        for cp in first + passed:
            cp.wait_send()
        for cp in mine:
            cp.wait()

    return Comm(xs, [jax.ShapeDtypeStruct((N_DEV,) + a.shape, a.dtype) for a in xs],
                [pltpu.SemaphoreType.DMA((7 * n,)), pltpu.SemaphoreType.DMA((7 * n,)), pltpu.SemaphoreType.DMA((n,))], start, finish)


def run_comm(name, comm):
    _call(lambda: None, [], name=name, grid=(1,), in_specs=[], out_specs=[], out_shape=[], comm=comm)
    return comm.results


def sibling_comm(gs):
    n = len(gs)

    def copies(g_refs, o_refs, sems):
        x, y, c = _me()
        return [pltpu.make_async_remote_copy(
            src_ref=g_refs[a].at[:, 1 - c], dst_ref=o_refs[a], send_sem=sems[0].at[a], recv_sem=sems[1].at[a],
            device_id=(x, y, 1 - c), device_id_type=MESH) for a in range(n)]

    def start(g_refs, o_refs, sems):
        for cp in copies(g_refs, o_refs, sems):
            cp.start()

    def finish(g_refs, o_refs, sems):
        for cp in copies(g_refs, o_refs, sems):
            cp.wait()

    return Comm(gs, [jax.ShapeDtypeStruct((4,) + g.shape[2:], g.dtype) for g in gs],
                [pltpu.SemaphoreType.DMA((n,)), pltpu.SemaphoreType.DMA((n,))], start, finish)


def chip_comm(ts):
    n = len(ts)

    def copies(t_refs, o_refs, sems):
        x, y, c = _me()
        chips = [(1 - x, y), (x, 1 - y), (1 - x, 1 - y)]
        return [pltpu.make_async_remote_copy(
            src_ref=t_refs[a].at[2 * px + py], dst_ref=o_refs[a].at[j],
            send_sem=sems[0].at[3 * a + j], recv_sem=sems[1].at[3 * a + j],
            device_id=(px, py, c), device_id_type=MESH) for j, (px, py) in enumerate(chips) for a in range(n)]

    def start(t_refs, o_refs, sems):
        for cp in copies(t_refs, o_refs, sems):
            cp.start()

    def finish(t_refs, o_refs, sems):
        for cp in copies(t_refs, o_refs, sems):
            cp.wait()

    return Comm(ts, [jax.ShapeDtypeStruct((3,) + t.shape[1:], t.dtype) for t in ts],
                [pltpu.SemaphoreType.DMA((3 * n,)), pltpu.SemaphoreType.DMA((3 * n,))], start, finish)


def _row_tile(rows):
    return _pick(rows, (512, 304, 256, 128))


def pair_add(name, g, r):
    _, _, R, C = g.shape
    tr = _row_tile(R)

    def body(c_ref, g_ref, r_ref, o_ref):
        o_ref[0] = (g_ref[0, 0].astype(f32) + r_ref[0].astype(f32)).astype(o_ref.dtype)

    return pl.pallas_call(
        body, name=name,
        grid_spec=pltpu.PrefetchScalarGridSpec(
            num_scalar_prefetch=1, grid=(4, R // tr),
            in_specs=[pl.BlockSpec((1, 1, tr, C), lambda p, i, c: (p, c[0], i, 0)),
                      pl.BlockSpec((1, tr, C), lambda p, i, c: (p, i, 0))],
            out_specs=pl.BlockSpec((1, tr, C), lambda p, i, c: (p, i, 0))),
        out_shape=jax.ShapeDtypeStruct((4, R, C), g.dtype),
        compiler_params=_cparams(("parallel", "parallel")),
    )(lax.axis_index("c").reshape(1).astype(jnp.int32), g, r)


def _sum_of_four(t_ref, r_ref):
    return ((t_ref[0].astype(f32) + r_ref[0].astype(f32)) + r_ref[1].astype(f32)) + r_ref[2].astype(f32)


def _my_chip():
    return (2 * lax.axis_index("x") + lax.axis_index("y")).reshape(1).astype(jnp.int32)


def final_adamw(name, t, r, w, m, v):
    _, R, C = t.shape
    tr = _pick(R, (256, 128, 64, 32, 16))
    tc = C if tr < R or C % 256 else 256

    def body(p_ref, t_ref, r_ref, w_ref, m_ref, v_ref, g_ref, d_ref, nm_ref, nv_ref):
        g_ = _sum_of_four(t_ref, r_ref)
        g_ref[...] = g_
        d_ref[...], nm_ref[...], nv_ref[...] = _adam_math(w_ref[...], g_, m_ref[...], v_ref[...])

    flat = pl.BlockSpec((tr, tc), lambda i, j, p: (i, j))
    return pl.pallas_call(
        body, name=name,
        grid_spec=pltpu.PrefetchScalarGridSpec(
            num_scalar_prefetch=1, grid=(R // tr, C // tc),
            in_specs=[pl.BlockSpec((1, tr, tc), lambda i, j, p: (p[0], i, j)),
                      pl.BlockSpec((3, tr, tc), lambda i, j, p: (0, i, j)), flat, flat, flat],
            out_specs=[flat] * 4),
        out_shape=[jax.ShapeDtypeStruct((R, C), f32)] * 4,
        compiler_params=_cparams(("parallel", "parallel")),
    )(_my_chip(), t, r, w, m, v)


def final_adamw_layers(name, t0, r0, t1, r1, w, m, v):
    _, R, C = t0.shape
    tr = _pick(R, (256, 128, 64, 32, 16))

    def body(p_ref, t0_ref, r0_ref, t1_ref, r1_ref, w_ref, m_ref, v_ref, g_ref, d_ref, nm_ref, nv_ref):
        g_ = jnp.where(pl.program_id(0) == 0, _sum_of_four(t0_ref, r0_ref), _sum_of_four(t1_ref, r1_ref))
        g_ref[0] = g_
        d_ref[0], nm_ref[0], nv_ref[0] = _adam_math(w_ref[0], g_, m_ref[0], v_ref[0])

    mine = pl.BlockSpec((1, tr, C), lambda l, i, p: (p[0], i, 0))
    theirs = pl.BlockSpec((3, tr, C), lambda l, i, p: (0, i, 0))
    layer = pl.BlockSpec((1, tr, C), lambda l, i, p: (l, i, 0))
    return pl.pallas_call(
        body, name=name,
        grid_spec=pltpu.PrefetchScalarGridSpec(
            num_scalar_prefetch=1, grid=(2, R // tr),
            in_specs=[mine, theirs, mine, theirs, layer, layer, layer], out_specs=[layer] * 4),
        out_shape=[jax.ShapeDtypeStruct((2, R, C), f32)] * 4,
        compiler_params=_cparams(("parallel", "parallel")),
    )(_my_chip(), t0, r0, t1, r1, w, m, v)


class ReduceScatter:
    def __init__(self, tag, keys, grads):
        self.tag, self.keys, self.grads = tag, keys, grads
        self.send = [g.reshape((4, 2, g.shape[0] // N_DEV) + g.shape[1:]) for g in grads]

    def sibling(self):
        self.c1 = sibling_comm(self.send)
        return self.c1

    def chips(self):
        self.pairs = [pair_add(f"rs_pair_add_{self.tag}{i}", g, r) for i, (g, r) in enumerate(zip(self.send, self.c1.results))]
        self.c2 = chip_comm(self.pairs)
        return self.c2

    def parts(self):
        return {k: (t, r) for k, t, r in zip(self.keys, self.pairs, self.c2.results)}


IN_ROWS = {"z": (0, 2048), "xs": (2048, 4096), "B": (4096, 5120), "C": (5120, 6144)}
IN_COLS = 2 * INNER + 2 * GROUPS * NSTATE + HEADS


def sum_devices(g):
    def body(g_ref, o_ref):
        acc = g_ref[0]
        for i in range(1, N_DEV):
            acc = acc + g_ref[i]
        o_ref[...] = acc

    return pl.pallas_call(body, name="sum_devices", out_shape=jax.ShapeDtypeStruct(g.shape[1:], f32),
                          compiler_params=_cparams())(g)


def _pack(parts, unit, dtype, lead=()):
    flat = jnp.concatenate([p.reshape(lead + (-1,)).astype(dtype) for p in parts], axis=-1)
    n = flat.shape[-1]
    rows = -(-n // (unit * PACK_W)) * unit
    flat = jnp.pad(flat, [(0, 0)] * len(lead) + [(0, rows * PACK_W - n)])
    return flat.reshape(lead + (rows, PACK_W))


def _unpack(buf, shapes, lead=()):
    flat = buf.reshape(lead + (-1,))
    out, off = [], 0
    for shp in shapes:
        n = math.prod(shp)
        out.append(flat[..., off:off + n].reshape(lead + tuple(shp)))
        off += n
    return out


def _pad_lanes(a):
    return jnp.pad(a, [(0, 0)] * (a.ndim - 1) + [(0, LANES - a.shape[-1])])


_NN = (((1,), (0,)), ((), ()))
_NT = (((1,), (1,)), ((), ()))


def _mm(a, b, dims):
    return lax.dot_general(a.astype(MXU), b.astype(MXU), dims, preferred_element_type=f32)


def _ffn_fwd(tag, x, norm_g, w_inT, conv_w, conv_b, mid_comm=None):
    (h,) = rowwise(f"{tag}_norm", lambda x_, g_: (_rms_fwd(x_, g_),), [x], [norm_g], [(D, MXU)])
    ct, nblk = FFN_CT, FFN // FFN_CT

    def body(h_ref, wg_ref, wv_ref, cw_ref, cb_ref, gp_ref, v_ref, a_ref):
        h_ = h_ref[...]
        gp, v_ = _mm(h_, wg_ref[...], _NT), _mm(h_, wv_ref[...], _NT)
        gp_ref[...] = gp
        v_ref[...] = v_
        a_ref[...] = (_silu(_conv(gp, cw_ref[...], cb_ref[...])) * v_).astype(a_ref.dtype)

    col = pl.BlockSpec((S, ct), lambda j: (0, j))
    gate_pre, val, act = _call(
        body, [h, w_inT, w_inT, conv_w, conv_b], comm=mid_comm, name=f"{tag}_in", grid=(nblk,),
        in_specs=[pl.BlockSpec((S, D), lambda j: (0, 0)), pl.BlockSpec((ct, D), lambda j: (j, 0)),
                  pl.BlockSpec((ct, D), lambda j: (nblk + j, 0)), pl.BlockSpec((CONV_F, ct), lambda j: (0, j)),
                  pl.BlockSpec((1, ct), lambda j: (0, j))],
        out_specs=[col, col, col],
        out_shape=[jax.ShapeDtypeStruct((S, FFN), f32), jax.ShapeDtypeStruct((S, FFN), f32), jax.ShapeDtypeStruct((S, FFN), MXU)],
        sem=("parallel",))
    return act, (x, h, gate_pre, val, act)


FFN_CT = 256
CONV_CT = 256


def _proj_conv(name, h, wT, row0, cw, cb, comm=None):
    C, ct = cw.shape[1], CONV_CT

    def body(h_ref, w_ref, cw_ref, cb_ref, p_ref, c_ref):
        p = _mm(h_ref[...], w_ref[...], _NT)
        p_ref[...] = p
        c_ref[...] = _silu(_conv(p, cw_ref[...], cb_ref[...]))

    col = pl.BlockSpec((S, ct), lambda j: (0, j))
    return _call(
        body, [h, wT, cw, cb], comm=comm, name=name, grid=(C // ct,),
        in_specs=[pl.BlockSpec((S, D), lambda j: (0, 0)), pl.BlockSpec((ct, D), lambda j: (row0 // ct + j, 0)),
                  pl.BlockSpec((CONV_A, ct), lambda j: (0, j)), pl.BlockSpec((1, ct), lambda j: (0, j))],
        out_specs=[col, col], out_shape=[jax.ShapeDtypeStruct((S, C), f32)] * 2, sem=("parallel",))


def _dconv_wgrad(name, pre, dconv, cw, cb, h):
    C, ct = cw.shape[1], CONV_CT

    def body(p_ref, do_ref, cw_ref, cb_ref, h_ref, dp_ref, g_ref, dw_ref, db_ref):
        p_, w_ = p_ref[...], cw_ref[...]
        taps = _taps(p_, CONV_A)
        dx, dw, db = _conv_bwd(p_, w_, do_ref[...] * _dsilu(_conv(p_, w_, cb_ref[...], taps)), taps)
        dpm = dx.astype(MXU)
        dp_ref[...] = dpm
        g_ref[...] = lax.dot_general(dpm, h_ref[...].astype(MXU), (((0,), (0,)), ((), ())),
                                     preferred_element_type=f32).astype(g_ref.dtype)
        dw_ref[...] = dw
        db_ref[...] = db

    col = pl.BlockSpec((S, ct), lambda j: (0, j))
    return _call(
        body, [pre, dconv, cw, cb, h], name=name, grid=(C // ct,),
        in_specs=[col, col, pl.BlockSpec((CONV_A, ct), lambda j: (0, j)), pl.BlockSpec((1, ct), lambda j: (0, j)),
                  pl.BlockSpec((S, D), lambda j: (0, 0))],
        out_specs=[col, pl.BlockSpec((ct, D), lambda j: (j, 0)), pl.BlockSpec((CONV_A, ct), lambda j: (0, j)),
                   pl.BlockSpec((1, ct), lambda j: (0, j))],
        out_shape=[jax.ShapeDtypeStruct((S, C), MXU), jax.ShapeDtypeStruct((C, D), MXU),
                   jax.ShapeDtypeStruct((CONV_A, C), f32), jax.ShapeDtypeStruct((1, C), f32)],
        sem=("parallel",))


def _ffn_mid_bwd(name, dout, w_down, gate_pre, val, conv_w, conv_b, comm=None):
    ct = FFN_CT

    def body(do_ref, wd_ref, gp_ref, v_ref, w_ref, b_ref, dgv_ref, dw_ref, db_ref, dob_s):
        @pl.when(pl.program_id(0) == 0)
        def _():
            dob_s[...] = do_ref[...].astype(MXU)

        da = _mm(dob_s[...], wd_ref[...], _NT)
        gp, v_, w_ = gp_ref[...], v_ref[...], w_ref[...]
        taps = _taps(gp, CONV_F)
        gate = _conv(gp, w_, b_ref[...], taps)
        sg = _sigmoid(gate)
        dgp, dw, db = _conv_bwd(gp, w_, da * v_ * (sg * (1.0 + gate * (1.0 - sg))), taps)
        dgv_ref[0] = dgp.astype(dgv_ref.dtype)
        dgv_ref[1] = (da * (gate * sg)).astype(dgv_ref.dtype)
        dw_ref[...] = dw
        db_ref[...] = db

    col = pl.BlockSpec((S, ct), lambda j: (0, j))
    return _call(
        body, [dout, w_down, gate_pre, val, conv_w, conv_b], comm=comm, name=name, grid=(FFN // ct,),
        in_specs=[pl.BlockSpec((S, D), lambda j: (0, 0)), pl.BlockSpec((ct, D), lambda j: (j, 0)), col, col,
                  pl.BlockSpec((CONV_F, ct), lambda j: (0, j)), pl.BlockSpec((1, ct), lambda j: (0, j))],
        out_specs=[pl.BlockSpec((2, S, ct), lambda j: (0, 0, j)), pl.BlockSpec((CONV_F, ct), lambda j: (0, j)),
                   pl.BlockSpec((1, ct), lambda j: (0, j))],
        out_shape=[jax.ShapeDtypeStruct((2, S, FFN), MXU), jax.ShapeDtypeStruct((CONV_F, FFN), f32), jax.ShapeDtypeStruct((1, FFN), f32)],
        scratch=[pltpu.VMEM((S, D), MXU)], sem=("arbitrary",))


def _ffn_bwd(tag, layer, saved, norm_g, w_inT, conv_w, conv_b, w_down, dout, mid_comm=None):
    x, h, gate_pre, val, act = saved
    g_down = matmul(f"{tag}_wdown", act, dout, "tn", out_dtype=MXU)
    dgv, g_cw, g_cb = _ffn_mid_bwd(f"{tag}_dmid", dout, w_down, gate_pre, val, conv_w, conv_b, comm=mid_comm)
    g_inT = matmul_tn_stacked(f"{tag}_win", dgv, h, MXU)

    def din_fn(dg_, dv_, x_, do_, g_, wT):
        dx, dg = _rms_bwd(x_, g_, _mm(dg_, wT[:FFN], _NN) + _mm(dv_, wT[FFN:], _NN))
        return do_ + dx, dg

    rs = ReduceScatter(tag, (f"f_inT{layer}", f"f_down{layer}"), [g_inT, g_down])
    dx, g_norm = rowwise(f"{tag}_din", din_fn, [(dgv, 0), (dgv, 1), x, dout], [norm_g, w_inT], [(D, f32)], [(1, D)],
                         comm=rs.sibling())
    return dx, {f"f_norm{layer}": g_norm, f"f_conv_w{layer}": g_cw, f"f_conv_b{layer}": g_cb}, rs


def _land(W, keys, comm):
    for k, g in zip(keys, comm.results):
        W[k] = g.reshape(-1, g.shape[2])


def _local_step(x, pos, tgt, W, shards):
    G = {}
    gather = lambda *keys: gather_comm([shards[k] for k in keys])
    inv_freq = (ROPE_THETA ** (-jnp.arange(AH // 2, dtype=f32) / (AH // 2))).reshape(1, AH // 2)

    def in_fn(x_, g_, wT, wdtT):
        h_ = _rms_fwd(x_, g_).astype(MXU)
        return h_, _mm(h_, wT[slice(*IN_ROWS["z"])], _NT), _mm(h_, wdtT, _NT)

    h0, z, dt_pre = rowwise("a_in", in_fn, [x], [W["a_norm"], W["inT"], W["in_dtT"]], [(D, MXU), (INNER, f32), (LANES, f32)])
    pre, conv = {}, {}
    early = {"xs": ("a_out",), "B": (), "C": ()}
    for k in ("xs", "B", "C"):
        c = gather(*early[k]) if early[k] else None
        pre[k], conv[k] = _proj_conv(f"a_in_{k}", h0, W["inT"], IN_ROWS[k][0], W[f"cw_{k}"], W[f"cb_{k}"], comm=c)
        if c is not None:
            _land(W, early[k], c)
    c = gather("f_inT0", "f_down0")
    y, states = ssd_fwd(conv["xs"], conv["B"], conv["C"], dt_pre, W["dt_bias"], W["A_log"], W["D"], comm=c)
    _land(W, ("f_inT0", "f_down0"), c)

    def gate_norm(y_, z_, g_):
        yg = y_ * _silu(z_)
        w = INNER // GROUPS
        return (jnp.concatenate([_rms_fwd(yg[:, w * i:w * (i + 1)], g_[:, w * i:w * (i + 1)]) for i in range(GROUPS)], axis=1),)

    def out_fn(y_, z_, x_, g_, w_):
        (gn_,) = gate_norm(y_, z_, g_)
        gn_ = gn_.astype(MXU)
        return gn_, x_ + _mm(gn_, w_, _NN)

    c = gather("f_down1")
    gn, x1 = rowwise("a_out", out_fn, [y, z, x], [W["a_gnorm"], W["a_out"]], [(INNER, MXU), (D, f32)], comm=c)
    _land(W, ("f_down1",), c)

    c = gather("w_kv", "w_q", "w_o")
    act0, ffn0 = _ffn_fwd("f0", x1, W["f_norm0"], W["f_inT0"], W["f_cw0"], W["f_cb0"], mid_comm=c)
    _land(W, ("w_kv", "w_q", "w_o"), c)
    x2 = matmul("f0_down", act0, W["f_down0"], "nn", residual=x1)

    def qkv_fn(x_, gk, gb, wkv, bkv, wq, bq):
        kvn_, h2_ = _rms_fwd(x_, gk).astype(MXU), _rms_fwd(x_, gb).astype(MXU)
        return kvn_, h2_, _mm(kvn_, wkv, _NN) + bkv, _mm(h2_, wq, _NN) + bq

    kw = KVH * AH
    kvn, h2, kv, q_pre = rowwise("qkv_proj", qkv_fn, [x2], [W["kv_norm"], W["b_norm"], W["w_kv"], W["b_kv"], W["w_q"], W["b_q"]],
                                 [(D, MXU), (D, MXU), (2 * kw, f32), (D, f32)])

    def k_fwd(kv_, pos_, if_, g_):
        cos, sin = _rope_tables(pos_, if_)
        return _headnorm_rope_fwd(kv_[:, :kw], g_, cos, sin, KVH), kv_[:, kw:]

    k_rot, v_val = rowwise("k_rope", k_fwd, [kv, pos], [inv_freq, W["k_norm"]], [(kw, f32), (kw, f32)])

    def q_fwd(q_, pos_, if_, g_):
        cos, sin = _rope_tables(pos_, if_)
        return (_headnorm_rope_fwd(q_, g_, cos, sin, QH),)

    (q,) = rowwise("q_rope", q_fwd, [q_pre, pos], [inv_freq, W["q_norm"]], [(D, f32)])
    c = gather("f_inT1")
    att = attn_fwd(q, k_rot, v_val, W["sinks"], comm=c)
    _land(W, ("f_inT1",), c)
    x3 = matmul("o_proj", att, W["w_o"], "nn", bias=W["b_o"], residual=x2)

    act1, ffn1 = _ffn_fwd("f1", x3, W["f_norm1"], W["f_inT1"], W["f_cw1"], W["f_cb1"])

    def loss_fn(a_, x_, t_, w_):
        diff = x_ + _mm(a_, w_, _NN) - t_
        rows = jnp.sum(diff * diff, axis=1, keepdims=True) * (0.5 / D)
        return diff * (1.0 / D), jnp.sum(rows, axis=0, keepdims=True)

    dx4, loss = rowwise("f1_down_loss", loss_fn, [act1, x3, tgt], [W["f_down1"]], [(D, f32)], [(1, 1)])

    dx3, g, rs_f1 = _ffn_bwd("f1", 1, ffn1, W["f_norm1"], W["f_inT1"], W["f_cw1"], W["f_cb1"], W["f_down1"], dx4)
    G.update(g)

    datt = matmul("o_dproj", dx3, W["w_o"], "nt")
    g_wo = matmul("o_wproj", att, dx3, "tn", out_dtype=MXU)
    dq, dkp, dkc, dvp, dvc, G["sinks"] = attn_bwd(q, k_rot, v_val, W["sinks"], datt, comm=rs_f1.chips())

    def q_bwd(q_, pos_, dq_, dx_, if_, g_):
        cos, sin = _rope_tables(pos_, if_)
        dqp, dg = _headnorm_rope_bwd(q_, g_, cos, sin, dq_, QH)
        return dqp, dg, jnp.sum(dqp, axis=0, keepdims=True), jnp.sum(dx_, axis=0, keepdims=True)

    dq_pre, G["q_norm"], G["b_q"], G["b_o"] = rowwise("q_drope", q_bwd, [q_pre, pos, dq, dx3], [inv_freq, W["q_norm"]],
                                                      [(D, MXU)], [(1, AH), (1, D), (1, D)])
    g_wq = matmul("q_wproj", h2, dq_pre, "tn", out_dtype=MXU)
    dkv, G["k_norm"], G["b_kv"] = kv_bwd(kv, pos, inv_freq, W["k_norm"], dkp, dkc, dvp, dvc)
    g_wkv = matmul("kv_wproj", kvn, dkv, "tn", out_dtype=MXU)
    rs_att = ReduceScatter("att", ("w_kv", "w_q", "w_o"), [g_wkv, g_wq, g_wo])

    def x2_bwd(x_, dq_, dkv_, dx_, gb_, gk_, wq, wkv):
        d1, dgb = _rms_bwd(x_, gb_, _mm(dq_, wq, _NT))
        d2, dgk = _rms_bwd(x_, gk_, _mm(dkv_, wkv, _NT))
        return dx_ + d1 + d2, dgb, dgk

    dx2, G["b_norm"], G["kv_norm"] = rowwise("qkv_dproj", x2_bwd, [x2, dq_pre, dkv, dx3],
                                             [W["b_norm"], W["kv_norm"], W["w_q"], W["w_kv"]],
                                             [(D, f32)], [(1, D), (1, D)], comm=rs_att.sibling())

    dx1, g, rs_f0 = _ffn_bwd("f0", 0, ffn0, W["f_norm0"], W["f_inT0"], W["f_cw0"], W["f_cb0"], W["f_down0"], dx2,
                             mid_comm=rs_att.chips())
    G.update(g)

    rs_out = ReduceScatter("a_out", ("a_out",), [matmul("a_wout", gn, dx1, "tn", out_dtype=MXU)])

    def gate_norm_bwd(y_, z_, dx_, g_, w_out):
        dgn_ = _mm(dx_, w_out, _NT)
        w = INNER // GROUPS
        sg = _sigmoid(z_)
        sz = z_ * sg
        yg = y_ * sz
        parts, dgs = [], []
        for i in range(GROUPS):
            dseg, dg = _rms_bwd(yg[:, w * i:w * (i + 1)], g_[:, w * i:w * (i + 1)], dgn_[:, w * i:w * (i + 1)])
            parts.append(dseg)
            dgs.append(dg)
        dyg = jnp.concatenate(parts, axis=1)
        return dyg * sz, dyg * y_ * (sg * (1.0 + z_ * (1.0 - sg))), jnp.concatenate(dgs, axis=1)

    dy, dz, G["a_gnorm"] = rowwise("a_dout", gate_norm_bwd, [y, z, dx1], [W["a_gnorm"], W["a_out"]],
                                   [(INNER, f32), (INNER, MXU)], [(1, INNER)], comm=rs_out.sibling())
    dconv = {}
    dconv["xs"], dconv["B"], dconv["C"], ddt_pre, G["dt_bias"], G["A_log"], G["D"] = ssd_bwd(
        conv["xs"], conv["B"], conv["C"], dt_pre, W["dt_bias"], W["A_log"], W["D"], states, dy,
        comm=merge_comms([rs_f0.chips(), rs_out.chips()]))

    g_in, dpre = [matmul("a_win_z", dz, h0, "tn", out_dtype=MXU)], {}
    for k in ("xs", "B", "C"):
        dpre[k], g_k, G[f"cw_{k}"], G[f"cb_{k}"] = _dconv_wgrad(f"a_dconv_{k}", pre[k], dconv[k], W[f"cw_{k}"], W[f"cb_{k}"], h0)
        g_in.append(g_k)
    g_in.append(matmul("a_win_dt", ddt_pre, h0, "tn", out_dtype=MXU)[:HEADS])
    rs_in = ReduceScatter("a_in", ("inT",), [jnp.concatenate(g_in, axis=0)])
    run_comm("rs_in_sibling", rs_in.sibling())

    def x0_bwd(dz_, dxs_, db_, dc_, ddt_, x_, do_, g_, wT, wdtT):
        parts = zip((dz_, dxs_, db_, dc_), IN_ROWS.values())
        dh = sum(_mm(d_, wT[a:b], _NN) for d_, (a, b) in parts) + _mm(ddt_, wdtT, _NN)
        dx, dg = _rms_bwd(x_, g_, dh)
        return do_ + dx, dg

    dx, G["a_norm"] = rowwise("a_din", x0_bwd, [dz, dpre["xs"], dpre["B"], dpre["C"], ddt_pre, x, dx1],
                              [W["a_norm"], W["inT"], W["in_dtT"]], [(D, f32)], [(1, D)], comm=rs_in.chips())
    return loss, dx, G, [rs_f1, rs_att, rs_f0, rs_out, rs_in]


ROW_KEYS = ("inT", "a_out", "f_inT0", "f_down0", "w_kv", "w_q", "w_o", "f_inT1", "f_down1")


def _row_blocks(src):
    return {"inT": src["a_in_proj"][0].T, "a_out": src["a_out_proj"][0], "w_kv": src["w_kv"], "w_q": src["w_q"][0],
            "w_o": src["w_o"][0], "f_inT0": src["f_w_in"][0].T, "f_inT1": src["f_w_in"][1].T,
            "f_down0": src["f_w_down"][0], "f_down1": src["f_w_down"][1]}


def _from_row_blocks(rb):
    out = {"a_in_proj": rb["inT"].T[None], "a_out_proj": rb["a_out"][None], "w_kv": rb["w_kv"], "w_q": rb["w_q"][None],
           "w_o": rb["w_o"][None]}
    if "f_inT0" in rb:
        out["f_w_in"] = jnp.stack([rb["f_inT0"].T, rb["f_inT1"].T])
        out["f_w_down"] = jnp.stack([rb["f_down0"], rb["f_down1"]])
    return out


SMALL_SHARDED = ("a_norm", "a_conv_w", "a_conv_b", "a_gnorm", "f_conv_w")
REPLICATED = ("a_dt_bias", "a_A_log", "a_D", "kv_norm", "b_kv", "k_norm", "b_norm", "b_q", "q_norm", "sinks", "b_o",
              "f_norm", "f_conv_b")
ORDER = ("a_norm", "a_in_proj", "a_conv_w", "a_conv_b", "a_dt_bias", "a_A_log", "a_D", "a_gnorm", "a_out_proj", "kv_norm",
         "w_kv", "b_kv", "k_norm", "b_norm", "w_q", "b_q", "q_norm", "sinks", "w_o", "b_o", "f_norm", "f_w_in",
         "f_conv_w", "f_conv_b", "f_w_down")


def _gathered_to_whole(name, g):
    if name == "a_conv_w":
        return jnp.moveaxis(g[:, 0], 0, 1).reshape(g.shape[2], -1)
    if name in ("a_norm", "a_conv_b", "a_gnorm"):
        return g[:, 0].reshape(1, -1)
    if name == "f_conv_w":
        return jnp.moveaxis(g, 0, 2).reshape(g.shape[1], g.shape[2], -1)
    raise ValueError(name)


def _whole_to_shards(name, w):
    if name == "a_conv_w":
        return jnp.moveaxis(w.reshape(w.shape[0], N_DEV, -1), 1, 0)[:, None]
    if name in ("a_norm", "a_conv_b", "a_gnorm"):
        return w.reshape(N_DEV, 1, -1)
    if name == "f_conv_w":
        return jnp.moveaxis(w.reshape(w.shape[0], w.shape[1], N_DEV, -1), 2, 0)
    raise ValueError(name)


def _small_weights(whole):
    W = {}
    cw, cb = whole["a_conv_w"], whole["a_conv_b"]
    o = 0
    for k, n in (("xs", INNER), ("B", GROUPS * NSTATE), ("C", GROUPS * NSTATE)):
        W[f"cw_{k}"], W[f"cb_{k}"] = cw[:, o:o + n], cb[:, o:o + n]
        o += n
    W["a_norm"], W["a_gnorm"] = whole["a_norm"], whole["a_gnorm"]
    W["dt_bias"], W["A_log"], W["D"] = (_pad_lanes(whole[k]) for k in ("a_dt_bias", "a_A_log", "a_D"))
    W["kv_norm"], W["b_kv"], W["k_norm"] = whole["kv_norm"].reshape(1, -1), whole["b_kv"].reshape(1, -1), whole["k_norm"].reshape(1, -1)
    for k in ("b_norm", "b_q", "q_norm", "sinks", "b_o"):
        W[k] = whole[k]
    for i in range(2):
        W[f"f_norm{i}"] = whole["f_norm"][i:i + 1]
        W[f"f_cw{i}"], W[f"f_cb{i}"] = whole["f_conv_w"][i], whole["f_conv_b"][i:i + 1]
    return W


def _small_grads(G, shapes):
    nh = HEADS
    out = {
        "a_conv_w": jnp.concatenate([G["cw_xs"], G["cw_B"], G["cw_C"]], axis=1),
        "a_conv_b": jnp.concatenate([G["cb_xs"], G["cb_B"], G["cb_C"]], axis=1),
        "a_norm": G["a_norm"], "a_gnorm": G["a_gnorm"],
        "a_dt_bias": G["dt_bias"][:, :nh], "a_A_log": G["A_log"][:, :nh], "a_D": G["D"][:, :nh],
        "kv_norm": G["kv_norm"], "b_kv": G["b_kv"], "k_norm": G["k_norm"], "b_norm": G["b_norm"],
        "b_q": G["b_q"], "q_norm": G["q_norm"], "sinks": G["sinks"], "b_o": G["b_o"],
        "f_norm": jnp.concatenate([G["f_norm0"], G["f_norm1"]], axis=0),
        "f_conv_w": jnp.stack([G["f_conv_w0"], G["f_conv_w1"]]),
        "f_conv_b": jnp.concatenate([G["f_conv_b0"], G["f_conv_b1"]], axis=0),
    }
    return {k: val.reshape(shapes[k]) if k in shapes else val for k, val in out.items()}


def kernel(x, positions, a_norm, a_in_proj, a_conv_w, a_conv_b, a_dt_bias, a_A_log, a_D, a_gnorm, a_out_proj, kv_norm, w_kv, b_kv, k_norm, b_norm, w_q, b_q, q_norm, sinks, w_o, b_o, f_norm, f_w_in, f_conv_w, f_conv_b, f_w_down, loss_target, m_a_norm, m_a_in_proj, m_a_conv_w, m_a_conv_b, m_a_dt_bias, m_a_A_log, m_a_D, m_a_gnorm, m_a_out_proj, m_kv_norm, m_w_kv, m_b_kv, m_k_norm, m_b_norm, m_w_q, m_b_q, m_q_norm, m_sinks, m_w_o, m_b_o, m_f_norm, m_f_w_in, m_f_conv_w, m_f_conv_b, m_f_w_down, v_a_norm, v_a_in_proj, v_a_conv_w, v_a_conv_b, v_a_dt_bias, v_a_A_log, v_a_D, v_a_gnorm, v_a_out_proj, v_kv_norm, v_w_kv, v_b_kv, v_k_norm, v_b_norm, v_w_q, v_b_q, v_q_norm, v_sinks, v_w_o, v_b_o, v_f_norm, v_f_w_in, v_f_conv_w, v_f_conv_b, v_f_w_down):
    given = dict(locals())
    w_in = {n: given[n] for n in ORDER}
    m_in = {n: given["m_" + n] for n in ORDER}
    v_in = {n: given["v_" + n] for n in ORDER}
    dev = 4 * lax.axis_index("x") + 2 * lax.axis_index("y") + lax.axis_index("c")

    w2, m2, v2 = _row_blocks(w_in), _row_blocks(m_in), _row_blocks(v_in)
    small_pack = _pack([w_in[n] for n in SMALL_SHARDED], 8, f32)
    shards = {k: w2[k].astype(MXU) for k in ROW_KEYS}
    in_all, small_all = run_comm("ag_head", gather_comm([shards["inT"], small_pack]))
    whole = {n: w_in[n] for n in REPLICATED}
    for n, g in zip(SMALL_SHARDED, _unpack(small_all, [w_in[n].shape for n in SMALL_SHARDED], lead=(N_DEV,))):
        whole[n] = _gathered_to_whole(n, g)
    W = _small_weights(whole)
    W["inT"] = in_all.reshape(-1, D)
    W["in_dtT"] = jnp.pad(W["inT"][IN_COLS - HEADS:], ((0, LANES - HEADS), (0, 0)))

    loss, dx, G, scatters = _local_step(x[0], positions.reshape(S, 1).astype(f32), loss_target[0], W, shards)
    grads = _small_grads(G, {n: whole[n].shape for n in REPLICATED})

    small_names = SMALL_SHARDED + REPLICATED
    small_part = _pack([grads[n] for n in small_names] + [loss], 8, f32)
    small_gather = gather_comm([small_part])
    run_comm("ag_small_grads", small_gather)
    parts = {}
    for rs in scatters:
        parts.update(rs.parts())

    single = tuple(k for k in ROW_KEYS if not k.startswith("f_"))
    stepped = {k: final_adamw(f"adamw_{k}", *parts[k], w2[k], m2[k], v2[k]) for k in single}
    g_out, delta, new_m, new_v = (_from_row_blocks({k: stepped[k][i] for k in single}) for i in range(4))
    for n, key, lay in (("f_w_in", "f_inT", lambda a: jnp.swapaxes(a, 1, 2)), ("f_w_down", "f_down", lambda a: a)):
        res = final_adamw_layers(f"adamw_{n}", *parts[key + "0"], *parts[key + "1"], lay(w_in[n]), lay(m_in[n]), lay(v_in[n]))
        g_out[n], delta[n], new_m[n], new_v[n] = (lay(a) for a in res)
    *small_sums, loss_all = _unpack(sum_devices(small_gather.results[0]), [grads[n].shape for n in small_names] + [(1, 1)])
    for n, g in zip(small_names, small_sums):
        if n in SMALL_SHARDED:
            g_out[n] = lax.dynamic_index_in_dim(_whole_to_shards(n, g), dev, axis=0, keepdims=False)
        else:
            g_out[n] = g.reshape(w_in[n].shape)

    packs = [_pack([src[n] for n in small_names], 8, f32) for src in (w_in, g_out, m_in, v_in)]
    outs = adamw("adamw_small", *packs)
    for dst, buf in zip((delta, new_m, new_v), outs):
        for n, a in zip(small_names, _unpack(buf, [w_in[n].shape for n in small_names])):
            dst[n] = a

    return (loss_all[0, 0], dx[None], *[g_out[n] for n in ORDER], *[delta[n] for n in ORDER],
            *[new_m[n] for n in ORDER], *[new_v[n] for n in ORDER])
```

```python
import functools
import math

import jax
import jax.numpy as jnp
from jax import lax
from jax.experimental import pallas as pl
from jax.experimental.pallas import tpu as pltpu

f32 = jnp.float32
bf16 = jnp.bfloat16
MXU = bf16

N_DEV = 8
S = 2048
D = 1024
EPS = 1e-5
INNER = 2048
HEADS = 32
HP = 64
GROUPS = 8
HPG = HEADS // GROUPS
NSTATE = 128
CONV_A = 4
CHUNK = 256
NCHUNK = S // CHUNK
AH = 64
QH = 16
KVH = 4
QPK = QH // KVH
WIN = 128
NBLK = S // WIN
ROPE_THETA = 10000.0
FFN = 2816
CONV_F = 3
LANES = 128
PACK_W = 1024
VMEM_LIMIT = 56 * 1024 * 1024

ADAM_LR, ADAM_B1, ADAM_B2, ADAM_EPS, ADAM_WD, ADAM_STEP = 0.001, 0.9, 0.999, 1e-08, 0.01, 10

MESH = pl.DeviceIdType.MESH


def _cparams(sem=None):
    return pltpu.CompilerParams(dimension_semantics=sem, vmem_limit_bytes=VMEM_LIMIT)


def _pick(n, cands):
    for c in cands:
        if n % c == 0:
            return c
    return n


class Comm:
    def __init__(self, ins, out_shapes, sems, start, finish):
        self.ins, self.out_shapes, self.sems, self.start, self.finish = list(ins), list(out_shapes), list(sems), start, finish
        self.results, self.children = None, ()

    def set_results(self, res):
        self.results, o = list(res), 0
        for ch in self.children:
            ch.set_results(res[o:o + len(ch.out_shapes)])
            o += len(ch.out_shapes)


def merge_comms(comms):
    def each(fn_name, ins, outs, sems):
        i = o = s = 0
        for c in comms:
            getattr(c, fn_name)(ins[i:i + len(c.ins)], outs[o:o + len(c.out_shapes)], sems[s:s + len(c.sems)])
            i, o, s = i + len(c.ins), o + len(c.out_shapes), s + len(c.sems)

    merged = Comm([a for c in comms for a in c.ins], [a for c in comms for a in c.out_shapes], [a for c in comms for a in c.sems],
                  functools.partial(each, "start"), functools.partial(each, "finish"))
    merged.children = tuple(comms)
    return merged


def _call(body, args, *, name, grid, in_specs, out_specs, out_shape, scratch=(), sem=None, comm=None):
    if comm is None:
        return pl.pallas_call(body, name=name, grid=grid, in_specs=list(in_specs), out_specs=list(out_specs),
                              out_shape=list(out_shape), scratch_shapes=list(scratch), compiler_params=_cparams(sem))(*args)
    n_in, n_out, n_scr, c_in, c_out = len(in_specs), len(out_shape), len(scratch), len(comm.ins), len(comm.out_shapes)
    any_spec = pl.BlockSpec(memory_space=pl.ANY)

    def outer(*refs):
        ins, c_ins = refs[:n_in], refs[n_in:n_in + c_in]
        o = n_in + c_in
        outs, c_outs = refs[o:o + n_out], refs[o + n_out:o + n_out + c_out]
        o += n_out + c_out
        scr, c_sems = refs[o:o + n_scr], refs[o + n_scr:]
        ids = [pl.program_id(i) for i in range(len(grid))]
        first = functools.reduce(jnp.logical_and, [i == 0 for i in ids])
        last = functools.reduce(jnp.logical_and, [i == g - 1 for i, g in zip(ids, grid)])

        @pl.when(first)
        def _():
            comm.start(c_ins, c_outs, c_sems)

        body(*ins, *outs, *scr)

        @pl.when(last)
        def _():
            comm.finish(c_ins, c_outs, c_sems)

    res = pl.pallas_call(
        outer, name=name, grid=grid, in_specs=list(in_specs) + [any_spec] * c_in,
        out_specs=list(out_specs) + [any_spec] * c_out, out_shape=list(out_shape) + comm.out_shapes,
        scratch_shapes=list(scratch) + comm.sems, compiler_params=_cparams(("arbitrary",) * len(grid)),
    )(*args, *comm.ins)
    comm.set_results(res[n_out:])
    return res[:n_out]


def matmul(name, a, b, mode, out_dtype=f32, bias=None, residual=None):
    if mode == "nn":
        (M, K), (K2, N) = a.shape, b.shape
    elif mode == "nt":
        (M, K), (N, K2) = a.shape, b.shape
    else:
        (K, M), (K2, N) = a.shape, b.shape
    assert K == K2, (name, a.shape, b.shape)
    if mode == "tn":
        tm, tn = M, _pick(N, (512, 256, 128) if M <= 1024 else (256, 128))
        a_spec = pl.BlockSpec((K, M), lambda j: (0, 0))
        b_spec = pl.BlockSpec((K, tn), lambda j: (0, j))
        dims = (((0,), (0,)), ((), ()))
        grid, o_map, row_map = (N // tn,), (lambda j: (0, j)), (lambda j: (0, j))
    else:
        tm, tn = (256 if N >= 2048 else 512), N
        a_spec = pl.BlockSpec((tm, K), lambda i: (i, 0))
        b_spec = pl.BlockSpec(b.shape, lambda i: (0, 0))
        dims = (((1,), (0,)), ((), ())) if mode == "nn" else (((1,), (1,)), ((), ()))
        grid, o_map, row_map = (M // tm,), (lambda i: (i, 0)), (lambda i: (0, 0))
    ins, in_specs = [a, b], [a_spec, b_spec]
    if bias is not None:
        ins.append(bias)
        in_specs.append(pl.BlockSpec((1, tn), row_map))
    if residual is not None:
        ins.append(residual)
        in_specs.append(pl.BlockSpec((tm, tn), o_map))
    has_bias, has_res = bias is not None, residual is not None

    def body(a_ref, b_ref, *rest):
        rest = list(rest)
        bias_ref = rest.pop(0) if has_bias else None
        res_ref = rest.pop(0) if has_res else None
        (o_ref,) = rest
        r = lax.dot_general(a_ref[...].astype(MXU), b_ref[...].astype(MXU), dims, preferred_element_type=f32)
        if has_bias:
            r = r + bias_ref[...]
        if has_res:
            r = r + res_ref[...]
        o_ref[...] = r.astype(out_dtype)

    return pl.pallas_call(
        body, name=name, grid=grid, in_specs=in_specs,
        out_specs=pl.BlockSpec((tm, tn), o_map),
        out_shape=jax.ShapeDtypeStruct((M, N), out_dtype),
        compiler_params=_cparams(("parallel",)),
    )(*ins)


def matmul_tn_stacked(name, a, b, out_dtype):
    R, K, M = a.shape
    N = b.shape[1]
    tn = _pick(N, (256, 128))

    def body(a_ref, b_ref, o_ref):
        o_ref[0] = lax.dot_general(a_ref[0].astype(MXU), b_ref[...].astype(MXU), (((0,), (0,)), ((), ())),
                                   preferred_element_type=f32).astype(out_dtype)

    out = pl.pallas_call(
        body, name=name, grid=(R, N // tn),
        in_specs=[pl.BlockSpec((1, K, M), lambda r, j: (r, 0, 0)), pl.BlockSpec((K, tn), lambda r, j: (0, j))],
        out_specs=pl.BlockSpec((1, M, tn), lambda r, j: (r, 0, j)),
        out_shape=jax.ShapeDtypeStruct((R, M, N), out_dtype),
        compiler_params=_cparams(("parallel", "parallel")),
    )(a, b)
    return out.reshape(R * M, N)


def rowwise(name, fn, rows, pars, outs, accs=(), tile=512, comm=None):
    n_in, n_out = len(rows) + len(pars), len(outs)
    in_specs = [pl.BlockSpec((None, tile, r[0].shape[2]), functools.partial(lambda i, lead: (lead, i, 0), lead=r[1]))
                if isinstance(r, tuple) else pl.BlockSpec((tile, r.shape[1]), lambda i: (i, 0)) for r in rows]
    rows = [r[0] if isinstance(r, tuple) else r for r in rows]
    in_specs += [pl.BlockSpec(p.shape, lambda i: (0, 0)) for p in pars]
    out_specs = [pl.BlockSpec((tile, c), lambda i: (i, 0)) for c, _ in outs]
    out_specs += [pl.BlockSpec(shp, lambda i: (0, 0)) for shp in accs]
    out_shape = [jax.ShapeDtypeStruct((S, c), dt) for c, dt in outs]
    out_shape += [jax.ShapeDtypeStruct(shp, f32) for shp in accs]

    def body(*refs):
        res = fn(*[r[...] for r in refs[:n_in]])
        o_refs = refs[n_in:n_in + n_out]
        a_refs = refs[n_in + n_out:]
        for ref, val in zip(o_refs, res[:n_out]):
            ref[...] = val.astype(ref.dtype)
        if a_refs:
            @pl.when(pl.program_id(0) == 0)
            def _():
                for ref in a_refs:
                    ref[...] = jnp.zeros_like(ref)
            for ref, val in zip(a_refs, res[n_out:]):
                ref[...] += val

    return _call(body, [*rows, *pars], name=name, grid=(S // tile,), in_specs=in_specs, out_specs=out_specs,
                 out_shape=out_shape, sem=("arbitrary",) if accs else ("parallel",), comm=comm)


def _sigmoid(x):
    return 0.5 * jnp.tanh(0.5 * x) + 0.5


def _silu(x):
    return x * _sigmoid(x)


def _dsilu(x):
    sg = _sigmoid(x)
    return sg * (1.0 + x * (1.0 - sg))


def _softplus(x):
    return jnp.maximum(x, 0.0) + jnp.log(1.0 + jnp.exp(-jnp.abs(x)))


def _rms_fwd(x, g):
    r = lax.rsqrt(jnp.mean(x * x, axis=-1, keepdims=True) + EPS)
    return x * r * g


def _rms_bwd(x, g, dh):
    r = lax.rsqrt(jnp.mean(x * x, axis=-1, keepdims=True) + EPS)
    xh = x * r
    dxh = dh * g
    dx = r * (dxh - xh * jnp.mean(dxh * xh, axis=-1, keepdims=True))
    return dx, jnp.sum(dh * xh, axis=0, keepdims=True)


def _taps(x, width):
    row = lax.broadcasted_iota(jnp.int32, (8, x.shape[1]), 0)

    def shifted(s):
        r = pltpu.roll(x, s, 0)
        return jnp.concatenate([jnp.where(row >= s, r[:8], 0.0), r[8:]], axis=0)

    return [shifted(s) for s in range(width - 1, 0, -1)] + [x]


def _conv(x, w, b, taps=None):
    width = w.shape[0]
    taps = _taps(x, width) if taps is None else taps
    out = b + w[0:1, :] * taps[0]
    for k in range(1, width):
        out = out + w[k:k + 1, :] * taps[k]
    return out


def _conv_bwd(x, w, dc, taps=None):
    width, n = w.shape[0], x.shape[0]
    taps = _taps(x, width) if taps is None else taps
    row = lax.broadcasted_iota(jnp.int32, (8, x.shape[1]), 0)
    dx = w[width - 1:width, :] * dc
    for k in range(width - 1):
        s = width - 1 - k
        r = pltpu.roll(dc, n - s, 0)
        dx = dx + w[k:k + 1, :] * jnp.concatenate([r[:n - 8], jnp.where(row < 8 - s, r[n - 8:], 0.0)], axis=0)
    dw = jnp.concatenate([jnp.sum(dc * t, axis=0, keepdims=True) for t in taps], axis=0)
    return dx, dw, jnp.sum(dc, axis=0, keepdims=True)


def _rope_tables(pos, inv_freq):
    ang = pos * inv_freq
    return jnp.cos(ang), jnp.sin(ang)


def _split2(v):
    hi = v.astype(bf16)
    return hi, (v - hi.astype(f32)).astype(bf16)


def _head_maps(width):
    shift = AH.bit_length() - 1
    to_head = (lax.broadcasted_iota(jnp.int32, (width, LANES), 0) >> shift) == lax.broadcasted_iota(jnp.int32, (width, LANES), 1)
    from_head = lax.broadcasted_iota(jnp.int32, (LANES, width), 0) == (lax.broadcasted_iota(jnp.int32, (LANES, width), 1) >> shift)
    return to_head.astype(bf16), from_head.astype(bf16)


def _head_sums(v, to_head):
    hi, lo = _split2(v)
    return jnp.dot(hi, to_head, preferred_element_type=f32) + jnp.dot(lo, to_head, preferred_element_type=f32)


def _head_spread(s, from_head):
    hi, lo = _split2(s)
    return jnp.dot(hi, from_head, preferred_element_type=f32) + jnp.dot(lo, from_head, preferred_element_type=f32)


def _rope_full(cos, sin, width):
    half = AH // 2
    pad = jnp.zeros((cos.shape[0], LANES - half), f32)
    r = lax.broadcasted_iota(jnp.int32, (LANES, width), 0)
    lane = lax.broadcasted_iota(jnp.int32, (LANES, width), 1)
    spread = ((lane & (half - 1)) == r).astype(bf16)
    full = lambda t: _head_spread(jnp.concatenate([t, pad], axis=1), spread)
    first = (lax.broadcasted_iota(jnp.int32, (1, width), 1) & (AH - 1)) < half
    sin_f = full(sin)
    return full(cos), jnp.where(first, -sin_f, sin_f), first


def _swap_halves(v, first):
    half, width = AH // 2, v.shape[1]
    return jnp.where(first, pltpu.roll(v, width - half, 1), pltpu.roll(v, half, 1))


def _headnorm_rope_fwd(x, g, cos, sin, heads):
    to_head, from_head = _head_maps(heads * AH)
    cos_f, sin_s, first = _rope_full(cos, sin, heads * AH)
    r = _head_spread(lax.rsqrt(_head_sums(x * x, to_head) * (1.0 / AH) + EPS), from_head)
    n = x * r * jnp.tile(g, (1, heads))
    return n * cos_f + _swap_halves(n, first) * sin_s


def _headnorm_rope_bwd(x, g, cos, sin, dout, heads):
    width = heads * AH
    to_head, from_head = _head_maps(width)
    cos_f, sin_s, first = _rope_full(cos, sin, width)
    r = _head_spread(lax.rsqrt(_head_sums(x * x, to_head) * (1.0 / AH) + EPS), from_head)
    xh = x * r
    dn = dout * cos_f - _swap_halves(dout, first) * sin_s
    dxh = dn * jnp.tile(g, (1, heads))
    m = _head_spread(_head_sums(dxh * xh, to_head) * (1.0 / AH), from_head)
    dx = r * (dxh - xh * m)
    dg_lanes = jnp.sum(dn * xh, axis=0, keepdims=True)
    fold = ((lax.broadcasted_iota(jnp.int32, (width, LANES), 0) & (AH - 1))
            == lax.broadcasted_iota(jnp.int32, (width, LANES), 1)).astype(f32)
    dg = jnp.dot(jnp.broadcast_to(dg_lanes, (8, width)), fold, precision=lax.Precision.HIGHEST, preferred_element_type=f32)
    return dx, dg[0:1, :AH]


def _ssd_prep(dt_pre, dt_bias, a_log, dt_s, acum_s, acumT_s):
    dt = _softplus(dt_pre + dt_bias)
    a = dt * (-jnp.exp(a_log))
    row = lax.broadcasted_iota(jnp.int32, (CHUNK, CHUNK), 0)
    col = lax.broadcasted_iota(jnp.int32, (CHUNK, CHUNK), 1)
    dt_s[...] = dt
    acum_s[...] = jnp.dot((col <= row).astype(f32), a, precision=lax.Precision.HIGHEST, preferred_element_type=f32)
    acumT_s[...] = lax.dot_general(a, (row <= col).astype(f32), (((0,), (0,)), ((), ())),
                                   precision=lax.Precision.HIGHEST, preferred_element_type=f32)


def _head_cols(h, dt_s, acum_s, acumT_s):
    lane = lax.broadcasted_iota(jnp.int32, (1, LANES), 1)
    oh_l = (lane == h).astype(f32)
    sub = lax.broadcasted_iota(jnp.int32, (LANES, 1), 0)
    oh_s = (sub == h).astype(f32)
    dt_h = jnp.sum(dt_s[...] * oh_l, axis=1, keepdims=True)
    ac_h = jnp.sum(acum_s[...] * oh_l, axis=1, keepdims=True)
    acr_h = jnp.sum(acumT_s[...] * oh_s, axis=0, keepdims=True)
    return oh_l, dt_h, ac_h, acr_h


def ssd_fwd(xs, Bm, Cm, dt_pre, dt_bias, a_log, d_skip, comm=None):
    def body(xs_ref, b_ref, c_ref, dtp_ref, bias_ref, alog_ref, d_ref, y_ref, st_ref, state, dt_s, acum_s, acumT_s):
        c, g = pl.program_id(0), pl.program_id(1)

        @pl.when(g == 0)
        def _():
            _ssd_prep(dtp_ref[...], bias_ref[...], alog_ref[...], dt_s, acum_s, acumT_s)

        row = lax.broadcasted_iota(jnp.int32, (CHUNK, CHUNK), 0)
        col = lax.broadcasted_iota(jnp.int32, (CHUNK, CHUNK), 1)
        causal = col <= row
        Bb, Cb = b_ref[...], c_ref[...]
        cb = lax.dot_general(Cb.astype(MXU), Bb.astype(MXU), (((1,), (1,)), ((), ())), preferred_element_type=f32)
        xs_blk = xs_ref[...]

        @pl.when(c == 0)
        def _():
            for j in range(HPG):
                state[g * HPG + j] = jnp.zeros((NSTATE, HP), f32)

        prevs = [state[g * HPG + j] for j in range(HPG)]
        y_off_all = jnp.dot(Cb.astype(MXU), jnp.concatenate(prevs, axis=1).astype(MXU), preferred_element_type=f32)
        ys, xds, e_ends = [], [], []
        for j in range(HPG):
            oh_l, dt_h, ac_h, acr_h = _head_cols(g * HPG + j, dt_s, acum_s, acumT_s)
            decay = jnp.exp(jnp.where(causal, ac_h - acr_h, -1e30))
            w = (cb * decay).astype(MXU)
            xs_h = xs_blk[:, HP * j:HP * (j + 1)]
            xd = xs_h * dt_h
            y_diag = jnp.dot(w, xd.astype(MXU), preferred_element_type=f32)
            y_off = y_off_all[:, HP * j:HP * (j + 1)] * jnp.exp(ac_h)
            d_h = jnp.sum(d_ref[...] * oh_l, axis=1, keepdims=True)
            ys.append(y_diag + y_off + xs_h * d_h)
            a_end = ac_h[CHUNK - 1:CHUNK, :]
            xds.append(xd * jnp.exp(a_end - ac_h))
            e_ends.append(jnp.exp(a_end))
        s_c = lax.dot_general(Bb.astype(MXU), jnp.concatenate(xds, axis=1).astype(MXU), (((0,), (0,)), ((), ())),
                              preferred_element_type=f32)
        for j in range(HPG):
            st_ref[0, j] = prevs[j]
            state[g * HPG + j] = prevs[j] * e_ends[j] + s_c[:, HP * j:HP * (j + 1)]
        y_ref[...] = jnp.concatenate(ys, axis=1)

    par = pl.BlockSpec((1, LANES), lambda c, g: (0, 0))
    return _call(
        body, [xs, Bm, Cm, dt_pre, dt_bias, a_log, d_skip], comm=comm, name="ssd_fwd", grid=(NCHUNK, GROUPS),
        in_specs=[pl.BlockSpec((CHUNK, HPG * HP), lambda c, g: (c, g)),
                  pl.BlockSpec((CHUNK, NSTATE), lambda c, g: (c, g)),
                  pl.BlockSpec((CHUNK, NSTATE), lambda c, g: (c, g)),
                  pl.BlockSpec((CHUNK, LANES), lambda c, g: (c, 0)), par, par, par],
        out_specs=[pl.BlockSpec((CHUNK, HPG * HP), lambda c, g: (c, g)),
                   pl.BlockSpec((1, HPG, NSTATE, HP), lambda c, g: (c, g, 0, 0))],
        out_shape=[jax.ShapeDtypeStruct((S, INNER), f32), jax.ShapeDtypeStruct((NCHUNK, HEADS, NSTATE, HP), f32)],
        scratch=[pltpu.VMEM((HEADS, NSTATE, HP), f32), pltpu.VMEM((CHUNK, LANES), f32),
                 pltpu.VMEM((CHUNK, LANES), f32), pltpu.VMEM((LANES, CHUNK), f32)],
        sem=("arbitrary", "arbitrary"))


def ssd_bwd(xs, Bm, Cm, dt_pre, dt_bias, a_log, d_skip, states, dy, comm=None):
    rev = lambda c: NCHUNK - 1 - c

    def body(xs_ref, b_ref, c_ref, dtp_ref, bias_ref, alog_ref, d_ref, st_ref, dy_ref,
             dxs_ref, db_ref, dc_ref, ddt_ref, dbias_ref, dalog_ref, dd_ref,
             dstate, dt_s, acum_s, acumT_s, dacum_s, ddt_s, da_s):
        c, g = pl.program_id(0), pl.program_id(1)

        @pl.when(g == 0)
        def _():
            _ssd_prep(dtp_ref[...], bias_ref[...], alog_ref[...], dt_s, acum_s, acumT_s)
            dacum_s[...] = jnp.zeros_like(dacum_s)
            ddt_s[...] = jnp.zeros_like(ddt_s)

        @pl.when((c == 0) & (g == 0))
        def _():
            da_s[...] = jnp.zeros_like(da_s)
            dd_ref[...] = jnp.zeros_like(dd_ref)
            dbias_ref[...] = jnp.zeros_like(dbias_ref)
            dalog_ref[...] = jnp.zeros_like(dalog_ref)

        row = lax.broadcasted_iota(jnp.int32, (CHUNK, CHUNK), 0)
        col = lax.broadcasted_iota(jnp.int32, (CHUNK, CHUNK), 1)
        sub_l = lax.broadcasted_iota(jnp.int32, (CHUNK, 1), 0)
        last = (sub_l == CHUNK - 1).astype(f32)
        nt = (((1,), (1,)), ((), ()))
        tn = (((0,), (0,)), ((), ()))
        Bb, Cb = b_ref[...], c_ref[...]
        Bm_, Cm_ = Bb.astype(MXU), Cb.astype(MXU)
        cb = lax.dot_general(Cm_, Bm_, nt, preferred_element_type=f32)
        bc = lax.dot_general(Bm_, Cm_, nt, preferred_element_type=f32)
        xs_blk, dy_blk = xs_ref[...], dy_ref[...]
        dxs, dB, dC = [], jnp.zeros((CHUNK, NSTATE), f32), jnp.zeros((CHUNK, NSTATE), f32)
        for j in range(HPG):
            h = g * HPG + j
            oh_l, dt_h, ac_h, acr_h = _head_cols(h, dt_s, acum_s, acumT_s)

            @pl.when(c == 0)
            def _():
                dstate[h] = jnp.zeros((NSTATE, HP), f32)

            dnext = dstate[h]
            prev = st_ref[0, j]
            lm = jnp.exp(jnp.where(col <= row, ac_h - acr_h, -1e30))
            lmT = jnp.exp(jnp.where(row <= col, acr_h - ac_h, -1e30))
            xs_h = xs_blk[:, HP * j:HP * (j + 1)]
            dy_h = dy_blk[:, HP * j:HP * (j + 1)]
            xd = xs_h * dt_h
            xdm, dym = xd.astype(MXU), dy_h.astype(MXU)
            ea = jnp.exp(ac_h)
            a_end = ac_h[CHUNK - 1:CHUNK, :]
            e_end = jnp.exp(a_end)
            dte = jnp.exp(a_end - ac_h)
            dnm, pvm = dnext.astype(MXU), prev.astype(MXU)
            bd = jnp.dot(Bm_, dnm, preferred_element_type=f32)
            dxd = jnp.dot((bc * lmT).astype(MXU), dym, preferred_element_type=f32) + dte * bd
            dw = lax.dot_general(dym, xdm, nt, preferred_element_type=f32)
            dwT = lax.dot_general(xdm, dym, nt, preferred_element_type=f32)
            dcb = dw * lm
            dbc = dwT * lmT
            eady = (ea * dy_h).astype(MXU)
            dC = dC + jnp.dot(dcb.astype(MXU), Bm_, preferred_element_type=f32) \
                + lax.dot_general(eady, pvm, nt, preferred_element_type=f32)
            dB = dB + jnp.dot(dbc.astype(MXU), Cm_, preferred_element_type=f32) \
                + dte * lax.dot_general(xdm, dnm, nt, preferred_element_type=f32)
            dstate[h] = lax.dot_general(Cm_, eady, tn, preferred_element_type=f32) + e_end * dnext
            r1 = jnp.sum(dcb * cb, axis=1, keepdims=True)
            r2 = jnp.sum(dbc * bc, axis=1, keepdims=True)
            y_off = jnp.dot(Cm_, pvm, preferred_element_type=f32) * ea
            t3 = jnp.sum(dy_h * y_off, axis=1, keepdims=True)
            t4 = jnp.sum(bd * xd, axis=1, keepdims=True) * dte
            end_extra = jnp.sum(t4, axis=0, keepdims=True) + e_end * jnp.sum(jnp.sum(prev * dnext, axis=1, keepdims=True), axis=0, keepdims=True)
            dacum_h = r1 - r2 + t3 - t4 + last * end_extra
            dacum_s[...] += dacum_h * oh_l
            ddt_s[...] += jnp.sum(dxd * xs_h, axis=1, keepdims=True) * oh_l
            d_h = jnp.sum(d_ref[...] * oh_l, axis=1, keepdims=True)
            dxs.append(dxd * dt_h + dy_h * d_h)
            dd_ref[...] += oh_l * jnp.sum(jnp.sum(dy_h * xs_h, axis=1, keepdims=True), axis=0, keepdims=True)
        dxs_ref[...] = jnp.concatenate(dxs, axis=1)
        db_ref[...] = dB
        dc_ref[...] = dC

        @pl.when(g == GROUPS - 1)
        def _():
            a_row = -jnp.exp(alog_ref[...])
            da = jnp.dot((row <= col).astype(f32), dacum_s[...], precision=lax.Precision.HIGHEST, preferred_element_type=f32)
            da_s[...] += jnp.sum(da * dt_s[...], axis=0, keepdims=True)
            z = dtp_ref[...] + bias_ref[...]
            ddt_pre = (ddt_s[...] + da * a_row) * _sigmoid(z)
            ddt_ref[...] = ddt_pre.astype(ddt_ref.dtype)
            dbias_ref[...] += jnp.sum(ddt_pre, axis=0, keepdims=True)

            @pl.when(c == NCHUNK - 1)
            def _():
                dalog_ref[...] = da_s[...] * a_row

    par = pl.BlockSpec((1, LANES), lambda c, g: (0, 0))
    return _call(
        body, [xs, Bm, Cm, dt_pre, dt_bias, a_log, d_skip, states, dy], comm=comm, name="ssd_bwd", grid=(NCHUNK, GROUPS),
        in_specs=[pl.BlockSpec((CHUNK, HPG * HP), lambda c, g: (rev(c), g)),
                  pl.BlockSpec((CHUNK, NSTATE), lambda c, g: (rev(c), g)),
                  pl.BlockSpec((CHUNK, NSTATE), lambda c, g: (rev(c), g)),
                  pl.BlockSpec((CHUNK, LANES), lambda c, g: (rev(c), 0)), par, par, par,
                  pl.BlockSpec((1, HPG, NSTATE, HP), lambda c, g: (rev(c), g, 0, 0)),
                  pl.BlockSpec((CHUNK, HPG * HP), lambda c, g: (rev(c), g))],
        out_specs=[pl.BlockSpec((CHUNK, HPG * HP), lambda c, g: (rev(c), g)),
                   pl.BlockSpec((CHUNK, NSTATE), lambda c, g: (rev(c), g)),
                   pl.BlockSpec((CHUNK, NSTATE), lambda c, g: (rev(c), g)),
                   pl.BlockSpec((CHUNK, LANES), lambda c, g: (rev(c), 0)), par, par, par],
        out_shape=[jax.ShapeDtypeStruct((S, INNER), f32), jax.ShapeDtypeStruct((S, GROUPS * NSTATE), f32),
                   jax.ShapeDtypeStruct((S, GROUPS * NSTATE), f32), jax.ShapeDtypeStruct((S, LANES), MXU),
                   jax.ShapeDtypeStruct((1, LANES), f32), jax.ShapeDtypeStruct((1, LANES), f32),
                   jax.ShapeDtypeStruct((1, LANES), f32)],
        scratch=[pltpu.VMEM((HEADS, NSTATE, HP), f32), pltpu.VMEM((CHUNK, LANES), f32),
                 pltpu.VMEM((CHUNK, LANES), f32), pltpu.VMEM((LANES, CHUNK), f32),
                 pltpu.VMEM((CHUNK, LANES), f32), pltpu.VMEM((CHUNK, LANES), f32), pltpu.VMEM((1, LANES), f32)],
        sem=("arbitrary", "arbitrary"))


ATT_STACK_FWD, ATT_STACK_BWD = 4, 2


def _attn_kv(kp, kc, vp, vc, hk):
    sl = slice(AH * hk, AH * (hk + 1))
    return (jnp.concatenate([kp[:, sl], kc[:, sl]], axis=0).astype(MXU),
            jnp.concatenate([vp[:, sl], vc[:, sl]], axis=0).astype(MXU))


def _stack_heads(x, heads):
    return jnp.concatenate([x[:, AH * h:AH * (h + 1)] for h in heads], axis=0)


def _attn_block(n, q, kb, sinks, heads):
    rows = len(heads) * WIN
    qi = lax.broadcasted_iota(jnp.int32, (rows, 2 * WIN), 0) & (WIN - 1)
    ki = lax.broadcasted_iota(jnp.int32, (rows, 2 * WIN), 1)
    rel = qi + WIN - ki
    mask = (rel >= 0) & (rel < WIN) & ((ki >= WIN) | (n > 0))
    qg = _stack_heads(q, heads).astype(MXU)
    s = lax.dot_general(qg, kb, (((1,), (1,)), ((), ())), preferred_element_type=f32) * (AH ** -0.5)
    s = jnp.where(mask, s, -1e30)
    sink = jnp.concatenate([jnp.broadcast_to(sinks[:, h:h + 1], (WIN, 1)) for h in heads], axis=0)
    m = jnp.maximum(jnp.max(s, axis=1, keepdims=True), sink)
    p = jnp.exp(s - m)
    ps = jnp.exp(sink - m)
    inv = 1.0 / (jnp.sum(p, axis=1, keepdims=True) + ps)
    return qg, p * inv, ps * inv


def _head_blocks(hk, stack):
    return [list(range(QPK * hk + i, QPK * hk + i + stack)) for i in range(0, QPK, stack)]


def _kv_specs():
    prev = lambda n: (jnp.maximum(n - 1, 0), 0)
    cur = lambda n: (n, 0)
    w = KVH * AH
    return [pl.BlockSpec((WIN, w), prev), pl.BlockSpec((WIN, w), cur), pl.BlockSpec((WIN, w), prev), pl.BlockSpec((WIN, w), cur)]


def attn_fwd(q, k, v, sinks, comm=None):
    def body(q_ref, kp_ref, kc_ref, vp_ref, vc_ref, s_ref, o_ref):
        n = pl.program_id(0)
        q_, kp, kc, vp, vc, sk = q_ref[...], kp_ref[...], kc_ref[...], vp_ref[...], vc_ref[...], s_ref[...]
        outs = []
        for hk in range(KVH):
            kb, vb = _attn_kv(kp, kc, vp, vc, hk)
            for heads in _head_blocks(hk, ATT_STACK_FWD):
                _, pr, _ = _attn_block(n, q_, kb, sk, heads)
                o = jnp.dot(pr.astype(MXU), vb, preferred_element_type=f32)
                outs += [o[WIN * i:WIN * (i + 1)] for i in range(len(heads))]
        o_ref[...] = jnp.concatenate(outs, axis=1)

    return _call(
        body, [q, k, k, v, v, sinks], comm=comm, name="attn_fwd", grid=(NBLK,),
        in_specs=[pl.BlockSpec((WIN, D), lambda n: (n, 0))] + _kv_specs() + [pl.BlockSpec((1, QH), lambda n: (0, 0))],
        out_specs=[pl.BlockSpec((WIN, D), lambda n: (n, 0))],
        out_shape=[jax.ShapeDtypeStruct((S, D), f32)], sem=("parallel",))[0]


def attn_bwd(q, k, v, sinks, dout, comm=None):
    def body(q_ref, kp_ref, kc_ref, vp_ref, vc_ref, s_ref, do_ref, dq_ref, dkp_ref, dkc_ref, dvp_ref, dvc_ref, ds_ref):
        n = pl.program_id(0)

        @pl.when(n == 0)
        def _():
            ds_ref[...] = jnp.zeros_like(ds_ref)

        q_, kp, kc, vp, vc, sk, do = q_ref[...], kp_ref[...], kc_ref[...], vp_ref[...], vc_ref[...], s_ref[...], do_ref[...]
        lane = lax.broadcasted_iota(jnp.int32, (1, QH), 1)
        nt = (((1,), (1,)), ((), ()))
        tn = (((0,), (0,)), ((), ()))
        dqs, dkps, dkcs, dvps, dvcs = [], [], [], [], []
        dsink = jnp.zeros((1, QH), f32)
        for hk in range(KVH):
            kb, vb = _attn_kv(kp, kc, vp, vc, hk)
            dkb, dvb = jnp.zeros((2 * WIN, AH), f32), jnp.zeros((2 * WIN, AH), f32)
            for heads in _head_blocks(hk, ATT_STACK_BWD):
                qg, pr, prs = _attn_block(n, q_, kb, sk, heads)
                dog = _stack_heads(do, heads).astype(MXU)
                dp = lax.dot_general(dog, vb, nt, preferred_element_type=f32)
                dvb = dvb + lax.dot_general(pr.astype(MXU), dog, tn, preferred_element_type=f32)
                delta = jnp.sum(pr * dp, axis=1, keepdims=True)
                ds = (pr * (dp - delta)).astype(MXU)
                dsk = -prs * delta
                for i, h in enumerate(heads):
                    dsink = dsink + jnp.sum(dsk[WIN * i:WIN * (i + 1)], axis=0, keepdims=True) * (lane == h).astype(f32)
                dqg = jnp.dot(ds, kb, preferred_element_type=f32) * (AH ** -0.5)
                dkb = dkb + lax.dot_general(ds, qg, tn, preferred_element_type=f32) * (AH ** -0.5)
                dqs += [dqg[WIN * i:WIN * (i + 1)] for i in range(len(heads))]
            dkps.append(dkb[:WIN])
            dkcs.append(dkb[WIN:])
            dvps.append(dvb[:WIN])
            dvcs.append(dvb[WIN:])
        dq_ref[...] = jnp.concatenate(dqs, axis=1)
        dkp_ref[...] = jnp.concatenate(dkps, axis=1)
        dkc_ref[...] = jnp.concatenate(dkcs, axis=1)
        dvp_ref[...] = jnp.concatenate(dvps, axis=1)
        dvc_ref[...] = jnp.concatenate(dvcs, axis=1)
        ds_ref[...] += dsink

    w = KVH * AH
    blk = lambda width: pl.BlockSpec((WIN, width), lambda n: (n, 0))
    return _call(
        body, [q, k, k, v, v, sinks, dout], comm=comm, name="attn_bwd", grid=(NBLK,),
        in_specs=[blk(D)] + _kv_specs() + [pl.BlockSpec((1, QH), lambda n: (0, 0)), blk(D)],
        out_specs=[blk(D), blk(w), blk(w), blk(w), blk(w), pl.BlockSpec((1, QH), lambda n: (0, 0))],
        out_shape=[jax.ShapeDtypeStruct((S, D), f32)] + [jax.ShapeDtypeStruct((S, w), f32)] * 4 + [jax.ShapeDtypeStruct((1, QH), f32)],
        sem=("arbitrary",))


def kv_bwd(kv, pos, inv_freq, k_norm, dkp, dkc, dvp, dvc):
    w = KVH * AH

    def body(kv_ref, pos_ref, if_ref, g_ref, dkp_ref, dkc_ref, dvp_ref, dvc_ref, o_ref, dg_ref, db_ref):
        n = pl.program_id(0)

        @pl.when(n == 0)
        def _():
            dg_ref[...] = jnp.zeros_like(dg_ref)
            db_ref[...] = jnp.zeros_like(db_ref)

        inside = (n < NBLK - 1).astype(f32)
        dk = dkc_ref[...] + inside * dkp_ref[...]
        dv = dvc_ref[...] + inside * dvp_ref[...]
        cos, sin = _rope_tables(pos_ref[...], if_ref[...])
        dkpre, dg = _headnorm_rope_bwd(kv_ref[...], g_ref[...], cos, sin, dk, KVH)
        dkv = jnp.concatenate([dkpre, dv], axis=1)
        o_ref[...] = dkv.astype(o_ref.dtype)
        dg_ref[...] += dg
        db_ref[...] += jnp.sum(dkv, axis=0, keepdims=True)

    nxt = lambda n: (jnp.minimum(n + 1, NBLK - 1), 0)
    cur = lambda n: (n, 0)
    const = lambda n: (0, 0)
    return pl.pallas_call(
        body, name="kv_bwd", grid=(NBLK,),
        in_specs=[pl.BlockSpec((WIN, w), cur), pl.BlockSpec((WIN, 1), cur), pl.BlockSpec((1, AH // 2), const),
                  pl.BlockSpec((1, AH), const), pl.BlockSpec((WIN, w), nxt), pl.BlockSpec((WIN, w), cur),
                  pl.BlockSpec((WIN, w), nxt), pl.BlockSpec((WIN, w), cur)],
        out_specs=[pl.BlockSpec((WIN, 2 * w), cur), pl.BlockSpec((1, AH), const), pl.BlockSpec((1, 2 * w), const)],
        out_shape=[jax.ShapeDtypeStruct((S, 2 * w), MXU), jax.ShapeDtypeStruct((1, AH), f32), jax.ShapeDtypeStruct((1, 2 * w), f32)],
        compiler_params=_cparams(("arbitrary",)),
    )(kv, pos, inv_freq, k_norm, dkp, dkc, dvp, dvc)


def _adam_math(w, g, m, v):
    m = ADAM_B1 * m + (1.0 - ADAM_B1) * g
    v = ADAM_B2 * v + (1.0 - ADAM_B2) * (g * g)
    m_hat = m / (1.0 - ADAM_B1 ** ADAM_STEP)
    v_hat = v / (1.0 - ADAM_B2 ** ADAM_STEP)
    return -ADAM_LR * (m_hat / (jnp.sqrt(v_hat) + ADAM_EPS) + ADAM_WD * w), m, v


def adamw(name, w, g, m, v):
    R, C = w.shape
    tr = _pick(R, (256, 128, 64, 32, 16, 8))
    tc = C if tr < R or C % 256 else 256

    def body(w_ref, g_ref, m_ref, v_ref, d_ref, nm_ref, nv_ref):
        d_ref[...], nm_ref[...], nv_ref[...] = _adam_math(w_ref[...], g_ref[...], m_ref[...], v_ref[...])

    spec = pl.BlockSpec((tr, tc), lambda i, j: (i, j))
    return pl.pallas_call(
        body, name=name, grid=(R // tr, C // tc), in_specs=[spec] * 4, out_specs=[spec] * 3,
        out_shape=[jax.ShapeDtypeStruct((R, C), f32)] * 3, compiler_params=_cparams(("parallel", "parallel")),
    )(w, g, m, v)


def _me():
    return lax.axis_index("x"), lax.axis_index("y"), lax.axis_index("c")


def gather_comm(xs):
    n = len(xs)

    def parts(x_refs, o_refs, sems):
        send_sems, recv_sems, local_sems = sems
        x, y, c = _me()
        me, sibling = (x, y, c), (x, y, 1 - c)
        chips = [(1 - x, y), (x, 1 - y), (1 - x, 1 - y)]

        def copy(a, k, block, to, src=None):
            dst = o_refs[a].at[4 * block[0] + 2 * block[1] + block[2]]
            return pltpu.make_async_remote_copy(
                src_ref=dst if src is None else src, dst_ref=dst,
                send_sem=send_sems.at[7 * a + k], recv_sem=recv_sems.at[7 * a + k], device_id=to, device_id_type=MESH)

        mine = [pltpu.make_async_copy(x_refs[a], o_refs[a].at[4 * x + 2 * y + c], local_sems.at[a]) for a in range(n)]
        first = []
        for a in range(n):
            first.append(copy(a, 0, me, sibling, src=x_refs[a]))
            first += [copy(a, 1 + j, me, (*chip, c), src=x_refs[a]) for j, chip in enumerate(chips)]
        return copy, mine, first, me, sibling, chips, c

    def start(x_refs, o_refs, sems):
        _, mine, first, *_ = parts(x_refs, o_refs, sems)
        for cp in mine + first:
            cp.start()

    def finish(x_refs, o_refs, sems):
        copy, mine, first, me, sibling, chips, c = parts(x_refs, o_refs, sems)
        passed = []
        for j, chip in enumerate(chips):
            for a in range(n):
                copy(a, 1 + j, (*chip, c), me).wait_recv()
                cp = copy(a, 4 + j, (*chip, c), sibling)
                cp.start()
                passed.append(cp)
        for a in range(n):
            copy(a, 0, sibling, me).wait_recv()
            for j, chip in enumerate(chips):
                copy(a, 4 + j, (*chip, 1 - c), me).wait_recv()
        for cp in first + passed:
            cp.wait_send()
        for cp in mine:
            cp.wait()

    return Comm(xs, [jax.ShapeDtypeStruct((N_DEV,) + a.shape, a.dtype) for a in xs],
                [pltpu.SemaphoreType.DMA((7 * n,)), pltpu.SemaphoreType.DMA((7 * n,)), pltpu.SemaphoreType.DMA((n,))], start, finish)


def run_comm(name, comm):
    _call(lambda: None, [], name=name, grid=(1,), in_specs=[], out_specs=[], out_shape=[], comm=comm)
    return comm.results


def sibling_comm(gs):
    n = len(gs)

    def copies(g_refs, o_refs, sems):
        x, y, c = _me()
        return [pltpu.make_async_remote_copy(
            src_ref=g_refs[a].at[:, 1 - c], dst_ref=o_refs[a], send_sem=sems[0].at[a], recv_sem=sems[1].at[a],
            device_id=(x, y, 1 - c), device_id_type=MESH) for a in range(n)]

    def start(g_refs, o_refs, sems):
        for cp in copies(g_refs, o_refs, sems):
            cp.start()

    def finish(g_refs, o_refs, sems):
        for cp in copies(g_refs, o_refs, sems):
            cp.wait()

    return Comm(gs, [jax.ShapeDtypeStruct((4,) + g.shape[2:], g.dtype) for g in gs],
                [pltpu.SemaphoreType.DMA((n,)), pltpu.SemaphoreType.DMA((n,))], start, finish)


def chip_comm(ts):
    n = len(ts)

    def copies(t_refs, o_refs, sems):
        x, y, c = _me()
        chips = [(1 - x, y), (x, 1 - y), (1 - x, 1 - y)]
        return [pltpu.make_async_remote_copy(
            src_ref=t_refs[a].at[2 * px + py], dst_ref=o_refs[a].at[j],
            send_sem=sems[0].at[3 * a + j], recv_sem=sems[1].at[3 * a + j],
            device_id=(px, py, c), device_id_type=MESH) for j, (px, py) in enumerate(chips) for a in range(n)]

    def start(t_refs, o_refs, sems):
        for cp in copies(t_refs, o_refs, sems):
            cp.start()

    def finish(t_refs, o_refs, sems):
        for cp in copies(t_refs, o_refs, sems):
            cp.wait()

    return Comm(ts, [jax.ShapeDtypeStruct((3,) + t.shape[1:], t.dtype) for t in ts],
                [pltpu.SemaphoreType.DMA((3 * n,)), pltpu.SemaphoreType.DMA((3 * n,))], start, finish)


def _row_tile(rows):
    return _pick(rows, (512, 304, 256, 128))


def pair_add(name, g, r):
    _, _, R, C = g.shape
    tr = _row_tile(R)

    def body(c_ref, g_ref, r_ref, o_ref):
        o_ref[0] = (g_ref[0, 0].astype(f32) + r_ref[0].astype(f32)).astype(o_ref.dtype)

    return pl.pallas_call(
        body, name=name,
        grid_spec=pltpu.PrefetchScalarGridSpec(
            num_scalar_prefetch=1, grid=(4, R // tr),
            in_specs=[pl.BlockSpec((1, 1, tr, C), lambda p, i, c: (p, c[0], i, 0)),
                      pl.BlockSpec((1, tr, C), lambda p, i, c: (p, i, 0))],
            out_specs=pl.BlockSpec((1, tr, C), lambda p, i, c: (p, i, 0))),
        out_shape=jax.ShapeDtypeStruct((4, R, C), g.dtype),
        compiler_params=_cparams(("parallel", "parallel")),
    )(lax.axis_index("c").reshape(1).astype(jnp.int32), g, r)


def _sum_of_four(t_ref, r_ref):
    return ((t_ref[0].astype(f32) + r_ref[0].astype(f32)) + r_ref[1].astype(f32)) + r_ref[2].astype(f32)


def _my_chip():
    return (2 * lax.axis_index("x") + lax.axis_index("y")).reshape(1).astype(jnp.int32)


def final_adamw(name, t, r, w, m, v):
    _, R, C = t.shape
    tr = _pick(R, (256, 128, 64, 32, 16))
    tc = C if tr < R or C % 256 else 256

    def body(p_ref, t_ref, r_ref, w_ref, m_ref, v_ref, g_ref, d_ref, nm_ref, nv_ref):
        g_ = _sum_of_four(t_ref, r_ref)
        g_ref[...] = g_
        d_ref[...], nm_ref[...], nv_ref[...] = _adam_math(w_ref[...], g_, m_ref[...], v_ref[...])

    flat = pl.BlockSpec((tr, tc), lambda i, j, p: (i, j))
    return pl.pallas_call(
        body, name=name,
        grid_spec=pltpu.PrefetchScalarGridSpec(
            num_scalar_prefetch=1, grid=(R // tr, C // tc),
            in_specs=[pl.BlockSpec((1, tr, tc), lambda i, j, p: (p[0], i, j)),
                      pl.BlockSpec((3, tr, tc), lambda i, j, p: (0, i, j)), flat, flat, flat],
            out_specs=[flat] * 4),
        out_shape=[jax.ShapeDtypeStruct((R, C), f32)] * 4,
        compiler_params=_cparams(("parallel", "parallel")),
    )(_my_chip(), t, r, w, m, v)


def final_adamw_layers(name, t0, r0, t1, r1, w, m, v):
    _, R, C = t0.shape
    tr = _pick(R, (256, 128, 64, 32, 16))

    def body(p_ref, t0_ref, r0_ref, t1_ref, r1_ref, w_ref, m_ref, v_ref, g_ref, d_ref, nm_ref, nv_ref):
        g_ = jnp.where(pl.program_id(0) == 0, _sum_of_four(t0_ref, r0_ref), _sum_of_four(t1_ref, r1_ref))
        g_ref[0] = g_
        d_ref[0], nm_ref[0], nv_ref[0] = _adam_math(w_ref[0], g_, m_ref[0], v_ref[0])

    mine = pl.BlockSpec((1, tr, C), lambda l, i, p: (p[0], i, 0))
    theirs = pl.BlockSpec((3, tr, C), lambda l, i, p: (0, i, 0))
    layer = pl.BlockSpec((1, tr, C), lambda l, i, p: (l, i, 0))
    return pl.pallas_call(
        body, name=name,
        grid_spec=pltpu.PrefetchScalarGridSpec(
            num_scalar_prefetch=1, grid=(2, R // tr),
            in_specs=[mine, theirs, mine, theirs, layer, layer, layer], out_specs=[layer] * 4),
        out_shape=[jax.ShapeDtypeStruct((2, R, C), f32)] * 4,
        compiler_params=_cparams(("parallel", "parallel")),
    )(_my_chip(), t0, r0, t1, r1, w, m, v)


class ReduceScatter:
    def __init__(self, tag, keys, grads):
        self.tag, self.keys, self.grads = tag, keys, grads
        self.send = [g.reshape((4, 2, g.shape[0] // N_DEV) + g.shape[1:]) for g in grads]

    def sibling(self):
        self.c1 = sibling_comm(self.send)
        return self.c1

    def chips(self):
        self.pairs = [pair_add(f"rs_pair_add_{self.tag}{i}", g, r) for i, (g, r) in enumerate(zip(self.send, self.c1.results))]
        self.c2 = chip_comm(self.pairs)
        return self.c2

    def parts(self):
        return {k: (t, r) for k, t, r in zip(self.keys, self.pairs, self.c2.results)}


IN_ROWS = {"z": (0, 2048), "xs": (2048, 4096), "B": (4096, 5120), "C": (5120, 6144)}
IN_COLS = 2 * INNER + 2 * GROUPS * NSTATE + HEADS


def sum_devices(g):
    def body(g_ref, o_ref):
        acc = g_ref[0]
        for i in range(1, N_DEV):
            acc = acc + g_ref[i]
        o_ref[...] = acc

    return pl.pallas_call(body, name="sum_devices", out_shape=jax.ShapeDtypeStruct(g.shape[1:], f32),
                          compiler_params=_cparams())(g)


def _pack(parts, unit, dtype, lead=()):
    flat = jnp.concatenate([p.reshape(lead + (-1,)).astype(dtype) for p in parts], axis=-1)
    n = flat.shape[-1]
    rows = -(-n // (unit * PACK_W)) * unit
    flat = jnp.pad(flat, [(0, 0)] * len(lead) + [(0, rows * PACK_W - n)])
    return flat.reshape(lead + (rows, PACK_W))


def _unpack(buf, shapes, lead=()):
    flat = buf.reshape(lead + (-1,))
    out, off = [], 0
    for shp in shapes:
        n = math.prod(shp)
        out.append(flat[..., off:off + n].reshape(lead + tuple(shp)))
        off += n
    return out


def _pad_lanes(a):
    return jnp.pad(a, [(0, 0)] * (a.ndim - 1) + [(0, LANES - a.shape[-1])])


_NN = (((1,), (0,)), ((), ()))
_NT = (((1,), (1,)), ((), ()))


def _mm(a, b, dims):
    return lax.dot_general(a.astype(MXU), b.astype(MXU), dims, preferred_element_type=f32)


def _ffn_fwd(tag, x, norm_g, w_inT, conv_w, conv_b, mid_comm=None):
    (h,) = rowwise(f"{tag}_norm", lambda x_, g_: (_rms_fwd(x_, g_),), [x], [norm_g], [(D, MXU)])
    ct, nblk = FFN_CT, FFN // FFN_CT

    def body(h_ref, wg_ref, wv_ref, cw_ref, cb_ref, gp_ref, v_ref, a_ref):
        h_ = h_ref[...]
        gp, v_ = _mm(h_, wg_ref[...], _NT), _mm(h_, wv_ref[...], _NT)
        gp_ref[...] = gp
        v_ref[...] = v_
        a_ref[...] = (_silu(_conv(gp, cw_ref[...], cb_ref[...])) * v_).astype(a_ref.dtype)

    col = pl.BlockSpec((S, ct), lambda j: (0, j))
    gate_pre, val, act = _call(
        body, [h, w_inT, w_inT, conv_w, conv_b], comm=mid_comm, name=f"{tag}_in", grid=(nblk,),
        in_specs=[pl.BlockSpec((S, D), lambda j: (0, 0)), pl.BlockSpec((ct, D), lambda j: (j, 0)),
                  pl.BlockSpec((ct, D), lambda j: (nblk + j, 0)), pl.BlockSpec((CONV_F, ct), lambda j: (0, j)),
                  pl.BlockSpec((1, ct), lambda j: (0, j))],
        out_specs=[col, col, col],
        out_shape=[jax.ShapeDtypeStruct((S, FFN), f32), jax.ShapeDtypeStruct((S, FFN), f32), jax.ShapeDtypeStruct((S, FFN), MXU)],
        sem=("parallel",))
    return act, (x, h, gate_pre, val, act)


FFN_CT = 256
CONV_CT = 256


def _proj_conv(name, h, wT, row0, cw, cb, comm=None):
    C, ct = cw.shape[1], CONV_CT

    def body(h_ref, w_ref, cw_ref, cb_ref, p_ref, c_ref):
        p = _mm(h_ref[...], w_ref[...], _NT)
        p_ref[...] = p
        c_ref[...] = _silu(_conv(p, cw_ref[...], cb_ref[...]))

    col = pl.BlockSpec((S, ct), lambda j: (0, j))
    return _call(
        body, [h, wT, cw, cb], comm=comm, name=name, grid=(C // ct,),
        in_specs=[pl.BlockSpec((S, D), lambda j: (0, 0)), pl.BlockSpec((ct, D), lambda j: (row0 // ct + j, 0)),
                  pl.BlockSpec((CONV_A, ct), lambda j: (0, j)), pl.BlockSpec((1, ct), lambda j: (0, j))],
        out_specs=[col, col], out_shape=[jax.ShapeDtypeStruct((S, C), f32)] * 2, sem=("parallel",))


def _dconv_wgrad(name, pre, dconv, cw, cb, h):
    C, ct = cw.shape[1], CONV_CT

    def body(p_ref, do_ref, cw_ref, cb_ref, h_ref, dp_ref, g_ref, dw_ref, db_ref):
        p_, w_ = p_ref[...], cw_ref[...]
        taps = _taps(p_, CONV_A)
        dx, dw, db = _conv_bwd(p_, w_, do_ref[...] * _dsilu(_conv(p_, w_, cb_ref[...], taps)), taps)
        dpm = dx.astype(MXU)
        dp_ref[...] = dpm
        g_ref[...] = lax.dot_general(dpm, h_ref[...].astype(MXU), (((0,), (0,)), ((), ())),
                                     preferred_element_type=f32).astype(g_ref.dtype)
        dw_ref[...] = dw
        db_ref[...] = db

    col = pl.BlockSpec((S, ct), lambda j: (0, j))
    return _call(
        body, [pre, dconv, cw, cb, h], name=name, grid=(C // ct,),
        in_specs=[col, col, pl.BlockSpec((CONV_A, ct), lambda j: (0, j)), pl.BlockSpec((1, ct), lambda j: (0, j)),
                  pl.BlockSpec((S, D), lambda j: (0, 0))],
        out_specs=[col, pl.BlockSpec((ct, D), lambda j: (j, 0)), pl.BlockSpec((CONV_A, ct), lambda j: (0, j)),
                   pl.BlockSpec((1, ct), lambda j: (0, j))],
        out_shape=[jax.ShapeDtypeStruct((S, C), MXU), jax.ShapeDtypeStruct((C, D), MXU),
                   jax.ShapeDtypeStruct((CONV_A, C), f32), jax.ShapeDtypeStruct((1, C), f32)],
        sem=("parallel",))


def _ffn_mid_bwd(name, dout, w_down, gate_pre, val, conv_w, conv_b, comm=None):
    ct = FFN_CT

    def body(do_ref, wd_ref, gp_ref, v_ref, w_ref, b_ref, dgv_ref, dw_ref, db_ref, dob_s):
        @pl.when(pl.program_id(0) == 0)
        def _():
            dob_s[...] = do_ref[...].astype(MXU)

        da = _mm(dob_s[...], wd_ref[...], _NT)
        gp, v_, w_ = gp_ref[...], v_ref[...], w_ref[...]
        taps = _taps(gp, CONV_F)
        gate = _conv(gp, w_, b_ref[...], taps)
        sg = _sigmoid(gate)
        dgp, dw, db = _conv_bwd(gp, w_, da * v_ * (sg * (1.0 + gate * (1.0 - sg))), taps)
        dgv_ref[0] = dgp.astype(dgv_ref.dtype)
        dgv_ref[1] = (da * (gate * sg)).astype(dgv_ref.dtype)
        dw_ref[...] = dw
        db_ref[...] = db

    col = pl.BlockSpec((S, ct), lambda j: (0, j))
    return _call(
        body, [dout, w_down, gate_pre, val, conv_w, conv_b], comm=comm, name=name, grid=(FFN // ct,),
        in_specs=[pl.BlockSpec((S, D), lambda j: (0, 0)), pl.BlockSpec((ct, D), lambda j: (j, 0)), col, col,
                  pl.BlockSpec((CONV_F, ct), lambda j: (0, j)), pl.BlockSpec((1, ct), lambda j: (0, j))],
        out_specs=[pl.BlockSpec((2, S, ct), lambda j: (0, 0, j)), pl.BlockSpec((CONV_F, ct), lambda j: (0, j)),
                   pl.BlockSpec((1, ct), lambda j: (0, j))],
        out_shape=[jax.ShapeDtypeStruct((2, S, FFN), MXU), jax.ShapeDtypeStruct((CONV_F, FFN), f32), jax.ShapeDtypeStruct((1, FFN), f32)],
        scratch=[pltpu.VMEM((S, D), MXU)], sem=("arbitrary",))


def _ffn_bwd(tag, layer, saved, norm_g, w_inT, conv_w, conv_b, w_down, dout, mid_comm=None):
    x, h, gate_pre, val, act = saved
    g_down = matmul(f"{tag}_wdown", act, dout, "tn", out_dtype=MXU)
    dgv, g_cw, g_cb = _ffn_mid_bwd(f"{tag}_dmid", dout, w_down, gate_pre, val, conv_w, conv_b, comm=mid_comm)
    g_inT = matmul_tn_stacked(f"{tag}_win", dgv, h, MXU)

    def din_fn(dg_, dv_, x_, do_, g_, wT):
        dx, dg = _rms_bwd(x_, g_, _mm(dg_, wT[:FFN], _NN) + _mm(dv_, wT[FFN:], _NN))
        return do_ + dx, dg

    rs = ReduceScatter(tag, (f"f_inT{layer}", f"f_down{layer}"), [g_inT, g_down])
    dx, g_norm = rowwise(f"{tag}_din", din_fn, [(dgv, 0), (dgv, 1), x, dout], [norm_g, w_inT], [(D, f32)], [(1, D)],
                         comm=rs.sibling())
    return dx, {f"f_norm{layer}": g_norm, f"f_conv_w{layer}": g_cw, f"f_conv_b{layer}": g_cb}, rs


def _land(W, keys, comm):
    for k, g in zip(keys, comm.results):
        W[k] = g.reshape(-1, g.shape[2])


def _local_step(x, pos, tgt, W, shards):
    G = {}
    gather = lambda *keys: gather_comm([shards[k] for k in keys])
    inv_freq = (ROPE_THETA ** (-jnp.arange(AH // 2, dtype=f32) / (AH // 2))).reshape(1, AH // 2)

    def in_fn(x_, g_, wT, wdtT):
        h_ = _rms_fwd(x_, g_).astype(MXU)
        return h_, _mm(h_, wT[slice(*IN_ROWS["z"])], _NT), _mm(h_, wdtT, _NT)

    h0, z, dt_pre = rowwise("a_in", in_fn, [x], [W["a_norm"], W["inT"], W["in_dtT"]], [(D, MXU), (INNER, f32), (LANES, f32)])
    pre, conv = {}, {}
    early = {"xs": ("a_out",), "B": (), "C": ()}
    for k in ("xs", "B", "C"):
        c = gather(*early[k]) if early[k] else None
        pre[k], conv[k] = _proj_conv(f"a_in_{k}", h0, W["inT"], IN_ROWS[k][0], W[f"cw_{k}"], W[f"cb_{k}"], comm=c)
        if c is not None:
            _land(W, early[k], c)
    c = gather("f_inT0", "f_down0")
    y, states = ssd_fwd(conv["xs"], conv["B"], conv["C"], dt_pre, W["dt_bias"], W["A_log"], W["D"], comm=c)
    _land(W, ("f_inT0", "f_down0"), c)

    def gate_norm(y_, z_, g_):
        yg = y_ * _silu(z_)
        w = INNER // GROUPS
        return (jnp.concatenate([_rms_fwd(yg[:, w * i:w * (i + 1)], g_[:, w * i:w * (i + 1)]) for i in range(GROUPS)], axis=1),)

    def out_fn(y_, z_, x_, g_, w_):
        (gn_,) = gate_norm(y_, z_, g_)
        gn_ = gn_.astype(MXU)
        return gn_, x_ + _mm(gn_, w_, _NN)

    c = gather("f_down1")
    gn, x1 = rowwise("a_out", out_fn, [y, z, x], [W["a_gnorm"], W["a_out"]], [(INNER, MXU), (D, f32)], comm=c)
    _land(W, ("f_down1",), c)

    c = gather("w_kv", "w_q", "w_o")
    act0, ffn0 = _ffn_fwd("f0", x1, W["f_norm0"], W["f_inT0"], W["f_cw0"], W["f_cb0"], mid_comm=c)
    _land(W, ("w_kv", "w_q", "w_o"), c)
    x2 = matmul("f0_down", act0, W["f_down0"], "nn", residual=x1)

    def qkv_fn(x_, gk, gb, wkv, bkv, wq, bq):
        kvn_, h2_ = _rms_fwd(x_, gk).astype(MXU), _rms_fwd(x_, gb).astype(MXU)
        return kvn_, h2_, _mm(kvn_, wkv, _NN) + bkv, _mm(h2_, wq, _NN) + bq

    kw = KVH * AH
    kvn, h2, kv, q_pre = rowwise("qkv_proj", qkv_fn, [x2], [W["kv_norm"], W["b_norm"], W["w_kv"], W["b_kv"], W["w_q"], W["b_q"]],
                                 [(D, MXU), (D, MXU), (2 * kw, f32), (D, f32)])

    def k_fwd(kv_, pos_, if_, g_):
        cos, sin = _rope_tables(pos_, if_)
        return _headnorm_rope_fwd(kv_[:, :kw], g_, cos, sin, KVH), kv_[:, kw:]

    k_rot, v_val = rowwise("k_rope", k_fwd, [kv, pos], [inv_freq, W["k_norm"]], [(kw, f32), (kw, f32)])

    def q_fwd(q_, pos_, if_, g_):
        cos, sin = _rope_tables(pos_, if_)
        return (_headnorm_rope_fwd(q_, g_, cos, sin, QH),)

    (q,) = rowwise("q_rope", q_fwd, [q_pre, pos], [inv_freq, W["q_norm"]], [(D, f32)])
    c = gather("f_inT1")
    att = attn_fwd(q, k_rot, v_val, W["sinks"], comm=c)
    _land(W, ("f_inT1",), c)
    x3 = matmul("o_proj", att, W["w_o"], "nn", bias=W["b_o"], residual=x2)

    act1, ffn1 = _ffn_fwd("f1", x3, W["f_norm1"], W["f_inT1"], W["f_cw1"], W["f_cb1"])

    def loss_fn(a_, x_, t_, w_):
        diff = x_ + _mm(a_, w_, _NN) - t_
        rows = jnp.sum(diff * diff, axis=1, keepdims=True) * (0.5 / D)
        return diff * (1.0 / D), jnp.sum(rows, axis=0, keepdims=True)

    dx4, loss = rowwise("f1_down_loss", loss_fn, [act1, x3, tgt], [W["f_down1"]], [(D, f32)], [(1, 1)])

    dx3, g, rs_f1 = _ffn_bwd("f1", 1, ffn1, W["f_norm1"], W["f_inT1"], W["f_cw1"], W["f_cb1"], W["f_down1"], dx4)
    G.update(g)

    datt = matmul("o_dproj", dx3, W["w_o"], "nt")
    g_wo = matmul("o_wproj", att, dx3, "tn", out_dtype=MXU)
    dq, dkp, dkc, dvp, dvc, G["sinks"] = attn_bwd(q, k_rot, v_val, W["sinks"], datt, comm=rs_f1.chips())

    def q_bwd(q_, pos_, dq_, dx_, if_, g_):
        cos, sin = _rope_tables(pos_, if_)
        dqp, dg = _headnorm_rope_bwd(q_, g_, cos, sin, dq_, QH)
        return dqp, dg, jnp.sum(dqp, axis=0, keepdims=True), jnp.sum(dx_, axis=0, keepdims=True)

    dq_pre, G["q_norm"], G["b_q"], G["b_o"] = rowwise("q_drope", q_bwd, [q_pre, pos, dq, dx3], [inv_freq, W["q_norm"]],
                                                      [(D, MXU)], [(1, AH), (1, D), (1, D)])
    g_wq = matmul("q_wproj", h2, dq_pre, "tn", out_dtype=MXU)
    dkv, G["k_norm"], G["b_kv"] = kv_bwd(kv, pos, inv_freq, W["k_norm"], dkp, dkc, dvp, dvc)
    g_wkv = matmul("kv_wproj", kvn, dkv, "tn", out_dtype=MXU)
    rs_att = ReduceScatter("att", ("w_kv", "w_q", "w_o"), [g_wkv, g_wq, g_wo])

    def x2_bwd(x_, dq_, dkv_, dx_, gb_, gk_, wq, wkv):
        d1, dgb = _rms_bwd(x_, gb_, _mm(dq_, wq, _NT))
        d2, dgk = _rms_bwd(x_, gk_, _mm(dkv_, wkv, _NT))
        return dx_ + d1 + d2, dgb, dgk

    dx2, G["b_norm"], G["kv_norm"] = rowwise("qkv_dproj", x2_bwd, [x2, dq_pre, dkv, dx3],
                                             [W["b_norm"], W["kv_norm"], W["w_q"], W["w_kv"]],
                                             [(D, f32)], [(1, D), (1, D)], comm=rs_att.sibling())

    dx1, g, rs_f0 = _ffn_bwd("f0", 0, ffn0, W["f_norm0"], W["f_inT0"], W["f_cw0"], W["f_cb0"], W["f_down0"], dx2,
                             mid_comm=rs_att.chips())
    G.update(g)

    rs_out = ReduceScatter("a_out", ("a_out",), [matmul("a_wout", gn, dx1, "tn", out_dtype=MXU)])

    def gate_norm_bwd(y_, z_, dx_, g_, w_out):
        dgn_ = _mm(dx_, w_out, _NT)
        w = INNER // GROUPS
        sg = _sigmoid(z_)
        sz = z_ * sg
        yg = y_ * sz
        parts, dgs = [], []
        for i in range(GROUPS):
            dseg, dg = _rms_bwd(yg[:, w * i:w * (i + 1)], g_[:, w * i:w * (i + 1)], dgn_[:, w * i:w * (i + 1)])
            parts.append(dseg)
            dgs.append(dg)
        dyg = jnp.concatenate(parts, axis=1)
        return dyg * sz, dyg * y_ * (sg * (1.0 + z_ * (1.0 - sg))), jnp.concatenate(dgs, axis=1)

    dy, dz, G["a_gnorm"] = rowwise("a_dout", gate_norm_bwd, [y, z, dx1], [W["a_gnorm"], W["a_out"]],
                                   [(INNER, f32), (INNER, MXU)], [(1, INNER)], comm=rs_out.sibling())
    dconv = {}
    dconv["xs"], dconv["B"], dconv["C"], ddt_pre, G["dt_bias"], G["A_log"], G["D"] = ssd_bwd(
        conv["xs"], conv["B"], conv["C"], dt_pre, W["dt_bias"], W["A_log"], W["D"], states, dy,
        comm=merge_comms([rs_f0.chips(), rs_out.chips()]))

    g_in, dpre = [matmul("a_win_z", dz, h0, "tn", out_dtype=MXU)], {}
    for k in ("xs", "B", "C"):
        dpre[k], g_k, G[f"cw_{k}"], G[f"cb_{k}"] = _dconv_wgrad(f"a_dconv_{k}", pre[k], dconv[k], W[f"cw_{k}"], W[f"cb_{k}"], h0)
        g_in.append(g_k)
    g_in.append(matmul("a_win_dt", ddt_pre, h0, "tn", out_dtype=MXU)[:HEADS])
    rs_in = ReduceScatter("a_in", ("inT",), [jnp.concatenate(g_in, axis=0)])
    run_comm("rs_in_sibling", rs_in.sibling())

    def x0_bwd(dz_, dxs_, db_, dc_, ddt_, x_, do_, g_, wT, wdtT):
        parts = zip((dz_, dxs_, db_, dc_), IN_ROWS.values())
        dh = sum(_mm(d_, wT[a:b], _NN) for d_, (a, b) in parts) + _mm(ddt_, wdtT, _NN)
        dx, dg = _rms_bwd(x_, g_, dh)
        return do_ + dx, dg

    dx, G["a_norm"] = rowwise("a_din", x0_bwd, [dz, dpre["xs"], dpre["B"], dpre["C"], ddt_pre, x, dx1],
                              [W["a_norm"], W["inT"], W["in_dtT"]], [(D, f32)], [(1, D)], comm=rs_in.chips())
    return loss, dx, G, [rs_f1, rs_att, rs_f0, rs_out, rs_in]


ROW_KEYS = ("inT", "a_out", "f_inT0", "f_down0", "w_kv", "w_q", "w_o", "f_inT1", "f_down1")


def _row_blocks(src):
    return {"inT": src["a_in_proj"][0].T, "a_out": src["a_out_proj"][0], "w_kv": src["w_kv"], "w_q": src["w_q"][0],
            "w_o": src["w_o"][0], "f_inT0": src["f_w_in"][0].T, "f_inT1": src["f_w_in"][1].T,
            "f_down0": src["f_w_down"][0], "f_down1": src["f_w_down"][1]}


def _from_row_blocks(rb):
    out = {"a_in_proj": rb["inT"].T[None], "a_out_proj": rb["a_out"][None], "w_kv": rb["w_kv"], "w_q": rb["w_q"][None],
           "w_o": rb["w_o"][None]}
    if "f_inT0" in rb:
        out["f_w_in"] = jnp.stack([rb["f_inT0"].T, rb["f_inT1"].T])
        out["f_w_down"] = jnp.stack([rb["f_down0"], rb["f_down1"]])
    return out


SMALL_SHARDED = ("a_norm", "a_conv_w", "a_conv_b", "a_gnorm", "f_conv_w")
REPLICATED = ("a_dt_bias", "a_A_log", "a_D", "kv_norm", "b_kv", "k_norm", "b_norm", "b_q", "q_norm", "sinks", "b_o",
              "f_norm", "f_conv_b")
ORDER = ("a_norm", "a_in_proj", "a_conv_w", "a_conv_b", "a_dt_bias", "a_A_log", "a_D", "a_gnorm", "a_out_proj", "kv_norm",
         "w_kv", "b_kv", "k_norm", "b_norm", "w_q", "b_q", "q_norm", "sinks", "w_o", "b_o", "f_norm", "f_w_in",
         "f_conv_w", "f_conv_b", "f_w_down")


def _gathered_to_whole(name, g):
    if name == "a_conv_w":
        return jnp.moveaxis(g[:, 0], 0, 1).reshape(g.shape[2], -1)
    if name in ("a_norm", "a_conv_b", "a_gnorm"):
        return g[:, 0].reshape(1, -1)
    if name == "f_conv_w":
        return jnp.moveaxis(g, 0, 2).reshape(g.shape[1], g.shape[2], -1)
    raise ValueError(name)


def _whole_to_shards(name, w):
    if name == "a_conv_w":
        return jnp.moveaxis(w.reshape(w.shape[0], N_DEV, -1), 1, 0)[:, None]
    if name in ("a_norm", "a_conv_b", "a_gnorm"):
        return w.reshape(N_DEV, 1, -1)
    if name == "f_conv_w":
        return jnp.moveaxis(w.reshape(w.shape[0], w.shape[1], N_DEV, -1), 2, 0)
    raise ValueError(name)


def _small_weights(whole):
    W = {}
    cw, cb = whole["a_conv_w"], whole["a_conv_b"]
    o = 0
    for k, n in (("xs", INNER), ("B", GROUPS * NSTATE), ("C", GROUPS * NSTATE)):
        W[f"cw_{k}"], W[f"cb_{k}"] = cw[:, o:o + n], cb[:, o:o + n]
        o += n
    W["a_norm"], W["a_gnorm"] = whole["a_norm"], whole["a_gnorm"]
    W["dt_bias"], W["A_log"], W["D"] = (_pad_lanes(whole[k]) for k in ("a_dt_bias", "a_A_log", "a_D"))
    W["kv_norm"], W["b_kv"], W["k_norm"] = whole["kv_norm"].reshape(1, -1), whole["b_kv"].reshape(1, -1), whole["k_norm"].reshape(1, -1)
    for k in ("b_norm", "b_q", "q_norm", "sinks", "b_o"):
        W[k] = whole[k]
    for i in range(2):
        W[f"f_norm{i}"] = whole["f_norm"][i:i + 1]
        W[f"f_cw{i}"], W[f"f_cb{i}"] = whole["f_conv_w"][i], whole["f_conv_b"][i:i + 1]
    return W


def _small_grads(G, shapes):
    nh = HEADS
    out = {
        "a_conv_w": jnp.concatenate([G["cw_xs"], G["cw_B"], G["cw_C"]], axis=1),
        "a_conv_b": jnp.concatenate([G["cb_xs"], G["cb_B"], G["cb_C"]], axis=1),
        "a_norm": G["a_norm"], "a_gnorm": G["a_gnorm"],
        "a_dt_bias": G["dt_bias"][:, :nh], "a_A_log": G["A_log"][:, :nh], "a_D": G["D"][:, :nh],
        "kv_norm": G["kv_norm"], "b_kv": G["b_kv"], "k_norm": G["k_norm"], "b_norm": G["b_norm"],
        "b_q": G["b_q"], "q_norm": G["q_norm"], "sinks": G["sinks"], "b_o": G["b_o"],
        "f_norm": jnp.concatenate([G["f_norm0"], G["f_norm1"]], axis=0),
        "f_conv_w": jnp.stack([G["f_conv_w0"], G["f_conv_w1"]]),
        "f_conv_b": jnp.concatenate([G["f_conv_b0"], G["f_conv_b1"]], axis=0),
    }
    return {k: val.reshape(shapes[k]) if k in shapes else val for k, val in out.items()}


def kernel(x, positions, a_norm, a_in_proj, a_conv_w, a_conv_b, a_dt_bias, a_A_log, a_D, a_gnorm, a_out_proj, kv_norm, w_kv, b_kv, k_norm, b_norm, w_q, b_q, q_norm, sinks, w_o, b_o, f_norm, f_w_in, f_conv_w, f_conv_b, f_w_down, loss_target, m_a_norm, m_a_in_proj, m_a_conv_w, m_a_conv_b, m_a_dt_bias, m_a_A_log, m_a_D, m_a_gnorm, m_a_out_proj, m_kv_norm, m_w_kv, m_b_kv, m_k_norm, m_b_norm, m_w_q, m_b_q, m_q_norm, m_sinks, m_w_o, m_b_o, m_f_norm, m_f_w_in, m_f_conv_w, m_f_conv_b, m_f_w_down, v_a_norm, v_a_in_proj, v_a_conv_w, v_a_conv_b, v_a_dt_bias, v_a_A_log, v_a_D, v_a_gnorm, v_a_out_proj, v_kv_norm, v_w_kv, v_b_kv, v_k_norm, v_b_norm, v_w_q, v_b_q, v_q_norm, v_sinks, v_w_o, v_b_o, v_f_norm, v_f_w_in, v_f_conv_w, v_f_conv_b, v_f_w_down):
    given = dict(locals())
    w_in = {n: given[n] for n in ORDER}
    m_in = {n: given["m_" + n] for n in ORDER}
    v_in = {n: given["v_" + n] for n in ORDER}
    dev = 4 * lax.axis_index("x") + 2 * lax.axis_index("y") + lax.axis_index("c")

    w2, m2, v2 = _row_blocks(w_in), _row_blocks(m_in), _row_blocks(v_in)
    small_pack = _pack([w_in[n] for n in SMALL_SHARDED], 8, f32)
    shards = {k: w2[k].astype(MXU) for k in ROW_KEYS}
    in_all, small_all = run_comm("ag_head", gather_comm([shards["inT"], small_pack]))
    whole = {n: w_in[n] for n in REPLICATED}
    for n, g in zip(SMALL_SHARDED, _unpack(small_all, [w_in[n].shape for n in SMALL_SHARDED], lead=(N_DEV,))):
        whole[n] = _gathered_to_whole(n, g)
    W = _small_weights(whole)
    W["inT"] = in_all.reshape(-1, D)
    W["in_dtT"] = jnp.pad(W["inT"][IN_COLS - HEADS:], ((0, LANES - HEADS), (0, 0)))

    loss, dx, G, scatters = _local_step(x[0], positions.reshape(S, 1).astype(f32), loss_target[0], W, shards)
    grads = _small_grads(G, {n: whole[n].shape for n in REPLICATED})

    small_names = SMALL_SHARDED + REPLICATED
    small_part = _pack([grads[n] for n in small_names] + [loss], 8, f32)
    small_gather = gather_comm([small_part])
    run_comm("ag_small_grads", small_gather)
    parts = {}
    for rs in scatters:
        parts.update(rs.parts())

    single = tuple(k for k in ROW_KEYS if not k.startswith("f_"))
    stepped = {k: final_adamw(f"adamw_{k}", *parts[k], w2[k], m2[k], v2[k]) for k in single}
    g_out, delta, new_m, new_v = (_from_row_blocks({k: stepped[k][i] for k in single}) for i in range(4))
    for n, key, lay in (("f_w_in", "f_inT", lambda a: jnp.swapaxes(a, 1, 2)), ("f_w_down", "f_down", lambda a: a)):
        res = final_adamw_layers(f"adamw_{n}", *parts[key + "0"], *parts[key + "1"], lay(w_in[n]), lay(m_in[n]), lay(v_in[n]))
        g_out[n], delta[n], new_m[n], new_v[n] = (lay(a) for a in res)
    *small_sums, loss_all = _unpack(sum_devices(small_gather.results[0]), [grads[n].shape for n in small_names] + [(1, 1)])
    for n, g in zip(small_names, small_sums):
        if n in SMALL_SHARDED:
            g_out[n] = lax.dynamic_index_in_dim(_whole_to_shards(n, g), dev, axis=0, keepdims=False)
        else:
            g_out[n] = g.reshape(w_in[n].shape)

    packs = [_pack([src[n] for n in small_names], 8, f32) for src in (w_in, g_out, m_in, v_in)]
    outs = adamw("adamw_small", *packs)
    for dst, buf in zip((delta, new_m, new_v), outs):
        for n, a in zip(small_names, _unpack(buf, [w_in[n].shape for n in small_names])):
            dst[n] = a

    return (loss_all[0, 0], dx[None], *[g_out[n] for n in ORDER], *[delta[n] for n in ORDER],
            *[new_m[n] for n in ORDER], *[new_v[n] for n in ORDER])
```

```python
import functools
import math

import jax
import jax.numpy as jnp
from jax import lax
from jax.experimental import pallas as pl
from jax.experimental.pallas import tpu as pltpu

f32 = jnp.float32
bf16 = jnp.bfloat16
MXU = bf16

N_DEV = 8
S = 2048
D = 1024
EPS = 1e-5
INNER = 2048
HEADS = 32
HP = 64
GROUPS = 8
HPG = HEADS // GROUPS
NSTATE = 128
CONV_A = 4
CHUNK = 256
NCHUNK = S // CHUNK
AH = 64
QH = 16
KVH = 4
QPK = QH // KVH
WIN = 128
NBLK = S // WIN
ROPE_THETA = 10000.0
FFN = 2816
CONV_F = 3
LANES = 128
PACK_W = 1024
VMEM_LIMIT = 56 * 1024 * 1024

ADAM_LR, ADAM_B1, ADAM_B2, ADAM_EPS, ADAM_WD, ADAM_STEP = 0.001, 0.9, 0.999, 1e-08, 0.01, 10

MESH = pl.DeviceIdType.MESH


def _cparams(sem=None):
    return pltpu.CompilerParams(dimension_semantics=sem, vmem_limit_bytes=VMEM_LIMIT)


def _pick(n, cands):
    for c in cands:
        if n % c == 0:
            return c
    return n


class Comm:
    def __init__(self, ins, out_shapes, sems, start, finish):
        self.ins, self.out_shapes, self.sems, self.start, self.finish = list(ins), list(out_shapes), list(sems), start, finish
        self.results, self.children = None, ()

    def set_results(self, res):
        self.results, o = list(res), 0
        for ch in self.children:
            ch.set_results(res[o:o + len(ch.out_shapes)])
            o += len(ch.out_shapes)


def merge_comms(comms):
    def each(fn_name, ins, outs, sems):
        i = o = s = 0
        for c in comms:
            getattr(c, fn_name)(ins[i:i + len(c.ins)], outs[o:o + len(c.out_shapes)], sems[s:s + len(c.sems)])
            i, o, s = i + len(c.ins), o + len(c.out_shapes), s + len(c.sems)

    merged = Comm([a for c in comms for a in c.ins], [a for c in comms for a in c.out_shapes], [a for c in comms for a in c.sems],
                  functools.partial(each, "start"), functools.partial(each, "finish"))
    merged.children = tuple(comms)
    return merged


def _call(body, args, *, name, grid, in_specs, out_specs, out_shape, scratch=(), sem=None, comm=None):
    if comm is None:
        return pl.pallas_call(body, name=name, grid=grid, in_specs=list(in_specs), out_specs=list(out_specs),
                              out_shape=list(out_shape), scratch_shapes=list(scratch), compiler_params=_cparams(sem))(*args)
    n_in, n_out, n_scr, c_in, c_out = len(in_specs), len(out_shape), len(scratch), len(comm.ins), len(comm.out_shapes)
    any_spec = pl.BlockSpec(memory_space=pl.ANY)

    def outer(*refs):
        ins, c_ins = refs[:n_in], refs[n_in:n_in + c_in]
        o = n_in + c_in
        outs, c_outs = refs[o:o + n_out], refs[o + n_out:o + n_out + c_out]
        o += n_out + c_out
        scr, c_sems = refs[o:o + n_scr], refs[o + n_scr:]
        ids = [pl.program_id(i) for i in range(len(grid))]
        first = functools.reduce(jnp.logical_and, [i == 0 for i in ids])
        last = functools.reduce(jnp.logical_and, [i == g - 1 for i, g in zip(ids, grid)])

        @pl.when(first)
        def _():
            comm.start(c_ins, c_outs, c_sems)

        body(*ins, *outs, *scr)

        @pl.when(last)
        def _():
            comm.finish(c_ins, c_outs, c_sems)

    res = pl.pallas_call(
        outer, name=name, grid=grid, in_specs=list(in_specs) + [any_spec] * c_in,
        out_specs=list(out_specs) + [any_spec] * c_out, out_shape=list(out_shape) + comm.out_shapes,
        scratch_shapes=list(scratch) + comm.sems, compiler_params=_cparams(("arbitrary",) * len(grid)),
    )(*args, *comm.ins)
    comm.set_results(res[n_out:])
    return res[:n_out]


def matmul(name, a, b, mode, out_dtype=f32, bias=None, residual=None):
    if mode == "nn":
        (M, K), (K2, N) = a.shape, b.shape
    elif mode == "nt":
        (M, K), (N, K2) = a.shape, b.shape
    else:
        (K, M), (K2, N) = a.shape, b.shape
    assert K == K2, (name, a.shape, b.shape)
    if mode == "tn":
        tm, tn = M, _pick(N, (512, 256, 128) if M <= 1024 else (256, 128))
        a_spec = pl.BlockSpec((K, M), lambda j: (0, 0))
        b_spec = pl.BlockSpec((K, tn), lambda j: (0, j))
        dims = (((0,), (0,)), ((), ()))
        grid, o_map, row_map = (N // tn,), (lambda j: (0, j)), (lambda j: (0, j))
    else:
        tm, tn = (256 if N >= 2048 else 512), N
        a_spec = pl.BlockSpec((tm, K), lambda i: (i, 0))
        b_spec = pl.BlockSpec(b.shape, lambda i: (0, 0))
        dims = (((1,), (0,)), ((), ())) if mode == "nn" else (((1,), (1,)), ((), ()))
        grid, o_map, row_map = (M // tm,), (lambda i: (i, 0)), (lambda i: (0, 0))
    ins, in_specs = [a, b], [a_spec, b_spec]
    if bias is not None:
        ins.append(bias)
        in_specs.append(pl.BlockSpec((1, tn), row_map))
    if residual is not None:
        ins.append(residual)
        in_specs.append(pl.BlockSpec((tm, tn), o_map))
    has_bias, has_res = bias is not None, residual is not None

    def body(a_ref, b_ref, *rest):
        rest = list(rest)
        bias_ref = rest.pop(0) if has_bias else None
        res_ref = rest.pop(0) if has_res else None
        (o_ref,) = rest
        r = lax.dot_general(a_ref[...].astype(MXU), b_ref[...].astype(MXU), dims, preferred_element_type=f32)
        if has_bias:
            r = r + bias_ref[...]
        if has_res:
            r = r + res_ref[...]
        o_ref[...] = r.astype(out_dtype)

    return pl.pallas_call(
        body, name=name, grid=grid, in_specs=in_specs,
        out_specs=pl.BlockSpec((tm, tn), o_map),
        out_shape=jax.ShapeDtypeStruct((M, N), out_dtype),
        compiler_params=_cparams(("parallel",)),
    )(*ins)


def matmul_tn_stacked(name, a, b, out_dtype):
    R, K, M = a.shape
    N = b.shape[1]
    tn = _pick(N, (256, 128))

    def body(a_ref, b_ref, o_ref):
        o_ref[0] = lax.dot_general(a_ref[0].astype(MXU), b_ref[...].astype(MXU), (((0,), (0,)), ((), ())),
                                   preferred_element_type=f32).astype(out_dtype)

    out = pl.pallas_call(
        body, name=name, grid=(R, N // tn),
        in_specs=[pl.BlockSpec((1, K, M), lambda r, j: (r, 0, 0)), pl.BlockSpec((K, tn), lambda r, j: (0, j))],
        out_specs=pl.BlockSpec((1, M, tn), lambda r, j: (r, 0, j)),
        out_shape=jax.ShapeDtypeStruct((R, M, N), out_dtype),
        compiler_params=_cparams(("parallel", "parallel")),
    )(a, b)
    return out.reshape(R * M, N)


def rowwise(name, fn, rows, pars, outs, accs=(), tile=256, comm=None):
    n_in, n_out = len(rows) + len(pars), len(outs)
    in_specs = [pl.BlockSpec((None, tile, r[0].shape[2]), functools.partial(lambda i, lead: (lead, i, 0), lead=r[1]))
                if isinstance(r, tuple) else pl.BlockSpec((tile, r.shape[1]), lambda i: (i, 0)) for r in rows]
    rows = [r[0] if isinstance(r, tuple) else r for r in rows]
    in_specs += [pl.BlockSpec(p.shape, lambda i: (0, 0)) for p in pars]
    out_specs = [pl.BlockSpec((tile, c), lambda i: (i, 0)) for c, _ in outs]
    out_specs += [pl.BlockSpec(shp, lambda i: (0, 0)) for shp in accs]
    out_shape = [jax.ShapeDtypeStruct((S, c), dt) for c, dt in outs]
    out_shape += [jax.ShapeDtypeStruct(shp, f32) for shp in accs]

    def body(*refs):
        res = fn(*[r[...] for r in refs[:n_in]])
        o_refs = refs[n_in:n_in + n_out]
        a_refs = refs[n_in + n_out:]
        for ref, val in zip(o_refs, res[:n_out]):
            ref[...] = val.astype(ref.dtype)
        if a_refs:
            @pl.when(pl.program_id(0) == 0)
            def _():
                for ref in a_refs:
                    ref[...] = jnp.zeros_like(ref)
            for ref, val in zip(a_refs, res[n_out:]):
                ref[...] += val

    return _call(body, [*rows, *pars], name=name, grid=(S // tile,), in_specs=in_specs, out_specs=out_specs,
                 out_shape=out_shape, sem=("arbitrary",) if accs else ("parallel",), comm=comm)


def _sigmoid(x):
    return 0.5 * jnp.tanh(0.5 * x) + 0.5


def _silu(x):
    return x * _sigmoid(x)


def _dsilu(x):
    sg = _sigmoid(x)
    return sg * (1.0 + x * (1.0 - sg))


def _softplus(x):
    return jnp.maximum(x, 0.0) + jnp.log(1.0 + jnp.exp(-jnp.abs(x)))


def _rms_fwd(x, g):
    r = lax.rsqrt(jnp.mean(x * x, axis=-1, keepdims=True) + EPS)
    return x * r * g


def _rms_bwd(x, g, dh):
    r = lax.rsqrt(jnp.mean(x * x, axis=-1, keepdims=True) + EPS)
    xh = x * r
    dxh = dh * g
    dx = r * (dxh - xh * jnp.mean(dxh * xh, axis=-1, keepdims=True))
    return dx, jnp.sum(dh * xh, axis=0, keepdims=True)


def _taps(x, width):
    row = lax.broadcasted_iota(jnp.int32, (8, x.shape[1]), 0)

    def shifted(s):
        r = pltpu.roll(x, s, 0)
        return jnp.concatenate([jnp.where(row >= s, r[:8], 0.0), r[8:]], axis=0)

    return [shifted(s) for s in range(width - 1, 0, -1)] + [x]


def _conv(x, w, b, taps=None):
    width = w.shape[0]
    taps = _taps(x, width) if taps is None else taps
    out = b + w[0:1, :] * taps[0]
    for k in range(1, width):
        out = out + w[k:k + 1, :] * taps[k]
    return out


def _conv_bwd(x, w, dc, taps=None):
    width, n = w.shape[0], x.shape[0]
    taps = _taps(x, width) if taps is None else taps
    row = lax.broadcasted_iota(jnp.int32, (8, x.shape[1]), 0)
    dx = w[width - 1:width, :] * dc
    for k in range(width - 1):
        s = width - 1 - k
        r = pltpu.roll(dc, n - s, 0)
        dx = dx + w[k:k + 1, :] * jnp.concatenate([r[:n - 8], jnp.where(row < 8 - s, r[n - 8:], 0.0)], axis=0)
    dw = jnp.concatenate([jnp.sum(dc * t, axis=0, keepdims=True) for t in taps], axis=0)
    return dx, dw, jnp.sum(dc, axis=0, keepdims=True)


def _rope_tables(pos, inv_freq):
    ang = pos * inv_freq
    return jnp.cos(ang), jnp.sin(ang)


def _split2(v):
    hi = v.astype(bf16)
    return hi, (v - hi.astype(f32)).astype(bf16)


def _head_maps(width):
    shift = AH.bit_length() - 1
    to_head = (lax.broadcasted_iota(jnp.int32, (width, LANES), 0) >> shift) == lax.broadcasted_iota(jnp.int32, (width, LANES), 1)
    from_head = lax.broadcasted_iota(jnp.int32, (LANES, width), 0) == (lax.broadcasted_iota(jnp.int32, (LANES, width), 1) >> shift)
    return to_head.astype(bf16), from_head.astype(bf16)


def _head_sums(v, to_head):
    hi, lo = _split2(v)
    return jnp.dot(hi, to_head, preferred_element_type=f32) + jnp.dot(lo, to_head, preferred_element_type=f32)


def _head_spread(s, from_head):
    hi, lo = _split2(s)
    return jnp.dot(hi, from_head, preferred_element_type=f32) + jnp.dot(lo, from_head, preferred_element_type=f32)


def _rope_full(cos, sin, width):
    half = AH // 2
    pad = jnp.zeros((cos.shape[0], LANES - half), f32)
    r = lax.broadcasted_iota(jnp.int32, (LANES, width), 0)
    lane = lax.broadcasted_iota(jnp.int32, (LANES, width), 1)
    spread = ((lane & (half - 1)) == r).astype(bf16)
    full = lambda t: _head_spread(jnp.concatenate([t, pad], axis=1), spread)
    first = (lax.broadcasted_iota(jnp.int32, (1, width), 1) & (AH - 1)) < half
    sin_f = full(sin)
    return full(cos), jnp.where(first, -sin_f, sin_f), first


def _swap_halves(v, first):
    half, width = AH // 2, v.shape[1]
    return jnp.where(first, pltpu.roll(v, width - half, 1), pltpu.roll(v, half, 1))


def _headnorm_rope_fwd(x, g, cos, sin, heads):
    to_head, from_head = _head_maps(heads * AH)
    cos_f, sin_s, first = _rope_full(cos, sin, heads * AH)
    r = _head_spread(lax.rsqrt(_head_sums(x * x, to_head) * (1.0 / AH) + EPS), from_head)
    n = x * r * jnp.tile(g, (1, heads))
    return n * cos_f + _swap_halves(n, first) * sin_s


def _headnorm_rope_bwd(x, g, cos, sin, dout, heads):
    width = heads * AH
    to_head, from_head = _head_maps(width)
    cos_f, sin_s, first = _rope_full(cos, sin, width)
    r = _head_spread(lax.rsqrt(_head_sums(x * x, to_head) * (1.0 / AH) + EPS), from_head)
    xh = x * r
    dn = dout * cos_f - _swap_halves(dout, first) * sin_s
    dxh = dn * jnp.tile(g, (1, heads))
    m = _head_spread(_head_sums(dxh * xh, to_head) * (1.0 / AH), from_head)
    dx = r * (dxh - xh * m)
    dg_lanes = jnp.sum(dn * xh, axis=0, keepdims=True)
    fold = ((lax.broadcasted_iota(jnp.int32, (width, LANES), 0) & (AH - 1))
            == lax.broadcasted_iota(jnp.int32, (width, LANES), 1)).astype(f32)
    dg = jnp.dot(jnp.broadcast_to(dg_lanes, (8, width)), fold, precision=lax.Precision.HIGHEST, preferred_element_type=f32)
    return dx, dg[0:1, :AH]


def _ssd_prep(dt_pre, dt_bias, a_log, dt_s, acum_s, acumT_s):
    dt = _softplus(dt_pre + dt_bias)
    a = dt * (-jnp.exp(a_log))
    row = lax.broadcasted_iota(jnp.int32, (CHUNK, CHUNK), 0)
    col = lax.broadcasted_iota(jnp.int32, (CHUNK, CHUNK), 1)
    dt_s[...] = dt
    acum_s[...] = jnp.dot((col <= row).astype(f32), a, precision=lax.Precision.HIGHEST, preferred_element_type=f32)
    acumT_s[...] = lax.dot_general(a, (row <= col).astype(f32), (((0,), (0,)), ((), ())),
                                   precision=lax.Precision.HIGHEST, preferred_element_type=f32)


def _head_cols(h, dt_s, acum_s, acumT_s):
    lane = lax.broadcasted_iota(jnp.int32, (1, LANES), 1)
    oh_l = (lane == h).astype(f32)
    sub = lax.broadcasted_iota(jnp.int32, (LANES, 1), 0)
    oh_s = (sub == h).astype(f32)
    dt_h = jnp.sum(dt_s[...] * oh_l, axis=1, keepdims=True)
    ac_h = jnp.sum(acum_s[...] * oh_l, axis=1, keepdims=True)
    acr_h = jnp.sum(acumT_s[...] * oh_s, axis=0, keepdims=True)
    return oh_l, dt_h, ac_h, acr_h


def ssd_fwd(xs, Bm, Cm, dt_pre, dt_bias, a_log, d_skip, comm=None):
    def body(xs_ref, b_ref, c_ref, dtp_ref, bias_ref, alog_ref, d_ref, y_ref, st_ref, state, dt_s, acum_s, acumT_s):
        c, g = pl.program_id(0), pl.program_id(1)

        @pl.when(g == 0)
        def _():
            _ssd_prep(dtp_ref[...], bias_ref[...], alog_ref[...], dt_s, acum_s, acumT_s)

        row = lax.broadcasted_iota(jnp.int32, (CHUNK, CHUNK), 0)
        col = lax.broadcasted_iota(jnp.int32, (CHUNK, CHUNK), 1)
        causal = col <= row
        Bb, Cb = b_ref[...], c_ref[...]
        cb = lax.dot_general(Cb.astype(MXU), Bb.astype(MXU), (((1,), (1,)), ((), ())), preferred_element_type=f32)
        xs_blk = xs_ref[...]

        @pl.when(c == 0)
        def _():
            for j in range(HPG):
                state[g * HPG + j] = jnp.zeros((NSTATE, HP), f32)

        prevs = [state[g * HPG + j] for j in range(HPG)]
        y_off_all = jnp.dot(Cb.astype(MXU), jnp.concatenate(prevs, axis=1).astype(MXU), preferred_element_type=f32)
        ys, xds, e_ends = [], [], []
        for j in range(HPG):
            oh_l, dt_h, ac_h, acr_h = _head_cols(g * HPG + j, dt_s, acum_s, acumT_s)
            decay = jnp.exp(jnp.where(causal, ac_h - acr_h, -1e30))
            w = (cb * decay).astype(MXU)
            xs_h = xs_blk[:, HP * j:HP * (j + 1)]
            xd = xs_h * dt_h
            y_diag = jnp.dot(w, xd.astype(MXU), preferred_element_type=f32)
            y_off = y_off_all[:, HP * j:HP * (j + 1)] * jnp.exp(ac_h)
            d_h = jnp.sum(d_ref[...] * oh_l, axis=1, keepdims=True)
            ys.append(y_diag + y_off + xs_h * d_h)
            a_end = ac_h[CHUNK - 1:CHUNK, :]
            xds.append(xd * jnp.exp(a_end - ac_h))
            e_ends.append(jnp.exp(a_end))
        s_c = lax.dot_general(Bb.astype(MXU), jnp.concatenate(xds, axis=1).astype(MXU), (((0,), (0,)), ((), ())),
                              preferred_element_type=f32)
        for j in range(HPG):
            st_ref[0, j] = prevs[j]
            state[g * HPG + j] = prevs[j] * e_ends[j] + s_c[:, HP * j:HP * (j + 1)]
        y_ref[...] = jnp.concatenate(ys, axis=1)

    par = pl.BlockSpec((1, LANES), lambda c, g: (0, 0))
    return _call(
        body, [xs, Bm, Cm, dt_pre, dt_bias, a_log, d_skip], comm=comm, name="ssd_fwd", grid=(NCHUNK, GROUPS),
        in_specs=[pl.BlockSpec((CHUNK, HPG * HP), lambda c, g: (c, g)),
                  pl.BlockSpec((CHUNK, NSTATE), lambda c, g: (c, g)),
                  pl.BlockSpec((CHUNK, NSTATE), lambda c, g: (c, g)),
                  pl.BlockSpec((CHUNK, LANES), lambda c, g: (c, 0)), par, par, par],
        out_specs=[pl.BlockSpec((CHUNK, HPG * HP), lambda c, g: (c, g)),
                   pl.BlockSpec((1, HPG, NSTATE, HP), lambda c, g: (c, g, 0, 0))],
        out_shape=[jax.ShapeDtypeStruct((S, INNER), f32), jax.ShapeDtypeStruct((NCHUNK, HEADS, NSTATE, HP), f32)],
        scratch=[pltpu.VMEM((HEADS, NSTATE, HP), f32), pltpu.VMEM((CHUNK, LANES), f32),
                 pltpu.VMEM((CHUNK, LANES), f32), pltpu.VMEM((LANES, CHUNK), f32)],
        sem=("arbitrary", "arbitrary"))


def ssd_bwd(xs, Bm, Cm, dt_pre, dt_bias, a_log, d_skip, states, dy, comm=None):
    rev = lambda c: NCHUNK - 1 - c

    def body(xs_ref, b_ref, c_ref, dtp_ref, bias_ref, alog_ref, d_ref, st_ref, dy_ref,
             dxs_ref, db_ref, dc_ref, ddt_ref, dbias_ref, dalog_ref, dd_ref,
             dstate, dt_s, acum_s, acumT_s, dacum_s, ddt_s, da_s):
        c, g = pl.program_id(0), pl.program_id(1)

        @pl.when(g == 0)
        def _():
            _ssd_prep(dtp_ref[...], bias_ref[...], alog_ref[...], dt_s, acum_s, acumT_s)
            dacum_s[...] = jnp.zeros_like(dacum_s)
            ddt_s[...] = jnp.zeros_like(ddt_s)

        @pl.when((c == 0) & (g == 0))
        def _():
            da_s[...] = jnp.zeros_like(da_s)
            dd_ref[...] = jnp.zeros_like(dd_ref)
            dbias_ref[...] = jnp.zeros_like(dbias_ref)
            dalog_ref[...] = jnp.zeros_like(dalog_ref)

        row = lax.broadcasted_iota(jnp.int32, (CHUNK, CHUNK), 0)
        col = lax.broadcasted_iota(jnp.int32, (CHUNK, CHUNK), 1)
        sub_l = lax.broadcasted_iota(jnp.int32, (CHUNK, 1), 0)
        last = (sub_l == CHUNK - 1).astype(f32)
        nt = (((1,), (1,)), ((), ()))
        tn = (((0,), (0,)), ((), ()))
        Bb, Cb = b_ref[...], c_ref[...]
        Bm_, Cm_ = Bb.astype(MXU), Cb.astype(MXU)
        cb = lax.dot_general(Cm_, Bm_, nt, preferred_element_type=f32)
        bc = lax.dot_general(Bm_, Cm_, nt, preferred_element_type=f32)
        xs_blk, dy_blk = xs_ref[...], dy_ref[...]
        dxs, dB, dC = [], jnp.zeros((CHUNK, NSTATE), f32), jnp.zeros((CHUNK, NSTATE), f32)
        for j in range(HPG):
            h = g * HPG + j
            oh_l, dt_h, ac_h, acr_h = _head_cols(h, dt_s, acum_s, acumT_s)

            @pl.when(c == 0)
            def _():
                dstate[h] = jnp.zeros((NSTATE, HP), f32)

            dnext = dstate[h]
            prev = st_ref[0, j]
            lm = jnp.exp(jnp.where(col <= row, ac_h - acr_h, -1e30))
            lmT = jnp.exp(jnp.where(row <= col, acr_h - ac_h, -1e30))
            xs_h = xs_blk[:, HP * j:HP * (j + 1)]
            dy_h = dy_blk[:, HP * j:HP * (j + 1)]
            xd = xs_h * dt_h
            xdm, dym = xd.astype(MXU), dy_h.astype(MXU)
            ea = jnp.exp(ac_h)
            a_end = ac_h[CHUNK - 1:CHUNK, :]
            e_end = jnp.exp(a_end)
            dte = jnp.exp(a_end - ac_h)
            dnm, pvm = dnext.astype(MXU), prev.astype(MXU)
            bd = jnp.dot(Bm_, dnm, preferred_element_type=f32)
            dxd = jnp.dot((bc * lmT).astype(MXU), dym, preferred_element_type=f32) + dte * bd
            dw = lax.dot_general(dym, xdm, nt, preferred_element_type=f32)
            dwT = lax.dot_general(xdm, dym, nt, preferred_element_type=f32)
            dcb = dw * lm
            dbc = dwT * lmT
            eady = (ea * dy_h).astype(MXU)
            dC = dC + jnp.dot(dcb.astype(MXU), Bm_, preferred_element_type=f32) \
                + lax.dot_general(eady, pvm, nt, preferred_element_type=f32)
            dB = dB + jnp.dot(dbc.astype(MXU), Cm_, preferred_element_type=f32) \
                + dte * lax.dot_general(xdm, dnm, nt, preferred_element_type=f32)
            dstate[h] = lax.dot_general(Cm_, eady, tn, preferred_element_type=f32) + e_end * dnext
            r1 = jnp.sum(dcb * cb, axis=1, keepdims=True)
            r2 = jnp.sum(dbc * bc, axis=1, keepdims=True)
            y_off = jnp.dot(Cm_, pvm, preferred_element_type=f32) * ea
            t3 = jnp.sum(dy_h * y_off, axis=1, keepdims=True)
            t4 = jnp.sum(bd * xd, axis=1, keepdims=True) * dte
            end_extra = jnp.sum(t4, axis=0, keepdims=True) + e_end * jnp.sum(jnp.sum(prev * dnext, axis=1, keepdims=True), axis=0, keepdims=True)
            dacum_h = r1 - r2 + t3 - t4 + last * end_extra
            dacum_s[...] += dacum_h * oh_l
            ddt_s[...] += jnp.sum(dxd * xs_h, axis=1, keepdims=True) * oh_l
            d_h = jnp.sum(d_ref[...] * oh_l, axis=1, keepdims=True)
            dxs.append(dxd * dt_h + dy_h * d_h)
            dd_ref[...] += oh_l * jnp.sum(jnp.sum(dy_h * xs_h, axis=1, keepdims=True), axis=0, keepdims=True)
        dxs_ref[...] = jnp.concatenate(dxs, axis=1)
        db_ref[...] = dB
        dc_ref[...] = dC

        @pl.when(g == GROUPS - 1)
        def _():
            a_row = -jnp.exp(alog_ref[...])
            da = jnp.dot((row <= col).astype(f32), dacum_s[...], precision=lax.Precision.HIGHEST, preferred_element_type=f32)
            da_s[...] += jnp.sum(da * dt_s[...], axis=0, keepdims=True)
            z = dtp_ref[...] + bias_ref[...]
            ddt_pre = (ddt_s[...] + da * a_row) * _sigmoid(z)
            ddt_ref[...] = ddt_pre.astype(ddt_ref.dtype)
            dbias_ref[...] += jnp.sum(ddt_pre, axis=0, keepdims=True)

            @pl.when(c == NCHUNK - 1)
            def _():
                dalog_ref[...] = da_s[...] * a_row

    par = pl.BlockSpec((1, LANES), lambda c, g: (0, 0))
    return _call(
        body, [xs, Bm, Cm, dt_pre, dt_bias, a_log, d_skip, states, dy], comm=comm, name="ssd_bwd", grid=(NCHUNK, GROUPS),
        in_specs=[pl.BlockSpec((CHUNK, HPG * HP), lambda c, g: (rev(c), g)),
                  pl.BlockSpec((CHUNK, NSTATE), lambda c, g: (rev(c), g)),
                  pl.BlockSpec((CHUNK, NSTATE), lambda c, g: (rev(c), g)),
                  pl.BlockSpec((CHUNK, LANES), lambda c, g: (rev(c), 0)), par, par, par,
                  pl.BlockSpec((1, HPG, NSTATE, HP), lambda c, g: (rev(c), g, 0, 0)),
                  pl.BlockSpec((CHUNK, HPG * HP), lambda c, g: (rev(c), g))],
        out_specs=[pl.BlockSpec((CHUNK, HPG * HP), lambda c, g: (rev(c), g)),
                   pl.BlockSpec((CHUNK, NSTATE), lambda c, g: (rev(c), g)),
                   pl.BlockSpec((CHUNK, NSTATE), lambda c, g: (rev(c), g)),
                   pl.BlockSpec((CHUNK, LANES), lambda c, g: (rev(c), 0)), par, par, par],
        out_shape=[jax.ShapeDtypeStruct((S, INNER), f32), jax.ShapeDtypeStruct((S, GROUPS * NSTATE), f32),
                   jax.ShapeDtypeStruct((S, GROUPS * NSTATE), f32), jax.ShapeDtypeStruct((S, LANES), MXU),
                   jax.ShapeDtypeStruct((1, LANES), f32), jax.ShapeDtypeStruct((1, LANES), f32),
                   jax.ShapeDtypeStruct((1, LANES), f32)],
        scratch=[pltpu.VMEM((HEADS, NSTATE, HP), f32), pltpu.VMEM((CHUNK, LANES), f32),
                 pltpu.VMEM((CHUNK, LANES), f32), pltpu.VMEM((LANES, CHUNK), f32),
                 pltpu.VMEM((CHUNK, LANES), f32), pltpu.VMEM((CHUNK, LANES), f32), pltpu.VMEM((1, LANES), f32)],
        sem=("arbitrary", "arbitrary"))


ATT_STACK_FWD, ATT_STACK_BWD = 4, 2


def _attn_kv(kp, kc, vp, vc, hk):
    sl = slice(AH * hk, AH * (hk + 1))
    return (jnp.concatenate([kp[:, sl], kc[:, sl]], axis=0).astype(MXU),
            jnp.concatenate([vp[:, sl], vc[:, sl]], axis=0).astype(MXU))


def _stack_heads(x, heads):
    return jnp.concatenate([x[:, AH * h:AH * (h + 1)] for h in heads], axis=0)


def _attn_block(n, q, kb, sinks, heads):
    rows = len(heads) * WIN
    qi = lax.broadcasted_iota(jnp.int32, (rows, 2 * WIN), 0) & (WIN - 1)
    ki = lax.broadcasted_iota(jnp.int32, (rows, 2 * WIN), 1)
    rel = qi + WIN - ki
    mask = (rel >= 0) & (rel < WIN) & ((ki >= WIN) | (n > 0))
    qg = _stack_heads(q, heads).astype(MXU)
    s = lax.dot_general(qg, kb, (((1,), (1,)), ((), ())), preferred_element_type=f32) * (AH ** -0.5)
    s = jnp.where(mask, s, -1e30)
    sink = jnp.concatenate([jnp.broadcast_to(sinks[:, h:h + 1], (WIN, 1)) for h in heads], axis=0)
    m = jnp.maximum(jnp.max(s, axis=1, keepdims=True), sink)
    p = jnp.exp(s - m)
    ps = jnp.exp(sink - m)
    inv = 1.0 / (jnp.sum(p, axis=1, keepdims=True) + ps)
    return qg, p * inv, ps * inv


def _head_blocks(hk, stack):
    return [list(range(QPK * hk + i, QPK * hk + i + stack)) for i in range(0, QPK, stack)]


def _kv_specs():
    prev = lambda n: (jnp.maximum(n - 1, 0), 0)
    cur = lambda n: (n, 0)
    w = KVH * AH
    return [pl.BlockSpec((WIN, w), prev), pl.BlockSpec((WIN, w), cur), pl.BlockSpec((WIN, w), prev), pl.BlockSpec((WIN, w), cur)]


def attn_fwd(q, k, v, sinks, comm=None):
    def body(q_ref, kp_ref, kc_ref, vp_ref, vc_ref, s_ref, o_ref):
        n = pl.program_id(0)
        q_, kp, kc, vp, vc, sk = q_ref[...], kp_ref[...], kc_ref[...], vp_ref[...], vc_ref[...], s_ref[...]
        outs = []
        for hk in range(KVH):
            kb, vb = _attn_kv(kp, kc, vp, vc, hk)
            for heads in _head_blocks(hk, ATT_STACK_FWD):
                _, pr, _ = _attn_block(n, q_, kb, sk, heads)
                o = jnp.dot(pr.astype(MXU), vb, preferred_element_type=f32)
                outs += [o[WIN * i:WIN * (i + 1)] for i in range(len(heads))]
        o_ref[...] = jnp.concatenate(outs, axis=1)

    return _call(
        body, [q, k, k, v, v, sinks], comm=comm, name="attn_fwd", grid=(NBLK,),
        in_specs=[pl.BlockSpec((WIN, D), lambda n: (n, 0))] + _kv_specs() + [pl.BlockSpec((1, QH), lambda n: (0, 0))],
        out_specs=[pl.BlockSpec((WIN, D), lambda n: (n, 0))],
        out_shape=[jax.ShapeDtypeStruct((S, D), f32)], sem=("parallel",))[0]


def attn_bwd(q, k, v, sinks, dout, comm=None):
    def body(q_ref, kp_ref, kc_ref, vp_ref, vc_ref, s_ref, do_ref, dq_ref, dkp_ref, dkc_ref, dvp_ref, dvc_ref, ds_ref):
        n = pl.program_id(0)

        @pl.when(n == 0)
        def _():
            ds_ref[...] = jnp.zeros_like(ds_ref)

        q_, kp, kc, vp, vc, sk, do = q_ref[...], kp_ref[...], kc_ref[...], vp_ref[...], vc_ref[...], s_ref[...], do_ref[...]
        lane = lax.broadcasted_iota(jnp.int32, (1, QH), 1)
        nt = (((1,), (1,)), ((), ()))
        tn = (((0,), (0,)), ((), ()))
        dqs, dkps, dkcs, dvps, dvcs = [], [], [], [], []
        dsink = jnp.zeros((1, QH), f32)
        for hk in range(KVH):
            kb, vb = _attn_kv(kp, kc, vp, vc, hk)
            dkb, dvb = jnp.zeros((2 * WIN, AH), f32), jnp.zeros((2 * WIN, AH), f32)
            for heads in _head_blocks(hk, ATT_STACK_BWD):
                qg, pr, prs = _attn_block(n, q_, kb, sk, heads)
                dog = _stack_heads(do, heads).astype(MXU)
                dp = lax.dot_general(dog, vb, nt, preferred_element_type=f32)
                dvb = dvb + lax.dot_general(pr.astype(MXU), dog, tn, preferred_element_type=f32)
                delta = jnp.sum(pr * dp, axis=1, keepdims=True)
                ds = (pr * (dp - delta)).astype(MXU)
                dsk = -prs * delta
                for i, h in enumerate(heads):
                    dsink = dsink + jnp.sum(dsk[WIN * i:WIN * (i + 1)], axis=0, keepdims=True) * (lane == h).astype(f32)
                dqg = jnp.dot(ds, kb, preferred_element_type=f32) * (AH ** -0.5)
                dkb = dkb + lax.dot_general(ds, qg, tn, preferred_element_type=f32) * (AH ** -0.5)
                dqs += [dqg[WIN * i:WIN * (i + 1)] for i in range(len(heads))]
            dkps.append(dkb[:WIN])
            dkcs.append(dkb[WIN:])
            dvps.append(dvb[:WIN])
            dvcs.append(dvb[WIN:])
        dq_ref[...] = jnp.concatenate(dqs, axis=1)
        dkp_ref[...] = jnp.concatenate(dkps, axis=1)
        dkc_ref[...] = jnp.concatenate(dkcs, axis=1)
        dvp_ref[...] = jnp.concatenate(dvps, axis=1)
        dvc_ref[...] = jnp.concatenate(dvcs, axis=1)
        ds_ref[...] += dsink

    w = KVH * AH
    blk = lambda width: pl.BlockSpec((WIN, width), lambda n: (n, 0))
    return _call(
        body, [q, k, k, v, v, sinks, dout], comm=comm, name="attn_bwd", grid=(NBLK,),
        in_specs=[blk(D)] + _kv_specs() + [pl.BlockSpec((1, QH), lambda n: (0, 0)), blk(D)],
        out_specs=[blk(D), blk(w), blk(w), blk(w), blk(w), pl.BlockSpec((1, QH), lambda n: (0, 0))],
        out_shape=[jax.ShapeDtypeStruct((S, D), f32)] + [jax.ShapeDtypeStruct((S, w), f32)] * 4 + [jax.ShapeDtypeStruct((1, QH), f32)],
        sem=("arbitrary",))


def kv_bwd(kv, pos, inv_freq, k_norm, dkp, dkc, dvp, dvc):
    w = KVH * AH

    def body(kv_ref, pos_ref, if_ref, g_ref, dkp_ref, dkc_ref, dvp_ref, dvc_ref, o_ref, dg_ref, db_ref):
        n = pl.program_id(0)

        @pl.when(n == 0)
        def _():
            dg_ref[...] = jnp.zeros_like(dg_ref)
            db_ref[...] = jnp.zeros_like(db_ref)

        inside = (n < NBLK - 1).astype(f32)
        dk = dkc_ref[...] + inside * dkp_ref[...]
        dv = dvc_ref[...] + inside * dvp_ref[...]
        cos, sin = _rope_tables(pos_ref[...], if_ref[...])
        dkpre, dg = _headnorm_rope_bwd(kv_ref[...], g_ref[...], cos, sin, dk, KVH)
        dkv = jnp.concatenate([dkpre, dv], axis=1)
        o_ref[...] = dkv.astype(o_ref.dtype)
        dg_ref[...] += dg
        db_ref[...] += jnp.sum(dkv, axis=0, keepdims=True)

    nxt = lambda n: (jnp.minimum(n + 1, NBLK - 1), 0)
    cur = lambda n: (n, 0)
    const = lambda n: (0, 0)
    return pl.pallas_call(
        body, name="kv_bwd", grid=(NBLK,),
        in_specs=[pl.BlockSpec((WIN, w), cur), pl.BlockSpec((WIN, 1), cur), pl.BlockSpec((1, AH // 2), const),
                  pl.BlockSpec((1, AH), const), pl.BlockSpec((WIN, w), nxt), pl.BlockSpec((WIN, w), cur),
                  pl.BlockSpec((WIN, w), nxt), pl.BlockSpec((WIN, w), cur)],
        out_specs=[pl.BlockSpec((WIN, 2 * w), cur), pl.BlockSpec((1, AH), const), pl.BlockSpec((1, 2 * w), const)],
        out_shape=[jax.ShapeDtypeStruct((S, 2 * w), MXU), jax.ShapeDtypeStruct((1, AH), f32), jax.ShapeDtypeStruct((1, 2 * w), f32)],
        compiler_params=_cparams(("arbitrary",)),
    )(kv, pos, inv_freq, k_norm, dkp, dkc, dvp, dvc)


def _adam_math(w, g, m, v):
    m = ADAM_B1 * m + (1.0 - ADAM_B1) * g
    v = ADAM_B2 * v + (1.0 - ADAM_B2) * (g * g)
    m_hat = m / (1.0 - ADAM_B1 ** ADAM_STEP)
    v_hat = v / (1.0 - ADAM_B2 ** ADAM_STEP)
    return -ADAM_LR * (m_hat / (jnp.sqrt(v_hat) + ADAM_EPS) + ADAM_WD * w), m, v


def adamw(name, w, g, m, v):
    R, C = w.shape
    tr = _pick(R, (256, 128, 64, 32, 16, 8))
    tc = C if tr < R or C % 256 else 256

    def body(w_ref, g_ref, m_ref, v_ref, d_ref, nm_ref, nv_ref):
        d_ref[...], nm_ref[...], nv_ref[...] = _adam_math(w_ref[...], g_ref[...], m_ref[...], v_ref[...])

    spec = pl.BlockSpec((tr, tc), lambda i, j: (i, j))
    return pl.pallas_call(
        body, name=name, grid=(R // tr, C // tc), in_specs=[spec] * 4, out_specs=[spec] * 3,
        out_shape=[jax.ShapeDtypeStruct((R, C), f32)] * 3, compiler_params=_cparams(("parallel", "parallel")),
    )(w, g, m, v)


def _me():
    return lax.axis_index("x"), lax.axis_index("y"), lax.axis_index("c")


def gather_comm(xs):
    n = len(xs)

    def parts(x_refs, o_refs, sems):
        send_sems, recv_sems, local_sems = sems
        x, y, c = _me()
        me, sibling = (x, y, c), (x, y, 1 - c)
        chips = [(1 - x, y), (x, 1 - y), (1 - x, 1 - y)]

        def copy(a, k, block, to, src=None):
            dst = o_refs[a].at[4 * block[0] + 2 * block[1] + block[2]]
            return pltpu.make_async_remote_copy(
                src_ref=dst if src is None else src, dst_ref=dst,
                send_sem=send_sems.at[7 * a + k], recv_sem=recv_sems.at[7 * a + k], device_id=to, device_id_type=MESH)

        mine = [pltpu.make_async_copy(x_refs[a], o_refs[a].at[4 * x + 2 * y + c], local_sems.at[a]) for a in range(n)]
        first = []
        for a in range(n):
            first.append(copy(a, 0, me, sibling, src=x_refs[a]))
            first += [copy(a, 1 + j, me, (*chip, c), src=x_refs[a]) for j, chip in enumerate(chips)]
        return copy, mine, first, me, sibling, chips, c

    def start(x_refs, o_refs, sems):
        _, mine, first, *_ = parts(x_refs, o_refs, sems)
        for cp in mine + first:
            cp.start()

    def finish(x_refs, o_refs, sems):
        copy, mine, first, me, sibling, chips, c = parts(x_refs, o_refs, sems)
        passed = []
        for j, chip in enumerate(chips):
            for a in range(n):
                copy(a, 1 + j, (*chip, c), me).wait_recv()
                cp = copy(a, 4 + j, (*chip, c), sibling)
                cp.start()
                passed.append(cp)
        for a in range(n):
            copy(a, 0, sibling, me).wait_recv()
            for j, chip in enumerate(chips):
                copy(a, 4 + j, (*chip, 1 - c), me).wait_recv()
        for cp in first + passed:
            cp.wait_send()
        for cp in mine:
            cp.wait()

    return Comm(xs, [jax.ShapeDtypeStruct((N_DEV,) + a.shape, a.dtype) for a in xs],
                [pltpu.SemaphoreType.DMA((7 * n,)), pltpu.SemaphoreType.DMA((7 * n,)), pltpu.SemaphoreType.DMA((n,))], start, finish)


def run_comm(name, comm):
    _call(lambda: None, [], name=name, grid=(1,), in_specs=[], out_specs=[], out_shape=[], comm=comm)
    return comm.results


def sibling_comm(gs):
    n = len(gs)

    def copies(g_refs, o_refs, sems):
        x, y, c = _me()
        return [pltpu.make_async_remote_copy(
            src_ref=g_refs[a].at[:, 1 - c], dst_ref=o_refs[a], send_sem=sems[0].at[a], recv_sem=sems[1].at[a],
            device_id=(x, y, 1 - c), device_id_type=MESH) for a in range(n)]

    def start(g_refs, o_refs, sems):
        for cp in copies(g_refs, o_refs, sems):
            cp.start()

    def finish(g_refs, o_refs, sems):
        for cp in copies(g_refs, o_refs, sems):
            cp.wait()

    return Comm(gs, [jax.ShapeDtypeStruct((4,) + g.shape[2:], g.dtype) for g in gs],
                [pltpu.SemaphoreType.DMA((n,)), pltpu.SemaphoreType.DMA((n,))], start, finish)


def chip_comm(ts):
    n = len(ts)

    def copies(t_refs, o_refs, sems):
        x, y, c = _me()
        chips = [(1 - x, y), (x, 1 - y), (1 - x, 1 - y)]
        return [pltpu.make_async_remote_copy(
            src_ref=t_refs[a].at[2 * px + py], dst_ref=o_refs[a].at[j],
            send_sem=sems[0].at[3 * a + j], recv_sem=sems[1].at[3 * a + j],
            device_id=(px, py, c), device_id_type=MESH) for j, (px, py) in enumerate(chips) for a in range(n)]

    def start(t_refs, o_refs, sems):
        for cp in copies(t_refs, o_refs, sems):
            cp.start()

    def finish(t_refs, o_refs, sems):
        for cp in copies(t_refs, o_refs, sems):
            cp.wait()

    return Comm(ts, [jax.ShapeDtypeStruct((3,) + t.shape[1:], t.dtype) for t in ts],
                [pltpu.SemaphoreType.DMA((3 * n,)), pltpu.SemaphoreType.DMA((3 * n,))], start, finish)


def _row_tile(rows):
    return _pick(rows, (512, 304, 256, 128))


def pair_add(name, g, r):
    _, _, R, C = g.shape
    tr = _row_tile(R)

    def body(c_ref, g_ref, r_ref, o_ref):
        o_ref[0] = (g_ref[0, 0].astype(f32) + r_ref[0].astype(f32)).astype(o_ref.dtype)

    return pl.pallas_call(
        body, name=name,
        grid_spec=pltpu.PrefetchScalarGridSpec(
            num_scalar_prefetch=1, grid=(4, R // tr),
            in_specs=[pl.BlockSpec((1, 1, tr, C), lambda p, i, c: (p, c[0], i, 0)),
                      pl.BlockSpec((1, tr, C), lambda p, i, c: (p, i, 0))],
            out_specs=pl.BlockSpec((1, tr, C), lambda p, i, c: (p, i, 0))),
        out_shape=jax.ShapeDtypeStruct((4, R, C), g.dtype),
        compiler_params=_cparams(("parallel", "parallel")),
    )(lax.axis_index("c").reshape(1).astype(jnp.int32), g, r)


def _sum_of_four(t_ref, r_ref):
    return ((t_ref[0].astype(f32) + r_ref[0].astype(f32)) + r_ref[1].astype(f32)) + r_ref[2].astype(f32)


def _my_chip():
    return (2 * lax.axis_index("x") + lax.axis_index("y")).reshape(1).astype(jnp.int32)


def final_adamw(name, t, r, w, m, v):
    _, R, C = t.shape
    tr = _pick(R, (256, 128, 64, 32, 16))
    tc = C if tr < R or C % 256 else 256

    def body(p_ref, t_ref, r_ref, w_ref, m_ref, v_ref, g_ref, d_ref, nm_ref, nv_ref):
        g_ = _sum_of_four(t_ref, r_ref)
        g_ref[...] = g_
        d_ref[...], nm_ref[...], nv_ref[...] = _adam_math(w_ref[...], g_, m_ref[...], v_ref[...])

    flat = pl.BlockSpec((tr, tc), lambda i, j, p: (i, j))
    return pl.pallas_call(
        body, name=name,
        grid_spec=pltpu.PrefetchScalarGridSpec(
            num_scalar_prefetch=1, grid=(R // tr, C // tc),
            in_specs=[pl.BlockSpec((1, tr, tc), lambda i, j, p: (p[0], i, j)),
                      pl.BlockSpec((3, tr, tc), lambda i, j, p: (0, i, j)), flat, flat, flat],
            out_specs=[flat] * 4),
        out_shape=[jax.ShapeDtypeStruct((R, C), f32)] * 4,
        compiler_params=_cparams(("parallel", "parallel")),
    )(_my_chip(), t, r, w, m, v)


def final_adamw_layers(name, t0, r0, t1, r1, w, m, v):
    _, R, C = t0.shape
    tr = _pick(R, (256, 128, 64, 32, 16))

    def body(p_ref, t0_ref, r0_ref, t1_ref, r1_ref, w_ref, m_ref, v_ref, g_ref, d_ref, nm_ref, nv_ref):
        g_ = jnp.where(pl.program_id(0) == 0, _sum_of_four(t0_ref, r0_ref), _sum_of_four(t1_ref, r1_ref))
        g_ref[0] = g_
        d_ref[0], nm_ref[0], nv_ref[0] = _adam_math(w_ref[0], g_, m_ref[0], v_ref[0])

    mine = pl.BlockSpec((1, tr, C), lambda l, i, p: (p[0], i, 0))
    theirs = pl.BlockSpec((3, tr, C), lambda l, i, p: (0, i, 0))
    layer = pl.BlockSpec((1, tr, C), lambda l, i, p: (l, i, 0))
    return pl.pallas_call(
        body, name=name,
        grid_spec=pltpu.PrefetchScalarGridSpec(
            num_scalar_prefetch=1, grid=(2, R // tr),
            in_specs=[mine, theirs, mine, theirs, layer, layer, layer], out_specs=[layer] * 4),
        out_shape=[jax.ShapeDtypeStruct((2, R, C), f32)] * 4,
        compiler_params=_cparams(("parallel", "parallel")),
    )(_my_chip(), t0, r0, t1, r1, w, m, v)


class ReduceScatter:
    def __init__(self, tag, keys, grads):
        self.tag, self.keys, self.grads = tag, keys, grads
        self.send = [g.reshape((4, 2, g.shape[0] // N_DEV) + g.shape[1:]) for g in grads]

    def sibling(self):
        self.c1 = sibling_comm(self.send)
        return self.c1

    def chips(self):
        self.pairs = [pair_add(f"rs_pair_add_{self.tag}{i}", g, r) for i, (g, r) in enumerate(zip(self.send, self.c1.results))]
        self.c2 = chip_comm(self.pairs)
        return self.c2

    def parts(self):
        return {k: (t, r) for k, t, r in zip(self.keys, self.pairs, self.c2.results)}


IN_ROWS = {"z": (0, 2048), "xs": (2048, 4096), "B": (4096, 5120), "C": (5120, 6144)}
IN_COLS = 2 * INNER + 2 * GROUPS * NSTATE + HEADS


def sum_devices(g):
    def body(g_ref, o_ref):
        acc = g_ref[0]
        for i in range(1, N_DEV):
            acc = acc + g_ref[i]
        o_ref[...] = acc

    return pl.pallas_call(body, name="sum_devices", out_shape=jax.ShapeDtypeStruct(g.shape[1:], f32),
                          compiler_params=_cparams())(g)


def _pack(parts, unit, dtype, lead=()):
    flat = jnp.concatenate([p.reshape(lead + (-1,)).astype(dtype) for p in parts], axis=-1)
    n = flat.shape[-1]
    rows = -(-n // (unit * PACK_W)) * unit
    flat = jnp.pad(flat, [(0, 0)] * len(lead) + [(0, rows * PACK_W - n)])
    return flat.reshape(lead + (rows, PACK_W))


def _unpack(buf, shapes, lead=()):
    flat = buf.reshape(lead + (-1,))
    out, off = [], 0
    for shp in shapes:
        n = math.prod(shp)
        out.append(flat[..., off:off + n].reshape(lead + tuple(shp)))
        off += n
    return out


def _pad_lanes(a):
    return jnp.pad(a, [(0, 0)] * (a.ndim - 1) + [(0, LANES - a.shape[-1])])


_NN = (((1,), (0,)), ((), ()))
_NT = (((1,), (1,)), ((), ()))


def _mm(a, b, dims):
    return lax.dot_general(a.astype(MXU), b.astype(MXU), dims, preferred_element_type=f32)


def _ffn_fwd(tag, x, h, w_inT, conv_w, conv_b, mid_comm=None):
    ct, nblk = FFN_CT, FFN // FFN_CT

    def body(h_ref, wg_ref, wv_ref, cw_ref, cb_ref, gp_ref, v_ref, a_ref):
        h_ = h_ref[...]
        gp, v_ = _mm(h_, wg_ref[...], _NT), _mm(h_, wv_ref[...], _NT)
        gp_ref[...] = gp
        v_ref[...] = v_
        a_ref[...] = (_silu(_conv(gp, cw_ref[...], cb_ref[...])) * v_).astype(a_ref.dtype)

    col = pl.BlockSpec((S, ct), lambda j: (0, j))
    gate_pre, val, act = _call(
        body, [h, w_inT, w_inT, conv_w, conv_b], comm=mid_comm, name=f"{tag}_in", grid=(nblk,),
        in_specs=[pl.BlockSpec((S, D), lambda j: (0, 0)), pl.BlockSpec((ct, D), lambda j: (j, 0)),
                  pl.BlockSpec((ct, D), lambda j: (nblk + j, 0)), pl.BlockSpec((CONV_F, ct), lambda j: (0, j)),
                  pl.BlockSpec((1, ct), lambda j: (0, j))],
        out_specs=[col, col, col],
        out_shape=[jax.ShapeDtypeStruct((S, FFN), f32), jax.ShapeDtypeStruct((S, FFN), f32), jax.ShapeDtypeStruct((S, FFN), MXU)],
        sem=("parallel",))
    return act, (x, h, gate_pre, val, act)


FFN_CT = 256
CONV_CT = 256


def _proj_conv(name, h, wT, row0, cw, cb, comm=None):
    C, ct = cw.shape[1], CONV_CT

    def body(h_ref, w_ref, cw_ref, cb_ref, p_ref, c_ref):
        p = _mm(h_ref[...], w_ref[...], _NT)
        p_ref[...] = p
        c_ref[...] = _silu(_conv(p, cw_ref[...], cb_ref[...]))

    col = pl.BlockSpec((S, ct), lambda j: (0, j))
    return _call(
        body, [h, wT, cw, cb], comm=comm, name=name, grid=(C // ct,),
        in_specs=[pl.BlockSpec((S, D), lambda j: (0, 0)), pl.BlockSpec((ct, D), lambda j: (row0 // ct + j, 0)),
                  pl.BlockSpec((CONV_A, ct), lambda j: (0, j)), pl.BlockSpec((1, ct), lambda j: (0, j))],
        out_specs=[col, col], out_shape=[jax.ShapeDtypeStruct((S, C), f32)] * 2, sem=("parallel",))


def _dconv_wgrad(name, pre, dconv, cw, cb, h):
    C, ct = cw.shape[1], CONV_CT

    def body(p_ref, do_ref, cw_ref, cb_ref, h_ref, dp_ref, g_ref, dw_ref, db_ref):
        p_, w_ = p_ref[...], cw_ref[...]
        taps = _taps(p_, CONV_A)
        dx, dw, db = _conv_bwd(p_, w_, do_ref[...] * _dsilu(_conv(p_, w_, cb_ref[...], taps)), taps)
        dpm = dx.astype(MXU)
        dp_ref[...] = dpm
        g_ref[...] = lax.dot_general(dpm, h_ref[...].astype(MXU), (((0,), (0,)), ((), ())),
                                     preferred_element_type=f32).astype(g_ref.dtype)
        dw_ref[...] = dw
        db_ref[...] = db

    col = pl.BlockSpec((S, ct), lambda j: (0, j))
    return _call(
        body, [pre, dconv, cw, cb, h], name=name, grid=(C // ct,),
        in_specs=[col, col, pl.BlockSpec((CONV_A, ct), lambda j: (0, j)), pl.BlockSpec((1, ct), lambda j: (0, j)),
                  pl.BlockSpec((S, D), lambda j: (0, 0))],
        out_specs=[col, pl.BlockSpec((ct, D), lambda j: (j, 0)), pl.BlockSpec((CONV_A, ct), lambda j: (0, j)),
                   pl.BlockSpec((1, ct), lambda j: (0, j))],
        out_shape=[jax.ShapeDtypeStruct((S, C), MXU), jax.ShapeDtypeStruct((C, D), MXU),
                   jax.ShapeDtypeStruct((CONV_A, C), f32), jax.ShapeDtypeStruct((1, C), f32)],
        sem=("parallel",))


def _ffn_mid_bwd(name, dout, w_down, gate_pre, val, conv_w, conv_b, comm=None):
    ct = FFN_CT

    def body(do_ref, wd_ref, gp_ref, v_ref, w_ref, b_ref, dgv_ref, dw_ref, db_ref, dob_s):
        @pl.when(pl.program_id(0) == 0)
        def _():
            dob_s[...] = do_ref[...].astype(MXU)

        da = _mm(dob_s[...], wd_ref[...], _NT)
        gp, v_, w_ = gp_ref[...], v_ref[...], w_ref[...]
        taps = _taps(gp, CONV_F)
        gate = _conv(gp, w_, b_ref[...], taps)
        sg = _sigmoid(gate)
        dgp, dw, db = _conv_bwd(gp, w_, da * v_ * (sg * (1.0 + gate * (1.0 - sg))), taps)
        dgv_ref[0] = dgp.astype(dgv_ref.dtype)
        dgv_ref[1] = (da * (gate * sg)).astype(dgv_ref.dtype)
        dw_ref[...] = dw
        db_ref[...] = db

    col = pl.BlockSpec((S, ct), lambda j: (0, j))
    return _call(
        body, [dout, w_down, gate_pre, val, conv_w, conv_b], comm=comm, name=name, grid=(FFN // ct,),
        in_specs=[pl.BlockSpec((S, D), lambda j: (0, 0)), pl.BlockSpec((ct, D), lambda j: (j, 0)), col, col,
                  pl.BlockSpec((CONV_F, ct), lambda j: (0, j)), pl.BlockSpec((1, ct), lambda j: (0, j))],
        out_specs=[pl.BlockSpec((2, S, ct), lambda j: (0, 0, j)), pl.BlockSpec((CONV_F, ct), lambda j: (0, j)),
                   pl.BlockSpec((1, ct), lambda j: (0, j))],
        out_shape=[jax.ShapeDtypeStruct((2, S, FFN), MXU), jax.ShapeDtypeStruct((CONV_F, FFN), f32), jax.ShapeDtypeStruct((1, FFN), f32)],
        scratch=[pltpu.VMEM((S, D), MXU)], sem=("arbitrary",))


def _ffn_bwd(tag, layer, saved, norm_g, w_inT, conv_w, conv_b, w_down, dout, mid_comm=None):
    x, h, gate_pre, val, act = saved
    g_down = matmul(f"{tag}_wdown", act, dout, "tn", out_dtype=MXU)
    dgv, g_cw, g_cb = _ffn_mid_bwd(f"{tag}_dmid", dout, w_down, gate_pre, val, conv_w, conv_b, comm=mid_comm)
    g_inT = matmul_tn_stacked(f"{tag}_win", dgv, h, MXU)

    def din_fn(dg_, dv_, x_, do_, g_, wT):
        dx, dg = _rms_bwd(x_, g_, _mm(dg_, wT[:FFN], _NN) + _mm(dv_, wT[FFN:], _NN))
        return do_ + dx, dg

    rs = ReduceScatter(tag, (f"f_inT{layer}", f"f_down{layer}"), [g_inT, g_down])
    dx, g_norm = rowwise(f"{tag}_din", din_fn, [(dgv, 0), (dgv, 1), x, dout], [norm_g, w_inT], [(D, f32)], [(1, D)],
                         comm=rs.sibling())
    return dx, {f"f_norm{layer}": g_norm, f"f_conv_w{layer}": g_cw, f"f_conv_b{layer}": g_cb}, rs


def _land(W, keys, comm):
    for k, g in zip(keys, comm.results):
        W[k] = g.reshape(-1, g.shape[2])


def _local_step(x, pos, tgt, W, shards):
    G = {}
    gather = lambda *keys: gather_comm([shards[k] for k in keys])
    inv_freq = (ROPE_THETA ** (-jnp.arange(AH // 2, dtype=f32) / (AH // 2))).reshape(1, AH // 2)

    def in_fn(x_, g_, wT, wdtT):
        h_ = _rms_fwd(x_, g_).astype(MXU)
        return h_, _mm(h_, wT[slice(*IN_ROWS["z"])], _NT), _mm(h_, wdtT, _NT)

    h0, z, dt_pre = rowwise("a_in", in_fn, [x], [W["a_norm"], W["inT"], W["in_dtT"]], [(D, MXU), (INNER, f32), (LANES, f32)])
    pre, conv = {}, {}
    early = {"xs": ("a_out",), "B": (), "C": ()}
    for k in ("xs", "B", "C"):
        c = gather(*early[k]) if early[k] else None
        pre[k], conv[k] = _proj_conv(f"a_in_{k}", h0, W["inT"], IN_ROWS[k][0], W[f"cw_{k}"], W[f"cb_{k}"], comm=c)
        if c is not None:
            _land(W, early[k], c)
    c = gather("f_inT0", "f_down0")
    y, states = ssd_fwd(conv["xs"], conv["B"], conv["C"], dt_pre, W["dt_bias"], W["A_log"], W["D"], comm=c)
    _land(W, ("f_inT0", "f_down0"), c)

    def gate_norm(y_, z_, g_):
        yg = y_ * _silu(z_)
        w = INNER // GROUPS
        return (jnp.concatenate([_rms_fwd(yg[:, w * i:w * (i + 1)], g_[:, w * i:w * (i + 1)]) for i in range(GROUPS)], axis=1),)

    def out_fn(y_, z_, x_, g_, w_, gf_):
        (gn_,) = gate_norm(y_, z_, g_)
        gn_ = gn_.astype(MXU)
        x1_ = x_ + _mm(gn_, w_, _NN)
        return gn_, x1_, _rms_fwd(x1_, gf_)

    c = gather("f_down1")
    gn, x1, h1 = rowwise("a_out", out_fn, [y, z, x], [W["a_gnorm"], W["a_out"], W["f_norm0"]],
                         [(INNER, MXU), (D, f32), (D, MXU)], comm=c)
    _land(W, ("f_down1",), c)

    c = gather("w_kv", "w_q", "w_o")
    act0, ffn0 = _ffn_fwd("f0", x1, h1, W["f_inT0"], W["f_cw0"], W["f_cb0"], mid_comm=c)
    _land(W, ("w_kv", "w_q", "w_o"), c)
    x2 = matmul("f0_down", act0, W["f_down0"], "nn", residual=x1)

    def qkv_fn(x_, gk, gb, wkv, bkv, wq, bq):
        kvn_, h2_ = _rms_fwd(x_, gk).astype(MXU), _rms_fwd(x_, gb).astype(MXU)
        return kvn_, h2_, _mm(kvn_, wkv, _NN) + bkv, _mm(h2_, wq, _NN) + bq

    kw = KVH * AH
    kvn, h2, kv, q_pre = rowwise("qkv_proj", qkv_fn, [x2], [W["kv_norm"], W["b_norm"], W["w_kv"], W["b_kv"], W["w_q"], W["b_q"]],
                                 [(D, MXU), (D, MXU), (2 * kw, f32), (D, f32)])

    def k_fwd(kv_, pos_, if_, g_):
        cos, sin = _rope_tables(pos_, if_)
        return _headnorm_rope_fwd(kv_[:, :kw], g_, cos, sin, KVH), kv_[:, kw:]

    k_rot, v_val = rowwise("k_rope", k_fwd, [kv, pos], [inv_freq, W["k_norm"]], [(kw, f32), (kw, f32)])

    def q_fwd(q_, pos_, if_, g_):
        cos, sin = _rope_tables(pos_, if_)
        return (_headnorm_rope_fwd(q_, g_, cos, sin, QH),)

    (q,) = rowwise("q_rope", q_fwd, [q_pre, pos], [inv_freq, W["q_norm"]], [(D, f32)])
    c = gather("f_inT1")
    att = attn_fwd(q, k_rot, v_val, W["sinks"], comm=c)
    _land(W, ("f_inT1",), c)
    def o_fn(att_, x_, w_, b_, gf_):
        x3_ = x_ + _mm(att_, w_, _NN) + b_
        return x3_, _rms_fwd(x3_, gf_)

    x3, h3 = rowwise("o_proj", o_fn, [att, x2], [W["w_o"], W["b_o"], W["f_norm1"]], [(D, f32), (D, MXU)])

    act1, ffn1 = _ffn_fwd("f1", x3, h3, W["f_inT1"], W["f_cw1"], W["f_cb1"])

    def loss_fn(a_, x_, t_, w_):
        diff = x_ + _mm(a_, w_, _NN) - t_
        rows = jnp.sum(diff * diff, axis=1, keepdims=True) * (0.5 / D)
        return diff * (1.0 / D), jnp.sum(rows, axis=0, keepdims=True)

    dx4, loss = rowwise("f1_down_loss", loss_fn, [act1, x3, tgt], [W["f_down1"]], [(D, f32)], [(1, 1)])

    dx3, g, rs_f1 = _ffn_bwd("f1", 1, ffn1, W["f_norm1"], W["f_inT1"], W["f_cw1"], W["f_cb1"], W["f_down1"], dx4)
    G.update(g)

    datt = matmul("o_dproj", dx3, W["w_o"], "nt")
    g_wo = matmul("o_wproj", att, dx3, "tn", out_dtype=MXU)
    dq, dkp, dkc, dvp, dvc, G["sinks"] = attn_bwd(q, k_rot, v_val, W["sinks"], datt, comm=rs_f1.chips())

    def q_bwd(q_, pos_, dq_, dx_, if_, g_):
        cos, sin = _rope_tables(pos_, if_)
        dqp, dg = _headnorm_rope_bwd(q_, g_, cos, sin, dq_, QH)
        return dqp, dg, jnp.sum(dqp, axis=0, keepdims=True), jnp.sum(dx_, axis=0, keepdims=True)

    dq_pre, G["q_norm"], G["b_q"], G["b_o"] = rowwise("q_drope", q_bwd, [q_pre, pos, dq, dx3], [inv_freq, W["q_norm"]],
                                                      [(D, MXU)], [(1, AH), (1, D), (1, D)])
    g_wq = matmul("q_wproj", h2, dq_pre, "tn", out_dtype=MXU)
    dkv, G["k_norm"], G["b_kv"] = kv_bwd(kv, pos, inv_freq, W["k_norm"], dkp, dkc, dvp, dvc)
    g_wkv = matmul("kv_wproj", kvn, dkv, "tn", out_dtype=MXU)
    rs_att = ReduceScatter("att", ("w_kv", "w_q", "w_o"), [g_wkv, g_wq, g_wo])

    def x2_bwd(x_, dq_, dkv_, dx_, gb_, gk_, wq, wkv):
        d1, dgb = _rms_bwd(x_, gb_, _mm(dq_, wq, _NT))
        d2, dgk = _rms_bwd(x_, gk_, _mm(dkv_, wkv, _NT))
        return dx_ + d1 + d2, dgb, dgk

    dx2, G["b_norm"], G["kv_norm"] = rowwise("qkv_dproj", x2_bwd, [x2, dq_pre, dkv, dx3],
                                             [W["b_norm"], W["kv_norm"], W["w_q"], W["w_kv"]],
                                             [(D, f32)], [(1, D), (1, D)], comm=rs_att.sibling())

    dx1, g, rs_f0 = _ffn_bwd("f0", 0, ffn0, W["f_norm0"], W["f_inT0"], W["f_cw0"], W["f_cb0"], W["f_down0"], dx2,
                             mid_comm=rs_att.chips())
    G.update(g)

    rs_out = ReduceScatter("a_out", ("a_out",), [matmul("a_wout", gn, dx1, "tn", out_dtype=MXU)])

    def gate_norm_bwd(y_, z_, dx_, g_, w_out):
        dgn_ = _mm(dx_, w_out, _NT)
        w = INNER // GROUPS
        sg = _sigmoid(z_)
        sz = z_ * sg
        yg = y_ * sz
        parts, dgs = [], []
        for i in range(GROUPS):
            dseg, dg = _rms_bwd(yg[:, w * i:w * (i + 1)], g_[:, w * i:w * (i + 1)], dgn_[:, w * i:w * (i + 1)])
            parts.append(dseg)
            dgs.append(dg)
        dyg = jnp.concatenate(parts, axis=1)
        return dyg * sz, dyg * y_ * (sg * (1.0 + z_ * (1.0 - sg))), jnp.concatenate(dgs, axis=1)

    dy, dz, G["a_gnorm"] = rowwise("a_dout", gate_norm_bwd, [y, z, dx1], [W["a_gnorm"], W["a_out"]],
                                   [(INNER, f32), (INNER, MXU)], [(1, INNER)], comm=rs_out.sibling())
    dconv = {}
    dconv["xs"], dconv["B"], dconv["C"], ddt_pre, G["dt_bias"], G["A_log"], G["D"] = ssd_bwd(
        conv["xs"], conv["B"], conv["C"], dt_pre, W["dt_bias"], W["A_log"], W["D"], states, dy,
        comm=merge_comms([rs_f0.chips(), rs_out.chips()]))

    g_in, dpre = [matmul("a_win_z", dz, h0, "tn", out_dtype=MXU)], {}
    for k in ("xs", "B", "C"):
        dpre[k], g_k, G[f"cw_{k}"], G[f"cb_{k}"] = _dconv_wgrad(f"a_dconv_{k}", pre[k], dconv[k], W[f"cw_{k}"], W[f"cb_{k}"], h0)
        g_in.append(g_k)
    g_in.append(matmul("a_win_dt", ddt_pre, h0, "tn", out_dtype=MXU)[:HEADS])
    rs_in = ReduceScatter("a_in", ("inT",), [jnp.concatenate(g_in, axis=0)])
    run_comm("rs_in_sibling", rs_in.sibling())

    def x0_bwd(dz_, dxs_, db_, dc_, ddt_, x_, do_, g_, wT, wdtT):
        parts = zip((dz_, dxs_, db_, dc_), IN_ROWS.values())
        dh = sum(_mm(d_, wT[a:b], _NN) for d_, (a, b) in parts) + _mm(ddt_, wdtT, _NN)
        dx, dg = _rms_bwd(x_, g_, dh)
        return do_ + dx, dg

    dx, G["a_norm"] = rowwise("a_din", x0_bwd, [dz, dpre["xs"], dpre["B"], dpre["C"], ddt_pre, x, dx1],
                              [W["a_norm"], W["inT"], W["in_dtT"]], [(D, f32)], [(1, D)], comm=rs_in.chips())
    return loss, dx, G, [rs_f1, rs_att, rs_f0, rs_out, rs_in]


ROW_KEYS = ("inT", "a_out", "f_inT0", "f_down0", "w_kv", "w_q", "w_o", "f_inT1", "f_down1")


def _row_blocks(src):
    return {"inT": src["a_in_proj"][0].T, "a_out": src["a_out_proj"][0], "w_kv": src["w_kv"], "w_q": src["w_q"][0],
            "w_o": src["w_o"][0], "f_inT0": src["f_w_in"][0].T, "f_inT1": src["f_w_in"][1].T,
            "f_down0": src["f_w_down"][0], "f_down1": src["f_w_down"][1]}


def _from_row_blocks(rb):
    out = {"a_in_proj": rb["inT"].T[None], "a_out_proj": rb["a_out"][None], "w_kv": rb["w_kv"], "w_q": rb["w_q"][None],
           "w_o": rb["w_o"][None]}
    if "f_inT0" in rb:
        out["f_w_in"] = jnp.stack([rb["f_inT0"].T, rb["f_inT1"].T])
        out["f_w_down"] = jnp.stack([rb["f_down0"], rb["f_down1"]])
    return out


SMALL_SHARDED = ("a_norm", "a_conv_w", "a_conv_b", "a_gnorm", "f_conv_w")
REPLICATED = ("a_dt_bias", "a_A_log", "a_D", "kv_norm", "b_kv", "k_norm", "b_norm", "b_q", "q_norm", "sinks", "b_o",
              "f_norm", "f_conv_b")
ORDER = ("a_norm", "a_in_proj", "a_conv_w", "a_conv_b", "a_dt_bias", "a_A_log", "a_D", "a_gnorm", "a_out_proj", "kv_norm",
         "w_kv", "b_kv", "k_norm", "b_norm", "w_q", "b_q", "q_norm", "sinks", "w_o", "b_o", "f_norm", "f_w_in",
         "f_conv_w", "f_conv_b", "f_w_down")


def _gathered_to_whole(name, g):
    if name == "a_conv_w":
        return jnp.moveaxis(g[:, 0], 0, 1).reshape(g.shape[2], -1)
    if name in ("a_norm", "a_conv_b", "a_gnorm"):
        return g[:, 0].reshape(1, -1)
    if name == "f_conv_w":
        return jnp.moveaxis(g, 0, 2).reshape(g.shape[1], g.shape[2], -1)
    raise ValueError(name)


def _whole_to_shards(name, w):
    if name == "a_conv_w":
        return jnp.moveaxis(w.reshape(w.shape[0], N_DEV, -1), 1, 0)[:, None]
    if name in ("a_norm", "a_conv_b", "a_gnorm"):
        return w.reshape(N_DEV, 1, -1)
    if name == "f_conv_w":
        return jnp.moveaxis(w.reshape(w.shape[0], w.shape[1], N_DEV, -1), 2, 0)
    raise ValueError(name)


def _small_weights(whole):
    W = {}
    cw, cb = whole["a_conv_w"], whole["a_conv_b"]
    o = 0
    for k, n in (("xs", INNER), ("B", GROUPS * NSTATE), ("C", GROUPS * NSTATE)):
        W[f"cw_{k}"], W[f"cb_{k}"] = cw[:, o:o + n], cb[:, o:o + n]
        o += n
    W["a_norm"], W["a_gnorm"] = whole["a_norm"], whole["a_gnorm"]
    W["dt_bias"], W["A_log"], W["D"] = (_pad_lanes(whole[k]) for k in ("a_dt_bias", "a_A_log", "a_D"))
    W["kv_norm"], W["b_kv"], W["k_norm"] = whole["kv_norm"].reshape(1, -1), whole["b_kv"].reshape(1, -1), whole["k_norm"].reshape(1, -1)
    for k in ("b_norm", "b_q", "q_norm", "sinks", "b_o"):
        W[k] = whole[k]
    for i in range(2):
        W[f"f_norm{i}"] = whole["f_norm"][i:i + 1]
        W[f"f_cw{i}"], W[f"f_cb{i}"] = whole["f_conv_w"][i], whole["f_conv_b"][i:i + 1]
    return W


def _small_grads(G, shapes):
    nh = HEADS
    out = {
        "a_conv_w": jnp.concatenate([G["cw_xs"], G["cw_B"], G["cw_C"]], axis=1),
        "a_conv_b": jnp.concatenate([G["cb_xs"], G["cb_B"], G["cb_C"]], axis=1),
        "a_norm": G["a_norm"], "a_gnorm": G["a_gnorm"],
        "a_dt_bias": G["dt_bias"][:, :nh], "a_A_log": G["A_log"][:, :nh], "a_D": G["D"][:, :nh],
        "kv_norm": G["kv_norm"], "b_kv": G["b_kv"], "k_norm": G["k_norm"], "b_norm": G["b_norm"],
        "b_q": G["b_q"], "q_norm": G["q_norm"], "sinks": G["sinks"], "b_o": G["b_o"],
        "f_norm": jnp.concatenate([G["f_norm0"], G["f_norm1"]], axis=0),
        "f_conv_w": jnp.stack([G["f_conv_w0"], G["f_conv_w1"]]),
        "f_conv_b": jnp.concatenate([G["f_conv_b0"], G["f_conv_b1"]], axis=0),
    }
    return {k: val.reshape(shapes[k]) if k in shapes else val for k, val in out.items()}


def kernel(x, positions, a_norm, a_in_proj, a_conv_w, a_conv_b, a_dt_bias, a_A_log, a_D, a_gnorm, a_out_proj, kv_norm, w_kv, b_kv, k_norm, b_norm, w_q, b_q, q_norm, sinks, w_o, b_o, f_norm, f_w_in, f_conv_w, f_conv_b, f_w_down, loss_target, m_a_norm, m_a_in_proj, m_a_conv_w, m_a_conv_b, m_a_dt_bias, m_a_A_log, m_a_D, m_a_gnorm, m_a_out_proj, m_kv_norm, m_w_kv, m_b_kv, m_k_norm, m_b_norm, m_w_q, m_b_q, m_q_norm, m_sinks, m_w_o, m_b_o, m_f_norm, m_f_w_in, m_f_conv_w, m_f_conv_b, m_f_w_down, v_a_norm, v_a_in_proj, v_a_conv_w, v_a_conv_b, v_a_dt_bias, v_a_A_log, v_a_D, v_a_gnorm, v_a_out_proj, v_kv_norm, v_w_kv, v_b_kv, v_k_norm, v_b_norm, v_w_q, v_b_q, v_q_norm, v_sinks, v_w_o, v_b_o, v_f_norm, v_f_w_in, v_f_conv_w, v_f_conv_b, v_f_w_down):
    given = dict(locals())
    w_in = {n: given[n] for n in ORDER}
    m_in = {n: given["m_" + n] for n in ORDER}
    v_in = {n: given["v_" + n] for n in ORDER}
    dev = 4 * lax.axis_index("x") + 2 * lax.axis_index("y") + lax.axis_index("c")

    w2, m2, v2 = _row_blocks(w_in), _row_blocks(m_in), _row_blocks(v_in)
    small_pack = _pack([w_in[n] for n in SMALL_SHARDED], 8, f32)
    shards = {k: w2[k].astype(MXU) for k in ROW_KEYS}
    in_all, small_all = run_comm("ag_head", gather_comm([shards["inT"], small_pack]))
    whole = {n: w_in[n] for n in REPLICATED}
    for n, g in zip(SMALL_SHARDED, _unpack(small_all, [w_in[n].shape for n in SMALL_SHARDED], lead=(N_DEV,))):
        whole[n] = _gathered_to_whole(n, g)
    W = _small_weights(whole)
    W["inT"] = in_all.reshape(-1, D)
    W["in_dtT"] = jnp.pad(W["inT"][IN_COLS - HEADS:], ((0, LANES - HEADS), (0, 0)))

    loss, dx, G, scatters = _local_step(x[0], positions.reshape(S, 1).astype(f32), loss_target[0], W, shards)
    grads = _small_grads(G, {n: whole[n].shape for n in REPLICATED})

    small_names = SMALL_SHARDED + REPLICATED
    small_part = _pack([grads[n] for n in small_names] + [loss], 8, f32)
    small_gather = gather_comm([small_part])
    run_comm("ag_small_grads", small_gather)
    parts = {}
    for rs in scatters:
        parts.update(rs.parts())

    single = tuple(k for k in ROW_KEYS if not k.startswith("f_"))
    stepped = {k: final_adamw(f"adamw_{k}", *parts[k], w2[k], m2[k], v2[k]) for k in single}
    g_out, delta, new_m, new_v = (_from_row_blocks({k: stepped[k][i] for k in single}) for i in range(4))
    for n, key, lay in (("f_w_in", "f_inT", lambda a: jnp.swapaxes(a, 1, 2)), ("f_w_down", "f_down", lambda a: a)):
        res = final_adamw_layers(f"adamw_{n}", *parts[key + "0"], *parts[key + "1"], lay(w_in[n]), lay(m_in[n]), lay(v_in[n]))
        g_out[n], delta[n], new_m[n], new_v[n] = (lay(a) for a in res)
    *small_sums, loss_all = _unpack(sum_devices(small_gather.results[0]), [grads[n].shape for n in small_names] + [(1, 1)])
    for n, g in zip(small_names, small_sums):
        if n in SMALL_SHARDED:
            g_out[n] = lax.dynamic_index_in_dim(_whole_to_shards(n, g), dev, axis=0, keepdims=False)
        else:
            g_out[n] = g.reshape(w_in[n].shape)

    packs = [_pack([src[n] for n in small_names], 8, f32) for src in (w_in, g_out, m_in, v_in)]
    outs = adamw("adamw_small", *packs)
    for dst, buf in zip((delta, new_m, new_v), outs):
        for n, a in zip(small_names, _unpack(buf, [w_in[n].shape for n in small_names])):
            dst[n] = a

    return (loss_all[0, 0], dx[None], *[g_out[n] for n in ORDER], *[delta[n] for n in ORDER],
            *[new_m[n] for n in ORDER], *[new_v[n] for n in ORDER])
```

```python
import functools
import math

import jax
import jax.numpy as jnp
from jax import lax
from jax.experimental import pallas as pl
from jax.experimental.pallas import tpu as pltpu

f32 = jnp.float32
bf16 = jnp.bfloat16
MXU = bf16

N_DEV = 8
S = 2048
D = 1024
EPS = 1e-5
INNER = 2048
HEADS = 32
HP = 64
GROUPS = 8
HPG = HEADS // GROUPS
NSTATE = 128
CONV_A = 4
CHUNK = 256
NCHUNK = S // CHUNK
AH = 64
QH = 16
KVH = 4
QPK = QH // KVH
WIN = 128
NBLK = S // WIN
ROPE_THETA = 10000.0
FFN = 2816
CONV_F = 3
LANES = 128
PACK_W = 1024
VMEM_LIMIT = 56 * 1024 * 1024

ADAM_LR, ADAM_B1, ADAM_B2, ADAM_EPS, ADAM_WD, ADAM_STEP = 0.001, 0.9, 0.999, 1e-08, 0.01, 10

MESH = pl.DeviceIdType.MESH


def _cparams(sem=None):
    return pltpu.CompilerParams(dimension_semantics=sem, vmem_limit_bytes=VMEM_LIMIT)


def _pick(n, cands):
    for c in cands:
        if n % c == 0:
            return c
    return n


class Comm:
    def __init__(self, ins, out_shapes, sems, start, finish):
        self.ins, self.out_shapes, self.sems, self.start, self.finish = list(ins), list(out_shapes), list(sems), start, finish
        self.results, self.children = None, ()

    def set_results(self, res):
        self.results, o = list(res), 0
        for ch in self.children:
            ch.set_results(res[o:o + len(ch.out_shapes)])
            o += len(ch.out_shapes)


def merge_comms(comms):
    def each(fn_name, ins, outs, sems):
        i = o = s = 0
        for c in comms:
            getattr(c, fn_name)(ins[i:i + len(c.ins)], outs[o:o + len(c.out_shapes)], sems[s:s + len(c.sems)])
            i, o, s = i + len(c.ins), o + len(c.out_shapes), s + len(c.sems)

    merged = Comm([a for c in comms for a in c.ins], [a for c in comms for a in c.out_shapes], [a for c in comms for a in c.sems],
                  functools.partial(each, "start"), functools.partial(each, "finish"))
    merged.children = tuple(comms)
    return merged


def _call(body, args, *, name, grid, in_specs, out_specs, out_shape, scratch=(), sem=None, comm=None):
    if comm is None:
        return pl.pallas_call(body, name=name, grid=grid, in_specs=list(in_specs), out_specs=list(out_specs),
                              out_shape=list(out_shape), scratch_shapes=list(scratch), compiler_params=_cparams(sem))(*args)
    n_in, n_out, n_scr, c_in, c_out = len(in_specs), len(out_shape), len(scratch), len(comm.ins), len(comm.out_shapes)
    any_spec = pl.BlockSpec(memory_space=pl.ANY)

    def outer(*refs):
        ins, c_ins = refs[:n_in], refs[n_in:n_in + c_in]
        o = n_in + c_in
        outs, c_outs = refs[o:o + n_out], refs[o + n_out:o + n_out + c_out]
        o += n_out + c_out
        scr, c_sems = refs[o:o + n_scr], refs[o + n_scr:]
        ids = [pl.program_id(i) for i in range(len(grid))]
        first = functools.reduce(jnp.logical_and, [i == 0 for i in ids])
        last = functools.reduce(jnp.logical_and, [i == g - 1 for i, g in zip(ids, grid)])

        @pl.when(first)
        def _():
            comm.start(c_ins, c_outs, c_sems)

        body(*ins, *outs, *scr)

        @pl.when(last)
        def _():
            comm.finish(c_ins, c_outs, c_sems)

    res = pl.pallas_call(
        outer, name=name, grid=grid, in_specs=list(in_specs) + [any_spec] * c_in,
        out_specs=list(out_specs) + [any_spec] * c_out, out_shape=list(out_shape) + comm.out_shapes,
        scratch_shapes=list(scratch) + comm.sems, compiler_params=_cparams(("arbitrary",) * len(grid)),
    )(*args, *comm.ins)
    comm.set_results(res[n_out:])
    return res[:n_out]


def matmul(name, a, b, mode, out_dtype=f32, bias=None, residual=None):
    if mode == "nn":
        (M, K), (K2, N) = a.shape, b.shape
    elif mode == "nt":
        (M, K), (N, K2) = a.shape, b.shape
    else:
        (K, M), (K2, N) = a.shape, b.shape
    assert K == K2, (name, a.shape, b.shape)
    if mode == "tn":
        tm, tn = M, _pick(N, (512, 256, 128) if M <= 1024 else (256, 128))
        a_spec = pl.BlockSpec((K, M), lambda j: (0, 0))
        b_spec = pl.BlockSpec((K, tn), lambda j: (0, j))
        dims = (((0,), (0,)), ((), ()))
        grid, o_map, row_map = (N // tn,), (lambda j: (0, j)), (lambda j: (0, j))
    else:
        tm, tn = (256 if N >= 2048 else 512), N
        a_spec = pl.BlockSpec((tm, K), lambda i: (i, 0))
        b_spec = pl.BlockSpec(b.shape, lambda i: (0, 0))
        dims = (((1,), (0,)), ((), ())) if mode == "nn" else (((1,), (1,)), ((), ()))
        grid, o_map, row_map = (M // tm,), (lambda i: (i, 0)), (lambda i: (0, 0))
    ins, in_specs = [a, b], [a_spec, b_spec]
    if bias is not None:
        ins.append(bias)
        in_specs.append(pl.BlockSpec((1, tn), row_map))
    if residual is not None:
        ins.append(residual)
        in_specs.append(pl.BlockSpec((tm, tn), o_map))
    has_bias, has_res = bias is not None, residual is not None

    def body(a_ref, b_ref, *rest):
        rest = list(rest)
        bias_ref = rest.pop(0) if has_bias else None
        res_ref = rest.pop(0) if has_res else None
        (o_ref,) = rest
        r = lax.dot_general(a_ref[...].astype(MXU), b_ref[...].astype(MXU), dims, preferred_element_type=f32)
        if has_bias:
            r = r + bias_ref[...]
        if has_res:
            r = r + res_ref[...]
        o_ref[...] = r.astype(out_dtype)

    return pl.pallas_call(
        body, name=name, grid=grid, in_specs=in_specs,
        out_specs=pl.BlockSpec((tm, tn), o_map),
        out_shape=jax.ShapeDtypeStruct((M, N), out_dtype),
        compiler_params=_cparams(("parallel",)),
    )(*ins)


def matmul_tn_stacked(name, a, b, out_dtype):
    R, K, M = a.shape
    N = b.shape[1]
    tn = _pick(N, (256, 128))

    def body(a_ref, b_ref, o_ref):
        o_ref[0] = lax.dot_general(a_ref[0].astype(MXU), b_ref[...].astype(MXU), (((0,), (0,)), ((), ())),
                                   preferred_element_type=f32).astype(out_dtype)

    out = pl.pallas_call(
        body, name=name, grid=(R, N // tn),
        in_specs=[pl.BlockSpec((1, K, M), lambda r, j: (r, 0, 0)), pl.BlockSpec((K, tn), lambda r, j: (0, j))],
        out_specs=pl.BlockSpec((1, M, tn), lambda r, j: (r, 0, j)),
        out_shape=jax.ShapeDtypeStruct((R, M, N), out_dtype),
        compiler_params=_cparams(("parallel", "parallel")),
    )(a, b)
    return out.reshape(R * M, N)


def rowwise(name, fn, rows, pars, outs, accs=(), tile=256, comm=None):
    n_in, n_out = len(rows) + len(pars), len(outs)
    in_specs = [pl.BlockSpec((None, tile, r[0].shape[2]), functools.partial(lambda i, lead: (lead, i, 0), lead=r[1]))
                if isinstance(r, tuple) else pl.BlockSpec((tile, r.shape[1]), lambda i: (i, 0)) for r in rows]
    rows = [r[0] if isinstance(r, tuple) else r for r in rows]
    in_specs += [pl.BlockSpec(p.shape, lambda i: (0, 0)) for p in pars]
    out_specs = [pl.BlockSpec((tile, c), lambda i: (i, 0)) for c, _ in outs]
    out_specs += [pl.BlockSpec(shp, lambda i: (0, 0)) for shp in accs]
    out_shape = [jax.ShapeDtypeStruct((S, c), dt) for c, dt in outs]
    out_shape += [jax.ShapeDtypeStruct(shp, f32) for shp in accs]

    def body(*refs):
        res = fn(*[r[...] for r in refs[:n_in]])
        o_refs = refs[n_in:n_in + n_out]
        a_refs = refs[n_in + n_out:]
        for ref, val in zip(o_refs, res[:n_out]):
            ref[...] = val.astype(ref.dtype)
        if a_refs:
            @pl.when(pl.program_id(0) == 0)
            def _():
                for ref in a_refs:
                    ref[...] = jnp.zeros_like(ref)
            for ref, val in zip(a_refs, res[n_out:]):
                ref[...] += val

    return _call(body, [*rows, *pars], name=name, grid=(S // tile,), in_specs=in_specs, out_specs=out_specs,
                 out_shape=out_shape, sem=("arbitrary",) if accs else ("parallel",), comm=comm)


def _sigmoid(x):
    return 0.5 * jnp.tanh(0.5 * x) + 0.5


def _silu(x):
    return x * _sigmoid(x)


def _dsilu(x):
    sg = _sigmoid(x)
    return sg * (1.0 + x * (1.0 - sg))


def _softplus(x):
    return jnp.maximum(x, 0.0) + jnp.log(1.0 + jnp.exp(-jnp.abs(x)))


def _rms_fwd(x, g):
    r = lax.rsqrt(jnp.mean(x * x, axis=-1, keepdims=True) + EPS)
    return x * r * g


def _rms_bwd(x, g, dh):
    r = lax.rsqrt(jnp.mean(x * x, axis=-1, keepdims=True) + EPS)
    xh = x * r
    dxh = dh * g
    dx = r * (dxh - xh * jnp.mean(dxh * xh, axis=-1, keepdims=True))
    return dx, jnp.sum(dh * xh, axis=0, keepdims=True)


def _taps(x, width):
    row = lax.broadcasted_iota(jnp.int32, (8, x.shape[1]), 0)

    def shifted(s):
        r = pltpu.roll(x, s, 0)
        return jnp.concatenate([jnp.where(row >= s, r[:8], 0.0), r[8:]], axis=0)

    return [shifted(s) for s in range(width - 1, 0, -1)] + [x]


def _conv(x, w, b, taps=None):
    width = w.shape[0]
    taps = _taps(x, width) if taps is None else taps
    out = b + w[0:1, :] * taps[0]
    for k in range(1, width):
        out = out + w[k:k + 1, :] * taps[k]
    return out


def _conv_bwd(x, w, dc, taps=None):
    width, n = w.shape[0], x.shape[0]
    taps = _taps(x, width) if taps is None else taps
    row = lax.broadcasted_iota(jnp.int32, (8, x.shape[1]), 0)
    dx = w[width - 1:width, :] * dc
    for k in range(width - 1):
        s = width - 1 - k
        r = pltpu.roll(dc, n - s, 0)
        dx = dx + w[k:k + 1, :] * jnp.concatenate([r[:n - 8], jnp.where(row < 8 - s, r[n - 8:], 0.0)], axis=0)
    dw = jnp.concatenate([jnp.sum(dc * t, axis=0, keepdims=True) for t in taps], axis=0)
    return dx, dw, jnp.sum(dc, axis=0, keepdims=True)


def _rope_tables(pos, inv_freq):
    ang = pos * inv_freq
    return jnp.cos(ang), jnp.sin(ang)


def _split2(v):
    hi = v.astype(bf16)
    return hi, (v - hi.astype(f32)).astype(bf16)


def _head_maps(width):
    shift = AH.bit_length() - 1
    to_head = (lax.broadcasted_iota(jnp.int32, (width, LANES), 0) >> shift) == lax.broadcasted_iota(jnp.int32, (width, LANES), 1)
    from_head = lax.broadcasted_iota(jnp.int32, (LANES, width), 0) == (lax.broadcasted_iota(jnp.int32, (LANES, width), 1) >> shift)
    return to_head.astype(bf16), from_head.astype(bf16)


def _head_sums(v, to_head):
    hi, lo = _split2(v)
    return jnp.dot(hi, to_head, preferred_element_type=f32) + jnp.dot(lo, to_head, preferred_element_type=f32)


def _head_spread(s, from_head):
    hi, lo = _split2(s)
    return jnp.dot(hi, from_head, preferred_element_type=f32) + jnp.dot(lo, from_head, preferred_element_type=f32)


def _rope_full(cos, sin, width):
    half = AH // 2
    pad = jnp.zeros((cos.shape[0], LANES - half), f32)
    r = lax.broadcasted_iota(jnp.int32, (LANES, width), 0)
    lane = lax.broadcasted_iota(jnp.int32, (LANES, width), 1)
    spread = ((lane & (half - 1)) == r).astype(bf16)
    full = lambda t: _head_spread(jnp.concatenate([t, pad], axis=1), spread)
    first = (lax.broadcasted_iota(jnp.int32, (1, width), 1) & (AH - 1)) < half
    sin_f = full(sin)
    return full(cos), jnp.where(first, -sin_f, sin_f), first


def _swap_halves(v, first):
    half, width = AH // 2, v.shape[1]
    return jnp.where(first, pltpu.roll(v, width - half, 1), pltpu.roll(v, half, 1))


def _headnorm_rope_fwd(x, g, cos, sin, heads):
    to_head, from_head = _head_maps(heads * AH)
    cos_f, sin_s, first = _rope_full(cos, sin, heads * AH)
    r = _head_spread(lax.rsqrt(_head_sums(x * x, to_head) * (1.0 / AH) + EPS), from_head)
    n = x * r * jnp.tile(g, (1, heads))
    return n * cos_f + _swap_halves(n, first) * sin_s


def _headnorm_rope_bwd(x, g, cos, sin, dout, heads):
    width = heads * AH
    to_head, from_head = _head_maps(width)
    cos_f, sin_s, first = _rope_full(cos, sin, width)
    r = _head_spread(lax.rsqrt(_head_sums(x * x, to_head) * (1.0 / AH) + EPS), from_head)
    xh = x * r
    dn = dout * cos_f - _swap_halves(dout, first) * sin_s
    dxh = dn * jnp.tile(g, (1, heads))
    m = _head_spread(_head_sums(dxh * xh, to_head) * (1.0 / AH), from_head)
    dx = r * (dxh - xh * m)
    dg_lanes = jnp.sum(dn * xh, axis=0, keepdims=True)
    fold = ((lax.broadcasted_iota(jnp.int32, (width, LANES), 0) & (AH - 1))
            == lax.broadcasted_iota(jnp.int32, (width, LANES), 1)).astype(f32)
    dg = jnp.dot(jnp.broadcast_to(dg_lanes, (8, width)), fold, precision=lax.Precision.HIGHEST, preferred_element_type=f32)
    return dx, dg[0:1, :AH]


def _ssd_prep(dt_pre, dt_bias, a_log, dt_s, acum_s, acumT_s):
    dt = _softplus(dt_pre + dt_bias)
    a = dt * (-jnp.exp(a_log))
    row = lax.broadcasted_iota(jnp.int32, (CHUNK, CHUNK), 0)
    col = lax.broadcasted_iota(jnp.int32, (CHUNK, CHUNK), 1)
    dt_s[...] = dt
    acum_s[...] = jnp.dot((col <= row).astype(f32), a, precision=lax.Precision.HIGHEST, preferred_element_type=f32)
    acumT_s[...] = lax.dot_general(a, (row <= col).astype(f32), (((0,), (0,)), ((), ())),
                                   precision=lax.Precision.HIGHEST, preferred_element_type=f32)


def _head_cols(h, dt_s, acum_s, acumT_s):
    lane = lax.broadcasted_iota(jnp.int32, (1, LANES), 1)
    oh_l = (lane == h).astype(f32)
    sub = lax.broadcasted_iota(jnp.int32, (LANES, 1), 0)
    oh_s = (sub == h).astype(f32)
    dt_h = jnp.sum(dt_s[...] * oh_l, axis=1, keepdims=True)
    ac_h = jnp.sum(acum_s[...] * oh_l, axis=1, keepdims=True)
    acr_h = jnp.sum(acumT_s[...] * oh_s, axis=0, keepdims=True)
    return oh_l, dt_h, ac_h, acr_h


def ssd_fwd(xs, Bm, Cm, dt_pre, dt_bias, a_log, d_skip, comm=None):
    def body(xs_ref, b_ref, c_ref, dtp_ref, bias_ref, alog_ref, d_ref, y_ref, st_ref, state, dt_s, acum_s, acumT_s):
        c, g = pl.program_id(0), pl.program_id(1)

        @pl.when(g == 0)
        def _():
            _ssd_prep(dtp_ref[...], bias_ref[...], alog_ref[...], dt_s, acum_s, acumT_s)

        row = lax.broadcasted_iota(jnp.int32, (CHUNK, CHUNK), 0)
        col = lax.broadcasted_iota(jnp.int32, (CHUNK, CHUNK), 1)
        causal = col <= row
        Bb, Cb = b_ref[...], c_ref[...]
        cb = lax.dot_general(Cb.astype(MXU), Bb.astype(MXU), (((1,), (1,)), ((), ())), preferred_element_type=f32)
        xs_blk = xs_ref[...]

        @pl.when(c == 0)
        def _():
            for j in range(HPG):
                state[g * HPG + j] = jnp.zeros((NSTATE, HP), f32)

        prevs = [state[g * HPG + j] for j in range(HPG)]
        y_off_all = jnp.dot(Cb.astype(MXU), jnp.concatenate(prevs, axis=1).astype(MXU), preferred_element_type=f32)
        ys, xds, e_ends = [], [], []
        for j in range(HPG):
            oh_l, dt_h, ac_h, acr_h = _head_cols(g * HPG + j, dt_s, acum_s, acumT_s)
            decay = jnp.exp(jnp.where(causal, ac_h - acr_h, -1e30))
            w = (cb * decay).astype(MXU)
            xs_h = xs_blk[:, HP * j:HP * (j + 1)]
            xd = xs_h * dt_h
            y_diag = jnp.dot(w, xd.astype(MXU), preferred_element_type=f32)
            y_off = y_off_all[:, HP * j:HP * (j + 1)] * jnp.exp(ac_h)
            d_h = jnp.sum(d_ref[...] * oh_l, axis=1, keepdims=True)
            ys.append(y_diag + y_off + xs_h * d_h)
            a_end = ac_h[CHUNK - 1:CHUNK, :]
            xds.append(xd * jnp.exp(a_end - ac_h))
            e_ends.append(jnp.exp(a_end))
        s_c = lax.dot_general(Bb.astype(MXU), jnp.concatenate(xds, axis=1).astype(MXU), (((0,), (0,)), ((), ())),
                              preferred_element_type=f32)
        for j in range(HPG):
            st_ref[0, j] = prevs[j]
            state[g * HPG + j] = prevs[j] * e_ends[j] + s_c[:, HP * j:HP * (j + 1)]
        y_ref[...] = jnp.concatenate(ys, axis=1)

    par = pl.BlockSpec((1, LANES), lambda c, g: (0, 0))
    return _call(
        body, [xs, Bm, Cm, dt_pre, dt_bias, a_log, d_skip], comm=comm, name="ssd_fwd", grid=(NCHUNK, GROUPS),
        in_specs=[pl.BlockSpec((CHUNK, HPG * HP), lambda c, g: (c, g)),
                  pl.BlockSpec((CHUNK, NSTATE), lambda c, g: (c, g)),
                  pl.BlockSpec((CHUNK, NSTATE), lambda c, g: (c, g)),
                  pl.BlockSpec((CHUNK, LANES), lambda c, g: (c, 0)), par, par, par],
        out_specs=[pl.BlockSpec((CHUNK, HPG * HP), lambda c, g: (c, g)),
                   pl.BlockSpec((1, HPG, NSTATE, HP), lambda c, g: (c, g, 0, 0))],
        out_shape=[jax.ShapeDtypeStruct((S, INNER), f32), jax.ShapeDtypeStruct((NCHUNK, HEADS, NSTATE, HP), f32)],
        scratch=[pltpu.VMEM((HEADS, NSTATE, HP), f32), pltpu.VMEM((CHUNK, LANES), f32),
                 pltpu.VMEM((CHUNK, LANES), f32), pltpu.VMEM((LANES, CHUNK), f32)],
        sem=("arbitrary", "arbitrary"))


def ssd_bwd(xs, Bm, Cm, dt_pre, dt_bias, a_log, d_skip, states, dy, comm=None):
    rev = lambda c: NCHUNK - 1 - c

    def body(xs_ref, b_ref, c_ref, dtp_ref, bias_ref, alog_ref, d_ref, st_ref, dy_ref,
             dxs_ref, db_ref, dc_ref, ddt_ref, dbias_ref, dalog_ref, dd_ref,
             dstate, dt_s, acum_s, acumT_s, dacum_s, ddt_s, da_s):
        c, g = pl.program_id(0), pl.program_id(1)

        @pl.when(g == 0)
        def _():
            _ssd_prep(dtp_ref[...], bias_ref[...], alog_ref[...], dt_s, acum_s, acumT_s)
            dacum_s[...] = jnp.zeros_like(dacum_s)
            ddt_s[...] = jnp.zeros_like(ddt_s)

        @pl.when((c == 0) & (g == 0))
        def _():
            da_s[...] = jnp.zeros_like(da_s)
            dd_ref[...] = jnp.zeros_like(dd_ref)
            dbias_ref[...] = jnp.zeros_like(dbias_ref)
            dalog_ref[...] = jnp.zeros_like(dalog_ref)

        row = lax.broadcasted_iota(jnp.int32, (CHUNK, CHUNK), 0)
        col = lax.broadcasted_iota(jnp.int32, (CHUNK, CHUNK), 1)
        sub_l = lax.broadcasted_iota(jnp.int32, (CHUNK, 1), 0)
        last = (sub_l == CHUNK - 1).astype(f32)
        nt = (((1,), (1,)), ((), ()))
        tn = (((0,), (0,)), ((), ()))
        Bb, Cb = b_ref[...], c_ref[...]
        Bm_, Cm_ = Bb.astype(MXU), Cb.astype(MXU)
        cb = lax.dot_general(Cm_, Bm_, nt, preferred_element_type=f32)
        bc = lax.dot_general(Bm_, Cm_, nt, preferred_element_type=f32)
        xs_blk, dy_blk = xs_ref[...], dy_ref[...]
        dxs, dB, dC = [], jnp.zeros((CHUNK, NSTATE), f32), jnp.zeros((CHUNK, NSTATE), f32)
        for j in range(HPG):
            h = g * HPG + j
            oh_l, dt_h, ac_h, acr_h = _head_cols(h, dt_s, acum_s, acumT_s)

            @pl.when(c == 0)
            def _():
                dstate[h] = jnp.zeros((NSTATE, HP), f32)

            dnext = dstate[h]
            prev = st_ref[0, j]
            lm = jnp.exp(jnp.where(col <= row, ac_h - acr_h, -1e30))
            lmT = jnp.exp(jnp.where(row <= col, acr_h - ac_h, -1e30))
            xs_h = xs_blk[:, HP * j:HP * (j + 1)]
            dy_h = dy_blk[:, HP * j:HP * (j + 1)]
            xd = xs_h * dt_h
            xdm, dym = xd.astype(MXU), dy_h.astype(MXU)
            ea = jnp.exp(ac_h)
            a_end = ac_h[CHUNK - 1:CHUNK, :]
            e_end = jnp.exp(a_end)
            dte = jnp.exp(a_end - ac_h)
            dnm, pvm = dnext.astype(MXU), prev.astype(MXU)
            bd = jnp.dot(Bm_, dnm, preferred_element_type=f32)
            dxd = jnp.dot((bc * lmT).astype(MXU), dym, preferred_element_type=f32) + dte * bd
            dw = lax.dot_general(dym, xdm, nt, preferred_element_type=f32)
            dwT = lax.dot_general(xdm, dym, nt, preferred_element_type=f32)
            dcb = dw * lm
            dbc = dwT * lmT
            eady = (ea * dy_h).astype(MXU)
            dC = dC + jnp.dot(dcb.astype(MXU), Bm_, preferred_element_type=f32) \
                + lax.dot_general(eady, pvm, nt, preferred_element_type=f32)
            dB = dB + jnp.dot(dbc.astype(MXU), Cm_, preferred_element_type=f32) \
                + dte * lax.dot_general(xdm, dnm, nt, preferred_element_type=f32)
            dstate[h] = lax.dot_general(Cm_, eady, tn, preferred_element_type=f32) + e_end * dnext
            r1 = jnp.sum(dcb * cb, axis=1, keepdims=True)
            r2 = jnp.sum(dbc * bc, axis=1, keepdims=True)
            y_off = jnp.dot(Cm_, pvm, preferred_element_type=f32) * ea
            t3 = jnp.sum(dy_h * y_off, axis=1, keepdims=True)
            t4 = jnp.sum(bd * xd, axis=1, keepdims=True) * dte
            end_extra = jnp.sum(t4, axis=0, keepdims=True) + e_end * jnp.sum(jnp.sum(prev * dnext, axis=1, keepdims=True), axis=0, keepdims=True)
            dacum_h = r1 - r2 + t3 - t4 + last * end_extra
            dacum_s[...] += dacum_h * oh_l
            ddt_s[...] += jnp.sum(dxd * xs_h, axis=1, keepdims=True) * oh_l
            d_h = jnp.sum(d_ref[...] * oh_l, axis=1, keepdims=True)
            dxs.append(dxd * dt_h + dy_h * d_h)
            dd_ref[...] += oh_l * jnp.sum(jnp.sum(dy_h * xs_h, axis=1, keepdims=True), axis=0, keepdims=True)
        dxs_ref[...] = jnp.concatenate(dxs, axis=1)
        db_ref[...] = dB
        dc_ref[...] = dC

        @pl.when(g == GROUPS - 1)
        def _():
            a_row = -jnp.exp(alog_ref[...])
            da = jnp.dot((row <= col).astype(f32), dacum_s[...], precision=lax.Precision.HIGHEST, preferred_element_type=f32)
            da_s[...] += jnp.sum(da * dt_s[...], axis=0, keepdims=True)
            z = dtp_ref[...] + bias_ref[...]
            ddt_pre = (ddt_s[...] + da * a_row) * _sigmoid(z)
            ddt_ref[...] = ddt_pre.astype(ddt_ref.dtype)
            dbias_ref[...] += jnp.sum(ddt_pre, axis=0, keepdims=True)

            @pl.when(c == NCHUNK - 1)
            def _():
                dalog_ref[...] = da_s[...] * a_row

    par = pl.BlockSpec((1, LANES), lambda c, g: (0, 0))
    return _call(
        body, [xs, Bm, Cm, dt_pre, dt_bias, a_log, d_skip, states, dy], comm=comm, name="ssd_bwd", grid=(NCHUNK, GROUPS),
        in_specs=[pl.BlockSpec((CHUNK, HPG * HP), lambda c, g: (rev(c), g)),
                  pl.BlockSpec((CHUNK, NSTATE), lambda c, g: (rev(c), g)),
                  pl.BlockSpec((CHUNK, NSTATE), lambda c, g: (rev(c), g)),
                  pl.BlockSpec((CHUNK, LANES), lambda c, g: (rev(c), 0)), par, par, par,
                  pl.BlockSpec((1, HPG, NSTATE, HP), lambda c, g: (rev(c), g, 0, 0)),
                  pl.BlockSpec((CHUNK, HPG * HP), lambda c, g: (rev(c), g))],
        out_specs=[pl.BlockSpec((CHUNK, HPG * HP), lambda c, g: (rev(c), g)),
                   pl.BlockSpec((CHUNK, NSTATE), lambda c, g: (rev(c), g)),
                   pl.BlockSpec((CHUNK, NSTATE), lambda c, g: (rev(c), g)),
                   pl.BlockSpec((CHUNK, LANES), lambda c, g: (rev(c), 0)), par, par, par],
        out_shape=[jax.ShapeDtypeStruct((S, INNER), f32), jax.ShapeDtypeStruct((S, GROUPS * NSTATE), f32),
                   jax.ShapeDtypeStruct((S, GROUPS * NSTATE), f32), jax.ShapeDtypeStruct((S, LANES), MXU),
                   jax.ShapeDtypeStruct((1, LANES), f32), jax.ShapeDtypeStruct((1, LANES), f32),
                   jax.ShapeDtypeStruct((1, LANES), f32)],
        scratch=[pltpu.VMEM((HEADS, NSTATE, HP), f32), pltpu.VMEM((CHUNK, LANES), f32),
                 pltpu.VMEM((CHUNK, LANES), f32), pltpu.VMEM((LANES, CHUNK), f32),
                 pltpu.VMEM((CHUNK, LANES), f32), pltpu.VMEM((CHUNK, LANES), f32), pltpu.VMEM((1, LANES), f32)],
        sem=("arbitrary", "arbitrary"))


ATT_STACK_FWD, ATT_STACK_BWD = 4, 2


def _attn_kv(kp, kc, vp, vc, hk):
    sl = slice(AH * hk, AH * (hk + 1))
    return (jnp.concatenate([kp[:, sl], kc[:, sl]], axis=0).astype(MXU),
            jnp.concatenate([vp[:, sl], vc[:, sl]], axis=0).astype(MXU))


def _stack_heads(x, heads):
    return jnp.concatenate([x[:, AH * h:AH * (h + 1)] for h in heads], axis=0)


def _attn_block(n, q, kb, sinks, heads):
    rows = len(heads) * WIN
    qi = lax.broadcasted_iota(jnp.int32, (rows, 2 * WIN), 0) & (WIN - 1)
    ki = lax.broadcasted_iota(jnp.int32, (rows, 2 * WIN), 1)
    rel = qi + WIN - ki
    mask = (rel >= 0) & (rel < WIN) & ((ki >= WIN) | (n > 0))
    qg = _stack_heads(q, heads).astype(MXU)
    s = lax.dot_general(qg, kb, (((1,), (1,)), ((), ())), preferred_element_type=f32) * (AH ** -0.5)
    s = jnp.where(mask, s, -1e30)
    sink = jnp.concatenate([jnp.broadcast_to(sinks[:, h:h + 1], (WIN, 1)) for h in heads], axis=0)
    m = jnp.maximum(jnp.max(s, axis=1, keepdims=True), sink)
    p = jnp.exp(s - m)
    ps = jnp.exp(sink - m)
    inv = 1.0 / (jnp.sum(p, axis=1, keepdims=True) + ps)
    return qg, p * inv, ps * inv


def _head_blocks(hk, stack):
    return [list(range(QPK * hk + i, QPK * hk + i + stack)) for i in range(0, QPK, stack)]


def _kv_specs():
    prev = lambda n: (jnp.maximum(n - 1, 0), 0)
    cur = lambda n: (n, 0)
    w = KVH * AH
    return [pl.BlockSpec((WIN, w), prev), pl.BlockSpec((WIN, w), cur), pl.BlockSpec((WIN, w), prev), pl.BlockSpec((WIN, w), cur)]


def attn_fwd(q, k, v, sinks, comm=None):
    def body(q_ref, kp_ref, kc_ref, vp_ref, vc_ref, s_ref, o_ref):
        n = pl.program_id(0)
        q_, kp, kc, vp, vc, sk = q_ref[...], kp_ref[...], kc_ref[...], vp_ref[...], vc_ref[...], s_ref[...]
        outs = []
        for hk in range(KVH):
            kb, vb = _attn_kv(kp, kc, vp, vc, hk)
            for heads in _head_blocks(hk, ATT_STACK_FWD):
                _, pr, _ = _attn_block(n, q_, kb, sk, heads)
                o = jnp.dot(pr.astype(MXU), vb, preferred_element_type=f32)
                outs += [o[WIN * i:WIN * (i + 1)] for i in range(len(heads))]
        o_ref[...] = jnp.concatenate(outs, axis=1)

    return _call(
        body, [q, k, k, v, v, sinks], comm=comm, name="attn_fwd", grid=(NBLK,),
        in_specs=[pl.BlockSpec((WIN, D), lambda n: (n, 0))] + _kv_specs() + [pl.BlockSpec((1, QH), lambda n: (0, 0))],
        out_specs=[pl.BlockSpec((WIN, D), lambda n: (n, 0))],
        out_shape=[jax.ShapeDtypeStruct((S, D), f32)], sem=("parallel",))[0]


def attn_bwd(q, k, v, sinks, dout, comm=None):
    def body(q_ref, kp_ref, kc_ref, vp_ref, vc_ref, s_ref, do_ref, dq_ref, dkp_ref, dkc_ref, dvp_ref, dvc_ref, ds_ref):
        n = pl.program_id(0)

        @pl.when(n == 0)
        def _():
            ds_ref[...] = jnp.zeros_like(ds_ref)

        q_, kp, kc, vp, vc, sk, do = q_ref[...], kp_ref[...], kc_ref[...], vp_ref[...], vc_ref[...], s_ref[...], do_ref[...]
        lane = lax.broadcasted_iota(jnp.int32, (1, QH), 1)
        nt = (((1,), (1,)), ((), ()))
        tn = (((0,), (0,)), ((), ()))
        dqs, dkps, dkcs, dvps, dvcs = [], [], [], [], []
        dsink = jnp.zeros((1, QH), f32)
        for hk in range(KVH):
            kb, vb = _attn_kv(kp, kc, vp, vc, hk)
            dkb, dvb = jnp.zeros((2 * WIN, AH), f32), jnp.zeros((2 * WIN, AH), f32)
            for heads in _head_blocks(hk, ATT_STACK_BWD):
                qg, pr, prs = _attn_block(n, q_, kb, sk, heads)
                dog = _stack_heads(do, heads).astype(MXU)
                dp = lax.dot_general(dog, vb, nt, preferred_element_type=f32)
                dvb = dvb + lax.dot_general(pr.astype(MXU), dog, tn, preferred_element_type=f32)
                delta = jnp.sum(pr * dp, axis=1, keepdims=True)
                ds = (pr * (dp - delta)).astype(MXU)
                dsk = -prs * delta
                for i, h in enumerate(heads):
                    dsink = dsink + jnp.sum(dsk[WIN * i:WIN * (i + 1)], axis=0, keepdims=True) * (lane == h).astype(f32)
                dqg = jnp.dot(ds, kb, preferred_element_type=f32) * (AH ** -0.5)
                dkb = dkb + lax.dot_general(ds, qg, tn, preferred_element_type=f32) * (AH ** -0.5)
                dqs += [dqg[WIN * i:WIN * (i + 1)] for i in range(len(heads))]
            dkps.append(dkb[:WIN])
            dkcs.append(dkb[WIN:])
            dvps.append(dvb[:WIN])
            dvcs.append(dvb[WIN:])
        dq_ref[...] = jnp.concatenate(dqs, axis=1)
        dkp_ref[...] = jnp.concatenate(dkps, axis=1)
        dkc_ref[...] = jnp.concatenate(dkcs, axis=1)
        dvp_ref[...] = jnp.concatenate(dvps, axis=1)
        dvc_ref[...] = jnp.concatenate(dvcs, axis=1)
        ds_ref[...] += dsink

    w = KVH * AH
    blk = lambda width: pl.BlockSpec((WIN, width), lambda n: (n, 0))
    return _call(
        body, [q, k, k, v, v, sinks, dout], comm=comm, name="attn_bwd", grid=(NBLK,),
        in_specs=[blk(D)] + _kv_specs() + [pl.BlockSpec((1, QH), lambda n: (0, 0)), blk(D)],
        out_specs=[blk(D), blk(w), blk(w), blk(w), blk(w), pl.BlockSpec((1, QH), lambda n: (0, 0))],
        out_shape=[jax.ShapeDtypeStruct((S, D), f32)] + [jax.ShapeDtypeStruct((S, w), f32)] * 4 + [jax.ShapeDtypeStruct((1, QH), f32)],
        sem=("arbitrary",))


def kv_bwd(kv, pos, inv_freq, k_norm, dkp, dkc, dvp, dvc):
    w = KVH * AH

    def body(kv_ref, pos_ref, if_ref, g_ref, dkp_ref, dkc_ref, dvp_ref, dvc_ref, o_ref, dg_ref, db_ref):
        n = pl.program_id(0)

        @pl.when(n == 0)
        def _():
            dg_ref[...] = jnp.zeros_like(dg_ref)
            db_ref[...] = jnp.zeros_like(db_ref)

        inside = (n < NBLK - 1).astype(f32)
        dk = dkc_ref[...] + inside * dkp_ref[...]
        dv = dvc_ref[...] + inside * dvp_ref[...]
        cos, sin = _rope_tables(pos_ref[...], if_ref[...])
        dkpre, dg = _headnorm_rope_bwd(kv_ref[...], g_ref[...], cos, sin, dk, KVH)
        dkv = jnp.concatenate([dkpre, dv], axis=1)
        o_ref[...] = dkv.astype(o_ref.dtype)
        dg_ref[...] += dg
        db_ref[...] += jnp.sum(dkv, axis=0, keepdims=True)

    nxt = lambda n: (jnp.minimum(n + 1, NBLK - 1), 0)
    cur = lambda n: (n, 0)
    const = lambda n: (0, 0)
    return pl.pallas_call(
        body, name="kv_bwd", grid=(NBLK,),
        in_specs=[pl.BlockSpec((WIN, w), cur), pl.BlockSpec((WIN, 1), cur), pl.BlockSpec((1, AH // 2), const),
                  pl.BlockSpec((1, AH), const), pl.BlockSpec((WIN, w), nxt), pl.BlockSpec((WIN, w), cur),
                  pl.BlockSpec((WIN, w), nxt), pl.BlockSpec((WIN, w), cur)],
        out_specs=[pl.BlockSpec((WIN, 2 * w), cur), pl.BlockSpec((1, AH), const), pl.BlockSpec((1, 2 * w), const)],
        out_shape=[jax.ShapeDtypeStruct((S, 2 * w), MXU), jax.ShapeDtypeStruct((1, AH), f32), jax.ShapeDtypeStruct((1, 2 * w), f32)],
        compiler_params=_cparams(("arbitrary",)),
    )(kv, pos, inv_freq, k_norm, dkp, dkc, dvp, dvc)


def _adam_math(w, g, m, v):
    m = ADAM_B1 * m + (1.0 - ADAM_B1) * g
    v = ADAM_B2 * v + (1.0 - ADAM_B2) * (g * g)
    m_hat = m / (1.0 - ADAM_B1 ** ADAM_STEP)
    v_hat = v / (1.0 - ADAM_B2 ** ADAM_STEP)
    return -ADAM_LR * (m_hat / (jnp.sqrt(v_hat) + ADAM_EPS) + ADAM_WD * w), m, v


def adamw(name, w, g, m, v):
    R, C = w.shape
    tr = _pick(R, (256, 128, 64, 32, 16, 8))
    tc = C if tr < R or C % 256 else 256

    def body(w_ref, g_ref, m_ref, v_ref, d_ref, nm_ref, nv_ref):
        d_ref[...], nm_ref[...], nv_ref[...] = _adam_math(w_ref[...], g_ref[...], m_ref[...], v_ref[...])

    spec = pl.BlockSpec((tr, tc), lambda i, j: (i, j))
    return pl.pallas_call(
        body, name=name, grid=(R // tr, C // tc), in_specs=[spec] * 4, out_specs=[spec] * 3,
        out_shape=[jax.ShapeDtypeStruct((R, C), f32)] * 3, compiler_params=_cparams(("parallel", "parallel")),
    )(w, g, m, v)


def _me():
    return lax.axis_index("x"), lax.axis_index("y"), lax.axis_index("c")


def gather_comm(xs):
    n = len(xs)

    def parts(x_refs, o_refs, sems):
        send_sems, recv_sems, local_sems = sems
        x, y, c = _me()
        me, sibling = (x, y, c), (x, y, 1 - c)
        chips = [(1 - x, y), (x, 1 - y), (1 - x, 1 - y)]

        def copy(a, k, block, to, src=None):
            dst = o_refs[a].at[4 * block[0] + 2 * block[1] + block[2]]
            return pltpu.make_async_remote_copy(
                src_ref=dst if src is None else src, dst_ref=dst,
                send_sem=send_sems.at[7 * a + k], recv_sem=recv_sems.at[7 * a + k], device_id=to, device_id_type=MESH)

        mine = [pltpu.make_async_copy(x_refs[a], o_refs[a].at[4 * x + 2 * y + c], local_sems.at[a]) for a in range(n)]
        first = []
        for a in range(n):
            first.append(copy(a, 0, me, sibling, src=x_refs[a]))
            first += [copy(a, 1 + j, me, (*chip, c), src=x_refs[a]) for j, chip in enumerate(chips)]
        return copy, mine, first, me, sibling, chips, c

    def start(x_refs, o_refs, sems):
        _, mine, first, *_ = parts(x_refs, o_refs, sems)
        for cp in mine + first:
            cp.start()

    def finish(x_refs, o_refs, sems):
        copy, mine, first, me, sibling, chips, c = parts(x_refs, o_refs, sems)
        passed = []
        for j, chip in enumerate(chips):
            for a in range(n):
                copy(a, 1 + j, (*chip, c), me).wait_recv()
                cp = copy(a, 4 + j, (*chip, c), sibling)
                cp.start()
                passed.append(cp)
        for a in range(n):
            copy(a, 0, sibling, me).wait_recv()
            for j, chip in enumerate(chips):
                copy(a, 4 + j, (*chip, 1 - c), me).wait_recv()
        for cp in first + passed:
            cp.wait_send()
        for cp in mine:
            cp.wait()

    return Comm(xs, [jax.ShapeDtypeStruct((N_DEV,) + a.shape, a.dtype) for a in xs],
                [pltpu.SemaphoreType.DMA((7 * n,)), pltpu.SemaphoreType.DMA((7 * n,)), pltpu.SemaphoreType.DMA((n,))], start, finish)


def run_comm(name, comm):
    _call(lambda: None, [], name=name, grid=(1,), in_specs=[], out_specs=[], out_shape=[], comm=comm)
    return comm.results


def sibling_comm(gs):
    n = len(gs)

    def copies(g_refs, o_refs, sems):
        x, y, c = _me()
        return [pltpu.make_async_remote_copy(
            src_ref=g_refs[a].at[:, 1 - c], dst_ref=o_refs[a], send_sem=sems[0].at[a], recv_sem=sems[1].at[a],
            device_id=(x, y, 1 - c), device_id_type=MESH) for a in range(n)]

    def start(g_refs, o_refs, sems):
        for cp in copies(g_refs, o_refs, sems):
            cp.start()

    def finish(g_refs, o_refs, sems):
        for cp in copies(g_refs, o_refs, sems):
            cp.wait()

    return Comm(gs, [jax.ShapeDtypeStruct((4,) + g.shape[2:], g.dtype) for g in gs],
                [pltpu.SemaphoreType.DMA((n,)), pltpu.SemaphoreType.DMA((n,))], start, finish)


def chip_comm(ts):
    n = len(ts)

    def copies(t_refs, o_refs, sems):
        x, y, c = _me()
        chips = [(1 - x, y), (x, 1 - y), (1 - x, 1 - y)]
        return [pltpu.make_async_remote_copy(
            src_ref=t_refs[a].at[2 * px + py], dst_ref=o_refs[a].at[j],
            send_sem=sems[0].at[3 * a + j], recv_sem=sems[1].at[3 * a + j],
            device_id=(px, py, c), device_id_type=MESH) for j, (px, py) in enumerate(chips) for a in range(n)]

    def start(t_refs, o_refs, sems):
        for cp in copies(t_refs, o_refs, sems):
            cp.start()

    def finish(t_refs, o_refs, sems):
        for cp in copies(t_refs, o_refs, sems):
            cp.wait()

    return Comm(ts, [jax.ShapeDtypeStruct((3,) + t.shape[1:], t.dtype) for t in ts],
                [pltpu.SemaphoreType.DMA((3 * n,)), pltpu.SemaphoreType.DMA((3 * n,))], start, finish)


def _row_tile(rows):
    return _pick(rows, (512, 304, 256, 128))


def pair_add(name, g, r):
    _, _, R, C = g.shape
    tr = _row_tile(R)

    def body(c_ref, g_ref, r_ref, o_ref):
        o_ref[0] = (g_ref[0, 0].astype(f32) + r_ref[0].astype(f32)).astype(o_ref.dtype)

    return pl.pallas_call(
        body, name=name,
        grid_spec=pltpu.PrefetchScalarGridSpec(
            num_scalar_prefetch=1, grid=(4, R // tr),
            in_specs=[pl.BlockSpec((1, 1, tr, C), lambda p, i, c: (p, c[0], i, 0)),
                      pl.BlockSpec((1, tr, C), lambda p, i, c: (p, i, 0))],
            out_specs=pl.BlockSpec((1, tr, C), lambda p, i, c: (p, i, 0))),
        out_shape=jax.ShapeDtypeStruct((4, R, C), g.dtype),
        compiler_params=_cparams(("parallel", "parallel")),
    )(lax.axis_index("c").reshape(1).astype(jnp.int32), g, r)


def _sum_of_four(t_ref, r_ref):
    return ((t_ref[0].astype(f32) + r_ref[0].astype(f32)) + r_ref[1].astype(f32)) + r_ref[2].astype(f32)


def _my_chip():
    return (2 * lax.axis_index("x") + lax.axis_index("y")).reshape(1).astype(jnp.int32)


def final_adamw(name, t, r, w, m, v):
    _, R, C = t.shape
    tr = _pick(R, (256, 128, 64, 32, 16))
    tc = C if tr < R or C % 256 else 256

    def body(p_ref, t_ref, r_ref, w_ref, m_ref, v_ref, g_ref, d_ref, nm_ref, nv_ref):
        g_ = _sum_of_four(t_ref, r_ref)
        g_ref[...] = g_
        d_ref[...], nm_ref[...], nv_ref[...] = _adam_math(w_ref[...], g_, m_ref[...], v_ref[...])

    flat = pl.BlockSpec((tr, tc), lambda i, j, p: (i, j))
    return pl.pallas_call(
        body, name=name,
        grid_spec=pltpu.PrefetchScalarGridSpec(
            num_scalar_prefetch=1, grid=(R // tr, C // tc),
            in_specs=[pl.BlockSpec((1, tr, tc), lambda i, j, p: (p[0], i, j)),
                      pl.BlockSpec((3, tr, tc), lambda i, j, p: (0, i, j)), flat, flat, flat],
            out_specs=[flat] * 4),
        out_shape=[jax.ShapeDtypeStruct((R, C), f32)] * 4,
        compiler_params=_cparams(("parallel", "parallel")),
    )(_my_chip(), t, r, w, m, v)


def final_adamw_layers(name, t0, r0, t1, r1, w, m, v):
    _, R, C = t0.shape
    tr = _pick(R, (256, 128, 64, 32, 16))

    def body(p_ref, t0_ref, r0_ref, t1_ref, r1_ref, w_ref, m_ref, v_ref, g_ref, d_ref, nm_ref, nv_ref):
        g_ = jnp.where(pl.program_id(0) == 0, _sum_of_four(t0_ref, r0_ref), _sum_of_four(t1_ref, r1_ref))
        g_ref[0] = g_
        d_ref[0], nm_ref[0], nv_ref[0] = _adam_math(w_ref[0], g_, m_ref[0], v_ref[0])

    mine = pl.BlockSpec((1, tr, C), lambda l, i, p: (p[0], i, 0))
    theirs = pl.BlockSpec((3, tr, C), lambda l, i, p: (0, i, 0))
    layer = pl.BlockSpec((1, tr, C), lambda l, i, p: (l, i, 0))
    return pl.pallas_call(
        body, name=name,
        grid_spec=pltpu.PrefetchScalarGridSpec(
            num_scalar_prefetch=1, grid=(2, R // tr),
            in_specs=[mine, theirs, mine, theirs, layer, layer, layer], out_specs=[layer] * 4),
        out_shape=[jax.ShapeDtypeStruct((2, R, C), f32)] * 4,
        compiler_params=_cparams(("parallel", "parallel")),
    )(_my_chip(), t0, r0, t1, r1, w, m, v)


class ReduceScatter:
    def __init__(self, tag, keys, grads):
        self.tag, self.keys, self.grads = tag, keys, grads
        self.send = [g.reshape((4, 2, g.shape[0] // N_DEV) + g.shape[1:]) for g in grads]

    def sibling(self):
        self.c1 = sibling_comm(self.send)
        return self.c1

    def chips(self):
        self.pairs = [pair_add(f"rs_pair_add_{self.tag}{i}", g, r) for i, (g, r) in enumerate(zip(self.send, self.c1.results))]
        self.c2 = chip_comm(self.pairs)
        return self.c2

    def parts(self):
        return {k: (t, r) for k, t, r in zip(self.keys, self.pairs, self.c2.results)}


IN_ROWS = {"z": (0, 2048), "xs": (2048, 4096), "B": (4096, 5120), "C": (5120, 6144)}
IN_COLS = 2 * INNER + 2 * GROUPS * NSTATE + HEADS


def sum_devices(g):
    def body(g_ref, o_ref):
        acc = g_ref[0]
        for i in range(1, N_DEV):
            acc = acc + g_ref[i]
        o_ref[...] = acc

    return pl.pallas_call(body, name="sum_devices", out_shape=jax.ShapeDtypeStruct(g.shape[1:], f32),
                          compiler_params=_cparams())(g)


def _pack(parts, unit, dtype, lead=()):
    flat = jnp.concatenate([p.reshape(lead + (-1,)).astype(dtype) for p in parts], axis=-1)
    n = flat.shape[-1]
    rows = -(-n // (unit * PACK_W)) * unit
    flat = jnp.pad(flat, [(0, 0)] * len(lead) + [(0, rows * PACK_W - n)])
    return flat.reshape(lead + (rows, PACK_W))


def _unpack(buf, shapes, lead=()):
    flat = buf.reshape(lead + (-1,))
    out, off = [], 0
    for shp in shapes:
        n = math.prod(shp)
        out.append(flat[..., off:off + n].reshape(lead + tuple(shp)))
        off += n
    return out


def _pad_lanes(a):
    return jnp.pad(a, [(0, 0)] * (a.ndim - 1) + [(0, LANES - a.shape[-1])])


_NN = (((1,), (0,)), ((), ()))
_NT = (((1,), (1,)), ((), ()))


def _mm(a, b, dims):
    return lax.dot_general(a.astype(MXU), b.astype(MXU), dims, preferred_element_type=f32)


def _ffn_fwd(tag, x, h, w_inT, conv_w, conv_b, mid_comm=None):
    ct, nblk = FFN_CT, FFN // FFN_CT

    def body(h_ref, wg_ref, wv_ref, cw_ref, cb_ref, gp_ref, v_ref, a_ref):
        h_ = h_ref[...]
        gp, v_ = _mm(h_, wg_ref[...], _NT), _mm(h_, wv_ref[...], _NT)
        gp_ref[...] = gp
        v_ref[...] = v_
        a_ref[...] = (_silu(_conv(gp, cw_ref[...], cb_ref[...])) * v_).astype(a_ref.dtype)

    col = pl.BlockSpec((S, ct), lambda j: (0, j))
    gate_pre, val, act = _call(
        body, [h, w_inT, w_inT, conv_w, conv_b], comm=mid_comm, name=f"{tag}_in", grid=(nblk,),
        in_specs=[pl.BlockSpec((S, D), lambda j: (0, 0)), pl.BlockSpec((ct, D), lambda j: (j, 0)),
                  pl.BlockSpec((ct, D), lambda j: (nblk + j, 0)), pl.BlockSpec((CONV_F, ct), lambda j: (0, j)),
                  pl.BlockSpec((1, ct), lambda j: (0, j))],
        out_specs=[col, col, col],
        out_shape=[jax.ShapeDtypeStruct((S, FFN), f32), jax.ShapeDtypeStruct((S, FFN), f32), jax.ShapeDtypeStruct((S, FFN), MXU)],
        sem=("parallel",))
    return act, (x, h, gate_pre, val, act)


FFN_CT = 256
CONV_CT = 256


def _proj_conv(name, h, wT, row0, cw, cb, comm=None):
    C, ct = cw.shape[1], CONV_CT

    def body(h_ref, w_ref, cw_ref, cb_ref, p_ref, c_ref):
        p = _mm(h_ref[...], w_ref[...], _NT)
        p_ref[...] = p
        c_ref[...] = _silu(_conv(p, cw_ref[...], cb_ref[...]))

    col = pl.BlockSpec((S, ct), lambda j: (0, j))
    return _call(
        body, [h, wT, cw, cb], comm=comm, name=name, grid=(C // ct,),
        in_specs=[pl.BlockSpec((S, D), lambda j: (0, 0)), pl.BlockSpec((ct, D), lambda j: (row0 // ct + j, 0)),
                  pl.BlockSpec((CONV_A, ct), lambda j: (0, j)), pl.BlockSpec((1, ct), lambda j: (0, j))],
        out_specs=[col, col], out_shape=[jax.ShapeDtypeStruct((S, C), f32)] * 2, sem=("parallel",))


def _dconv_wgrad(name, pre, dconv, cw, cb, h):
    C, ct = cw.shape[1], CONV_CT

    def body(p_ref, do_ref, cw_ref, cb_ref, h_ref, dp_ref, g_ref, dw_ref, db_ref):
        p_, w_ = p_ref[...], cw_ref[...]
        taps = _taps(p_, CONV_A)
        dx, dw, db = _conv_bwd(p_, w_, do_ref[...] * _dsilu(_conv(p_, w_, cb_ref[...], taps)), taps)
        dpm = dx.astype(MXU)
        dp_ref[...] = dpm
        g_ref[...] = lax.dot_general(dpm, h_ref[...].astype(MXU), (((0,), (0,)), ((), ())),
                                     preferred_element_type=f32).astype(g_ref.dtype)
        dw_ref[...] = dw
        db_ref[...] = db

    col = pl.BlockSpec((S, ct), lambda j: (0, j))
    return _call(
        body, [pre, dconv, cw, cb, h], name=name, grid=(C // ct,),
        in_specs=[col, col, pl.BlockSpec((CONV_A, ct), lambda j: (0, j)), pl.BlockSpec((1, ct), lambda j: (0, j)),
                  pl.BlockSpec((S, D), lambda j: (0, 0))],
        out_specs=[col, pl.BlockSpec((ct, D), lambda j: (j, 0)), pl.BlockSpec((CONV_A, ct), lambda j: (0, j)),
                   pl.BlockSpec((1, ct), lambda j: (0, j))],
        out_shape=[jax.ShapeDtypeStruct((S, C), MXU), jax.ShapeDtypeStruct((C, D), MXU),
                   jax.ShapeDtypeStruct((CONV_A, C), f32), jax.ShapeDtypeStruct((1, C), f32)],
        sem=("parallel",))


def _ffn_mid_bwd(name, dout, w_down, gate_pre, val, conv_w, conv_b, comm=None):
    ct = FFN_CT

    def body(do_ref, wd_ref, gp_ref, v_ref, w_ref, b_ref, dgv_ref, dw_ref, db_ref, dob_s):
        @pl.when(pl.program_id(0) == 0)
        def _():
            dob_s[...] = do_ref[...].astype(MXU)

        da = _mm(dob_s[...], wd_ref[...], _NT)
        gp, v_, w_ = gp_ref[...], v_ref[...], w_ref[...]
        taps = _taps(gp, CONV_F)
        gate = _conv(gp, w_, b_ref[...], taps)
        sg = _sigmoid(gate)
        dgp, dw, db = _conv_bwd(gp, w_, da * v_ * (sg * (1.0 + gate * (1.0 - sg))), taps)
        dgv_ref[0] = dgp.astype(dgv_ref.dtype)
        dgv_ref[1] = (da * (gate * sg)).astype(dgv_ref.dtype)
        dw_ref[...] = dw
        db_ref[...] = db

    col = pl.BlockSpec((S, ct), lambda j: (0, j))
    return _call(
        body, [dout, w_down, gate_pre, val, conv_w, conv_b], comm=comm, name=name, grid=(FFN // ct,),
        in_specs=[pl.BlockSpec((S, D), lambda j: (0, 0)), pl.BlockSpec((ct, D), lambda j: (j, 0)), col, col,
                  pl.BlockSpec((CONV_F, ct), lambda j: (0, j)), pl.BlockSpec((1, ct), lambda j: (0, j))],
        out_specs=[pl.BlockSpec((2, S, ct), lambda j: (0, 0, j)), pl.BlockSpec((CONV_F, ct), lambda j: (0, j)),
                   pl.BlockSpec((1, ct), lambda j: (0, j))],
        out_shape=[jax.ShapeDtypeStruct((2, S, FFN), MXU), jax.ShapeDtypeStruct((CONV_F, FFN), f32), jax.ShapeDtypeStruct((1, FFN), f32)],
        scratch=[pltpu.VMEM((S, D), MXU)], sem=("arbitrary",))


def _ffn_bwd(tag, layer, saved, norm_g, w_inT, conv_w, conv_b, w_down, dout, mid_comm=None):
    x, h, gate_pre, val, act = saved
    g_down = matmul(f"{tag}_wdown", act, dout, "tn", out_dtype=MXU)
    dgv, g_cw, g_cb = _ffn_mid_bwd(f"{tag}_dmid", dout, w_down, gate_pre, val, conv_w, conv_b, comm=mid_comm)
    g_inT = matmul_tn_stacked(f"{tag}_win", dgv, h, MXU)

    def din_fn(dg_, dv_, x_, do_, g_, wT):
        dx, dg = _rms_bwd(x_, g_, _mm(dg_, wT[:FFN], _NN) + _mm(dv_, wT[FFN:], _NN))
        return do_ + dx, dg

    rs = ReduceScatter(tag, (f"f_inT{layer}", f"f_down{layer}"), [g_inT, g_down])
    dx, g_norm = rowwise(f"{tag}_din", din_fn, [(dgv, 0), (dgv, 1), x, dout], [norm_g, w_inT], [(D, f32)], [(1, D)],
                         comm=rs.sibling())
    return dx, {f"f_norm{layer}": g_norm, f"f_conv_w{layer}": g_cw, f"f_conv_b{layer}": g_cb}, rs


def _land(W, keys, comm):
    for k, g in zip(keys, comm.results):
        W[k] = g.reshape(-1, g.shape[2])


def _local_step(x, pos, tgt, W, shards):
    G = {}
    gather = lambda *keys: gather_comm([shards[k] for k in keys])
    inv_freq = (ROPE_THETA ** (-jnp.arange(AH // 2, dtype=f32) / (AH // 2))).reshape(1, AH // 2)

    def in_fn(x_, g_, wT, wdtT):
        h_ = _rms_fwd(x_, g_).astype(MXU)
        return h_, _mm(h_, wT[slice(*IN_ROWS["z"])], _NT), _mm(h_, wdtT, _NT)

    h0, z, dt_pre = rowwise("a_in", in_fn, [x], [W["a_norm"], W["inT"], W["in_dtT"]], [(D, MXU), (INNER, f32), (LANES, f32)])
    pre, conv = {}, {}
    early = {"xs": ("a_out",), "B": (), "C": ()}
    for k in ("xs", "B", "C"):
        c = gather(*early[k]) if early[k] else None
        pre[k], conv[k] = _proj_conv(f"a_in_{k}", h0, W["inT"], IN_ROWS[k][0], W[f"cw_{k}"], W[f"cb_{k}"], comm=c)
        if c is not None:
            _land(W, early[k], c)
    c = gather("f_inT0", "f_down0")
    y, states = ssd_fwd(conv["xs"], conv["B"], conv["C"], dt_pre, W["dt_bias"], W["A_log"], W["D"], comm=c)
    _land(W, ("f_inT0", "f_down0"), c)

    def gate_norm(y_, z_, g_):
        yg = y_ * _silu(z_)
        w = INNER // GROUPS
        return (jnp.concatenate([_rms_fwd(yg[:, w * i:w * (i + 1)], g_[:, w * i:w * (i + 1)]) for i in range(GROUPS)], axis=1),)

    def out_fn(y_, z_, x_, g_, w_, gf_):
        (gn_,) = gate_norm(y_, z_, g_)
        gn_ = gn_.astype(MXU)
        x1_ = x_ + _mm(gn_, w_, _NN)
        return gn_, x1_, _rms_fwd(x1_, gf_)

    c = gather("f_down1")
    gn, x1, h1 = rowwise("a_out", out_fn, [y, z, x], [W["a_gnorm"], W["a_out"], W["f_norm0"]],
                         [(INNER, MXU), (D, f32), (D, MXU)], comm=c)
    _land(W, ("f_down1",), c)

    c = gather("w_kv", "w_q", "w_o")
    act0, ffn0 = _ffn_fwd("f0", x1, h1, W["f_inT0"], W["f_cw0"], W["f_cb0"], mid_comm=c)
    _land(W, ("w_kv", "w_q", "w_o"), c)
    kw = KVH * AH

    def qkv_fn(a_, x_, pos_, wd, gk, gb, wkv, bkv, wq, bq, if_, kn, qn):
        x2_ = x_ + _mm(a_, wd, _NN)
        kvn_, h2_ = _rms_fwd(x2_, gk).astype(MXU), _rms_fwd(x2_, gb).astype(MXU)
        kv_, qp_ = _mm(kvn_, wkv, _NN) + bkv, _mm(h2_, wq, _NN) + bq
        cos, sin = _rope_tables(pos_, if_)
        return (x2_, kvn_, h2_, kv_, qp_, _headnorm_rope_fwd(kv_[:, :kw], kn, cos, sin, KVH), kv_[:, kw:],
                _headnorm_rope_fwd(qp_, qn, cos, sin, QH))

    x2, kvn, h2, kv, q_pre, k_rot, v_val, q = rowwise(
        "f0_down_qkv", qkv_fn, [act0, x1, pos],
        [W["f_down0"], W["kv_norm"], W["b_norm"], W["w_kv"], W["b_kv"], W["w_q"], W["b_q"], inv_freq, W["k_norm"], W["q_norm"]],
        [(D, f32), (D, MXU), (D, MXU), (2 * kw, f32), (D, f32), (kw, f32), (kw, f32), (D, f32)])
    c = gather("f_inT1")
    att = attn_fwd(q, k_rot, v_val, W["sinks"], comm=c)
    _land(W, ("f_inT1",), c)
    def o_fn(att_, x_, w_, b_, gf_):
        x3_ = x_ + _mm(att_, w_, _NN) + b_
        return x3_, _rms_fwd(x3_, gf_)

    x3, h3 = rowwise("o_proj", o_fn, [att, x2], [W["w_o"], W["b_o"], W["f_norm1"]], [(D, f32), (D, MXU)])

    act1, ffn1 = _ffn_fwd("f1", x3, h3, W["f_inT1"], W["f_cw1"], W["f_cb1"])

    def loss_fn(a_, x_, t_, w_):
        diff = x_ + _mm(a_, w_, _NN) - t_
        rows = jnp.sum(diff * diff, axis=1, keepdims=True) * (0.5 / D)
        return diff * (1.0 / D), jnp.sum(rows, axis=0, keepdims=True)

    dx4, loss = rowwise("f1_down_loss", loss_fn, [act1, x3, tgt], [W["f_down1"]], [(D, f32)], [(1, 1)])

    dx3, g, rs_f1 = _ffn_bwd("f1", 1, ffn1, W["f_norm1"], W["f_inT1"], W["f_cw1"], W["f_cb1"], W["f_down1"], dx4)
    G.update(g)

    datt = matmul("o_dproj", dx3, W["w_o"], "nt")
    g_wo = matmul("o_wproj", att, dx3, "tn", out_dtype=MXU)
    dq, dkp, dkc, dvp, dvc, G["sinks"] = attn_bwd(q, k_rot, v_val, W["sinks"], datt, comm=rs_f1.chips())

    def q_bwd(q_, pos_, dq_, dx_, if_, g_):
        cos, sin = _rope_tables(pos_, if_)
        dqp, dg = _headnorm_rope_bwd(q_, g_, cos, sin, dq_, QH)
        return dqp, dg, jnp.sum(dqp, axis=0, keepdims=True), jnp.sum(dx_, axis=0, keepdims=True)

    dq_pre, G["q_norm"], G["b_q"], G["b_o"] = rowwise("q_drope", q_bwd, [q_pre, pos, dq, dx3], [inv_freq, W["q_norm"]],
                                                      [(D, MXU)], [(1, AH), (1, D), (1, D)])
    g_wq = matmul("q_wproj", h2, dq_pre, "tn", out_dtype=MXU)
    dkv, G["k_norm"], G["b_kv"] = kv_bwd(kv, pos, inv_freq, W["k_norm"], dkp, dkc, dvp, dvc)
    g_wkv = matmul("kv_wproj", kvn, dkv, "tn", out_dtype=MXU)
    rs_att = ReduceScatter("att", ("w_kv", "w_q", "w_o"), [g_wkv, g_wq, g_wo])

    def x2_bwd(x_, dq_, dkv_, dx_, gb_, gk_, wq, wkv):
        d1, dgb = _rms_bwd(x_, gb_, _mm(dq_, wq, _NT))
        d2, dgk = _rms_bwd(x_, gk_, _mm(dkv_, wkv, _NT))
        return dx_ + d1 + d2, dgb, dgk

    dx2, G["b_norm"], G["kv_norm"] = rowwise("qkv_dproj", x2_bwd, [x2, dq_pre, dkv, dx3],
                                             [W["b_norm"], W["kv_norm"], W["w_q"], W["w_kv"]],
                                             [(D, f32)], [(1, D), (1, D)], comm=rs_att.sibling())

    dx1, g, rs_f0 = _ffn_bwd("f0", 0, ffn0, W["f_norm0"], W["f_inT0"], W["f_cw0"], W["f_cb0"], W["f_down0"], dx2,
                             mid_comm=rs_att.chips())
    G.update(g)

    rs_out = ReduceScatter("a_out", ("a_out",), [matmul("a_wout", gn, dx1, "tn", out_dtype=MXU)])

    def gate_norm_bwd(y_, z_, dx_, g_, w_out):
        dgn_ = _mm(dx_, w_out, _NT)
        w = INNER // GROUPS
        sg = _sigmoid(z_)
        sz = z_ * sg
        yg = y_ * sz
        parts, dgs = [], []
        for i in range(GROUPS):
            dseg, dg = _rms_bwd(yg[:, w * i:w * (i + 1)], g_[:, w * i:w * (i + 1)], dgn_[:, w * i:w * (i + 1)])
            parts.append(dseg)
            dgs.append(dg)
        dyg = jnp.concatenate(parts, axis=1)
        return dyg * sz, dyg * y_ * (sg * (1.0 + z_ * (1.0 - sg))), jnp.concatenate(dgs, axis=1)

    dy, dz, G["a_gnorm"] = rowwise("a_dout", gate_norm_bwd, [y, z, dx1], [W["a_gnorm"], W["a_out"]],
                                   [(INNER, f32), (INNER, MXU)], [(1, INNER)], comm=rs_out.sibling())
    dconv = {}
    dconv["xs"], dconv["B"], dconv["C"], ddt_pre, G["dt_bias"], G["A_log"], G["D"] = ssd_bwd(
        conv["xs"], conv["B"], conv["C"], dt_pre, W["dt_bias"], W["A_log"], W["D"], states, dy,
        comm=merge_comms([rs_f0.chips(), rs_out.chips()]))

    g_in, dpre = [matmul("a_win_z", dz, h0, "tn", out_dtype=MXU)], {}
    for k in ("xs", "B", "C"):
        dpre[k], g_k, G[f"cw_{k}"], G[f"cb_{k}"] = _dconv_wgrad(f"a_dconv_{k}", pre[k], dconv[k], W[f"cw_{k}"], W[f"cb_{k}"], h0)
        g_in.append(g_k)
    g_in.append(matmul("a_win_dt", ddt_pre, h0, "tn", out_dtype=MXU)[:HEADS])
    rs_in = ReduceScatter("a_in", ("inT",), [jnp.concatenate(g_in, axis=0)])
    run_comm("rs_in_sibling", rs_in.sibling())

    def x0_bwd(dz_, dxs_, db_, dc_, ddt_, x_, do_, g_, wT, wdtT):
        parts = zip((dz_, dxs_, db_, dc_), IN_ROWS.values())
        dh = sum(_mm(d_, wT[a:b], _NN) for d_, (a, b) in parts) + _mm(ddt_, wdtT, _NN)
        dx, dg = _rms_bwd(x_, g_, dh)
        return do_ + dx, dg

    dx, G["a_norm"] = rowwise("a_din", x0_bwd, [dz, dpre["xs"], dpre["B"], dpre["C"], ddt_pre, x, dx1],
                              [W["a_norm"], W["inT"], W["in_dtT"]], [(D, f32)], [(1, D)], comm=rs_in.chips())
    return loss, dx, G, [rs_f1, rs_att, rs_f0, rs_out, rs_in]


ROW_KEYS = ("inT", "a_out", "f_inT0", "f_down0", "w_kv", "w_q", "w_o", "f_inT1", "f_down1")


def _row_blocks(src):
    return {"inT": src["a_in_proj"][0].T, "a_out": src["a_out_proj"][0], "w_kv": src["w_kv"], "w_q": src["w_q"][0],
            "w_o": src["w_o"][0], "f_inT0": src["f_w_in"][0].T, "f_inT1": src["f_w_in"][1].T,
            "f_down0": src["f_w_down"][0], "f_down1": src["f_w_down"][1]}


def _from_row_blocks(rb):
    out = {"a_in_proj": rb["inT"].T[None], "a_out_proj": rb["a_out"][None], "w_kv": rb["w_kv"], "w_q": rb["w_q"][None],
           "w_o": rb["w_o"][None]}
    if "f_inT0" in rb:
        out["f_w_in"] = jnp.stack([rb["f_inT0"].T, rb["f_inT1"].T])
        out["f_w_down"] = jnp.stack([rb["f_down0"], rb["f_down1"]])
    return out


SMALL_SHARDED = ("a_norm", "a_conv_w", "a_conv_b", "a_gnorm", "f_conv_w")
REPLICATED = ("a_dt_bias", "a_A_log", "a_D", "kv_norm", "b_kv", "k_norm", "b_norm", "b_q", "q_norm", "sinks", "b_o",
              "f_norm", "f_conv_b")
ORDER = ("a_norm", "a_in_proj", "a_conv_w", "a_conv_b", "a_dt_bias", "a_A_log", "a_D", "a_gnorm", "a_out_proj", "kv_norm",
         "w_kv", "b_kv", "k_norm", "b_norm", "w_q", "b_q", "q_norm", "sinks", "w_o", "b_o", "f_norm", "f_w_in",
         "f_conv_w", "f_conv_b", "f_w_down")


def _gathered_to_whole(name, g):
    if name == "a_conv_w":
        return jnp.moveaxis(g[:, 0], 0, 1).reshape(g.shape[2], -1)
    if name in ("a_norm", "a_conv_b", "a_gnorm"):
        return g[:, 0].reshape(1, -1)
    if name == "f_conv_w":
        return jnp.moveaxis(g, 0, 2).reshape(g.shape[1], g.shape[2], -1)
    raise ValueError(name)


def _whole_to_shards(name, w):
    if name == "a_conv_w":
        return jnp.moveaxis(w.reshape(w.shape[0], N_DEV, -1), 1, 0)[:, None]
    if name in ("a_norm", "a_conv_b", "a_gnorm"):
        return w.reshape(N_DEV, 1, -1)
    if name == "f_conv_w":
        return jnp.moveaxis(w.reshape(w.shape[0], w.shape[1], N_DEV, -1), 2, 0)
    raise ValueError(name)


def _small_weights(whole):
    W = {}
    cw, cb = whole["a_conv_w"], whole["a_conv_b"]
    o = 0
    for k, n in (("xs", INNER), ("B", GROUPS * NSTATE), ("C", GROUPS * NSTATE)):
        W[f"cw_{k}"], W[f"cb_{k}"] = cw[:, o:o + n], cb[:, o:o + n]
        o += n
    W["a_norm"], W["a_gnorm"] = whole["a_norm"], whole["a_gnorm"]
    W["dt_bias"], W["A_log"], W["D"] = (_pad_lanes(whole[k]) for k in ("a_dt_bias", "a_A_log", "a_D"))
    W["kv_norm"], W["b_kv"], W["k_norm"] = whole["kv_norm"].reshape(1, -1), whole["b_kv"].reshape(1, -1), whole["k_norm"].reshape(1, -1)
    for k in ("b_norm", "b_q", "q_norm", "sinks", "b_o"):
        W[k] = whole[k]
    for i in range(2):
        W[f"f_norm{i}"] = whole["f_norm"][i:i + 1]
        W[f"f_cw{i}"], W[f"f_cb{i}"] = whole["f_conv_w"][i], whole["f_conv_b"][i:i + 1]
    return W


def _small_grads(G, shapes):
    nh = HEADS
    out = {
        "a_conv_w": jnp.concatenate([G["cw_xs"], G["cw_B"], G["cw_C"]], axis=1),
        "a_conv_b": jnp.concatenate([G["cb_xs"], G["cb_B"], G["cb_C"]], axis=1),
        "a_norm": G["a_norm"], "a_gnorm": G["a_gnorm"],
        "a_dt_bias": G["dt_bias"][:, :nh], "a_A_log": G["A_log"][:, :nh], "a_D": G["D"][:, :nh],
        "kv_norm": G["kv_norm"], "b_kv": G["b_kv"], "k_norm": G["k_norm"], "b_norm": G["b_norm"],
        "b_q": G["b_q"], "q_norm": G["q_norm"], "sinks": G["sinks"], "b_o": G["b_o"],
        "f_norm": jnp.concatenate([G["f_norm0"], G["f_norm1"]], axis=0),
        "f_conv_w": jnp.stack([G["f_conv_w0"], G["f_conv_w1"]]),
        "f_conv_b": jnp.concatenate([G["f_conv_b0"], G["f_conv_b1"]], axis=0),
    }
    return {k: val.reshape(shapes[k]) if k in shapes else val for k, val in out.items()}


def kernel(x, positions, a_norm, a_in_proj, a_conv_w, a_conv_b, a_dt_bias, a_A_log, a_D, a_gnorm, a_out_proj, kv_norm, w_kv, b_kv, k_norm, b_norm, w_q, b_q, q_norm, sinks, w_o, b_o, f_norm, f_w_in, f_conv_w, f_conv_b, f_w_down, loss_target, m_a_norm, m_a_in_proj, m_a_conv_w, m_a_conv_b, m_a_dt_bias, m_a_A_log, m_a_D, m_a_gnorm, m_a_out_proj, m_kv_norm, m_w_kv, m_b_kv, m_k_norm, m_b_norm, m_w_q, m_b_q, m_q_norm, m_sinks, m_w_o, m_b_o, m_f_norm, m_f_w_in, m_f_conv_w, m_f_conv_b, m_f_w_down, v_a_norm, v_a_in_proj, v_a_conv_w, v_a_conv_b, v_a_dt_bias, v_a_A_log, v_a_D, v_a_gnorm, v_a_out_proj, v_kv_norm, v_w_kv, v_b_kv, v_k_norm, v_b_norm, v_w_q, v_b_q, v_q_norm, v_sinks, v_w_o, v_b_o, v_f_norm, v_f_w_in, v_f_conv_w, v_f_conv_b, v_f_w_down):
    given = dict(locals())
    w_in = {n: given[n] for n in ORDER}
    m_in = {n: given["m_" + n] for n in ORDER}
    v_in = {n: given["v_" + n] for n in ORDER}
    dev = 4 * lax.axis_index("x") + 2 * lax.axis_index("y") + lax.axis_index("c")

    w2, m2, v2 = _row_blocks(w_in), _row_blocks(m_in), _row_blocks(v_in)
    small_pack = _pack([w_in[n] for n in SMALL_SHARDED], 8, f32)
    shards = {k: w2[k].astype(MXU) for k in ROW_KEYS}
    in_all, small_all = run_comm("ag_head", gather_comm([shards["inT"], small_pack]))
    whole = {n: w_in[n] for n in REPLICATED}
    for n, g in zip(SMALL_SHARDED, _unpack(small_all, [w_in[n].shape for n in SMALL_SHARDED], lead=(N_DEV,))):
        whole[n] = _gathered_to_whole(n, g)
    W = _small_weights(whole)
    W["inT"] = in_all.reshape(-1, D)
    W["in_dtT"] = jnp.pad(W["inT"][IN_COLS - HEADS:], ((0, LANES - HEADS), (0, 0)))

    loss, dx, G, scatters = _local_step(x[0], positions.reshape(S, 1).astype(f32), loss_target[0], W, shards)
    grads = _small_grads(G, {n: whole[n].shape for n in REPLICATED})

    small_names = SMALL_SHARDED + REPLICATED
    small_part = _pack([grads[n] for n in small_names] + [loss], 8, f32)
    small_gather = gather_comm([small_part])
    run_comm("ag_small_grads", small_gather)
    parts = {}
    for rs in scatters:
        parts.update(rs.parts())

    single = tuple(k for k in ROW_KEYS if not k.startswith("f_"))
    stepped = {k: final_adamw(f"adamw_{k}", *parts[k], w2[k], m2[k], v2[k]) for k in single}
    g_out, delta, new_m, new_v = (_from_row_blocks({k: stepped[k][i] for k in single}) for i in range(4))
    for n, key, lay in (("f_w_in", "f_inT", lambda a: jnp.swapaxes(a, 1, 2)), ("f_w_down", "f_down", lambda a: a)):
        res = final_adamw_layers(f"adamw_{n}", *parts[key + "0"], *parts[key + "1"], lay(w_in[n]), lay(m_in[n]), lay(v_in[n]))
        g_out[n], delta[n], new_m[n], new_v[n] = (lay(a) for a in res)
    *small_sums, loss_all = _unpack(sum_devices(small_gather.results[0]), [grads[n].shape for n in small_names] + [(1, 1)])
    for n, g in zip(small_names, small_sums):
        if n in SMALL_SHARDED:
            g_out[n] = lax.dynamic_index_in_dim(_whole_to_shards(n, g), dev, axis=0, keepdims=False)
        else:
            g_out[n] = g.reshape(w_in[n].shape)

    packs = [_pack([src[n] for n in small_names], 8, f32) for src in (w_in, g_out, m_in, v_in)]
    outs = adamw("adamw_small", *packs)
    for dst, buf in zip((delta, new_m, new_v), outs):
        for n, a in zip(small_names, _unpack(buf, [w_in[n].shape for n in small_names])):
            dst[n] = a

    return (loss_all[0, 0], dx[None], *[g_out[n] for n in ORDER], *[delta[n] for n in ORDER],
            *[new_m[n] for n in ORDER], *[new_v[n] for n in ORDER])
```

```python
import functools
import math

import jax
import jax.numpy as jnp
from jax import lax
from jax.experimental import pallas as pl
from jax.experimental.pallas import tpu as pltpu

f32 = jnp.float32
bf16 = jnp.bfloat16
MXU = bf16

N_DEV = 8
S = 2048
D = 1024
EPS = 1e-5
INNER = 2048
HEADS = 32
HP = 64
GROUPS = 8
HPG = HEADS // GROUPS
NSTATE = 128
CONV_A = 4
CHUNK = 256
NCHUNK = S // CHUNK
AH = 64
QH = 16
KVH = 4
QPK = QH // KVH
WIN = 128
NBLK = S // WIN
ROPE_THETA = 10000.0
FFN = 2816
CONV_F = 3
LANES = 128
PACK_W = 1024
VMEM_LIMIT = 56 * 1024 * 1024

ADAM_LR, ADAM_B1, ADAM_B2, ADAM_EPS, ADAM_WD, ADAM_STEP = 0.001, 0.9, 0.999, 1e-08, 0.01, 10

MESH = pl.DeviceIdType.MESH


def _cparams(sem=None):
    return pltpu.CompilerParams(dimension_semantics=sem, vmem_limit_bytes=VMEM_LIMIT)


def _pick(n, cands):
    for c in cands:
        if n % c == 0:
            return c
    return n


class Comm:
    def __init__(self, ins, out_shapes, sems, start, finish):
        self.ins, self.out_shapes, self.sems, self.start, self.finish = list(ins), list(out_shapes), list(sems), start, finish
        self.results, self.children = None, ()

    def set_results(self, res):
        self.results, o = list(res), 0
        for ch in self.children:
            ch.set_results(res[o:o + len(ch.out_shapes)])
            o += len(ch.out_shapes)


def merge_comms(comms):
    def each(fn_name, ins, outs, sems):
        i = o = s = 0
        for c in comms:
            getattr(c, fn_name)(ins[i:i + len(c.ins)], outs[o:o + len(c.out_shapes)], sems[s:s + len(c.sems)])
            i, o, s = i + len(c.ins), o + len(c.out_shapes), s + len(c.sems)

    merged = Comm([a for c in comms for a in c.ins], [a for c in comms for a in c.out_shapes], [a for c in comms for a in c.sems],
                  functools.partial(each, "start"), functools.partial(each, "finish"))
    merged.children = tuple(comms)
    return merged


def _call(body, args, *, name, grid, in_specs, out_specs, out_shape, scratch=(), sem=None, comm=None):
    if comm is None:
        return pl.pallas_call(body, name=name, grid=grid, in_specs=list(in_specs), out_specs=list(out_specs),
                              out_shape=list(out_shape), scratch_shapes=list(scratch), compiler_params=_cparams(sem))(*args)
    n_in, n_out, n_scr, c_in, c_out = len(in_specs), len(out_shape), len(scratch), len(comm.ins), len(comm.out_shapes)
    any_spec = pl.BlockSpec(memory_space=pl.ANY)

    def outer(*refs):
        ins, c_ins = refs[:n_in], refs[n_in:n_in + c_in]
        o = n_in + c_in
        outs, c_outs = refs[o:o + n_out], refs[o + n_out:o + n_out + c_out]
        o += n_out + c_out
        scr, c_sems = refs[o:o + n_scr], refs[o + n_scr:]
        ids = [pl.program_id(i) for i in range(len(grid))]
        first = functools.reduce(jnp.logical_and, [i == 0 for i in ids])
        last = functools.reduce(jnp.logical_and, [i == g - 1 for i, g in zip(ids, grid)])

        @pl.when(first)
        def _():
            comm.start(c_ins, c_outs, c_sems)

        body(*ins, *outs, *scr)

        @pl.when(last)
        def _():
            comm.finish(c_ins, c_outs, c_sems)

    res = pl.pallas_call(
        outer, name=name, grid=grid, in_specs=list(in_specs) + [any_spec] * c_in,
        out_specs=list(out_specs) + [any_spec] * c_out, out_shape=list(out_shape) + comm.out_shapes,
        scratch_shapes=list(scratch) + comm.sems, compiler_params=_cparams(("arbitrary",) * len(grid)),
    )(*args, *comm.ins)
    comm.set_results(res[n_out:])
    return res[:n_out]


def matmul(name, a, b, mode, out_dtype=f32, bias=None, residual=None, comm=None):
    if mode == "nn":
        (M, K), (K2, N) = a.shape, b.shape
    elif mode == "nt":
        (M, K), (N, K2) = a.shape, b.shape
    else:
        (K, M), (K2, N) = a.shape, b.shape
    assert K == K2, (name, a.shape, b.shape)
    if mode == "tn":
        tm, tn = M, _pick(N, (512, 256, 128) if M <= 1024 else (256, 128))
        a_spec = pl.BlockSpec((K, M), lambda j: (0, 0))
        b_spec = pl.BlockSpec((K, tn), lambda j: (0, j))
        dims = (((0,), (0,)), ((), ()))
        grid, o_map, row_map = (N // tn,), (lambda j: (0, j)), (lambda j: (0, j))
    else:
        tm, tn = (256 if N >= 2048 else 512), N
        a_spec = pl.BlockSpec((tm, K), lambda i: (i, 0))
        b_spec = pl.BlockSpec(b.shape, lambda i: (0, 0))
        dims = (((1,), (0,)), ((), ())) if mode == "nn" else (((1,), (1,)), ((), ()))
        grid, o_map, row_map = (M // tm,), (lambda i: (i, 0)), (lambda i: (0, 0))
    ins, in_specs = [a, b], [a_spec, b_spec]
    if bias is not None:
        ins.append(bias)
        in_specs.append(pl.BlockSpec((1, tn), row_map))
    if residual is not None:
        ins.append(residual)
        in_specs.append(pl.BlockSpec((tm, tn), o_map))
    has_bias, has_res = bias is not None, residual is not None

    def body(a_ref, b_ref, *rest):
        rest = list(rest)
        bias_ref = rest.pop(0) if has_bias else None
        res_ref = rest.pop(0) if has_res else None
        (o_ref,) = rest
        r = lax.dot_general(a_ref[...].astype(MXU), b_ref[...].astype(MXU), dims, preferred_element_type=f32)
        if has_bias:
            r = r + bias_ref[...]
        if has_res:
            r = r + res_ref[...]
        o_ref[...] = r.astype(out_dtype)

    return _call(body, ins, comm=comm, name=name, grid=grid, in_specs=in_specs, out_specs=[pl.BlockSpec((tm, tn), o_map)],
                 out_shape=[jax.ShapeDtypeStruct((M, N), out_dtype)], sem=("parallel",))[0]


def matmul_tn_stacked(name, a, b, out_dtype):
    R, K, M = a.shape
    N = b.shape[1]
    tn = _pick(N, (256, 128))

    def body(a_ref, b_ref, o_ref):
        o_ref[0] = lax.dot_general(a_ref[0].astype(MXU), b_ref[...].astype(MXU), (((0,), (0,)), ((), ())),
                                   preferred_element_type=f32).astype(out_dtype)

    out = pl.pallas_call(
        body, name=name, grid=(R, N // tn),
        in_specs=[pl.BlockSpec((1, K, M), lambda r, j: (r, 0, 0)), pl.BlockSpec((K, tn), lambda r, j: (0, j))],
        out_specs=pl.BlockSpec((1, M, tn), lambda r, j: (r, 0, j)),
        out_shape=jax.ShapeDtypeStruct((R, M, N), out_dtype),
        compiler_params=_cparams(("parallel", "parallel")),
    )(a, b)
    return out.reshape(R * M, N)


def rowwise(name, fn, rows, pars, outs, accs=(), tile=256, comm=None):
    n_in, n_out = len(rows) + len(pars), len(outs)
    in_specs = [pl.BlockSpec((None, tile, r[0].shape[2]), functools.partial(lambda i, lead: (lead, i, 0), lead=r[1]))
                if isinstance(r, tuple) else pl.BlockSpec((tile, r.shape[1]), lambda i: (i, 0)) for r in rows]
    rows = [r[0] if isinstance(r, tuple) else r for r in rows]
    in_specs += [pl.BlockSpec(p.shape, lambda i: (0, 0)) for p in pars]
    out_specs = [pl.BlockSpec((tile, c), lambda i: (i, 0)) for c, _ in outs]
    out_specs += [pl.BlockSpec(shp, lambda i: (0, 0)) for shp in accs]
    out_shape = [jax.ShapeDtypeStruct((S, c), dt) for c, dt in outs]
    out_shape += [jax.ShapeDtypeStruct(shp, f32) for shp in accs]

    def body(*refs):
        res = fn(*[r[...] for r in refs[:n_in]])
        o_refs = refs[n_in:n_in + n_out]
        a_refs = refs[n_in + n_out:]
        for ref, val in zip(o_refs, res[:n_out]):
            ref[...] = val.astype(ref.dtype)
        if a_refs:
            @pl.when(pl.program_id(0) == 0)
            def _():
                for ref in a_refs:
                    ref[...] = jnp.zeros_like(ref)
            for ref, val in zip(a_refs, res[n_out:]):
                ref[...] += val

    return _call(body, [*rows, *pars], name=name, grid=(S // tile,), in_specs=in_specs, out_specs=out_specs,
                 out_shape=out_shape, sem=("arbitrary",) if accs else ("parallel",), comm=comm)


def _sigmoid(x):
    return 0.5 * jnp.tanh(0.5 * x) + 0.5


def _silu(x):
    return x * _sigmoid(x)


def _dsilu(x):
    sg = _sigmoid(x)
    return sg * (1.0 + x * (1.0 - sg))


def _softplus(x):
    return jnp.maximum(x, 0.0) + jnp.log(1.0 + jnp.exp(-jnp.abs(x)))


def _rms_fwd(x, g):
    r = lax.rsqrt(jnp.mean(x * x, axis=-1, keepdims=True) + EPS)
    return x * r * g


def _rms_bwd(x, g, dh):
    r = lax.rsqrt(jnp.mean(x * x, axis=-1, keepdims=True) + EPS)
    xh = x * r
    dxh = dh * g
    dx = r * (dxh - xh * jnp.mean(dxh * xh, axis=-1, keepdims=True))
    return dx, jnp.sum(dh * xh, axis=0, keepdims=True)


def _taps(x, width):
    row = lax.broadcasted_iota(jnp.int32, (8, x.shape[1]), 0)

    def shifted(s):
        r = pltpu.roll(x, s, 0)
        return jnp.concatenate([jnp.where(row >= s, r[:8], 0.0), r[8:]], axis=0)

    return [shifted(s) for s in range(width - 1, 0, -1)] + [x]


def _conv(x, w, b, taps=None):
    width = w.shape[0]
    taps = _taps(x, width) if taps is None else taps
    out = b + w[0:1, :] * taps[0]
    for k in range(1, width):
        out = out + w[k:k + 1, :] * taps[k]
    return out


def _conv_bwd(x, w, dc, taps=None):
    width, n = w.shape[0], x.shape[0]
    taps = _taps(x, width) if taps is None else taps
    row = lax.broadcasted_iota(jnp.int32, (8, x.shape[1]), 0)
    dx = w[width - 1:width, :] * dc
    for k in range(width - 1):
        s = width - 1 - k
        r = pltpu.roll(dc, n - s, 0)
        dx = dx + w[k:k + 1, :] * jnp.concatenate([r[:n - 8], jnp.where(row < 8 - s, r[n - 8:], 0.0)], axis=0)
    dw = jnp.concatenate([jnp.sum(dc * t, axis=0, keepdims=True) for t in taps], axis=0)
    return dx, dw, jnp.sum(dc, axis=0, keepdims=True)


def _rope_tables(pos, inv_freq):
    ang = pos * inv_freq
    return jnp.cos(ang), jnp.sin(ang)


def _split2(v):
    hi = v.astype(bf16)
    return hi, (v - hi.astype(f32)).astype(bf16)


def _head_maps(width):
    shift = AH.bit_length() - 1
    to_head = (lax.broadcasted_iota(jnp.int32, (width, LANES), 0) >> shift) == lax.broadcasted_iota(jnp.int32, (width, LANES), 1)
    from_head = lax.broadcasted_iota(jnp.int32, (LANES, width), 0) == (lax.broadcasted_iota(jnp.int32, (LANES, width), 1) >> shift)
    return to_head.astype(bf16), from_head.astype(bf16)


def _head_sums(v, to_head):
    hi, lo = _split2(v)
    return jnp.dot(hi, to_head, preferred_element_type=f32) + jnp.dot(lo, to_head, preferred_element_type=f32)


def _head_spread(s, from_head):
    hi, lo = _split2(s)
    return jnp.dot(hi, from_head, preferred_element_type=f32) + jnp.dot(lo, from_head, preferred_element_type=f32)


def _rope_full(cos, sin, width):
    half = AH // 2
    pad = jnp.zeros((cos.shape[0], LANES - half), f32)
    r = lax.broadcasted_iota(jnp.int32, (LANES, width), 0)
    lane = lax.broadcasted_iota(jnp.int32, (LANES, width), 1)
    spread = ((lane & (half - 1)) == r).astype(bf16)
    full = lambda t: _head_spread(jnp.concatenate([t, pad], axis=1), spread)
    first = (lax.broadcasted_iota(jnp.int32, (1, width), 1) & (AH - 1)) < half
    sin_f = full(sin)
    return full(cos), jnp.where(first, -sin_f, sin_f), first


def _swap_halves(v, first):
    half, width = AH // 2, v.shape[1]
    return jnp.where(first, pltpu.roll(v, width - half, 1), pltpu.roll(v, half, 1))


def _headnorm_rope_fwd(x, g, cos, sin, heads):
    to_head, from_head = _head_maps(heads * AH)
    cos_f, sin_s, first = _rope_full(cos, sin, heads * AH)
    r = _head_spread(lax.rsqrt(_head_sums(x * x, to_head) * (1.0 / AH) + EPS), from_head)
    n = x * r * jnp.tile(g, (1, heads))
    return n * cos_f + _swap_halves(n, first) * sin_s


def _headnorm_rope_bwd(x, g, cos, sin, dout, heads):
    width = heads * AH
    to_head, from_head = _head_maps(width)
    cos_f, sin_s, first = _rope_full(cos, sin, width)
    r = _head_spread(lax.rsqrt(_head_sums(x * x, to_head) * (1.0 / AH) + EPS), from_head)
    xh = x * r
    dn = dout * cos_f - _swap_halves(dout, first) * sin_s
    dxh = dn * jnp.tile(g, (1, heads))
    m = _head_spread(_head_sums(dxh * xh, to_head) * (1.0 / AH), from_head)
    dx = r * (dxh - xh * m)
    dg_lanes = jnp.sum(dn * xh, axis=0, keepdims=True)
    fold = ((lax.broadcasted_iota(jnp.int32, (width, LANES), 0) & (AH - 1))
            == lax.broadcasted_iota(jnp.int32, (width, LANES), 1)).astype(f32)
    dg = jnp.dot(jnp.broadcast_to(dg_lanes, (8, width)), fold, precision=lax.Precision.HIGHEST, preferred_element_type=f32)
    return dx, dg[0:1, :AH]


def _ssd_prep(dt_pre, dt_bias, a_log, dt_s, acum_s, acumT_s):
    dt = _softplus(dt_pre + dt_bias)
    a = dt * (-jnp.exp(a_log))
    row = lax.broadcasted_iota(jnp.int32, (CHUNK, CHUNK), 0)
    col = lax.broadcasted_iota(jnp.int32, (CHUNK, CHUNK), 1)
    dt_s[...] = dt
    acum_s[...] = jnp.dot((col <= row).astype(f32), a, precision=lax.Precision.HIGHEST, preferred_element_type=f32)
    acumT_s[...] = lax.dot_general(a, (row <= col).astype(f32), (((0,), (0,)), ((), ())),
                                   precision=lax.Precision.HIGHEST, preferred_element_type=f32)


def _head_cols(h, dt_s, acum_s, acumT_s):
    lane = lax.broadcasted_iota(jnp.int32, (1, LANES), 1)
    oh_l = (lane == h).astype(f32)
    sub = lax.broadcasted_iota(jnp.int32, (LANES, 1), 0)
    oh_s = (sub == h).astype(f32)
    dt_h = jnp.sum(dt_s[...] * oh_l, axis=1, keepdims=True)
    ac_h = jnp.sum(acum_s[...] * oh_l, axis=1, keepdims=True)
    acr_h = jnp.sum(acumT_s[...] * oh_s, axis=0, keepdims=True)
    return oh_l, dt_h, ac_h, acr_h


def ssd_fwd(xs, Bm, Cm, dt_pre, dt_bias, a_log, d_skip, comm=None):
    def body(xs_ref, b_ref, c_ref, dtp_ref, bias_ref, alog_ref, d_ref, y_ref, st_ref, state, dt_s, acum_s, acumT_s):
        c, g = pl.program_id(0), pl.program_id(1)

        @pl.when(g == 0)
        def _():
            _ssd_prep(dtp_ref[...], bias_ref[...], alog_ref[...], dt_s, acum_s, acumT_s)

        row = lax.broadcasted_iota(jnp.int32, (CHUNK, CHUNK), 0)
        col = lax.broadcasted_iota(jnp.int32, (CHUNK, CHUNK), 1)
        causal = col <= row
        Bb, Cb = b_ref[...], c_ref[...]
        cb = lax.dot_general(Cb.astype(MXU), Bb.astype(MXU), (((1,), (1,)), ((), ())), preferred_element_type=f32)
        xs_blk = xs_ref[...]

        @pl.when(c == 0)
        def _():
            for j in range(HPG):
                state[g * HPG + j] = jnp.zeros((NSTATE, HP), f32)

        prevs = [state[g * HPG + j] for j in range(HPG)]
        y_off_all = jnp.dot(Cb.astype(MXU), jnp.concatenate(prevs, axis=1).astype(MXU), preferred_element_type=f32)
        ys, xds, e_ends = [], [], []
        for j in range(HPG):
            oh_l, dt_h, ac_h, acr_h = _head_cols(g * HPG + j, dt_s, acum_s, acumT_s)
            decay = jnp.exp(jnp.where(causal, ac_h - acr_h, -1e30))
            w = (cb * decay).astype(MXU)
            xs_h = xs_blk[:, HP * j:HP * (j + 1)]
            xd = xs_h * dt_h
            y_diag = jnp.dot(w, xd.astype(MXU), preferred_element_type=f32)
            y_off = y_off_all[:, HP * j:HP * (j + 1)] * jnp.exp(ac_h)
            d_h = jnp.sum(d_ref[...] * oh_l, axis=1, keepdims=True)
            ys.append(y_diag + y_off + xs_h * d_h)
            a_end = ac_h[CHUNK - 1:CHUNK, :]
            xds.append(xd * jnp.exp(a_end - ac_h))
            e_ends.append(jnp.exp(a_end))
        s_c = lax.dot_general(Bb.astype(MXU), jnp.concatenate(xds, axis=1).astype(MXU), (((0,), (0,)), ((), ())),
                              preferred_element_type=f32)
        for j in range(HPG):
            st_ref[0, j] = prevs[j]
            state[g * HPG + j] = prevs[j] * e_ends[j] + s_c[:, HP * j:HP * (j + 1)]
        y_ref[...] = jnp.concatenate(ys, axis=1)

    par = pl.BlockSpec((1, LANES), lambda c, g: (0, 0))
    return _call(
        body, [xs, Bm, Cm, dt_pre, dt_bias, a_log, d_skip], comm=comm, name="ssd_fwd", grid=(NCHUNK, GROUPS),
        in_specs=[pl.BlockSpec((CHUNK, HPG * HP), lambda c, g: (c, g)),
                  pl.BlockSpec((CHUNK, NSTATE), lambda c, g: (c, g)),
                  pl.BlockSpec((CHUNK, NSTATE), lambda c, g: (c, g)),
                  pl.BlockSpec((CHUNK, LANES), lambda c, g: (c, 0)), par, par, par],
        out_specs=[pl.BlockSpec((CHUNK, HPG * HP), lambda c, g: (c, g)),
                   pl.BlockSpec((1, HPG, NSTATE, HP), lambda c, g: (c, g, 0, 0))],
        out_shape=[jax.ShapeDtypeStruct((S, INNER), f32), jax.ShapeDtypeStruct((NCHUNK, HEADS, NSTATE, HP), f32)],
        scratch=[pltpu.VMEM((HEADS, NSTATE, HP), f32), pltpu.VMEM((CHUNK, LANES), f32),
                 pltpu.VMEM((CHUNK, LANES), f32), pltpu.VMEM((LANES, CHUNK), f32)],
        sem=("arbitrary", "arbitrary"))


def ssd_bwd(xs, Bm, Cm, dt_pre, dt_bias, a_log, d_skip, states, dy, comm=None):
    rev = lambda c: NCHUNK - 1 - c

    def body(xs_ref, b_ref, c_ref, dtp_ref, bias_ref, alog_ref, d_ref, st_ref, dy_ref,
             dxs_ref, db_ref, dc_ref, ddt_ref, dbias_ref, dalog_ref, dd_ref,
             dstate, dt_s, acum_s, acumT_s, dacum_s, ddt_s, da_s):
        c, g = pl.program_id(0), pl.program_id(1)

        @pl.when(g == 0)
        def _():
            _ssd_prep(dtp_ref[...], bias_ref[...], alog_ref[...], dt_s, acum_s, acumT_s)
            dacum_s[...] = jnp.zeros_like(dacum_s)
            ddt_s[...] = jnp.zeros_like(ddt_s)

        @pl.when((c == 0) & (g == 0))
        def _():
            da_s[...] = jnp.zeros_like(da_s)
            dd_ref[...] = jnp.zeros_like(dd_ref)
            dbias_ref[...] = jnp.zeros_like(dbias_ref)
            dalog_ref[...] = jnp.zeros_like(dalog_ref)

        row = lax.broadcasted_iota(jnp.int32, (CHUNK, CHUNK), 0)
        col = lax.broadcasted_iota(jnp.int32, (CHUNK, CHUNK), 1)
        sub_l = lax.broadcasted_iota(jnp.int32, (CHUNK, 1), 0)
        last = (sub_l == CHUNK - 1).astype(f32)
        nt = (((1,), (1,)), ((), ()))
        tn = (((0,), (0,)), ((), ()))
        Bb, Cb = b_ref[...], c_ref[...]
        Bm_, Cm_ = Bb.astype(MXU), Cb.astype(MXU)
        cb = lax.dot_general(Cm_, Bm_, nt, preferred_element_type=f32)
        bc = lax.dot_general(Bm_, Cm_, nt, preferred_element_type=f32)
        xs_blk, dy_blk = xs_ref[...], dy_ref[...]
        dxs, dB, dC = [], jnp.zeros((CHUNK, NSTATE), f32), jnp.zeros((CHUNK, NSTATE), f32)
        for j in range(HPG):
            h = g * HPG + j
            oh_l, dt_h, ac_h, acr_h = _head_cols(h, dt_s, acum_s, acumT_s)

            @pl.when(c == 0)
            def _():
                dstate[h] = jnp.zeros((NSTATE, HP), f32)

            dnext = dstate[h]
            prev = st_ref[0, j]
            lm = jnp.exp(jnp.where(col <= row, ac_h - acr_h, -1e30))
            lmT = jnp.exp(jnp.where(row <= col, acr_h - ac_h, -1e30))
            xs_h = xs_blk[:, HP * j:HP * (j + 1)]
            dy_h = dy_blk[:, HP * j:HP * (j + 1)]
            xd = xs_h * dt_h
            xdm, dym = xd.astype(MXU), dy_h.astype(MXU)
            ea = jnp.exp(ac_h)
            a_end = ac_h[CHUNK - 1:CHUNK, :]
            e_end = jnp.exp(a_end)
            dte = jnp.exp(a_end - ac_h)
            dnm, pvm = dnext.astype(MXU), prev.astype(MXU)
            bd = jnp.dot(Bm_, dnm, preferred_element_type=f32)
            dxd = jnp.dot((bc * lmT).astype(MXU), dym, preferred_element_type=f32) + dte * bd
            dw = lax.dot_general(dym, xdm, nt, preferred_element_type=f32)
            dwT = lax.dot_general(xdm, dym, nt, preferred_element_type=f32)
            dcb = dw * lm
            dbc = dwT * lmT
            eady = (ea * dy_h).astype(MXU)
            dC = dC + jnp.dot(dcb.astype(MXU), Bm_, preferred_element_type=f32) \
                + lax.dot_general(eady, pvm, nt, preferred_element_type=f32)
            dB = dB + jnp.dot(dbc.astype(MXU), Cm_, preferred_element_type=f32) \
                + dte * lax.dot_general(xdm, dnm, nt, preferred_element_type=f32)
            dstate[h] = lax.dot_general(Cm_, eady, tn, preferred_element_type=f32) + e_end * dnext
            r1 = jnp.sum(dcb * cb, axis=1, keepdims=True)
            r2 = jnp.sum(dbc * bc, axis=1, keepdims=True)
            y_off = jnp.dot(Cm_, pvm, preferred_element_type=f32) * ea
            t3 = jnp.sum(dy_h * y_off, axis=1, keepdims=True)
            t4 = jnp.sum(bd * xd, axis=1, keepdims=True) * dte
            end_extra = jnp.sum(t4, axis=0, keepdims=True) + e_end * jnp.sum(jnp.sum(prev * dnext, axis=1, keepdims=True), axis=0, keepdims=True)
            dacum_h = r1 - r2 + t3 - t4 + last * end_extra
            dacum_s[...] += dacum_h * oh_l
            ddt_s[...] += jnp.sum(dxd * xs_h, axis=1, keepdims=True) * oh_l
            d_h = jnp.sum(d_ref[...] * oh_l, axis=1, keepdims=True)
            dxs.append(dxd * dt_h + dy_h * d_h)
            dd_ref[...] += oh_l * jnp.sum(jnp.sum(dy_h * xs_h, axis=1, keepdims=True), axis=0, keepdims=True)
        dxs_ref[...] = jnp.concatenate(dxs, axis=1)
        db_ref[...] = dB
        dc_ref[...] = dC

        @pl.when(g == GROUPS - 1)
        def _():
            a_row = -jnp.exp(alog_ref[...])
            da = jnp.dot((row <= col).astype(f32), dacum_s[...], precision=lax.Precision.HIGHEST, preferred_element_type=f32)
            da_s[...] += jnp.sum(da * dt_s[...], axis=0, keepdims=True)
            z = dtp_ref[...] + bias_ref[...]
            ddt_pre = (ddt_s[...] + da * a_row) * _sigmoid(z)
            ddt_ref[...] = ddt_pre.astype(ddt_ref.dtype)
            dbias_ref[...] += jnp.sum(ddt_pre, axis=0, keepdims=True)

            @pl.when(c == NCHUNK - 1)
            def _():
                dalog_ref[...] = da_s[...] * a_row

    par = pl.BlockSpec((1, LANES), lambda c, g: (0, 0))
    return _call(
        body, [xs, Bm, Cm, dt_pre, dt_bias, a_log, d_skip, states, dy], comm=comm, name="ssd_bwd", grid=(NCHUNK, GROUPS),
        in_specs=[pl.BlockSpec((CHUNK, HPG * HP), lambda c, g: (rev(c), g)),
                  pl.BlockSpec((CHUNK, NSTATE), lambda c, g: (rev(c), g)),
                  pl.BlockSpec((CHUNK, NSTATE), lambda c, g: (rev(c), g)),
                  pl.BlockSpec((CHUNK, LANES), lambda c, g: (rev(c), 0)), par, par, par,
                  pl.BlockSpec((1, HPG, NSTATE, HP), lambda c, g: (rev(c), g, 0, 0)),
                  pl.BlockSpec((CHUNK, HPG * HP), lambda c, g: (rev(c), g))],
        out_specs=[pl.BlockSpec((CHUNK, HPG * HP), lambda c, g: (rev(c), g)),
                   pl.BlockSpec((CHUNK, NSTATE), lambda c, g: (rev(c), g)),
                   pl.BlockSpec((CHUNK, NSTATE), lambda c, g: (rev(c), g)),
                   pl.BlockSpec((CHUNK, LANES), lambda c, g: (rev(c), 0)), par, par, par],
        out_shape=[jax.ShapeDtypeStruct((S, INNER), f32), jax.ShapeDtypeStruct((S, GROUPS * NSTATE), f32),
                   jax.ShapeDtypeStruct((S, GROUPS * NSTATE), f32), jax.ShapeDtypeStruct((S, LANES), MXU),
                   jax.ShapeDtypeStruct((1, LANES), f32), jax.ShapeDtypeStruct((1, LANES), f32),
                   jax.ShapeDtypeStruct((1, LANES), f32)],
        scratch=[pltpu.VMEM((HEADS, NSTATE, HP), f32), pltpu.VMEM((CHUNK, LANES), f32),
                 pltpu.VMEM((CHUNK, LANES), f32), pltpu.VMEM((LANES, CHUNK), f32),
                 pltpu.VMEM((CHUNK, LANES), f32), pltpu.VMEM((CHUNK, LANES), f32), pltpu.VMEM((1, LANES), f32)],
        sem=("arbitrary", "arbitrary"))


ATT_STACK_FWD, ATT_STACK_BWD = 4, 2


def _attn_kv(kp, kc, vp, vc, hk):
    sl = slice(AH * hk, AH * (hk + 1))
    return (jnp.concatenate([kp[:, sl], kc[:, sl]], axis=0).astype(MXU),
            jnp.concatenate([vp[:, sl], vc[:, sl]], axis=0).astype(MXU))


def _stack_heads(x, heads):
    return jnp.concatenate([x[:, AH * h:AH * (h + 1)] for h in heads], axis=0)


def _attn_block(n, q, kb, sinks, heads):
    rows = len(heads) * WIN
    qi = lax.broadcasted_iota(jnp.int32, (rows, 2 * WIN), 0) & (WIN - 1)
    ki = lax.broadcasted_iota(jnp.int32, (rows, 2 * WIN), 1)
    rel = qi + WIN - ki
    mask = (rel >= 0) & (rel < WIN) & ((ki >= WIN) | (n > 0))
    qg = _stack_heads(q, heads).astype(MXU)
    s = lax.dot_general(qg, kb, (((1,), (1,)), ((), ())), preferred_element_type=f32) * (AH ** -0.5)
    s = jnp.where(mask, s, -1e30)
    sink = jnp.concatenate([jnp.broadcast_to(sinks[:, h:h + 1], (WIN, 1)) for h in heads], axis=0)
    m = jnp.maximum(jnp.max(s, axis=1, keepdims=True), sink)
    p = jnp.exp(s - m)
    ps = jnp.exp(sink - m)
    inv = 1.0 / (jnp.sum(p, axis=1, keepdims=True) + ps)
    return qg, p * inv, ps * inv


def _head_blocks(hk, stack):
    return [list(range(QPK * hk + i, QPK * hk + i + stack)) for i in range(0, QPK, stack)]


def _kv_specs():
    prev = lambda n: (jnp.maximum(n - 1, 0), 0)
    cur = lambda n: (n, 0)
    w = KVH * AH
    return [pl.BlockSpec((WIN, w), prev), pl.BlockSpec((WIN, w), cur), pl.BlockSpec((WIN, w), prev), pl.BlockSpec((WIN, w), cur)]


def attn_fwd(q, k, v, sinks, comm=None):
    def body(q_ref, kp_ref, kc_ref, vp_ref, vc_ref, s_ref, o_ref):
        n = pl.program_id(0)
        q_, kp, kc, vp, vc, sk = q_ref[...], kp_ref[...], kc_ref[...], vp_ref[...], vc_ref[...], s_ref[...]
        outs = []
        for hk in range(KVH):
            kb, vb = _attn_kv(kp, kc, vp, vc, hk)
            for heads in _head_blocks(hk, ATT_STACK_FWD):
                _, pr, _ = _attn_block(n, q_, kb, sk, heads)
                o = jnp.dot(pr.astype(MXU), vb, preferred_element_type=f32)
                outs += [o[WIN * i:WIN * (i + 1)] for i in range(len(heads))]
        o_ref[...] = jnp.concatenate(outs, axis=1)

    return _call(
        body, [q, k, k, v, v, sinks], comm=comm, name="attn_fwd", grid=(NBLK,),
        in_specs=[pl.BlockSpec((WIN, D), lambda n: (n, 0))] + _kv_specs() + [pl.BlockSpec((1, QH), lambda n: (0, 0))],
        out_specs=[pl.BlockSpec((WIN, D), lambda n: (n, 0))],
        out_shape=[jax.ShapeDtypeStruct((S, D), f32)], sem=("parallel",))[0]


def attn_bwd(q, k, v, sinks, dout, comm=None):
    def body(q_ref, kp_ref, kc_ref, vp_ref, vc_ref, s_ref, do_ref, dq_ref, dkp_ref, dkc_ref, dvp_ref, dvc_ref, ds_ref):
        n = pl.program_id(0)

        @pl.when(n == 0)
        def _():
            ds_ref[...] = jnp.zeros_like(ds_ref)

        q_, kp, kc, vp, vc, sk, do = q_ref[...], kp_ref[...], kc_ref[...], vp_ref[...], vc_ref[...], s_ref[...], do_ref[...]
        lane = lax.broadcasted_iota(jnp.int32, (1, QH), 1)
        nt = (((1,), (1,)), ((), ()))
        tn = (((0,), (0,)), ((), ()))
        dqs, dkps, dkcs, dvps, dvcs = [], [], [], [], []
        dsink = jnp.zeros((1, QH), f32)
        for hk in range(KVH):
            kb, vb = _attn_kv(kp, kc, vp, vc, hk)
            dkb, dvb = jnp.zeros((2 * WIN, AH), f32), jnp.zeros((2 * WIN, AH), f32)
            for heads in _head_blocks(hk, ATT_STACK_BWD):
                qg, pr, prs = _attn_block(n, q_, kb, sk, heads)
                dog = _stack_heads(do, heads).astype(MXU)
                dp = lax.dot_general(dog, vb, nt, preferred_element_type=f32)
                dvb = dvb + lax.dot_general(pr.astype(MXU), dog, tn, preferred_element_type=f32)
                delta = jnp.sum(pr * dp, axis=1, keepdims=True)
                ds = (pr * (dp - delta)).astype(MXU)
                dsk = -prs * delta
                for i, h in enumerate(heads):
                    dsink = dsink + jnp.sum(dsk[WIN * i:WIN * (i + 1)], axis=0, keepdims=True) * (lane == h).astype(f32)
                dqg = jnp.dot(ds, kb, preferred_element_type=f32) * (AH ** -0.5)
                dkb = dkb + lax.dot_general(ds, qg, tn, preferred_element_type=f32) * (AH ** -0.5)
                dqs += [dqg[WIN * i:WIN * (i + 1)] for i in range(len(heads))]
            dkps.append(dkb[:WIN])
            dkcs.append(dkb[WIN:])
            dvps.append(dvb[:WIN])
            dvcs.append(dvb[WIN:])
        dq_ref[...] = jnp.concatenate(dqs, axis=1)
        dkp_ref[...] = jnp.concatenate(dkps, axis=1)
        dkc_ref[...] = jnp.concatenate(dkcs, axis=1)
        dvp_ref[...] = jnp.concatenate(dvps, axis=1)
        dvc_ref[...] = jnp.concatenate(dvcs, axis=1)
        ds_ref[...] += dsink

    w = KVH * AH
    blk = lambda width: pl.BlockSpec((WIN, width), lambda n: (n, 0))
    return _call(
        body, [q, k, k, v, v, sinks, dout], comm=comm, name="attn_bwd", grid=(NBLK,),
        in_specs=[blk(D)] + _kv_specs() + [pl.BlockSpec((1, QH), lambda n: (0, 0)), blk(D)],
        out_specs=[blk(D), blk(w), blk(w), blk(w), blk(w), pl.BlockSpec((1, QH), lambda n: (0, 0))],
        out_shape=[jax.ShapeDtypeStruct((S, D), f32)] + [jax.ShapeDtypeStruct((S, w), f32)] * 4 + [jax.ShapeDtypeStruct((1, QH), f32)],
        sem=("arbitrary",))


def kv_bwd(kv, kvn, pos, inv_freq, k_norm, dkp, dkc, dvp, dvc):
    w = KVH * AH

    def body(kv_ref, kvn_ref, pos_ref, if_ref, g_ref, dkp_ref, dkc_ref, dvp_ref, dvc_ref, o_ref, dg_ref, db_ref, gw_ref):
        n = pl.program_id(0)

        @pl.when(n == 0)
        def _():
            dg_ref[...] = jnp.zeros_like(dg_ref)
            db_ref[...] = jnp.zeros_like(db_ref)
            gw_ref[...] = jnp.zeros_like(gw_ref)

        inside = (n < NBLK - 1).astype(f32)
        dk = dkc_ref[...] + inside * dkp_ref[...]
        dv = dvc_ref[...] + inside * dvp_ref[...]
        cos, sin = _rope_tables(pos_ref[...], if_ref[...])
        dkpre, dg = _headnorm_rope_bwd(kv_ref[...], g_ref[...], cos, sin, dk, KVH)
        dkv = jnp.concatenate([dkpre, dv], axis=1)
        o_ref[...] = dkv.astype(o_ref.dtype)
        dg_ref[...] += dg
        db_ref[...] += jnp.sum(dkv, axis=0, keepdims=True)
        gw_ref[...] += lax.dot_general(kvn_ref[...].astype(MXU), dkv.astype(MXU), (((0,), (0,)), ((), ())), preferred_element_type=f32)

    nxt = lambda n: (jnp.minimum(n + 1, NBLK - 1), 0)
    cur = lambda n: (n, 0)
    const = lambda n: (0, 0)
    return pl.pallas_call(
        body, name="kv_bwd", grid=(NBLK,),
        in_specs=[pl.BlockSpec((WIN, w), cur), pl.BlockSpec((WIN, D), cur), pl.BlockSpec((WIN, 1), cur),
                  pl.BlockSpec((1, AH // 2), const), pl.BlockSpec((1, AH), const), pl.BlockSpec((WIN, w), nxt),
                  pl.BlockSpec((WIN, w), cur), pl.BlockSpec((WIN, w), nxt), pl.BlockSpec((WIN, w), cur)],
        out_specs=[pl.BlockSpec((WIN, 2 * w), cur), pl.BlockSpec((1, AH), const), pl.BlockSpec((1, 2 * w), const),
                   pl.BlockSpec((D, 2 * w), const)],
        out_shape=[jax.ShapeDtypeStruct((S, 2 * w), MXU), jax.ShapeDtypeStruct((1, AH), f32), jax.ShapeDtypeStruct((1, 2 * w), f32),
                   jax.ShapeDtypeStruct((D, 2 * w), f32)],
        compiler_params=_cparams(("arbitrary",)),
    )(kv, kvn, pos, inv_freq, k_norm, dkp, dkc, dvp, dvc)


def _adam_math(w, g, m, v):
    m = ADAM_B1 * m + (1.0 - ADAM_B1) * g
    v = ADAM_B2 * v + (1.0 - ADAM_B2) * (g * g)
    m_hat = m / (1.0 - ADAM_B1 ** ADAM_STEP)
    v_hat = v / (1.0 - ADAM_B2 ** ADAM_STEP)
    return -ADAM_LR * (m_hat / (jnp.sqrt(v_hat) + ADAM_EPS) + ADAM_WD * w), m, v


def adamw(name, w, g, m, v):
    R, C = w.shape
    tr = _pick(R, (256, 128, 64, 32, 16, 8))
    tc = C if tr < R or C % 256 else 256

    def body(w_ref, g_ref, m_ref, v_ref, d_ref, nm_ref, nv_ref):
        d_ref[...], nm_ref[...], nv_ref[...] = _adam_math(w_ref[...], g_ref[...], m_ref[...], v_ref[...])

    spec = pl.BlockSpec((tr, tc), lambda i, j: (i, j))
    return pl.pallas_call(
        body, name=name, grid=(R // tr, C // tc), in_specs=[spec] * 4, out_specs=[spec] * 3,
        out_shape=[jax.ShapeDtypeStruct((R, C), f32)] * 3, compiler_params=_cparams(("parallel", "parallel")),
    )(w, g, m, v)


def _me():
    return lax.axis_index("x"), lax.axis_index("y"), lax.axis_index("c")


def gather_comm(xs):
    n = len(xs)

    def parts(x_refs, o_refs, sems):
        send_sems, recv_sems, local_sems = sems
        x, y, c = _me()
        me, sibling = (x, y, c), (x, y, 1 - c)
        chips = [(1 - x, y), (x, 1 - y), (1 - x, 1 - y)]

        def copy(a, k, block, to, src=None):
            dst = o_refs[a].at[4 * block[0] + 2 * block[1] + block[2]]
            return pltpu.make_async_remote_copy(
                src_ref=dst if src is None else src, dst_ref=dst,
                send_sem=send_sems.at[7 * a + k], recv_sem=recv_sems.at[7 * a + k], device_id=to, device_id_type=MESH)

        mine = [pltpu.make_async_copy(x_refs[a], o_refs[a].at[4 * x + 2 * y + c], local_sems.at[a]) for a in range(n)]
        first = []
        for a in range(n):
            first.append(copy(a, 0, me, sibling, src=x_refs[a]))
            first += [copy(a, 1 + j, me, (*chip, c), src=x_refs[a]) for j, chip in enumerate(chips)]
        return copy, mine, first, me, sibling, chips, c

    def start(x_refs, o_refs, sems):
        _, mine, first, *_ = parts(x_refs, o_refs, sems)
        for cp in mine + first:
            cp.start()

    def finish(x_refs, o_refs, sems):
        copy, mine, first, me, sibling, chips, c = parts(x_refs, o_refs, sems)
        passed = []
        for j, chip in enumerate(chips):
            for a in range(n):
                copy(a, 1 + j, (*chip, c), me).wait_recv()
                cp = copy(a, 4 + j, (*chip, c), sibling)
                cp.start()
                passed.append(cp)
        for a in range(n):
            copy(a, 0, sibling, me).wait_recv()
            for j, chip in enumerate(chips):
                copy(a, 4 + j, (*chip, 1 - c), me).wait_recv()
        for cp in first + passed:
            cp.wait_send()
        for cp in mine:
            cp.wait()

    return Comm(xs, [jax.ShapeDtypeStruct((N_DEV,) + a.shape, a.dtype) for a in xs],
                [pltpu.SemaphoreType.DMA((7 * n,)), pltpu.SemaphoreType.DMA((7 * n,)), pltpu.SemaphoreType.DMA((n,))], start, finish)


def run_comm(name, comm):
    _call(lambda: None, [], name=name, grid=(1,), in_specs=[], out_specs=[], out_shape=[], comm=comm)
    return comm.results


def sibling_comm(gs):
    n = len(gs)

    def copies(g_refs, o_refs, sems):
        x, y, c = _me()
        return [pltpu.make_async_remote_copy(
            src_ref=g_refs[a].at[:, 1 - c], dst_ref=o_refs[a], send_sem=sems[0].at[a], recv_sem=sems[1].at[a],
            device_id=(x, y, 1 - c), device_id_type=MESH) for a in range(n)]

    def start(g_refs, o_refs, sems):
        for cp in copies(g_refs, o_refs, sems):
            cp.start()

    def finish(g_refs, o_refs, sems):
        for cp in copies(g_refs, o_refs, sems):
            cp.wait()

    return Comm(gs, [jax.ShapeDtypeStruct((4,) + g.shape[2:], g.dtype) for g in gs],
                [pltpu.SemaphoreType.DMA((n,)), pltpu.SemaphoreType.DMA((n,))], start, finish)


def chip_comm(ts):
    n = len(ts)

    def copies(t_refs, o_refs, sems):
        x, y, c = _me()
        chips = [(1 - x, y), (x, 1 - y), (1 - x, 1 - y)]
        return [pltpu.make_async_remote_copy(
            src_ref=t_refs[a].at[2 * px + py], dst_ref=o_refs[a].at[j],
            send_sem=sems[0].at[3 * a + j], recv_sem=sems[1].at[3 * a + j],
            device_id=(px, py, c), device_id_type=MESH) for j, (px, py) in enumerate(chips) for a in range(n)]

    def start(t_refs, o_refs, sems):
        for cp in copies(t_refs, o_refs, sems):
            cp.start()

    def finish(t_refs, o_refs, sems):
        for cp in copies(t_refs, o_refs, sems):
            cp.wait()

    return Comm(ts, [jax.ShapeDtypeStruct((3,) + t.shape[1:], t.dtype) for t in ts],
                [pltpu.SemaphoreType.DMA((3 * n,)), pltpu.SemaphoreType.DMA((3 * n,))], start, finish)


def _row_tile(rows):
    return _pick(rows, (512, 304, 256, 128))


def pair_add(name, g, r):
    _, _, R, C = g.shape
    tr = _row_tile(R)

    def body(c_ref, g_ref, r_ref, o_ref):
        o_ref[0] = (g_ref[0, 0].astype(f32) + r_ref[0].astype(f32)).astype(o_ref.dtype)

    return pl.pallas_call(
        body, name=name,
        grid_spec=pltpu.PrefetchScalarGridSpec(
            num_scalar_prefetch=1, grid=(4, R // tr),
            in_specs=[pl.BlockSpec((1, 1, tr, C), lambda p, i, c: (p, c[0], i, 0)),
                      pl.BlockSpec((1, tr, C), lambda p, i, c: (p, i, 0))],
            out_specs=pl.BlockSpec((1, tr, C), lambda p, i, c: (p, i, 0))),
        out_shape=jax.ShapeDtypeStruct((4, R, C), g.dtype),
        compiler_params=_cparams(("parallel", "parallel")),
    )(lax.axis_index("c").reshape(1).astype(jnp.int32), g, r)


def _sum_of_four(t_ref, r_ref):
    return ((t_ref[0].astype(f32) + r_ref[0].astype(f32)) + r_ref[1].astype(f32)) + r_ref[2].astype(f32)


def _my_chip():
    return (2 * lax.axis_index("x") + lax.axis_index("y")).reshape(1).astype(jnp.int32)


def final_adamw(name, t, r, w, m, v):
    _, R, C = t.shape
    tr = _pick(R, (256, 128, 64, 32, 16))
    tc = C if tr < R or C % 256 else 256

    def body(p_ref, t_ref, r_ref, w_ref, m_ref, v_ref, g_ref, d_ref, nm_ref, nv_ref):
        g_ = _sum_of_four(t_ref, r_ref)
        g_ref[...] = g_
        d_ref[...], nm_ref[...], nv_ref[...] = _adam_math(w_ref[...], g_, m_ref[...], v_ref[...])

    flat = pl.BlockSpec((tr, tc), lambda i, j, p: (i, j))
    return pl.pallas_call(
        body, name=name,
        grid_spec=pltpu.PrefetchScalarGridSpec(
            num_scalar_prefetch=1, grid=(R // tr, C // tc),
            in_specs=[pl.BlockSpec((1, tr, tc), lambda i, j, p: (p[0], i, j)),
                      pl.BlockSpec((3, tr, tc), lambda i, j, p: (0, i, j)), flat, flat, flat],
            out_specs=[flat] * 4),
        out_shape=[jax.ShapeDtypeStruct((R, C), f32)] * 4,
        compiler_params=_cparams(("parallel", "parallel")),
    )(_my_chip(), t, r, w, m, v)


def final_adamw_layers(name, t0, r0, t1, r1, w, m, v):
    _, R, C = t0.shape
    tr = _pick(R, (256, 128, 64, 32, 16))

    def body(p_ref, t0_ref, r0_ref, t1_ref, r1_ref, w_ref, m_ref, v_ref, g_ref, d_ref, nm_ref, nv_ref):
        g_ = jnp.where(pl.program_id(0) == 0, _sum_of_four(t0_ref, r0_ref), _sum_of_four(t1_ref, r1_ref))
        g_ref[0] = g_
        d_ref[0], nm_ref[0], nv_ref[0] = _adam_math(w_ref[0], g_, m_ref[0], v_ref[0])

    mine = pl.BlockSpec((1, tr, C), lambda l, i, p: (p[0], i, 0))
    theirs = pl.BlockSpec((3, tr, C), lambda l, i, p: (0, i, 0))
    layer = pl.BlockSpec((1, tr, C), lambda l, i, p: (l, i, 0))
    return pl.pallas_call(
        body, name=name,
        grid_spec=pltpu.PrefetchScalarGridSpec(
            num_scalar_prefetch=1, grid=(2, R // tr),
            in_specs=[mine, theirs, mine, theirs, layer, layer, layer], out_specs=[layer] * 4),
        out_shape=[jax.ShapeDtypeStruct((2, R, C), f32)] * 4,
        compiler_params=_cparams(("parallel", "parallel")),
    )(_my_chip(), t0, r0, t1, r1, w, m, v)


class ReduceScatter:
    def __init__(self, tag, keys, grads):
        self.tag, self.keys, self.grads = tag, keys, grads
        self.send = [g.reshape((4, 2, g.shape[0] // N_DEV) + g.shape[1:]) for g in grads]

    def sibling(self):
        self.c1 = sibling_comm(self.send)
        return self.c1

    def chips(self):
        self.pairs = [pair_add(f"rs_pair_add_{self.tag}{i}", g, r) for i, (g, r) in enumerate(zip(self.send, self.c1.results))]
        self.c2 = chip_comm(self.pairs)
        return self.c2

    def parts(self):
        return {k: (t, r) for k, t, r in zip(self.keys, self.pairs, self.c2.results)}


IN_ROWS = {"z": (0, 2048), "xs": (2048, 4096), "B": (4096, 5120), "C": (5120, 6144)}
IN_COLS = 2 * INNER + 2 * GROUPS * NSTATE + HEADS


def sum_devices(g):
    def body(g_ref, o_ref):
        acc = g_ref[0]
        for i in range(1, N_DEV):
            acc = acc + g_ref[i]
        o_ref[...] = acc

    return pl.pallas_call(body, name="sum_devices", out_shape=jax.ShapeDtypeStruct(g.shape[1:], f32),
                          compiler_params=_cparams())(g)


def _pack(parts, unit, dtype, lead=()):
    flat = jnp.concatenate([p.reshape(lead + (-1,)).astype(dtype) for p in parts], axis=-1)
    n = flat.shape[-1]
    rows = -(-n // (unit * PACK_W)) * unit
    flat = jnp.pad(flat, [(0, 0)] * len(lead) + [(0, rows * PACK_W - n)])
    return flat.reshape(lead + (rows, PACK_W))


def _unpack(buf, shapes, lead=()):
    flat = buf.reshape(lead + (-1,))
    out, off = [], 0
    for shp in shapes:
        n = math.prod(shp)
        out.append(flat[..., off:off + n].reshape(lead + tuple(shp)))
        off += n
    return out


def _pad_lanes(a):
    return jnp.pad(a, [(0, 0)] * (a.ndim - 1) + [(0, LANES - a.shape[-1])])


_NN = (((1,), (0,)), ((), ()))
_NT = (((1,), (1,)), ((), ()))


def _mm(a, b, dims):
    return lax.dot_general(a.astype(MXU), b.astype(MXU), dims, preferred_element_type=f32)


def _ffn_fwd(tag, x, h, w_inT, conv_w, conv_b, mid_comm=None):
    ct, nblk = FFN_CT, FFN // FFN_CT

    def body(h_ref, wg_ref, wv_ref, cw_ref, cb_ref, gp_ref, v_ref, a_ref):
        h_ = h_ref[...]
        gp, v_ = _mm(h_, wg_ref[...], _NT), _mm(h_, wv_ref[...], _NT)
        gp_ref[...] = gp
        v_ref[...] = v_
        a_ref[...] = (_silu(_conv(gp, cw_ref[...], cb_ref[...])) * v_).astype(a_ref.dtype)

    col = pl.BlockSpec((S, ct), lambda j: (0, j))
    gate_pre, val, act = _call(
        body, [h, w_inT, w_inT, conv_w, conv_b], comm=mid_comm, name=f"{tag}_in", grid=(nblk,),
        in_specs=[pl.BlockSpec((S, D), lambda j: (0, 0)), pl.BlockSpec((ct, D), lambda j: (j, 0)),
                  pl.BlockSpec((ct, D), lambda j: (nblk + j, 0)), pl.BlockSpec((CONV_F, ct), lambda j: (0, j)),
                  pl.BlockSpec((1, ct), lambda j: (0, j))],
        out_specs=[col, col, col],
        out_shape=[jax.ShapeDtypeStruct((S, FFN), f32), jax.ShapeDtypeStruct((S, FFN), f32), jax.ShapeDtypeStruct((S, FFN), MXU)],
        sem=("parallel",))
    return act, (x, h, gate_pre, val, act)


FFN_CT = 256
CONV_CT = 256


def _proj_conv(name, h, wT, row0, cw, cb, comm=None):
    C, ct = cw.shape[1], CONV_CT

    def body(h_ref, w_ref, cw_ref, cb_ref, p_ref, c_ref):
        p = _mm(h_ref[...], w_ref[...], _NT)
        p_ref[...] = p
        c_ref[...] = _silu(_conv(p, cw_ref[...], cb_ref[...]))

    col = pl.BlockSpec((S, ct), lambda j: (0, j))
    return _call(
        body, [h, wT, cw, cb], comm=comm, name=name, grid=(C // ct,),
        in_specs=[pl.BlockSpec((S, D), lambda j: (0, 0)), pl.BlockSpec((ct, D), lambda j: (row0 // ct + j, 0)),
                  pl.BlockSpec((CONV_A, ct), lambda j: (0, j)), pl.BlockSpec((1, ct), lambda j: (0, j))],
        out_specs=[col, col], out_shape=[jax.ShapeDtypeStruct((S, C), f32)] * 2, sem=("parallel",))


def _dconv_wgrad(name, pre, dconv, cw, cb, h):
    C, ct = cw.shape[1], CONV_CT

    def body(p_ref, do_ref, cw_ref, cb_ref, h_ref, dp_ref, g_ref, dw_ref, db_ref):
        p_, w_ = p_ref[...], cw_ref[...]
        taps = _taps(p_, CONV_A)
        dx, dw, db = _conv_bwd(p_, w_, do_ref[...] * _dsilu(_conv(p_, w_, cb_ref[...], taps)), taps)
        dpm = dx.astype(MXU)
        dp_ref[...] = dpm
        g_ref[...] = lax.dot_general(dpm, h_ref[...].astype(MXU), (((0,), (0,)), ((), ())),
                                     preferred_element_type=f32).astype(g_ref.dtype)
        dw_ref[...] = dw
        db_ref[...] = db

    col = pl.BlockSpec((S, ct), lambda j: (0, j))
    return _call(
        body, [pre, dconv, cw, cb, h], name=name, grid=(C // ct,),
        in_specs=[col, col, pl.BlockSpec((CONV_A, ct), lambda j: (0, j)), pl.BlockSpec((1, ct), lambda j: (0, j)),
                  pl.BlockSpec((S, D), lambda j: (0, 0))],
        out_specs=[col, pl.BlockSpec((ct, D), lambda j: (j, 0)), pl.BlockSpec((CONV_A, ct), lambda j: (0, j)),
                   pl.BlockSpec((1, ct), lambda j: (0, j))],
        out_shape=[jax.ShapeDtypeStruct((S, C), MXU), jax.ShapeDtypeStruct((C, D), MXU),
                   jax.ShapeDtypeStruct((CONV_A, C), f32), jax.ShapeDtypeStruct((1, C), f32)],
        sem=("parallel",))


def _ffn_mid_bwd(name, dout, w_down, gate_pre, val, conv_w, conv_b, comm=None):
    ct = FFN_CT

    def body(do_ref, wd_ref, gp_ref, v_ref, w_ref, b_ref, dgv_ref, dw_ref, db_ref, dob_s):
        @pl.when(pl.program_id(0) == 0)
        def _():
            dob_s[...] = do_ref[...].astype(MXU)

        da = _mm(dob_s[...], wd_ref[...], _NT)
        gp, v_, w_ = gp_ref[...], v_ref[...], w_ref[...]
        taps = _taps(gp, CONV_F)
        gate = _conv(gp, w_, b_ref[...], taps)
        sg = _sigmoid(gate)
        dgp, dw, db = _conv_bwd(gp, w_, da * v_ * (sg * (1.0 + gate * (1.0 - sg))), taps)
        dgv_ref[0] = dgp.astype(dgv_ref.dtype)
        dgv_ref[1] = (da * (gate * sg)).astype(dgv_ref.dtype)
        dw_ref[...] = dw
        db_ref[...] = db

    col = pl.BlockSpec((S, ct), lambda j: (0, j))
    return _call(
        body, [dout, w_down, gate_pre, val, conv_w, conv_b], comm=comm, name=name, grid=(FFN // ct,),
        in_specs=[pl.BlockSpec((S, D), lambda j: (0, 0)), pl.BlockSpec((ct, D), lambda j: (j, 0)), col, col,
                  pl.BlockSpec((CONV_F, ct), lambda j: (0, j)), pl.BlockSpec((1, ct), lambda j: (0, j))],
        out_specs=[pl.BlockSpec((2, S, ct), lambda j: (0, 0, j)), pl.BlockSpec((CONV_F, ct), lambda j: (0, j)),
                   pl.BlockSpec((1, ct), lambda j: (0, j))],
        out_shape=[jax.ShapeDtypeStruct((2, S, FFN), MXU), jax.ShapeDtypeStruct((CONV_F, FFN), f32), jax.ShapeDtypeStruct((1, FFN), f32)],
        scratch=[pltpu.VMEM((S, D), MXU)], sem=("arbitrary",))


def _ffn_bwd(tag, layer, saved, norm_g, w_inT, conv_w, conv_b, w_down, dout, carry=None):
    x, h, gate_pre, val, act = saved
    g_down = matmul(f"{tag}_wdown", act, dout, "tn", out_dtype=MXU, comm=carry and carry.sibling())
    dgv, g_cw, g_cb = _ffn_mid_bwd(f"{tag}_dmid", dout, w_down, gate_pre, val, conv_w, conv_b, comm=carry and carry.chips())
    g_inT = matmul_tn_stacked(f"{tag}_win", dgv, h, MXU)

    def din_fn(dg_, dv_, x_, do_, g_, wT):
        dx, dg = _rms_bwd(x_, g_, _mm(dg_, wT[:FFN], _NN) + _mm(dv_, wT[FFN:], _NN))
        return do_ + dx, dg

    rs = ReduceScatter(tag, (f"f_inT{layer}", f"f_down{layer}"), [g_inT, g_down])
    dx, g_norm = rowwise(f"{tag}_din", din_fn, [(dgv, 0), (dgv, 1), x, dout], [norm_g, w_inT], [(D, f32)], [(1, D)],
                         comm=rs.sibling())
    return dx, {f"f_norm{layer}": g_norm, f"f_conv_w{layer}": g_cw, f"f_conv_b{layer}": g_cb}, rs


def _land(W, keys, comm):
    for k, g in zip(keys, comm.results):
        W[k] = g.reshape(-1, g.shape[2])


def _local_step(x, pos, tgt, W, shards):
    G = {}
    gather = lambda *keys: gather_comm([shards[k] for k in keys])
    inv_freq = (ROPE_THETA ** (-jnp.arange(AH // 2, dtype=f32) / (AH // 2))).reshape(1, AH // 2)

    def in_fn(x_, g_, wT, wdtT):
        h_ = _rms_fwd(x_, g_).astype(MXU)
        return h_, _mm(h_, wT[slice(*IN_ROWS["z"])], _NT), _mm(h_, wdtT, _NT)

    h0, z, dt_pre = rowwise("a_in", in_fn, [x], [W["a_norm"], W["inT"], W["in_dtT"]], [(D, MXU), (INNER, f32), (LANES, f32)])
    pre, conv = {}, {}
    early = {"xs": ("a_out",), "B": (), "C": ()}
    for k in ("xs", "B", "C"):
        c = gather(*early[k]) if early[k] else None
        pre[k], conv[k] = _proj_conv(f"a_in_{k}", h0, W["inT"], IN_ROWS[k][0], W[f"cw_{k}"], W[f"cb_{k}"], comm=c)
        if c is not None:
            _land(W, early[k], c)
    c = gather("f_inT0", "f_down0")
    y, states = ssd_fwd(conv["xs"], conv["B"], conv["C"], dt_pre, W["dt_bias"], W["A_log"], W["D"], comm=c)
    _land(W, ("f_inT0", "f_down0"), c)

    def gate_norm(y_, z_, g_):
        yg = y_ * _silu(z_)
        w = INNER // GROUPS
        return (jnp.concatenate([_rms_fwd(yg[:, w * i:w * (i + 1)], g_[:, w * i:w * (i + 1)]) for i in range(GROUPS)], axis=1),)

    def out_fn(y_, z_, x_, g_, w_, gf_):
        (gn_,) = gate_norm(y_, z_, g_)
        gn_ = gn_.astype(MXU)
        x1_ = x_ + _mm(gn_, w_, _NN)
        return gn_, x1_, _rms_fwd(x1_, gf_)

    c = gather("f_down1")
    gn, x1, h1 = rowwise("a_out", out_fn, [y, z, x], [W["a_gnorm"], W["a_out"], W["f_norm0"]],
                         [(INNER, MXU), (D, f32), (D, MXU)], comm=c)
    _land(W, ("f_down1",), c)

    c = gather("w_kv", "w_q", "w_o")
    act0, ffn0 = _ffn_fwd("f0", x1, h1, W["f_inT0"], W["f_cw0"], W["f_cb0"], mid_comm=c)
    _land(W, ("w_kv", "w_q", "w_o"), c)
    kw = KVH * AH

    def qkv_fn(a_, x_, pos_, wd, gk, gb, wkv, bkv, wq, bq, if_, kn, qn):
        x2_ = x_ + _mm(a_, wd, _NN)
        kvn_, h2_ = _rms_fwd(x2_, gk).astype(MXU), _rms_fwd(x2_, gb).astype(MXU)
        kv_, qp_ = _mm(kvn_, wkv, _NN) + bkv, _mm(h2_, wq, _NN) + bq
        cos, sin = _rope_tables(pos_, if_)
        return (x2_, kvn_, h2_, kv_, qp_, _headnorm_rope_fwd(kv_[:, :kw], kn, cos, sin, KVH), kv_[:, kw:],
                _headnorm_rope_fwd(qp_, qn, cos, sin, QH))

    x2, kvn, h2, kv, q_pre, k_rot, v_val, q = rowwise(
        "f0_down_qkv", qkv_fn, [act0, x1, pos],
        [W["f_down0"], W["kv_norm"], W["b_norm"], W["w_kv"], W["b_kv"], W["w_q"], W["b_q"], inv_freq, W["k_norm"], W["q_norm"]],
        [(D, f32), (D, MXU), (D, MXU), (2 * kw, f32), (D, f32), (kw, f32), (kw, f32), (D, f32)])
    c = gather("f_inT1")
    att = attn_fwd(q, k_rot, v_val, W["sinks"], comm=c)
    _land(W, ("f_inT1",), c)
    def o_fn(att_, x_, w_, b_, gf_):
        x3_ = x_ + _mm(att_, w_, _NN) + b_
        return x3_, _rms_fwd(x3_, gf_)

    x3, h3 = rowwise("o_proj", o_fn, [att, x2], [W["w_o"], W["b_o"], W["f_norm1"]], [(D, f32), (D, MXU)])

    act1, ffn1 = _ffn_fwd("f1", x3, h3, W["f_inT1"], W["f_cw1"], W["f_cb1"])

    def loss_fn(a_, x_, t_, w_):
        diff = x_ + _mm(a_, w_, _NN) - t_
        rows = jnp.sum(diff * diff, axis=1, keepdims=True) * (0.5 / D)
        return diff * (1.0 / D), jnp.sum(rows, axis=0, keepdims=True)

    dx4, loss = rowwise("f1_down_loss", loss_fn, [act1, x3, tgt], [W["f_down1"]], [(D, f32)], [(1, 1)])

    dx3, g, rs_f1 = _ffn_bwd("f1", 1, ffn1, W["f_norm1"], W["f_inT1"], W["f_cw1"], W["f_cb1"], W["f_down1"], dx4)
    G.update(g)

    datt = matmul("o_dproj", dx3, W["w_o"], "nt")
    g_wo = matmul("o_wproj", att, dx3, "tn", out_dtype=MXU)
    dq, dkp, dkc, dvp, dvc, G["sinks"] = attn_bwd(q, k_rot, v_val, W["sinks"], datt, comm=rs_f1.chips())

    dkv, G["k_norm"], G["b_kv"], g_wkv = kv_bwd(kv, kvn, pos, inv_freq, W["k_norm"], dkp, dkc, dvp, dvc)

    def x2_bwd(q_, pos_, dq_, dx_, x_, dkv_, h2_, if_, g_, gb_, gk_, wq, wkv):
        cos, sin = _rope_tables(pos_, if_)
        dqp, dg = _headnorm_rope_bwd(q_, g_, cos, sin, dq_, QH)
        dqm = dqp.astype(MXU)
        d1, dgb = _rms_bwd(x_, gb_, _mm(dqm, wq, _NT))
        d2, dgk = _rms_bwd(x_, gk_, _mm(dkv_, wkv, _NT))
        g_wq_ = lax.dot_general(h2_, dqm, (((0,), (0,)), ((), ())), preferred_element_type=f32)
        return (dx_ + d1 + d2, dg, jnp.sum(dqp, axis=0, keepdims=True), jnp.sum(dx_, axis=0, keepdims=True), dgb, dgk, g_wq_)

    dx2, G["q_norm"], G["b_q"], G["b_o"], G["b_norm"], G["kv_norm"], g_wq = rowwise(
        "qkv_dproj", x2_bwd, [q_pre, pos, dq, dx3, x2, dkv, h2], [inv_freq, W["q_norm"], W["b_norm"], W["kv_norm"], W["w_q"], W["w_kv"]],
        [(D, f32)], [(1, AH), (1, D), (1, D), (1, D), (1, D), (D, D)])
    rs_att = ReduceScatter("att", ("w_kv", "w_q", "w_o"), [g_wkv.astype(MXU), g_wq.astype(MXU), g_wo])

    dx1, g, rs_f0 = _ffn_bwd("f0", 0, ffn0, W["f_norm0"], W["f_inT0"], W["f_cw0"], W["f_cb0"], W["f_down0"], dx2, carry=rs_att)
    G.update(g)

    rs_out = ReduceScatter("a_out", ("a_out",), [matmul("a_wout", gn, dx1, "tn", out_dtype=MXU)])

    def gate_norm_bwd(y_, z_, dx_, g_, w_out):
        dgn_ = _mm(dx_, w_out, _NT)
        w = INNER // GROUPS
        sg = _sigmoid(z_)
        sz = z_ * sg
        yg = y_ * sz
        parts, dgs = [], []
        for i in range(GROUPS):
            dseg, dg = _rms_bwd(yg[:, w * i:w * (i + 1)], g_[:, w * i:w * (i + 1)], dgn_[:, w * i:w * (i + 1)])
            parts.append(dseg)
            dgs.append(dg)
        dyg = jnp.concatenate(parts, axis=1)
        return dyg * sz, dyg * y_ * (sg * (1.0 + z_ * (1.0 - sg))), jnp.concatenate(dgs, axis=1)

    dy, dz, G["a_gnorm"] = rowwise("a_dout", gate_norm_bwd, [y, z, dx1], [W["a_gnorm"], W["a_out"]],
                                   [(INNER, f32), (INNER, MXU)], [(1, INNER)], comm=rs_out.sibling())
    dconv = {}
    dconv["xs"], dconv["B"], dconv["C"], ddt_pre, G["dt_bias"], G["A_log"], G["D"] = ssd_bwd(
        conv["xs"], conv["B"], conv["C"], dt_pre, W["dt_bias"], W["A_log"], W["D"], states, dy,
        comm=merge_comms([rs_f0.chips(), rs_out.chips()]))

    g_in, dpre = [matmul("a_win_z", dz, h0, "tn", out_dtype=MXU)], {}
    for k in ("xs", "B", "C"):
        dpre[k], g_k, G[f"cw_{k}"], G[f"cb_{k}"] = _dconv_wgrad(f"a_dconv_{k}", pre[k], dconv[k], W[f"cw_{k}"], W[f"cb_{k}"], h0)
        g_in.append(g_k)
    g_in.append(matmul("a_win_dt", ddt_pre, h0, "tn", out_dtype=MXU)[:HEADS])
    rs_in = ReduceScatter("a_in", ("inT",), [jnp.concatenate(g_in, axis=0)])
    run_comm("rs_in_sibling", rs_in.sibling())

    def x0_bwd(dz_, dxs_, db_, dc_, ddt_, x_, do_, g_, wT, wdtT):
        parts = zip((dz_, dxs_, db_, dc_), IN_ROWS.values())
        dh = sum(_mm(d_, wT[a:b], _NN) for d_, (a, b) in parts) + _mm(ddt_, wdtT, _NN)
        dx, dg = _rms_bwd(x_, g_, dh)
        return do_ + dx, dg

    dx, G["a_norm"] = rowwise("a_din", x0_bwd, [dz, dpre["xs"], dpre["B"], dpre["C"], ddt_pre, x, dx1],
                              [W["a_norm"], W["inT"], W["in_dtT"]], [(D, f32)], [(1, D)], comm=rs_in.chips())
    return loss, dx, G, [rs_f1, rs_att, rs_f0, rs_out, rs_in]


ROW_KEYS = ("inT", "a_out", "f_inT0", "f_down0", "w_kv", "w_q", "w_o", "f_inT1", "f_down1")


def _row_blocks(src):
    return {"inT": src["a_in_proj"][0].T, "a_out": src["a_out_proj"][0], "w_kv": src["w_kv"], "w_q": src["w_q"][0],
            "w_o": src["w_o"][0], "f_inT0": src["f_w_in"][0].T, "f_inT1": src["f_w_in"][1].T,
            "f_down0": src["f_w_down"][0], "f_down1": src["f_w_down"][1]}


def _from_row_blocks(rb):
    out = {"a_in_proj": rb["inT"].T[None], "a_out_proj": rb["a_out"][None], "w_kv": rb["w_kv"], "w_q": rb["w_q"][None],
           "w_o": rb["w_o"][None]}
    if "f_inT0" in rb:
        out["f_w_in"] = jnp.stack([rb["f_inT0"].T, rb["f_inT1"].T])
        out["f_w_down"] = jnp.stack([rb["f_down0"], rb["f_down1"]])
    return out


SMALL_SHARDED = ("a_norm", "a_conv_w", "a_conv_b", "a_gnorm", "f_conv_w")
REPLICATED = ("a_dt_bias", "a_A_log", "a_D", "kv_norm", "b_kv", "k_norm", "b_norm", "b_q", "q_norm", "sinks", "b_o",
              "f_norm", "f_conv_b")
ORDER = ("a_norm", "a_in_proj", "a_conv_w", "a_conv_b", "a_dt_bias", "a_A_log", "a_D", "a_gnorm", "a_out_proj", "kv_norm",
         "w_kv", "b_kv", "k_norm", "b_norm", "w_q", "b_q", "q_norm", "sinks", "w_o", "b_o", "f_norm", "f_w_in",
         "f_conv_w", "f_conv_b", "f_w_down")


def _gathered_to_whole(name, g):
    if name == "a_conv_w":
        return jnp.moveaxis(g[:, 0], 0, 1).reshape(g.shape[2], -1)
    if name in ("a_norm", "a_conv_b", "a_gnorm"):
        return g[:, 0].reshape(1, -1)
    if name == "f_conv_w":
        return jnp.moveaxis(g, 0, 2).reshape(g.shape[1], g.shape[2], -1)
    raise ValueError(name)


def _whole_to_shards(name, w):
    if name == "a_conv_w":
        return jnp.moveaxis(w.reshape(w.shape[0], N_DEV, -1), 1, 0)[:, None]
    if name in ("a_norm", "a_conv_b", "a_gnorm"):
        return w.reshape(N_DEV, 1, -1)
    if name == "f_conv_w":
        return jnp.moveaxis(w.reshape(w.shape[0], w.shape[1], N_DEV, -1), 2, 0)
    raise ValueError(name)


def _small_weights(whole):
    W = {}
    cw, cb = whole["a_conv_w"], whole["a_conv_b"]
    o = 0
    for k, n in (("xs", INNER), ("B", GROUPS * NSTATE), ("C", GROUPS * NSTATE)):
        W[f"cw_{k}"], W[f"cb_{k}"] = cw[:, o:o + n], cb[:, o:o + n]
        o += n
    W["a_norm"], W["a_gnorm"] = whole["a_norm"], whole["a_gnorm"]
    W["dt_bias"], W["A_log"], W["D"] = (_pad_lanes(whole[k]) for k in ("a_dt_bias", "a_A_log", "a_D"))
    W["kv_norm"], W["b_kv"], W["k_norm"] = whole["kv_norm"].reshape(1, -1), whole["b_kv"].reshape(1, -1), whole["k_norm"].reshape(1, -1)
    for k in ("b_norm", "b_q", "q_norm", "sinks", "b_o"):
        W[k] = whole[k]
    for i in range(2):
        W[f"f_norm{i}"] = whole["f_norm"][i:i + 1]
        W[f"f_cw{i}"], W[f"f_cb{i}"] = whole["f_conv_w"][i], whole["f_conv_b"][i:i + 1]
    return W


def _small_grads(G, shapes):
    nh = HEADS
    out = {
        "a_conv_w": jnp.concatenate([G["cw_xs"], G["cw_B"], G["cw_C"]], axis=1),
        "a_conv_b": jnp.concatenate([G["cb_xs"], G["cb_B"], G["cb_C"]], axis=1),
        "a_norm": G["a_norm"], "a_gnorm": G["a_gnorm"],
        "a_dt_bias": G["dt_bias"][:, :nh], "a_A_log": G["A_log"][:, :nh], "a_D": G["D"][:, :nh],
        "kv_norm": G["kv_norm"], "b_kv": G["b_kv"], "k_norm": G["k_norm"], "b_norm": G["b_norm"],
        "b_q": G["b_q"], "q_norm": G["q_norm"], "sinks": G["sinks"], "b_o": G["b_o"],
        "f_norm": jnp.concatenate([G["f_norm0"], G["f_norm1"]], axis=0),
        "f_conv_w": jnp.stack([G["f_conv_w0"], G["f_conv_w1"]]),
        "f_conv_b": jnp.concatenate([G["f_conv_b0"], G["f_conv_b1"]], axis=0),
    }
    return {k: val.reshape(shapes[k]) if k in shapes else val for k, val in out.items()}


def kernel(x, positions, a_norm, a_in_proj, a_conv_w, a_conv_b, a_dt_bias, a_A_log, a_D, a_gnorm, a_out_proj, kv_norm, w_kv, b_kv, k_norm, b_norm, w_q, b_q, q_norm, sinks, w_o, b_o, f_norm, f_w_in, f_conv_w, f_conv_b, f_w_down, loss_target, m_a_norm, m_a_in_proj, m_a_conv_w, m_a_conv_b, m_a_dt_bias, m_a_A_log, m_a_D, m_a_gnorm, m_a_out_proj, m_kv_norm, m_w_kv, m_b_kv, m_k_norm, m_b_norm, m_w_q, m_b_q, m_q_norm, m_sinks, m_w_o, m_b_o, m_f_norm, m_f_w_in, m_f_conv_w, m_f_conv_b, m_f_w_down, v_a_norm, v_a_in_proj, v_a_conv_w, v_a_conv_b, v_a_dt_bias, v_a_A_log, v_a_D, v_a_gnorm, v_a_out_proj, v_kv_norm, v_w_kv, v_b_kv, v_k_norm, v_b_norm, v_w_q, v_b_q, v_q_norm, v_sinks, v_w_o, v_b_o, v_f_norm, v_f_w_in, v_f_conv_w, v_f_conv_b, v_f_w_down):
    given = dict(locals())
    w_in = {n: given[n] for n in ORDER}
    m_in = {n: given["m_" + n] for n in ORDER}
    v_in = {n: given["v_" + n] for n in ORDER}
    dev = 4 * lax.axis_index("x") + 2 * lax.axis_index("y") + lax.axis_index("c")

    w2, m2, v2 = _row_blocks(w_in), _row_blocks(m_in), _row_blocks(v_in)
    small_pack = _pack([w_in[n] for n in SMALL_SHARDED], 8, f32)
    shards = {k: w2[k].astype(MXU) for k in ROW_KEYS}
    in_all, small_all = run_comm("ag_head", gather_comm([shards["inT"], small_pack]))
    whole = {n: w_in[n] for n in REPLICATED}
    for n, g in zip(SMALL_SHARDED, _unpack(small_all, [w_in[n].shape for n in SMALL_SHARDED], lead=(N_DEV,))):
        whole[n] = _gathered_to_whole(n, g)
    W = _small_weights(whole)
    W["inT"] = in_all.reshape(-1, D)
    W["in_dtT"] = jnp.pad(W["inT"][IN_COLS - HEADS:], ((0, LANES - HEADS), (0, 0)))

    loss, dx, G, scatters = _local_step(x[0], positions.reshape(S, 1).astype(f32), loss_target[0], W, shards)
    grads = _small_grads(G, {n: whole[n].shape for n in REPLICATED})

    small_names = SMALL_SHARDED + REPLICATED
    small_part = _pack([grads[n] for n in small_names] + [loss], 8, f32)
    small_gather = gather_comm([small_part])
    run_comm("ag_small_grads", small_gather)
    parts = {}
    for rs in scatters:
        parts.update(rs.parts())

    single = tuple(k for k in ROW_KEYS if not k.startswith("f_"))
    stepped = {k: final_adamw(f"adamw_{k}", *parts[k], w2[k], m2[k], v2[k]) for k in single}
    g_out, delta, new_m, new_v = (_from_row_blocks({k: stepped[k][i] for k in single}) for i in range(4))
    for n, key, lay in (("f_w_in", "f_inT", lambda a: jnp.swapaxes(a, 1, 2)), ("f_w_down", "f_down", lambda a: a)):
        res = final_adamw_layers(f"adamw_{n}", *parts[key + "0"], *parts[key + "1"], lay(w_in[n]), lay(m_in[n]), lay(v_in[n]))
        g_out[n], delta[n], new_m[n], new_v[n] = (lay(a) for a in res)
    *small_sums, loss_all = _unpack(sum_devices(small_gather.results[0]), [grads[n].shape for n in small_names] + [(1, 1)])
    for n, g in zip(small_names, small_sums):
        if n in SMALL_SHARDED:
            g_out[n] = lax.dynamic_index_in_dim(_whole_to_shards(n, g), dev, axis=0, keepdims=False)
        else:
            g_out[n] = g.reshape(w_in[n].shape)

    packs = [_pack([src[n] for n in small_names], 8, f32) for src in (w_in, g_out, m_in, v_in)]
    outs = adamw("adamw_small", *packs)
    for dst, buf in zip((delta, new_m, new_v), outs):
        for n, a in zip(small_names, _unpack(buf, [w_in[n].shape for n in small_names])):
            dst[n] = a

    return (loss_all[0, 0], dx[None], *[g_out[n] for n in ORDER], *[delta[n] for n in ORDER],
            *[new_m[n] for n in ORDER], *[new_v[n] for n in ORDER])
```

```python
import functools
import math

import jax
import jax.numpy as jnp
from jax import lax
from jax.experimental import pallas as pl
from jax.experimental.pallas import tpu as pltpu

f32 = jnp.float32
bf16 = jnp.bfloat16
MXU = bf16

N_DEV = 8
S = 2048
D = 1024
EPS = 1e-5
INNER = 2048
HEADS = 32
HP = 64
GROUPS = 8
HPG = HEADS // GROUPS
NSTATE = 128
CONV_A = 4
CHUNK = 256
NCHUNK = S // CHUNK
AH = 64
QH = 16
KVH = 4
QPK = QH // KVH
WIN = 128
NBLK = S // WIN
ROPE_THETA = 10000.0
FFN = 2816
CONV_F = 3
LANES = 128
PACK_W = 1024
VMEM_LIMIT = 56 * 1024 * 1024

ADAM_LR, ADAM_B1, ADAM_B2, ADAM_EPS, ADAM_WD, ADAM_STEP = 0.001, 0.9, 0.999, 1e-08, 0.01, 10

MESH = pl.DeviceIdType.MESH


def _cparams(sem=None):
    return pltpu.CompilerParams(dimension_semantics=sem, vmem_limit_bytes=VMEM_LIMIT)


def _pick(n, cands):
    for c in cands:
        if n % c == 0:
            return c
    return n


class Comm:
    def __init__(self, ins, out_shapes, sems, start, finish):
        self.ins, self.out_shapes, self.sems, self.start, self.finish = list(ins), list(out_shapes), list(sems), start, finish
        self.results, self.children = None, ()

    def set_results(self, res):
        self.results, o = list(res), 0
        for ch in self.children:
            ch.set_results(res[o:o + len(ch.out_shapes)])
            o += len(ch.out_shapes)


def merge_comms(comms):
    def each(fn_name, ins, outs, sems):
        i = o = s = 0
        for c in comms:
            getattr(c, fn_name)(ins[i:i + len(c.ins)], outs[o:o + len(c.out_shapes)], sems[s:s + len(c.sems)])
            i, o, s = i + len(c.ins), o + len(c.out_shapes), s + len(c.sems)

    merged = Comm([a for c in comms for a in c.ins], [a for c in comms for a in c.out_shapes], [a for c in comms for a in c.sems],
                  functools.partial(each, "start"), functools.partial(each, "finish"))
    merged.children = tuple(comms)
    return merged


def _call(body, args, *, name, grid, in_specs, out_specs, out_shape, scratch=(), sem=None, comm=None):
    if comm is None:
        return pl.pallas_call(body, name=name, grid=grid, in_specs=list(in_specs), out_specs=list(out_specs),
                              out_shape=list(out_shape), scratch_shapes=list(scratch), compiler_params=_cparams(sem))(*args)
    n_in, n_out, n_scr, c_in, c_out = len(in_specs), len(out_shape), len(scratch), len(comm.ins), len(comm.out_shapes)
    any_spec = pl.BlockSpec(memory_space=pl.ANY)

    def outer(*refs):
        ins, c_ins = refs[:n_in], refs[n_in:n_in + c_in]
        o = n_in + c_in
        outs, c_outs = refs[o:o + n_out], refs[o + n_out:o + n_out + c_out]
        o += n_out + c_out
        scr, c_sems = refs[o:o + n_scr], refs[o + n_scr:]
        ids = [pl.program_id(i) for i in range(len(grid))]
        first = functools.reduce(jnp.logical_and, [i == 0 for i in ids])
        last = functools.reduce(jnp.logical_and, [i == g - 1 for i, g in zip(ids, grid)])

        @pl.when(first)
        def _():
            comm.start(c_ins, c_outs, c_sems)

        body(*ins, *outs, *scr)

        @pl.when(last)
        def _():
            comm.finish(c_ins, c_outs, c_sems)

    res = pl.pallas_call(
        outer, name=name, grid=grid, in_specs=list(in_specs) + [any_spec] * c_in,
        out_specs=list(out_specs) + [any_spec] * c_out, out_shape=list(out_shape) + comm.out_shapes,
        scratch_shapes=list(scratch) + comm.sems, compiler_params=_cparams(("arbitrary",) * len(grid)),
    )(*args, *comm.ins)
    comm.set_results(res[n_out:])
    return res[:n_out]


def matmul(name, a, b, mode, out_dtype=f32, bias=None, residual=None):
    if mode == "nn":
        (M, K), (K2, N) = a.shape, b.shape
    elif mode == "nt":
        (M, K), (N, K2) = a.shape, b.shape
    else:
        (K, M), (K2, N) = a.shape, b.shape
    assert K == K2, (name, a.shape, b.shape)
    if mode == "tn":
        tm, tn = M, _pick(N, (512, 256, 128) if M <= 1024 else (256, 128))
        a_spec = pl.BlockSpec((K, M), lambda j: (0, 0))
        b_spec = pl.BlockSpec((K, tn), lambda j: (0, j))
        dims = (((0,), (0,)), ((), ()))
        grid, o_map, row_map = (N // tn,), (lambda j: (0, j)), (lambda j: (0, j))
    else:
        tm, tn = (256 if N >= 2048 else 512), N
        a_spec = pl.BlockSpec((tm, K), lambda i: (i, 0))
        b_spec = pl.BlockSpec(b.shape, lambda i: (0, 0))
        dims = (((1,), (0,)), ((), ())) if mode == "nn" else (((1,), (1,)), ((), ()))
        grid, o_map, row_map = (M // tm,), (lambda i: (i, 0)), (lambda i: (0, 0))
    ins, in_specs = [a, b], [a_spec, b_spec]
    if bias is not None:
        ins.append(bias)
        in_specs.append(pl.BlockSpec((1, tn), row_map))
    if residual is not None:
        ins.append(residual)
        in_specs.append(pl.BlockSpec((tm, tn), o_map))
    has_bias, has_res = bias is not None, residual is not None

    def body(a_ref, b_ref, *rest):
        rest = list(rest)
        bias_ref = rest.pop(0) if has_bias else None
        res_ref = rest.pop(0) if has_res else None
        (o_ref,) = rest
        r = lax.dot_general(a_ref[...].astype(MXU), b_ref[...].astype(MXU), dims, preferred_element_type=f32)
        if has_bias:
            r = r + bias_ref[...]
        if has_res:
            r = r + res_ref[...]
        o_ref[...] = r.astype(out_dtype)

    return pl.pallas_call(
        body, name=name, grid=grid, in_specs=in_specs,
        out_specs=pl.BlockSpec((tm, tn), o_map),
        out_shape=jax.ShapeDtypeStruct((M, N), out_dtype),
        compiler_params=_cparams(("parallel",)),
    )(*ins)


def matmul_tn_stacked(name, a, b, out_dtype):
    R, K, M = a.shape
    N = b.shape[1]
    tn = _pick(N, (256, 128))

    def body(a_ref, b_ref, o_ref):
        o_ref[0] = lax.dot_general(a_ref[0].astype(MXU), b_ref[...].astype(MXU), (((0,), (0,)), ((), ())),
                                   preferred_element_type=f32).astype(out_dtype)

    out = pl.pallas_call(
        body, name=name, grid=(R, N // tn),
        in_specs=[pl.BlockSpec((1, K, M), lambda r, j: (r, 0, 0)), pl.BlockSpec((K, tn), lambda r, j: (0, j))],
        out_specs=pl.BlockSpec((1, M, tn), lambda r, j: (r, 0, j)),
        out_shape=jax.ShapeDtypeStruct((R, M, N), out_dtype),
        compiler_params=_cparams(("parallel", "parallel")),
    )(a, b)
    return out.reshape(R * M, N)


def rowwise(name, fn, rows, pars, outs, accs=(), tile=256, comm=None):
    n_in, n_out = len(rows) + len(pars), len(outs)
    in_specs = [pl.BlockSpec((None, tile, r[0].shape[2]), functools.partial(lambda i, lead: (lead, i, 0), lead=r[1]))
                if isinstance(r, tuple) else pl.BlockSpec((tile, r.shape[1]), lambda i: (i, 0)) for r in rows]
    rows = [r[0] if isinstance(r, tuple) else r for r in rows]
    in_specs += [pl.BlockSpec(p.shape, lambda i: (0, 0)) for p in pars]
    out_specs = [pl.BlockSpec((tile, c), lambda i: (i, 0)) for c, _ in outs]
    out_specs += [pl.BlockSpec(shp, lambda i: (0, 0)) for shp in accs]
    out_shape = [jax.ShapeDtypeStruct((S, c), dt) for c, dt in outs]
    out_shape += [jax.ShapeDtypeStruct(shp, f32) for shp in accs]

    def body(*refs):
        res = fn(*[r[...] for r in refs[:n_in]])
        o_refs = refs[n_in:n_in + n_out]
        a_refs = refs[n_in + n_out:]
        for ref, val in zip(o_refs, res[:n_out]):
            ref[...] = val.astype(ref.dtype)
        if a_refs:
            @pl.when(pl.program_id(0) == 0)
            def _():
                for ref in a_refs:
                    ref[...] = jnp.zeros_like(ref)
            for ref, val in zip(a_refs, res[n_out:]):
                ref[...] += val

    return _call(body, [*rows, *pars], name=name, grid=(S // tile,), in_specs=in_specs, out_specs=out_specs,
                 out_shape=out_shape, sem=("arbitrary",) if accs else ("parallel",), comm=comm)


def _sigmoid(x):
    return 0.5 * jnp.tanh(0.5 * x) + 0.5


def _silu(x):
    return x * _sigmoid(x)


def _dsilu(x):
    sg = _sigmoid(x)
    return sg * (1.0 + x * (1.0 - sg))


def _softplus(x):
    return jnp.maximum(x, 0.0) + jnp.log(1.0 + jnp.exp(-jnp.abs(x)))


def _rms_fwd(x, g):
    r = lax.rsqrt(jnp.mean(x * x, axis=-1, keepdims=True) + EPS)
    return x * r * g


def _rms_bwd(x, g, dh):
    r = lax.rsqrt(jnp.mean(x * x, axis=-1, keepdims=True) + EPS)
    xh = x * r
    dxh = dh * g
    dx = r * (dxh - xh * jnp.mean(dxh * xh, axis=-1, keepdims=True))
    return dx, jnp.sum(dh * xh, axis=0, keepdims=True)


def _taps(x, width):
    row = lax.broadcasted_iota(jnp.int32, (8, x.shape[1]), 0)

    def shifted(s):
        r = pltpu.roll(x, s, 0)
        return jnp.concatenate([jnp.where(row >= s, r[:8], 0.0), r[8:]], axis=0)

    return [shifted(s) for s in range(width - 1, 0, -1)] + [x]


def _conv(x, w, b, taps=None):
    width = w.shape[0]
    taps = _taps(x, width) if taps is None else taps
    out = b + w[0:1, :] * taps[0]
    for k in range(1, width):
        out = out + w[k:k + 1, :] * taps[k]
    return out


def _conv_bwd(x, w, dc, taps=None):
    width, n = w.shape[0], x.shape[0]
    taps = _taps(x, width) if taps is None else taps
    row = lax.broadcasted_iota(jnp.int32, (8, x.shape[1]), 0)
    dx = w[width - 1:width, :] * dc
    for k in range(width - 1):
        s = width - 1 - k
        r = pltpu.roll(dc, n - s, 0)
        dx = dx + w[k:k + 1, :] * jnp.concatenate([r[:n - 8], jnp.where(row < 8 - s, r[n - 8:], 0.0)], axis=0)
    dw = jnp.concatenate([jnp.sum(dc * t, axis=0, keepdims=True) for t in taps], axis=0)
    return dx, dw, jnp.sum(dc, axis=0, keepdims=True)


def _rope_tables(pos, inv_freq):
    ang = pos * inv_freq
    return jnp.cos(ang), jnp.sin(ang)


def _split2(v):
    hi = v.astype(bf16)
    return hi, (v - hi.astype(f32)).astype(bf16)


def _head_maps(width):
    shift = AH.bit_length() - 1
    to_head = (lax.broadcasted_iota(jnp.int32, (width, LANES), 0) >> shift) == lax.broadcasted_iota(jnp.int32, (width, LANES), 1)
    from_head = lax.broadcasted_iota(jnp.int32, (LANES, width), 0) == (lax.broadcasted_iota(jnp.int32, (LANES, width), 1) >> shift)
    return to_head.astype(bf16), from_head.astype(bf16)


def _head_sums(v, to_head):
    hi, lo = _split2(v)
    return jnp.dot(hi, to_head, preferred_element_type=f32) + jnp.dot(lo, to_head, preferred_element_type=f32)


def _head_spread(s, from_head):
    hi, lo = _split2(s)
    return jnp.dot(hi, from_head, preferred_element_type=f32) + jnp.dot(lo, from_head, preferred_element_type=f32)


def _rope_full(cos, sin, width):
    half = AH // 2
    pad = jnp.zeros((cos.shape[0], LANES - half), f32)
    r = lax.broadcasted_iota(jnp.int32, (LANES, width), 0)
    lane = lax.broadcasted_iota(jnp.int32, (LANES, width), 1)
    spread = ((lane & (half - 1)) == r).astype(bf16)
    full = lambda t: _head_spread(jnp.concatenate([t, pad], axis=1), spread)
    first = (lax.broadcasted_iota(jnp.int32, (1, width), 1) & (AH - 1)) < half
    sin_f = full(sin)
    return full(cos), jnp.where(first, -sin_f, sin_f), first


def _swap_halves(v, first):
    half, width = AH // 2, v.shape[1]
    return jnp.where(first, pltpu.roll(v, width - half, 1), pltpu.roll(v, half, 1))


def _headnorm_rope_fwd(x, g, cos, sin, heads):
    to_head, from_head = _head_maps(heads * AH)
    cos_f, sin_s, first = _rope_full(cos, sin, heads * AH)
    r = _head_spread(lax.rsqrt(_head_sums(x * x, to_head) * (1.0 / AH) + EPS), from_head)
    n = x * r * jnp.tile(g, (1, heads))
    return n * cos_f + _swap_halves(n, first) * sin_s


def _headnorm_rope_bwd(x, g, cos, sin, dout, heads):
    width = heads * AH
    to_head, from_head = _head_maps(width)
    cos_f, sin_s, first = _rope_full(cos, sin, width)
    r = _head_spread(lax.rsqrt(_head_sums(x * x, to_head) * (1.0 / AH) + EPS), from_head)
    xh = x * r
    dn = dout * cos_f - _swap_halves(dout, first) * sin_s
    dxh = dn * jnp.tile(g, (1, heads))
    m = _head_spread(_head_sums(dxh * xh, to_head) * (1.0 / AH), from_head)
    dx = r * (dxh - xh * m)
    dg_lanes = jnp.sum(dn * xh, axis=0, keepdims=True)
    fold = ((lax.broadcasted_iota(jnp.int32, (width, LANES), 0) & (AH - 1))
            == lax.broadcasted_iota(jnp.int32, (width, LANES), 1)).astype(f32)
    dg = jnp.dot(jnp.broadcast_to(dg_lanes, (8, width)), fold, precision=lax.Precision.HIGHEST, preferred_element_type=f32)
    return dx, dg[0:1, :AH]


def _ssd_prep(dt_pre, dt_bias, a_log, dt_s, acum_s, acumT_s):
    dt = _softplus(dt_pre + dt_bias)
    a = dt * (-jnp.exp(a_log))
    row = lax.broadcasted_iota(jnp.int32, (CHUNK, CHUNK), 0)
    col = lax.broadcasted_iota(jnp.int32, (CHUNK, CHUNK), 1)
    dt_s[...] = dt
    acum_s[...] = jnp.dot((col <= row).astype(f32), a, precision=lax.Precision.HIGHEST, preferred_element_type=f32)
    acumT_s[...] = lax.dot_general(a, (row <= col).astype(f32), (((0,), (0,)), ((), ())),
                                   precision=lax.Precision.HIGHEST, preferred_element_type=f32)


def _head_cols(h, dt_s, acum_s, acumT_s):
    lane = lax.broadcasted_iota(jnp.int32, (1, LANES), 1)
    oh_l = (lane == h).astype(f32)
    sub = lax.broadcasted_iota(jnp.int32, (LANES, 1), 0)
    oh_s = (sub == h).astype(f32)
    dt_h = jnp.sum(dt_s[...] * oh_l, axis=1, keepdims=True)
    ac_h = jnp.sum(acum_s[...] * oh_l, axis=1, keepdims=True)
    acr_h = jnp.sum(acumT_s[...] * oh_s, axis=0, keepdims=True)
    return oh_l, dt_h, ac_h, acr_h


def ssd_fwd(xs, Bm, Cm, dt_pre, dt_bias, a_log, d_skip, comm=None):
    def body(xs_ref, b_ref, c_ref, dtp_ref, bias_ref, alog_ref, d_ref, y_ref, st_ref, state, dt_s, acum_s, acumT_s):
        c, g = pl.program_id(0), pl.program_id(1)

        @pl.when(g == 0)
        def _():
            _ssd_prep(dtp_ref[...], bias_ref[...], alog_ref[...], dt_s, acum_s, acumT_s)

        row = lax.broadcasted_iota(jnp.int32, (CHUNK, CHUNK), 0)
        col = lax.broadcasted_iota(jnp.int32, (CHUNK, CHUNK), 1)
        causal = col <= row
        Bb, Cb = b_ref[...], c_ref[...]
        cb = lax.dot_general(Cb.astype(MXU), Bb.astype(MXU), (((1,), (1,)), ((), ())), preferred_element_type=f32)
        xs_blk = xs_ref[...]

        @pl.when(c == 0)
        def _():
            for j in range(HPG):
                state[g * HPG + j] = jnp.zeros((NSTATE, HP), f32)

        prevs = [state[g * HPG + j] for j in range(HPG)]
        y_off_all = jnp.dot(Cb.astype(MXU), jnp.concatenate(prevs, axis=1).astype(MXU), preferred_element_type=f32)
        ys, xds, e_ends = [], [], []
        for j in range(HPG):
            oh_l, dt_h, ac_h, acr_h = _head_cols(g * HPG + j, dt_s, acum_s, acumT_s)
            decay = jnp.exp(jnp.where(causal, ac_h - acr_h, -1e30))
            w = (cb * decay).astype(MXU)
            xs_h = xs_blk[:, HP * j:HP * (j + 1)]
            xd = xs_h * dt_h
            y_diag = jnp.dot(w, xd.astype(MXU), preferred_element_type=f32)
            y_off = y_off_all[:, HP * j:HP * (j + 1)] * jnp.exp(ac_h)
            d_h = jnp.sum(d_ref[...] * oh_l, axis=1, keepdims=True)
            ys.append(y_diag + y_off + xs_h * d_h)
            a_end = ac_h[CHUNK - 1:CHUNK, :]
            xds.append(xd * jnp.exp(a_end - ac_h))
            e_ends.append(jnp.exp(a_end))
        s_c = lax.dot_general(Bb.astype(MXU), jnp.concatenate(xds, axis=1).astype(MXU), (((0,), (0,)), ((), ())),
                              preferred_element_type=f32)
        for j in range(HPG):
            st_ref[0, j] = prevs[j]
            state[g * HPG + j] = prevs[j] * e_ends[j] + s_c[:, HP * j:HP * (j + 1)]
        y_ref[...] = jnp.concatenate(ys, axis=1)

    par = pl.BlockSpec((1, LANES), lambda c, g: (0, 0))
    return _call(
        body, [xs, Bm, Cm, dt_pre, dt_bias, a_log, d_skip], comm=comm, name="ssd_fwd", grid=(NCHUNK, GROUPS),
        in_specs=[pl.BlockSpec((CHUNK, HPG * HP), lambda c, g: (c, g)),
                  pl.BlockSpec((CHUNK, NSTATE), lambda c, g: (c, g)),
                  pl.BlockSpec((CHUNK, NSTATE), lambda c, g: (c, g)),
                  pl.BlockSpec((CHUNK, LANES), lambda c, g: (c, 0)), par, par, par],
        out_specs=[pl.BlockSpec((CHUNK, HPG * HP), lambda c, g: (c, g)),
                   pl.BlockSpec((1, HPG, NSTATE, HP), lambda c, g: (c, g, 0, 0))],
        out_shape=[jax.ShapeDtypeStruct((S, INNER), f32), jax.ShapeDtypeStruct((NCHUNK, HEADS, NSTATE, HP), f32)],
        scratch=[pltpu.VMEM((HEADS, NSTATE, HP), f32), pltpu.VMEM((CHUNK, LANES), f32),
                 pltpu.VMEM((CHUNK, LANES), f32), pltpu.VMEM((LANES, CHUNK), f32)],
        sem=("arbitrary", "arbitrary"))


def ssd_bwd(xs, Bm, Cm, dt_pre, dt_bias, a_log, d_skip, states, dy, comm=None):
    rev = lambda c: NCHUNK - 1 - c

    def body(xs_ref, b_ref, c_ref, dtp_ref, bias_ref, alog_ref, d_ref, st_ref, dy_ref,
             dxs_ref, db_ref, dc_ref, ddt_ref, dbias_ref, dalog_ref, dd_ref,
             dstate, dt_s, acum_s, acumT_s, dacum_s, ddt_s, da_s):
        c, g = pl.program_id(0), pl.program_id(1)

        @pl.when(g == 0)
        def _():
            _ssd_prep(dtp_ref[...], bias_ref[...], alog_ref[...], dt_s, acum_s, acumT_s)
            dacum_s[...] = jnp.zeros_like(dacum_s)
            ddt_s[...] = jnp.zeros_like(ddt_s)

        @pl.when((c == 0) & (g == 0))
        def _():
            da_s[...] = jnp.zeros_like(da_s)
            dd_ref[...] = jnp.zeros_like(dd_ref)
            dbias_ref[...] = jnp.zeros_like(dbias_ref)
            dalog_ref[...] = jnp.zeros_like(dalog_ref)

        row = lax.broadcasted_iota(jnp.int32, (CHUNK, CHUNK), 0)
        col = lax.broadcasted_iota(jnp.int32, (CHUNK, CHUNK), 1)
        sub_l = lax.broadcasted_iota(jnp.int32, (CHUNK, 1), 0)
        last = (sub_l == CHUNK - 1).astype(f32)
        nt = (((1,), (1,)), ((), ()))
        tn = (((0,), (0,)), ((), ()))
        Bb, Cb = b_ref[...], c_ref[...]
        Bm_, Cm_ = Bb.astype(MXU), Cb.astype(MXU)
        cb = lax.dot_general(Cm_, Bm_, nt, preferred_element_type=f32)
        bc = lax.dot_general(Bm_, Cm_, nt, preferred_element_type=f32)
        xs_blk, dy_blk = xs_ref[...], dy_ref[...]
        dxs, dB, dC = [], jnp.zeros((CHUNK, NSTATE), f32), jnp.zeros((CHUNK, NSTATE), f32)
        for j in range(HPG):
            h = g * HPG + j
            oh_l, dt_h, ac_h, acr_h = _head_cols(h, dt_s, acum_s, acumT_s)

            @pl.when(c == 0)
            def _():
                dstate[h] = jnp.zeros((NSTATE, HP), f32)

            dnext = dstate[h]
            prev = st_ref[0, j]
            lm = jnp.exp(jnp.where(col <= row, ac_h - acr_h, -1e30))
            lmT = jnp.exp(jnp.where(row <= col, acr_h - ac_h, -1e30))
            xs_h = xs_blk[:, HP * j:HP * (j + 1)]
            dy_h = dy_blk[:, HP * j:HP * (j + 1)]
            xd = xs_h * dt_h
            xdm, dym = xd.astype(MXU), dy_h.astype(MXU)
            ea = jnp.exp(ac_h)
            a_end = ac_h[CHUNK - 1:CHUNK, :]
            e_end = jnp.exp(a_end)
            dte = jnp.exp(a_end - ac_h)
            dnm, pvm = dnext.astype(MXU), prev.astype(MXU)
            bd = jnp.dot(Bm_, dnm, preferred_element_type=f32)
            dxd = jnp.dot((bc * lmT).astype(MXU), dym, preferred_element_type=f32) + dte * bd
            dw = lax.dot_general(dym, xdm, nt, preferred_element_type=f32)
            dwT = lax.dot_general(xdm, dym, nt, preferred_element_type=f32)
            dcb = dw * lm
            dbc = dwT * lmT
            eady = (ea * dy_h).astype(MXU)
            dC = dC + jnp.dot(dcb.astype(MXU), Bm_, preferred_element_type=f32) \
                + lax.dot_general(eady, pvm, nt, preferred_element_type=f32)
            dB = dB + jnp.dot(dbc.astype(MXU), Cm_, preferred_element_type=f32) \
                + dte * lax.dot_general(xdm, dnm, nt, preferred_element_type=f32)
            dstate[h] = lax.dot_general(Cm_, eady, tn, preferred_element_type=f32) + e_end * dnext
            r1 = jnp.sum(dcb * cb, axis=1, keepdims=True)
            r2 = jnp.sum(dbc * bc, axis=1, keepdims=True)
            y_off = jnp.dot(Cm_, pvm, preferred_element_type=f32) * ea
            t3 = jnp.sum(dy_h * y_off, axis=1, keepdims=True)
            t4 = jnp.sum(bd * xd, axis=1, keepdims=True) * dte
            end_extra = jnp.sum(t4, axis=0, keepdims=True) + e_end * jnp.sum(jnp.sum(prev * dnext, axis=1, keepdims=True), axis=0, keepdims=True)
            dacum_h = r1 - r2 + t3 - t4 + last * end_extra
            dacum_s[...] += dacum_h * oh_l
            ddt_s[...] += jnp.sum(dxd * xs_h, axis=1, keepdims=True) * oh_l
            d_h = jnp.sum(d_ref[...] * oh_l, axis=1, keepdims=True)
            dxs.append(dxd * dt_h + dy_h * d_h)
            dd_ref[...] += oh_l * jnp.sum(jnp.sum(dy_h * xs_h, axis=1, keepdims=True), axis=0, keepdims=True)
        dxs_ref[...] = jnp.concatenate(dxs, axis=1)
        db_ref[...] = dB
        dc_ref[...] = dC

        @pl.when(g == GROUPS - 1)
        def _():
            a_row = -jnp.exp(alog_ref[...])
            da = jnp.dot((row <= col).astype(f32), dacum_s[...], precision=lax.Precision.HIGHEST, preferred_element_type=f32)
            da_s[...] += jnp.sum(da * dt_s[...], axis=0, keepdims=True)
            z = dtp_ref[...] + bias_ref[...]
            ddt_pre = (ddt_s[...] + da * a_row) * _sigmoid(z)
            ddt_ref[...] = ddt_pre.astype(ddt_ref.dtype)
            dbias_ref[...] += jnp.sum(ddt_pre, axis=0, keepdims=True)

            @pl.when(c == NCHUNK - 1)
            def _():
                dalog_ref[...] = da_s[...] * a_row

    par = pl.BlockSpec((1, LANES), lambda c, g: (0, 0))
    return _call(
        body, [xs, Bm, Cm, dt_pre, dt_bias, a_log, d_skip, states, dy], comm=comm, name="ssd_bwd", grid=(NCHUNK, GROUPS),
        in_specs=[pl.BlockSpec((CHUNK, HPG * HP), lambda c, g: (rev(c), g)),
                  pl.BlockSpec((CHUNK, NSTATE), lambda c, g: (rev(c), g)),
                  pl.BlockSpec((CHUNK, NSTATE), lambda c, g: (rev(c), g)),
                  pl.BlockSpec((CHUNK, LANES), lambda c, g: (rev(c), 0)), par, par, par,
                  pl.BlockSpec((1, HPG, NSTATE, HP), lambda c, g: (rev(c), g, 0, 0)),
                  pl.BlockSpec((CHUNK, HPG * HP), lambda c, g: (rev(c), g))],
        out_specs=[pl.BlockSpec((CHUNK, HPG * HP), lambda c, g: (rev(c), g)),
                   pl.BlockSpec((CHUNK, NSTATE), lambda c, g: (rev(c), g)),
                   pl.BlockSpec((CHUNK, NSTATE), lambda c, g: (rev(c), g)),
                   pl.BlockSpec((CHUNK, LANES), lambda c, g: (rev(c), 0)), par, par, par],
        out_shape=[jax.ShapeDtypeStruct((S, INNER), f32), jax.ShapeDtypeStruct((S, GROUPS * NSTATE), f32),
                   jax.ShapeDtypeStruct((S, GROUPS * NSTATE), f32), jax.ShapeDtypeStruct((S, LANES), MXU),
                   jax.ShapeDtypeStruct((1, LANES), f32), jax.ShapeDtypeStruct((1, LANES), f32),
                   jax.ShapeDtypeStruct((1, LANES), f32)],
        scratch=[pltpu.VMEM((HEADS, NSTATE, HP), f32), pltpu.VMEM((CHUNK, LANES), f32),
                 pltpu.VMEM((CHUNK, LANES), f32), pltpu.VMEM((LANES, CHUNK), f32),
                 pltpu.VMEM((CHUNK, LANES), f32), pltpu.VMEM((CHUNK, LANES), f32), pltpu.VMEM((1, LANES), f32)],
        sem=("arbitrary", "arbitrary"))


ATT_STACK_FWD, ATT_STACK_BWD = 4, 2


def _attn_kv(kp, kc, vp, vc, hk):
    sl = slice(AH * hk, AH * (hk + 1))
    return (jnp.concatenate([kp[:, sl], kc[:, sl]], axis=0).astype(MXU),
            jnp.concatenate([vp[:, sl], vc[:, sl]], axis=0).astype(MXU))


def _stack_heads(x, heads):
    return jnp.concatenate([x[:, AH * h:AH * (h + 1)] for h in heads], axis=0)


def _attn_block(n, q, kb, sinks, heads):
    rows = len(heads) * WIN
    qi = lax.broadcasted_iota(jnp.int32, (rows, 2 * WIN), 0) & (WIN - 1)
    ki = lax.broadcasted_iota(jnp.int32, (rows, 2 * WIN), 1)
    rel = qi + WIN - ki
    mask = (rel >= 0) & (rel < WIN) & ((ki >= WIN) | (n > 0))
    qg = _stack_heads(q, heads).astype(MXU)
    s = lax.dot_general(qg, kb, (((1,), (1,)), ((), ())), preferred_element_type=f32) * (AH ** -0.5)
    s = jnp.where(mask, s, -1e30)
    sink = jnp.concatenate([jnp.broadcast_to(sinks[:, h:h + 1], (WIN, 1)) for h in heads], axis=0)
    m = jnp.maximum(jnp.max(s, axis=1, keepdims=True), sink)
    p = jnp.exp(s - m)
    ps = jnp.exp(sink - m)
    inv = 1.0 / (jnp.sum(p, axis=1, keepdims=True) + ps)
    return qg, p * inv, ps * inv


def _head_blocks(hk, stack):
    return [list(range(QPK * hk + i, QPK * hk + i + stack)) for i in range(0, QPK, stack)]


def _kv_specs():
    prev = lambda n: (jnp.maximum(n - 1, 0), 0)
    cur = lambda n: (n, 0)
    w = KVH * AH
    return [pl.BlockSpec((WIN, w), prev), pl.BlockSpec((WIN, w), cur), pl.BlockSpec((WIN, w), prev), pl.BlockSpec((WIN, w), cur)]


def attn_fwd(q, k, v, sinks, comm=None):
    def body(q_ref, kp_ref, kc_ref, vp_ref, vc_ref, s_ref, o_ref):
        n = pl.program_id(0)
        q_, kp, kc, vp, vc, sk = q_ref[...], kp_ref[...], kc_ref[...], vp_ref[...], vc_ref[...], s_ref[...]
        outs = []
        for hk in range(KVH):
            kb, vb = _attn_kv(kp, kc, vp, vc, hk)
            for heads in _head_blocks(hk, ATT_STACK_FWD):
                _, pr, _ = _attn_block(n, q_, kb, sk, heads)
                o = jnp.dot(pr.astype(MXU), vb, preferred_element_type=f32)
                outs += [o[WIN * i:WIN * (i + 1)] for i in range(len(heads))]
        o_ref[...] = jnp.concatenate(outs, axis=1)

    return _call(
        body, [q, k, k, v, v, sinks], comm=comm, name="attn_fwd", grid=(NBLK,),
        in_specs=[pl.BlockSpec((WIN, D), lambda n: (n, 0))] + _kv_specs() + [pl.BlockSpec((1, QH), lambda n: (0, 0))],
        out_specs=[pl.BlockSpec((WIN, D), lambda n: (n, 0))],
        out_shape=[jax.ShapeDtypeStruct((S, D), f32)], sem=("parallel",))[0]


def attn_bwd(q, k, v, sinks, dout, comm=None):
    def body(q_ref, kp_ref, kc_ref, vp_ref, vc_ref, s_ref, do_ref, dq_ref, dkp_ref, dkc_ref, dvp_ref, dvc_ref, ds_ref):
        n = pl.program_id(0)

        @pl.when(n == 0)
        def _():
            ds_ref[...] = jnp.zeros_like(ds_ref)

        q_, kp, kc, vp, vc, sk, do = q_ref[...], kp_ref[...], kc_ref[...], vp_ref[...], vc_ref[...], s_ref[...], do_ref[...]
        lane = lax.broadcasted_iota(jnp.int32, (1, QH), 1)
        nt = (((1,), (1,)), ((), ()))
        tn = (((0,), (0,)), ((), ()))
        dqs, dkps, dkcs, dvps, dvcs = [], [], [], [], []
        dsink = jnp.zeros((1, QH), f32)
        for hk in range(KVH):
            kb, vb = _attn_kv(kp, kc, vp, vc, hk)
            dkb, dvb = jnp.zeros((2 * WIN, AH), f32), jnp.zeros((2 * WIN, AH), f32)
            for heads in _head_blocks(hk, ATT_STACK_BWD):
                qg, pr, prs = _attn_block(n, q_, kb, sk, heads)
                dog = _stack_heads(do, heads).astype(MXU)
                dp = lax.dot_general(dog, vb, nt, preferred_element_type=f32)
                dvb = dvb + lax.dot_general(pr.astype(MXU), dog, tn, preferred_element_type=f32)
                delta = jnp.sum(pr * dp, axis=1, keepdims=True)
                ds = (pr * (dp - delta)).astype(MXU)
                dsk = -prs * delta
                for i, h in enumerate(heads):
                    dsink = dsink + jnp.sum(dsk[WIN * i:WIN * (i + 1)], axis=0, keepdims=True) * (lane == h).astype(f32)
                dqg = jnp.dot(ds, kb, preferred_element_type=f32) * (AH ** -0.5)
                dkb = dkb + lax.dot_general(ds, qg, tn, preferred_element_type=f32) * (AH ** -0.5)
                dqs += [dqg[WIN * i:WIN * (i + 1)] for i in range(len(heads))]
            dkps.append(dkb[:WIN])
            dkcs.append(dkb[WIN:])
            dvps.append(dvb[:WIN])
            dvcs.append(dvb[WIN:])
        dq_ref[...] = jnp.concatenate(dqs, axis=1)
        dkp_ref[...] = jnp.concatenate(dkps, axis=1)
        dkc_ref[...] = jnp.concatenate(dkcs, axis=1)
        dvp_ref[...] = jnp.concatenate(dvps, axis=1)
        dvc_ref[...] = jnp.concatenate(dvcs, axis=1)
        ds_ref[...] += dsink

    w = KVH * AH
    blk = lambda width: pl.BlockSpec((WIN, width), lambda n: (n, 0))
    return _call(
        body, [q, k, k, v, v, sinks, dout], comm=comm, name="attn_bwd", grid=(NBLK,),
        in_specs=[blk(D)] + _kv_specs() + [pl.BlockSpec((1, QH), lambda n: (0, 0)), blk(D)],
        out_specs=[blk(D), blk(w), blk(w), blk(w), blk(w), pl.BlockSpec((1, QH), lambda n: (0, 0))],
        out_shape=[jax.ShapeDtypeStruct((S, D), f32)] + [jax.ShapeDtypeStruct((S, w), f32)] * 4 + [jax.ShapeDtypeStruct((1, QH), f32)],
        sem=("arbitrary",))


def kv_bwd(kv, pos, inv_freq, k_norm, dkp, dkc, dvp, dvc):
    w = KVH * AH

    def body(kv_ref, pos_ref, if_ref, g_ref, dkp_ref, dkc_ref, dvp_ref, dvc_ref, o_ref, dg_ref, db_ref):
        n = pl.program_id(0)

        @pl.when(n == 0)
        def _():
            dg_ref[...] = jnp.zeros_like(dg_ref)
            db_ref[...] = jnp.zeros_like(db_ref)

        inside = (n < NBLK - 1).astype(f32)
        dk = dkc_ref[...] + inside * dkp_ref[...]
        dv = dvc_ref[...] + inside * dvp_ref[...]
        cos, sin = _rope_tables(pos_ref[...], if_ref[...])
        dkpre, dg = _headnorm_rope_bwd(kv_ref[...], g_ref[...], cos, sin, dk, KVH)
        dkv = jnp.concatenate([dkpre, dv], axis=1)
        o_ref[...] = dkv.astype(o_ref.dtype)
        dg_ref[...] += dg
        db_ref[...] += jnp.sum(dkv, axis=0, keepdims=True)

    nxt = lambda n: (jnp.minimum(n + 1, NBLK - 1), 0)
    cur = lambda n: (n, 0)
    const = lambda n: (0, 0)
    return pl.pallas_call(
        body, name="kv_bwd", grid=(NBLK,),
        in_specs=[pl.BlockSpec((WIN, w), cur), pl.BlockSpec((WIN, 1), cur), pl.BlockSpec((1, AH // 2), const),
                  pl.BlockSpec((1, AH), const), pl.BlockSpec((WIN, w), nxt), pl.BlockSpec((WIN, w), cur),
                  pl.BlockSpec((WIN, w), nxt), pl.BlockSpec((WIN, w), cur)],
        out_specs=[pl.BlockSpec((WIN, 2 * w), cur), pl.BlockSpec((1, AH), const), pl.BlockSpec((1, 2 * w), const)],
        out_shape=[jax.ShapeDtypeStruct((S, 2 * w), MXU), jax.ShapeDtypeStruct((1, AH), f32), jax.ShapeDtypeStruct((1, 2 * w), f32)],
        compiler_params=_cparams(("arbitrary",)),
    )(kv, pos, inv_freq, k_norm, dkp, dkc, dvp, dvc)


def _adam_math(w, g, m, v):
    m = ADAM_B1 * m + (1.0 - ADAM_B1) * g
    v = ADAM_B2 * v + (1.0 - ADAM_B2) * (g * g)
    m_hat = m / (1.0 - ADAM_B1 ** ADAM_STEP)
    v_hat = v / (1.0 - ADAM_B2 ** ADAM_STEP)
    return -ADAM_LR * (m_hat / (jnp.sqrt(v_hat) + ADAM_EPS) + ADAM_WD * w), m, v


def adamw(name, w, g, m, v):
    R, C = w.shape
    tr = _pick(R, (256, 128, 64, 32, 16, 8))
    tc = C if tr < R or C % 256 else 256

    def body(w_ref, g_ref, m_ref, v_ref, d_ref, nm_ref, nv_ref):
        d_ref[...], nm_ref[...], nv_ref[...] = _adam_math(w_ref[...], g_ref[...], m_ref[...], v_ref[...])

    spec = pl.BlockSpec((tr, tc), lambda i, j: (i, j))
    return pl.pallas_call(
        body, name=name, grid=(R // tr, C // tc), in_specs=[spec] * 4, out_specs=[spec] * 3,
        out_shape=[jax.ShapeDtypeStruct((R, C), f32)] * 3, compiler_params=_cparams(("parallel", "parallel")),
    )(w, g, m, v)


def _me():
    return lax.axis_index("x"), lax.axis_index("y"), lax.axis_index("c")


def gather_comm(xs):
    n = len(xs)

    def parts(x_refs, o_refs, sems):
        send_sems, recv_sems, local_sems = sems
        x, y, c = _me()
        me, sibling = (x, y, c), (x, y, 1 - c)
        chips = [(1 - x, y), (x, 1 - y), (1 - x, 1 - y)]

        def copy(a, k, block, to, src=None):
            dst = o_refs[a].at[4 * block[0] + 2 * block[1] + block[2]]
            return pltpu.make_async_remote_copy(
                src_ref=dst if src is None else src, dst_ref=dst,
                send_sem=send_sems.at[7 * a + k], recv_sem=recv_sems.at[7 * a + k], device_id=to, device_id_type=MESH)

        mine = [pltpu.make_async_copy(x_refs[a], o_refs[a].at[4 * x + 2 * y + c], local_sems.at[a]) for a in range(n)]
        first = []
        for a in range(n):
            first.append(copy(a, 0, me, sibling, src=x_refs[a]))
            first += [copy(a, 1 + j, me, (*chip, c), src=x_refs[a]) for j, chip in enumerate(chips)]
        return copy, mine, first, me, sibling, chips, c

    def start(x_refs, o_refs, sems):
        _, mine, first, *_ = parts(x_refs, o_refs, sems)
        for cp in mine + first:
            cp.start()

    def finish(x_refs, o_refs, sems):
        copy, mine, first, me, sibling, chips, c = parts(x_refs, o_refs, sems)
        passed = []
        for j, chip in enumerate(chips):
            for a in range(n):
                copy(a, 1 + j, (*chip, c), me).wait_recv()
                cp = copy(a, 4 + j, (*chip, c), sibling)
                cp.start()
                passed.append(cp)
        for a in range(n):
            copy(a, 0, sibling, me).wait_recv()
            for j, chip in enumerate(chips):
                copy(a, 4 + j, (*chip, 1 - c), me).wait_recv()
        for cp in first + passed:
            cp.wait_send()
        for cp in mine:
            cp.wait()

    return Comm(xs, [jax.ShapeDtypeStruct((N_DEV,) + a.shape, a.dtype) for a in xs],
                [pltpu.SemaphoreType.DMA((7 * n,)), pltpu.SemaphoreType.DMA((7 * n,)), pltpu.SemaphoreType.DMA((n,))], start, finish)


def run_comm(name, comm):
    _call(lambda: None, [], name=name, grid=(1,), in_specs=[], out_specs=[], out_shape=[], comm=comm)
    return comm.results


def sibling_comm(gs):
    n = len(gs)

    def copies(g_refs, o_refs, sems):
        x, y, c = _me()
        return [pltpu.make_async_remote_copy(
            src_ref=g_refs[a].at[:, 1 - c], dst_ref=o_refs[a], send_sem=sems[0].at[a], recv_sem=sems[1].at[a],
            device_id=(x, y, 1 - c), device_id_type=MESH) for a in range(n)]

    def start(g_refs, o_refs, sems):
        for cp in copies(g_refs, o_refs, sems):
            cp.start()

    def finish(g_refs, o_refs, sems):
        for cp in copies(g_refs, o_refs, sems):
            cp.wait()

    return Comm(gs, [jax.ShapeDtypeStruct((4,) + g.shape[2:], g.dtype) for g in gs],
                [pltpu.SemaphoreType.DMA((n,)), pltpu.SemaphoreType.DMA((n,))], start, finish)


def chip_comm(ts):
    n = len(ts)

    def copies(t_refs, o_refs, sems):
        x, y, c = _me()
        chips = [(1 - x, y), (x, 1 - y), (1 - x, 1 - y)]
        return [pltpu.make_async_remote_copy(
            src_ref=t_refs[a].at[2 * px + py], dst_ref=o_refs[a].at[j],
            send_sem=sems[0].at[3 * a + j], recv_sem=sems[1].at[3 * a + j],
            device_id=(px, py, c), device_id_type=MESH) for j, (px, py) in enumerate(chips) for a in range(n)]

    def start(t_refs, o_refs, sems):
        for cp in copies(t_refs, o_refs, sems):
            cp.start()

    def finish(t_refs, o_refs, sems):
        for cp in copies(t_refs, o_refs, sems):
            cp.wait()

    return Comm(ts, [jax.ShapeDtypeStruct((3,) + t.shape[1:], t.dtype) for t in ts],
                [pltpu.SemaphoreType.DMA((3 * n,)), pltpu.SemaphoreType.DMA((3 * n,))], start, finish)


def _row_tile(rows):
    return _pick(rows, (512, 304, 256, 128))


def pair_add(name, g, r):
    _, _, R, C = g.shape
    tr = _row_tile(R)

    def body(c_ref, g_ref, r_ref, o_ref):
        o_ref[0] = (g_ref[0, 0].astype(f32) + r_ref[0].astype(f32)).astype(o_ref.dtype)

    return pl.pallas_call(
        body, name=name,
        grid_spec=pltpu.PrefetchScalarGridSpec(
            num_scalar_prefetch=1, grid=(4, R // tr),
            in_specs=[pl.BlockSpec((1, 1, tr, C), lambda p, i, c: (p, c[0], i, 0)),
                      pl.BlockSpec((1, tr, C), lambda p, i, c: (p, i, 0))],
            out_specs=pl.BlockSpec((1, tr, C), lambda p, i, c: (p, i, 0))),
        out_shape=jax.ShapeDtypeStruct((4, R, C), g.dtype),
        compiler_params=_cparams(("parallel", "parallel")),
    )(lax.axis_index("c").reshape(1).astype(jnp.int32), g, r)


def _sum_of_four(t_ref, r_ref):
    return ((t_ref[0].astype(f32) + r_ref[0].astype(f32)) + r_ref[1].astype(f32)) + r_ref[2].astype(f32)


def _my_chip():
    return (2 * lax.axis_index("x") + lax.axis_index("y")).reshape(1).astype(jnp.int32)


def final_adamw(name, t, r, w, m, v):
    _, R, C = t.shape
    tr = _pick(R, (256, 128, 64, 32, 16))
    tc = C if tr < R or C % 256 else 256

    def body(p_ref, t_ref, r_ref, w_ref, m_ref, v_ref, g_ref, d_ref, nm_ref, nv_ref):
        g_ = _sum_of_four(t_ref, r_ref)
        g_ref[...] = g_
        d_ref[...], nm_ref[...], nv_ref[...] = _adam_math(w_ref[...], g_, m_ref[...], v_ref[...])

    flat = pl.BlockSpec((tr, tc), lambda i, j, p: (i, j))
    return pl.pallas_call(
        body, name=name,
        grid_spec=pltpu.PrefetchScalarGridSpec(
            num_scalar_prefetch=1, grid=(R // tr, C // tc),
            in_specs=[pl.BlockSpec((1, tr, tc), lambda i, j, p: (p[0], i, j)),
                      pl.BlockSpec((3, tr, tc), lambda i, j, p: (0, i, j)), flat, flat, flat],
            out_specs=[flat] * 4),
        out_shape=[jax.ShapeDtypeStruct((R, C), f32)] * 4,
        compiler_params=_cparams(("parallel", "parallel")),
    )(_my_chip(), t, r, w, m, v)


def final_adamw_layers(name, t0, r0, t1, r1, w, m, v):
    _, R, C = t0.shape
    tr = _pick(R, (256, 128, 64, 32, 16))

    def body(p_ref, t0_ref, r0_ref, t1_ref, r1_ref, w_ref, m_ref, v_ref, g_ref, d_ref, nm_ref, nv_ref):
        g_ = jnp.where(pl.program_id(0) == 0, _sum_of_four(t0_ref, r0_ref), _sum_of_four(t1_ref, r1_ref))
        g_ref[0] = g_
        d_ref[0], nm_ref[0], nv_ref[0] = _adam_math(w_ref[0], g_, m_ref[0], v_ref[0])

    mine = pl.BlockSpec((1, tr, C), lambda l, i, p: (p[0], i, 0))
    theirs = pl.BlockSpec((3, tr, C), lambda l, i, p: (0, i, 0))
    layer = pl.BlockSpec((1, tr, C), lambda l, i, p: (l, i, 0))
    return pl.pallas_call(
        body, name=name,
        grid_spec=pltpu.PrefetchScalarGridSpec(
            num_scalar_prefetch=1, grid=(2, R // tr),
            in_specs=[mine, theirs, mine, theirs, layer, layer, layer], out_specs=[layer] * 4),
        out_shape=[jax.ShapeDtypeStruct((2, R, C), f32)] * 4,
        compiler_params=_cparams(("parallel", "parallel")),
    )(_my_chip(), t0, r0, t1, r1, w, m, v)


class ReduceScatter:
    def __init__(self, tag, keys, grads):
        self.tag, self.keys, self.grads = tag, keys, grads
        self.send = [g.reshape((4, 2, g.shape[0] // N_DEV) + g.shape[1:]) for g in grads]

    def sibling(self):
        self.c1 = sibling_comm(self.send)
        return self.c1

    def chips(self):
        self.pairs = [pair_add(f"rs_pair_add_{self.tag}{i}", g, r) for i, (g, r) in enumerate(zip(self.send, self.c1.results))]
        self.c2 = chip_comm(self.pairs)
        return self.c2

    def parts(self):
        return {k: (t, r) for k, t, r in zip(self.keys, self.pairs, self.c2.results)}


IN_ROWS = {"z": (0, 2048), "xs": (2048, 4096), "B": (4096, 5120), "C": (5120, 6144)}
IN_COLS = 2 * INNER + 2 * GROUPS * NSTATE + HEADS


def sum_devices(g):
    def body(g_ref, o_ref):
        acc = g_ref[0]
        for i in range(1, N_DEV):
            acc = acc + g_ref[i]
        o_ref[...] = acc

    return pl.pallas_call(body, name="sum_devices", out_shape=jax.ShapeDtypeStruct(g.shape[1:], f32),
                          compiler_params=_cparams())(g)


def _pack(parts, unit, dtype, lead=()):
    flat = jnp.concatenate([p.reshape(lead + (-1,)).astype(dtype) for p in parts], axis=-1)
    n = flat.shape[-1]
    rows = -(-n // (unit * PACK_W)) * unit
    flat = jnp.pad(flat, [(0, 0)] * len(lead) + [(0, rows * PACK_W - n)])
    return flat.reshape(lead + (rows, PACK_W))


def _unpack(buf, shapes, lead=()):
    flat = buf.reshape(lead + (-1,))
    out, off = [], 0
    for shp in shapes:
        n = math.prod(shp)
        out.append(flat[..., off:off + n].reshape(lead + tuple(shp)))
        off += n
    return out


def _pad_lanes(a):
    return jnp.pad(a, [(0, 0)] * (a.ndim - 1) + [(0, LANES - a.shape[-1])])


_NN = (((1,), (0,)), ((), ()))
_NT = (((1,), (1,)), ((), ()))


def _mm(a, b, dims):
    return lax.dot_general(a.astype(MXU), b.astype(MXU), dims, preferred_element_type=f32)


def _ffn_fwd(tag, x, h, w_inT, conv_w, conv_b, mid_comm=None):
    ct, nblk = FFN_CT, FFN // FFN_CT

    def body(h_ref, wg_ref, wv_ref, cw_ref, cb_ref, gp_ref, v_ref, a_ref):
        h_ = h_ref[...]
        gp, v_ = _mm(h_, wg_ref[...], _NT), _mm(h_, wv_ref[...], _NT)
        gp_ref[...] = gp
        v_ref[...] = v_
        a_ref[...] = (_silu(_conv(gp, cw_ref[...], cb_ref[...])) * v_).astype(a_ref.dtype)

    col = pl.BlockSpec((S, ct), lambda j: (0, j))
    gate_pre, val, act = _call(
        body, [h, w_inT, w_inT, conv_w, conv_b], comm=mid_comm, name=f"{tag}_in", grid=(nblk,),
        in_specs=[pl.BlockSpec((S, D), lambda j: (0, 0)), pl.BlockSpec((ct, D), lambda j: (j, 0)),
                  pl.BlockSpec((ct, D), lambda j: (nblk + j, 0)), pl.BlockSpec((CONV_F, ct), lambda j: (0, j)),
                  pl.BlockSpec((1, ct), lambda j: (0, j))],
        out_specs=[col, col, col],
        out_shape=[jax.ShapeDtypeStruct((S, FFN), f32), jax.ShapeDtypeStruct((S, FFN), f32), jax.ShapeDtypeStruct((S, FFN), MXU)],
        sem=("parallel",))
    return act, (x, h, gate_pre, val, act)


FFN_CT = 256
CONV_CT = 256


def _proj_conv(name, h, wT, row0, cw, cb, comm=None):
    C, ct = cw.shape[1], CONV_CT

    def body(h_ref, w_ref, cw_ref, cb_ref, p_ref, c_ref):
        p = _mm(h_ref[...], w_ref[...], _NT)
        p_ref[...] = p
        c_ref[...] = _silu(_conv(p, cw_ref[...], cb_ref[...]))

    col = pl.BlockSpec((S, ct), lambda j: (0, j))
    return _call(
        body, [h, wT, cw, cb], comm=comm, name=name, grid=(C // ct,),
        in_specs=[pl.BlockSpec((S, D), lambda j: (0, 0)), pl.BlockSpec((ct, D), lambda j: (row0 // ct + j, 0)),
                  pl.BlockSpec((CONV_A, ct), lambda j: (0, j)), pl.BlockSpec((1, ct), lambda j: (0, j))],
        out_specs=[col, col], out_shape=[jax.ShapeDtypeStruct((S, C), f32)] * 2, sem=("parallel",))


def _dconv_wgrad(name, pre, dconv, cw, cb, h):
    C, ct = cw.shape[1], CONV_CT

    def body(p_ref, do_ref, cw_ref, cb_ref, h_ref, dp_ref, g_ref, dw_ref, db_ref):
        p_, w_ = p_ref[...], cw_ref[...]
        taps = _taps(p_, CONV_A)
        dx, dw, db = _conv_bwd(p_, w_, do_ref[...] * _dsilu(_conv(p_, w_, cb_ref[...], taps)), taps)
        dpm = dx.astype(MXU)
        dp_ref[...] = dpm
        g_ref[...] = lax.dot_general(dpm, h_ref[...].astype(MXU), (((0,), (0,)), ((), ())),
                                     preferred_element_type=f32).astype(g_ref.dtype)
        dw_ref[...] = dw
        db_ref[...] = db

    col = pl.BlockSpec((S, ct), lambda j: (0, j))
    return _call(
        body, [pre, dconv, cw, cb, h], name=name, grid=(C // ct,),
        in_specs=[col, col, pl.BlockSpec((CONV_A, ct), lambda j: (0, j)), pl.BlockSpec((1, ct), lambda j: (0, j)),
                  pl.BlockSpec((S, D), lambda j: (0, 0))],
        out_specs=[col, pl.BlockSpec((ct, D), lambda j: (j, 0)), pl.BlockSpec((CONV_A, ct), lambda j: (0, j)),
                   pl.BlockSpec((1, ct), lambda j: (0, j))],
        out_shape=[jax.ShapeDtypeStruct((S, C), MXU), jax.ShapeDtypeStruct((C, D), MXU),
                   jax.ShapeDtypeStruct((CONV_A, C), f32), jax.ShapeDtypeStruct((1, C), f32)],
        sem=("parallel",))


def _ffn_mid_bwd(name, dout, w_down, gate_pre, val, conv_w, conv_b, comm=None):
    ct = FFN_CT

    def body(do_ref, wd_ref, gp_ref, v_ref, w_ref, b_ref, dgv_ref, dw_ref, db_ref, dob_s):
        @pl.when(pl.program_id(0) == 0)
        def _():
            dob_s[...] = do_ref[...].astype(MXU)

        da = _mm(dob_s[...], wd_ref[...], _NT)
        gp, v_, w_ = gp_ref[...], v_ref[...], w_ref[...]
        taps = _taps(gp, CONV_F)
        gate = _conv(gp, w_, b_ref[...], taps)
        sg = _sigmoid(gate)
        dgp, dw, db = _conv_bwd(gp, w_, da * v_ * (sg * (1.0 + gate * (1.0 - sg))), taps)
        dgv_ref[0] = dgp.astype(dgv_ref.dtype)
        dgv_ref[1] = (da * (gate * sg)).astype(dgv_ref.dtype)
        dw_ref[...] = dw
        db_ref[...] = db

    col = pl.BlockSpec((S, ct), lambda j: (0, j))
    return _call(
        body, [dout, w_down, gate_pre, val, conv_w, conv_b], comm=comm, name=name, grid=(FFN // ct,),
        in_specs=[pl.BlockSpec((S, D), lambda j: (0, 0)), pl.BlockSpec((ct, D), lambda j: (j, 0)), col, col,
                  pl.BlockSpec((CONV_F, ct), lambda j: (0, j)), pl.BlockSpec((1, ct), lambda j: (0, j))],
        out_specs=[pl.BlockSpec((2, S, ct), lambda j: (0, 0, j)), pl.BlockSpec((CONV_F, ct), lambda j: (0, j)),
                   pl.BlockSpec((1, ct), lambda j: (0, j))],
        out_shape=[jax.ShapeDtypeStruct((2, S, FFN), MXU), jax.ShapeDtypeStruct((CONV_F, FFN), f32), jax.ShapeDtypeStruct((1, FFN), f32)],
        scratch=[pltpu.VMEM((S, D), MXU)], sem=("arbitrary",))


def _ffn_bwd(tag, layer, saved, norm_g, w_inT, conv_w, conv_b, w_down, dout, mid_comm=None, before=None):
    x, h, gate_pre, val, act = saved
    g_down = matmul(f"{tag}_wdown", act, dout, "tn", out_dtype=MXU)
    dgv, g_cw, g_cb = _ffn_mid_bwd(f"{tag}_dmid", dout, w_down, gate_pre, val, conv_w, conv_b, comm=mid_comm)
    g_inT = matmul_tn_stacked(f"{tag}_win", dgv, h, MXU)

    def din_fn(dg_, dv_, x_, do_, g_, wT, *wb):
        dx, dg = _rms_bwd(x_, g_, _mm(dg_, wT[:FFN], _NN) + _mm(dv_, wT[FFN:], _NN))
        dx = do_ + dx
        return (dx,) + tuple(_mm(dx, w_, _NT) for w_ in wb) + (dg,)

    rs = ReduceScatter(tag, (f"f_inT{layer}", f"f_down{layer}"), [g_inT, g_down])
    extra = [] if before is None else [before]
    *dxs, g_norm = rowwise(f"{tag}_din", din_fn, [(dgv, 0), (dgv, 1), x, dout], [norm_g, w_inT] + extra,
                           [(D, f32)] + [(w_.shape[0], f32) for w_ in extra], [(1, D)], comm=rs.sibling())
    small = {f"f_norm{layer}": g_norm, f"f_conv_w{layer}": g_cw, f"f_conv_b{layer}": g_cb}
    return (dxs[0] if before is None else tuple(dxs)), small, rs


def _land(W, keys, comm):
    for k, g in zip(keys, comm.results):
        W[k] = g.reshape(-1, g.shape[2])


def _local_step(x, pos, tgt, W, shards):
    G = {}
    gather = lambda *keys: gather_comm([shards[k] for k in keys])
    inv_freq = (ROPE_THETA ** (-jnp.arange(AH // 2, dtype=f32) / (AH // 2))).reshape(1, AH // 2)

    def in_fn(x_, g_, wT, wdtT):
        h_ = _rms_fwd(x_, g_).astype(MXU)
        return h_, _mm(h_, wT[slice(*IN_ROWS["z"])], _NT), _mm(h_, wdtT, _NT)

    h0, z, dt_pre = rowwise("a_in", in_fn, [x], [W["a_norm"], W["inT"], W["in_dtT"]], [(D, MXU), (INNER, f32), (LANES, f32)])
    pre, conv = {}, {}
    early = {"xs": ("a_out",), "B": (), "C": ()}
    for k in ("xs", "B", "C"):
        c = gather(*early[k]) if early[k] else None
        pre[k], conv[k] = _proj_conv(f"a_in_{k}", h0, W["inT"], IN_ROWS[k][0], W[f"cw_{k}"], W[f"cb_{k}"], comm=c)
        if c is not None:
            _land(W, early[k], c)
    c = gather("f_inT0", "f_down0")
    y, states = ssd_fwd(conv["xs"], conv["B"], conv["C"], dt_pre, W["dt_bias"], W["A_log"], W["D"], comm=c)
    _land(W, ("f_inT0", "f_down0"), c)

    def gate_norm(y_, z_, g_):
        yg = y_ * _silu(z_)
        w = INNER // GROUPS
        return (jnp.concatenate([_rms_fwd(yg[:, w * i:w * (i + 1)], g_[:, w * i:w * (i + 1)]) for i in range(GROUPS)], axis=1),)

    def out_fn(y_, z_, x_, g_, w_, gf_):
        (gn_,) = gate_norm(y_, z_, g_)
        gn_ = gn_.astype(MXU)
        x1_ = x_ + _mm(gn_, w_, _NN)
        return gn_, x1_, _rms_fwd(x1_, gf_)

    c = gather("f_down1")
    gn, x1, h1 = rowwise("a_out", out_fn, [y, z, x], [W["a_gnorm"], W["a_out"], W["f_norm0"]],
                         [(INNER, MXU), (D, f32), (D, MXU)], comm=c)
    _land(W, ("f_down1",), c)

    c = gather("w_kv", "w_q", "w_o")
    act0, ffn0 = _ffn_fwd("f0", x1, h1, W["f_inT0"], W["f_cw0"], W["f_cb0"], mid_comm=c)
    _land(W, ("w_kv", "w_q", "w_o"), c)
    kw = KVH * AH

    def qkv_fn(a_, x_, pos_, wd, gk, gb, wkv, bkv, wq, bq, if_, kn, qn):
        x2_ = x_ + _mm(a_, wd, _NN)
        kvn_, h2_ = _rms_fwd(x2_, gk).astype(MXU), _rms_fwd(x2_, gb).astype(MXU)
        kv_, qp_ = _mm(kvn_, wkv, _NN) + bkv, _mm(h2_, wq, _NN) + bq
        cos, sin = _rope_tables(pos_, if_)
        return (x2_, kvn_, h2_, kv_, qp_, _headnorm_rope_fwd(kv_[:, :kw], kn, cos, sin, KVH), kv_[:, kw:],
                _headnorm_rope_fwd(qp_, qn, cos, sin, QH))

    x2, kvn, h2, kv, q_pre, k_rot, v_val, q = rowwise(
        "f0_down_qkv", qkv_fn, [act0, x1, pos],
        [W["f_down0"], W["kv_norm"], W["b_norm"], W["w_kv"], W["b_kv"], W["w_q"], W["b_q"], inv_freq, W["k_norm"], W["q_norm"]],
        [(D, f32), (D, MXU), (D, MXU), (2 * kw, f32), (D, f32), (kw, f32), (kw, f32), (D, f32)])
    c = gather("f_inT1")
    att = attn_fwd(q, k_rot, v_val, W["sinks"], comm=c)
    _land(W, ("f_inT1",), c)
    def o_fn(att_, x_, w_, b_, gf_):
        x3_ = x_ + _mm(att_, w_, _NN) + b_
        return x3_, _rms_fwd(x3_, gf_)

    x3, h3 = rowwise("o_proj", o_fn, [att, x2], [W["w_o"], W["b_o"], W["f_norm1"]], [(D, f32), (D, MXU)])

    act1, ffn1 = _ffn_fwd("f1", x3, h3, W["f_inT1"], W["f_cw1"], W["f_cb1"])

    def loss_fn(a_, x_, t_, w_):
        diff = x_ + _mm(a_, w_, _NN) - t_
        rows = jnp.sum(diff * diff, axis=1, keepdims=True) * (0.5 / D)
        return diff * (1.0 / D), jnp.sum(rows, axis=0, keepdims=True)

    dx4, loss = rowwise("f1_down_loss", loss_fn, [act1, x3, tgt], [W["f_down1"]], [(D, f32)], [(1, 1)])

    (dx3, datt), g, rs_f1 = _ffn_bwd("f1", 1, ffn1, W["f_norm1"], W["f_inT1"], W["f_cw1"], W["f_cb1"], W["f_down1"], dx4,
                                     before=W["w_o"])
    G.update(g)

    g_wo = matmul("o_wproj", att, dx3, "tn", out_dtype=MXU)
    dq, dkp, dkc, dvp, dvc, G["sinks"] = attn_bwd(q, k_rot, v_val, W["sinks"], datt, comm=rs_f1.chips())

    def q_bwd(q_, pos_, dq_, dx_, if_, g_):
        cos, sin = _rope_tables(pos_, if_)
        dqp, dg = _headnorm_rope_bwd(q_, g_, cos, sin, dq_, QH)
        return dqp, dg, jnp.sum(dqp, axis=0, keepdims=True), jnp.sum(dx_, axis=0, keepdims=True)

    dq_pre, G["q_norm"], G["b_q"], G["b_o"] = rowwise("q_drope", q_bwd, [q_pre, pos, dq, dx3], [inv_freq, W["q_norm"]],
                                                      [(D, MXU)], [(1, AH), (1, D), (1, D)])
    g_wq = matmul("q_wproj", h2, dq_pre, "tn", out_dtype=MXU)
    dkv, G["k_norm"], G["b_kv"] = kv_bwd(kv, pos, inv_freq, W["k_norm"], dkp, dkc, dvp, dvc)
    g_wkv = matmul("kv_wproj", kvn, dkv, "tn", out_dtype=MXU)
    rs_att = ReduceScatter("att", ("w_kv", "w_q", "w_o"), [g_wkv, g_wq, g_wo])

    def x2_bwd(x_, dq_, dkv_, dx_, gb_, gk_, wq, wkv):
        d1, dgb = _rms_bwd(x_, gb_, _mm(dq_, wq, _NT))
        d2, dgk = _rms_bwd(x_, gk_, _mm(dkv_, wkv, _NT))
        return dx_ + d1 + d2, dgb, dgk

    dx2, G["b_norm"], G["kv_norm"] = rowwise("qkv_dproj", x2_bwd, [x2, dq_pre, dkv, dx3],
                                             [W["b_norm"], W["kv_norm"], W["w_q"], W["w_kv"]],
                                             [(D, f32)], [(1, D), (1, D)], comm=rs_att.sibling())

    dx1, g, rs_f0 = _ffn_bwd("f0", 0, ffn0, W["f_norm0"], W["f_inT0"], W["f_cw0"], W["f_cb0"], W["f_down0"], dx2,
                             mid_comm=rs_att.chips())
    G.update(g)

    rs_out = ReduceScatter("a_out", ("a_out",), [matmul("a_wout", gn, dx1, "tn", out_dtype=MXU)])

    def gate_norm_bwd(y_, z_, dx_, g_, w_out):
        dgn_ = _mm(dx_, w_out, _NT)
        w = INNER // GROUPS
        sg = _sigmoid(z_)
        sz = z_ * sg
        yg = y_ * sz
        parts, dgs = [], []
        for i in range(GROUPS):
            dseg, dg = _rms_bwd(yg[:, w * i:w * (i + 1)], g_[:, w * i:w * (i + 1)], dgn_[:, w * i:w * (i + 1)])
            parts.append(dseg)
            dgs.append(dg)
        dyg = jnp.concatenate(parts, axis=1)
        return dyg * sz, dyg * y_ * (sg * (1.0 + z_ * (1.0 - sg))), jnp.concatenate(dgs, axis=1)

    dy, dz, G["a_gnorm"] = rowwise("a_dout", gate_norm_bwd, [y, z, dx1], [W["a_gnorm"], W["a_out"]],
                                   [(INNER, f32), (INNER, MXU)], [(1, INNER)], comm=rs_out.sibling())
    dconv = {}
    dconv["xs"], dconv["B"], dconv["C"], ddt_pre, G["dt_bias"], G["A_log"], G["D"] = ssd_bwd(
        conv["xs"], conv["B"], conv["C"], dt_pre, W["dt_bias"], W["A_log"], W["D"], states, dy,
        comm=merge_comms([rs_f0.chips(), rs_out.chips()]))

    g_in, dpre = [matmul("a_win_z", dz, h0, "tn", out_dtype=MXU)], {}
    for k in ("xs", "B", "C"):
        dpre[k], g_k, G[f"cw_{k}"], G[f"cb_{k}"] = _dconv_wgrad(f"a_dconv_{k}", pre[k], dconv[k], W[f"cw_{k}"], W[f"cb_{k}"], h0)
        g_in.append(g_k)
    g_in.append(matmul("a_win_dt", ddt_pre, h0, "tn", out_dtype=MXU)[:HEADS])
    rs_in = ReduceScatter("a_in", ("inT",), [jnp.concatenate(g_in, axis=0)])
    run_comm("rs_in_sibling", rs_in.sibling())

    def x0_bwd(dz_, dxs_, db_, dc_, ddt_, x_, do_, g_, wT, wdtT):
        parts = zip((dz_, dxs_, db_, dc_), IN_ROWS.values())
        dh = sum(_mm(d_, wT[a:b], _NN) for d_, (a, b) in parts) + _mm(ddt_, wdtT, _NN)
        dx, dg = _rms_bwd(x_, g_, dh)
        return do_ + dx, dg

    dx, G["a_norm"] = rowwise("a_din", x0_bwd, [dz, dpre["xs"], dpre["B"], dpre["C"], ddt_pre, x, dx1],
                              [W["a_norm"], W["inT"], W["in_dtT"]], [(D, f32)], [(1, D)], comm=rs_in.chips())
    return loss, dx, G, [rs_f1, rs_att, rs_f0, rs_out, rs_in]


ROW_KEYS = ("inT", "a_out", "f_inT0", "f_down0", "w_kv", "w_q", "w_o", "f_inT1", "f_down1")


def _row_blocks(src):
    return {"inT": src["a_in_proj"][0].T, "a_out": src["a_out_proj"][0], "w_kv": src["w_kv"], "w_q": src["w_q"][0],
            "w_o": src["w_o"][0], "f_inT0": src["f_w_in"][0].T, "f_inT1": src["f_w_in"][1].T,
            "f_down0": src["f_w_down"][0], "f_down1": src["f_w_down"][1]}


def _from_row_blocks(rb):
    out = {"a_in_proj": rb["inT"].T[None], "a_out_proj": rb["a_out"][None], "w_kv": rb["w_kv"], "w_q": rb["w_q"][None],
           "w_o": rb["w_o"][None]}
    if "f_inT0" in rb:
        out["f_w_in"] = jnp.stack([rb["f_inT0"].T, rb["f_inT1"].T])
        out["f_w_down"] = jnp.stack([rb["f_down0"], rb["f_down1"]])
    return out


SMALL_SHARDED = ("a_norm", "a_conv_w", "a_conv_b", "a_gnorm", "f_conv_w")
REPLICATED = ("a_dt_bias", "a_A_log", "a_D", "kv_norm", "b_kv", "k_norm", "b_norm", "b_q", "q_norm", "sinks", "b_o",
              "f_norm", "f_conv_b")
ORDER = ("a_norm", "a_in_proj", "a_conv_w", "a_conv_b", "a_dt_bias", "a_A_log", "a_D", "a_gnorm", "a_out_proj", "kv_norm",
         "w_kv", "b_kv", "k_norm", "b_norm", "w_q", "b_q", "q_norm", "sinks", "w_o", "b_o", "f_norm", "f_w_in",
         "f_conv_w", "f_conv_b", "f_w_down")


def _gathered_to_whole(name, g):
    if name == "a_conv_w":
        return jnp.moveaxis(g[:, 0], 0, 1).reshape(g.shape[2], -1)
    if name in ("a_norm", "a_conv_b", "a_gnorm"):
        return g[:, 0].reshape(1, -1)
    if name == "f_conv_w":
        return jnp.moveaxis(g, 0, 2).reshape(g.shape[1], g.shape[2], -1)
    raise ValueError(name)


def _whole_to_shards(name, w):
    if name == "a_conv_w":
        return jnp.moveaxis(w.reshape(w.shape[0], N_DEV, -1), 1, 0)[:, None]
    if name in ("a_norm", "a_conv_b", "a_gnorm"):
        return w.reshape(N_DEV, 1, -1)
    if name == "f_conv_w":
        return jnp.moveaxis(w.reshape(w.shape[0], w.shape[1], N_DEV, -1), 2, 0)
    raise ValueError(name)


def _small_weights(whole):
    W = {}
    cw, cb = whole["a_conv_w"], whole["a_conv_b"]
    o = 0
    for k, n in (("xs", INNER), ("B", GROUPS * NSTATE), ("C", GROUPS * NSTATE)):
        W[f"cw_{k}"], W[f"cb_{k}"] = cw[:, o:o + n], cb[:, o:o + n]
        o += n
    W["a_norm"], W["a_gnorm"] = whole["a_norm"], whole["a_gnorm"]
    W["dt_bias"], W["A_log"], W["D"] = (_pad_lanes(whole[k]) for k in ("a_dt_bias", "a_A_log", "a_D"))
    W["kv_norm"], W["b_kv"], W["k_norm"] = whole["kv_norm"].reshape(1, -1), whole["b_kv"].reshape(1, -1), whole["k_norm"].reshape(1, -1)
    for k in ("b_norm", "b_q", "q_norm", "sinks", "b_o"):
        W[k] = whole[k]
    for i in range(2):
        W[f"f_norm{i}"] = whole["f_norm"][i:i + 1]
        W[f"f_cw{i}"], W[f"f_cb{i}"] = whole["f_conv_w"][i], whole["f_conv_b"][i:i + 1]
    return W


def _small_grads(G, shapes):
    nh = HEADS
    out = {
        "a_conv_w": jnp.concatenate([G["cw_xs"], G["cw_B"], G["cw_C"]], axis=1),
        "a_conv_b": jnp.concatenate([G["cb_xs"], G["cb_B"], G["cb_C"]], axis=1),
        "a_norm": G["a_norm"], "a_gnorm": G["a_gnorm"],
        "a_dt_bias": G["dt_bias"][:, :nh], "a_A_log": G["A_log"][:, :nh], "a_D": G["D"][:, :nh],
        "kv_norm": G["kv_norm"], "b_kv": G["b_kv"], "k_norm": G["k_norm"], "b_norm": G["b_norm"],
        "b_q": G["b_q"], "q_norm": G["q_norm"], "sinks": G["sinks"], "b_o": G["b_o"],
        "f_norm": jnp.concatenate([G["f_norm0"], G["f_norm1"]], axis=0),
        "f_conv_w": jnp.stack([G["f_conv_w0"], G["f_conv_w1"]]),
        "f_conv_b": jnp.concatenate([G["f_conv_b0"], G["f_conv_b1"]], axis=0),
    }
    return {k: val.reshape(shapes[k]) if k in shapes else val for k, val in out.items()}


def kernel(x, positions, a_norm, a_in_proj, a_conv_w, a_conv_b, a_dt_bias, a_A_log, a_D, a_gnorm, a_out_proj, kv_norm, w_kv, b_kv, k_norm, b_norm, w_q, b_q, q_norm, sinks, w_o, b_o, f_norm, f_w_in, f_conv_w, f_conv_b, f_w_down, loss_target, m_a_norm, m_a_in_proj, m_a_conv_w, m_a_conv_b, m_a_dt_bias, m_a_A_log, m_a_D, m_a_gnorm, m_a_out_proj, m_kv_norm, m_w_kv, m_b_kv, m_k_norm, m_b_norm, m_w_q, m_b_q, m_q_norm, m_sinks, m_w_o, m_b_o, m_f_norm, m_f_w_in, m_f_conv_w, m_f_conv_b, m_f_w_down, v_a_norm, v_a_in_proj, v_a_conv_w, v_a_conv_b, v_a_dt_bias, v_a_A_log, v_a_D, v_a_gnorm, v_a_out_proj, v_kv_norm, v_w_kv, v_b_kv, v_k_norm, v_b_norm, v_w_q, v_b_q, v_q_norm, v_sinks, v_w_o, v_b_o, v_f_norm, v_f_w_in, v_f_conv_w, v_f_conv_b, v_f_w_down):
    given = dict(locals())
    w_in = {n: given[n] for n in ORDER}
    m_in = {n: given["m_" + n] for n in ORDER}
    v_in = {n: given["v_" + n] for n in ORDER}
    dev = 4 * lax.axis_index("x") + 2 * lax.axis_index("y") + lax.axis_index("c")

    w2, m2, v2 = _row_blocks(w_in), _row_blocks(m_in), _row_blocks(v_in)
    small_pack = _pack([w_in[n] for n in SMALL_SHARDED], 8, f32)
    shards = {k: w2[k].astype(MXU) for k in ROW_KEYS}
    in_all, small_all = run_comm("ag_head", gather_comm([shards["inT"], small_pack]))
    whole = {n: w_in[n] for n in REPLICATED}
    for n, g in zip(SMALL_SHARDED, _unpack(small_all, [w_in[n].shape for n in SMALL_SHARDED], lead=(N_DEV,))):
        whole[n] = _gathered_to_whole(n, g)
    W = _small_weights(whole)
    W["inT"] = in_all.reshape(-1, D)
    W["in_dtT"] = jnp.pad(W["inT"][IN_COLS - HEADS:], ((0, LANES - HEADS), (0, 0)))

    loss, dx, G, scatters = _local_step(x[0], positions.reshape(S, 1).astype(f32), loss_target[0], W, shards)
    grads = _small_grads(G, {n: whole[n].shape for n in REPLICATED})

    small_names = SMALL_SHARDED + REPLICATED
    small_part = _pack([grads[n] for n in small_names] + [loss], 8, f32)
    small_gather = gather_comm([small_part])
    run_comm("ag_small_grads", small_gather)
    parts = {}
    for rs in scatters:
        parts.update(rs.parts())

    single = tuple(k for k in ROW_KEYS if not k.startswith("f_"))
    stepped = {k: final_adamw(f"adamw_{k}", *parts[k], w2[k], m2[k], v2[k]) for k in single}
    g_out, delta, new_m, new_v = (_from_row_blocks({k: stepped[k][i] for k in single}) for i in range(4))
    for n, key, lay in (("f_w_in", "f_inT", lambda a: jnp.swapaxes(a, 1, 2)), ("f_w_down", "f_down", lambda a: a)):
        res = final_adamw_layers(f"adamw_{n}", *parts[key + "0"], *parts[key + "1"], lay(w_in[n]), lay(m_in[n]), lay(v_in[n]))
        g_out[n], delta[n], new_m[n], new_v[n] = (lay(a) for a in res)
    *small_sums, loss_all = _unpack(sum_devices(small_gather.results[0]), [grads[n].shape for n in small_names] + [(1, 1)])
    for n, g in zip(small_names, small_sums):
        if n in SMALL_SHARDED:
            g_out[n] = lax.dynamic_index_in_dim(_whole_to_shards(n, g), dev, axis=0, keepdims=False)
        else:
            g_out[n] = g.reshape(w_in[n].shape)

    packs = [_pack([src[n] for n in small_names], 8, f32) for src in (w_in, g_out, m_in, v_in)]
    outs = adamw("adamw_small", *packs)
    for dst, buf in zip((delta, new_m, new_v), outs):
        for n, a in zip(small_names, _unpack(buf, [w_in[n].shape for n in small_names])):
            dst[n] = a

    return (loss_all[0, 0], dx[None], *[g_out[n] for n in ORDER], *[delta[n] for n in ORDER],
            *[new_m[n] for n in ORDER], *[new_v[n] for n in ORDER])
```

```python
import functools
import math

import jax
import jax.numpy as jnp
from jax import lax
from jax.experimental import pallas as pl
from jax.experimental.pallas import tpu as pltpu

f32 = jnp.float32
bf16 = jnp.bfloat16
MXU = bf16

N_DEV = 8
S = 2048
D = 1024
EPS = 1e-5
INNER = 2048
HEADS = 32
HP = 64
GROUPS = 8
HPG = HEADS // GROUPS
NSTATE = 128
CONV_A = 4
CHUNK = 256
NCHUNK = S // CHUNK
AH = 64
QH = 16
KVH = 4
QPK = QH // KVH
WIN = 128
NBLK = S // WIN
ROPE_THETA = 10000.0
FFN = 2816
CONV_F = 3
LANES = 128
PACK_W = 1024
VMEM_LIMIT = 56 * 1024 * 1024

ADAM_LR, ADAM_B1, ADAM_B2, ADAM_EPS, ADAM_WD, ADAM_STEP = 0.001, 0.9, 0.999, 1e-08, 0.01, 10

MESH = pl.DeviceIdType.MESH


def _cparams(sem=None):
    return pltpu.CompilerParams(dimension_semantics=sem, vmem_limit_bytes=VMEM_LIMIT)


def _pick(n, cands):
    for c in cands:
        if n % c == 0:
            return c
    return n


class Comm:
    def __init__(self, ins, out_shapes, sems, start, finish):
        self.ins, self.out_shapes, self.sems, self.start, self.finish = list(ins), list(out_shapes), list(sems), start, finish
        self.results, self.children = None, ()

    def set_results(self, res):
        self.results, o = list(res), 0
        for ch in self.children:
            ch.set_results(res[o:o + len(ch.out_shapes)])
            o += len(ch.out_shapes)


def merge_comms(comms):
    def each(fn_name, ins, outs, sems):
        i = o = s = 0
        for c in comms:
            getattr(c, fn_name)(ins[i:i + len(c.ins)], outs[o:o + len(c.out_shapes)], sems[s:s + len(c.sems)])
            i, o, s = i + len(c.ins), o + len(c.out_shapes), s + len(c.sems)

    merged = Comm([a for c in comms for a in c.ins], [a for c in comms for a in c.out_shapes], [a for c in comms for a in c.sems],
                  functools.partial(each, "start"), functools.partial(each, "finish"))
    merged.children = tuple(comms)
    return merged


def _call(body, args, *, name, grid, in_specs, out_specs, out_shape, scratch=(), sem=None, comm=None):
    if comm is None:
        return pl.pallas_call(body, name=name, grid=grid, in_specs=list(in_specs), out_specs=list(out_specs),
                              out_shape=list(out_shape), scratch_shapes=list(scratch), compiler_params=_cparams(sem))(*args)
    n_in, n_out, n_scr, c_in, c_out = len(in_specs), len(out_shape), len(scratch), len(comm.ins), len(comm.out_shapes)
    any_spec = pl.BlockSpec(memory_space=pl.ANY)

    def outer(*refs):
        ins, c_ins = refs[:n_in], refs[n_in:n_in + c_in]
        o = n_in + c_in
        outs, c_outs = refs[o:o + n_out], refs[o + n_out:o + n_out + c_out]
        o += n_out + c_out
        scr, c_sems = refs[o:o + n_scr], refs[o + n_scr:]
        ids = [pl.program_id(i) for i in range(len(grid))]
        first = functools.reduce(jnp.logical_and, [i == 0 for i in ids])
        last = functools.reduce(jnp.logical_and, [i == g - 1 for i, g in zip(ids, grid)])

        @pl.when(first)
        def _():
            comm.start(c_ins, c_outs, c_sems)

        body(*ins, *outs, *scr)

        @pl.when(last)
        def _():
            comm.finish(c_ins, c_outs, c_sems)

    res = pl.pallas_call(
        outer, name=name, grid=grid, in_specs=list(in_specs) + [any_spec] * c_in,
        out_specs=list(out_specs) + [any_spec] * c_out, out_shape=list(out_shape) + comm.out_shapes,
        scratch_shapes=list(scratch) + comm.sems, compiler_params=_cparams(("arbitrary",) * len(grid)),
    )(*args, *comm.ins)
    comm.set_results(res[n_out:])
    return res[:n_out]


def matmul(name, a, b, mode, out_dtype=f32, bias=None, residual=None):
    if mode == "nn":
        (M, K), (K2, N) = a.shape, b.shape
    elif mode == "nt":
        (M, K), (N, K2) = a.shape, b.shape
    else:
        (K, M), (K2, N) = a.shape, b.shape
    assert K == K2, (name, a.shape, b.shape)
    if mode == "tn":
        tm, tn = M, _pick(N, (512, 256, 128) if M <= 1024 else (256, 128))
        a_spec = pl.BlockSpec((K, M), lambda j: (0, 0))
        b_spec = pl.BlockSpec((K, tn), lambda j: (0, j))
        dims = (((0,), (0,)), ((), ()))
        grid, o_map, row_map = (N // tn,), (lambda j: (0, j)), (lambda j: (0, j))
    else:
        tm, tn = (256 if N >= 2048 else 512), N
        a_spec = pl.BlockSpec((tm, K), lambda i: (i, 0))
        b_spec = pl.BlockSpec(b.shape, lambda i: (0, 0))
        dims = (((1,), (0,)), ((), ())) if mode == "nn" else (((1,), (1,)), ((), ()))
        grid, o_map, row_map = (M // tm,), (lambda i: (i, 0)), (lambda i: (0, 0))
    ins, in_specs = [a, b], [a_spec, b_spec]
    if bias is not None:
        ins.append(bias)
        in_specs.append(pl.BlockSpec((1, tn), row_map))
    if residual is not None:
        ins.append(residual)
        in_specs.append(pl.BlockSpec((tm, tn), o_map))
    has_bias, has_res = bias is not None, residual is not None

    def body(a_ref, b_ref, *rest):
        rest = list(rest)
        bias_ref = rest.pop(0) if has_bias else None
        res_ref = rest.pop(0) if has_res else None
        (o_ref,) = rest
        r = lax.dot_general(a_ref[...].astype(MXU), b_ref[...].astype(MXU), dims, preferred_element_type=f32)
        if has_bias:
            r = r + bias_ref[...]
        if has_res:
            r = r + res_ref[...]
        o_ref[...] = r.astype(out_dtype)

    return pl.pallas_call(
        body, name=name, grid=grid, in_specs=in_specs,
        out_specs=pl.BlockSpec((tm, tn), o_map),
        out_shape=jax.ShapeDtypeStruct((M, N), out_dtype),
        compiler_params=_cparams(("parallel",)),
    )(*ins)


def matmul_tn_stacked(name, a, b, out_dtype):
    R, K, M = a.shape
    N = b.shape[1]
    tn = _pick(N, (256, 128))

    def body(a_ref, b_ref, o_ref):
        o_ref[0] = lax.dot_general(a_ref[0].astype(MXU), b_ref[...].astype(MXU), (((0,), (0,)), ((), ())),
                                   preferred_element_type=f32).astype(out_dtype)

    out = pl.pallas_call(
        body, name=name, grid=(R, N // tn),
        in_specs=[pl.BlockSpec((1, K, M), lambda r, j: (r, 0, 0)), pl.BlockSpec((K, tn), lambda r, j: (0, j))],
        out_specs=pl.BlockSpec((1, M, tn), lambda r, j: (r, 0, j)),
        out_shape=jax.ShapeDtypeStruct((R, M, N), out_dtype),
        compiler_params=_cparams(("parallel", "parallel")),
    )(a, b)
    return out.reshape(R * M, N)


def rowwise(name, fn, rows, pars, outs, accs=(), tile=256, comm=None):
    n_in, n_out = len(rows) + len(pars), len(outs)
    in_specs = [pl.BlockSpec((None, tile, r[0].shape[2]), functools.partial(lambda i, lead: (lead, i, 0), lead=r[1]))
                if isinstance(r, tuple) else pl.BlockSpec((tile, r.shape[1]), lambda i: (i, 0)) for r in rows]
    rows = [r[0] if isinstance(r, tuple) else r for r in rows]
    in_specs += [pl.BlockSpec(p.shape, lambda i: (0, 0)) for p in pars]
    out_specs = [pl.BlockSpec((tile, c), lambda i: (i, 0)) for c, _ in outs]
    out_specs += [pl.BlockSpec(shp, lambda i: (0, 0)) for shp in accs]
    out_shape = [jax.ShapeDtypeStruct((S, c), dt) for c, dt in outs]
    out_shape += [jax.ShapeDtypeStruct(shp, f32) for shp in accs]

    def body(*refs):
        res = fn(*[r[...] for r in refs[:n_in]])
        o_refs = refs[n_in:n_in + n_out]
        a_refs = refs[n_in + n_out:]
        for ref, val in zip(o_refs, res[:n_out]):
            ref[...] = val.astype(ref.dtype)
        if a_refs:
            @pl.when(pl.program_id(0) == 0)
            def _():
                for ref in a_refs:
                    ref[...] = jnp.zeros_like(ref)
            for ref, val in zip(a_refs, res[n_out:]):
                ref[...] += val

    return _call(body, [*rows, *pars], name=name, grid=(S // tile,), in_specs=in_specs, out_specs=out_specs,
                 out_shape=out_shape, sem=("arbitrary",) if accs else ("parallel",), comm=comm)


def _sigmoid(x):
    return 0.5 * jnp.tanh(0.5 * x) + 0.5


def _silu(x):
    return x * _sigmoid(x)


def _dsilu(x):
    sg = _sigmoid(x)
    return sg * (1.0 + x * (1.0 - sg))


def _softplus(x):
    return jnp.maximum(x, 0.0) + jnp.log(1.0 + jnp.exp(-jnp.abs(x)))


def _rms_fwd(x, g):
    r = lax.rsqrt(jnp.mean(x * x, axis=-1, keepdims=True) + EPS)
    return x * r * g


def _rms_bwd(x, g, dh):
    r = lax.rsqrt(jnp.mean(x * x, axis=-1, keepdims=True) + EPS)
    xh = x * r
    dxh = dh * g
    dx = r * (dxh - xh * jnp.mean(dxh * xh, axis=-1, keepdims=True))
    return dx, jnp.sum(dh * xh, axis=0, keepdims=True)


def _taps(x, width):
    row = lax.broadcasted_iota(jnp.int32, (8, x.shape[1]), 0)

    def shifted(s):
        r = pltpu.roll(x, s, 0)
        return jnp.concatenate([jnp.where(row >= s, r[:8], 0.0), r[8:]], axis=0)

    return [shifted(s) for s in range(width - 1, 0, -1)] + [x]


def _conv(x, w, b, taps=None):
    width = w.shape[0]
    taps = _taps(x, width) if taps is None else taps
    out = b + w[0:1, :] * taps[0]
    for k in range(1, width):
        out = out + w[k:k + 1, :] * taps[k]
    return out


def _conv_bwd(x, w, dc, taps=None):
    width, n = w.shape[0], x.shape[0]
    taps = _taps(x, width) if taps is None else taps
    row = lax.broadcasted_iota(jnp.int32, (8, x.shape[1]), 0)
    dx = w[width - 1:width, :] * dc
    for k in range(width - 1):
        s = width - 1 - k
        r = pltpu.roll(dc, n - s, 0)
        dx = dx + w[k:k + 1, :] * jnp.concatenate([r[:n - 8], jnp.where(row < 8 - s, r[n - 8:], 0.0)], axis=0)
    dw = jnp.concatenate([jnp.sum(dc * t, axis=0, keepdims=True) for t in taps], axis=0)
    return dx, dw, jnp.sum(dc, axis=0, keepdims=True)


def _rope_tables(pos, inv_freq):
    ang = pos * inv_freq
    return jnp.cos(ang), jnp.sin(ang)


def _split2(v):
    hi = v.astype(bf16)
    return hi, (v - hi.astype(f32)).astype(bf16)


def _head_maps(width):
    shift = AH.bit_length() - 1
    to_head = (lax.broadcasted_iota(jnp.int32, (width, LANES), 0) >> shift) == lax.broadcasted_iota(jnp.int32, (width, LANES), 1)
    from_head = lax.broadcasted_iota(jnp.int32, (LANES, width), 0) == (lax.broadcasted_iota(jnp.int32, (LANES, width), 1) >> shift)
    return to_head.astype(bf16), from_head.astype(bf16)


def _head_sums(v, to_head):
    hi, lo = _split2(v)
    return jnp.dot(hi, to_head, preferred_element_type=f32) + jnp.dot(lo, to_head, preferred_element_type=f32)


def _head_spread(s, from_head):
    hi, lo = _split2(s)
    return jnp.dot(hi, from_head, preferred_element_type=f32) + jnp.dot(lo, from_head, preferred_element_type=f32)


def _rope_full(cos, sin, width):
    half = AH // 2
    pad = jnp.zeros((cos.shape[0], LANES - half), f32)
    r = lax.broadcasted_iota(jnp.int32, (LANES, width), 0)
    lane = lax.broadcasted_iota(jnp.int32, (LANES, width), 1)
    spread = ((lane & (half - 1)) == r).astype(bf16)
    full = lambda t: _head_spread(jnp.concatenate([t, pad], axis=1), spread)
    first = (lax.broadcasted_iota(jnp.int32, (1, width), 1) & (AH - 1)) < half
    sin_f = full(sin)
    return full(cos), jnp.where(first, -sin_f, sin_f), first


def _swap_halves(v, first):
    half, width = AH // 2, v.shape[1]
    return jnp.where(first, pltpu.roll(v, width - half, 1), pltpu.roll(v, half, 1))


def _headnorm_rope_fwd(x, g, cos, sin, heads):
    to_head, from_head = _head_maps(heads * AH)
    cos_f, sin_s, first = _rope_full(cos, sin, heads * AH)
    r = _head_spread(lax.rsqrt(_head_sums(x * x, to_head) * (1.0 / AH) + EPS), from_head)
    n = x * r * jnp.tile(g, (1, heads))
    return n * cos_f + _swap_halves(n, first) * sin_s


def _headnorm_rope_bwd(x, g, cos, sin, dout, heads):
    width = heads * AH
    to_head, from_head = _head_maps(width)
    cos_f, sin_s, first = _rope_full(cos, sin, width)
    r = _head_spread(lax.rsqrt(_head_sums(x * x, to_head) * (1.0 / AH) + EPS), from_head)
    xh = x * r
    dn = dout * cos_f - _swap_halves(dout, first) * sin_s
    dxh = dn * jnp.tile(g, (1, heads))
    m = _head_spread(_head_sums(dxh * xh, to_head) * (1.0 / AH), from_head)
    dx = r * (dxh - xh * m)
    dg_lanes = jnp.sum(dn * xh, axis=0, keepdims=True)
    fold = ((lax.broadcasted_iota(jnp.int32, (width, LANES), 0) & (AH - 1))
            == lax.broadcasted_iota(jnp.int32, (width, LANES), 1)).astype(f32)
    dg = jnp.dot(jnp.broadcast_to(dg_lanes, (8, width)), fold, precision=lax.Precision.HIGHEST, preferred_element_type=f32)
    return dx, dg[0:1, :AH]


def _ssd_prep(dt_pre, dt_bias, a_log, dt_s, acum_s, acumT_s):
    dt = _softplus(dt_pre + dt_bias)
    a = dt * (-jnp.exp(a_log))
    row = lax.broadcasted_iota(jnp.int32, (CHUNK, CHUNK), 0)
    col = lax.broadcasted_iota(jnp.int32, (CHUNK, CHUNK), 1)
    dt_s[...] = dt
    acum_s[...] = jnp.dot((col <= row).astype(f32), a, precision=lax.Precision.HIGHEST, preferred_element_type=f32)
    acumT_s[...] = lax.dot_general(a, (row <= col).astype(f32), (((0,), (0,)), ((), ())),
                                   precision=lax.Precision.HIGHEST, preferred_element_type=f32)


def _head_cols(h, dt_s, acum_s, acumT_s):
    lane = lax.broadcasted_iota(jnp.int32, (1, LANES), 1)
    oh_l = (lane == h).astype(f32)
    sub = lax.broadcasted_iota(jnp.int32, (LANES, 1), 0)
    oh_s = (sub == h).astype(f32)
    dt_h = jnp.sum(dt_s[...] * oh_l, axis=1, keepdims=True)
    ac_h = jnp.sum(acum_s[...] * oh_l, axis=1, keepdims=True)
    acr_h = jnp.sum(acumT_s[...] * oh_s, axis=0, keepdims=True)
    return oh_l, dt_h, ac_h, acr_h


def ssd_fwd(xs, Bm, Cm, dt_pre, dt_bias, a_log, d_skip, comm=None):
    def body(xs_ref, b_ref, c_ref, dtp_ref, bias_ref, alog_ref, d_ref, y_ref, st_ref, state, dt_s, acum_s, acumT_s):
        c, g = pl.program_id(0), pl.program_id(1)

        @pl.when(g == 0)
        def _():
            _ssd_prep(dtp_ref[...], bias_ref[...], alog_ref[...], dt_s, acum_s, acumT_s)

        row = lax.broadcasted_iota(jnp.int32, (CHUNK, CHUNK), 0)
        col = lax.broadcasted_iota(jnp.int32, (CHUNK, CHUNK), 1)
        causal = col <= row
        Bb, Cb = b_ref[...], c_ref[...]
        cb = lax.dot_general(Cb.astype(MXU), Bb.astype(MXU), (((1,), (1,)), ((), ())), preferred_element_type=f32)
        xs_blk = xs_ref[...]

        @pl.when(c == 0)
        def _():
            for j in range(HPG):
                state[g * HPG + j] = jnp.zeros((NSTATE, HP), f32)

        prevs = [state[g * HPG + j] for j in range(HPG)]
        y_off_all = jnp.dot(Cb.astype(MXU), jnp.concatenate(prevs, axis=1).astype(MXU), preferred_element_type=f32)
        ys, xds, e_ends = [], [], []
        for j in range(HPG):
            oh_l, dt_h, ac_h, acr_h = _head_cols(g * HPG + j, dt_s, acum_s, acumT_s)
            decay = jnp.exp(jnp.where(causal, ac_h - acr_h, -1e30))
            w = (cb * decay).astype(MXU)
            xs_h = xs_blk[:, HP * j:HP * (j + 1)]
            xd = xs_h * dt_h
            y_diag = jnp.dot(w, xd.astype(MXU), preferred_element_type=f32)
            y_off = y_off_all[:, HP * j:HP * (j + 1)] * jnp.exp(ac_h)
            d_h = jnp.sum(d_ref[...] * oh_l, axis=1, keepdims=True)
            ys.append(y_diag + y_off + xs_h * d_h)
            a_end = ac_h[CHUNK - 1:CHUNK, :]
            xds.append(xd * jnp.exp(a_end - ac_h))
            e_ends.append(jnp.exp(a_end))
        s_c = lax.dot_general(Bb.astype(MXU), jnp.concatenate(xds, axis=1).astype(MXU), (((0,), (0,)), ((), ())),
                              preferred_element_type=f32)
        for j in range(HPG):
            st_ref[0, j] = prevs[j]
            state[g * HPG + j] = prevs[j] * e_ends[j] + s_c[:, HP * j:HP * (j + 1)]
        y_ref[...] = jnp.concatenate(ys, axis=1)

    par = pl.BlockSpec((1, LANES), lambda c, g: (0, 0))
    return _call(
        body, [xs, Bm, Cm, dt_pre, dt_bias, a_log, d_skip], comm=comm, name="ssd_fwd", grid=(NCHUNK, GROUPS),
        in_specs=[pl.BlockSpec((CHUNK, HPG * HP), lambda c, g: (c, g)),
                  pl.BlockSpec((CHUNK, NSTATE), lambda c, g: (c, g)),
                  pl.BlockSpec((CHUNK, NSTATE), lambda c, g: (c, g)),
                  pl.BlockSpec((CHUNK, LANES), lambda c, g: (c, 0)), par, par, par],
        out_specs=[pl.BlockSpec((CHUNK, HPG * HP), lambda c, g: (c, g)),
                   pl.BlockSpec((1, HPG, NSTATE, HP), lambda c, g: (c, g, 0, 0))],
        out_shape=[jax.ShapeDtypeStruct((S, INNER), f32), jax.ShapeDtypeStruct((NCHUNK, HEADS, NSTATE, HP), f32)],
        scratch=[pltpu.VMEM((HEADS, NSTATE, HP), f32), pltpu.VMEM((CHUNK, LANES), f32),
                 pltpu.VMEM((CHUNK, LANES), f32), pltpu.VMEM((LANES, CHUNK), f32)],
        sem=("arbitrary", "arbitrary"))


def ssd_bwd(xs, Bm, Cm, dt_pre, dt_bias, a_log, d_skip, states, dy, comm=None):
    rev = lambda c: NCHUNK - 1 - c

    def body(xs_ref, b_ref, c_ref, dtp_ref, bias_ref, alog_ref, d_ref, st_ref, dy_ref,
             dxs_ref, db_ref, dc_ref, ddt_ref, dbias_ref, dalog_ref, dd_ref,
             dstate, dt_s, acum_s, acumT_s, dacum_s, ddt_s, da_s):
        c, g = pl.program_id(0), pl.program_id(1)

        @pl.when(g == 0)
        def _():
            _ssd_prep(dtp_ref[...], bias_ref[...], alog_ref[...], dt_s, acum_s, acumT_s)
            dacum_s[...] = jnp.zeros_like(dacum_s)
            ddt_s[...] = jnp.zeros_like(ddt_s)

        @pl.when((c == 0) & (g == 0))
        def _():
            da_s[...] = jnp.zeros_like(da_s)
            dd_ref[...] = jnp.zeros_like(dd_ref)
            dbias_ref[...] = jnp.zeros_like(dbias_ref)
            dalog_ref[...] = jnp.zeros_like(dalog_ref)

        row = lax.broadcasted_iota(jnp.int32, (CHUNK, CHUNK), 0)
        col = lax.broadcasted_iota(jnp.int32, (CHUNK, CHUNK), 1)
        sub_l = lax.broadcasted_iota(jnp.int32, (CHUNK, 1), 0)
        last = (sub_l == CHUNK - 1).astype(f32)
        nt = (((1,), (1,)), ((), ()))
        tn = (((0,), (0,)), ((), ()))
        Bb, Cb = b_ref[...], c_ref[...]
        Bm_, Cm_ = Bb.astype(MXU), Cb.astype(MXU)
        cb = lax.dot_general(Cm_, Bm_, nt, preferred_element_type=f32)
        bc = lax.dot_general(Bm_, Cm_, nt, preferred_element_type=f32)
        xs_blk, dy_blk = xs_ref[...], dy_ref[...]
        dxs, dB, dC = [], jnp.zeros((CHUNK, NSTATE), f32), jnp.zeros((CHUNK, NSTATE), f32)
        for j in range(HPG):
            h = g * HPG + j
            oh_l, dt_h, ac_h, acr_h = _head_cols(h, dt_s, acum_s, acumT_s)

            @pl.when(c == 0)
            def _():
                dstate[h] = jnp.zeros((NSTATE, HP), f32)

            dnext = dstate[h]
            prev = st_ref[0, j]
            lm = jnp.exp(jnp.where(col <= row, ac_h - acr_h, -1e30))
            lmT = jnp.exp(jnp.where(row <= col, acr_h - ac_h, -1e30))
            xs_h = xs_blk[:, HP * j:HP * (j + 1)]
            dy_h = dy_blk[:, HP * j:HP * (j + 1)]
            xd = xs_h * dt_h
            xdm, dym = xd.astype(MXU), dy_h.astype(MXU)
            ea = jnp.exp(ac_h)
            a_end = ac_h[CHUNK - 1:CHUNK, :]
            e_end = jnp.exp(a_end)
            dte = jnp.exp(a_end - ac_h)
            dnm, pvm = dnext.astype(MXU), prev.astype(MXU)
            bd = jnp.dot(Bm_, dnm, preferred_element_type=f32)
            dxd = jnp.dot((bc * lmT).astype(MXU), dym, preferred_element_type=f32) + dte * bd
            dw = lax.dot_general(dym, xdm, nt, preferred_element_type=f32)
            dwT = lax.dot_general(xdm, dym, nt, preferred_element_type=f32)
            dcb = dw * lm
            dbc = dwT * lmT
            eady = (ea * dy_h).astype(MXU)
            dC = dC + jnp.dot(dcb.astype(MXU), Bm_, preferred_element_type=f32) \
                + lax.dot_general(eady, pvm, nt, preferred_element_type=f32)
            dB = dB + jnp.dot(dbc.astype(MXU), Cm_, preferred_element_type=f32) \
                + dte * lax.dot_general(xdm, dnm, nt, preferred_element_type=f32)
            dstate[h] = lax.dot_general(Cm_, eady, tn, preferred_element_type=f32) + e_end * dnext
            r1 = jnp.sum(dcb * cb, axis=1, keepdims=True)
            r2 = jnp.sum(dbc * bc, axis=1, keepdims=True)
            y_off = jnp.dot(Cm_, pvm, preferred_element_type=f32) * ea
            t3 = jnp.sum(dy_h * y_off, axis=1, keepdims=True)
            t4 = jnp.sum(bd * xd, axis=1, keepdims=True) * dte
            end_extra = jnp.sum(t4, axis=0, keepdims=True) + e_end * jnp.sum(jnp.sum(prev * dnext, axis=1, keepdims=True), axis=0, keepdims=True)
            dacum_h = r1 - r2 + t3 - t4 + last * end_extra
            dacum_s[...] += dacum_h * oh_l
            ddt_s[...] += jnp.sum(dxd * xs_h, axis=1, keepdims=True) * oh_l
            d_h = jnp.sum(d_ref[...] * oh_l, axis=1, keepdims=True)
            dxs.append(dxd * dt_h + dy_h * d_h)
            dd_ref[...] += oh_l * jnp.sum(jnp.sum(dy_h * xs_h, axis=1, keepdims=True), axis=0, keepdims=True)
        dxs_ref[...] = jnp.concatenate(dxs, axis=1)
        db_ref[...] = dB
        dc_ref[...] = dC

        @pl.when(g == GROUPS - 1)
        def _():
            a_row = -jnp.exp(alog_ref[...])
            da = jnp.dot((row <= col).astype(f32), dacum_s[...], precision=lax.Precision.HIGHEST, preferred_element_type=f32)
            da_s[...] += jnp.sum(da * dt_s[...], axis=0, keepdims=True)
            z = dtp_ref[...] + bias_ref[...]
            ddt_pre = (ddt_s[...] + da * a_row) * _sigmoid(z)
            ddt_ref[...] = ddt_pre.astype(ddt_ref.dtype)
            dbias_ref[...] += jnp.sum(ddt_pre, axis=0, keepdims=True)

            @pl.when(c == NCHUNK - 1)
            def _():
                dalog_ref[...] = da_s[...] * a_row

    par = pl.BlockSpec((1, LANES), lambda c, g: (0, 0))
    return _call(
        body, [xs, Bm, Cm, dt_pre, dt_bias, a_log, d_skip, states, dy], comm=comm, name="ssd_bwd", grid=(NCHUNK, GROUPS),
        in_specs=[pl.BlockSpec((CHUNK, HPG * HP), lambda c, g: (rev(c), g)),
                  pl.BlockSpec((CHUNK, NSTATE), lambda c, g: (rev(c), g)),
                  pl.BlockSpec((CHUNK, NSTATE), lambda c, g: (rev(c), g)),
                  pl.BlockSpec((CHUNK, LANES), lambda c, g: (rev(c), 0)), par, par, par,
                  pl.BlockSpec((1, HPG, NSTATE, HP), lambda c, g: (rev(c), g, 0, 0)),
                  pl.BlockSpec((CHUNK, HPG * HP), lambda c, g: (rev(c), g))],
        out_specs=[pl.BlockSpec((CHUNK, HPG * HP), lambda c, g: (rev(c), g)),
                   pl.BlockSpec((CHUNK, NSTATE), lambda c, g: (rev(c), g)),
                   pl.BlockSpec((CHUNK, NSTATE), lambda c, g: (rev(c), g)),
                   pl.BlockSpec((CHUNK, LANES), lambda c, g: (rev(c), 0)), par, par, par],
        out_shape=[jax.ShapeDtypeStruct((S, INNER), f32), jax.ShapeDtypeStruct((S, GROUPS * NSTATE), f32),
                   jax.ShapeDtypeStruct((S, GROUPS * NSTATE), f32), jax.ShapeDtypeStruct((S, LANES), MXU),
                   jax.ShapeDtypeStruct((1, LANES), f32), jax.ShapeDtypeStruct((1, LANES), f32),
                   jax.ShapeDtypeStruct((1, LANES), f32)],
        scratch=[pltpu.VMEM((HEADS, NSTATE, HP), f32), pltpu.VMEM((CHUNK, LANES), f32),
                 pltpu.VMEM((CHUNK, LANES), f32), pltpu.VMEM((LANES, CHUNK), f32),
                 pltpu.VMEM((CHUNK, LANES), f32), pltpu.VMEM((CHUNK, LANES), f32), pltpu.VMEM((1, LANES), f32)],
        sem=("arbitrary", "arbitrary"))


ATT_STACK_FWD, ATT_STACK_BWD = 4, 2


def _attn_kv(kp, kc, vp, vc, hk):
    sl = slice(AH * hk, AH * (hk + 1))
    return (jnp.concatenate([kp[:, sl], kc[:, sl]], axis=0).astype(MXU),
            jnp.concatenate([vp[:, sl], vc[:, sl]], axis=0).astype(MXU))


def _stack_heads(x, heads):
    return jnp.concatenate([x[:, AH * h:AH * (h + 1)] for h in heads], axis=0)


def _attn_block(n, q, kb, sinks, heads):
    rows = len(heads) * WIN
    qi = lax.broadcasted_iota(jnp.int32, (rows, 2 * WIN), 0) & (WIN - 1)
    ki = lax.broadcasted_iota(jnp.int32, (rows, 2 * WIN), 1)
    rel = qi + WIN - ki
    mask = (rel >= 0) & (rel < WIN) & ((ki >= WIN) | (n > 0))
    qg = _stack_heads(q, heads).astype(MXU)
    s = lax.dot_general(qg, kb, (((1,), (1,)), ((), ())), preferred_element_type=f32) * (AH ** -0.5)
    s = jnp.where(mask, s, -1e30)
    sink = jnp.concatenate([jnp.broadcast_to(sinks[:, h:h + 1], (WIN, 1)) for h in heads], axis=0)
    m = jnp.maximum(jnp.max(s, axis=1, keepdims=True), sink)
    p = jnp.exp(s - m)
    ps = jnp.exp(sink - m)
    inv = 1.0 / (jnp.sum(p, axis=1, keepdims=True) + ps)
    return qg, p * inv, ps * inv


def _head_blocks(hk, stack):
    return [list(range(QPK * hk + i, QPK * hk + i + stack)) for i in range(0, QPK, stack)]


def _kv_specs():
    prev = lambda n: (jnp.maximum(n - 1, 0), 0)
    cur = lambda n: (n, 0)
    w = KVH * AH
    return [pl.BlockSpec((WIN, w), prev), pl.BlockSpec((WIN, w), cur), pl.BlockSpec((WIN, w), prev), pl.BlockSpec((WIN, w), cur)]


def attn_fwd(q, k, v, sinks, comm=None):
    def body(q_ref, kp_ref, kc_ref, vp_ref, vc_ref, s_ref, o_ref):
        n = pl.program_id(0)
        q_, kp, kc, vp, vc, sk = q_ref[...], kp_ref[...], kc_ref[...], vp_ref[...], vc_ref[...], s_ref[...]
        outs = []
        for hk in range(KVH):
            kb, vb = _attn_kv(kp, kc, vp, vc, hk)
            for heads in _head_blocks(hk, ATT_STACK_FWD):
                _, pr, _ = _attn_block(n, q_, kb, sk, heads)
                o = jnp.dot(pr.astype(MXU), vb, preferred_element_type=f32)
                outs += [o[WIN * i:WIN * (i + 1)] for i in range(len(heads))]
        o_ref[...] = jnp.concatenate(outs, axis=1)

    return _call(
        body, [q, k, k, v, v, sinks], comm=comm, name="attn_fwd", grid=(NBLK,),
        in_specs=[pl.BlockSpec((WIN, D), lambda n: (n, 0))] + _kv_specs() + [pl.BlockSpec((1, QH), lambda n: (0, 0))],
        out_specs=[pl.BlockSpec((WIN, D), lambda n: (n, 0))],
        out_shape=[jax.ShapeDtypeStruct((S, D), f32)], sem=("parallel",))[0]


def attn_bwd(q, k, v, sinks, dout, comm=None):
    def body(q_ref, kp_ref, kc_ref, vp_ref, vc_ref, s_ref, do_ref, dq_ref, dkp_ref, dkc_ref, dvp_ref, dvc_ref, ds_ref):
        n = pl.program_id(0)

        @pl.when(n == 0)
        def _():
            ds_ref[...] = jnp.zeros_like(ds_ref)

        q_, kp, kc, vp, vc, sk, do = q_ref[...], kp_ref[...], kc_ref[...], vp_ref[...], vc_ref[...], s_ref[...], do_ref[...]
        lane = lax.broadcasted_iota(jnp.int32, (1, QH), 1)
        nt = (((1,), (1,)), ((), ()))
        tn = (((0,), (0,)), ((), ()))
        dqs, dkps, dkcs, dvps, dvcs = [], [], [], [], []
        dsink = jnp.zeros((1, QH), f32)
        for hk in range(KVH):
            kb, vb = _attn_kv(kp, kc, vp, vc, hk)
            dkb, dvb = jnp.zeros((2 * WIN, AH), f32), jnp.zeros((2 * WIN, AH), f32)
            for heads in _head_blocks(hk, ATT_STACK_BWD):
                qg, pr, prs = _attn_block(n, q_, kb, sk, heads)
                dog = _stack_heads(do, heads).astype(MXU)
                dp = lax.dot_general(dog, vb, nt, preferred_element_type=f32)
                dvb = dvb + lax.dot_general(pr.astype(MXU), dog, tn, preferred_element_type=f32)
                delta = jnp.sum(pr * dp, axis=1, keepdims=True)
                ds = (pr * (dp - delta)).astype(MXU)
                dsk = -prs * delta
                for i, h in enumerate(heads):
                    dsink = dsink + jnp.sum(dsk[WIN * i:WIN * (i + 1)], axis=0, keepdims=True) * (lane == h).astype(f32)
                dqg = jnp.dot(ds, kb, preferred_element_type=f32) * (AH ** -0.5)
                dkb = dkb + lax.dot_general(ds, qg, tn, preferred_element_type=f32) * (AH ** -0.5)
                dqs += [dqg[WIN * i:WIN * (i + 1)] for i in range(len(heads))]
            dkps.append(dkb[:WIN])
            dkcs.append(dkb[WIN:])
            dvps.append(dvb[:WIN])
            dvcs.append(dvb[WIN:])
        dq_ref[...] = jnp.concatenate(dqs, axis=1)
        dkp_ref[...] = jnp.concatenate(dkps, axis=1)
        dkc_ref[...] = jnp.concatenate(dkcs, axis=1)
        dvp_ref[...] = jnp.concatenate(dvps, axis=1)
        dvc_ref[...] = jnp.concatenate(dvcs, axis=1)
        ds_ref[...] += dsink

    w = KVH * AH
    blk = lambda width: pl.BlockSpec((WIN, width), lambda n: (n, 0))
    return _call(
        body, [q, k, k, v, v, sinks, dout], comm=comm, name="attn_bwd", grid=(NBLK,),
        in_specs=[blk(D)] + _kv_specs() + [pl.BlockSpec((1, QH), lambda n: (0, 0)), blk(D)],
        out_specs=[blk(D), blk(w), blk(w), blk(w), blk(w), pl.BlockSpec((1, QH), lambda n: (0, 0))],
        out_shape=[jax.ShapeDtypeStruct((S, D), f32)] + [jax.ShapeDtypeStruct((S, w), f32)] * 4 + [jax.ShapeDtypeStruct((1, QH), f32)],
        sem=("arbitrary",))


def kv_bwd(kv, pos, inv_freq, k_norm, dkp, dkc, dvp, dvc):
    w = KVH * AH

    def body(kv_ref, pos_ref, if_ref, g_ref, dkp_ref, dkc_ref, dvp_ref, dvc_ref, o_ref, dg_ref, db_ref):
        n = pl.program_id(0)

        @pl.when(n == 0)
        def _():
            dg_ref[...] = jnp.zeros_like(dg_ref)
            db_ref[...] = jnp.zeros_like(db_ref)

        inside = (n < NBLK - 1).astype(f32)
        dk = dkc_ref[...] + inside * dkp_ref[...]
        dv = dvc_ref[...] + inside * dvp_ref[...]
        cos, sin = _rope_tables(pos_ref[...], if_ref[...])
        dkpre, dg = _headnorm_rope_bwd(kv_ref[...], g_ref[...], cos, sin, dk, KVH)
        dkv = jnp.concatenate([dkpre, dv], axis=1)
        o_ref[...] = dkv.astype(o_ref.dtype)
        dg_ref[...] += dg
        db_ref[...] += jnp.sum(dkv, axis=0, keepdims=True)

    nxt = lambda n: (jnp.minimum(n + 1, NBLK - 1), 0)
    cur = lambda n: (n, 0)
    const = lambda n: (0, 0)
    return pl.pallas_call(
        body, name="kv_bwd", grid=(NBLK,),
        in_specs=[pl.BlockSpec((WIN, w), cur), pl.BlockSpec((WIN, 1), cur), pl.BlockSpec((1, AH // 2), const),
                  pl.BlockSpec((1, AH), const), pl.BlockSpec((WIN, w), nxt), pl.BlockSpec((WIN, w), cur),
                  pl.BlockSpec((WIN, w), nxt), pl.BlockSpec((WIN, w), cur)],
        out_specs=[pl.BlockSpec((WIN, 2 * w), cur), pl.BlockSpec((1, AH), const), pl.BlockSpec((1, 2 * w), const)],
        out_shape=[jax.ShapeDtypeStruct((S, 2 * w), MXU), jax.ShapeDtypeStruct((1, AH), f32), jax.ShapeDtypeStruct((1, 2 * w), f32)],
        compiler_params=_cparams(("arbitrary",)),
    )(kv, pos, inv_freq, k_norm, dkp, dkc, dvp, dvc)


def _adam_math(w, g, m, v):
    m = ADAM_B1 * m + (1.0 - ADAM_B1) * g
    v = ADAM_B2 * v + (1.0 - ADAM_B2) * (g * g)
    m_hat = m / (1.0 - ADAM_B1 ** ADAM_STEP)
    v_hat = v / (1.0 - ADAM_B2 ** ADAM_STEP)
    return -ADAM_LR * (m_hat / (jnp.sqrt(v_hat) + ADAM_EPS) + ADAM_WD * w), m, v


def adamw(name, w, g, m, v):
    R, C = w.shape
    tr = _pick(R, (256, 128, 64, 32, 16, 8))
    tc = C if tr < R or C % 256 else 256

    def body(w_ref, g_ref, m_ref, v_ref, d_ref, nm_ref, nv_ref):
        d_ref[...], nm_ref[...], nv_ref[...] = _adam_math(w_ref[...], g_ref[...], m_ref[...], v_ref[...])

    spec = pl.BlockSpec((tr, tc), lambda i, j: (i, j))
    return pl.pallas_call(
        body, name=name, grid=(R // tr, C // tc), in_specs=[spec] * 4, out_specs=[spec] * 3,
        out_shape=[jax.ShapeDtypeStruct((R, C), f32)] * 3, compiler_params=_cparams(("parallel", "parallel")),
    )(w, g, m, v)


def _me():
    return lax.axis_index("x"), lax.axis_index("y"), lax.axis_index("c")


def gather_comm(xs):
    n = len(xs)

    def parts(x_refs, o_refs, sems):
        send_sems, recv_sems, local_sems = sems
        x, y, c = _me()
        me, sibling = (x, y, c), (x, y, 1 - c)
        chips = [(1 - x, y), (x, 1 - y), (1 - x, 1 - y)]

        def copy(a, k, block, to, src=None):
            dst = o_refs[a].at[4 * block[0] + 2 * block[1] + block[2]]
            return pltpu.make_async_remote_copy(
                src_ref=dst if src is None else src, dst_ref=dst,
                send_sem=send_sems.at[7 * a + k], recv_sem=recv_sems.at[7 * a + k], device_id=to, device_id_type=MESH)

        mine = [pltpu.make_async_copy(x_refs[a], o_refs[a].at[4 * x + 2 * y + c], local_sems.at[a]) for a in range(n)]
        first = []
        for a in range(n):
            first.append(copy(a, 0, me, sibling, src=x_refs[a]))
            first += [copy(a, 1 + j, me, (*chip, c), src=x_refs[a]) for j, chip in enumerate(chips)]
        return copy, mine, first, me, sibling, chips, c

    def start(x_refs, o_refs, sems):
        _, mine, first, *_ = parts(x_refs, o_refs, sems)
        for cp in mine + first:
            cp.start()

    def finish(x_refs, o_refs, sems):
        copy, mine, first, me, sibling, chips, c = parts(x_refs, o_refs, sems)
        passed = []
        for j, chip in enumerate(chips):
            for a in range(n):
                copy(a, 1 + j, (*chip, c), me).wait_recv()
                cp = copy(a, 4 + j, (*chip, c), sibling)
                cp.start()
                passed.append(cp)
        for a in range(n):
            copy(a, 0, sibling, me).wait_recv()
            for j, chip in enumerate(chips):
                copy(a, 4 + j, (*chip, 1 - c), me).wait_recv()
        for cp in first + passed:
            cp.wait_send()
        for cp in mine:
            cp.wait()

    return Comm(xs, [jax.ShapeDtypeStruct((N_DEV,) + a.shape, a.dtype) for a in xs],
                [pltpu.SemaphoreType.DMA((7 * n,)), pltpu.SemaphoreType.DMA((7 * n,)), pltpu.SemaphoreType.DMA((n,))], start, finish)


def run_comm(name, comm):
    _call(lambda: None, [], name=name, grid=(1,), in_specs=[], out_specs=[], out_shape=[], comm=comm)
    return comm.results


def sibling_comm(gs):
    n = len(gs)

    def copies(g_refs, o_refs, sems):
        x, y, c = _me()
        return [pltpu.make_async_remote_copy(
            src_ref=g_refs[a].at[:, 1 - c], dst_ref=o_refs[a], send_sem=sems[0].at[a], recv_sem=sems[1].at[a],
            device_id=(x, y, 1 - c), device_id_type=MESH) for a in range(n)]

    def start(g_refs, o_refs, sems):
        for cp in copies(g_refs, o_refs, sems):
            cp.start()

    def finish(g_refs, o_refs, sems):
        for cp in copies(g_refs, o_refs, sems):
            cp.wait()

    return Comm(gs, [jax.ShapeDtypeStruct((4,) + g.shape[2:], g.dtype) for g in gs],
                [pltpu.SemaphoreType.DMA((n,)), pltpu.SemaphoreType.DMA((n,))], start, finish)


def chip_comm(ts):
    n = len(ts)

    def copies(t_refs, o_refs, sems):
        x, y, c = _me()
        chips = [(1 - x, y), (x, 1 - y), (1 - x, 1 - y)]
        return [pltpu.make_async_remote_copy(
            src_ref=t_refs[a].at[2 * px + py], dst_ref=o_refs[a].at[j],
            send_sem=sems[0].at[3 * a + j], recv_sem=sems[1].at[3 * a + j],
            device_id=(px, py, c), device_id_type=MESH) for j, (px, py) in enumerate(chips) for a in range(n)]

    def start(t_refs, o_refs, sems):
        for cp in copies(t_refs, o_refs, sems):
            cp.start()

    def finish(t_refs, o_refs, sems):
        for cp in copies(t_refs, o_refs, sems):
            cp.wait()

    return Comm(ts, [jax.ShapeDtypeStruct((3,) + t.shape[1:], t.dtype) for t in ts],
                [pltpu.SemaphoreType.DMA((3 * n,)), pltpu.SemaphoreType.DMA((3 * n,))], start, finish)


def _row_tile(rows):
    return _pick(rows, (512, 304, 256, 128))


def pair_add(name, g, r):
    _, _, R, C = g.shape
    tr = _row_tile(R)

    def body(c_ref, g_ref, r_ref, o_ref):
        o_ref[0] = (g_ref[0, 0].astype(f32) + r_ref[0].astype(f32)).astype(o_ref.dtype)

    return pl.pallas_call(
        body, name=name,
        grid_spec=pltpu.PrefetchScalarGridSpec(
            num_scalar_prefetch=1, grid=(4, R // tr),
            in_specs=[pl.BlockSpec((1, 1, tr, C), lambda p, i, c: (p, c[0], i, 0)),
                      pl.BlockSpec((1, tr, C), lambda p, i, c: (p, i, 0))],
            out_specs=pl.BlockSpec((1, tr, C), lambda p, i, c: (p, i, 0))),
        out_shape=jax.ShapeDtypeStruct((4, R, C), g.dtype),
        compiler_params=_cparams(("parallel", "parallel")),
    )(lax.axis_index("c").reshape(1).astype(jnp.int32), g, r)


def _sum_of_four(t_ref, r_ref):
    return ((t_ref[0].astype(f32) + r_ref[0].astype(f32)) + r_ref[1].astype(f32)) + r_ref[2].astype(f32)


def _my_chip():
    return (2 * lax.axis_index("x") + lax.axis_index("y")).reshape(1).astype(jnp.int32)


def final_adamw(name, t, r, w, m, v):
    _, R, C = t.shape
    tr = _pick(R, (256, 128, 64, 32, 16))
    tc = C if tr < R or C % 256 else 256

    def body(p_ref, t_ref, r_ref, w_ref, m_ref, v_ref, g_ref, d_ref, nm_ref, nv_ref):
        g_ = _sum_of_four(t_ref, r_ref)
        g_ref[...] = g_
        d_ref[...], nm_ref[...], nv_ref[...] = _adam_math(w_ref[...], g_, m_ref[...], v_ref[...])

    flat = pl.BlockSpec((tr, tc), lambda i, j, p: (i, j))
    return pl.pallas_call(
        body, name=name,
        grid_spec=pltpu.PrefetchScalarGridSpec(
            num_scalar_prefetch=1, grid=(R // tr, C // tc),
            in_specs=[pl.BlockSpec((1, tr, tc), lambda i, j, p: (p[0], i, j)),
                      pl.BlockSpec((3, tr, tc), lambda i, j, p: (0, i, j)), flat, flat, flat],
            out_specs=[flat] * 4),
        out_shape=[jax.ShapeDtypeStruct((R, C), f32)] * 4,
        compiler_params=_cparams(("parallel", "parallel")),
    )(_my_chip(), t, r, w, m, v)


def final_sum(name, t, r):
    _, R, C = t.shape
    tc = _pick(C, (256, 128))

    def body(p_ref, t_ref, r_ref, g_ref):
        g_ref[...] = _sum_of_four(t_ref, r_ref)

    return pl.pallas_call(
        body, name=name,
        grid_spec=pltpu.PrefetchScalarGridSpec(
            num_scalar_prefetch=1, grid=(C // tc,),
            in_specs=[pl.BlockSpec((1, R, tc), lambda j, p: (p[0], 0, j)), pl.BlockSpec((3, R, tc), lambda j, p: (0, 0, j))],
            out_specs=pl.BlockSpec((R, tc), lambda j, p: (0, j))),
        out_shape=jax.ShapeDtypeStruct((R, C), f32),
        compiler_params=_cparams(("parallel",)),
    )(_my_chip(), t, r)


def final_adamw_layers(name, t0, r0, t1, r1, w, m, v):
    _, R, C = t0.shape
    tr = _pick(R, (256, 128, 64, 32, 16))

    def body(p_ref, t0_ref, r0_ref, t1_ref, r1_ref, w_ref, m_ref, v_ref, g_ref, d_ref, nm_ref, nv_ref):
        g_ = jnp.where(pl.program_id(0) == 0, _sum_of_four(t0_ref, r0_ref), _sum_of_four(t1_ref, r1_ref))
        g_ref[0] = g_
        d_ref[0], nm_ref[0], nv_ref[0] = _adam_math(w_ref[0], g_, m_ref[0], v_ref[0])

    mine = pl.BlockSpec((1, tr, C), lambda l, i, p: (p[0], i, 0))
    theirs = pl.BlockSpec((3, tr, C), lambda l, i, p: (0, i, 0))
    layer = pl.BlockSpec((1, tr, C), lambda l, i, p: (l, i, 0))
    return pl.pallas_call(
        body, name=name,
        grid_spec=pltpu.PrefetchScalarGridSpec(
            num_scalar_prefetch=1, grid=(2, R // tr),
            in_specs=[mine, theirs, mine, theirs, layer, layer, layer], out_specs=[layer] * 4),
        out_shape=[jax.ShapeDtypeStruct((2, R, C), f32)] * 4,
        compiler_params=_cparams(("parallel", "parallel")),
    )(_my_chip(), t0, r0, t1, r1, w, m, v)


class ReduceScatter:
    def __init__(self, tag, keys, grads):
        self.tag, self.keys, self.grads = tag, keys, grads
        self.send = [g.reshape((4, 2, g.shape[0] // N_DEV) + g.shape[1:]) for g in grads]

    def sibling(self):
        self.c1 = sibling_comm(self.send)
        return self.c1

    def chips(self):
        self.pairs = [pair_add(f"rs_pair_add_{self.tag}{i}", g, r) for i, (g, r) in enumerate(zip(self.send, self.c1.results))]
        self.c2 = chip_comm(self.pairs)
        return self.c2

    def parts(self):
        return {k: (t, r) for k, t, r in zip(self.keys, self.pairs, self.c2.results)}


IN_ROWS = {"z": (0, 2048), "xs": (2048, 4096), "B": (4096, 5120), "C": (5120, 6144)}
IN_COLS = 2 * INNER + 2 * GROUPS * NSTATE + HEADS


def sum_devices(g):
    def body(g_ref, o_ref):
        acc = g_ref[0]
        for i in range(1, N_DEV):
            acc = acc + g_ref[i]
        o_ref[...] = acc

    return pl.pallas_call(body, name="sum_devices", out_shape=jax.ShapeDtypeStruct(g.shape[1:], f32),
                          compiler_params=_cparams())(g)


def _pack(parts, unit, dtype, lead=()):
    flat = jnp.concatenate([p.reshape(lead + (-1,)).astype(dtype) for p in parts], axis=-1)
    n = flat.shape[-1]
    rows = -(-n // (unit * PACK_W)) * unit
    flat = jnp.pad(flat, [(0, 0)] * len(lead) + [(0, rows * PACK_W - n)])
    return flat.reshape(lead + (rows, PACK_W))


def _unpack(buf, shapes, lead=()):
    flat = buf.reshape(lead + (-1,))
    out, off = [], 0
    for shp in shapes:
        n = math.prod(shp)
        out.append(flat[..., off:off + n].reshape(lead + tuple(shp)))
        off += n
    return out


def _pad_lanes(a):
    return jnp.pad(a, [(0, 0)] * (a.ndim - 1) + [(0, LANES - a.shape[-1])])


_NN = (((1,), (0,)), ((), ()))
_NT = (((1,), (1,)), ((), ()))


def _mm(a, b, dims):
    return lax.dot_general(a.astype(MXU), b.astype(MXU), dims, preferred_element_type=f32)


def _ffn_fwd(tag, x, h, w_inT, conv_w, conv_b, mid_comm=None):
    ct, nblk = FFN_CT, FFN // FFN_CT

    def body(h_ref, wg_ref, wv_ref, cw_ref, cb_ref, gp_ref, v_ref, a_ref):
        h_ = h_ref[...]
        gp, v_ = _mm(h_, wg_ref[...], _NT), _mm(h_, wv_ref[...], _NT)
        gp_ref[...] = gp
        v_ref[...] = v_
        a_ref[...] = (_silu(_conv(gp, cw_ref[...], cb_ref[...])) * v_).astype(a_ref.dtype)

    col = pl.BlockSpec((S, ct), lambda j: (0, j))
    gate_pre, val, act = _call(
        body, [h, w_inT, w_inT, conv_w, conv_b], comm=mid_comm, name=f"{tag}_in", grid=(nblk,),
        in_specs=[pl.BlockSpec((S, D), lambda j: (0, 0)), pl.BlockSpec((ct, D), lambda j: (j, 0)),
                  pl.BlockSpec((ct, D), lambda j: (nblk + j, 0)), pl.BlockSpec((CONV_F, ct), lambda j: (0, j)),
                  pl.BlockSpec((1, ct), lambda j: (0, j))],
        out_specs=[col, col, col],
        out_shape=[jax.ShapeDtypeStruct((S, FFN), f32), jax.ShapeDtypeStruct((S, FFN), f32), jax.ShapeDtypeStruct((S, FFN), MXU)],
        sem=("parallel",))
    return act, (x, h, gate_pre, val, act)


FFN_CT = 256
CONV_CT = 256


def _proj_conv(name, h, wT, row0, cw, cb, comm=None):
    C, ct = cw.shape[1], CONV_CT

    def body(h_ref, w_ref, cw_ref, cb_ref, p_ref, c_ref):
        p = _mm(h_ref[...], w_ref[...], _NT)
        p_ref[...] = p
        c_ref[...] = _silu(_conv(p, cw_ref[...], cb_ref[...]))

    col = pl.BlockSpec((S, ct), lambda j: (0, j))
    return _call(
        body, [h, wT, cw, cb], comm=comm, name=name, grid=(C // ct,),
        in_specs=[pl.BlockSpec((S, D), lambda j: (0, 0)), pl.BlockSpec((ct, D), lambda j: (row0 // ct + j, 0)),
                  pl.BlockSpec((CONV_A, ct), lambda j: (0, j)), pl.BlockSpec((1, ct), lambda j: (0, j))],
        out_specs=[col, col], out_shape=[jax.ShapeDtypeStruct((S, C), f32)] * 2, sem=("parallel",))


def _dconv_wgrad(name, pre, dconv, cw, cb, h):
    C, ct = cw.shape[1], CONV_CT

    def body(p_ref, do_ref, cw_ref, cb_ref, h_ref, dp_ref, g_ref, dw_ref, db_ref):
        p_, w_ = p_ref[...], cw_ref[...]
        taps = _taps(p_, CONV_A)
        dx, dw, db = _conv_bwd(p_, w_, do_ref[...] * _dsilu(_conv(p_, w_, cb_ref[...], taps)), taps)
        dpm = dx.astype(MXU)
        dp_ref[...] = dpm
        g_ref[...] = lax.dot_general(dpm, h_ref[...].astype(MXU), (((0,), (0,)), ((), ())),
                                     preferred_element_type=f32).astype(g_ref.dtype)
        dw_ref[...] = dw
        db_ref[...] = db

    col = pl.BlockSpec((S, ct), lambda j: (0, j))
    return _call(
        body, [pre, dconv, cw, cb, h], name=name, grid=(C // ct,),
        in_specs=[col, col, pl.BlockSpec((CONV_A, ct), lambda j: (0, j)), pl.BlockSpec((1, ct), lambda j: (0, j)),
                  pl.BlockSpec((S, D), lambda j: (0, 0))],
        out_specs=[col, pl.BlockSpec((ct, D), lambda j: (j, 0)), pl.BlockSpec((CONV_A, ct), lambda j: (0, j)),
                   pl.BlockSpec((1, ct), lambda j: (0, j))],
        out_shape=[jax.ShapeDtypeStruct((S, C), MXU), jax.ShapeDtypeStruct((C, D), MXU),
                   jax.ShapeDtypeStruct((CONV_A, C), f32), jax.ShapeDtypeStruct((1, C), f32)],
        sem=("parallel",))


def _ffn_mid_bwd(name, dout, w_down, gate_pre, val, conv_w, conv_b, comm=None):
    ct = FFN_CT

    def body(do_ref, wd_ref, gp_ref, v_ref, w_ref, b_ref, dgv_ref, dw_ref, db_ref, dob_s):
        @pl.when(pl.program_id(0) == 0)
        def _():
            dob_s[...] = do_ref[...].astype(MXU)

        da = _mm(dob_s[...], wd_ref[...], _NT)
        gp, v_, w_ = gp_ref[...], v_ref[...], w_ref[...]
        taps = _taps(gp, CONV_F)
        gate = _conv(gp, w_, b_ref[...], taps)
        sg = _sigmoid(gate)
        dgp, dw, db = _conv_bwd(gp, w_, da * v_ * (sg * (1.0 + gate * (1.0 - sg))), taps)
        dgv_ref[0] = dgp.astype(dgv_ref.dtype)
        dgv_ref[1] = (da * (gate * sg)).astype(dgv_ref.dtype)
        dw_ref[...] = dw
        db_ref[...] = db

    col = pl.BlockSpec((S, ct), lambda j: (0, j))
    return _call(
        body, [dout, w_down, gate_pre, val, conv_w, conv_b], comm=comm, name=name, grid=(FFN // ct,),
        in_specs=[pl.BlockSpec((S, D), lambda j: (0, 0)), pl.BlockSpec((ct, D), lambda j: (j, 0)), col, col,
                  pl.BlockSpec((CONV_F, ct), lambda j: (0, j)), pl.BlockSpec((1, ct), lambda j: (0, j))],
        out_specs=[pl.BlockSpec((2, S, ct), lambda j: (0, 0, j)), pl.BlockSpec((CONV_F, ct), lambda j: (0, j)),
                   pl.BlockSpec((1, ct), lambda j: (0, j))],
        out_shape=[jax.ShapeDtypeStruct((2, S, FFN), MXU), jax.ShapeDtypeStruct((CONV_F, FFN), f32), jax.ShapeDtypeStruct((1, FFN), f32)],
        scratch=[pltpu.VMEM((S, D), MXU)], sem=("arbitrary",))


def _ffn_bwd(tag, layer, saved, norm_g, w_inT, conv_w, conv_b, w_down, dout, mid_comm=None, before=None):
    x, h, gate_pre, val, act = saved
    g_down = matmul(f"{tag}_wdown", act, dout, "tn", out_dtype=MXU)
    dgv, g_cw, g_cb = _ffn_mid_bwd(f"{tag}_dmid", dout, w_down, gate_pre, val, conv_w, conv_b, comm=mid_comm)
    g_inT = matmul_tn_stacked(f"{tag}_win", dgv, h, MXU)

    def din_fn(dg_, dv_, x_, do_, g_, wT, *wb):
        dx, dg = _rms_bwd(x_, g_, _mm(dg_, wT[:FFN], _NN) + _mm(dv_, wT[FFN:], _NN))
        dx = do_ + dx
        return (dx,) + tuple(_mm(dx, w_, _NT) for w_ in wb) + (dg,)

    rs = ReduceScatter(tag, (f"f_inT{layer}", f"f_down{layer}"), [g_inT, g_down])
    extra = [] if before is None else [before]
    *dxs, g_norm = rowwise(f"{tag}_din", din_fn, [(dgv, 0), (dgv, 1), x, dout], [norm_g, w_inT] + extra,
                           [(D, f32)] + [(w_.shape[0], f32) for w_ in extra], [(1, D)], comm=rs.sibling())
    small = {f"f_norm{layer}": g_norm, f"f_conv_w{layer}": g_cw, f"f_conv_b{layer}": g_cb}
    return (dxs[0] if before is None else tuple(dxs)), small, rs


def _land(W, keys, comm):
    for k, g in zip(keys, comm.results):
        W[k] = g.reshape(-1, g.shape[2])


def _local_step(x, pos, tgt, W, shards):
    G = {}
    gather = lambda *keys: gather_comm([shards[k] for k in keys])
    inv_freq = (ROPE_THETA ** (-jnp.arange(AH // 2, dtype=f32) / (AH // 2))).reshape(1, AH // 2)

    def in_fn(x_, g_, wT, wdtT):
        h_ = _rms_fwd(x_, g_).astype(MXU)
        return h_, _mm(h_, wT[slice(*IN_ROWS["z"])], _NT), _mm(h_, wdtT, _NT)

    h0, z, dt_pre = rowwise("a_in", in_fn, [x], [W["a_norm"], W["inT"], W["in_dtT"]], [(D, MXU), (INNER, f32), (LANES, f32)])
    pre, conv = {}, {}
    early = {"xs": ("a_out",), "B": (), "C": ()}
    for k in ("xs", "B", "C"):
        c = gather(*early[k]) if early[k] else None
        pre[k], conv[k] = _proj_conv(f"a_in_{k}", h0, W["inT"], IN_ROWS[k][0], W[f"cw_{k}"], W[f"cb_{k}"], comm=c)
        if c is not None:
            _land(W, early[k], c)
    c = gather("f_inT0", "f_down0")
    y, states = ssd_fwd(conv["xs"], conv["B"], conv["C"], dt_pre, W["dt_bias"], W["A_log"], W["D"], comm=c)
    _land(W, ("f_inT0", "f_down0"), c)

    def gate_norm(y_, z_, g_):
        yg = y_ * _silu(z_)
        w = INNER // GROUPS
        return (jnp.concatenate([_rms_fwd(yg[:, w * i:w * (i + 1)], g_[:, w * i:w * (i + 1)]) for i in range(GROUPS)], axis=1),)

    def out_fn(y_, z_, x_, g_, w_, gf_):
        (gn_,) = gate_norm(y_, z_, g_)
        gn_ = gn_.astype(MXU)
        x1_ = x_ + _mm(gn_, w_, _NN)
        return gn_, x1_, _rms_fwd(x1_, gf_)

    c = gather("f_down1")
    gn, x1, h1 = rowwise("a_out", out_fn, [y, z, x], [W["a_gnorm"], W["a_out"], W["f_norm0"]],
                         [(INNER, MXU), (D, f32), (D, MXU)], comm=c)
    _land(W, ("f_down1",), c)

    c = gather("w_kv", "w_q", "w_o")
    act0, ffn0 = _ffn_fwd("f0", x1, h1, W["f_inT0"], W["f_cw0"], W["f_cb0"], mid_comm=c)
    _land(W, ("w_kv", "w_q", "w_o"), c)
    kw = KVH * AH

    def qkv_fn(a_, x_, pos_, wd, gk, gb, wkv, bkv, wq, bq, if_, kn, qn):
        x2_ = x_ + _mm(a_, wd, _NN)
        kvn_, h2_ = _rms_fwd(x2_, gk).astype(MXU), _rms_fwd(x2_, gb).astype(MXU)
        kv_, qp_ = _mm(kvn_, wkv, _NN) + bkv, _mm(h2_, wq, _NN) + bq
        cos, sin = _rope_tables(pos_, if_)
        return (x2_, kvn_, h2_, kv_, qp_, _headnorm_rope_fwd(kv_[:, :kw], kn, cos, sin, KVH), kv_[:, kw:],
                _headnorm_rope_fwd(qp_, qn, cos, sin, QH))

    x2, kvn, h2, kv, q_pre, k_rot, v_val, q = rowwise(
        "f0_down_qkv", qkv_fn, [act0, x1, pos],
        [W["f_down0"], W["kv_norm"], W["b_norm"], W["w_kv"], W["b_kv"], W["w_q"], W["b_q"], inv_freq, W["k_norm"], W["q_norm"]],
        [(D, f32), (D, MXU), (D, MXU), (2 * kw, f32), (D, f32), (kw, f32), (kw, f32), (D, f32)])
    c = gather("f_inT1")
    att = attn_fwd(q, k_rot, v_val, W["sinks"], comm=c)
    _land(W, ("f_inT1",), c)
    def o_fn(att_, x_, w_, b_, gf_):
        x3_ = x_ + _mm(att_, w_, _NN) + b_
        return x3_, _rms_fwd(x3_, gf_)

    x3, h3 = rowwise("o_proj", o_fn, [att, x2], [W["w_o"], W["b_o"], W["f_norm1"]], [(D, f32), (D, MXU)])

    act1, ffn1 = _ffn_fwd("f1", x3, h3, W["f_inT1"], W["f_cw1"], W["f_cb1"])

    def loss_fn(a_, x_, t_, w_):
        diff = x_ + _mm(a_, w_, _NN) - t_
        rows = jnp.sum(diff * diff, axis=1, keepdims=True) * (0.5 / D)
        return diff * (1.0 / D), jnp.sum(rows, axis=0, keepdims=True)

    dx4, loss = rowwise("f1_down_loss", loss_fn, [act1, x3, tgt], [W["f_down1"]], [(D, f32)], [(1, 1)])

    (dx3, datt), g, rs_f1 = _ffn_bwd("f1", 1, ffn1, W["f_norm1"], W["f_inT1"], W["f_cw1"], W["f_cb1"], W["f_down1"], dx4,
                                     before=W["w_o"])
    G.update(g)

    g_wo = matmul("o_wproj", att, dx3, "tn", out_dtype=MXU)
    dq, dkp, dkc, dvp, dvc, G["sinks"] = attn_bwd(q, k_rot, v_val, W["sinks"], datt, comm=rs_f1.chips())

    def q_bwd(q_, pos_, dq_, dx_, if_, g_):
        cos, sin = _rope_tables(pos_, if_)
        dqp, dg = _headnorm_rope_bwd(q_, g_, cos, sin, dq_, QH)
        return dqp, dg, jnp.sum(dqp, axis=0, keepdims=True), jnp.sum(dx_, axis=0, keepdims=True)

    dq_pre, G["q_norm"], G["b_q"], G["b_o"] = rowwise("q_drope", q_bwd, [q_pre, pos, dq, dx3], [inv_freq, W["q_norm"]],
                                                      [(D, MXU)], [(1, AH), (1, D), (1, D)])
    g_wq = matmul("q_wproj", h2, dq_pre, "tn", out_dtype=MXU)
    dkv, G["k_norm"], G["b_kv"] = kv_bwd(kv, pos, inv_freq, W["k_norm"], dkp, dkc, dvp, dvc)
    g_wkv = matmul("kv_wproj", kvn, dkv, "tn", out_dtype=MXU)
    rs_att = ReduceScatter("att", ("w_kv", "w_q", "w_o"), [g_wkv, g_wq, g_wo])

    def x2_bwd(x_, dq_, dkv_, dx_, gb_, gk_, wq, wkv):
        d1, dgb = _rms_bwd(x_, gb_, _mm(dq_, wq, _NT))
        d2, dgk = _rms_bwd(x_, gk_, _mm(dkv_, wkv, _NT))
        return dx_ + d1 + d2, dgb, dgk

    dx2, G["b_norm"], G["kv_norm"] = rowwise("qkv_dproj", x2_bwd, [x2, dq_pre, dkv, dx3],
                                             [W["b_norm"], W["kv_norm"], W["w_q"], W["w_kv"]],
                                             [(D, f32)], [(1, D), (1, D)], comm=rs_att.sibling())

    dx1, g, rs_f0 = _ffn_bwd("f0", 0, ffn0, W["f_norm0"], W["f_inT0"], W["f_cw0"], W["f_cb0"], W["f_down0"], dx2,
                             mid_comm=rs_att.chips())
    G.update(g)

    rs_out = ReduceScatter("a_out", ("a_out",), [matmul("a_wout", gn, dx1, "tn", out_dtype=MXU)])

    def gate_norm_bwd(y_, z_, dx_, g_, w_out):
        dgn_ = _mm(dx_, w_out, _NT)
        w = INNER // GROUPS
        sg = _sigmoid(z_)
        sz = z_ * sg
        yg = y_ * sz
        parts, dgs = [], []
        for i in range(GROUPS):
            dseg, dg = _rms_bwd(yg[:, w * i:w * (i + 1)], g_[:, w * i:w * (i + 1)], dgn_[:, w * i:w * (i + 1)])
            parts.append(dseg)
            dgs.append(dg)
        dyg = jnp.concatenate(parts, axis=1)
        return dyg * sz, dyg * y_ * (sg * (1.0 + z_ * (1.0 - sg))), jnp.concatenate(dgs, axis=1)

    dy, dz, G["a_gnorm"] = rowwise("a_dout", gate_norm_bwd, [y, z, dx1], [W["a_gnorm"], W["a_out"]],
                                   [(INNER, f32), (INNER, MXU)], [(1, INNER)], comm=rs_out.sibling())
    dconv = {}
    dconv["xs"], dconv["B"], dconv["C"], ddt_pre, G["dt_bias"], G["A_log"], G["D"] = ssd_bwd(
        conv["xs"], conv["B"], conv["C"], dt_pre, W["dt_bias"], W["A_log"], W["D"], states, dy,
        comm=merge_comms([rs_f0.chips(), rs_out.chips()]))

    g_in, dpre = [matmul("a_win_z", dz, h0, "tn", out_dtype=MXU)], {}
    for k in ("xs", "B", "C"):
        dpre[k], g_k, G[f"cw_{k}"], G[f"cb_{k}"] = _dconv_wgrad(f"a_dconv_{k}", pre[k], dconv[k], W[f"cw_{k}"], W[f"cb_{k}"], h0)
        g_in.append(g_k)
    g_in.append(matmul("a_win_dt", ddt_pre, h0, "tn", out_dtype=MXU)[:HEADS])
    rs_in = ReduceScatter("a_in", ("inT",), [jnp.concatenate(g_in, axis=0)])
    run_comm("rs_in_sibling", rs_in.sibling())

    def x0_bwd(dz_, dxs_, db_, dc_, ddt_, x_, do_, g_, wT, wdtT):
        parts = zip((dz_, dxs_, db_, dc_), IN_ROWS.values())
        dh = sum(_mm(d_, wT[a:b], _NN) for d_, (a, b) in parts) + _mm(ddt_, wdtT, _NN)
        dx, dg = _rms_bwd(x_, g_, dh)
        return do_ + dx, dg

    dx, G["a_norm"] = rowwise("a_din", x0_bwd, [dz, dpre["xs"], dpre["B"], dpre["C"], ddt_pre, x, dx1],
                              [W["a_norm"], W["inT"], W["in_dtT"]], [(D, f32)], [(1, D)], comm=rs_in.chips())
    return loss, dx, G, [rs_f1, rs_att, rs_f0, rs_out, rs_in]


ROW_KEYS = ("inT", "a_out", "f_inT0", "f_down0", "w_kv", "w_q", "w_o", "f_inT1", "f_down1")


def _row_blocks(src):
    return {"inT": src["a_in_proj"][0].T, "a_out": src["a_out_proj"][0], "w_kv": src["w_kv"], "w_q": src["w_q"][0],
            "w_o": src["w_o"][0], "f_inT0": src["f_w_in"][0].T, "f_inT1": src["f_w_in"][1].T,
            "f_down0": src["f_w_down"][0], "f_down1": src["f_w_down"][1]}


def _from_row_blocks(rb):
    out = {"a_out_proj": rb["a_out"][None], "w_kv": rb["w_kv"], "w_q": rb["w_q"][None], "w_o": rb["w_o"][None]}
    if "inT" in rb:
        out["a_in_proj"] = rb["inT"].T[None]
    if "f_inT0" in rb:
        out["f_w_in"] = jnp.stack([rb["f_inT0"].T, rb["f_inT1"].T])
        out["f_w_down"] = jnp.stack([rb["f_down0"], rb["f_down1"]])
    return out


SMALL_SHARDED = ("a_norm", "a_conv_w", "a_conv_b", "a_gnorm", "f_conv_w")
REPLICATED = ("a_dt_bias", "a_A_log", "a_D", "kv_norm", "b_kv", "k_norm", "b_norm", "b_q", "q_norm", "sinks", "b_o",
              "f_norm", "f_conv_b")
ORDER = ("a_norm", "a_in_proj", "a_conv_w", "a_conv_b", "a_dt_bias", "a_A_log", "a_D", "a_gnorm", "a_out_proj", "kv_norm",
         "w_kv", "b_kv", "k_norm", "b_norm", "w_q", "b_q", "q_norm", "sinks", "w_o", "b_o", "f_norm", "f_w_in",
         "f_conv_w", "f_conv_b", "f_w_down")


def _gathered_to_whole(name, g):
    if name == "a_conv_w":
        return jnp.moveaxis(g[:, 0], 0, 1).reshape(g.shape[2], -1)
    if name in ("a_norm", "a_conv_b", "a_gnorm"):
        return g[:, 0].reshape(1, -1)
    if name == "f_conv_w":
        return jnp.moveaxis(g, 0, 2).reshape(g.shape[1], g.shape[2], -1)
    raise ValueError(name)


def _whole_to_shards(name, w):
    if name == "a_conv_w":
        return jnp.moveaxis(w.reshape(w.shape[0], N_DEV, -1), 1, 0)[:, None]
    if name in ("a_norm", "a_conv_b", "a_gnorm"):
        return w.reshape(N_DEV, 1, -1)
    if name == "f_conv_w":
        return jnp.moveaxis(w.reshape(w.shape[0], w.shape[1], N_DEV, -1), 2, 0)
    raise ValueError(name)


def _small_weights(whole):
    W = {}
    cw, cb = whole["a_conv_w"], whole["a_conv_b"]
    o = 0
    for k, n in (("xs", INNER), ("B", GROUPS * NSTATE), ("C", GROUPS * NSTATE)):
        W[f"cw_{k}"], W[f"cb_{k}"] = cw[:, o:o + n], cb[:, o:o + n]
        o += n
    W["a_norm"], W["a_gnorm"] = whole["a_norm"], whole["a_gnorm"]
    W["dt_bias"], W["A_log"], W["D"] = (_pad_lanes(whole[k]) for k in ("a_dt_bias", "a_A_log", "a_D"))
    W["kv_norm"], W["b_kv"], W["k_norm"] = whole["kv_norm"].reshape(1, -1), whole["b_kv"].reshape(1, -1), whole["k_norm"].reshape(1, -1)
    for k in ("b_norm", "b_q", "q_norm", "sinks", "b_o"):
        W[k] = whole[k]
    for i in range(2):
        W[f"f_norm{i}"] = whole["f_norm"][i:i + 1]
        W[f"f_cw{i}"], W[f"f_cb{i}"] = whole["f_conv_w"][i], whole["f_conv_b"][i:i + 1]
    return W


def _small_grads(G, shapes):
    nh = HEADS
    out = {
        "a_conv_w": jnp.concatenate([G["cw_xs"], G["cw_B"], G["cw_C"]], axis=1),
        "a_conv_b": jnp.concatenate([G["cb_xs"], G["cb_B"], G["cb_C"]], axis=1),
        "a_norm": G["a_norm"], "a_gnorm": G["a_gnorm"],
        "a_dt_bias": G["dt_bias"][:, :nh], "a_A_log": G["A_log"][:, :nh], "a_D": G["D"][:, :nh],
        "kv_norm": G["kv_norm"], "b_kv": G["b_kv"], "k_norm": G["k_norm"], "b_norm": G["b_norm"],
        "b_q": G["b_q"], "q_norm": G["q_norm"], "sinks": G["sinks"], "b_o": G["b_o"],
        "f_norm": jnp.concatenate([G["f_norm0"], G["f_norm1"]], axis=0),
        "f_conv_w": jnp.stack([G["f_conv_w0"], G["f_conv_w1"]]),
        "f_conv_b": jnp.concatenate([G["f_conv_b0"], G["f_conv_b1"]], axis=0),
    }
    return {k: val.reshape(shapes[k]) if k in shapes else val for k, val in out.items()}


def kernel(x, positions, a_norm, a_in_proj, a_conv_w, a_conv_b, a_dt_bias, a_A_log, a_D, a_gnorm, a_out_proj, kv_norm, w_kv, b_kv, k_norm, b_norm, w_q, b_q, q_norm, sinks, w_o, b_o, f_norm, f_w_in, f_conv_w, f_conv_b, f_w_down, loss_target, m_a_norm, m_a_in_proj, m_a_conv_w, m_a_conv_b, m_a_dt_bias, m_a_A_log, m_a_D, m_a_gnorm, m_a_out_proj, m_kv_norm, m_w_kv, m_b_kv, m_k_norm, m_b_norm, m_w_q, m_b_q, m_q_norm, m_sinks, m_w_o, m_b_o, m_f_norm, m_f_w_in, m_f_conv_w, m_f_conv_b, m_f_w_down, v_a_norm, v_a_in_proj, v_a_conv_w, v_a_conv_b, v_a_dt_bias, v_a_A_log, v_a_D, v_a_gnorm, v_a_out_proj, v_kv_norm, v_w_kv, v_b_kv, v_k_norm, v_b_norm, v_w_q, v_b_q, v_q_norm, v_sinks, v_w_o, v_b_o, v_f_norm, v_f_w_in, v_f_conv_w, v_f_conv_b, v_f_w_down):
    given = dict(locals())
    w_in = {n: given[n] for n in ORDER}
    m_in = {n: given["m_" + n] for n in ORDER}
    v_in = {n: given["v_" + n] for n in ORDER}
    dev = 4 * lax.axis_index("x") + 2 * lax.axis_index("y") + lax.axis_index("c")

    w2, m2, v2 = _row_blocks(w_in), _row_blocks(m_in), _row_blocks(v_in)
    small_pack = _pack([w_in[n] for n in SMALL_SHARDED], 8, f32)
    shards = {k: w2[k].astype(MXU) for k in ROW_KEYS}
    in_all, small_all = run_comm("ag_head", gather_comm([shards["inT"], small_pack]))
    whole = {n: w_in[n] for n in REPLICATED}
    for n, g in zip(SMALL_SHARDED, _unpack(small_all, [w_in[n].shape for n in SMALL_SHARDED], lead=(N_DEV,))):
        whole[n] = _gathered_to_whole(n, g)
    W = _small_weights(whole)
    W["inT"] = in_all.reshape(-1, D)
    W["in_dtT"] = jnp.pad(W["inT"][IN_COLS - HEADS:], ((0, LANES - HEADS), (0, 0)))

    loss, dx, G, scatters = _local_step(x[0], positions.reshape(S, 1).astype(f32), loss_target[0], W, shards)
    grads = _small_grads(G, {n: whole[n].shape for n in REPLICATED})

    small_names = SMALL_SHARDED + REPLICATED
    small_part = _pack([grads[n] for n in small_names] + [loss], 8, f32)
    small_gather = gather_comm([small_part])
    run_comm("ag_small_grads", small_gather)
    parts = {}
    for rs in scatters:
        parts.update(rs.parts())

    single = tuple(k for k in ROW_KEYS if not k.startswith("f_") and k != "inT")
    stepped = {k: final_adamw(f"adamw_{k}", *parts[k], w2[k], m2[k], v2[k]) for k in single}
    g_out, delta, new_m, new_v = (_from_row_blocks({k: stepped[k][i] for k in single}) for i in range(4))
    n = "a_in_proj"
    g_out[n] = final_sum("rs_final_sum_inT", *parts["inT"]).T[None]
    delta[n], new_m[n], new_v[n] = (a[None] for a in adamw("adamw_a_in_proj", w_in[n][0], g_out[n][0], m_in[n][0], v_in[n][0]))
    for n, key, lay in (("f_w_in", "f_inT", lambda a: jnp.swapaxes(a, 1, 2)), ("f_w_down", "f_down", lambda a: a)):
        res = final_adamw_layers(f"adamw_{n}", *parts[key + "0"], *parts[key + "1"], lay(w_in[n]), lay(m_in[n]), lay(v_in[n]))
        g_out[n], delta[n], new_m[n], new_v[n] = (lay(a) for a in res)
    *small_sums, loss_all = _unpack(sum_devices(small_gather.results[0]), [grads[n].shape for n in small_names] + [(1, 1)])
    for n, g in zip(small_names, small_sums):
        if n in SMALL_SHARDED:
            g_out[n] = lax.dynamic_index_in_dim(_whole_to_shards(n, g), dev, axis=0, keepdims=False)
        else:
            g_out[n] = g.reshape(w_in[n].shape)

    packs = [_pack([src[n] for n in small_names], 8, f32) for src in (w_in, g_out, m_in, v_in)]
    outs = adamw("adamw_small", *packs)
    for dst, buf in zip((delta, new_m, new_v), outs):
        for n, a in zip(small_names, _unpack(buf, [w_in[n].shape for n in small_names])):
            dst[n] = a

    return (loss_all[0, 0], dx[None], *[g_out[n] for n in ORDER], *[delta[n] for n in ORDER],
            *[new_m[n] for n in ORDER], *[new_v[n] for n in ORDER])
```

```python
import functools
import math

import jax
import jax.numpy as jnp
from jax import lax
from jax.experimental import pallas as pl
from jax.experimental.pallas import tpu as pltpu

f32 = jnp.float32
bf16 = jnp.bfloat16
MXU = bf16

N_DEV = 8
S = 2048
D = 1024
EPS = 1e-5
INNER = 2048
HEADS = 32
HP = 64
GROUPS = 8
HPG = HEADS // GROUPS
NSTATE = 128
CONV_A = 4
CHUNK = 256
NCHUNK = S // CHUNK
AH = 64
QH = 16
KVH = 4
QPK = QH // KVH
WIN = 128
NBLK = S // WIN
ROPE_THETA = 10000.0
FFN = 2816
CONV_F = 3
LANES = 128
PACK_W = 1024
VMEM_LIMIT = 56 * 1024 * 1024

ADAM_LR, ADAM_B1, ADAM_B2, ADAM_EPS, ADAM_WD, ADAM_STEP = 0.001, 0.9, 0.999, 1e-08, 0.01, 10

MESH = pl.DeviceIdType.MESH


def _cparams(sem=None):
    return pltpu.CompilerParams(dimension_semantics=sem, vmem_limit_bytes=VMEM_LIMIT)


def _pick(n, cands):
    for c in cands:
        if n % c == 0:
            return c
    return n


class Comm:
    def __init__(self, ins, out_shapes, sems, start, finish):
        self.ins, self.out_shapes, self.sems, self.start, self.finish = list(ins), list(out_shapes), list(sems), start, finish
        self.results, self.children = None, ()

    def set_results(self, res):
        self.results, o = list(res), 0
        for ch in self.children:
            ch.set_results(res[o:o + len(ch.out_shapes)])
            o += len(ch.out_shapes)


def merge_comms(comms):
    def each(fn_name, ins, outs, sems):
        i = o = s = 0
        for c in comms:
            getattr(c, fn_name)(ins[i:i + len(c.ins)], outs[o:o + len(c.out_shapes)], sems[s:s + len(c.sems)])
            i, o, s = i + len(c.ins), o + len(c.out_shapes), s + len(c.sems)

    merged = Comm([a for c in comms for a in c.ins], [a for c in comms for a in c.out_shapes], [a for c in comms for a in c.sems],
                  functools.partial(each, "start"), functools.partial(each, "finish"))
    merged.children = tuple(comms)
    return merged


def _call(body, args, *, name, grid, in_specs, out_specs, out_shape, scratch=(), sem=None, comm=None):
    if comm is None:
        return pl.pallas_call(body, name=name, grid=grid, in_specs=list(in_specs), out_specs=list(out_specs),
                              out_shape=list(out_shape), scratch_shapes=list(scratch), compiler_params=_cparams(sem))(*args)
    n_in, n_out, n_scr, c_in, c_out = len(in_specs), len(out_shape), len(scratch), len(comm.ins), len(comm.out_shapes)
    any_spec = pl.BlockSpec(memory_space=pl.ANY)

    def outer(*refs):
        ins, c_ins = refs[:n_in], refs[n_in:n_in + c_in]
        o = n_in + c_in
        outs, c_outs = refs[o:o + n_out], refs[o + n_out:o + n_out + c_out]
        o += n_out + c_out
        scr, c_sems = refs[o:o + n_scr], refs[o + n_scr:]
        ids = [pl.program_id(i) for i in range(len(grid))]
        first = functools.reduce(jnp.logical_and, [i == 0 for i in ids])
        last = functools.reduce(jnp.logical_and, [i == g - 1 for i, g in zip(ids, grid)])

        @pl.when(first)
        def _():
            comm.start(c_ins, c_outs, c_sems)

        body(*ins, *outs, *scr)

        @pl.when(last)
        def _():
            comm.finish(c_ins, c_outs, c_sems)

    res = pl.pallas_call(
        outer, name=name, grid=grid, in_specs=list(in_specs) + [any_spec] * c_in,
        out_specs=list(out_specs) + [any_spec] * c_out, out_shape=list(out_shape) + comm.out_shapes,
        scratch_shapes=list(scratch) + comm.sems, compiler_params=_cparams(("arbitrary",) * len(grid)),
    )(*args, *comm.ins)
    comm.set_results(res[n_out:])
    return res[:n_out]


def matmul(name, a, b, mode, out_dtype=f32, bias=None, residual=None):
    if mode == "nn":
        (M, K), (K2, N) = a.shape, b.shape
    elif mode == "nt":
        (M, K), (N, K2) = a.shape, b.shape
    else:
        (K, M), (K2, N) = a.shape, b.shape
    assert K == K2, (name, a.shape, b.shape)
    if mode == "tn":
        tm, tn = M, _pick(N, (512, 256, 128) if M <= 1024 else (256, 128))
        a_spec = pl.BlockSpec((K, M), lambda j: (0, 0))
        b_spec = pl.BlockSpec((K, tn), lambda j: (0, j))
        dims = (((0,), (0,)), ((), ()))
        grid, o_map, row_map = (N // tn,), (lambda j: (0, j)), (lambda j: (0, j))
    else:
        tm, tn = (256 if N >= 2048 else 512), N
        a_spec = pl.BlockSpec((tm, K), lambda i: (i, 0))
        b_spec = pl.BlockSpec(b.shape, lambda i: (0, 0))
        dims = (((1,), (0,)), ((), ())) if mode == "nn" else (((1,), (1,)), ((), ()))
        grid, o_map, row_map = (M // tm,), (lambda i: (i, 0)), (lambda i: (0, 0))
    ins, in_specs = [a, b], [a_spec, b_spec]
    if bias is not None:
        ins.append(bias)
        in_specs.append(pl.BlockSpec((1, tn), row_map))
    if residual is not None:
        ins.append(residual)
        in_specs.append(pl.BlockSpec((tm, tn), o_map))
    has_bias, has_res = bias is not None, residual is not None

    def body(a_ref, b_ref, *rest):
        rest = list(rest)
        bias_ref = rest.pop(0) if has_bias else None
        res_ref = rest.pop(0) if has_res else None
        (o_ref,) = rest
        r = lax.dot_general(a_ref[...].astype(MXU), b_ref[...].astype(MXU), dims, preferred_element_type=f32)
        if has_bias:
            r = r + bias_ref[...]
        if has_res:
            r = r + res_ref[...]
        o_ref[...] = r.astype(out_dtype)

    return pl.pallas_call(
        body, name=name, grid=grid, in_specs=in_specs,
        out_specs=pl.BlockSpec((tm, tn), o_map),
        out_shape=jax.ShapeDtypeStruct((M, N), out_dtype),
        compiler_params=_cparams(("parallel",)),
    )(*ins)


def matmul_tn_stacked(name, a, b, out_dtype):
    R, K, M = a.shape
    N = b.shape[1]
    tn = _pick(N, (256, 128))

    def body(a_ref, b_ref, o_ref):
        o_ref[0] = lax.dot_general(a_ref[0].astype(MXU), b_ref[...].astype(MXU), (((0,), (0,)), ((), ())),
                                   preferred_element_type=f32).astype(out_dtype)

    out = pl.pallas_call(
        body, name=name, grid=(R, N // tn),
        in_specs=[pl.BlockSpec((1, K, M), lambda r, j: (r, 0, 0)), pl.BlockSpec((K, tn), lambda r, j: (0, j))],
        out_specs=pl.BlockSpec((1, M, tn), lambda r, j: (r, 0, j)),
        out_shape=jax.ShapeDtypeStruct((R, M, N), out_dtype),
        compiler_params=_cparams(("parallel", "parallel")),
    )(a, b)
    return out.reshape(R * M, N)


def rowwise(name, fn, rows, pars, outs, accs=(), tile=256, comm=None):
    n_in, n_out = len(rows) + len(pars), len(outs)
    in_specs = [pl.BlockSpec((None, tile, r[0].shape[2]), functools.partial(lambda i, lead: (lead, i, 0), lead=r[1]))
                if isinstance(r, tuple) else pl.BlockSpec((tile, r.shape[1]), lambda i: (i, 0)) for r in rows]
    rows = [r[0] if isinstance(r, tuple) else r for r in rows]
    in_specs += [pl.BlockSpec(p.shape, lambda i: (0, 0)) for p in pars]
    out_specs = [pl.BlockSpec((tile, c), lambda i: (i, 0)) for c, _ in outs]
    out_specs += [pl.BlockSpec(shp, lambda i: (0, 0)) for shp in accs]
    out_shape = [jax.ShapeDtypeStruct((S, c), dt) for c, dt in outs]
    out_shape += [jax.ShapeDtypeStruct(shp, f32) for shp in accs]

    def body(*refs):
        res = fn(*[r[...] for r in refs[:n_in]])
        o_refs = refs[n_in:n_in + n_out]
        a_refs = refs[n_in + n_out:]
        for ref, val in zip(o_refs, res[:n_out]):
            ref[...] = val.astype(ref.dtype)
        if a_refs:
            @pl.when(pl.program_id(0) == 0)
            def _():
                for ref in a_refs:
                    ref[...] = jnp.zeros_like(ref)
            for ref, val in zip(a_refs, res[n_out:]):
                ref[...] += val

    return _call(body, [*rows, *pars], name=name, grid=(S // tile,), in_specs=in_specs, out_specs=out_specs,
                 out_shape=out_shape, sem=("arbitrary",) if accs else ("parallel",), comm=comm)


def _sigmoid(x):
    return 0.5 * jnp.tanh(0.5 * x) + 0.5


def _silu(x):
    return x * _sigmoid(x)


def _dsilu(x):
    sg = _sigmoid(x)
    return sg * (1.0 + x * (1.0 - sg))


def _softplus(x):
    return jnp.maximum(x, 0.0) + jnp.log(1.0 + jnp.exp(-jnp.abs(x)))


def _rms_fwd(x, g):
    r = lax.rsqrt(jnp.mean(x * x, axis=-1, keepdims=True) + EPS)
    return x * r * g


def _rms_bwd(x, g, dh):
    r = lax.rsqrt(jnp.mean(x * x, axis=-1, keepdims=True) + EPS)
    xh = x * r
    dxh = dh * g
    dx = r * (dxh - xh * jnp.mean(dxh * xh, axis=-1, keepdims=True))
    return dx, jnp.sum(dh * xh, axis=0, keepdims=True)


def _taps(x, width):
    row = lax.broadcasted_iota(jnp.int32, (8, x.shape[1]), 0)

    def shifted(s):
        r = pltpu.roll(x, s, 0)
        return jnp.concatenate([jnp.where(row >= s, r[:8], 0.0), r[8:]], axis=0)

    return [shifted(s) for s in range(width - 1, 0, -1)] + [x]


def _conv(x, w, b, taps=None):
    width = w.shape[0]
    taps = _taps(x, width) if taps is None else taps
    out = b + w[0:1, :] * taps[0]
    for k in range(1, width):
        out = out + w[k:k + 1, :] * taps[k]
    return out


def _conv_bwd(x, w, dc, taps=None):
    width, n = w.shape[0], x.shape[0]
    taps = _taps(x, width) if taps is None else taps
    row = lax.broadcasted_iota(jnp.int32, (8, x.shape[1]), 0)
    dx = w[width - 1:width, :] * dc
    for k in range(width - 1):
        s = width - 1 - k
        r = pltpu.roll(dc, n - s, 0)
        dx = dx + w[k:k + 1, :] * jnp.concatenate([r[:n - 8], jnp.where(row < 8 - s, r[n - 8:], 0.0)], axis=0)
    dw = jnp.concatenate([jnp.sum(dc * t, axis=0, keepdims=True) for t in taps], axis=0)
    return dx, dw, jnp.sum(dc, axis=0, keepdims=True)


def _rope_tables(pos, inv_freq):
    ang = pos * inv_freq
    return jnp.cos(ang), jnp.sin(ang)


def _split2(v):
    hi = v.astype(bf16)
    return hi, (v - hi.astype(f32)).astype(bf16)


def _head_maps(width):
    shift = AH.bit_length() - 1
    to_head = (lax.broadcasted_iota(jnp.int32, (width, LANES), 0) >> shift) == lax.broadcasted_iota(jnp.int32, (width, LANES), 1)
    from_head = lax.broadcasted_iota(jnp.int32, (LANES, width), 0) == (lax.broadcasted_iota(jnp.int32, (LANES, width), 1) >> shift)
    return to_head.astype(bf16), from_head.astype(bf16)


def _head_sums(v, to_head):
    hi, lo = _split2(v)
    return jnp.dot(hi, to_head, preferred_element_type=f32) + jnp.dot(lo, to_head, preferred_element_type=f32)


def _head_spread(s, from_head):
    hi, lo = _split2(s)
    return jnp.dot(hi, from_head, preferred_element_type=f32) + jnp.dot(lo, from_head, preferred_element_type=f32)


def _rope_full(cos, sin, width):
    half = AH // 2
    pad = jnp.zeros((cos.shape[0], LANES - half), f32)
    r = lax.broadcasted_iota(jnp.int32, (LANES, width), 0)
    lane = lax.broadcasted_iota(jnp.int32, (LANES, width), 1)
    spread = ((lane & (half - 1)) == r).astype(bf16)
    full = lambda t: _head_spread(jnp.concatenate([t, pad], axis=1), spread)
    first = (lax.broadcasted_iota(jnp.int32, (1, width), 1) & (AH - 1)) < half
    sin_f = full(sin)
    return full(cos), jnp.where(first, -sin_f, sin_f), first


def _swap_halves(v, first):
    half, width = AH // 2, v.shape[1]
    return jnp.where(first, pltpu.roll(v, width - half, 1), pltpu.roll(v, half, 1))


def _headnorm_rope_fwd(x, g, cos, sin, heads):
    to_head, from_head = _head_maps(heads * AH)
    cos_f, sin_s, first = _rope_full(cos, sin, heads * AH)
    r = _head_spread(lax.rsqrt(_head_sums(x * x, to_head) * (1.0 / AH) + EPS), from_head)
    n = x * r * jnp.tile(g, (1, heads))
    return n * cos_f + _swap_halves(n, first) * sin_s


def _headnorm_rope_bwd(x, g, cos, sin, dout, heads):
    width = heads * AH
    to_head, from_head = _head_maps(width)
    cos_f, sin_s, first = _rope_full(cos, sin, width)
    r = _head_spread(lax.rsqrt(_head_sums(x * x, to_head) * (1.0 / AH) + EPS), from_head)
    xh = x * r
    dn = dout * cos_f - _swap_halves(dout, first) * sin_s
    dxh = dn * jnp.tile(g, (1, heads))
    m = _head_spread(_head_sums(dxh * xh, to_head) * (1.0 / AH), from_head)
    dx = r * (dxh - xh * m)
    dg_lanes = jnp.sum(dn * xh, axis=0, keepdims=True)
    fold = ((lax.broadcasted_iota(jnp.int32, (width, LANES), 0) & (AH - 1))
            == lax.broadcasted_iota(jnp.int32, (width, LANES), 1)).astype(f32)
    dg = jnp.dot(jnp.broadcast_to(dg_lanes, (8, width)), fold, precision=lax.Precision.HIGHEST, preferred_element_type=f32)
    return dx, dg[0:1, :AH]


def _ssd_prep(dt_pre, dt_bias, a_log, dt_s, acum_s, acumT_s):
    dt = _softplus(dt_pre + dt_bias)
    a = dt * (-jnp.exp(a_log))
    row = lax.broadcasted_iota(jnp.int32, (CHUNK, CHUNK), 0)
    col = lax.broadcasted_iota(jnp.int32, (CHUNK, CHUNK), 1)
    dt_s[...] = dt
    acum_s[...] = jnp.dot((col <= row).astype(f32), a, precision=lax.Precision.HIGHEST, preferred_element_type=f32)
    acumT_s[...] = lax.dot_general(a, (row <= col).astype(f32), (((0,), (0,)), ((), ())),
                                   precision=lax.Precision.HIGHEST, preferred_element_type=f32)


def _head_cols(h, dt_s, acum_s, acumT_s):
    lane = lax.broadcasted_iota(jnp.int32, (1, LANES), 1)
    oh_l = (lane == h).astype(f32)
    sub = lax.broadcasted_iota(jnp.int32, (LANES, 1), 0)
    oh_s = (sub == h).astype(f32)
    dt_h = jnp.sum(dt_s[...] * oh_l, axis=1, keepdims=True)
    ac_h = jnp.sum(acum_s[...] * oh_l, axis=1, keepdims=True)
    acr_h = jnp.sum(acumT_s[...] * oh_s, axis=0, keepdims=True)
    return oh_l, dt_h, ac_h, acr_h


def ssd_fwd(xs, Bm, Cm, dt_pre, dt_bias, a_log, d_skip, comm=None):
    def body(xs_ref, b_ref, c_ref, dtp_ref, bias_ref, alog_ref, d_ref, y_ref, st_ref, state, dt_s, acum_s, acumT_s):
        c, g = pl.program_id(0), pl.program_id(1)

        @pl.when(g == 0)
        def _():
            _ssd_prep(dtp_ref[...], bias_ref[...], alog_ref[...], dt_s, acum_s, acumT_s)

        row = lax.broadcasted_iota(jnp.int32, (CHUNK, CHUNK), 0)
        col = lax.broadcasted_iota(jnp.int32, (CHUNK, CHUNK), 1)
        causal = col <= row
        Bb, Cb = b_ref[...], c_ref[...]
        cb = lax.dot_general(Cb.astype(MXU), Bb.astype(MXU), (((1,), (1,)), ((), ())), preferred_element_type=f32)
        xs_blk = xs_ref[...]

        @pl.when(c == 0)
        def _():
            for j in range(HPG):
                state[g * HPG + j] = jnp.zeros((NSTATE, HP), f32)

        prevs = [state[g * HPG + j] for j in range(HPG)]
        y_off_all = jnp.dot(Cb.astype(MXU), jnp.concatenate(prevs, axis=1).astype(MXU), preferred_element_type=f32)
        ys, xds, e_ends = [], [], []
        for j in range(HPG):
            oh_l, dt_h, ac_h, acr_h = _head_cols(g * HPG + j, dt_s, acum_s, acumT_s)
            decay = jnp.exp(jnp.where(causal, ac_h - acr_h, -1e30))
            w = (cb * decay).astype(MXU)
            xs_h = xs_blk[:, HP * j:HP * (j + 1)]
            xd = xs_h * dt_h
            y_diag = jnp.dot(w, xd.astype(MXU), preferred_element_type=f32)
            y_off = y_off_all[:, HP * j:HP * (j + 1)] * jnp.exp(ac_h)
            d_h = jnp.sum(d_ref[...] * oh_l, axis=1, keepdims=True)
            ys.append(y_diag + y_off + xs_h * d_h)
            a_end = ac_h[CHUNK - 1:CHUNK, :]
            xds.append(xd * jnp.exp(a_end - ac_h))
            e_ends.append(jnp.exp(a_end))
        s_c = lax.dot_general(Bb.astype(MXU), jnp.concatenate(xds, axis=1).astype(MXU), (((0,), (0,)), ((), ())),
                              preferred_element_type=f32)
        for j in range(HPG):
            st_ref[0, j] = prevs[j]
            state[g * HPG + j] = prevs[j] * e_ends[j] + s_c[:, HP * j:HP * (j + 1)]
        y_ref[...] = jnp.concatenate(ys, axis=1)

    par = pl.BlockSpec((1, LANES), lambda c, g: (0, 0))
    return _call(
        body, [xs, Bm, Cm, dt_pre, dt_bias, a_log, d_skip], comm=comm, name="ssd_fwd", grid=(NCHUNK, GROUPS),
        in_specs=[pl.BlockSpec((CHUNK, HPG * HP), lambda c, g: (c, g)),
                  pl.BlockSpec((CHUNK, NSTATE), lambda c, g: (c, g)),
                  pl.BlockSpec((CHUNK, NSTATE), lambda c, g: (c, g)),
                  pl.BlockSpec((CHUNK, LANES), lambda c, g: (c, 0)), par, par, par],
        out_specs=[pl.BlockSpec((CHUNK, HPG * HP), lambda c, g: (c, g)),
                   pl.BlockSpec((1, HPG, NSTATE, HP), lambda c, g: (c, g, 0, 0))],
        out_shape=[jax.ShapeDtypeStruct((S, INNER), f32), jax.ShapeDtypeStruct((NCHUNK, HEADS, NSTATE, HP), f32)],
        scratch=[pltpu.VMEM((HEADS, NSTATE, HP), f32), pltpu.VMEM((CHUNK, LANES), f32),
                 pltpu.VMEM((CHUNK, LANES), f32), pltpu.VMEM((LANES, CHUNK), f32)],
        sem=("arbitrary", "arbitrary"))


def ssd_bwd(xs, Bm, Cm, dt_pre, dt_bias, a_log, d_skip, states, dy, comm=None):
    rev = lambda c: NCHUNK - 1 - c

    def body(xs_ref, b_ref, c_ref, dtp_ref, bias_ref, alog_ref, d_ref, st_ref, dy_ref,
             dxs_ref, db_ref, dc_ref, ddt_ref, dbias_ref, dalog_ref, dd_ref,
             dstate, dt_s, acum_s, acumT_s, dacum_s, ddt_s, da_s):
        c, g = pl.program_id(0), pl.program_id(1)

        @pl.when(g == 0)
        def _():
            _ssd_prep(dtp_ref[...], bias_ref[...], alog_ref[...], dt_s, acum_s, acumT_s)
            dacum_s[...] = jnp.zeros_like(dacum_s)
            ddt_s[...] = jnp.zeros_like(ddt_s)

        @pl.when((c == 0) & (g == 0))
        def _():
            da_s[...] = jnp.zeros_like(da_s)
            dd_ref[...] = jnp.zeros_like(dd_ref)
            dbias_ref[...] = jnp.zeros_like(dbias_ref)
            dalog_ref[...] = jnp.zeros_like(dalog_ref)

        row = lax.broadcasted_iota(jnp.int32, (CHUNK, CHUNK), 0)
        col = lax.broadcasted_iota(jnp.int32, (CHUNK, CHUNK), 1)
        sub_l = lax.broadcasted_iota(jnp.int32, (CHUNK, 1), 0)
        last = (sub_l == CHUNK - 1).astype(f32)
        nt = (((1,), (1,)), ((), ()))
        tn = (((0,), (0,)), ((), ()))
        Bb, Cb = b_ref[...], c_ref[...]
        Bm_, Cm_ = Bb.astype(MXU), Cb.astype(MXU)
        cb = lax.dot_general(Cm_, Bm_, nt, preferred_element_type=f32)
        bc = lax.dot_general(Bm_, Cm_, nt, preferred_element_type=f32)
        xs_blk, dy_blk = xs_ref[...], dy_ref[...]
        dxs, dB, dC = [], jnp.zeros((CHUNK, NSTATE), f32), jnp.zeros((CHUNK, NSTATE), f32)
        for j in range(HPG):
            h = g * HPG + j
            oh_l, dt_h, ac_h, acr_h = _head_cols(h, dt_s, acum_s, acumT_s)

            @pl.when(c == 0)
            def _():
                dstate[h] = jnp.zeros((NSTATE, HP), f32)

            dnext = dstate[h]
            prev = st_ref[0, j]
            lm = jnp.exp(jnp.where(col <= row, ac_h - acr_h, -1e30))
            lmT = jnp.exp(jnp.where(row <= col, acr_h - ac_h, -1e30))
            xs_h = xs_blk[:, HP * j:HP * (j + 1)]
            dy_h = dy_blk[:, HP * j:HP * (j + 1)]
            xd = xs_h * dt_h
            xdm, dym = xd.astype(MXU), dy_h.astype(MXU)
            ea = jnp.exp(ac_h)
            a_end = ac_h[CHUNK - 1:CHUNK, :]
            e_end = jnp.exp(a_end)
            dte = jnp.exp(a_end - ac_h)
            dnm, pvm = dnext.astype(MXU), prev.astype(MXU)
            bd = jnp.dot(Bm_, dnm, preferred_element_type=f32)
            dxd = jnp.dot((bc * lmT).astype(MXU), dym, preferred_element_type=f32) + dte * bd
            dw = lax.dot_general(dym, xdm, nt, preferred_element_type=f32)
            dwT = lax.dot_general(xdm, dym, nt, preferred_element_type=f32)
            dcb = dw * lm
            dbc = dwT * lmT
            eady = (ea * dy_h).astype(MXU)
            dC = dC + jnp.dot(dcb.astype(MXU), Bm_, preferred_element_type=f32) \
                + lax.dot_general(eady, pvm, nt, preferred_element_type=f32)
            dB = dB + jnp.dot(dbc.astype(MXU), Cm_, preferred_element_type=f32) \
                + dte * lax.dot_general(xdm, dnm, nt, preferred_element_type=f32)
            dstate[h] = lax.dot_general(Cm_, eady, tn, preferred_element_type=f32) + e_end * dnext
            r1 = jnp.sum(dcb * cb, axis=1, keepdims=True)
            r2 = jnp.sum(dbc * bc, axis=1, keepdims=True)
            y_off = jnp.dot(Cm_, pvm, preferred_element_type=f32) * ea
            t3 = jnp.sum(dy_h * y_off, axis=1, keepdims=True)
            t4 = jnp.sum(bd * xd, axis=1, keepdims=True) * dte
            end_extra = jnp.sum(t4, axis=0, keepdims=True) + e_end * jnp.sum(jnp.sum(prev * dnext, axis=1, keepdims=True), axis=0, keepdims=True)
            dacum_h = r1 - r2 + t3 - t4 + last * end_extra
            dacum_s[...] += dacum_h * oh_l
            ddt_s[...] += jnp.sum(dxd * xs_h, axis=1, keepdims=True) * oh_l
            d_h = jnp.sum(d_ref[...] * oh_l, axis=1, keepdims=True)
            dxs.append(dxd * dt_h + dy_h * d_h)
            dd_ref[...] += oh_l * jnp.sum(jnp.sum(dy_h * xs_h, axis=1, keepdims=True), axis=0, keepdims=True)
        dxs_ref[...] = jnp.concatenate(dxs, axis=1)
        db_ref[...] = dB
        dc_ref[...] = dC

        @pl.when(g == GROUPS - 1)
        def _():
            a_row = -jnp.exp(alog_ref[...])
            da = jnp.dot((row <= col).astype(f32), dacum_s[...], precision=lax.Precision.HIGHEST, preferred_element_type=f32)
            da_s[...] += jnp.sum(da * dt_s[...], axis=0, keepdims=True)
            z = dtp_ref[...] + bias_ref[...]
            ddt_pre = (ddt_s[...] + da * a_row) * _sigmoid(z)
            ddt_ref[...] = ddt_pre.astype(ddt_ref.dtype)
            dbias_ref[...] += jnp.sum(ddt_pre, axis=0, keepdims=True)

            @pl.when(c == NCHUNK - 1)
            def _():
                dalog_ref[...] = da_s[...] * a_row

    par = pl.BlockSpec((1, LANES), lambda c, g: (0, 0))
    return _call(
        body, [xs, Bm, Cm, dt_pre, dt_bias, a_log, d_skip, states, dy], comm=comm, name="ssd_bwd", grid=(NCHUNK, GROUPS),
        in_specs=[pl.BlockSpec((CHUNK, HPG * HP), lambda c, g: (rev(c), g)),
                  pl.BlockSpec((CHUNK, NSTATE), lambda c, g: (rev(c), g)),
                  pl.BlockSpec((CHUNK, NSTATE), lambda c, g: (rev(c), g)),
                  pl.BlockSpec((CHUNK, LANES), lambda c, g: (rev(c), 0)), par, par, par,
                  pl.BlockSpec((1, HPG, NSTATE, HP), lambda c, g: (rev(c), g, 0, 0)),
                  pl.BlockSpec((CHUNK, HPG * HP), lambda c, g: (rev(c), g))],
        out_specs=[pl.BlockSpec((CHUNK, HPG * HP), lambda c, g: (rev(c), g)),
                   pl.BlockSpec((CHUNK, NSTATE), lambda c, g: (rev(c), g)),
                   pl.BlockSpec((CHUNK, NSTATE), lambda c, g: (rev(c), g)),
                   pl.BlockSpec((CHUNK, LANES), lambda c, g: (rev(c), 0)), par, par, par],
        out_shape=[jax.ShapeDtypeStruct((S, INNER), f32), jax.ShapeDtypeStruct((S, GROUPS * NSTATE), f32),
                   jax.ShapeDtypeStruct((S, GROUPS * NSTATE), f32), jax.ShapeDtypeStruct((S, LANES), MXU),
                   jax.ShapeDtypeStruct((1, LANES), f32), jax.ShapeDtypeStruct((1, LANES), f32),
                   jax.ShapeDtypeStruct((1, LANES), f32)],
        scratch=[pltpu.VMEM((HEADS, NSTATE, HP), f32), pltpu.VMEM((CHUNK, LANES), f32),
                 pltpu.VMEM((CHUNK, LANES), f32), pltpu.VMEM((LANES, CHUNK), f32),
                 pltpu.VMEM((CHUNK, LANES), f32), pltpu.VMEM((CHUNK, LANES), f32), pltpu.VMEM((1, LANES), f32)],
        sem=("arbitrary", "arbitrary"))


ATT_STACK_FWD, ATT_STACK_BWD = 4, 2


def _attn_kv(kp, kc, vp, vc, hk):
    sl = slice(AH * hk, AH * (hk + 1))
    return (jnp.concatenate([kp[:, sl], kc[:, sl]], axis=0).astype(MXU),
            jnp.concatenate([vp[:, sl], vc[:, sl]], axis=0).astype(MXU))


def _stack_heads(x, heads):
    return jnp.concatenate([x[:, AH * h:AH * (h + 1)] for h in heads], axis=0)


def _attn_block(n, q, kb, sinks, heads):
    rows = len(heads) * WIN
    qi = lax.broadcasted_iota(jnp.int32, (rows, 2 * WIN), 0) & (WIN - 1)
    ki = lax.broadcasted_iota(jnp.int32, (rows, 2 * WIN), 1)
    rel = qi + WIN - ki
    mask = (rel >= 0) & (rel < WIN) & ((ki >= WIN) | (n > 0))
    qg = _stack_heads(q, heads).astype(MXU)
    s = lax.dot_general(qg, kb, (((1,), (1,)), ((), ())), preferred_element_type=f32) * (AH ** -0.5)
    s = jnp.where(mask, s, -1e30)
    sink = jnp.concatenate([jnp.broadcast_to(sinks[:, h:h + 1], (WIN, 1)) for h in heads], axis=0)
    m = jnp.maximum(jnp.max(s, axis=1, keepdims=True), sink)
    p = jnp.exp(s - m)
    ps = jnp.exp(sink - m)
    inv = 1.0 / (jnp.sum(p, axis=1, keepdims=True) + ps)
    return qg, p * inv, ps * inv


def _head_blocks(hk, stack):
    return [list(range(QPK * hk + i, QPK * hk + i + stack)) for i in range(0, QPK, stack)]


def _kv_specs():
    prev = lambda n: (jnp.maximum(n - 1, 0), 0)
    cur = lambda n: (n, 0)
    w = KVH * AH
    return [pl.BlockSpec((WIN, w), prev), pl.BlockSpec((WIN, w), cur), pl.BlockSpec((WIN, w), prev), pl.BlockSpec((WIN, w), cur)]


def attn_fwd(q, k, v, sinks, comm=None):
    def body(q_ref, kp_ref, kc_ref, vp_ref, vc_ref, s_ref, o_ref):
        n = pl.program_id(0)
        q_, kp, kc, vp, vc, sk = q_ref[...], kp_ref[...], kc_ref[...], vp_ref[...], vc_ref[...], s_ref[...]
        outs = []
        for hk in range(KVH):
            kb, vb = _attn_kv(kp, kc, vp, vc, hk)
            for heads in _head_blocks(hk, ATT_STACK_FWD):
                _, pr, _ = _attn_block(n, q_, kb, sk, heads)
                o = jnp.dot(pr.astype(MXU), vb, preferred_element_type=f32)
                outs += [o[WIN * i:WIN * (i + 1)] for i in range(len(heads))]
        o_ref[...] = jnp.concatenate(outs, axis=1)

    return _call(
        body, [q, k, k, v, v, sinks], comm=comm, name="attn_fwd", grid=(NBLK,),
        in_specs=[pl.BlockSpec((WIN, D), lambda n: (n, 0))] + _kv_specs() + [pl.BlockSpec((1, QH), lambda n: (0, 0))],
        out_specs=[pl.BlockSpec((WIN, D), lambda n: (n, 0))],
        out_shape=[jax.ShapeDtypeStruct((S, D), f32)], sem=("parallel",))[0]


def attn_bwd(q, k, v, sinks, dout, comm=None):
    def body(q_ref, kp_ref, kc_ref, vp_ref, vc_ref, s_ref, do_ref, dq_ref, dkp_ref, dkc_ref, dvp_ref, dvc_ref, ds_ref):
        n = pl.program_id(0)

        @pl.when(n == 0)
        def _():
            ds_ref[...] = jnp.zeros_like(ds_ref)

        q_, kp, kc, vp, vc, sk, do = q_ref[...], kp_ref[...], kc_ref[...], vp_ref[...], vc_ref[...], s_ref[...], do_ref[...]
        lane = lax.broadcasted_iota(jnp.int32, (1, QH), 1)
        nt = (((1,), (1,)), ((), ()))
        tn = (((0,), (0,)), ((), ()))
        dqs, dkps, dkcs, dvps, dvcs = [], [], [], [], []
        dsink = jnp.zeros((1, QH), f32)
        for hk in range(KVH):
            kb, vb = _attn_kv(kp, kc, vp, vc, hk)
            dkb, dvb = jnp.zeros((2 * WIN, AH), f32), jnp.zeros((2 * WIN, AH), f32)
            for heads in _head_blocks(hk, ATT_STACK_BWD):
                qg, pr, prs = _attn_block(n, q_, kb, sk, heads)
                dog = _stack_heads(do, heads).astype(MXU)
                dp = lax.dot_general(dog, vb, nt, preferred_element_type=f32)
                dvb = dvb + lax.dot_general(pr.astype(MXU), dog, tn, preferred_element_type=f32)
                delta = jnp.sum(pr * dp, axis=1, keepdims=True)
                ds = (pr * (dp - delta)).astype(MXU)
                dsk = -prs * delta
                for i, h in enumerate(heads):
                    dsink = dsink + jnp.sum(dsk[WIN * i:WIN * (i + 1)], axis=0, keepdims=True) * (lane == h).astype(f32)
                dqg = jnp.dot(ds, kb, preferred_element_type=f32) * (AH ** -0.5)
                dkb = dkb + lax.dot_general(ds, qg, tn, preferred_element_type=f32) * (AH ** -0.5)
                dqs += [dqg[WIN * i:WIN * (i + 1)] for i in range(len(heads))]
            dkps.append(dkb[:WIN])
            dkcs.append(dkb[WIN:])
            dvps.append(dvb[:WIN])
            dvcs.append(dvb[WIN:])
        dq_ref[...] = jnp.concatenate(dqs, axis=1)
        dkp_ref[...] = jnp.concatenate(dkps, axis=1)
        dkc_ref[...] = jnp.concatenate(dkcs, axis=1)
        dvp_ref[...] = jnp.concatenate(dvps, axis=1)
        dvc_ref[...] = jnp.concatenate(dvcs, axis=1)
        ds_ref[...] += dsink

    w = KVH * AH
    blk = lambda width: pl.BlockSpec((WIN, width), lambda n: (n, 0))
    return _call(
        body, [q, k, k, v, v, sinks, dout], comm=comm, name="attn_bwd", grid=(NBLK,),
        in_specs=[blk(D)] + _kv_specs() + [pl.BlockSpec((1, QH), lambda n: (0, 0)), blk(D)],
        out_specs=[blk(D), blk(w), blk(w), blk(w), blk(w), pl.BlockSpec((1, QH), lambda n: (0, 0))],
        out_shape=[jax.ShapeDtypeStruct((S, D), f32)] + [jax.ShapeDtypeStruct((S, w), f32)] * 4 + [jax.ShapeDtypeStruct((1, QH), f32)],
        sem=("arbitrary",))


def kv_bwd(kv, pos, inv_freq, k_norm, dkp, dkc, dvp, dvc):
    w = KVH * AH

    def body(kv_ref, pos_ref, if_ref, g_ref, dkp_ref, dkc_ref, dvp_ref, dvc_ref, o_ref, dg_ref, db_ref):
        n = pl.program_id(0)

        @pl.when(n == 0)
        def _():
            dg_ref[...] = jnp.zeros_like(dg_ref)
            db_ref[...] = jnp.zeros_like(db_ref)

        inside = (n < NBLK - 1).astype(f32)
        dk = dkc_ref[...] + inside * dkp_ref[...]
        dv = dvc_ref[...] + inside * dvp_ref[...]
        cos, sin = _rope_tables(pos_ref[...], if_ref[...])
        dkpre, dg = _headnorm_rope_bwd(kv_ref[...], g_ref[...], cos, sin, dk, KVH)
        dkv = jnp.concatenate([dkpre, dv], axis=1)
        o_ref[...] = dkv.astype(o_ref.dtype)
        dg_ref[...] += dg
        db_ref[...] += jnp.sum(dkv, axis=0, keepdims=True)

    nxt = lambda n: (jnp.minimum(n + 1, NBLK - 1), 0)
    cur = lambda n: (n, 0)
    const = lambda n: (0, 0)
    return pl.pallas_call(
        body, name="kv_bwd", grid=(NBLK,),
        in_specs=[pl.BlockSpec((WIN, w), cur), pl.BlockSpec((WIN, 1), cur), pl.BlockSpec((1, AH // 2), const),
                  pl.BlockSpec((1, AH), const), pl.BlockSpec((WIN, w), nxt), pl.BlockSpec((WIN, w), cur),
                  pl.BlockSpec((WIN, w), nxt), pl.BlockSpec((WIN, w), cur)],
        out_specs=[pl.BlockSpec((WIN, 2 * w), cur), pl.BlockSpec((1, AH), const), pl.BlockSpec((1, 2 * w), const)],
        out_shape=[jax.ShapeDtypeStruct((S, 2 * w), MXU), jax.ShapeDtypeStruct((1, AH), f32), jax.ShapeDtypeStruct((1, 2 * w), f32)],
        compiler_params=_cparams(("arbitrary",)),
    )(kv, pos, inv_freq, k_norm, dkp, dkc, dvp, dvc)


def _adam_math(w, g, m, v):
    m = ADAM_B1 * m + (1.0 - ADAM_B1) * g
    v = ADAM_B2 * v + (1.0 - ADAM_B2) * (g * g)
    m_hat = m / (1.0 - ADAM_B1 ** ADAM_STEP)
    v_hat = v / (1.0 - ADAM_B2 ** ADAM_STEP)
    return -ADAM_LR * (m_hat / (jnp.sqrt(v_hat) + ADAM_EPS) + ADAM_WD * w), m, v


def adamw(name, w, g, m, v):
    R, C = w.shape
    tr = _pick(R, (256, 128, 64, 32, 16, 8))
    tc = C if tr < R or C % 256 else 256

    def body(w_ref, g_ref, m_ref, v_ref, d_ref, nm_ref, nv_ref):
        d_ref[...], nm_ref[...], nv_ref[...] = _adam_math(w_ref[...], g_ref[...], m_ref[...], v_ref[...])

    spec = pl.BlockSpec((tr, tc), lambda i, j: (i, j))
    return pl.pallas_call(
        body, name=name, grid=(R // tr, C // tc), in_specs=[spec] * 4, out_specs=[spec] * 3,
        out_shape=[jax.ShapeDtypeStruct((R, C), f32)] * 3, compiler_params=_cparams(("parallel", "parallel")),
    )(w, g, m, v)


def _me():
    return lax.axis_index("x"), lax.axis_index("y"), lax.axis_index("c")


def gather_comm(xs):
    n = len(xs)

    def parts(x_refs, o_refs, sems):
        send_sems, recv_sems, local_sems = sems
        x, y, c = _me()
        me, sibling = (x, y, c), (x, y, 1 - c)
        chips = [(1 - x, y), (x, 1 - y), (1 - x, 1 - y)]

        def copy(a, k, block, to, src=None):
            dst = o_refs[a].at[4 * block[0] + 2 * block[1] + block[2]]
            return pltpu.make_async_remote_copy(
                src_ref=dst if src is None else src, dst_ref=dst,
                send_sem=send_sems.at[7 * a + k], recv_sem=recv_sems.at[7 * a + k], device_id=to, device_id_type=MESH)

        mine = [pltpu.make_async_copy(x_refs[a], o_refs[a].at[4 * x + 2 * y + c], local_sems.at[a]) for a in range(n)]
        first = []
        for a in range(n):
            first.append(copy(a, 0, me, sibling, src=x_refs[a]))
            first += [copy(a, 1 + j, me, (*chip, c), src=x_refs[a]) for j, chip in enumerate(chips)]
        return copy, mine, first, me, sibling, chips, c

    def start(x_refs, o_refs, sems):
        _, mine, first, *_ = parts(x_refs, o_refs, sems)
        for cp in mine + first:
            cp.start()

    def finish(x_refs, o_refs, sems):
        copy, mine, first, me, sibling, chips, c = parts(x_refs, o_refs, sems)
        passed = []
        for j, chip in enumerate(chips):
            for a in range(n):
                copy(a, 1 + j, (*chip, c), me).wait_recv()
                cp = copy(a, 4 + j, (*chip, c), sibling)
                cp.start()
                passed.append(cp)
        for a in range(n):
            copy(a, 0, sibling, me).wait_recv()
            for j, chip in enumerate(chips):
                copy(a, 4 + j, (*chip, 1 - c), me).wait_recv()
        for cp in first + passed:
            cp.wait_send()
        for cp in mine:
            cp.wait()

    return Comm(xs, [jax.ShapeDtypeStruct((N_DEV,) + a.shape, a.dtype) for a in xs],
                [pltpu.SemaphoreType.DMA((7 * n,)), pltpu.SemaphoreType.DMA((7 * n,)), pltpu.SemaphoreType.DMA((n,))], start, finish)


def run_comm(name, comm):
    _call(lambda: None, [], name=name, grid=(1,), in_specs=[], out_specs=[], out_shape=[], comm=comm)
    return comm.results


def sibling_comm(gs):
    n = len(gs)

    def copies(g_refs, o_refs, sems):
        x, y, c = _me()
        return [pltpu.make_async_remote_copy(
            src_ref=g_refs[a].at[:, 1 - c], dst_ref=o_refs[a], send_sem=sems[0].at[a], recv_sem=sems[1].at[a],
            device_id=(x, y, 1 - c), device_id_type=MESH) for a in range(n)]

    def start(g_refs, o_refs, sems):
        for cp in copies(g_refs, o_refs, sems):
            cp.start()

    def finish(g_refs, o_refs, sems):
        for cp in copies(g_refs, o_refs, sems):
            cp.wait()

    return Comm(gs, [jax.ShapeDtypeStruct((4,) + g.shape[2:], g.dtype) for g in gs],
                [pltpu.SemaphoreType.DMA((n,)), pltpu.SemaphoreType.DMA((n,))], start, finish)


def chip_comm(ts):
    n = len(ts)

    def copies(t_refs, o_refs, sems):
        x, y, c = _me()
        chips = [(1 - x, y), (x, 1 - y), (1 - x, 1 - y)]
        return [pltpu.make_async_remote_copy(
            src_ref=t_refs[a].at[2 * px + py], dst_ref=o_refs[a].at[j],
            send_sem=sems[0].at[3 * a + j], recv_sem=sems[1].at[3 * a + j],
            device_id=(px, py, c), device_id_type=MESH) for j, (px, py) in enumerate(chips) for a in range(n)]

    def start(t_refs, o_refs, sems):
        for cp in copies(t_refs, o_refs, sems):
            cp.start()

    def finish(t_refs, o_refs, sems):
        for cp in copies(t_refs, o_refs, sems):
            cp.wait()

    return Comm(ts, [jax.ShapeDtypeStruct((3,) + t.shape[1:], t.dtype) for t in ts],
                [pltpu.SemaphoreType.DMA((3 * n,)), pltpu.SemaphoreType.DMA((3 * n,))], start, finish)


def _row_tile(rows):
    return _pick(rows, (512, 304, 256, 128))


def pair_add(name, g, r):
    _, _, R, C = g.shape
    tr = _row_tile(R)

    def body(c_ref, g_ref, r_ref, o_ref):
        o_ref[0] = (g_ref[0, 0].astype(f32) + r_ref[0].astype(f32)).astype(o_ref.dtype)

    return pl.pallas_call(
        body, name=name,
        grid_spec=pltpu.PrefetchScalarGridSpec(
            num_scalar_prefetch=1, grid=(4, R // tr),
            in_specs=[pl.BlockSpec((1, 1, tr, C), lambda p, i, c: (p, c[0], i, 0)),
                      pl.BlockSpec((1, tr, C), lambda p, i, c: (p, i, 0))],
            out_specs=pl.BlockSpec((1, tr, C), lambda p, i, c: (p, i, 0))),
        out_shape=jax.ShapeDtypeStruct((4, R, C), g.dtype),
        compiler_params=_cparams(("parallel", "parallel")),
    )(lax.axis_index("c").reshape(1).astype(jnp.int32), g, r)


def _sum_of_four(t_ref, r_ref):
    return ((t_ref[0].astype(f32) + r_ref[0].astype(f32)) + r_ref[1].astype(f32)) + r_ref[2].astype(f32)


def _my_chip():
    return (2 * lax.axis_index("x") + lax.axis_index("y")).reshape(1).astype(jnp.int32)


def final_adamw(name, t, r, w, m, v):
    _, R, C = t.shape
    tr = _pick(R, (256, 128, 64, 32, 16))
    tc = C if tr < R or C % 256 else 256

    def body(p_ref, t_ref, r_ref, w_ref, m_ref, v_ref, g_ref, d_ref, nm_ref, nv_ref):
        g_ = _sum_of_four(t_ref, r_ref)
        g_ref[...] = g_
        d_ref[...], nm_ref[...], nv_ref[...] = _adam_math(w_ref[...], g_, m_ref[...], v_ref[...])

    flat = pl.BlockSpec((tr, tc), lambda i, j, p: (i, j))
    return pl.pallas_call(
        body, name=name,
        grid_spec=pltpu.PrefetchScalarGridSpec(
            num_scalar_prefetch=1, grid=(R // tr, C // tc),
            in_specs=[pl.BlockSpec((1, tr, tc), lambda i, j, p: (p[0], i, j)),
                      pl.BlockSpec((3, tr, tc), lambda i, j, p: (0, i, j)), flat, flat, flat],
            out_specs=[flat] * 4),
        out_shape=[jax.ShapeDtypeStruct((R, C), f32)] * 4,
        compiler_params=_cparams(("parallel", "parallel")),
    )(_my_chip(), t, r, w, m, v)


def final_adamw_layers(name, t0, r0, t1, r1, w, m, v, comm=None):
    _, R, C = t0.shape
    tr = _pick(R, (256, 128, 64, 32, 16))
    steps = R // tr
    c_ins, c_shapes, c_sems = (comm.ins, comm.out_shapes, comm.sems) if comm else ([], [], [])

    def body(p_ref, t0_ref, r0_ref, t1_ref, r1_ref, w_ref, m_ref, v_ref, *rest):
        ci, (g_ref, d_ref, nm_ref, nv_ref) = rest[:len(c_ins)], rest[len(c_ins):len(c_ins) + 4]
        co, cs = rest[len(c_ins) + 4:len(c_ins) + 4 + len(c_shapes)], rest[len(c_ins) + 4 + len(c_shapes):]
        if comm:
            @pl.when((pl.program_id(0) == 0) & (pl.program_id(1) == 0))
            def _():
                comm.start(ci, co, cs)

        g_ = jnp.where(pl.program_id(0) == 0, _sum_of_four(t0_ref, r0_ref), _sum_of_four(t1_ref, r1_ref))
        g_ref[0] = g_
        d_ref[0], nm_ref[0], nv_ref[0] = _adam_math(w_ref[0], g_, m_ref[0], v_ref[0])
        if comm:
            @pl.when((pl.program_id(0) == 1) & (pl.program_id(1) == steps - 1))
            def _():
                comm.finish(ci, co, cs)

    mine = pl.BlockSpec((1, tr, C), lambda l, i, p: (p[0], i, 0))
    theirs = pl.BlockSpec((3, tr, C), lambda l, i, p: (0, i, 0))
    layer = pl.BlockSpec((1, tr, C), lambda l, i, p: (l, i, 0))
    any_spec = pl.BlockSpec(memory_space=pl.ANY)
    res = pl.pallas_call(
        body, name=name,
        grid_spec=pltpu.PrefetchScalarGridSpec(
            num_scalar_prefetch=1, grid=(2, steps),
            in_specs=[mine, theirs, mine, theirs, layer, layer, layer] + [any_spec] * len(c_ins),
            out_specs=[layer] * 4 + [any_spec] * len(c_shapes), scratch_shapes=list(c_sems)),
        out_shape=[jax.ShapeDtypeStruct((2, R, C), f32)] * 4 + list(c_shapes),
        compiler_params=_cparams(("arbitrary", "arbitrary") if comm else ("parallel", "parallel")),
    )(_my_chip(), t0, r0, t1, r1, w, m, v, *c_ins)
    if comm:
        comm.set_results(res[4:])
    return res[:4]


class ReduceScatter:
    def __init__(self, tag, keys, grads):
        self.tag, self.keys, self.grads = tag, keys, grads
        self.send = [g.reshape((4, 2, g.shape[0] // N_DEV) + g.shape[1:]) for g in grads]

    def sibling(self):
        self.c1 = sibling_comm(self.send)
        return self.c1

    def chips(self):
        self.pairs = [pair_add(f"rs_pair_add_{self.tag}{i}", g, r) for i, (g, r) in enumerate(zip(self.send, self.c1.results))]
        self.c2 = chip_comm(self.pairs)
        return self.c2

    def parts(self):
        return {k: (t, r) for k, t, r in zip(self.keys, self.pairs, self.c2.results)}


IN_ROWS = {"z": (0, 2048), "xs": (2048, 4096), "B": (4096, 5120), "C": (5120, 6144)}
IN_COLS = 2 * INNER + 2 * GROUPS * NSTATE + HEADS


def sum_devices(g):
    def body(g_ref, o_ref):
        acc = g_ref[0]
        for i in range(1, N_DEV):
            acc = acc + g_ref[i]
        o_ref[...] = acc

    return pl.pallas_call(body, name="sum_devices", out_shape=jax.ShapeDtypeStruct(g.shape[1:], f32),
                          compiler_params=_cparams())(g)


def _pack(parts, unit, dtype, lead=()):
    flat = jnp.concatenate([p.reshape(lead + (-1,)).astype(dtype) for p in parts], axis=-1)
    n = flat.shape[-1]
    rows = -(-n // (unit * PACK_W)) * unit
    flat = jnp.pad(flat, [(0, 0)] * len(lead) + [(0, rows * PACK_W - n)])
    return flat.reshape(lead + (rows, PACK_W))


def _unpack(buf, shapes, lead=()):
    flat = buf.reshape(lead + (-1,))
    out, off = [], 0
    for shp in shapes:
        n = math.prod(shp)
        out.append(flat[..., off:off + n].reshape(lead + tuple(shp)))
        off += n
    return out


def _pad_lanes(a):
    return jnp.pad(a, [(0, 0)] * (a.ndim - 1) + [(0, LANES - a.shape[-1])])


_NN = (((1,), (0,)), ((), ()))
_NT = (((1,), (1,)), ((), ()))


def _mm(a, b, dims):
    return lax.dot_general(a.astype(MXU), b.astype(MXU), dims, preferred_element_type=f32)


def _ffn_fwd(tag, x, h, w_inT, conv_w, conv_b, mid_comm=None):
    ct, nblk = FFN_CT, FFN // FFN_CT

    def body(h_ref, wg_ref, wv_ref, cw_ref, cb_ref, gp_ref, v_ref, a_ref):
        h_ = h_ref[...]
        gp, v_ = _mm(h_, wg_ref[...], _NT), _mm(h_, wv_ref[...], _NT)
        gp_ref[...] = gp
        v_ref[...] = v_
        a_ref[...] = (_silu(_conv(gp, cw_ref[...], cb_ref[...])) * v_).astype(a_ref.dtype)

    col = pl.BlockSpec((S, ct), lambda j: (0, j))
    gate_pre, val, act = _call(
        body, [h, w_inT, w_inT, conv_w, conv_b], comm=mid_comm, name=f"{tag}_in", grid=(nblk,),
        in_specs=[pl.BlockSpec((S, D), lambda j: (0, 0)), pl.BlockSpec((ct, D), lambda j: (j, 0)),
                  pl.BlockSpec((ct, D), lambda j: (nblk + j, 0)), pl.BlockSpec((CONV_F, ct), lambda j: (0, j)),
                  pl.BlockSpec((1, ct), lambda j: (0, j))],
        out_specs=[col, col, col],
        out_shape=[jax.ShapeDtypeStruct((S, FFN), f32), jax.ShapeDtypeStruct((S, FFN), f32), jax.ShapeDtypeStruct((S, FFN), MXU)],
        sem=("parallel",))
    return act, (x, h, gate_pre, val, act)


FFN_CT = 256
CONV_CT = 256


def _proj_conv(name, h, wT, row0, cw, cb, comm=None):
    C, ct = cw.shape[1], CONV_CT

    def body(h_ref, w_ref, cw_ref, cb_ref, p_ref, c_ref):
        p = _mm(h_ref[...], w_ref[...], _NT)
        p_ref[...] = p
        c_ref[...] = _silu(_conv(p, cw_ref[...], cb_ref[...]))

    col = pl.BlockSpec((S, ct), lambda j: (0, j))
    return _call(
        body, [h, wT, cw, cb], comm=comm, name=name, grid=(C // ct,),
        in_specs=[pl.BlockSpec((S, D), lambda j: (0, 0)), pl.BlockSpec((ct, D), lambda j: (row0 // ct + j, 0)),
                  pl.BlockSpec((CONV_A, ct), lambda j: (0, j)), pl.BlockSpec((1, ct), lambda j: (0, j))],
        out_specs=[col, col], out_shape=[jax.ShapeDtypeStruct((S, C), f32)] * 2, sem=("parallel",))


def _dconv_wgrad(name, pre, dconv, cw, cb, h):
    C, ct = cw.shape[1], CONV_CT

    def body(p_ref, do_ref, cw_ref, cb_ref, h_ref, dp_ref, g_ref, dw_ref, db_ref):
        p_, w_ = p_ref[...], cw_ref[...]
        taps = _taps(p_, CONV_A)
        dx, dw, db = _conv_bwd(p_, w_, do_ref[...] * _dsilu(_conv(p_, w_, cb_ref[...], taps)), taps)
        dpm = dx.astype(MXU)
        dp_ref[...] = dpm
        g_ref[...] = lax.dot_general(dpm, h_ref[...].astype(MXU), (((0,), (0,)), ((), ())),
                                     preferred_element_type=f32).astype(g_ref.dtype)
        dw_ref[...] = dw
        db_ref[...] = db

    col = pl.BlockSpec((S, ct), lambda j: (0, j))
    return _call(
        body, [pre, dconv, cw, cb, h], name=name, grid=(C // ct,),
        in_specs=[col, col, pl.BlockSpec((CONV_A, ct), lambda j: (0, j)), pl.BlockSpec((1, ct), lambda j: (0, j)),
                  pl.BlockSpec((S, D), lambda j: (0, 0))],
        out_specs=[col, pl.BlockSpec((ct, D), lambda j: (j, 0)), pl.BlockSpec((CONV_A, ct), lambda j: (0, j)),
                   pl.BlockSpec((1, ct), lambda j: (0, j))],
        out_shape=[jax.ShapeDtypeStruct((S, C), MXU), jax.ShapeDtypeStruct((C, D), MXU),
                   jax.ShapeDtypeStruct((CONV_A, C), f32), jax.ShapeDtypeStruct((1, C), f32)],
        sem=("parallel",))


def _ffn_mid_bwd(name, dout, w_down, gate_pre, val, conv_w, conv_b, comm=None):
    ct = FFN_CT

    def body(do_ref, wd_ref, gp_ref, v_ref, w_ref, b_ref, dgv_ref, dw_ref, db_ref, dob_s):
        @pl.when(pl.program_id(0) == 0)
        def _():
            dob_s[...] = do_ref[...].astype(MXU)

        da = _mm(dob_s[...], wd_ref[...], _NT)
        gp, v_, w_ = gp_ref[...], v_ref[...], w_ref[...]
        taps = _taps(gp, CONV_F)
        gate = _conv(gp, w_, b_ref[...], taps)
        sg = _sigmoid(gate)
        dgp, dw, db = _conv_bwd(gp, w_, da * v_ * (sg * (1.0 + gate * (1.0 - sg))), taps)
        dgv_ref[0] = dgp.astype(dgv_ref.dtype)
        dgv_ref[1] = (da * (gate * sg)).astype(dgv_ref.dtype)
        dw_ref[...] = dw
        db_ref[...] = db

    col = pl.BlockSpec((S, ct), lambda j: (0, j))
    return _call(
        body, [dout, w_down, gate_pre, val, conv_w, conv_b], comm=comm, name=name, grid=(FFN // ct,),
        in_specs=[pl.BlockSpec((S, D), lambda j: (0, 0)), pl.BlockSpec((ct, D), lambda j: (j, 0)), col, col,
                  pl.BlockSpec((CONV_F, ct), lambda j: (0, j)), pl.BlockSpec((1, ct), lambda j: (0, j))],
        out_specs=[pl.BlockSpec((2, S, ct), lambda j: (0, 0, j)), pl.BlockSpec((CONV_F, ct), lambda j: (0, j)),
                   pl.BlockSpec((1, ct), lambda j: (0, j))],
        out_shape=[jax.ShapeDtypeStruct((2, S, FFN), MXU), jax.ShapeDtypeStruct((CONV_F, FFN), f32), jax.ShapeDtypeStruct((1, FFN), f32)],
        scratch=[pltpu.VMEM((S, D), MXU)], sem=("arbitrary",))


def _ffn_bwd(tag, layer, saved, norm_g, w_inT, conv_w, conv_b, w_down, dout, mid_comm=None, before=None):
    x, h, gate_pre, val, act = saved
    g_down = matmul(f"{tag}_wdown", act, dout, "tn", out_dtype=MXU)
    dgv, g_cw, g_cb = _ffn_mid_bwd(f"{tag}_dmid", dout, w_down, gate_pre, val, conv_w, conv_b, comm=mid_comm)
    g_inT = matmul_tn_stacked(f"{tag}_win", dgv, h, MXU)

    def din_fn(dg_, dv_, x_, do_, g_, wT, *wb):
        dx, dg = _rms_bwd(x_, g_, _mm(dg_, wT[:FFN], _NN) + _mm(dv_, wT[FFN:], _NN))
        dx = do_ + dx
        return (dx,) + tuple(_mm(dx, w_, _NT) for w_ in wb) + (dg,)

    rs = ReduceScatter(tag, (f"f_inT{layer}", f"f_down{layer}"), [g_inT, g_down])
    extra = [] if before is None else [before]
    *dxs, g_norm = rowwise(f"{tag}_din", din_fn, [(dgv, 0), (dgv, 1), x, dout], [norm_g, w_inT] + extra,
                           [(D, f32)] + [(w_.shape[0], f32) for w_ in extra], [(1, D)], comm=rs.sibling())
    small = {f"f_norm{layer}": g_norm, f"f_conv_w{layer}": g_cw, f"f_conv_b{layer}": g_cb}
    return (dxs[0] if before is None else tuple(dxs)), small, rs


def _land(W, keys, comm):
    for k, g in zip(keys, comm.results):
        W[k] = g.reshape(-1, g.shape[2])


def _local_step(x, pos, tgt, W, shards):
    G = {}
    gather = lambda *keys: gather_comm([shards[k] for k in keys])
    inv_freq = (ROPE_THETA ** (-jnp.arange(AH // 2, dtype=f32) / (AH // 2))).reshape(1, AH // 2)

    def in_fn(x_, g_, wT, wdtT):
        h_ = _rms_fwd(x_, g_).astype(MXU)
        return h_, _mm(h_, wT[slice(*IN_ROWS["z"])], _NT), _mm(h_, wdtT, _NT)

    h0, z, dt_pre = rowwise("a_in", in_fn, [x], [W["a_norm"], W["inT"], W["in_dtT"]], [(D, MXU), (INNER, f32), (LANES, f32)])
    pre, conv = {}, {}
    early = {"xs": ("a_out",), "B": (), "C": ()}
    for k in ("xs", "B", "C"):
        c = gather(*early[k]) if early[k] else None
        pre[k], conv[k] = _proj_conv(f"a_in_{k}", h0, W["inT"], IN_ROWS[k][0], W[f"cw_{k}"], W[f"cb_{k}"], comm=c)
        if c is not None:
            _land(W, early[k], c)
    c = gather("f_inT0", "f_down0")
    y, states = ssd_fwd(conv["xs"], conv["B"], conv["C"], dt_pre, W["dt_bias"], W["A_log"], W["D"], comm=c)
    _land(W, ("f_inT0", "f_down0"), c)

    def gate_norm(y_, z_, g_):
        yg = y_ * _silu(z_)
        w = INNER // GROUPS
        return (jnp.concatenate([_rms_fwd(yg[:, w * i:w * (i + 1)], g_[:, w * i:w * (i + 1)]) for i in range(GROUPS)], axis=1),)

    def out_fn(y_, z_, x_, g_, w_, gf_):
        (gn_,) = gate_norm(y_, z_, g_)
        gn_ = gn_.astype(MXU)
        x1_ = x_ + _mm(gn_, w_, _NN)
        return gn_, x1_, _rms_fwd(x1_, gf_)

    c = gather("f_down1")
    gn, x1, h1 = rowwise("a_out", out_fn, [y, z, x], [W["a_gnorm"], W["a_out"], W["f_norm0"]],
                         [(INNER, MXU), (D, f32), (D, MXU)], comm=c)
    _land(W, ("f_down1",), c)

    c = gather("w_kv", "w_q", "w_o")
    act0, ffn0 = _ffn_fwd("f0", x1, h1, W["f_inT0"], W["f_cw0"], W["f_cb0"], mid_comm=c)
    _land(W, ("w_kv", "w_q", "w_o"), c)
    kw = KVH * AH

    def qkv_fn(a_, x_, pos_, wd, gk, gb, wkv, bkv, wq, bq, if_, kn, qn):
        x2_ = x_ + _mm(a_, wd, _NN)
        kvn_, h2_ = _rms_fwd(x2_, gk).astype(MXU), _rms_fwd(x2_, gb).astype(MXU)
        kv_, qp_ = _mm(kvn_, wkv, _NN) + bkv, _mm(h2_, wq, _NN) + bq
        cos, sin = _rope_tables(pos_, if_)
        return (x2_, kvn_, h2_, kv_, qp_, _headnorm_rope_fwd(kv_[:, :kw], kn, cos, sin, KVH), kv_[:, kw:],
                _headnorm_rope_fwd(qp_, qn, cos, sin, QH))

    x2, kvn, h2, kv, q_pre, k_rot, v_val, q = rowwise(
        "f0_down_qkv", qkv_fn, [act0, x1, pos],
        [W["f_down0"], W["kv_norm"], W["b_norm"], W["w_kv"], W["b_kv"], W["w_q"], W["b_q"], inv_freq, W["k_norm"], W["q_norm"]],
        [(D, f32), (D, MXU), (D, MXU), (2 * kw, f32), (D, f32), (kw, f32), (kw, f32), (D, f32)])
    c = gather("f_inT1")
    att = attn_fwd(q, k_rot, v_val, W["sinks"], comm=c)
    _land(W, ("f_inT1",), c)
    def o_fn(att_, x_, w_, b_, gf_):
        x3_ = x_ + _mm(att_, w_, _NN) + b_
        return x3_, _rms_fwd(x3_, gf_)

    x3, h3 = rowwise("o_proj", o_fn, [att, x2], [W["w_o"], W["b_o"], W["f_norm1"]], [(D, f32), (D, MXU)])

    act1, ffn1 = _ffn_fwd("f1", x3, h3, W["f_inT1"], W["f_cw1"], W["f_cb1"])

    def loss_fn(a_, x_, t_, w_):
        diff = x_ + _mm(a_, w_, _NN) - t_
        rows = jnp.sum(diff * diff, axis=1, keepdims=True) * (0.5 / D)
        return diff * (1.0 / D), jnp.sum(rows, axis=0, keepdims=True)

    dx4, loss = rowwise("f1_down_loss", loss_fn, [act1, x3, tgt], [W["f_down1"]], [(D, f32)], [(1, 1)])

    (dx3, datt), g, rs_f1 = _ffn_bwd("f1", 1, ffn1, W["f_norm1"], W["f_inT1"], W["f_cw1"], W["f_cb1"], W["f_down1"], dx4,
                                     before=W["w_o"])
    G.update(g)

    g_wo = matmul("o_wproj", att, dx3, "tn", out_dtype=MXU)
    dq, dkp, dkc, dvp, dvc, G["sinks"] = attn_bwd(q, k_rot, v_val, W["sinks"], datt, comm=rs_f1.chips())

    def q_bwd(q_, pos_, dq_, dx_, if_, g_):
        cos, sin = _rope_tables(pos_, if_)
        dqp, dg = _headnorm_rope_bwd(q_, g_, cos, sin, dq_, QH)
        return dqp, dg, jnp.sum(dqp, axis=0, keepdims=True), jnp.sum(dx_, axis=0, keepdims=True)

    dq_pre, G["q_norm"], G["b_q"], G["b_o"] = rowwise("q_drope", q_bwd, [q_pre, pos, dq, dx3], [inv_freq, W["q_norm"]],
                                                      [(D, MXU)], [(1, AH), (1, D), (1, D)])
    g_wq = matmul("q_wproj", h2, dq_pre, "tn", out_dtype=MXU)
    dkv, G["k_norm"], G["b_kv"] = kv_bwd(kv, pos, inv_freq, W["k_norm"], dkp, dkc, dvp, dvc)
    g_wkv = matmul("kv_wproj", kvn, dkv, "tn", out_dtype=MXU)
    rs_att = ReduceScatter("att", ("w_kv", "w_q", "w_o"), [g_wkv, g_wq, g_wo])

    def x2_bwd(x_, dq_, dkv_, dx_, gb_, gk_, wq, wkv):
        d1, dgb = _rms_bwd(x_, gb_, _mm(dq_, wq, _NT))
        d2, dgk = _rms_bwd(x_, gk_, _mm(dkv_, wkv, _NT))
        return dx_ + d1 + d2, dgb, dgk

    dx2, G["b_norm"], G["kv_norm"] = rowwise("qkv_dproj", x2_bwd, [x2, dq_pre, dkv, dx3],
                                             [W["b_norm"], W["kv_norm"], W["w_q"], W["w_kv"]],
                                             [(D, f32)], [(1, D), (1, D)], comm=rs_att.sibling())

    dx1, g, rs_f0 = _ffn_bwd("f0", 0, ffn0, W["f_norm0"], W["f_inT0"], W["f_cw0"], W["f_cb0"], W["f_down0"], dx2,
                             mid_comm=rs_att.chips())
    G.update(g)

    rs_out = ReduceScatter("a_out", ("a_out",), [matmul("a_wout", gn, dx1, "tn", out_dtype=MXU)])

    def gate_norm_bwd(y_, z_, dx_, g_, w_out):
        dgn_ = _mm(dx_, w_out, _NT)
        w = INNER // GROUPS
        sg = _sigmoid(z_)
        sz = z_ * sg
        yg = y_ * sz
        parts, dgs = [], []
        for i in range(GROUPS):
            dseg, dg = _rms_bwd(yg[:, w * i:w * (i + 1)], g_[:, w * i:w * (i + 1)], dgn_[:, w * i:w * (i + 1)])
            parts.append(dseg)
            dgs.append(dg)
        dyg = jnp.concatenate(parts, axis=1)
        return dyg * sz, dyg * y_ * (sg * (1.0 + z_ * (1.0 - sg))), jnp.concatenate(dgs, axis=1)

    dy, dz, G["a_gnorm"] = rowwise("a_dout", gate_norm_bwd, [y, z, dx1], [W["a_gnorm"], W["a_out"]],
                                   [(INNER, f32), (INNER, MXU)], [(1, INNER)], comm=rs_out.sibling())
    dconv = {}
    dconv["xs"], dconv["B"], dconv["C"], ddt_pre, G["dt_bias"], G["A_log"], G["D"] = ssd_bwd(
        conv["xs"], conv["B"], conv["C"], dt_pre, W["dt_bias"], W["A_log"], W["D"], states, dy,
        comm=merge_comms([rs_f0.chips(), rs_out.chips()]))

    g_in, dpre = [matmul("a_win_z", dz, h0, "tn", out_dtype=MXU)], {}
    for k in ("xs", "B", "C"):
        dpre[k], g_k, G[f"cw_{k}"], G[f"cb_{k}"] = _dconv_wgrad(f"a_dconv_{k}", pre[k], dconv[k], W[f"cw_{k}"], W[f"cb_{k}"], h0)
        g_in.append(g_k)
    g_in.append(matmul("a_win_dt", ddt_pre, h0, "tn", out_dtype=MXU)[:HEADS])
    rs_in = ReduceScatter("a_in", ("inT",), [jnp.concatenate(g_in, axis=0)])
    run_comm("rs_in_sibling", rs_in.sibling())

    def x0_bwd(dz_, dxs_, db_, dc_, ddt_, x_, do_, g_, wT, wdtT):
        parts = zip((dz_, dxs_, db_, dc_), IN_ROWS.values())
        dh = sum(_mm(d_, wT[a:b], _NN) for d_, (a, b) in parts) + _mm(ddt_, wdtT, _NN)
        dx, dg = _rms_bwd(x_, g_, dh)
        return do_ + dx, dg

    dx, G["a_norm"] = rowwise("a_din", x0_bwd, [dz, dpre["xs"], dpre["B"], dpre["C"], ddt_pre, x, dx1],
                              [W["a_norm"], W["inT"], W["in_dtT"]], [(D, f32)], [(1, D)], comm=rs_in.chips())
    return loss, dx, G, [rs_f1, rs_att, rs_f0, rs_out, rs_in]


ROW_KEYS = ("inT", "a_out", "f_inT0", "f_down0", "w_kv", "w_q", "w_o", "f_inT1", "f_down1")


def _row_blocks(src):
    return {"inT": src["a_in_proj"][0].T, "a_out": src["a_out_proj"][0], "w_kv": src["w_kv"], "w_q": src["w_q"][0],
            "w_o": src["w_o"][0], "f_inT0": src["f_w_in"][0].T, "f_inT1": src["f_w_in"][1].T,
            "f_down0": src["f_w_down"][0], "f_down1": src["f_w_down"][1]}


def _from_row_blocks(rb):
    out = {"a_in_proj": rb["inT"].T[None], "a_out_proj": rb["a_out"][None], "w_kv": rb["w_kv"], "w_q": rb["w_q"][None],
           "w_o": rb["w_o"][None]}
    if "f_inT0" in rb:
        out["f_w_in"] = jnp.stack([rb["f_inT0"].T, rb["f_inT1"].T])
        out["f_w_down"] = jnp.stack([rb["f_down0"], rb["f_down1"]])
    return out


SMALL_SHARDED = ("a_norm", "a_conv_w", "a_conv_b", "a_gnorm", "f_conv_w")
REPLICATED = ("a_dt_bias", "a_A_log", "a_D", "kv_norm", "b_kv", "k_norm", "b_norm", "b_q", "q_norm", "sinks", "b_o",
              "f_norm", "f_conv_b")
ORDER = ("a_norm", "a_in_proj", "a_conv_w", "a_conv_b", "a_dt_bias", "a_A_log", "a_D", "a_gnorm", "a_out_proj", "kv_norm",
         "w_kv", "b_kv", "k_norm", "b_norm", "w_q", "b_q", "q_norm", "sinks", "w_o", "b_o", "f_norm", "f_w_in",
         "f_conv_w", "f_conv_b", "f_w_down")


def _gathered_to_whole(name, g):
    if name == "a_conv_w":
        return jnp.moveaxis(g[:, 0], 0, 1).reshape(g.shape[2], -1)
    if name in ("a_norm", "a_conv_b", "a_gnorm"):
        return g[:, 0].reshape(1, -1)
    if name == "f_conv_w":
        return jnp.moveaxis(g, 0, 2).reshape(g.shape[1], g.shape[2], -1)
    raise ValueError(name)


def _whole_to_shards(name, w):
    if name == "a_conv_w":
        return jnp.moveaxis(w.reshape(w.shape[0], N_DEV, -1), 1, 0)[:, None]
    if name in ("a_norm", "a_conv_b", "a_gnorm"):
        return w.reshape(N_DEV, 1, -1)
    if name == "f_conv_w":
        return jnp.moveaxis(w.reshape(w.shape[0], w.shape[1], N_DEV, -1), 2, 0)
    raise ValueError(name)


def _small_weights(whole):
    W = {}
    cw, cb = whole["a_conv_w"], whole["a_conv_b"]
    o = 0
    for k, n in (("xs", INNER), ("B", GROUPS * NSTATE), ("C", GROUPS * NSTATE)):
        W[f"cw_{k}"], W[f"cb_{k}"] = cw[:, o:o + n], cb[:, o:o + n]
        o += n
    W["a_norm"], W["a_gnorm"] = whole["a_norm"], whole["a_gnorm"]
    W["dt_bias"], W["A_log"], W["D"] = (_pad_lanes(whole[k]) for k in ("a_dt_bias", "a_A_log", "a_D"))
    W["kv_norm"], W["b_kv"], W["k_norm"] = whole["kv_norm"].reshape(1, -1), whole["b_kv"].reshape(1, -1), whole["k_norm"].reshape(1, -1)
    for k in ("b_norm", "b_q", "q_norm", "sinks", "b_o"):
        W[k] = whole[k]
    for i in range(2):
        W[f"f_norm{i}"] = whole["f_norm"][i:i + 1]
        W[f"f_cw{i}"], W[f"f_cb{i}"] = whole["f_conv_w"][i], whole["f_conv_b"][i:i + 1]
    return W


def _small_grads(G, shapes):
    nh = HEADS
    out = {
        "a_conv_w": jnp.concatenate([G["cw_xs"], G["cw_B"], G["cw_C"]], axis=1),
        "a_conv_b": jnp.concatenate([G["cb_xs"], G["cb_B"], G["cb_C"]], axis=1),
        "a_norm": G["a_norm"], "a_gnorm": G["a_gnorm"],
        "a_dt_bias": G["dt_bias"][:, :nh], "a_A_log": G["A_log"][:, :nh], "a_D": G["D"][:, :nh],
        "kv_norm": G["kv_norm"], "b_kv": G["b_kv"], "k_norm": G["k_norm"], "b_norm": G["b_norm"],
        "b_q": G["b_q"], "q_norm": G["q_norm"], "sinks": G["sinks"], "b_o": G["b_o"],
        "f_norm": jnp.concatenate([G["f_norm0"], G["f_norm1"]], axis=0),
        "f_conv_w": jnp.stack([G["f_conv_w0"], G["f_conv_w1"]]),
        "f_conv_b": jnp.concatenate([G["f_conv_b0"], G["f_conv_b1"]], axis=0),
    }
    return {k: val.reshape(shapes[k]) if k in shapes else val for k, val in out.items()}


def kernel(x, positions, a_norm, a_in_proj, a_conv_w, a_conv_b, a_dt_bias, a_A_log, a_D, a_gnorm, a_out_proj, kv_norm, w_kv, b_kv, k_norm, b_norm, w_q, b_q, q_norm, sinks, w_o, b_o, f_norm, f_w_in, f_conv_w, f_conv_b, f_w_down, loss_target, m_a_norm, m_a_in_proj, m_a_conv_w, m_a_conv_b, m_a_dt_bias, m_a_A_log, m_a_D, m_a_gnorm, m_a_out_proj, m_kv_norm, m_w_kv, m_b_kv, m_k_norm, m_b_norm, m_w_q, m_b_q, m_q_norm, m_sinks, m_w_o, m_b_o, m_f_norm, m_f_w_in, m_f_conv_w, m_f_conv_b, m_f_w_down, v_a_norm, v_a_in_proj, v_a_conv_w, v_a_conv_b, v_a_dt_bias, v_a_A_log, v_a_D, v_a_gnorm, v_a_out_proj, v_kv_norm, v_w_kv, v_b_kv, v_k_norm, v_b_norm, v_w_q, v_b_q, v_q_norm, v_sinks, v_w_o, v_b_o, v_f_norm, v_f_w_in, v_f_conv_w, v_f_conv_b, v_f_w_down):
    given = dict(locals())
    w_in = {n: given[n] for n in ORDER}
    m_in = {n: given["m_" + n] for n in ORDER}
    v_in = {n: given["v_" + n] for n in ORDER}
    dev = 4 * lax.axis_index("x") + 2 * lax.axis_index("y") + lax.axis_index("c")

    w2, m2, v2 = _row_blocks(w_in), _row_blocks(m_in), _row_blocks(v_in)
    small_pack = _pack([w_in[n] for n in SMALL_SHARDED], 8, f32)
    shards = {k: w2[k].astype(MXU) for k in ROW_KEYS}
    in_all, small_all = run_comm("ag_head", gather_comm([shards["inT"], small_pack]))
    whole = {n: w_in[n] for n in REPLICATED}
    for n, g in zip(SMALL_SHARDED, _unpack(small_all, [w_in[n].shape for n in SMALL_SHARDED], lead=(N_DEV,))):
        whole[n] = _gathered_to_whole(n, g)
    W = _small_weights(whole)
    W["inT"] = in_all.reshape(-1, D)
    W["in_dtT"] = jnp.pad(W["inT"][IN_COLS - HEADS:], ((0, LANES - HEADS), (0, 0)))

    loss, dx, G, scatters = _local_step(x[0], positions.reshape(S, 1).astype(f32), loss_target[0], W, shards)
    grads = _small_grads(G, {n: whole[n].shape for n in REPLICATED})

    small_names = SMALL_SHARDED + REPLICATED
    small_part = _pack([grads[n] for n in small_names] + [loss], 8, f32)
    small_gather = gather_comm([small_part])
    parts = {}
    for rs in scatters:
        parts.update(rs.parts())

    single = tuple(k for k in ROW_KEYS if not k.startswith("f_"))
    stepped = {k: final_adamw(f"adamw_{k}", *parts[k], w2[k], m2[k], v2[k]) for k in single}
    g_out, delta, new_m, new_v = (_from_row_blocks({k: stepped[k][i] for k in single}) for i in range(4))
    for n, key, lay in (("f_w_in", "f_inT", lambda a: jnp.swapaxes(a, 1, 2)), ("f_w_down", "f_down", lambda a: a)):
        res = final_adamw_layers(f"adamw_{n}", *parts[key + "0"], *parts[key + "1"], lay(w_in[n]), lay(m_in[n]), lay(v_in[n]),
                                 comm=small_gather if n == "f_w_in" else None)
        g_out[n], delta[n], new_m[n], new_v[n] = (lay(a) for a in res)
    *small_sums, loss_all = _unpack(sum_devices(small_gather.results[0]), [grads[n].shape for n in small_names] + [(1, 1)])
    for n, g in zip(small_names, small_sums):
        if n in SMALL_SHARDED:
            g_out[n] = lax.dynamic_index_in_dim(_whole_to_shards(n, g), dev, axis=0, keepdims=False)
        else:
            g_out[n] = g.reshape(w_in[n].shape)

    packs = [_pack([src[n] for n in small_names], 8, f32) for src in (w_in, g_out, m_in, v_in)]
    outs = adamw("adamw_small", *packs)
    for dst, buf in zip((delta, new_m, new_v), outs):
        for n, a in zip(small_names, _unpack(buf, [w_in[n].shape for n in small_names])):
            dst[n] = a

    return (loss_all[0, 0], dx[None], *[g_out[n] for n in ORDER], *[delta[n] for n in ORDER],
            *[new_m[n] for n in ORDER], *[new_v[n] for n in ORDER])
```

```python
import functools
import math

import jax
import jax.numpy as jnp
from jax import lax
from jax.experimental import pallas as pl
from jax.experimental.pallas import tpu as pltpu

f32 = jnp.float32
bf16 = jnp.bfloat16
MXU = bf16

N_DEV = 8
S = 2048
D = 1024
EPS = 1e-5
INNER = 2048
HEADS = 32
HP = 64
GROUPS = 8
HPG = HEADS // GROUPS
NSTATE = 128
CONV_A = 4
CHUNK = 256
NCHUNK = S // CHUNK
AH = 64
QH = 16
KVH = 4
QPK = QH // KVH
WIN = 128
NBLK = S // WIN
ROPE_THETA = 10000.0
FFN = 2816
CONV_F = 3
LANES = 128
PACK_W = 1024
VMEM_LIMIT = 56 * 1024 * 1024

ADAM_LR, ADAM_B1, ADAM_B2, ADAM_EPS, ADAM_WD, ADAM_STEP = 0.001, 0.9, 0.999, 1e-08, 0.01, 10

MESH = pl.DeviceIdType.MESH


def _cparams(sem=None):
    return pltpu.CompilerParams(dimension_semantics=sem, vmem_limit_bytes=VMEM_LIMIT)


def _pick(n, cands):
    for c in cands:
        if n % c == 0:
            return c
    return n


class Comm:
    def __init__(self, ins, out_shapes, sems, start, finish):
        self.ins, self.out_shapes, self.sems, self.start, self.finish = list(ins), list(out_shapes), list(sems), start, finish
        self.results, self.children = None, ()

    def set_results(self, res):
        self.results, o = list(res), 0
        for ch in self.children:
            ch.set_results(res[o:o + len(ch.out_shapes)])
            o += len(ch.out_shapes)


def merge_comms(comms):
    def each(fn_name, ins, outs, sems):
        i = o = s = 0
        for c in comms:
            getattr(c, fn_name)(ins[i:i + len(c.ins)], outs[o:o + len(c.out_shapes)], sems[s:s + len(c.sems)])
            i, o, s = i + len(c.ins), o + len(c.out_shapes), s + len(c.sems)

    merged = Comm([a for c in comms for a in c.ins], [a for c in comms for a in c.out_shapes], [a for c in comms for a in c.sems],
                  functools.partial(each, "start"), functools.partial(each, "finish"))
    merged.children = tuple(comms)
    return merged


def _call(body, args, *, name, grid, in_specs, out_specs, out_shape, scratch=(), sem=None, comm=None):
    if comm is None:
        return pl.pallas_call(body, name=name, grid=grid, in_specs=list(in_specs), out_specs=list(out_specs),
                              out_shape=list(out_shape), scratch_shapes=list(scratch), compiler_params=_cparams(sem))(*args)
    n_in, n_out, n_scr, c_in, c_out = len(in_specs), len(out_shape), len(scratch), len(comm.ins), len(comm.out_shapes)
    any_spec = pl.BlockSpec(memory_space=pl.ANY)

    def outer(*refs):
        ins, c_ins = refs[:n_in], refs[n_in:n_in + c_in]
        o = n_in + c_in
        outs, c_outs = refs[o:o + n_out], refs[o + n_out:o + n_out + c_out]
        o += n_out + c_out
        scr, c_sems = refs[o:o + n_scr], refs[o + n_scr:]
        ids = [pl.program_id(i) for i in range(len(grid))]
        first = functools.reduce(jnp.logical_and, [i == 0 for i in ids])
        last = functools.reduce(jnp.logical_and, [i == g - 1 for i, g in zip(ids, grid)])

        @pl.when(first)
        def _():
            comm.start(c_ins, c_outs, c_sems)

        body(*ins, *outs, *scr)

        @pl.when(last)
        def _():
            comm.finish(c_ins, c_outs, c_sems)

    res = pl.pallas_call(
        outer, name=name, grid=grid, in_specs=list(in_specs) + [any_spec] * c_in,
        out_specs=list(out_specs) + [any_spec] * c_out, out_shape=list(out_shape) + comm.out_shapes,
        scratch_shapes=list(scratch) + comm.sems, compiler_params=_cparams(("arbitrary",) * len(grid)),
    )(*args, *comm.ins)
    comm.set_results(res[n_out:])
    return res[:n_out]


def matmul(name, a, b, mode, out_dtype=f32, bias=None, residual=None):
    if mode == "nn":
        (M, K), (K2, N) = a.shape, b.shape
    elif mode == "nt":
        (M, K), (N, K2) = a.shape, b.shape
    else:
        (K, M), (K2, N) = a.shape, b.shape
    assert K == K2, (name, a.shape, b.shape)
    if mode == "tn":
        tm, tn = M, _pick(N, (512, 256, 128) if M <= 1024 else (256, 128))
        a_spec = pl.BlockSpec((K, M), lambda j: (0, 0))
        b_spec = pl.BlockSpec((K, tn), lambda j: (0, j))
        dims = (((0,), (0,)), ((), ()))
        grid, o_map, row_map = (N // tn,), (lambda j: (0, j)), (lambda j: (0, j))
    else:
        tm, tn = (256 if N >= 2048 else 512), N
        a_spec = pl.BlockSpec((tm, K), lambda i: (i, 0))
        b_spec = pl.BlockSpec(b.shape, lambda i: (0, 0))
        dims = (((1,), (0,)), ((), ())) if mode == "nn" else (((1,), (1,)), ((), ()))
        grid, o_map, row_map = (M // tm,), (lambda i: (i, 0)), (lambda i: (0, 0))
    ins, in_specs = [a, b], [a_spec, b_spec]
    if bias is not None:
        ins.append(bias)
        in_specs.append(pl.BlockSpec((1, tn), row_map))
    if residual is not None:
        ins.append(residual)
        in_specs.append(pl.BlockSpec((tm, tn), o_map))
    has_bias, has_res = bias is not None, residual is not None

    def body(a_ref, b_ref, *rest):
        rest = list(rest)
        bias_ref = rest.pop(0) if has_bias else None
        res_ref = rest.pop(0) if has_res else None
        (o_ref,) = rest
        r = lax.dot_general(a_ref[...].astype(MXU), b_ref[...].astype(MXU), dims, preferred_element_type=f32)
        if has_bias:
            r = r + bias_ref[...]
        if has_res:
            r = r + res_ref[...]
        o_ref[...] = r.astype(out_dtype)

    return pl.pallas_call(
        body, name=name, grid=grid, in_specs=in_specs,
        out_specs=pl.BlockSpec((tm, tn), o_map),
        out_shape=jax.ShapeDtypeStruct((M, N), out_dtype),
        compiler_params=_cparams(("parallel",)),
    )(*ins)


def matmul_tn_stacked(name, a, b, out_dtype):
    R, K, M = a.shape
    N = b.shape[1]
    tn = _pick(N, (256, 128))

    def body(a_ref, b_ref, o_ref):
        o_ref[0] = lax.dot_general(a_ref[0].astype(MXU), b_ref[...].astype(MXU), (((0,), (0,)), ((), ())),
                                   preferred_element_type=f32).astype(out_dtype)

    out = pl.pallas_call(
        body, name=name, grid=(R, N // tn),
        in_specs=[pl.BlockSpec((1, K, M), lambda r, j: (r, 0, 0)), pl.BlockSpec((K, tn), lambda r, j: (0, j))],
        out_specs=pl.BlockSpec((1, M, tn), lambda r, j: (r, 0, j)),
        out_shape=jax.ShapeDtypeStruct((R, M, N), out_dtype),
        compiler_params=_cparams(("parallel", "parallel")),
    )(a, b)
    return out.reshape(R * M, N)


def rowwise(name, fn, rows, pars, outs, accs=(), tile=256, comm=None):
    n_in, n_out = len(rows) + len(pars), len(outs)
    in_specs = [pl.BlockSpec((None, tile, r[0].shape[2]), functools.partial(lambda i, lead: (lead, i, 0), lead=r[1]))
                if isinstance(r, tuple) else pl.BlockSpec((tile, r.shape[1]), lambda i: (i, 0)) for r in rows]
    rows = [r[0] if isinstance(r, tuple) else r for r in rows]
    in_specs += [pl.BlockSpec(p.shape, lambda i: (0, 0)) for p in pars]
    out_specs = [pl.BlockSpec((tile, c), lambda i: (i, 0)) for c, _ in outs]
    out_specs += [pl.BlockSpec(shp, lambda i: (0, 0)) for shp in accs]
    out_shape = [jax.ShapeDtypeStruct((S, c), dt) for c, dt in outs]
    out_shape += [jax.ShapeDtypeStruct(shp, f32) for shp in accs]

    def body(*refs):
        res = fn(*[r[...] for r in refs[:n_in]])
        o_refs = refs[n_in:n_in + n_out]
        a_refs = refs[n_in + n_out:]
        for ref, val in zip(o_refs, res[:n_out]):
            ref[...] = val.astype(ref.dtype)
        if a_refs:
            @pl.when(pl.program_id(0) == 0)
            def _():
                for ref in a_refs:
                    ref[...] = jnp.zeros_like(ref)
            for ref, val in zip(a_refs, res[n_out:]):
                ref[...] += val

    return _call(body, [*rows, *pars], name=name, grid=(S // tile,), in_specs=in_specs, out_specs=out_specs,
                 out_shape=out_shape, sem=("arbitrary",) if accs else ("parallel",), comm=comm)


def _sigmoid(x):
    return 0.5 * jnp.tanh(0.5 * x) + 0.5


def _silu(x):
    return x * _sigmoid(x)


def _dsilu(x):
    sg = _sigmoid(x)
    return sg * (1.0 + x * (1.0 - sg))


def _softplus(x):
    return jnp.maximum(x, 0.0) + jnp.log(1.0 + jnp.exp(-jnp.abs(x)))


def _rms_fwd(x, g):
    r = lax.rsqrt(jnp.mean(x * x, axis=-1, keepdims=True) + EPS)
    return x * r * g


def _rms_bwd(x, g, dh):
    r = lax.rsqrt(jnp.mean(x * x, axis=-1, keepdims=True) + EPS)
    xh = x * r
    dxh = dh * g
    dx = r * (dxh - xh * jnp.mean(dxh * xh, axis=-1, keepdims=True))
    return dx, jnp.sum(dh * xh, axis=0, keepdims=True)


def _taps(x, width):
    row = lax.broadcasted_iota(jnp.int32, (8, x.shape[1]), 0)

    def shifted(s):
        r = pltpu.roll(x, s, 0)
        return jnp.concatenate([jnp.where(row >= s, r[:8], 0.0), r[8:]], axis=0)

    return [shifted(s) for s in range(width - 1, 0, -1)] + [x]


def _conv(x, w, b, taps=None):
    width = w.shape[0]
    taps = _taps(x, width) if taps is None else taps
    out = b + w[0:1, :] * taps[0]
    for k in range(1, width):
        out = out + w[k:k + 1, :] * taps[k]
    return out


def _conv_bwd(x, w, dc, taps=None):
    width, n = w.shape[0], x.shape[0]
    taps = _taps(x, width) if taps is None else taps
    row = lax.broadcasted_iota(jnp.int32, (8, x.shape[1]), 0)
    dx = w[width - 1:width, :] * dc
    for k in range(width - 1):
        s = width - 1 - k
        r = pltpu.roll(dc, n - s, 0)
        dx = dx + w[k:k + 1, :] * jnp.concatenate([r[:n - 8], jnp.where(row < 8 - s, r[n - 8:], 0.0)], axis=0)
    dw = jnp.concatenate([jnp.sum(dc * t, axis=0, keepdims=True) for t in taps], axis=0)
    return dx, dw, jnp.sum(dc, axis=0, keepdims=True)


def _rope_tables(pos, inv_freq):
    ang = pos * inv_freq
    return jnp.cos(ang), jnp.sin(ang)


def _split2(v):
    hi = v.astype(bf16)
    return hi, (v - hi.astype(f32)).astype(bf16)


def _head_maps(width):
    shift = AH.bit_length() - 1
    to_head = (lax.broadcasted_iota(jnp.int32, (width, LANES), 0) >> shift) == lax.broadcasted_iota(jnp.int32, (width, LANES), 1)
    from_head = lax.broadcasted_iota(jnp.int32, (LANES, width), 0) == (lax.broadcasted_iota(jnp.int32, (LANES, width), 1) >> shift)
    return to_head.astype(bf16), from_head.astype(bf16)


def _head_sums(v, to_head):
    hi, lo = _split2(v)
    return jnp.dot(hi, to_head, preferred_element_type=f32) + jnp.dot(lo, to_head, preferred_element_type=f32)


def _head_spread(s, from_head):
    hi, lo = _split2(s)
    return jnp.dot(hi, from_head, preferred_element_type=f32) + jnp.dot(lo, from_head, preferred_element_type=f32)


def _rope_full(cos, sin, width):
    half = AH // 2
    pad = jnp.zeros((cos.shape[0], LANES - half), f32)
    r = lax.broadcasted_iota(jnp.int32, (LANES, width), 0)
    lane = lax.broadcasted_iota(jnp.int32, (LANES, width), 1)
    spread = ((lane & (half - 1)) == r).astype(bf16)
    full = lambda t: _head_spread(jnp.concatenate([t, pad], axis=1), spread)
    first = (lax.broadcasted_iota(jnp.int32, (1, width), 1) & (AH - 1)) < half
    sin_f = full(sin)
    return full(cos), jnp.where(first, -sin_f, sin_f), first


def _swap_halves(v, first):
    half, width = AH // 2, v.shape[1]
    return jnp.where(first, pltpu.roll(v, width - half, 1), pltpu.roll(v, half, 1))


def _headnorm_rope_fwd(x, g, cos, sin, heads):
    to_head, from_head = _head_maps(heads * AH)
    cos_f, sin_s, first = _rope_full(cos, sin, heads * AH)
    r = _head_spread(lax.rsqrt(_head_sums(x * x, to_head) * (1.0 / AH) + EPS), from_head)
    n = x * r * jnp.tile(g, (1, heads))
    return n * cos_f + _swap_halves(n, first) * sin_s


def _headnorm_rope_bwd(x, g, cos, sin, dout, heads):
    width = heads * AH
    to_head, from_head = _head_maps(width)
    cos_f, sin_s, first = _rope_full(cos, sin, width)
    r = _head_spread(lax.rsqrt(_head_sums(x * x, to_head) * (1.0 / AH) + EPS), from_head)
    xh = x * r
    dn = dout * cos_f - _swap_halves(dout, first) * sin_s
    dxh = dn * jnp.tile(g, (1, heads))
    m = _head_spread(_head_sums(dxh * xh, to_head) * (1.0 / AH), from_head)
    dx = r * (dxh - xh * m)
    dg_lanes = jnp.sum(dn * xh, axis=0, keepdims=True)
    fold = ((lax.broadcasted_iota(jnp.int32, (width, LANES), 0) & (AH - 1))
            == lax.broadcasted_iota(jnp.int32, (width, LANES), 1)).astype(f32)
    dg = jnp.dot(jnp.broadcast_to(dg_lanes, (8, width)), fold, precision=lax.Precision.HIGHEST, preferred_element_type=f32)
    return dx, dg[0:1, :AH]


def _ssd_prep(dt_pre, dt_bias, a_log, dt_s, acum_s, acumT_s):
    dt = _softplus(dt_pre + dt_bias)
    a = dt * (-jnp.exp(a_log))
    row = lax.broadcasted_iota(jnp.int32, (CHUNK, CHUNK), 0)
    col = lax.broadcasted_iota(jnp.int32, (CHUNK, CHUNK), 1)
    dt_s[...] = dt
    acum_s[...] = jnp.dot((col <= row).astype(f32), a, precision=lax.Precision.HIGHEST, preferred_element_type=f32)
    acumT_s[...] = lax.dot_general(a, (row <= col).astype(f32), (((0,), (0,)), ((), ())),
                                   precision=lax.Precision.HIGHEST, preferred_element_type=f32)


def _head_cols(h, dt_s, acum_s, acumT_s):
    lane = lax.broadcasted_iota(jnp.int32, (1, LANES), 1)
    oh_l = (lane == h).astype(f32)
    sub = lax.broadcasted_iota(jnp.int32, (LANES, 1), 0)
    oh_s = (sub == h).astype(f32)
    dt_h = jnp.sum(dt_s[...] * oh_l, axis=1, keepdims=True)
    ac_h = jnp.sum(acum_s[...] * oh_l, axis=1, keepdims=True)
    acr_h = jnp.sum(acumT_s[...] * oh_s, axis=0, keepdims=True)
    return oh_l, dt_h, ac_h, acr_h


def ssd_fwd(xs, Bm, Cm, dt_pre, dt_bias, a_log, d_skip, comm=None):
    def body(xs_ref, b_ref, c_ref, dtp_ref, bias_ref, alog_ref, d_ref, y_ref, st_ref, state, dt_s, acum_s, acumT_s):
        c, g = pl.program_id(0), pl.program_id(1)

        @pl.when(g == 0)
        def _():
            _ssd_prep(dtp_ref[...], bias_ref[...], alog_ref[...], dt_s, acum_s, acumT_s)

        row = lax.broadcasted_iota(jnp.int32, (CHUNK, CHUNK), 0)
        col = lax.broadcasted_iota(jnp.int32, (CHUNK, CHUNK), 1)
        causal = col <= row
        Bb, Cb = b_ref[...], c_ref[...]
        cb = lax.dot_general(Cb.astype(MXU), Bb.astype(MXU), (((1,), (1,)), ((), ())), preferred_element_type=f32)
        xs_blk = xs_ref[...]

        @pl.when(c == 0)
        def _():
            for j in range(HPG):
                state[g * HPG + j] = jnp.zeros((NSTATE, HP), f32)

        prevs = [state[g * HPG + j] for j in range(HPG)]
        y_off_all = jnp.dot(Cb.astype(MXU), jnp.concatenate(prevs, axis=1).astype(MXU), preferred_element_type=f32)
        ys, xds, e_ends = [], [], []
        for j in range(HPG):
            oh_l, dt_h, ac_h, acr_h = _head_cols(g * HPG + j, dt_s, acum_s, acumT_s)
            decay = jnp.exp(jnp.where(causal, ac_h - acr_h, -1e30))
            w = (cb * decay).astype(MXU)
            xs_h = xs_blk[:, HP * j:HP * (j + 1)]
            xd = xs_h * dt_h
            y_diag = jnp.dot(w, xd.astype(MXU), preferred_element_type=f32)
            y_off = y_off_all[:, HP * j:HP * (j + 1)] * jnp.exp(ac_h)
            d_h = jnp.sum(d_ref[...] * oh_l, axis=1, keepdims=True)
            ys.append(y_diag + y_off + xs_h * d_h)
            a_end = ac_h[CHUNK - 1:CHUNK, :]
            xds.append(xd * jnp.exp(a_end - ac_h))
            e_ends.append(jnp.exp(a_end))
        s_c = lax.dot_general(Bb.astype(MXU), jnp.concatenate(xds, axis=1).astype(MXU), (((0,), (0,)), ((), ())),
                              preferred_element_type=f32)
        for j in range(HPG):
            st_ref[0, j] = prevs[j]
            state[g * HPG + j] = prevs[j] * e_ends[j] + s_c[:, HP * j:HP * (j + 1)]
        y_ref[...] = jnp.concatenate(ys, axis=1)

    par = pl.BlockSpec((1, LANES), lambda c, g: (0, 0))
    return _call(
        body, [xs, Bm, Cm, dt_pre, dt_bias, a_log, d_skip], comm=comm, name="ssd_fwd", grid=(NCHUNK, GROUPS),
        in_specs=[pl.BlockSpec((CHUNK, HPG * HP), lambda c, g: (c, g)),
                  pl.BlockSpec((CHUNK, NSTATE), lambda c, g: (c, g)),
                  pl.BlockSpec((CHUNK, NSTATE), lambda c, g: (c, g)),
                  pl.BlockSpec((CHUNK, LANES), lambda c, g: (c, 0)), par, par, par],
        out_specs=[pl.BlockSpec((CHUNK, HPG * HP), lambda c, g: (c, g)),
                   pl.BlockSpec((1, HPG, NSTATE, HP), lambda c, g: (c, g, 0, 0))],
        out_shape=[jax.ShapeDtypeStruct((S, INNER), f32), jax.ShapeDtypeStruct((NCHUNK, HEADS, NSTATE, HP), f32)],
        scratch=[pltpu.VMEM((HEADS, NSTATE, HP), f32), pltpu.VMEM((CHUNK, LANES), f32),
                 pltpu.VMEM((CHUNK, LANES), f32), pltpu.VMEM((LANES, CHUNK), f32)],
        sem=("arbitrary", "arbitrary"))


def ssd_bwd(xs, Bm, Cm, dt_pre, dt_bias, a_log, d_skip, states, dy, comm=None):
    rev = lambda c: NCHUNK - 1 - c

    def body(xs_ref, b_ref, c_ref, dtp_ref, bias_ref, alog_ref, d_ref, st_ref, dy_ref,
             dxs_ref, db_ref, dc_ref, ddt_ref, dbias_ref, dalog_ref, dd_ref,
             dstate, dt_s, acum_s, acumT_s, dacum_s, ddt_s, da_s):
        c, g = pl.program_id(0), pl.program_id(1)

        @pl.when(g == 0)
        def _():
            _ssd_prep(dtp_ref[...], bias_ref[...], alog_ref[...], dt_s, acum_s, acumT_s)
            dacum_s[...] = jnp.zeros_like(dacum_s)
            ddt_s[...] = jnp.zeros_like(ddt_s)

        @pl.when((c == 0) & (g == 0))
        def _():
            da_s[...] = jnp.zeros_like(da_s)
            dd_ref[...] = jnp.zeros_like(dd_ref)
            dbias_ref[...] = jnp.zeros_like(dbias_ref)
            dalog_ref[...] = jnp.zeros_like(dalog_ref)

        row = lax.broadcasted_iota(jnp.int32, (CHUNK, CHUNK), 0)
        col = lax.broadcasted_iota(jnp.int32, (CHUNK, CHUNK), 1)
        sub_l = lax.broadcasted_iota(jnp.int32, (CHUNK, 1), 0)
        last = (sub_l == CHUNK - 1).astype(f32)
        nt = (((1,), (1,)), ((), ()))
        tn = (((0,), (0,)), ((), ()))
        Bb, Cb = b_ref[...], c_ref[...]
        Bm_, Cm_ = Bb.astype(MXU), Cb.astype(MXU)
        cb = lax.dot_general(Cm_, Bm_, nt, preferred_element_type=f32)
        bc = lax.dot_general(Bm_, Cm_, nt, preferred_element_type=f32)
        xs_blk, dy_blk = xs_ref[...], dy_ref[...]
        dxs, dB, dC = [], jnp.zeros((CHUNK, NSTATE), f32), jnp.zeros((CHUNK, NSTATE), f32)
        for j in range(HPG):
            h = g * HPG + j
            oh_l, dt_h, ac_h, acr_h = _head_cols(h, dt_s, acum_s, acumT_s)

            @pl.when(c == 0)
            def _():
                dstate[h] = jnp.zeros((NSTATE, HP), f32)

            dnext = dstate[h]
            prev = st_ref[0, j]
            lm = jnp.exp(jnp.where(col <= row, ac_h - acr_h, -1e30))
            lmT = jnp.exp(jnp.where(row <= col, acr_h - ac_h, -1e30))
            xs_h = xs_blk[:, HP * j:HP * (j + 1)]
            dy_h = dy_blk[:, HP * j:HP * (j + 1)]
            xd = xs_h * dt_h
            xdm, dym = xd.astype(MXU), dy_h.astype(MXU)
            ea = jnp.exp(ac_h)
            a_end = ac_h[CHUNK - 1:CHUNK, :]
            e_end = jnp.exp(a_end)
            dte = jnp.exp(a_end - ac_h)
            dnm, pvm = dnext.astype(MXU), prev.astype(MXU)
            bd = jnp.dot(Bm_, dnm, preferred_element_type=f32)
            dxd = jnp.dot((bc * lmT).astype(MXU), dym, preferred_element_type=f32) + dte * bd
            dw = lax.dot_general(dym, xdm, nt, preferred_element_type=f32)
            dwT = lax.dot_general(xdm, dym, nt, preferred_element_type=f32)
            dcb = dw * lm
            dbc = dwT * lmT
            eady = (ea * dy_h).astype(MXU)
            dC = dC + jnp.dot(dcb.astype(MXU), Bm_, preferred_element_type=f32) \
                + lax.dot_general(eady, pvm, nt, preferred_element_type=f32)
            dB = dB + jnp.dot(dbc.astype(MXU), Cm_, preferred_element_type=f32) \
                + dte * lax.dot_general(xdm, dnm, nt, preferred_element_type=f32)
            dstate[h] = lax.dot_general(Cm_, eady, tn, preferred_element_type=f32) + e_end * dnext
            r1 = jnp.sum(dcb * cb, axis=1, keepdims=True)
            r2 = jnp.sum(dbc * bc, axis=1, keepdims=True)
            y_off = jnp.dot(Cm_, pvm, preferred_element_type=f32) * ea
            t3 = jnp.sum(dy_h * y_off, axis=1, keepdims=True)
            t4 = jnp.sum(bd * xd, axis=1, keepdims=True) * dte
            end_extra = jnp.sum(t4, axis=0, keepdims=True) + e_end * jnp.sum(jnp.sum(prev * dnext, axis=1, keepdims=True), axis=0, keepdims=True)
            dacum_h = r1 - r2 + t3 - t4 + last * end_extra
            dacum_s[...] += dacum_h * oh_l
            ddt_s[...] += jnp.sum(dxd * xs_h, axis=1, keepdims=True) * oh_l
            d_h = jnp.sum(d_ref[...] * oh_l, axis=1, keepdims=True)
            dxs.append(dxd * dt_h + dy_h * d_h)
            dd_ref[...] += oh_l * jnp.sum(jnp.sum(dy_h * xs_h, axis=1, keepdims=True), axis=0, keepdims=True)
        dxs_ref[...] = jnp.concatenate(dxs, axis=1)
        db_ref[...] = dB
        dc_ref[...] = dC

        @pl.when(g == GROUPS - 1)
        def _():
            a_row = -jnp.exp(alog_ref[...])
            da = jnp.dot((row <= col).astype(f32), dacum_s[...], precision=lax.Precision.HIGHEST, preferred_element_type=f32)
            da_s[...] += jnp.sum(da * dt_s[...], axis=0, keepdims=True)
            z = dtp_ref[...] + bias_ref[...]
            ddt_pre = (ddt_s[...] + da * a_row) * _sigmoid(z)
            ddt_ref[...] = ddt_pre.astype(ddt_ref.dtype)
            dbias_ref[...] += jnp.sum(ddt_pre, axis=0, keepdims=True)

            @pl.when(c == NCHUNK - 1)
            def _():
                dalog_ref[...] = da_s[...] * a_row

    par = pl.BlockSpec((1, LANES), lambda c, g: (0, 0))
    return _call(
        body, [xs, Bm, Cm, dt_pre, dt_bias, a_log, d_skip, states, dy], comm=comm, name="ssd_bwd", grid=(NCHUNK, GROUPS),
        in_specs=[pl.BlockSpec((CHUNK, HPG * HP), lambda c, g: (rev(c), g)),
                  pl.BlockSpec((CHUNK, NSTATE), lambda c, g: (rev(c), g)),
                  pl.BlockSpec((CHUNK, NSTATE), lambda c, g: (rev(c), g)),
                  pl.BlockSpec((CHUNK, LANES), lambda c, g: (rev(c), 0)), par, par, par,
                  pl.BlockSpec((1, HPG, NSTATE, HP), lambda c, g: (rev(c), g, 0, 0)),
                  pl.BlockSpec((CHUNK, HPG * HP), lambda c, g: (rev(c), g))],
        out_specs=[pl.BlockSpec((CHUNK, HPG * HP), lambda c, g: (rev(c), g)),
                   pl.BlockSpec((CHUNK, NSTATE), lambda c, g: (rev(c), g)),
                   pl.BlockSpec((CHUNK, NSTATE), lambda c, g: (rev(c), g)),
                   pl.BlockSpec((CHUNK, LANES), lambda c, g: (rev(c), 0)), par, par, par],
        out_shape=[jax.ShapeDtypeStruct((S, INNER), f32), jax.ShapeDtypeStruct((S, GROUPS * NSTATE), f32),
                   jax.ShapeDtypeStruct((S, GROUPS * NSTATE), f32), jax.ShapeDtypeStruct((S, LANES), MXU),
                   jax.ShapeDtypeStruct((1, LANES), f32), jax.ShapeDtypeStruct((1, LANES), f32),
                   jax.ShapeDtypeStruct((1, LANES), f32)],
        scratch=[pltpu.VMEM((HEADS, NSTATE, HP), f32), pltpu.VMEM((CHUNK, LANES), f32),
                 pltpu.VMEM((CHUNK, LANES), f32), pltpu.VMEM((LANES, CHUNK), f32),
                 pltpu.VMEM((CHUNK, LANES), f32), pltpu.VMEM((CHUNK, LANES), f32), pltpu.VMEM((1, LANES), f32)],
        sem=("arbitrary", "arbitrary"))


ATT_STACK_FWD, ATT_STACK_BWD = 4, 2


def _attn_kv(kp, kc, vp, vc, hk):
    sl = slice(AH * hk, AH * (hk + 1))
    return (jnp.concatenate([kp[:, sl], kc[:, sl]], axis=0).astype(MXU),
            jnp.concatenate([vp[:, sl], vc[:, sl]], axis=0).astype(MXU))


def _stack_heads(x, heads):
    return jnp.concatenate([x[:, AH * h:AH * (h + 1)] for h in heads], axis=0)


def _attn_block(n, q, kb, sinks, heads):
    rows = len(heads) * WIN
    qi = lax.broadcasted_iota(jnp.int32, (rows, 2 * WIN), 0) & (WIN - 1)
    ki = lax.broadcasted_iota(jnp.int32, (rows, 2 * WIN), 1)
    rel = qi + WIN - ki
    mask = (rel >= 0) & (rel < WIN) & ((ki >= WIN) | (n > 0))
    qg = _stack_heads(q, heads).astype(MXU)
    s = lax.dot_general(qg, kb, (((1,), (1,)), ((), ())), preferred_element_type=f32) * (AH ** -0.5)
    s = jnp.where(mask, s, -1e30)
    sink = jnp.concatenate([jnp.broadcast_to(sinks[:, h:h + 1], (WIN, 1)) for h in heads], axis=0)
    m = jnp.maximum(jnp.max(s, axis=1, keepdims=True), sink)
    p = jnp.exp(s - m)
    ps = jnp.exp(sink - m)
    inv = 1.0 / (jnp.sum(p, axis=1, keepdims=True) + ps)
    return qg, p * inv, ps * inv


def _head_blocks(hk, stack):
    return [list(range(QPK * hk + i, QPK * hk + i + stack)) for i in range(0, QPK, stack)]


def _kv_specs():
    prev = lambda n: (jnp.maximum(n - 1, 0), 0)
    cur = lambda n: (n, 0)
    w = KVH * AH
    return [pl.BlockSpec((WIN, w), prev), pl.BlockSpec((WIN, w), cur), pl.BlockSpec((WIN, w), prev), pl.BlockSpec((WIN, w), cur)]


def attn_fwd(q, k, v, sinks, comm=None):
    def body(q_ref, kp_ref, kc_ref, vp_ref, vc_ref, s_ref, o_ref):
        n = pl.program_id(0)
        q_, kp, kc, vp, vc, sk = q_ref[...], kp_ref[...], kc_ref[...], vp_ref[...], vc_ref[...], s_ref[...]
        outs = []
        for hk in range(KVH):
            kb, vb = _attn_kv(kp, kc, vp, vc, hk)
            for heads in _head_blocks(hk, ATT_STACK_FWD):
                _, pr, _ = _attn_block(n, q_, kb, sk, heads)
                o = jnp.dot(pr.astype(MXU), vb, preferred_element_type=f32)
                outs += [o[WIN * i:WIN * (i + 1)] for i in range(len(heads))]
        o_ref[...] = jnp.concatenate(outs, axis=1)

    return _call(
        body, [q, k, k, v, v, sinks], comm=comm, name="attn_fwd", grid=(NBLK,),
        in_specs=[pl.BlockSpec((WIN, D), lambda n: (n, 0))] + _kv_specs() + [pl.BlockSpec((1, QH), lambda n: (0, 0))],
        out_specs=[pl.BlockSpec((WIN, D), lambda n: (n, 0))],
        out_shape=[jax.ShapeDtypeStruct((S, D), f32)], sem=("parallel",))[0]


def attn_bwd(q, k, v, sinks, dout, comm=None):
    def body(q_ref, kp_ref, kc_ref, vp_ref, vc_ref, s_ref, do_ref, dq_ref, dkp_ref, dkc_ref, dvp_ref, dvc_ref, ds_ref):
        n = pl.program_id(0)

        @pl.when(n == 0)
        def _():
            ds_ref[...] = jnp.zeros_like(ds_ref)

        q_, kp, kc, vp, vc, sk, do = q_ref[...], kp_ref[...], kc_ref[...], vp_ref[...], vc_ref[...], s_ref[...], do_ref[...]
        lane = lax.broadcasted_iota(jnp.int32, (1, QH), 1)
        nt = (((1,), (1,)), ((), ()))
        tn = (((0,), (0,)), ((), ()))
        dqs, dkps, dkcs, dvps, dvcs = [], [], [], [], []
        dsink = jnp.zeros((1, QH), f32)
        for hk in range(KVH):
            kb, vb = _attn_kv(kp, kc, vp, vc, hk)
            dkb, dvb = jnp.zeros((2 * WIN, AH), f32), jnp.zeros((2 * WIN, AH), f32)
            for heads in _head_blocks(hk, ATT_STACK_BWD):
                qg, pr, prs = _attn_block(n, q_, kb, sk, heads)
                dog = _stack_heads(do, heads).astype(MXU)
                dp = lax.dot_general(dog, vb, nt, preferred_element_type=f32)
                dvb = dvb + lax.dot_general(pr.astype(MXU), dog, tn, preferred_element_type=f32)
                delta = jnp.sum(pr * dp, axis=1, keepdims=True)
                ds = (pr * (dp - delta)).astype(MXU)
                dsk = -prs * delta
                for i, h in enumerate(heads):
                    dsink = dsink + jnp.sum(dsk[WIN * i:WIN * (i + 1)], axis=0, keepdims=True) * (lane == h).astype(f32)
                dqg = jnp.dot(ds, kb, preferred_element_type=f32) * (AH ** -0.5)
                dkb = dkb + lax.dot_general(ds, qg, tn, preferred_element_type=f32) * (AH ** -0.5)
                dqs += [dqg[WIN * i:WIN * (i + 1)] for i in range(len(heads))]
            dkps.append(dkb[:WIN])
            dkcs.append(dkb[WIN:])
            dvps.append(dvb[:WIN])
            dvcs.append(dvb[WIN:])
        dq_ref[...] = jnp.concatenate(dqs, axis=1)
        dkp_ref[...] = jnp.concatenate(dkps, axis=1)
        dkc_ref[...] = jnp.concatenate(dkcs, axis=1)
        dvp_ref[...] = jnp.concatenate(dvps, axis=1)
        dvc_ref[...] = jnp.concatenate(dvcs, axis=1)
        ds_ref[...] += dsink

    w = KVH * AH
    blk = lambda width: pl.BlockSpec((WIN, width), lambda n: (n, 0))
    return _call(
        body, [q, k, k, v, v, sinks, dout], comm=comm, name="attn_bwd", grid=(NBLK,),
        in_specs=[blk(D)] + _kv_specs() + [pl.BlockSpec((1, QH), lambda n: (0, 0)), blk(D)],
        out_specs=[blk(D), blk(w), blk(w), blk(w), blk(w), pl.BlockSpec((1, QH), lambda n: (0, 0))],
        out_shape=[jax.ShapeDtypeStruct((S, D), f32)] + [jax.ShapeDtypeStruct((S, w), f32)] * 4 + [jax.ShapeDtypeStruct((1, QH), f32)],
        sem=("arbitrary",))


def kv_bwd(kv, pos, inv_freq, k_norm, dkp, dkc, dvp, dvc):
    w = KVH * AH

    def body(kv_ref, pos_ref, if_ref, g_ref, dkp_ref, dkc_ref, dvp_ref, dvc_ref, o_ref, dg_ref, db_ref):
        n = pl.program_id(0)

        @pl.when(n == 0)
        def _():
            dg_ref[...] = jnp.zeros_like(dg_ref)
            db_ref[...] = jnp.zeros_like(db_ref)

        inside = (n < NBLK - 1).astype(f32)
        dk = dkc_ref[...] + inside * dkp_ref[...]
        dv = dvc_ref[...] + inside * dvp_ref[...]
        cos, sin = _rope_tables(pos_ref[...], if_ref[...])
        dkpre, dg = _headnorm_rope_bwd(kv_ref[...], g_ref[...], cos, sin, dk, KVH)
        dkv = jnp.concatenate([dkpre, dv], axis=1)
        o_ref[...] = dkv.astype(o_ref.dtype)
        dg_ref[...] += dg
        db_ref[...] += jnp.sum(dkv, axis=0, keepdims=True)

    nxt = lambda n: (jnp.minimum(n + 1, NBLK - 1), 0)
    cur = lambda n: (n, 0)
    const = lambda n: (0, 0)
    return pl.pallas_call(
        body, name="kv_bwd", grid=(NBLK,),
        in_specs=[pl.BlockSpec((WIN, w), cur), pl.BlockSpec((WIN, 1), cur), pl.BlockSpec((1, AH // 2), const),
                  pl.BlockSpec((1, AH), const), pl.BlockSpec((WIN, w), nxt), pl.BlockSpec((WIN, w), cur),
                  pl.BlockSpec((WIN, w), nxt), pl.BlockSpec((WIN, w), cur)],
        out_specs=[pl.BlockSpec((WIN, 2 * w), cur), pl.BlockSpec((1, AH), const), pl.BlockSpec((1, 2 * w), const)],
        out_shape=[jax.ShapeDtypeStruct((S, 2 * w), MXU), jax.ShapeDtypeStruct((1, AH), f32), jax.ShapeDtypeStruct((1, 2 * w), f32)],
        compiler_params=_cparams(("arbitrary",)),
    )(kv, pos, inv_freq, k_norm, dkp, dkc, dvp, dvc)


def _adam_math(w, g, m, v):
    m = ADAM_B1 * m + (1.0 - ADAM_B1) * g
    v = ADAM_B2 * v + (1.0 - ADAM_B2) * (g * g)
    m_hat = m / (1.0 - ADAM_B1 ** ADAM_STEP)
    v_hat = v / (1.0 - ADAM_B2 ** ADAM_STEP)
    return -ADAM_LR * (m_hat / (jnp.sqrt(v_hat) + ADAM_EPS) + ADAM_WD * w), m, v


def adamw(name, w, g, m, v):
    R, C = w.shape
    tr = _pick(R, (256, 128, 64, 32, 16, 8))
    tc = C if tr < R or C % 256 else 256

    def body(w_ref, g_ref, m_ref, v_ref, d_ref, nm_ref, nv_ref):
        d_ref[...], nm_ref[...], nv_ref[...] = _adam_math(w_ref[...], g_ref[...], m_ref[...], v_ref[...])

    spec = pl.BlockSpec((tr, tc), lambda i, j: (i, j))
    return pl.pallas_call(
        body, name=name, grid=(R // tr, C // tc), in_specs=[spec] * 4, out_specs=[spec] * 3,
        out_shape=[jax.ShapeDtypeStruct((R, C), f32)] * 3, compiler_params=_cparams(("parallel", "parallel")),
    )(w, g, m, v)


def _me():
    return lax.axis_index("x"), lax.axis_index("y"), lax.axis_index("c")


def gather_comm(xs):
    n = len(xs)

    def parts(x_refs, o_refs, sems):
        send_sems, recv_sems, local_sems = sems
        x, y, c = _me()
        me, sibling = (x, y, c), (x, y, 1 - c)
        chips = [(1 - x, y), (x, 1 - y), (1 - x, 1 - y)]

        def copy(a, k, block, to, src=None):
            dst = o_refs[a].at[4 * block[0] + 2 * block[1] + block[2]]
            return pltpu.make_async_remote_copy(
                src_ref=dst if src is None else src, dst_ref=dst,
                send_sem=send_sems.at[7 * a + k], recv_sem=recv_sems.at[7 * a + k], device_id=to, device_id_type=MESH)

        mine = [pltpu.make_async_copy(x_refs[a], o_refs[a].at[4 * x + 2 * y + c], local_sems.at[a]) for a in range(n)]
        first = []
        for a in range(n):
            first.append(copy(a, 0, me, sibling, src=x_refs[a]))
            first += [copy(a, 1 + j, me, (*chip, c), src=x_refs[a]) for j, chip in enumerate(chips)]
        return copy, mine, first, me, sibling, chips, c

    def start(x_refs, o_refs, sems):
        _, mine, first, *_ = parts(x_refs, o_refs, sems)
        for cp in mine + first:
            cp.start()

    def finish(x_refs, o_refs, sems):
        copy, mine, first, me, sibling, chips, c = parts(x_refs, o_refs, sems)
        passed = []
        for j, chip in enumerate(chips):
            for a in range(n):
                copy(a, 1 + j, (*chip, c), me).wait_recv()
                cp = copy(a, 4 + j, (*chip, c), sibling)
                cp.start()
                passed.append(cp)
        for a in range(n):
            copy(a, 0, sibling, me).wait_recv()
            for j, chip in enumerate(chips):
                copy(a, 4 + j, (*chip, 1 - c), me).wait_recv()
        for cp in first + passed:
            cp.wait_send()
        for cp in mine:
            cp.wait()

    return Comm(xs, [jax.ShapeDtypeStruct((N_DEV,) + a.shape, a.dtype) for a in xs],
                [pltpu.SemaphoreType.DMA((7 * n,)), pltpu.SemaphoreType.DMA((7 * n,)), pltpu.SemaphoreType.DMA((n,))], start, finish)


def run_comm(name, comm):
    _call(lambda: None, [], name=name, grid=(1,), in_specs=[], out_specs=[], out_shape=[], comm=comm)
    return comm.results


def sibling_comm(gs):
    n = len(gs)

    def copies(g_refs, o_refs, sems):
        x, y, c = _me()
        return [pltpu.make_async_remote_copy(
            src_ref=g_refs[a].at[:, 1 - c], dst_ref=o_refs[a], send_sem=sems[0].at[a], recv_sem=sems[1].at[a],
            device_id=(x, y, 1 - c), device_id_type=MESH) for a in range(n)]

    def start(g_refs, o_refs, sems):
        for cp in copies(g_refs, o_refs, sems):
            cp.start()

    def finish(g_refs, o_refs, sems):
        for cp in copies(g_refs, o_refs, sems):
            cp.wait()

    return Comm(gs, [jax.ShapeDtypeStruct((4,) + g.shape[2:], g.dtype) for g in gs],
                [pltpu.SemaphoreType.DMA((n,)), pltpu.SemaphoreType.DMA((n,))], start, finish)


def chip_comm(ts, rows=None):
    n = len(ts)
    span = [rows or (0, t.shape[1]) for t in ts]

    def copies(t_refs, o_refs, sems):
        x, y, c = _me()
        chips = [(1 - x, y), (x, 1 - y), (1 - x, 1 - y)]
        return [pltpu.make_async_remote_copy(
            src_ref=t_refs[a].at[2 * px + py, pl.ds(*span[a])], dst_ref=o_refs[a].at[j],
            send_sem=sems[0].at[3 * a + j], recv_sem=sems[1].at[3 * a + j],
            device_id=(px, py, c), device_id_type=MESH) for j, (px, py) in enumerate(chips) for a in range(n)]

    def start(t_refs, o_refs, sems):
        for cp in copies(t_refs, o_refs, sems):
            cp.start()

    def finish(t_refs, o_refs, sems):
        for cp in copies(t_refs, o_refs, sems):
            cp.wait()

    return Comm(ts, [jax.ShapeDtypeStruct((3, nr) + t.shape[2:], t.dtype) for t, (_, nr) in zip(ts, span)],
                [pltpu.SemaphoreType.DMA((3 * n,)), pltpu.SemaphoreType.DMA((3 * n,))], start, finish)


def _row_tile(rows):
    return _pick(rows, (512, 304, 256, 128))


def pair_add(name, g, r):
    _, _, R, C = g.shape
    tr = _row_tile(R)

    def body(c_ref, g_ref, r_ref, o_ref):
        o_ref[0] = (g_ref[0, 0].astype(f32) + r_ref[0].astype(f32)).astype(o_ref.dtype)

    return pl.pallas_call(
        body, name=name,
        grid_spec=pltpu.PrefetchScalarGridSpec(
            num_scalar_prefetch=1, grid=(4, R // tr),
            in_specs=[pl.BlockSpec((1, 1, tr, C), lambda p, i, c: (p, c[0], i, 0)),
                      pl.BlockSpec((1, tr, C), lambda p, i, c: (p, i, 0))],
            out_specs=pl.BlockSpec((1, tr, C), lambda p, i, c: (p, i, 0))),
        out_shape=jax.ShapeDtypeStruct((4, R, C), g.dtype),
        compiler_params=_cparams(("parallel", "parallel")),
    )(lax.axis_index("c").reshape(1).astype(jnp.int32), g, r)


def _sum_of_four(t_ref, r_ref):
    return ((t_ref[0].astype(f32) + r_ref[0].astype(f32)) + r_ref[1].astype(f32)) + r_ref[2].astype(f32)


def _my_chip():
    return (2 * lax.axis_index("x") + lax.axis_index("y")).reshape(1).astype(jnp.int32)


def final_adamw(name, t, r, w, m, v):
    _, R, C = t.shape
    tr = _pick(R, (256, 128, 64, 32, 16))
    tc = C if tr < R or C % 256 else 256

    def body(p_ref, t_ref, r_ref, w_ref, m_ref, v_ref, g_ref, d_ref, nm_ref, nv_ref):
        g_ = _sum_of_four(t_ref, r_ref)
        g_ref[...] = g_
        d_ref[...], nm_ref[...], nv_ref[...] = _adam_math(w_ref[...], g_, m_ref[...], v_ref[...])

    flat = pl.BlockSpec((tr, tc), lambda i, j, p: (i, j))
    return pl.pallas_call(
        body, name=name,
        grid_spec=pltpu.PrefetchScalarGridSpec(
            num_scalar_prefetch=1, grid=(R // tr, C // tc),
            in_specs=[pl.BlockSpec((1, tr, tc), lambda i, j, p: (p[0], i, j)),
                      pl.BlockSpec((3, tr, tc), lambda i, j, p: (0, i, j)), flat, flat, flat],
            out_specs=[flat] * 4),
        out_shape=[jax.ShapeDtypeStruct((R, C), f32)] * 4,
        compiler_params=_cparams(("parallel", "parallel")),
    )(_my_chip(), t, r, w, m, v)


def final_adamw_layers(name, t0, r0, t1, r1, w, m, v, comm=None):
    _, R, C = t0.shape
    tr = _pick(R, (256, 128, 64, 32, 16))
    steps = R // tr
    c_ins, c_shapes, c_sems = (comm.ins, comm.out_shapes, comm.sems) if comm else ([], [], [])

    def body(p_ref, t0_ref, r0_ref, t1_ref, r1_ref, w_ref, m_ref, v_ref, *rest):
        ci, (g_ref, d_ref, nm_ref, nv_ref) = rest[:len(c_ins)], rest[len(c_ins):len(c_ins) + 4]
        co, cs = rest[len(c_ins) + 4:len(c_ins) + 4 + len(c_shapes)], rest[len(c_ins) + 4 + len(c_shapes):]
        if comm:
            @pl.when((pl.program_id(0) == 0) & (pl.program_id(1) == 0))
            def _():
                comm.start(ci, co, cs)

        g_ = jnp.where(pl.program_id(0) == 0, _sum_of_four(t0_ref, r0_ref), _sum_of_four(t1_ref, r1_ref))
        g_ref[0] = g_
        d_ref[0], nm_ref[0], nv_ref[0] = _adam_math(w_ref[0], g_, m_ref[0], v_ref[0])
        if comm:
            @pl.when((pl.program_id(0) == 1) & (pl.program_id(1) == steps - 1))
            def _():
                comm.finish(ci, co, cs)

    mine = pl.BlockSpec((1, tr, C), lambda l, i, p: (p[0], i, 0))
    theirs = pl.BlockSpec((3, tr, C), lambda l, i, p: (0, i, 0))
    layer = pl.BlockSpec((1, tr, C), lambda l, i, p: (l, i, 0))
    any_spec = pl.BlockSpec(memory_space=pl.ANY)
    res = pl.pallas_call(
        body, name=name,
        grid_spec=pltpu.PrefetchScalarGridSpec(
            num_scalar_prefetch=1, grid=(2, steps),
            in_specs=[mine, theirs, mine, theirs, layer, layer, layer] + [any_spec] * len(c_ins),
            out_specs=[layer] * 4 + [any_spec] * len(c_shapes), scratch_shapes=list(c_sems)),
        out_shape=[jax.ShapeDtypeStruct((2, R, C), f32)] * 4 + list(c_shapes),
        compiler_params=_cparams(("arbitrary", "arbitrary") if comm else ("parallel", "parallel")),
    )(_my_chip(), t0, r0, t1, r1, w, m, v, *c_ins)
    if comm:
        comm.set_results(res[4:])
    return res[:4]


class ReduceScatter:
    def __init__(self, tag, keys, grads):
        self.tag, self.keys, self.grads = tag, keys, grads
        self.send = [g.reshape((4, 2, g.shape[0] // N_DEV) + g.shape[1:]) for g in grads]
        self.c3 = None

    def sibling(self):
        self.c1 = sibling_comm(self.send)
        return self.c1

    def chips(self, split=None):
        self.pairs = [pair_add(f"rs_pair_add_{self.tag}{i}", g, r) for i, (g, r) in enumerate(zip(self.send, self.c1.results))]
        if split is None:
            self.c2 = chip_comm(self.pairs)
        else:
            self.c2 = chip_comm(self.pairs, (0, split))
            self.c3 = chip_comm(self.pairs, (split, self.pairs[0].shape[1] - split))
        return self.c2

    def parts(self):
        arrived = self.c2.results
        if self.c3 is not None:
            arrived = [jnp.concatenate([a, b], axis=1) for a, b in zip(arrived, self.c3.results)]
        return {k: (t, r) for k, t, r in zip(self.keys, self.pairs, arrived)}


IN_ROWS = {"z": (0, 2048), "xs": (2048, 4096), "B": (4096, 5120), "C": (5120, 6144)}
IN_COLS = 2 * INNER + 2 * GROUPS * NSTATE + HEADS
IN_SPLIT = 448


def sum_devices(g):
    def body(g_ref, o_ref):
        acc = g_ref[0]
        for i in range(1, N_DEV):
            acc = acc + g_ref[i]
        o_ref[...] = acc

    return pl.pallas_call(body, name="sum_devices", out_shape=jax.ShapeDtypeStruct(g.shape[1:], f32),
                          compiler_params=_cparams())(g)


def _pack(parts, unit, dtype, lead=()):
    flat = jnp.concatenate([p.reshape(lead + (-1,)).astype(dtype) for p in parts], axis=-1)
    n = flat.shape[-1]
    rows = -(-n // (unit * PACK_W)) * unit
    flat = jnp.pad(flat, [(0, 0)] * len(lead) + [(0, rows * PACK_W - n)])
    return flat.reshape(lead + (rows, PACK_W))


def _unpack(buf, shapes, lead=()):
    flat = buf.reshape(lead + (-1,))
    out, off = [], 0
    for shp in shapes:
        n = math.prod(shp)
        out.append(flat[..., off:off + n].reshape(lead + tuple(shp)))
        off += n
    return out


def _pad_lanes(a):
    return jnp.pad(a, [(0, 0)] * (a.ndim - 1) + [(0, LANES - a.shape[-1])])


_NN = (((1,), (0,)), ((), ()))
_NT = (((1,), (1,)), ((), ()))


def _mm(a, b, dims):
    return lax.dot_general(a.astype(MXU), b.astype(MXU), dims, preferred_element_type=f32)


def _ffn_fwd(tag, x, h, w_inT, conv_w, conv_b, mid_comm=None):
    ct, nblk = FFN_CT, FFN // FFN_CT

    def body(h_ref, wg_ref, wv_ref, cw_ref, cb_ref, gp_ref, v_ref, a_ref):
        h_ = h_ref[...]
        gp, v_ = _mm(h_, wg_ref[...], _NT), _mm(h_, wv_ref[...], _NT)
        gp_ref[...] = gp
        v_ref[...] = v_
        a_ref[...] = (_silu(_conv(gp, cw_ref[...], cb_ref[...])) * v_).astype(a_ref.dtype)

    col = pl.BlockSpec((S, ct), lambda j: (0, j))
    gate_pre, val, act = _call(
        body, [h, w_inT, w_inT, conv_w, conv_b], comm=mid_comm, name=f"{tag}_in", grid=(nblk,),
        in_specs=[pl.BlockSpec((S, D), lambda j: (0, 0)), pl.BlockSpec((ct, D), lambda j: (j, 0)),
                  pl.BlockSpec((ct, D), lambda j: (nblk + j, 0)), pl.BlockSpec((CONV_F, ct), lambda j: (0, j)),
                  pl.BlockSpec((1, ct), lambda j: (0, j))],
        out_specs=[col, col, col],
        out_shape=[jax.ShapeDtypeStruct((S, FFN), f32), jax.ShapeDtypeStruct((S, FFN), f32), jax.ShapeDtypeStruct((S, FFN), MXU)],
        sem=("parallel",))
    return act, (x, h, gate_pre, val, act)


FFN_CT = 256
CONV_CT = 256


def _proj_conv(name, h, wT, row0, cw, cb, comm=None):
    C, ct = cw.shape[1], CONV_CT

    def body(h_ref, w_ref, cw_ref, cb_ref, p_ref, c_ref):
        p = _mm(h_ref[...], w_ref[...], _NT)
        p_ref[...] = p
        c_ref[...] = _silu(_conv(p, cw_ref[...], cb_ref[...]))

    col = pl.BlockSpec((S, ct), lambda j: (0, j))
    return _call(
        body, [h, wT, cw, cb], comm=comm, name=name, grid=(C // ct,),
        in_specs=[pl.BlockSpec((S, D), lambda j: (0, 0)), pl.BlockSpec((ct, D), lambda j: (row0 // ct + j, 0)),
                  pl.BlockSpec((CONV_A, ct), lambda j: (0, j)), pl.BlockSpec((1, ct), lambda j: (0, j))],
        out_specs=[col, col], out_shape=[jax.ShapeDtypeStruct((S, C), f32)] * 2, sem=("parallel",))


def _dconv_wgrad(name, pre, dconv, cw, cb, h):
    C, ct = cw.shape[1], CONV_CT

    def body(p_ref, do_ref, cw_ref, cb_ref, h_ref, dp_ref, g_ref, dw_ref, db_ref):
        p_, w_ = p_ref[...], cw_ref[...]
        taps = _taps(p_, CONV_A)
        dx, dw, db = _conv_bwd(p_, w_, do_ref[...] * _dsilu(_conv(p_, w_, cb_ref[...], taps)), taps)
        dpm = dx.astype(MXU)
        dp_ref[...] = dpm
        g_ref[...] = lax.dot_general(dpm, h_ref[...].astype(MXU), (((0,), (0,)), ((), ())),
                                     preferred_element_type=f32).astype(g_ref.dtype)
        dw_ref[...] = dw
        db_ref[...] = db

    col = pl.BlockSpec((S, ct), lambda j: (0, j))
    return _call(
        body, [pre, dconv, cw, cb, h], name=name, grid=(C // ct,),
        in_specs=[col, col, pl.BlockSpec((CONV_A, ct), lambda j: (0, j)), pl.BlockSpec((1, ct), lambda j: (0, j)),
                  pl.BlockSpec((S, D), lambda j: (0, 0))],
        out_specs=[col, pl.BlockSpec((ct, D), lambda j: (j, 0)), pl.BlockSpec((CONV_A, ct), lambda j: (0, j)),
                   pl.BlockSpec((1, ct), lambda j: (0, j))],
        out_shape=[jax.ShapeDtypeStruct((S, C), MXU), jax.ShapeDtypeStruct((C, D), MXU),
                   jax.ShapeDtypeStruct((CONV_A, C), f32), jax.ShapeDtypeStruct((1, C), f32)],
        sem=("parallel",))


def _ffn_mid_bwd(name, dout, w_down, gate_pre, val, conv_w, conv_b, comm=None):
    ct = FFN_CT

    def body(do_ref, wd_ref, gp_ref, v_ref, w_ref, b_ref, dgv_ref, dw_ref, db_ref, dob_s):
        @pl.when(pl.program_id(0) == 0)
        def _():
            dob_s[...] = do_ref[...].astype(MXU)

        da = _mm(dob_s[...], wd_ref[...], _NT)
        gp, v_, w_ = gp_ref[...], v_ref[...], w_ref[...]
        taps = _taps(gp, CONV_F)
        gate = _conv(gp, w_, b_ref[...], taps)
        sg = _sigmoid(gate)
        dgp, dw, db = _conv_bwd(gp, w_, da * v_ * (sg * (1.0 + gate * (1.0 - sg))), taps)
        dgv_ref[0] = dgp.astype(dgv_ref.dtype)
        dgv_ref[1] = (da * (gate * sg)).astype(dgv_ref.dtype)
        dw_ref[...] = dw
        db_ref[...] = db

    col = pl.BlockSpec((S, ct), lambda j: (0, j))
    return _call(
        body, [dout, w_down, gate_pre, val, conv_w, conv_b], comm=comm, name=name, grid=(FFN // ct,),
        in_specs=[pl.BlockSpec((S, D), lambda j: (0, 0)), pl.BlockSpec((ct, D), lambda j: (j, 0)), col, col,
                  pl.BlockSpec((CONV_F, ct), lambda j: (0, j)), pl.BlockSpec((1, ct), lambda j: (0, j))],
        out_specs=[pl.BlockSpec((2, S, ct), lambda j: (0, 0, j)), pl.BlockSpec((CONV_F, ct), lambda j: (0, j)),
                   pl.BlockSpec((1, ct), lambda j: (0, j))],
        out_shape=[jax.ShapeDtypeStruct((2, S, FFN), MXU), jax.ShapeDtypeStruct((CONV_F, FFN), f32), jax.ShapeDtypeStruct((1, FFN), f32)],
        scratch=[pltpu.VMEM((S, D), MXU)], sem=("arbitrary",))


def _ffn_bwd(tag, layer, saved, norm_g, w_inT, conv_w, conv_b, w_down, dout, mid_comm=None, before=None):
    x, h, gate_pre, val, act = saved
    g_down = matmul(f"{tag}_wdown", act, dout, "tn", out_dtype=MXU)
    dgv, g_cw, g_cb = _ffn_mid_bwd(f"{tag}_dmid", dout, w_down, gate_pre, val, conv_w, conv_b, comm=mid_comm)
    g_inT = matmul_tn_stacked(f"{tag}_win", dgv, h, MXU)

    def din_fn(dg_, dv_, x_, do_, g_, wT, *wb):
        dx, dg = _rms_bwd(x_, g_, _mm(dg_, wT[:FFN], _NN) + _mm(dv_, wT[FFN:], _NN))
        dx = do_ + dx
        return (dx,) + tuple(_mm(dx, w_, _NT) for w_ in wb) + (dg,)

    rs = ReduceScatter(tag, (f"f_inT{layer}", f"f_down{layer}"), [g_inT, g_down])
    extra = [] if before is None else [before]
    *dxs, g_norm = rowwise(f"{tag}_din", din_fn, [(dgv, 0), (dgv, 1), x, dout], [norm_g, w_inT] + extra,
                           [(D, f32)] + [(w_.shape[0], f32) for w_ in extra], [(1, D)], comm=rs.sibling())
    small = {f"f_norm{layer}": g_norm, f"f_conv_w{layer}": g_cw, f"f_conv_b{layer}": g_cb}
    return (dxs[0] if before is None else tuple(dxs)), small, rs


def _land(W, keys, comm):
    for k, g in zip(keys, comm.results):
        W[k] = g.reshape(-1, g.shape[2])


def _local_step(x, pos, tgt, W, shards):
    G = {}
    gather = lambda *keys: gather_comm([shards[k] for k in keys])
    inv_freq = (ROPE_THETA ** (-jnp.arange(AH // 2, dtype=f32) / (AH // 2))).reshape(1, AH // 2)

    def in_fn(x_, g_, wT, wdtT):
        h_ = _rms_fwd(x_, g_).astype(MXU)
        return h_, _mm(h_, wT[slice(*IN_ROWS["z"])], _NT), _mm(h_, wdtT, _NT)

    h0, z, dt_pre = rowwise("a_in", in_fn, [x], [W["a_norm"], W["inT"], W["in_dtT"]], [(D, MXU), (INNER, f32), (LANES, f32)])
    pre, conv = {}, {}
    early = {"xs": ("a_out",), "B": (), "C": ()}
    for k in ("xs", "B", "C"):
        c = gather(*early[k]) if early[k] else None
        pre[k], conv[k] = _proj_conv(f"a_in_{k}", h0, W["inT"], IN_ROWS[k][0], W[f"cw_{k}"], W[f"cb_{k}"], comm=c)
        if c is not None:
            _land(W, early[k], c)
    c = gather("f_inT0", "f_down0")
    y, states = ssd_fwd(conv["xs"], conv["B"], conv["C"], dt_pre, W["dt_bias"], W["A_log"], W["D"], comm=c)
    _land(W, ("f_inT0", "f_down0"), c)

    def gate_norm(y_, z_, g_):
        yg = y_ * _silu(z_)
        w = INNER // GROUPS
        return (jnp.concatenate([_rms_fwd(yg[:, w * i:w * (i + 1)], g_[:, w * i:w * (i + 1)]) for i in range(GROUPS)], axis=1),)

    def out_fn(y_, z_, x_, g_, w_, gf_):
        (gn_,) = gate_norm(y_, z_, g_)
        gn_ = gn_.astype(MXU)
        x1_ = x_ + _mm(gn_, w_, _NN)
        return gn_, x1_, _rms_fwd(x1_, gf_)

    c = gather("f_down1")
    gn, x1, h1 = rowwise("a_out", out_fn, [y, z, x], [W["a_gnorm"], W["a_out"], W["f_norm0"]],
                         [(INNER, MXU), (D, f32), (D, MXU)], comm=c)
    _land(W, ("f_down1",), c)

    c = gather("w_kv", "w_q", "w_o")
    act0, ffn0 = _ffn_fwd("f0", x1, h1, W["f_inT0"], W["f_cw0"], W["f_cb0"], mid_comm=c)
    _land(W, ("w_kv", "w_q", "w_o"), c)
    kw = KVH * AH

    def qkv_fn(a_, x_, pos_, wd, gk, gb, wkv, bkv, wq, bq, if_, kn, qn):
        x2_ = x_ + _mm(a_, wd, _NN)
        kvn_, h2_ = _rms_fwd(x2_, gk).astype(MXU), _rms_fwd(x2_, gb).astype(MXU)
        kv_, qp_ = _mm(kvn_, wkv, _NN) + bkv, _mm(h2_, wq, _NN) + bq
        cos, sin = _rope_tables(pos_, if_)
        return (x2_, kvn_, h2_, kv_, qp_, _headnorm_rope_fwd(kv_[:, :kw], kn, cos, sin, KVH), kv_[:, kw:],
                _headnorm_rope_fwd(qp_, qn, cos, sin, QH))

    x2, kvn, h2, kv, q_pre, k_rot, v_val, q = rowwise(
        "f0_down_qkv", qkv_fn, [act0, x1, pos],
        [W["f_down0"], W["kv_norm"], W["b_norm"], W["w_kv"], W["b_kv"], W["w_q"], W["b_q"], inv_freq, W["k_norm"], W["q_norm"]],
        [(D, f32), (D, MXU), (D, MXU), (2 * kw, f32), (D, f32), (kw, f32), (kw, f32), (D, f32)])
    c = gather("f_inT1")
    att = attn_fwd(q, k_rot, v_val, W["sinks"], comm=c)
    _land(W, ("f_inT1",), c)
    def o_fn(att_, x_, w_, b_, gf_):
        x3_ = x_ + _mm(att_, w_, _NN) + b_
        return x3_, _rms_fwd(x3_, gf_)

    x3, h3 = rowwise("o_proj", o_fn, [att, x2], [W["w_o"], W["b_o"], W["f_norm1"]], [(D, f32), (D, MXU)])

    act1, ffn1 = _ffn_fwd("f1", x3, h3, W["f_inT1"], W["f_cw1"], W["f_cb1"])

    def loss_fn(a_, x_, t_, w_):
        diff = x_ + _mm(a_, w_, _NN) - t_
        rows = jnp.sum(diff * diff, axis=1, keepdims=True) * (0.5 / D)
        return diff * (1.0 / D), jnp.sum(rows, axis=0, keepdims=True)

    dx4, loss = rowwise("f1_down_loss", loss_fn, [act1, x3, tgt], [W["f_down1"]], [(D, f32)], [(1, 1)])

    (dx3, datt), g, rs_f1 = _ffn_bwd("f1", 1, ffn1, W["f_norm1"], W["f_inT1"], W["f_cw1"], W["f_cb1"], W["f_down1"], dx4,
                                     before=W["w_o"])
    G.update(g)

    g_wo = matmul("o_wproj", att, dx3, "tn", out_dtype=MXU)
    dq, dkp, dkc, dvp, dvc, G["sinks"] = attn_bwd(q, k_rot, v_val, W["sinks"], datt, comm=rs_f1.chips())

    def q_bwd(q_, pos_, dq_, dx_, if_, g_):
        cos, sin = _rope_tables(pos_, if_)
        dqp, dg = _headnorm_rope_bwd(q_, g_, cos, sin, dq_, QH)
        return dqp, dg, jnp.sum(dqp, axis=0, keepdims=True), jnp.sum(dx_, axis=0, keepdims=True)

    dq_pre, G["q_norm"], G["b_q"], G["b_o"] = rowwise("q_drope", q_bwd, [q_pre, pos, dq, dx3], [inv_freq, W["q_norm"]],
                                                      [(D, MXU)], [(1, AH), (1, D), (1, D)])
    g_wq = matmul("q_wproj", h2, dq_pre, "tn", out_dtype=MXU)
    dkv, G["k_norm"], G["b_kv"] = kv_bwd(kv, pos, inv_freq, W["k_norm"], dkp, dkc, dvp, dvc)
    g_wkv = matmul("kv_wproj", kvn, dkv, "tn", out_dtype=MXU)
    rs_att = ReduceScatter("att", ("w_kv", "w_q", "w_o"), [g_wkv, g_wq, g_wo])

    def x2_bwd(x_, dq_, dkv_, dx_, gb_, gk_, wq, wkv):
        d1, dgb = _rms_bwd(x_, gb_, _mm(dq_, wq, _NT))
        d2, dgk = _rms_bwd(x_, gk_, _mm(dkv_, wkv, _NT))
        return dx_ + d1 + d2, dgb, dgk

    dx2, G["b_norm"], G["kv_norm"] = rowwise("qkv_dproj", x2_bwd, [x2, dq_pre, dkv, dx3],
                                             [W["b_norm"], W["kv_norm"], W["w_q"], W["w_kv"]],
                                             [(D, f32)], [(1, D), (1, D)], comm=rs_att.sibling())

    dx1, g, rs_f0 = _ffn_bwd("f0", 0, ffn0, W["f_norm0"], W["f_inT0"], W["f_cw0"], W["f_cb0"], W["f_down0"], dx2,
                             mid_comm=rs_att.chips())
    G.update(g)

    rs_out = ReduceScatter("a_out", ("a_out",), [matmul("a_wout", gn, dx1, "tn", out_dtype=MXU)])

    def gate_norm_bwd(y_, z_, dx_, g_, w_out):
        dgn_ = _mm(dx_, w_out, _NT)
        w = INNER // GROUPS
        sg = _sigmoid(z_)
        sz = z_ * sg
        yg = y_ * sz
        parts, dgs = [], []
        for i in range(GROUPS):
            dseg, dg = _rms_bwd(yg[:, w * i:w * (i + 1)], g_[:, w * i:w * (i + 1)], dgn_[:, w * i:w * (i + 1)])
            parts.append(dseg)
            dgs.append(dg)
        dyg = jnp.concatenate(parts, axis=1)
        return dyg * sz, dyg * y_ * (sg * (1.0 + z_ * (1.0 - sg))), jnp.concatenate(dgs, axis=1)

    dy, dz, G["a_gnorm"] = rowwise("a_dout", gate_norm_bwd, [y, z, dx1], [W["a_gnorm"], W["a_out"]],
                                   [(INNER, f32), (INNER, MXU)], [(1, INNER)], comm=rs_out.sibling())
    dconv = {}
    dconv["xs"], dconv["B"], dconv["C"], ddt_pre, G["dt_bias"], G["A_log"], G["D"] = ssd_bwd(
        conv["xs"], conv["B"], conv["C"], dt_pre, W["dt_bias"], W["A_log"], W["D"], states, dy,
        comm=merge_comms([rs_f0.chips(), rs_out.chips()]))

    g_in, dpre = [matmul("a_win_z", dz, h0, "tn", out_dtype=MXU)], {}
    for k in ("xs", "B", "C"):
        dpre[k], g_k, G[f"cw_{k}"], G[f"cb_{k}"] = _dconv_wgrad(f"a_dconv_{k}", pre[k], dconv[k], W[f"cw_{k}"], W[f"cb_{k}"], h0)
        g_in.append(g_k)
    g_in.append(matmul("a_win_dt", ddt_pre, h0, "tn", out_dtype=MXU)[:HEADS])
    rs_in = ReduceScatter("a_in", ("inT",), [jnp.concatenate(g_in, axis=0)])
    run_comm("rs_in_sibling", rs_in.sibling())

    def x0_bwd(dz_, dxs_, db_, dc_, ddt_, x_, do_, g_, wT, wdtT):
        parts = zip((dz_, dxs_, db_, dc_), IN_ROWS.values())
        dh = sum(_mm(d_, wT[a:b], _NN) for d_, (a, b) in parts) + _mm(ddt_, wdtT, _NN)
        dx, dg = _rms_bwd(x_, g_, dh)
        return do_ + dx, dg

    dx, G["a_norm"] = rowwise("a_din", x0_bwd, [dz, dpre["xs"], dpre["B"], dpre["C"], ddt_pre, x, dx1],
                              [W["a_norm"], W["inT"], W["in_dtT"]], [(D, f32)], [(1, D)], comm=rs_in.chips(split=IN_SPLIT))
    return loss, dx, G, [rs_f1, rs_att, rs_f0, rs_out, rs_in]


ROW_KEYS = ("inT", "a_out", "f_inT0", "f_down0", "w_kv", "w_q", "w_o", "f_inT1", "f_down1")


def _row_blocks(src):
    return {"inT": src["a_in_proj"][0].T, "a_out": src["a_out_proj"][0], "w_kv": src["w_kv"], "w_q": src["w_q"][0],
            "w_o": src["w_o"][0], "f_inT0": src["f_w_in"][0].T, "f_inT1": src["f_w_in"][1].T,
            "f_down0": src["f_w_down"][0], "f_down1": src["f_w_down"][1]}


def _from_row_blocks(rb):
    out = {"a_in_proj": rb["inT"].T[None], "a_out_proj": rb["a_out"][None], "w_kv": rb["w_kv"], "w_q": rb["w_q"][None],
           "w_o": rb["w_o"][None]}
    if "f_inT0" in rb:
        out["f_w_in"] = jnp.stack([rb["f_inT0"].T, rb["f_inT1"].T])
        out["f_w_down"] = jnp.stack([rb["f_down0"], rb["f_down1"]])
    return out


SMALL_SHARDED = ("a_norm", "a_conv_w", "a_conv_b", "a_gnorm", "f_conv_w")
REPLICATED = ("a_dt_bias", "a_A_log", "a_D", "kv_norm", "b_kv", "k_norm", "b_norm", "b_q", "q_norm", "sinks", "b_o",
              "f_norm", "f_conv_b")
ORDER = ("a_norm", "a_in_proj", "a_conv_w", "a_conv_b", "a_dt_bias", "a_A_log", "a_D", "a_gnorm", "a_out_proj", "kv_norm",
         "w_kv", "b_kv", "k_norm", "b_norm", "w_q", "b_q", "q_norm", "sinks", "w_o", "b_o", "f_norm", "f_w_in",
         "f_conv_w", "f_conv_b", "f_w_down")


def _gathered_to_whole(name, g):
    if name == "a_conv_w":
        return jnp.moveaxis(g[:, 0], 0, 1).reshape(g.shape[2], -1)
    if name in ("a_norm", "a_conv_b", "a_gnorm"):
        return g[:, 0].reshape(1, -1)
    if name == "f_conv_w":
        return jnp.moveaxis(g, 0, 2).reshape(g.shape[1], g.shape[2], -1)
    raise ValueError(name)


def _whole_to_shards(name, w):
    if name == "a_conv_w":
        return jnp.moveaxis(w.reshape(w.shape[0], N_DEV, -1), 1, 0)[:, None]
    if name in ("a_norm", "a_conv_b", "a_gnorm"):
        return w.reshape(N_DEV, 1, -1)
    if name == "f_conv_w":
        return jnp.moveaxis(w.reshape(w.shape[0], w.shape[1], N_DEV, -1), 2, 0)
    raise ValueError(name)


def _small_weights(whole):
    W = {}
    cw, cb = whole["a_conv_w"], whole["a_conv_b"]
    o = 0
    for k, n in (("xs", INNER), ("B", GROUPS * NSTATE), ("C", GROUPS * NSTATE)):
        W[f"cw_{k}"], W[f"cb_{k}"] = cw[:, o:o + n], cb[:, o:o + n]
        o += n
    W["a_norm"], W["a_gnorm"] = whole["a_norm"], whole["a_gnorm"]
    W["dt_bias"], W["A_log"], W["D"] = (_pad_lanes(whole[k]) for k in ("a_dt_bias", "a_A_log", "a_D"))
    W["kv_norm"], W["b_kv"], W["k_norm"] = whole["kv_norm"].reshape(1, -1), whole["b_kv"].reshape(1, -1), whole["k_norm"].reshape(1, -1)
    for k in ("b_norm", "b_q", "q_norm", "sinks", "b_o"):
        W[k] = whole[k]
    for i in range(2):
        W[f"f_norm{i}"] = whole["f_norm"][i:i + 1]
        W[f"f_cw{i}"], W[f"f_cb{i}"] = whole["f_conv_w"][i], whole["f_conv_b"][i:i + 1]
    return W


def _small_grads(G, shapes):
    nh = HEADS
    out = {
        "a_conv_w": jnp.concatenate([G["cw_xs"], G["cw_B"], G["cw_C"]], axis=1),
        "a_conv_b": jnp.concatenate([G["cb_xs"], G["cb_B"], G["cb_C"]], axis=1),
        "a_norm": G["a_norm"], "a_gnorm": G["a_gnorm"],
        "a_dt_bias": G["dt_bias"][:, :nh], "a_A_log": G["A_log"][:, :nh], "a_D": G["D"][:, :nh],
        "kv_norm": G["kv_norm"], "b_kv": G["b_kv"], "k_norm": G["k_norm"], "b_norm": G["b_norm"],
        "b_q": G["b_q"], "q_norm": G["q_norm"], "sinks": G["sinks"], "b_o": G["b_o"],
        "f_norm": jnp.concatenate([G["f_norm0"], G["f_norm1"]], axis=0),
        "f_conv_w": jnp.stack([G["f_conv_w0"], G["f_conv_w1"]]),
        "f_conv_b": jnp.concatenate([G["f_conv_b0"], G["f_conv_b1"]], axis=0),
    }
    return {k: val.reshape(shapes[k]) if k in shapes else val for k, val in out.items()}


def kernel(x, positions, a_norm, a_in_proj, a_conv_w, a_conv_b, a_dt_bias, a_A_log, a_D, a_gnorm, a_out_proj, kv_norm, w_kv, b_kv, k_norm, b_norm, w_q, b_q, q_norm, sinks, w_o, b_o, f_norm, f_w_in, f_conv_w, f_conv_b, f_w_down, loss_target, m_a_norm, m_a_in_proj, m_a_conv_w, m_a_conv_b, m_a_dt_bias, m_a_A_log, m_a_D, m_a_gnorm, m_a_out_proj, m_kv_norm, m_w_kv, m_b_kv, m_k_norm, m_b_norm, m_w_q, m_b_q, m_q_norm, m_sinks, m_w_o, m_b_o, m_f_norm, m_f_w_in, m_f_conv_w, m_f_conv_b, m_f_w_down, v_a_norm, v_a_in_proj, v_a_conv_w, v_a_conv_b, v_a_dt_bias, v_a_A_log, v_a_D, v_a_gnorm, v_a_out_proj, v_kv_norm, v_w_kv, v_b_kv, v_k_norm, v_b_norm, v_w_q, v_b_q, v_q_norm, v_sinks, v_w_o, v_b_o, v_f_norm, v_f_w_in, v_f_conv_w, v_f_conv_b, v_f_w_down):
    given = dict(locals())
    w_in = {n: given[n] for n in ORDER}
    m_in = {n: given["m_" + n] for n in ORDER}
    v_in = {n: given["v_" + n] for n in ORDER}
    dev = 4 * lax.axis_index("x") + 2 * lax.axis_index("y") + lax.axis_index("c")

    w2, m2, v2 = _row_blocks(w_in), _row_blocks(m_in), _row_blocks(v_in)
    small_pack = _pack([w_in[n] for n in SMALL_SHARDED], 8, f32)
    shards = {k: w2[k].astype(MXU) for k in ROW_KEYS}
    in_all, small_all = run_comm("ag_head", gather_comm([shards["inT"], small_pack]))
    whole = {n: w_in[n] for n in REPLICATED}
    for n, g in zip(SMALL_SHARDED, _unpack(small_all, [w_in[n].shape for n in SMALL_SHARDED], lead=(N_DEV,))):
        whole[n] = _gathered_to_whole(n, g)
    W = _small_weights(whole)
    W["inT"] = in_all.reshape(-1, D)
    W["in_dtT"] = jnp.pad(W["inT"][IN_COLS - HEADS:], ((0, LANES - HEADS), (0, 0)))

    loss, dx, G, scatters = _local_step(x[0], positions.reshape(S, 1).astype(f32), loss_target[0], W, shards)
    grads = _small_grads(G, {n: whole[n].shape for n in REPLICATED})

    small_names = SMALL_SHARDED + REPLICATED
    small_part = _pack([grads[n] for n in small_names] + [loss], 8, f32)
    small_gather = gather_comm([small_part])
    parts, last = {}, scatters[-1]
    for rs in scatters[:-1]:
        parts.update(rs.parts())
    riding = merge_comms([small_gather] + ([last.c3] if last.c3 is not None else []))

    layered = {}
    for n, key, lay in (("f_w_in", "f_inT", lambda a: jnp.swapaxes(a, 1, 2)), ("f_w_down", "f_down", lambda a: a)):
        res = final_adamw_layers(f"adamw_{n}", *parts[key + "0"], *parts[key + "1"], lay(w_in[n]), lay(m_in[n]), lay(v_in[n]),
                                 comm=riding if n == "f_w_in" else None)
        layered[n] = [lay(a) for a in res]
    parts.update(last.parts())
    single = tuple(k for k in ROW_KEYS if not k.startswith("f_"))
    stepped = {k: final_adamw(f"adamw_{k}", *parts[k], w2[k], m2[k], v2[k]) for k in single}
    g_out, delta, new_m, new_v = (_from_row_blocks({k: stepped[k][i] for k in single}) for i in range(4))
    for n, res in layered.items():
        g_out[n], delta[n], new_m[n], new_v[n] = res
    *small_sums, loss_all = _unpack(sum_devices(small_gather.results[0]), [grads[n].shape for n in small_names] + [(1, 1)])
    for n, g in zip(small_names, small_sums):
        if n in SMALL_SHARDED:
            g_out[n] = lax.dynamic_index_in_dim(_whole_to_shards(n, g), dev, axis=0, keepdims=False)
        else:
            g_out[n] = g.reshape(w_in[n].shape)

    packs = [_pack([src[n] for n in small_names], 8, f32) for src in (w_in, g_out, m_in, v_in)]
    outs = adamw("adamw_small", *packs)
    for dst, buf in zip((delta, new_m, new_v), outs):
        for n, a in zip(small_names, _unpack(buf, [w_in[n].shape for n in small_names])):
            dst[n] = a

    return (loss_all[0, 0], dx[None], *[g_out[n] for n in ORDER], *[delta[n] for n in ORDER],
            *[new_m[n] for n in ORDER], *[new_v[n] for n in ORDER])
```

```python
import functools
import math

import jax
import jax.numpy as jnp
from jax import lax
from jax.experimental import pallas as pl
from jax.experimental.pallas import tpu as pltpu

f32 = jnp.float32
bf16 = jnp.bfloat16
MXU = bf16

N_DEV = 8
S = 2048
D = 1024
EPS = 1e-5
INNER = 2048
HEADS = 32
HP = 64
GROUPS = 8
HPG = HEADS // GROUPS
NSTATE = 128
CONV_A = 4
CHUNK = 256
NCHUNK = S // CHUNK
AH = 64
QH = 16
KVH = 4
QPK = QH // KVH
WIN = 128
NBLK = S // WIN
ROPE_THETA = 10000.0
FFN = 2816
CONV_F = 3
LANES = 128
PACK_W = 1024
VMEM_LIMIT = 56 * 1024 * 1024

ADAM_LR, ADAM_B1, ADAM_B2, ADAM_EPS, ADAM_WD, ADAM_STEP = 0.001, 0.9, 0.999, 1e-08, 0.01, 10

MESH = pl.DeviceIdType.MESH


def _cparams(sem=None):
    return pltpu.CompilerParams(dimension_semantics=sem, vmem_limit_bytes=VMEM_LIMIT)


def _pick(n, cands):
    for c in cands:
        if n % c == 0:
            return c
    return n


class Comm:
    def __init__(self, ins, out_shapes, sems, start, finish):
        self.ins, self.out_shapes, self.sems, self.start, self.finish = list(ins), list(out_shapes), list(sems), start, finish
        self.results, self.children = None, ()

    def set_results(self, res):
        self.results, o = list(res), 0
        for ch in self.children:
            ch.set_results(res[o:o + len(ch.out_shapes)])
            o += len(ch.out_shapes)


def merge_comms(comms):
    def each(fn_name, ins, outs, sems):
        i = o = s = 0
        for c in comms:
            getattr(c, fn_name)(ins[i:i + len(c.ins)], outs[o:o + len(c.out_shapes)], sems[s:s + len(c.sems)])
            i, o, s = i + len(c.ins), o + len(c.out_shapes), s + len(c.sems)

    merged = Comm([a for c in comms for a in c.ins], [a for c in comms for a in c.out_shapes], [a for c in comms for a in c.sems],
                  functools.partial(each, "start"), functools.partial(each, "finish"))
    merged.children = tuple(comms)
    return merged


def _call(body, args, *, name, grid, in_specs, out_specs, out_shape, scratch=(), sem=None, comm=None):
    if comm is None:
        return pl.pallas_call(body, name=name, grid=grid, in_specs=list(in_specs), out_specs=list(out_specs),
                              out_shape=list(out_shape), scratch_shapes=list(scratch), compiler_params=_cparams(sem))(*args)
    n_in, n_out, n_scr, c_in, c_out = len(in_specs), len(out_shape), len(scratch), len(comm.ins), len(comm.out_shapes)
    any_spec = pl.BlockSpec(memory_space=pl.ANY)

    def outer(*refs):
        ins, c_ins = refs[:n_in], refs[n_in:n_in + c_in]
        o = n_in + c_in
        outs, c_outs = refs[o:o + n_out], refs[o + n_out:o + n_out + c_out]
        o += n_out + c_out
        scr, c_sems = refs[o:o + n_scr], refs[o + n_scr:]
        ids = [pl.program_id(i) for i in range(len(grid))]
        first = functools.reduce(jnp.logical_and, [i == 0 for i in ids])
        last = functools.reduce(jnp.logical_and, [i == g - 1 for i, g in zip(ids, grid)])

        @pl.when(first)
        def _():
            comm.start(c_ins, c_outs, c_sems)

        body(*ins, *outs, *scr)

        @pl.when(last)
        def _():
            comm.finish(c_ins, c_outs, c_sems)

    res = pl.pallas_call(
        outer, name=name, grid=grid, in_specs=list(in_specs) + [any_spec] * c_in,
        out_specs=list(out_specs) + [any_spec] * c_out, out_shape=list(out_shape) + comm.out_shapes,
        scratch_shapes=list(scratch) + comm.sems, compiler_params=_cparams(("arbitrary",) * len(grid)),
    )(*args, *comm.ins)
    comm.set_results(res[n_out:])
    return res[:n_out]


def matmul(name, a, b, mode, out_dtype=f32, bias=None, residual=None):
    if mode == "nn":
        (M, K), (K2, N) = a.shape, b.shape
    elif mode == "nt":
        (M, K), (N, K2) = a.shape, b.shape
    else:
        (K, M), (K2, N) = a.shape, b.shape
    assert K == K2, (name, a.shape, b.shape)
    if mode == "tn":
        tm, tn = M, _pick(N, (512, 256, 128) if M <= 1024 else (256, 128))
        a_spec = pl.BlockSpec((K, M), lambda j: (0, 0))
        b_spec = pl.BlockSpec((K, tn), lambda j: (0, j))
        dims = (((0,), (0,)), ((), ()))
        grid, o_map, row_map = (N // tn,), (lambda j: (0, j)), (lambda j: (0, j))
    else:
        tm, tn = (256 if N >= 2048 else 512), N
        a_spec = pl.BlockSpec((tm, K), lambda i: (i, 0))
        b_spec = pl.BlockSpec(b.shape, lambda i: (0, 0))
        dims = (((1,), (0,)), ((), ())) if mode == "nn" else (((1,), (1,)), ((), ()))
        grid, o_map, row_map = (M // tm,), (lambda i: (i, 0)), (lambda i: (0, 0))
    ins, in_specs = [a, b], [a_spec, b_spec]
    if bias is not None:
        ins.append(bias)
        in_specs.append(pl.BlockSpec((1, tn), row_map))
    if residual is not None:
        ins.append(residual)
        in_specs.append(pl.BlockSpec((tm, tn), o_map))
    has_bias, has_res = bias is not None, residual is not None

    def body(a_ref, b_ref, *rest):
        rest = list(rest)
        bias_ref = rest.pop(0) if has_bias else None
        res_ref = rest.pop(0) if has_res else None
        (o_ref,) = rest
        r = lax.dot_general(a_ref[...].astype(MXU), b_ref[...].astype(MXU), dims, preferred_element_type=f32)
        if has_bias:
            r = r + bias_ref[...]
        if has_res:
            r = r + res_ref[...]
        o_ref[...] = r.astype(out_dtype)

    return pl.pallas_call(
        body, name=name, grid=grid, in_specs=in_specs,
        out_specs=pl.BlockSpec((tm, tn), o_map),
        out_shape=jax.ShapeDtypeStruct((M, N), out_dtype),
        compiler_params=_cparams(("parallel",)),
    )(*ins)


def matmul_tn_stacked(name, a, b, out_dtype):
    R, K, M = a.shape
    N = b.shape[1]
    tn = _pick(N, (256, 128))

    def body(a_ref, b_ref, o_ref):
        o_ref[0] = lax.dot_general(a_ref[0].astype(MXU), b_ref[...].astype(MXU), (((0,), (0,)), ((), ())),
                                   preferred_element_type=f32).astype(out_dtype)

    out = pl.pallas_call(
        body, name=name, grid=(R, N // tn),
        in_specs=[pl.BlockSpec((1, K, M), lambda r, j: (r, 0, 0)), pl.BlockSpec((K, tn), lambda r, j: (0, j))],
        out_specs=pl.BlockSpec((1, M, tn), lambda r, j: (r, 0, j)),
        out_shape=jax.ShapeDtypeStruct((R, M, N), out_dtype),
        compiler_params=_cparams(("parallel", "parallel")),
    )(a, b)
    return out.reshape(R * M, N)


def rowwise(name, fn, rows, pars, outs, accs=(), tile=256, comm=None):
    n_in, n_out = len(rows) + len(pars), len(outs)
    in_specs = [pl.BlockSpec((None, tile, r[0].shape[2]), functools.partial(lambda i, lead: (lead, i, 0), lead=r[1]))
                if isinstance(r, tuple) else pl.BlockSpec((tile, r.shape[1]), lambda i: (i, 0)) for r in rows]
    rows = [r[0] if isinstance(r, tuple) else r for r in rows]
    in_specs += [pl.BlockSpec(p.shape, lambda i: (0, 0)) for p in pars]
    out_specs = [pl.BlockSpec((tile, c), lambda i: (i, 0)) for c, _ in outs]
    out_specs += [pl.BlockSpec(shp, lambda i: (0, 0)) for shp in accs]
    out_shape = [jax.ShapeDtypeStruct((S, c), dt) for c, dt in outs]
    out_shape += [jax.ShapeDtypeStruct(shp, f32) for shp in accs]

    def body(*refs):
        res = fn(*[r[...] for r in refs[:n_in]])
        o_refs = refs[n_in:n_in + n_out]
        a_refs = refs[n_in + n_out:]
        for ref, val in zip(o_refs, res[:n_out]):
            ref[...] = val.astype(ref.dtype)
        if a_refs:
            @pl.when(pl.program_id(0) == 0)
            def _():
                for ref in a_refs:
                    ref[...] = jnp.zeros_like(ref)
            for ref, val in zip(a_refs, res[n_out:]):
                ref[...] += val

    return _call(body, [*rows, *pars], name=name, grid=(S // tile,), in_specs=in_specs, out_specs=out_specs,
                 out_shape=out_shape, sem=("arbitrary",) if accs else ("parallel",), comm=comm)


def _sigmoid(x):
    return 0.5 * jnp.tanh(0.5 * x) + 0.5


def _silu(x):
    return x * _sigmoid(x)


def _dsilu(x):
    sg = _sigmoid(x)
    return sg * (1.0 + x * (1.0 - sg))


def _softplus(x):
    return jnp.maximum(x, 0.0) + jnp.log(1.0 + jnp.exp(-jnp.abs(x)))


def _rms_fwd(x, g):
    r = lax.rsqrt(jnp.mean(x * x, axis=-1, keepdims=True) + EPS)
    return x * r * g


def _rms_bwd(x, g, dh):
    r = lax.rsqrt(jnp.mean(x * x, axis=-1, keepdims=True) + EPS)
    xh = x * r
    dxh = dh * g
    dx = r * (dxh - xh * jnp.mean(dxh * xh, axis=-1, keepdims=True))
    return dx, jnp.sum(dh * xh, axis=0, keepdims=True)


def _taps(x, width):
    row = lax.broadcasted_iota(jnp.int32, (8, x.shape[1]), 0)

    def shifted(s):
        r = pltpu.roll(x, s, 0)
        return jnp.concatenate([jnp.where(row >= s, r[:8], 0.0), r[8:]], axis=0)

    return [shifted(s) for s in range(width - 1, 0, -1)] + [x]


def _conv(x, w, b, taps=None):
    width = w.shape[0]
    taps = _taps(x, width) if taps is None else taps
    out = b + w[0:1, :] * taps[0]
    for k in range(1, width):
        out = out + w[k:k + 1, :] * taps[k]
    return out


def _conv_bwd(x, w, dc, taps=None):
    width, n = w.shape[0], x.shape[0]
    taps = _taps(x, width) if taps is None else taps
    row = lax.broadcasted_iota(jnp.int32, (8, x.shape[1]), 0)
    dx = w[width - 1:width, :] * dc
    for k in range(width - 1):
        s = width - 1 - k
        r = pltpu.roll(dc, n - s, 0)
        dx = dx + w[k:k + 1, :] * jnp.concatenate([r[:n - 8], jnp.where(row < 8 - s, r[n - 8:], 0.0)], axis=0)
    dw = jnp.concatenate([jnp.sum(dc * t, axis=0, keepdims=True) for t in taps], axis=0)
    return dx, dw, jnp.sum(dc, axis=0, keepdims=True)


def _rope_tables(pos, inv_freq):
    ang = pos * inv_freq
    return jnp.cos(ang), jnp.sin(ang)


def _split2(v):
    hi = v.astype(bf16)
    return hi, (v - hi.astype(f32)).astype(bf16)


def _head_maps(width):
    shift = AH.bit_length() - 1
    to_head = (lax.broadcasted_iota(jnp.int32, (width, LANES), 0) >> shift) == lax.broadcasted_iota(jnp.int32, (width, LANES), 1)
    from_head = lax.broadcasted_iota(jnp.int32, (LANES, width), 0) == (lax.broadcasted_iota(jnp.int32, (LANES, width), 1) >> shift)
    return to_head.astype(bf16), from_head.astype(bf16)


def _head_sums(v, to_head):
    hi, lo = _split2(v)
    return jnp.dot(hi, to_head, preferred_element_type=f32) + jnp.dot(lo, to_head, preferred_element_type=f32)


def _head_spread(s, from_head):
    hi, lo = _split2(s)
    return jnp.dot(hi, from_head, preferred_element_type=f32) + jnp.dot(lo, from_head, preferred_element_type=f32)


def _rope_full(cos, sin, width):
    half = AH // 2
    pad = jnp.zeros((cos.shape[0], LANES - half), f32)
    r = lax.broadcasted_iota(jnp.int32, (LANES, width), 0)
    lane = lax.broadcasted_iota(jnp.int32, (LANES, width), 1)
    spread = ((lane & (half - 1)) == r).astype(bf16)
    full = lambda t: _head_spread(jnp.concatenate([t, pad], axis=1), spread)
    first = (lax.broadcasted_iota(jnp.int32, (1, width), 1) & (AH - 1)) < half
    sin_f = full(sin)
    return full(cos), jnp.where(first, -sin_f, sin_f), first


def _swap_halves(v, first):
    half, width = AH // 2, v.shape[1]
    return jnp.where(first, pltpu.roll(v, width - half, 1), pltpu.roll(v, half, 1))


def _headnorm_rope_fwd(x, g, cos, sin, heads):
    to_head, from_head = _head_maps(heads * AH)
    cos_f, sin_s, first = _rope_full(cos, sin, heads * AH)
    r = _head_spread(lax.rsqrt(_head_sums(x * x, to_head) * (1.0 / AH) + EPS), from_head)
    n = x * r * jnp.tile(g, (1, heads))
    return n * cos_f + _swap_halves(n, first) * sin_s


def _headnorm_rope_bwd(x, g, cos, sin, dout, heads):
    width = heads * AH
    to_head, from_head = _head_maps(width)
    cos_f, sin_s, first = _rope_full(cos, sin, width)
    r = _head_spread(lax.rsqrt(_head_sums(x * x, to_head) * (1.0 / AH) + EPS), from_head)
    xh = x * r
    dn = dout * cos_f - _swap_halves(dout, first) * sin_s
    dxh = dn * jnp.tile(g, (1, heads))
    m = _head_spread(_head_sums(dxh * xh, to_head) * (1.0 / AH), from_head)
    dx = r * (dxh - xh * m)
    dg_lanes = jnp.sum(dn * xh, axis=0, keepdims=True)
    fold = ((lax.broadcasted_iota(jnp.int32, (width, LANES), 0) & (AH - 1))
            == lax.broadcasted_iota(jnp.int32, (width, LANES), 1)).astype(f32)
    dg = jnp.dot(jnp.broadcast_to(dg_lanes, (8, width)), fold, precision=lax.Precision.HIGHEST, preferred_element_type=f32)
    return dx, dg[0:1, :AH]


def _ssd_prep(dt_pre, dt_bias, a_log, dt_s, acum_s, acumT_s):
    dt = _softplus(dt_pre + dt_bias)
    a = dt * (-jnp.exp(a_log))
    row = lax.broadcasted_iota(jnp.int32, (CHUNK, CHUNK), 0)
    col = lax.broadcasted_iota(jnp.int32, (CHUNK, CHUNK), 1)
    dt_s[...] = dt
    acum_s[...] = jnp.dot((col <= row).astype(f32), a, precision=lax.Precision.HIGHEST, preferred_element_type=f32)
    acumT_s[...] = lax.dot_general(a, (row <= col).astype(f32), (((0,), (0,)), ((), ())),
                                   precision=lax.Precision.HIGHEST, preferred_element_type=f32)


def _head_cols(h, dt_s, acum_s, acumT_s):
    lane = lax.broadcasted_iota(jnp.int32, (1, LANES), 1)
    oh_l = (lane == h).astype(f32)
    sub = lax.broadcasted_iota(jnp.int32, (LANES, 1), 0)
    oh_s = (sub == h).astype(f32)
    dt_h = jnp.sum(dt_s[...] * oh_l, axis=1, keepdims=True)
    ac_h = jnp.sum(acum_s[...] * oh_l, axis=1, keepdims=True)
    acr_h = jnp.sum(acumT_s[...] * oh_s, axis=0, keepdims=True)
    return oh_l, dt_h, ac_h, acr_h


def ssd_fwd(xs, Bm, Cm, dt_pre, dt_bias, a_log, d_skip, comm=None):
    def body(xs_ref, b_ref, c_ref, dtp_ref, bias_ref, alog_ref, d_ref, y_ref, st_ref, state, dt_s, acum_s, acumT_s):
        c, g = pl.program_id(0), pl.program_id(1)

        @pl.when(g == 0)
        def _():
            _ssd_prep(dtp_ref[...], bias_ref[...], alog_ref[...], dt_s, acum_s, acumT_s)

        row = lax.broadcasted_iota(jnp.int32, (CHUNK, CHUNK), 0)
        col = lax.broadcasted_iota(jnp.int32, (CHUNK, CHUNK), 1)
        causal = col <= row
        Bb, Cb = b_ref[...], c_ref[...]
        cb = lax.dot_general(Cb.astype(MXU), Bb.astype(MXU), (((1,), (1,)), ((), ())), preferred_element_type=f32)
        xs_blk = xs_ref[...]

        @pl.when(c == 0)
        def _():
            for j in range(HPG):
                state[g * HPG + j] = jnp.zeros((NSTATE, HP), f32)

        prevs = [state[g * HPG + j] for j in range(HPG)]
        y_off_all = jnp.dot(Cb.astype(MXU), jnp.concatenate(prevs, axis=1).astype(MXU), preferred_element_type=f32)
        ys, xds, e_ends = [], [], []
        for j in range(HPG):
            oh_l, dt_h, ac_h, acr_h = _head_cols(g * HPG + j, dt_s, acum_s, acumT_s)
            decay = jnp.exp(jnp.where(causal, ac_h - acr_h, -1e30))
            w = (cb * decay).astype(MXU)
            xs_h = xs_blk[:, HP * j:HP * (j + 1)]
            xd = xs_h * dt_h
            y_diag = jnp.dot(w, xd.astype(MXU), preferred_element_type=f32)
            y_off = y_off_all[:, HP * j:HP * (j + 1)] * jnp.exp(ac_h)
            d_h = jnp.sum(d_ref[...] * oh_l, axis=1, keepdims=True)
            ys.append(y_diag + y_off + xs_h * d_h)
            a_end = ac_h[CHUNK - 1:CHUNK, :]
            xds.append(xd * jnp.exp(a_end - ac_h))
            e_ends.append(jnp.exp(a_end))
        s_c = lax.dot_general(Bb.astype(MXU), jnp.concatenate(xds, axis=1).astype(MXU), (((0,), (0,)), ((), ())),
                              preferred_element_type=f32)
        for j in range(HPG):
            st_ref[0, j] = prevs[j]
            state[g * HPG + j] = prevs[j] * e_ends[j] + s_c[:, HP * j:HP * (j + 1)]
        y_ref[...] = jnp.concatenate(ys, axis=1)

    par = pl.BlockSpec((1, LANES), lambda c, g: (0, 0))
    return _call(
        body, [xs, Bm, Cm, dt_pre, dt_bias, a_log, d_skip], comm=comm, name="ssd_fwd", grid=(NCHUNK, GROUPS),
        in_specs=[pl.BlockSpec((CHUNK, HPG * HP), lambda c, g: (c, g)),
                  pl.BlockSpec((CHUNK, NSTATE), lambda c, g: (c, g)),
                  pl.BlockSpec((CHUNK, NSTATE), lambda c, g: (c, g)),
                  pl.BlockSpec((CHUNK, LANES), lambda c, g: (c, 0)), par, par, par],
        out_specs=[pl.BlockSpec((CHUNK, HPG * HP), lambda c, g: (c, g)),
                   pl.BlockSpec((1, HPG, NSTATE, HP), lambda c, g: (c, g, 0, 0))],
        out_shape=[jax.ShapeDtypeStruct((S, INNER), f32), jax.ShapeDtypeStruct((NCHUNK, HEADS, NSTATE, HP), f32)],
        scratch=[pltpu.VMEM((HEADS, NSTATE, HP), f32), pltpu.VMEM((CHUNK, LANES), f32),
                 pltpu.VMEM((CHUNK, LANES), f32), pltpu.VMEM((LANES, CHUNK), f32)],
        sem=("arbitrary", "arbitrary"))


def ssd_bwd(xs, Bm, Cm, dt_pre, dt_bias, a_log, d_skip, states, dy, comm=None):
    rev = lambda c: NCHUNK - 1 - c

    def body(xs_ref, b_ref, c_ref, dtp_ref, bias_ref, alog_ref, d_ref, st_ref, dy_ref,
             dxs_ref, db_ref, dc_ref, ddt_ref, dbias_ref, dalog_ref, dd_ref,
             dstate, dt_s, acum_s, acumT_s, dacum_s, ddt_s, da_s):
        c, g = pl.program_id(0), pl.program_id(1)

        @pl.when(g == 0)
        def _():
            _ssd_prep(dtp_ref[...], bias_ref[...], alog_ref[...], dt_s, acum_s, acumT_s)
            dacum_s[...] = jnp.zeros_like(dacum_s)
            ddt_s[...] = jnp.zeros_like(ddt_s)

        @pl.when((c == 0) & (g == 0))
        def _():
            da_s[...] = jnp.zeros_like(da_s)
            dd_ref[...] = jnp.zeros_like(dd_ref)
            dbias_ref[...] = jnp.zeros_like(dbias_ref)
            dalog_ref[...] = jnp.zeros_like(dalog_ref)

        row = lax.broadcasted_iota(jnp.int32, (CHUNK, CHUNK), 0)
        col = lax.broadcasted_iota(jnp.int32, (CHUNK, CHUNK), 1)
        sub_l = lax.broadcasted_iota(jnp.int32, (CHUNK, 1), 0)
        last = (sub_l == CHUNK - 1).astype(f32)
        nt = (((1,), (1,)), ((), ()))
        tn = (((0,), (0,)), ((), ()))
        Bb, Cb = b_ref[...], c_ref[...]
        Bm_, Cm_ = Bb.astype(MXU), Cb.astype(MXU)
        cb = lax.dot_general(Cm_, Bm_, nt, preferred_element_type=f32)
        bc = lax.dot_general(Bm_, Cm_, nt, preferred_element_type=f32)
        xs_blk, dy_blk = xs_ref[...], dy_ref[...]
        dxs, dB, dC = [], jnp.zeros((CHUNK, NSTATE), f32), jnp.zeros((CHUNK, NSTATE), f32)
        for j in range(HPG):
            h = g * HPG + j
            oh_l, dt_h, ac_h, acr_h = _head_cols(h, dt_s, acum_s, acumT_s)

            @pl.when(c == 0)
            def _():
                dstate[h] = jnp.zeros((NSTATE, HP), f32)

            dnext = dstate[h]
            prev = st_ref[0, j]
            lm = jnp.exp(jnp.where(col <= row, ac_h - acr_h, -1e30))
            lmT = jnp.exp(jnp.where(row <= col, acr_h - ac_h, -1e30))
            xs_h = xs_blk[:, HP * j:HP * (j + 1)]
            dy_h = dy_blk[:, HP * j:HP * (j + 1)]
            xd = xs_h * dt_h
            xdm, dym = xd.astype(MXU), dy_h.astype(MXU)
            ea = jnp.exp(ac_h)
            a_end = ac_h[CHUNK - 1:CHUNK, :]
            e_end = jnp.exp(a_end)
            dte = jnp.exp(a_end - ac_h)
            dnm, pvm = dnext.astype(MXU), prev.astype(MXU)
            bd = jnp.dot(Bm_, dnm, preferred_element_type=f32)
            dxd = jnp.dot((bc * lmT).astype(MXU), dym, preferred_element_type=f32) + dte * bd
            dw = lax.dot_general(dym, xdm, nt, preferred_element_type=f32)
            dwT = lax.dot_general(xdm, dym, nt, preferred_element_type=f32)
            dcb = dw * lm
            dbc = dwT * lmT
            eady = (ea * dy_h).astype(MXU)
            dC = dC + jnp.dot(dcb.astype(MXU), Bm_, preferred_element_type=f32) \
                + lax.dot_general(eady, pvm, nt, preferred_element_type=f32)
            dB = dB + jnp.dot(dbc.astype(MXU), Cm_, preferred_element_type=f32) \
                + dte * lax.dot_general(xdm, dnm, nt, preferred_element_type=f32)
            dstate[h] = lax.dot_general(Cm_, eady, tn, preferred_element_type=f32) + e_end * dnext
            r1 = jnp.sum(dcb * cb, axis=1, keepdims=True)
            r2 = jnp.sum(dbc * bc, axis=1, keepdims=True)
            y_off = jnp.dot(Cm_, pvm, preferred_element_type=f32) * ea
            t3 = jnp.sum(dy_h * y_off, axis=1, keepdims=True)
            t4 = jnp.sum(bd * xd, axis=1, keepdims=True) * dte
            end_extra = jnp.sum(t4, axis=0, keepdims=True) + e_end * jnp.sum(jnp.sum(prev * dnext, axis=1, keepdims=True), axis=0, keepdims=True)
            dacum_h = r1 - r2 + t3 - t4 + last * end_extra
            dacum_s[...] += dacum_h * oh_l
            ddt_s[...] += jnp.sum(dxd * xs_h, axis=1, keepdims=True) * oh_l
            d_h = jnp.sum(d_ref[...] * oh_l, axis=1, keepdims=True)
            dxs.append(dxd * dt_h + dy_h * d_h)
            dd_ref[...] += oh_l * jnp.sum(jnp.sum(dy_h * xs_h, axis=1, keepdims=True), axis=0, keepdims=True)
        dxs_ref[...] = jnp.concatenate(dxs, axis=1)
        db_ref[...] = dB
        dc_ref[...] = dC

        @pl.when(g == GROUPS - 1)
        def _():
            a_row = -jnp.exp(alog_ref[...])
            da = jnp.dot((row <= col).astype(f32), dacum_s[...], precision=lax.Precision.HIGHEST, preferred_element_type=f32)
            da_s[...] += jnp.sum(da * dt_s[...], axis=0, keepdims=True)
            z = dtp_ref[...] + bias_ref[...]
            ddt_pre = (ddt_s[...] + da * a_row) * _sigmoid(z)
            ddt_ref[...] = ddt_pre.astype(ddt_ref.dtype)
            dbias_ref[...] += jnp.sum(ddt_pre, axis=0, keepdims=True)

            @pl.when(c == NCHUNK - 1)
            def _():
                dalog_ref[...] = da_s[...] * a_row

    par = pl.BlockSpec((1, LANES), lambda c, g: (0, 0))
    return _call(
        body, [xs, Bm, Cm, dt_pre, dt_bias, a_log, d_skip, states, dy], comm=comm, name="ssd_bwd", grid=(NCHUNK, GROUPS),
        in_specs=[pl.BlockSpec((CHUNK, HPG * HP), lambda c, g: (rev(c), g)),
                  pl.BlockSpec((CHUNK, NSTATE), lambda c, g: (rev(c), g)),
                  pl.BlockSpec((CHUNK, NSTATE), lambda c, g: (rev(c), g)),
                  pl.BlockSpec((CHUNK, LANES), lambda c, g: (rev(c), 0)), par, par, par,
                  pl.BlockSpec((1, HPG, NSTATE, HP), lambda c, g: (rev(c), g, 0, 0)),
                  pl.BlockSpec((CHUNK, HPG * HP), lambda c, g: (rev(c), g))],
        out_specs=[pl.BlockSpec((CHUNK, HPG * HP), lambda c, g: (rev(c), g)),
                   pl.BlockSpec((CHUNK, NSTATE), lambda c, g: (rev(c), g)),
                   pl.BlockSpec((CHUNK, NSTATE), lambda c, g: (rev(c), g)),
                   pl.BlockSpec((CHUNK, LANES), lambda c, g: (rev(c), 0)), par, par, par],
        out_shape=[jax.ShapeDtypeStruct((S, INNER), f32), jax.ShapeDtypeStruct((S, GROUPS * NSTATE), f32),
                   jax.ShapeDtypeStruct((S, GROUPS * NSTATE), f32), jax.ShapeDtypeStruct((S, LANES), MXU),
                   jax.ShapeDtypeStruct((1, LANES), f32), jax.ShapeDtypeStruct((1, LANES), f32),
                   jax.ShapeDtypeStruct((1, LANES), f32)],
        scratch=[pltpu.VMEM((HEADS, NSTATE, HP), f32), pltpu.VMEM((CHUNK, LANES), f32),
                 pltpu.VMEM((CHUNK, LANES), f32), pltpu.VMEM((LANES, CHUNK), f32),
                 pltpu.VMEM((CHUNK, LANES), f32), pltpu.VMEM((CHUNK, LANES), f32), pltpu.VMEM((1, LANES), f32)],
        sem=("arbitrary", "arbitrary"))


ATT_STACK_FWD, ATT_STACK_BWD = 4, 2


def _attn_kv(kp, kc, vp, vc, hk):
    sl = slice(AH * hk, AH * (hk + 1))
    return (jnp.concatenate([kp[:, sl], kc[:, sl]], axis=0).astype(MXU),
            jnp.concatenate([vp[:, sl], vc[:, sl]], axis=0).astype(MXU))


def _stack_heads(x, heads):
    return jnp.concatenate([x[:, AH * h:AH * (h + 1)] for h in heads], axis=0)


def _attn_block(n, q, kb, sinks, heads):
    rows = len(heads) * WIN
    qi = lax.broadcasted_iota(jnp.int32, (rows, 2 * WIN), 0) & (WIN - 1)
    ki = lax.broadcasted_iota(jnp.int32, (rows, 2 * WIN), 1)
    rel = qi + WIN - ki
    mask = (rel >= 0) & (rel < WIN) & ((ki >= WIN) | (n > 0))
    qg = _stack_heads(q, heads).astype(MXU)
    s = lax.dot_general(qg, kb, (((1,), (1,)), ((), ())), preferred_element_type=f32) * (AH ** -0.5)
    s = jnp.where(mask, s, -1e30)
    sink = jnp.concatenate([jnp.broadcast_to(sinks[:, h:h + 1], (WIN, 1)) for h in heads], axis=0)
    m = jnp.maximum(jnp.max(s, axis=1, keepdims=True), sink)
    p = jnp.exp(s - m)
    ps = jnp.exp(sink - m)
    inv = 1.0 / (jnp.sum(p, axis=1, keepdims=True) + ps)
    return qg, p * inv, ps * inv


def _head_blocks(hk, stack):
    return [list(range(QPK * hk + i, QPK * hk + i + stack)) for i in range(0, QPK, stack)]


def _kv_specs():
    prev = lambda n: (jnp.maximum(n - 1, 0), 0)
    cur = lambda n: (n, 0)
    w = KVH * AH
    return [pl.BlockSpec((WIN, w), prev), pl.BlockSpec((WIN, w), cur), pl.BlockSpec((WIN, w), prev), pl.BlockSpec((WIN, w), cur)]


def attn_fwd(q, k, v, sinks, comm=None):
    def body(q_ref, kp_ref, kc_ref, vp_ref, vc_ref, s_ref, o_ref):
        n = pl.program_id(0)
        q_, kp, kc, vp, vc, sk = q_ref[...], kp_ref[...], kc_ref[...], vp_ref[...], vc_ref[...], s_ref[...]
        outs = []
        for hk in range(KVH):
            kb, vb = _attn_kv(kp, kc, vp, vc, hk)
            for heads in _head_blocks(hk, ATT_STACK_FWD):
                _, pr, _ = _attn_block(n, q_, kb, sk, heads)
                o = jnp.dot(pr.astype(MXU), vb, preferred_element_type=f32)
                outs += [o[WIN * i:WIN * (i + 1)] for i in range(len(heads))]
        o_ref[...] = jnp.concatenate(outs, axis=1)

    return _call(
        body, [q, k, k, v, v, sinks], comm=comm, name="attn_fwd", grid=(NBLK,),
        in_specs=[pl.BlockSpec((WIN, D), lambda n: (n, 0))] + _kv_specs() + [pl.BlockSpec((1, QH), lambda n: (0, 0))],
        out_specs=[pl.BlockSpec((WIN, D), lambda n: (n, 0))],
        out_shape=[jax.ShapeDtypeStruct((S, D), f32)], sem=("parallel",))[0]


def attn_bwd(q, k, v, sinks, dout, comm=None):
    def body(q_ref, kp_ref, kc_ref, vp_ref, vc_ref, s_ref, do_ref, dq_ref, dkp_ref, dkc_ref, dvp_ref, dvc_ref, ds_ref):
        n = pl.program_id(0)

        @pl.when(n == 0)
        def _():
            ds_ref[...] = jnp.zeros_like(ds_ref)

        q_, kp, kc, vp, vc, sk, do = q_ref[...], kp_ref[...], kc_ref[...], vp_ref[...], vc_ref[...], s_ref[...], do_ref[...]
        lane = lax.broadcasted_iota(jnp.int32, (1, QH), 1)
        nt = (((1,), (1,)), ((), ()))
        tn = (((0,), (0,)), ((), ()))
        dqs, dkps, dkcs, dvps, dvcs = [], [], [], [], []
        dsink = jnp.zeros((1, QH), f32)
        for hk in range(KVH):
            kb, vb = _attn_kv(kp, kc, vp, vc, hk)
            dkb, dvb = jnp.zeros((2 * WIN, AH), f32), jnp.zeros((2 * WIN, AH), f32)
            for heads in _head_blocks(hk, ATT_STACK_BWD):
                qg, pr, prs = _attn_block(n, q_, kb, sk, heads)
                dog = _stack_heads(do, heads).astype(MXU)
                dp = lax.dot_general(dog, vb, nt, preferred_element_type=f32)
                dvb = dvb + lax.dot_general(pr.astype(MXU), dog, tn, preferred_element_type=f32)
                delta = jnp.sum(pr * dp, axis=1, keepdims=True)
                ds = (pr * (dp - delta)).astype(MXU)
                dsk = -prs * delta
                for i, h in enumerate(heads):
                    dsink = dsink + jnp.sum(dsk[WIN * i:WIN * (i + 1)], axis=0, keepdims=True) * (lane == h).astype(f32)
                dqg = jnp.dot(ds, kb, preferred_element_type=f32) * (AH ** -0.5)
                dkb = dkb + lax.dot_general(ds, qg, tn, preferred_element_type=f32) * (AH ** -0.5)
                dqs += [dqg[WIN * i:WIN * (i + 1)] for i in range(len(heads))]
            dkps.append(dkb[:WIN])
            dkcs.append(dkb[WIN:])
            dvps.append(dvb[:WIN])
            dvcs.append(dvb[WIN:])
        dq_ref[...] = jnp.concatenate(dqs, axis=1)
        dkp_ref[...] = jnp.concatenate(dkps, axis=1)
        dkc_ref[...] = jnp.concatenate(dkcs, axis=1)
        dvp_ref[...] = jnp.concatenate(dvps, axis=1)
        dvc_ref[...] = jnp.concatenate(dvcs, axis=1)
        ds_ref[...] += dsink

    w = KVH * AH
    blk = lambda width: pl.BlockSpec((WIN, width), lambda n: (n, 0))
    return _call(
        body, [q, k, k, v, v, sinks, dout], comm=comm, name="attn_bwd", grid=(NBLK,),
        in_specs=[blk(D)] + _kv_specs() + [pl.BlockSpec((1, QH), lambda n: (0, 0)), blk(D)],
        out_specs=[blk(D), blk(w), blk(w), blk(w), blk(w), pl.BlockSpec((1, QH), lambda n: (0, 0))],
        out_shape=[jax.ShapeDtypeStruct((S, D), f32)] + [jax.ShapeDtypeStruct((S, w), f32)] * 4 + [jax.ShapeDtypeStruct((1, QH), f32)],
        sem=("arbitrary",))


def kv_bwd(kv, pos, inv_freq, k_norm, dkp, dkc, dvp, dvc):
    w = KVH * AH

    def body(kv_ref, pos_ref, if_ref, g_ref, dkp_ref, dkc_ref, dvp_ref, dvc_ref, o_ref, dg_ref, db_ref):
        n = pl.program_id(0)

        @pl.when(n == 0)
        def _():
            dg_ref[...] = jnp.zeros_like(dg_ref)
            db_ref[...] = jnp.zeros_like(db_ref)

        inside = (n < NBLK - 1).astype(f32)
        dk = dkc_ref[...] + inside * dkp_ref[...]
        dv = dvc_ref[...] + inside * dvp_ref[...]
        cos, sin = _rope_tables(pos_ref[...], if_ref[...])
        dkpre, dg = _headnorm_rope_bwd(kv_ref[...], g_ref[...], cos, sin, dk, KVH)
        dkv = jnp.concatenate([dkpre, dv], axis=1)
        o_ref[...] = dkv.astype(o_ref.dtype)
        dg_ref[...] += dg
        db_ref[...] += jnp.sum(dkv, axis=0, keepdims=True)

    nxt = lambda n: (jnp.minimum(n + 1, NBLK - 1), 0)
    cur = lambda n: (n, 0)
    const = lambda n: (0, 0)
    return pl.pallas_call(
        body, name="kv_bwd", grid=(NBLK,),
        in_specs=[pl.BlockSpec((WIN, w), cur), pl.BlockSpec((WIN, 1), cur), pl.BlockSpec((1, AH // 2), const),
                  pl.BlockSpec((1, AH), const), pl.BlockSpec((WIN, w), nxt), pl.BlockSpec((WIN, w), cur),
                  pl.BlockSpec((WIN, w), nxt), pl.BlockSpec((WIN, w), cur)],
        out_specs=[pl.BlockSpec((WIN, 2 * w), cur), pl.BlockSpec((1, AH), const), pl.BlockSpec((1, 2 * w), const)],
        out_shape=[jax.ShapeDtypeStruct((S, 2 * w), MXU), jax.ShapeDtypeStruct((1, AH), f32), jax.ShapeDtypeStruct((1, 2 * w), f32)],
        compiler_params=_cparams(("arbitrary",)),
    )(kv, pos, inv_freq, k_norm, dkp, dkc, dvp, dvc)


def _adam_math(w, g, m, v):
    m = ADAM_B1 * m + (1.0 - ADAM_B1) * g
    v = ADAM_B2 * v + (1.0 - ADAM_B2) * (g * g)
    m_hat = m / (1.0 - ADAM_B1 ** ADAM_STEP)
    v_hat = v / (1.0 - ADAM_B2 ** ADAM_STEP)
    return -ADAM_LR * (m_hat / (jnp.sqrt(v_hat) + ADAM_EPS) + ADAM_WD * w), m, v


def adamw(name, w, g, m, v):
    R, C = w.shape
    tr = _pick(R, (256, 128, 64, 32, 16, 8))
    tc = C if tr < R or C % 256 else 256

    def body(w_ref, g_ref, m_ref, v_ref, d_ref, nm_ref, nv_ref):
        d_ref[...], nm_ref[...], nv_ref[...] = _adam_math(w_ref[...], g_ref[...], m_ref[...], v_ref[...])

    spec = pl.BlockSpec((tr, tc), lambda i, j: (i, j))
    return pl.pallas_call(
        body, name=name, grid=(R // tr, C // tc), in_specs=[spec] * 4, out_specs=[spec] * 3,
        out_shape=[jax.ShapeDtypeStruct((R, C), f32)] * 3, compiler_params=_cparams(("parallel", "parallel")),
    )(w, g, m, v)


def _me():
    return lax.axis_index("x"), lax.axis_index("y"), lax.axis_index("c")


def gather_comm(xs):
    n = len(xs)

    def parts(x_refs, o_refs, sems):
        send_sems, recv_sems, local_sems = sems
        x, y, c = _me()
        me, sibling = (x, y, c), (x, y, 1 - c)
        chips = [(1 - x, y), (x, 1 - y), (1 - x, 1 - y)]

        def copy(a, k, block, to, src=None):
            dst = o_refs[a].at[4 * block[0] + 2 * block[1] + block[2]]
            return pltpu.make_async_remote_copy(
                src_ref=dst if src is None else src, dst_ref=dst,
                send_sem=send_sems.at[7 * a + k], recv_sem=recv_sems.at[7 * a + k], device_id=to, device_id_type=MESH)

        mine = [pltpu.make_async_copy(x_refs[a], o_refs[a].at[4 * x + 2 * y + c], local_sems.at[a]) for a in range(n)]
        first = []
        for a in range(n):
            first.append(copy(a, 0, me, sibling, src=x_refs[a]))
            first += [copy(a, 1 + j, me, (*chip, c), src=x_refs[a]) for j, chip in enumerate(chips)]
        return copy, mine, first, me, sibling, chips, c

    def start(x_refs, o_refs, sems):
        _, mine, first, *_ = parts(x_refs, o_refs, sems)
        for cp in mine + first:
            cp.start()

    def finish(x_refs, o_refs, sems):
        copy, mine, first, me, sibling, chips, c = parts(x_refs, o_refs, sems)
        passed = []
        for j, chip in enumerate(chips):
            for a in range(n):
                copy(a, 1 + j, (*chip, c), me).wait_recv()
                cp = copy(a, 4 + j, (*chip, c), sibling)
                cp.start()
                passed.append(cp)
        for a in range(n):
            copy(a, 0, sibling, me).wait_recv()
            for j, chip in enumerate(chips):
                copy(a, 4 + j, (*chip, 1 - c), me).wait_recv()
        for cp in first + passed:
            cp.wait_send()
        for cp in mine:
            cp.wait()

    return Comm(xs, [jax.ShapeDtypeStruct((N_DEV,) + a.shape, a.dtype) for a in xs],
                [pltpu.SemaphoreType.DMA((7 * n,)), pltpu.SemaphoreType.DMA((7 * n,)), pltpu.SemaphoreType.DMA((n,))], start, finish)


def run_comm(name, comm):
    _call(lambda: None, [], name=name, grid=(1,), in_specs=[], out_specs=[], out_shape=[], comm=comm)
    return comm.results


def sibling_comm(gs):
    n = len(gs)

    def copies(g_refs, o_refs, sems):
        x, y, c = _me()
        return [pltpu.make_async_remote_copy(
            src_ref=g_refs[a].at[:, 1 - c], dst_ref=o_refs[a], send_sem=sems[0].at[a], recv_sem=sems[1].at[a],
            device_id=(x, y, 1 - c), device_id_type=MESH) for a in range(n)]

    def start(g_refs, o_refs, sems):
        for cp in copies(g_refs, o_refs, sems):
            cp.start()

    def finish(g_refs, o_refs, sems):
        for cp in copies(g_refs, o_refs, sems):
            cp.wait()

    return Comm(gs, [jax.ShapeDtypeStruct((4,) + g.shape[2:], g.dtype) for g in gs],
                [pltpu.SemaphoreType.DMA((n,)), pltpu.SemaphoreType.DMA((n,))], start, finish)


def chip_comm(ts, rows=None):
    n = len(ts)
    span = [rows or (0, t.shape[1]) for t in ts]

    def copies(t_refs, o_refs, sems):
        x, y, c = _me()
        chips = [(1 - x, y), (x, 1 - y), (1 - x, 1 - y)]
        return [pltpu.make_async_remote_copy(
            src_ref=t_refs[a].at[2 * px + py, pl.ds(*span[a])], dst_ref=o_refs[a].at[j],
            send_sem=sems[0].at[3 * a + j], recv_sem=sems[1].at[3 * a + j],
            device_id=(px, py, c), device_id_type=MESH) for j, (px, py) in enumerate(chips) for a in range(n)]

    def start(t_refs, o_refs, sems):
        for cp in copies(t_refs, o_refs, sems):
            cp.start()

    def finish(t_refs, o_refs, sems):
        for cp in copies(t_refs, o_refs, sems):
            cp.wait()

    return Comm(ts, [jax.ShapeDtypeStruct((3, nr) + t.shape[2:], t.dtype) for t, (_, nr) in zip(ts, span)],
                [pltpu.SemaphoreType.DMA((3 * n,)), pltpu.SemaphoreType.DMA((3 * n,))], start, finish)


def _row_tile(rows):
    return _pick(rows, (512, 304, 256, 128))


def pair_add(name, g, r):
    _, _, R, C = g.shape
    tr = _row_tile(R)

    def body(c_ref, g_ref, r_ref, o_ref):
        o_ref[0] = (g_ref[0, 0].astype(f32) + r_ref[0].astype(f32)).astype(o_ref.dtype)

    return pl.pallas_call(
        body, name=name,
        grid_spec=pltpu.PrefetchScalarGridSpec(
            num_scalar_prefetch=1, grid=(4, R // tr),
            in_specs=[pl.BlockSpec((1, 1, tr, C), lambda p, i, c: (p, c[0], i, 0)),
                      pl.BlockSpec((1, tr, C), lambda p, i, c: (p, i, 0))],
            out_specs=pl.BlockSpec((1, tr, C), lambda p, i, c: (p, i, 0))),
        out_shape=jax.ShapeDtypeStruct((4, R, C), g.dtype),
        compiler_params=_cparams(("parallel", "parallel")),
    )(lax.axis_index("c").reshape(1).astype(jnp.int32), g, r)


def _sum_of_four(t_ref, r_ref):
    return ((t_ref[0].astype(f32) + r_ref[0].astype(f32)) + r_ref[1].astype(f32)) + r_ref[2].astype(f32)


def _my_chip():
    return (2 * lax.axis_index("x") + lax.axis_index("y")).reshape(1).astype(jnp.int32)


def final_adamw(name, t, r, w, m, v):
    _, R, C = t.shape
    tr = _pick(R, (256, 128, 64, 32, 16))
    tc = C if tr < R or C % 256 else 256

    def body(p_ref, t_ref, r_ref, w_ref, m_ref, v_ref, g_ref, d_ref, nm_ref, nv_ref):
        g_ = _sum_of_four(t_ref, r_ref)
        g_ref[...] = g_
        d_ref[...], nm_ref[...], nv_ref[...] = _adam_math(w_ref[...], g_, m_ref[...], v_ref[...])

    flat = pl.BlockSpec((tr, tc), lambda i, j, p: (i, j))
    return pl.pallas_call(
        body, name=name,
        grid_spec=pltpu.PrefetchScalarGridSpec(
            num_scalar_prefetch=1, grid=(R // tr, C // tc),
            in_specs=[pl.BlockSpec((1, tr, tc), lambda i, j, p: (p[0], i, j)),
                      pl.BlockSpec((3, tr, tc), lambda i, j, p: (0, i, j)), flat, flat, flat],
            out_specs=[flat] * 4),
        out_shape=[jax.ShapeDtypeStruct((R, C), f32)] * 4,
        compiler_params=_cparams(("parallel", "parallel")),
    )(_my_chip(), t, r, w, m, v)


def final_adamw_layers(name, t0, r0, t1, r1, w, m, v, comm=None):
    _, R, C = t0.shape
    tr = _pick(R, (256, 128, 64, 32, 16))
    steps = R // tr
    c_ins, c_shapes, c_sems = (comm.ins, comm.out_shapes, comm.sems) if comm else ([], [], [])

    def body(p_ref, t0_ref, r0_ref, t1_ref, r1_ref, w_ref, m_ref, v_ref, *rest):
        ci, (g_ref, d_ref, nm_ref, nv_ref) = rest[:len(c_ins)], rest[len(c_ins):len(c_ins) + 4]
        co, cs = rest[len(c_ins) + 4:len(c_ins) + 4 + len(c_shapes)], rest[len(c_ins) + 4 + len(c_shapes):]
        if comm:
            @pl.when((pl.program_id(0) == 0) & (pl.program_id(1) == 0))
            def _():
                comm.start(ci, co, cs)

        g_ = jnp.where(pl.program_id(0) == 0, _sum_of_four(t0_ref, r0_ref), _sum_of_four(t1_ref, r1_ref))
        g_ref[0] = g_
        d_ref[0], nm_ref[0], nv_ref[0] = _adam_math(w_ref[0], g_, m_ref[0], v_ref[0])
        if comm:
            @pl.when((pl.program_id(0) == 1) & (pl.program_id(1) == steps - 1))
            def _():
                comm.finish(ci, co, cs)

    mine = pl.BlockSpec((1, tr, C), lambda l, i, p: (p[0], i, 0))
    theirs = pl.BlockSpec((3, tr, C), lambda l, i, p: (0, i, 0))
    layer = pl.BlockSpec((1, tr, C), lambda l, i, p: (l, i, 0))
    any_spec = pl.BlockSpec(memory_space=pl.ANY)
    res = pl.pallas_call(
        body, name=name,
        grid_spec=pltpu.PrefetchScalarGridSpec(
            num_scalar_prefetch=1, grid=(2, steps),
            in_specs=[mine, theirs, mine, theirs, layer, layer, layer] + [any_spec] * len(c_ins),
            out_specs=[layer] * 4 + [any_spec] * len(c_shapes), scratch_shapes=list(c_sems)),
        out_shape=[jax.ShapeDtypeStruct((2, R, C), f32)] * 4 + list(c_shapes),
        compiler_params=_cparams(("arbitrary", "arbitrary") if comm else ("parallel", "parallel")),
    )(_my_chip(), t0, r0, t1, r1, w, m, v, *c_ins)
    if comm:
        comm.set_results(res[4:])
    return res[:4]


class ReduceScatter:
    def __init__(self, tag, keys, grads):
        self.tag, self.keys, self.grads = tag, keys, grads
        self.send = [g.reshape((4, 2, g.shape[0] // N_DEV) + g.shape[1:]) for g in grads]
        self.later = []

    def sibling(self):
        self.c1 = sibling_comm(self.send)
        return self.c1

    def chips(self, split=None):
        self.pairs = [pair_add(f"rs_pair_add_{self.tag}{i}", g, r) for i, (g, r) in enumerate(zip(self.send, self.c1.results))]
        if split is None:
            self.c2 = chip_comm(self.pairs)
        else:
            cuts = (0,) + tuple(split) + (self.pairs[0].shape[1],)
            self.c2, *self.later = [chip_comm(self.pairs, (a, b - a)) for a, b in zip(cuts, cuts[1:])]
        return self.c2

    def parts(self):
        arrived = self.c2.results
        if self.later:
            arrived = [jnp.concatenate(rows, axis=1) for rows in zip(arrived, *[c.results for c in self.later])]
        return {k: (t, r) for k, t, r in zip(self.keys, self.pairs, arrived)}


IN_ROWS = {"z": (0, 2048), "xs": (2048, 4096), "B": (4096, 5120), "C": (5120, 6144)}
IN_COLS = 2 * INNER + 2 * GROUPS * NSTATE + HEADS
IN_SPLIT = (448, 640)


def sum_devices(g):
    def body(g_ref, o_ref):
        acc = g_ref[0]
        for i in range(1, N_DEV):
            acc = acc + g_ref[i]
        o_ref[...] = acc

    return pl.pallas_call(body, name="sum_devices", out_shape=jax.ShapeDtypeStruct(g.shape[1:], f32),
                          compiler_params=_cparams())(g)


def _pack(parts, unit, dtype, lead=()):
    flat = jnp.concatenate([p.reshape(lead + (-1,)).astype(dtype) for p in parts], axis=-1)
    n = flat.shape[-1]
    rows = -(-n // (unit * PACK_W)) * unit
    flat = jnp.pad(flat, [(0, 0)] * len(lead) + [(0, rows * PACK_W - n)])
    return flat.reshape(lead + (rows, PACK_W))


def _unpack(buf, shapes, lead=()):
    flat = buf.reshape(lead + (-1,))
    out, off = [], 0
    for shp in shapes:
        n = math.prod(shp)
        out.append(flat[..., off:off + n].reshape(lead + tuple(shp)))
        off += n
    return out


def _pad_lanes(a):
    return jnp.pad(a, [(0, 0)] * (a.ndim - 1) + [(0, LANES - a.shape[-1])])


_NN = (((1,), (0,)), ((), ()))
_NT = (((1,), (1,)), ((), ()))


def _mm(a, b, dims):
    return lax.dot_general(a.astype(MXU), b.astype(MXU), dims, preferred_element_type=f32)


def _ffn_fwd(tag, x, h, w_inT, conv_w, conv_b, mid_comm=None):
    ct, nblk = FFN_CT, FFN // FFN_CT

    def body(h_ref, wg_ref, wv_ref, cw_ref, cb_ref, gp_ref, v_ref, a_ref):
        h_ = h_ref[...]
        gp, v_ = _mm(h_, wg_ref[...], _NT), _mm(h_, wv_ref[...], _NT)
        gp_ref[...] = gp
        v_ref[...] = v_
        a_ref[...] = (_silu(_conv(gp, cw_ref[...], cb_ref[...])) * v_).astype(a_ref.dtype)

    col = pl.BlockSpec((S, ct), lambda j: (0, j))
    gate_pre, val, act = _call(
        body, [h, w_inT, w_inT, conv_w, conv_b], comm=mid_comm, name=f"{tag}_in", grid=(nblk,),
        in_specs=[pl.BlockSpec((S, D), lambda j: (0, 0)), pl.BlockSpec((ct, D), lambda j: (j, 0)),
                  pl.BlockSpec((ct, D), lambda j: (nblk + j, 0)), pl.BlockSpec((CONV_F, ct), lambda j: (0, j)),
                  pl.BlockSpec((1, ct), lambda j: (0, j))],
        out_specs=[col, col, col],
        out_shape=[jax.ShapeDtypeStruct((S, FFN), f32), jax.ShapeDtypeStruct((S, FFN), f32), jax.ShapeDtypeStruct((S, FFN), MXU)],
        sem=("parallel",))
    return act, (x, h, gate_pre, val, act)


FFN_CT = 256
CONV_CT = 256


def _proj_conv(name, h, wT, row0, cw, cb, comm=None):
    C, ct = cw.shape[1], CONV_CT

    def body(h_ref, w_ref, cw_ref, cb_ref, p_ref, c_ref):
        p = _mm(h_ref[...], w_ref[...], _NT)
        p_ref[...] = p
        c_ref[...] = _silu(_conv(p, cw_ref[...], cb_ref[...]))

    col = pl.BlockSpec((S, ct), lambda j: (0, j))
    return _call(
        body, [h, wT, cw, cb], comm=comm, name=name, grid=(C // ct,),
        in_specs=[pl.BlockSpec((S, D), lambda j: (0, 0)), pl.BlockSpec((ct, D), lambda j: (row0 // ct + j, 0)),
                  pl.BlockSpec((CONV_A, ct), lambda j: (0, j)), pl.BlockSpec((1, ct), lambda j: (0, j))],
        out_specs=[col, col], out_shape=[jax.ShapeDtypeStruct((S, C), f32)] * 2, sem=("parallel",))


def _dconv_wgrad(name, pre, dconv, cw, cb, h):
    C, ct = cw.shape[1], CONV_CT

    def body(p_ref, do_ref, cw_ref, cb_ref, h_ref, dp_ref, g_ref, dw_ref, db_ref):
        p_, w_ = p_ref[...], cw_ref[...]
        taps = _taps(p_, CONV_A)
        dx, dw, db = _conv_bwd(p_, w_, do_ref[...] * _dsilu(_conv(p_, w_, cb_ref[...], taps)), taps)
        dpm = dx.astype(MXU)
        dp_ref[...] = dpm
        g_ref[...] = lax.dot_general(dpm, h_ref[...].astype(MXU), (((0,), (0,)), ((), ())),
                                     preferred_element_type=f32).astype(g_ref.dtype)
        dw_ref[...] = dw
        db_ref[...] = db

    col = pl.BlockSpec((S, ct), lambda j: (0, j))
    return _call(
        body, [pre, dconv, cw, cb, h], name=name, grid=(C // ct,),
        in_specs=[col, col, pl.BlockSpec((CONV_A, ct), lambda j: (0, j)), pl.BlockSpec((1, ct), lambda j: (0, j)),
                  pl.BlockSpec((S, D), lambda j: (0, 0))],
        out_specs=[col, pl.BlockSpec((ct, D), lambda j: (j, 0)), pl.BlockSpec((CONV_A, ct), lambda j: (0, j)),
                   pl.BlockSpec((1, ct), lambda j: (0, j))],
        out_shape=[jax.ShapeDtypeStruct((S, C), MXU), jax.ShapeDtypeStruct((C, D), MXU),
                   jax.ShapeDtypeStruct((CONV_A, C), f32), jax.ShapeDtypeStruct((1, C), f32)],
        sem=("parallel",))


def _ffn_mid_bwd(name, dout, w_down, gate_pre, val, conv_w, conv_b, comm=None):
    ct = FFN_CT

    def body(do_ref, wd_ref, gp_ref, v_ref, w_ref, b_ref, dgv_ref, dw_ref, db_ref, dob_s):
        @pl.when(pl.program_id(0) == 0)
        def _():
            dob_s[...] = do_ref[...].astype(MXU)

        da = _mm(dob_s[...], wd_ref[...], _NT)
        gp, v_, w_ = gp_ref[...], v_ref[...], w_ref[...]
        taps = _taps(gp, CONV_F)
        gate = _conv(gp, w_, b_ref[...], taps)
        sg = _sigmoid(gate)
        dgp, dw, db = _conv_bwd(gp, w_, da * v_ * (sg * (1.0 + gate * (1.0 - sg))), taps)
        dgv_ref[0] = dgp.astype(dgv_ref.dtype)
        dgv_ref[1] = (da * (gate * sg)).astype(dgv_ref.dtype)
        dw_ref[...] = dw
        db_ref[...] = db

    col = pl.BlockSpec((S, ct), lambda j: (0, j))
    return _call(
        body, [dout, w_down, gate_pre, val, conv_w, conv_b], comm=comm, name=name, grid=(FFN // ct,),
        in_specs=[pl.BlockSpec((S, D), lambda j: (0, 0)), pl.BlockSpec((ct, D), lambda j: (j, 0)), col, col,
                  pl.BlockSpec((CONV_F, ct), lambda j: (0, j)), pl.BlockSpec((1, ct), lambda j: (0, j))],
        out_specs=[pl.BlockSpec((2, S, ct), lambda j: (0, 0, j)), pl.BlockSpec((CONV_F, ct), lambda j: (0, j)),
                   pl.BlockSpec((1, ct), lambda j: (0, j))],
        out_shape=[jax.ShapeDtypeStruct((2, S, FFN), MXU), jax.ShapeDtypeStruct((CONV_F, FFN), f32), jax.ShapeDtypeStruct((1, FFN), f32)],
        scratch=[pltpu.VMEM((S, D), MXU)], sem=("arbitrary",))


def _ffn_bwd(tag, layer, saved, norm_g, w_inT, conv_w, conv_b, w_down, dout, mid_comm=None, before=None):
    x, h, gate_pre, val, act = saved
    g_down = matmul(f"{tag}_wdown", act, dout, "tn", out_dtype=MXU)
    dgv, g_cw, g_cb = _ffn_mid_bwd(f"{tag}_dmid", dout, w_down, gate_pre, val, conv_w, conv_b, comm=mid_comm)
    g_inT = matmul_tn_stacked(f"{tag}_win", dgv, h, MXU)

    def din_fn(dg_, dv_, x_, do_, g_, wT, *wb):
        dx, dg = _rms_bwd(x_, g_, _mm(dg_, wT[:FFN], _NN) + _mm(dv_, wT[FFN:], _NN))
        dx = do_ + dx
        return (dx,) + tuple(_mm(dx, w_, _NT) for w_ in wb) + (dg,)

    rs = ReduceScatter(tag, (f"f_inT{layer}", f"f_down{layer}"), [g_inT, g_down])
    extra = [] if before is None else [before]
    *dxs, g_norm = rowwise(f"{tag}_din", din_fn, [(dgv, 0), (dgv, 1), x, dout], [norm_g, w_inT] + extra,
                           [(D, f32)] + [(w_.shape[0], f32) for w_ in extra], [(1, D)], comm=rs.sibling())
    small = {f"f_norm{layer}": g_norm, f"f_conv_w{layer}": g_cw, f"f_conv_b{layer}": g_cb}
    return (dxs[0] if before is None else tuple(dxs)), small, rs


def _land(W, keys, comm):
    for k, g in zip(keys, comm.results):
        W[k] = g.reshape(-1, g.shape[2])


def _local_step(x, pos, tgt, W, shards):
    G = {}
    gather = lambda *keys: gather_comm([shards[k] for k in keys])
    inv_freq = (ROPE_THETA ** (-jnp.arange(AH // 2, dtype=f32) / (AH // 2))).reshape(1, AH // 2)

    def in_fn(x_, g_, wT, wdtT):
        h_ = _rms_fwd(x_, g_).astype(MXU)
        return h_, _mm(h_, wT[slice(*IN_ROWS["z"])], _NT), _mm(h_, wdtT, _NT)

    h0, z, dt_pre = rowwise("a_in", in_fn, [x], [W["a_norm"], W["inT"], W["in_dtT"]], [(D, MXU), (INNER, f32), (LANES, f32)])
    pre, conv = {}, {}
    early = {"xs": ("a_out",), "B": (), "C": ()}
    for k in ("xs", "B", "C"):
        c = gather(*early[k]) if early[k] else None
        pre[k], conv[k] = _proj_conv(f"a_in_{k}", h0, W["inT"], IN_ROWS[k][0], W[f"cw_{k}"], W[f"cb_{k}"], comm=c)
        if c is not None:
            _land(W, early[k], c)
    c = gather("f_inT0", "f_down0")
    y, states = ssd_fwd(conv["xs"], conv["B"], conv["C"], dt_pre, W["dt_bias"], W["A_log"], W["D"], comm=c)
    _land(W, ("f_inT0", "f_down0"), c)

    def gate_norm(y_, z_, g_):
        yg = y_ * _silu(z_)
        w = INNER // GROUPS
        return (jnp.concatenate([_rms_fwd(yg[:, w * i:w * (i + 1)], g_[:, w * i:w * (i + 1)]) for i in range(GROUPS)], axis=1),)

    def out_fn(y_, z_, x_, g_, w_, gf_):
        (gn_,) = gate_norm(y_, z_, g_)
        gn_ = gn_.astype(MXU)
        x1_ = x_ + _mm(gn_, w_, _NN)
        return gn_, x1_, _rms_fwd(x1_, gf_)

    c = gather("f_down1")
    gn, x1, h1 = rowwise("a_out", out_fn, [y, z, x], [W["a_gnorm"], W["a_out"], W["f_norm0"]],
                         [(INNER, MXU), (D, f32), (D, MXU)], comm=c)
    _land(W, ("f_down1",), c)

    c = gather("w_kv", "w_q", "w_o")
    act0, ffn0 = _ffn_fwd("f0", x1, h1, W["f_inT0"], W["f_cw0"], W["f_cb0"], mid_comm=c)
    _land(W, ("w_kv", "w_q", "w_o"), c)
    kw = KVH * AH

    def qkv_fn(a_, x_, pos_, wd, gk, gb, wkv, bkv, wq, bq, if_, kn, qn):
        x2_ = x_ + _mm(a_, wd, _NN)
        kvn_, h2_ = _rms_fwd(x2_, gk).astype(MXU), _rms_fwd(x2_, gb).astype(MXU)
        kv_, qp_ = _mm(kvn_, wkv, _NN) + bkv, _mm(h2_, wq, _NN) + bq
        cos, sin = _rope_tables(pos_, if_)
        return (x2_, kvn_, h2_, kv_, qp_, _headnorm_rope_fwd(kv_[:, :kw], kn, cos, sin, KVH), kv_[:, kw:],
                _headnorm_rope_fwd(qp_, qn, cos, sin, QH))

    x2, kvn, h2, kv, q_pre, k_rot, v_val, q = rowwise(
        "f0_down_qkv", qkv_fn, [act0, x1, pos],
        [W["f_down0"], W["kv_norm"], W["b_norm"], W["w_kv"], W["b_kv"], W["w_q"], W["b_q"], inv_freq, W["k_norm"], W["q_norm"]],
        [(D, f32), (D, MXU), (D, MXU), (2 * kw, f32), (D, f32), (kw, f32), (kw, f32), (D, f32)])
    c = gather("f_inT1")
    att = attn_fwd(q, k_rot, v_val, W["sinks"], comm=c)
    _land(W, ("f_inT1",), c)
    def o_fn(att_, x_, w_, b_, gf_):
        x3_ = x_ + _mm(att_, w_, _NN) + b_
        return x3_, _rms_fwd(x3_, gf_)

    x3, h3 = rowwise("o_proj", o_fn, [att, x2], [W["w_o"], W["b_o"], W["f_norm1"]], [(D, f32), (D, MXU)])

    act1, ffn1 = _ffn_fwd("f1", x3, h3, W["f_inT1"], W["f_cw1"], W["f_cb1"])

    def loss_fn(a_, x_, t_, w_):
        diff = x_ + _mm(a_, w_, _NN) - t_
        rows = jnp.sum(diff * diff, axis=1, keepdims=True) * (0.5 / D)
        return diff * (1.0 / D), jnp.sum(rows, axis=0, keepdims=True)

    dx4, loss = rowwise("f1_down_loss", loss_fn, [act1, x3, tgt], [W["f_down1"]], [(D, f32)], [(1, 1)])

    (dx3, datt), g, rs_f1 = _ffn_bwd("f1", 1, ffn1, W["f_norm1"], W["f_inT1"], W["f_cw1"], W["f_cb1"], W["f_down1"], dx4,
                                     before=W["w_o"])
    G.update(g)

    g_wo = matmul("o_wproj", att, dx3, "tn", out_dtype=MXU)
    dq, dkp, dkc, dvp, dvc, G["sinks"] = attn_bwd(q, k_rot, v_val, W["sinks"], datt, comm=rs_f1.chips())

    def q_bwd(q_, pos_, dq_, dx_, if_, g_):
        cos, sin = _rope_tables(pos_, if_)
        dqp, dg = _headnorm_rope_bwd(q_, g_, cos, sin, dq_, QH)
        return dqp, dg, jnp.sum(dqp, axis=0, keepdims=True), jnp.sum(dx_, axis=0, keepdims=True)

    dq_pre, G["q_norm"], G["b_q"], G["b_o"] = rowwise("q_drope", q_bwd, [q_pre, pos, dq, dx3], [inv_freq, W["q_norm"]],
                                                      [(D, MXU)], [(1, AH), (1, D), (1, D)])
    g_wq = matmul("q_wproj", h2, dq_pre, "tn", out_dtype=MXU)
    dkv, G["k_norm"], G["b_kv"] = kv_bwd(kv, pos, inv_freq, W["k_norm"], dkp, dkc, dvp, dvc)
    g_wkv = matmul("kv_wproj", kvn, dkv, "tn", out_dtype=MXU)
    rs_att = ReduceScatter("att", ("w_kv", "w_q", "w_o"), [g_wkv, g_wq, g_wo])

    def x2_bwd(x_, dq_, dkv_, dx_, gb_, gk_, wq, wkv):
        d1, dgb = _rms_bwd(x_, gb_, _mm(dq_, wq, _NT))
        d2, dgk = _rms_bwd(x_, gk_, _mm(dkv_, wkv, _NT))
        return dx_ + d1 + d2, dgb, dgk

    dx2, G["b_norm"], G["kv_norm"] = rowwise("qkv_dproj", x2_bwd, [x2, dq_pre, dkv, dx3],
                                             [W["b_norm"], W["kv_norm"], W["w_q"], W["w_kv"]],
                                             [(D, f32)], [(1, D), (1, D)], comm=rs_att.sibling())

    dx1, g, rs_f0 = _ffn_bwd("f0", 0, ffn0, W["f_norm0"], W["f_inT0"], W["f_cw0"], W["f_cb0"], W["f_down0"], dx2,
                             mid_comm=rs_att.chips())
    G.update(g)

    rs_out = ReduceScatter("a_out", ("a_out",), [matmul("a_wout", gn, dx1, "tn", out_dtype=MXU)])

    def gate_norm_bwd(y_, z_, dx_, g_, w_out):
        dgn_ = _mm(dx_, w_out, _NT)
        w = INNER // GROUPS
        sg = _sigmoid(z_)
        sz = z_ * sg
        yg = y_ * sz
        parts, dgs = [], []
        for i in range(GROUPS):
            dseg, dg = _rms_bwd(yg[:, w * i:w * (i + 1)], g_[:, w * i:w * (i + 1)], dgn_[:, w * i:w * (i + 1)])
            parts.append(dseg)
            dgs.append(dg)
        dyg = jnp.concatenate(parts, axis=1)
        return dyg * sz, dyg * y_ * (sg * (1.0 + z_ * (1.0 - sg))), jnp.concatenate(dgs, axis=1)

    dy, dz, G["a_gnorm"] = rowwise("a_dout", gate_norm_bwd, [y, z, dx1], [W["a_gnorm"], W["a_out"]],
                                   [(INNER, f32), (INNER, MXU)], [(1, INNER)], comm=rs_out.sibling())
    dconv = {}
    dconv["xs"], dconv["B"], dconv["C"], ddt_pre, G["dt_bias"], G["A_log"], G["D"] = ssd_bwd(
        conv["xs"], conv["B"], conv["C"], dt_pre, W["dt_bias"], W["A_log"], W["D"], states, dy,
        comm=merge_comms([rs_f0.chips(), rs_out.chips()]))

    g_in, dpre = [matmul("a_win_z", dz, h0, "tn", out_dtype=MXU)], {}
    for k in ("xs", "B", "C"):
        dpre[k], g_k, G[f"cw_{k}"], G[f"cb_{k}"] = _dconv_wgrad(f"a_dconv_{k}", pre[k], dconv[k], W[f"cw_{k}"], W[f"cb_{k}"], h0)
        g_in.append(g_k)
    g_in.append(matmul("a_win_dt", ddt_pre, h0, "tn", out_dtype=MXU)[:HEADS])
    rs_in = ReduceScatter("a_in", ("inT",), [jnp.concatenate(g_in, axis=0)])
    run_comm("rs_in_sibling", rs_in.sibling())

    def x0_bwd(dz_, dxs_, db_, dc_, ddt_, x_, do_, g_, wT, wdtT):
        parts = zip((dz_, dxs_, db_, dc_), IN_ROWS.values())
        dh = sum(_mm(d_, wT[a:b], _NN) for d_, (a, b) in parts) + _mm(ddt_, wdtT, _NN)
        dx, dg = _rms_bwd(x_, g_, dh)
        return do_ + dx, dg

    dx, G["a_norm"] = rowwise("a_din", x0_bwd, [dz, dpre["xs"], dpre["B"], dpre["C"], ddt_pre, x, dx1],
                              [W["a_norm"], W["inT"], W["in_dtT"]], [(D, f32)], [(1, D)], comm=rs_in.chips(split=IN_SPLIT))
    return loss, dx, G, [rs_f1, rs_att, rs_f0, rs_out, rs_in]


ROW_KEYS = ("inT", "a_out", "f_inT0", "f_down0", "w_kv", "w_q", "w_o", "f_inT1", "f_down1")


def _row_blocks(src):
    return {"inT": src["a_in_proj"][0].T, "a_out": src["a_out_proj"][0], "w_kv": src["w_kv"], "w_q": src["w_q"][0],
            "w_o": src["w_o"][0], "f_inT0": src["f_w_in"][0].T, "f_inT1": src["f_w_in"][1].T,
            "f_down0": src["f_w_down"][0], "f_down1": src["f_w_down"][1]}


def _from_row_blocks(rb):
    out = {"a_in_proj": rb["inT"].T[None], "a_out_proj": rb["a_out"][None], "w_kv": rb["w_kv"], "w_q": rb["w_q"][None],
           "w_o": rb["w_o"][None]}
    if "f_inT0" in rb:
        out["f_w_in"] = jnp.stack([rb["f_inT0"].T, rb["f_inT1"].T])
        out["f_w_down"] = jnp.stack([rb["f_down0"], rb["f_down1"]])
    return out


SMALL_SHARDED = ("a_norm", "a_conv_w", "a_conv_b", "a_gnorm", "f_conv_w")
REPLICATED = ("a_dt_bias", "a_A_log", "a_D", "kv_norm", "b_kv", "k_norm", "b_norm", "b_q", "q_norm", "sinks", "b_o",
              "f_norm", "f_conv_b")
ORDER = ("a_norm", "a_in_proj", "a_conv_w", "a_conv_b", "a_dt_bias", "a_A_log", "a_D", "a_gnorm", "a_out_proj", "kv_norm",
         "w_kv", "b_kv", "k_norm", "b_norm", "w_q", "b_q", "q_norm", "sinks", "w_o", "b_o", "f_norm", "f_w_in",
         "f_conv_w", "f_conv_b", "f_w_down")


def _gathered_to_whole(name, g):
    if name == "a_conv_w":
        return jnp.moveaxis(g[:, 0], 0, 1).reshape(g.shape[2], -1)
    if name in ("a_norm", "a_conv_b", "a_gnorm"):
        return g[:, 0].reshape(1, -1)
    if name == "f_conv_w":
        return jnp.moveaxis(g, 0, 2).reshape(g.shape[1], g.shape[2], -1)
    raise ValueError(name)


def _whole_to_shards(name, w):
    if name == "a_conv_w":
        return jnp.moveaxis(w.reshape(w.shape[0], N_DEV, -1), 1, 0)[:, None]
    if name in ("a_norm", "a_conv_b", "a_gnorm"):
        return w.reshape(N_DEV, 1, -1)
    if name == "f_conv_w":
        return jnp.moveaxis(w.reshape(w.shape[0], w.shape[1], N_DEV, -1), 2, 0)
    raise ValueError(name)


def _small_weights(whole):
    W = {}
    cw, cb = whole["a_conv_w"], whole["a_conv_b"]
    o = 0
    for k, n in (("xs", INNER), ("B", GROUPS * NSTATE), ("C", GROUPS * NSTATE)):
        W[f"cw_{k}"], W[f"cb_{k}"] = cw[:, o:o + n], cb[:, o:o + n]
        o += n
    W["a_norm"], W["a_gnorm"] = whole["a_norm"], whole["a_gnorm"]
    W["dt_bias"], W["A_log"], W["D"] = (_pad_lanes(whole[k]) for k in ("a_dt_bias", "a_A_log", "a_D"))
    W["kv_norm"], W["b_kv"], W["k_norm"] = whole["kv_norm"].reshape(1, -1), whole["b_kv"].reshape(1, -1), whole["k_norm"].reshape(1, -1)
    for k in ("b_norm", "b_q", "q_norm", "sinks", "b_o"):
        W[k] = whole[k]
    for i in range(2):
        W[f"f_norm{i}"] = whole["f_norm"][i:i + 1]
        W[f"f_cw{i}"], W[f"f_cb{i}"] = whole["f_conv_w"][i], whole["f_conv_b"][i:i + 1]
    return W


def _small_grads(G, shapes):
    nh = HEADS
    out = {
        "a_conv_w": jnp.concatenate([G["cw_xs"], G["cw_B"], G["cw_C"]], axis=1),
        "a_conv_b": jnp.concatenate([G["cb_xs"], G["cb_B"], G["cb_C"]], axis=1),
        "a_norm": G["a_norm"], "a_gnorm": G["a_gnorm"],
        "a_dt_bias": G["dt_bias"][:, :nh], "a_A_log": G["A_log"][:, :nh], "a_D": G["D"][:, :nh],
        "kv_norm": G["kv_norm"], "b_kv": G["b_kv"], "k_norm": G["k_norm"], "b_norm": G["b_norm"],
        "b_q": G["b_q"], "q_norm": G["q_norm"], "sinks": G["sinks"], "b_o": G["b_o"],
        "f_norm": jnp.concatenate([G["f_norm0"], G["f_norm1"]], axis=0),
        "f_conv_w": jnp.stack([G["f_conv_w0"], G["f_conv_w1"]]),
        "f_conv_b": jnp.concatenate([G["f_conv_b0"], G["f_conv_b1"]], axis=0),
    }
    return {k: val.reshape(shapes[k]) if k in shapes else val for k, val in out.items()}


def kernel(x, positions, a_norm, a_in_proj, a_conv_w, a_conv_b, a_dt_bias, a_A_log, a_D, a_gnorm, a_out_proj, kv_norm, w_kv, b_kv, k_norm, b_norm, w_q, b_q, q_norm, sinks, w_o, b_o, f_norm, f_w_in, f_conv_w, f_conv_b, f_w_down, loss_target, m_a_norm, m_a_in_proj, m_a_conv_w, m_a_conv_b, m_a_dt_bias, m_a_A_log, m_a_D, m_a_gnorm, m_a_out_proj, m_kv_norm, m_w_kv, m_b_kv, m_k_norm, m_b_norm, m_w_q, m_b_q, m_q_norm, m_sinks, m_w_o, m_b_o, m_f_norm, m_f_w_in, m_f_conv_w, m_f_conv_b, m_f_w_down, v_a_norm, v_a_in_proj, v_a_conv_w, v_a_conv_b, v_a_dt_bias, v_a_A_log, v_a_D, v_a_gnorm, v_a_out_proj, v_kv_norm, v_w_kv, v_b_kv, v_k_norm, v_b_norm, v_w_q, v_b_q, v_q_norm, v_sinks, v_w_o, v_b_o, v_f_norm, v_f_w_in, v_f_conv_w, v_f_conv_b, v_f_w_down):
    given = dict(locals())
    w_in = {n: given[n] for n in ORDER}
    m_in = {n: given["m_" + n] for n in ORDER}
    v_in = {n: given["v_" + n] for n in ORDER}
    dev = 4 * lax.axis_index("x") + 2 * lax.axis_index("y") + lax.axis_index("c")

    w2, m2, v2 = _row_blocks(w_in), _row_blocks(m_in), _row_blocks(v_in)
    small_pack = _pack([w_in[n] for n in SMALL_SHARDED], 8, f32)
    shards = {k: w2[k].astype(MXU) for k in ROW_KEYS}
    in_all, small_all = run_comm("ag_head", gather_comm([shards["inT"], small_pack]))
    whole = {n: w_in[n] for n in REPLICATED}
    for n, g in zip(SMALL_SHARDED, _unpack(small_all, [w_in[n].shape for n in SMALL_SHARDED], lead=(N_DEV,))):
        whole[n] = _gathered_to_whole(n, g)
    W = _small_weights(whole)
    W["inT"] = in_all.reshape(-1, D)
    W["in_dtT"] = jnp.pad(W["inT"][IN_COLS - HEADS:], ((0, LANES - HEADS), (0, 0)))

    loss, dx, G, scatters = _local_step(x[0], positions.reshape(S, 1).astype(f32), loss_target[0], W, shards)
    grads = _small_grads(G, {n: whole[n].shape for n in REPLICATED})

    small_names = SMALL_SHARDED + REPLICATED
    small_part = _pack([grads[n] for n in small_names] + [loss], 8, f32)
    small_gather = gather_comm([small_part])
    parts, last = {}, scatters[-1]
    for rs in scatters[:-1]:
        parts.update(rs.parts())
    riding = {"f_w_in": merge_comms([small_gather] + last.later[:1]), "f_w_down": last.later[1] if len(last.later) > 1 else None}

    layered = {}
    for n, key, lay in (("f_w_in", "f_inT", lambda a: jnp.swapaxes(a, 1, 2)), ("f_w_down", "f_down", lambda a: a)):
        res = final_adamw_layers(f"adamw_{n}", *parts[key + "0"], *parts[key + "1"], lay(w_in[n]), lay(m_in[n]), lay(v_in[n]),
                                 comm=riding[n])
        layered[n] = [lay(a) for a in res]
    parts.update(last.parts())
    single = tuple(k for k in ROW_KEYS if not k.startswith("f_"))
    stepped = {k: final_adamw(f"adamw_{k}", *parts[k], w2[k], m2[k], v2[k]) for k in single}
    g_out, delta, new_m, new_v = (_from_row_blocks({k: stepped[k][i] for k in single}) for i in range(4))
    for n, res in layered.items():
        g_out[n], delta[n], new_m[n], new_v[n] = res
    *small_sums, loss_all = _unpack(sum_devices(small_gather.results[0]), [grads[n].shape for n in small_names] + [(1, 1)])
    for n, g in zip(small_names, small_sums):
        if n in SMALL_SHARDED:
            g_out[n] = lax.dynamic_index_in_dim(_whole_to_shards(n, g), dev, axis=0, keepdims=False)
        else:
            g_out[n] = g.reshape(w_in[n].shape)

    packs = [_pack([src[n] for n in small_names], 8, f32) for src in (w_in, g_out, m_in, v_in)]
    outs = adamw("adamw_small", *packs)
    for dst, buf in zip((delta, new_m, new_v), outs):
        for n, a in zip(small_names, _unpack(buf, [w_in[n].shape for n in small_names])):
            dst[n] = a

    return (loss_all[0, 0], dx[None], *[g_out[n] for n in ORDER], *[delta[n] for n in ORDER],
            *[new_m[n] for n in ORDER], *[new_v[n] for n in ORDER])
```
